```python
import jax, jax.numpy as jnp
from jax import lax
import numpy as np

D_MODEL = 1024
BATCH = 8
SEQ = 8192
DEPTH = 1

MIX_WIDTH = D_MODEL
POOL_WIDTH = D_MODEL // 4
POOL_WINDOWS = (2, 4, 8, 16)
POOL_GROUP = POOL_WIDTH // len(POOL_WINDOWS)
HEAD_DIM = 64
ATTN_WIDTH = MIX_WIDTH - POOL_WIDTH
N_HEADS = ATTN_WIDTH // HEAD_DIM
DILATED_CONFIGS = ((128, 1), (512, 4), (2048, 16))
BLOCK = 128
ROPE_THETA = 10000.0
IN_WIDTH = POOL_WIDTH + 3 * ATTN_WIDTH
_FF_RAW = -(-8 * D_MODEL // 3)
D_FF = ((_FF_RAW + 255) // 256) * 256
EPS = 1e-6

kernel_name = "hybrid_pool_dilated_attn_block"


def rms_norm(x, g):
    xf = x.astype(jnp.float32)
    y = xf * lax.rsqrt(jnp.mean(xf * xf, axis=-1, keepdims=True) + EPS)
    return (y * g.astype(jnp.float32)).astype(x.dtype)


def rope(x, pos):
    half = x.shape[-1] // 2
    freqs = ROPE_THETA ** (-jnp.arange(half, dtype=jnp.float32) * (2.0 / x.shape[-1]))
    ang = pos.astype(jnp.float32)[:, None] * freqs[None, :]
    cos = jnp.cos(ang)[None, :, None, :]
    sin = jnp.sin(ang)[None, :, None, :]
    xf = x.astype(jnp.float32)
    x1, x2 = xf[..., :half], xf[..., half:]
    out = jnp.concatenate([x1 * cos - x2 * sin, x2 * cos + x1 * sin], axis=-1)
    return out.astype(x.dtype)


def multi_scale_pool(u, w_pool, pool_scale):
    B, S, _ = u.shape
    ug = u.astype(jnp.float32).reshape(B, S, len(POOL_WINDOWS), POOL_GROUP)
    csum = lax.cumsum(ug, axis=1)
    t = jnp.arange(S)
    outs = []
    for gi, win in enumerate(POOL_WINDOWS):
        cg = csum[:, :, gi]
        shifted = jnp.pad(cg, ((0, 0), (win, 0), (0, 0)))[:, :S]
        cnt = jnp.minimum(t + 1, win).astype(jnp.float32)[None, :, None]
        outs.append((cg - shifted) / cnt - ug[:, :, gi])
    d = jnp.stack(outs, axis=2)
    y = jnp.einsum('bsgc,gcd->bsgd', d, w_pool.astype(jnp.float32))
    y = y.reshape(B, S, POOL_WIDTH) * pool_scale.astype(jnp.float32)
    return y.astype(u.dtype)


def dilated_branch(q, k, v, window, dilation):
    B, S, H, Dh = q.shape
    L = S // dilation
    nb = -(-L // BLOCK)
    Lp = nb * BLOCK
    w_sub = window // dilation

    def to_sub(a):
        a = a.reshape(B, L, dilation, H, Dh).transpose(0, 2, 1, 3, 4)
        return jnp.pad(a, ((0, 0), (0, 0), (0, Lp - L), (0, 0), (0, 0)))

    qs = to_sub(q).reshape(B, dilation, nb, BLOCK, H, Dh)
    kp = jnp.pad(to_sub(k), ((0, 0), (0, 0), (BLOCK, 0), (0, 0), (0, 0)))
    vp = jnp.pad(to_sub(v), ((0, 0), (0, 0), (BLOCK, 0), (0, 0), (0, 0)))

    def band(a):
        prev = a[:, :, :Lp].reshape(B, dilation, nb, BLOCK, H, Dh)
        cur = a[:, :, BLOCK:].reshape(B, dilation, nb, BLOCK, H, Dh)
        return jnp.concatenate([prev, cur], axis=3)

    kb, vb = band(kp), band(vp)
    scale = 1.0 / np.sqrt(Dh).astype(np.float32)
    s = jnp.einsum('brnqhd,brnkhd->brnhqk', qs.astype(jnp.float32), kb.astype(jnp.float32)) * scale

    qi = jnp.arange(BLOCK)[:, None]
    kj = jnp.arange(2 * BLOCK)[None, :]
    dist = qi + BLOCK - kj
    blk = jnp.arange(nb)[:, None, None]
    valid = (dist >= 0) & (dist <= w_sub) & (blk * BLOCK + kj - BLOCK >= 0)
    s = jnp.where(valid[None, None, :, None], s, -jnp.inf)

    m = jnp.max(s, axis=-1, keepdims=True)
    e = jnp.exp(s - m)
    den = jnp.sum(e, axis=-1, keepdims=True)
    lse = (m + jnp.log(den))[..., 0]
    o = jnp.einsum('brnhqk,brnkhd->brnqhd', e / den, vb.astype(jnp.float32))

    o = o.reshape(B, dilation, Lp, H, Dh)[:, :, :L].transpose(0, 2, 1, 3, 4).reshape(B, S, H, Dh)
    lse = lse.transpose(0, 1, 2, 4, 3).reshape(B, dilation, Lp, H)[:, :, :L]
    lse = lse.transpose(0, 2, 1, 3).reshape(B, S, H)
    return o, lse


def dilated_attention(q, k, v):
    outs, lses = [], []
    for window, dilation in DILATED_CONFIGS:
        o, lse = dilated_branch(q, k, v, window, dilation)
        outs.append(o)
        lses.append(lse)
    w = jax.nn.softmax(jnp.stack(lses, axis=0), axis=0)
    o = jnp.sum(w[..., None] * jnp.stack(outs, axis=0), axis=0)
    return o.astype(q.dtype)


def _fwd_setup_inputs(seed: int = 0) -> dict:
    key = jax.random.key(seed)
    ks = jax.random.split(key, 13)
    f32 = jnp.float32
    nrm = lambda k, shape, s: jax.random.normal(k, shape, f32) * s
    return {
        "x": jax.random.normal(ks[0], (BATCH, SEQ, D_MODEL), f32),
        "ln_pre_mix": 1.0 + nrm(ks[1], (DEPTH, D_MODEL), 0.05),
        "w_in": nrm(ks[2], (DEPTH, D_MODEL, IN_WIDTH), D_MODEL ** -0.5),
        "w_pool": nrm(ks[3], (DEPTH, len(POOL_WINDOWS), POOL_GROUP, POOL_GROUP), POOL_GROUP ** -0.5),
        "pool_scale": 1.0 + nrm(ks[4], (DEPTH, POOL_WIDTH), 0.1),
        "w_out": nrm(ks[5], (DEPTH, MIX_WIDTH, D_MODEL), MIX_WIDTH ** -0.5),
        "ln_post_mix": 1.0 + nrm(ks[6], (DEPTH, D_MODEL), 0.05),
        "ln_pre_ffn": 1.0 + nrm(ks[7], (DEPTH, D_MODEL), 0.05),
        "w_gate": nrm(ks[8], (DEPTH, D_MODEL, D_FF), D_MODEL ** -0.5),
        "w_up": nrm(ks[9], (DEPTH, D_MODEL, D_FF), D_MODEL ** -0.5),
        "w_down": nrm(ks[10], (DEPTH, D_FF, D_MODEL), D_FF ** -0.5),
        "ln_post_ffn": 1.0 + nrm(ks[11], (DEPTH, D_MODEL), 0.05),
    }


def _fwd_reference(x, ln_pre_mix, w_in, w_pool, pool_scale, w_out, ln_post_mix,
              ln_pre_ffn, w_gate, w_up, w_down, ln_post_ffn):
    B, S, _ = x.shape
    pos = jnp.arange(S)
    for l in range(DEPTH):
        h = rms_norm(x, ln_pre_mix[l])
        proj = h @ w_in[l]
        u_pool = proj[..., :POOL_WIDTH]
        q = proj[..., POOL_WIDTH:POOL_WIDTH + ATTN_WIDTH].reshape(B, S, N_HEADS, HEAD_DIM)
        k = proj[..., POOL_WIDTH + ATTN_WIDTH:POOL_WIDTH + 2 * ATTN_WIDTH].reshape(B, S, N_HEADS, HEAD_DIM)
        v = proj[..., POOL_WIDTH + 2 * ATTN_WIDTH:].reshape(B, S, N_HEADS, HEAD_DIM)
        q, k = rope(q, pos), rope(k, pos)
        pool_out = multi_scale_pool(u_pool, w_pool[l], pool_scale[l])
        attn_out = dilated_attention(q, k, v).reshape(B, S, ATTN_WIDTH)
        mix = jnp.concatenate([pool_out, attn_out], axis=-1) @ w_out[l]
        x = x + rms_norm(mix, ln_post_mix[l])
        h = rms_norm(x, ln_pre_ffn[l])
        f = (jax.nn.silu(h @ w_gate[l]) * (h @ w_up[l])) @ w_down[l]
        x = x + rms_norm(f, ln_post_ffn[l])
    return x


import jax as _jax
import jax.numpy as _jnp

TWIN_FORMAT = 'train_step'
FWD_PARAMS = ['x', 'ln_pre_mix', 'w_in', 'w_pool', 'pool_scale', 'w_out', 'ln_post_mix', 'ln_pre_ffn', 'w_gate', 'w_up', 'w_down', 'ln_post_ffn']
TWIN_WEIGHTS = ['ln_pre_mix', 'w_in', 'w_pool', 'pool_scale', 'w_out', 'ln_post_mix', 'ln_pre_ffn', 'w_gate', 'w_up', 'w_down', 'ln_post_ffn']
TWIN_DIFF_INPUT = 'x'
TWIN_INPUTS = ['x', 'ln_pre_mix', 'w_in', 'w_pool', 'pool_scale', 'w_out', 'ln_post_mix', 'ln_pre_ffn', 'w_gate', 'w_up', 'w_down', 'ln_post_ffn', 'loss_target', 'm_ln_pre_mix', 'm_w_in', 'm_w_pool', 'm_pool_scale', 'm_w_out', 'm_ln_post_mix', 'm_ln_pre_ffn', 'm_w_gate', 'm_w_up', 'm_w_down', 'm_ln_post_ffn', 'v_ln_pre_mix', 'v_w_in', 'v_w_pool', 'v_pool_scale', 'v_w_out', 'v_ln_post_mix', 'v_ln_pre_ffn', 'v_w_gate', 'v_w_up', 'v_w_down', 'v_ln_post_ffn']
TWIN_OUTPUTS = ['loss', 'grad_x', 'grad_ln_pre_mix', 'grad_w_in', 'grad_w_pool', 'grad_pool_scale', 'grad_w_out', 'grad_ln_post_mix', 'grad_ln_pre_ffn', 'grad_w_gate', 'grad_w_up', 'grad_w_down', 'grad_ln_post_ffn', 'delta_ln_pre_mix', 'delta_w_in', 'delta_w_pool', 'delta_pool_scale', 'delta_w_out', 'delta_ln_post_mix', 'delta_ln_pre_ffn', 'delta_w_gate', 'delta_w_up', 'delta_w_down', 'delta_ln_post_ffn', 'new_m_ln_pre_mix', 'new_m_w_in', 'new_m_w_pool', 'new_m_pool_scale', 'new_m_w_out', 'new_m_ln_post_mix', 'new_m_ln_pre_ffn', 'new_m_w_gate', 'new_m_w_up', 'new_m_w_down', 'new_m_ln_post_ffn', 'new_v_ln_pre_mix', 'new_v_w_in', 'new_v_w_pool', 'new_v_pool_scale', 'new_v_w_out', 'new_v_ln_post_mix', 'new_v_ln_pre_ffn', 'new_v_w_gate', 'new_v_w_up', 'new_v_w_down', 'new_v_ln_post_ffn']
TWIN_LEAF_KINDS = {'loss': 'loss', 'grad_x': 'grad_x', 'grad_ln_pre_mix': 'grad_w', 'grad_w_in': 'grad_w', 'grad_w_pool': 'grad_w', 'grad_pool_scale': 'grad_w', 'grad_w_out': 'grad_w', 'grad_ln_post_mix': 'grad_w', 'grad_ln_pre_ffn': 'grad_w', 'grad_w_gate': 'grad_w', 'grad_w_up': 'grad_w', 'grad_w_down': 'grad_w', 'grad_ln_post_ffn': 'grad_w', 'delta_ln_pre_mix': 'delta_w', 'delta_w_in': 'delta_w', 'delta_w_pool': 'delta_w', 'delta_pool_scale': 'delta_w', 'delta_w_out': 'delta_w', 'delta_ln_post_mix': 'delta_w', 'delta_ln_pre_ffn': 'delta_w', 'delta_w_gate': 'delta_w', 'delta_w_up': 'delta_w', 'delta_w_down': 'delta_w', 'delta_ln_post_ffn': 'delta_w', 'new_m_ln_pre_mix': 'new_m', 'new_m_w_in': 'new_m', 'new_m_w_pool': 'new_m', 'new_m_pool_scale': 'new_m', 'new_m_w_out': 'new_m', 'new_m_ln_post_mix': 'new_m', 'new_m_ln_pre_ffn': 'new_m', 'new_m_w_gate': 'new_m', 'new_m_w_up': 'new_m', 'new_m_w_down': 'new_m', 'new_m_ln_post_ffn': 'new_m', 'new_v_ln_pre_mix': 'new_v', 'new_v_w_in': 'new_v', 'new_v_w_pool': 'new_v', 'new_v_pool_scale': 'new_v', 'new_v_w_out': 'new_v', 'new_v_ln_post_mix': 'new_v', 'new_v_ln_pre_ffn': 'new_v', 'new_v_w_gate': 'new_v', 'new_v_w_up': 'new_v', 'new_v_w_down': 'new_v', 'new_v_ln_post_ffn': 'new_v'}


def _forward(args):
    return _fwd_reference(*[args[k] for k in FWD_PARAMS])


def _output_shape():
    def fwd():
        inp = _fwd_setup_inputs(0)
        return _fwd_reference(*[inp[k] for k in FWD_PARAMS])
    out = _jax.eval_shape(fwd)
    return out.shape, out.dtype

N_MICROBATCH = 1
ADAM_LR = 0.001
ADAM_B1 = 0.9
ADAM_B2 = 0.999
ADAM_EPS = 1e-08
ADAM_WD = 0.01
ADAM_STEP = 10
PER_EXAMPLE_BATCH_AXIS = {'x': 0, 'loss_target': 0}
SHARED_INPUTS = []
_WEIGHT_DTYPES = {'ln_pre_mix': _jnp.float32, 'w_in': _jnp.float32, 'w_pool': _jnp.float32, 'pool_scale': _jnp.float32, 'w_out': _jnp.float32, 'ln_post_mix': _jnp.float32, 'ln_pre_ffn': _jnp.float32, 'w_gate': _jnp.float32, 'w_up': _jnp.float32, 'w_down': _jnp.float32, 'ln_post_ffn': _jnp.float32}
MOMENT_SCALE = {'ln_pre_mix': 1.380167e+00, 'w_in': 8.435643e-01, 'w_pool': 3.708955e+00, 'pool_scale': 5.448596e+00, 'w_out': 2.142646e+00, 'ln_post_mix': 6.461086e+01, 'ln_pre_ffn': 1.431501e+00, 'w_gate': 2.898043e-01, 'w_up': 8.314358e-01, 'w_down': 1.378006e+00, 'ln_post_ffn': 6.416101e+01}


def _to_microbatches(a, axis):
    t = _jnp.moveaxis(a, axis, 0)
    t = t.reshape((N_MICROBATCH, t.shape[0] // N_MICROBATCH) + t.shape[1:])
    return _jnp.moveaxis(t, 1, axis + 1)


def setup_inputs(seed: int = 0) -> dict:
    inp = _fwd_setup_inputs(seed)
    key = _jax.random.fold_in(_jax.random.key(seed), 7919)
    shape, _ = _output_shape()
    out = dict(inp)
    out["loss_target"] = _jax.random.normal(_jax.random.fold_in(key, 0), shape, _jnp.float32)
    for i, name in enumerate(TWIN_WEIGHTS):
        w = inp[name].astype(_jnp.float32)
        if MOMENT_SCALE is None:
            s = _jnp.sqrt(_jnp.mean(_jnp.square(w)) + 1e-30)
        else:
            s = MOMENT_SCALE[name]
        km, kv = _jax.random.split(_jax.random.fold_in(key, i + 1))
        out[name] = w
        out["m_" + name] = s * _jax.random.normal(km, w.shape, _jnp.float32)
        out["v_" + name] = (s * s) * _jax.random.uniform(kv, w.shape, _jnp.float32, 0.5, 1.5)
    if N_MICROBATCH > 1:
        for name, axis in PER_EXAMPLE_BATCH_AXIS.items():
            out[name] = _to_microbatches(out[name], axis)
    return {'x': out['x'], 'ln_pre_mix': out['ln_pre_mix'], 'w_in': out['w_in'], 'w_pool': out['w_pool'], 'pool_scale': out['pool_scale'], 'w_out': out['w_out'], 'ln_post_mix': out['ln_post_mix'], 'ln_pre_ffn': out['ln_pre_ffn'], 'w_gate': out['w_gate'], 'w_up': out['w_up'], 'w_down': out['w_down'], 'ln_post_ffn': out['ln_post_ffn'], 'loss_target': out['loss_target'], 'm_ln_pre_mix': out['m_ln_pre_mix'], 'm_w_in': out['m_w_in'], 'm_w_pool': out['m_w_pool'], 'm_pool_scale': out['m_pool_scale'], 'm_w_out': out['m_w_out'], 'm_ln_post_mix': out['m_ln_post_mix'], 'm_ln_pre_ffn': out['m_ln_pre_ffn'], 'm_w_gate': out['m_w_gate'], 'm_w_up': out['m_w_up'], 'm_w_down': out['m_w_down'], 'm_ln_post_ffn': out['m_ln_post_ffn'], 'v_ln_pre_mix': out['v_ln_pre_mix'], 'v_w_in': out['v_w_in'], 'v_w_pool': out['v_w_pool'], 'v_pool_scale': out['v_pool_scale'], 'v_w_out': out['v_w_out'], 'v_ln_post_mix': out['v_ln_post_mix'], 'v_ln_pre_ffn': out['v_ln_pre_ffn'], 'v_w_gate': out['v_w_gate'], 'v_w_up': out['v_w_up'], 'v_w_down': out['v_w_down'], 'v_ln_post_ffn': out['v_ln_post_ffn']}


def _loss(weights, diff, rest, loss_target):
    with _jax.named_scope("forward"):
        args = {**rest, TWIN_DIFF_INPUT: diff, **{k: w.astype(_WEIGHT_DTYPES[k]) for k, w in weights.items()}}
        y = _forward(args)
    with _jax.named_scope("loss_head"):
        err = _jnp.square(y.astype(_jnp.float32) - loss_target)
        return 0.5 * _jnp.sum(_jnp.mean(err, axis=-1)) if err.ndim else 0.5 * err


def _adamw(w, g, m, v):
    m = ADAM_B1 * m + (1.0 - ADAM_B1) * g
    v = ADAM_B2 * v + (1.0 - ADAM_B2) * _jnp.square(g)
    m_hat = m / (1.0 - ADAM_B1 ** ADAM_STEP)
    v_hat = v / (1.0 - ADAM_B2 ** ADAM_STEP)
    delta = -ADAM_LR * (m_hat / (_jnp.sqrt(v_hat) + ADAM_EPS) + ADAM_WD * w)
    return delta, m, v


def reference(x, ln_pre_mix, w_in, w_pool, pool_scale, w_out, ln_post_mix, ln_pre_ffn, w_gate, w_up, w_down, ln_post_ffn, loss_target, m_ln_pre_mix, m_w_in, m_w_pool, m_pool_scale, m_w_out, m_ln_post_mix, m_ln_pre_ffn, m_w_gate, m_w_up, m_w_down, m_ln_post_ffn, v_ln_pre_mix, v_w_in, v_w_pool, v_pool_scale, v_w_out, v_ln_post_mix, v_ln_pre_ffn, v_w_gate, v_w_up, v_w_down, v_ln_post_ffn):
    given = dict(x=x, ln_pre_mix=ln_pre_mix, w_in=w_in, w_pool=w_pool, pool_scale=pool_scale, w_out=w_out, ln_post_mix=ln_post_mix, ln_pre_ffn=ln_pre_ffn, w_gate=w_gate, w_up=w_up, w_down=w_down, ln_post_ffn=ln_post_ffn, loss_target=loss_target, m_ln_pre_mix=m_ln_pre_mix, m_w_in=m_w_in, m_w_pool=m_w_pool, m_pool_scale=m_pool_scale, m_w_out=m_w_out, m_ln_post_mix=m_ln_post_mix, m_ln_pre_ffn=m_ln_pre_ffn, m_w_gate=m_w_gate, m_w_up=m_w_up, m_w_down=m_w_down, m_ln_post_ffn=m_ln_post_ffn, v_ln_pre_mix=v_ln_pre_mix, v_w_in=v_w_in, v_w_pool=v_w_pool, v_pool_scale=v_pool_scale, v_w_out=v_w_out, v_ln_post_mix=v_ln_post_mix, v_ln_pre_ffn=v_ln_pre_ffn, v_w_gate=v_w_gate, v_w_up=v_w_up, v_w_down=v_w_down, v_ln_post_ffn=v_ln_post_ffn)
    weights = {n: given[n] for n in TWIN_WEIGHTS}
    shared = {n: given[n] for n in SHARED_INPUTS}
    per_example = {n: given[n] for n in ['x']}
    grad_fn = _jax.value_and_grad(_loss, argnums=(0, 1))

    def one_microbatch(ex, loss_target):
        ex = dict(ex)
        diff = ex.pop(TWIN_DIFF_INPUT)
        return grad_fn(weights, diff, {**shared, **ex}, loss_target)

    if N_MICROBATCH == 1:
        loss, (grad_w, grad_x) = one_microbatch(per_example, given["loss_target"])
    else:
        def body(carry, xs):
            loss_sum, grad_sum = carry
            l_k, (gw_k, gx_k) = one_microbatch(xs[0], xs[1])
            with _jax.named_scope("update"):
                return (loss_sum + l_k, _jax.tree.map(_jnp.add, grad_sum, gw_k)), gx_k

        init = (_jnp.zeros((), _jnp.float32), _jax.tree.map(_jnp.zeros_like, weights))
        (loss, grad_w), grad_x = _jax.lax.scan(body, init, (per_example, given["loss_target"]))
    with _jax.named_scope("update"):
        delta_w, new_m, new_v = {}, {}, {}
        for n in TWIN_WEIGHTS:
            delta_w[n], new_m[n], new_v[n] = _adamw(weights[n], grad_w[n], given["m_" + n], given["v_" + n])
    return (loss, grad_x, *[grad_w[n] for n in TWIN_WEIGHTS], *[delta_w[n] for n in TWIN_WEIGHTS],
            *[new_m[n] for n in TWIN_WEIGHTS], *[new_v[n] for n in TWIN_WEIGHTS])
```

```python
import numpy as np
import jax
import jax.numpy as jnp
from jax import lax
from jax.experimental import pallas as pl
from jax.experimental.pallas import tpu as pltpu

D_MODEL = 1024
POOL_WIDTH = 256
POOL_GROUP = 64
ATTN_WIDTH = 768
HEAD_DIM = 64
IN_WIDTH = 2560
D_FF = 2816
BLOCK = 128
DILATIONS = (1, 4, 16)
ROPE_THETA = 10000.0
EPS = 1e-6
ATTN_SCALE = 0.125
NEG = -1e30

ADAM_LR = 0.001
ADAM_B1 = 0.9
ADAM_B2 = 0.999
ADAM_EPS = 1e-08
ADAM_WD = 0.01
ADAM_STEP = 10

N_CHIPS = 4
N_DEV = 8
PACK_SPLITS = (640, 256, 704, 704, 704)
PACK_ROWS = sum(PACK_SPLITS)
HALF_ROWS = PACK_ROWS // 2
SMALL_ROWS = 24

VMEM_LIMIT_V7X = 56 * 1024 * 1024
MESH = pl.DeviceIdType.MESH

f32 = jnp.float32
bf16 = jnp.bfloat16


def _params(*sem):
    return pltpu.CompilerParams(dimension_semantics=sem, vmem_limit_bytes=VMEM_LIMIT_V7X)


def _dot(a, b):
    return jnp.dot(a, b, preferred_element_type=f32)


def _dot_nt(a, b):
    return lax.dot_general(a, b, (((1,), (1,)), ((), ())), preferred_element_type=f32)


def _dot_tn(a, b):
    return lax.dot_general(a, b, (((0,), (0,)), ((), ())), preferred_element_type=f32)


def _rope_partner(a, first_half):
    return jnp.where(first_half, pltpu.roll(a, 96, 1), pltpu.roll(a, 32, 1))


def _first_half_mask(rows):
    lane = lax.broadcasted_iota(jnp.int32, (rows, 128), 1)
    return (lane % HEAD_DIM) < (HEAD_DIM // 2)


def _in_proj(x, g1, w_in, cos_t, sin_t):
    S = x.shape[0]
    ts = 512

    def body(x_ref, g_ref, w_ref, cos_ref, sin_ref, h_ref, u_ref, q_ref, k_ref, v_ref):
        xv = x_ref[...]
        r = lax.rsqrt(jnp.mean(xv * xv, axis=-1, keepdims=True) + EPS)
        h = ((xv * r) * g_ref[...]).astype(bf16)
        h_ref[...] = h
        proj = _dot(h, w_ref[...])
        u_ref[...] = proj[:, :POOL_WIDTH]
        cos = cos_ref[...]
        sin = sin_ref[...]
        first = _first_half_mask(ts)
        for j in range(ATTN_WIDTH // 128):
            for base, ref in ((POOL_WIDTH, q_ref), (POOL_WIDTH + ATTN_WIDTH, k_ref)):
                a = proj[:, base + j * 128: base + (j + 1) * 128]
                ref[:, j * 128:(j + 1) * 128] = (a * cos + _rope_partner(a, first) * sin).astype(bf16)
        v_ref[...] = proj[:, POOL_WIDTH + 2 * ATTN_WIDTH:].astype(bf16)

    row = lambda w: pl.BlockSpec((ts, w), lambda i: (i, 0))
    return pl.pallas_call(
        body, name="in_proj", grid=(S // ts,),
        in_specs=[row(D_MODEL), pl.BlockSpec((1, D_MODEL), lambda i: (0, 0)),
                  pl.BlockSpec((D_MODEL, IN_WIDTH), lambda i: (0, 0)), row(128), row(128)],
        out_specs=[row(D_MODEL), row(POOL_WIDTH), row(ATTN_WIDTH), row(ATTN_WIDTH), row(ATTN_WIDTH)],
        out_shape=[jax.ShapeDtypeStruct((S, D_MODEL), bf16), jax.ShapeDtypeStruct((S, POOL_WIDTH), f32),
                   jax.ShapeDtypeStruct((S, ATTN_WIDTH), bf16), jax.ShapeDtypeStruct((S, ATTN_WIDTH), bf16),
                   jax.ShapeDtypeStruct((S, ATTN_WIDTH), bf16)],
        compiler_params=_params("parallel"),
    )(x, g1, w_in, cos_t, sin_t)


POOL_HALO = 16


def _pool_lane_group(rows):
    return lax.broadcasted_iota(jnp.int32, (rows, POOL_WIDTH), 1) // POOL_GROUP


def _pool_select(group, s2, s4, s8, s16):
    return jnp.where(group == 0, s2, jnp.where(group == 1, s4, jnp.where(group == 2, s8, s16)))


def _pool_count(t0, rows):
    group = _pool_lane_group(rows)
    t = t0 + lax.broadcasted_iota(jnp.int32, (rows, POOL_WIDTH), 0)
    win = _pool_select(group, 2, 4, 8, 16)
    return jnp.minimum(t + 1, win).astype(f32)


def _pool_diff(u_halo, u_tile, t0):
    ts = u_tile.shape[0]
    ext = jnp.concatenate([u_halo, u_tile], axis=0)
    s2 = ext + pltpu.roll(ext, 1, 0)
    s4 = s2 + pltpu.roll(s2, 2, 0)
    s8 = s4 + pltpu.roll(s4, 4, 0)
    s16 = s8 + pltpu.roll(s8, 8, 0)
    group = _pool_lane_group(ts + POOL_HALO)
    wsum = _pool_select(group, s2, s4, s8, s16)[POOL_HALO:]
    return wsum / _pool_count(t0, ts) - u_tile


def _pool_specs(ts, n_tiles):
    tile = pl.BlockSpec((ts, POOL_WIDTH), lambda i: (i, 0))
    per = ts // POOL_HALO
    before = pl.BlockSpec((POOL_HALO, POOL_WIDTH), lambda i: (jnp.maximum(i * per - 1, 0), 0))
    after = pl.BlockSpec((POOL_HALO, POOL_WIDTH), lambda i: (jnp.minimum((i + 1) * per, n_tiles * per - 1), 0))
    return tile, before, after


def _pool_fwd(u, w_bd, scale):
    S = u.shape[0]
    ts = 512
    n_tiles = S // ts

    def body(u_ref, halo_ref, w_ref, sc_ref, y_ref):
        i = pl.program_id(0)
        halo = jnp.where(i > 0, halo_ref[...], 0.0)
        d = _pool_diff(halo, u_ref[...], i * ts)
        y_ref[...] = (_dot(d.astype(bf16), w_ref[...]) * sc_ref[...]).astype(bf16)

    tile, before, _ = _pool_specs(ts, n_tiles)
    return pl.pallas_call(
        body, name="pool_fwd", grid=(n_tiles,),
        in_specs=[tile, before, pl.BlockSpec((POOL_WIDTH, POOL_WIDTH), lambda i: (0, 0)),
                  pl.BlockSpec((1, POOL_WIDTH), lambda i: (0, 0))],
        out_specs=tile, out_shape=jax.ShapeDtypeStruct((S, POOL_WIDTH), bf16),
        compiler_params=_params("parallel"),
    )(u, u, w_bd, scale)


def _pool_bwd(u, dy, w_bd, scale):
    S = u.shape[0]
    ts = 512
    n_tiles = S // ts

    def body(u_ref, halo_ref, dy_ref, dy_next_ref, w_ref, sc_ref, du_ref, dw_ref, dsc_ref):
        i = pl.program_id(0)

        @pl.when(i == 0)
        def _():
            dw_ref[...] = jnp.zeros_like(dw_ref)
            dsc_ref[...] = jnp.zeros_like(dsc_ref)

        halo = jnp.where(i > 0, halo_ref[...], 0.0)
        d = _pool_diff(halo, u_ref[...], i * ts).astype(bf16)
        w = w_ref[...]
        sc = sc_ref[...]
        dy_tile = dy_ref[...]
        z = _dot(d, w)
        dsc_ref[...] += jnp.sum(dy_tile * z, axis=0, keepdims=True)
        dy_next = jnp.where(i < n_tiles - 1, dy_next_ref[...], 0.0)
        dz = (jnp.concatenate([dy_tile, dy_next], axis=0) * sc).astype(bf16)
        dw_ref[...] += _dot_tn(d, dz[:ts])
        dd = _dot_nt(dz, w)
        e = dd / _pool_count(i * ts, ts + POOL_HALO)
        n = ts + POOL_HALO
        f2 = e + pltpu.roll(e, n - 1, 0)
        f4 = f2 + pltpu.roll(f2, n - 2, 0)
        f8 = f4 + pltpu.roll(f4, n - 4, 0)
        f16 = f8 + pltpu.roll(f8, n - 8, 0)
        fsum = _pool_select(_pool_lane_group(n), f2, f4, f8, f16)
        du_ref[...] = (fsum[:ts] - dd[:ts]).astype(bf16)

    tile, before, after = _pool_specs(ts, n_tiles)
    return pl.pallas_call(
        body, name="pool_bwd", grid=(n_tiles,),
        in_specs=[tile, before, tile, after, pl.BlockSpec((POOL_WIDTH, POOL_WIDTH), lambda i: (0, 0)),
                  pl.BlockSpec((1, POOL_WIDTH), lambda i: (0, 0))],
        out_specs=[tile, pl.BlockSpec((POOL_WIDTH, POOL_WIDTH), lambda i: (0, 0)),
                   pl.BlockSpec((1, POOL_WIDTH), lambda i: (0, 0))],
        out_shape=[jax.ShapeDtypeStruct((S, POOL_WIDTH), bf16), jax.ShapeDtypeStruct((POOL_WIDTH, POOL_WIDTH), f32),
                   jax.ShapeDtypeStruct((1, POOL_WIDTH), f32)],
        compiler_params=_params("arbitrary"),
    )(u, u, dy, dy, w_bd, scale)


def _band_mask(n):
    qi = lax.broadcasted_iota(jnp.int32, (BLOCK, 2 * BLOCK), 0)
    kj = lax.broadcasted_iota(jnp.int32, (BLOCK, 2 * BLOCK), 1)
    return (kj >= qi) & (kj <= qi + BLOCK) & ((kj >= BLOCK) | (n > 0))


def _head0_mask(rows=BLOCK):
    return lax.broadcasted_iota(jnp.int32, (rows, 128), 1) < HEAD_DIM


def _per_head(stat, h0, h):
    other = pltpu.roll(stat, HEAD_DIM, 1)
    full = jnp.where(h0, stat, other) if h == 0 else jnp.where(h0, other, stat)
    return jnp.concatenate([full, full], axis=1)


def _attn_specs(L, d):
    nb = L // BLOCK
    cur = pl.BlockSpec((BLOCK, ATTN_WIDTH), lambda r, n: (n, r))
    prev = pl.BlockSpec((BLOCK, ATTN_WIDTH), lambda r, n: (jnp.maximum(n - 1, 0), r))
    return nb, cur, prev


def _attn_fwd(q, k, v, d):
    S = q.shape[0]
    L = S // d
    view = lambda a: a.reshape(L, d * ATTN_WIDTH)

    def body(q_ref, kc_ref, kp_ref, vc_ref, vp_ref, o_ref, lse_ref):
        valid = _band_mask(pl.program_id(1))
        h0 = _head0_mask()
        for j in range(ATTN_WIDTH // 128):
            cols = slice(j * 128, (j + 1) * 128)
            qv = q_ref[:, cols]
            kb = jnp.concatenate([kp_ref[:, cols], kc_ref[:, cols]], axis=0)
            vb = jnp.concatenate([vp_ref[:, cols], vc_ref[:, cols]], axis=0)
            outs, lses = [], []
            for h in range(2):
                keep = h0 if h == 0 else jnp.logical_not(h0)
                qh = jnp.where(keep, qv, jnp.zeros_like(qv))
                s = jnp.where(valid, _dot_nt(qh, kb) * ATTN_SCALE, NEG)
                m = jnp.max(s, axis=1, keepdims=True)
                e = jnp.exp(s - m)
                den = jnp.sum(e, axis=1, keepdims=True)
                outs.append(_dot((e / den).astype(bf16), vb))
                lses.append(jnp.broadcast_to(m + jnp.log(den), (BLOCK, 128)))
            o_ref[:, cols] = jnp.where(h0, outs[0], outs[1])
            lse_ref[:, cols] = jnp.where(h0, lses[0], lses[1])

    nb, cur, prev = _attn_specs(L, d)
    o, lse = pl.pallas_call(
        body, name=f"attn_fwd_d{d}", grid=(d, nb),
        in_specs=[cur, cur, prev, cur, prev], out_specs=[cur, cur],
        out_shape=[jax.ShapeDtypeStruct((L, d * ATTN_WIDTH), f32)] * 2,
        compiler_params=_params("parallel", "parallel"),
    )(view(q), view(k), view(k), view(v), view(v))
    return o.reshape(S, ATTN_WIDTH), lse.reshape(S, ATTN_WIDTH)


def _attn_merge(os, lses):
    S = os[0].shape[0]
    ts = 512

    def body(o1, o2, o3, l1, l2, l3, out_ref, lse_ref):
        a, b, c = l1[...], l2[...], l3[...]
        m = jnp.maximum(jnp.maximum(a, b), c)
        ea, eb, ec = jnp.exp(a - m), jnp.exp(b - m), jnp.exp(c - m)
        tot = ea + eb + ec
        out_ref[...] = ((ea / tot) * o1[...] + (eb / tot) * o2[...] + (ec / tot) * o3[...]).astype(bf16)
        lse_ref[...] = m + jnp.log(tot)

    row = pl.BlockSpec((ts, ATTN_WIDTH), lambda i: (i, 0))
    return pl.pallas_call(
        body, name="attn_merge", grid=(S // ts,), in_specs=[row] * 6, out_specs=[row, row],
        out_shape=[jax.ShapeDtypeStruct((S, ATTN_WIDTH), bf16), jax.ShapeDtypeStruct((S, ATTN_WIDTH), f32)],
        compiler_params=_params("parallel"),
    )(*os, *lses)


def _attn_bwd(q, k, v, do, lse, delta, d):
    S = q.shape[0]
    L = S // d
    nb = L // BLOCK
    view = lambda a: a.reshape(L, d * ATTN_WIDTH)

    def body(q_ref, kc_ref, kp_ref, vc_ref, vp_ref, do_ref, lse_ref, dl_ref, dq_ref, dk_ref, dv_ref, ck_ref, cv_ref):
        n = pl.program_id(1)

        @pl.when(n < nb)
        def _():
            valid = _band_mask(n)
            h0 = _head0_mask()
            h0_band = _head0_mask(2 * BLOCK)
            for j in range(ATTN_WIDTH // 128):
                cols = slice(j * 128, (j + 1) * 128)
                qv = q_ref[:, cols]
                dov = do_ref[:, cols]
                kb = jnp.concatenate([kp_ref[:, cols], kc_ref[:, cols]], axis=0)
                vb = jnp.concatenate([vp_ref[:, cols], vc_ref[:, cols]], axis=0)
                lse_v = lse_ref[:, cols]
                dl_v = dl_ref[:, cols]
                dqs, dks, dvs = [], [], []
                for h in range(2):
                    keep = h0 if h == 0 else jnp.logical_not(h0)
                    qh = jnp.where(keep, qv, jnp.zeros_like(qv))
                    doh = jnp.where(keep, dov, jnp.zeros_like(dov))
                    s = jnp.where(valid, _dot_nt(qh, kb) * ATTN_SCALE, NEG)
                    p = jnp.exp(s - _per_head(lse_v, h0, h))
                    dp = _dot_nt(doh, vb)
                    ds = (p * (dp - _per_head(dl_v, h0, h)) * ATTN_SCALE).astype(bf16)
                    dqs.append(_dot(ds, kb))
                    dks.append(_dot_tn(ds, qv))
                    dvs.append(_dot_tn(p.astype(bf16), dov))
                dq_ref[:, cols] = jnp.where(h0, dqs[0], dqs[1]).astype(bf16)
                dkb = jnp.where(h0_band, dks[0], dks[1])
                dvb = jnp.where(h0_band, dvs[0], dvs[1])

                @pl.when(n > 0)
                def _():
                    dk_ref[:, cols] = (ck_ref[:, cols] + dkb[:BLOCK]).astype(bf16)
                    dv_ref[:, cols] = (cv_ref[:, cols] + dvb[:BLOCK]).astype(bf16)

                ck_ref[:, cols] = dkb[BLOCK:]
                cv_ref[:, cols] = dvb[BLOCK:]

        @pl.when(n == nb)
        def _():
            dk_ref[...] = ck_ref[...].astype(bf16)
            dv_ref[...] = cv_ref[...].astype(bf16)

    last = nb - 1
    cur = pl.BlockSpec((BLOCK, ATTN_WIDTH), lambda r, n: (jnp.minimum(n, last), r))
    prev = pl.BlockSpec((BLOCK, ATTN_WIDTH), lambda r, n: (jnp.clip(n - 1, 0, last), r))
    out = jax.ShapeDtypeStruct((L, d * ATTN_WIDTH), bf16)
    dq, dk, dv = pl.pallas_call(
        body, name=f"attn_bwd_d{d}", grid=(d, nb + 1),
        in_specs=[cur, cur, prev, cur, prev, cur, cur, cur], out_specs=[cur, prev, prev],
        out_shape=[out, out, out],
        scratch_shapes=[pltpu.VMEM((BLOCK, ATTN_WIDTH), f32), pltpu.VMEM((BLOCK, ATTN_WIDTH), f32)],
        compiler_params=_params("parallel", "arbitrary"),
    )(view(q), view(k), view(k), view(v), view(v), view(do), view(lse), view(delta))
    return dq.reshape(S, ATTN_WIDTH), dk.reshape(S, ATTN_WIDTH), dv.reshape(S, ATTN_WIDTH)


def _rms(v):
    return lax.rsqrt(jnp.mean(v * v, axis=-1, keepdims=True) + EPS)


def _out_proj(pool_out, attn_out, w_out, x, g2, g3):
    S = x.shape[0]
    ts = 512

    def body(p_ref, a_ref, w_ref, x_ref, g2_ref, g3_ref, mix_ref, x2_ref, h2_ref):
        mix = _dot(p_ref[...], w_ref[:POOL_WIDTH, :]) + _dot(a_ref[...], w_ref[POOL_WIDTH:, :])
        mix_ref[...] = mix
        x2 = x_ref[...] + (mix * _rms(mix)) * g2_ref[...]
        x2_ref[...] = x2
        h2_ref[...] = ((x2 * _rms(x2)) * g3_ref[...]).astype(bf16)

    row = lambda w: pl.BlockSpec((ts, w), lambda i: (i, 0))
    gain = pl.BlockSpec((1, D_MODEL), lambda i: (0, 0))
    return pl.pallas_call(
        body, name="out_proj", grid=(S // ts,),
        in_specs=[row(POOL_WIDTH), row(ATTN_WIDTH), pl.BlockSpec((D_MODEL, D_MODEL), lambda i: (0, 0)),
                  row(D_MODEL), gain, gain],
        out_specs=[row(D_MODEL)] * 3,
        out_shape=[jax.ShapeDtypeStruct((S, D_MODEL), f32), jax.ShapeDtypeStruct((S, D_MODEL), f32),
                   jax.ShapeDtypeStruct((S, D_MODEL), bf16)],
        compiler_params=_params("parallel"),
    )(pool_out, attn_out, w_out, x, g2, g3)


FF_TILE = 256
FF_HALF = D_FF // 2


def _sigmoid(g):
    return 1.0 / (1.0 + jnp.exp(-g))


def _ffn_fwd(h2, w_gate, w_up, w_down):
    S = h2.shape[0]
    ts = 1024

    def body(h_ref, wg_ref, wu_ref, wd_ref, gate_ref, up_ref, f_ref):
        j = pl.program_id(1)
        h = h_ref[...]
        gate = _dot(h, wg_ref[...])
        up = _dot(h, wu_ref[...])
        gate_ref[...] = gate.astype(bf16)
        up_ref[...] = up.astype(bf16)
        part = _dot((gate * _sigmoid(gate) * up).astype(bf16), wd_ref[...])

        @pl.when(j == 0)
        def _():
            f_ref[...] = part

        @pl.when(j > 0)
        def _():
            f_ref[...] += part

    act = pl.BlockSpec((ts, FF_TILE), lambda i, j: (i, j))
    return pl.pallas_call(
        body, name="ffn_fwd", grid=(S // ts, D_FF // FF_TILE),
        in_specs=[pl.BlockSpec((ts, D_MODEL), lambda i, j: (i, 0)),
                  pl.BlockSpec((D_MODEL, FF_TILE), lambda i, j: (0, j)),
                  pl.BlockSpec((D_MODEL, FF_TILE), lambda i, j: (0, j)),
                  pl.BlockSpec((FF_TILE, D_MODEL), lambda i, j: (j, 0))],
        out_specs=[act, act, pl.BlockSpec((ts, D_MODEL), lambda i, j: (i, 0))],
        out_shape=[jax.ShapeDtypeStruct((S, D_FF), bf16), jax.ShapeDtypeStruct((S, D_FF), bf16),
                   jax.ShapeDtypeStruct((S, D_MODEL), f32)],
        compiler_params=_params("parallel", "arbitrary"),
    )(h2, w_gate, w_up, w_down)


def _loss_head(f, x2, target, g4):
    S = f.shape[0]
    ts = 512

    def body(f_ref, x2_ref, t_ref, g_ref, dy_ref, df_ref, dg_ref, loss_ref):
        @pl.when(pl.program_id(0) == 0)
        def _():
            dg_ref[...] = jnp.zeros_like(dg_ref)
            loss_ref[...] = jnp.zeros_like(loss_ref)

        fv = f_ref[...]
        g = g_ref[...]
        r = _rms(fv)
        fhat = fv * r
        err = (x2_ref[...] + fhat * g) - t_ref[...]
        loss_ref[...] += 0.5 * jnp.sum(jnp.mean(err * err, axis=-1, keepdims=True), axis=0, keepdims=True)
        dy = err * (1.0 / D_MODEL)
        dy_ref[...] = dy
        dg_ref[...] += jnp.sum(dy * fhat, axis=0, keepdims=True)
        dyg = dy * g
        df_ref[...] = (r * (dyg - fhat * jnp.mean(dyg * fhat, axis=-1, keepdims=True))).astype(bf16)

    row = pl.BlockSpec((ts, D_MODEL), lambda i: (i, 0))
    gain = pl.BlockSpec((1, D_MODEL), lambda i: (0, 0))
    return pl.pallas_call(
        body, name="loss_head", grid=(S // ts,), in_specs=[row, row, row, gain],
        out_specs=[row, row, gain, pl.BlockSpec((1, 1), lambda i: (0, 0))],
        out_shape=[jax.ShapeDtypeStruct((S, D_MODEL), f32), jax.ShapeDtypeStruct((S, D_MODEL), bf16),
                   jax.ShapeDtypeStruct((1, D_MODEL), f32), jax.ShapeDtypeStruct((1, 1), f32)],
        compiler_params=_params("arbitrary"),
    )(f, x2, target, g4)


def _ffn_bwd(df, gate, up, w_gate, w_up, w_down):
    S = df.shape[0]
    ts = 1024

    def body(df_ref, gate_ref, up_ref, wg_ref, wu_ref, wd_ref, a_ref, dgate_ref, dup_ref, dh_ref):
        j = pl.program_id(1)
        da = _dot_nt(df_ref[...], wd_ref[...])
        g = gate_ref[...].astype(f32)
        u = up_ref[...].astype(f32)
        sig = _sigmoid(g)
        silu = g * sig
        a_ref[...] = (silu * u).astype(bf16)
        dup = (da * silu).astype(bf16)
        dgate = (da * u * (sig * (1.0 + g * (1.0 - sig)))).astype(bf16)
        dup_ref[...] = dup
        dgate_ref[...] = dgate
        part = _dot_nt(dgate, wg_ref[...]) + _dot_nt(dup, wu_ref[...])

        @pl.when(j == 0)
        def _():
            dh_ref[...] = part

        @pl.when(j > 0)
        def _():
            dh_ref[...] += part

    act = pl.BlockSpec((ts, FF_TILE), lambda i, j: (i, j))
    row = pl.BlockSpec((ts, D_MODEL), lambda i, j: (i, 0))
    return pl.pallas_call(
        body, name="ffn_bwd", grid=(S // ts, D_FF // FF_TILE),
        in_specs=[row, act, act,
                  pl.BlockSpec((D_MODEL, FF_TILE), lambda i, j: (0, j)),
                  pl.BlockSpec((D_MODEL, FF_TILE), lambda i, j: (0, j)),
                  pl.BlockSpec((FF_TILE, D_MODEL), lambda i, j: (j, 0))],
        out_specs=[act, act, act, row],
        out_shape=[jax.ShapeDtypeStruct((S, D_FF), bf16)] * 3 + [jax.ShapeDtypeStruct((S, D_MODEL), f32)],
        compiler_params=_params("parallel", "arbitrary"),
    )(df, gate, up, w_gate, w_up, w_down)


def _norm_bwd(dh2, dy, x2, mix, g3, g2):
    S = dh2.shape[0]
    ts = 512

    def body(dh_ref, dy_ref, x2_ref, mix_ref, g3_ref, g2_ref, dx2_ref, dmix_ref, dg3_ref, dg2_ref):
        @pl.when(pl.program_id(0) == 0)
        def _():
            dg3_ref[...] = jnp.zeros_like(dg3_ref)
            dg2_ref[...] = jnp.zeros_like(dg2_ref)

        dh = dh_ref[...]
        x2 = x2_ref[...]
        r3 = _rms(x2)
        xhat = x2 * r3
        dg3_ref[...] += jnp.sum(dh * xhat, axis=0, keepdims=True)
        dhg = dh * g3_ref[...]
        dx2 = dy_ref[...] + r3 * (dhg - xhat * jnp.mean(dhg * xhat, axis=-1, keepdims=True))
        dx2_ref[...] = dx2
        mix = mix_ref[...]
        r2 = _rms(mix)
        mhat = mix * r2
        dg2_ref[...] += jnp.sum(dx2 * mhat, axis=0, keepdims=True)
        dmg = dx2 * g2_ref[...]
        dmix_ref[...] = (r2 * (dmg - mhat * jnp.mean(dmg * mhat, axis=-1, keepdims=True))).astype(bf16)

    row = pl.BlockSpec((ts, D_MODEL), lambda i: (i, 0))
    gain = pl.BlockSpec((1, D_MODEL), lambda i: (0, 0))
    return pl.pallas_call(
        body, name="norm_bwd", grid=(S // ts,), in_specs=[row, row, row, row, gain, gain],
        out_specs=[row, row, gain, gain],
        out_shape=[jax.ShapeDtypeStruct((S, D_MODEL), f32), jax.ShapeDtypeStruct((S, D_MODEL), bf16),
                   jax.ShapeDtypeStruct((1, D_MODEL), f32), jax.ShapeDtypeStruct((1, D_MODEL), f32)],
        compiler_params=_params("arbitrary"),
    )(dh2, dy, x2, mix, g3, g2)


def _out_proj_bwd(dmix, w_out, attn_out, head_ones):
    S = dmix.shape[0]
    ts = 512

    def body(dm_ref, w_ref, o_ref, ones_ref, dp_ref, do_ref, dl_ref):
        dcat = _dot_nt(dm_ref[...], w_ref[...])
        dp_ref[...] = dcat[:, :POOL_WIDTH]
        do = dcat[:, POOL_WIDTH:]
        do_ref[...] = do.astype(bf16)
        prod = do * o_ref[...].astype(f32)
        hi = prod.astype(bf16)
        lo = (prod - hi.astype(f32)).astype(bf16)
        dl_ref[...] = _dot(hi, ones_ref[...]) + _dot(lo, ones_ref[...])

    row = lambda w: pl.BlockSpec((ts, w), lambda i: (i, 0))
    return pl.pallas_call(
        body, name="out_proj_bwd", grid=(S // ts,),
        in_specs=[row(D_MODEL), pl.BlockSpec((D_MODEL, D_MODEL), lambda i: (0, 0)), row(ATTN_WIDTH),
                  pl.BlockSpec((ATTN_WIDTH, ATTN_WIDTH), lambda i: (0, 0))],
        out_specs=[row(POOL_WIDTH), row(ATTN_WIDTH), row(ATTN_WIDTH)],
        out_shape=[jax.ShapeDtypeStruct((S, POOL_WIDTH), f32), jax.ShapeDtypeStruct((S, ATTN_WIDTH), bf16),
                   jax.ShapeDtypeStruct((S, ATTN_WIDTH), f32)],
        compiler_params=_params("parallel"),
    )(dmix, w_out, attn_out, head_ones)


def _in_proj_bwd(du, dqs, dks, dvs, cos_t, sin_t, w_in, x, dx2, g1):
    S = x.shape[0]
    ts = 256

    def body(du_ref, dq1, dq2, dq3, dk1, dk2, dk3, dv1, dv2, dv3, cos_ref, sin_ref, w_ref, x_ref, dx2_ref, g_ref,
             gx_ref, dproj_ref, dg_ref):
        @pl.when(pl.program_id(0) == 0)
        def _():
            dg_ref[...] = jnp.zeros_like(dg_ref)

        dproj_ref[:, :POOL_WIDTH] = du_ref[...]
        cos = cos_ref[...]
        sin = sin_ref[...]
        first = _first_half_mask(ts)
        for j in range(ATTN_WIDTH // 128):
            cols = slice(j * 128, (j + 1) * 128)
            for base, (r1, r2, r3) in ((POOL_WIDTH, (dq1, dq2, dq3)), (POOL_WIDTH + ATTN_WIDTH, (dk1, dk2, dk3))):
                g = r1[:, cols].astype(f32) + r2[:, cols].astype(f32) + r3[:, cols].astype(f32)
                pre = g * cos + _rope_partner(g * sin, first)
                dproj_ref[:, base + j * 128: base + (j + 1) * 128] = pre.astype(bf16)
        dv = dv1[...].astype(f32) + dv2[...].astype(f32) + dv3[...].astype(f32)
        dproj_ref[:, POOL_WIDTH + 2 * ATTN_WIDTH:] = dv.astype(bf16)

        dh = _dot_nt(dproj_ref[...], w_ref[...])
        xv = x_ref[...]
        r = _rms(xv)
        xhat = xv * r
        dg_ref[...] += jnp.sum(dh * xhat, axis=0, keepdims=True)
        dhg = dh * g_ref[...]
        gx_ref[...] = dx2_ref[...] + r * (dhg - xhat * jnp.mean(dhg * xhat, axis=-1, keepdims=True))

    row = lambda w: pl.BlockSpec((ts, w), lambda i: (i, 0))
    gain = pl.BlockSpec((1, D_MODEL), lambda i: (0, 0))
    return pl.pallas_call(
        body, name="in_proj_bwd", grid=(S // ts,),
        in_specs=[row(POOL_WIDTH)] + [row(ATTN_WIDTH)] * 9 + [row(128), row(128),
                  pl.BlockSpec((D_MODEL, IN_WIDTH), lambda i: (0, 0)), row(D_MODEL), row(D_MODEL), gain],
        out_specs=[row(D_MODEL), row(IN_WIDTH), gain],
        out_shape=[jax.ShapeDtypeStruct((S, D_MODEL), f32), jax.ShapeDtypeStruct((S, IN_WIDTH), bf16),
                   jax.ShapeDtypeStruct((1, D_MODEL), f32)],
        compiler_params=_params("arbitrary"),
    )(du, *dqs, *dks, *dvs, cos_t, sin_t, w_in, x, dx2, g1)


def _matmul_tn(a, b, tn, name):
    K, M = a.shape
    N = b.shape[1]
    tk = 512

    def body(a_ref, b_ref, o_ref):
        part = _dot_tn(a_ref[...], b_ref[...])

        @pl.when(pl.program_id(1) == 0)
        def _():
            o_ref[...] = part

        @pl.when(pl.program_id(1) > 0)
        def _():
            o_ref[...] += part

    return pl.pallas_call(
        body, name=name, grid=(N // tn, K // tk),
        in_specs=[pl.BlockSpec((tk, M), lambda n, k: (k, 0)), pl.BlockSpec((tk, tn), lambda n, k: (k, n))],
        out_specs=pl.BlockSpec((M, tn), lambda n, k: (0, n)),
        out_shape=jax.ShapeDtypeStruct((M, N), f32),
        compiler_params=_params("parallel", "arbitrary"),
    )(a, b)


def _rope_tables(S):
    half = HEAD_DIM // 2
    freqs = ROPE_THETA ** (-jnp.arange(half, dtype=f32) * (2.0 / HEAD_DIM))
    ang = jnp.arange(S).astype(f32)[:, None] * freqs[None, :]
    cos = jnp.tile(jnp.cos(ang), (1, 4))
    sin = jnp.sin(ang)
    sin = jnp.tile(jnp.concatenate([-sin, sin], axis=1), (1, 2))
    return cos, sin


def _block_diag(w_pool):
    w = jnp.zeros((POOL_WIDTH, POOL_WIDTH), w_pool.dtype)
    for g in range(POOL_WIDTH // POOL_GROUP):
        w = lax.dynamic_update_slice(w, w_pool[g], (g * POOL_GROUP, g * POOL_GROUP))
    return w


def _head_ones():
    head = np.arange(ATTN_WIDTH) // HEAD_DIM
    return jnp.asarray(head[:, None] == head[None, :], dtype=bf16)


def _local_grads(x, target, g1, w_pool, pool_scale, g2, g3, g4, w_in, w_out, w_gate, w_up, w_down):
    S = x.shape[0]
    cos_t, sin_t = _rope_tables(S)
    w_bd = _block_diag(w_pool).astype(bf16)

    h1, u, q, k, v = _in_proj(x, g1, w_in, cos_t, sin_t)
    pool_out = _pool_fwd(u, w_bd, pool_scale)
    branches = [_attn_fwd(q, k, v, d) for d in DILATIONS]
    attn_out, lse = _attn_merge([b[0] for b in branches], [b[1] for b in branches])
    mix, x2, h2 = _out_proj(pool_out, attn_out, w_out, x, g2, g3)
    gate, up, f = _ffn_fwd(h2, w_gate, w_up, w_down)
    dy, df, dg4, loss = _loss_head(f, x2, target, g4)

    a, dgate, dup, dh2 = _ffn_bwd(df, gate, up, w_gate, w_up, w_down)
    d_w_down = _matmul_tn(a, df, D_MODEL, "grad_w_down")
    d_w_gate = _matmul_tn(h2, dgate, FF_HALF, "grad_w_gate")
    d_w_up = _matmul_tn(h2, dup, FF_HALF, "grad_w_up")
    dx2, dmix, dg3, dg2 = _norm_bwd(dh2, dy, x2, mix, g3, g2)
    d_w_out = jnp.concatenate([_matmul_tn(pool_out, dmix, D_MODEL, "grad_w_out_pool"),
                               _matmul_tn(attn_out, dmix, D_MODEL, "grad_w_out_attn")], axis=0)
    dpool, do, delta = _out_proj_bwd(dmix, w_out, attn_out, _head_ones())
    du, d_w_bd, d_scale = _pool_bwd(u, dpool, w_bd, pool_scale)
    parts = [_attn_bwd(q, k, v, do, lse, delta, d) for d in DILATIONS]
    grad_x, dproj, dg1 = _in_proj_bwd(du, [p[0] for p in parts], [p[1] for p in parts], [p[2] for p in parts],
                                      cos_t, sin_t, w_in, x, dx2, g1)
    d_w_in = _matmul_tn(h1, dproj, IN_WIDTH // 2, "grad_w_in")
    d_w_pool = jnp.stack([d_w_bd[g * POOL_GROUP:(g + 1) * POOL_GROUP, g * POOL_GROUP:(g + 1) * POOL_GROUP]
                          for g in range(POOL_WIDTH // POOL_GROUP)])
    large = dict(w_in=d_w_in, w_out=d_w_out, w_gate=d_w_gate, w_up=d_w_up, w_down=d_w_down)
    small = dict(ln_pre_mix=dg1, ln_post_mix=dg2, ln_pre_ffn=dg3, ln_post_ffn=dg4, pool_scale=d_scale, w_pool=d_w_pool)
    return loss, grad_x, large, small


def _place():
    x, y, c = lax.axis_index("x"), lax.axis_index("y"), lax.axis_index("c")
    chips = [(1 - x, y), (x, 1 - y), (1 - x, 1 - y)]
    return x, y, c, chips


ANY = pl.BlockSpec(memory_space=pl.ANY)


def _gather_weights(pack):
    def body(w_ref, out_ref, send1, recv1, send2, recv2, local_sem):
        x, y, c, chips = _place()
        me = 2 * x + y
        sibling = (x, y, 1 - c)
        mine = pl.ds(c * HALF_ROWS, HALF_ROWS)
        other = pl.ds((1 - c) * HALF_ROWS, HALF_ROWS)
        own = pltpu.make_async_copy(w_ref, out_ref.at[me], local_sem)
        own.start()

        def direct(j, chip_xy, src_chip):
            cx, cy = chip_xy
            return pltpu.make_async_remote_copy(
                src_ref=w_ref.at[mine], dst_ref=out_ref.at[src_chip, mine], send_sem=send1.at[j], recv_sem=recv1.at[j],
                device_id=(cx, cy, c), device_id_type=MESH)

        def passed(j, chip, rows):
            return pltpu.make_async_remote_copy(
                src_ref=out_ref.at[chip, rows], dst_ref=out_ref.at[chip, rows], send_sem=send2.at[j],
                recv_sem=recv2.at[j], device_id=sibling, device_id_type=MESH)

        sends = [direct(j, chip, me) for j, chip in enumerate(chips)]
        for cp in sends:
            cp.start()
        forwards = []
        for j, (cx, cy) in enumerate(chips):
            direct(j, (cx, cy), 2 * cx + cy).wait_recv()
            fw = passed(j, 2 * cx + cy, mine)
            fw.start()
            forwards.append(fw)
        for j, (cx, cy) in enumerate(chips):
            passed(j, 2 * cx + cy, other).wait_recv()
        for cp in sends + forwards:
            cp.wait_send()
        own.wait()

    return pl.pallas_call(
        body, name="gather_weights", in_specs=[ANY], out_specs=ANY,
        out_shape=jax.ShapeDtypeStruct((N_CHIPS, PACK_ROWS, D_MODEL), pack.dtype),
        scratch_shapes=[pltpu.SemaphoreType.DMA((3,))] * 4 + [pltpu.SemaphoreType.DMA],
    )(pack)


def _swap_halves(g):
    def body(g_ref, mine_ref, theirs_ref, send_sem, recv_sem, local_sem):
        x, y, c, _ = _place()
        own = pltpu.make_async_copy(g_ref.at[pl.ds(0, N_CHIPS), pl.ds(c * HALF_ROWS, HALF_ROWS)], mine_ref, local_sem)
        own.start()
        cp = pltpu.make_async_remote_copy(
            src_ref=g_ref.at[pl.ds(0, N_CHIPS), pl.ds((1 - c) * HALF_ROWS, HALF_ROWS)], dst_ref=theirs_ref, send_sem=send_sem,
            recv_sem=recv_sem, device_id=(x, y, 1 - c), device_id_type=MESH)
        cp.start()
        cp.wait()
        own.wait()

    half = jax.ShapeDtypeStruct((N_CHIPS, HALF_ROWS, D_MODEL), g.dtype)
    return pl.pallas_call(
        body, name="swap_halves", in_specs=[ANY], out_specs=[ANY, ANY], out_shape=[half, half],
        scratch_shapes=[pltpu.SemaphoreType.DMA] * 3,
    )(g)


def _scatter_to_chips(h):
    def body(h_ref, out_ref, send, recv, local_sem):
        x, y, c, chips = _place()
        me = 2 * x + y
        own = pltpu.make_async_copy(h_ref.at[me], out_ref.at[me], local_sem)
        own.start()
        sends = []
        for j, (cx, cy) in enumerate(chips):
            cp = pltpu.make_async_remote_copy(
                src_ref=h_ref.at[2 * cx + cy], dst_ref=out_ref.at[me], send_sem=send.at[j], recv_sem=recv.at[j],
                device_id=(cx, cy, c), device_id_type=MESH)
            cp.start()
            sends.append(cp)
        for j, (cx, cy) in enumerate(chips):
            pltpu.make_async_remote_copy(
                src_ref=h_ref.at[me], dst_ref=out_ref.at[2 * cx + cy], send_sem=send.at[j], recv_sem=recv.at[j],
                device_id=(cx, cy, c), device_id_type=MESH).wait_recv()
        for cp in sends:
            cp.wait_send()
        own.wait()

    return pl.pallas_call(
        body, name="scatter_to_chips", in_specs=[ANY], out_specs=ANY,
        out_shape=jax.ShapeDtypeStruct(h.shape, h.dtype),
        scratch_shapes=[pltpu.SemaphoreType.DMA((3,))] * 2 + [pltpu.SemaphoreType.DMA],
    )(h)


def _join_halves(r):
    def body(r_ref, out_ref, send_sem, recv_sem, local_sem):
        x, y, c, _ = _place()
        own = pltpu.make_async_copy(r_ref, out_ref.at[c], local_sem)
        own.start()
        cp = pltpu.make_async_remote_copy(
            src_ref=r_ref, dst_ref=out_ref.at[c], send_sem=send_sem, recv_sem=recv_sem,
            device_id=(x, y, 1 - c), device_id_type=MESH)
        cp.start()
        cp.wait_send()
        pltpu.make_async_remote_copy(
            src_ref=r_ref, dst_ref=out_ref.at[1 - c], send_sem=send_sem, recv_sem=recv_sem,
            device_id=(x, y, 1 - c), device_id_type=MESH).wait_recv()
        own.wait()

    return pl.pallas_call(
        body, name="join_halves", in_specs=[ANY], out_specs=ANY,
        out_shape=jax.ShapeDtypeStruct((2,) + r.shape, r.dtype),
        scratch_shapes=[pltpu.SemaphoreType.DMA] * 3,
    )(r)


def _add_slabs(terms, k, name):
    rows = terms[0][0].shape[1]
    tr = 376
    n = len(terms)

    def body(*refs):
        acc = refs[0][...]
        for r in refs[1:n]:
            acc = acc + r[...]
        refs[n][...] = acc

    slab = lambda first: pl.BlockSpec((1, tr, D_MODEL), lambda i, t: (first + i, t, 0))
    return pl.pallas_call(
        body, name=name, grid=(k, rows // tr), in_specs=[slab(first) for _, first in terms],
        out_specs=pl.BlockSpec((1, tr, D_MODEL), lambda i, t: (i, t, 0)),
        out_shape=jax.ShapeDtypeStruct((k, rows, D_MODEL), terms[0][0].dtype),
        compiler_params=_params("parallel", "parallel"),
    )(*[a for a, _ in terms])


def _sum_small(block):
    def body(b_ref, out_ref, gathered, send, recv):
        x, y, c, _ = _place()
        me = 4 * x + 2 * y + c
        gathered[me] = b_ref[...]
        sends = []
        for kk in range(1, N_DEV):
            flip = lambda v, bit: 1 - v if bit else v
            peer = (flip(x, kk & 4), flip(y, kk & 2), flip(c, kk & 1))
            cp = pltpu.make_async_remote_copy(
                src_ref=b_ref, dst_ref=gathered.at[me], send_sem=send.at[kk - 1], recv_sem=recv.at[kk - 1],
                device_id=peer, device_id_type=MESH)
            cp.start()
            sends.append(cp)
        for kk in range(1, N_DEV):
            peer_index = jnp.bitwise_xor(me, kk)
            pltpu.make_async_remote_copy(
                src_ref=b_ref, dst_ref=gathered.at[peer_index], send_sem=send.at[kk - 1], recv_sem=recv.at[kk - 1],
                device_id=(x, y, c), device_id_type=MESH).wait_recv()
        for cp in sends:
            cp.wait_send()
        acc = gathered[0]
        for dev in range(1, N_DEV):
            acc = acc + gathered[dev]
        out_ref[...] = acc

    vmem = pl.BlockSpec(memory_space=pltpu.VMEM)
    return pl.pallas_call(
        body, name="sum_small", in_specs=[vmem], out_specs=vmem,
        out_shape=jax.ShapeDtypeStruct(block.shape, block.dtype),
        scratch_shapes=[pltpu.VMEM((N_DEV,) + block.shape, block.dtype),
                        pltpu.SemaphoreType.DMA((N_DEV - 1,)), pltpu.SemaphoreType.DMA((N_DEV - 1,))],
    )(block)


def _adamw(w, g, m, v, name):
    rows, cols = w.shape
    tr = rows
    for cand in (512, 256, 128, 64, 32, 16, 8):
        if rows % cand == 0:
            tr = cand
            break
    c1 = 1.0 - ADAM_B1 ** ADAM_STEP
    c2 = 1.0 - ADAM_B2 ** ADAM_STEP

    def body(w_ref, g_ref, m_ref, v_ref, d_ref, nm_ref, nv_ref):
        gv = g_ref[...]
        nm = ADAM_B1 * m_ref[...] + (1.0 - ADAM_B1) * gv
        nv = ADAM_B2 * v_ref[...] + (1.0 - ADAM_B2) * (gv * gv)
        nm_ref[...] = nm
        nv_ref[...] = nv
        d_ref[...] = -ADAM_LR * ((nm / c1) / (jnp.sqrt(nv / c2) + ADAM_EPS) + ADAM_WD * w_ref[...])

    blk = pl.BlockSpec((tr, cols), lambda i: (i, 0))
    shape = jax.ShapeDtypeStruct((rows, cols), f32)
    return pl.pallas_call(
        body, name=name, grid=(rows // tr,), in_specs=[blk] * 4, out_specs=[blk] * 3, out_shape=[shape] * 3,
        compiler_params=_params("parallel"),
    )(w, g, m, v)


LARGE = ("w_in", "w_out", "w_gate", "w_up", "w_down")
SMALL = ("ln_pre_mix", "ln_post_mix", "ln_pre_ffn", "ln_post_ffn", "pool_scale", "w_pool")
COLUMN_SHARDED = {"w_in": IN_WIDTH // N_CHIPS, "w_gate": D_FF // N_CHIPS, "w_up": D_FF // N_CHIPS}


def _pack_shard(shards):
    return jnp.concatenate([shards[n].reshape(-1, D_MODEL) for n in LARGE], axis=0)


def _unpack_shard(pack, shapes):
    out, row = {}, 0
    for n, rows in zip(LARGE, PACK_SPLITS):
        out[n] = pack[row:row + rows].reshape(shapes[n])
        row += rows
    return out


def _whole_from_shards(packs):
    out, row = {}, 0
    for n, rows in zip(LARGE, PACK_SPLITS):
        part = packs[:, row:row + rows]
        if n in COLUMN_SHARDED:
            width = COLUMN_SHARDED[n]
            part = part.reshape(N_CHIPS, D_MODEL, width).transpose(1, 0, 2).reshape(D_MODEL, N_CHIPS * width)
        else:
            part = part.reshape(N_CHIPS * rows, D_MODEL)
        out[n] = part
        row += rows
    return out


def _shards_from_whole(grads):
    parts = []
    for n, rows in zip(LARGE, PACK_SPLITS):
        g = grads[n]
        if n in COLUMN_SHARDED:
            width = COLUMN_SHARDED[n]
            g = g.reshape(D_MODEL, N_CHIPS, width).transpose(1, 0, 2)
        parts.append(g.reshape(N_CHIPS, rows, D_MODEL))
    return jnp.concatenate(parts, axis=1)


def _pack_small(vals):
    rows = [vals[n].reshape(1, D_MODEL) for n in SMALL[:4]]
    rows.append(jnp.pad(vals["pool_scale"].reshape(1, POOL_WIDTH), ((0, 0), (0, D_MODEL - POOL_WIDTH))))
    rows.append(jnp.pad(vals["loss"].reshape(1, 1), ((0, 0), (0, D_MODEL - 1))))
    rows.append(jnp.zeros((2, D_MODEL), f32))
    rows.append(vals["w_pool"].reshape(16, D_MODEL))
    return jnp.concatenate(rows, axis=0)


def _unpack_small(block):
    out = {n: block[i:i + 1] for i, n in enumerate(SMALL[:4])}
    out["pool_scale"] = block[4:5, :POOL_WIDTH]
    out["loss"] = block[5, 0]
    out["w_pool"] = block[8:24].reshape(1, 4, POOL_GROUP, POOL_GROUP)
    return out


def kernel(x, ln_pre_mix, w_in, w_pool, pool_scale, w_out, ln_post_mix, ln_pre_ffn, w_gate, w_up, w_down, ln_post_ffn, loss_target, m_ln_pre_mix, m_w_in, m_w_pool, m_pool_scale, m_w_out, m_ln_post_mix, m_ln_pre_ffn, m_w_gate, m_w_up, m_w_down, m_ln_post_ffn, v_ln_pre_mix, v_w_in, v_w_pool, v_pool_scale, v_w_out, v_ln_post_mix, v_ln_pre_ffn, v_w_gate, v_w_up, v_w_down, v_ln_post_ffn):
    w = dict(ln_pre_mix=ln_pre_mix, w_in=w_in, w_pool=w_pool, pool_scale=pool_scale, w_out=w_out,
             ln_post_mix=ln_post_mix, ln_pre_ffn=ln_pre_ffn, w_gate=w_gate, w_up=w_up, w_down=w_down,
             ln_post_ffn=ln_post_ffn)
    m = dict(ln_pre_mix=m_ln_pre_mix, w_in=m_w_in, w_pool=m_w_pool, pool_scale=m_pool_scale, w_out=m_w_out,
             ln_post_mix=m_ln_post_mix, ln_pre_ffn=m_ln_pre_ffn, w_gate=m_w_gate, w_up=m_w_up, w_down=m_w_down,
             ln_post_ffn=m_ln_post_ffn)
    v = dict(ln_pre_mix=v_ln_pre_mix, w_in=v_w_in, w_pool=v_w_pool, pool_scale=v_pool_scale, w_out=v_w_out,
             ln_post_mix=v_ln_post_mix, ln_pre_ffn=v_ln_pre_ffn, w_gate=v_w_gate, w_up=v_w_up, w_down=v_w_down,
             ln_post_ffn=v_ln_post_ffn)

    packs = _gather_weights(_pack_shard({n: w[n][0].astype(bf16) for n in LARGE}))
    whole = _whole_from_shards(packs)

    loss, grad_x, large, small = _local_grads(
        x[0], loss_target[0], ln_pre_mix, w_pool[0], pool_scale, ln_post_mix, ln_pre_ffn, ln_post_ffn,
        whole["w_in"], whole["w_out"], whole["w_gate"], whole["w_up"], whole["w_down"])

    mine, theirs = _swap_halves(_shards_from_whole(large))
    chip_sum = _add_slabs([(mine, 0), (theirs, 0)], N_CHIPS, "add_cores")
    pieces = _scatter_to_chips(chip_sum)
    reduced_half = _add_slabs([(pieces, j) for j in range(N_CHIPS)], 1, "add_chips")[0]
    reduced = _join_halves(reduced_half).reshape(PACK_ROWS, D_MODEL)
    shapes = {n: w[n].shape[1:] for n in LARGE}
    grads = _unpack_shard(reduced, shapes)

    total = _unpack_small(_sum_small(_pack_small(dict(small, loss=loss))))
    for n in SMALL:
        grads[n] = total[n]

    delta, new_m, new_v = {}, {}, {}
    for n in LARGE:
        delta[n], new_m[n], new_v[n] = _adamw(w[n][0], grads[n], m[n][0], v[n][0], "adamw_" + n)
    small_state = [_pack_small(dict({n: s[n] for n in SMALL}, loss=jnp.zeros((), f32))) for s in (w, m, v)]
    small_grad = _pack_small(dict({n: grads[n] for n in SMALL}, loss=jnp.zeros((), f32)))
    sd, sm, sv = _adamw(small_state[0], small_grad, small_state[1], small_state[2], "adamw_small")
    for out, block in ((delta, sd), (new_m, sm), (new_v, sv)):
        un = _unpack_small(block)
        for n in SMALL:
            out[n] = un[n]

    names = ("ln_pre_mix", "w_in", "w_pool", "pool_scale", "w_out", "ln_post_mix", "ln_pre_ffn", "w_gate", "w_up",
             "w_down", "ln_post_ffn")
    full = lambda d: [d[n].reshape(w[n].shape) for n in names]
    return (total["loss"], grad_x[None], *full(grads), *full(delta), *full(new_m), *full(new_v))
```

```python
import numpy as np
import jax
import jax.numpy as jnp
from jax import lax
from jax.experimental import pallas as pl
from jax.experimental.pallas import tpu as pltpu

D_MODEL = 1024
POOL_WIDTH = 256
POOL_GROUP = 64
ATTN_WIDTH = 768
HEAD_DIM = 64
IN_WIDTH = 2560
D_FF = 2816
BLOCK = 128
DILATIONS = (1, 4, 16)
ROPE_THETA = 10000.0
EPS = 1e-6
ATTN_SCALE = 0.125
NEG = -1e30

ADAM_LR = 0.001
ADAM_B1 = 0.9
ADAM_B2 = 0.999
ADAM_EPS = 1e-08
ADAM_WD = 0.01
ADAM_STEP = 10

N_CHIPS = 4
N_DEV = 8
PACK_SPLITS = (640, 256, 704, 704, 704)
PACK_ROWS = sum(PACK_SPLITS)
HALF_ROWS = PACK_ROWS // 2
SMALL_ROWS = 24

VMEM_LIMIT_V7X = 56 * 1024 * 1024
MESH = pl.DeviceIdType.MESH

f32 = jnp.float32
bf16 = jnp.bfloat16


def _params(*sem):
    return pltpu.CompilerParams(dimension_semantics=sem, vmem_limit_bytes=VMEM_LIMIT_V7X)


def _dot(a, b):
    return jnp.dot(a, b, preferred_element_type=f32)


def _dot_nt(a, b):
    return lax.dot_general(a, b, (((1,), (1,)), ((), ())), preferred_element_type=f32)


def _dot_tn(a, b):
    return lax.dot_general(a, b, (((0,), (0,)), ((), ())), preferred_element_type=f32)


def _rope_partner(a, first_half):
    return jnp.where(first_half, pltpu.roll(a, 96, 1), pltpu.roll(a, 32, 1))


def _first_half_mask(rows):
    lane = lax.broadcasted_iota(jnp.int32, (rows, 128), 1)
    return (lane % HEAD_DIM) < (HEAD_DIM // 2)


def _in_proj(x, g1, w_in, cos_t, sin_t):
    S = x.shape[0]
    ts = 512

    def body(x_ref, g_ref, w_ref, cos_ref, sin_ref, h_ref, u_ref, q_ref, k_ref, v_ref):
        xv = x_ref[...]
        r = lax.rsqrt(jnp.mean(xv * xv, axis=-1, keepdims=True) + EPS)
        h = ((xv * r) * g_ref[...]).astype(bf16)
        h_ref[...] = h
        proj = _dot(h, w_ref[...])
        u_ref[...] = proj[:, :POOL_WIDTH]
        cos = cos_ref[...]
        sin = sin_ref[...]
        first = _first_half_mask(ts)
        for j in range(ATTN_WIDTH // 128):
            for base, ref in ((POOL_WIDTH, q_ref), (POOL_WIDTH + ATTN_WIDTH, k_ref)):
                a = proj[:, base + j * 128: base + (j + 1) * 128]
                ref[:, j * 128:(j + 1) * 128] = (a * cos + _rope_partner(a, first) * sin).astype(bf16)
        v_ref[...] = proj[:, POOL_WIDTH + 2 * ATTN_WIDTH:].astype(bf16)

    row = lambda w: pl.BlockSpec((ts, w), lambda i: (i, 0))
    return pl.pallas_call(
        body, name="in_proj", grid=(S // ts,),
        in_specs=[row(D_MODEL), pl.BlockSpec((1, D_MODEL), lambda i: (0, 0)),
                  pl.BlockSpec((D_MODEL, IN_WIDTH), lambda i: (0, 0)), row(128), row(128)],
        out_specs=[row(D_MODEL), row(POOL_WIDTH), row(ATTN_WIDTH), row(ATTN_WIDTH), row(ATTN_WIDTH)],
        out_shape=[jax.ShapeDtypeStruct((S, D_MODEL), bf16), jax.ShapeDtypeStruct((S, POOL_WIDTH), f32),
                   jax.ShapeDtypeStruct((S, ATTN_WIDTH), bf16), jax.ShapeDtypeStruct((S, ATTN_WIDTH), bf16),
                   jax.ShapeDtypeStruct((S, ATTN_WIDTH), bf16)],
        compiler_params=_params("parallel"),
    )(x, g1, w_in, cos_t, sin_t)


POOL_HALO = 16


def _pool_lane_group(rows):
    return lax.broadcasted_iota(jnp.int32, (rows, POOL_WIDTH), 1) // POOL_GROUP


def _pool_select(group, s2, s4, s8, s16):
    return jnp.where(group == 0, s2, jnp.where(group == 1, s4, jnp.where(group == 2, s8, s16)))


def _pool_count(t0, rows):
    group = _pool_lane_group(rows)
    t = t0 + lax.broadcasted_iota(jnp.int32, (rows, POOL_WIDTH), 0)
    win = _pool_select(group, 2, 4, 8, 16)
    return jnp.minimum(t + 1, win).astype(f32)


def _pool_diff(u_halo, u_tile, t0):
    ts = u_tile.shape[0]
    ext = jnp.concatenate([u_halo, u_tile], axis=0)
    s2 = ext + pltpu.roll(ext, 1, 0)
    s4 = s2 + pltpu.roll(s2, 2, 0)
    s8 = s4 + pltpu.roll(s4, 4, 0)
    s16 = s8 + pltpu.roll(s8, 8, 0)
    group = _pool_lane_group(ts + POOL_HALO)
    wsum = _pool_select(group, s2, s4, s8, s16)[POOL_HALO:]
    return wsum / _pool_count(t0, ts) - u_tile


def _pool_specs(ts, n_tiles):
    tile = pl.BlockSpec((ts, POOL_WIDTH), lambda i: (i, 0))
    per = ts // POOL_HALO
    before = pl.BlockSpec((POOL_HALO, POOL_WIDTH), lambda i: (jnp.maximum(i * per - 1, 0), 0))
    after = pl.BlockSpec((POOL_HALO, POOL_WIDTH), lambda i: (jnp.minimum((i + 1) * per, n_tiles * per - 1), 0))
    return tile, before, after


def _pool_fwd(u, w_bd, scale):
    S = u.shape[0]
    ts = 512
    n_tiles = S // ts

    def body(u_ref, halo_ref, w_ref, sc_ref, y_ref):
        i = pl.program_id(0)
        halo = jnp.where(i > 0, halo_ref[...], 0.0)
        d = _pool_diff(halo, u_ref[...], i * ts)
        y_ref[...] = (_dot(d.astype(bf16), w_ref[...]) * sc_ref[...]).astype(bf16)

    tile, before, _ = _pool_specs(ts, n_tiles)
    return pl.pallas_call(
        body, name="pool_fwd", grid=(n_tiles,),
        in_specs=[tile, before, pl.BlockSpec((POOL_WIDTH, POOL_WIDTH), lambda i: (0, 0)),
                  pl.BlockSpec((1, POOL_WIDTH), lambda i: (0, 0))],
        out_specs=tile, out_shape=jax.ShapeDtypeStruct((S, POOL_WIDTH), bf16),
        compiler_params=_params("parallel"),
    )(u, u, w_bd, scale)


def _pool_bwd(u, dy, w_bd, scale):
    S = u.shape[0]
    ts = 512
    n_tiles = S // ts

    def body(u_ref, halo_ref, dy_ref, dy_next_ref, w_ref, sc_ref, du_ref, dw_ref, dsc_ref):
        i = pl.program_id(0)

        @pl.when(i == 0)
        def _():
            dw_ref[...] = jnp.zeros_like(dw_ref)
            dsc_ref[...] = jnp.zeros_like(dsc_ref)

        halo = jnp.where(i > 0, halo_ref[...], 0.0)
        d = _pool_diff(halo, u_ref[...], i * ts).astype(bf16)
        w = w_ref[...]
        sc = sc_ref[...]
        dy_tile = dy_ref[...]
        z = _dot(d, w)
        dsc_ref[...] += jnp.sum(dy_tile * z, axis=0, keepdims=True)
        dy_next = jnp.where(i < n_tiles - 1, dy_next_ref[...], 0.0)
        dz = (jnp.concatenate([dy_tile, dy_next], axis=0) * sc).astype(bf16)
        dw_ref[...] += _dot_tn(d, dz[:ts])
        dd = _dot_nt(dz, w)
        e = dd / _pool_count(i * ts, ts + POOL_HALO)
        n = ts + POOL_HALO
        f2 = e + pltpu.roll(e, n - 1, 0)
        f4 = f2 + pltpu.roll(f2, n - 2, 0)
        f8 = f4 + pltpu.roll(f4, n - 4, 0)
        f16 = f8 + pltpu.roll(f8, n - 8, 0)
        fsum = _pool_select(_pool_lane_group(n), f2, f4, f8, f16)
        du_ref[...] = (fsum[:ts] - dd[:ts]).astype(bf16)

    tile, before, after = _pool_specs(ts, n_tiles)
    return pl.pallas_call(
        body, name="pool_bwd", grid=(n_tiles,),
        in_specs=[tile, before, tile, after, pl.BlockSpec((POOL_WIDTH, POOL_WIDTH), lambda i: (0, 0)),
                  pl.BlockSpec((1, POOL_WIDTH), lambda i: (0, 0))],
        out_specs=[tile, pl.BlockSpec((POOL_WIDTH, POOL_WIDTH), lambda i: (0, 0)),
                   pl.BlockSpec((1, POOL_WIDTH), lambda i: (0, 0))],
        out_shape=[jax.ShapeDtypeStruct((S, POOL_WIDTH), bf16), jax.ShapeDtypeStruct((POOL_WIDTH, POOL_WIDTH), f32),
                   jax.ShapeDtypeStruct((1, POOL_WIDTH), f32)],
        compiler_params=_params("arbitrary"),
    )(u, u, dy, dy, w_bd, scale)


def _band_mask(n):
    qi = lax.broadcasted_iota(jnp.int32, (BLOCK, 2 * BLOCK), 0)
    kj = lax.broadcasted_iota(jnp.int32, (BLOCK, 2 * BLOCK), 1)
    return (kj >= qi) & (kj <= qi + BLOCK) & ((kj >= BLOCK) | (n > 0))


def _head0_mask(rows=BLOCK):
    return lax.broadcasted_iota(jnp.int32, (rows, 128), 1) < HEAD_DIM


def _per_head(stat, h0, h):
    other = pltpu.roll(stat, HEAD_DIM, 1)
    full = jnp.where(h0, stat, other) if h == 0 else jnp.where(h0, other, stat)
    return jnp.concatenate([full, full], axis=1)


def _attn_specs(L, d):
    nb = L // BLOCK
    cur = pl.BlockSpec((BLOCK, ATTN_WIDTH), lambda r, n: (n, r))
    prev = pl.BlockSpec((BLOCK, ATTN_WIDTH), lambda r, n: (jnp.maximum(n - 1, 0), r))
    return nb, cur, prev


def _attn_fwd(q, k, v, d):
    S = q.shape[0]
    L = S // d
    view = lambda a: a.reshape(L, d * ATTN_WIDTH)

    def body(q_ref, kc_ref, kp_ref, vc_ref, vp_ref, o_ref, lse_ref):
        valid = _band_mask(pl.program_id(1))
        h0 = _head0_mask()
        for j in range(ATTN_WIDTH // 128):
            cols = slice(j * 128, (j + 1) * 128)
            qv = q_ref[:, cols]
            kb = jnp.concatenate([kp_ref[:, cols], kc_ref[:, cols]], axis=0)
            vb = jnp.concatenate([vp_ref[:, cols], vc_ref[:, cols]], axis=0)
            outs, lses = [], []
            for h in range(2):
                keep = h0 if h == 0 else jnp.logical_not(h0)
                qh = jnp.where(keep, qv, jnp.zeros_like(qv))
                s = jnp.where(valid, _dot_nt(qh, kb) * ATTN_SCALE, NEG)
                m = jnp.max(s, axis=1, keepdims=True)
                e = jnp.exp(s - m)
                den = jnp.sum(e, axis=1, keepdims=True)
                outs.append(_dot((e / den).astype(bf16), vb))
                lses.append(jnp.broadcast_to(m + jnp.log(den), (BLOCK, 128)))
            o_ref[:, cols] = jnp.where(h0, outs[0], outs[1])
            lse_ref[:, cols] = jnp.where(h0, lses[0], lses[1])

    nb, cur, prev = _attn_specs(L, d)
    o, lse = pl.pallas_call(
        body, name=f"attn_fwd_d{d}", grid=(d, nb),
        in_specs=[cur, cur, prev, cur, prev], out_specs=[cur, cur],
        out_shape=[jax.ShapeDtypeStruct((L, d * ATTN_WIDTH), f32)] * 2,
        compiler_params=_params("parallel", "parallel"),
    )(view(q), view(k), view(k), view(v), view(v))
    return o.reshape(S, ATTN_WIDTH), lse.reshape(S, ATTN_WIDTH)


def _attn_merge(os, lses):
    S = os[0].shape[0]
    ts = 512

    def body(o1, o2, o3, l1, l2, l3, out_ref, lse_ref):
        a, b, c = l1[...], l2[...], l3[...]
        m = jnp.maximum(jnp.maximum(a, b), c)
        ea, eb, ec = jnp.exp(a - m), jnp.exp(b - m), jnp.exp(c - m)
        tot = ea + eb + ec
        out_ref[...] = ((ea / tot) * o1[...] + (eb / tot) * o2[...] + (ec / tot) * o3[...]).astype(bf16)
        lse_ref[...] = m + jnp.log(tot)

    row = pl.BlockSpec((ts, ATTN_WIDTH), lambda i: (i, 0))
    return pl.pallas_call(
        body, name="attn_merge", grid=(S // ts,), in_specs=[row] * 6, out_specs=[row, row],
        out_shape=[jax.ShapeDtypeStruct((S, ATTN_WIDTH), bf16), jax.ShapeDtypeStruct((S, ATTN_WIDTH), f32)],
        compiler_params=_params("parallel"),
    )(*os, *lses)


def _attn_bwd(q, k, v, do, lse, delta, d):
    S = q.shape[0]
    L = S // d
    nb = L // BLOCK
    view = lambda a: a.reshape(L, d * ATTN_WIDTH)

    def body(q_ref, kc_ref, kp_ref, vc_ref, vp_ref, do_ref, lse_ref, dl_ref, dq_ref, dk_ref, dv_ref, ck_ref, cv_ref):
        n = pl.program_id(1)

        @pl.when(n < nb)
        def _():
            valid = _band_mask(n)
            h0 = _head0_mask()
            h0_band = _head0_mask(2 * BLOCK)
            for j in range(ATTN_WIDTH // 128):
                cols = slice(j * 128, (j + 1) * 128)
                qv = q_ref[:, cols]
                dov = do_ref[:, cols]
                kb = jnp.concatenate([kp_ref[:, cols], kc_ref[:, cols]], axis=0)
                vb = jnp.concatenate([vp_ref[:, cols], vc_ref[:, cols]], axis=0)
                lse_v = lse_ref[:, cols]
                dl_v = dl_ref[:, cols]
                dqs, dks, dvs = [], [], []
                for h in range(2):
                    keep = h0 if h == 0 else jnp.logical_not(h0)
                    qh = jnp.where(keep, qv, jnp.zeros_like(qv))
                    doh = jnp.where(keep, dov, jnp.zeros_like(dov))
                    s = jnp.where(valid, _dot_nt(qh, kb) * ATTN_SCALE, NEG)
                    p = jnp.exp(s - _per_head(lse_v, h0, h))
                    dp = _dot_nt(doh, vb)
                    ds = (p * (dp - _per_head(dl_v, h0, h)) * ATTN_SCALE).astype(bf16)
                    dqs.append(_dot(ds, kb))
                    dks.append(_dot_tn(ds, qv))
                    dvs.append(_dot_tn(p.astype(bf16), dov))
                dq_ref[:, cols] = jnp.where(h0, dqs[0], dqs[1]).astype(bf16)
                dkb = jnp.where(h0_band, dks[0], dks[1])
                dvb = jnp.where(h0_band, dvs[0], dvs[1])

                @pl.when(n > 0)
                def _():
                    dk_ref[:, cols] = (ck_ref[:, cols] + dkb[:BLOCK]).astype(bf16)
                    dv_ref[:, cols] = (cv_ref[:, cols] + dvb[:BLOCK]).astype(bf16)

                ck_ref[:, cols] = dkb[BLOCK:]
                cv_ref[:, cols] = dvb[BLOCK:]

        @pl.when(n == nb)
        def _():
            dk_ref[...] = ck_ref[...].astype(bf16)
            dv_ref[...] = cv_ref[...].astype(bf16)

    last = nb - 1
    cur = pl.BlockSpec((BLOCK, ATTN_WIDTH), lambda r, n: (jnp.minimum(n, last), r))
    prev = pl.BlockSpec((BLOCK, ATTN_WIDTH), lambda r, n: (jnp.clip(n - 1, 0, last), r))
    out = jax.ShapeDtypeStruct((L, d * ATTN_WIDTH), bf16)
    dq, dk, dv = pl.pallas_call(
        body, name=f"attn_bwd_d{d}", grid=(d, nb + 1),
        in_specs=[cur, cur, prev, cur, prev, cur, cur, cur], out_specs=[cur, prev, prev],
        out_shape=[out, out, out],
        scratch_shapes=[pltpu.VMEM((BLOCK, ATTN_WIDTH), f32), pltpu.VMEM((BLOCK, ATTN_WIDTH), f32)],
        compiler_params=_params("parallel", "arbitrary"),
    )(view(q), view(k), view(k), view(v), view(v), view(do), view(lse), view(delta))
    return dq.reshape(S, ATTN_WIDTH), dk.reshape(S, ATTN_WIDTH), dv.reshape(S, ATTN_WIDTH)


def _rms(v):
    return lax.rsqrt(jnp.mean(v * v, axis=-1, keepdims=True) + EPS)


def _out_proj(pool_out, attn_out, w_out, x, g2, g3):
    S = x.shape[0]
    ts = 512

    def body(p_ref, a_ref, w_ref, x_ref, g2_ref, g3_ref, mix_ref, x2_ref, h2_ref):
        mix = _dot(p_ref[...], w_ref[:POOL_WIDTH, :]) + _dot(a_ref[...], w_ref[POOL_WIDTH:, :])
        mix_ref[...] = mix
        x2 = x_ref[...] + (mix * _rms(mix)) * g2_ref[...]
        x2_ref[...] = x2
        h2_ref[...] = ((x2 * _rms(x2)) * g3_ref[...]).astype(bf16)

    row = lambda w: pl.BlockSpec((ts, w), lambda i: (i, 0))
    gain = pl.BlockSpec((1, D_MODEL), lambda i: (0, 0))
    return pl.pallas_call(
        body, name="out_proj", grid=(S // ts,),
        in_specs=[row(POOL_WIDTH), row(ATTN_WIDTH), pl.BlockSpec((D_MODEL, D_MODEL), lambda i: (0, 0)),
                  row(D_MODEL), gain, gain],
        out_specs=[row(D_MODEL)] * 3,
        out_shape=[jax.ShapeDtypeStruct((S, D_MODEL), f32), jax.ShapeDtypeStruct((S, D_MODEL), f32),
                   jax.ShapeDtypeStruct((S, D_MODEL), bf16)],
        compiler_params=_params("parallel"),
    )(pool_out, attn_out, w_out, x, g2, g3)


FF_TILE = 256
FF_HALF = D_FF // 2


def _sigmoid(g):
    return 1.0 / (1.0 + jnp.exp(-g))


def _ffn_fwd(h2, w_gate, w_up, w_down):
    S = h2.shape[0]
    ts = 1024

    def body(h_ref, wg_ref, wu_ref, wd_ref, gate_ref, up_ref, f_ref):
        j = pl.program_id(1)
        h = h_ref[...]
        gate = _dot(h, wg_ref[...])
        up = _dot(h, wu_ref[...])
        gate_ref[...] = gate.astype(bf16)
        up_ref[...] = up.astype(bf16)
        part = _dot((gate * _sigmoid(gate) * up).astype(bf16), wd_ref[...])

        @pl.when(j == 0)
        def _():
            f_ref[...] = part

        @pl.when(j > 0)
        def _():
            f_ref[...] += part

    act = pl.BlockSpec((ts, FF_TILE), lambda i, j: (i, j))
    return pl.pallas_call(
        body, name="ffn_fwd", grid=(S // ts, D_FF // FF_TILE),
        in_specs=[pl.BlockSpec((ts, D_MODEL), lambda i, j: (i, 0)),
                  pl.BlockSpec((D_MODEL, FF_TILE), lambda i, j: (0, j)),
                  pl.BlockSpec((D_MODEL, FF_TILE), lambda i, j: (0, j)),
                  pl.BlockSpec((FF_TILE, D_MODEL), lambda i, j: (j, 0))],
        out_specs=[act, act, pl.BlockSpec((ts, D_MODEL), lambda i, j: (i, 0))],
        out_shape=[jax.ShapeDtypeStruct((S, D_FF), bf16), jax.ShapeDtypeStruct((S, D_FF), bf16),
                   jax.ShapeDtypeStruct((S, D_MODEL), f32)],
        compiler_params=_params("parallel", "arbitrary"),
    )(h2, w_gate, w_up, w_down)


def _loss_head(f, x2, target, g4):
    S = f.shape[0]
    ts = 512

    def body(f_ref, x2_ref, t_ref, g_ref, dy_ref, df_ref, dg_ref, loss_ref):
        @pl.when(pl.program_id(0) == 0)
        def _():
            dg_ref[...] = jnp.zeros_like(dg_ref)
            loss_ref[...] = jnp.zeros_like(loss_ref)

        fv = f_ref[...]
        g = g_ref[...]
        r = _rms(fv)
        fhat = fv * r
        err = (x2_ref[...] + fhat * g) - t_ref[...]
        loss_ref[...] += 0.5 * jnp.sum(jnp.mean(err * err, axis=-1, keepdims=True), axis=0, keepdims=True)
        dy = err * (1.0 / D_MODEL)
        dy_ref[...] = dy
        dg_ref[...] += jnp.sum(dy * fhat, axis=0, keepdims=True)
        dyg = dy * g
        df_ref[...] = (r * (dyg - fhat * jnp.mean(dyg * fhat, axis=-1, keepdims=True))).astype(bf16)

    row = pl.BlockSpec((ts, D_MODEL), lambda i: (i, 0))
    gain = pl.BlockSpec((1, D_MODEL), lambda i: (0, 0))
    return pl.pallas_call(
        body, name="loss_head", grid=(S // ts,), in_specs=[row, row, row, gain],
        out_specs=[row, row, gain, pl.BlockSpec((1, 1), lambda i: (0, 0))],
        out_shape=[jax.ShapeDtypeStruct((S, D_MODEL), f32), jax.ShapeDtypeStruct((S, D_MODEL), bf16),
                   jax.ShapeDtypeStruct((1, D_MODEL), f32), jax.ShapeDtypeStruct((1, 1), f32)],
        compiler_params=_params("arbitrary"),
    )(f, x2, target, g4)


def _ffn_bwd(df, gate, up, w_gate, w_up, w_down):
    S = df.shape[0]
    ts = 1024

    def body(df_ref, gate_ref, up_ref, wg_ref, wu_ref, wd_ref, a_ref, dgate_ref, dup_ref, dh_ref):
        j = pl.program_id(1)
        da = _dot_nt(df_ref[...], wd_ref[...])
        g = gate_ref[...].astype(f32)
        u = up_ref[...].astype(f32)
        sig = _sigmoid(g)
        silu = g * sig
        a_ref[...] = (silu * u).astype(bf16)
        dup = (da * silu).astype(bf16)
        dgate = (da * u * (sig * (1.0 + g * (1.0 - sig)))).astype(bf16)
        dup_ref[...] = dup
        dgate_ref[...] = dgate
        part = _dot_nt(dgate, wg_ref[...]) + _dot_nt(dup, wu_ref[...])

        @pl.when(j == 0)
        def _():
            dh_ref[...] = part

        @pl.when(j > 0)
        def _():
            dh_ref[...] += part

    act = pl.BlockSpec((ts, FF_TILE), lambda i, j: (i, j))
    row = pl.BlockSpec((ts, D_MODEL), lambda i, j: (i, 0))
    return pl.pallas_call(
        body, name="ffn_bwd", grid=(S // ts, D_FF // FF_TILE),
        in_specs=[row, act, act,
                  pl.BlockSpec((D_MODEL, FF_TILE), lambda i, j: (0, j)),
                  pl.BlockSpec((D_MODEL, FF_TILE), lambda i, j: (0, j)),
                  pl.BlockSpec((FF_TILE, D_MODEL), lambda i, j: (j, 0))],
        out_specs=[act, act, act, row],
        out_shape=[jax.ShapeDtypeStruct((S, D_FF), bf16)] * 3 + [jax.ShapeDtypeStruct((S, D_MODEL), f32)],
        compiler_params=_params("parallel", "arbitrary"),
    )(df, gate, up, w_gate, w_up, w_down)


def _norm_bwd(dh2, dy, x2, mix, g3, g2):
    S = dh2.shape[0]
    ts = 512

    def body(dh_ref, dy_ref, x2_ref, mix_ref, g3_ref, g2_ref, dx2_ref, dmix_ref, dg3_ref, dg2_ref):
        @pl.when(pl.program_id(0) == 0)
        def _():
            dg3_ref[...] = jnp.zeros_like(dg3_ref)
            dg2_ref[...] = jnp.zeros_like(dg2_ref)

        dh = dh_ref[...]
        x2 = x2_ref[...]
        r3 = _rms(x2)
        xhat = x2 * r3
        dg3_ref[...] += jnp.sum(dh * xhat, axis=0, keepdims=True)
        dhg = dh * g3_ref[...]
        dx2 = dy_ref[...] + r3 * (dhg - xhat * jnp.mean(dhg * xhat, axis=-1, keepdims=True))
        dx2_ref[...] = dx2
        mix = mix_ref[...]
        r2 = _rms(mix)
        mhat = mix * r2
        dg2_ref[...] += jnp.sum(dx2 * mhat, axis=0, keepdims=True)
        dmg = dx2 * g2_ref[...]
        dmix_ref[...] = (r2 * (dmg - mhat * jnp.mean(dmg * mhat, axis=-1, keepdims=True))).astype(bf16)

    row = pl.BlockSpec((ts, D_MODEL), lambda i: (i, 0))
    gain = pl.BlockSpec((1, D_MODEL), lambda i: (0, 0))
    return pl.pallas_call(
        body, name="norm_bwd", grid=(S // ts,), in_specs=[row, row, row, row, gain, gain],
        out_specs=[row, row, gain, gain],
        out_shape=[jax.ShapeDtypeStruct((S, D_MODEL), f32), jax.ShapeDtypeStruct((S, D_MODEL), bf16),
                   jax.ShapeDtypeStruct((1, D_MODEL), f32), jax.ShapeDtypeStruct((1, D_MODEL), f32)],
        compiler_params=_params("arbitrary"),
    )(dh2, dy, x2, mix, g3, g2)


def _out_proj_bwd(dmix, w_out, attn_out, head_ones):
    S = dmix.shape[0]
    ts = 512

    def body(dm_ref, w_ref, o_ref, ones_ref, dp_ref, do_ref, dl_ref):
        dcat = _dot_nt(dm_ref[...], w_ref[...])
        dp_ref[...] = dcat[:, :POOL_WIDTH]
        do = dcat[:, POOL_WIDTH:]
        do_ref[...] = do.astype(bf16)
        prod = do * o_ref[...].astype(f32)
        hi = prod.astype(bf16)
        lo = (prod - hi.astype(f32)).astype(bf16)
        dl_ref[...] = _dot(hi, ones_ref[...]) + _dot(lo, ones_ref[...])

    row = lambda w: pl.BlockSpec((ts, w), lambda i: (i, 0))
    return pl.pallas_call(
        body, name="out_proj_bwd", grid=(S // ts,),
        in_specs=[row(D_MODEL), pl.BlockSpec((D_MODEL, D_MODEL), lambda i: (0, 0)), row(ATTN_WIDTH),
                  pl.BlockSpec((ATTN_WIDTH, ATTN_WIDTH), lambda i: (0, 0))],
        out_specs=[row(POOL_WIDTH), row(ATTN_WIDTH), row(ATTN_WIDTH)],
        out_shape=[jax.ShapeDtypeStruct((S, POOL_WIDTH), f32), jax.ShapeDtypeStruct((S, ATTN_WIDTH), bf16),
                   jax.ShapeDtypeStruct((S, ATTN_WIDTH), f32)],
        compiler_params=_params("parallel"),
    )(dmix, w_out, attn_out, head_ones)


def _in_proj_bwd(du, dqs, dks, dvs, cos_t, sin_t, w_in, x, dx2, g1):
    S = x.shape[0]
    ts = 256

    def body(du_ref, dq1, dq2, dq3, dk1, dk2, dk3, dv1, dv2, dv3, cos_ref, sin_ref, w_ref, x_ref, dx2_ref, g_ref,
             gx_ref, dproj_ref, dg_ref):
        @pl.when(pl.program_id(0) == 0)
        def _():
            dg_ref[...] = jnp.zeros_like(dg_ref)

        dproj_ref[:, :POOL_WIDTH] = du_ref[...]
        cos = cos_ref[...]
        sin = sin_ref[...]
        first = _first_half_mask(ts)
        for j in range(ATTN_WIDTH // 128):
            cols = slice(j * 128, (j + 1) * 128)
            for base, (r1, r2, r3) in ((POOL_WIDTH, (dq1, dq2, dq3)), (POOL_WIDTH + ATTN_WIDTH, (dk1, dk2, dk3))):
                g = r1[:, cols].astype(f32) + r2[:, cols].astype(f32) + r3[:, cols].astype(f32)
                pre = g * cos + _rope_partner(g * sin, first)
                dproj_ref[:, base + j * 128: base + (j + 1) * 128] = pre.astype(bf16)
        dv = dv1[...].astype(f32) + dv2[...].astype(f32) + dv3[...].astype(f32)
        dproj_ref[:, POOL_WIDTH + 2 * ATTN_WIDTH:] = dv.astype(bf16)

        dh = _dot_nt(dproj_ref[...], w_ref[...])
        xv = x_ref[...]
        r = _rms(xv)
        xhat = xv * r
        dg_ref[...] += jnp.sum(dh * xhat, axis=0, keepdims=True)
        dhg = dh * g_ref[...]
        gx_ref[...] = dx2_ref[...] + r * (dhg - xhat * jnp.mean(dhg * xhat, axis=-1, keepdims=True))

    row = lambda w: pl.BlockSpec((ts, w), lambda i: (i, 0))
    gain = pl.BlockSpec((1, D_MODEL), lambda i: (0, 0))
    return pl.pallas_call(
        body, name="in_proj_bwd", grid=(S // ts,),
        in_specs=[row(POOL_WIDTH)] + [row(ATTN_WIDTH)] * 9 + [row(128), row(128),
                  pl.BlockSpec((D_MODEL, IN_WIDTH), lambda i: (0, 0)), row(D_MODEL), row(D_MODEL), gain],
        out_specs=[row(D_MODEL), row(IN_WIDTH), gain],
        out_shape=[jax.ShapeDtypeStruct((S, D_MODEL), f32), jax.ShapeDtypeStruct((S, IN_WIDTH), bf16),
                   jax.ShapeDtypeStruct((1, D_MODEL), f32)],
        compiler_params=_params("arbitrary"),
    )(du, *dqs, *dks, *dvs, cos_t, sin_t, w_in, x, dx2, g1)


def _matmul_tn(a, b, tn, name):
    K, M = a.shape
    N = b.shape[1]
    tk = 512

    def body(a_ref, b_ref, o_ref):
        part = _dot_tn(a_ref[...], b_ref[...])

        @pl.when(pl.program_id(1) == 0)
        def _():
            o_ref[...] = part

        @pl.when(pl.program_id(1) > 0)
        def _():
            o_ref[...] += part

    return pl.pallas_call(
        body, name=name, grid=(N // tn, K // tk),
        in_specs=[pl.BlockSpec((tk, M), lambda n, k: (k, 0)), pl.BlockSpec((tk, tn), lambda n, k: (k, n))],
        out_specs=pl.BlockSpec((M, tn), lambda n, k: (0, n)),
        out_shape=jax.ShapeDtypeStruct((M, N), f32),
        compiler_params=_params("parallel", "arbitrary"),
    )(a, b)


def _rope_tables(S):
    half = HEAD_DIM // 2
    freqs = ROPE_THETA ** (-jnp.arange(half, dtype=f32) * (2.0 / HEAD_DIM))
    ang = jnp.arange(S).astype(f32)[:, None] * freqs[None, :]
    cos = jnp.tile(jnp.cos(ang), (1, 4))
    sin = jnp.sin(ang)
    sin = jnp.tile(jnp.concatenate([-sin, sin], axis=1), (1, 2))
    return cos, sin


def _block_diag(w_pool):
    w = jnp.zeros((POOL_WIDTH, POOL_WIDTH), w_pool.dtype)
    for g in range(POOL_WIDTH // POOL_GROUP):
        w = lax.dynamic_update_slice(w, w_pool[g], (g * POOL_GROUP, g * POOL_GROUP))
    return w


def _head_ones():
    head = np.arange(ATTN_WIDTH) // HEAD_DIM
    return jnp.asarray(head[:, None] == head[None, :], dtype=bf16)


def _local_grads(x, target, g1, w_pool, pool_scale, g2, g3, g4, w_in, w_out, w_gate, w_up, w_down):
    S = x.shape[0]
    cos_t, sin_t = _rope_tables(S)
    w_bd = _block_diag(w_pool).astype(bf16)

    h1, u, q, k, v = _in_proj(x, g1, w_in, cos_t, sin_t)
    pool_out = _pool_fwd(u, w_bd, pool_scale)
    branches = [_attn_fwd(q, k, v, d) for d in DILATIONS]
    attn_out, lse = _attn_merge([b[0] for b in branches], [b[1] for b in branches])
    mix, x2, h2 = _out_proj(pool_out, attn_out, w_out, x, g2, g3)
    gate, up, f = _ffn_fwd(h2, w_gate, w_up, w_down)
    dy, df, dg4, loss = _loss_head(f, x2, target, g4)

    a, dgate, dup, dh2 = _ffn_bwd(df, gate, up, w_gate, w_up, w_down)
    d_w_down = _matmul_tn(a, df, D_MODEL, "grad_w_down")
    d_w_gate = _matmul_tn(h2, dgate, FF_HALF, "grad_w_gate")
    d_w_up = _matmul_tn(h2, dup, FF_HALF, "grad_w_up")
    dx2, dmix, dg3, dg2 = _norm_bwd(dh2, dy, x2, mix, g3, g2)
    d_w_out = jnp.concatenate([_matmul_tn(pool_out, dmix, D_MODEL, "grad_w_out_pool"),
                               _matmul_tn(attn_out, dmix, D_MODEL, "grad_w_out_attn")], axis=0)
    dpool, do, delta = _out_proj_bwd(dmix, w_out, attn_out, _head_ones())
    du, d_w_bd, d_scale = _pool_bwd(u, dpool, w_bd, pool_scale)
    parts = [_attn_bwd(q, k, v, do, lse, delta, d) for d in DILATIONS]
    grad_x, dproj, dg1 = _in_proj_bwd(du, [p[0] for p in parts], [p[1] for p in parts], [p[2] for p in parts],
                                      cos_t, sin_t, w_in, x, dx2, g1)
    d_w_in = _matmul_tn(h1, dproj, IN_WIDTH // 2, "grad_w_in")
    d_w_pool = jnp.stack([d_w_bd[g * POOL_GROUP:(g + 1) * POOL_GROUP, g * POOL_GROUP:(g + 1) * POOL_GROUP]
                          for g in range(POOL_WIDTH // POOL_GROUP)])
    large = dict(w_in=d_w_in, w_out=d_w_out, w_gate=d_w_gate, w_up=d_w_up, w_down=d_w_down)
    small = dict(ln_pre_mix=dg1, ln_post_mix=dg2, ln_pre_ffn=dg3, ln_post_ffn=dg4, pool_scale=d_scale, w_pool=d_w_pool)
    return loss, grad_x, large, small


def _place():
    x, y, c = lax.axis_index("x"), lax.axis_index("y"), lax.axis_index("c")
    chips = [(1 - x, y), (x, 1 - y), (1 - x, 1 - y)]
    return x, y, c, chips


ANY = pl.BlockSpec(memory_space=pl.ANY)


def _row_chunks(rows, n, unit):
    units = rows // unit
    out, start = [], 0
    for i in range(n):
        size = (units // n + (1 if i < units % n else 0)) * unit
        out.append((start, size))
        start += size
    return out


GATHER_CHUNKS = _row_chunks(HALF_ROWS, 4, 32)
SWAP_CHUNKS = _row_chunks(HALF_ROWS, 8, 32)
JOIN_CHUNKS = _row_chunks(HALF_ROWS, 16, 32)


def _gather_weights(pack):
    n_ch = len(GATHER_CHUNKS)

    def body(w_ref, out_ref, send1, recv1, send2, recv2, local_sem):
        x, y, c, chips = _place()
        me = 2 * x + y
        sibling = (x, y, 1 - c)
        own = pltpu.make_async_copy(w_ref, out_ref.at[me], local_sem)
        own.start()

        def rows(core, ch):
            start, size = GATHER_CHUNKS[ch]
            return pl.ds(core * HALF_ROWS + start, size)

        def direct(j, ch, chip_xy, src_chip):
            cx, cy = chip_xy
            return pltpu.make_async_remote_copy(
                src_ref=w_ref.at[rows(c, ch)], dst_ref=out_ref.at[src_chip, rows(c, ch)],
                send_sem=send1.at[j * n_ch + ch], recv_sem=recv1.at[j * n_ch + ch],
                device_id=(cx, cy, c), device_id_type=MESH)

        def passed(j, ch, chip, core):
            return pltpu.make_async_remote_copy(
                src_ref=out_ref.at[chip, rows(core, ch)], dst_ref=out_ref.at[chip, rows(core, ch)],
                send_sem=send2.at[j * n_ch + ch], recv_sem=recv2.at[j * n_ch + ch],
                device_id=sibling, device_id_type=MESH)

        sends = [direct(j, ch, chip, me) for ch in range(n_ch) for j, chip in enumerate(chips)]
        for cp in sends:
            cp.start()
        forwards = []
        for ch in range(n_ch):
            for j, (cx, cy) in enumerate(chips):
                direct(j, ch, (cx, cy), 2 * cx + cy).wait_recv()
                fw = passed(j, ch, 2 * cx + cy, c)
                fw.start()
                forwards.append(fw)
        for ch in range(n_ch):
            for j, (cx, cy) in enumerate(chips):
                passed(j, ch, 2 * cx + cy, 1 - c).wait_recv()
        for cp in sends + forwards:
            cp.wait_send()
        own.wait()

    n_sem = 3 * n_ch
    return pl.pallas_call(
        body, name="gather_weights", in_specs=[ANY], out_specs=ANY,
        out_shape=jax.ShapeDtypeStruct((N_CHIPS, PACK_ROWS, D_MODEL), pack.dtype),
        scratch_shapes=[pltpu.SemaphoreType.DMA((n_sem,))] * 4 + [pltpu.SemaphoreType.DMA],
    )(pack)


def _swap_halves(g):
    n_ch = len(SWAP_CHUNKS)

    def body(g_ref, theirs_ref, send, recv):
        x, y, c, _ = _place()

        def piece(s, ch, core):
            start, size = SWAP_CHUNKS[ch]
            return pltpu.make_async_remote_copy(
                src_ref=g_ref.at[s, pl.ds(core * HALF_ROWS + start, size)], dst_ref=theirs_ref.at[s, pl.ds(start, size)],
                send_sem=send.at[s * n_ch + ch], recv_sem=recv.at[s * n_ch + ch],
                device_id=(x, y, 1 - c), device_id_type=MESH)

        copies = [piece(s, ch, 1 - c) for s in range(N_CHIPS) for ch in range(n_ch)]
        for cp in copies:
            cp.start()
        for cp in copies:
            cp.wait()

    return pl.pallas_call(
        body, name="swap_halves", in_specs=[ANY], out_specs=ANY,
        out_shape=jax.ShapeDtypeStruct((N_CHIPS, HALF_ROWS, D_MODEL), g.dtype),
        scratch_shapes=[pltpu.SemaphoreType.DMA((N_CHIPS * n_ch,))] * 2,
    )(g)


ADD_ROWS = 376


def _add_cores(g, theirs):
    n_t = HALF_ROWS // ADD_ROWS

    def body(c_ref, g_ref, t_ref, o_ref):
        o_ref[...] = g_ref[...] + t_ref[...]

    blk = pl.BlockSpec((1, ADD_ROWS, D_MODEL), lambda s, t, c_ref: (s, t, 0))
    return pl.pallas_call(
        body, name="add_cores",
        grid_spec=pltpu.PrefetchScalarGridSpec(
            num_scalar_prefetch=1, grid=(N_CHIPS, n_t),
            in_specs=[pl.BlockSpec((1, ADD_ROWS, D_MODEL), lambda s, t, c_ref: (s, c_ref[0] * n_t + t, 0)), blk],
            out_specs=blk),
        out_shape=jax.ShapeDtypeStruct(theirs.shape, theirs.dtype),
        compiler_params=_params("parallel", "parallel"),
    )(lax.axis_index("c").astype(jnp.int32).reshape(1), g, theirs)


def _scatter_to_chips(h):
    def body(h_ref, out_ref, send, recv, local_sem):
        x, y, c, chips = _place()
        me = 2 * x + y
        own = pltpu.make_async_copy(h_ref.at[me], out_ref.at[me], local_sem)
        own.start()
        sends = []
        for j, (cx, cy) in enumerate(chips):
            cp = pltpu.make_async_remote_copy(
                src_ref=h_ref.at[2 * cx + cy], dst_ref=out_ref.at[me], send_sem=send.at[j], recv_sem=recv.at[j],
                device_id=(cx, cy, c), device_id_type=MESH)
            cp.start()
            sends.append(cp)
        for j, (cx, cy) in enumerate(chips):
            pltpu.make_async_remote_copy(
                src_ref=h_ref.at[me], dst_ref=out_ref.at[2 * cx + cy], send_sem=send.at[j], recv_sem=recv.at[j],
                device_id=(cx, cy, c), device_id_type=MESH).wait_recv()
        for cp in sends:
            cp.wait_send()
        own.wait()

    return pl.pallas_call(
        body, name="scatter_to_chips", in_specs=[ANY], out_specs=ANY,
        out_shape=jax.ShapeDtypeStruct(h.shape, h.dtype),
        scratch_shapes=[pltpu.SemaphoreType.DMA((3,))] * 2 + [pltpu.SemaphoreType.DMA],
    )(h)


def _join_halves(r):
    n_ch = len(JOIN_CHUNKS)

    def body(r_ref, out_ref, send, recv, local_sem):
        x, y, c, _ = _place()
        own = pltpu.make_async_copy(r_ref, out_ref.at[c], local_sem)
        own.start()

        def piece(ch, core):
            start, size = JOIN_CHUNKS[ch]
            return pltpu.make_async_remote_copy(
                src_ref=r_ref.at[pl.ds(start, size)], dst_ref=out_ref.at[core, pl.ds(start, size)],
                send_sem=send.at[ch], recv_sem=recv.at[ch], device_id=(x, y, 1 - c), device_id_type=MESH)

        copies = [piece(ch, c) for ch in range(n_ch)]
        for cp in copies:
            cp.start()
        for ch in range(n_ch):
            piece(ch, 1 - c).wait_recv()
        for cp in copies:
            cp.wait_send()
        own.wait()

    return pl.pallas_call(
        body, name="join_halves", in_specs=[ANY], out_specs=ANY,
        out_shape=jax.ShapeDtypeStruct((2,) + r.shape, r.dtype),
        scratch_shapes=[pltpu.SemaphoreType.DMA((n_ch,))] * 2 + [pltpu.SemaphoreType.DMA],
    )(r)


def _add_slabs(terms, k, name):
    rows = terms[0][0].shape[1]
    tr = ADD_ROWS
    n = len(terms)

    def body(*refs):
        acc = refs[0][...]
        for r in refs[1:n]:
            acc = acc + r[...]
        refs[n][...] = acc

    slab = lambda first: pl.BlockSpec((1, tr, D_MODEL), lambda i, t: (first + i, t, 0))
    return pl.pallas_call(
        body, name=name, grid=(k, rows // tr), in_specs=[slab(first) for _, first in terms],
        out_specs=pl.BlockSpec((1, tr, D_MODEL), lambda i, t: (i, t, 0)),
        out_shape=jax.ShapeDtypeStruct((k, rows, D_MODEL), terms[0][0].dtype),
        compiler_params=_params("parallel", "parallel"),
    )(*[a for a, _ in terms])


def _sum_small(block):
    def body(b_ref, out_ref, gathered, send, recv):
        x, y, c, _ = _place()
        me = 4 * x + 2 * y + c
        gathered[me] = b_ref[...]
        sends = []
        for kk in range(1, N_DEV):
            flip = lambda v, bit: 1 - v if bit else v
            peer = (flip(x, kk & 4), flip(y, kk & 2), flip(c, kk & 1))
            cp = pltpu.make_async_remote_copy(
                src_ref=b_ref, dst_ref=gathered.at[me], send_sem=send.at[kk - 1], recv_sem=recv.at[kk - 1],
                device_id=peer, device_id_type=MESH)
            cp.start()
            sends.append(cp)
        for kk in range(1, N_DEV):
            peer_index = jnp.bitwise_xor(me, kk)
            pltpu.make_async_remote_copy(
                src_ref=b_ref, dst_ref=gathered.at[peer_index], send_sem=send.at[kk - 1], recv_sem=recv.at[kk - 1],
                device_id=(x, y, c), device_id_type=MESH).wait_recv()
        for cp in sends:
            cp.wait_send()
        acc = gathered[0]
        for dev in range(1, N_DEV):
            acc = acc + gathered[dev]
        out_ref[...] = acc

    vmem = pl.BlockSpec(memory_space=pltpu.VMEM)
    return pl.pallas_call(
        body, name="sum_small", in_specs=[vmem], out_specs=vmem,
        out_shape=jax.ShapeDtypeStruct(block.shape, block.dtype),
        scratch_shapes=[pltpu.VMEM((N_DEV,) + block.shape, block.dtype),
                        pltpu.SemaphoreType.DMA((N_DEV - 1,)), pltpu.SemaphoreType.DMA((N_DEV - 1,))],
    )(block)


def _adamw(w, g, m, v, name):
    rows, cols = w.shape
    tr = rows
    for cand in (512, 256, 128, 64, 32, 16, 8):
        if rows % cand == 0:
            tr = cand
            break
    c1 = 1.0 - ADAM_B1 ** ADAM_STEP
    c2 = 1.0 - ADAM_B2 ** ADAM_STEP

    def body(w_ref, g_ref, m_ref, v_ref, d_ref, nm_ref, nv_ref):
        gv = g_ref[...]
        nm = ADAM_B1 * m_ref[...] + (1.0 - ADAM_B1) * gv
        nv = ADAM_B2 * v_ref[...] + (1.0 - ADAM_B2) * (gv * gv)
        nm_ref[...] = nm
        nv_ref[...] = nv
        d_ref[...] = -ADAM_LR * ((nm / c1) / (jnp.sqrt(nv / c2) + ADAM_EPS) + ADAM_WD * w_ref[...])

    blk = pl.BlockSpec((tr, cols), lambda i: (i, 0))
    shape = jax.ShapeDtypeStruct((rows, cols), f32)
    return pl.pallas_call(
        body, name=name, grid=(rows // tr,), in_specs=[blk] * 4, out_specs=[blk] * 3, out_shape=[shape] * 3,
        compiler_params=_params("parallel"),
    )(w, g, m, v)


LARGE = ("w_in", "w_out", "w_gate", "w_up", "w_down")
SMALL = ("ln_pre_mix", "ln_post_mix", "ln_pre_ffn", "ln_post_ffn", "pool_scale", "w_pool")
COLUMN_SHARDED = {"w_in": IN_WIDTH // N_CHIPS, "w_gate": D_FF // N_CHIPS, "w_up": D_FF // N_CHIPS}


def _pack_shard(shards):
    return jnp.concatenate([shards[n].reshape(-1, D_MODEL) for n in LARGE], axis=0)


def _unpack_shard(pack, shapes):
    out, row = {}, 0
    for n, rows in zip(LARGE, PACK_SPLITS):
        out[n] = pack[row:row + rows].reshape(shapes[n])
        row += rows
    return out


def _whole_from_shards(packs):
    out, row = {}, 0
    for n, rows in zip(LARGE, PACK_SPLITS):
        part = packs[:, row:row + rows]
        if n in COLUMN_SHARDED:
            width = COLUMN_SHARDED[n]
            part = part.reshape(N_CHIPS, D_MODEL, width).transpose(1, 0, 2).reshape(D_MODEL, N_CHIPS * width)
        else:
            part = part.reshape(N_CHIPS * rows, D_MODEL)
        out[n] = part
        row += rows
    return out


def _shards_from_whole(grads):
    parts = []
    for n, rows in zip(LARGE, PACK_SPLITS):
        g = grads[n]
        if n in COLUMN_SHARDED:
            width = COLUMN_SHARDED[n]
            g = g.reshape(D_MODEL, N_CHIPS, width).transpose(1, 0, 2)
        parts.append(g.reshape(N_CHIPS, rows, D_MODEL))
    return jnp.concatenate(parts, axis=1)


def _pack_small(vals):
    rows = [vals[n].reshape(1, D_MODEL) for n in SMALL[:4]]
    rows.append(jnp.pad(vals["pool_scale"].reshape(1, POOL_WIDTH), ((0, 0), (0, D_MODEL - POOL_WIDTH))))
    rows.append(jnp.pad(vals["loss"].reshape(1, 1), ((0, 0), (0, D_MODEL - 1))))
    rows.append(jnp.zeros((2, D_MODEL), f32))
    rows.append(vals["w_pool"].reshape(16, D_MODEL))
    return jnp.concatenate(rows, axis=0)


def _unpack_small(block):
    out = {n: block[i:i + 1] for i, n in enumerate(SMALL[:4])}
    out["pool_scale"] = block[4:5, :POOL_WIDTH]
    out["loss"] = block[5, 0]
    out["w_pool"] = block[8:24].reshape(1, 4, POOL_GROUP, POOL_GROUP)
    return out


def kernel(x, ln_pre_mix, w_in, w_pool, pool_scale, w_out, ln_post_mix, ln_pre_ffn, w_gate, w_up, w_down, ln_post_ffn, loss_target, m_ln_pre_mix, m_w_in, m_w_pool, m_pool_scale, m_w_out, m_ln_post_mix, m_ln_pre_ffn, m_w_gate, m_w_up, m_w_down, m_ln_post_ffn, v_ln_pre_mix, v_w_in, v_w_pool, v_pool_scale, v_w_out, v_ln_post_mix, v_ln_pre_ffn, v_w_gate, v_w_up, v_w_down, v_ln_post_ffn):
    w = dict(ln_pre_mix=ln_pre_mix, w_in=w_in, w_pool=w_pool, pool_scale=pool_scale, w_out=w_out,
             ln_post_mix=ln_post_mix, ln_pre_ffn=ln_pre_ffn, w_gate=w_gate, w_up=w_up, w_down=w_down,
             ln_post_ffn=ln_post_ffn)
    m = dict(ln_pre_mix=m_ln_pre_mix, w_in=m_w_in, w_pool=m_w_pool, pool_scale=m_pool_scale, w_out=m_w_out,
             ln_post_mix=m_ln_post_mix, ln_pre_ffn=m_ln_pre_ffn, w_gate=m_w_gate, w_up=m_w_up, w_down=m_w_down,
             ln_post_ffn=m_ln_post_ffn)
    v = dict(ln_pre_mix=v_ln_pre_mix, w_in=v_w_in, w_pool=v_w_pool, pool_scale=v_pool_scale, w_out=v_w_out,
             ln_post_mix=v_ln_post_mix, ln_pre_ffn=v_ln_pre_ffn, w_gate=v_w_gate, w_up=v_w_up, w_down=v_w_down,
             ln_post_ffn=v_ln_post_ffn)

    packs = _gather_weights(_pack_shard({n: w[n][0].astype(bf16) for n in LARGE}))
    whole = _whole_from_shards(packs)

    loss, grad_x, large, small = _local_grads(
        x[0], loss_target[0], ln_pre_mix, w_pool[0], pool_scale, ln_post_mix, ln_pre_ffn, ln_post_ffn,
        whole["w_in"], whole["w_out"], whole["w_gate"], whole["w_up"], whole["w_down"])

    shard_major = _shards_from_whole(large)
    chip_sum = _add_cores(shard_major, _swap_halves(shard_major))
    pieces = _scatter_to_chips(chip_sum)
    reduced_half = _add_slabs([(pieces, j) for j in range(N_CHIPS)], 1, "add_chips")[0]
    reduced = _join_halves(reduced_half).reshape(PACK_ROWS, D_MODEL)
    shapes = {n: w[n].shape[1:] for n in LARGE}
    grads = _unpack_shard(reduced, shapes)

    total = _unpack_small(_sum_small(_pack_small(dict(small, loss=loss))))
    for n in SMALL:
        grads[n] = total[n]

    delta, new_m, new_v = {}, {}, {}
    for n in LARGE:
        delta[n], new_m[n], new_v[n] = _adamw(w[n][0], grads[n], m[n][0], v[n][0], "adamw_" + n)
    small_state = [_pack_small(dict({n: s[n] for n in SMALL}, loss=jnp.zeros((), f32))) for s in (w, m, v)]
    small_grad = _pack_small(dict({n: grads[n] for n in SMALL}, loss=jnp.zeros((), f32)))
    sd, sm, sv = _adamw(small_state[0], small_grad, small_state[1], small_state[2], "adamw_small")
    for out, block in ((delta, sd), (new_m, sm), (new_v, sv)):
        un = _unpack_small(block)
        for n in SMALL:
            out[n] = un[n]

    names = ("ln_pre_mix", "w_in", "w_pool", "pool_scale", "w_out", "ln_post_mix", "ln_pre_ffn", "w_gate", "w_up",
             "w_down", "ln_post_ffn")
    full = lambda d: [d[n].reshape(w[n].shape) for n in names]
    return (total["loss"], grad_x[None], *full(grads), *full(delta), *full(new_m), *full(new_v))
```

```python
import numpy as np
import jax
import jax.numpy as jnp
from jax import lax
from jax.experimental import pallas as pl
from jax.experimental.pallas import tpu as pltpu

D_MODEL = 1024
POOL_WIDTH = 256
POOL_GROUP = 64
ATTN_WIDTH = 768
HEAD_DIM = 64
IN_WIDTH = 2560
D_FF = 2816
BLOCK = 128
DILATIONS = (1, 4, 16)
ROPE_THETA = 10000.0
EPS = 1e-6
ATTN_SCALE = 0.125
NEG = -1e30

ADAM_LR = 0.001
ADAM_B1 = 0.9
ADAM_B2 = 0.999
ADAM_EPS = 1e-08
ADAM_WD = 0.01
ADAM_STEP = 10

N_CHIPS = 4
N_DEV = 8
PACK_SPLITS = (640, 256, 704, 704, 704)
PACK_ROWS = sum(PACK_SPLITS)
HALF_ROWS = PACK_ROWS // 2
SMALL_ROWS = 24

VMEM_LIMIT_V7X = 56 * 1024 * 1024
MESH = pl.DeviceIdType.MESH

f32 = jnp.float32
bf16 = jnp.bfloat16


def _params(*sem):
    return pltpu.CompilerParams(dimension_semantics=sem, vmem_limit_bytes=VMEM_LIMIT_V7X)


def _dot(a, b):
    return jnp.dot(a, b, preferred_element_type=f32)


def _dot_nt(a, b):
    return lax.dot_general(a, b, (((1,), (1,)), ((), ())), preferred_element_type=f32)


def _dot_tn(a, b):
    return lax.dot_general(a, b, (((0,), (0,)), ((), ())), preferred_element_type=f32)


def _rope_partner(a, first_half):
    return jnp.where(first_half, pltpu.roll(a, 96, 1), pltpu.roll(a, 32, 1))


def _first_half_mask(rows):
    lane = lax.broadcasted_iota(jnp.int32, (rows, 128), 1)
    return (lane % HEAD_DIM) < (HEAD_DIM // 2)


def _in_proj(x, g1, w_in, cos_t, sin_t):
    S = x.shape[0]
    ts = 512

    def body(x_ref, g_ref, w_ref, cos_ref, sin_ref, h_ref, u_ref, q_ref, k_ref, v_ref):
        xv = x_ref[...]
        r = lax.rsqrt(jnp.mean(xv * xv, axis=-1, keepdims=True) + EPS)
        h = ((xv * r) * g_ref[...]).astype(bf16)
        h_ref[...] = h
        proj = _dot(h, w_ref[...])
        u_ref[...] = proj[:, :POOL_WIDTH]
        cos = cos_ref[...]
        sin = sin_ref[...]
        first = _first_half_mask(ts)
        for j in range(ATTN_WIDTH // 128):
            for base, ref in ((POOL_WIDTH, q_ref), (POOL_WIDTH + ATTN_WIDTH, k_ref)):
                a = proj[:, base + j * 128: base + (j + 1) * 128]
                ref[:, j * 128:(j + 1) * 128] = (a * cos + _rope_partner(a, first) * sin).astype(bf16)
        v_ref[...] = proj[:, POOL_WIDTH + 2 * ATTN_WIDTH:].astype(bf16)

    row = lambda w: pl.BlockSpec((ts, w), lambda i: (i, 0))
    return pl.pallas_call(
        body, name="in_proj", grid=(S // ts,),
        in_specs=[row(D_MODEL), pl.BlockSpec((1, D_MODEL), lambda i: (0, 0)),
                  pl.BlockSpec((D_MODEL, IN_WIDTH), lambda i: (0, 0)), row(128), row(128)],
        out_specs=[row(D_MODEL), row(POOL_WIDTH), row(ATTN_WIDTH), row(ATTN_WIDTH), row(ATTN_WIDTH)],
        out_shape=[jax.ShapeDtypeStruct((S, D_MODEL), bf16), jax.ShapeDtypeStruct((S, POOL_WIDTH), f32),
                   jax.ShapeDtypeStruct((S, ATTN_WIDTH), bf16), jax.ShapeDtypeStruct((S, ATTN_WIDTH), bf16),
                   jax.ShapeDtypeStruct((S, ATTN_WIDTH), bf16)],
        compiler_params=_params("parallel"),
    )(x, g1, w_in, cos_t, sin_t)


POOL_HALO = 16


def _pool_lane_group(rows):
    return lax.broadcasted_iota(jnp.int32, (rows, POOL_WIDTH), 1) // POOL_GROUP


def _pool_select(group, s2, s4, s8, s16):
    return jnp.where(group == 0, s2, jnp.where(group == 1, s4, jnp.where(group == 2, s8, s16)))


def _pool_count(t0, rows):
    group = _pool_lane_group(rows)
    t = t0 + lax.broadcasted_iota(jnp.int32, (rows, POOL_WIDTH), 0)
    win = _pool_select(group, 2, 4, 8, 16)
    return jnp.minimum(t + 1, win).astype(f32)


def _pool_diff(u_halo, u_tile, t0):
    ts = u_tile.shape[0]
    ext = jnp.concatenate([u_halo, u_tile], axis=0)
    s2 = ext + pltpu.roll(ext, 1, 0)
    s4 = s2 + pltpu.roll(s2, 2, 0)
    s8 = s4 + pltpu.roll(s4, 4, 0)
    s16 = s8 + pltpu.roll(s8, 8, 0)
    group = _pool_lane_group(ts + POOL_HALO)
    wsum = _pool_select(group, s2, s4, s8, s16)[POOL_HALO:]
    return wsum / _pool_count(t0, ts) - u_tile


def _pool_specs(ts, n_tiles):
    tile = pl.BlockSpec((ts, POOL_WIDTH), lambda i: (i, 0))
    per = ts // POOL_HALO
    before = pl.BlockSpec((POOL_HALO, POOL_WIDTH), lambda i: (jnp.maximum(i * per - 1, 0), 0))
    after = pl.BlockSpec((POOL_HALO, POOL_WIDTH), lambda i: (jnp.minimum((i + 1) * per, n_tiles * per - 1), 0))
    return tile, before, after


def _pool_fwd(u, w_bd, scale):
    S = u.shape[0]
    ts = 512
    n_tiles = S // ts

    def body(u_ref, halo_ref, w_ref, sc_ref, y_ref):
        i = pl.program_id(0)
        halo = jnp.where(i > 0, halo_ref[...], 0.0)
        d = _pool_diff(halo, u_ref[...], i * ts)
        y_ref[...] = (_dot(d.astype(bf16), w_ref[...]) * sc_ref[...]).astype(bf16)

    tile, before, _ = _pool_specs(ts, n_tiles)
    return pl.pallas_call(
        body, name="pool_fwd", grid=(n_tiles,),
        in_specs=[tile, before, pl.BlockSpec((POOL_WIDTH, POOL_WIDTH), lambda i: (0, 0)),
                  pl.BlockSpec((1, POOL_WIDTH), lambda i: (0, 0))],
        out_specs=tile, out_shape=jax.ShapeDtypeStruct((S, POOL_WIDTH), bf16),
        compiler_params=_params("parallel"),
    )(u, u, w_bd, scale)


def _pool_bwd(u, dy, w_bd, scale):
    S = u.shape[0]
    ts = 512
    n_tiles = S // ts

    def body(u_ref, halo_ref, dy_ref, dy_next_ref, w_ref, sc_ref, du_ref, dw_ref, dsc_ref):
        i = pl.program_id(0)

        @pl.when(i == 0)
        def _():
            dw_ref[...] = jnp.zeros_like(dw_ref)
            dsc_ref[...] = jnp.zeros_like(dsc_ref)

        halo = jnp.where(i > 0, halo_ref[...], 0.0)
        d = _pool_diff(halo, u_ref[...], i * ts).astype(bf16)
        w = w_ref[...]
        sc = sc_ref[...]
        dy_tile = dy_ref[...]
        z = _dot(d, w)
        dsc_ref[...] += jnp.sum(dy_tile * z, axis=0, keepdims=True)
        dy_next = jnp.where(i < n_tiles - 1, dy_next_ref[...], 0.0)
        dz = (jnp.concatenate([dy_tile, dy_next], axis=0) * sc).astype(bf16)
        dw_ref[...] += _dot_tn(d, dz[:ts])
        dd = _dot_nt(dz, w)
        e = dd / _pool_count(i * ts, ts + POOL_HALO)
        n = ts + POOL_HALO
        f2 = e + pltpu.roll(e, n - 1, 0)
        f4 = f2 + pltpu.roll(f2, n - 2, 0)
        f8 = f4 + pltpu.roll(f4, n - 4, 0)
        f16 = f8 + pltpu.roll(f8, n - 8, 0)
        fsum = _pool_select(_pool_lane_group(n), f2, f4, f8, f16)
        du_ref[...] = (fsum[:ts] - dd[:ts]).astype(bf16)

    tile, before, after = _pool_specs(ts, n_tiles)
    return pl.pallas_call(
        body, name="pool_bwd", grid=(n_tiles,),
        in_specs=[tile, before, tile, after, pl.BlockSpec((POOL_WIDTH, POOL_WIDTH), lambda i: (0, 0)),
                  pl.BlockSpec((1, POOL_WIDTH), lambda i: (0, 0))],
        out_specs=[tile, pl.BlockSpec((POOL_WIDTH, POOL_WIDTH), lambda i: (0, 0)),
                   pl.BlockSpec((1, POOL_WIDTH), lambda i: (0, 0))],
        out_shape=[jax.ShapeDtypeStruct((S, POOL_WIDTH), bf16), jax.ShapeDtypeStruct((POOL_WIDTH, POOL_WIDTH), f32),
                   jax.ShapeDtypeStruct((1, POOL_WIDTH), f32)],
        compiler_params=_params("arbitrary"),
    )(u, u, dy, dy, w_bd, scale)


def _band_mask(n):
    qi = lax.broadcasted_iota(jnp.int32, (BLOCK, 2 * BLOCK), 0)
    kj = lax.broadcasted_iota(jnp.int32, (BLOCK, 2 * BLOCK), 1)
    return (kj >= qi) & (kj <= qi + BLOCK) & ((kj >= BLOCK) | (n > 0))


def _head0_mask(rows=BLOCK):
    return lax.broadcasted_iota(jnp.int32, (rows, 128), 1) < HEAD_DIM


def _per_head(stat, h0, h):
    other = pltpu.roll(stat, HEAD_DIM, 1)
    full = jnp.where(h0, stat, other) if h == 0 else jnp.where(h0, other, stat)
    return jnp.concatenate([full, full], axis=1)


def _attn_specs(L, d):
    nb = L // BLOCK
    cur = pl.BlockSpec((BLOCK, ATTN_WIDTH), lambda r, n: (n, r))
    prev = pl.BlockSpec((BLOCK, ATTN_WIDTH), lambda r, n: (jnp.maximum(n - 1, 0), r))
    return nb, cur, prev


def _attn_fwd(q, k, v, d):
    S = q.shape[0]
    L = S // d
    view = lambda a: a.reshape(L, d * ATTN_WIDTH)

    def body(q_ref, kc_ref, kp_ref, vc_ref, vp_ref, o_ref, lse_ref):
        valid = _band_mask(pl.program_id(1))
        h0 = _head0_mask()
        for j in range(ATTN_WIDTH // 128):
            cols = slice(j * 128, (j + 1) * 128)
            qv = q_ref[:, cols]
            kb = jnp.concatenate([kp_ref[:, cols], kc_ref[:, cols]], axis=0)
            vb = jnp.concatenate([vp_ref[:, cols], vc_ref[:, cols]], axis=0)
            outs, lses = [], []
            for h in range(2):
                keep = h0 if h == 0 else jnp.logical_not(h0)
                qh = jnp.where(keep, qv, jnp.zeros_like(qv))
                s = jnp.where(valid, _dot_nt(qh, kb) * ATTN_SCALE, NEG)
                m = jnp.max(s, axis=1, keepdims=True)
                e = jnp.exp(s - m)
                den = jnp.sum(e, axis=1, keepdims=True)
                outs.append(_dot((e / den).astype(bf16), vb))
                lses.append(jnp.broadcast_to(m + jnp.log(den), (BLOCK, 128)))
            o_ref[:, cols] = jnp.where(h0, outs[0], outs[1])
            lse_ref[:, cols] = jnp.where(h0, lses[0], lses[1])

    nb, cur, prev = _attn_specs(L, d)
    o, lse = pl.pallas_call(
        body, name=f"attn_fwd_d{d}", grid=(d, nb),
        in_specs=[cur, cur, prev, cur, prev], out_specs=[cur, cur],
        out_shape=[jax.ShapeDtypeStruct((L, d * ATTN_WIDTH), f32)] * 2,
        compiler_params=_params("parallel", "parallel"),
    )(view(q), view(k), view(k), view(v), view(v))
    return o.reshape(S, ATTN_WIDTH), lse.reshape(S, ATTN_WIDTH)


def _attn_merge(os, lses):
    S = os[0].shape[0]
    ts = 512

    def body(o1, o2, o3, l1, l2, l3, out_ref, lse_ref):
        a, b, c = l1[...], l2[...], l3[...]
        m = jnp.maximum(jnp.maximum(a, b), c)
        ea, eb, ec = jnp.exp(a - m), jnp.exp(b - m), jnp.exp(c - m)
        tot = ea + eb + ec
        out_ref[...] = ((ea / tot) * o1[...] + (eb / tot) * o2[...] + (ec / tot) * o3[...]).astype(bf16)
        lse_ref[...] = m + jnp.log(tot)

    row = pl.BlockSpec((ts, ATTN_WIDTH), lambda i: (i, 0))
    return pl.pallas_call(
        body, name="attn_merge", grid=(S // ts,), in_specs=[row] * 6, out_specs=[row, row],
        out_shape=[jax.ShapeDtypeStruct((S, ATTN_WIDTH), bf16), jax.ShapeDtypeStruct((S, ATTN_WIDTH), f32)],
        compiler_params=_params("parallel"),
    )(*os, *lses)


def _attn_bwd(q, k, v, do, lse, delta, d):
    S = q.shape[0]
    L = S // d
    nb = L // BLOCK
    view = lambda a: a.reshape(L, d * ATTN_WIDTH)

    def body(q_ref, kc_ref, kp_ref, vc_ref, vp_ref, do_ref, lse_ref, dl_ref, dq_ref, dk_ref, dv_ref, ck_ref, cv_ref):
        n = pl.program_id(1)

        @pl.when(n < nb)
        def _():
            valid = _band_mask(n)
            h0 = _head0_mask()
            h0_band = _head0_mask(2 * BLOCK)
            for j in range(ATTN_WIDTH // 128):
                cols = slice(j * 128, (j + 1) * 128)
                qv = q_ref[:, cols]
                dov = do_ref[:, cols]
                kb = jnp.concatenate([kp_ref[:, cols], kc_ref[:, cols]], axis=0)
                vb = jnp.concatenate([vp_ref[:, cols], vc_ref[:, cols]], axis=0)
                lse_v = lse_ref[:, cols]
                dl_v = dl_ref[:, cols]
                dqs, dks, dvs = [], [], []
                for h in range(2):
                    keep = h0 if h == 0 else jnp.logical_not(h0)
                    qh = jnp.where(keep, qv, jnp.zeros_like(qv))
                    doh = jnp.where(keep, dov, jnp.zeros_like(dov))
                    s = jnp.where(valid, _dot_nt(qh, kb) * ATTN_SCALE, NEG)
                    p = jnp.exp(s - _per_head(lse_v, h0, h))
                    dp = _dot_nt(doh, vb)
                    ds = (p * (dp - _per_head(dl_v, h0, h)) * ATTN_SCALE).astype(bf16)
                    dqs.append(_dot(ds, kb))
                    dks.append(_dot_tn(ds, qv))
                    dvs.append(_dot_tn(p.astype(bf16), dov))
                dq_ref[:, cols] = jnp.where(h0, dqs[0], dqs[1]).astype(bf16)
                dkb = jnp.where(h0_band, dks[0], dks[1])
                dvb = jnp.where(h0_band, dvs[0], dvs[1])

                @pl.when(n > 0)
                def _():
                    dk_ref[:, cols] = (ck_ref[:, cols] + dkb[:BLOCK]).astype(bf16)
                    dv_ref[:, cols] = (cv_ref[:, cols] + dvb[:BLOCK]).astype(bf16)

                ck_ref[:, cols] = dkb[BLOCK:]
                cv_ref[:, cols] = dvb[BLOCK:]

        @pl.when(n == nb)
        def _():
            dk_ref[...] = ck_ref[...].astype(bf16)
            dv_ref[...] = cv_ref[...].astype(bf16)

    last = nb - 1
    cur = pl.BlockSpec((BLOCK, ATTN_WIDTH), lambda r, n: (jnp.minimum(n, last), r))
    prev = pl.BlockSpec((BLOCK, ATTN_WIDTH), lambda r, n: (jnp.clip(n - 1, 0, last), r))
    out = jax.ShapeDtypeStruct((L, d * ATTN_WIDTH), bf16)
    dq, dk, dv = pl.pallas_call(
        body, name=f"attn_bwd_d{d}", grid=(d, nb + 1),
        in_specs=[cur, cur, prev, cur, prev, cur, cur, cur], out_specs=[cur, prev, prev],
        out_shape=[out, out, out],
        scratch_shapes=[pltpu.VMEM((BLOCK, ATTN_WIDTH), f32), pltpu.VMEM((BLOCK, ATTN_WIDTH), f32)],
        compiler_params=_params("parallel", "arbitrary"),
    )(view(q), view(k), view(k), view(v), view(v), view(do), view(lse), view(delta))
    return dq.reshape(S, ATTN_WIDTH), dk.reshape(S, ATTN_WIDTH), dv.reshape(S, ATTN_WIDTH)


def _rms(v):
    return lax.rsqrt(jnp.mean(v * v, axis=-1, keepdims=True) + EPS)


def _out_proj(pool_out, attn_out, w_out, x, g2, g3):
    S = x.shape[0]
    ts = 512

    def body(p_ref, a_ref, w_ref, x_ref, g2_ref, g3_ref, mix_ref, x2_ref, h2_ref):
        mix = _dot(p_ref[...], w_ref[:POOL_WIDTH, :]) + _dot(a_ref[...], w_ref[POOL_WIDTH:, :])
        mix_ref[...] = mix
        x2 = x_ref[...] + (mix * _rms(mix)) * g2_ref[...]
        x2_ref[...] = x2
        h2_ref[...] = ((x2 * _rms(x2)) * g3_ref[...]).astype(bf16)

    row = lambda w: pl.BlockSpec((ts, w), lambda i: (i, 0))
    gain = pl.BlockSpec((1, D_MODEL), lambda i: (0, 0))
    return pl.pallas_call(
        body, name="out_proj", grid=(S // ts,),
        in_specs=[row(POOL_WIDTH), row(ATTN_WIDTH), pl.BlockSpec((D_MODEL, D_MODEL), lambda i: (0, 0)),
                  row(D_MODEL), gain, gain],
        out_specs=[row(D_MODEL)] * 3,
        out_shape=[jax.ShapeDtypeStruct((S, D_MODEL), f32), jax.ShapeDtypeStruct((S, D_MODEL), f32),
                   jax.ShapeDtypeStruct((S, D_MODEL), bf16)],
        compiler_params=_params("parallel"),
    )(pool_out, attn_out, w_out, x, g2, g3)


FF_TILE = 256
FF_HALF = D_FF // 2


def _sigmoid(g):
    return 1.0 / (1.0 + jnp.exp(-g))


def _ffn_fwd(h2, w_gate, w_up, w_down):
    S = h2.shape[0]
    ts = 1024

    def body(h_ref, wg_ref, wu_ref, wd_ref, gate_ref, up_ref, f_ref):
        j = pl.program_id(1)
        h = h_ref[...]
        gate = _dot(h, wg_ref[...])
        up = _dot(h, wu_ref[...])
        gate_ref[...] = gate.astype(bf16)
        up_ref[...] = up.astype(bf16)
        part = _dot((gate * _sigmoid(gate) * up).astype(bf16), wd_ref[...])

        @pl.when(j == 0)
        def _():
            f_ref[...] = part

        @pl.when(j > 0)
        def _():
            f_ref[...] += part

    act = pl.BlockSpec((ts, FF_TILE), lambda i, j: (i, j))
    return pl.pallas_call(
        body, name="ffn_fwd", grid=(S // ts, D_FF // FF_TILE),
        in_specs=[pl.BlockSpec((ts, D_MODEL), lambda i, j: (i, 0)),
                  pl.BlockSpec((D_MODEL, FF_TILE), lambda i, j: (0, j)),
                  pl.BlockSpec((D_MODEL, FF_TILE), lambda i, j: (0, j)),
                  pl.BlockSpec((FF_TILE, D_MODEL), lambda i, j: (j, 0))],
        out_specs=[act, act, pl.BlockSpec((ts, D_MODEL), lambda i, j: (i, 0))],
        out_shape=[jax.ShapeDtypeStruct((S, D_FF), bf16), jax.ShapeDtypeStruct((S, D_FF), bf16),
                   jax.ShapeDtypeStruct((S, D_MODEL), f32)],
        compiler_params=_params("parallel", "arbitrary"),
    )(h2, w_gate, w_up, w_down)


def _loss_head(f, x2, target, g4):
    S = f.shape[0]
    ts = 512

    def body(f_ref, x2_ref, t_ref, g_ref, dy_ref, df_ref, dg_ref, loss_ref):
        @pl.when(pl.program_id(0) == 0)
        def _():
            dg_ref[...] = jnp.zeros_like(dg_ref)
            loss_ref[...] = jnp.zeros_like(loss_ref)

        fv = f_ref[...]
        g = g_ref[...]
        r = _rms(fv)
        fhat = fv * r
        err = (x2_ref[...] + fhat * g) - t_ref[...]
        loss_ref[...] += 0.5 * jnp.sum(jnp.mean(err * err, axis=-1, keepdims=True), axis=0, keepdims=True)
        dy = err * (1.0 / D_MODEL)
        dy_ref[...] = dy
        dg_ref[...] += jnp.sum(dy * fhat, axis=0, keepdims=True)
        dyg = dy * g
        df_ref[...] = (r * (dyg - fhat * jnp.mean(dyg * fhat, axis=-1, keepdims=True))).astype(bf16)

    row = pl.BlockSpec((ts, D_MODEL), lambda i: (i, 0))
    gain = pl.BlockSpec((1, D_MODEL), lambda i: (0, 0))
    return pl.pallas_call(
        body, name="loss_head", grid=(S // ts,), in_specs=[row, row, row, gain],
        out_specs=[row, row, gain, pl.BlockSpec((1, 1), lambda i: (0, 0))],
        out_shape=[jax.ShapeDtypeStruct((S, D_MODEL), f32), jax.ShapeDtypeStruct((S, D_MODEL), bf16),
                   jax.ShapeDtypeStruct((1, D_MODEL), f32), jax.ShapeDtypeStruct((1, 1), f32)],
        compiler_params=_params("arbitrary"),
    )(f, x2, target, g4)


def _ffn_bwd(df, gate, up, w_gate, w_up, w_down):
    S = df.shape[0]
    ts = 1024

    def body(df_ref, gate_ref, up_ref, wg_ref, wu_ref, wd_ref, a_ref, dgate_ref, dup_ref, dh_ref):
        j = pl.program_id(1)
        da = _dot_nt(df_ref[...], wd_ref[...])
        g = gate_ref[...].astype(f32)
        u = up_ref[...].astype(f32)
        sig = _sigmoid(g)
        silu = g * sig
        a_ref[...] = (silu * u).astype(bf16)
        dup = (da * silu).astype(bf16)
        dgate = (da * u * (sig * (1.0 + g * (1.0 - sig)))).astype(bf16)
        dup_ref[...] = dup
        dgate_ref[...] = dgate
        part = _dot_nt(dgate, wg_ref[...]) + _dot_nt(dup, wu_ref[...])

        @pl.when(j == 0)
        def _():
            dh_ref[...] = part

        @pl.when(j > 0)
        def _():
            dh_ref[...] += part

    act = pl.BlockSpec((ts, FF_TILE), lambda i, j: (i, j))
    row = pl.BlockSpec((ts, D_MODEL), lambda i, j: (i, 0))
    return pl.pallas_call(
        body, name="ffn_bwd", grid=(S // ts, D_FF // FF_TILE),
        in_specs=[row, act, act,
                  pl.BlockSpec((D_MODEL, FF_TILE), lambda i, j: (0, j)),
                  pl.BlockSpec((D_MODEL, FF_TILE), lambda i, j: (0, j)),
                  pl.BlockSpec((FF_TILE, D_MODEL), lambda i, j: (j, 0))],
        out_specs=[act, act, act, row],
        out_shape=[jax.ShapeDtypeStruct((S, D_FF), bf16)] * 3 + [jax.ShapeDtypeStruct((S, D_MODEL), f32)],
        compiler_params=_params("parallel", "arbitrary"),
    )(df, gate, up, w_gate, w_up, w_down)


def _norm_bwd(dh2, dy, x2, mix, g3, g2):
    S = dh2.shape[0]
    ts = 512

    def body(dh_ref, dy_ref, x2_ref, mix_ref, g3_ref, g2_ref, dx2_ref, dmix_ref, dg3_ref, dg2_ref):
        @pl.when(pl.program_id(0) == 0)
        def _():
            dg3_ref[...] = jnp.zeros_like(dg3_ref)
            dg2_ref[...] = jnp.zeros_like(dg2_ref)

        dh = dh_ref[...]
        x2 = x2_ref[...]
        r3 = _rms(x2)
        xhat = x2 * r3
        dg3_ref[...] += jnp.sum(dh * xhat, axis=0, keepdims=True)
        dhg = dh * g3_ref[...]
        dx2 = dy_ref[...] + r3 * (dhg - xhat * jnp.mean(dhg * xhat, axis=-1, keepdims=True))
        dx2_ref[...] = dx2
        mix = mix_ref[...]
        r2 = _rms(mix)
        mhat = mix * r2
        dg2_ref[...] += jnp.sum(dx2 * mhat, axis=0, keepdims=True)
        dmg = dx2 * g2_ref[...]
        dmix_ref[...] = (r2 * (dmg - mhat * jnp.mean(dmg * mhat, axis=-1, keepdims=True))).astype(bf16)

    row = pl.BlockSpec((ts, D_MODEL), lambda i: (i, 0))
    gain = pl.BlockSpec((1, D_MODEL), lambda i: (0, 0))
    return pl.pallas_call(
        body, name="norm_bwd", grid=(S // ts,), in_specs=[row, row, row, row, gain, gain],
        out_specs=[row, row, gain, gain],
        out_shape=[jax.ShapeDtypeStruct((S, D_MODEL), f32), jax.ShapeDtypeStruct((S, D_MODEL), bf16),
                   jax.ShapeDtypeStruct((1, D_MODEL), f32), jax.ShapeDtypeStruct((1, D_MODEL), f32)],
        compiler_params=_params("arbitrary"),
    )(dh2, dy, x2, mix, g3, g2)


def _out_proj_bwd(dmix, w_out, attn_out, head_ones):
    S = dmix.shape[0]
    ts = 512

    def body(dm_ref, w_ref, o_ref, ones_ref, dp_ref, do_ref, dl_ref):
        dcat = _dot_nt(dm_ref[...], w_ref[...])
        dp_ref[...] = dcat[:, :POOL_WIDTH]
        do = dcat[:, POOL_WIDTH:]
        do_ref[...] = do.astype(bf16)
        prod = do * o_ref[...].astype(f32)
        hi = prod.astype(bf16)
        lo = (prod - hi.astype(f32)).astype(bf16)
        dl_ref[...] = _dot(hi, ones_ref[...]) + _dot(lo, ones_ref[...])

    row = lambda w: pl.BlockSpec((ts, w), lambda i: (i, 0))
    return pl.pallas_call(
        body, name="out_proj_bwd", grid=(S // ts,),
        in_specs=[row(D_MODEL), pl.BlockSpec((D_MODEL, D_MODEL), lambda i: (0, 0)), row(ATTN_WIDTH),
                  pl.BlockSpec((ATTN_WIDTH, ATTN_WIDTH), lambda i: (0, 0))],
        out_specs=[row(POOL_WIDTH), row(ATTN_WIDTH), row(ATTN_WIDTH)],
        out_shape=[jax.ShapeDtypeStruct((S, POOL_WIDTH), f32), jax.ShapeDtypeStruct((S, ATTN_WIDTH), bf16),
                   jax.ShapeDtypeStruct((S, ATTN_WIDTH), f32)],
        compiler_params=_params("parallel"),
    )(dmix, w_out, attn_out, head_ones)


def _in_proj_bwd(du, dqs, dks, dvs, cos_t, sin_t, w_in, x, dx2, g1):
    S = x.shape[0]
    ts = 256

    def body(du_ref, dq1, dq2, dq3, dk1, dk2, dk3, dv1, dv2, dv3, cos_ref, sin_ref, w_ref, x_ref, dx2_ref, g_ref,
             gx_ref, dproj_ref, dg_ref):
        @pl.when(pl.program_id(0) == 0)
        def _():
            dg_ref[...] = jnp.zeros_like(dg_ref)

        dproj_ref[:, :POOL_WIDTH] = du_ref[...]
        cos = cos_ref[...]
        sin = sin_ref[...]
        first = _first_half_mask(ts)
        for j in range(ATTN_WIDTH // 128):
            cols = slice(j * 128, (j + 1) * 128)
            for base, (r1, r2, r3) in ((POOL_WIDTH, (dq1, dq2, dq3)), (POOL_WIDTH + ATTN_WIDTH, (dk1, dk2, dk3))):
                g = r1[:, cols].astype(f32) + r2[:, cols].astype(f32) + r3[:, cols].astype(f32)
                pre = g * cos + _rope_partner(g * sin, first)
                dproj_ref[:, base + j * 128: base + (j + 1) * 128] = pre.astype(bf16)
        dv = dv1[...].astype(f32) + dv2[...].astype(f32) + dv3[...].astype(f32)
        dproj_ref[:, POOL_WIDTH + 2 * ATTN_WIDTH:] = dv.astype(bf16)

        dh = _dot_nt(dproj_ref[...], w_ref[...])
        xv = x_ref[...]
        r = _rms(xv)
        xhat = xv * r
        dg_ref[...] += jnp.sum(dh * xhat, axis=0, keepdims=True)
        dhg = dh * g_ref[...]
        gx_ref[...] = dx2_ref[...] + r * (dhg - xhat * jnp.mean(dhg * xhat, axis=-1, keepdims=True))

    row = lambda w: pl.BlockSpec((ts, w), lambda i: (i, 0))
    gain = pl.BlockSpec((1, D_MODEL), lambda i: (0, 0))
    return pl.pallas_call(
        body, name="in_proj_bwd", grid=(S // ts,),
        in_specs=[row(POOL_WIDTH)] + [row(ATTN_WIDTH)] * 9 + [row(128), row(128),
                  pl.BlockSpec((D_MODEL, IN_WIDTH), lambda i: (0, 0)), row(D_MODEL), row(D_MODEL), gain],
        out_specs=[row(D_MODEL), row(IN_WIDTH), gain],
        out_shape=[jax.ShapeDtypeStruct((S, D_MODEL), f32), jax.ShapeDtypeStruct((S, IN_WIDTH), bf16),
                   jax.ShapeDtypeStruct((1, D_MODEL), f32)],
        compiler_params=_params("arbitrary"),
    )(du, *dqs, *dks, *dvs, cos_t, sin_t, w_in, x, dx2, g1)


def _matmul_tn(a, b, tn, name):
    K, M = a.shape
    N = b.shape[1]
    tk = 512

    def body(a_ref, b_ref, o_ref):
        part = _dot_tn(a_ref[...], b_ref[...])

        @pl.when(pl.program_id(1) == 0)
        def _():
            o_ref[...] = part

        @pl.when(pl.program_id(1) > 0)
        def _():
            o_ref[...] += part

    return pl.pallas_call(
        body, name=name, grid=(N // tn, K // tk),
        in_specs=[pl.BlockSpec((tk, M), lambda n, k: (k, 0)), pl.BlockSpec((tk, tn), lambda n, k: (k, n))],
        out_specs=pl.BlockSpec((M, tn), lambda n, k: (0, n)),
        out_shape=jax.ShapeDtypeStruct((M, N), f32),
        compiler_params=_params("parallel", "arbitrary"),
    )(a, b)


def _rope_tables(S):
    half = HEAD_DIM // 2
    freqs = ROPE_THETA ** (-jnp.arange(half, dtype=f32) * (2.0 / HEAD_DIM))
    ang = jnp.arange(S).astype(f32)[:, None] * freqs[None, :]
    cos = jnp.tile(jnp.cos(ang), (1, 4))
    sin = jnp.sin(ang)
    sin = jnp.tile(jnp.concatenate([-sin, sin], axis=1), (1, 2))
    return cos, sin


def _block_diag(w_pool):
    w = jnp.zeros((POOL_WIDTH, POOL_WIDTH), w_pool.dtype)
    for g in range(POOL_WIDTH // POOL_GROUP):
        w = lax.dynamic_update_slice(w, w_pool[g], (g * POOL_GROUP, g * POOL_GROUP))
    return w


def _head_ones():
    head = np.arange(ATTN_WIDTH) // HEAD_DIM
    return jnp.asarray(head[:, None] == head[None, :], dtype=bf16)


def _local_grads(x, target, g1, w_pool, pool_scale, g2, g3, g4, w_in, w_out, w_gate, w_up, w_down):
    S = x.shape[0]
    cos_t, sin_t = _rope_tables(S)
    w_bd = _block_diag(w_pool).astype(bf16)

    h1, u, q, k, v = _in_proj(x, g1, w_in, cos_t, sin_t)
    pool_out = _pool_fwd(u, w_bd, pool_scale)
    branches = [_attn_fwd(q, k, v, d) for d in DILATIONS]
    attn_out, lse = _attn_merge([b[0] for b in branches], [b[1] for b in branches])
    mix, x2, h2 = _out_proj(pool_out, attn_out, w_out, x, g2, g3)
    gate, up, f = _ffn_fwd(h2, w_gate, w_up, w_down)
    dy, df, dg4, loss = _loss_head(f, x2, target, g4)

    a, dgate, dup, dh2 = _ffn_bwd(df, gate, up, w_gate, w_up, w_down)
    d_w_down = _matmul_tn(a, df, D_MODEL, "grad_w_down")
    d_w_gate = _matmul_tn(h2, dgate, FF_HALF, "grad_w_gate")
    d_w_up = _matmul_tn(h2, dup, FF_HALF, "grad_w_up")
    dx2, dmix, dg3, dg2 = _norm_bwd(dh2, dy, x2, mix, g3, g2)
    d_w_out = jnp.concatenate([_matmul_tn(pool_out, dmix, D_MODEL, "grad_w_out_pool"),
                               _matmul_tn(attn_out, dmix, D_MODEL, "grad_w_out_attn")], axis=0)
    dpool, do, delta = _out_proj_bwd(dmix, w_out, attn_out, _head_ones())
    du, d_w_bd, d_scale = _pool_bwd(u, dpool, w_bd, pool_scale)
    parts = [_attn_bwd(q, k, v, do, lse, delta, d) for d in DILATIONS]
    grad_x, dproj, dg1 = _in_proj_bwd(du, [p[0] for p in parts], [p[1] for p in parts], [p[2] for p in parts],
                                      cos_t, sin_t, w_in, x, dx2, g1)
    d_w_in = _matmul_tn(h1, dproj, IN_WIDTH // 2, "grad_w_in")
    d_w_pool = jnp.stack([d_w_bd[g * POOL_GROUP:(g + 1) * POOL_GROUP, g * POOL_GROUP:(g + 1) * POOL_GROUP]
                          for g in range(POOL_WIDTH // POOL_GROUP)])
    large = dict(w_in=d_w_in, w_out=d_w_out, w_gate=d_w_gate, w_up=d_w_up, w_down=d_w_down)
    small = dict(ln_pre_mix=dg1, ln_post_mix=dg2, ln_pre_ffn=dg3, ln_post_ffn=dg4, pool_scale=d_scale, w_pool=d_w_pool)
    return loss, grad_x, large, small


def _place():
    x, y, c = lax.axis_index("x"), lax.axis_index("y"), lax.axis_index("c")
    chips = [(1 - x, y), (x, 1 - y), (1 - x, 1 - y)]
    return x, y, c, chips


ANY = pl.BlockSpec(memory_space=pl.ANY)


def _row_chunks(rows, n, unit):
    units = rows // unit
    out, start = [], 0
    for i in range(n):
        size = (units // n + (1 if i < units % n else 0)) * unit
        out.append((start, size))
        start += size
    return out


GATHER_CHUNKS = _row_chunks(HALF_ROWS, 4, 32)
SWAP_CHUNKS = _row_chunks(HALF_ROWS, 8, 32)
JOIN_CHUNKS = _row_chunks(HALF_ROWS, 16, 32)
LOCAL_COPIES = 16


def _local_copy(src_rows, dst_rows, rows, sems):
    copies = [pltpu.make_async_copy(src_rows(pl.ds(start, size)), dst_rows(pl.ds(start, size)), sems.at[i])
              for i, (start, size) in enumerate(_row_chunks(rows, LOCAL_COPIES, 32))]
    for cp in copies:
        cp.start()
    return copies


def _gather_weights(pack):
    n_ch = len(GATHER_CHUNKS)

    def body(w_ref, out_ref, send1, recv1, send2, recv2, local_sem):
        x, y, c, chips = _place()
        me = 2 * x + y
        sibling = (x, y, 1 - c)
        own = _local_copy(lambda r: w_ref.at[r], lambda r: out_ref.at[me, r], PACK_ROWS, local_sem)

        def rows(core, ch):
            start, size = GATHER_CHUNKS[ch]
            return pl.ds(core * HALF_ROWS + start, size)

        def direct(j, ch, chip_xy, src_chip):
            cx, cy = chip_xy
            return pltpu.make_async_remote_copy(
                src_ref=w_ref.at[rows(c, ch)], dst_ref=out_ref.at[src_chip, rows(c, ch)],
                send_sem=send1.at[j * n_ch + ch], recv_sem=recv1.at[j * n_ch + ch],
                device_id=(cx, cy, c), device_id_type=MESH)

        def passed(j, ch, chip, core):
            return pltpu.make_async_remote_copy(
                src_ref=out_ref.at[chip, rows(core, ch)], dst_ref=out_ref.at[chip, rows(core, ch)],
                send_sem=send2.at[j * n_ch + ch], recv_sem=recv2.at[j * n_ch + ch],
                device_id=sibling, device_id_type=MESH)

        sends = [direct(j, ch, chip, me) for ch in range(n_ch) for j, chip in enumerate(chips)]
        for cp in sends:
            cp.start()
        forwards = []
        for ch in range(n_ch):
            for j, (cx, cy) in enumerate(chips):
                direct(j, ch, (cx, cy), 2 * cx + cy).wait_recv()
                fw = passed(j, ch, 2 * cx + cy, c)
                fw.start()
                forwards.append(fw)
        for ch in range(n_ch):
            for j, (cx, cy) in enumerate(chips):
                passed(j, ch, 2 * cx + cy, 1 - c).wait_recv()
        for cp in sends + forwards:
            cp.wait_send()
        for cp in own:
            cp.wait()

    n_sem = 3 * n_ch
    return pl.pallas_call(
        body, name="gather_weights", in_specs=[ANY], out_specs=ANY,
        out_shape=jax.ShapeDtypeStruct((N_CHIPS, PACK_ROWS, D_MODEL), pack.dtype),
        scratch_shapes=[pltpu.SemaphoreType.DMA((n_sem,))] * 4 + [pltpu.SemaphoreType.DMA((LOCAL_COPIES,))],
    )(pack)


def _swap_halves(g):
    n_ch = len(SWAP_CHUNKS)

    def body(g_ref, theirs_ref, send, recv):
        x, y, c, _ = _place()

        def piece(s, ch, core):
            start, size = SWAP_CHUNKS[ch]
            return pltpu.make_async_remote_copy(
                src_ref=g_ref.at[s, pl.ds(core * HALF_ROWS + start, size)], dst_ref=theirs_ref.at[s, pl.ds(start, size)],
                send_sem=send.at[s * n_ch + ch], recv_sem=recv.at[s * n_ch + ch],
                device_id=(x, y, 1 - c), device_id_type=MESH)

        copies = [piece(s, ch, 1 - c) for s in range(N_CHIPS) for ch in range(n_ch)]
        for cp in copies:
            cp.start()
        for cp in copies:
            cp.wait()

    return pl.pallas_call(
        body, name="swap_halves", in_specs=[ANY], out_specs=ANY,
        out_shape=jax.ShapeDtypeStruct((N_CHIPS, HALF_ROWS, D_MODEL), g.dtype),
        scratch_shapes=[pltpu.SemaphoreType.DMA((N_CHIPS * n_ch,))] * 2,
    )(g)


ADD_ROWS = 376


def _add_cores(g, theirs):
    n_t = HALF_ROWS // ADD_ROWS

    def body(c_ref, g_ref, t_ref, o_ref):
        o_ref[...] = g_ref[...] + t_ref[...]

    blk = pl.BlockSpec((1, ADD_ROWS, D_MODEL), lambda s, t, c_ref: (s, t, 0))
    return pl.pallas_call(
        body, name="add_cores",
        grid_spec=pltpu.PrefetchScalarGridSpec(
            num_scalar_prefetch=1, grid=(N_CHIPS, n_t),
            in_specs=[pl.BlockSpec((1, ADD_ROWS, D_MODEL), lambda s, t, c_ref: (s, c_ref[0] * n_t + t, 0)), blk],
            out_specs=blk),
        out_shape=jax.ShapeDtypeStruct(theirs.shape, theirs.dtype),
        compiler_params=_params("parallel", "parallel"),
    )(lax.axis_index("c").astype(jnp.int32).reshape(1), g, theirs)


def _scatter_to_chips(h):
    n_ch = len(GATHER_CHUNKS)

    def body(h_ref, out_ref, send, recv, local_sem):
        x, y, c, chips = _place()
        me = 2 * x + y
        own = _local_copy(lambda r: h_ref.at[me, r], lambda r: out_ref.at[me, r], HALF_ROWS, local_sem)

        def piece(j, ch, chip_xy, dst_chip, src_chip):
            cx, cy = chip_xy
            start, size = GATHER_CHUNKS[ch]
            return pltpu.make_async_remote_copy(
                src_ref=h_ref.at[dst_chip, pl.ds(start, size)], dst_ref=out_ref.at[src_chip, pl.ds(start, size)],
                send_sem=send.at[j * n_ch + ch], recv_sem=recv.at[j * n_ch + ch],
                device_id=(cx, cy, c), device_id_type=MESH)

        sends = [piece(j, ch, (cx, cy), 2 * cx + cy, me) for ch in range(n_ch) for j, (cx, cy) in enumerate(chips)]
        for cp in sends:
            cp.start()
        for ch in range(n_ch):
            for j, (cx, cy) in enumerate(chips):
                piece(j, ch, (cx, cy), me, 2 * cx + cy).wait_recv()
        for cp in sends:
            cp.wait_send()
        for cp in own:
            cp.wait()

    return pl.pallas_call(
        body, name="scatter_to_chips", in_specs=[ANY], out_specs=ANY,
        out_shape=jax.ShapeDtypeStruct(h.shape, h.dtype),
        scratch_shapes=[pltpu.SemaphoreType.DMA((3 * n_ch,))] * 2 + [pltpu.SemaphoreType.DMA((LOCAL_COPIES,))],
    )(h)


def _join_halves(r):
    n_ch = len(JOIN_CHUNKS)

    def body(r_ref, out_ref, send, recv, local_sem):
        x, y, c, _ = _place()
        own = _local_copy(lambda r: r_ref.at[r], lambda r: out_ref.at[c, r], HALF_ROWS, local_sem)

        def piece(ch, core):
            start, size = JOIN_CHUNKS[ch]
            return pltpu.make_async_remote_copy(
                src_ref=r_ref.at[pl.ds(start, size)], dst_ref=out_ref.at[core, pl.ds(start, size)],
                send_sem=send.at[ch], recv_sem=recv.at[ch], device_id=(x, y, 1 - c), device_id_type=MESH)

        copies = [piece(ch, c) for ch in range(n_ch)]
        for cp in copies:
            cp.start()
        for ch in range(n_ch):
            piece(ch, 1 - c).wait_recv()
        for cp in copies:
            cp.wait_send()
        for cp in own:
            cp.wait()

    return pl.pallas_call(
        body, name="join_halves", in_specs=[ANY], out_specs=ANY,
        out_shape=jax.ShapeDtypeStruct((2,) + r.shape, r.dtype),
        scratch_shapes=[pltpu.SemaphoreType.DMA((n_ch,))] * 2 + [pltpu.SemaphoreType.DMA((LOCAL_COPIES,))],
    )(r)


def _add_slabs(terms, k, name):
    rows = terms[0][0].shape[1]
    tr = ADD_ROWS
    n = len(terms)

    def body(*refs):
        acc = refs[0][...]
        for r in refs[1:n]:
            acc = acc + r[...]
        refs[n][...] = acc

    slab = lambda first: pl.BlockSpec((1, tr, D_MODEL), lambda i, t: (first + i, t, 0))
    return pl.pallas_call(
        body, name=name, grid=(k, rows // tr), in_specs=[slab(first) for _, first in terms],
        out_specs=pl.BlockSpec((1, tr, D_MODEL), lambda i, t: (i, t, 0)),
        out_shape=jax.ShapeDtypeStruct((k, rows, D_MODEL), terms[0][0].dtype),
        compiler_params=_params("parallel", "parallel"),
    )(*[a for a, _ in terms])


def _sum_small(block):
    def body(b_ref, out_ref, gathered, send, recv):
        x, y, c, _ = _place()
        me = 4 * x + 2 * y + c
        gathered[me] = b_ref[...]
        sends = []
        for kk in range(1, N_DEV):
            flip = lambda v, bit: 1 - v if bit else v
            peer = (flip(x, kk & 4), flip(y, kk & 2), flip(c, kk & 1))
            cp = pltpu.make_async_remote_copy(
                src_ref=b_ref, dst_ref=gathered.at[me], send_sem=send.at[kk - 1], recv_sem=recv.at[kk - 1],
                device_id=peer, device_id_type=MESH)
            cp.start()
            sends.append(cp)
        for kk in range(1, N_DEV):
            peer_index = jnp.bitwise_xor(me, kk)
            pltpu.make_async_remote_copy(
                src_ref=b_ref, dst_ref=gathered.at[peer_index], send_sem=send.at[kk - 1], recv_sem=recv.at[kk - 1],
                device_id=(x, y, c), device_id_type=MESH).wait_recv()
        for cp in sends:
            cp.wait_send()
        acc = gathered[0]
        for dev in range(1, N_DEV):
            acc = acc + gathered[dev]
        out_ref[...] = acc

    vmem = pl.BlockSpec(memory_space=pltpu.VMEM)
    return pl.pallas_call(
        body, name="sum_small", in_specs=[vmem], out_specs=vmem,
        out_shape=jax.ShapeDtypeStruct(block.shape, block.dtype),
        scratch_shapes=[pltpu.VMEM((N_DEV,) + block.shape, block.dtype),
                        pltpu.SemaphoreType.DMA((N_DEV - 1,)), pltpu.SemaphoreType.DMA((N_DEV - 1,))],
    )(block)


def _adamw(w, g, m, v, name):
    rows, cols = w.shape
    tr = rows
    for cand in (512, 256, 128, 64, 32, 16, 8):
        if rows % cand == 0:
            tr = cand
            break
    c1 = 1.0 - ADAM_B1 ** ADAM_STEP
    c2 = 1.0 - ADAM_B2 ** ADAM_STEP

    def body(w_ref, g_ref, m_ref, v_ref, d_ref, nm_ref, nv_ref):
        gv = g_ref[...]
        nm = ADAM_B1 * m_ref[...] + (1.0 - ADAM_B1) * gv
        nv = ADAM_B2 * v_ref[...] + (1.0 - ADAM_B2) * (gv * gv)
        nm_ref[...] = nm
        nv_ref[...] = nv
        d_ref[...] = -ADAM_LR * ((nm / c1) / (jnp.sqrt(nv / c2) + ADAM_EPS) + ADAM_WD * w_ref[...])

    blk = pl.BlockSpec((tr, cols), lambda i: (i, 0))
    shape = jax.ShapeDtypeStruct((rows, cols), f32)
    return pl.pallas_call(
        body, name=name, grid=(rows // tr,), in_specs=[blk] * 4, out_specs=[blk] * 3, out_shape=[shape] * 3,
        compiler_params=_params("parallel"),
    )(w, g, m, v)


LARGE = ("w_in", "w_out", "w_gate", "w_up", "w_down")
SMALL = ("ln_pre_mix", "ln_post_mix", "ln_pre_ffn", "ln_post_ffn", "pool_scale", "w_pool")
COLUMN_SHARDED = {"w_in": IN_WIDTH // N_CHIPS, "w_gate": D_FF // N_CHIPS, "w_up": D_FF // N_CHIPS}


def _pack_shard(shards):
    return jnp.concatenate([shards[n].reshape(-1, D_MODEL) for n in LARGE], axis=0)


def _unpack_shard(pack, shapes):
    out, row = {}, 0
    for n, rows in zip(LARGE, PACK_SPLITS):
        out[n] = pack[row:row + rows].reshape(shapes[n])
        row += rows
    return out


def _whole_from_shards(packs):
    out, row = {}, 0
    for n, rows in zip(LARGE, PACK_SPLITS):
        part = packs[:, row:row + rows]
        if n in COLUMN_SHARDED:
            width = COLUMN_SHARDED[n]
            part = part.reshape(N_CHIPS, D_MODEL, width).transpose(1, 0, 2).reshape(D_MODEL, N_CHIPS * width)
        else:
            part = part.reshape(N_CHIPS * rows, D_MODEL)
        out[n] = part
        row += rows
    return out


def _shards_from_whole(grads):
    parts = []
    for n, rows in zip(LARGE, PACK_SPLITS):
        g = grads[n]
        if n in COLUMN_SHARDED:
            width = COLUMN_SHARDED[n]
            g = g.reshape(D_MODEL, N_CHIPS, width).transpose(1, 0, 2)
        parts.append(g.reshape(N_CHIPS, rows, D_MODEL))
    return jnp.concatenate(parts, axis=1)


def _pack_small(vals):
    rows = [vals[n].reshape(1, D_MODEL) for n in SMALL[:4]]
    rows.append(jnp.pad(vals["pool_scale"].reshape(1, POOL_WIDTH), ((0, 0), (0, D_MODEL - POOL_WIDTH))))
    rows.append(jnp.pad(vals["loss"].reshape(1, 1), ((0, 0), (0, D_MODEL - 1))))
    rows.append(jnp.zeros((2, D_MODEL), f32))
    rows.append(vals["w_pool"].reshape(16, D_MODEL))
    return jnp.concatenate(rows, axis=0)


def _unpack_small(block):
    out = {n: block[i:i + 1] for i, n in enumerate(SMALL[:4])}
    out["pool_scale"] = block[4:5, :POOL_WIDTH]
    out["loss"] = block[5, 0]
    out["w_pool"] = block[8:24].reshape(1, 4, POOL_GROUP, POOL_GROUP)
    return out


def kernel(x, ln_pre_mix, w_in, w_pool, pool_scale, w_out, ln_post_mix, ln_pre_ffn, w_gate, w_up, w_down, ln_post_ffn, loss_target, m_ln_pre_mix, m_w_in, m_w_pool, m_pool_scale, m_w_out, m_ln_post_mix, m_ln_pre_ffn, m_w_gate, m_w_up, m_w_down, m_ln_post_ffn, v_ln_pre_mix, v_w_in, v_w_pool, v_pool_scale, v_w_out, v_ln_post_mix, v_ln_pre_ffn, v_w_gate, v_w_up, v_w_down, v_ln_post_ffn):
    w = dict(ln_pre_mix=ln_pre_mix, w_in=w_in, w_pool=w_pool, pool_scale=pool_scale, w_out=w_out,
             ln_post_mix=ln_post_mix, ln_pre_ffn=ln_pre_ffn, w_gate=w_gate, w_up=w_up, w_down=w_down,
             ln_post_ffn=ln_post_ffn)
    m = dict(ln_pre_mix=m_ln_pre_mix, w_in=m_w_in, w_pool=m_w_pool, pool_scale=m_pool_scale, w_out=m_w_out,
             ln_post_mix=m_ln_post_mix, ln_pre_ffn=m_ln_pre_ffn, w_gate=m_w_gate, w_up=m_w_up, w_down=m_w_down,
             ln_post_ffn=m_ln_post_ffn)
    v = dict(ln_pre_mix=v_ln_pre_mix, w_in=v_w_in, w_pool=v_w_pool, pool_scale=v_pool_scale, w_out=v_w_out,
             ln_post_mix=v_ln_post_mix, ln_pre_ffn=v_ln_pre_ffn, w_gate=v_w_gate, w_up=v_w_up, w_down=v_w_down,
             ln_post_ffn=v_ln_post_ffn)

    packs = _gather_weights(_pack_shard({n: w[n][0].astype(bf16) for n in LARGE}))
    whole = _whole_from_shards(packs)

    loss, grad_x, large, small = _local_grads(
        x[0], loss_target[0], ln_pre_mix, w_pool[0], pool_scale, ln_post_mix, ln_pre_ffn, ln_post_ffn,
        whole["w_in"], whole["w_out"], whole["w_gate"], whole["w_up"], whole["w_down"])

    shard_major = _shards_from_whole(large)
    chip_sum = _add_cores(shard_major, _swap_halves(shard_major))
    pieces = _scatter_to_chips(chip_sum)
    reduced_half = _add_slabs([(pieces, j) for j in range(N_CHIPS)], 1, "add_chips")[0]
    reduced = _join_halves(reduced_half).reshape(PACK_ROWS, D_MODEL)
    shapes = {n: w[n].shape[1:] for n in LARGE}
    grads = _unpack_shard(reduced, shapes)

    total = _unpack_small(_sum_small(_pack_small(dict(small, loss=loss))))
    for n in SMALL:
        grads[n] = total[n]

    delta, new_m, new_v = {}, {}, {}
    for n in LARGE:
        delta[n], new_m[n], new_v[n] = _adamw(w[n][0], grads[n], m[n][0], v[n][0], "adamw_" + n)
    small_state = [_pack_small(dict({n: s[n] for n in SMALL}, loss=jnp.zeros((), f32))) for s in (w, m, v)]
    small_grad = _pack_small(dict({n: grads[n] for n in SMALL}, loss=jnp.zeros((), f32)))
    sd, sm, sv = _adamw(small_state[0], small_grad, small_state[1], small_state[2], "adamw_small")
    for out, block in ((delta, sd), (new_m, sm), (new_v, sv)):
        un = _unpack_small(block)
        for n in SMALL:
            out[n] = un[n]

    names = ("ln_pre_mix", "w_in", "w_pool", "pool_scale", "w_out", "ln_post_mix", "ln_pre_ffn", "w_gate", "w_up",
             "w_down", "ln_post_ffn")
    full = lambda d: [d[n].reshape(w[n].shape) for n in names]
    return (total["loss"], grad_x[None], *full(grads), *full(delta), *full(new_m), *full(new_v))
```

```python
import numpy as np
import jax
import jax.numpy as jnp
from jax import lax
from jax.experimental import pallas as pl
from jax.experimental.pallas import tpu as pltpu

D_MODEL = 1024
POOL_WIDTH = 256
POOL_GROUP = 64
ATTN_WIDTH = 768
HEAD_DIM = 64
IN_WIDTH = 2560
D_FF = 2816
BLOCK = 128
DILATIONS = (1, 4, 16)
ROPE_THETA = 10000.0
EPS = 1e-6
ATTN_SCALE = 0.125
NEG = -1e30

ADAM_LR = 0.001
ADAM_B1 = 0.9
ADAM_B2 = 0.999
ADAM_EPS = 1e-08
ADAM_WD = 0.01
ADAM_STEP = 10

N_CHIPS = 4
N_DEV = 8
PACK_SPLITS = (640, 256, 704, 704, 704)
PACK_ROWS = sum(PACK_SPLITS)
HALF_ROWS = PACK_ROWS // 2
SMALL_ROWS = 24

VMEM_LIMIT_V7X = 56 * 1024 * 1024
MESH = pl.DeviceIdType.MESH

f32 = jnp.float32
bf16 = jnp.bfloat16


def _params(*sem):
    return pltpu.CompilerParams(dimension_semantics=sem, vmem_limit_bytes=VMEM_LIMIT_V7X)


def _dot(a, b):
    return jnp.dot(a, b, preferred_element_type=f32)


def _dot_nt(a, b):
    return lax.dot_general(a, b, (((1,), (1,)), ((), ())), preferred_element_type=f32)


def _dot_tn(a, b):
    return lax.dot_general(a, b, (((0,), (0,)), ((), ())), preferred_element_type=f32)


def _rope_partner(a, first_half):
    return jnp.where(first_half, pltpu.roll(a, 96, 1), pltpu.roll(a, 32, 1))


def _first_half_mask(rows):
    lane = lax.broadcasted_iota(jnp.int32, (rows, 128), 1)
    return (lane % HEAD_DIM) < (HEAD_DIM // 2)


def _in_proj(x, g1, w_in, cos_t, sin_t):
    S = x.shape[0]
    ts = 512

    def body(x_ref, g_ref, w_ref, cos_ref, sin_ref, h_ref, u_ref, q_ref, k_ref, v_ref):
        xv = x_ref[...]
        r = lax.rsqrt(jnp.mean(xv * xv, axis=-1, keepdims=True) + EPS)
        h = ((xv * r) * g_ref[...]).astype(bf16)
        h_ref[...] = h
        proj = _dot(h, w_ref[...])
        u_ref[...] = proj[:, :POOL_WIDTH]
        cos = cos_ref[...]
        sin = sin_ref[...]
        first = _first_half_mask(ts)
        for j in range(ATTN_WIDTH // 128):
            for base, ref in ((POOL_WIDTH, q_ref), (POOL_WIDTH + ATTN_WIDTH, k_ref)):
                a = proj[:, base + j * 128: base + (j + 1) * 128]
                ref[:, j * 128:(j + 1) * 128] = (a * cos + _rope_partner(a, first) * sin).astype(bf16)
        v_ref[...] = proj[:, POOL_WIDTH + 2 * ATTN_WIDTH:].astype(bf16)

    row = lambda w: pl.BlockSpec((ts, w), lambda i: (i, 0))
    return pl.pallas_call(
        body, name="in_proj", grid=(S // ts,),
        in_specs=[row(D_MODEL), pl.BlockSpec((1, D_MODEL), lambda i: (0, 0)),
                  pl.BlockSpec((D_MODEL, IN_WIDTH), lambda i: (0, 0)), row(128), row(128)],
        out_specs=[row(D_MODEL), row(POOL_WIDTH), row(ATTN_WIDTH), row(ATTN_WIDTH), row(ATTN_WIDTH)],
        out_shape=[jax.ShapeDtypeStruct((S, D_MODEL), bf16), jax.ShapeDtypeStruct((S, POOL_WIDTH), f32),
                   jax.ShapeDtypeStruct((S, ATTN_WIDTH), bf16), jax.ShapeDtypeStruct((S, ATTN_WIDTH), bf16),
                   jax.ShapeDtypeStruct((S, ATTN_WIDTH), bf16)],
        compiler_params=_params("parallel"),
    )(x, g1, w_in, cos_t, sin_t)


POOL_HALO = 16


def _pool_lane_group(rows):
    return lax.broadcasted_iota(jnp.int32, (rows, POOL_WIDTH), 1) // POOL_GROUP


def _pool_select(group, s2, s4, s8, s16):
    return jnp.where(group == 0, s2, jnp.where(group == 1, s4, jnp.where(group == 2, s8, s16)))


def _pool_count(t0, rows):
    group = _pool_lane_group(rows)
    t = t0 + lax.broadcasted_iota(jnp.int32, (rows, POOL_WIDTH), 0)
    win = _pool_select(group, 2, 4, 8, 16)
    return jnp.minimum(t + 1, win).astype(f32)


def _pool_diff(u_halo, u_tile, t0):
    ts = u_tile.shape[0]
    ext = jnp.concatenate([u_halo, u_tile], axis=0)
    s2 = ext + pltpu.roll(ext, 1, 0)
    s4 = s2 + pltpu.roll(s2, 2, 0)
    s8 = s4 + pltpu.roll(s4, 4, 0)
    s16 = s8 + pltpu.roll(s8, 8, 0)
    group = _pool_lane_group(ts + POOL_HALO)
    wsum = _pool_select(group, s2, s4, s8, s16)[POOL_HALO:]
    return wsum / _pool_count(t0, ts) - u_tile


def _pool_specs(ts, n_tiles):
    tile = pl.BlockSpec((ts, POOL_WIDTH), lambda i: (i, 0))
    per = ts // POOL_HALO
    before = pl.BlockSpec((POOL_HALO, POOL_WIDTH), lambda i: (jnp.maximum(i * per - 1, 0), 0))
    after = pl.BlockSpec((POOL_HALO, POOL_WIDTH), lambda i: (jnp.minimum((i + 1) * per, n_tiles * per - 1), 0))
    return tile, before, after


def _pool_fwd(u, w_bd, scale):
    S = u.shape[0]
    ts = 512
    n_tiles = S // ts

    def body(u_ref, halo_ref, w_ref, sc_ref, y_ref):
        i = pl.program_id(0)
        halo = jnp.where(i > 0, halo_ref[...], 0.0)
        d = _pool_diff(halo, u_ref[...], i * ts)
        y_ref[...] = (_dot(d.astype(bf16), w_ref[...]) * sc_ref[...]).astype(bf16)

    tile, before, _ = _pool_specs(ts, n_tiles)
    return pl.pallas_call(
        body, name="pool_fwd", grid=(n_tiles,),
        in_specs=[tile, before, pl.BlockSpec((POOL_WIDTH, POOL_WIDTH), lambda i: (0, 0)),
                  pl.BlockSpec((1, POOL_WIDTH), lambda i: (0, 0))],
        out_specs=tile, out_shape=jax.ShapeDtypeStruct((S, POOL_WIDTH), bf16),
        compiler_params=_params("parallel"),
    )(u, u, w_bd, scale)


def _pool_bwd(u, dy, w_bd, scale):
    S = u.shape[0]
    ts = 512
    n_tiles = S // ts

    def body(u_ref, halo_ref, dy_ref, dy_next_ref, w_ref, sc_ref, du_ref, dw_ref, dsc_ref):
        i = pl.program_id(0)

        @pl.when(i == 0)
        def _():
            dw_ref[...] = jnp.zeros_like(dw_ref)
            dsc_ref[...] = jnp.zeros_like(dsc_ref)

        halo = jnp.where(i > 0, halo_ref[...], 0.0)
        d = _pool_diff(halo, u_ref[...], i * ts).astype(bf16)
        w = w_ref[...]
        sc = sc_ref[...]
        dy_tile = dy_ref[...]
        z = _dot(d, w)
        dsc_ref[...] += jnp.sum(dy_tile * z, axis=0, keepdims=True)
        dy_next = jnp.where(i < n_tiles - 1, dy_next_ref[...], 0.0)
        dz = (jnp.concatenate([dy_tile, dy_next], axis=0) * sc).astype(bf16)
        dw_ref[...] += _dot_tn(d, dz[:ts])
        dd = _dot_nt(dz, w)
        e = dd / _pool_count(i * ts, ts + POOL_HALO)
        n = ts + POOL_HALO
        f2 = e + pltpu.roll(e, n - 1, 0)
        f4 = f2 + pltpu.roll(f2, n - 2, 0)
        f8 = f4 + pltpu.roll(f4, n - 4, 0)
        f16 = f8 + pltpu.roll(f8, n - 8, 0)
        fsum = _pool_select(_pool_lane_group(n), f2, f4, f8, f16)
        du_ref[...] = (fsum[:ts] - dd[:ts]).astype(bf16)

    tile, before, after = _pool_specs(ts, n_tiles)
    return pl.pallas_call(
        body, name="pool_bwd", grid=(n_tiles,),
        in_specs=[tile, before, tile, after, pl.BlockSpec((POOL_WIDTH, POOL_WIDTH), lambda i: (0, 0)),
                  pl.BlockSpec((1, POOL_WIDTH), lambda i: (0, 0))],
        out_specs=[tile, pl.BlockSpec((POOL_WIDTH, POOL_WIDTH), lambda i: (0, 0)),
                   pl.BlockSpec((1, POOL_WIDTH), lambda i: (0, 0))],
        out_shape=[jax.ShapeDtypeStruct((S, POOL_WIDTH), bf16), jax.ShapeDtypeStruct((POOL_WIDTH, POOL_WIDTH), f32),
                   jax.ShapeDtypeStruct((1, POOL_WIDTH), f32)],
        compiler_params=_params("arbitrary"),
    )(u, u, dy, dy, w_bd, scale)


def _band_mask(n):
    qi = lax.broadcasted_iota(jnp.int32, (BLOCK, 2 * BLOCK), 0)
    kj = lax.broadcasted_iota(jnp.int32, (BLOCK, 2 * BLOCK), 1)
    return (kj >= qi) & (kj <= qi + BLOCK) & ((kj >= BLOCK) | (n > 0))


def _head0_mask(rows=BLOCK):
    return lax.broadcasted_iota(jnp.int32, (rows, 128), 1) < HEAD_DIM


def _per_head(stat, h0, h):
    other = pltpu.roll(stat, HEAD_DIM, 1)
    full = jnp.where(h0, stat, other) if h == 0 else jnp.where(h0, other, stat)
    return jnp.concatenate([full, full], axis=1)


def _attn_specs(L, d):
    nb = L // BLOCK
    cur = pl.BlockSpec((BLOCK, ATTN_WIDTH), lambda r, n: (n, r))
    prev = pl.BlockSpec((BLOCK, ATTN_WIDTH), lambda r, n: (jnp.maximum(n - 1, 0), r))
    return nb, cur, prev


def _attn_fwd(q, k, v, d):
    S = q.shape[0]
    L = S // d
    view = lambda a: a.reshape(L, d * ATTN_WIDTH)

    def body(q_ref, kc_ref, kp_ref, vc_ref, vp_ref, o_ref, lse_ref):
        valid = _band_mask(pl.program_id(1))
        h0 = _head0_mask()
        for j in range(ATTN_WIDTH // 128):
            cols = slice(j * 128, (j + 1) * 128)
            qv = q_ref[:, cols]
            kb = jnp.concatenate([kp_ref[:, cols], kc_ref[:, cols]], axis=0)
            vb = jnp.concatenate([vp_ref[:, cols], vc_ref[:, cols]], axis=0)
            outs, lses = [], []
            for h in range(2):
                keep = h0 if h == 0 else jnp.logical_not(h0)
                qh = jnp.where(keep, qv, jnp.zeros_like(qv))
                s = jnp.where(valid, _dot_nt(qh, kb) * ATTN_SCALE, NEG)
                m = jnp.max(s, axis=1, keepdims=True)
                e = jnp.exp(s - m)
                den = jnp.sum(e, axis=1, keepdims=True)
                outs.append(_dot((e / den).astype(bf16), vb))
                lses.append(jnp.broadcast_to(m + jnp.log(den), (BLOCK, 128)))
            o_ref[:, cols] = jnp.where(h0, outs[0], outs[1])
            lse_ref[:, cols] = jnp.where(h0, lses[0], lses[1])

    nb, cur, prev = _attn_specs(L, d)
    o, lse = pl.pallas_call(
        body, name=f"attn_fwd_d{d}", grid=(d, nb),
        in_specs=[cur, cur, prev, cur, prev], out_specs=[cur, cur],
        out_shape=[jax.ShapeDtypeStruct((L, d * ATTN_WIDTH), f32)] * 2,
        compiler_params=_params("parallel", "parallel"),
    )(view(q), view(k), view(k), view(v), view(v))
    return o.reshape(S, ATTN_WIDTH), lse.reshape(S, ATTN_WIDTH)


def _attn_merge(os, lses):
    S = os[0].shape[0]
    ts = 512

    def body(o1, o2, o3, l1, l2, l3, out_ref, lse_ref):
        a, b, c = l1[...], l2[...], l3[...]
        m = jnp.maximum(jnp.maximum(a, b), c)
        ea, eb, ec = jnp.exp(a - m), jnp.exp(b - m), jnp.exp(c - m)
        tot = ea + eb + ec
        out_ref[...] = ((ea / tot) * o1[...] + (eb / tot) * o2[...] + (ec / tot) * o3[...]).astype(bf16)
        lse_ref[...] = m + jnp.log(tot)

    row = pl.BlockSpec((ts, ATTN_WIDTH), lambda i: (i, 0))
    return pl.pallas_call(
        body, name="attn_merge", grid=(S // ts,), in_specs=[row] * 6, out_specs=[row, row],
        out_shape=[jax.ShapeDtypeStruct((S, ATTN_WIDTH), bf16), jax.ShapeDtypeStruct((S, ATTN_WIDTH), f32)],
        compiler_params=_params("parallel"),
    )(*os, *lses)


def _attn_bwd(q, k, v, do, lse, delta, d):
    S = q.shape[0]
    L = S // d
    nb = L // BLOCK
    view = lambda a: a.reshape(L, d * ATTN_WIDTH)

    def body(q_ref, kc_ref, kp_ref, vc_ref, vp_ref, do_ref, lse_ref, dl_ref, dq_ref, dk_ref, dv_ref, ck_ref, cv_ref):
        n = pl.program_id(1)

        @pl.when(n < nb)
        def _():
            valid = _band_mask(n)
            h0 = _head0_mask()
            h0_band = _head0_mask(2 * BLOCK)
            for j in range(ATTN_WIDTH // 128):
                cols = slice(j * 128, (j + 1) * 128)
                qv = q_ref[:, cols]
                dov = do_ref[:, cols]
                kb = jnp.concatenate([kp_ref[:, cols], kc_ref[:, cols]], axis=0)
                vb = jnp.concatenate([vp_ref[:, cols], vc_ref[:, cols]], axis=0)
                lse_v = lse_ref[:, cols]
                dl_v = dl_ref[:, cols]
                dqs, dks, dvs = [], [], []
                for h in range(2):
                    keep = h0 if h == 0 else jnp.logical_not(h0)
                    qh = jnp.where(keep, qv, jnp.zeros_like(qv))
                    doh = jnp.where(keep, dov, jnp.zeros_like(dov))
                    s = jnp.where(valid, _dot_nt(qh, kb) * ATTN_SCALE, NEG)
                    p = jnp.exp(s - _per_head(lse_v, h0, h))
                    dp = _dot_nt(doh, vb)
                    ds = (p * (dp - _per_head(dl_v, h0, h)) * ATTN_SCALE).astype(bf16)
                    dqs.append(_dot(ds, kb))
                    dks.append(_dot_tn(ds, qv))
                    dvs.append(_dot_tn(p.astype(bf16), dov))
                dq_ref[:, cols] = jnp.where(h0, dqs[0], dqs[1]).astype(bf16)
                dkb = jnp.where(h0_band, dks[0], dks[1])
                dvb = jnp.where(h0_band, dvs[0], dvs[1])

                @pl.when(n > 0)
                def _():
                    dk_ref[:, cols] = (ck_ref[:, cols] + dkb[:BLOCK]).astype(bf16)
                    dv_ref[:, cols] = (cv_ref[:, cols] + dvb[:BLOCK]).astype(bf16)

                ck_ref[:, cols] = dkb[BLOCK:]
                cv_ref[:, cols] = dvb[BLOCK:]

        @pl.when(n == nb)
        def _():
            dk_ref[...] = ck_ref[...].astype(bf16)
            dv_ref[...] = cv_ref[...].astype(bf16)

    last = nb - 1
    cur = pl.BlockSpec((BLOCK, ATTN_WIDTH), lambda r, n: (jnp.minimum(n, last), r))
    prev = pl.BlockSpec((BLOCK, ATTN_WIDTH), lambda r, n: (jnp.clip(n - 1, 0, last), r))
    out = jax.ShapeDtypeStruct((L, d * ATTN_WIDTH), bf16)
    dq, dk, dv = pl.pallas_call(
        body, name=f"attn_bwd_d{d}", grid=(d, nb + 1),
        in_specs=[cur, cur, prev, cur, prev, cur, cur, cur], out_specs=[cur, prev, prev],
        out_shape=[out, out, out],
        scratch_shapes=[pltpu.VMEM((BLOCK, ATTN_WIDTH), f32), pltpu.VMEM((BLOCK, ATTN_WIDTH), f32)],
        compiler_params=_params("parallel", "arbitrary"),
    )(view(q), view(k), view(k), view(v), view(v), view(do), view(lse), view(delta))
    return dq.reshape(S, ATTN_WIDTH), dk.reshape(S, ATTN_WIDTH), dv.reshape(S, ATTN_WIDTH)


def _rms(v):
    return lax.rsqrt(jnp.mean(v * v, axis=-1, keepdims=True) + EPS)


def _out_proj(pool_out, attn_out, w_out, x, g2, g3):
    S = x.shape[0]
    ts = 512

    def body(p_ref, a_ref, w_ref, x_ref, g2_ref, g3_ref, mix_ref, x2_ref, h2_ref):
        mix = _dot(p_ref[...], w_ref[:POOL_WIDTH, :]) + _dot(a_ref[...], w_ref[POOL_WIDTH:, :])
        mix_ref[...] = mix
        x2 = x_ref[...] + (mix * _rms(mix)) * g2_ref[...]
        x2_ref[...] = x2
        h2_ref[...] = ((x2 * _rms(x2)) * g3_ref[...]).astype(bf16)

    row = lambda w: pl.BlockSpec((ts, w), lambda i: (i, 0))
    gain = pl.BlockSpec((1, D_MODEL), lambda i: (0, 0))
    return pl.pallas_call(
        body, name="out_proj", grid=(S // ts,),
        in_specs=[row(POOL_WIDTH), row(ATTN_WIDTH), pl.BlockSpec((D_MODEL, D_MODEL), lambda i: (0, 0)),
                  row(D_MODEL), gain, gain],
        out_specs=[row(D_MODEL)] * 3,
        out_shape=[jax.ShapeDtypeStruct((S, D_MODEL), f32), jax.ShapeDtypeStruct((S, D_MODEL), f32),
                   jax.ShapeDtypeStruct((S, D_MODEL), bf16)],
        compiler_params=_params("parallel"),
    )(pool_out, attn_out, w_out, x, g2, g3)


FF_TILE = 256
FF_HALF = D_FF // 2


def _sigmoid(g):
    return 1.0 / (1.0 + jnp.exp(-g))


def _ffn_fwd(h2, w_gate, w_up, w_down):
    S = h2.shape[0]
    ts = 1024

    def body(h_ref, wg_ref, wu_ref, wd_ref, gate_ref, up_ref, f_ref):
        j = pl.program_id(1)
        h = h_ref[...]
        gate = _dot(h, wg_ref[...])
        up = _dot(h, wu_ref[...])
        gate_ref[...] = gate.astype(bf16)
        up_ref[...] = up.astype(bf16)
        part = _dot((gate * _sigmoid(gate) * up).astype(bf16), wd_ref[...])

        @pl.when(j == 0)
        def _():
            f_ref[...] = part

        @pl.when(j > 0)
        def _():
            f_ref[...] += part

    act = pl.BlockSpec((ts, FF_TILE), lambda i, j: (i, j))
    return pl.pallas_call(
        body, name="ffn_fwd", grid=(S // ts, D_FF // FF_TILE),
        in_specs=[pl.BlockSpec((ts, D_MODEL), lambda i, j: (i, 0)),
                  pl.BlockSpec((D_MODEL, FF_TILE), lambda i, j: (0, j)),
                  pl.BlockSpec((D_MODEL, FF_TILE), lambda i, j: (0, j)),
                  pl.BlockSpec((FF_TILE, D_MODEL), lambda i, j: (j, 0))],
        out_specs=[act, act, pl.BlockSpec((ts, D_MODEL), lambda i, j: (i, 0))],
        out_shape=[jax.ShapeDtypeStruct((S, D_FF), bf16), jax.ShapeDtypeStruct((S, D_FF), bf16),
                   jax.ShapeDtypeStruct((S, D_MODEL), f32)],
        compiler_params=_params("parallel", "arbitrary"),
    )(h2, w_gate, w_up, w_down)


def _loss_head(f, x2, target, g4):
    S = f.shape[0]
    ts = 512

    def body(f_ref, x2_ref, t_ref, g_ref, dy_ref, df_ref, dg_ref, loss_ref):
        @pl.when(pl.program_id(0) == 0)
        def _():
            dg_ref[...] = jnp.zeros_like(dg_ref)
            loss_ref[...] = jnp.zeros_like(loss_ref)

        fv = f_ref[...]
        g = g_ref[...]
        r = _rms(fv)
        fhat = fv * r
        err = (x2_ref[...] + fhat * g) - t_ref[...]
        loss_ref[...] += 0.5 * jnp.sum(jnp.mean(err * err, axis=-1, keepdims=True), axis=0, keepdims=True)
        dy = err * (1.0 / D_MODEL)
        dy_ref[...] = dy
        dg_ref[...] += jnp.sum(dy * fhat, axis=0, keepdims=True)
        dyg = dy * g
        df_ref[...] = (r * (dyg - fhat * jnp.mean(dyg * fhat, axis=-1, keepdims=True))).astype(bf16)

    row = pl.BlockSpec((ts, D_MODEL), lambda i: (i, 0))
    gain = pl.BlockSpec((1, D_MODEL), lambda i: (0, 0))
    return pl.pallas_call(
        body, name="loss_head", grid=(S // ts,), in_specs=[row, row, row, gain],
        out_specs=[row, row, gain, pl.BlockSpec((1, 1), lambda i: (0, 0))],
        out_shape=[jax.ShapeDtypeStruct((S, D_MODEL), f32), jax.ShapeDtypeStruct((S, D_MODEL), bf16),
                   jax.ShapeDtypeStruct((1, D_MODEL), f32), jax.ShapeDtypeStruct((1, 1), f32)],
        compiler_params=_params("arbitrary"),
    )(f, x2, target, g4)


def _ffn_bwd(df, gate, up, w_gate, w_up, w_down):
    S = df.shape[0]
    ts = 1024

    def body(df_ref, gate_ref, up_ref, wg_ref, wu_ref, wd_ref, a_ref, dgate_ref, dup_ref, dh_ref):
        j = pl.program_id(1)
        da = _dot_nt(df_ref[...], wd_ref[...])
        g = gate_ref[...].astype(f32)
        u = up_ref[...].astype(f32)
        sig = _sigmoid(g)
        silu = g * sig
        a_ref[...] = (silu * u).astype(bf16)
        dup = (da * silu).astype(bf16)
        dgate = (da * u * (sig * (1.0 + g * (1.0 - sig)))).astype(bf16)
        dup_ref[...] = dup
        dgate_ref[...] = dgate
        part = _dot_nt(dgate, wg_ref[...]) + _dot_nt(dup, wu_ref[...])

        @pl.when(j == 0)
        def _():
            dh_ref[...] = part

        @pl.when(j > 0)
        def _():
            dh_ref[...] += part

    act = pl.BlockSpec((ts, FF_TILE), lambda i, j: (i, j))
    row = pl.BlockSpec((ts, D_MODEL), lambda i, j: (i, 0))
    return pl.pallas_call(
        body, name="ffn_bwd", grid=(S // ts, D_FF // FF_TILE),
        in_specs=[row, act, act,
                  pl.BlockSpec((D_MODEL, FF_TILE), lambda i, j: (0, j)),
                  pl.BlockSpec((D_MODEL, FF_TILE), lambda i, j: (0, j)),
                  pl.BlockSpec((FF_TILE, D_MODEL), lambda i, j: (j, 0))],
        out_specs=[act, act, act, row],
        out_shape=[jax.ShapeDtypeStruct((S, D_FF), bf16)] * 3 + [jax.ShapeDtypeStruct((S, D_MODEL), f32)],
        compiler_params=_params("parallel", "arbitrary"),
    )(df, gate, up, w_gate, w_up, w_down)


def _norm_bwd(dh2, dy, x2, mix, g3, g2):
    S = dh2.shape[0]
    ts = 512

    def body(dh_ref, dy_ref, x2_ref, mix_ref, g3_ref, g2_ref, dx2_ref, dmix_ref, dg3_ref, dg2_ref):
        @pl.when(pl.program_id(0) == 0)
        def _():
            dg3_ref[...] = jnp.zeros_like(dg3_ref)
            dg2_ref[...] = jnp.zeros_like(dg2_ref)

        dh = dh_ref[...]
        x2 = x2_ref[...]
        r3 = _rms(x2)
        xhat = x2 * r3
        dg3_ref[...] += jnp.sum(dh * xhat, axis=0, keepdims=True)
        dhg = dh * g3_ref[...]
        dx2 = dy_ref[...] + r3 * (dhg - xhat * jnp.mean(dhg * xhat, axis=-1, keepdims=True))
        dx2_ref[...] = dx2
        mix = mix_ref[...]
        r2 = _rms(mix)
        mhat = mix * r2
        dg2_ref[...] += jnp.sum(dx2 * mhat, axis=0, keepdims=True)
        dmg = dx2 * g2_ref[...]
        dmix_ref[...] = (r2 * (dmg - mhat * jnp.mean(dmg * mhat, axis=-1, keepdims=True))).astype(bf16)

    row = pl.BlockSpec((ts, D_MODEL), lambda i: (i, 0))
    gain = pl.BlockSpec((1, D_MODEL), lambda i: (0, 0))
    return pl.pallas_call(
        body, name="norm_bwd", grid=(S // ts,), in_specs=[row, row, row, row, gain, gain],
        out_specs=[row, row, gain, gain],
        out_shape=[jax.ShapeDtypeStruct((S, D_MODEL), f32), jax.ShapeDtypeStruct((S, D_MODEL), bf16),
                   jax.ShapeDtypeStruct((1, D_MODEL), f32), jax.ShapeDtypeStruct((1, D_MODEL), f32)],
        compiler_params=_params("arbitrary"),
    )(dh2, dy, x2, mix, g3, g2)


def _out_proj_bwd(dmix, w_out, attn_out, head_ones):
    S = dmix.shape[0]
    ts = 512

    def body(dm_ref, w_ref, o_ref, ones_ref, dp_ref, do_ref, dl_ref):
        dcat = _dot_nt(dm_ref[...], w_ref[...])
        dp_ref[...] = dcat[:, :POOL_WIDTH]
        do = dcat[:, POOL_WIDTH:]
        do_ref[...] = do.astype(bf16)
        prod = do * o_ref[...].astype(f32)
        hi = prod.astype(bf16)
        lo = (prod - hi.astype(f32)).astype(bf16)
        dl_ref[...] = _dot(hi, ones_ref[...]) + _dot(lo, ones_ref[...])

    row = lambda w: pl.BlockSpec((ts, w), lambda i: (i, 0))
    return pl.pallas_call(
        body, name="out_proj_bwd", grid=(S // ts,),
        in_specs=[row(D_MODEL), pl.BlockSpec((D_MODEL, D_MODEL), lambda i: (0, 0)), row(ATTN_WIDTH),
                  pl.BlockSpec((ATTN_WIDTH, ATTN_WIDTH), lambda i: (0, 0))],
        out_specs=[row(POOL_WIDTH), row(ATTN_WIDTH), row(ATTN_WIDTH)],
        out_shape=[jax.ShapeDtypeStruct((S, POOL_WIDTH), f32), jax.ShapeDtypeStruct((S, ATTN_WIDTH), bf16),
                   jax.ShapeDtypeStruct((S, ATTN_WIDTH), f32)],
        compiler_params=_params("parallel"),
    )(dmix, w_out, attn_out, head_ones)


def _in_proj_bwd(du, dqs, dks, dvs, cos_t, sin_t, w_in, x, dx2, g1):
    S = x.shape[0]
    ts = 256

    def body(du_ref, dq1, dq2, dq3, dk1, dk2, dk3, dv1, dv2, dv3, cos_ref, sin_ref, w_ref, x_ref, dx2_ref, g_ref,
             gx_ref, dproj_ref, dg_ref):
        @pl.when(pl.program_id(0) == 0)
        def _():
            dg_ref[...] = jnp.zeros_like(dg_ref)

        dproj_ref[:, :POOL_WIDTH] = du_ref[...]
        cos = cos_ref[...]
        sin = sin_ref[...]
        first = _first_half_mask(ts)
        for j in range(ATTN_WIDTH // 128):
            cols = slice(j * 128, (j + 1) * 128)
            for base, (r1, r2, r3) in ((POOL_WIDTH, (dq1, dq2, dq3)), (POOL_WIDTH + ATTN_WIDTH, (dk1, dk2, dk3))):
                g = r1[:, cols].astype(f32) + r2[:, cols].astype(f32) + r3[:, cols].astype(f32)
                pre = g * cos + _rope_partner(g * sin, first)
                dproj_ref[:, base + j * 128: base + (j + 1) * 128] = pre.astype(bf16)
        dv = dv1[...].astype(f32) + dv2[...].astype(f32) + dv3[...].astype(f32)
        dproj_ref[:, POOL_WIDTH + 2 * ATTN_WIDTH:] = dv.astype(bf16)

        dh = _dot_nt(dproj_ref[...], w_ref[...])
        xv = x_ref[...]
        r = _rms(xv)
        xhat = xv * r
        dg_ref[...] += jnp.sum(dh * xhat, axis=0, keepdims=True)
        dhg = dh * g_ref[...]
        gx_ref[...] = dx2_ref[...] + r * (dhg - xhat * jnp.mean(dhg * xhat, axis=-1, keepdims=True))

    row = lambda w: pl.BlockSpec((ts, w), lambda i: (i, 0))
    gain = pl.BlockSpec((1, D_MODEL), lambda i: (0, 0))
    return pl.pallas_call(
        body, name="in_proj_bwd", grid=(S // ts,),
        in_specs=[row(POOL_WIDTH)] + [row(ATTN_WIDTH)] * 9 + [row(128), row(128),
                  pl.BlockSpec((D_MODEL, IN_WIDTH), lambda i: (0, 0)), row(D_MODEL), row(D_MODEL), gain],
        out_specs=[row(D_MODEL), row(IN_WIDTH), gain],
        out_shape=[jax.ShapeDtypeStruct((S, D_MODEL), f32), jax.ShapeDtypeStruct((S, IN_WIDTH), bf16),
                   jax.ShapeDtypeStruct((1, D_MODEL), f32)],
        compiler_params=_params("arbitrary"),
    )(du, *dqs, *dks, *dvs, cos_t, sin_t, w_in, x, dx2, g1)


def _matmul_tn(a, b, tn, name):
    K, M = a.shape
    N = b.shape[1]
    tk = 512

    def body(a_ref, b_ref, o_ref):
        part = _dot_tn(a_ref[...], b_ref[...])

        @pl.when(pl.program_id(1) == 0)
        def _():
            o_ref[...] = part

        @pl.when(pl.program_id(1) > 0)
        def _():
            o_ref[...] += part

    return pl.pallas_call(
        body, name=name, grid=(N // tn, K // tk),
        in_specs=[pl.BlockSpec((tk, M), lambda n, k: (k, 0)), pl.BlockSpec((tk, tn), lambda n, k: (k, n))],
        out_specs=pl.BlockSpec((M, tn), lambda n, k: (0, n)),
        out_shape=jax.ShapeDtypeStruct((M, N), f32),
        compiler_params=_params("parallel", "arbitrary"),
    )(a, b)


def _rope_tables(S):
    half = HEAD_DIM // 2
    freqs = ROPE_THETA ** (-jnp.arange(half, dtype=f32) * (2.0 / HEAD_DIM))
    ang = jnp.arange(S).astype(f32)[:, None] * freqs[None, :]
    cos = jnp.tile(jnp.cos(ang), (1, 4))
    sin = jnp.sin(ang)
    sin = jnp.tile(jnp.concatenate([-sin, sin], axis=1), (1, 2))
    return cos, sin


def _block_diag(w_pool):
    w = jnp.zeros((POOL_WIDTH, POOL_WIDTH), w_pool.dtype)
    for g in range(POOL_WIDTH // POOL_GROUP):
        w = lax.dynamic_update_slice(w, w_pool[g], (g * POOL_GROUP, g * POOL_GROUP))
    return w


def _head_ones():
    head = np.arange(ATTN_WIDTH) // HEAD_DIM
    return jnp.asarray(head[:, None] == head[None, :], dtype=bf16)


def _local_grads(x, target, g1, w_pool, pool_scale, g2, g3, g4, w_in, w_out, w_gate, w_up, w_down):
    S = x.shape[0]
    cos_t, sin_t = _rope_tables(S)
    w_bd = _block_diag(w_pool).astype(bf16)

    h1, u, q, k, v = _in_proj(x, g1, w_in, cos_t, sin_t)
    pool_out = _pool_fwd(u, w_bd, pool_scale)
    branches = [_attn_fwd(q, k, v, d) for d in DILATIONS]
    attn_out, lse = _attn_merge([b[0] for b in branches], [b[1] for b in branches])
    mix, x2, h2 = _out_proj(pool_out, attn_out, w_out, x, g2, g3)
    gate, up, f = _ffn_fwd(h2, w_gate, w_up, w_down)
    dy, df, dg4, loss = _loss_head(f, x2, target, g4)

    a, dgate, dup, dh2 = _ffn_bwd(df, gate, up, w_gate, w_up, w_down)
    d_w_down = _matmul_tn(a, df, D_MODEL, "grad_w_down")
    d_w_gate = _matmul_tn(h2, dgate, FF_HALF, "grad_w_gate")
    d_w_up = _matmul_tn(h2, dup, FF_HALF, "grad_w_up")
    dx2, dmix, dg3, dg2 = _norm_bwd(dh2, dy, x2, mix, g3, g2)
    d_w_out = jnp.concatenate([_matmul_tn(pool_out, dmix, D_MODEL, "grad_w_out_pool"),
                               _matmul_tn(attn_out, dmix, D_MODEL, "grad_w_out_attn")], axis=0)
    dpool, do, delta = _out_proj_bwd(dmix, w_out, attn_out, _head_ones())
    du, d_w_bd, d_scale = _pool_bwd(u, dpool, w_bd, pool_scale)
    parts = [_attn_bwd(q, k, v, do, lse, delta, d) for d in DILATIONS]
    grad_x, dproj, dg1 = _in_proj_bwd(du, [p[0] for p in parts], [p[1] for p in parts], [p[2] for p in parts],
                                      cos_t, sin_t, w_in, x, dx2, g1)
    d_w_in = _matmul_tn(h1, dproj, IN_WIDTH // 2, "grad_w_in")
    d_w_pool = jnp.stack([d_w_bd[g * POOL_GROUP:(g + 1) * POOL_GROUP, g * POOL_GROUP:(g + 1) * POOL_GROUP]
                          for g in range(POOL_WIDTH // POOL_GROUP)])
    large = dict(w_in=d_w_in, w_out=d_w_out, w_gate=d_w_gate, w_up=d_w_up, w_down=d_w_down)
    small = dict(ln_pre_mix=dg1, ln_post_mix=dg2, ln_pre_ffn=dg3, ln_post_ffn=dg4, pool_scale=d_scale, w_pool=d_w_pool)
    return loss, grad_x, large, small


def _place():
    x, y, c = lax.axis_index("x"), lax.axis_index("y"), lax.axis_index("c")
    chips = [(1 - x, y), (x, 1 - y), (1 - x, 1 - y)]
    return x, y, c, chips


ANY = pl.BlockSpec(memory_space=pl.ANY)


def _row_chunks(rows, n, unit):
    units = rows // unit
    out, start = [], 0
    for i in range(n):
        size = (units // n + (1 if i < units % n else 0)) * unit
        out.append((start, size))
        start += size
    return out


GATHER_CHUNKS = _row_chunks(HALF_ROWS, 4, 32)
SWAP_CHUNKS = _row_chunks(HALF_ROWS, 8, 32)
JOIN_CHUNKS = _row_chunks(HALF_ROWS, 16, 32)
LOCAL_COPIES = 8


class _LocalCopy:
    def __init__(self, src_rows, dst_rows, rows, buf, sems_in, sems_out):
        self.loads, self.stores = [], []
        for i, (start, size) in enumerate(_row_chunks(rows, LOCAL_COPIES, 32)):
            r = pl.ds(start, size)
            self.loads.append(pltpu.make_async_copy(src_rows(r), buf.at[r], sems_in.at[i]))
            self.stores.append(pltpu.make_async_copy(buf.at[r], dst_rows(r), sems_out.at[i]))
        for cp in self.loads:
            cp.start()

    def finish(self):
        for load, store in zip(self.loads, self.stores):
            load.wait()
            store.start()
        for store in self.stores:
            store.wait()


def _local_scratch(rows, dtype):
    return [pltpu.VMEM((rows, D_MODEL), dtype), pltpu.SemaphoreType.DMA((LOCAL_COPIES,)),
            pltpu.SemaphoreType.DMA((LOCAL_COPIES,))]


def _gather_weights(pack):
    n_ch = len(GATHER_CHUNKS)

    def body(w_ref, out_ref, send1, recv1, send2, recv2, buf, sems_in, sems_out):
        x, y, c, chips = _place()
        me = 2 * x + y
        sibling = (x, y, 1 - c)
        own = _LocalCopy(lambda r: w_ref.at[r], lambda r: out_ref.at[me, r], PACK_ROWS, buf, sems_in, sems_out)

        def rows(core, ch):
            start, size = GATHER_CHUNKS[ch]
            return pl.ds(core * HALF_ROWS + start, size)

        def direct(j, ch, chip_xy, src_chip):
            cx, cy = chip_xy
            return pltpu.make_async_remote_copy(
                src_ref=w_ref.at[rows(c, ch)], dst_ref=out_ref.at[src_chip, rows(c, ch)],
                send_sem=send1.at[j * n_ch + ch], recv_sem=recv1.at[j * n_ch + ch],
                device_id=(cx, cy, c), device_id_type=MESH)

        def passed(j, ch, chip, core):
            return pltpu.make_async_remote_copy(
                src_ref=out_ref.at[chip, rows(core, ch)], dst_ref=out_ref.at[chip, rows(core, ch)],
                send_sem=send2.at[j * n_ch + ch], recv_sem=recv2.at[j * n_ch + ch],
                device_id=sibling, device_id_type=MESH)

        sends = [direct(j, ch, chip, me) for ch in range(n_ch) for j, chip in enumerate(chips)]
        for cp in sends:
            cp.start()
        own.finish()
        forwards = []
        for ch in range(n_ch):
            for j, (cx, cy) in enumerate(chips):
                direct(j, ch, (cx, cy), 2 * cx + cy).wait_recv()
                fw = passed(j, ch, 2 * cx + cy, c)
                fw.start()
                forwards.append(fw)
        for ch in range(n_ch):
            for j, (cx, cy) in enumerate(chips):
                passed(j, ch, 2 * cx + cy, 1 - c).wait_recv()
        for cp in sends + forwards:
            cp.wait_send()

    n_sem = 3 * n_ch
    return pl.pallas_call(
        body, name="gather_weights", in_specs=[ANY], out_specs=ANY,
        out_shape=jax.ShapeDtypeStruct((N_CHIPS, PACK_ROWS, D_MODEL), pack.dtype),
        scratch_shapes=[pltpu.SemaphoreType.DMA((n_sem,))] * 4 + _local_scratch(PACK_ROWS, pack.dtype),
        compiler_params=pltpu.CompilerParams(vmem_limit_bytes=VMEM_LIMIT_V7X),
    )(pack)


def _swap_halves(g):
    n_ch = len(SWAP_CHUNKS)

    def body(g_ref, theirs_ref, send, recv):
        x, y, c, _ = _place()

        def piece(s, ch, core):
            start, size = SWAP_CHUNKS[ch]
            return pltpu.make_async_remote_copy(
                src_ref=g_ref.at[s, pl.ds(core * HALF_ROWS + start, size)], dst_ref=theirs_ref.at[s, pl.ds(start, size)],
                send_sem=send.at[s * n_ch + ch], recv_sem=recv.at[s * n_ch + ch],
                device_id=(x, y, 1 - c), device_id_type=MESH)

        copies = [piece(s, ch, 1 - c) for s in range(N_CHIPS) for ch in range(n_ch)]
        for cp in copies:
            cp.start()
        for cp in copies:
            cp.wait()

    return pl.pallas_call(
        body, name="swap_halves", in_specs=[ANY], out_specs=ANY,
        out_shape=jax.ShapeDtypeStruct((N_CHIPS, HALF_ROWS, D_MODEL), g.dtype),
        scratch_shapes=[pltpu.SemaphoreType.DMA((N_CHIPS * n_ch,))] * 2,
    )(g)


ADD_ROWS = 376


def _add_cores(g, theirs):
    n_t = HALF_ROWS // ADD_ROWS

    def body(c_ref, g_ref, t_ref, o_ref):
        o_ref[...] = g_ref[...] + t_ref[...]

    blk = pl.BlockSpec((1, ADD_ROWS, D_MODEL), lambda s, t, c_ref: (s, t, 0))
    return pl.pallas_call(
        body, name="add_cores",
        grid_spec=pltpu.PrefetchScalarGridSpec(
            num_scalar_prefetch=1, grid=(N_CHIPS, n_t),
            in_specs=[pl.BlockSpec((1, ADD_ROWS, D_MODEL), lambda s, t, c_ref: (s, c_ref[0] * n_t + t, 0)), blk],
            out_specs=blk),
        out_shape=jax.ShapeDtypeStruct(theirs.shape, theirs.dtype),
        compiler_params=_params("parallel", "parallel"),
    )(lax.axis_index("c").astype(jnp.int32).reshape(1), g, theirs)


def _scatter_to_chips(h):
    n_ch = len(GATHER_CHUNKS)

    def body(h_ref, out_ref, send, recv, buf, sems_in, sems_out):
        x, y, c, chips = _place()
        me = 2 * x + y
        own = _LocalCopy(lambda r: h_ref.at[me, r], lambda r: out_ref.at[me, r], HALF_ROWS, buf, sems_in, sems_out)

        def piece(j, ch, chip_xy, dst_chip, src_chip):
            cx, cy = chip_xy
            start, size = GATHER_CHUNKS[ch]
            return pltpu.make_async_remote_copy(
                src_ref=h_ref.at[dst_chip, pl.ds(start, size)], dst_ref=out_ref.at[src_chip, pl.ds(start, size)],
                send_sem=send.at[j * n_ch + ch], recv_sem=recv.at[j * n_ch + ch],
                device_id=(cx, cy, c), device_id_type=MESH)

        sends = [piece(j, ch, (cx, cy), 2 * cx + cy, me) for ch in range(n_ch) for j, (cx, cy) in enumerate(chips)]
        for cp in sends:
            cp.start()
        own.finish()
        for ch in range(n_ch):
            for j, (cx, cy) in enumerate(chips):
                piece(j, ch, (cx, cy), me, 2 * cx + cy).wait_recv()
        for cp in sends:
            cp.wait_send()

    return pl.pallas_call(
        body, name="scatter_to_chips", in_specs=[ANY], out_specs=ANY,
        out_shape=jax.ShapeDtypeStruct(h.shape, h.dtype),
        scratch_shapes=[pltpu.SemaphoreType.DMA((3 * n_ch,))] * 2 + _local_scratch(HALF_ROWS, h.dtype),
        compiler_params=pltpu.CompilerParams(vmem_limit_bytes=VMEM_LIMIT_V7X),
    )(h)


def _join_halves(r):
    n_ch = len(JOIN_CHUNKS)

    def body(r_ref, out_ref, send, recv, buf, sems_in, sems_out):
        x, y, c, _ = _place()
        own = _LocalCopy(lambda r: r_ref.at[r], lambda r: out_ref.at[c, r], HALF_ROWS, buf, sems_in, sems_out)

        def piece(ch, core):
            start, size = JOIN_CHUNKS[ch]
            return pltpu.make_async_remote_copy(
                src_ref=r_ref.at[pl.ds(start, size)], dst_ref=out_ref.at[core, pl.ds(start, size)],
                send_sem=send.at[ch], recv_sem=recv.at[ch], device_id=(x, y, 1 - c), device_id_type=MESH)

        copies = [piece(ch, c) for ch in range(n_ch)]
        for cp in copies:
            cp.start()
        own.finish()
        for ch in range(n_ch):
            piece(ch, 1 - c).wait_recv()
        for cp in copies:
            cp.wait_send()

    return pl.pallas_call(
        body, name="join_halves", in_specs=[ANY], out_specs=ANY,
        out_shape=jax.ShapeDtypeStruct((2,) + r.shape, r.dtype),
        scratch_shapes=[pltpu.SemaphoreType.DMA((n_ch,))] * 2 + _local_scratch(HALF_ROWS, r.dtype),
        compiler_params=pltpu.CompilerParams(vmem_limit_bytes=VMEM_LIMIT_V7X),
    )(r)


def _add_slabs(terms, k, name):
    rows = terms[0][0].shape[1]
    tr = ADD_ROWS
    n = len(terms)

    def body(*refs):
        acc = refs[0][...]
        for r in refs[1:n]:
            acc = acc + r[...]
        refs[n][...] = acc

    slab = lambda first: pl.BlockSpec((1, tr, D_MODEL), lambda i, t: (first + i, t, 0))
    return pl.pallas_call(
        body, name=name, grid=(k, rows // tr), in_specs=[slab(first) for _, first in terms],
        out_specs=pl.BlockSpec((1, tr, D_MODEL), lambda i, t: (i, t, 0)),
        out_shape=jax.ShapeDtypeStruct((k, rows, D_MODEL), terms[0][0].dtype),
        compiler_params=_params("parallel", "parallel"),
    )(*[a for a, _ in terms])


def _sum_small(block):
    def body(b_ref, out_ref, gathered, send, recv):
        x, y, c, _ = _place()
        me = 4 * x + 2 * y + c
        gathered[me] = b_ref[...]
        sends = []
        for kk in range(1, N_DEV):
            flip = lambda v, bit: 1 - v if bit else v
            peer = (flip(x, kk & 4), flip(y, kk & 2), flip(c, kk & 1))
            cp = pltpu.make_async_remote_copy(
                src_ref=b_ref, dst_ref=gathered.at[me], send_sem=send.at[kk - 1], recv_sem=recv.at[kk - 1],
                device_id=peer, device_id_type=MESH)
            cp.start()
            sends.append(cp)
        for kk in range(1, N_DEV):
            peer_index = jnp.bitwise_xor(me, kk)
            pltpu.make_async_remote_copy(
                src_ref=b_ref, dst_ref=gathered.at[peer_index], send_sem=send.at[kk - 1], recv_sem=recv.at[kk - 1],
                device_id=(x, y, c), device_id_type=MESH).wait_recv()
        for cp in sends:
            cp.wait_send()
        acc = gathered[0]
        for dev in range(1, N_DEV):
            acc = acc + gathered[dev]
        out_ref[...] = acc

    vmem = pl.BlockSpec(memory_space=pltpu.VMEM)
    return pl.pallas_call(
        body, name="sum_small", in_specs=[vmem], out_specs=vmem,
        out_shape=jax.ShapeDtypeStruct(block.shape, block.dtype),
        scratch_shapes=[pltpu.VMEM((N_DEV,) + block.shape, block.dtype),
                        pltpu.SemaphoreType.DMA((N_DEV - 1,)), pltpu.SemaphoreType.DMA((N_DEV - 1,))],
    )(block)


def _adamw(w, g, m, v, name):
    rows, cols = w.shape
    tr = rows
    for cand in (512, 256, 128, 64, 32, 16, 8):
        if rows % cand == 0:
            tr = cand
            break
    c1 = 1.0 - ADAM_B1 ** ADAM_STEP
    c2 = 1.0 - ADAM_B2 ** ADAM_STEP

    def body(w_ref, g_ref, m_ref, v_ref, d_ref, nm_ref, nv_ref):
        gv = g_ref[...]
        nm = ADAM_B1 * m_ref[...] + (1.0 - ADAM_B1) * gv
        nv = ADAM_B2 * v_ref[...] + (1.0 - ADAM_B2) * (gv * gv)
        nm_ref[...] = nm
        nv_ref[...] = nv
        d_ref[...] = -ADAM_LR * ((nm / c1) / (jnp.sqrt(nv / c2) + ADAM_EPS) + ADAM_WD * w_ref[...])

    blk = pl.BlockSpec((tr, cols), lambda i: (i, 0))
    shape = jax.ShapeDtypeStruct((rows, cols), f32)
    return pl.pallas_call(
        body, name=name, grid=(rows // tr,), in_specs=[blk] * 4, out_specs=[blk] * 3, out_shape=[shape] * 3,
        compiler_params=_params("parallel"),
    )(w, g, m, v)


LARGE = ("w_in", "w_out", "w_gate", "w_up", "w_down")
SMALL = ("ln_pre_mix", "ln_post_mix", "ln_pre_ffn", "ln_post_ffn", "pool_scale", "w_pool")
COLUMN_SHARDED = {"w_in": IN_WIDTH // N_CHIPS, "w_gate": D_FF // N_CHIPS, "w_up": D_FF // N_CHIPS}


def _pack_shard(shards):
    return jnp.concatenate([shards[n].reshape(-1, D_MODEL) for n in LARGE], axis=0)


def _unpack_shard(pack, shapes):
    out, row = {}, 0
    for n, rows in zip(LARGE, PACK_SPLITS):
        out[n] = pack[row:row + rows].reshape(shapes[n])
        row += rows
    return out


def _whole_from_shards(packs):
    out, row = {}, 0
    for n, rows in zip(LARGE, PACK_SPLITS):
        part = packs[:, row:row + rows]
        if n in COLUMN_SHARDED:
            width = COLUMN_SHARDED[n]
            part = part.reshape(N_CHIPS, D_MODEL, width).transpose(1, 0, 2).reshape(D_MODEL, N_CHIPS * width)
        else:
            part = part.reshape(N_CHIPS * rows, D_MODEL)
        out[n] = part
        row += rows
    return out


def _shards_from_whole(grads):
    parts = []
    for n, rows in zip(LARGE, PACK_SPLITS):
        g = grads[n]
        if n in COLUMN_SHARDED:
            width = COLUMN_SHARDED[n]
            g = g.reshape(D_MODEL, N_CHIPS, width).transpose(1, 0, 2)
        parts.append(g.reshape(N_CHIPS, rows, D_MODEL))
    return jnp.concatenate(parts, axis=1)


def _pack_small(vals):
    rows = [vals[n].reshape(1, D_MODEL) for n in SMALL[:4]]
    rows.append(jnp.pad(vals["pool_scale"].reshape(1, POOL_WIDTH), ((0, 0), (0, D_MODEL - POOL_WIDTH))))
    rows.append(jnp.pad(vals["loss"].reshape(1, 1), ((0, 0), (0, D_MODEL - 1))))
    rows.append(jnp.zeros((2, D_MODEL), f32))
    rows.append(vals["w_pool"].reshape(16, D_MODEL))
    return jnp.concatenate(rows, axis=0)


def _unpack_small(block):
    out = {n: block[i:i + 1] for i, n in enumerate(SMALL[:4])}
    out["pool_scale"] = block[4:5, :POOL_WIDTH]
    out["loss"] = block[5, 0]
    out["w_pool"] = block[8:24].reshape(1, 4, POOL_GROUP, POOL_GROUP)
    return out


def kernel(x, ln_pre_mix, w_in, w_pool, pool_scale, w_out, ln_post_mix, ln_pre_ffn, w_gate, w_up, w_down, ln_post_ffn, loss_target, m_ln_pre_mix, m_w_in, m_w_pool, m_pool_scale, m_w_out, m_ln_post_mix, m_ln_pre_ffn, m_w_gate, m_w_up, m_w_down, m_ln_post_ffn, v_ln_pre_mix, v_w_in, v_w_pool, v_pool_scale, v_w_out, v_ln_post_mix, v_ln_pre_ffn, v_w_gate, v_w_up, v_w_down, v_ln_post_ffn):
    w = dict(ln_pre_mix=ln_pre_mix, w_in=w_in, w_pool=w_pool, pool_scale=pool_scale, w_out=w_out,
             ln_post_mix=ln_post_mix, ln_pre_ffn=ln_pre_ffn, w_gate=w_gate, w_up=w_up, w_down=w_down,
             ln_post_ffn=ln_post_ffn)
    m = dict(ln_pre_mix=m_ln_pre_mix, w_in=m_w_in, w_pool=m_w_pool, pool_scale=m_pool_scale, w_out=m_w_out,
             ln_post_mix=m_ln_post_mix, ln_pre_ffn=m_ln_pre_ffn, w_gate=m_w_gate, w_up=m_w_up, w_down=m_w_down,
             ln_post_ffn=m_ln_post_ffn)
    v = dict(ln_pre_mix=v_ln_pre_mix, w_in=v_w_in, w_pool=v_w_pool, pool_scale=v_pool_scale, w_out=v_w_out,
             ln_post_mix=v_ln_post_mix, ln_pre_ffn=v_ln_pre_ffn, w_gate=v_w_gate, w_up=v_w_up, w_down=v_w_down,
             ln_post_ffn=v_ln_post_ffn)

    packs = _gather_weights(_pack_shard({n: w[n][0].astype(bf16) for n in LARGE}))
    whole = _whole_from_shards(packs)

    loss, grad_x, large, small = _local_grads(
        x[0], loss_target[0], ln_pre_mix, w_pool[0], pool_scale, ln_post_mix, ln_pre_ffn, ln_post_ffn,
        whole["w_in"], whole["w_out"], whole["w_gate"], whole["w_up"], whole["w_down"])

    shard_major = _shards_from_whole(large)
    chip_sum = _add_cores(shard_major, _swap_halves(shard_major))
    pieces = _scatter_to_chips(chip_sum)
    reduced_half = _add_slabs([(pieces, j) for j in range(N_CHIPS)], 1, "add_chips")[0]
    reduced = _join_halves(reduced_half).reshape(PACK_ROWS, D_MODEL)
    shapes = {n: w[n].shape[1:] for n in LARGE}
    grads = _unpack_shard(reduced, shapes)

    total = _unpack_small(_sum_small(_pack_small(dict(small, loss=loss))))
    for n in SMALL:
        grads[n] = total[n]

    delta, new_m, new_v = {}, {}, {}
    for n in LARGE:
        delta[n], new_m[n], new_v[n] = _adamw(w[n][0], grads[n], m[n][0], v[n][0], "adamw_" + n)
    small_state = [_pack_small(dict({n: s[n] for n in SMALL}, loss=jnp.zeros((), f32))) for s in (w, m, v)]
    small_grad = _pack_small(dict({n: grads[n] for n in SMALL}, loss=jnp.zeros((), f32)))
    sd, sm, sv = _adamw(small_state[0], small_grad, small_state[1], small_state[2], "adamw_small")
    for out, block in ((delta, sd), (new_m, sm), (new_v, sv)):
        un = _unpack_small(block)
        for n in SMALL:
            out[n] = un[n]

    names = ("ln_pre_mix", "w_in", "w_pool", "pool_scale", "w_out", "ln_post_mix", "ln_pre_ffn", "w_gate", "w_up",
             "w_down", "ln_post_ffn")
    full = lambda d: [d[n].reshape(w[n].shape) for n in names]
    return (total["loss"], grad_x[None], *full(grads), *full(delta), *full(new_m), *full(new_v))
```

```python
import numpy as np
import jax
import jax.numpy as jnp
from jax import lax
from jax.experimental import pallas as pl
from jax.experimental.pallas import tpu as pltpu

D_MODEL = 1024
POOL_WIDTH = 256
POOL_GROUP = 64
ATTN_WIDTH = 768
HEAD_DIM = 64
IN_WIDTH = 2560
D_FF = 2816
BLOCK = 128
DILATIONS = (1, 4, 16)
ROPE_THETA = 10000.0
EPS = 1e-6
ATTN_SCALE = 0.125
NEG = -1e30

ADAM_LR = 0.001
ADAM_B1 = 0.9
ADAM_B2 = 0.999
ADAM_EPS = 1e-08
ADAM_WD = 0.01
ADAM_STEP = 10

N_CHIPS = 4
N_DEV = 8
PACK_SPLITS = (640, 256, 704, 704, 704)
PACK_ROWS = sum(PACK_SPLITS)
HALF_ROWS = PACK_ROWS // 2
SMALL_ROWS = 24

VMEM_LIMIT_V7X = 56 * 1024 * 1024
MESH = pl.DeviceIdType.MESH

f32 = jnp.float32
bf16 = jnp.bfloat16


def _params(*sem):
    return pltpu.CompilerParams(dimension_semantics=sem, vmem_limit_bytes=VMEM_LIMIT_V7X)


def _dot(a, b):
    return jnp.dot(a, b, preferred_element_type=f32)


def _dot_nt(a, b):
    return lax.dot_general(a, b, (((1,), (1,)), ((), ())), preferred_element_type=f32)


def _dot_tn(a, b):
    return lax.dot_general(a, b, (((0,), (0,)), ((), ())), preferred_element_type=f32)


def _rope_partner(a, first_half):
    return jnp.where(first_half, pltpu.roll(a, 96, 1), pltpu.roll(a, 32, 1))


def _first_half_mask(rows):
    lane = lax.broadcasted_iota(jnp.int32, (rows, 128), 1)
    return (lane % HEAD_DIM) < (HEAD_DIM // 2)


def _stream_spec(d, ts):
    return pl.BlockSpec((d, ts // d, ATTN_WIDTH), lambda i: (0, i, 0))


def _stream_shape(S, d):
    return jax.ShapeDtypeStruct((d, S // d, ATTN_WIDTH), bf16)


N_STAGE = ATTN_WIDTH // 128


def _stage_scratch(ts):
    return [pltpu.VMEM((ts, 128), f32)] * N_STAGE


def _store_streams(stage, out_refs, ts):
    for d, ref in zip(DILATIONS, out_refs):
        for r in range(d):
            rows = pl.ds(0, ts) if d == 1 else pl.ds(r, ts // d, stride=d)
            for j in range(N_STAGE):
                ref[r, :, j * 128:(j + 1) * 128] = stage[j][rows, :].astype(bf16)


def _in_proj(x, g1, w_in, cos_t, sin_t):
    S = x.shape[0]
    ts = 512

    def body(x_ref, g_ref, w_ref, cos_ref, sin_ref, h_ref, u_ref, *rest):
        outs, stage = rest[:-N_STAGE], rest[-N_STAGE:]
        xv = x_ref[...]
        r = lax.rsqrt(jnp.mean(xv * xv, axis=-1, keepdims=True) + EPS)
        h = ((xv * r) * g_ref[...]).astype(bf16)
        h_ref[...] = h
        proj = _dot(h, w_ref[...])
        u_ref[...] = proj[:, :POOL_WIDTH]
        cos = cos_ref[...]
        sin = sin_ref[...]
        first = _first_half_mask(ts)
        n_dil = len(DILATIONS)
        for which, base in enumerate((POOL_WIDTH, POOL_WIDTH + ATTN_WIDTH)):
            for j in range(ATTN_WIDTH // 128):
                a = proj[:, base + j * 128: base + (j + 1) * 128]
                stage[j][...] = a * cos + _rope_partner(a, first) * sin
            _store_streams(stage, outs[which * n_dil:(which + 1) * n_dil], ts)
        for j in range(ATTN_WIDTH // 128):
            base = POOL_WIDTH + 2 * ATTN_WIDTH + j * 128
            stage[j][...] = proj[:, base:base + 128]
        _store_streams(stage, outs[2 * n_dil:], ts)

    row = lambda w: pl.BlockSpec((ts, w), lambda i: (i, 0))
    streams = [_stream_spec(d, ts) for d in DILATIONS] * 3
    res = pl.pallas_call(
        body, name="in_proj", grid=(S // ts,),
        in_specs=[row(D_MODEL), pl.BlockSpec((1, D_MODEL), lambda i: (0, 0)),
                  pl.BlockSpec((D_MODEL, IN_WIDTH), lambda i: (0, 0)), row(128), row(128)],
        out_specs=[row(D_MODEL), row(POOL_WIDTH)] + streams,
        out_shape=[jax.ShapeDtypeStruct((S, D_MODEL), bf16), jax.ShapeDtypeStruct((S, POOL_WIDTH), f32)]
        + [_stream_shape(S, d) for d in DILATIONS] * 3,
        scratch_shapes=_stage_scratch(ts),
        compiler_params=_params("parallel"),
    )(x, g1, w_in, cos_t, sin_t)
    n = len(DILATIONS)
    return res[0], res[1], res[2:2 + n], res[2 + n:2 + 2 * n], res[2 + 2 * n:]


POOL_HALO = 16


def _pool_lane_group(rows):
    return lax.broadcasted_iota(jnp.int32, (rows, POOL_WIDTH), 1) // POOL_GROUP


def _pool_select(group, s2, s4, s8, s16):
    return jnp.where(group == 0, s2, jnp.where(group == 1, s4, jnp.where(group == 2, s8, s16)))


def _pool_count(t0, rows):
    group = _pool_lane_group(rows)
    t = t0 + lax.broadcasted_iota(jnp.int32, (rows, POOL_WIDTH), 0)
    win = _pool_select(group, 2, 4, 8, 16)
    return jnp.minimum(t + 1, win).astype(f32)


def _pool_diff(u_halo, u_tile, t0):
    ts = u_tile.shape[0]
    ext = jnp.concatenate([u_halo, u_tile], axis=0)
    s2 = ext + pltpu.roll(ext, 1, 0)
    s4 = s2 + pltpu.roll(s2, 2, 0)
    s8 = s4 + pltpu.roll(s4, 4, 0)
    s16 = s8 + pltpu.roll(s8, 8, 0)
    group = _pool_lane_group(ts + POOL_HALO)
    wsum = _pool_select(group, s2, s4, s8, s16)[POOL_HALO:]
    return wsum / _pool_count(t0, ts) - u_tile


def _pool_specs(ts, n_tiles):
    tile = pl.BlockSpec((ts, POOL_WIDTH), lambda i: (i, 0))
    per = ts // POOL_HALO
    before = pl.BlockSpec((POOL_HALO, POOL_WIDTH), lambda i: (jnp.maximum(i * per - 1, 0), 0))
    after = pl.BlockSpec((POOL_HALO, POOL_WIDTH), lambda i: (jnp.minimum((i + 1) * per, n_tiles * per - 1), 0))
    return tile, before, after


def _pool_fwd(u, w_bd, scale):
    S = u.shape[0]
    ts = 512
    n_tiles = S // ts

    def body(u_ref, halo_ref, w_ref, sc_ref, y_ref):
        i = pl.program_id(0)
        halo = jnp.where(i > 0, halo_ref[...], 0.0)
        d = _pool_diff(halo, u_ref[...], i * ts)
        y_ref[...] = (_dot(d.astype(bf16), w_ref[...]) * sc_ref[...]).astype(bf16)

    tile, before, _ = _pool_specs(ts, n_tiles)
    return pl.pallas_call(
        body, name="pool_fwd", grid=(n_tiles,),
        in_specs=[tile, before, pl.BlockSpec((POOL_WIDTH, POOL_WIDTH), lambda i: (0, 0)),
                  pl.BlockSpec((1, POOL_WIDTH), lambda i: (0, 0))],
        out_specs=tile, out_shape=jax.ShapeDtypeStruct((S, POOL_WIDTH), bf16),
        compiler_params=_params("parallel"),
    )(u, u, w_bd, scale)


def _pool_bwd(u, dy, w_bd, scale):
    S = u.shape[0]
    ts = 512
    n_tiles = S // ts

    def body(u_ref, halo_ref, dy_ref, dy_next_ref, w_ref, sc_ref, du_ref, dw_ref, dsc_ref):
        i = pl.program_id(0)

        @pl.when(i == 0)
        def _():
            dw_ref[...] = jnp.zeros_like(dw_ref)
            dsc_ref[...] = jnp.zeros_like(dsc_ref)

        halo = jnp.where(i > 0, halo_ref[...], 0.0)
        d = _pool_diff(halo, u_ref[...], i * ts).astype(bf16)
        w = w_ref[...]
        sc = sc_ref[...]
        dy_tile = dy_ref[...]
        z = _dot(d, w)
        dsc_ref[...] += jnp.sum(dy_tile * z, axis=0, keepdims=True)
        dy_next = jnp.where(i < n_tiles - 1, dy_next_ref[...], 0.0)
        dz = (jnp.concatenate([dy_tile, dy_next], axis=0) * sc).astype(bf16)
        dw_ref[...] += _dot_tn(d, dz[:ts])
        dd = _dot_nt(dz, w)
        e = dd / _pool_count(i * ts, ts + POOL_HALO)
        n = ts + POOL_HALO
        f2 = e + pltpu.roll(e, n - 1, 0)
        f4 = f2 + pltpu.roll(f2, n - 2, 0)
        f8 = f4 + pltpu.roll(f4, n - 4, 0)
        f16 = f8 + pltpu.roll(f8, n - 8, 0)
        fsum = _pool_select(_pool_lane_group(n), f2, f4, f8, f16)
        du_ref[...] = (fsum[:ts] - dd[:ts]).astype(bf16)

    tile, before, after = _pool_specs(ts, n_tiles)
    return pl.pallas_call(
        body, name="pool_bwd", grid=(n_tiles,),
        in_specs=[tile, before, tile, after, pl.BlockSpec((POOL_WIDTH, POOL_WIDTH), lambda i: (0, 0)),
                  pl.BlockSpec((1, POOL_WIDTH), lambda i: (0, 0))],
        out_specs=[tile, pl.BlockSpec((POOL_WIDTH, POOL_WIDTH), lambda i: (0, 0)),
                   pl.BlockSpec((1, POOL_WIDTH), lambda i: (0, 0))],
        out_shape=[jax.ShapeDtypeStruct((S, POOL_WIDTH), bf16), jax.ShapeDtypeStruct((POOL_WIDTH, POOL_WIDTH), f32),
                   jax.ShapeDtypeStruct((1, POOL_WIDTH), f32)],
        compiler_params=_params("arbitrary"),
    )(u, u, dy, dy, w_bd, scale)


SUPER = BLOCK * DILATIONS[-1]
UNITS = SUPER // BLOCK


def _band_mask(has_prev):
    qi = lax.broadcasted_iota(jnp.int32, (BLOCK, 2 * BLOCK), 0)
    kj = lax.broadcasted_iota(jnp.int32, (BLOCK, 2 * BLOCK), 1)
    return (kj >= qi) & (kj <= qi + BLOCK) & ((kj >= BLOCK) | has_prev)


def _head0_mask(rows=BLOCK):
    return lax.broadcasted_iota(jnp.int32, (rows, 128), 1) < HEAD_DIM


def _per_head(stat, h0, h):
    other = pltpu.roll(stat, HEAD_DIM, 1)
    full = jnp.where(h0, stat, other) if h == 0 else jnp.where(h0, other, stat)
    return jnp.concatenate([full, full], axis=1)


def _natural_rows(d, r, n):
    if d == 1:
        return pl.ds(pl.multiple_of(n * BLOCK, BLOCK), BLOCK)
    return pl.ds(n * (BLOCK * d) + r, BLOCK, stride=d)


def _unit_place(d, u):
    per_stream = UNITS // d
    return u // per_stream, u % per_stream, per_stream


def _block_rows(n):
    return pl.ds(pl.multiple_of(n * BLOCK, BLOCK), BLOCK)


def _band(cur_ref, tail_ref, r, n):
    before = jnp.where(n > 0, cur_ref[r, _block_rows(jnp.maximum(n - 1, 0)), :], tail_ref[r])
    return jnp.concatenate([before, cur_ref[r, _block_rows(n), :]], axis=0)


def _attn_in_specs(S, with_do):
    specs = []
    last = S // SUPER - 1
    for d in DILATIONS:
        per_stream = UNITS // d
        cur = pl.BlockSpec((d, SUPER // d, 128), lambda hp, sb: (0, jnp.minimum(sb, last), hp))
        tail = pl.BlockSpec(
            (d, BLOCK, 128),
            lambda hp, sb, per_stream=per_stream: (0, jnp.maximum(jnp.minimum(sb, last) * per_stream - 1, 0), hp))
        specs += [cur] * (2 if with_do else 1) + [cur, tail, cur, tail]
    return specs


def _attn_fwd(qs, ks, vs):
    S = qs[0].shape[1]
    n_dil = len(DILATIONS)

    def body(*refs):
        ins, (out_ref, lse_ref) = refs[:-2 - 2 * n_dil], refs[-2 - 2 * n_dil:-2 * n_dil]
        o_sc, l_sc = refs[-2 * n_dil:-n_dil], refs[-n_dil:]
        sb = pl.program_id(1)
        h0 = _head0_mask()
        for ci, d in enumerate(DILATIONS):
            q_ref, kc_ref, kp_ref, vc_ref, vp_ref = ins[5 * ci:5 * ci + 5]

            def unit(u, carry, d=d, ci=ci, q_ref=q_ref, kc_ref=kc_ref, kp_ref=kp_ref, vc_ref=vc_ref, vp_ref=vp_ref):
                r, n, _ = _unit_place(d, u)
                qv = q_ref[r, _block_rows(n), :]
                kb = _band(kc_ref, kp_ref, r, n)
                vb = _band(vc_ref, vp_ref, r, n)
                valid = _band_mask((sb > 0) | (n > 0))
                outs, lses = [], []
                for h in range(2):
                    keep = h0 if h == 0 else jnp.logical_not(h0)
                    qh = jnp.where(keep, qv, jnp.zeros_like(qv))
                    s = jnp.where(valid, _dot_nt(qh, kb) * ATTN_SCALE, NEG)
                    m = jnp.max(s, axis=1, keepdims=True)
                    e = jnp.exp(s - m)
                    den = jnp.sum(e, axis=1, keepdims=True)
                    outs.append(_dot((e / den).astype(bf16), vb))
                    lses.append(jnp.broadcast_to(m + jnp.log(den), (BLOCK, 128)))
                rows = _natural_rows(d, r, n)
                o_sc[ci][rows, :] = jnp.where(h0, outs[0], outs[1])
                l_sc[ci][rows, :] = jnp.where(h0, lses[0], lses[1])
                return carry

            lax.fori_loop(0, UNITS, unit, 0)

        def merge(t, carry):
            rows = pl.ds(pl.multiple_of(t * 256, 256), 256)
            a, b, c = l_sc[0][rows, :], l_sc[1][rows, :], l_sc[2][rows, :]
            m = jnp.maximum(jnp.maximum(a, b), c)
            ea, eb, ec = jnp.exp(a - m), jnp.exp(b - m), jnp.exp(c - m)
            tot = ea + eb + ec
            out_ref[rows, :] = ((ea / tot) * o_sc[0][rows, :] + (eb / tot) * o_sc[1][rows, :]
                                + (ec / tot) * o_sc[2][rows, :]).astype(bf16)
            lse_ref[rows, :] = m + jnp.log(tot)
            return carry

        lax.fori_loop(0, SUPER // 256, merge, 0)

    args = []
    for q, k, v in zip(qs, ks, vs):
        args += [q, k, k, v, v]
    nat = pl.BlockSpec((SUPER, 128), lambda hp, sb: (sb, hp))
    return pl.pallas_call(
        body, name="attn_fwd", grid=(ATTN_WIDTH // 128, S // SUPER),
        in_specs=_attn_in_specs(S, False), out_specs=[nat, nat],
        out_shape=[jax.ShapeDtypeStruct((S, ATTN_WIDTH), bf16), jax.ShapeDtypeStruct((S, ATTN_WIDTH), f32)],
        scratch_shapes=[pltpu.VMEM((SUPER, 128), f32)] * (2 * n_dil),
        compiler_params=_params("parallel", "arbitrary"),
    )(*args)


def _attn_bwd(qs, ks, vs, dos, lse, delta):
    S = qs[0].shape[1]
    n_steps = S // SUPER
    last = n_steps - 1

    def body(*refs):
        ins, (lse_ref, dl_ref, dq_ref, dk_ref, dv_ref, dq_acc, dk_acc, dv_acc) = refs[:-8], refs[-8:]
        sb = pl.program_id(1)
        cur = sb % 2
        prv = 1 - cur

        @pl.when(sb < n_steps)
        def _():
            dq_acc[...] = jnp.zeros_like(dq_acc)
            dk_acc[cur] = jnp.zeros((SUPER, 128), f32)
            dv_acc[cur] = jnp.zeros((SUPER, 128), f32)
            h0 = _head0_mask()
            h0_band = _head0_mask(2 * BLOCK)
            for ci, d in enumerate(DILATIONS):
                q_ref, do_ref, kc_ref, kp_ref, vc_ref, vp_ref = ins[6 * ci:6 * ci + 6]

                def unit(u, carry, d=d, q_ref=q_ref, do_ref=do_ref, kc_ref=kc_ref, kp_ref=kp_ref, vc_ref=vc_ref,
                         vp_ref=vp_ref):
                    r, n, per_stream = _unit_place(d, u)
                    qv = q_ref[r, _block_rows(n), :]
                    dov = do_ref[r, _block_rows(n), :]
                    kb = _band(kc_ref, kp_ref, r, n)
                    vb = _band(vc_ref, vp_ref, r, n)
                    rows = _natural_rows(d, r, n)
                    lse_v = lse_ref[rows, :]
                    dl_v = dl_ref[rows, :]
                    has_prev = (sb > 0) | (n > 0)
                    valid = _band_mask(has_prev)
                    dqs, dks, dvs = [], [], []
                    for h in range(2):
                        keep = h0 if h == 0 else jnp.logical_not(h0)
                        qh = jnp.where(keep, qv, jnp.zeros_like(qv))
                        doh = jnp.where(keep, dov, jnp.zeros_like(dov))
                        s = jnp.where(valid, _dot_nt(qh, kb) * ATTN_SCALE, NEG)
                        p = jnp.exp(s - _per_head(lse_v, h0, h))
                        dp = _dot_nt(doh, vb)
                        ds = (p * (dp - _per_head(dl_v, h0, h)) * ATTN_SCALE).astype(bf16)
                        dqs.append(_dot(ds, kb))
                        dks.append(_dot_tn(ds, qv))
                        dvs.append(_dot_tn(p.astype(bf16), dov))
                    dq_acc[rows, :] += jnp.where(h0, dqs[0], dqs[1])
                    dkb = jnp.where(h0_band, dks[0], dks[1])
                    dvb = jnp.where(h0_band, dvs[0], dvs[1])
                    dk_acc[cur, rows, :] += dkb[BLOCK:]
                    dv_acc[cur, rows, :] += dvb[BLOCK:]

                    @pl.when(has_prev)
                    def _():
                        slot = jnp.where(n > 0, cur, prv)
                        before = _natural_rows(d, r, jnp.where(n > 0, n - 1, per_stream - 1))
                        dk_acc[slot, before, :] += dkb[:BLOCK]
                        dv_acc[slot, before, :] += dvb[:BLOCK]

                    return carry

                lax.fori_loop(0, UNITS, unit, 0)
            dq_ref[...] = dq_acc[...].astype(bf16)

        @pl.when(sb > 0)
        def _():
            dk_ref[...] = dk_acc[prv].astype(bf16)
            dv_ref[...] = dv_acc[prv].astype(bf16)

    args = []
    for q, k, v, do in zip(qs, ks, vs, dos):
        args += [q, do, k, k, v, v]
    nat = pl.BlockSpec((SUPER, 128), lambda hp, sb: (jnp.minimum(sb, last), hp))
    nat_before = pl.BlockSpec((SUPER, 128), lambda hp, sb: (jnp.clip(sb - 1, 0, last), hp))
    out = jax.ShapeDtypeStruct((S, ATTN_WIDTH), bf16)
    return pl.pallas_call(
        body, name="attn_bwd", grid=(ATTN_WIDTH // 128, n_steps + 1),
        in_specs=_attn_in_specs(S, True) + [nat, nat], out_specs=[nat, nat_before, nat_before],
        out_shape=[out, out, out],
        scratch_shapes=[pltpu.VMEM((SUPER, 128), f32), pltpu.VMEM((2, SUPER, 128), f32),
                        pltpu.VMEM((2, SUPER, 128), f32)],
        compiler_params=_params("parallel", "arbitrary"),
    )(*args, lse, delta)


def _rms(v):
    return lax.rsqrt(jnp.mean(v * v, axis=-1, keepdims=True) + EPS)


def _out_proj(pool_out, attn_out, w_out, x, g2, g3):
    S = x.shape[0]
    ts = 512

    def body(p_ref, a_ref, w_ref, x_ref, g2_ref, g3_ref, mix_ref, x2_ref, h2_ref):
        mix = _dot(p_ref[...], w_ref[:POOL_WIDTH, :]) + _dot(a_ref[...], w_ref[POOL_WIDTH:, :])
        mix_ref[...] = mix
        x2 = x_ref[...] + (mix * _rms(mix)) * g2_ref[...]
        x2_ref[...] = x2
        h2_ref[...] = ((x2 * _rms(x2)) * g3_ref[...]).astype(bf16)

    row = lambda w: pl.BlockSpec((ts, w), lambda i: (i, 0))
    gain = pl.BlockSpec((1, D_MODEL), lambda i: (0, 0))
    return pl.pallas_call(
        body, name="out_proj", grid=(S // ts,),
        in_specs=[row(POOL_WIDTH), row(ATTN_WIDTH), pl.BlockSpec((D_MODEL, D_MODEL), lambda i: (0, 0)),
                  row(D_MODEL), gain, gain],
        out_specs=[row(D_MODEL)] * 3,
        out_shape=[jax.ShapeDtypeStruct((S, D_MODEL), f32), jax.ShapeDtypeStruct((S, D_MODEL), f32),
                   jax.ShapeDtypeStruct((S, D_MODEL), bf16)],
        compiler_params=_params("parallel"),
    )(pool_out, attn_out, w_out, x, g2, g3)


FF_TILE = 256
FF_HALF = D_FF // 2


def _sigmoid(g):
    return 1.0 / (1.0 + jnp.exp(-g))


def _ffn_fwd(h2, w_gate, w_up, w_down):
    S = h2.shape[0]
    ts = 1024

    def body(h_ref, wg_ref, wu_ref, wd_ref, gate_ref, up_ref, f_ref):
        j = pl.program_id(1)
        h = h_ref[...]
        gate = _dot(h, wg_ref[...])
        up = _dot(h, wu_ref[...])
        gate_ref[...] = gate.astype(bf16)
        up_ref[...] = up.astype(bf16)
        part = _dot((gate * _sigmoid(gate) * up).astype(bf16), wd_ref[...])

        @pl.when(j == 0)
        def _():
            f_ref[...] = part

        @pl.when(j > 0)
        def _():
            f_ref[...] += part

    act = pl.BlockSpec((ts, FF_TILE), lambda i, j: (i, j))
    return pl.pallas_call(
        body, name="ffn_fwd", grid=(S // ts, D_FF // FF_TILE),
        in_specs=[pl.BlockSpec((ts, D_MODEL), lambda i, j: (i, 0)),
                  pl.BlockSpec((D_MODEL, FF_TILE), lambda i, j: (0, j)),
                  pl.BlockSpec((D_MODEL, FF_TILE), lambda i, j: (0, j)),
                  pl.BlockSpec((FF_TILE, D_MODEL), lambda i, j: (j, 0))],
        out_specs=[act, act, pl.BlockSpec((ts, D_MODEL), lambda i, j: (i, 0))],
        out_shape=[jax.ShapeDtypeStruct((S, D_FF), bf16), jax.ShapeDtypeStruct((S, D_FF), bf16),
                   jax.ShapeDtypeStruct((S, D_MODEL), f32)],
        compiler_params=_params("parallel", "arbitrary"),
    )(h2, w_gate, w_up, w_down)


def _loss_head(f, x2, target, g4):
    S = f.shape[0]
    ts = 512

    def body(f_ref, x2_ref, t_ref, g_ref, dy_ref, df_ref, dg_ref, loss_ref):
        @pl.when(pl.program_id(0) == 0)
        def _():
            dg_ref[...] = jnp.zeros_like(dg_ref)
            loss_ref[...] = jnp.zeros_like(loss_ref)

        fv = f_ref[...]
        g = g_ref[...]
        r = _rms(fv)
        fhat = fv * r
        err = (x2_ref[...] + fhat * g) - t_ref[...]
        loss_ref[...] += 0.5 * jnp.sum(jnp.mean(err * err, axis=-1, keepdims=True), axis=0, keepdims=True)
        dy = err * (1.0 / D_MODEL)
        dy_ref[...] = dy
        dg_ref[...] += jnp.sum(dy * fhat, axis=0, keepdims=True)
        dyg = dy * g
        df_ref[...] = (r * (dyg - fhat * jnp.mean(dyg * fhat, axis=-1, keepdims=True))).astype(bf16)

    row = pl.BlockSpec((ts, D_MODEL), lambda i: (i, 0))
    gain = pl.BlockSpec((1, D_MODEL), lambda i: (0, 0))
    return pl.pallas_call(
        body, name="loss_head", grid=(S // ts,), in_specs=[row, row, row, gain],
        out_specs=[row, row, gain, pl.BlockSpec((1, 1), lambda i: (0, 0))],
        out_shape=[jax.ShapeDtypeStruct((S, D_MODEL), f32), jax.ShapeDtypeStruct((S, D_MODEL), bf16),
                   jax.ShapeDtypeStruct((1, D_MODEL), f32), jax.ShapeDtypeStruct((1, 1), f32)],
        compiler_params=_params("arbitrary"),
    )(f, x2, target, g4)


def _ffn_bwd(df, gate, up, w_gate, w_up, w_down):
    S = df.shape[0]
    ts = 1024

    def body(df_ref, gate_ref, up_ref, wg_ref, wu_ref, wd_ref, a_ref, dgate_ref, dup_ref, dh_ref):
        j = pl.program_id(1)
        da = _dot_nt(df_ref[...], wd_ref[...])
        g = gate_ref[...].astype(f32)
        u = up_ref[...].astype(f32)
        sig = _sigmoid(g)
        silu = g * sig
        a_ref[...] = (silu * u).astype(bf16)
        dup = (da * silu).astype(bf16)
        dgate = (da * u * (sig * (1.0 + g * (1.0 - sig)))).astype(bf16)
        dup_ref[...] = dup
        dgate_ref[...] = dgate
        part = _dot_nt(dgate, wg_ref[...]) + _dot_nt(dup, wu_ref[...])

        @pl.when(j == 0)
        def _():
            dh_ref[...] = part

        @pl.when(j > 0)
        def _():
            dh_ref[...] += part

    act = pl.BlockSpec((ts, FF_TILE), lambda i, j: (i, j))
    row = pl.BlockSpec((ts, D_MODEL), lambda i, j: (i, 0))
    return pl.pallas_call(
        body, name="ffn_bwd", grid=(S // ts, D_FF // FF_TILE),
        in_specs=[row, act, act,
                  pl.BlockSpec((D_MODEL, FF_TILE), lambda i, j: (0, j)),
                  pl.BlockSpec((D_MODEL, FF_TILE), lambda i, j: (0, j)),
                  pl.BlockSpec((FF_TILE, D_MODEL), lambda i, j: (j, 0))],
        out_specs=[act, act, act, row],
        out_shape=[jax.ShapeDtypeStruct((S, D_FF), bf16)] * 3 + [jax.ShapeDtypeStruct((S, D_MODEL), f32)],
        compiler_params=_params("parallel", "arbitrary"),
    )(df, gate, up, w_gate, w_up, w_down)


def _norm_bwd(dh2, dy, x2, mix, g3, g2):
    S = dh2.shape[0]
    ts = 512

    def body(dh_ref, dy_ref, x2_ref, mix_ref, g3_ref, g2_ref, dx2_ref, dmix_ref, dg3_ref, dg2_ref):
        @pl.when(pl.program_id(0) == 0)
        def _():
            dg3_ref[...] = jnp.zeros_like(dg3_ref)
            dg2_ref[...] = jnp.zeros_like(dg2_ref)

        dh = dh_ref[...]
        x2 = x2_ref[...]
        r3 = _rms(x2)
        xhat = x2 * r3
        dg3_ref[...] += jnp.sum(dh * xhat, axis=0, keepdims=True)
        dhg = dh * g3_ref[...]
        dx2 = dy_ref[...] + r3 * (dhg - xhat * jnp.mean(dhg * xhat, axis=-1, keepdims=True))
        dx2_ref[...] = dx2
        mix = mix_ref[...]
        r2 = _rms(mix)
        mhat = mix * r2
        dg2_ref[...] += jnp.sum(dx2 * mhat, axis=0, keepdims=True)
        dmg = dx2 * g2_ref[...]
        dmix_ref[...] = (r2 * (dmg - mhat * jnp.mean(dmg * mhat, axis=-1, keepdims=True))).astype(bf16)

    row = pl.BlockSpec((ts, D_MODEL), lambda i: (i, 0))
    gain = pl.BlockSpec((1, D_MODEL), lambda i: (0, 0))
    return pl.pallas_call(
        body, name="norm_bwd", grid=(S // ts,), in_specs=[row, row, row, row, gain, gain],
        out_specs=[row, row, gain, gain],
        out_shape=[jax.ShapeDtypeStruct((S, D_MODEL), f32), jax.ShapeDtypeStruct((S, D_MODEL), bf16),
                   jax.ShapeDtypeStruct((1, D_MODEL), f32), jax.ShapeDtypeStruct((1, D_MODEL), f32)],
        compiler_params=_params("arbitrary"),
    )(dh2, dy, x2, mix, g3, g2)


def _out_proj_bwd(dmix, w_out, attn_out, head_ones):
    S = dmix.shape[0]
    ts = 512

    def body(dm_ref, w_ref, o_ref, ones_ref, dp_ref, dl_ref, *rest):
        do_refs, stage = rest[:-N_STAGE], rest[-N_STAGE:]
        dcat = _dot_nt(dm_ref[...], w_ref[...])
        dp_ref[...] = dcat[:, :POOL_WIDTH]
        do = dcat[:, POOL_WIDTH:]
        for j in range(ATTN_WIDTH // 128):
            stage[j][...] = do[:, j * 128:(j + 1) * 128]
        _store_streams(stage, do_refs, ts)
        prod = do * o_ref[...].astype(f32)
        hi = prod.astype(bf16)
        lo = (prod - hi.astype(f32)).astype(bf16)
        dl_ref[...] = _dot(hi, ones_ref[...]) + _dot(lo, ones_ref[...])

    row = lambda w: pl.BlockSpec((ts, w), lambda i: (i, 0))
    res = pl.pallas_call(
        body, name="out_proj_bwd", grid=(S // ts,),
        in_specs=[row(D_MODEL), pl.BlockSpec((D_MODEL, D_MODEL), lambda i: (0, 0)), row(ATTN_WIDTH),
                  pl.BlockSpec((ATTN_WIDTH, ATTN_WIDTH), lambda i: (0, 0))],
        out_specs=[row(POOL_WIDTH), row(ATTN_WIDTH)] + [_stream_spec(d, ts) for d in DILATIONS],
        out_shape=[jax.ShapeDtypeStruct((S, POOL_WIDTH), f32), jax.ShapeDtypeStruct((S, ATTN_WIDTH), f32)]
        + [_stream_shape(S, d) for d in DILATIONS],
        scratch_shapes=_stage_scratch(ts),
        compiler_params=_params("parallel"),
    )(dmix, w_out, attn_out, head_ones)
    return res[0], res[1], res[2:]


def _in_proj_bwd(du, dq, dk, dv, cos_t, sin_t, w_in, x, dx2, g1):
    S = x.shape[0]
    ts = 256

    def body(du_ref, dq_ref, dk_ref, dv_ref, cos_ref, sin_ref, w_ref, x_ref, dx2_ref, g_ref, gx_ref, dproj_ref, dg_ref):
        @pl.when(pl.program_id(0) == 0)
        def _():
            dg_ref[...] = jnp.zeros_like(dg_ref)

        dproj_ref[:, :POOL_WIDTH] = du_ref[...]
        cos = cos_ref[...]
        sin = sin_ref[...]
        first = _first_half_mask(ts)
        for j in range(ATTN_WIDTH // 128):
            cols = slice(j * 128, (j + 1) * 128)
            for base, ref in ((POOL_WIDTH, dq_ref), (POOL_WIDTH + ATTN_WIDTH, dk_ref)):
                g = ref[:, cols].astype(f32)
                pre = g * cos + _rope_partner(g * sin, first)
                dproj_ref[:, base + j * 128: base + (j + 1) * 128] = pre.astype(bf16)
        dproj_ref[:, POOL_WIDTH + 2 * ATTN_WIDTH:] = dv_ref[...]

        dh = _dot_nt(dproj_ref[...], w_ref[...])
        xv = x_ref[...]
        r = _rms(xv)
        xhat = xv * r
        dg_ref[...] += jnp.sum(dh * xhat, axis=0, keepdims=True)
        dhg = dh * g_ref[...]
        gx_ref[...] = dx2_ref[...] + r * (dhg - xhat * jnp.mean(dhg * xhat, axis=-1, keepdims=True))

    row = lambda w: pl.BlockSpec((ts, w), lambda i: (i, 0))
    gain = pl.BlockSpec((1, D_MODEL), lambda i: (0, 0))
    return pl.pallas_call(
        body, name="in_proj_bwd", grid=(S // ts,),
        in_specs=[row(POOL_WIDTH)] + [row(ATTN_WIDTH)] * 3 + [row(128), row(128),
                  pl.BlockSpec((D_MODEL, IN_WIDTH), lambda i: (0, 0)), row(D_MODEL), row(D_MODEL), gain],
        out_specs=[row(D_MODEL), row(IN_WIDTH), gain],
        out_shape=[jax.ShapeDtypeStruct((S, D_MODEL), f32), jax.ShapeDtypeStruct((S, IN_WIDTH), bf16),
                   jax.ShapeDtypeStruct((1, D_MODEL), f32)],
        compiler_params=_params("arbitrary"),
    )(du, dq, dk, dv, cos_t, sin_t, w_in, x, dx2, g1)


def _matmul_tn(a, b, tn, name):
    K, M = a.shape
    N = b.shape[1]
    tk = 512

    def body(a_ref, b_ref, o_ref):
        part = _dot_tn(a_ref[...], b_ref[...])

        @pl.when(pl.program_id(1) == 0)
        def _():
            o_ref[...] = part

        @pl.when(pl.program_id(1) > 0)
        def _():
            o_ref[...] += part

    return pl.pallas_call(
        body, name=name, grid=(N // tn, K // tk),
        in_specs=[pl.BlockSpec((tk, M), lambda n, k: (k, 0)), pl.BlockSpec((tk, tn), lambda n, k: (k, n))],
        out_specs=pl.BlockSpec((M, tn), lambda n, k: (0, n)),
        out_shape=jax.ShapeDtypeStruct((M, N), f32),
        compiler_params=_params("parallel", "arbitrary"),
    )(a, b)


def _rope_tables(S):
    half = HEAD_DIM // 2
    freqs = ROPE_THETA ** (-jnp.arange(half, dtype=f32) * (2.0 / HEAD_DIM))
    ang = jnp.arange(S).astype(f32)[:, None] * freqs[None, :]
    cos = jnp.tile(jnp.cos(ang), (1, 4))
    sin = jnp.sin(ang)
    sin = jnp.tile(jnp.concatenate([-sin, sin], axis=1), (1, 2))
    return cos, sin


def _block_diag(w_pool):
    w = jnp.zeros((POOL_WIDTH, POOL_WIDTH), w_pool.dtype)
    for g in range(POOL_WIDTH // POOL_GROUP):
        w = lax.dynamic_update_slice(w, w_pool[g], (g * POOL_GROUP, g * POOL_GROUP))
    return w


def _head_ones():
    head = np.arange(ATTN_WIDTH) // HEAD_DIM
    return jnp.asarray(head[:, None] == head[None, :], dtype=bf16)


def _local_grads(x, target, g1, w_pool, pool_scale, g2, g3, g4, w_in, w_out, w_gate, w_up, w_down):
    S = x.shape[0]
    cos_t, sin_t = _rope_tables(S)
    w_bd = _block_diag(w_pool).astype(bf16)

    h1, u, qs, ks, vs = _in_proj(x, g1, w_in, cos_t, sin_t)
    pool_out = _pool_fwd(u, w_bd, pool_scale)
    attn_out, lse = _attn_fwd(qs, ks, vs)
    mix, x2, h2 = _out_proj(pool_out, attn_out, w_out, x, g2, g3)
    gate, up, f = _ffn_fwd(h2, w_gate, w_up, w_down)
    dy, df, dg4, loss = _loss_head(f, x2, target, g4)

    a, dgate, dup, dh2 = _ffn_bwd(df, gate, up, w_gate, w_up, w_down)
    d_w_down = _matmul_tn(a, df, D_MODEL, "grad_w_down")
    d_w_gate = _matmul_tn(h2, dgate, FF_HALF, "grad_w_gate")
    d_w_up = _matmul_tn(h2, dup, FF_HALF, "grad_w_up")
    dx2, dmix, dg3, dg2 = _norm_bwd(dh2, dy, x2, mix, g3, g2)
    d_w_out = jnp.concatenate([_matmul_tn(pool_out, dmix, D_MODEL, "grad_w_out_pool"),
                               _matmul_tn(attn_out, dmix, D_MODEL, "grad_w_out_attn")], axis=0)
    dpool, delta, dos = _out_proj_bwd(dmix, w_out, attn_out, _head_ones())
    du, d_w_bd, d_scale = _pool_bwd(u, dpool, w_bd, pool_scale)
    dq, dk, dv = _attn_bwd(qs, ks, vs, dos, lse, delta)
    grad_x, dproj, dg1 = _in_proj_bwd(du, dq, dk, dv, cos_t, sin_t, w_in, x, dx2, g1)
    d_w_in = _matmul_tn(h1, dproj, IN_WIDTH // 2, "grad_w_in")
    d_w_pool = jnp.stack([d_w_bd[g * POOL_GROUP:(g + 1) * POOL_GROUP, g * POOL_GROUP:(g + 1) * POOL_GROUP]
                          for g in range(POOL_WIDTH // POOL_GROUP)])
    large = dict(w_in=d_w_in, w_out=d_w_out, w_gate=d_w_gate, w_up=d_w_up, w_down=d_w_down)
    small = dict(ln_pre_mix=dg1, ln_post_mix=dg2, ln_pre_ffn=dg3, ln_post_ffn=dg4, pool_scale=d_scale, w_pool=d_w_pool)
    return loss, grad_x, large, small


def _place():
    x, y, c = lax.axis_index("x"), lax.axis_index("y"), lax.axis_index("c")
    chips = [(1 - x, y), (x, 1 - y), (1 - x, 1 - y)]
    return x, y, c, chips


ANY = pl.BlockSpec(memory_space=pl.ANY)


def _row_chunks(rows, n, unit):
    units = rows // unit
    out, start = [], 0
    for i in range(n):
        size = (units // n + (1 if i < units % n else 0)) * unit
        out.append((start, size))
        start += size
    return out


GATHER_CHUNKS = _row_chunks(HALF_ROWS, 4, 32)
SWAP_CHUNKS = _row_chunks(HALF_ROWS, 8, 32)
JOIN_CHUNKS = _row_chunks(HALF_ROWS, 16, 32)
LOCAL_COPIES = 8


class _LocalCopy:
    def __init__(self, src_rows, dst_rows, rows, buf, sems_in, sems_out):
        self.loads, self.stores = [], []
        for i, (start, size) in enumerate(_row_chunks(rows, LOCAL_COPIES, 32)):
            r = pl.ds(start, size)
            self.loads.append(pltpu.make_async_copy(src_rows(r), buf.at[r], sems_in.at[i]))
            self.stores.append(pltpu.make_async_copy(buf.at[r], dst_rows(r), sems_out.at[i]))
        for cp in self.loads:
            cp.start()

    def finish(self):
        for load, store in zip(self.loads, self.stores):
            load.wait()
            store.start()
        for store in self.stores:
            store.wait()


def _local_scratch(rows, dtype):
    return [pltpu.VMEM((rows, D_MODEL), dtype), pltpu.SemaphoreType.DMA((LOCAL_COPIES,)),
            pltpu.SemaphoreType.DMA((LOCAL_COPIES,))]


def _gather_weights(pack):
    n_ch = len(GATHER_CHUNKS)

    def body(w_ref, out_ref, send1, recv1, send2, recv2, buf, sems_in, sems_out):
        x, y, c, chips = _place()
        me = 2 * x + y
        sibling = (x, y, 1 - c)
        own = _LocalCopy(lambda r: w_ref.at[r], lambda r: out_ref.at[me, r], PACK_ROWS, buf, sems_in, sems_out)

        def rows(core, ch):
            start, size = GATHER_CHUNKS[ch]
            return pl.ds(core * HALF_ROWS + start, size)

        def direct(j, ch, chip_xy, src_chip):
            cx, cy = chip_xy
            return pltpu.make_async_remote_copy(
                src_ref=w_ref.at[rows(c, ch)], dst_ref=out_ref.at[src_chip, rows(c, ch)],
                send_sem=send1.at[j * n_ch + ch], recv_sem=recv1.at[j * n_ch + ch],
                device_id=(cx, cy, c), device_id_type=MESH)

        def passed(j, ch, chip, core):
            return pltpu.make_async_remote_copy(
                src_ref=out_ref.at[chip, rows(core, ch)], dst_ref=out_ref.at[chip, rows(core, ch)],
                send_sem=send2.at[j * n_ch + ch], recv_sem=recv2.at[j * n_ch + ch],
                device_id=sibling, device_id_type=MESH)

        sends = [direct(j, ch, chip, me) for ch in range(n_ch) for j, chip in enumerate(chips)]
        for cp in sends:
            cp.start()
        own.finish()
        forwards = []
        for ch in range(n_ch):
            for j, (cx, cy) in enumerate(chips):
                direct(j, ch, (cx, cy), 2 * cx + cy).wait_recv()
                fw = passed(j, ch, 2 * cx + cy, c)
                fw.start()
                forwards.append(fw)
        for ch in range(n_ch):
            for j, (cx, cy) in enumerate(chips):
                passed(j, ch, 2 * cx + cy, 1 - c).wait_recv()
        for cp in sends + forwards:
            cp.wait_send()

    n_sem = 3 * n_ch
    return pl.pallas_call(
        body, name="gather_weights", in_specs=[ANY], out_specs=ANY,
        out_shape=jax.ShapeDtypeStruct((N_CHIPS, PACK_ROWS, D_MODEL), pack.dtype),
        scratch_shapes=[pltpu.SemaphoreType.DMA((n_sem,))] * 4 + _local_scratch(PACK_ROWS, pack.dtype),
        compiler_params=pltpu.CompilerParams(vmem_limit_bytes=VMEM_LIMIT_V7X),
    )(pack)


def _swap_halves(g):
    n_ch = len(SWAP_CHUNKS)

    def body(g_ref, theirs_ref, send, recv):
        x, y, c, _ = _place()

        def piece(s, ch, core):
            start, size = SWAP_CHUNKS[ch]
            return pltpu.make_async_remote_copy(
                src_ref=g_ref.at[s, pl.ds(core * HALF_ROWS + start, size)], dst_ref=theirs_ref.at[s, pl.ds(start, size)],
                send_sem=send.at[s * n_ch + ch], recv_sem=recv.at[s * n_ch + ch],
                device_id=(x, y, 1 - c), device_id_type=MESH)

        copies = [piece(s, ch, 1 - c) for s in range(N_CHIPS) for ch in range(n_ch)]
        for cp in copies:
            cp.start()
        for cp in copies:
            cp.wait()

    return pl.pallas_call(
        body, name="swap_halves", in_specs=[ANY], out_specs=ANY,
        out_shape=jax.ShapeDtypeStruct((N_CHIPS, HALF_ROWS, D_MODEL), g.dtype),
        scratch_shapes=[pltpu.SemaphoreType.DMA((N_CHIPS * n_ch,))] * 2,
    )(g)


ADD_ROWS = 376


def _add_cores(g, theirs):
    n_t = HALF_ROWS // ADD_ROWS

    def body(c_ref, g_ref, t_ref, o_ref):
        o_ref[...] = g_ref[...] + t_ref[...]

    blk = pl.BlockSpec((1, ADD_ROWS, D_MODEL), lambda s, t, c_ref: (s, t, 0))
    return pl.pallas_call(
        body, name="add_cores",
        grid_spec=pltpu.PrefetchScalarGridSpec(
            num_scalar_prefetch=1, grid=(N_CHIPS, n_t),
            in_specs=[pl.BlockSpec((1, ADD_ROWS, D_MODEL), lambda s, t, c_ref: (s, c_ref[0] * n_t + t, 0)), blk],
            out_specs=blk),
        out_shape=jax.ShapeDtypeStruct(theirs.shape, theirs.dtype),
        compiler_params=_params("parallel", "parallel"),
    )(lax.axis_index("c").astype(jnp.int32).reshape(1), g, theirs)


def _scatter_to_chips(h):
    n_ch = len(GATHER_CHUNKS)

    def body(h_ref, out_ref, send, recv, buf, sems_in, sems_out):
        x, y, c, chips = _place()
        me = 2 * x + y
        own = _LocalCopy(lambda r: h_ref.at[me, r], lambda r: out_ref.at[me, r], HALF_ROWS, buf, sems_in, sems_out)

        def piece(j, ch, chip_xy, dst_chip, src_chip):
            cx, cy = chip_xy
            start, size = GATHER_CHUNKS[ch]
            return pltpu.make_async_remote_copy(
                src_ref=h_ref.at[dst_chip, pl.ds(start, size)], dst_ref=out_ref.at[src_chip, pl.ds(start, size)],
                send_sem=send.at[j * n_ch + ch], recv_sem=recv.at[j * n_ch + ch],
                device_id=(cx, cy, c), device_id_type=MESH)

        sends = [piece(j, ch, (cx, cy), 2 * cx + cy, me) for ch in range(n_ch) for j, (cx, cy) in enumerate(chips)]
        for cp in sends:
            cp.start()
        own.finish()
        for ch in range(n_ch):
            for j, (cx, cy) in enumerate(chips):
                piece(j, ch, (cx, cy), me, 2 * cx + cy).wait_recv()
        for cp in sends:
            cp.wait_send()

    return pl.pallas_call(
        body, name="scatter_to_chips", in_specs=[ANY], out_specs=ANY,
        out_shape=jax.ShapeDtypeStruct(h.shape, h.dtype),
        scratch_shapes=[pltpu.SemaphoreType.DMA((3 * n_ch,))] * 2 + _local_scratch(HALF_ROWS, h.dtype),
        compiler_params=pltpu.CompilerParams(vmem_limit_bytes=VMEM_LIMIT_V7X),
    )(h)


def _join_halves(r):
    n_ch = len(JOIN_CHUNKS)

    def body(r_ref, out_ref, send, recv, buf, sems_in, sems_out):
        x, y, c, _ = _place()
        own = _LocalCopy(lambda r: r_ref.at[r], lambda r: out_ref.at[c, r], HALF_ROWS, buf, sems_in, sems_out)

        def piece(ch, core):
            start, size = JOIN_CHUNKS[ch]
            return pltpu.make_async_remote_copy(
                src_ref=r_ref.at[pl.ds(start, size)], dst_ref=out_ref.at[core, pl.ds(start, size)],
                send_sem=send.at[ch], recv_sem=recv.at[ch], device_id=(x, y, 1 - c), device_id_type=MESH)

        copies = [piece(ch, c) for ch in range(n_ch)]
        for cp in copies:
            cp.start()
        own.finish()
        for ch in range(n_ch):
            piece(ch, 1 - c).wait_recv()
        for cp in copies:
            cp.wait_send()

    return pl.pallas_call(
        body, name="join_halves", in_specs=[ANY], out_specs=ANY,
        out_shape=jax.ShapeDtypeStruct((2,) + r.shape, r.dtype),
        scratch_shapes=[pltpu.SemaphoreType.DMA((n_ch,))] * 2 + _local_scratch(HALF_ROWS, r.dtype),
        compiler_params=pltpu.CompilerParams(vmem_limit_bytes=VMEM_LIMIT_V7X),
    )(r)


def _add_slabs(terms, k, name):
    rows = terms[0][0].shape[1]
    tr = ADD_ROWS
    n = len(terms)

    def body(*refs):
        acc = refs[0][...]
        for r in refs[1:n]:
            acc = acc + r[...]
        refs[n][...] = acc

    slab = lambda first: pl.BlockSpec((1, tr, D_MODEL), lambda i, t: (first + i, t, 0))
    return pl.pallas_call(
        body, name=name, grid=(k, rows // tr), in_specs=[slab(first) for _, first in terms],
        out_specs=pl.BlockSpec((1, tr, D_MODEL), lambda i, t: (i, t, 0)),
        out_shape=jax.ShapeDtypeStruct((k, rows, D_MODEL), terms[0][0].dtype),
        compiler_params=_params("parallel", "parallel"),
    )(*[a for a, _ in terms])


def _sum_small(block):
    def body(b_ref, out_ref, gathered, send, recv):
        x, y, c, _ = _place()
        me = 4 * x + 2 * y + c
        gathered[me] = b_ref[...]
        sends = []
        for kk in range(1, N_DEV):
            flip = lambda v, bit: 1 - v if bit else v
            peer = (flip(x, kk & 4), flip(y, kk & 2), flip(c, kk & 1))
            cp = pltpu.make_async_remote_copy(
                src_ref=b_ref, dst_ref=gathered.at[me], send_sem=send.at[kk - 1], recv_sem=recv.at[kk - 1],
                device_id=peer, device_id_type=MESH)
            cp.start()
            sends.append(cp)
        for kk in range(1, N_DEV):
            peer_index = jnp.bitwise_xor(me, kk)
            pltpu.make_async_remote_copy(
                src_ref=b_ref, dst_ref=gathered.at[peer_index], send_sem=send.at[kk - 1], recv_sem=recv.at[kk - 1],
                device_id=(x, y, c), device_id_type=MESH).wait_recv()
        for cp in sends:
            cp.wait_send()
        acc = gathered[0]
        for dev in range(1, N_DEV):
            acc = acc + gathered[dev]
        out_ref[...] = acc

    vmem = pl.BlockSpec(memory_space=pltpu.VMEM)
    return pl.pallas_call(
        body, name="sum_small", in_specs=[vmem], out_specs=vmem,
        out_shape=jax.ShapeDtypeStruct(block.shape, block.dtype),
        scratch_shapes=[pltpu.VMEM((N_DEV,) + block.shape, block.dtype),
                        pltpu.SemaphoreType.DMA((N_DEV - 1,)), pltpu.SemaphoreType.DMA((N_DEV - 1,))],
    )(block)


def _adamw(w, g, m, v, name):
    rows, cols = w.shape
    tr = rows
    for cand in (512, 256, 128, 64, 32, 16, 8):
        if rows % cand == 0:
            tr = cand
            break
    c1 = 1.0 - ADAM_B1 ** ADAM_STEP
    c2 = 1.0 - ADAM_B2 ** ADAM_STEP

    def body(w_ref, g_ref, m_ref, v_ref, d_ref, nm_ref, nv_ref):
        gv = g_ref[...]
        nm = ADAM_B1 * m_ref[...] + (1.0 - ADAM_B1) * gv
        nv = ADAM_B2 * v_ref[...] + (1.0 - ADAM_B2) * (gv * gv)
        nm_ref[...] = nm
        nv_ref[...] = nv
        d_ref[...] = -ADAM_LR * ((nm / c1) / (jnp.sqrt(nv / c2) + ADAM_EPS) + ADAM_WD * w_ref[...])

    blk = pl.BlockSpec((tr, cols), lambda i: (i, 0))
    shape = jax.ShapeDtypeStruct((rows, cols), f32)
    return pl.pallas_call(
        body, name=name, grid=(rows // tr,), in_specs=[blk] * 4, out_specs=[blk] * 3, out_shape=[shape] * 3,
        compiler_params=_params("parallel"),
    )(w, g, m, v)


LARGE = ("w_in", "w_out", "w_gate", "w_up", "w_down")
SMALL = ("ln_pre_mix", "ln_post_mix", "ln_pre_ffn", "ln_post_ffn", "pool_scale", "w_pool")
COLUMN_SHARDED = {"w_in": IN_WIDTH // N_CHIPS, "w_gate": D_FF // N_CHIPS, "w_up": D_FF // N_CHIPS}


def _pack_shard(shards):
    return jnp.concatenate([shards[n].reshape(-1, D_MODEL) for n in LARGE], axis=0)


def _unpack_shard(pack, shapes):
    out, row = {}, 0
    for n, rows in zip(LARGE, PACK_SPLITS):
        out[n] = pack[row:row + rows].reshape(shapes[n])
        row += rows
    return out


def _whole_from_shards(packs):
    out, row = {}, 0
    for n, rows in zip(LARGE, PACK_SPLITS):
        part = packs[:, row:row + rows]
        if n in COLUMN_SHARDED:
            width = COLUMN_SHARDED[n]
            part = part.reshape(N_CHIPS, D_MODEL, width).transpose(1, 0, 2).reshape(D_MODEL, N_CHIPS * width)
        else:
            part = part.reshape(N_CHIPS * rows, D_MODEL)
        out[n] = part
        row += rows
    return out


def _shards_from_whole(grads):
    parts = []
    for n, rows in zip(LARGE, PACK_SPLITS):
        g = grads[n]
        if n in COLUMN_SHARDED:
            width = COLUMN_SHARDED[n]
            g = g.reshape(D_MODEL, N_CHIPS, width).transpose(1, 0, 2)
        parts.append(g.reshape(N_CHIPS, rows, D_MODEL))
    return jnp.concatenate(parts, axis=1)


def _pack_small(vals):
    rows = [vals[n].reshape(1, D_MODEL) for n in SMALL[:4]]
    rows.append(jnp.pad(vals["pool_scale"].reshape(1, POOL_WIDTH), ((0, 0), (0, D_MODEL - POOL_WIDTH))))
    rows.append(jnp.pad(vals["loss"].reshape(1, 1), ((0, 0), (0, D_MODEL - 1))))
    rows.append(jnp.zeros((2, D_MODEL), f32))
    rows.append(vals["w_pool"].reshape(16, D_MODEL))
    return jnp.concatenate(rows, axis=0)


def _unpack_small(block):
    out = {n: block[i:i + 1] for i, n in enumerate(SMALL[:4])}
    out["pool_scale"] = block[4:5, :POOL_WIDTH]
    out["loss"] = block[5, 0]
    out["w_pool"] = block[8:24].reshape(1, 4, POOL_GROUP, POOL_GROUP)
    return out


def kernel(x, ln_pre_mix, w_in, w_pool, pool_scale, w_out, ln_post_mix, ln_pre_ffn, w_gate, w_up, w_down, ln_post_ffn, loss_target, m_ln_pre_mix, m_w_in, m_w_pool, m_pool_scale, m_w_out, m_ln_post_mix, m_ln_pre_ffn, m_w_gate, m_w_up, m_w_down, m_ln_post_ffn, v_ln_pre_mix, v_w_in, v_w_pool, v_pool_scale, v_w_out, v_ln_post_mix, v_ln_pre_ffn, v_w_gate, v_w_up, v_w_down, v_ln_post_ffn):
    w = dict(ln_pre_mix=ln_pre_mix, w_in=w_in, w_pool=w_pool, pool_scale=pool_scale, w_out=w_out,
             ln_post_mix=ln_post_mix, ln_pre_ffn=ln_pre_ffn, w_gate=w_gate, w_up=w_up, w_down=w_down,
             ln_post_ffn=ln_post_ffn)
    m = dict(ln_pre_mix=m_ln_pre_mix, w_in=m_w_in, w_pool=m_w_pool, pool_scale=m_pool_scale, w_out=m_w_out,
             ln_post_mix=m_ln_post_mix, ln_pre_ffn=m_ln_pre_ffn, w_gate=m_w_gate, w_up=m_w_up, w_down=m_w_down,
             ln_post_ffn=m_ln_post_ffn)
    v = dict(ln_pre_mix=v_ln_pre_mix, w_in=v_w_in, w_pool=v_w_pool, pool_scale=v_pool_scale, w_out=v_w_out,
             ln_post_mix=v_ln_post_mix, ln_pre_ffn=v_ln_pre_ffn, w_gate=v_w_gate, w_up=v_w_up, w_down=v_w_down,
             ln_post_ffn=v_ln_post_ffn)

    packs = _gather_weights(_pack_shard({n: w[n][0].astype(bf16) for n in LARGE}))
    whole = _whole_from_shards(packs)

    loss, grad_x, large, small = _local_grads(
        x[0], loss_target[0], ln_pre_mix, w_pool[0], pool_scale, ln_post_mix, ln_pre_ffn, ln_post_ffn,
        whole["w_in"], whole["w_out"], whole["w_gate"], whole["w_up"], whole["w_down"])

    shard_major = _shards_from_whole(large)
    chip_sum = _add_cores(shard_major, _swap_halves(shard_major))
    pieces = _scatter_to_chips(chip_sum)
    reduced_half = _add_slabs([(pieces, j) for j in range(N_CHIPS)], 1, "add_chips")[0]
    reduced = _join_halves(reduced_half).reshape(PACK_ROWS, D_MODEL)
    shapes = {n: w[n].shape[1:] for n in LARGE}
    grads = _unpack_shard(reduced, shapes)

    total = _unpack_small(_sum_small(_pack_small(dict(small, loss=loss))))
    for n in SMALL:
        grads[n] = total[n]

    delta, new_m, new_v = {}, {}, {}
    for n in LARGE:
        delta[n], new_m[n], new_v[n] = _adamw(w[n][0], grads[n], m[n][0], v[n][0], "adamw_" + n)
    small_state = [_pack_small(dict({n: s[n] for n in SMALL}, loss=jnp.zeros((), f32))) for s in (w, m, v)]
    small_grad = _pack_small(dict({n: grads[n] for n in SMALL}, loss=jnp.zeros((), f32)))
    sd, sm, sv = _adamw(small_state[0], small_grad, small_state[1], small_state[2], "adamw_small")
    for out, block in ((delta, sd), (new_m, sm), (new_v, sv)):
        un = _unpack_small(block)
        for n in SMALL:
            out[n] = un[n]

    names = ("ln_pre_mix", "w_in", "w_pool", "pool_scale", "w_out", "ln_post_mix", "ln_pre_ffn", "w_gate", "w_up",
             "w_down", "ln_post_ffn")
    full = lambda d: [d[n].reshape(w[n].shape) for n in names]
    return (total["loss"], grad_x[None], *full(grads), *full(delta), *full(new_m), *full(new_v))
```

```python
import numpy as np
import jax
import jax.numpy as jnp
from jax import lax
from jax.experimental import pallas as pl
from jax.experimental.pallas import tpu as pltpu

D_MODEL = 1024
POOL_WIDTH = 256
POOL_GROUP = 64
ATTN_WIDTH = 768
HEAD_DIM = 64
IN_WIDTH = 2560
D_FF = 2816
BLOCK = 128
DILATIONS = (1, 4, 16)
ROPE_THETA = 10000.0
EPS = 1e-6
ATTN_SCALE = 0.125
NEG = -1e30

ADAM_LR = 0.001
ADAM_B1 = 0.9
ADAM_B2 = 0.999
ADAM_EPS = 1e-08
ADAM_WD = 0.01
ADAM_STEP = 10

N_CHIPS = 4
N_DEV = 8
PACK_SPLITS = (640, 256, 704, 704, 704)
PACK_ROWS = sum(PACK_SPLITS)
HALF_ROWS = PACK_ROWS // 2
SMALL_ROWS = 24

VMEM_LIMIT_V7X = 56 * 1024 * 1024
MESH = pl.DeviceIdType.MESH

f32 = jnp.float32
bf16 = jnp.bfloat16


def _params(*sem):
    return pltpu.CompilerParams(dimension_semantics=sem, vmem_limit_bytes=VMEM_LIMIT_V7X)


def _dot(a, b):
    return jnp.dot(a, b, preferred_element_type=f32)


def _dot_nt(a, b):
    return lax.dot_general(a, b, (((1,), (1,)), ((), ())), preferred_element_type=f32)


def _dot_tn(a, b):
    return lax.dot_general(a, b, (((0,), (0,)), ((), ())), preferred_element_type=f32)


def _rope_partner(a, first_half):
    return jnp.where(first_half, pltpu.roll(a, 96, 1), pltpu.roll(a, 32, 1))


def _first_half_mask(rows):
    lane = lax.broadcasted_iota(jnp.int32, (rows, 128), 1)
    return (lane % HEAD_DIM) < (HEAD_DIM // 2)


def _stream_spec(d, ts):
    return pl.BlockSpec((d, ts // d, ATTN_WIDTH), lambda i: (0, i, 0))


def _stream_shape(S, d):
    return jax.ShapeDtypeStruct((d, S // d, ATTN_WIDTH), bf16)


N_STAGE = ATTN_WIDTH // 128


def _stage_scratch(ts):
    return [pltpu.VMEM((ts, 128), f32)] * N_STAGE


def _store_streams(stage, out_refs, ts):
    for d, ref in zip(DILATIONS, out_refs):
        for r in range(d):
            rows = pl.ds(0, ts) if d == 1 else pl.ds(r, ts // d, stride=d)
            for j in range(N_STAGE):
                ref[r, :, j * 128:(j + 1) * 128] = stage[j][rows, :].astype(bf16)


def _in_proj(x, g1, w_in, cos_t, sin_t):
    S = x.shape[0]
    ts = 512

    def body(x_ref, g_ref, w_ref, cos_ref, sin_ref, h_ref, u_ref, *rest):
        outs, stage = rest[:-N_STAGE], rest[-N_STAGE:]
        xv = x_ref[...]
        r = lax.rsqrt(jnp.mean(xv * xv, axis=-1, keepdims=True) + EPS)
        h = ((xv * r) * g_ref[...]).astype(bf16)
        h_ref[...] = h
        proj = _dot(h, w_ref[...])
        u_ref[...] = proj[:, :POOL_WIDTH]
        cos = cos_ref[...]
        sin = sin_ref[...]
        first = _first_half_mask(ts)
        n_dil = len(DILATIONS)
        for which, base in enumerate((POOL_WIDTH, POOL_WIDTH + ATTN_WIDTH)):
            for j in range(ATTN_WIDTH // 128):
                a = proj[:, base + j * 128: base + (j + 1) * 128]
                stage[j][...] = a * cos + _rope_partner(a, first) * sin
            _store_streams(stage, outs[which * n_dil:(which + 1) * n_dil], ts)
        for j in range(ATTN_WIDTH // 128):
            base = POOL_WIDTH + 2 * ATTN_WIDTH + j * 128
            stage[j][...] = proj[:, base:base + 128]
        _store_streams(stage, outs[2 * n_dil:], ts)

    row = lambda w: pl.BlockSpec((ts, w), lambda i: (i, 0))
    streams = [_stream_spec(d, ts) for d in DILATIONS] * 3
    res = pl.pallas_call(
        body, name="in_proj", grid=(S // ts,),
        in_specs=[row(D_MODEL), pl.BlockSpec((1, D_MODEL), lambda i: (0, 0)),
                  pl.BlockSpec((D_MODEL, IN_WIDTH), lambda i: (0, 0)), row(128), row(128)],
        out_specs=[row(D_MODEL), row(POOL_WIDTH)] + streams,
        out_shape=[jax.ShapeDtypeStruct((S, D_MODEL), bf16), jax.ShapeDtypeStruct((S, POOL_WIDTH), f32)]
        + [_stream_shape(S, d) for d in DILATIONS] * 3,
        scratch_shapes=_stage_scratch(ts),
        compiler_params=_params("parallel"),
    )(x, g1, w_in, cos_t, sin_t)
    n = len(DILATIONS)
    return res[0], res[1], res[2:2 + n], res[2 + n:2 + 2 * n], res[2 + 2 * n:]


POOL_HALO = 16


def _pool_lane_group(rows):
    return lax.broadcasted_iota(jnp.int32, (rows, POOL_WIDTH), 1) // POOL_GROUP


def _pool_select(group, s2, s4, s8, s16):
    return jnp.where(group == 0, s2, jnp.where(group == 1, s4, jnp.where(group == 2, s8, s16)))


def _pool_count(t0, rows):
    group = _pool_lane_group(rows)
    t = t0 + lax.broadcasted_iota(jnp.int32, (rows, POOL_WIDTH), 0)
    win = _pool_select(group, 2, 4, 8, 16)
    return jnp.minimum(t + 1, win).astype(f32)


def _pool_diff(u_halo, u_tile, t0):
    ts = u_tile.shape[0]
    ext = jnp.concatenate([u_halo, u_tile], axis=0)
    s2 = ext + pltpu.roll(ext, 1, 0)
    s4 = s2 + pltpu.roll(s2, 2, 0)
    s8 = s4 + pltpu.roll(s4, 4, 0)
    s16 = s8 + pltpu.roll(s8, 8, 0)
    group = _pool_lane_group(ts + POOL_HALO)
    wsum = _pool_select(group, s2, s4, s8, s16)[POOL_HALO:]
    return wsum / _pool_count(t0, ts) - u_tile


def _pool_specs(ts, n_tiles):
    tile = pl.BlockSpec((ts, POOL_WIDTH), lambda i: (i, 0))
    per = ts // POOL_HALO
    before = pl.BlockSpec((POOL_HALO, POOL_WIDTH), lambda i: (jnp.maximum(i * per - 1, 0), 0))
    after = pl.BlockSpec((POOL_HALO, POOL_WIDTH), lambda i: (jnp.minimum((i + 1) * per, n_tiles * per - 1), 0))
    return tile, before, after


def _pool_fwd(u, w_bd, scale):
    S = u.shape[0]
    ts = 512
    n_tiles = S // ts

    def body(u_ref, halo_ref, w_ref, sc_ref, y_ref):
        i = pl.program_id(0)
        halo = jnp.where(i > 0, halo_ref[...], 0.0)
        d = _pool_diff(halo, u_ref[...], i * ts)
        y_ref[...] = (_dot(d.astype(bf16), w_ref[...]) * sc_ref[...]).astype(bf16)

    tile, before, _ = _pool_specs(ts, n_tiles)
    return pl.pallas_call(
        body, name="pool_fwd", grid=(n_tiles,),
        in_specs=[tile, before, pl.BlockSpec((POOL_WIDTH, POOL_WIDTH), lambda i: (0, 0)),
                  pl.BlockSpec((1, POOL_WIDTH), lambda i: (0, 0))],
        out_specs=tile, out_shape=jax.ShapeDtypeStruct((S, POOL_WIDTH), bf16),
        compiler_params=_params("parallel"),
    )(u, u, w_bd, scale)


def _pool_bwd(u, dy, w_bd, scale):
    S = u.shape[0]
    ts = 512
    n_tiles = S // ts

    def body(u_ref, halo_ref, dy_ref, dy_next_ref, w_ref, sc_ref, du_ref, dw_ref, dsc_ref):
        i = pl.program_id(0)

        @pl.when(i == 0)
        def _():
            dw_ref[...] = jnp.zeros_like(dw_ref)
            dsc_ref[...] = jnp.zeros_like(dsc_ref)

        halo = jnp.where(i > 0, halo_ref[...], 0.0)
        d = _pool_diff(halo, u_ref[...], i * ts).astype(bf16)
        w = w_ref[...]
        sc = sc_ref[...]
        dy_tile = dy_ref[...]
        z = _dot(d, w)
        dsc_ref[...] += jnp.sum(dy_tile * z, axis=0, keepdims=True)
        dy_next = jnp.where(i < n_tiles - 1, dy_next_ref[...], 0.0)
        dz = (jnp.concatenate([dy_tile, dy_next], axis=0) * sc).astype(bf16)
        dw_ref[...] += _dot_tn(d, dz[:ts])
        dd = _dot_nt(dz, w)
        e = dd / _pool_count(i * ts, ts + POOL_HALO)
        n = ts + POOL_HALO
        f2 = e + pltpu.roll(e, n - 1, 0)
        f4 = f2 + pltpu.roll(f2, n - 2, 0)
        f8 = f4 + pltpu.roll(f4, n - 4, 0)
        f16 = f8 + pltpu.roll(f8, n - 8, 0)
        fsum = _pool_select(_pool_lane_group(n), f2, f4, f8, f16)
        du_ref[...] = (fsum[:ts] - dd[:ts]).astype(bf16)

    tile, before, after = _pool_specs(ts, n_tiles)
    return pl.pallas_call(
        body, name="pool_bwd", grid=(n_tiles,),
        in_specs=[tile, before, tile, after, pl.BlockSpec((POOL_WIDTH, POOL_WIDTH), lambda i: (0, 0)),
                  pl.BlockSpec((1, POOL_WIDTH), lambda i: (0, 0))],
        out_specs=[tile, pl.BlockSpec((POOL_WIDTH, POOL_WIDTH), lambda i: (0, 0)),
                   pl.BlockSpec((1, POOL_WIDTH), lambda i: (0, 0))],
        out_shape=[jax.ShapeDtypeStruct((S, POOL_WIDTH), bf16), jax.ShapeDtypeStruct((POOL_WIDTH, POOL_WIDTH), f32),
                   jax.ShapeDtypeStruct((1, POOL_WIDTH), f32)],
        compiler_params=_params("arbitrary"),
    )(u, u, dy, dy, w_bd, scale)


SUPER = BLOCK * DILATIONS[-1]
UNITS = SUPER // BLOCK
FWD_UNROLL = 4
BWD_UNROLL = 4


def _band_mask(has_prev):
    qi = lax.broadcasted_iota(jnp.int32, (BLOCK, 2 * BLOCK), 0)
    kj = lax.broadcasted_iota(jnp.int32, (BLOCK, 2 * BLOCK), 1)
    return (kj >= qi) & (kj <= qi + BLOCK) & ((kj >= BLOCK) | has_prev)


def _head0_mask(rows=BLOCK):
    return lax.broadcasted_iota(jnp.int32, (rows, 128), 1) < HEAD_DIM


def _per_head(stat, h0, h):
    other = pltpu.roll(stat, HEAD_DIM, 1)
    full = jnp.where(h0, stat, other) if h == 0 else jnp.where(h0, other, stat)
    return jnp.concatenate([full, full], axis=1)


def _natural_rows(d, r, n):
    if d == 1:
        return pl.ds(pl.multiple_of(n * BLOCK, BLOCK), BLOCK)
    return pl.ds(n * (BLOCK * d) + r, BLOCK, stride=d)


def _unit_place(d, u):
    per_stream = UNITS // d
    return u // per_stream, u % per_stream, per_stream


def _block_rows(n):
    return pl.ds(pl.multiple_of(n * BLOCK, BLOCK), BLOCK)


def _band(cur_ref, tail_ref, r, n):
    before = jnp.where(n > 0, cur_ref[r, _block_rows(jnp.maximum(n - 1, 0)), :], tail_ref[r])
    return jnp.concatenate([before, cur_ref[r, _block_rows(n), :]], axis=0)


def _attn_in_specs(S, with_do):
    specs = []
    last = S // SUPER - 1
    for d in DILATIONS:
        per_stream = UNITS // d
        cur = pl.BlockSpec((d, SUPER // d, 128), lambda hp, sb: (0, jnp.minimum(sb, last), hp))
        tail = pl.BlockSpec(
            (d, BLOCK, 128),
            lambda hp, sb, per_stream=per_stream: (0, jnp.maximum(jnp.minimum(sb, last) * per_stream - 1, 0), hp))
        specs += [cur] * (2 if with_do else 1) + [cur, tail, cur, tail]
    return specs


def _attn_fwd(qs, ks, vs):
    S = qs[0].shape[1]
    n_dil = len(DILATIONS)

    def body(*refs):
        ins, (out_ref, lse_ref) = refs[:-2 - 2 * n_dil], refs[-2 - 2 * n_dil:-2 * n_dil]
        o_sc, l_sc = refs[-2 * n_dil:-n_dil], refs[-n_dil:]
        sb = pl.program_id(1)
        h0 = _head0_mask()
        for ci, d in enumerate(DILATIONS):
            q_ref, kc_ref, kp_ref, vc_ref, vp_ref = ins[5 * ci:5 * ci + 5]

            def unit(u, carry, d=d, ci=ci, q_ref=q_ref, kc_ref=kc_ref, kp_ref=kp_ref, vc_ref=vc_ref, vp_ref=vp_ref):
                r, n, _ = _unit_place(d, u)
                qv = q_ref[r, _block_rows(n), :]
                kb = _band(kc_ref, kp_ref, r, n)
                vb = _band(vc_ref, vp_ref, r, n)
                valid = _band_mask((sb > 0) | (n > 0))
                outs, lses = [], []
                for h in range(2):
                    keep = h0 if h == 0 else jnp.logical_not(h0)
                    qh = jnp.where(keep, qv, jnp.zeros_like(qv))
                    s = jnp.where(valid, _dot_nt(qh, kb) * ATTN_SCALE, NEG)
                    m = jnp.max(s, axis=1, keepdims=True)
                    e = jnp.exp(s - m)
                    den = jnp.sum(e, axis=1, keepdims=True)
                    outs.append(_dot((e / den).astype(bf16), vb))
                    lses.append(jnp.broadcast_to(m + jnp.log(den), (BLOCK, 128)))
                rows = _natural_rows(d, r, n)
                o_sc[ci][rows, :] = jnp.where(h0, outs[0], outs[1])
                l_sc[ci][rows, :] = jnp.where(h0, lses[0], lses[1])
                return carry

            lax.fori_loop(0, UNITS, unit, 0, unroll=FWD_UNROLL)

        def merge(t, carry):
            rows = pl.ds(pl.multiple_of(t * 256, 256), 256)
            a, b, c = l_sc[0][rows, :], l_sc[1][rows, :], l_sc[2][rows, :]
            m = jnp.maximum(jnp.maximum(a, b), c)
            ea, eb, ec = jnp.exp(a - m), jnp.exp(b - m), jnp.exp(c - m)
            tot = ea + eb + ec
            out_ref[rows, :] = ((ea / tot) * o_sc[0][rows, :] + (eb / tot) * o_sc[1][rows, :]
                                + (ec / tot) * o_sc[2][rows, :]).astype(bf16)
            lse_ref[rows, :] = m + jnp.log(tot)
            return carry

        lax.fori_loop(0, SUPER // 256, merge, 0)

    args = []
    for q, k, v in zip(qs, ks, vs):
        args += [q, k, k, v, v]
    nat = pl.BlockSpec((SUPER, 128), lambda hp, sb: (sb, hp))
    return pl.pallas_call(
        body, name="attn_fwd", grid=(ATTN_WIDTH // 128, S // SUPER),
        in_specs=_attn_in_specs(S, False), out_specs=[nat, nat],
        out_shape=[jax.ShapeDtypeStruct((S, ATTN_WIDTH), bf16), jax.ShapeDtypeStruct((S, ATTN_WIDTH), f32)],
        scratch_shapes=[pltpu.VMEM((SUPER, 128), f32)] * (2 * n_dil),
        compiler_params=_params("parallel", "arbitrary"),
    )(*args)


def _attn_bwd(qs, ks, vs, dos, lse, delta):
    S = qs[0].shape[1]
    n_steps = S // SUPER
    last = n_steps - 1

    def body(*refs):
        ins, (lse_ref, dl_ref, dq_ref, dk_ref, dv_ref, dq_acc, dk_acc, dv_acc) = refs[:-8], refs[-8:]
        sb = pl.program_id(1)
        cur = sb % 2
        prv = 1 - cur

        @pl.when(sb < n_steps)
        def _():
            dq_acc[...] = jnp.zeros_like(dq_acc)
            dk_acc[cur] = jnp.zeros((SUPER, 128), f32)
            dv_acc[cur] = jnp.zeros((SUPER, 128), f32)
            h0 = _head0_mask()
            h0_band = _head0_mask(2 * BLOCK)
            for ci, d in enumerate(DILATIONS):
                q_ref, do_ref, kc_ref, kp_ref, vc_ref, vp_ref = ins[6 * ci:6 * ci + 6]

                def unit(u, carry, d=d, q_ref=q_ref, do_ref=do_ref, kc_ref=kc_ref, kp_ref=kp_ref, vc_ref=vc_ref,
                         vp_ref=vp_ref):
                    r, n, per_stream = _unit_place(d, u)
                    qv = q_ref[r, _block_rows(n), :]
                    dov = do_ref[r, _block_rows(n), :]
                    kb = _band(kc_ref, kp_ref, r, n)
                    vb = _band(vc_ref, vp_ref, r, n)
                    rows = _natural_rows(d, r, n)
                    lse_v = lse_ref[rows, :]
                    dl_v = dl_ref[rows, :]
                    has_prev = (sb > 0) | (n > 0)
                    valid = _band_mask(has_prev)
                    dqs, dks, dvs = [], [], []
                    for h in range(2):
                        keep = h0 if h == 0 else jnp.logical_not(h0)
                        qh = jnp.where(keep, qv, jnp.zeros_like(qv))
                        doh = jnp.where(keep, dov, jnp.zeros_like(dov))
                        s = jnp.where(valid, _dot_nt(qh, kb) * ATTN_SCALE, NEG)
                        p = jnp.exp(s - _per_head(lse_v, h0, h))
                        dp = _dot_nt(doh, vb)
                        ds = (p * (dp - _per_head(dl_v, h0, h)) * ATTN_SCALE).astype(bf16)
                        dqs.append(_dot(ds, kb))
                        dks.append(_dot_tn(ds, qv))
                        dvs.append(_dot_tn(p.astype(bf16), dov))
                    dq_acc[rows, :] += jnp.where(h0, dqs[0], dqs[1])
                    dkb = jnp.where(h0_band, dks[0], dks[1])
                    dvb = jnp.where(h0_band, dvs[0], dvs[1])
                    dk_acc[cur, rows, :] += dkb[BLOCK:]
                    dv_acc[cur, rows, :] += dvb[BLOCK:]

                    slot = jnp.where((n > 0) | (sb == 0), cur, prv)
                    before = _natural_rows(d, r, jnp.where(n > 0, n - 1, per_stream - 1))
                    dk_acc[slot, before, :] += dkb[:BLOCK]
                    dv_acc[slot, before, :] += dvb[:BLOCK]
                    return carry

                lax.fori_loop(0, UNITS, unit, 0, unroll=BWD_UNROLL)
            dq_ref[...] = dq_acc[...].astype(bf16)

        @pl.when(sb > 0)
        def _():
            dk_ref[...] = dk_acc[prv].astype(bf16)
            dv_ref[...] = dv_acc[prv].astype(bf16)

    args = []
    for q, k, v, do in zip(qs, ks, vs, dos):
        args += [q, do, k, k, v, v]
    nat = pl.BlockSpec((SUPER, 128), lambda hp, sb: (jnp.minimum(sb, last), hp))
    nat_before = pl.BlockSpec((SUPER, 128), lambda hp, sb: (jnp.clip(sb - 1, 0, last), hp))
    out = jax.ShapeDtypeStruct((S, ATTN_WIDTH), bf16)
    return pl.pallas_call(
        body, name="attn_bwd", grid=(ATTN_WIDTH // 128, n_steps + 1),
        in_specs=_attn_in_specs(S, True) + [nat, nat], out_specs=[nat, nat_before, nat_before],
        out_shape=[out, out, out],
        scratch_shapes=[pltpu.VMEM((SUPER, 128), f32), pltpu.VMEM((2, SUPER, 128), f32),
                        pltpu.VMEM((2, SUPER, 128), f32)],
        compiler_params=_params("parallel", "arbitrary"),
    )(*args, lse, delta)


def _rms(v):
    return lax.rsqrt(jnp.mean(v * v, axis=-1, keepdims=True) + EPS)


def _out_proj(pool_out, attn_out, w_out, x, g2, g3):
    S = x.shape[0]
    ts = 512

    def body(p_ref, a_ref, w_ref, x_ref, g2_ref, g3_ref, mix_ref, x2_ref, h2_ref):
        mix = _dot(p_ref[...], w_ref[:POOL_WIDTH, :]) + _dot(a_ref[...], w_ref[POOL_WIDTH:, :])
        mix_ref[...] = mix
        x2 = x_ref[...] + (mix * _rms(mix)) * g2_ref[...]
        x2_ref[...] = x2
        h2_ref[...] = ((x2 * _rms(x2)) * g3_ref[...]).astype(bf16)

    row = lambda w: pl.BlockSpec((ts, w), lambda i: (i, 0))
    gain = pl.BlockSpec((1, D_MODEL), lambda i: (0, 0))
    return pl.pallas_call(
        body, name="out_proj", grid=(S // ts,),
        in_specs=[row(POOL_WIDTH), row(ATTN_WIDTH), pl.BlockSpec((D_MODEL, D_MODEL), lambda i: (0, 0)),
                  row(D_MODEL), gain, gain],
        out_specs=[row(D_MODEL)] * 3,
        out_shape=[jax.ShapeDtypeStruct((S, D_MODEL), f32), jax.ShapeDtypeStruct((S, D_MODEL), f32),
                   jax.ShapeDtypeStruct((S, D_MODEL), bf16)],
        compiler_params=_params("parallel"),
    )(pool_out, attn_out, w_out, x, g2, g3)


FF_TILE = 256
FF_HALF = D_FF // 2


def _sigmoid(g):
    return 1.0 / (1.0 + jnp.exp(-g))


def _ffn_fwd(h2, w_gate, w_up, w_down):
    S = h2.shape[0]
    ts = 1024

    def body(h_ref, wg_ref, wu_ref, wd_ref, gate_ref, up_ref, f_ref):
        j = pl.program_id(1)
        h = h_ref[...]
        gate = _dot(h, wg_ref[...])
        up = _dot(h, wu_ref[...])
        gate_ref[...] = gate.astype(bf16)
        up_ref[...] = up.astype(bf16)
        part = _dot((gate * _sigmoid(gate) * up).astype(bf16), wd_ref[...])

        @pl.when(j == 0)
        def _():
            f_ref[...] = part

        @pl.when(j > 0)
        def _():
            f_ref[...] += part

    act = pl.BlockSpec((ts, FF_TILE), lambda i, j: (i, j))
    return pl.pallas_call(
        body, name="ffn_fwd", grid=(S // ts, D_FF // FF_TILE),
        in_specs=[pl.BlockSpec((ts, D_MODEL), lambda i, j: (i, 0)),
                  pl.BlockSpec((D_MODEL, FF_TILE), lambda i, j: (0, j)),
                  pl.BlockSpec((D_MODEL, FF_TILE), lambda i, j: (0, j)),
                  pl.BlockSpec((FF_TILE, D_MODEL), lambda i, j: (j, 0))],
        out_specs=[act, act, pl.BlockSpec((ts, D_MODEL), lambda i, j: (i, 0))],
        out_shape=[jax.ShapeDtypeStruct((S, D_FF), bf16), jax.ShapeDtypeStruct((S, D_FF), bf16),
                   jax.ShapeDtypeStruct((S, D_MODEL), f32)],
        compiler_params=_params("parallel", "arbitrary"),
    )(h2, w_gate, w_up, w_down)


def _loss_head(f, x2, target, g4):
    S = f.shape[0]
    ts = 512

    def body(f_ref, x2_ref, t_ref, g_ref, dy_ref, df_ref, dg_ref, loss_ref):
        @pl.when(pl.program_id(0) == 0)
        def _():
            dg_ref[...] = jnp.zeros_like(dg_ref)
            loss_ref[...] = jnp.zeros_like(loss_ref)

        fv = f_ref[...]
        g = g_ref[...]
        r = _rms(fv)
        fhat = fv * r
        err = (x2_ref[...] + fhat * g) - t_ref[...]
        loss_ref[...] += 0.5 * jnp.sum(jnp.mean(err * err, axis=-1, keepdims=True), axis=0, keepdims=True)
        dy = err * (1.0 / D_MODEL)
        dy_ref[...] = dy
        dg_ref[...] += jnp.sum(dy * fhat, axis=0, keepdims=True)
        dyg = dy * g
        df_ref[...] = (r * (dyg - fhat * jnp.mean(dyg * fhat, axis=-1, keepdims=True))).astype(bf16)

    row = pl.BlockSpec((ts, D_MODEL), lambda i: (i, 0))
    gain = pl.BlockSpec((1, D_MODEL), lambda i: (0, 0))
    return pl.pallas_call(
        body, name="loss_head", grid=(S // ts,), in_specs=[row, row, row, gain],
        out_specs=[row, row, gain, pl.BlockSpec((1, 1), lambda i: (0, 0))],
        out_shape=[jax.ShapeDtypeStruct((S, D_MODEL), f32), jax.ShapeDtypeStruct((S, D_MODEL), bf16),
                   jax.ShapeDtypeStruct((1, D_MODEL), f32), jax.ShapeDtypeStruct((1, 1), f32)],
        compiler_params=_params("arbitrary"),
    )(f, x2, target, g4)


def _ffn_bwd(df, gate, up, w_gate, w_up, w_down):
    S = df.shape[0]
    ts = 1024

    def body(df_ref, gate_ref, up_ref, wg_ref, wu_ref, wd_ref, a_ref, dgate_ref, dup_ref, dh_ref):
        j = pl.program_id(1)
        da = _dot_nt(df_ref[...], wd_ref[...])
        g = gate_ref[...].astype(f32)
        u = up_ref[...].astype(f32)
        sig = _sigmoid(g)
        silu = g * sig
        a_ref[...] = (silu * u).astype(bf16)
        dup = (da * silu).astype(bf16)
        dgate = (da * u * (sig * (1.0 + g * (1.0 - sig)))).astype(bf16)
        dup_ref[...] = dup
        dgate_ref[...] = dgate
        part = _dot_nt(dgate, wg_ref[...]) + _dot_nt(dup, wu_ref[...])

        @pl.when(j == 0)
        def _():
            dh_ref[...] = part

        @pl.when(j > 0)
        def _():
            dh_ref[...] += part

    act = pl.BlockSpec((ts, FF_TILE), lambda i, j: (i, j))
    row = pl.BlockSpec((ts, D_MODEL), lambda i, j: (i, 0))
    return pl.pallas_call(
        body, name="ffn_bwd", grid=(S // ts, D_FF // FF_TILE),
        in_specs=[row, act, act,
                  pl.BlockSpec((D_MODEL, FF_TILE), lambda i, j: (0, j)),
                  pl.BlockSpec((D_MODEL, FF_TILE), lambda i, j: (0, j)),
                  pl.BlockSpec((FF_TILE, D_MODEL), lambda i, j: (j, 0))],
        out_specs=[act, act, act, row],
        out_shape=[jax.ShapeDtypeStruct((S, D_FF), bf16)] * 3 + [jax.ShapeDtypeStruct((S, D_MODEL), f32)],
        compiler_params=_params("parallel", "arbitrary"),
    )(df, gate, up, w_gate, w_up, w_down)


def _norm_bwd(dh2, dy, x2, mix, g3, g2):
    S = dh2.shape[0]
    ts = 512

    def body(dh_ref, dy_ref, x2_ref, mix_ref, g3_ref, g2_ref, dx2_ref, dmix_ref, dg3_ref, dg2_ref):
        @pl.when(pl.program_id(0) == 0)
        def _():
            dg3_ref[...] = jnp.zeros_like(dg3_ref)
            dg2_ref[...] = jnp.zeros_like(dg2_ref)

        dh = dh_ref[...]
        x2 = x2_ref[...]
        r3 = _rms(x2)
        xhat = x2 * r3
        dg3_ref[...] += jnp.sum(dh * xhat, axis=0, keepdims=True)
        dhg = dh * g3_ref[...]
        dx2 = dy_ref[...] + r3 * (dhg - xhat * jnp.mean(dhg * xhat, axis=-1, keepdims=True))
        dx2_ref[...] = dx2
        mix = mix_ref[...]
        r2 = _rms(mix)
        mhat = mix * r2
        dg2_ref[...] += jnp.sum(dx2 * mhat, axis=0, keepdims=True)
        dmg = dx2 * g2_ref[...]
        dmix_ref[...] = (r2 * (dmg - mhat * jnp.mean(dmg * mhat, axis=-1, keepdims=True))).astype(bf16)

    row = pl.BlockSpec((ts, D_MODEL), lambda i: (i, 0))
    gain = pl.BlockSpec((1, D_MODEL), lambda i: (0, 0))
    return pl.pallas_call(
        body, name="norm_bwd", grid=(S // ts,), in_specs=[row, row, row, row, gain, gain],
        out_specs=[row, row, gain, gain],
        out_shape=[jax.ShapeDtypeStruct((S, D_MODEL), f32), jax.ShapeDtypeStruct((S, D_MODEL), bf16),
                   jax.ShapeDtypeStruct((1, D_MODEL), f32), jax.ShapeDtypeStruct((1, D_MODEL), f32)],
        compiler_params=_params("arbitrary"),
    )(dh2, dy, x2, mix, g3, g2)


def _out_proj_bwd(dmix, w_out, attn_out, head_ones):
    S = dmix.shape[0]
    ts = 512

    def body(dm_ref, w_ref, o_ref, ones_ref, dp_ref, dl_ref, *rest):
        do_refs, stage = rest[:-N_STAGE], rest[-N_STAGE:]
        dcat = _dot_nt(dm_ref[...], w_ref[...])
        dp_ref[...] = dcat[:, :POOL_WIDTH]
        do = dcat[:, POOL_WIDTH:]
        for j in range(ATTN_WIDTH // 128):
            stage[j][...] = do[:, j * 128:(j + 1) * 128]
        _store_streams(stage, do_refs, ts)
        prod = do * o_ref[...].astype(f32)
        hi = prod.astype(bf16)
        lo = (prod - hi.astype(f32)).astype(bf16)
        dl_ref[...] = _dot(hi, ones_ref[...]) + _dot(lo, ones_ref[...])

    row = lambda w: pl.BlockSpec((ts, w), lambda i: (i, 0))
    res = pl.pallas_call(
        body, name="out_proj_bwd", grid=(S // ts,),
        in_specs=[row(D_MODEL), pl.BlockSpec((D_MODEL, D_MODEL), lambda i: (0, 0)), row(ATTN_WIDTH),
                  pl.BlockSpec((ATTN_WIDTH, ATTN_WIDTH), lambda i: (0, 0))],
        out_specs=[row(POOL_WIDTH), row(ATTN_WIDTH)] + [_stream_spec(d, ts) for d in DILATIONS],
        out_shape=[jax.ShapeDtypeStruct((S, POOL_WIDTH), f32), jax.ShapeDtypeStruct((S, ATTN_WIDTH), f32)]
        + [_stream_shape(S, d) for d in DILATIONS],
        scratch_shapes=_stage_scratch(ts),
        compiler_params=_params("parallel"),
    )(dmix, w_out, attn_out, head_ones)
    return res[0], res[1], res[2:]


def _in_proj_bwd(du, dq, dk, dv, cos_t, sin_t, w_in, x, dx2, g1):
    S = x.shape[0]
    ts = 256

    def body(du_ref, dq_ref, dk_ref, dv_ref, cos_ref, sin_ref, w_ref, x_ref, dx2_ref, g_ref, gx_ref, dproj_ref, dg_ref):
        @pl.when(pl.program_id(0) == 0)
        def _():
            dg_ref[...] = jnp.zeros_like(dg_ref)

        dproj_ref[:, :POOL_WIDTH] = du_ref[...]
        cos = cos_ref[...]
        sin = sin_ref[...]
        first = _first_half_mask(ts)
        for j in range(ATTN_WIDTH // 128):
            cols = slice(j * 128, (j + 1) * 128)
            for base, ref in ((POOL_WIDTH, dq_ref), (POOL_WIDTH + ATTN_WIDTH, dk_ref)):
                g = ref[:, cols].astype(f32)
                pre = g * cos + _rope_partner(g * sin, first)
                dproj_ref[:, base + j * 128: base + (j + 1) * 128] = pre.astype(bf16)
        dproj_ref[:, POOL_WIDTH + 2 * ATTN_WIDTH:] = dv_ref[...]

        dh = _dot_nt(dproj_ref[...], w_ref[...])
        xv = x_ref[...]
        r = _rms(xv)
        xhat = xv * r
        dg_ref[...] += jnp.sum(dh * xhat, axis=0, keepdims=True)
        dhg = dh * g_ref[...]
        gx_ref[...] = dx2_ref[...] + r * (dhg - xhat * jnp.mean(dhg * xhat, axis=-1, keepdims=True))

    row = lambda w: pl.BlockSpec((ts, w), lambda i: (i, 0))
    gain = pl.BlockSpec((1, D_MODEL), lambda i: (0, 0))
    return pl.pallas_call(
        body, name="in_proj_bwd", grid=(S // ts,),
        in_specs=[row(POOL_WIDTH)] + [row(ATTN_WIDTH)] * 3 + [row(128), row(128),
                  pl.BlockSpec((D_MODEL, IN_WIDTH), lambda i: (0, 0)), row(D_MODEL), row(D_MODEL), gain],
        out_specs=[row(D_MODEL), row(IN_WIDTH), gain],
        out_shape=[jax.ShapeDtypeStruct((S, D_MODEL), f32), jax.ShapeDtypeStruct((S, IN_WIDTH), bf16),
                   jax.ShapeDtypeStruct((1, D_MODEL), f32)],
        compiler_params=_params("arbitrary"),
    )(du, dq, dk, dv, cos_t, sin_t, w_in, x, dx2, g1)


def _matmul_tn(a, b, tn, name):
    K, M = a.shape
    N = b.shape[1]
    tk = 512

    def body(a_ref, b_ref, o_ref):
        part = _dot_tn(a_ref[...], b_ref[...])

        @pl.when(pl.program_id(1) == 0)
        def _():
            o_ref[...] = part

        @pl.when(pl.program_id(1) > 0)
        def _():
            o_ref[...] += part

    return pl.pallas_call(
        body, name=name, grid=(N // tn, K // tk),
        in_specs=[pl.BlockSpec((tk, M), lambda n, k: (k, 0)), pl.BlockSpec((tk, tn), lambda n, k: (k, n))],
        out_specs=pl.BlockSpec((M, tn), lambda n, k: (0, n)),
        out_shape=jax.ShapeDtypeStruct((M, N), f32),
        compiler_params=_params("parallel", "arbitrary"),
    )(a, b)


def _rope_tables(S):
    half = HEAD_DIM // 2
    freqs = ROPE_THETA ** (-jnp.arange(half, dtype=f32) * (2.0 / HEAD_DIM))
    ang = jnp.arange(S).astype(f32)[:, None] * freqs[None, :]
    cos = jnp.tile(jnp.cos(ang), (1, 4))
    sin = jnp.sin(ang)
    sin = jnp.tile(jnp.concatenate([-sin, sin], axis=1), (1, 2))
    return cos, sin


def _block_diag(w_pool):
    w = jnp.zeros((POOL_WIDTH, POOL_WIDTH), w_pool.dtype)
    for g in range(POOL_WIDTH // POOL_GROUP):
        w = lax.dynamic_update_slice(w, w_pool[g], (g * POOL_GROUP, g * POOL_GROUP))
    return w


def _head_ones():
    head = np.arange(ATTN_WIDTH) // HEAD_DIM
    return jnp.asarray(head[:, None] == head[None, :], dtype=bf16)


def _local_grads(x, target, g1, w_pool, pool_scale, g2, g3, g4, w_in, w_out, w_gate, w_up, w_down):
    S = x.shape[0]
    cos_t, sin_t = _rope_tables(S)
    w_bd = _block_diag(w_pool).astype(bf16)

    h1, u, qs, ks, vs = _in_proj(x, g1, w_in, cos_t, sin_t)
    pool_out = _pool_fwd(u, w_bd, pool_scale)
    attn_out, lse = _attn_fwd(qs, ks, vs)
    mix, x2, h2 = _out_proj(pool_out, attn_out, w_out, x, g2, g3)
    gate, up, f = _ffn_fwd(h2, w_gate, w_up, w_down)
    dy, df, dg4, loss = _loss_head(f, x2, target, g4)

    a, dgate, dup, dh2 = _ffn_bwd(df, gate, up, w_gate, w_up, w_down)
    d_w_down = _matmul_tn(a, df, D_MODEL, "grad_w_down")
    d_w_gate = _matmul_tn(h2, dgate, FF_HALF, "grad_w_gate")
    d_w_up = _matmul_tn(h2, dup, FF_HALF, "grad_w_up")
    dx2, dmix, dg3, dg2 = _norm_bwd(dh2, dy, x2, mix, g3, g2)
    d_w_out = jnp.concatenate([_matmul_tn(pool_out, dmix, D_MODEL, "grad_w_out_pool"),
                               _matmul_tn(attn_out, dmix, D_MODEL, "grad_w_out_attn")], axis=0)
    dpool, delta, dos = _out_proj_bwd(dmix, w_out, attn_out, _head_ones())
    du, d_w_bd, d_scale = _pool_bwd(u, dpool, w_bd, pool_scale)
    dq, dk, dv = _attn_bwd(qs, ks, vs, dos, lse, delta)
    grad_x, dproj, dg1 = _in_proj_bwd(du, dq, dk, dv, cos_t, sin_t, w_in, x, dx2, g1)
    d_w_in = _matmul_tn(h1, dproj, IN_WIDTH // 2, "grad_w_in")
    d_w_pool = jnp.stack([d_w_bd[g * POOL_GROUP:(g + 1) * POOL_GROUP, g * POOL_GROUP:(g + 1) * POOL_GROUP]
                          for g in range(POOL_WIDTH // POOL_GROUP)])
    large = dict(w_in=d_w_in, w_out=d_w_out, w_gate=d_w_gate, w_up=d_w_up, w_down=d_w_down)
    small = dict(ln_pre_mix=dg1, ln_post_mix=dg2, ln_pre_ffn=dg3, ln_post_ffn=dg4, pool_scale=d_scale, w_pool=d_w_pool)
    return loss, grad_x, large, small


def _place():
    x, y, c = lax.axis_index("x"), lax.axis_index("y"), lax.axis_index("c")
    chips = [(1 - x, y), (x, 1 - y), (1 - x, 1 - y)]
    return x, y, c, chips


ANY = pl.BlockSpec(memory_space=pl.ANY)


def _row_chunks(rows, n, unit):
    units = rows // unit
    out, start = [], 0
    for i in range(n):
        size = (units // n + (1 if i < units % n else 0)) * unit
        out.append((start, size))
        start += size
    return out


GATHER_CHUNKS = _row_chunks(HALF_ROWS, 4, 32)
SWAP_CHUNKS = _row_chunks(HALF_ROWS, 8, 32)
JOIN_CHUNKS = _row_chunks(HALF_ROWS, 16, 32)
LOCAL_COPIES = 8


class _LocalCopy:
    def __init__(self, src_rows, dst_rows, rows, buf, sems_in, sems_out):
        self.loads, self.stores = [], []
        for i, (start, size) in enumerate(_row_chunks(rows, LOCAL_COPIES, 32)):
            r = pl.ds(start, size)
            self.loads.append(pltpu.make_async_copy(src_rows(r), buf.at[r], sems_in.at[i]))
            self.stores.append(pltpu.make_async_copy(buf.at[r], dst_rows(r), sems_out.at[i]))
        for cp in self.loads:
            cp.start()

    def finish(self):
        for load, store in zip(self.loads, self.stores):
            load.wait()
            store.start()
        for store in self.stores:
            store.wait()


def _local_scratch(rows, dtype):
    return [pltpu.VMEM((rows, D_MODEL), dtype), pltpu.SemaphoreType.DMA((LOCAL_COPIES,)),
            pltpu.SemaphoreType.DMA((LOCAL_COPIES,))]


def _gather_weights(pack):
    n_ch = len(GATHER_CHUNKS)

    def body(w_ref, out_ref, send1, recv1, send2, recv2, buf, sems_in, sems_out):
        x, y, c, chips = _place()
        me = 2 * x + y
        sibling = (x, y, 1 - c)
        own = _LocalCopy(lambda r: w_ref.at[r], lambda r: out_ref.at[me, r], PACK_ROWS, buf, sems_in, sems_out)

        def rows(core, ch):
            start, size = GATHER_CHUNKS[ch]
            return pl.ds(core * HALF_ROWS + start, size)

        def direct(j, ch, chip_xy, src_chip):
            cx, cy = chip_xy
            return pltpu.make_async_remote_copy(
                src_ref=w_ref.at[rows(c, ch)], dst_ref=out_ref.at[src_chip, rows(c, ch)],
                send_sem=send1.at[j * n_ch + ch], recv_sem=recv1.at[j * n_ch + ch],
                device_id=(cx, cy, c), device_id_type=MESH)

        def passed(j, ch, chip, core):
            return pltpu.make_async_remote_copy(
                src_ref=out_ref.at[chip, rows(core, ch)], dst_ref=out_ref.at[chip, rows(core, ch)],
                send_sem=send2.at[j * n_ch + ch], recv_sem=recv2.at[j * n_ch + ch],
                device_id=sibling, device_id_type=MESH)

        sends = [direct(j, ch, chip, me) for ch in range(n_ch) for j, chip in enumerate(chips)]
        for cp in sends:
            cp.start()
        own.finish()
        forwards = []
        for ch in range(n_ch):
            for j, (cx, cy) in enumerate(chips):
                direct(j, ch, (cx, cy), 2 * cx + cy).wait_recv()
                fw = passed(j, ch, 2 * cx + cy, c)
                fw.start()
                forwards.append(fw)
        for ch in range(n_ch):
            for j, (cx, cy) in enumerate(chips):
                passed(j, ch, 2 * cx + cy, 1 - c).wait_recv()
        for cp in sends + forwards:
            cp.wait_send()

    n_sem = 3 * n_ch
    return pl.pallas_call(
        body, name="gather_weights", in_specs=[ANY], out_specs=ANY,
        out_shape=jax.ShapeDtypeStruct((N_CHIPS, PACK_ROWS, D_MODEL), pack.dtype),
        scratch_shapes=[pltpu.SemaphoreType.DMA((n_sem,))] * 4 + _local_scratch(PACK_ROWS, pack.dtype),
        compiler_params=pltpu.CompilerParams(vmem_limit_bytes=VMEM_LIMIT_V7X),
    )(pack)


def _swap_halves(g):
    n_ch = len(SWAP_CHUNKS)

    def body(g_ref, theirs_ref, send, recv):
        x, y, c, _ = _place()

        def piece(s, ch, core):
            start, size = SWAP_CHUNKS[ch]
            return pltpu.make_async_remote_copy(
                src_ref=g_ref.at[s, pl.ds(core * HALF_ROWS + start, size)], dst_ref=theirs_ref.at[s, pl.ds(start, size)],
                send_sem=send.at[s * n_ch + ch], recv_sem=recv.at[s * n_ch + ch],
                device_id=(x, y, 1 - c), device_id_type=MESH)

        copies = [piece(s, ch, 1 - c) for s in range(N_CHIPS) for ch in range(n_ch)]
        for cp in copies:
            cp.start()
        for cp in copies:
            cp.wait()

    return pl.pallas_call(
        body, name="swap_halves", in_specs=[ANY], out_specs=ANY,
        out_shape=jax.ShapeDtypeStruct((N_CHIPS, HALF_ROWS, D_MODEL), g.dtype),
        scratch_shapes=[pltpu.SemaphoreType.DMA((N_CHIPS * n_ch,))] * 2,
    )(g)


ADD_ROWS = 376


def _add_cores(g, theirs):
    n_t = HALF_ROWS // ADD_ROWS

    def body(c_ref, g_ref, t_ref, o_ref):
        o_ref[...] = g_ref[...] + t_ref[...]

    blk = pl.BlockSpec((1, ADD_ROWS, D_MODEL), lambda s, t, c_ref: (s, t, 0))
    return pl.pallas_call(
        body, name="add_cores",
        grid_spec=pltpu.PrefetchScalarGridSpec(
            num_scalar_prefetch=1, grid=(N_CHIPS, n_t),
            in_specs=[pl.BlockSpec((1, ADD_ROWS, D_MODEL), lambda s, t, c_ref: (s, c_ref[0] * n_t + t, 0)), blk],
            out_specs=blk),
        out_shape=jax.ShapeDtypeStruct(theirs.shape, theirs.dtype),
        compiler_params=_params("parallel", "parallel"),
    )(lax.axis_index("c").astype(jnp.int32).reshape(1), g, theirs)


def _scatter_to_chips(h):
    n_ch = len(GATHER_CHUNKS)

    def body(h_ref, out_ref, send, recv, buf, sems_in, sems_out):
        x, y, c, chips = _place()
        me = 2 * x + y
        own = _LocalCopy(lambda r: h_ref.at[me, r], lambda r: out_ref.at[me, r], HALF_ROWS, buf, sems_in, sems_out)

        def piece(j, ch, chip_xy, dst_chip, src_chip):
            cx, cy = chip_xy
            start, size = GATHER_CHUNKS[ch]
            return pltpu.make_async_remote_copy(
                src_ref=h_ref.at[dst_chip, pl.ds(start, size)], dst_ref=out_ref.at[src_chip, pl.ds(start, size)],
                send_sem=send.at[j * n_ch + ch], recv_sem=recv.at[j * n_ch + ch],
                device_id=(cx, cy, c), device_id_type=MESH)

        sends = [piece(j, ch, (cx, cy), 2 * cx + cy, me) for ch in range(n_ch) for j, (cx, cy) in enumerate(chips)]
        for cp in sends:
            cp.start()
        own.finish()
        for ch in range(n_ch):
            for j, (cx, cy) in enumerate(chips):
                piece(j, ch, (cx, cy), me, 2 * cx + cy).wait_recv()
        for cp in sends:
            cp.wait_send()

    return pl.pallas_call(
        body, name="scatter_to_chips", in_specs=[ANY], out_specs=ANY,
        out_shape=jax.ShapeDtypeStruct(h.shape, h.dtype),
        scratch_shapes=[pltpu.SemaphoreType.DMA((3 * n_ch,))] * 2 + _local_scratch(HALF_ROWS, h.dtype),
        compiler_params=pltpu.CompilerParams(vmem_limit_bytes=VMEM_LIMIT_V7X),
    )(h)


def _join_halves(r):
    n_ch = len(JOIN_CHUNKS)

    def body(r_ref, out_ref, send, recv, buf, sems_in, sems_out):
        x, y, c, _ = _place()
        own = _LocalCopy(lambda r: r_ref.at[r], lambda r: out_ref.at[c, r], HALF_ROWS, buf, sems_in, sems_out)

        def piece(ch, core):
            start, size = JOIN_CHUNKS[ch]
            return pltpu.make_async_remote_copy(
                src_ref=r_ref.at[pl.ds(start, size)], dst_ref=out_ref.at[core, pl.ds(start, size)],
                send_sem=send.at[ch], recv_sem=recv.at[ch], device_id=(x, y, 1 - c), device_id_type=MESH)

        copies = [piece(ch, c) for ch in range(n_ch)]
        for cp in copies:
            cp.start()
        own.finish()
        for ch in range(n_ch):
            piece(ch, 1 - c).wait_recv()
        for cp in copies:
            cp.wait_send()

    return pl.pallas_call(
        body, name="join_halves", in_specs=[ANY], out_specs=ANY,
        out_shape=jax.ShapeDtypeStruct((2,) + r.shape, r.dtype),
        scratch_shapes=[pltpu.SemaphoreType.DMA((n_ch,))] * 2 + _local_scratch(HALF_ROWS, r.dtype),
        compiler_params=pltpu.CompilerParams(vmem_limit_bytes=VMEM_LIMIT_V7X),
    )(r)


def _add_slabs(terms, k, name):
    rows = terms[0][0].shape[1]
    tr = ADD_ROWS
    n = len(terms)

    def body(*refs):
        acc = refs[0][...]
        for r in refs[1:n]:
            acc = acc + r[...]
        refs[n][...] = acc

    slab = lambda first: pl.BlockSpec((1, tr, D_MODEL), lambda i, t: (first + i, t, 0))
    return pl.pallas_call(
        body, name=name, grid=(k, rows // tr), in_specs=[slab(first) for _, first in terms],
        out_specs=pl.BlockSpec((1, tr, D_MODEL), lambda i, t: (i, t, 0)),
        out_shape=jax.ShapeDtypeStruct((k, rows, D_MODEL), terms[0][0].dtype),
        compiler_params=_params("parallel", "parallel"),
    )(*[a for a, _ in terms])


def _sum_small(block):
    def body(b_ref, out_ref, gathered, send, recv):
        x, y, c, _ = _place()
        me = 4 * x + 2 * y + c
        gathered[me] = b_ref[...]
        sends = []
        for kk in range(1, N_DEV):
            flip = lambda v, bit: 1 - v if bit else v
            peer = (flip(x, kk & 4), flip(y, kk & 2), flip(c, kk & 1))
            cp = pltpu.make_async_remote_copy(
                src_ref=b_ref, dst_ref=gathered.at[me], send_sem=send.at[kk - 1], recv_sem=recv.at[kk - 1],
                device_id=peer, device_id_type=MESH)
            cp.start()
            sends.append(cp)
        for kk in range(1, N_DEV):
            peer_index = jnp.bitwise_xor(me, kk)
            pltpu.make_async_remote_copy(
                src_ref=b_ref, dst_ref=gathered.at[peer_index], send_sem=send.at[kk - 1], recv_sem=recv.at[kk - 1],
                device_id=(x, y, c), device_id_type=MESH).wait_recv()
        for cp in sends:
            cp.wait_send()
        acc = gathered[0]
        for dev in range(1, N_DEV):
            acc = acc + gathered[dev]
        out_ref[...] = acc

    vmem = pl.BlockSpec(memory_space=pltpu.VMEM)
    return pl.pallas_call(
        body, name="sum_small", in_specs=[vmem], out_specs=vmem,
        out_shape=jax.ShapeDtypeStruct(block.shape, block.dtype),
        scratch_shapes=[pltpu.VMEM((N_DEV,) + block.shape, block.dtype),
                        pltpu.SemaphoreType.DMA((N_DEV - 1,)), pltpu.SemaphoreType.DMA((N_DEV - 1,))],
    )(block)


def _adamw(w, g, m, v, name):
    rows, cols = w.shape
    tr = rows
    for cand in (512, 256, 128, 64, 32, 16, 8):
        if rows % cand == 0:
            tr = cand
            break
    c1 = 1.0 - ADAM_B1 ** ADAM_STEP
    c2 = 1.0 - ADAM_B2 ** ADAM_STEP

    def body(w_ref, g_ref, m_ref, v_ref, d_ref, nm_ref, nv_ref):
        gv = g_ref[...]
        nm = ADAM_B1 * m_ref[...] + (1.0 - ADAM_B1) * gv
        nv = ADAM_B2 * v_ref[...] + (1.0 - ADAM_B2) * (gv * gv)
        nm_ref[...] = nm
        nv_ref[...] = nv
        d_ref[...] = -ADAM_LR * ((nm / c1) / (jnp.sqrt(nv / c2) + ADAM_EPS) + ADAM_WD * w_ref[...])

    blk = pl.BlockSpec((tr, cols), lambda i: (i, 0))
    shape = jax.ShapeDtypeStruct((rows, cols), f32)
    return pl.pallas_call(
        body, name=name, grid=(rows // tr,), in_specs=[blk] * 4, out_specs=[blk] * 3, out_shape=[shape] * 3,
        compiler_params=_params("parallel"),
    )(w, g, m, v)


LARGE = ("w_in", "w_out", "w_gate", "w_up", "w_down")
SMALL = ("ln_pre_mix", "ln_post_mix", "ln_pre_ffn", "ln_post_ffn", "pool_scale", "w_pool")
COLUMN_SHARDED = {"w_in": IN_WIDTH // N_CHIPS, "w_gate": D_FF // N_CHIPS, "w_up": D_FF // N_CHIPS}


def _pack_shard(shards):
    return jnp.concatenate([shards[n].reshape(-1, D_MODEL) for n in LARGE], axis=0)


def _unpack_shard(pack, shapes):
    out, row = {}, 0
    for n, rows in zip(LARGE, PACK_SPLITS):
        out[n] = pack[row:row + rows].reshape(shapes[n])
        row += rows
    return out


def _whole_from_shards(packs):
    out, row = {}, 0
    for n, rows in zip(LARGE, PACK_SPLITS):
        part = packs[:, row:row + rows]
        if n in COLUMN_SHARDED:
            width = COLUMN_SHARDED[n]
            part = part.reshape(N_CHIPS, D_MODEL, width).transpose(1, 0, 2).reshape(D_MODEL, N_CHIPS * width)
        else:
            part = part.reshape(N_CHIPS * rows, D_MODEL)
        out[n] = part
        row += rows
    return out


def _shards_from_whole(grads):
    parts = []
    for n, rows in zip(LARGE, PACK_SPLITS):
        g = grads[n]
        if n in COLUMN_SHARDED:
            width = COLUMN_SHARDED[n]
            g = g.reshape(D_MODEL, N_CHIPS, width).transpose(1, 0, 2)
        parts.append(g.reshape(N_CHIPS, rows, D_MODEL))
    return jnp.concatenate(parts, axis=1)


def _pack_small(vals):
    rows = [vals[n].reshape(1, D_MODEL) for n in SMALL[:4]]
    rows.append(jnp.pad(vals["pool_scale"].reshape(1, POOL_WIDTH), ((0, 0), (0, D_MODEL - POOL_WIDTH))))
    rows.append(jnp.pad(vals["loss"].reshape(1, 1), ((0, 0), (0, D_MODEL - 1))))
    rows.append(jnp.zeros((2, D_MODEL), f32))
    rows.append(vals["w_pool"].reshape(16, D_MODEL))
    return jnp.concatenate(rows, axis=0)


def _unpack_small(block):
    out = {n: block[i:i + 1] for i, n in enumerate(SMALL[:4])}
    out["pool_scale"] = block[4:5, :POOL_WIDTH]
    out["loss"] = block[5, 0]
    out["w_pool"] = block[8:24].reshape(1, 4, POOL_GROUP, POOL_GROUP)
    return out


def kernel(x, ln_pre_mix, w_in, w_pool, pool_scale, w_out, ln_post_mix, ln_pre_ffn, w_gate, w_up, w_down, ln_post_ffn, loss_target, m_ln_pre_mix, m_w_in, m_w_pool, m_pool_scale, m_w_out, m_ln_post_mix, m_ln_pre_ffn, m_w_gate, m_w_up, m_w_down, m_ln_post_ffn, v_ln_pre_mix, v_w_in, v_w_pool, v_pool_scale, v_w_out, v_ln_post_mix, v_ln_pre_ffn, v_w_gate, v_w_up, v_w_down, v_ln_post_ffn):
    w = dict(ln_pre_mix=ln_pre_mix, w_in=w_in, w_pool=w_pool, pool_scale=pool_scale, w_out=w_out,
             ln_post_mix=ln_post_mix, ln_pre_ffn=ln_pre_ffn, w_gate=w_gate, w_up=w_up, w_down=w_down,
             ln_post_ffn=ln_post_ffn)
    m = dict(ln_pre_mix=m_ln_pre_mix, w_in=m_w_in, w_pool=m_w_pool, pool_scale=m_pool_scale, w_out=m_w_out,
             ln_post_mix=m_ln_post_mix, ln_pre_ffn=m_ln_pre_ffn, w_gate=m_w_gate, w_up=m_w_up, w_down=m_w_down,
             ln_post_ffn=m_ln_post_ffn)
    v = dict(ln_pre_mix=v_ln_pre_mix, w_in=v_w_in, w_pool=v_w_pool, pool_scale=v_pool_scale, w_out=v_w_out,
             ln_post_mix=v_ln_post_mix, ln_pre_ffn=v_ln_pre_ffn, w_gate=v_w_gate, w_up=v_w_up, w_down=v_w_down,
             ln_post_ffn=v_ln_post_ffn)

    packs = _gather_weights(_pack_shard({n: w[n][0].astype(bf16) for n in LARGE}))
    whole = _whole_from_shards(packs)

    loss, grad_x, large, small = _local_grads(
        x[0], loss_target[0], ln_pre_mix, w_pool[0], pool_scale, ln_post_mix, ln_pre_ffn, ln_post_ffn,
        whole["w_in"], whole["w_out"], whole["w_gate"], whole["w_up"], whole["w_down"])

    shard_major = _shards_from_whole(large)
    chip_sum = _add_cores(shard_major, _swap_halves(shard_major))
    pieces = _scatter_to_chips(chip_sum)
    reduced_half = _add_slabs([(pieces, j) for j in range(N_CHIPS)], 1, "add_chips")[0]
    reduced = _join_halves(reduced_half).reshape(PACK_ROWS, D_MODEL)
    shapes = {n: w[n].shape[1:] for n in LARGE}
    grads = _unpack_shard(reduced, shapes)

    total = _unpack_small(_sum_small(_pack_small(dict(small, loss=loss))))
    for n in SMALL:
        grads[n] = total[n]

    delta, new_m, new_v = {}, {}, {}
    for n in LARGE:
        delta[n], new_m[n], new_v[n] = _adamw(w[n][0], grads[n], m[n][0], v[n][0], "adamw_" + n)
    small_state = [_pack_small(dict({n: s[n] for n in SMALL}, loss=jnp.zeros((), f32))) for s in (w, m, v)]
    small_grad = _pack_small(dict({n: grads[n] for n in SMALL}, loss=jnp.zeros((), f32)))
    sd, sm, sv = _adamw(small_state[0], small_grad, small_state[1], small_state[2], "adamw_small")
    for out, block in ((delta, sd), (new_m, sm), (new_v, sv)):
        un = _unpack_small(block)
        for n in SMALL:
            out[n] = un[n]

    names = ("ln_pre_mix", "w_in", "w_pool", "pool_scale", "w_out", "ln_post_mix", "ln_pre_ffn", "w_gate", "w_up",
             "w_down", "ln_post_ffn")
    full = lambda d: [d[n].reshape(w[n].shape) for n in names]
    return (total["loss"], grad_x[None], *full(grads), *full(delta), *full(new_m), *full(new_v))
```

```python
import numpy as np
import jax
import jax.numpy as jnp
from jax import lax
from jax.experimental import pallas as pl
from jax.experimental.pallas import tpu as pltpu

D_MODEL = 1024
POOL_WIDTH = 256
POOL_GROUP = 64
ATTN_WIDTH = 768
HEAD_DIM = 64
IN_WIDTH = 2560
D_FF = 2816
BLOCK = 128
DILATIONS = (1, 4, 16)
ROPE_THETA = 10000.0
EPS = 1e-6
ATTN_SCALE = 0.125
NEG = -1e30

ADAM_LR = 0.001
ADAM_B1 = 0.9
ADAM_B2 = 0.999
ADAM_EPS = 1e-08
ADAM_WD = 0.01
ADAM_STEP = 10

N_CHIPS = 4
N_DEV = 8
PACK_SPLITS = (640, 256, 704, 704, 704)
PACK_ROWS = sum(PACK_SPLITS)
HALF_ROWS = PACK_ROWS // 2
SMALL_ROWS = 24

VMEM_LIMIT_V7X = 56 * 1024 * 1024
MESH = pl.DeviceIdType.MESH

f32 = jnp.float32
bf16 = jnp.bfloat16


def _params(*sem):
    return pltpu.CompilerParams(dimension_semantics=sem, vmem_limit_bytes=VMEM_LIMIT_V7X)


def _dot(a, b):
    return jnp.dot(a, b, preferred_element_type=f32)


def _dot_nt(a, b):
    return lax.dot_general(a, b, (((1,), (1,)), ((), ())), preferred_element_type=f32)


def _dot_tn(a, b):
    return lax.dot_general(a, b, (((0,), (0,)), ((), ())), preferred_element_type=f32)


def _rope_partner(a, first_half):
    return jnp.where(first_half, pltpu.roll(a, 96, 1), pltpu.roll(a, 32, 1))


def _first_half_mask(rows):
    lane = lax.broadcasted_iota(jnp.int32, (rows, 128), 1)
    return (lane % HEAD_DIM) < (HEAD_DIM // 2)


def _stream_spec(d, ts):
    return pl.BlockSpec((d, ts // d, ATTN_WIDTH), lambda i: (0, i, 0))


def _stream_shape(S, d):
    return jax.ShapeDtypeStruct((d, S // d, ATTN_WIDTH), bf16)


N_STAGE = ATTN_WIDTH // 128


def _stage_scratch(ts):
    return [pltpu.VMEM((ts, 128), f32)] * N_STAGE


def _store_streams(stage, out_refs, ts):
    for d, ref in zip(DILATIONS, out_refs):
        for r in range(d):
            rows = pl.ds(0, ts) if d == 1 else pl.ds(r, ts // d, stride=d)
            for j in range(N_STAGE):
                ref[r, :, j * 128:(j + 1) * 128] = stage[j][rows, :].astype(bf16)


def _in_proj(x, g1, w_in, cos_t, sin_t):
    S = x.shape[0]
    ts = 512

    def body(x_ref, g_ref, w_ref, cos_ref, sin_ref, h_ref, u_ref, *rest):
        outs, stage = rest[:-N_STAGE], rest[-N_STAGE:]
        xv = x_ref[...]
        r = lax.rsqrt(jnp.mean(xv * xv, axis=-1, keepdims=True) + EPS)
        h = ((xv * r) * g_ref[...]).astype(bf16)
        h_ref[...] = h
        proj = _dot(h, w_ref[...])
        u_ref[...] = proj[:, :POOL_WIDTH]
        cos = cos_ref[...]
        sin = sin_ref[...]
        first = _first_half_mask(ts)
        n_dil = len(DILATIONS)
        for which, base in enumerate((POOL_WIDTH, POOL_WIDTH + ATTN_WIDTH)):
            for j in range(ATTN_WIDTH // 128):
                a = proj[:, base + j * 128: base + (j + 1) * 128]
                stage[j][...] = a * cos + _rope_partner(a, first) * sin
            _store_streams(stage, outs[which * n_dil:(which + 1) * n_dil], ts)
        for j in range(ATTN_WIDTH // 128):
            base = POOL_WIDTH + 2 * ATTN_WIDTH + j * 128
            stage[j][...] = proj[:, base:base + 128]
        _store_streams(stage, outs[2 * n_dil:], ts)

    row = lambda w: pl.BlockSpec((ts, w), lambda i: (i, 0))
    streams = [_stream_spec(d, ts) for d in DILATIONS] * 3
    res = pl.pallas_call(
        body, name="in_proj", grid=(S // ts,),
        in_specs=[row(D_MODEL), pl.BlockSpec((1, D_MODEL), lambda i: (0, 0)),
                  pl.BlockSpec((D_MODEL, IN_WIDTH), lambda i: (0, 0)), row(128), row(128)],
        out_specs=[row(D_MODEL), row(POOL_WIDTH)] + streams,
        out_shape=[jax.ShapeDtypeStruct((S, D_MODEL), bf16), jax.ShapeDtypeStruct((S, POOL_WIDTH), f32)]
        + [_stream_shape(S, d) for d in DILATIONS] * 3,
        scratch_shapes=_stage_scratch(ts),
        compiler_params=_params("parallel"),
    )(x, g1, w_in, cos_t, sin_t)
    n = len(DILATIONS)
    return res[0], res[1], res[2:2 + n], res[2 + n:2 + 2 * n], res[2 + 2 * n:]


POOL_HALO = 16


def _pool_lane_group(rows):
    return lax.broadcasted_iota(jnp.int32, (rows, POOL_WIDTH), 1) // POOL_GROUP


def _pool_select(group, s2, s4, s8, s16):
    return jnp.where(group == 0, s2, jnp.where(group == 1, s4, jnp.where(group == 2, s8, s16)))


def _pool_count(t0, rows):
    group = _pool_lane_group(rows)
    t = t0 + lax.broadcasted_iota(jnp.int32, (rows, POOL_WIDTH), 0)
    win = _pool_select(group, 2, 4, 8, 16)
    return jnp.minimum(t + 1, win).astype(f32)


def _pool_diff(u_halo, u_tile, t0):
    ts = u_tile.shape[0]
    ext = jnp.concatenate([u_halo, u_tile], axis=0)
    s2 = ext + pltpu.roll(ext, 1, 0)
    s4 = s2 + pltpu.roll(s2, 2, 0)
    s8 = s4 + pltpu.roll(s4, 4, 0)
    s16 = s8 + pltpu.roll(s8, 8, 0)
    group = _pool_lane_group(ts + POOL_HALO)
    wsum = _pool_select(group, s2, s4, s8, s16)[POOL_HALO:]
    return wsum / _pool_count(t0, ts) - u_tile


def _pool_specs(ts, n_tiles):
    tile = pl.BlockSpec((ts, POOL_WIDTH), lambda i: (i, 0))
    per = ts // POOL_HALO
    before = pl.BlockSpec((POOL_HALO, POOL_WIDTH), lambda i: (jnp.maximum(i * per - 1, 0), 0))
    after = pl.BlockSpec((POOL_HALO, POOL_WIDTH), lambda i: (jnp.minimum((i + 1) * per, n_tiles * per - 1), 0))
    return tile, before, after


def _pool_fwd(u, w_bd, scale):
    S = u.shape[0]
    ts = 512
    n_tiles = S // ts

    def body(u_ref, halo_ref, w_ref, sc_ref, y_ref):
        i = pl.program_id(0)
        halo = jnp.where(i > 0, halo_ref[...], 0.0)
        d = _pool_diff(halo, u_ref[...], i * ts)
        y_ref[...] = (_dot(d.astype(bf16), w_ref[...]) * sc_ref[...]).astype(bf16)

    tile, before, _ = _pool_specs(ts, n_tiles)
    return pl.pallas_call(
        body, name="pool_fwd", grid=(n_tiles,),
        in_specs=[tile, before, pl.BlockSpec((POOL_WIDTH, POOL_WIDTH), lambda i: (0, 0)),
                  pl.BlockSpec((1, POOL_WIDTH), lambda i: (0, 0))],
        out_specs=tile, out_shape=jax.ShapeDtypeStruct((S, POOL_WIDTH), bf16),
        compiler_params=_params("parallel"),
    )(u, u, w_bd, scale)


def _pool_bwd(u, dy, w_bd, scale):
    S = u.shape[0]
    ts = 512
    n_tiles = S // ts

    def body(u_ref, halo_ref, dy_ref, dy_next_ref, w_ref, sc_ref, du_ref, dw_ref, dsc_ref):
        i = pl.program_id(0)

        @pl.when(i == 0)
        def _():
            dw_ref[...] = jnp.zeros_like(dw_ref)
            dsc_ref[...] = jnp.zeros_like(dsc_ref)

        halo = jnp.where(i > 0, halo_ref[...], 0.0)
        d = _pool_diff(halo, u_ref[...], i * ts).astype(bf16)
        w = w_ref[...]
        sc = sc_ref[...]
        dy_tile = dy_ref[...]
        z = _dot(d, w)
        dsc_ref[...] += jnp.sum(dy_tile * z, axis=0, keepdims=True)
        dy_next = jnp.where(i < n_tiles - 1, dy_next_ref[...], 0.0)
        dz = (jnp.concatenate([dy_tile, dy_next], axis=0) * sc).astype(bf16)
        dw_ref[...] += _dot_tn(d, dz[:ts])
        dd = _dot_nt(dz, w)
        e = dd / _pool_count(i * ts, ts + POOL_HALO)
        n = ts + POOL_HALO
        f2 = e + pltpu.roll(e, n - 1, 0)
        f4 = f2 + pltpu.roll(f2, n - 2, 0)
        f8 = f4 + pltpu.roll(f4, n - 4, 0)
        f16 = f8 + pltpu.roll(f8, n - 8, 0)
        fsum = _pool_select(_pool_lane_group(n), f2, f4, f8, f16)
        du_ref[...] = (fsum[:ts] - dd[:ts]).astype(bf16)

    tile, before, after = _pool_specs(ts, n_tiles)
    return pl.pallas_call(
        body, name="pool_bwd", grid=(n_tiles,),
        in_specs=[tile, before, tile, after, pl.BlockSpec((POOL_WIDTH, POOL_WIDTH), lambda i: (0, 0)),
                  pl.BlockSpec((1, POOL_WIDTH), lambda i: (0, 0))],
        out_specs=[tile, pl.BlockSpec((POOL_WIDTH, POOL_WIDTH), lambda i: (0, 0)),
                   pl.BlockSpec((1, POOL_WIDTH), lambda i: (0, 0))],
        out_shape=[jax.ShapeDtypeStruct((S, POOL_WIDTH), bf16), jax.ShapeDtypeStruct((POOL_WIDTH, POOL_WIDTH), f32),
                   jax.ShapeDtypeStruct((1, POOL_WIDTH), f32)],
        compiler_params=_params("arbitrary"),
    )(u, u, dy, dy, w_bd, scale)


SUPER = BLOCK * DILATIONS[-1]
UNITS = SUPER // BLOCK
FWD_UNROLL = 4
BWD_UNROLL = 4


def _band_mask(has_prev):
    qi = lax.broadcasted_iota(jnp.int32, (BLOCK, 2 * BLOCK), 0)
    kj = lax.broadcasted_iota(jnp.int32, (BLOCK, 2 * BLOCK), 1)
    return (kj >= qi) & (kj <= qi + BLOCK) & ((kj >= BLOCK) | has_prev)


def _head0_mask(rows=BLOCK):
    return lax.broadcasted_iota(jnp.int32, (rows, 128), 1) < HEAD_DIM


def _per_head(stat, h0, h):
    other = pltpu.roll(stat, HEAD_DIM, 1)
    full = jnp.where(h0, stat, other) if h == 0 else jnp.where(h0, other, stat)
    return jnp.concatenate([full, full], axis=1)


def _band_mask_t(has_prev):
    ki = lax.broadcasted_iota(jnp.int32, (2 * BLOCK, 2 * BLOCK), 0)
    qj = lax.broadcasted_iota(jnp.int32, (2 * BLOCK, 2 * BLOCK), 1) % BLOCK
    return (ki >= qj) & (ki <= qj + BLOCK) & ((ki >= BLOCK) | has_prev)


def _head_pair_rows(a, h0):
    zero = jnp.zeros_like(a)
    return jnp.concatenate([jnp.where(h0, a, zero), jnp.where(h0, zero, a)], axis=0)


def _per_query_row(stat):
    t = stat.T
    return jnp.concatenate([jnp.concatenate([t[:HEAD_DIM]] * 4, axis=0), jnp.concatenate([t[HEAD_DIM:]] * 4, axis=0)],
                           axis=1)


def _natural_rows(d, r, n):
    if d == 1:
        return pl.ds(pl.multiple_of(n * BLOCK, BLOCK), BLOCK)
    return pl.ds(n * (BLOCK * d) + r, BLOCK, stride=d)


def _unit_place(d, u):
    per_stream = UNITS // d
    return u // per_stream, u % per_stream, per_stream


def _block_rows(n):
    return pl.ds(pl.multiple_of(n * BLOCK, BLOCK), BLOCK)


def _band(cur_ref, tail_ref, r, n):
    before = jnp.where(n > 0, cur_ref[r, _block_rows(jnp.maximum(n - 1, 0)), :], tail_ref[r])
    return jnp.concatenate([before, cur_ref[r, _block_rows(n), :]], axis=0)


def _attn_in_specs(S, with_do):
    specs = []
    last = S // SUPER - 1
    for d in DILATIONS:
        per_stream = UNITS // d
        cur = pl.BlockSpec((d, SUPER // d, 128), lambda hp, sb: (0, jnp.minimum(sb, last), hp))
        tail = pl.BlockSpec(
            (d, BLOCK, 128),
            lambda hp, sb, per_stream=per_stream: (0, jnp.maximum(jnp.minimum(sb, last) * per_stream - 1, 0), hp))
        specs += [cur] * (2 if with_do else 1) + [cur, tail, cur, tail]
    return specs


def _attn_fwd(qs, ks, vs):
    S = qs[0].shape[1]
    n_dil = len(DILATIONS)

    def body(*refs):
        ins, (out_ref, lse_ref) = refs[:-2 - 2 * n_dil], refs[-2 - 2 * n_dil:-2 * n_dil]
        o_sc, l_sc = refs[-2 * n_dil:-n_dil], refs[-n_dil:]
        sb = pl.program_id(1)
        h0 = _head0_mask()
        for ci, d in enumerate(DILATIONS):
            q_ref, kc_ref, kp_ref, vc_ref, vp_ref = ins[5 * ci:5 * ci + 5]

            def unit(u, carry, d=d, ci=ci, q_ref=q_ref, kc_ref=kc_ref, kp_ref=kp_ref, vc_ref=vc_ref, vp_ref=vp_ref):
                r, n, _ = _unit_place(d, u)
                qv = q_ref[r, _block_rows(n), :]
                kb = _band(kc_ref, kp_ref, r, n)
                vb = _band(vc_ref, vp_ref, r, n)
                valid = _band_mask((sb > 0) | (n > 0))
                outs, lses = [], []
                for h in range(2):
                    keep = h0 if h == 0 else jnp.logical_not(h0)
                    qh = jnp.where(keep, qv, jnp.zeros_like(qv))
                    s = jnp.where(valid, _dot_nt(qh, kb) * ATTN_SCALE, NEG)
                    m = jnp.max(s, axis=1, keepdims=True)
                    e = jnp.exp(s - m)
                    den = jnp.sum(e, axis=1, keepdims=True)
                    outs.append(_dot((e / den).astype(bf16), vb))
                    lses.append(jnp.broadcast_to(m + jnp.log(den), (BLOCK, 128)))
                rows = _natural_rows(d, r, n)
                o_sc[ci][rows, :] = jnp.where(h0, outs[0], outs[1])
                l_sc[ci][rows, :] = jnp.where(h0, lses[0], lses[1])
                return carry

            lax.fori_loop(0, UNITS, unit, 0, unroll=FWD_UNROLL)

        def merge(t, carry):
            rows = pl.ds(pl.multiple_of(t * 256, 256), 256)
            a, b, c = l_sc[0][rows, :], l_sc[1][rows, :], l_sc[2][rows, :]
            m = jnp.maximum(jnp.maximum(a, b), c)
            ea, eb, ec = jnp.exp(a - m), jnp.exp(b - m), jnp.exp(c - m)
            tot = ea + eb + ec
            out_ref[rows, :] = ((ea / tot) * o_sc[0][rows, :] + (eb / tot) * o_sc[1][rows, :]
                                + (ec / tot) * o_sc[2][rows, :]).astype(bf16)
            lse_ref[rows, :] = m + jnp.log(tot)
            return carry

        lax.fori_loop(0, SUPER // 256, merge, 0)

    args = []
    for q, k, v in zip(qs, ks, vs):
        args += [q, k, k, v, v]
    nat = pl.BlockSpec((SUPER, 128), lambda hp, sb: (sb, hp))
    return pl.pallas_call(
        body, name="attn_fwd", grid=(ATTN_WIDTH // 128, S // SUPER),
        in_specs=_attn_in_specs(S, False), out_specs=[nat, nat],
        out_shape=[jax.ShapeDtypeStruct((S, ATTN_WIDTH), bf16), jax.ShapeDtypeStruct((S, ATTN_WIDTH), f32)],
        scratch_shapes=[pltpu.VMEM((SUPER, 128), f32)] * (2 * n_dil),
        compiler_params=_params("parallel", "arbitrary"),
    )(*args)


def _attn_bwd(qs, ks, vs, dos, lse, delta):
    S = qs[0].shape[1]
    n_steps = S // SUPER
    last = n_steps - 1

    def body(*refs):
        ins, (lse_ref, dl_ref, dq_ref, dk_ref, dv_ref, dq_acc, dk_acc, dv_acc) = refs[:-8], refs[-8:]
        sb = pl.program_id(1)
        cur = sb % 2
        prv = 1 - cur

        @pl.when(sb < n_steps)
        def _():
            dq_acc[...] = jnp.zeros_like(dq_acc)
            dk_acc[cur] = jnp.zeros((SUPER, 128), f32)
            dv_acc[cur] = jnp.zeros((SUPER, 128), f32)
            h0 = _head0_mask()
            for ci, d in enumerate(DILATIONS):
                q_ref, do_ref, kc_ref, kp_ref, vc_ref, vp_ref = ins[6 * ci:6 * ci + 6]

                def unit(u, carry, d=d, q_ref=q_ref, do_ref=do_ref, kc_ref=kc_ref, kp_ref=kp_ref, vc_ref=vc_ref,
                         vp_ref=vp_ref):
                    r, n, per_stream = _unit_place(d, u)
                    qv = q_ref[r, _block_rows(n), :]
                    dov = do_ref[r, _block_rows(n), :]
                    kb = _band(kc_ref, kp_ref, r, n)
                    vb = _band(vc_ref, vp_ref, r, n)
                    rows = _natural_rows(d, r, n)
                    has_prev = (sb > 0) | (n > 0)
                    q_pair = _head_pair_rows(qv, h0)
                    do_pair = _head_pair_rows(dov, h0)
                    s_t = jnp.where(_band_mask_t(has_prev), _dot_nt(kb, q_pair) * ATTN_SCALE, NEG)
                    p_t = jnp.exp(s_t - _per_query_row(lse_ref[rows, :]))
                    dp_t = _dot_nt(vb, do_pair)
                    ds_t = (p_t * (dp_t - _per_query_row(dl_ref[rows, :])) * ATTN_SCALE).astype(bf16)
                    dvb = _dot(p_t.astype(bf16), do_pair)
                    dkb = _dot(ds_t, q_pair)
                    dq_pair = _dot_tn(ds_t, kb)
                    dq_acc[rows, :] += jnp.where(h0, dq_pair[:BLOCK], dq_pair[BLOCK:])
                    dk_acc[cur, rows, :] += dkb[BLOCK:]
                    dv_acc[cur, rows, :] += dvb[BLOCK:]

                    slot = jnp.where((n > 0) | (sb == 0), cur, prv)
                    before = _natural_rows(d, r, jnp.where(n > 0, n - 1, per_stream - 1))
                    dk_acc[slot, before, :] += dkb[:BLOCK]
                    dv_acc[slot, before, :] += dvb[:BLOCK]
                    return carry

                lax.fori_loop(0, UNITS, unit, 0, unroll=BWD_UNROLL)
            dq_ref[...] = dq_acc[...].astype(bf16)

        @pl.when(sb > 0)
        def _():
            dk_ref[...] = dk_acc[prv].astype(bf16)
            dv_ref[...] = dv_acc[prv].astype(bf16)

    args = []
    for q, k, v, do in zip(qs, ks, vs, dos):
        args += [q, do, k, k, v, v]
    nat = pl.BlockSpec((SUPER, 128), lambda hp, sb: (jnp.minimum(sb, last), hp))
    nat_before = pl.BlockSpec((SUPER, 128), lambda hp, sb: (jnp.clip(sb - 1, 0, last), hp))
    out = jax.ShapeDtypeStruct((S, ATTN_WIDTH), bf16)
    return pl.pallas_call(
        body, name="attn_bwd", grid=(ATTN_WIDTH // 128, n_steps + 1),
        in_specs=_attn_in_specs(S, True) + [nat, nat], out_specs=[nat, nat_before, nat_before],
        out_shape=[out, out, out],
        scratch_shapes=[pltpu.VMEM((SUPER, 128), f32), pltpu.VMEM((2, SUPER, 128), f32),
                        pltpu.VMEM((2, SUPER, 128), f32)],
        compiler_params=_params("parallel", "arbitrary"),
    )(*args, lse, delta)


def _rms(v):
    return lax.rsqrt(jnp.mean(v * v, axis=-1, keepdims=True) + EPS)


def _out_proj(pool_out, attn_out, w_out, x, g2, g3):
    S = x.shape[0]
    ts = 512

    def body(p_ref, a_ref, w_ref, x_ref, g2_ref, g3_ref, mix_ref, x2_ref, h2_ref):
        mix = _dot(p_ref[...], w_ref[:POOL_WIDTH, :]) + _dot(a_ref[...], w_ref[POOL_WIDTH:, :])
        mix_ref[...] = mix
        x2 = x_ref[...] + (mix * _rms(mix)) * g2_ref[...]
        x2_ref[...] = x2
        h2_ref[...] = ((x2 * _rms(x2)) * g3_ref[...]).astype(bf16)

    row = lambda w: pl.BlockSpec((ts, w), lambda i: (i, 0))
    gain = pl.BlockSpec((1, D_MODEL), lambda i: (0, 0))
    return pl.pallas_call(
        body, name="out_proj", grid=(S // ts,),
        in_specs=[row(POOL_WIDTH), row(ATTN_WIDTH), pl.BlockSpec((D_MODEL, D_MODEL), lambda i: (0, 0)),
                  row(D_MODEL), gain, gain],
        out_specs=[row(D_MODEL)] * 3,
        out_shape=[jax.ShapeDtypeStruct((S, D_MODEL), f32), jax.ShapeDtypeStruct((S, D_MODEL), f32),
                   jax.ShapeDtypeStruct((S, D_MODEL), bf16)],
        compiler_params=_params("parallel"),
    )(pool_out, attn_out, w_out, x, g2, g3)


FF_TILE = 256
FF_HALF = D_FF // 2


def _sigmoid(g):
    return 1.0 / (1.0 + jnp.exp(-g))


def _ffn_fwd(h2, w_gate, w_up, w_down):
    S = h2.shape[0]
    ts = 1024

    def body(h_ref, wg_ref, wu_ref, wd_ref, gate_ref, up_ref, f_ref):
        j = pl.program_id(1)
        h = h_ref[...]
        gate = _dot(h, wg_ref[...])
        up = _dot(h, wu_ref[...])
        gate_ref[...] = gate.astype(bf16)
        up_ref[...] = up.astype(bf16)
        part = _dot((gate * _sigmoid(gate) * up).astype(bf16), wd_ref[...])

        @pl.when(j == 0)
        def _():
            f_ref[...] = part

        @pl.when(j > 0)
        def _():
            f_ref[...] += part

    act = pl.BlockSpec((ts, FF_TILE), lambda i, j: (i, j))
    return pl.pallas_call(
        body, name="ffn_fwd", grid=(S // ts, D_FF // FF_TILE),
        in_specs=[pl.BlockSpec((ts, D_MODEL), lambda i, j: (i, 0)),
                  pl.BlockSpec((D_MODEL, FF_TILE), lambda i, j: (0, j)),
                  pl.BlockSpec((D_MODEL, FF_TILE), lambda i, j: (0, j)),
                  pl.BlockSpec((FF_TILE, D_MODEL), lambda i, j: (j, 0))],
        out_specs=[act, act, pl.BlockSpec((ts, D_MODEL), lambda i, j: (i, 0))],
        out_shape=[jax.ShapeDtypeStruct((S, D_FF), bf16), jax.ShapeDtypeStruct((S, D_FF), bf16),
                   jax.ShapeDtypeStruct((S, D_MODEL), f32)],
        compiler_params=_params("parallel", "arbitrary"),
    )(h2, w_gate, w_up, w_down)


def _loss_head(f, x2, target, g4):
    S = f.shape[0]
    ts = 512

    def body(f_ref, x2_ref, t_ref, g_ref, dy_ref, df_ref, dg_ref, loss_ref):
        @pl.when(pl.program_id(0) == 0)
        def _():
            dg_ref[...] = jnp.zeros_like(dg_ref)
            loss_ref[...] = jnp.zeros_like(loss_ref)

        fv = f_ref[...]
        g = g_ref[...]
        r = _rms(fv)
        fhat = fv * r
        err = (x2_ref[...] + fhat * g) - t_ref[...]
        loss_ref[...] += 0.5 * jnp.sum(jnp.mean(err * err, axis=-1, keepdims=True), axis=0, keepdims=True)
        dy = err * (1.0 / D_MODEL)
        dy_ref[...] = dy
        dg_ref[...] += jnp.sum(dy * fhat, axis=0, keepdims=True)
        dyg = dy * g
        df_ref[...] = (r * (dyg - fhat * jnp.mean(dyg * fhat, axis=-1, keepdims=True))).astype(bf16)

    row = pl.BlockSpec((ts, D_MODEL), lambda i: (i, 0))
    gain = pl.BlockSpec((1, D_MODEL), lambda i: (0, 0))
    return pl.pallas_call(
        body, name="loss_head", grid=(S // ts,), in_specs=[row, row, row, gain],
        out_specs=[row, row, gain, pl.BlockSpec((1, 1), lambda i: (0, 0))],
        out_shape=[jax.ShapeDtypeStruct((S, D_MODEL), f32), jax.ShapeDtypeStruct((S, D_MODEL), bf16),
                   jax.ShapeDtypeStruct((1, D_MODEL), f32), jax.ShapeDtypeStruct((1, 1), f32)],
        compiler_params=_params("arbitrary"),
    )(f, x2, target, g4)


def _ffn_bwd(df, gate, up, w_gate, w_up, w_down):
    S = df.shape[0]
    ts = 1024

    def body(df_ref, gate_ref, up_ref, wg_ref, wu_ref, wd_ref, a_ref, dgate_ref, dup_ref, dh_ref):
        j = pl.program_id(1)
        da = _dot_nt(df_ref[...], wd_ref[...])
        g = gate_ref[...].astype(f32)
        u = up_ref[...].astype(f32)
        sig = _sigmoid(g)
        silu = g * sig
        a_ref[...] = (silu * u).astype(bf16)
        dup = (da * silu).astype(bf16)
        dgate = (da * u * (sig * (1.0 + g * (1.0 - sig)))).astype(bf16)
        dup_ref[...] = dup
        dgate_ref[...] = dgate
        part = _dot_nt(dgate, wg_ref[...]) + _dot_nt(dup, wu_ref[...])

        @pl.when(j == 0)
        def _():
            dh_ref[...] = part

        @pl.when(j > 0)
        def _():
            dh_ref[...] += part

    act = pl.BlockSpec((ts, FF_TILE), lambda i, j: (i, j))
    row = pl.BlockSpec((ts, D_MODEL), lambda i, j: (i, 0))
    return pl.pallas_call(
        body, name="ffn_bwd", grid=(S // ts, D_FF // FF_TILE),
        in_specs=[row, act, act,
                  pl.BlockSpec((D_MODEL, FF_TILE), lambda i, j: (0, j)),
                  pl.BlockSpec((D_MODEL, FF_TILE), lambda i, j: (0, j)),
                  pl.BlockSpec((FF_TILE, D_MODEL), lambda i, j: (j, 0))],
        out_specs=[act, act, act, row],
        out_shape=[jax.ShapeDtypeStruct((S, D_FF), bf16)] * 3 + [jax.ShapeDtypeStruct((S, D_MODEL), f32)],
        compiler_params=_params("parallel", "arbitrary"),
    )(df, gate, up, w_gate, w_up, w_down)


def _norm_bwd(dh2, dy, x2, mix, g3, g2):
    S = dh2.shape[0]
    ts = 512

    def body(dh_ref, dy_ref, x2_ref, mix_ref, g3_ref, g2_ref, dx2_ref, dmix_ref, dg3_ref, dg2_ref):
        @pl.when(pl.program_id(0) == 0)
        def _():
            dg3_ref[...] = jnp.zeros_like(dg3_ref)
            dg2_ref[...] = jnp.zeros_like(dg2_ref)

        dh = dh_ref[...]
        x2 = x2_ref[...]
        r3 = _rms(x2)
        xhat = x2 * r3
        dg3_ref[...] += jnp.sum(dh * xhat, axis=0, keepdims=True)
        dhg = dh * g3_ref[...]
        dx2 = dy_ref[...] + r3 * (dhg - xhat * jnp.mean(dhg * xhat, axis=-1, keepdims=True))
        dx2_ref[...] = dx2
        mix = mix_ref[...]
        r2 = _rms(mix)
        mhat = mix * r2
        dg2_ref[...] += jnp.sum(dx2 * mhat, axis=0, keepdims=True)
        dmg = dx2 * g2_ref[...]
        dmix_ref[...] = (r2 * (dmg - mhat * jnp.mean(dmg * mhat, axis=-1, keepdims=True))).astype(bf16)

    row = pl.BlockSpec((ts, D_MODEL), lambda i: (i, 0))
    gain = pl.BlockSpec((1, D_MODEL), lambda i: (0, 0))
    return pl.pallas_call(
        body, name="norm_bwd", grid=(S // ts,), in_specs=[row, row, row, row, gain, gain],
        out_specs=[row, row, gain, gain],
        out_shape=[jax.ShapeDtypeStruct((S, D_MODEL), f32), jax.ShapeDtypeStruct((S, D_MODEL), bf16),
                   jax.ShapeDtypeStruct((1, D_MODEL), f32), jax.ShapeDtypeStruct((1, D_MODEL), f32)],
        compiler_params=_params("arbitrary"),
    )(dh2, dy, x2, mix, g3, g2)


def _out_proj_bwd(dmix, w_out, attn_out, head_ones):
    S = dmix.shape[0]
    ts = 512

    def body(dm_ref, w_ref, o_ref, ones_ref, dp_ref, dl_ref, *rest):
        do_refs, stage = rest[:-N_STAGE], rest[-N_STAGE:]
        dcat = _dot_nt(dm_ref[...], w_ref[...])
        dp_ref[...] = dcat[:, :POOL_WIDTH]
        do = dcat[:, POOL_WIDTH:]
        for j in range(ATTN_WIDTH // 128):
            stage[j][...] = do[:, j * 128:(j + 1) * 128]
        _store_streams(stage, do_refs, ts)
        prod = do * o_ref[...].astype(f32)
        hi = prod.astype(bf16)
        lo = (prod - hi.astype(f32)).astype(bf16)
        dl_ref[...] = _dot(hi, ones_ref[...]) + _dot(lo, ones_ref[...])

    row = lambda w: pl.BlockSpec((ts, w), lambda i: (i, 0))
    res = pl.pallas_call(
        body, name="out_proj_bwd", grid=(S // ts,),
        in_specs=[row(D_MODEL), pl.BlockSpec((D_MODEL, D_MODEL), lambda i: (0, 0)), row(ATTN_WIDTH),
                  pl.BlockSpec((ATTN_WIDTH, ATTN_WIDTH), lambda i: (0, 0))],
        out_specs=[row(POOL_WIDTH), row(ATTN_WIDTH)] + [_stream_spec(d, ts) for d in DILATIONS],
        out_shape=[jax.ShapeDtypeStruct((S, POOL_WIDTH), f32), jax.ShapeDtypeStruct((S, ATTN_WIDTH), f32)]
        + [_stream_shape(S, d) for d in DILATIONS],
        scratch_shapes=_stage_scratch(ts),
        compiler_params=_params("parallel"),
    )(dmix, w_out, attn_out, head_ones)
    return res[0], res[1], res[2:]


def _in_proj_bwd(du, dq, dk, dv, cos_t, sin_t, w_in, x, dx2, g1):
    S = x.shape[0]
    ts = 256

    def body(du_ref, dq_ref, dk_ref, dv_ref, cos_ref, sin_ref, w_ref, x_ref, dx2_ref, g_ref, gx_ref, dproj_ref, dg_ref):
        @pl.when(pl.program_id(0) == 0)
        def _():
            dg_ref[...] = jnp.zeros_like(dg_ref)

        dproj_ref[:, :POOL_WIDTH] = du_ref[...]
        cos = cos_ref[...]
        sin = sin_ref[...]
        first = _first_half_mask(ts)
        for j in range(ATTN_WIDTH // 128):
            cols = slice(j * 128, (j + 1) * 128)
            for base, ref in ((POOL_WIDTH, dq_ref), (POOL_WIDTH + ATTN_WIDTH, dk_ref)):
                g = ref[:, cols].astype(f32)
                pre = g * cos + _rope_partner(g * sin, first)
                dproj_ref[:, base + j * 128: base + (j + 1) * 128] = pre.astype(bf16)
        dproj_ref[:, POOL_WIDTH + 2 * ATTN_WIDTH:] = dv_ref[...]

        dh = _dot_nt(dproj_ref[...], w_ref[...])
        xv = x_ref[...]
        r = _rms(xv)
        xhat = xv * r
        dg_ref[...] += jnp.sum(dh * xhat, axis=0, keepdims=True)
        dhg = dh * g_ref[...]
        gx_ref[...] = dx2_ref[...] + r * (dhg - xhat * jnp.mean(dhg * xhat, axis=-1, keepdims=True))

    row = lambda w: pl.BlockSpec((ts, w), lambda i: (i, 0))
    gain = pl.BlockSpec((1, D_MODEL), lambda i: (0, 0))
    return pl.pallas_call(
        body, name="in_proj_bwd", grid=(S // ts,),
        in_specs=[row(POOL_WIDTH)] + [row(ATTN_WIDTH)] * 3 + [row(128), row(128),
                  pl.BlockSpec((D_MODEL, IN_WIDTH), lambda i: (0, 0)), row(D_MODEL), row(D_MODEL), gain],
        out_specs=[row(D_MODEL), row(IN_WIDTH), gain],
        out_shape=[jax.ShapeDtypeStruct((S, D_MODEL), f32), jax.ShapeDtypeStruct((S, IN_WIDTH), bf16),
                   jax.ShapeDtypeStruct((1, D_MODEL), f32)],
        compiler_params=_params("arbitrary"),
    )(du, dq, dk, dv, cos_t, sin_t, w_in, x, dx2, g1)


def _matmul_tn(a, b, tn, name):
    K, M = a.shape
    N = b.shape[1]
    tk = 512

    def body(a_ref, b_ref, o_ref):
        part = _dot_tn(a_ref[...], b_ref[...])

        @pl.when(pl.program_id(1) == 0)
        def _():
            o_ref[...] = part

        @pl.when(pl.program_id(1) > 0)
        def _():
            o_ref[...] += part

    return pl.pallas_call(
        body, name=name, grid=(N // tn, K // tk),
        in_specs=[pl.BlockSpec((tk, M), lambda n, k: (k, 0)), pl.BlockSpec((tk, tn), lambda n, k: (k, n))],
        out_specs=pl.BlockSpec((M, tn), lambda n, k: (0, n)),
        out_shape=jax.ShapeDtypeStruct((M, N), f32),
        compiler_params=_params("parallel", "arbitrary"),
    )(a, b)


def _rope_tables(S):
    half = HEAD_DIM // 2
    freqs = ROPE_THETA ** (-jnp.arange(half, dtype=f32) * (2.0 / HEAD_DIM))
    ang = jnp.arange(S).astype(f32)[:, None] * freqs[None, :]
    cos = jnp.tile(jnp.cos(ang), (1, 4))
    sin = jnp.sin(ang)
    sin = jnp.tile(jnp.concatenate([-sin, sin], axis=1), (1, 2))
    return cos, sin


def _block_diag(w_pool):
    w = jnp.zeros((POOL_WIDTH, POOL_WIDTH), w_pool.dtype)
    for g in range(POOL_WIDTH // POOL_GROUP):
        w = lax.dynamic_update_slice(w, w_pool[g], (g * POOL_GROUP, g * POOL_GROUP))
    return w


def _head_ones():
    head = np.arange(ATTN_WIDTH) // HEAD_DIM
    return jnp.asarray(head[:, None] == head[None, :], dtype=bf16)


def _local_grads(x, target, g1, w_pool, pool_scale, g2, g3, g4, w_in, w_out, w_gate, w_up, w_down):
    S = x.shape[0]
    cos_t, sin_t = _rope_tables(S)
    w_bd = _block_diag(w_pool).astype(bf16)

    h1, u, qs, ks, vs = _in_proj(x, g1, w_in, cos_t, sin_t)
    pool_out = _pool_fwd(u, w_bd, pool_scale)
    attn_out, lse = _attn_fwd(qs, ks, vs)
    mix, x2, h2 = _out_proj(pool_out, attn_out, w_out, x, g2, g3)
    gate, up, f = _ffn_fwd(h2, w_gate, w_up, w_down)
    dy, df, dg4, loss = _loss_head(f, x2, target, g4)

    a, dgate, dup, dh2 = _ffn_bwd(df, gate, up, w_gate, w_up, w_down)
    d_w_down = _matmul_tn(a, df, D_MODEL, "grad_w_down")
    d_w_gate = _matmul_tn(h2, dgate, FF_HALF, "grad_w_gate")
    d_w_up = _matmul_tn(h2, dup, FF_HALF, "grad_w_up")
    dx2, dmix, dg3, dg2 = _norm_bwd(dh2, dy, x2, mix, g3, g2)
    d_w_out = jnp.concatenate([_matmul_tn(pool_out, dmix, D_MODEL, "grad_w_out_pool"),
                               _matmul_tn(attn_out, dmix, D_MODEL, "grad_w_out_attn")], axis=0)
    dpool, delta, dos = _out_proj_bwd(dmix, w_out, attn_out, _head_ones())
    du, d_w_bd, d_scale = _pool_bwd(u, dpool, w_bd, pool_scale)
    dq, dk, dv = _attn_bwd(qs, ks, vs, dos, lse, delta)
    grad_x, dproj, dg1 = _in_proj_bwd(du, dq, dk, dv, cos_t, sin_t, w_in, x, dx2, g1)
    d_w_in = _matmul_tn(h1, dproj, IN_WIDTH // 2, "grad_w_in")
    d_w_pool = jnp.stack([d_w_bd[g * POOL_GROUP:(g + 1) * POOL_GROUP, g * POOL_GROUP:(g + 1) * POOL_GROUP]
                          for g in range(POOL_WIDTH // POOL_GROUP)])
    large = dict(w_in=d_w_in, w_out=d_w_out, w_gate=d_w_gate, w_up=d_w_up, w_down=d_w_down)
    small = dict(ln_pre_mix=dg1, ln_post_mix=dg2, ln_pre_ffn=dg3, ln_post_ffn=dg4, pool_scale=d_scale, w_pool=d_w_pool)
    return loss, grad_x, large, small


def _place():
    x, y, c = lax.axis_index("x"), lax.axis_index("y"), lax.axis_index("c")
    chips = [(1 - x, y), (x, 1 - y), (1 - x, 1 - y)]
    return x, y, c, chips


ANY = pl.BlockSpec(memory_space=pl.ANY)


def _row_chunks(rows, n, unit):
    units = rows // unit
    out, start = [], 0
    for i in range(n):
        size = (units // n + (1 if i < units % n else 0)) * unit
        out.append((start, size))
        start += size
    return out


GATHER_CHUNKS = _row_chunks(HALF_ROWS, 4, 32)
SWAP_CHUNKS = _row_chunks(HALF_ROWS, 8, 32)
JOIN_CHUNKS = _row_chunks(HALF_ROWS, 16, 32)
LOCAL_COPIES = 8


class _LocalCopy:
    def __init__(self, src_rows, dst_rows, rows, buf, sems_in, sems_out):
        self.loads, self.stores = [], []
        for i, (start, size) in enumerate(_row_chunks(rows, LOCAL_COPIES, 32)):
            r = pl.ds(start, size)
            self.loads.append(pltpu.make_async_copy(src_rows(r), buf.at[r], sems_in.at[i]))
            self.stores.append(pltpu.make_async_copy(buf.at[r], dst_rows(r), sems_out.at[i]))
        for cp in self.loads:
            cp.start()

    def finish(self):
        for load, store in zip(self.loads, self.stores):
            load.wait()
            store.start()
        for store in self.stores:
            store.wait()


def _local_scratch(rows, dtype):
    return [pltpu.VMEM((rows, D_MODEL), dtype), pltpu.SemaphoreType.DMA((LOCAL_COPIES,)),
            pltpu.SemaphoreType.DMA((LOCAL_COPIES,))]


def _gather_weights(pack):
    n_ch = len(GATHER_CHUNKS)

    def body(w_ref, out_ref, send1, recv1, send2, recv2, buf, sems_in, sems_out):
        x, y, c, chips = _place()
        me = 2 * x + y
        sibling = (x, y, 1 - c)
        own = _LocalCopy(lambda r: w_ref.at[r], lambda r: out_ref.at[me, r], PACK_ROWS, buf, sems_in, sems_out)

        def rows(core, ch):
            start, size = GATHER_CHUNKS[ch]
            return pl.ds(core * HALF_ROWS + start, size)

        def direct(j, ch, chip_xy, src_chip):
            cx, cy = chip_xy
            return pltpu.make_async_remote_copy(
                src_ref=w_ref.at[rows(c, ch)], dst_ref=out_ref.at[src_chip, rows(c, ch)],
                send_sem=send1.at[j * n_ch + ch], recv_sem=recv1.at[j * n_ch + ch],
                device_id=(cx, cy, c), device_id_type=MESH)

        def passed(j, ch, chip, core):
            return pltpu.make_async_remote_copy(
                src_ref=out_ref.at[chip, rows(core, ch)], dst_ref=out_ref.at[chip, rows(core, ch)],
                send_sem=send2.at[j * n_ch + ch], recv_sem=recv2.at[j * n_ch + ch],
                device_id=sibling, device_id_type=MESH)

        sends = [direct(j, ch, chip, me) for ch in range(n_ch) for j, chip in enumerate(chips)]
        for cp in sends:
            cp.start()
        own.finish()
        forwards = []
        for ch in range(n_ch):
            for j, (cx, cy) in enumerate(chips):
                direct(j, ch, (cx, cy), 2 * cx + cy).wait_recv()
                fw = passed(j, ch, 2 * cx + cy, c)
                fw.start()
                forwards.append(fw)
        for ch in range(n_ch):
            for j, (cx, cy) in enumerate(chips):
                passed(j, ch, 2 * cx + cy, 1 - c).wait_recv()
        for cp in sends + forwards:
            cp.wait_send()

    n_sem = 3 * n_ch
    return pl.pallas_call(
        body, name="gather_weights", in_specs=[ANY], out_specs=ANY,
        out_shape=jax.ShapeDtypeStruct((N_CHIPS, PACK_ROWS, D_MODEL), pack.dtype),
        scratch_shapes=[pltpu.SemaphoreType.DMA((n_sem,))] * 4 + _local_scratch(PACK_ROWS, pack.dtype),
        compiler_params=pltpu.CompilerParams(vmem_limit_bytes=VMEM_LIMIT_V7X),
    )(pack)


def _swap_halves(g):
    n_ch = len(SWAP_CHUNKS)

    def body(g_ref, theirs_ref, send, recv):
        x, y, c, _ = _place()

        def piece(s, ch, core):
            start, size = SWAP_CHUNKS[ch]
            return pltpu.make_async_remote_copy(
                src_ref=g_ref.at[s, pl.ds(core * HALF_ROWS + start, size)], dst_ref=theirs_ref.at[s, pl.ds(start, size)],
                send_sem=send.at[s * n_ch + ch], recv_sem=recv.at[s * n_ch + ch],
                device_id=(x, y, 1 - c), device_id_type=MESH)

        copies = [piece(s, ch, 1 - c) for s in range(N_CHIPS) for ch in range(n_ch)]
        for cp in copies:
            cp.start()
        for cp in copies:
            cp.wait()

    return pl.pallas_call(
        body, name="swap_halves", in_specs=[ANY], out_specs=ANY,
        out_shape=jax.ShapeDtypeStruct((N_CHIPS, HALF_ROWS, D_MODEL), g.dtype),
        scratch_shapes=[pltpu.SemaphoreType.DMA((N_CHIPS * n_ch,))] * 2,
    )(g)


ADD_ROWS = 376


def _add_cores(g, theirs):
    n_t = HALF_ROWS // ADD_ROWS

    def body(c_ref, g_ref, t_ref, o_ref):
        o_ref[...] = g_ref[...] + t_ref[...]

    blk = pl.BlockSpec((1, ADD_ROWS, D_MODEL), lambda s, t, c_ref: (s, t, 0))
    return pl.pallas_call(
        body, name="add_cores",
        grid_spec=pltpu.PrefetchScalarGridSpec(
            num_scalar_prefetch=1, grid=(N_CHIPS, n_t),
            in_specs=[pl.BlockSpec((1, ADD_ROWS, D_MODEL), lambda s, t, c_ref: (s, c_ref[0] * n_t + t, 0)), blk],
            out_specs=blk),
        out_shape=jax.ShapeDtypeStruct(theirs.shape, theirs.dtype),
        compiler_params=_params("parallel", "parallel"),
    )(lax.axis_index("c").astype(jnp.int32).reshape(1), g, theirs)


def _scatter_to_chips(h):
    n_ch = len(GATHER_CHUNKS)

    def body(h_ref, out_ref, send, recv, buf, sems_in, sems_out):
        x, y, c, chips = _place()
        me = 2 * x + y
        own = _LocalCopy(lambda r: h_ref.at[me, r], lambda r: out_ref.at[me, r], HALF_ROWS, buf, sems_in, sems_out)

        def piece(j, ch, chip_xy, dst_chip, src_chip):
            cx, cy = chip_xy
            start, size = GATHER_CHUNKS[ch]
            return pltpu.make_async_remote_copy(
                src_ref=h_ref.at[dst_chip, pl.ds(start, size)], dst_ref=out_ref.at[src_chip, pl.ds(start, size)],
                send_sem=send.at[j * n_ch + ch], recv_sem=recv.at[j * n_ch + ch],
                device_id=(cx, cy, c), device_id_type=MESH)

        sends = [piece(j, ch, (cx, cy), 2 * cx + cy, me) for ch in range(n_ch) for j, (cx, cy) in enumerate(chips)]
        for cp in sends:
            cp.start()
        own.finish()
        for ch in range(n_ch):
            for j, (cx, cy) in enumerate(chips):
                piece(j, ch, (cx, cy), me, 2 * cx + cy).wait_recv()
        for cp in sends:
            cp.wait_send()

    return pl.pallas_call(
        body, name="scatter_to_chips", in_specs=[ANY], out_specs=ANY,
        out_shape=jax.ShapeDtypeStruct(h.shape, h.dtype),
        scratch_shapes=[pltpu.SemaphoreType.DMA((3 * n_ch,))] * 2 + _local_scratch(HALF_ROWS, h.dtype),
        compiler_params=pltpu.CompilerParams(vmem_limit_bytes=VMEM_LIMIT_V7X),
    )(h)


def _join_halves(r):
    n_ch = len(JOIN_CHUNKS)

    def body(r_ref, out_ref, send, recv, buf, sems_in, sems_out):
        x, y, c, _ = _place()
        own = _LocalCopy(lambda r: r_ref.at[r], lambda r: out_ref.at[c, r], HALF_ROWS, buf, sems_in, sems_out)

        def piece(ch, core):
            start, size = JOIN_CHUNKS[ch]
            return pltpu.make_async_remote_copy(
                src_ref=r_ref.at[pl.ds(start, size)], dst_ref=out_ref.at[core, pl.ds(start, size)],
                send_sem=send.at[ch], recv_sem=recv.at[ch], device_id=(x, y, 1 - c), device_id_type=MESH)

        copies = [piece(ch, c) for ch in range(n_ch)]
        for cp in copies:
            cp.start()
        own.finish()
        for ch in range(n_ch):
            piece(ch, 1 - c).wait_recv()
        for cp in copies:
            cp.wait_send()

    return pl.pallas_call(
        body, name="join_halves", in_specs=[ANY], out_specs=ANY,
        out_shape=jax.ShapeDtypeStruct((2,) + r.shape, r.dtype),
        scratch_shapes=[pltpu.SemaphoreType.DMA((n_ch,))] * 2 + _local_scratch(HALF_ROWS, r.dtype),
        compiler_params=pltpu.CompilerParams(vmem_limit_bytes=VMEM_LIMIT_V7X),
    )(r)


def _add_slabs(terms, k, name):
    rows = terms[0][0].shape[1]
    tr = ADD_ROWS
    n = len(terms)

    def body(*refs):
        acc = refs[0][...]
        for r in refs[1:n]:
            acc = acc + r[...]
        refs[n][...] = acc

    slab = lambda first: pl.BlockSpec((1, tr, D_MODEL), lambda i, t: (first + i, t, 0))
    return pl.pallas_call(
        body, name=name, grid=(k, rows // tr), in_specs=[slab(first) for _, first in terms],
        out_specs=pl.BlockSpec((1, tr, D_MODEL), lambda i, t: (i, t, 0)),
        out_shape=jax.ShapeDtypeStruct((k, rows, D_MODEL), terms[0][0].dtype),
        compiler_params=_params("parallel", "parallel"),
    )(*[a for a, _ in terms])


def _sum_small(block):
    def body(b_ref, out_ref, gathered, send, recv):
        x, y, c, _ = _place()
        me = 4 * x + 2 * y + c
        gathered[me] = b_ref[...]
        sends = []
        for kk in range(1, N_DEV):
            flip = lambda v, bit: 1 - v if bit else v
            peer = (flip(x, kk & 4), flip(y, kk & 2), flip(c, kk & 1))
            cp = pltpu.make_async_remote_copy(
                src_ref=b_ref, dst_ref=gathered.at[me], send_sem=send.at[kk - 1], recv_sem=recv.at[kk - 1],
                device_id=peer, device_id_type=MESH)
            cp.start()
            sends.append(cp)
        for kk in range(1, N_DEV):
            peer_index = jnp.bitwise_xor(me, kk)
            pltpu.make_async_remote_copy(
                src_ref=b_ref, dst_ref=gathered.at[peer_index], send_sem=send.at[kk - 1], recv_sem=recv.at[kk - 1],
                device_id=(x, y, c), device_id_type=MESH).wait_recv()
        for cp in sends:
            cp.wait_send()
        acc = gathered[0]
        for dev in range(1, N_DEV):
            acc = acc + gathered[dev]
        out_ref[...] = acc

    vmem = pl.BlockSpec(memory_space=pltpu.VMEM)
    return pl.pallas_call(
        body, name="sum_small", in_specs=[vmem], out_specs=vmem,
        out_shape=jax.ShapeDtypeStruct(block.shape, block.dtype),
        scratch_shapes=[pltpu.VMEM((N_DEV,) + block.shape, block.dtype),
                        pltpu.SemaphoreType.DMA((N_DEV - 1,)), pltpu.SemaphoreType.DMA((N_DEV - 1,))],
    )(block)


def _adamw(w, g, m, v, name):
    rows, cols = w.shape
    tr = rows
    for cand in (512, 256, 128, 64, 32, 16, 8):
        if rows % cand == 0:
            tr = cand
            break
    c1 = 1.0 - ADAM_B1 ** ADAM_STEP
    c2 = 1.0 - ADAM_B2 ** ADAM_STEP

    def body(w_ref, g_ref, m_ref, v_ref, d_ref, nm_ref, nv_ref):
        gv = g_ref[...]
        nm = ADAM_B1 * m_ref[...] + (1.0 - ADAM_B1) * gv
        nv = ADAM_B2 * v_ref[...] + (1.0 - ADAM_B2) * (gv * gv)
        nm_ref[...] = nm
        nv_ref[...] = nv
        d_ref[...] = -ADAM_LR * ((nm / c1) / (jnp.sqrt(nv / c2) + ADAM_EPS) + ADAM_WD * w_ref[...])

    blk = pl.BlockSpec((tr, cols), lambda i: (i, 0))
    shape = jax.ShapeDtypeStruct((rows, cols), f32)
    return pl.pallas_call(
        body, name=name, grid=(rows // tr,), in_specs=[blk] * 4, out_specs=[blk] * 3, out_shape=[shape] * 3,
        compiler_params=_params("parallel"),
    )(w, g, m, v)


LARGE = ("w_in", "w_out", "w_gate", "w_up", "w_down")
SMALL = ("ln_pre_mix", "ln_post_mix", "ln_pre_ffn", "ln_post_ffn", "pool_scale", "w_pool")
COLUMN_SHARDED = {"w_in": IN_WIDTH // N_CHIPS, "w_gate": D_FF // N_CHIPS, "w_up": D_FF // N_CHIPS}


def _pack_shard(shards):
    return jnp.concatenate([shards[n].reshape(-1, D_MODEL) for n in LARGE], axis=0)


def _unpack_shard(pack, shapes):
    out, row = {}, 0
    for n, rows in zip(LARGE, PACK_SPLITS):
        out[n] = pack[row:row + rows].reshape(shapes[n])
        row += rows
    return out


def _whole_from_shards(packs):
    out, row = {}, 0
    for n, rows in zip(LARGE, PACK_SPLITS):
        part = packs[:, row:row + rows]
        if n in COLUMN_SHARDED:
            width = COLUMN_SHARDED[n]
            part = part.reshape(N_CHIPS, D_MODEL, width).transpose(1, 0, 2).reshape(D_MODEL, N_CHIPS * width)
        else:
            part = part.reshape(N_CHIPS * rows, D_MODEL)
        out[n] = part
        row += rows
    return out


def _shards_from_whole(grads):
    parts = []
    for n, rows in zip(LARGE, PACK_SPLITS):
        g = grads[n]
        if n in COLUMN_SHARDED:
            width = COLUMN_SHARDED[n]
            g = g.reshape(D_MODEL, N_CHIPS, width).transpose(1, 0, 2)
        parts.append(g.reshape(N_CHIPS, rows, D_MODEL))
    return jnp.concatenate(parts, axis=1)


def _pack_small(vals):
    rows = [vals[n].reshape(1, D_MODEL) for n in SMALL[:4]]
    rows.append(jnp.pad(vals["pool_scale"].reshape(1, POOL_WIDTH), ((0, 0), (0, D_MODEL - POOL_WIDTH))))
    rows.append(jnp.pad(vals["loss"].reshape(1, 1), ((0, 0), (0, D_MODEL - 1))))
    rows.append(jnp.zeros((2, D_MODEL), f32))
    rows.append(vals["w_pool"].reshape(16, D_MODEL))
    return jnp.concatenate(rows, axis=0)


def _unpack_small(block):
    out = {n: block[i:i + 1] for i, n in enumerate(SMALL[:4])}
    out["pool_scale"] = block[4:5, :POOL_WIDTH]
    out["loss"] = block[5, 0]
    out["w_pool"] = block[8:24].reshape(1, 4, POOL_GROUP, POOL_GROUP)
    return out


def kernel(x, ln_pre_mix, w_in, w_pool, pool_scale, w_out, ln_post_mix, ln_pre_ffn, w_gate, w_up, w_down, ln_post_ffn, loss_target, m_ln_pre_mix, m_w_in, m_w_pool, m_pool_scale, m_w_out, m_ln_post_mix, m_ln_pre_ffn, m_w_gate, m_w_up, m_w_down, m_ln_post_ffn, v_ln_pre_mix, v_w_in, v_w_pool, v_pool_scale, v_w_out, v_ln_post_mix, v_ln_pre_ffn, v_w_gate, v_w_up, v_w_down, v_ln_post_ffn):
    w = dict(ln_pre_mix=ln_pre_mix, w_in=w_in, w_pool=w_pool, pool_scale=pool_scale, w_out=w_out,
             ln_post_mix=ln_post_mix, ln_pre_ffn=ln_pre_ffn, w_gate=w_gate, w_up=w_up, w_down=w_down,
             ln_post_ffn=ln_post_ffn)
    m = dict(ln_pre_mix=m_ln_pre_mix, w_in=m_w_in, w_pool=m_w_pool, pool_scale=m_pool_scale, w_out=m_w_out,
             ln_post_mix=m_ln_post_mix, ln_pre_ffn=m_ln_pre_ffn, w_gate=m_w_gate, w_up=m_w_up, w_down=m_w_down,
             ln_post_ffn=m_ln_post_ffn)
    v = dict(ln_pre_mix=v_ln_pre_mix, w_in=v_w_in, w_pool=v_w_pool, pool_scale=v_pool_scale, w_out=v_w_out,
             ln_post_mix=v_ln_post_mix, ln_pre_ffn=v_ln_pre_ffn, w_gate=v_w_gate, w_up=v_w_up, w_down=v_w_down,
             ln_post_ffn=v_ln_post_ffn)

    packs = _gather_weights(_pack_shard({n: w[n][0].astype(bf16) for n in LARGE}))
    whole = _whole_from_shards(packs)

    loss, grad_x, large, small = _local_grads(
        x[0], loss_target[0], ln_pre_mix, w_pool[0], pool_scale, ln_post_mix, ln_pre_ffn, ln_post_ffn,
        whole["w_in"], whole["w_out"], whole["w_gate"], whole["w_up"], whole["w_down"])

    shard_major = _shards_from_whole(large)
    chip_sum = _add_cores(shard_major, _swap_halves(shard_major))
    pieces = _scatter_to_chips(chip_sum)
    reduced_half = _add_slabs([(pieces, j) for j in range(N_CHIPS)], 1, "add_chips")[0]
    reduced = _join_halves(reduced_half).reshape(PACK_ROWS, D_MODEL)
    shapes = {n: w[n].shape[1:] for n in LARGE}
    grads = _unpack_shard(reduced, shapes)

    total = _unpack_small(_sum_small(_pack_small(dict(small, loss=loss))))
    for n in SMALL:
        grads[n] = total[n]

    delta, new_m, new_v = {}, {}, {}
    for n in LARGE:
        delta[n], new_m[n], new_v[n] = _adamw(w[n][0], grads[n], m[n][0], v[n][0], "adamw_" + n)
    small_state = [_pack_small(dict({n: s[n] for n in SMALL}, loss=jnp.zeros((), f32))) for s in (w, m, v)]
    small_grad = _pack_small(dict({n: grads[n] for n in SMALL}, loss=jnp.zeros((), f32)))
    sd, sm, sv = _adamw(small_state[0], small_grad, small_state[1], small_state[2], "adamw_small")
    for out, block in ((delta, sd), (new_m, sm), (new_v, sv)):
        un = _unpack_small(block)
        for n in SMALL:
            out[n] = un[n]

    names = ("ln_pre_mix", "w_in", "w_pool", "pool_scale", "w_out", "ln_post_mix", "ln_pre_ffn", "w_gate", "w_up",
             "w_down", "ln_post_ffn")
    full = lambda d: [d[n].reshape(w[n].shape) for n in names]
    return (total["loss"], grad_x[None], *full(grads), *full(delta), *full(new_m), *full(new_v))
```

```python
import numpy as np
import jax
import jax.numpy as jnp
from jax import lax
from jax.experimental import pallas as pl
from jax.experimental.pallas import tpu as pltpu

D_MODEL = 1024
POOL_WIDTH = 256
POOL_GROUP = 64
ATTN_WIDTH = 768
HEAD_DIM = 64
IN_WIDTH = 2560
D_FF = 2816
BLOCK = 128
DILATIONS = (1, 4, 16)
ROPE_THETA = 10000.0
EPS = 1e-6
ATTN_SCALE = 0.125
NEG = -1e30

ADAM_LR = 0.001
ADAM_B1 = 0.9
ADAM_B2 = 0.999
ADAM_EPS = 1e-08
ADAM_WD = 0.01
ADAM_STEP = 10

N_CHIPS = 4
N_DEV = 8
VMEM_LIMIT_V7X = 56 * 1024 * 1024
MESH = pl.DeviceIdType.MESH

f32 = jnp.float32
bf16 = jnp.bfloat16


def _params(*sem):
    return pltpu.CompilerParams(dimension_semantics=sem, vmem_limit_bytes=VMEM_LIMIT_V7X)


def _dot(a, b):
    return jnp.dot(a, b, preferred_element_type=f32)


def _dot_nt(a, b):
    return lax.dot_general(a, b, (((1,), (1,)), ((), ())), preferred_element_type=f32)


def _dot_tn(a, b):
    return lax.dot_general(a, b, (((0,), (0,)), ((), ())), preferred_element_type=f32)


def _rope_partner(a, first_half):
    return jnp.where(first_half, pltpu.roll(a, 96, 1), pltpu.roll(a, 32, 1))


def _first_half_mask(rows):
    lane = lax.broadcasted_iota(jnp.int32, (rows, 128), 1)
    return (lane % HEAD_DIM) < (HEAD_DIM // 2)


def _stream_spec(d, ts):
    return pl.BlockSpec((d, ts // d, ATTN_WIDTH), lambda i: (0, i, 0))


def _stream_shape(S, d):
    return jax.ShapeDtypeStruct((d, S // d, ATTN_WIDTH), bf16)


N_STAGE = ATTN_WIDTH // 128


def _stage_scratch(ts):
    return [pltpu.VMEM((ts, 128), f32)] * N_STAGE


def _store_streams(stage, out_refs, ts):
    for d, ref in zip(DILATIONS, out_refs):
        for r in range(d):
            rows = pl.ds(0, ts) if d == 1 else pl.ds(r, ts // d, stride=d)
            for j in range(N_STAGE):
                ref[r, :, j * 128:(j + 1) * 128] = stage[j][rows, :].astype(bf16)


def _in_proj(x, g1, w_in, cos_t, sin_t):
    S = x.shape[0]
    ts = 512

    def body(x_ref, g_ref, w_ref, cos_ref, sin_ref, h_ref, u_ref, *rest):
        outs, stage = rest[:-N_STAGE], rest[-N_STAGE:]
        xv = x_ref[...]
        r = lax.rsqrt(jnp.mean(xv * xv, axis=-1, keepdims=True) + EPS)
        h = ((xv * r) * g_ref[...]).astype(bf16)
        h_ref[...] = h
        proj = _dot(h, w_ref[...])
        u_ref[...] = proj[:, :POOL_WIDTH]
        cos = cos_ref[...]
        sin = sin_ref[...]
        first = _first_half_mask(ts)
        n_dil = len(DILATIONS)
        for which, base in enumerate((POOL_WIDTH, POOL_WIDTH + ATTN_WIDTH)):
            for j in range(ATTN_WIDTH // 128):
                a = proj[:, base + j * 128: base + (j + 1) * 128]
                stage[j][...] = a * cos + _rope_partner(a, first) * sin
            _store_streams(stage, outs[which * n_dil:(which + 1) * n_dil], ts)
        for j in range(ATTN_WIDTH // 128):
            base = POOL_WIDTH + 2 * ATTN_WIDTH + j * 128
            stage[j][...] = proj[:, base:base + 128]
        _store_streams(stage, outs[2 * n_dil:], ts)

    row = lambda w: pl.BlockSpec((ts, w), lambda i: (i, 0))
    streams = [_stream_spec(d, ts) for d in DILATIONS] * 3
    res = pl.pallas_call(
        body, name="in_proj", grid=(S // ts,),
        in_specs=[row(D_MODEL), pl.BlockSpec((1, D_MODEL), lambda i: (0, 0)),
                  pl.BlockSpec((D_MODEL, IN_WIDTH), lambda i: (0, 0)), row(128), row(128)],
        out_specs=[row(D_MODEL), row(POOL_WIDTH)] + streams,
        out_shape=[jax.ShapeDtypeStruct((S, D_MODEL), bf16), jax.ShapeDtypeStruct((S, POOL_WIDTH), f32)]
        + [_stream_shape(S, d) for d in DILATIONS] * 3,
        scratch_shapes=_stage_scratch(ts),
        compiler_params=_params("parallel"),
    )(x, g1, w_in, cos_t, sin_t)
    n = len(DILATIONS)
    return res[0], res[1], res[2:2 + n], res[2 + n:2 + 2 * n], res[2 + 2 * n:]


POOL_HALO = 16


def _pool_lane_group(rows):
    return lax.broadcasted_iota(jnp.int32, (rows, POOL_WIDTH), 1) // POOL_GROUP


def _pool_select(group, s2, s4, s8, s16):
    return jnp.where(group == 0, s2, jnp.where(group == 1, s4, jnp.where(group == 2, s8, s16)))


def _pool_count(t0, rows):
    group = _pool_lane_group(rows)
    t = t0 + lax.broadcasted_iota(jnp.int32, (rows, POOL_WIDTH), 0)
    win = _pool_select(group, 2, 4, 8, 16)
    return jnp.minimum(t + 1, win).astype(f32)


def _pool_diff(u_halo, u_tile, t0):
    ts = u_tile.shape[0]
    ext = jnp.concatenate([u_halo, u_tile], axis=0)
    s2 = ext + pltpu.roll(ext, 1, 0)
    s4 = s2 + pltpu.roll(s2, 2, 0)
    s8 = s4 + pltpu.roll(s4, 4, 0)
    s16 = s8 + pltpu.roll(s8, 8, 0)
    group = _pool_lane_group(ts + POOL_HALO)
    wsum = _pool_select(group, s2, s4, s8, s16)[POOL_HALO:]
    return wsum / _pool_count(t0, ts) - u_tile


def _pool_specs(ts, n_tiles):
    tile = pl.BlockSpec((ts, POOL_WIDTH), lambda i: (i, 0))
    per = ts // POOL_HALO
    before = pl.BlockSpec((POOL_HALO, POOL_WIDTH), lambda i: (jnp.maximum(i * per - 1, 0), 0))
    after = pl.BlockSpec((POOL_HALO, POOL_WIDTH), lambda i: (jnp.minimum((i + 1) * per, n_tiles * per - 1), 0))
    return tile, before, after


def _pool_fwd(u, w_bd, scale):
    S = u.shape[0]
    ts = 512
    n_tiles = S // ts

    def body(u_ref, halo_ref, w_ref, sc_ref, y_ref):
        i = pl.program_id(0)
        halo = jnp.where(i > 0, halo_ref[...], 0.0)
        d = _pool_diff(halo, u_ref[...], i * ts)
        y_ref[...] = (_dot(d.astype(bf16), w_ref[...]) * sc_ref[...]).astype(bf16)

    tile, before, _ = _pool_specs(ts, n_tiles)
    return pl.pallas_call(
        body, name="pool_fwd", grid=(n_tiles,),
        in_specs=[tile, before, pl.BlockSpec((POOL_WIDTH, POOL_WIDTH), lambda i: (0, 0)),
                  pl.BlockSpec((1, POOL_WIDTH), lambda i: (0, 0))],
        out_specs=tile, out_shape=jax.ShapeDtypeStruct((S, POOL_WIDTH), bf16),
        compiler_params=_params("parallel"),
    )(u, u, w_bd, scale)


def _pool_bwd(u, dy, w_bd, scale):
    S = u.shape[0]
    ts = 512
    n_tiles = S // ts

    def body(u_ref, halo_ref, dy_ref, dy_next_ref, w_ref, sc_ref, du_ref, dw_ref, dsc_ref):
        i = pl.program_id(0)

        @pl.when(i == 0)
        def _():
            dw_ref[...] = jnp.zeros_like(dw_ref)
            dsc_ref[...] = jnp.zeros_like(dsc_ref)

        halo = jnp.where(i > 0, halo_ref[...], 0.0)
        d = _pool_diff(halo, u_ref[...], i * ts).astype(bf16)
        w = w_ref[...]
        sc = sc_ref[...]
        dy_tile = dy_ref[...]
        z = _dot(d, w)
        dsc_ref[...] += jnp.sum(dy_tile * z, axis=0, keepdims=True)
        dy_next = jnp.where(i < n_tiles - 1, dy_next_ref[...], 0.0)
        dz = (jnp.concatenate([dy_tile, dy_next], axis=0) * sc).astype(bf16)
        dw_ref[...] += _dot_tn(d, dz[:ts])
        dd = _dot_nt(dz, w)
        e = dd / _pool_count(i * ts, ts + POOL_HALO)
        n = ts + POOL_HALO
        f2 = e + pltpu.roll(e, n - 1, 0)
        f4 = f2 + pltpu.roll(f2, n - 2, 0)
        f8 = f4 + pltpu.roll(f4, n - 4, 0)
        f16 = f8 + pltpu.roll(f8, n - 8, 0)
        fsum = _pool_select(_pool_lane_group(n), f2, f4, f8, f16)
        du_ref[...] = (fsum[:ts] - dd[:ts]).astype(bf16)

    tile, before, after = _pool_specs(ts, n_tiles)
    return pl.pallas_call(
        body, name="pool_bwd", grid=(n_tiles,),
        in_specs=[tile, before, tile, after, pl.BlockSpec((POOL_WIDTH, POOL_WIDTH), lambda i: (0, 0)),
                  pl.BlockSpec((1, POOL_WIDTH), lambda i: (0, 0))],
        out_specs=[tile, pl.BlockSpec((POOL_WIDTH, POOL_WIDTH), lambda i: (0, 0)),
                   pl.BlockSpec((1, POOL_WIDTH), lambda i: (0, 0))],
        out_shape=[jax.ShapeDtypeStruct((S, POOL_WIDTH), bf16), jax.ShapeDtypeStruct((POOL_WIDTH, POOL_WIDTH), f32),
                   jax.ShapeDtypeStruct((1, POOL_WIDTH), f32)],
        compiler_params=_params("arbitrary"),
    )(u, u, dy, dy, w_bd, scale)


SUPER = BLOCK * DILATIONS[-1]
UNITS = SUPER // BLOCK
FWD_UNROLL = 4
BWD_UNROLL = 4


def _band_mask(has_prev):
    qi = lax.broadcasted_iota(jnp.int32, (BLOCK, 2 * BLOCK), 0)
    kj = lax.broadcasted_iota(jnp.int32, (BLOCK, 2 * BLOCK), 1)
    return (kj >= qi) & (kj <= qi + BLOCK) & ((kj >= BLOCK) | has_prev)


def _head0_mask(rows=BLOCK):
    return lax.broadcasted_iota(jnp.int32, (rows, 128), 1) < HEAD_DIM


def _band_mask_t(has_prev):
    ki = lax.broadcasted_iota(jnp.int32, (2 * BLOCK, 2 * BLOCK), 0)
    qj = lax.broadcasted_iota(jnp.int32, (2 * BLOCK, 2 * BLOCK), 1) % BLOCK
    return (ki >= qj) & (ki <= qj + BLOCK) & ((ki >= BLOCK) | has_prev)


def _head_pair_rows(a, h0):
    zero = jnp.zeros_like(a)
    return jnp.concatenate([jnp.where(h0, a, zero), jnp.where(h0, zero, a)], axis=0)


def _per_query_row(stat):
    t = stat.T
    return jnp.concatenate([jnp.concatenate([t[:HEAD_DIM]] * 4, axis=0), jnp.concatenate([t[HEAD_DIM:]] * 4, axis=0)],
                           axis=1)


def _natural_rows(d, r, n):
    if d == 1:
        return pl.ds(pl.multiple_of(n * BLOCK, BLOCK), BLOCK)
    return pl.ds(n * (BLOCK * d) + r, BLOCK, stride=d)


def _unit_place(d, u):
    per_stream = UNITS // d
    return u // per_stream, u % per_stream, per_stream


def _block_rows(n):
    return pl.ds(pl.multiple_of(n * BLOCK, BLOCK), BLOCK)


def _band(cur_ref, tail_ref, r, n):
    before = jnp.where(n > 0, cur_ref[r, _block_rows(jnp.maximum(n - 1, 0)), :], tail_ref[r])
    return jnp.concatenate([before, cur_ref[r, _block_rows(n), :]], axis=0)


def _attn_in_specs(S, with_do):
    specs = []
    last = S // SUPER - 1
    for d in DILATIONS:
        per_stream = UNITS // d
        cur = pl.BlockSpec((d, SUPER // d, 128), lambda hp, sb: (0, jnp.minimum(sb, last), hp))
        tail = pl.BlockSpec(
            (d, BLOCK, 128),
            lambda hp, sb, per_stream=per_stream: (0, jnp.maximum(jnp.minimum(sb, last) * per_stream - 1, 0), hp))
        specs += [cur] * (2 if with_do else 1) + [cur, tail, cur, tail]
    return specs


def _attn_fwd(qs, ks, vs, pack):
    S = qs[0].shape[1]
    n_dil = len(DILATIONS)
    n_steps = S // SUPER
    n_total = (ATTN_WIDTH // 128) * n_steps

    def body(*refs):
        ins, pack_ref = refs[:5 * n_dil], refs[5 * n_dil]
        out_ref, lse_ref, gathered_ref = refs[5 * n_dil + 1:5 * n_dil + 4]
        scratch = refs[5 * n_dil + 4:]
        o_sc, l_sc = scratch[:n_dil], scratch[n_dil:2 * n_dil]
        gather = _Gather(pack_ref, gathered_ref, *scratch[2 * n_dil:])
        sb = pl.program_id(1)
        step = pl.program_id(0) * n_steps + sb

        @pl.when(step == 0)
        def _():
            gather.start()

        h0 = _head0_mask()
        for ci, d in enumerate(DILATIONS):
            q_ref, kc_ref, kp_ref, vc_ref, vp_ref = ins[5 * ci:5 * ci + 5]

            def unit(u, carry, d=d, ci=ci, q_ref=q_ref, kc_ref=kc_ref, kp_ref=kp_ref, vc_ref=vc_ref, vp_ref=vp_ref):
                r, n, _ = _unit_place(d, u)
                qv = q_ref[r, _block_rows(n), :]
                kb = _band(kc_ref, kp_ref, r, n)
                vb = _band(vc_ref, vp_ref, r, n)
                valid = _band_mask((sb > 0) | (n > 0))
                outs, lses = [], []
                for h in range(2):
                    keep = h0 if h == 0 else jnp.logical_not(h0)
                    qh = jnp.where(keep, qv, jnp.zeros_like(qv))
                    s = jnp.where(valid, _dot_nt(qh, kb) * ATTN_SCALE, NEG)
                    m = jnp.max(s, axis=1, keepdims=True)
                    e = jnp.exp(s - m)
                    den = jnp.sum(e, axis=1, keepdims=True)
                    outs.append(_dot((e / den).astype(bf16), vb))
                    lses.append(jnp.broadcast_to(m + jnp.log(den), (BLOCK, 128)))
                rows = _natural_rows(d, r, n)
                o_sc[ci][rows, :] = jnp.where(h0, outs[0], outs[1])
                l_sc[ci][rows, :] = jnp.where(h0, lses[0], lses[1])
                return carry

            lax.fori_loop(0, UNITS, unit, 0, unroll=FWD_UNROLL)

        def merge(t, carry):
            rows = pl.ds(pl.multiple_of(t * 256, 256), 256)
            a, b, c = l_sc[0][rows, :], l_sc[1][rows, :], l_sc[2][rows, :]
            m = jnp.maximum(jnp.maximum(a, b), c)
            ea, eb, ec = jnp.exp(a - m), jnp.exp(b - m), jnp.exp(c - m)
            tot = ea + eb + ec
            out_ref[rows, :] = ((ea / tot) * o_sc[0][rows, :] + (eb / tot) * o_sc[1][rows, :]
                                + (ec / tot) * o_sc[2][rows, :]).astype(bf16)
            lse_ref[rows, :] = m + jnp.log(tot)
            return carry

        lax.fori_loop(0, SUPER // 256, merge, 0)

        @pl.when(step == n_total // 2)
        def _():
            gather.pass_on()

        @pl.when(step == n_total - 1)
        def _():
            gather.finish()

    args = []
    for q, k, v in zip(qs, ks, vs):
        args += [q, k, k, v, v]
    nat = pl.BlockSpec((SUPER, 128), lambda hp, sb: (sb, hp))
    rows = pack.shape[0]
    return pl.pallas_call(
        body, name="attn_fwd", grid=(ATTN_WIDTH // 128, n_steps),
        in_specs=_attn_in_specs(S, False) + [ANY], out_specs=[nat, nat, ANY],
        out_shape=[jax.ShapeDtypeStruct((S, ATTN_WIDTH), bf16), jax.ShapeDtypeStruct((S, ATTN_WIDTH), f32),
                   _Gather.out_shape(rows, pack.dtype)],
        scratch_shapes=[pltpu.VMEM((SUPER, 128), f32)] * (2 * n_dil) + _Gather.scratch(rows, pack.dtype),
        compiler_params=_params("arbitrary", "arbitrary"),
    )(*args, pack)


def _attn_bwd(qs, ks, vs, dos, lse, delta, chip_sum):
    S = qs[0].shape[1]
    n_steps = S // SUPER
    last = n_steps - 1
    n_dil = len(DILATIONS)
    n_total = (ATTN_WIDTH // 128) * (n_steps + 1)

    def body(*refs):
        ins, (lse_ref, dl_ref, sum_ref) = refs[:6 * n_dil], refs[6 * n_dil:6 * n_dil + 3]
        dq_ref, dk_ref, dv_ref, others_ref = refs[6 * n_dil + 3:6 * n_dil + 7]
        dq_acc, dk_acc, dv_acc = refs[6 * n_dil + 7:6 * n_dil + 10]
        scatter = _Scatter(sum_ref, others_ref, *refs[6 * n_dil + 10:])
        sb = pl.program_id(1)
        step = pl.program_id(0) * (n_steps + 1) + sb
        cur = sb % 2
        prv = 1 - cur

        @pl.when(step == 0)
        def _():
            scatter.start()

        @pl.when(sb < n_steps)
        def _():
            dq_acc[...] = jnp.zeros_like(dq_acc)
            dk_acc[cur] = jnp.zeros((SUPER, 128), f32)
            dv_acc[cur] = jnp.zeros((SUPER, 128), f32)
            h0 = _head0_mask()
            for ci, d in enumerate(DILATIONS):
                q_ref, do_ref, kc_ref, kp_ref, vc_ref, vp_ref = ins[6 * ci:6 * ci + 6]

                def unit(u, carry, d=d, q_ref=q_ref, do_ref=do_ref, kc_ref=kc_ref, kp_ref=kp_ref, vc_ref=vc_ref,
                         vp_ref=vp_ref):
                    r, n, per_stream = _unit_place(d, u)
                    qv = q_ref[r, _block_rows(n), :]
                    dov = do_ref[r, _block_rows(n), :]
                    kb = _band(kc_ref, kp_ref, r, n)
                    vb = _band(vc_ref, vp_ref, r, n)
                    rows = _natural_rows(d, r, n)
                    has_prev = (sb > 0) | (n > 0)
                    q_pair = _head_pair_rows(qv, h0)
                    do_pair = _head_pair_rows(dov, h0)
                    s_t = jnp.where(_band_mask_t(has_prev), _dot_nt(kb, q_pair) * ATTN_SCALE, NEG)
                    p_t = jnp.exp(s_t - _per_query_row(lse_ref[rows, :]))
                    dp_t = _dot_nt(vb, do_pair)
                    ds_t = (p_t * (dp_t - _per_query_row(dl_ref[rows, :])) * ATTN_SCALE).astype(bf16)
                    dvb = _dot(p_t.astype(bf16), do_pair)
                    dkb = _dot(ds_t, q_pair)
                    dq_pair = _dot_tn(ds_t, kb)
                    dq_acc[rows, :] += jnp.where(h0, dq_pair[:BLOCK], dq_pair[BLOCK:])
                    dk_acc[cur, rows, :] += dkb[BLOCK:]
                    dv_acc[cur, rows, :] += dvb[BLOCK:]

                    slot = jnp.where((n > 0) | (sb == 0), cur, prv)
                    before = _natural_rows(d, r, jnp.where(n > 0, n - 1, per_stream - 1))
                    dk_acc[slot, before, :] += dkb[:BLOCK]
                    dv_acc[slot, before, :] += dvb[:BLOCK]
                    return carry

                lax.fori_loop(0, UNITS, unit, 0, unroll=BWD_UNROLL)
            dq_ref[...] = dq_acc[...].astype(bf16)

        @pl.when(sb > 0)
        def _():
            dk_ref[...] = dk_acc[prv].astype(bf16)
            dv_ref[...] = dv_acc[prv].astype(bf16)

        @pl.when(step == n_total - 1)
        def _():
            scatter.finish()

    args = []
    for q, k, v, do in zip(qs, ks, vs, dos):
        args += [q, do, k, k, v, v]
    nat = pl.BlockSpec((SUPER, 128), lambda hp, sb: (jnp.minimum(sb, last), hp))
    nat_before = pl.BlockSpec((SUPER, 128), lambda hp, sb: (jnp.clip(sb - 1, 0, last), hp))
    out = jax.ShapeDtypeStruct((S, ATTN_WIDTH), bf16)
    half = chip_sum.shape[1]
    return pl.pallas_call(
        body, name="attn_bwd", grid=(ATTN_WIDTH // 128, n_steps + 1),
        in_specs=_attn_in_specs(S, True) + [nat, nat, ANY], out_specs=[nat, nat_before, nat_before, ANY],
        out_shape=[out, out, out, _Scatter.out_shape(half, chip_sum.dtype)],
        scratch_shapes=[pltpu.VMEM((SUPER, 128), f32), pltpu.VMEM((2, SUPER, 128), f32),
                        pltpu.VMEM((2, SUPER, 128), f32)] + _Scatter.scratch(half),
        compiler_params=_params("arbitrary", "arbitrary"),
    )(*args, lse, delta, chip_sum)


def _rms(v):
    return lax.rsqrt(jnp.mean(v * v, axis=-1, keepdims=True) + EPS)


def _out_proj(pool_out, attn_out, w_out, x, g2, g3):
    S = x.shape[0]
    ts = 512

    def body(p_ref, a_ref, w_ref, x_ref, g2_ref, g3_ref, mix_ref, x2_ref, h2_ref):
        mix = _dot(p_ref[...], w_ref[:POOL_WIDTH, :]) + _dot(a_ref[...], w_ref[POOL_WIDTH:, :])
        mix_ref[...] = mix
        x2 = x_ref[...] + (mix * _rms(mix)) * g2_ref[...]
        x2_ref[...] = x2
        h2_ref[...] = ((x2 * _rms(x2)) * g3_ref[...]).astype(bf16)

    row = lambda w: pl.BlockSpec((ts, w), lambda i: (i, 0))
    gain = pl.BlockSpec((1, D_MODEL), lambda i: (0, 0))
    return pl.pallas_call(
        body, name="out_proj", grid=(S // ts,),
        in_specs=[row(POOL_WIDTH), row(ATTN_WIDTH), pl.BlockSpec((D_MODEL, D_MODEL), lambda i: (0, 0)),
                  row(D_MODEL), gain, gain],
        out_specs=[row(D_MODEL)] * 3,
        out_shape=[jax.ShapeDtypeStruct((S, D_MODEL), f32), jax.ShapeDtypeStruct((S, D_MODEL), f32),
                   jax.ShapeDtypeStruct((S, D_MODEL), bf16)],
        compiler_params=_params("parallel"),
    )(pool_out, attn_out, w_out, x, g2, g3)


FF_TILE = 256
FF_HALF = D_FF // 2


def _sigmoid(g):
    return 1.0 / (1.0 + jnp.exp(-g))


def _ffn_fwd(h2, w_gate, w_up, w_down):
    S = h2.shape[0]
    ts = 1024

    def body(h_ref, wg_ref, wu_ref, wd_ref, gate_ref, up_ref, f_ref):
        j = pl.program_id(1)
        h = h_ref[...]
        gate = _dot(h, wg_ref[...])
        up = _dot(h, wu_ref[...])
        gate_ref[...] = gate.astype(bf16)
        up_ref[...] = up.astype(bf16)
        part = _dot((gate * _sigmoid(gate) * up).astype(bf16), wd_ref[...])

        @pl.when(j == 0)
        def _():
            f_ref[...] = part

        @pl.when(j > 0)
        def _():
            f_ref[...] += part

    act = pl.BlockSpec((ts, FF_TILE), lambda i, j: (i, j))
    return pl.pallas_call(
        body, name="ffn_fwd", grid=(S // ts, D_FF // FF_TILE),
        in_specs=[pl.BlockSpec((ts, D_MODEL), lambda i, j: (i, 0)),
                  pl.BlockSpec((D_MODEL, FF_TILE), lambda i, j: (0, j)),
                  pl.BlockSpec((D_MODEL, FF_TILE), lambda i, j: (0, j)),
                  pl.BlockSpec((FF_TILE, D_MODEL), lambda i, j: (j, 0))],
        out_specs=[act, act, pl.BlockSpec((ts, D_MODEL), lambda i, j: (i, 0))],
        out_shape=[jax.ShapeDtypeStruct((S, D_FF), bf16), jax.ShapeDtypeStruct((S, D_FF), bf16),
                   jax.ShapeDtypeStruct((S, D_MODEL), f32)],
        compiler_params=_params("parallel", "arbitrary"),
    )(h2, w_gate, w_up, w_down)


def _loss_head(f, x2, target, g4):
    S = f.shape[0]
    ts = 512

    def body(f_ref, x2_ref, t_ref, g_ref, dy_ref, df_ref, dg_ref, loss_ref):
        @pl.when(pl.program_id(0) == 0)
        def _():
            dg_ref[...] = jnp.zeros_like(dg_ref)
            loss_ref[...] = jnp.zeros_like(loss_ref)

        fv = f_ref[...]
        g = g_ref[...]
        r = _rms(fv)
        fhat = fv * r
        err = (x2_ref[...] + fhat * g) - t_ref[...]
        loss_ref[...] += 0.5 * jnp.sum(jnp.mean(err * err, axis=-1, keepdims=True), axis=0, keepdims=True)
        dy = err * (1.0 / D_MODEL)
        dy_ref[...] = dy
        dg_ref[...] += jnp.sum(dy * fhat, axis=0, keepdims=True)
        dyg = dy * g
        df_ref[...] = (r * (dyg - fhat * jnp.mean(dyg * fhat, axis=-1, keepdims=True))).astype(bf16)

    row = pl.BlockSpec((ts, D_MODEL), lambda i: (i, 0))
    gain = pl.BlockSpec((1, D_MODEL), lambda i: (0, 0))
    return pl.pallas_call(
        body, name="loss_head", grid=(S // ts,), in_specs=[row, row, row, gain],
        out_specs=[row, row, gain, pl.BlockSpec((1, 1), lambda i: (0, 0))],
        out_shape=[jax.ShapeDtypeStruct((S, D_MODEL), f32), jax.ShapeDtypeStruct((S, D_MODEL), bf16),
                   jax.ShapeDtypeStruct((1, D_MODEL), f32), jax.ShapeDtypeStruct((1, 1), f32)],
        compiler_params=_params("arbitrary"),
    )(f, x2, target, g4)


def _ffn_bwd(df, gate, up, w_gate, w_up, w_down):
    S = df.shape[0]
    ts = 1024

    def body(df_ref, gate_ref, up_ref, wg_ref, wu_ref, wd_ref, a_ref, dgate_ref, dup_ref, dh_ref):
        j = pl.program_id(1)
        da = _dot_nt(df_ref[...], wd_ref[...])
        g = gate_ref[...].astype(f32)
        u = up_ref[...].astype(f32)
        sig = _sigmoid(g)
        silu = g * sig
        a_ref[...] = (silu * u).astype(bf16)
        dup = (da * silu).astype(bf16)
        dgate = (da * u * (sig * (1.0 + g * (1.0 - sig)))).astype(bf16)
        dup_ref[...] = dup
        dgate_ref[...] = dgate
        part = _dot_nt(dgate, wg_ref[...]) + _dot_nt(dup, wu_ref[...])

        @pl.when(j == 0)
        def _():
            dh_ref[...] = part

        @pl.when(j > 0)
        def _():
            dh_ref[...] += part

    act = pl.BlockSpec((ts, FF_TILE), lambda i, j: (i, j))
    row = pl.BlockSpec((ts, D_MODEL), lambda i, j: (i, 0))
    return pl.pallas_call(
        body, name="ffn_bwd", grid=(S // ts, D_FF // FF_TILE),
        in_specs=[row, act, act,
                  pl.BlockSpec((D_MODEL, FF_TILE), lambda i, j: (0, j)),
                  pl.BlockSpec((D_MODEL, FF_TILE), lambda i, j: (0, j)),
                  pl.BlockSpec((FF_TILE, D_MODEL), lambda i, j: (j, 0))],
        out_specs=[act, act, act, row],
        out_shape=[jax.ShapeDtypeStruct((S, D_FF), bf16)] * 3 + [jax.ShapeDtypeStruct((S, D_MODEL), f32)],
        compiler_params=_params("parallel", "arbitrary"),
    )(df, gate, up, w_gate, w_up, w_down)


def _norm_bwd(dh2, dy, x2, mix, g3, g2):
    S = dh2.shape[0]
    ts = 512

    def body(dh_ref, dy_ref, x2_ref, mix_ref, g3_ref, g2_ref, dx2_ref, dmix_ref, dg3_ref, dg2_ref):
        @pl.when(pl.program_id(0) == 0)
        def _():
            dg3_ref[...] = jnp.zeros_like(dg3_ref)
            dg2_ref[...] = jnp.zeros_like(dg2_ref)

        dh = dh_ref[...]
        x2 = x2_ref[...]
        r3 = _rms(x2)
        xhat = x2 * r3
        dg3_ref[...] += jnp.sum(dh * xhat, axis=0, keepdims=True)
        dhg = dh * g3_ref[...]
        dx2 = dy_ref[...] + r3 * (dhg - xhat * jnp.mean(dhg * xhat, axis=-1, keepdims=True))
        dx2_ref[...] = dx2
        mix = mix_ref[...]
        r2 = _rms(mix)
        mhat = mix * r2
        dg2_ref[...] += jnp.sum(dx2 * mhat, axis=0, keepdims=True)
        dmg = dx2 * g2_ref[...]
        dmix_ref[...] = (r2 * (dmg - mhat * jnp.mean(dmg * mhat, axis=-1, keepdims=True))).astype(bf16)

    row = pl.BlockSpec((ts, D_MODEL), lambda i: (i, 0))
    gain = pl.BlockSpec((1, D_MODEL), lambda i: (0, 0))
    return pl.pallas_call(
        body, name="norm_bwd", grid=(S // ts,), in_specs=[row, row, row, row, gain, gain],
        out_specs=[row, row, gain, gain],
        out_shape=[jax.ShapeDtypeStruct((S, D_MODEL), f32), jax.ShapeDtypeStruct((S, D_MODEL), bf16),
                   jax.ShapeDtypeStruct((1, D_MODEL), f32), jax.ShapeDtypeStruct((1, D_MODEL), f32)],
        compiler_params=_params("arbitrary"),
    )(dh2, dy, x2, mix, g3, g2)


def _out_proj_bwd(dmix, w_out, attn_out, head_ones):
    S = dmix.shape[0]
    ts = 512

    def body(dm_ref, w_ref, o_ref, ones_ref, dp_ref, dl_ref, *rest):
        do_refs, stage = rest[:-N_STAGE], rest[-N_STAGE:]
        dcat = _dot_nt(dm_ref[...], w_ref[...])
        dp_ref[...] = dcat[:, :POOL_WIDTH]
        do = dcat[:, POOL_WIDTH:]
        for j in range(ATTN_WIDTH // 128):
            stage[j][...] = do[:, j * 128:(j + 1) * 128]
        _store_streams(stage, do_refs, ts)
        prod = do * o_ref[...].astype(f32)
        hi = prod.astype(bf16)
        lo = (prod - hi.astype(f32)).astype(bf16)
        dl_ref[...] = _dot(hi, ones_ref[...]) + _dot(lo, ones_ref[...])

    row = lambda w: pl.BlockSpec((ts, w), lambda i: (i, 0))
    res = pl.pallas_call(
        body, name="out_proj_bwd", grid=(S // ts,),
        in_specs=[row(D_MODEL), pl.BlockSpec((D_MODEL, D_MODEL), lambda i: (0, 0)), row(ATTN_WIDTH),
                  pl.BlockSpec((ATTN_WIDTH, ATTN_WIDTH), lambda i: (0, 0))],
        out_specs=[row(POOL_WIDTH), row(ATTN_WIDTH)] + [_stream_spec(d, ts) for d in DILATIONS],
        out_shape=[jax.ShapeDtypeStruct((S, POOL_WIDTH), f32), jax.ShapeDtypeStruct((S, ATTN_WIDTH), f32)]
        + [_stream_shape(S, d) for d in DILATIONS],
        scratch_shapes=_stage_scratch(ts),
        compiler_params=_params("parallel"),
    )(dmix, w_out, attn_out, head_ones)
    return res[0], res[1], res[2:]


def _in_proj_bwd(du, dq, dk, dv, cos_t, sin_t, w_in, x, dx2, g1):
    S = x.shape[0]
    ts = 256

    def body(du_ref, dq_ref, dk_ref, dv_ref, cos_ref, sin_ref, w_ref, x_ref, dx2_ref, g_ref, gx_ref, dproj_ref, dg_ref):
        @pl.when(pl.program_id(0) == 0)
        def _():
            dg_ref[...] = jnp.zeros_like(dg_ref)

        dproj_ref[:, :POOL_WIDTH] = du_ref[...]
        cos = cos_ref[...]
        sin = sin_ref[...]
        first = _first_half_mask(ts)
        for j in range(ATTN_WIDTH // 128):
            cols = slice(j * 128, (j + 1) * 128)
            for base, ref in ((POOL_WIDTH, dq_ref), (POOL_WIDTH + ATTN_WIDTH, dk_ref)):
                g = ref[:, cols].astype(f32)
                pre = g * cos + _rope_partner(g * sin, first)
                dproj_ref[:, base + j * 128: base + (j + 1) * 128] = pre.astype(bf16)
        dproj_ref[:, POOL_WIDTH + 2 * ATTN_WIDTH:] = dv_ref[...]

        dh = _dot_nt(dproj_ref[...], w_ref[...])
        xv = x_ref[...]
        r = _rms(xv)
        xhat = xv * r
        dg_ref[...] += jnp.sum(dh * xhat, axis=0, keepdims=True)
        dhg = dh * g_ref[...]
        gx_ref[...] = dx2_ref[...] + r * (dhg - xhat * jnp.mean(dhg * xhat, axis=-1, keepdims=True))

    row = lambda w: pl.BlockSpec((ts, w), lambda i: (i, 0))
    gain = pl.BlockSpec((1, D_MODEL), lambda i: (0, 0))
    return pl.pallas_call(
        body, name="in_proj_bwd", grid=(S // ts,),
        in_specs=[row(POOL_WIDTH)] + [row(ATTN_WIDTH)] * 3 + [row(128), row(128),
                  pl.BlockSpec((D_MODEL, IN_WIDTH), lambda i: (0, 0)), row(D_MODEL), row(D_MODEL), gain],
        out_specs=[row(D_MODEL), row(IN_WIDTH), gain],
        out_shape=[jax.ShapeDtypeStruct((S, D_MODEL), f32), jax.ShapeDtypeStruct((S, IN_WIDTH), bf16),
                   jax.ShapeDtypeStruct((1, D_MODEL), f32)],
        compiler_params=_params("arbitrary"),
    )(du, dq, dk, dv, cos_t, sin_t, w_in, x, dx2, g1)


def _matmul_tn(a, b, tn, name):
    K, M = a.shape
    N = b.shape[1]
    tk = 512

    def body(a_ref, b_ref, o_ref):
        part = _dot_tn(a_ref[...], b_ref[...])

        @pl.when(pl.program_id(1) == 0)
        def _():
            o_ref[...] = part

        @pl.when(pl.program_id(1) > 0)
        def _():
            o_ref[...] += part

    return pl.pallas_call(
        body, name=name, grid=(N // tn, K // tk),
        in_specs=[pl.BlockSpec((tk, M), lambda n, k: (k, 0)), pl.BlockSpec((tk, tn), lambda n, k: (k, n))],
        out_specs=pl.BlockSpec((M, tn), lambda n, k: (0, n)),
        out_shape=jax.ShapeDtypeStruct((M, N), f32),
        compiler_params=_params("parallel", "arbitrary"),
    )(a, b)


def _rope_tables(S):
    half = HEAD_DIM // 2
    freqs = ROPE_THETA ** (-jnp.arange(half, dtype=f32) * (2.0 / HEAD_DIM))
    ang = jnp.arange(S).astype(f32)[:, None] * freqs[None, :]
    cos = jnp.tile(jnp.cos(ang), (1, 4))
    sin = jnp.sin(ang)
    sin = jnp.tile(jnp.concatenate([-sin, sin], axis=1), (1, 2))
    return cos, sin


def _block_diag(w_pool):
    w = jnp.zeros((POOL_WIDTH, POOL_WIDTH), w_pool.dtype)
    for g in range(POOL_WIDTH // POOL_GROUP):
        w = lax.dynamic_update_slice(w, w_pool[g], (g * POOL_GROUP, g * POOL_GROUP))
    return w


def _head_ones():
    head = np.arange(ATTN_WIDTH) // HEAD_DIM
    return jnp.asarray(head[:, None] == head[None, :], dtype=bf16)


def _place():
    x, y, c = lax.axis_index("x"), lax.axis_index("y"), lax.axis_index("c")
    chips = [(1 - x, y), (x, 1 - y), (1 - x, 1 - y)]
    return x, y, c, chips


ANY = pl.BlockSpec(memory_space=pl.ANY)
N_PEER_CHIPS = N_CHIPS - 1
ICI_PIECES = 4
D2D_PIECES = 8
LOCAL_PIECES = 8


def _row_chunks(rows, n, unit=32):
    units = rows // unit
    out, start = [], 0
    for i in range(n):
        size = (units // n + (1 if i < units % n else 0)) * unit
        out.append((start, size))
        start += size
    return [piece for piece in out if piece[1]]


class _LocalCopy:
    def __init__(self, src_rows, dst_rows, rows, buf, sems_in, sems_out):
        self.loads, self.stores = [], []
        for i, (start, size) in enumerate(_row_chunks(rows, LOCAL_PIECES)):
            r = pl.ds(start, size)
            self.loads.append(pltpu.make_async_copy(src_rows(r), buf.at[r], sems_in.at[i]))
            self.stores.append(pltpu.make_async_copy(buf.at[r], dst_rows(r), sems_out.at[i]))

    def start(self):
        for cp in self.loads:
            cp.start()

    def pass_on(self):
        for load, store in zip(self.loads, self.stores):
            load.wait()
            store.start()

    def finish(self):
        for store in self.stores:
            store.wait()

    @staticmethod
    def scratch(rows, dtype):
        return [pltpu.VMEM((rows, D_MODEL), dtype), pltpu.SemaphoreType.DMA((LOCAL_PIECES,)),
                pltpu.SemaphoreType.DMA((LOCAL_PIECES,))]


class _Gather:
    def __init__(self, w_ref, out_ref, send1, recv1, send2, recv2, buf, sems_in, sems_out):
        x, y, c, chips = _place()
        me = 2 * x + y
        rows = w_ref.shape[0]
        half = rows // 2
        pieces = _row_chunks(half, ICI_PIECES)
        self.own = _LocalCopy(lambda r: w_ref.at[r], lambda r: out_ref.at[me, r], rows, buf, sems_in, sems_out)

        def rows_of(core, piece):
            start, size = piece
            return pl.ds(core * half + start, size)

        self.sends, self.arrivals, self.forwards, self.forward_arrivals = [], [], [], []
        for i, piece in enumerate(pieces):
            for j, (cx, cy) in enumerate(chips):
                k = j * len(pieces) + i
                there = 2 * cx + cy

                def direct(src_chip, cx=cx, cy=cy, k=k, piece=piece):
                    return pltpu.make_async_remote_copy(
                        src_ref=w_ref.at[rows_of(c, piece)], dst_ref=out_ref.at[src_chip, rows_of(c, piece)],
                        send_sem=send1.at[k], recv_sem=recv1.at[k], device_id=(cx, cy, c), device_id_type=MESH)

                def passed(core, there=there, k=k, piece=piece):
                    return pltpu.make_async_remote_copy(
                        src_ref=out_ref.at[there, rows_of(core, piece)], dst_ref=out_ref.at[there, rows_of(core, piece)],
                        send_sem=send2.at[k], recv_sem=recv2.at[k], device_id=(x, y, 1 - c), device_id_type=MESH)

                self.sends.append(direct(me))
                self.arrivals.append(direct(there))
                self.forwards.append(passed(c))
                self.forward_arrivals.append(passed(1 - c))

    def start(self):
        for cp in self.sends:
            cp.start()
        self.own.start()

    def pass_on(self):
        self.own.pass_on()
        for arrival, forward in zip(self.arrivals, self.forwards):
            arrival.wait_recv()
            forward.start()

    def finish(self):
        for arrival in self.forward_arrivals:
            arrival.wait_recv()
        for cp in self.sends + self.forwards:
            cp.wait_send()
        self.own.finish()

    @staticmethod
    def scratch(rows, dtype):
        n = N_PEER_CHIPS * len(_row_chunks(rows // 2, ICI_PIECES))
        return [pltpu.SemaphoreType.DMA((n,))] * 4 + _LocalCopy.scratch(rows, dtype)

    @staticmethod
    def out_shape(rows, dtype):
        return jax.ShapeDtypeStruct((N_CHIPS, rows, D_MODEL), dtype)


def _gather_weights(pack):
    rows = pack.shape[0]

    def body(w_ref, out_ref, *scratch):
        gather = _Gather(w_ref, out_ref, *scratch)
        gather.start()
        gather.pass_on()
        gather.finish()

    return pl.pallas_call(
        body, name="gather_weights", in_specs=[ANY], out_specs=ANY, out_shape=_Gather.out_shape(rows, pack.dtype),
        scratch_shapes=_Gather.scratch(rows, pack.dtype),
        compiler_params=pltpu.CompilerParams(vmem_limit_bytes=VMEM_LIMIT_V7X),
    )(pack)


class _Scatter:
    def __init__(self, h_ref, out_ref, send, recv):
        x, y, c, chips = _place()
        pieces = _row_chunks(h_ref.shape[1], ICI_PIECES)
        self.copies = []
        for i, (start, size) in enumerate(pieces):
            for j, (cx, cy) in enumerate(chips):
                k = j * len(pieces) + i
                self.copies.append(pltpu.make_async_remote_copy(
                    src_ref=h_ref.at[2 * cx + cy, pl.ds(start, size)], dst_ref=out_ref.at[j, pl.ds(start, size)],
                    send_sem=send.at[k], recv_sem=recv.at[k], device_id=(cx, cy, c), device_id_type=MESH))

    def start(self):
        for cp in self.copies:
            cp.start()

    def finish(self):
        for cp in self.copies:
            cp.wait_recv()
        for cp in self.copies:
            cp.wait_send()

    @staticmethod
    def scratch(half):
        n = N_PEER_CHIPS * len(_row_chunks(half, ICI_PIECES))
        return [pltpu.SemaphoreType.DMA((n,))] * 2

    @staticmethod
    def out_shape(half, dtype):
        return jax.ShapeDtypeStruct((N_PEER_CHIPS, half, D_MODEL), dtype)


def _scatter_to_chips(h):
    half = h.shape[1]

    def body(h_ref, out_ref, send, recv):
        scatter = _Scatter(h_ref, out_ref, send, recv)
        scatter.start()
        scatter.finish()

    return pl.pallas_call(
        body, name="scatter_to_chips", in_specs=[ANY], out_specs=ANY, out_shape=_Scatter.out_shape(half, h.dtype),
        scratch_shapes=_Scatter.scratch(half),
    )(h)


def _swap_halves(g, name):
    half = g.shape[1] // 2
    pieces = _row_chunks(half, D2D_PIECES)
    n = len(pieces)

    def body(g_ref, theirs_ref, send, recv):
        x, y, c, _ = _place()
        copies = []
        for s in range(N_CHIPS):
            for i, (start, size) in enumerate(pieces):
                copies.append(pltpu.make_async_remote_copy(
                    src_ref=g_ref.at[s, pl.ds((1 - c) * half + start, size)], dst_ref=theirs_ref.at[s, pl.ds(start, size)],
                    send_sem=send.at[s * n + i], recv_sem=recv.at[s * n + i],
                    device_id=(x, y, 1 - c), device_id_type=MESH))
        for cp in copies:
            cp.start()
        for cp in copies:
            cp.wait()

    return pl.pallas_call(
        body, name=name, in_specs=[ANY], out_specs=ANY,
        out_shape=jax.ShapeDtypeStruct((N_CHIPS, half, D_MODEL), g.dtype),
        scratch_shapes=[pltpu.SemaphoreType.DMA((N_CHIPS * n,))] * 2,
    )(g)


def _add_tile(half):
    for tr in (528, 448, 376, 352, 224, 176, 112, 64, 32):
        if half % tr == 0:
            return tr
    return half


def _add_cores(g, theirs, name):
    half = theirs.shape[1]
    tr = _add_tile(half)
    n_t = half // tr

    def body(c_ref, g_ref, t_ref, o_ref):
        o_ref[...] = g_ref[...] + t_ref[...]

    blk = pl.BlockSpec((1, tr, D_MODEL), lambda s, t, c_ref: (s, t, 0))
    return pl.pallas_call(
        body, name=name,
        grid_spec=pltpu.PrefetchScalarGridSpec(
            num_scalar_prefetch=1, grid=(N_CHIPS, n_t),
            in_specs=[pl.BlockSpec((1, tr, D_MODEL), lambda s, t, c_ref: (s, c_ref[0] * n_t + t, 0)), blk],
            out_specs=blk),
        out_shape=jax.ShapeDtypeStruct(theirs.shape, theirs.dtype),
        compiler_params=_params("parallel", "parallel"),
    )(lax.axis_index("c").astype(jnp.int32).reshape(1), g, theirs)


def _add_chips(chip_sum, others, name):
    half = chip_sum.shape[1]
    tr = _add_tile(half)

    def body(me_ref, own_ref, o0, o1, o2, out_ref):
        out_ref[...] = ((own_ref[0] + o0[0]) + o1[0]) + o2[0]

    other = lambda j: pl.BlockSpec((1, tr, D_MODEL), lambda t, me_ref: (j, t, 0))
    return pl.pallas_call(
        body, name=name,
        grid_spec=pltpu.PrefetchScalarGridSpec(
            num_scalar_prefetch=1, grid=(half // tr,),
            in_specs=[pl.BlockSpec((1, tr, D_MODEL), lambda t, me_ref: (me_ref[0], t, 0)), other(0), other(1), other(2)],
            out_specs=pl.BlockSpec((tr, D_MODEL), lambda t, me_ref: (t, 0))),
        out_shape=jax.ShapeDtypeStruct((half, D_MODEL), chip_sum.dtype),
        compiler_params=_params("parallel"),
    )((2 * lax.axis_index("x") + lax.axis_index("y")).astype(jnp.int32).reshape(1), chip_sum, others, others, others)


def _join_halves(r):
    half = r.shape[0]
    pieces = _row_chunks(half, 2 * D2D_PIECES)
    n = len(pieces)

    def body(r_ref, out_ref, send, recv, buf, sems_in, sems_out):
        x, y, c, _ = _place()
        own = _LocalCopy(lambda rr: r_ref.at[rr], lambda rr: out_ref.at[c, rr], half, buf, sems_in, sems_out)
        own.start()

        def piece(i, core):
            start, size = pieces[i]
            return pltpu.make_async_remote_copy(
                src_ref=r_ref.at[pl.ds(start, size)], dst_ref=out_ref.at[core, pl.ds(start, size)],
                send_sem=send.at[i], recv_sem=recv.at[i], device_id=(x, y, 1 - c), device_id_type=MESH)

        copies = [piece(i, c) for i in range(n)]
        for cp in copies:
            cp.start()
        own.pass_on()
        for i in range(n):
            piece(i, 1 - c).wait_recv()
        for cp in copies:
            cp.wait_send()
        own.finish()

    return pl.pallas_call(
        body, name="join_halves", in_specs=[ANY], out_specs=ANY,
        out_shape=jax.ShapeDtypeStruct((2,) + r.shape, r.dtype),
        scratch_shapes=[pltpu.SemaphoreType.DMA((n,))] * 2 + _LocalCopy.scratch(half, r.dtype),
        compiler_params=pltpu.CompilerParams(vmem_limit_bytes=VMEM_LIMIT_V7X),
    )(r)


def _sum_small(block):
    def body(b_ref, out_ref, gathered, send, recv):
        x, y, c, _ = _place()
        me = 4 * x + 2 * y + c
        gathered[me] = b_ref[...]
        sends = []
        for kk in range(1, N_DEV):
            flip = lambda v, bit: 1 - v if bit else v
            peer = (flip(x, kk & 4), flip(y, kk & 2), flip(c, kk & 1))
            cp = pltpu.make_async_remote_copy(
                src_ref=b_ref, dst_ref=gathered.at[me], send_sem=send.at[kk - 1], recv_sem=recv.at[kk - 1],
                device_id=peer, device_id_type=MESH)
            cp.start()
            sends.append(cp)
        for kk in range(1, N_DEV):
            peer_index = jnp.bitwise_xor(me, kk)
            pltpu.make_async_remote_copy(
                src_ref=b_ref, dst_ref=gathered.at[peer_index], send_sem=send.at[kk - 1], recv_sem=recv.at[kk - 1],
                device_id=(x, y, c), device_id_type=MESH).wait_recv()
        for cp in sends:
            cp.wait_send()
        acc = gathered[0]
        for dev in range(1, N_DEV):
            acc = acc + gathered[dev]
        out_ref[...] = acc

    vmem = pl.BlockSpec(memory_space=pltpu.VMEM)
    return pl.pallas_call(
        body, name="sum_small", in_specs=[vmem], out_specs=vmem,
        out_shape=jax.ShapeDtypeStruct(block.shape, block.dtype),
        scratch_shapes=[pltpu.VMEM((N_DEV,) + block.shape, block.dtype),
                        pltpu.SemaphoreType.DMA((N_DEV - 1,)), pltpu.SemaphoreType.DMA((N_DEV - 1,))],
    )(block)


def _adamw(w, g, m, v, name):
    rows, cols = w.shape
    tr = rows
    for cand in (512, 256, 128, 64, 32, 16, 8):
        if rows % cand == 0:
            tr = cand
            break
    c1 = 1.0 - ADAM_B1 ** ADAM_STEP
    c2 = 1.0 - ADAM_B2 ** ADAM_STEP

    def body(w_ref, g_ref, m_ref, v_ref, d_ref, nm_ref, nv_ref):
        gv = g_ref[...]
        nm = ADAM_B1 * m_ref[...] + (1.0 - ADAM_B1) * gv
        nv = ADAM_B2 * v_ref[...] + (1.0 - ADAM_B2) * (gv * gv)
        nm_ref[...] = nm
        nv_ref[...] = nv
        d_ref[...] = -ADAM_LR * ((nm / c1) / (jnp.sqrt(nv / c2) + ADAM_EPS) + ADAM_WD * w_ref[...])

    blk = pl.BlockSpec((tr, cols), lambda i: (i, 0))
    shape = jax.ShapeDtypeStruct((rows, cols), f32)
    return pl.pallas_call(
        body, name=name, grid=(rows // tr,), in_specs=[blk] * 4, out_specs=[blk] * 3, out_shape=[shape] * 3,
        compiler_params=_params("parallel"),
    )(w, g, m, v)


LARGE = ("w_in", "w_out", "w_gate", "w_up", "w_down")
SMALL = ("ln_pre_mix", "ln_post_mix", "ln_pre_ffn", "ln_post_ffn", "pool_scale", "w_pool")
SHARD_ROWS = {"w_in": 640, "w_out": 256, "w_gate": 704, "w_up": 704, "w_down": 704}
COLUMN_SHARDED = {"w_in": IN_WIDTH // N_CHIPS, "w_gate": D_FF // N_CHIPS, "w_up": D_FF // N_CHIPS}
NEEDED_FIRST = ("w_in",)
NEEDED_LATER = ("w_out", "w_gate", "w_up", "w_down")
READY_EARLY = ("w_gate", "w_up", "w_down")
READY_LATE = ("w_in", "w_out")


def _pack_shard(shards, names):
    return jnp.concatenate([shards[n].reshape(-1, D_MODEL) for n in names], axis=0)


def _unpack_shard(pack, names, shapes):
    out, row = {}, 0
    for n in names:
        out[n] = pack[row:row + SHARD_ROWS[n]].reshape(shapes[n])
        row += SHARD_ROWS[n]
    return out


def _whole_from_shards(packs, names):
    out, row = {}, 0
    for n in names:
        rows = SHARD_ROWS[n]
        part = packs[:, row:row + rows]
        if n in COLUMN_SHARDED:
            width = COLUMN_SHARDED[n]
            part = part.reshape(N_CHIPS, D_MODEL, width).transpose(1, 0, 2).reshape(D_MODEL, N_CHIPS * width)
        else:
            part = part.reshape(N_CHIPS * rows, D_MODEL)
        out[n] = part
        row += rows
    return out


def _shards_from_whole(grads, names):
    parts = []
    for n in names:
        g = grads[n]
        if n in COLUMN_SHARDED:
            width = COLUMN_SHARDED[n]
            g = g.reshape(D_MODEL, N_CHIPS, width).transpose(1, 0, 2)
        parts.append(g.reshape(N_CHIPS, SHARD_ROWS[n], D_MODEL))
    return jnp.concatenate(parts, axis=1)


def _pack_small(vals):
    rows = [vals[n].reshape(1, D_MODEL) for n in SMALL[:4]]
    rows.append(jnp.pad(vals["pool_scale"].reshape(1, POOL_WIDTH), ((0, 0), (0, D_MODEL - POOL_WIDTH))))
    rows.append(jnp.pad(vals["loss"].reshape(1, 1), ((0, 0), (0, D_MODEL - 1))))
    rows.append(jnp.zeros((2, D_MODEL), f32))
    rows.append(vals["w_pool"].reshape(16, D_MODEL))
    return jnp.concatenate(rows, axis=0)


def _unpack_small(block):
    out = {n: block[i:i + 1] for i, n in enumerate(SMALL[:4])}
    out["pool_scale"] = block[4:5, :POOL_WIDTH]
    out["loss"] = block[5, 0]
    out["w_pool"] = block[8:24].reshape(1, 4, POOL_GROUP, POOL_GROUP)
    return out


def kernel(x, ln_pre_mix, w_in, w_pool, pool_scale, w_out, ln_post_mix, ln_pre_ffn, w_gate, w_up, w_down, ln_post_ffn, loss_target, m_ln_pre_mix, m_w_in, m_w_pool, m_pool_scale, m_w_out, m_ln_post_mix, m_ln_pre_ffn, m_w_gate, m_w_up, m_w_down, m_ln_post_ffn, v_ln_pre_mix, v_w_in, v_w_pool, v_pool_scale, v_w_out, v_ln_post_mix, v_ln_pre_ffn, v_w_gate, v_w_up, v_w_down, v_ln_post_ffn):
    w = dict(ln_pre_mix=ln_pre_mix, w_in=w_in, w_pool=w_pool, pool_scale=pool_scale, w_out=w_out,
             ln_post_mix=ln_post_mix, ln_pre_ffn=ln_pre_ffn, w_gate=w_gate, w_up=w_up, w_down=w_down,
             ln_post_ffn=ln_post_ffn)
    m = dict(ln_pre_mix=m_ln_pre_mix, w_in=m_w_in, w_pool=m_w_pool, pool_scale=m_pool_scale, w_out=m_w_out,
             ln_post_mix=m_ln_post_mix, ln_pre_ffn=m_ln_pre_ffn, w_gate=m_w_gate, w_up=m_w_up, w_down=m_w_down,
             ln_post_ffn=m_ln_post_ffn)
    v = dict(ln_pre_mix=v_ln_pre_mix, w_in=v_w_in, w_pool=v_w_pool, pool_scale=v_pool_scale, w_out=v_w_out,
             ln_post_mix=v_ln_post_mix, ln_pre_ffn=v_ln_pre_ffn, w_gate=v_w_gate, w_up=v_w_up, w_down=v_w_down,
             ln_post_ffn=v_ln_post_ffn)

    xs, target = x[0], loss_target[0]
    cos_t, sin_t = _rope_tables(xs.shape[0])
    w_bd = _block_diag(w_pool[0]).astype(bf16)
    shard = {n: w[n][0].astype(bf16) for n in LARGE}

    w_in_whole = _whole_from_shards(_gather_weights(_pack_shard(shard, NEEDED_FIRST)), NEEDED_FIRST)["w_in"]
    h1, u, qs, ks, vs = _in_proj(xs, ln_pre_mix, w_in_whole, cos_t, sin_t)
    pool_out = _pool_fwd(u, w_bd, pool_scale)
    attn_out, lse, later = _attn_fwd(qs, ks, vs, _pack_shard(shard, NEEDED_LATER))
    whole = _whole_from_shards(later, NEEDED_LATER)
    mix, x2, h2 = _out_proj(pool_out, attn_out, whole["w_out"], xs, ln_post_mix, ln_pre_ffn)
    gate, up, f = _ffn_fwd(h2, whole["w_gate"], whole["w_up"], whole["w_down"])
    dy, df, dg4, loss = _loss_head(f, x2, target, ln_post_ffn)

    large = {}
    a, dgate, dup, dh2 = _ffn_bwd(df, gate, up, whole["w_gate"], whole["w_up"], whole["w_down"])
    large["w_down"] = _matmul_tn(a, df, D_MODEL, "grad_w_down")
    large["w_gate"] = _matmul_tn(h2, dgate, FF_HALF, "grad_w_gate")
    large["w_up"] = _matmul_tn(h2, dup, FF_HALF, "grad_w_up")
    early = _shards_from_whole(large, READY_EARLY)
    early_chip = _add_cores(early, _swap_halves(early, "swap_halves_early"), "add_cores_early")
    dx2, dmix, dg3, dg2 = _norm_bwd(dh2, dy, x2, mix, ln_pre_ffn, ln_post_mix)
    large["w_out"] = jnp.concatenate([_matmul_tn(pool_out, dmix, D_MODEL, "grad_w_out_pool"),
                                      _matmul_tn(attn_out, dmix, D_MODEL, "grad_w_out_attn")], axis=0)
    dpool, delta, dos = _out_proj_bwd(dmix, whole["w_out"], attn_out, _head_ones())
    du, d_w_bd, d_scale = _pool_bwd(u, dpool, w_bd, pool_scale)
    dq, dk, dv, early_others = _attn_bwd(qs, ks, vs, dos, lse, delta, early_chip)
    grad_x, dproj, dg1 = _in_proj_bwd(du, dq, dk, dv, cos_t, sin_t, w_in_whole, xs, dx2, ln_pre_mix)
    large["w_in"] = _matmul_tn(h1, dproj, IN_WIDTH // 2, "grad_w_in")
    late = _shards_from_whole(large, READY_LATE)
    late_chip = _add_cores(late, _swap_halves(late, "swap_halves_late"), "add_cores_late")
    late_others = _scatter_to_chips(late_chip)
    early_half = _add_chips(early_chip, early_others, "add_chips_early")
    late_half = _add_chips(late_chip, late_others, "add_chips_late")
    joined = _join_halves(jnp.concatenate([early_half, late_half], axis=0))
    n_early = early_half.shape[0]
    shapes = {n: w[n].shape[1:] for n in LARGE}
    grads = _unpack_shard(joined[:, :n_early].reshape(-1, D_MODEL), READY_EARLY, shapes)
    grads.update(_unpack_shard(joined[:, n_early:].reshape(-1, D_MODEL), READY_LATE, shapes))

    d_w_pool = jnp.stack([d_w_bd[g * POOL_GROUP:(g + 1) * POOL_GROUP, g * POOL_GROUP:(g + 1) * POOL_GROUP]
                          for g in range(POOL_WIDTH // POOL_GROUP)])
    small = dict(ln_pre_mix=dg1, ln_post_mix=dg2, ln_pre_ffn=dg3, ln_post_ffn=dg4, pool_scale=d_scale, w_pool=d_w_pool)
    total = _unpack_small(_sum_small(_pack_small(dict(small, loss=loss))))
    for n in SMALL:
        grads[n] = total[n]

    delta_w, new_m, new_v = {}, {}, {}
    for n in LARGE:
        delta_w[n], new_m[n], new_v[n] = _adamw(w[n][0], grads[n], m[n][0], v[n][0], "adamw_" + n)
    small_state = [_pack_small(dict({n: s[n] for n in SMALL}, loss=jnp.zeros((), f32))) for s in (w, m, v)]
    small_grad = _pack_small(dict({n: grads[n] for n in SMALL}, loss=jnp.zeros((), f32)))
    sd, sm, sv = _adamw(small_state[0], small_grad, small_state[1], small_state[2], "adamw_small")
    for out, block in ((delta_w, sd), (new_m, sm), (new_v, sv)):
        un = _unpack_small(block)
        for n in SMALL:
            out[n] = un[n]

    names = ("ln_pre_mix", "w_in", "w_pool", "pool_scale", "w_out", "ln_post_mix", "ln_pre_ffn", "w_gate", "w_up",
             "w_down", "ln_post_ffn")
    full = lambda d: [d[n].reshape(w[n].shape) for n in names]
    return (total["loss"], grad_x[None], *full(grads), *full(delta_w), *full(new_m), *full(new_v))
```

```python
import numpy as np
import jax
import jax.numpy as jnp
from jax import lax
from jax.experimental import pallas as pl
from jax.experimental.pallas import tpu as pltpu

D_MODEL = 1024
POOL_WIDTH = 256
POOL_GROUP = 64
ATTN_WIDTH = 768
HEAD_DIM = 64
IN_WIDTH = 2560
D_FF = 2816
BLOCK = 128
DILATIONS = (1, 4, 16)
ROPE_THETA = 10000.0
EPS = 1e-6
ATTN_SCALE = 0.125
NEG = -1e30

ADAM_LR = 0.001
ADAM_B1 = 0.9
ADAM_B2 = 0.999
ADAM_EPS = 1e-08
ADAM_WD = 0.01
ADAM_STEP = 10

N_CHIPS = 4
N_DEV = 8
VMEM_LIMIT_V7X = 56 * 1024 * 1024
MESH = pl.DeviceIdType.MESH

f32 = jnp.float32
bf16 = jnp.bfloat16


def _params(*sem):
    return pltpu.CompilerParams(dimension_semantics=sem, vmem_limit_bytes=VMEM_LIMIT_V7X)


def _dot(a, b):
    return jnp.dot(a, b, preferred_element_type=f32)


def _dot_nt(a, b):
    return lax.dot_general(a, b, (((1,), (1,)), ((), ())), preferred_element_type=f32)


def _dot_tn(a, b):
    return lax.dot_general(a, b, (((0,), (0,)), ((), ())), preferred_element_type=f32)


def _rope_partner(a, first_half):
    return jnp.where(first_half, pltpu.roll(a, 96, 1), pltpu.roll(a, 32, 1))


def _first_half_mask(rows):
    lane = lax.broadcasted_iota(jnp.int32, (rows, 128), 1)
    return (lane % HEAD_DIM) < (HEAD_DIM // 2)


def _stream_spec(d, ts):
    return pl.BlockSpec((d, ts // d, ATTN_WIDTH), lambda i: (0, i, 0))


def _stream_shape(S, d):
    return jax.ShapeDtypeStruct((d, S // d, ATTN_WIDTH), bf16)


N_STAGE = ATTN_WIDTH // 128


def _stage_scratch(ts):
    return [pltpu.VMEM((ts, 128), f32)] * N_STAGE


def _store_streams(stage, out_refs, ts):
    for d, ref in zip(DILATIONS, out_refs):
        for r in range(d):
            rows = pl.ds(0, ts) if d == 1 else pl.ds(r, ts // d, stride=d)
            for j in range(N_STAGE):
                ref[r, :, j * 128:(j + 1) * 128] = stage[j][rows, :].astype(bf16)


def _in_proj(x, g1, w_in, cos_t, sin_t):
    S = x.shape[0]
    ts = 512

    def body(x_ref, g_ref, w_ref, cos_ref, sin_ref, h_ref, u_ref, *rest):
        outs, stage = rest[:-N_STAGE], rest[-N_STAGE:]
        xv = x_ref[...]
        r = lax.rsqrt(jnp.mean(xv * xv, axis=-1, keepdims=True) + EPS)
        h = ((xv * r) * g_ref[...]).astype(bf16)
        h_ref[...] = h
        proj = _dot(h, w_ref[...])
        u_ref[...] = proj[:, :POOL_WIDTH]
        cos = cos_ref[...]
        sin = sin_ref[...]
        first = _first_half_mask(ts)
        n_dil = len(DILATIONS)
        for which, base in enumerate((POOL_WIDTH, POOL_WIDTH + ATTN_WIDTH)):
            for j in range(ATTN_WIDTH // 128):
                a = proj[:, base + j * 128: base + (j + 1) * 128]
                stage[j][...] = a * cos + _rope_partner(a, first) * sin
            _store_streams(stage, outs[which * n_dil:(which + 1) * n_dil], ts)
        for j in range(ATTN_WIDTH // 128):
            base = POOL_WIDTH + 2 * ATTN_WIDTH + j * 128
            stage[j][...] = proj[:, base:base + 128]
        _store_streams(stage, outs[2 * n_dil:], ts)

    row = lambda w: pl.BlockSpec((ts, w), lambda i: (i, 0))
    streams = [_stream_spec(d, ts) for d in DILATIONS] * 3
    res = pl.pallas_call(
        body, name="in_proj", grid=(S // ts,),
        in_specs=[row(D_MODEL), pl.BlockSpec((1, D_MODEL), lambda i: (0, 0)),
                  pl.BlockSpec((D_MODEL, IN_WIDTH), lambda i: (0, 0)), row(128), row(128)],
        out_specs=[row(D_MODEL), row(POOL_WIDTH)] + streams,
        out_shape=[jax.ShapeDtypeStruct((S, D_MODEL), bf16), jax.ShapeDtypeStruct((S, POOL_WIDTH), f32)]
        + [_stream_shape(S, d) for d in DILATIONS] * 3,
        scratch_shapes=_stage_scratch(ts),
        compiler_params=_params("parallel"),
    )(x, g1, w_in, cos_t, sin_t)
    n = len(DILATIONS)
    return res[0], res[1], res[2:2 + n], res[2 + n:2 + 2 * n], res[2 + 2 * n:]


POOL_HALO = 16


def _pool_lane_group(rows):
    return lax.broadcasted_iota(jnp.int32, (rows, POOL_WIDTH), 1) // POOL_GROUP


def _pool_select(group, s2, s4, s8, s16):
    return jnp.where(group == 0, s2, jnp.where(group == 1, s4, jnp.where(group == 2, s8, s16)))


def _pool_count(t0, rows):
    group = _pool_lane_group(rows)
    t = t0 + lax.broadcasted_iota(jnp.int32, (rows, POOL_WIDTH), 0)
    win = _pool_select(group, 2, 4, 8, 16)
    return jnp.minimum(t + 1, win).astype(f32)


def _pool_diff(u_halo, u_tile, t0):
    ts = u_tile.shape[0]
    ext = jnp.concatenate([u_halo, u_tile], axis=0)
    s2 = ext + pltpu.roll(ext, 1, 0)
    s4 = s2 + pltpu.roll(s2, 2, 0)
    s8 = s4 + pltpu.roll(s4, 4, 0)
    s16 = s8 + pltpu.roll(s8, 8, 0)
    group = _pool_lane_group(ts + POOL_HALO)
    wsum = _pool_select(group, s2, s4, s8, s16)[POOL_HALO:]
    return wsum / _pool_count(t0, ts) - u_tile


def _pool_specs(ts, n_tiles):
    tile = pl.BlockSpec((ts, POOL_WIDTH), lambda i: (i, 0))
    per = ts // POOL_HALO
    before = pl.BlockSpec((POOL_HALO, POOL_WIDTH), lambda i: (jnp.maximum(i * per - 1, 0), 0))
    after = pl.BlockSpec((POOL_HALO, POOL_WIDTH), lambda i: (jnp.minimum((i + 1) * per, n_tiles * per - 1), 0))
    return tile, before, after


def _pool_fwd(u, w_bd, scale):
    S = u.shape[0]
    ts = 512
    n_tiles = S // ts

    def body(u_ref, halo_ref, w_ref, sc_ref, y_ref):
        i = pl.program_id(0)
        halo = jnp.where(i > 0, halo_ref[...], 0.0)
        d = _pool_diff(halo, u_ref[...], i * ts)
        y_ref[...] = (_dot(d.astype(bf16), w_ref[...]) * sc_ref[...]).astype(bf16)

    tile, before, _ = _pool_specs(ts, n_tiles)
    return pl.pallas_call(
        body, name="pool_fwd", grid=(n_tiles,),
        in_specs=[tile, before, pl.BlockSpec((POOL_WIDTH, POOL_WIDTH), lambda i: (0, 0)),
                  pl.BlockSpec((1, POOL_WIDTH), lambda i: (0, 0))],
        out_specs=tile, out_shape=jax.ShapeDtypeStruct((S, POOL_WIDTH), bf16),
        compiler_params=_params("parallel"),
    )(u, u, w_bd, scale)


def _pool_bwd(u, dy, w_bd, scale):
    S = u.shape[0]
    ts = 512
    n_tiles = S // ts

    def body(u_ref, halo_ref, dy_ref, dy_next_ref, w_ref, sc_ref, du_ref, dw_ref, dsc_ref):
        i = pl.program_id(0)

        @pl.when(i == 0)
        def _():
            dw_ref[...] = jnp.zeros_like(dw_ref)
            dsc_ref[...] = jnp.zeros_like(dsc_ref)

        halo = jnp.where(i > 0, halo_ref[...], 0.0)
        d = _pool_diff(halo, u_ref[...], i * ts).astype(bf16)
        w = w_ref[...]
        sc = sc_ref[...]
        dy_tile = dy_ref[...]
        z = _dot(d, w)
        dsc_ref[...] += jnp.sum(dy_tile * z, axis=0, keepdims=True)
        dy_next = jnp.where(i < n_tiles - 1, dy_next_ref[...], 0.0)
        dz = (jnp.concatenate([dy_tile, dy_next], axis=0) * sc).astype(bf16)
        dw_ref[...] += _dot_tn(d, dz[:ts])
        dd = _dot_nt(dz, w)
        e = dd / _pool_count(i * ts, ts + POOL_HALO)
        n = ts + POOL_HALO
        f2 = e + pltpu.roll(e, n - 1, 0)
        f4 = f2 + pltpu.roll(f2, n - 2, 0)
        f8 = f4 + pltpu.roll(f4, n - 4, 0)
        f16 = f8 + pltpu.roll(f8, n - 8, 0)
        fsum = _pool_select(_pool_lane_group(n), f2, f4, f8, f16)
        du_ref[...] = (fsum[:ts] - dd[:ts]).astype(bf16)

    tile, before, after = _pool_specs(ts, n_tiles)
    return pl.pallas_call(
        body, name="pool_bwd", grid=(n_tiles,),
        in_specs=[tile, before, tile, after, pl.BlockSpec((POOL_WIDTH, POOL_WIDTH), lambda i: (0, 0)),
                  pl.BlockSpec((1, POOL_WIDTH), lambda i: (0, 0))],
        out_specs=[tile, pl.BlockSpec((POOL_WIDTH, POOL_WIDTH), lambda i: (0, 0)),
                   pl.BlockSpec((1, POOL_WIDTH), lambda i: (0, 0))],
        out_shape=[jax.ShapeDtypeStruct((S, POOL_WIDTH), bf16), jax.ShapeDtypeStruct((POOL_WIDTH, POOL_WIDTH), f32),
                   jax.ShapeDtypeStruct((1, POOL_WIDTH), f32)],
        compiler_params=_params("arbitrary"),
    )(u, u, dy, dy, w_bd, scale)


SUPER = BLOCK * DILATIONS[-1]
UNITS = SUPER // BLOCK
FWD_UNROLL = 4
BWD_UNROLL = 4


def _band_mask(has_prev):
    qi = lax.broadcasted_iota(jnp.int32, (BLOCK, 2 * BLOCK), 0)
    kj = lax.broadcasted_iota(jnp.int32, (BLOCK, 2 * BLOCK), 1)
    return (kj >= qi) & (kj <= qi + BLOCK) & ((kj >= BLOCK) | has_prev)


def _head0_mask(rows=BLOCK):
    return lax.broadcasted_iota(jnp.int32, (rows, 128), 1) < HEAD_DIM


def _band_mask_t(has_prev):
    ki = lax.broadcasted_iota(jnp.int32, (2 * BLOCK, 2 * BLOCK), 0)
    qj = lax.broadcasted_iota(jnp.int32, (2 * BLOCK, 2 * BLOCK), 1) % BLOCK
    return (ki >= qj) & (ki <= qj + BLOCK) & ((ki >= BLOCK) | has_prev)


def _head_pair_rows(a, h0):
    zero = jnp.zeros_like(a)
    return jnp.concatenate([jnp.where(h0, a, zero), jnp.where(h0, zero, a)], axis=0)


def _per_query_row(stat):
    t = stat.T
    return jnp.concatenate([jnp.concatenate([t[:HEAD_DIM]] * 4, axis=0), jnp.concatenate([t[HEAD_DIM:]] * 4, axis=0)],
                           axis=1)


def _natural_rows(d, r, n):
    if d == 1:
        return pl.ds(pl.multiple_of(n * BLOCK, BLOCK), BLOCK)
    return pl.ds(n * (BLOCK * d) + r, BLOCK, stride=d)


def _unit_place(d, u):
    per_stream = UNITS // d
    return u // per_stream, u % per_stream, per_stream


def _block_rows(n):
    return pl.ds(pl.multiple_of(n * BLOCK, BLOCK), BLOCK)


def _band(cur_ref, tail_ref, r, n):
    before = jnp.where(n > 0, cur_ref[r, _block_rows(jnp.maximum(n - 1, 0)), :], tail_ref[r])
    return jnp.concatenate([before, cur_ref[r, _block_rows(n), :]], axis=0)


def _attn_in_specs(S, with_do):
    specs = []
    last = S // SUPER - 1
    for d in DILATIONS:
        per_stream = UNITS // d
        cur = pl.BlockSpec((d, SUPER // d, 128), lambda hp, sb: (0, jnp.minimum(sb, last), hp))
        tail = pl.BlockSpec(
            (d, BLOCK, 128),
            lambda hp, sb, per_stream=per_stream: (0, jnp.maximum(jnp.minimum(sb, last) * per_stream - 1, 0), hp))
        specs += [cur] * (2 if with_do else 1) + [cur, tail, cur, tail]
    return specs


def _attn_fwd(qs, ks, vs, pack):
    S = qs[0].shape[1]
    n_dil = len(DILATIONS)
    n_steps = S // SUPER
    n_total = (ATTN_WIDTH // 128) * n_steps

    def body(*refs):
        ins, pack_ref = refs[:5 * n_dil], refs[5 * n_dil]
        out_ref, lse_ref, gathered_ref = refs[5 * n_dil + 1:5 * n_dil + 4]
        scratch = refs[5 * n_dil + 4:]
        o_sc, l_sc = scratch[:n_dil], scratch[n_dil:2 * n_dil]
        gather = _Gather(pack_ref, gathered_ref, *scratch[2 * n_dil:])
        sb = pl.program_id(1)
        step = pl.program_id(0) * n_steps + sb

        @pl.when(step == 0)
        def _():
            gather.start()

        h0 = _head0_mask()
        for ci, d in enumerate(DILATIONS):
            q_ref, kc_ref, kp_ref, vc_ref, vp_ref = ins[5 * ci:5 * ci + 5]

            def unit(u, carry, d=d, ci=ci, q_ref=q_ref, kc_ref=kc_ref, kp_ref=kp_ref, vc_ref=vc_ref, vp_ref=vp_ref):
                r, n, _ = _unit_place(d, u)
                qv = q_ref[r, _block_rows(n), :]
                kb = _band(kc_ref, kp_ref, r, n)
                vb = _band(vc_ref, vp_ref, r, n)
                valid = _band_mask((sb > 0) | (n > 0))
                outs, lses = [], []
                for h in range(2):
                    keep = h0 if h == 0 else jnp.logical_not(h0)
                    qh = jnp.where(keep, qv, jnp.zeros_like(qv))
                    s = jnp.where(valid, _dot_nt(qh, kb) * ATTN_SCALE, NEG)
                    m = jnp.max(s, axis=1, keepdims=True)
                    e = jnp.exp(s - m)
                    den = jnp.sum(e, axis=1, keepdims=True)
                    outs.append(_dot((e / den).astype(bf16), vb))
                    lses.append(jnp.broadcast_to(m + jnp.log(den), (BLOCK, 128)))
                rows = _natural_rows(d, r, n)
                o_sc[ci][rows, :] = jnp.where(h0, outs[0], outs[1])
                l_sc[ci][rows, :] = jnp.where(h0, lses[0], lses[1])
                return carry

            lax.fori_loop(0, UNITS, unit, 0, unroll=FWD_UNROLL)

        def merge(t, carry):
            rows = pl.ds(pl.multiple_of(t * 256, 256), 256)
            a, b, c = l_sc[0][rows, :], l_sc[1][rows, :], l_sc[2][rows, :]
            m = jnp.maximum(jnp.maximum(a, b), c)
            ea, eb, ec = jnp.exp(a - m), jnp.exp(b - m), jnp.exp(c - m)
            tot = ea + eb + ec
            out_ref[rows, :] = ((ea / tot) * o_sc[0][rows, :] + (eb / tot) * o_sc[1][rows, :]
                                + (ec / tot) * o_sc[2][rows, :]).astype(bf16)
            lse_ref[rows, :] = m + jnp.log(tot)
            return carry

        lax.fori_loop(0, SUPER // 256, merge, 0)

        @pl.when(step == n_total // 2)
        def _():
            gather.pass_on()

        @pl.when(step == n_total - 1)
        def _():
            gather.finish()

    args = []
    for q, k, v in zip(qs, ks, vs):
        args += [q, k, k, v, v]
    nat = pl.BlockSpec((SUPER, 128), lambda hp, sb: (sb, hp))
    rows = pack.shape[0]
    return pl.pallas_call(
        body, name="attn_fwd", grid=(ATTN_WIDTH // 128, n_steps),
        in_specs=_attn_in_specs(S, False) + [ANY], out_specs=[nat, nat, ANY],
        out_shape=[jax.ShapeDtypeStruct((S, ATTN_WIDTH), bf16), jax.ShapeDtypeStruct((S, ATTN_WIDTH), f32),
                   _Gather.out_shape(rows, pack.dtype)],
        scratch_shapes=[pltpu.VMEM((SUPER, 128), f32)] * (2 * n_dil) + _Gather.scratch(rows, pack.dtype),
        compiler_params=_params("arbitrary", "arbitrary"),
    )(*args, pack)


def _attn_bwd(qs, ks, vs, dos, lse, delta, chip_sum):
    S = qs[0].shape[1]
    n_steps = S // SUPER
    last = n_steps - 1
    n_dil = len(DILATIONS)
    n_total = (ATTN_WIDTH // 128) * (n_steps + 1)

    def body(*refs):
        ins, (lse_ref, dl_ref, sum_ref) = refs[:6 * n_dil], refs[6 * n_dil:6 * n_dil + 3]
        dq_ref, dk_ref, dv_ref, others_ref = refs[6 * n_dil + 3:6 * n_dil + 7]
        dq_acc, dk_acc, dv_acc = refs[6 * n_dil + 7:6 * n_dil + 10]
        scatter = _Scatter(sum_ref, others_ref, *refs[6 * n_dil + 10:])
        sb = pl.program_id(1)
        step = pl.program_id(0) * (n_steps + 1) + sb
        cur = sb % 2
        prv = 1 - cur

        @pl.when(step == 0)
        def _():
            scatter.start()

        @pl.when(sb < n_steps)
        def _():
            dq_acc[...] = jnp.zeros_like(dq_acc)
            dk_acc[cur] = jnp.zeros((SUPER, 128), f32)
            dv_acc[cur] = jnp.zeros((SUPER, 128), f32)
            h0 = _head0_mask()
            for ci, d in enumerate(DILATIONS):
                q_ref, do_ref, kc_ref, kp_ref, vc_ref, vp_ref = ins[6 * ci:6 * ci + 6]

                def unit(u, carry, d=d, q_ref=q_ref, do_ref=do_ref, kc_ref=kc_ref, kp_ref=kp_ref, vc_ref=vc_ref,
                         vp_ref=vp_ref):
                    r, n, per_stream = _unit_place(d, u)
                    qv = q_ref[r, _block_rows(n), :]
                    dov = do_ref[r, _block_rows(n), :]
                    kb = _band(kc_ref, kp_ref, r, n)
                    vb = _band(vc_ref, vp_ref, r, n)
                    rows = _natural_rows(d, r, n)
                    has_prev = (sb > 0) | (n > 0)
                    q_pair = _head_pair_rows(qv, h0)
                    do_pair = _head_pair_rows(dov, h0)
                    s_t = jnp.where(_band_mask_t(has_prev), _dot_nt(kb, q_pair) * ATTN_SCALE, NEG)
                    p_t = jnp.exp(s_t - _per_query_row(lse_ref[rows, :]))
                    dp_t = _dot_nt(vb, do_pair)
                    ds_t = (p_t * (dp_t - _per_query_row(dl_ref[rows, :])) * ATTN_SCALE).astype(bf16)
                    dvb = _dot(p_t.astype(bf16), do_pair)
                    dkb = _dot(ds_t, q_pair)
                    dq_pair = _dot_tn(ds_t, kb)
                    dq_acc[rows, :] += jnp.where(h0, dq_pair[:BLOCK], dq_pair[BLOCK:])
                    dk_acc[cur, rows, :] += dkb[BLOCK:]
                    dv_acc[cur, rows, :] += dvb[BLOCK:]

                    slot = jnp.where((n > 0) | (sb == 0), cur, prv)
                    before = _natural_rows(d, r, jnp.where(n > 0, n - 1, per_stream - 1))
                    dk_acc[slot, before, :] += dkb[:BLOCK]
                    dv_acc[slot, before, :] += dvb[:BLOCK]
                    return carry

                lax.fori_loop(0, UNITS, unit, 0, unroll=BWD_UNROLL)
            dq_ref[...] = dq_acc[...].astype(bf16)

        @pl.when(sb > 0)
        def _():
            dk_ref[...] = dk_acc[prv].astype(bf16)
            dv_ref[...] = dv_acc[prv].astype(bf16)

        @pl.when(step == n_total - 1)
        def _():
            scatter.finish()

    args = []
    for q, k, v, do in zip(qs, ks, vs, dos):
        args += [q, do, k, k, v, v]
    nat = pl.BlockSpec((SUPER, 128), lambda hp, sb: (jnp.minimum(sb, last), hp))
    nat_before = pl.BlockSpec((SUPER, 128), lambda hp, sb: (jnp.clip(sb - 1, 0, last), hp))
    out = jax.ShapeDtypeStruct((S, ATTN_WIDTH), bf16)
    half = chip_sum.shape[1]
    return pl.pallas_call(
        body, name="attn_bwd", grid=(ATTN_WIDTH // 128, n_steps + 1),
        in_specs=_attn_in_specs(S, True) + [nat, nat, ANY], out_specs=[nat, nat_before, nat_before, ANY],
        out_shape=[out, out, out, _Scatter.out_shape(half, chip_sum.dtype)],
        scratch_shapes=[pltpu.VMEM((SUPER, 128), f32), pltpu.VMEM((2, SUPER, 128), f32),
                        pltpu.VMEM((2, SUPER, 128), f32)] + _Scatter.scratch(half),
        compiler_params=_params("arbitrary", "arbitrary"),
    )(*args, lse, delta, chip_sum)


def _rms(v):
    return lax.rsqrt(jnp.mean(v * v, axis=-1, keepdims=True) + EPS)


def _out_proj(pool_out, attn_out, w_out, x, g2, g3):
    S = x.shape[0]
    ts = 512

    def body(p_ref, a_ref, w_ref, x_ref, g2_ref, g3_ref, mix_ref, x2_ref, h2_ref):
        mix = _dot(p_ref[...], w_ref[:POOL_WIDTH, :]) + _dot(a_ref[...], w_ref[POOL_WIDTH:, :])
        mix_ref[...] = mix
        x2 = x_ref[...] + (mix * _rms(mix)) * g2_ref[...]
        x2_ref[...] = x2
        h2_ref[...] = ((x2 * _rms(x2)) * g3_ref[...]).astype(bf16)

    row = lambda w: pl.BlockSpec((ts, w), lambda i: (i, 0))
    gain = pl.BlockSpec((1, D_MODEL), lambda i: (0, 0))
    return pl.pallas_call(
        body, name="out_proj", grid=(S // ts,),
        in_specs=[row(POOL_WIDTH), row(ATTN_WIDTH), pl.BlockSpec((D_MODEL, D_MODEL), lambda i: (0, 0)),
                  row(D_MODEL), gain, gain],
        out_specs=[row(D_MODEL)] * 3,
        out_shape=[jax.ShapeDtypeStruct((S, D_MODEL), f32), jax.ShapeDtypeStruct((S, D_MODEL), f32),
                   jax.ShapeDtypeStruct((S, D_MODEL), bf16)],
        compiler_params=_params("parallel"),
    )(pool_out, attn_out, w_out, x, g2, g3)


FF_TILE = 256
FF_ROWS = 256
FF_HALF = D_FF // 2


def _sigmoid(g):
    return 1.0 / (1.0 + jnp.exp(-g))


def _ffn_fwd(h2, w_gate, w_up, w_down):
    S = h2.shape[0]
    ts = 1024

    def body(h_ref, wg_ref, wu_ref, wd_ref, gate_ref, up_ref, f_ref):
        def rows_pass(first):
            def sub(i, carry):
                rows = pl.ds(pl.multiple_of(i * FF_ROWS, FF_ROWS), FF_ROWS)
                h = h_ref[rows, :]
                gate = _dot(h, wg_ref[...])
                up = _dot(h, wu_ref[...])
                gate_ref[rows, :] = gate.astype(bf16)
                up_ref[rows, :] = up.astype(bf16)
                part = _dot((gate * _sigmoid(gate) * up).astype(bf16), wd_ref[...])
                if first:
                    f_ref[rows, :] = part
                else:
                    f_ref[rows, :] += part
                return carry

            lax.fori_loop(0, ts // FF_ROWS, sub, 0, unroll=True)

        @pl.when(pl.program_id(1) == 0)
        def _():
            rows_pass(True)

        @pl.when(pl.program_id(1) > 0)
        def _():
            rows_pass(False)

    act = pl.BlockSpec((ts, FF_TILE), lambda i, j: (i, j))
    return pl.pallas_call(
        body, name="ffn_fwd", grid=(S // ts, D_FF // FF_TILE),
        in_specs=[pl.BlockSpec((ts, D_MODEL), lambda i, j: (i, 0)),
                  pl.BlockSpec((D_MODEL, FF_TILE), lambda i, j: (0, j)),
                  pl.BlockSpec((D_MODEL, FF_TILE), lambda i, j: (0, j)),
                  pl.BlockSpec((FF_TILE, D_MODEL), lambda i, j: (j, 0))],
        out_specs=[act, act, pl.BlockSpec((ts, D_MODEL), lambda i, j: (i, 0))],
        out_shape=[jax.ShapeDtypeStruct((S, D_FF), bf16), jax.ShapeDtypeStruct((S, D_FF), bf16),
                   jax.ShapeDtypeStruct((S, D_MODEL), f32)],
        compiler_params=_params("parallel", "arbitrary"),
    )(h2, w_gate, w_up, w_down)


def _loss_head(f, x2, target, g4):
    S = f.shape[0]
    ts = 512

    def body(f_ref, x2_ref, t_ref, g_ref, dy_ref, df_ref, dg_ref, loss_ref):
        @pl.when(pl.program_id(0) == 0)
        def _():
            dg_ref[...] = jnp.zeros_like(dg_ref)
            loss_ref[...] = jnp.zeros_like(loss_ref)

        fv = f_ref[...]
        g = g_ref[...]
        r = _rms(fv)
        fhat = fv * r
        err = (x2_ref[...] + fhat * g) - t_ref[...]
        loss_ref[...] += 0.5 * jnp.sum(jnp.mean(err * err, axis=-1, keepdims=True), axis=0, keepdims=True)
        dy = err * (1.0 / D_MODEL)
        dy_ref[...] = dy
        dg_ref[...] += jnp.sum(dy * fhat, axis=0, keepdims=True)
        dyg = dy * g
        df_ref[...] = (r * (dyg - fhat * jnp.mean(dyg * fhat, axis=-1, keepdims=True))).astype(bf16)

    row = pl.BlockSpec((ts, D_MODEL), lambda i: (i, 0))
    gain = pl.BlockSpec((1, D_MODEL), lambda i: (0, 0))
    return pl.pallas_call(
        body, name="loss_head", grid=(S // ts,), in_specs=[row, row, row, gain],
        out_specs=[row, row, gain, pl.BlockSpec((1, 1), lambda i: (0, 0))],
        out_shape=[jax.ShapeDtypeStruct((S, D_MODEL), f32), jax.ShapeDtypeStruct((S, D_MODEL), bf16),
                   jax.ShapeDtypeStruct((1, D_MODEL), f32), jax.ShapeDtypeStruct((1, 1), f32)],
        compiler_params=_params("arbitrary"),
    )(f, x2, target, g4)


def _ffn_bwd(df, gate, up, w_gate, w_up, w_down):
    S = df.shape[0]
    ts = 1024

    def body(df_ref, gate_ref, up_ref, wg_ref, wu_ref, wd_ref, a_ref, dgate_ref, dup_ref, dh_ref):
        def rows_pass(first):
            def sub(i, carry):
                rows = pl.ds(pl.multiple_of(i * FF_ROWS, FF_ROWS), FF_ROWS)
                da = _dot_nt(df_ref[rows, :], wd_ref[...])
                g = gate_ref[rows, :].astype(f32)
                u = up_ref[rows, :].astype(f32)
                sig = _sigmoid(g)
                silu = g * sig
                a_ref[rows, :] = (silu * u).astype(bf16)
                dup = (da * silu).astype(bf16)
                dgate = (da * u * (sig * (1.0 + g * (1.0 - sig)))).astype(bf16)
                dup_ref[rows, :] = dup
                dgate_ref[rows, :] = dgate
                part = _dot_nt(dgate, wg_ref[...]) + _dot_nt(dup, wu_ref[...])
                if first:
                    dh_ref[rows, :] = part
                else:
                    dh_ref[rows, :] += part
                return carry

            lax.fori_loop(0, ts // FF_ROWS, sub, 0, unroll=True)

        @pl.when(pl.program_id(1) == 0)
        def _():
            rows_pass(True)

        @pl.when(pl.program_id(1) > 0)
        def _():
            rows_pass(False)

    act = pl.BlockSpec((ts, FF_TILE), lambda i, j: (i, j))
    row = pl.BlockSpec((ts, D_MODEL), lambda i, j: (i, 0))
    return pl.pallas_call(
        body, name="ffn_bwd", grid=(S // ts, D_FF // FF_TILE),
        in_specs=[row, act, act,
                  pl.BlockSpec((D_MODEL, FF_TILE), lambda i, j: (0, j)),
                  pl.BlockSpec((D_MODEL, FF_TILE), lambda i, j: (0, j)),
                  pl.BlockSpec((FF_TILE, D_MODEL), lambda i, j: (j, 0))],
        out_specs=[act, act, act, row],
        out_shape=[jax.ShapeDtypeStruct((S, D_FF), bf16)] * 3 + [jax.ShapeDtypeStruct((S, D_MODEL), f32)],
        compiler_params=_params("parallel", "arbitrary"),
    )(df, gate, up, w_gate, w_up, w_down)


def _norm_bwd(dh2, dy, x2, mix, g3, g2):
    S = dh2.shape[0]
    ts = 512

    def body(dh_ref, dy_ref, x2_ref, mix_ref, g3_ref, g2_ref, dx2_ref, dmix_ref, dg3_ref, dg2_ref):
        @pl.when(pl.program_id(0) == 0)
        def _():
            dg3_ref[...] = jnp.zeros_like(dg3_ref)
            dg2_ref[...] = jnp.zeros_like(dg2_ref)

        dh = dh_ref[...]
        x2 = x2_ref[...]
        r3 = _rms(x2)
        xhat = x2 * r3
        dg3_ref[...] += jnp.sum(dh * xhat, axis=0, keepdims=True)
        dhg = dh * g3_ref[...]
        dx2 = dy_ref[...] + r3 * (dhg - xhat * jnp.mean(dhg * xhat, axis=-1, keepdims=True))
        dx2_ref[...] = dx2
        mix = mix_ref[...]
        r2 = _rms(mix)
        mhat = mix * r2
        dg2_ref[...] += jnp.sum(dx2 * mhat, axis=0, keepdims=True)
        dmg = dx2 * g2_ref[...]
        dmix_ref[...] = (r2 * (dmg - mhat * jnp.mean(dmg * mhat, axis=-1, keepdims=True))).astype(bf16)

    row = pl.BlockSpec((ts, D_MODEL), lambda i: (i, 0))
    gain = pl.BlockSpec((1, D_MODEL), lambda i: (0, 0))
    return pl.pallas_call(
        body, name="norm_bwd", grid=(S // ts,), in_specs=[row, row, row, row, gain, gain],
        out_specs=[row, row, gain, gain],
        out_shape=[jax.ShapeDtypeStruct((S, D_MODEL), f32), jax.ShapeDtypeStruct((S, D_MODEL), bf16),
                   jax.ShapeDtypeStruct((1, D_MODEL), f32), jax.ShapeDtypeStruct((1, D_MODEL), f32)],
        compiler_params=_params("arbitrary"),
    )(dh2, dy, x2, mix, g3, g2)


def _out_proj_bwd(dmix, w_out, attn_out, head_ones):
    S = dmix.shape[0]
    ts = 512

    def body(dm_ref, w_ref, o_ref, ones_ref, dp_ref, dl_ref, *rest):
        do_refs, stage = rest[:-N_STAGE], rest[-N_STAGE:]
        dcat = _dot_nt(dm_ref[...], w_ref[...])
        dp_ref[...] = dcat[:, :POOL_WIDTH]
        do = dcat[:, POOL_WIDTH:]
        for j in range(ATTN_WIDTH // 128):
            stage[j][...] = do[:, j * 128:(j + 1) * 128]
        _store_streams(stage, do_refs, ts)
        prod = do * o_ref[...].astype(f32)
        hi = prod.astype(bf16)
        lo = (prod - hi.astype(f32)).astype(bf16)
        dl_ref[...] = _dot(hi, ones_ref[...]) + _dot(lo, ones_ref[...])

    row = lambda w: pl.BlockSpec((ts, w), lambda i: (i, 0))
    res = pl.pallas_call(
        body, name="out_proj_bwd", grid=(S // ts,),
        in_specs=[row(D_MODEL), pl.BlockSpec((D_MODEL, D_MODEL), lambda i: (0, 0)), row(ATTN_WIDTH),
                  pl.BlockSpec((ATTN_WIDTH, ATTN_WIDTH), lambda i: (0, 0))],
        out_specs=[row(POOL_WIDTH), row(ATTN_WIDTH)] + [_stream_spec(d, ts) for d in DILATIONS],
        out_shape=[jax.ShapeDtypeStruct((S, POOL_WIDTH), f32), jax.ShapeDtypeStruct((S, ATTN_WIDTH), f32)]
        + [_stream_shape(S, d) for d in DILATIONS],
        scratch_shapes=_stage_scratch(ts),
        compiler_params=_params("parallel"),
    )(dmix, w_out, attn_out, head_ones)
    return res[0], res[1], res[2:]


def _in_proj_bwd(du, dq, dk, dv, cos_t, sin_t, w_in, x, dx2, g1):
    S = x.shape[0]
    ts = 256

    def body(du_ref, dq_ref, dk_ref, dv_ref, cos_ref, sin_ref, w_ref, x_ref, dx2_ref, g_ref, gx_ref, dproj_ref, dg_ref):
        @pl.when(pl.program_id(0) == 0)
        def _():
            dg_ref[...] = jnp.zeros_like(dg_ref)

        dproj_ref[:, :POOL_WIDTH] = du_ref[...]
        cos = cos_ref[...]
        sin = sin_ref[...]
        first = _first_half_mask(ts)
        for j in range(ATTN_WIDTH // 128):
            cols = slice(j * 128, (j + 1) * 128)
            for base, ref in ((POOL_WIDTH, dq_ref), (POOL_WIDTH + ATTN_WIDTH, dk_ref)):
                g = ref[:, cols].astype(f32)
                pre = g * cos + _rope_partner(g * sin, first)
                dproj_ref[:, base + j * 128: base + (j + 1) * 128] = pre.astype(bf16)
        dproj_ref[:, POOL_WIDTH + 2 * ATTN_WIDTH:] = dv_ref[...]

        dh = _dot_nt(dproj_ref[...], w_ref[...])
        xv = x_ref[...]
        r = _rms(xv)
        xhat = xv * r
        dg_ref[...] += jnp.sum(dh * xhat, axis=0, keepdims=True)
        dhg = dh * g_ref[...]
        gx_ref[...] = dx2_ref[...] + r * (dhg - xhat * jnp.mean(dhg * xhat, axis=-1, keepdims=True))

    row = lambda w: pl.BlockSpec((ts, w), lambda i: (i, 0))
    gain = pl.BlockSpec((1, D_MODEL), lambda i: (0, 0))
    return pl.pallas_call(
        body, name="in_proj_bwd", grid=(S // ts,),
        in_specs=[row(POOL_WIDTH)] + [row(ATTN_WIDTH)] * 3 + [row(128), row(128),
                  pl.BlockSpec((D_MODEL, IN_WIDTH), lambda i: (0, 0)), row(D_MODEL), row(D_MODEL), gain],
        out_specs=[row(D_MODEL), row(IN_WIDTH), gain],
        out_shape=[jax.ShapeDtypeStruct((S, D_MODEL), f32), jax.ShapeDtypeStruct((S, IN_WIDTH), bf16),
                   jax.ShapeDtypeStruct((1, D_MODEL), f32)],
        compiler_params=_params("arbitrary"),
    )(du, dq, dk, dv, cos_t, sin_t, w_in, x, dx2, g1)


def _matmul_tn(a, b, tn, name):
    K, M = a.shape
    N = b.shape[1]
    tk = 1024

    def body(a_ref, b_ref, o_ref):
        part = _dot_tn(a_ref[...], b_ref[...])

        @pl.when(pl.program_id(1) == 0)
        def _():
            o_ref[...] = part

        @pl.when(pl.program_id(1) > 0)
        def _():
            o_ref[...] += part

    return pl.pallas_call(
        body, name=name, grid=(N // tn, K // tk),
        in_specs=[pl.BlockSpec((tk, M), lambda n, k: (k, 0)), pl.BlockSpec((tk, tn), lambda n, k: (k, n))],
        out_specs=pl.BlockSpec((M, tn), lambda n, k: (0, n)),
        out_shape=jax.ShapeDtypeStruct((M, N), f32),
        compiler_params=_params("parallel", "arbitrary"),
    )(a, b)


def _rope_tables(S):
    half = HEAD_DIM // 2
    freqs = ROPE_THETA ** (-jnp.arange(half, dtype=f32) * (2.0 / HEAD_DIM))
    ang = jnp.arange(S).astype(f32)[:, None] * freqs[None, :]
    cos = jnp.tile(jnp.cos(ang), (1, 4))
    sin = jnp.sin(ang)
    sin = jnp.tile(jnp.concatenate([-sin, sin], axis=1), (1, 2))
    return cos, sin


def _block_diag(w_pool):
    w = jnp.zeros((POOL_WIDTH, POOL_WIDTH), w_pool.dtype)
    for g in range(POOL_WIDTH // POOL_GROUP):
        w = lax.dynamic_update_slice(w, w_pool[g], (g * POOL_GROUP, g * POOL_GROUP))
    return w


def _head_ones():
    head = np.arange(ATTN_WIDTH) // HEAD_DIM
    return jnp.asarray(head[:, None] == head[None, :], dtype=bf16)


def _place():
    x, y, c = lax.axis_index("x"), lax.axis_index("y"), lax.axis_index("c")
    chips = [(1 - x, y), (x, 1 - y), (1 - x, 1 - y)]
    return x, y, c, chips


ANY = pl.BlockSpec(memory_space=pl.ANY)
N_PEER_CHIPS = N_CHIPS - 1
ICI_PIECES = 4
D2D_PIECES = 8
LOCAL_PIECES = 8


def _row_chunks(rows, n, unit=32):
    units = rows // unit
    out, start = [], 0
    for i in range(n):
        size = (units // n + (1 if i < units % n else 0)) * unit
        out.append((start, size))
        start += size
    return [piece for piece in out if piece[1]]


class _LocalCopy:
    def __init__(self, src_rows, dst_rows, rows, buf, sems_in, sems_out):
        self.loads, self.stores = [], []
        for i, (start, size) in enumerate(_row_chunks(rows, LOCAL_PIECES)):
            r = pl.ds(start, size)
            self.loads.append(pltpu.make_async_copy(src_rows(r), buf.at[r], sems_in.at[i]))
            self.stores.append(pltpu.make_async_copy(buf.at[r], dst_rows(r), sems_out.at[i]))

    def start(self):
        for cp in self.loads:
            cp.start()

    def pass_on(self):
        for load, store in zip(self.loads, self.stores):
            load.wait()
            store.start()

    def finish(self):
        for store in self.stores:
            store.wait()

    @staticmethod
    def scratch(rows, dtype):
        return [pltpu.VMEM((rows, D_MODEL), dtype), pltpu.SemaphoreType.DMA((LOCAL_PIECES,)),
                pltpu.SemaphoreType.DMA((LOCAL_PIECES,))]


class _Gather:
    def __init__(self, w_ref, out_ref, send1, recv1, send2, recv2, buf, sems_in, sems_out):
        x, y, c, chips = _place()
        me = 2 * x + y
        rows = w_ref.shape[0]
        half = rows // 2
        pieces = _row_chunks(half, ICI_PIECES)
        self.own = _LocalCopy(lambda r: w_ref.at[r], lambda r: out_ref.at[me, r], rows, buf, sems_in, sems_out)

        def rows_of(core, piece):
            start, size = piece
            return pl.ds(core * half + start, size)

        self.sends, self.arrivals, self.forwards, self.forward_arrivals = [], [], [], []
        for i, piece in enumerate(pieces):
            for j, (cx, cy) in enumerate(chips):
                k = j * len(pieces) + i
                there = 2 * cx + cy

                def direct(src_chip, cx=cx, cy=cy, k=k, piece=piece):
                    return pltpu.make_async_remote_copy(
                        src_ref=w_ref.at[rows_of(c, piece)], dst_ref=out_ref.at[src_chip, rows_of(c, piece)],
                        send_sem=send1.at[k], recv_sem=recv1.at[k], device_id=(cx, cy, c), device_id_type=MESH)

                def passed(core, there=there, k=k, piece=piece):
                    return pltpu.make_async_remote_copy(
                        src_ref=out_ref.at[there, rows_of(core, piece)], dst_ref=out_ref.at[there, rows_of(core, piece)],
                        send_sem=send2.at[k], recv_sem=recv2.at[k], device_id=(x, y, 1 - c), device_id_type=MESH)

                self.sends.append(direct(me))
                self.arrivals.append(direct(there))
                self.forwards.append(passed(c))
                self.forward_arrivals.append(passed(1 - c))

    def start(self):
        for cp in self.sends:
            cp.start()
        self.own.start()

    def pass_on(self):
        self.own.pass_on()
        for arrival, forward in zip(self.arrivals, self.forwards):
            arrival.wait_recv()
            forward.start()

    def finish(self):
        for arrival in self.forward_arrivals:
            arrival.wait_recv()
        for cp in self.sends + self.forwards:
            cp.wait_send()
        self.own.finish()

    @staticmethod
    def scratch(rows, dtype):
        n = N_PEER_CHIPS * len(_row_chunks(rows // 2, ICI_PIECES))
        return [pltpu.SemaphoreType.DMA((n,))] * 4 + _LocalCopy.scratch(rows, dtype)

    @staticmethod
    def out_shape(rows, dtype):
        return jax.ShapeDtypeStruct((N_CHIPS, rows, D_MODEL), dtype)


def _gather_weights(pack):
    rows = pack.shape[0]

    def body(w_ref, out_ref, *scratch):
        gather = _Gather(w_ref, out_ref, *scratch)
        gather.start()
        gather.pass_on()
        gather.finish()

    return pl.pallas_call(
        body, name="gather_weights", in_specs=[ANY], out_specs=ANY, out_shape=_Gather.out_shape(rows, pack.dtype),
        scratch_shapes=_Gather.scratch(rows, pack.dtype),
        compiler_params=pltpu.CompilerParams(vmem_limit_bytes=VMEM_LIMIT_V7X),
    )(pack)


class _Scatter:
    def __init__(self, h_ref, out_ref, send, recv):
        x, y, c, chips = _place()
        pieces = _row_chunks(h_ref.shape[1], ICI_PIECES)
        self.copies = []
        for i, (start, size) in enumerate(pieces):
            for j, (cx, cy) in enumerate(chips):
                k = j * len(pieces) + i
                self.copies.append(pltpu.make_async_remote_copy(
                    src_ref=h_ref.at[2 * cx + cy, pl.ds(start, size)], dst_ref=out_ref.at[j, pl.ds(start, size)],
                    send_sem=send.at[k], recv_sem=recv.at[k], device_id=(cx, cy, c), device_id_type=MESH))

    def start(self):
        for cp in self.copies:
            cp.start()

    def finish(self):
        for cp in self.copies:
            cp.wait_recv()
        for cp in self.copies:
            cp.wait_send()

    @staticmethod
    def scratch(half):
        n = N_PEER_CHIPS * len(_row_chunks(half, ICI_PIECES))
        return [pltpu.SemaphoreType.DMA((n,))] * 2

    @staticmethod
    def out_shape(half, dtype):
        return jax.ShapeDtypeStruct((N_PEER_CHIPS, half, D_MODEL), dtype)


def _scatter_to_chips(h):
    half = h.shape[1]

    def body(h_ref, out_ref, send, recv):
        scatter = _Scatter(h_ref, out_ref, send, recv)
        scatter.start()
        scatter.finish()

    return pl.pallas_call(
        body, name="scatter_to_chips", in_specs=[ANY], out_specs=ANY, out_shape=_Scatter.out_shape(half, h.dtype),
        scratch_shapes=_Scatter.scratch(half),
    )(h)


def _swap_halves(g, name):
    half = g.shape[1] // 2
    pieces = _row_chunks(half, D2D_PIECES)
    n = len(pieces)

    def body(g_ref, theirs_ref, send, recv):
        x, y, c, _ = _place()
        copies = []
        for s in range(N_CHIPS):
            for i, (start, size) in enumerate(pieces):
                copies.append(pltpu.make_async_remote_copy(
                    src_ref=g_ref.at[s, pl.ds((1 - c) * half + start, size)], dst_ref=theirs_ref.at[s, pl.ds(start, size)],
                    send_sem=send.at[s * n + i], recv_sem=recv.at[s * n + i],
                    device_id=(x, y, 1 - c), device_id_type=MESH))
        for cp in copies:
            cp.start()
        for cp in copies:
            cp.wait()

    return pl.pallas_call(
        body, name=name, in_specs=[ANY], out_specs=ANY,
        out_shape=jax.ShapeDtypeStruct((N_CHIPS, half, D_MODEL), g.dtype),
        scratch_shapes=[pltpu.SemaphoreType.DMA((N_CHIPS * n,))] * 2,
    )(g)


ADD_TILE_MAX_ROWS = 600


def _add_tile(half):
    return max(t for t in range(8, ADD_TILE_MAX_ROWS + 1, 8) if half % t == 0)


def _add_cores(g, theirs, name):
    half = theirs.shape[1]
    tr = _add_tile(half)
    n_t = half // tr

    def body(c_ref, g_ref, t_ref, o_ref):
        o_ref[...] = g_ref[...] + t_ref[...]

    blk = pl.BlockSpec((1, tr, D_MODEL), lambda s, t, c_ref: (s, t, 0))
    return pl.pallas_call(
        body, name=name,
        grid_spec=pltpu.PrefetchScalarGridSpec(
            num_scalar_prefetch=1, grid=(N_CHIPS, n_t),
            in_specs=[pl.BlockSpec((1, tr, D_MODEL), lambda s, t, c_ref: (s, c_ref[0] * n_t + t, 0)), blk],
            out_specs=blk),
        out_shape=jax.ShapeDtypeStruct(theirs.shape, theirs.dtype),
        compiler_params=_params("parallel", "parallel"),
    )(lax.axis_index("c").astype(jnp.int32).reshape(1), g, theirs)


def _add_chips(chip_sum, others, name):
    half = chip_sum.shape[1]
    tr = _add_tile(half)

    def body(me_ref, own_ref, o0, o1, o2, out_ref):
        out_ref[...] = ((own_ref[0] + o0[0]) + o1[0]) + o2[0]

    other = lambda j: pl.BlockSpec((1, tr, D_MODEL), lambda t, me_ref: (j, t, 0))
    return pl.pallas_call(
        body, name=name,
        grid_spec=pltpu.PrefetchScalarGridSpec(
            num_scalar_prefetch=1, grid=(half // tr,),
            in_specs=[pl.BlockSpec((1, tr, D_MODEL), lambda t, me_ref: (me_ref[0], t, 0)), other(0), other(1), other(2)],
            out_specs=pl.BlockSpec((tr, D_MODEL), lambda t, me_ref: (t, 0))),
        out_shape=jax.ShapeDtypeStruct((half, D_MODEL), chip_sum.dtype),
        compiler_params=_params("parallel"),
    )((2 * lax.axis_index("x") + lax.axis_index("y")).astype(jnp.int32).reshape(1), chip_sum, others, others, others)


def _join_halves(r):
    half = r.shape[0]
    pieces = _row_chunks(half, 2 * D2D_PIECES)
    n = len(pieces)

    def body(r_ref, out_ref, send, recv, buf, sems_in, sems_out):
        x, y, c, _ = _place()
        own = _LocalCopy(lambda rr: r_ref.at[rr], lambda rr: out_ref.at[c, rr], half, buf, sems_in, sems_out)
        own.start()

        def piece(i, core):
            start, size = pieces[i]
            return pltpu.make_async_remote_copy(
                src_ref=r_ref.at[pl.ds(start, size)], dst_ref=out_ref.at[core, pl.ds(start, size)],
                send_sem=send.at[i], recv_sem=recv.at[i], device_id=(x, y, 1 - c), device_id_type=MESH)

        copies = [piece(i, c) for i in range(n)]
        for cp in copies:
            cp.start()
        own.pass_on()
        for i in range(n):
            piece(i, 1 - c).wait_recv()
        for cp in copies:
            cp.wait_send()
        own.finish()

    return pl.pallas_call(
        body, name="join_halves", in_specs=[ANY], out_specs=ANY,
        out_shape=jax.ShapeDtypeStruct((2,) + r.shape, r.dtype),
        scratch_shapes=[pltpu.SemaphoreType.DMA((n,))] * 2 + _LocalCopy.scratch(half, r.dtype),
        compiler_params=pltpu.CompilerParams(vmem_limit_bytes=VMEM_LIMIT_V7X),
    )(r)


def _sum_small(block):
    def body(b_ref, out_ref, gathered, send, recv):
        x, y, c, _ = _place()
        me = 4 * x + 2 * y + c
        gathered[me] = b_ref[...]
        sends = []
        for kk in range(1, N_DEV):
            flip = lambda v, bit: 1 - v if bit else v
            peer = (flip(x, kk & 4), flip(y, kk & 2), flip(c, kk & 1))
            cp = pltpu.make_async_remote_copy(
                src_ref=b_ref, dst_ref=gathered.at[me], send_sem=send.at[kk - 1], recv_sem=recv.at[kk - 1],
                device_id=peer, device_id_type=MESH)
            cp.start()
            sends.append(cp)
        for kk in range(1, N_DEV):
            peer_index = jnp.bitwise_xor(me, kk)
            pltpu.make_async_remote_copy(
                src_ref=b_ref, dst_ref=gathered.at[peer_index], send_sem=send.at[kk - 1], recv_sem=recv.at[kk - 1],
                device_id=(x, y, c), device_id_type=MESH).wait_recv()
        for cp in sends:
            cp.wait_send()
        acc = gathered[0]
        for dev in range(1, N_DEV):
            acc = acc + gathered[dev]
        out_ref[...] = acc

    vmem = pl.BlockSpec(memory_space=pltpu.VMEM)
    return pl.pallas_call(
        body, name="sum_small", in_specs=[vmem], out_specs=vmem,
        out_shape=jax.ShapeDtypeStruct(block.shape, block.dtype),
        scratch_shapes=[pltpu.VMEM((N_DEV,) + block.shape, block.dtype),
                        pltpu.SemaphoreType.DMA((N_DEV - 1,)), pltpu.SemaphoreType.DMA((N_DEV - 1,))],
    )(block)


def _adamw(w, g, m, v, name):
    rows, cols = w.shape
    tr = rows
    for cand in (512, 256, 128, 64, 32, 16, 8):
        if rows % cand == 0:
            tr = cand
            break
    c1 = 1.0 - ADAM_B1 ** ADAM_STEP
    c2 = 1.0 - ADAM_B2 ** ADAM_STEP

    def body(w_ref, g_ref, m_ref, v_ref, d_ref, nm_ref, nv_ref):
        gv = g_ref[...]
        nm = ADAM_B1 * m_ref[...] + (1.0 - ADAM_B1) * gv
        nv = ADAM_B2 * v_ref[...] + (1.0 - ADAM_B2) * (gv * gv)
        nm_ref[...] = nm
        nv_ref[...] = nv
        d_ref[...] = -ADAM_LR * ((nm / c1) / (jnp.sqrt(nv / c2) + ADAM_EPS) + ADAM_WD * w_ref[...])

    blk = pl.BlockSpec((tr, cols), lambda i: (i, 0))
    shape = jax.ShapeDtypeStruct((rows, cols), f32)
    return pl.pallas_call(
        body, name=name, grid=(rows // tr,), in_specs=[blk] * 4, out_specs=[blk] * 3, out_shape=[shape] * 3,
        compiler_params=_params("parallel"),
    )(w, g, m, v)


LARGE = ("w_in", "w_out", "w_gate", "w_up", "w_down")
SMALL = ("ln_pre_mix", "ln_post_mix", "ln_pre_ffn", "ln_post_ffn", "pool_scale", "w_pool")
SHARD_ROWS = {"w_in": 640, "w_out": 256, "w_gate": 704, "w_up": 704, "w_down": 704}
COLUMN_SHARDED = {"w_in": IN_WIDTH // N_CHIPS, "w_gate": D_FF // N_CHIPS, "w_up": D_FF // N_CHIPS}
NEEDED_FIRST = ("w_in",)
NEEDED_LATER = ("w_out", "w_gate", "w_up", "w_down")
READY_EARLY = ("w_out", "w_gate", "w_up", "w_down")
READY_LATE = ("w_in",)


def _pack_shard(shards, names):
    return jnp.concatenate([shards[n].reshape(-1, D_MODEL) for n in names], axis=0)


def _unpack_shard(pack, names, shapes):
    out, row = {}, 0
    for n in names:
        out[n] = pack[row:row + SHARD_ROWS[n]].reshape(shapes[n])
        row += SHARD_ROWS[n]
    return out


def _whole_from_shards(packs, names):
    out, row = {}, 0
    for n in names:
        rows = SHARD_ROWS[n]
        part = packs[:, row:row + rows]
        if n in COLUMN_SHARDED:
            width = COLUMN_SHARDED[n]
            part = part.reshape(N_CHIPS, D_MODEL, width).transpose(1, 0, 2).reshape(D_MODEL, N_CHIPS * width)
        else:
            part = part.reshape(N_CHIPS * rows, D_MODEL)
        out[n] = part
        row += rows
    return out


def _shards_from_whole(grads, names):
    parts = []
    for n in names:
        g = grads[n]
        if n in COLUMN_SHARDED:
            width = COLUMN_SHARDED[n]
            g = g.reshape(D_MODEL, N_CHIPS, width).transpose(1, 0, 2)
        parts.append(g.reshape(N_CHIPS, SHARD_ROWS[n], D_MODEL))
    return jnp.concatenate(parts, axis=1)


def _pack_small(vals):
    rows = [vals[n].reshape(1, D_MODEL) for n in SMALL[:4]]
    rows.append(jnp.pad(vals["pool_scale"].reshape(1, POOL_WIDTH), ((0, 0), (0, D_MODEL - POOL_WIDTH))))
    rows.append(jnp.pad(vals["loss"].reshape(1, 1), ((0, 0), (0, D_MODEL - 1))))
    rows.append(jnp.zeros((2, D_MODEL), f32))
    rows.append(vals["w_pool"].reshape(16, D_MODEL))
    return jnp.concatenate(rows, axis=0)


def _unpack_small(block):
    out = {n: block[i:i + 1] for i, n in enumerate(SMALL[:4])}
    out["pool_scale"] = block[4:5, :POOL_WIDTH]
    out["loss"] = block[5, 0]
    out["w_pool"] = block[8:24].reshape(1, 4, POOL_GROUP, POOL_GROUP)
    return out


def kernel(x, ln_pre_mix, w_in, w_pool, pool_scale, w_out, ln_post_mix, ln_pre_ffn, w_gate, w_up, w_down, ln_post_ffn, loss_target, m_ln_pre_mix, m_w_in, m_w_pool, m_pool_scale, m_w_out, m_ln_post_mix, m_ln_pre_ffn, m_w_gate, m_w_up, m_w_down, m_ln_post_ffn, v_ln_pre_mix, v_w_in, v_w_pool, v_pool_scale, v_w_out, v_ln_post_mix, v_ln_pre_ffn, v_w_gate, v_w_up, v_w_down, v_ln_post_ffn):
    w = dict(ln_pre_mix=ln_pre_mix, w_in=w_in, w_pool=w_pool, pool_scale=pool_scale, w_out=w_out,
             ln_post_mix=ln_post_mix, ln_pre_ffn=ln_pre_ffn, w_gate=w_gate, w_up=w_up, w_down=w_down,
             ln_post_ffn=ln_post_ffn)
    m = dict(ln_pre_mix=m_ln_pre_mix, w_in=m_w_in, w_pool=m_w_pool, pool_scale=m_pool_scale, w_out=m_w_out,
             ln_post_mix=m_ln_post_mix, ln_pre_ffn=m_ln_pre_ffn, w_gate=m_w_gate, w_up=m_w_up, w_down=m_w_down,
             ln_post_ffn=m_ln_post_ffn)
    v = dict(ln_pre_mix=v_ln_pre_mix, w_in=v_w_in, w_pool=v_w_pool, pool_scale=v_pool_scale, w_out=v_w_out,
             ln_post_mix=v_ln_post_mix, ln_pre_ffn=v_ln_pre_ffn, w_gate=v_w_gate, w_up=v_w_up, w_down=v_w_down,
             ln_post_ffn=v_ln_post_ffn)

    xs, target = x[0], loss_target[0]
    cos_t, sin_t = _rope_tables(xs.shape[0])
    w_bd = _block_diag(w_pool[0]).astype(bf16)
    shard = {n: w[n][0].astype(bf16) for n in LARGE}

    w_in_whole = _whole_from_shards(_gather_weights(_pack_shard(shard, NEEDED_FIRST)), NEEDED_FIRST)["w_in"]
    h1, u, qs, ks, vs = _in_proj(xs, ln_pre_mix, w_in_whole, cos_t, sin_t)
    pool_out = _pool_fwd(u, w_bd, pool_scale)
    attn_out, lse, later = _attn_fwd(qs, ks, vs, _pack_shard(shard, NEEDED_LATER))
    whole = _whole_from_shards(later, NEEDED_LATER)
    mix, x2, h2 = _out_proj(pool_out, attn_out, whole["w_out"], xs, ln_post_mix, ln_pre_ffn)
    gate, up, f = _ffn_fwd(h2, whole["w_gate"], whole["w_up"], whole["w_down"])
    dy, df, dg4, loss = _loss_head(f, x2, target, ln_post_ffn)

    large = {}
    a, dgate, dup, dh2 = _ffn_bwd(df, gate, up, whole["w_gate"], whole["w_up"], whole["w_down"])
    large["w_down"] = _matmul_tn(a, df, D_MODEL, "grad_w_down")
    large["w_gate"] = _matmul_tn(h2, dgate, FF_HALF, "grad_w_gate")
    large["w_up"] = _matmul_tn(h2, dup, FF_HALF, "grad_w_up")
    dx2, dmix, dg3, dg2 = _norm_bwd(dh2, dy, x2, mix, ln_pre_ffn, ln_post_mix)
    large["w_out"] = jnp.concatenate([_matmul_tn(pool_out, dmix, D_MODEL, "grad_w_out_pool"),
                                      _matmul_tn(attn_out, dmix, D_MODEL, "grad_w_out_attn")], axis=0)
    early = _shards_from_whole(large, READY_EARLY)
    early_chip = _add_cores(early, _swap_halves(early, "swap_halves_early"), "add_cores_early")
    dpool, delta, dos = _out_proj_bwd(dmix, whole["w_out"], attn_out, _head_ones())
    du, d_w_bd, d_scale = _pool_bwd(u, dpool, w_bd, pool_scale)
    dq, dk, dv, early_others = _attn_bwd(qs, ks, vs, dos, lse, delta, early_chip)
    grad_x, dproj, dg1 = _in_proj_bwd(du, dq, dk, dv, cos_t, sin_t, w_in_whole, xs, dx2, ln_pre_mix)
    large["w_in"] = _matmul_tn(h1, dproj, IN_WIDTH // 2, "grad_w_in")
    late = _shards_from_whole(large, READY_LATE)
    late_chip = _add_cores(late, _swap_halves(late, "swap_halves_late"), "add_cores_late")
    late_others = _scatter_to_chips(late_chip)
    early_half = _add_chips(early_chip, early_others, "add_chips_early")
    late_half = _add_chips(late_chip, late_others, "add_chips_late")
    joined = _join_halves(jnp.concatenate([early_half, late_half], axis=0))
    n_early = early_half.shape[0]
    shapes = {n: w[n].shape[1:] for n in LARGE}
    grads = _unpack_shard(joined[:, :n_early].reshape(-1, D_MODEL), READY_EARLY, shapes)
    grads.update(_unpack_shard(joined[:, n_early:].reshape(-1, D_MODEL), READY_LATE, shapes))

    d_w_pool = jnp.stack([d_w_bd[g * POOL_GROUP:(g + 1) * POOL_GROUP, g * POOL_GROUP:(g + 1) * POOL_GROUP]
                          for g in range(POOL_WIDTH // POOL_GROUP)])
    small = dict(ln_pre_mix=dg1, ln_post_mix=dg2, ln_pre_ffn=dg3, ln_post_ffn=dg4, pool_scale=d_scale, w_pool=d_w_pool)
    total = _unpack_small(_sum_small(_pack_small(dict(small, loss=loss))))
    for n in SMALL:
        grads[n] = total[n]

    delta_w, new_m, new_v = {}, {}, {}
    for n in LARGE:
        delta_w[n], new_m[n], new_v[n] = _adamw(w[n][0], grads[n], m[n][0], v[n][0], "adamw_" + n)
    small_state = [_pack_small(dict({n: s[n] for n in SMALL}, loss=jnp.zeros((), f32))) for s in (w, m, v)]
    small_grad = _pack_small(dict({n: grads[n] for n in SMALL}, loss=jnp.zeros((), f32)))
    sd, sm, sv = _adamw(small_state[0], small_grad, small_state[1], small_state[2], "adamw_small")
    for out, block in ((delta_w, sd), (new_m, sm), (new_v, sv)):
        un = _unpack_small(block)
        for n in SMALL:
            out[n] = un[n]

    names = ("ln_pre_mix", "w_in", "w_pool", "pool_scale", "w_out", "ln_post_mix", "ln_pre_ffn", "w_gate", "w_up",
             "w_down", "ln_post_ffn")
    full = lambda d: [d[n].reshape(w[n].shape) for n in names]
    return (total["loss"], grad_x[None], *full(grads), *full(delta_w), *full(new_m), *full(new_v))
```

```python
import numpy as np
import jax
import jax.numpy as jnp
from jax import lax
from jax.experimental import pallas as pl
from jax.experimental.pallas import tpu as pltpu

D_MODEL = 1024
POOL_WIDTH = 256
POOL_GROUP = 64
ATTN_WIDTH = 768
HEAD_DIM = 64
IN_WIDTH = 2560
D_FF = 2816
BLOCK = 128
DILATIONS = (1, 4, 16)
ROPE_THETA = 10000.0
EPS = 1e-6
ATTN_SCALE = 0.125
NEG = -1e30

ADAM_LR = 0.001
ADAM_B1 = 0.9
ADAM_B2 = 0.999
ADAM_EPS = 1e-08
ADAM_WD = 0.01
ADAM_STEP = 10

N_CHIPS = 4
N_DEV = 8
VMEM_LIMIT_V7X = 56 * 1024 * 1024
MESH = pl.DeviceIdType.MESH

f32 = jnp.float32
bf16 = jnp.bfloat16


def _params(*sem):
    return pltpu.CompilerParams(dimension_semantics=sem, vmem_limit_bytes=VMEM_LIMIT_V7X)


def _dot(a, b):
    return jnp.dot(a, b, preferred_element_type=f32)


def _dot_nt(a, b):
    return lax.dot_general(a, b, (((1,), (1,)), ((), ())), preferred_element_type=f32)


def _dot_tn(a, b):
    return lax.dot_general(a, b, (((0,), (0,)), ((), ())), preferred_element_type=f32)


def _rope_partner(a, first_half):
    return jnp.where(first_half, pltpu.roll(a, 96, 1), pltpu.roll(a, 32, 1))


def _first_half_mask(rows):
    lane = lax.broadcasted_iota(jnp.int32, (rows, 128), 1)
    return (lane % HEAD_DIM) < (HEAD_DIM // 2)


def _stream_spec(d, ts):
    return pl.BlockSpec((d, ts // d, ATTN_WIDTH), lambda i: (0, i, 0))


def _stream_shape(S, d):
    return jax.ShapeDtypeStruct((d, S // d, ATTN_WIDTH), bf16)


N_STAGE = ATTN_WIDTH // 128


def _stage_scratch(ts):
    return [pltpu.VMEM((ts, 128), f32)] * N_STAGE


def _store_streams(stage, out_refs, ts):
    for d, ref in zip(DILATIONS, out_refs):
        for r in range(d):
            rows = pl.ds(0, ts) if d == 1 else pl.ds(r, ts // d, stride=d)
            for j in range(N_STAGE):
                ref[r, :, j * 128:(j + 1) * 128] = stage[j][rows, :].astype(bf16)


def _in_proj(x, g1, w_in, cos_t, sin_t):
    S = x.shape[0]
    ts = 512

    def body(x_ref, g_ref, w_ref, cos_ref, sin_ref, h_ref, u_ref, *rest):
        outs, stage = rest[:-N_STAGE], rest[-N_STAGE:]
        xv = x_ref[...]
        r = lax.rsqrt(jnp.mean(xv * xv, axis=-1, keepdims=True) + EPS)
        h = ((xv * r) * g_ref[...]).astype(bf16)
        h_ref[...] = h
        proj = _dot_nt(h, w_ref[...])
        u_ref[...] = proj[:, :POOL_WIDTH]
        cos = cos_ref[...]
        sin = sin_ref[...]
        first = _first_half_mask(ts)
        n_dil = len(DILATIONS)
        for which, base in enumerate((POOL_WIDTH, POOL_WIDTH + ATTN_WIDTH)):
            for j in range(ATTN_WIDTH // 128):
                a = proj[:, base + j * 128: base + (j + 1) * 128]
                stage[j][...] = a * cos + _rope_partner(a, first) * sin
            _store_streams(stage, outs[which * n_dil:(which + 1) * n_dil], ts)
        for j in range(ATTN_WIDTH // 128):
            base = POOL_WIDTH + 2 * ATTN_WIDTH + j * 128
            stage[j][...] = proj[:, base:base + 128]
        _store_streams(stage, outs[2 * n_dil:], ts)

    row = lambda w: pl.BlockSpec((ts, w), lambda i: (i, 0))
    streams = [_stream_spec(d, ts) for d in DILATIONS] * 3
    res = pl.pallas_call(
        body, name="in_proj", grid=(S // ts,),
        in_specs=[row(D_MODEL), pl.BlockSpec((1, D_MODEL), lambda i: (0, 0)),
                  pl.BlockSpec((IN_WIDTH, D_MODEL), lambda i: (0, 0)), row(128), row(128)],
        out_specs=[row(D_MODEL), row(POOL_WIDTH)] + streams,
        out_shape=[jax.ShapeDtypeStruct((S, D_MODEL), bf16), jax.ShapeDtypeStruct((S, POOL_WIDTH), f32)]
        + [_stream_shape(S, d) for d in DILATIONS] * 3,
        scratch_shapes=_stage_scratch(ts),
        compiler_params=_params("parallel"),
    )(x, g1, w_in, cos_t, sin_t)
    n = len(DILATIONS)
    return res[0], res[1], res[2:2 + n], res[2 + n:2 + 2 * n], res[2 + 2 * n:]


POOL_HALO = 16


def _pool_lane_group(rows):
    return lax.broadcasted_iota(jnp.int32, (rows, POOL_WIDTH), 1) // POOL_GROUP


def _pool_select(group, s2, s4, s8, s16):
    return jnp.where(group == 0, s2, jnp.where(group == 1, s4, jnp.where(group == 2, s8, s16)))


def _pool_count(t0, rows):
    group = _pool_lane_group(rows)
    t = t0 + lax.broadcasted_iota(jnp.int32, (rows, POOL_WIDTH), 0)
    win = _pool_select(group, 2, 4, 8, 16)
    return jnp.minimum(t + 1, win).astype(f32)


def _pool_diff(u_halo, u_tile, t0):
    ts = u_tile.shape[0]
    ext = jnp.concatenate([u_halo, u_tile], axis=0)
    s2 = ext + pltpu.roll(ext, 1, 0)
    s4 = s2 + pltpu.roll(s2, 2, 0)
    s8 = s4 + pltpu.roll(s4, 4, 0)
    s16 = s8 + pltpu.roll(s8, 8, 0)
    group = _pool_lane_group(ts + POOL_HALO)
    wsum = _pool_select(group, s2, s4, s8, s16)[POOL_HALO:]
    return wsum / _pool_count(t0, ts) - u_tile


def _pool_specs(ts, n_tiles):
    tile = pl.BlockSpec((ts, POOL_WIDTH), lambda i: (i, 0))
    per = ts // POOL_HALO
    before = pl.BlockSpec((POOL_HALO, POOL_WIDTH), lambda i: (jnp.maximum(i * per - 1, 0), 0))
    after = pl.BlockSpec((POOL_HALO, POOL_WIDTH), lambda i: (jnp.minimum((i + 1) * per, n_tiles * per - 1), 0))
    return tile, before, after


def _pool_fwd(u, w_bd, scale):
    S = u.shape[0]
    ts = 512
    n_tiles = S // ts

    def body(u_ref, halo_ref, w_ref, sc_ref, y_ref):
        i = pl.program_id(0)
        halo = jnp.where(i > 0, halo_ref[...], 0.0)
        d = _pool_diff(halo, u_ref[...], i * ts)
        y_ref[...] = (_dot(d.astype(bf16), w_ref[...]) * sc_ref[...]).astype(bf16)

    tile, before, _ = _pool_specs(ts, n_tiles)
    return pl.pallas_call(
        body, name="pool_fwd", grid=(n_tiles,),
        in_specs=[tile, before, pl.BlockSpec((POOL_WIDTH, POOL_WIDTH), lambda i: (0, 0)),
                  pl.BlockSpec((1, POOL_WIDTH), lambda i: (0, 0))],
        out_specs=tile, out_shape=jax.ShapeDtypeStruct((S, POOL_WIDTH), bf16),
        compiler_params=_params("parallel"),
    )(u, u, w_bd, scale)


def _pool_bwd(u, dy, w_bd, scale):
    S = u.shape[0]
    ts = 512
    n_tiles = S // ts

    def body(u_ref, halo_ref, dy_ref, dy_next_ref, w_ref, sc_ref, du_ref, dw_ref, dsc_ref):
        i = pl.program_id(0)

        @pl.when(i == 0)
        def _():
            dw_ref[...] = jnp.zeros_like(dw_ref)
            dsc_ref[...] = jnp.zeros_like(dsc_ref)

        halo = jnp.where(i > 0, halo_ref[...], 0.0)
        d = _pool_diff(halo, u_ref[...], i * ts).astype(bf16)
        w = w_ref[...]
        sc = sc_ref[...]
        dy_tile = dy_ref[...]
        z = _dot(d, w)
        dsc_ref[...] += jnp.sum(dy_tile * z, axis=0, keepdims=True)
        dy_next = jnp.where(i < n_tiles - 1, dy_next_ref[...], 0.0)
        dz = (jnp.concatenate([dy_tile, dy_next], axis=0) * sc).astype(bf16)
        dw_ref[...] += _dot_tn(d, dz[:ts])
        dd = _dot_nt(dz, w)
        e = dd / _pool_count(i * ts, ts + POOL_HALO)
        n = ts + POOL_HALO
        f2 = e + pltpu.roll(e, n - 1, 0)
        f4 = f2 + pltpu.roll(f2, n - 2, 0)
        f8 = f4 + pltpu.roll(f4, n - 4, 0)
        f16 = f8 + pltpu.roll(f8, n - 8, 0)
        fsum = _pool_select(_pool_lane_group(n), f2, f4, f8, f16)
        du_ref[...] = (fsum[:ts] - dd[:ts]).astype(bf16)

    tile, before, after = _pool_specs(ts, n_tiles)
    return pl.pallas_call(
        body, name="pool_bwd", grid=(n_tiles,),
        in_specs=[tile, before, tile, after, pl.BlockSpec((POOL_WIDTH, POOL_WIDTH), lambda i: (0, 0)),
                  pl.BlockSpec((1, POOL_WIDTH), lambda i: (0, 0))],
        out_specs=[tile, pl.BlockSpec((POOL_WIDTH, POOL_WIDTH), lambda i: (0, 0)),
                   pl.BlockSpec((1, POOL_WIDTH), lambda i: (0, 0))],
        out_shape=[jax.ShapeDtypeStruct((S, POOL_WIDTH), bf16), jax.ShapeDtypeStruct((POOL_WIDTH, POOL_WIDTH), f32),
                   jax.ShapeDtypeStruct((1, POOL_WIDTH), f32)],
        compiler_params=_params("arbitrary"),
    )(u, u, dy, dy, w_bd, scale)


SUPER = BLOCK * DILATIONS[-1]
UNITS = SUPER // BLOCK
FWD_UNROLL = 4
BWD_UNROLL = 4


def _band_mask(has_prev):
    qi = lax.broadcasted_iota(jnp.int32, (BLOCK, 2 * BLOCK), 0)
    kj = lax.broadcasted_iota(jnp.int32, (BLOCK, 2 * BLOCK), 1)
    return (kj >= qi) & (kj <= qi + BLOCK) & ((kj >= BLOCK) | has_prev)


def _head0_mask(rows=BLOCK):
    return lax.broadcasted_iota(jnp.int32, (rows, 128), 1) < HEAD_DIM


def _band_mask_t(has_prev):
    ki = lax.broadcasted_iota(jnp.int32, (2 * BLOCK, 2 * BLOCK), 0)
    qj = lax.broadcasted_iota(jnp.int32, (2 * BLOCK, 2 * BLOCK), 1) % BLOCK
    return (ki >= qj) & (ki <= qj + BLOCK) & ((ki >= BLOCK) | has_prev)


def _head_pair_rows(a, h0):
    zero = jnp.zeros_like(a)
    return jnp.concatenate([jnp.where(h0, a, zero), jnp.where(h0, zero, a)], axis=0)


def _per_query_row(stat):
    t = stat.T
    return jnp.concatenate([jnp.concatenate([t[:HEAD_DIM]] * 4, axis=0), jnp.concatenate([t[HEAD_DIM:]] * 4, axis=0)],
                           axis=1)


def _natural_rows(d, r, n):
    if d == 1:
        return pl.ds(pl.multiple_of(n * BLOCK, BLOCK), BLOCK)
    return pl.ds(n * (BLOCK * d) + r, BLOCK, stride=d)


def _unit_place(d, u):
    per_stream = UNITS // d
    return u // per_stream, u % per_stream, per_stream


def _block_rows(n):
    return pl.ds(pl.multiple_of(n * BLOCK, BLOCK), BLOCK)


def _band(cur_ref, tail_ref, r, n):
    before = jnp.where(n > 0, cur_ref[r, _block_rows(jnp.maximum(n - 1, 0)), :], tail_ref[r])
    return jnp.concatenate([before, cur_ref[r, _block_rows(n), :]], axis=0)


def _attn_in_specs(S, with_do):
    specs = []
    last = S // SUPER - 1
    for d in DILATIONS:
        per_stream = UNITS // d
        cur = pl.BlockSpec((d, SUPER // d, 128), lambda hp, sb: (0, jnp.minimum(sb, last), hp))
        tail = pl.BlockSpec(
            (d, BLOCK, 128),
            lambda hp, sb, per_stream=per_stream: (0, jnp.maximum(jnp.minimum(sb, last) * per_stream - 1, 0), hp))
        specs += [cur] * (2 if with_do else 1) + [cur, tail, cur, tail]
    return specs


def _attn_fwd(qs, ks, vs, pack):
    S = qs[0].shape[1]
    n_dil = len(DILATIONS)
    n_steps = S // SUPER
    n_total = (ATTN_WIDTH // 128) * n_steps

    def body(*refs):
        ins, pack_ref = refs[:5 * n_dil], refs[5 * n_dil]
        out_ref, lse_ref, gathered_ref = refs[5 * n_dil + 1:5 * n_dil + 4]
        scratch = refs[5 * n_dil + 4:]
        o_sc, l_sc = scratch[:n_dil], scratch[n_dil:2 * n_dil]
        gather = _Gather(pack_ref, gathered_ref, *scratch[2 * n_dil:])
        sb = pl.program_id(1)
        step = pl.program_id(0) * n_steps + sb

        @pl.when(step == 0)
        def _():
            gather.start()

        h0 = _head0_mask()
        for ci, d in enumerate(DILATIONS):
            q_ref, kc_ref, kp_ref, vc_ref, vp_ref = ins[5 * ci:5 * ci + 5]

            def unit(u, carry, d=d, ci=ci, q_ref=q_ref, kc_ref=kc_ref, kp_ref=kp_ref, vc_ref=vc_ref, vp_ref=vp_ref):
                r, n, _ = _unit_place(d, u)
                qv = q_ref[r, _block_rows(n), :]
                kb = _band(kc_ref, kp_ref, r, n)
                vb = _band(vc_ref, vp_ref, r, n)
                valid = _band_mask((sb > 0) | (n > 0))
                outs, lses = [], []
                for h in range(2):
                    keep = h0 if h == 0 else jnp.logical_not(h0)
                    qh = jnp.where(keep, qv, jnp.zeros_like(qv))
                    s = jnp.where(valid, _dot_nt(qh, kb) * ATTN_SCALE, NEG)
                    m = jnp.max(s, axis=1, keepdims=True)
                    e = jnp.exp(s - m)
                    den = jnp.sum(e, axis=1, keepdims=True)
                    outs.append(_dot((e / den).astype(bf16), vb))
                    lses.append(jnp.broadcast_to(m + jnp.log(den), (BLOCK, 128)))
                rows = _natural_rows(d, r, n)
                o_sc[ci][rows, :] = jnp.where(h0, outs[0], outs[1])
                l_sc[ci][rows, :] = jnp.where(h0, lses[0], lses[1])
                return carry

            lax.fori_loop(0, UNITS, unit, 0, unroll=FWD_UNROLL)

        def merge(t, carry):
            rows = pl.ds(pl.multiple_of(t * 256, 256), 256)
            a, b, c = l_sc[0][rows, :], l_sc[1][rows, :], l_sc[2][rows, :]
            m = jnp.maximum(jnp.maximum(a, b), c)
            ea, eb, ec = jnp.exp(a - m), jnp.exp(b - m), jnp.exp(c - m)
            tot = ea + eb + ec
            out_ref[rows, :] = ((ea / tot) * o_sc[0][rows, :] + (eb / tot) * o_sc[1][rows, :]
                                + (ec / tot) * o_sc[2][rows, :]).astype(bf16)
            lse_ref[rows, :] = m + jnp.log(tot)
            return carry

        lax.fori_loop(0, SUPER // 256, merge, 0)

        @pl.when(step == n_total // 2)
        def _():
            gather.pass_on()

        @pl.when(step == n_total - 1)
        def _():
            gather.finish()

    args = []
    for q, k, v in zip(qs, ks, vs):
        args += [q, k, k, v, v]
    nat = pl.BlockSpec((SUPER, 128), lambda hp, sb: (sb, hp))
    rows = pack.shape[0]
    return pl.pallas_call(
        body, name="attn_fwd", grid=(ATTN_WIDTH // 128, n_steps),
        in_specs=_attn_in_specs(S, False) + [ANY], out_specs=[nat, nat, ANY],
        out_shape=[jax.ShapeDtypeStruct((S, ATTN_WIDTH), bf16), jax.ShapeDtypeStruct((S, ATTN_WIDTH), f32),
                   _Gather.out_shape(rows, pack.dtype)],
        scratch_shapes=[pltpu.VMEM((SUPER, 128), f32)] * (2 * n_dil) + _Gather.scratch(rows, pack.dtype),
        compiler_params=_params("arbitrary", "arbitrary"),
    )(*args, pack)


def _attn_bwd(qs, ks, vs, dos, lse, delta, chip_sum):
    S = qs[0].shape[1]
    n_steps = S // SUPER
    last = n_steps - 1
    n_dil = len(DILATIONS)
    n_total = (ATTN_WIDTH // 128) * (n_steps + 1)

    def body(*refs):
        ins, (lse_ref, dl_ref, sum_ref) = refs[:6 * n_dil], refs[6 * n_dil:6 * n_dil + 3]
        dq_ref, dk_ref, dv_ref, others_ref = refs[6 * n_dil + 3:6 * n_dil + 7]
        dq_acc, dk_acc, dv_acc = refs[6 * n_dil + 7:6 * n_dil + 10]
        scatter = _Scatter(sum_ref, others_ref, *refs[6 * n_dil + 10:])
        sb = pl.program_id(1)
        step = pl.program_id(0) * (n_steps + 1) + sb
        cur = sb % 2
        prv = 1 - cur

        @pl.when(step == 0)
        def _():
            scatter.start()

        @pl.when(sb < n_steps)
        def _():
            dq_acc[...] = jnp.zeros_like(dq_acc)
            dk_acc[cur] = jnp.zeros((SUPER, 128), f32)
            dv_acc[cur] = jnp.zeros((SUPER, 128), f32)
            h0 = _head0_mask()
            for ci, d in enumerate(DILATIONS):
                q_ref, do_ref, kc_ref, kp_ref, vc_ref, vp_ref = ins[6 * ci:6 * ci + 6]

                def unit(u, carry, d=d, q_ref=q_ref, do_ref=do_ref, kc_ref=kc_ref, kp_ref=kp_ref, vc_ref=vc_ref,
                         vp_ref=vp_ref):
                    r, n, per_stream = _unit_place(d, u)
                    qv = q_ref[r, _block_rows(n), :]
                    dov = do_ref[r, _block_rows(n), :]
                    kb = _band(kc_ref, kp_ref, r, n)
                    vb = _band(vc_ref, vp_ref, r, n)
                    rows = _natural_rows(d, r, n)
                    has_prev = (sb > 0) | (n > 0)
                    q_pair = _head_pair_rows(qv, h0)
                    do_pair = _head_pair_rows(dov, h0)
                    s_t = jnp.where(_band_mask_t(has_prev), _dot_nt(kb, q_pair) * ATTN_SCALE, NEG)
                    p_t = jnp.exp(s_t - _per_query_row(lse_ref[rows, :]))
                    dp_t = _dot_nt(vb, do_pair)
                    ds_t = (p_t * (dp_t - _per_query_row(dl_ref[rows, :])) * ATTN_SCALE).astype(bf16)
                    dvb = _dot(p_t.astype(bf16), do_pair)
                    dkb = _dot(ds_t, q_pair)
                    dq_pair = _dot_tn(ds_t, kb)
                    dq_acc[rows, :] += jnp.where(h0, dq_pair[:BLOCK], dq_pair[BLOCK:])
                    dk_acc[cur, rows, :] += dkb[BLOCK:]
                    dv_acc[cur, rows, :] += dvb[BLOCK:]

                    slot = jnp.where((n > 0) | (sb == 0), cur, prv)
                    before = _natural_rows(d, r, jnp.where(n > 0, n - 1, per_stream - 1))
                    dk_acc[slot, before, :] += dkb[:BLOCK]
                    dv_acc[slot, before, :] += dvb[:BLOCK]
                    return carry

                lax.fori_loop(0, UNITS, unit, 0, unroll=BWD_UNROLL)
            dq_ref[...] = dq_acc[...].astype(bf16)

        @pl.when(sb > 0)
        def _():
            dk_ref[...] = dk_acc[prv].astype(bf16)
            dv_ref[...] = dv_acc[prv].astype(bf16)

        @pl.when(step == n_total - 1)
        def _():
            scatter.finish()

    args = []
    for q, k, v, do in zip(qs, ks, vs, dos):
        args += [q, do, k, k, v, v]
    nat = pl.BlockSpec((SUPER, 128), lambda hp, sb: (jnp.minimum(sb, last), hp))
    nat_before = pl.BlockSpec((SUPER, 128), lambda hp, sb: (jnp.clip(sb - 1, 0, last), hp))
    out = jax.ShapeDtypeStruct((S, ATTN_WIDTH), bf16)
    half = chip_sum.shape[1]
    return pl.pallas_call(
        body, name="attn_bwd", grid=(ATTN_WIDTH // 128, n_steps + 1),
        in_specs=_attn_in_specs(S, True) + [nat, nat, ANY], out_specs=[nat, nat_before, nat_before, ANY],
        out_shape=[out, out, out, _Scatter.out_shape(half, chip_sum.dtype)],
        scratch_shapes=[pltpu.VMEM((SUPER, 128), f32), pltpu.VMEM((2, SUPER, 128), f32),
                        pltpu.VMEM((2, SUPER, 128), f32)] + _Scatter.scratch(half),
        compiler_params=_params("arbitrary", "arbitrary"),
    )(*args, lse, delta, chip_sum)


def _rms(v):
    return lax.rsqrt(jnp.mean(v * v, axis=-1, keepdims=True) + EPS)


def _out_proj(pool_out, attn_out, w_out, x, g2, g3):
    S = x.shape[0]
    ts = 512

    def body(p_ref, a_ref, w_ref, x_ref, g2_ref, g3_ref, mix_ref, x2_ref, h2_ref):
        mix = _dot(p_ref[...], w_ref[:POOL_WIDTH, :]) + _dot(a_ref[...], w_ref[POOL_WIDTH:, :])
        mix_ref[...] = mix
        x2 = x_ref[...] + (mix * _rms(mix)) * g2_ref[...]
        x2_ref[...] = x2
        h2_ref[...] = ((x2 * _rms(x2)) * g3_ref[...]).astype(bf16)

    row = lambda w: pl.BlockSpec((ts, w), lambda i: (i, 0))
    gain = pl.BlockSpec((1, D_MODEL), lambda i: (0, 0))
    return pl.pallas_call(
        body, name="out_proj", grid=(S // ts,),
        in_specs=[row(POOL_WIDTH), row(ATTN_WIDTH), pl.BlockSpec((D_MODEL, D_MODEL), lambda i: (0, 0)),
                  row(D_MODEL), gain, gain],
        out_specs=[row(D_MODEL)] * 3,
        out_shape=[jax.ShapeDtypeStruct((S, D_MODEL), f32), jax.ShapeDtypeStruct((S, D_MODEL), f32),
                   jax.ShapeDtypeStruct((S, D_MODEL), bf16)],
        compiler_params=_params("parallel"),
    )(pool_out, attn_out, w_out, x, g2, g3)


FF_TILE = 256
FF_ROWS = 256


def _sigmoid(g):
    return 1.0 / (1.0 + jnp.exp(-g))


def _ffn_fwd(h2, w_gate, w_up, w_down):
    S = h2.shape[0]
    ts = 1024

    def body(h_ref, wg_ref, wu_ref, wd_ref, gate_ref, up_ref, f_ref):
        def rows_pass(first):
            def sub(i, carry):
                rows = pl.ds(pl.multiple_of(i * FF_ROWS, FF_ROWS), FF_ROWS)
                h = h_ref[rows, :]
                gate = _dot_nt(h, wg_ref[...])
                up = _dot_nt(h, wu_ref[...])
                gate_ref[rows, :] = gate.astype(bf16)
                up_ref[rows, :] = up.astype(bf16)
                part = _dot((gate * _sigmoid(gate) * up).astype(bf16), wd_ref[...])
                if first:
                    f_ref[rows, :] = part
                else:
                    f_ref[rows, :] += part
                return carry

            lax.fori_loop(0, ts // FF_ROWS, sub, 0, unroll=True)

        @pl.when(pl.program_id(1) == 0)
        def _():
            rows_pass(True)

        @pl.when(pl.program_id(1) > 0)
        def _():
            rows_pass(False)

    act = pl.BlockSpec((ts, FF_TILE), lambda i, j: (i, j))
    return pl.pallas_call(
        body, name="ffn_fwd", grid=(S // ts, D_FF // FF_TILE),
        in_specs=[pl.BlockSpec((ts, D_MODEL), lambda i, j: (i, 0)),
                  pl.BlockSpec((FF_TILE, D_MODEL), lambda i, j: (j, 0)),
                  pl.BlockSpec((FF_TILE, D_MODEL), lambda i, j: (j, 0)),
                  pl.BlockSpec((FF_TILE, D_MODEL), lambda i, j: (j, 0))],
        out_specs=[act, act, pl.BlockSpec((ts, D_MODEL), lambda i, j: (i, 0))],
        out_shape=[jax.ShapeDtypeStruct((S, D_FF), bf16), jax.ShapeDtypeStruct((S, D_FF), bf16),
                   jax.ShapeDtypeStruct((S, D_MODEL), f32)],
        compiler_params=_params("parallel", "arbitrary"),
    )(h2, w_gate, w_up, w_down)


def _loss_head(f, x2, target, g4):
    S = f.shape[0]
    ts = 512

    def body(f_ref, x2_ref, t_ref, g_ref, dy_ref, df_ref, dg_ref, loss_ref):
        @pl.when(pl.program_id(0) == 0)
        def _():
            dg_ref[...] = jnp.zeros_like(dg_ref)
            loss_ref[...] = jnp.zeros_like(loss_ref)

        fv = f_ref[...]
        g = g_ref[...]
        r = _rms(fv)
        fhat = fv * r
        err = (x2_ref[...] + fhat * g) - t_ref[...]
        loss_ref[...] += 0.5 * jnp.sum(jnp.mean(err * err, axis=-1, keepdims=True), axis=0, keepdims=True)
        dy = err * (1.0 / D_MODEL)
        dy_ref[...] = dy
        dg_ref[...] += jnp.sum(dy * fhat, axis=0, keepdims=True)
        dyg = dy * g
        df_ref[...] = (r * (dyg - fhat * jnp.mean(dyg * fhat, axis=-1, keepdims=True))).astype(bf16)

    row = pl.BlockSpec((ts, D_MODEL), lambda i: (i, 0))
    gain = pl.BlockSpec((1, D_MODEL), lambda i: (0, 0))
    return pl.pallas_call(
        body, name="loss_head", grid=(S // ts,), in_specs=[row, row, row, gain],
        out_specs=[row, row, gain, pl.BlockSpec((1, 1), lambda i: (0, 0))],
        out_shape=[jax.ShapeDtypeStruct((S, D_MODEL), f32), jax.ShapeDtypeStruct((S, D_MODEL), bf16),
                   jax.ShapeDtypeStruct((1, D_MODEL), f32), jax.ShapeDtypeStruct((1, 1), f32)],
        compiler_params=_params("arbitrary"),
    )(f, x2, target, g4)


def _ffn_bwd(df, gate, up, w_gate, w_up, w_down):
    S = df.shape[0]
    ts = 1024

    def body(df_ref, gate_ref, up_ref, wg_ref, wu_ref, wd_ref, a_ref, dgate_ref, dup_ref, dh_ref):
        def rows_pass(first):
            def sub(i, carry):
                rows = pl.ds(pl.multiple_of(i * FF_ROWS, FF_ROWS), FF_ROWS)
                da = _dot_nt(df_ref[rows, :], wd_ref[...])
                g = gate_ref[rows, :].astype(f32)
                u = up_ref[rows, :].astype(f32)
                sig = _sigmoid(g)
                silu = g * sig
                a_ref[rows, :] = (silu * u).astype(bf16)
                dup = (da * silu).astype(bf16)
                dgate = (da * u * (sig * (1.0 + g * (1.0 - sig)))).astype(bf16)
                dup_ref[rows, :] = dup
                dgate_ref[rows, :] = dgate
                part = _dot(dgate, wg_ref[...]) + _dot(dup, wu_ref[...])
                if first:
                    dh_ref[rows, :] = part
                else:
                    dh_ref[rows, :] += part
                return carry

            lax.fori_loop(0, ts // FF_ROWS, sub, 0, unroll=True)

        @pl.when(pl.program_id(1) == 0)
        def _():
            rows_pass(True)

        @pl.when(pl.program_id(1) > 0)
        def _():
            rows_pass(False)

    act = pl.BlockSpec((ts, FF_TILE), lambda i, j: (i, j))
    row = pl.BlockSpec((ts, D_MODEL), lambda i, j: (i, 0))
    return pl.pallas_call(
        body, name="ffn_bwd", grid=(S // ts, D_FF // FF_TILE),
        in_specs=[row, act, act,
                  pl.BlockSpec((FF_TILE, D_MODEL), lambda i, j: (j, 0)),
                  pl.BlockSpec((FF_TILE, D_MODEL), lambda i, j: (j, 0)),
                  pl.BlockSpec((FF_TILE, D_MODEL), lambda i, j: (j, 0))],
        out_specs=[act, act, act, row],
        out_shape=[jax.ShapeDtypeStruct((S, D_FF), bf16)] * 3 + [jax.ShapeDtypeStruct((S, D_MODEL), f32)],
        compiler_params=_params("parallel", "arbitrary"),
    )(df, gate, up, w_gate, w_up, w_down)


def _norm_bwd(dh2, dy, x2, mix, g3, g2):
    S = dh2.shape[0]
    ts = 512

    def body(dh_ref, dy_ref, x2_ref, mix_ref, g3_ref, g2_ref, dx2_ref, dmix_ref, dg3_ref, dg2_ref):
        @pl.when(pl.program_id(0) == 0)
        def _():
            dg3_ref[...] = jnp.zeros_like(dg3_ref)
            dg2_ref[...] = jnp.zeros_like(dg2_ref)

        dh = dh_ref[...]
        x2 = x2_ref[...]
        r3 = _rms(x2)
        xhat = x2 * r3
        dg3_ref[...] += jnp.sum(dh * xhat, axis=0, keepdims=True)
        dhg = dh * g3_ref[...]
        dx2 = dy_ref[...] + r3 * (dhg - xhat * jnp.mean(dhg * xhat, axis=-1, keepdims=True))
        dx2_ref[...] = dx2
        mix = mix_ref[...]
        r2 = _rms(mix)
        mhat = mix * r2
        dg2_ref[...] += jnp.sum(dx2 * mhat, axis=0, keepdims=True)
        dmg = dx2 * g2_ref[...]
        dmix_ref[...] = (r2 * (dmg - mhat * jnp.mean(dmg * mhat, axis=-1, keepdims=True))).astype(bf16)

    row = pl.BlockSpec((ts, D_MODEL), lambda i: (i, 0))
    gain = pl.BlockSpec((1, D_MODEL), lambda i: (0, 0))
    return pl.pallas_call(
        body, name="norm_bwd", grid=(S // ts,), in_specs=[row, row, row, row, gain, gain],
        out_specs=[row, row, gain, gain],
        out_shape=[jax.ShapeDtypeStruct((S, D_MODEL), f32), jax.ShapeDtypeStruct((S, D_MODEL), bf16),
                   jax.ShapeDtypeStruct((1, D_MODEL), f32), jax.ShapeDtypeStruct((1, D_MODEL), f32)],
        compiler_params=_params("arbitrary"),
    )(dh2, dy, x2, mix, g3, g2)


def _out_proj_bwd(dmix, w_out, attn_out, head_ones):
    S = dmix.shape[0]
    ts = 512

    def body(dm_ref, w_ref, o_ref, ones_ref, dp_ref, dl_ref, *rest):
        do_refs, stage = rest[:-N_STAGE], rest[-N_STAGE:]
        dcat = _dot_nt(dm_ref[...], w_ref[...])
        dp_ref[...] = dcat[:, :POOL_WIDTH]
        do = dcat[:, POOL_WIDTH:]
        for j in range(ATTN_WIDTH // 128):
            stage[j][...] = do[:, j * 128:(j + 1) * 128]
        _store_streams(stage, do_refs, ts)
        prod = do * o_ref[...].astype(f32)
        hi = prod.astype(bf16)
        lo = (prod - hi.astype(f32)).astype(bf16)
        dl_ref[...] = _dot(hi, ones_ref[...]) + _dot(lo, ones_ref[...])

    row = lambda w: pl.BlockSpec((ts, w), lambda i: (i, 0))
    res = pl.pallas_call(
        body, name="out_proj_bwd", grid=(S // ts,),
        in_specs=[row(D_MODEL), pl.BlockSpec((D_MODEL, D_MODEL), lambda i: (0, 0)), row(ATTN_WIDTH),
                  pl.BlockSpec((ATTN_WIDTH, ATTN_WIDTH), lambda i: (0, 0))],
        out_specs=[row(POOL_WIDTH), row(ATTN_WIDTH)] + [_stream_spec(d, ts) for d in DILATIONS],
        out_shape=[jax.ShapeDtypeStruct((S, POOL_WIDTH), f32), jax.ShapeDtypeStruct((S, ATTN_WIDTH), f32)]
        + [_stream_shape(S, d) for d in DILATIONS],
        scratch_shapes=_stage_scratch(ts),
        compiler_params=_params("parallel"),
    )(dmix, w_out, attn_out, head_ones)
    return res[0], res[1], res[2:]


def _in_proj_bwd(du, dq, dk, dv, cos_t, sin_t, w_in, x, dx2, g1):
    S = x.shape[0]
    ts = 256

    def body(du_ref, dq_ref, dk_ref, dv_ref, cos_ref, sin_ref, w_ref, x_ref, dx2_ref, g_ref, gx_ref, dproj_ref, dg_ref):
        @pl.when(pl.program_id(0) == 0)
        def _():
            dg_ref[...] = jnp.zeros_like(dg_ref)

        dproj_ref[:, :POOL_WIDTH] = du_ref[...]
        cos = cos_ref[...]
        sin = sin_ref[...]
        first = _first_half_mask(ts)
        for j in range(ATTN_WIDTH // 128):
            cols = slice(j * 128, (j + 1) * 128)
            for base, ref in ((POOL_WIDTH, dq_ref), (POOL_WIDTH + ATTN_WIDTH, dk_ref)):
                g = ref[:, cols].astype(f32)
                pre = g * cos + _rope_partner(g * sin, first)
                dproj_ref[:, base + j * 128: base + (j + 1) * 128] = pre.astype(bf16)
        dproj_ref[:, POOL_WIDTH + 2 * ATTN_WIDTH:] = dv_ref[...]

        dh = _dot(dproj_ref[...], w_ref[...])
        xv = x_ref[...]
        r = _rms(xv)
        xhat = xv * r
        dg_ref[...] += jnp.sum(dh * xhat, axis=0, keepdims=True)
        dhg = dh * g_ref[...]
        gx_ref[...] = dx2_ref[...] + r * (dhg - xhat * jnp.mean(dhg * xhat, axis=-1, keepdims=True))

    row = lambda w: pl.BlockSpec((ts, w), lambda i: (i, 0))
    gain = pl.BlockSpec((1, D_MODEL), lambda i: (0, 0))
    return pl.pallas_call(
        body, name="in_proj_bwd", grid=(S // ts,),
        in_specs=[row(POOL_WIDTH)] + [row(ATTN_WIDTH)] * 3 + [row(128), row(128),
                  pl.BlockSpec((IN_WIDTH, D_MODEL), lambda i: (0, 0)), row(D_MODEL), row(D_MODEL), gain],
        out_specs=[row(D_MODEL), row(IN_WIDTH), gain],
        out_shape=[jax.ShapeDtypeStruct((S, D_MODEL), f32), jax.ShapeDtypeStruct((S, IN_WIDTH), bf16),
                   jax.ShapeDtypeStruct((1, D_MODEL), f32)],
        compiler_params=_params("arbitrary"),
    )(du, dq, dk, dv, cos_t, sin_t, w_in, x, dx2, g1)


def _matmul_tn(a, b, tn, name):
    K, M = a.shape
    N = b.shape[1]
    tk = 1024

    def body(a_ref, b_ref, o_ref):
        part = _dot_tn(a_ref[...], b_ref[...])

        @pl.when(pl.program_id(1) == 0)
        def _():
            o_ref[...] = part

        @pl.when(pl.program_id(1) > 0)
        def _():
            o_ref[...] += part

    return pl.pallas_call(
        body, name=name, grid=(N // tn, K // tk),
        in_specs=[pl.BlockSpec((tk, M), lambda n, k: (k, 0)), pl.BlockSpec((tk, tn), lambda n, k: (k, n))],
        out_specs=pl.BlockSpec((M, tn), lambda n, k: (0, n)),
        out_shape=jax.ShapeDtypeStruct((M, N), f32),
        compiler_params=_params("parallel", "arbitrary"),
    )(a, b)


def _rope_tables(S):
    half = HEAD_DIM // 2
    freqs = ROPE_THETA ** (-jnp.arange(half, dtype=f32) * (2.0 / HEAD_DIM))
    ang = jnp.arange(S).astype(f32)[:, None] * freqs[None, :]
    cos = jnp.tile(jnp.cos(ang), (1, 4))
    sin = jnp.sin(ang)
    sin = jnp.tile(jnp.concatenate([-sin, sin], axis=1), (1, 2))
    return cos, sin


def _block_diag(w_pool):
    w = jnp.zeros((POOL_WIDTH, POOL_WIDTH), w_pool.dtype)
    for g in range(POOL_WIDTH // POOL_GROUP):
        w = lax.dynamic_update_slice(w, w_pool[g], (g * POOL_GROUP, g * POOL_GROUP))
    return w


def _head_ones():
    head = np.arange(ATTN_WIDTH) // HEAD_DIM
    return jnp.asarray(head[:, None] == head[None, :], dtype=bf16)


def _place():
    x, y, c = lax.axis_index("x"), lax.axis_index("y"), lax.axis_index("c")
    chips = [(1 - x, y), (x, 1 - y), (1 - x, 1 - y)]
    return x, y, c, chips


ANY = pl.BlockSpec(memory_space=pl.ANY)
N_PEER_CHIPS = N_CHIPS - 1
ICI_PIECES = 4
D2D_PIECES = 8
LOCAL_PIECES = 8


def _row_chunks(rows, n, unit=32):
    units = rows // unit
    out, start = [], 0
    for i in range(n):
        size = (units // n + (1 if i < units % n else 0)) * unit
        out.append((start, size))
        start += size
    return [piece for piece in out if piece[1]]


class _LocalCopy:
    def __init__(self, src_rows, dst_rows, rows, buf, sems_in, sems_out):
        self.loads, self.stores = [], []
        for i, (start, size) in enumerate(_row_chunks(rows, LOCAL_PIECES)):
            r = pl.ds(start, size)
            self.loads.append(pltpu.make_async_copy(src_rows(r), buf.at[r], sems_in.at[i]))
            self.stores.append(pltpu.make_async_copy(buf.at[r], dst_rows(r), sems_out.at[i]))

    def start(self):
        for cp in self.loads:
            cp.start()

    def pass_on(self):
        for load, store in zip(self.loads, self.stores):
            load.wait()
            store.start()

    def finish(self):
        for store in self.stores:
            store.wait()

    @staticmethod
    def scratch(rows, dtype):
        return [pltpu.VMEM((rows, D_MODEL), dtype), pltpu.SemaphoreType.DMA((LOCAL_PIECES,)),
                pltpu.SemaphoreType.DMA((LOCAL_PIECES,))]


class _Gather:
    def __init__(self, w_ref, out_ref, send1, recv1, send2, recv2, buf, sems_in, sems_out):
        x, y, c, chips = _place()
        me = 2 * x + y
        rows = w_ref.shape[0]
        half = rows // 2
        pieces = _row_chunks(half, ICI_PIECES)
        self.own = _LocalCopy(lambda r: w_ref.at[r], lambda r: out_ref.at[me, r], rows, buf, sems_in, sems_out)

        def rows_of(core, piece):
            start, size = piece
            return pl.ds(core * half + start, size)

        self.sends, self.arrivals, self.forwards, self.forward_arrivals = [], [], [], []
        for i, piece in enumerate(pieces):
            for j, (cx, cy) in enumerate(chips):
                k = j * len(pieces) + i
                there = 2 * cx + cy

                def direct(src_chip, cx=cx, cy=cy, k=k, piece=piece):
                    return pltpu.make_async_remote_copy(
                        src_ref=w_ref.at[rows_of(c, piece)], dst_ref=out_ref.at[src_chip, rows_of(c, piece)],
                        send_sem=send1.at[k], recv_sem=recv1.at[k], device_id=(cx, cy, c), device_id_type=MESH)

                def passed(core, there=there, k=k, piece=piece):
                    return pltpu.make_async_remote_copy(
                        src_ref=out_ref.at[there, rows_of(core, piece)], dst_ref=out_ref.at[there, rows_of(core, piece)],
                        send_sem=send2.at[k], recv_sem=recv2.at[k], device_id=(x, y, 1 - c), device_id_type=MESH)

                self.sends.append(direct(me))
                self.arrivals.append(direct(there))
                self.forwards.append(passed(c))
                self.forward_arrivals.append(passed(1 - c))

    def start(self):
        for cp in self.sends:
            cp.start()
        self.own.start()

    def pass_on(self):
        self.own.pass_on()
        for arrival, forward in zip(self.arrivals, self.forwards):
            arrival.wait_recv()
            forward.start()

    def finish(self):
        for arrival in self.forward_arrivals:
            arrival.wait_recv()
        for cp in self.sends + self.forwards:
            cp.wait_send()
        self.own.finish()

    @staticmethod
    def scratch(rows, dtype):
        n = N_PEER_CHIPS * len(_row_chunks(rows // 2, ICI_PIECES))
        return [pltpu.SemaphoreType.DMA((n,))] * 4 + _LocalCopy.scratch(rows, dtype)

    @staticmethod
    def out_shape(rows, dtype):
        return jax.ShapeDtypeStruct((N_CHIPS, rows, D_MODEL), dtype)


def _gather_weights(pack):
    rows = pack.shape[0]

    def body(w_ref, out_ref, *scratch):
        gather = _Gather(w_ref, out_ref, *scratch)
        gather.start()
        gather.pass_on()
        gather.finish()

    return pl.pallas_call(
        body, name="gather_weights", in_specs=[ANY], out_specs=ANY, out_shape=_Gather.out_shape(rows, pack.dtype),
        scratch_shapes=_Gather.scratch(rows, pack.dtype),
        compiler_params=pltpu.CompilerParams(vmem_limit_bytes=VMEM_LIMIT_V7X),
    )(pack)


class _Scatter:
    def __init__(self, h_ref, out_ref, send, recv):
        x, y, c, chips = _place()
        pieces = _row_chunks(h_ref.shape[1], ICI_PIECES)
        self.copies = []
        for i, (start, size) in enumerate(pieces):
            for j, (cx, cy) in enumerate(chips):
                k = j * len(pieces) + i
                self.copies.append(pltpu.make_async_remote_copy(
                    src_ref=h_ref.at[2 * cx + cy, pl.ds(start, size)], dst_ref=out_ref.at[j, pl.ds(start, size)],
                    send_sem=send.at[k], recv_sem=recv.at[k], device_id=(cx, cy, c), device_id_type=MESH))

    def start(self):
        for cp in self.copies:
            cp.start()

    def finish(self):
        for cp in self.copies:
            cp.wait_recv()
        for cp in self.copies:
            cp.wait_send()

    @staticmethod
    def scratch(half):
        n = N_PEER_CHIPS * len(_row_chunks(half, ICI_PIECES))
        return [pltpu.SemaphoreType.DMA((n,))] * 2

    @staticmethod
    def out_shape(half, dtype):
        return jax.ShapeDtypeStruct((N_PEER_CHIPS, half, D_MODEL), dtype)


def _scatter_to_chips(h):
    half = h.shape[1]

    def body(h_ref, out_ref, send, recv):
        scatter = _Scatter(h_ref, out_ref, send, recv)
        scatter.start()
        scatter.finish()

    return pl.pallas_call(
        body, name="scatter_to_chips", in_specs=[ANY], out_specs=ANY, out_shape=_Scatter.out_shape(half, h.dtype),
        scratch_shapes=_Scatter.scratch(half),
    )(h)


def _swap_halves(g, name):
    half = g.shape[1] // 2
    pieces = _row_chunks(half, D2D_PIECES)
    n = len(pieces)

    def body(g_ref, theirs_ref, send, recv):
        x, y, c, _ = _place()
        copies = []
        for s in range(N_CHIPS):
            for i, (start, size) in enumerate(pieces):
                copies.append(pltpu.make_async_remote_copy(
                    src_ref=g_ref.at[s, pl.ds((1 - c) * half + start, size)], dst_ref=theirs_ref.at[s, pl.ds(start, size)],
                    send_sem=send.at[s * n + i], recv_sem=recv.at[s * n + i],
                    device_id=(x, y, 1 - c), device_id_type=MESH))
        for cp in copies:
            cp.start()
        for cp in copies:
            cp.wait()

    return pl.pallas_call(
        body, name=name, in_specs=[ANY], out_specs=ANY,
        out_shape=jax.ShapeDtypeStruct((N_CHIPS, half, D_MODEL), g.dtype),
        scratch_shapes=[pltpu.SemaphoreType.DMA((N_CHIPS * n,))] * 2,
    )(g)


ADD_TILE_MAX_ROWS = 600


def _add_tile(half):
    return max(t for t in range(8, ADD_TILE_MAX_ROWS + 1, 8) if half % t == 0)


def _add_cores(g, theirs, name):
    half = theirs.shape[1]
    tr = _add_tile(half)
    n_t = half // tr

    def body(c_ref, g_ref, t_ref, o_ref):
        o_ref[...] = g_ref[...] + t_ref[...]

    blk = pl.BlockSpec((1, tr, D_MODEL), lambda s, t, c_ref: (s, t, 0))
    return pl.pallas_call(
        body, name=name,
        grid_spec=pltpu.PrefetchScalarGridSpec(
            num_scalar_prefetch=1, grid=(N_CHIPS, n_t),
            in_specs=[pl.BlockSpec((1, tr, D_MODEL), lambda s, t, c_ref: (s, c_ref[0] * n_t + t, 0)), blk],
            out_specs=blk),
        out_shape=jax.ShapeDtypeStruct(theirs.shape, theirs.dtype),
        compiler_params=_params("parallel", "parallel"),
    )(lax.axis_index("c").astype(jnp.int32).reshape(1), g, theirs)


def _add_chips(chip_sum, others, name):
    half = chip_sum.shape[1]
    tr = _add_tile(half)

    def body(me_ref, own_ref, o0, o1, o2, out_ref):
        out_ref[...] = ((own_ref[0] + o0[0]) + o1[0]) + o2[0]

    other = lambda j: pl.BlockSpec((1, tr, D_MODEL), lambda t, me_ref: (j, t, 0))
    return pl.pallas_call(
        body, name=name,
        grid_spec=pltpu.PrefetchScalarGridSpec(
            num_scalar_prefetch=1, grid=(half // tr,),
            in_specs=[pl.BlockSpec((1, tr, D_MODEL), lambda t, me_ref: (me_ref[0], t, 0)), other(0), other(1), other(2)],
            out_specs=pl.BlockSpec((tr, D_MODEL), lambda t, me_ref: (t, 0))),
        out_shape=jax.ShapeDtypeStruct((half, D_MODEL), chip_sum.dtype),
        compiler_params=_params("parallel"),
    )((2 * lax.axis_index("x") + lax.axis_index("y")).astype(jnp.int32).reshape(1), chip_sum, others, others, others)


def _join_halves(r):
    half = r.shape[0]
    pieces = _row_chunks(half, 2 * D2D_PIECES)
    n = len(pieces)

    def body(r_ref, out_ref, send, recv, buf, sems_in, sems_out):
        x, y, c, _ = _place()
        own = _LocalCopy(lambda rr: r_ref.at[rr], lambda rr: out_ref.at[c, rr], half, buf, sems_in, sems_out)
        own.start()

        def piece(i, core):
            start, size = pieces[i]
            return pltpu.make_async_remote_copy(
                src_ref=r_ref.at[pl.ds(start, size)], dst_ref=out_ref.at[core, pl.ds(start, size)],
                send_sem=send.at[i], recv_sem=recv.at[i], device_id=(x, y, 1 - c), device_id_type=MESH)

        copies = [piece(i, c) for i in range(n)]
        for cp in copies:
            cp.start()
        own.pass_on()
        for i in range(n):
            piece(i, 1 - c).wait_recv()
        for cp in copies:
            cp.wait_send()
        own.finish()

    return pl.pallas_call(
        body, name="join_halves", in_specs=[ANY], out_specs=ANY,
        out_shape=jax.ShapeDtypeStruct((2,) + r.shape, r.dtype),
        scratch_shapes=[pltpu.SemaphoreType.DMA((n,))] * 2 + _LocalCopy.scratch(half, r.dtype),
        compiler_params=pltpu.CompilerParams(vmem_limit_bytes=VMEM_LIMIT_V7X),
    )(r)


def _sum_small(block):
    def body(b_ref, out_ref, gathered, send, recv):
        x, y, c, _ = _place()
        me = 4 * x + 2 * y + c
        gathered[me] = b_ref[...]
        sends = []
        for kk in range(1, N_DEV):
            flip = lambda v, bit: 1 - v if bit else v
            peer = (flip(x, kk & 4), flip(y, kk & 2), flip(c, kk & 1))
            cp = pltpu.make_async_remote_copy(
                src_ref=b_ref, dst_ref=gathered.at[me], send_sem=send.at[kk - 1], recv_sem=recv.at[kk - 1],
                device_id=peer, device_id_type=MESH)
            cp.start()
            sends.append(cp)
        for kk in range(1, N_DEV):
            peer_index = jnp.bitwise_xor(me, kk)
            pltpu.make_async_remote_copy(
                src_ref=b_ref, dst_ref=gathered.at[peer_index], send_sem=send.at[kk - 1], recv_sem=recv.at[kk - 1],
                device_id=(x, y, c), device_id_type=MESH).wait_recv()
        for cp in sends:
            cp.wait_send()
        acc = gathered[0]
        for dev in range(1, N_DEV):
            acc = acc + gathered[dev]
        out_ref[...] = acc

    vmem = pl.BlockSpec(memory_space=pltpu.VMEM)
    return pl.pallas_call(
        body, name="sum_small", in_specs=[vmem], out_specs=vmem,
        out_shape=jax.ShapeDtypeStruct(block.shape, block.dtype),
        scratch_shapes=[pltpu.VMEM((N_DEV,) + block.shape, block.dtype),
                        pltpu.SemaphoreType.DMA((N_DEV - 1,)), pltpu.SemaphoreType.DMA((N_DEV - 1,))],
    )(block)


def _adamw(w, g, m, v, name):
    rows, cols = w.shape
    tr = rows
    for cand in (512, 256, 128, 64, 32, 16, 8):
        if rows % cand == 0:
            tr = cand
            break
    c1 = 1.0 - ADAM_B1 ** ADAM_STEP
    c2 = 1.0 - ADAM_B2 ** ADAM_STEP

    def body(w_ref, g_ref, m_ref, v_ref, d_ref, nm_ref, nv_ref):
        gv = g_ref[...]
        nm = ADAM_B1 * m_ref[...] + (1.0 - ADAM_B1) * gv
        nv = ADAM_B2 * v_ref[...] + (1.0 - ADAM_B2) * (gv * gv)
        nm_ref[...] = nm
        nv_ref[...] = nv
        d_ref[...] = -ADAM_LR * ((nm / c1) / (jnp.sqrt(nv / c2) + ADAM_EPS) + ADAM_WD * w_ref[...])

    blk = pl.BlockSpec((tr, cols), lambda i: (i, 0))
    shape = jax.ShapeDtypeStruct((rows, cols), f32)
    return pl.pallas_call(
        body, name=name, grid=(rows // tr,), in_specs=[blk] * 4, out_specs=[blk] * 3, out_shape=[shape] * 3,
        compiler_params=_params("parallel"),
    )(w, g, m, v)


LARGE = ("w_in", "w_out", "w_gate", "w_up", "w_down")
SMALL = ("ln_pre_mix", "ln_post_mix", "ln_pre_ffn", "ln_post_ffn", "pool_scale", "w_pool")
SHARD_ROWS = {"w_in": 640, "w_out": 256, "w_gate": 704, "w_up": 704, "w_down": 704}
COLUMN_SHARDED = ("w_in", "w_gate", "w_up")
NEEDED_FIRST = ("w_in",)
NEEDED_LATER = ("w_out", "w_gate", "w_up", "w_down")
READY_EARLY = ("w_out", "w_gate", "w_up", "w_down")
READY_LATE = ("w_in",)


def _pack_shard(shards, names):
    return jnp.concatenate([shards[n].T if n in COLUMN_SHARDED else shards[n] for n in names], axis=0)


def _unpack_shard(pack, names):
    out, row = {}, 0
    for n in names:
        part = pack[row:row + SHARD_ROWS[n]]
        out[n] = part.T if n in COLUMN_SHARDED else part
        row += SHARD_ROWS[n]
    return out


def _whole_from_shards(packs, names):
    out, row = {}, 0
    for n in names:
        rows = SHARD_ROWS[n]
        out[n] = packs[:, row:row + rows].reshape(N_CHIPS * rows, D_MODEL)
        row += rows
    return out


def _shards_from_whole(grads, names):
    return jnp.concatenate([grads[n].reshape(N_CHIPS, SHARD_ROWS[n], D_MODEL) for n in names], axis=1)


def _pack_small(vals):
    rows = [vals[n].reshape(1, D_MODEL) for n in SMALL[:4]]
    rows.append(jnp.pad(vals["pool_scale"].reshape(1, POOL_WIDTH), ((0, 0), (0, D_MODEL - POOL_WIDTH))))
    rows.append(jnp.pad(vals["loss"].reshape(1, 1), ((0, 0), (0, D_MODEL - 1))))
    rows.append(jnp.zeros((2, D_MODEL), f32))
    rows.append(vals["w_pool"].reshape(16, D_MODEL))
    return jnp.concatenate(rows, axis=0)


def _unpack_small(block):
    out = {n: block[i:i + 1] for i, n in enumerate(SMALL[:4])}
    out["pool_scale"] = block[4:5, :POOL_WIDTH]
    out["loss"] = block[5, 0]
    out["w_pool"] = block[8:24].reshape(1, 4, POOL_GROUP, POOL_GROUP)
    return out


def kernel(x, ln_pre_mix, w_in, w_pool, pool_scale, w_out, ln_post_mix, ln_pre_ffn, w_gate, w_up, w_down, ln_post_ffn, loss_target, m_ln_pre_mix, m_w_in, m_w_pool, m_pool_scale, m_w_out, m_ln_post_mix, m_ln_pre_ffn, m_w_gate, m_w_up, m_w_down, m_ln_post_ffn, v_ln_pre_mix, v_w_in, v_w_pool, v_pool_scale, v_w_out, v_ln_post_mix, v_ln_pre_ffn, v_w_gate, v_w_up, v_w_down, v_ln_post_ffn):
    w = dict(ln_pre_mix=ln_pre_mix, w_in=w_in, w_pool=w_pool, pool_scale=pool_scale, w_out=w_out,
             ln_post_mix=ln_post_mix, ln_pre_ffn=ln_pre_ffn, w_gate=w_gate, w_up=w_up, w_down=w_down,
             ln_post_ffn=ln_post_ffn)
    m = dict(ln_pre_mix=m_ln_pre_mix, w_in=m_w_in, w_pool=m_w_pool, pool_scale=m_pool_scale, w_out=m_w_out,
             ln_post_mix=m_ln_post_mix, ln_pre_ffn=m_ln_pre_ffn, w_gate=m_w_gate, w_up=m_w_up, w_down=m_w_down,
             ln_post_ffn=m_ln_post_ffn)
    v = dict(ln_pre_mix=v_ln_pre_mix, w_in=v_w_in, w_pool=v_w_pool, pool_scale=v_pool_scale, w_out=v_w_out,
             ln_post_mix=v_ln_post_mix, ln_pre_ffn=v_ln_pre_ffn, w_gate=v_w_gate, w_up=v_w_up, w_down=v_w_down,
             ln_post_ffn=v_ln_post_ffn)

    xs, target = x[0], loss_target[0]
    cos_t, sin_t = _rope_tables(xs.shape[0])
    w_bd = _block_diag(w_pool[0]).astype(bf16)
    shard = {n: w[n][0].astype(bf16) for n in LARGE}

    w_in_whole = _whole_from_shards(_gather_weights(_pack_shard(shard, NEEDED_FIRST)), NEEDED_FIRST)["w_in"]
    h1, u, qs, ks, vs = _in_proj(xs, ln_pre_mix, w_in_whole, cos_t, sin_t)
    pool_out = _pool_fwd(u, w_bd, pool_scale)
    attn_out, lse, later = _attn_fwd(qs, ks, vs, _pack_shard(shard, NEEDED_LATER))
    whole = _whole_from_shards(later, NEEDED_LATER)
    mix, x2, h2 = _out_proj(pool_out, attn_out, whole["w_out"], xs, ln_post_mix, ln_pre_ffn)
    gate, up, f = _ffn_fwd(h2, whole["w_gate"], whole["w_up"], whole["w_down"])
    dy, df, dg4, loss = _loss_head(f, x2, target, ln_post_ffn)

    large = {}
    a, dgate, dup, dh2 = _ffn_bwd(df, gate, up, whole["w_gate"], whole["w_up"], whole["w_down"])
    large["w_down"] = _matmul_tn(a, df, D_MODEL, "grad_w_down")
    large["w_gate"] = _matmul_tn(dgate, h2, D_MODEL, "grad_w_gate")
    large["w_up"] = _matmul_tn(dup, h2, D_MODEL, "grad_w_up")
    dx2, dmix, dg3, dg2 = _norm_bwd(dh2, dy, x2, mix, ln_pre_ffn, ln_post_mix)
    large["w_out"] = jnp.concatenate([_matmul_tn(pool_out, dmix, D_MODEL, "grad_w_out_pool"),
                                      _matmul_tn(attn_out, dmix, D_MODEL, "grad_w_out_attn")], axis=0)
    early = _shards_from_whole(large, READY_EARLY)
    early_chip = _add_cores(early, _swap_halves(early, "swap_halves_early"), "add_cores_early")
    dpool, delta, dos = _out_proj_bwd(dmix, whole["w_out"], attn_out, _head_ones())
    du, d_w_bd, d_scale = _pool_bwd(u, dpool, w_bd, pool_scale)
    dq, dk, dv, early_others = _attn_bwd(qs, ks, vs, dos, lse, delta, early_chip)
    grad_x, dproj, dg1 = _in_proj_bwd(du, dq, dk, dv, cos_t, sin_t, w_in_whole, xs, dx2, ln_pre_mix)
    large["w_in"] = _matmul_tn(dproj, h1, D_MODEL, "grad_w_in")
    late = _shards_from_whole(large, READY_LATE)
    late_chip = _add_cores(late, _swap_halves(late, "swap_halves_late"), "add_cores_late")
    late_others = _scatter_to_chips(late_chip)
    early_half = _add_chips(early_chip, early_others, "add_chips_early")
    late_half = _add_chips(late_chip, late_others, "add_chips_late")
    joined = _join_halves(jnp.concatenate([early_half, late_half], axis=0))
    n_early = early_half.shape[0]
    grads = _unpack_shard(joined[:, :n_early].reshape(-1, D_MODEL), READY_EARLY)
    grads.update(_unpack_shard(joined[:, n_early:].reshape(-1, D_MODEL), READY_LATE))

    d_w_pool = jnp.stack([d_w_bd[g * POOL_GROUP:(g + 1) * POOL_GROUP, g * POOL_GROUP:(g + 1) * POOL_GROUP]
                          for g in range(POOL_WIDTH // POOL_GROUP)])
    small = dict(ln_pre_mix=dg1, ln_post_mix=dg2, ln_pre_ffn=dg3, ln_post_ffn=dg4, pool_scale=d_scale, w_pool=d_w_pool)
    total = _unpack_small(_sum_small(_pack_small(dict(small, loss=loss))))
    for n in SMALL:
        grads[n] = total[n]

    delta_w, new_m, new_v = {}, {}, {}
    for n in LARGE:
        delta_w[n], new_m[n], new_v[n] = _adamw(w[n][0], grads[n], m[n][0], v[n][0], "adamw_" + n)
    small_state = [_pack_small(dict({n: s[n] for n in SMALL}, loss=jnp.zeros((), f32))) for s in (w, m, v)]
    small_grad = _pack_small(dict({n: grads[n] for n in SMALL}, loss=jnp.zeros((), f32)))
    sd, sm, sv = _adamw(small_state[0], small_grad, small_state[1], small_state[2], "adamw_small")
    for out, block in ((delta_w, sd), (new_m, sm), (new_v, sv)):
        un = _unpack_small(block)
        for n in SMALL:
            out[n] = un[n]

    names = ("ln_pre_mix", "w_in", "w_pool", "pool_scale", "w_out", "ln_post_mix", "ln_pre_ffn", "w_gate", "w_up",
             "w_down", "ln_post_ffn")
    full = lambda d: [d[n].reshape(w[n].shape) for n in names]
    return (total["loss"], grad_x[None], *full(grads), *full(delta_w), *full(new_m), *full(new_v))
```

```python
import numpy as np
import jax
import jax.numpy as jnp
from jax import lax
from jax.experimental import pallas as pl
from jax.experimental.pallas import tpu as pltpu

D_MODEL = 1024
POOL_WIDTH = 256
POOL_GROUP = 64
ATTN_WIDTH = 768
HEAD_DIM = 64
IN_WIDTH = 2560
D_FF = 2816
BLOCK = 128
DILATIONS = (1, 4, 16)
ROPE_THETA = 10000.0
EPS = 1e-6
ATTN_SCALE = 0.125
NEG = -1e30

ADAM_LR = 0.001
ADAM_B1 = 0.9
ADAM_B2 = 0.999
ADAM_EPS = 1e-08
ADAM_WD = 0.01
ADAM_STEP = 10

N_CHIPS = 4
N_DEV = 8
VMEM_LIMIT_V7X = 56 * 1024 * 1024
MESH = pl.DeviceIdType.MESH

f32 = jnp.float32
bf16 = jnp.bfloat16


def _params(*sem):
    return pltpu.CompilerParams(dimension_semantics=sem, vmem_limit_bytes=VMEM_LIMIT_V7X)


def _dot(a, b):
    return jnp.dot(a, b, preferred_element_type=f32)


def _dot_nt(a, b):
    return lax.dot_general(a, b, (((1,), (1,)), ((), ())), preferred_element_type=f32)


def _dot_tn(a, b):
    return lax.dot_general(a, b, (((0,), (0,)), ((), ())), preferred_element_type=f32)


def _rope_partner(a, first_half):
    return jnp.where(first_half, pltpu.roll(a, 96, 1), pltpu.roll(a, 32, 1))


def _first_half_mask(rows):
    lane = lax.broadcasted_iota(jnp.int32, (rows, 128), 1)
    return (lane % HEAD_DIM) < (HEAD_DIM // 2)


def _stream_spec(d, ts):
    return pl.BlockSpec((d, ts // d, ATTN_WIDTH), lambda i: (0, i, 0))


def _stream_shape(S, d):
    return jax.ShapeDtypeStruct((d, S // d, ATTN_WIDTH), bf16)


N_STAGE = ATTN_WIDTH // 128


def _stage_scratch(ts):
    return [pltpu.VMEM((ts, 128), f32)] * N_STAGE


def _store_streams(stage, out_refs, ts):
    for d, ref in zip(DILATIONS, out_refs):
        for r in range(d):
            rows = pl.ds(0, ts) if d == 1 else pl.ds(r, ts // d, stride=d)
            for j in range(N_STAGE):
                ref[r, :, j * 128:(j + 1) * 128] = stage[j][rows, :].astype(bf16)


def _in_proj(x, g1, w_in, cos_t, sin_t):
    S = x.shape[0]
    ts = 512

    def body(x_ref, g_ref, w_ref, cos_ref, sin_ref, h_ref, u_ref, *rest):
        outs, stage = rest[:-N_STAGE], rest[-N_STAGE:]
        xv = x_ref[...]
        r = lax.rsqrt(jnp.mean(xv * xv, axis=-1, keepdims=True) + EPS)
        h = ((xv * r) * g_ref[...]).astype(bf16)
        h_ref[...] = h
        proj = _dot_nt(h, w_ref[...])
        u_ref[...] = proj[:, :POOL_WIDTH]
        cos = cos_ref[...]
        sin = sin_ref[...]
        first = _first_half_mask(ts)
        n_dil = len(DILATIONS)
        for which, base in enumerate((POOL_WIDTH, POOL_WIDTH + ATTN_WIDTH)):
            for j in range(ATTN_WIDTH // 128):
                a = proj[:, base + j * 128: base + (j + 1) * 128]
                stage[j][...] = a * cos + _rope_partner(a, first) * sin
            _store_streams(stage, outs[which * n_dil:(which + 1) * n_dil], ts)
        for j in range(ATTN_WIDTH // 128):
            base = POOL_WIDTH + 2 * ATTN_WIDTH + j * 128
            stage[j][...] = proj[:, base:base + 128]
        _store_streams(stage, outs[2 * n_dil:], ts)

    row = lambda w: pl.BlockSpec((ts, w), lambda i: (i, 0))
    streams = [_stream_spec(d, ts) for d in DILATIONS] * 3
    res = pl.pallas_call(
        body, name="in_proj", grid=(S // ts,),
        in_specs=[row(D_MODEL), pl.BlockSpec((1, D_MODEL), lambda i: (0, 0)),
                  pl.BlockSpec((IN_WIDTH, D_MODEL), lambda i: (0, 0)), row(128), row(128)],
        out_specs=[row(D_MODEL), row(POOL_WIDTH)] + streams,
        out_shape=[jax.ShapeDtypeStruct((S, D_MODEL), bf16), jax.ShapeDtypeStruct((S, POOL_WIDTH), f32)]
        + [_stream_shape(S, d) for d in DILATIONS] * 3,
        scratch_shapes=_stage_scratch(ts),
        compiler_params=_params("parallel"),
    )(x, g1, w_in, cos_t, sin_t)
    n = len(DILATIONS)
    return res[0], res[1], res[2:2 + n], res[2 + n:2 + 2 * n], res[2 + 2 * n:]


POOL_HALO = 16


def _pool_lane_group(rows):
    return lax.broadcasted_iota(jnp.int32, (rows, POOL_WIDTH), 1) // POOL_GROUP


def _pool_select(group, s2, s4, s8, s16):
    return jnp.where(group == 0, s2, jnp.where(group == 1, s4, jnp.where(group == 2, s8, s16)))


def _pool_count(t0, rows):
    group = _pool_lane_group(rows)
    t = t0 + lax.broadcasted_iota(jnp.int32, (rows, POOL_WIDTH), 0)
    win = _pool_select(group, 2, 4, 8, 16)
    return jnp.minimum(t + 1, win).astype(f32)


def _pool_diff(u_halo, u_tile, t0):
    ts = u_tile.shape[0]
    ext = jnp.concatenate([u_halo, u_tile], axis=0)
    s2 = ext + pltpu.roll(ext, 1, 0)
    s4 = s2 + pltpu.roll(s2, 2, 0)
    s8 = s4 + pltpu.roll(s4, 4, 0)
    s16 = s8 + pltpu.roll(s8, 8, 0)
    group = _pool_lane_group(ts + POOL_HALO)
    wsum = _pool_select(group, s2, s4, s8, s16)[POOL_HALO:]
    return wsum / _pool_count(t0, ts) - u_tile


def _pool_specs(ts, n_tiles):
    tile = pl.BlockSpec((ts, POOL_WIDTH), lambda i: (i, 0))
    per = ts // POOL_HALO
    before = pl.BlockSpec((POOL_HALO, POOL_WIDTH), lambda i: (jnp.maximum(i * per - 1, 0), 0))
    after = pl.BlockSpec((POOL_HALO, POOL_WIDTH), lambda i: (jnp.minimum((i + 1) * per, n_tiles * per - 1), 0))
    return tile, before, after


def _pool_fwd(u, w_bd, scale):
    S = u.shape[0]
    ts = 512
    n_tiles = S // ts

    def body(u_ref, halo_ref, w_ref, sc_ref, y_ref):
        i = pl.program_id(0)
        halo = jnp.where(i > 0, halo_ref[...], 0.0)
        d = _pool_diff(halo, u_ref[...], i * ts)
        y_ref[...] = (_dot(d.astype(bf16), w_ref[...]) * sc_ref[...]).astype(bf16)

    tile, before, _ = _pool_specs(ts, n_tiles)
    return pl.pallas_call(
        body, name="pool_fwd", grid=(n_tiles,),
        in_specs=[tile, before, pl.BlockSpec((POOL_WIDTH, POOL_WIDTH), lambda i: (0, 0)),
                  pl.BlockSpec((1, POOL_WIDTH), lambda i: (0, 0))],
        out_specs=tile, out_shape=jax.ShapeDtypeStruct((S, POOL_WIDTH), bf16),
        compiler_params=_params("parallel"),
    )(u, u, w_bd, scale)


def _pool_bwd(u, dy, w_bd, scale):
    S = u.shape[0]
    ts = 512
    n_tiles = S // ts

    def body(u_ref, halo_ref, dy_ref, dy_next_ref, w_ref, sc_ref, du_ref, dw_ref, dsc_ref):
        i = pl.program_id(0)

        @pl.when(i == 0)
        def _():
            dw_ref[...] = jnp.zeros_like(dw_ref)
            dsc_ref[...] = jnp.zeros_like(dsc_ref)

        halo = jnp.where(i > 0, halo_ref[...], 0.0)
        d = _pool_diff(halo, u_ref[...], i * ts).astype(bf16)
        w = w_ref[...]
        sc = sc_ref[...]
        dy_tile = dy_ref[...]
        z = _dot(d, w)
        dsc_ref[...] += jnp.sum(dy_tile * z, axis=0, keepdims=True)
        dy_next = jnp.where(i < n_tiles - 1, dy_next_ref[...], 0.0)
        dz = (jnp.concatenate([dy_tile, dy_next], axis=0) * sc).astype(bf16)
        dw_ref[...] += _dot_tn(d, dz[:ts])
        dd = _dot_nt(dz, w)
        e = dd / _pool_count(i * ts, ts + POOL_HALO)
        n = ts + POOL_HALO
        f2 = e + pltpu.roll(e, n - 1, 0)
        f4 = f2 + pltpu.roll(f2, n - 2, 0)
        f8 = f4 + pltpu.roll(f4, n - 4, 0)
        f16 = f8 + pltpu.roll(f8, n - 8, 0)
        fsum = _pool_select(_pool_lane_group(n), f2, f4, f8, f16)
        du_ref[...] = (fsum[:ts] - dd[:ts]).astype(bf16)

    tile, before, after = _pool_specs(ts, n_tiles)
    return pl.pallas_call(
        body, name="pool_bwd", grid=(n_tiles,),
        in_specs=[tile, before, tile, after, pl.BlockSpec((POOL_WIDTH, POOL_WIDTH), lambda i: (0, 0)),
                  pl.BlockSpec((1, POOL_WIDTH), lambda i: (0, 0))],
        out_specs=[tile, pl.BlockSpec((POOL_WIDTH, POOL_WIDTH), lambda i: (0, 0)),
                   pl.BlockSpec((1, POOL_WIDTH), lambda i: (0, 0))],
        out_shape=[jax.ShapeDtypeStruct((S, POOL_WIDTH), bf16), jax.ShapeDtypeStruct((POOL_WIDTH, POOL_WIDTH), f32),
                   jax.ShapeDtypeStruct((1, POOL_WIDTH), f32)],
        compiler_params=_params("arbitrary"),
    )(u, u, dy, dy, w_bd, scale)


SUPER = BLOCK * DILATIONS[-1]
UNITS = SUPER // BLOCK
FWD_UNROLL = 16
BWD_UNROLL = 8


def _band_mask(has_prev):
    qi = lax.broadcasted_iota(jnp.int32, (BLOCK, 2 * BLOCK), 0)
    kj = lax.broadcasted_iota(jnp.int32, (BLOCK, 2 * BLOCK), 1)
    return (kj >= qi) & (kj <= qi + BLOCK) & ((kj >= BLOCK) | has_prev)


def _head0_mask(rows=BLOCK):
    return lax.broadcasted_iota(jnp.int32, (rows, 128), 1) < HEAD_DIM


def _band_mask_t(has_prev):
    ki = lax.broadcasted_iota(jnp.int32, (2 * BLOCK, 2 * BLOCK), 0)
    qj = lax.broadcasted_iota(jnp.int32, (2 * BLOCK, 2 * BLOCK), 1) % BLOCK
    return (ki >= qj) & (ki <= qj + BLOCK) & ((ki >= BLOCK) | has_prev)


def _head_pair_rows(a, h0):
    zero = jnp.zeros_like(a)
    return jnp.concatenate([jnp.where(h0, a, zero), jnp.where(h0, zero, a)], axis=0)


def _per_query_row(stat):
    t = stat.T
    return jnp.concatenate([jnp.concatenate([t[:HEAD_DIM]] * 4, axis=0), jnp.concatenate([t[HEAD_DIM:]] * 4, axis=0)],
                           axis=1)


def _natural_rows(d, r, n):
    if d == 1:
        return pl.ds(pl.multiple_of(n * BLOCK, BLOCK), BLOCK)
    return pl.ds(n * (BLOCK * d) + r, BLOCK, stride=d)


def _unit_place(d, u):
    per_stream = UNITS // d
    return u // per_stream, u % per_stream, per_stream


def _block_rows(n):
    return pl.ds(pl.multiple_of(n * BLOCK, BLOCK), BLOCK)


def _band(cur_ref, tail_ref, r, n):
    before = jnp.where(n > 0, cur_ref[r, _block_rows(jnp.maximum(n - 1, 0)), :], tail_ref[r])
    return jnp.concatenate([before, cur_ref[r, _block_rows(n), :]], axis=0)


def _attn_in_specs(S, with_do):
    specs = []
    last = S // SUPER - 1
    for d in DILATIONS:
        per_stream = UNITS // d
        cur = pl.BlockSpec((d, SUPER // d, 128), lambda hp, sb: (0, jnp.minimum(sb, last), hp))
        tail = pl.BlockSpec(
            (d, BLOCK, 128),
            lambda hp, sb, per_stream=per_stream: (0, jnp.maximum(jnp.minimum(sb, last) * per_stream - 1, 0), hp))
        specs += [cur] * (2 if with_do else 1) + [cur, tail, cur, tail]
    return specs


def _attn_fwd(qs, ks, vs, pack):
    S = qs[0].shape[1]
    n_dil = len(DILATIONS)
    n_steps = S // SUPER
    n_total = (ATTN_WIDTH // 128) * n_steps

    def body(*refs):
        ins, pack_ref = refs[:5 * n_dil], refs[5 * n_dil]
        out_ref, lse_ref, gathered_ref = refs[5 * n_dil + 1:5 * n_dil + 4]
        scratch = refs[5 * n_dil + 4:]
        o_sc, l_sc = scratch[:n_dil], scratch[n_dil:2 * n_dil]
        gather = _Gather(pack_ref, gathered_ref, *scratch[2 * n_dil:])
        sb = pl.program_id(1)
        step = pl.program_id(0) * n_steps + sb

        @pl.when(step == 0)
        def _():
            gather.start()

        h0 = _head0_mask()
        for ci, d in enumerate(DILATIONS):
            q_ref, kc_ref, kp_ref, vc_ref, vp_ref = ins[5 * ci:5 * ci + 5]

            def unit(u, carry, d=d, ci=ci, q_ref=q_ref, kc_ref=kc_ref, kp_ref=kp_ref, vc_ref=vc_ref, vp_ref=vp_ref):
                r, n, _ = _unit_place(d, u)
                qv = q_ref[r, _block_rows(n), :]
                kb = _band(kc_ref, kp_ref, r, n)
                vb = _band(vc_ref, vp_ref, r, n)
                valid = _band_mask((sb > 0) | (n > 0))
                outs, lses = [], []
                for h in range(2):
                    keep = h0 if h == 0 else jnp.logical_not(h0)
                    qh = jnp.where(keep, qv, jnp.zeros_like(qv))
                    s = jnp.where(valid, _dot_nt(qh, kb) * ATTN_SCALE, NEG)
                    m = jnp.max(s, axis=1, keepdims=True)
                    e = jnp.exp(s - m)
                    den = jnp.sum(e, axis=1, keepdims=True)
                    outs.append(_dot((e / den).astype(bf16), vb))
                    lses.append(jnp.broadcast_to(m + jnp.log(den), (BLOCK, 128)))
                rows = _natural_rows(d, r, n)
                o_sc[ci][rows, :] = jnp.where(h0, outs[0], outs[1])
                l_sc[ci][rows, :] = jnp.where(h0, lses[0], lses[1])
                return carry

            lax.fori_loop(0, UNITS, unit, 0, unroll=FWD_UNROLL)

        def merge(t, carry):
            rows = pl.ds(pl.multiple_of(t * 256, 256), 256)
            a, b, c = l_sc[0][rows, :], l_sc[1][rows, :], l_sc[2][rows, :]
            m = jnp.maximum(jnp.maximum(a, b), c)
            ea, eb, ec = jnp.exp(a - m), jnp.exp(b - m), jnp.exp(c - m)
            tot = ea + eb + ec
            out_ref[rows, :] = ((ea / tot) * o_sc[0][rows, :] + (eb / tot) * o_sc[1][rows, :]
                                + (ec / tot) * o_sc[2][rows, :]).astype(bf16)
            lse_ref[rows, :] = m + jnp.log(tot)
            return carry

        lax.fori_loop(0, SUPER // 256, merge, 0)

        @pl.when(step == n_total // 2)
        def _():
            gather.pass_on()

        @pl.when(step == n_total - 1)
        def _():
            gather.finish()

    args = []
    for q, k, v in zip(qs, ks, vs):
        args += [q, k, k, v, v]
    nat = pl.BlockSpec((SUPER, 128), lambda hp, sb: (sb, hp))
    rows = pack.shape[0]
    return pl.pallas_call(
        body, name="attn_fwd", grid=(ATTN_WIDTH // 128, n_steps),
        in_specs=_attn_in_specs(S, False) + [ANY], out_specs=[nat, nat, ANY],
        out_shape=[jax.ShapeDtypeStruct((S, ATTN_WIDTH), bf16), jax.ShapeDtypeStruct((S, ATTN_WIDTH), f32),
                   _Gather.out_shape(rows, pack.dtype)],
        scratch_shapes=[pltpu.VMEM((SUPER, 128), f32)] * (2 * n_dil) + _Gather.scratch(rows, pack.dtype),
        compiler_params=_params("arbitrary", "arbitrary"),
    )(*args, pack)


def _attn_bwd(qs, ks, vs, dos, lse, delta, chip_sum):
    S = qs[0].shape[1]
    n_steps = S // SUPER
    last = n_steps - 1
    n_dil = len(DILATIONS)
    n_total = (ATTN_WIDTH // 128) * (n_steps + 1)

    def body(*refs):
        ins, (lse_ref, dl_ref, sum_ref) = refs[:6 * n_dil], refs[6 * n_dil:6 * n_dil + 3]
        dq_ref, dk_ref, dv_ref, others_ref = refs[6 * n_dil + 3:6 * n_dil + 7]
        dq_acc, dk_acc, dv_acc = refs[6 * n_dil + 7:6 * n_dil + 10]
        scatter = _Scatter(sum_ref, others_ref, *refs[6 * n_dil + 10:])
        sb = pl.program_id(1)
        step = pl.program_id(0) * (n_steps + 1) + sb
        cur = sb % 2
        prv = 1 - cur

        @pl.when(step == 0)
        def _():
            scatter.start()

        @pl.when(sb < n_steps)
        def _():
            dq_acc[...] = jnp.zeros_like(dq_acc)
            dk_acc[cur] = jnp.zeros((SUPER, 128), f32)
            dv_acc[cur] = jnp.zeros((SUPER, 128), f32)
            h0 = _head0_mask()
            for ci, d in enumerate(DILATIONS):
                q_ref, do_ref, kc_ref, kp_ref, vc_ref, vp_ref = ins[6 * ci:6 * ci + 6]

                def unit(u, carry, d=d, q_ref=q_ref, do_ref=do_ref, kc_ref=kc_ref, kp_ref=kp_ref, vc_ref=vc_ref,
                         vp_ref=vp_ref):
                    r, n, per_stream = _unit_place(d, u)
                    qv = q_ref[r, _block_rows(n), :]
                    dov = do_ref[r, _block_rows(n), :]
                    kb = _band(kc_ref, kp_ref, r, n)
                    vb = _band(vc_ref, vp_ref, r, n)
                    rows = _natural_rows(d, r, n)
                    has_prev = (sb > 0) | (n > 0)
                    q_pair = _head_pair_rows(qv, h0)
                    do_pair = _head_pair_rows(dov, h0)
                    s_t = jnp.where(_band_mask_t(has_prev), _dot_nt(kb, q_pair) * ATTN_SCALE, NEG)
                    p_t = jnp.exp(s_t - _per_query_row(lse_ref[rows, :]))
                    dp_t = _dot_nt(vb, do_pair)
                    ds_t = (p_t * (dp_t - _per_query_row(dl_ref[rows, :])) * ATTN_SCALE).astype(bf16)
                    dvb = _dot(p_t.astype(bf16), do_pair)
                    dkb = _dot(ds_t, q_pair)
                    dq_pair = _dot_tn(ds_t, kb)
                    dq_acc[rows, :] += jnp.where(h0, dq_pair[:BLOCK], dq_pair[BLOCK:])
                    dk_acc[cur, rows, :] += dkb[BLOCK:]
                    dv_acc[cur, rows, :] += dvb[BLOCK:]

                    slot = jnp.where((n > 0) | (sb == 0), cur, prv)
                    before = _natural_rows(d, r, jnp.where(n > 0, n - 1, per_stream - 1))
                    dk_acc[slot, before, :] += dkb[:BLOCK]
                    dv_acc[slot, before, :] += dvb[:BLOCK]
                    return carry

                lax.fori_loop(0, UNITS, unit, 0, unroll=BWD_UNROLL)
            dq_ref[...] = dq_acc[...].astype(bf16)

        @pl.when(sb > 0)
        def _():
            dk_ref[...] = dk_acc[prv].astype(bf16)
            dv_ref[...] = dv_acc[prv].astype(bf16)

        @pl.when(step == n_total - 1)
        def _():
            scatter.finish()

    args = []
    for q, k, v, do in zip(qs, ks, vs, dos):
        args += [q, do, k, k, v, v]
    nat = pl.BlockSpec((SUPER, 128), lambda hp, sb: (jnp.minimum(sb, last), hp))
    nat_before = pl.BlockSpec((SUPER, 128), lambda hp, sb: (jnp.clip(sb - 1, 0, last), hp))
    out = jax.ShapeDtypeStruct((S, ATTN_WIDTH), bf16)
    half = chip_sum.shape[1]
    return pl.pallas_call(
        body, name="attn_bwd", grid=(ATTN_WIDTH // 128, n_steps + 1),
        in_specs=_attn_in_specs(S, True) + [nat, nat, ANY], out_specs=[nat, nat_before, nat_before, ANY],
        out_shape=[out, out, out, _Scatter.out_shape(half, chip_sum.dtype)],
        scratch_shapes=[pltpu.VMEM((SUPER, 128), f32), pltpu.VMEM((2, SUPER, 128), f32),
                        pltpu.VMEM((2, SUPER, 128), f32)] + _Scatter.scratch(half),
        compiler_params=_params("arbitrary", "arbitrary"),
    )(*args, lse, delta, chip_sum)


def _rms(v):
    return lax.rsqrt(jnp.mean(v * v, axis=-1, keepdims=True) + EPS)


def _out_proj(pool_out, attn_out, w_out, x, g2, g3):
    S = x.shape[0]
    ts = 512

    def body(p_ref, a_ref, w_ref, x_ref, g2_ref, g3_ref, mix_ref, x2_ref, h2_ref):
        mix = _dot(p_ref[...], w_ref[:POOL_WIDTH, :]) + _dot(a_ref[...], w_ref[POOL_WIDTH:, :])
        mix_ref[...] = mix
        x2 = x_ref[...] + (mix * _rms(mix)) * g2_ref[...]
        x2_ref[...] = x2
        h2_ref[...] = ((x2 * _rms(x2)) * g3_ref[...]).astype(bf16)

    row = lambda w: pl.BlockSpec((ts, w), lambda i: (i, 0))
    gain = pl.BlockSpec((1, D_MODEL), lambda i: (0, 0))
    return pl.pallas_call(
        body, name="out_proj", grid=(S // ts,),
        in_specs=[row(POOL_WIDTH), row(ATTN_WIDTH), pl.BlockSpec((D_MODEL, D_MODEL), lambda i: (0, 0)),
                  row(D_MODEL), gain, gain],
        out_specs=[row(D_MODEL)] * 3,
        out_shape=[jax.ShapeDtypeStruct((S, D_MODEL), f32), jax.ShapeDtypeStruct((S, D_MODEL), f32),
                   jax.ShapeDtypeStruct((S, D_MODEL), bf16)],
        compiler_params=_params("parallel"),
    )(pool_out, attn_out, w_out, x, g2, g3)


FF_TILE = 256
FF_ROWS = 256


def _sigmoid(g):
    return 1.0 / (1.0 + jnp.exp(-g))


def _ffn_fwd(h2, w_gate, w_up, w_down):
    S = h2.shape[0]
    ts = 1024

    def body(h_ref, wg_ref, wu_ref, wd_ref, gate_ref, up_ref, f_ref):
        def rows_pass(first):
            def sub(i, carry):
                rows = pl.ds(pl.multiple_of(i * FF_ROWS, FF_ROWS), FF_ROWS)
                h = h_ref[rows, :]
                gate = _dot_nt(h, wg_ref[...])
                up = _dot_nt(h, wu_ref[...])
                gate_ref[rows, :] = gate.astype(bf16)
                up_ref[rows, :] = up.astype(bf16)
                part = _dot((gate * _sigmoid(gate) * up).astype(bf16), wd_ref[...])
                if first:
                    f_ref[rows, :] = part
                else:
                    f_ref[rows, :] += part
                return carry

            lax.fori_loop(0, ts // FF_ROWS, sub, 0, unroll=True)

        @pl.when(pl.program_id(1) == 0)
        def _():
            rows_pass(True)

        @pl.when(pl.program_id(1) > 0)
        def _():
            rows_pass(False)

    act = pl.BlockSpec((ts, FF_TILE), lambda i, j: (i, j))
    return pl.pallas_call(
        body, name="ffn_fwd", grid=(S // ts, D_FF // FF_TILE),
        in_specs=[pl.BlockSpec((ts, D_MODEL), lambda i, j: (i, 0)),
                  pl.BlockSpec((FF_TILE, D_MODEL), lambda i, j: (j, 0)),
                  pl.BlockSpec((FF_TILE, D_MODEL), lambda i, j: (j, 0)),
                  pl.BlockSpec((FF_TILE, D_MODEL), lambda i, j: (j, 0))],
        out_specs=[act, act, pl.BlockSpec((ts, D_MODEL), lambda i, j: (i, 0))],
        out_shape=[jax.ShapeDtypeStruct((S, D_FF), bf16), jax.ShapeDtypeStruct((S, D_FF), bf16),
                   jax.ShapeDtypeStruct((S, D_MODEL), f32)],
        compiler_params=_params("parallel", "arbitrary"),
    )(h2, w_gate, w_up, w_down)


def _loss_head(f, x2, target, g4):
    S = f.shape[0]
    ts = 512

    def body(f_ref, x2_ref, t_ref, g_ref, dy_ref, df_ref, dg_ref, loss_ref):
        @pl.when(pl.program_id(0) == 0)
        def _():
            dg_ref[...] = jnp.zeros_like(dg_ref)
            loss_ref[...] = jnp.zeros_like(loss_ref)

        fv = f_ref[...]
        g = g_ref[...]
        r = _rms(fv)
        fhat = fv * r
        err = (x2_ref[...] + fhat * g) - t_ref[...]
        loss_ref[...] += 0.5 * jnp.sum(jnp.mean(err * err, axis=-1, keepdims=True), axis=0, keepdims=True)
        dy = err * (1.0 / D_MODEL)
        dy_ref[...] = dy
        dg_ref[...] += jnp.sum(dy * fhat, axis=0, keepdims=True)
        dyg = dy * g
        df_ref[...] = (r * (dyg - fhat * jnp.mean(dyg * fhat, axis=-1, keepdims=True))).astype(bf16)

    row = pl.BlockSpec((ts, D_MODEL), lambda i: (i, 0))
    gain = pl.BlockSpec((1, D_MODEL), lambda i: (0, 0))
    return pl.pallas_call(
        body, name="loss_head", grid=(S // ts,), in_specs=[row, row, row, gain],
        out_specs=[row, row, gain, pl.BlockSpec((1, 1), lambda i: (0, 0))],
        out_shape=[jax.ShapeDtypeStruct((S, D_MODEL), f32), jax.ShapeDtypeStruct((S, D_MODEL), bf16),
                   jax.ShapeDtypeStruct((1, D_MODEL), f32), jax.ShapeDtypeStruct((1, 1), f32)],
        compiler_params=_params("arbitrary"),
    )(f, x2, target, g4)


def _ffn_bwd(df, gate, up, w_gate, w_up, w_down):
    S = df.shape[0]
    ts = 1024

    def body(df_ref, gate_ref, up_ref, wg_ref, wu_ref, wd_ref, a_ref, dgate_ref, dup_ref, dh_ref):
        def rows_pass(first):
            def sub(i, carry):
                rows = pl.ds(pl.multiple_of(i * FF_ROWS, FF_ROWS), FF_ROWS)
                da = _dot_nt(df_ref[rows, :], wd_ref[...])
                g = gate_ref[rows, :].astype(f32)
                u = up_ref[rows, :].astype(f32)
                sig = _sigmoid(g)
                silu = g * sig
                a_ref[rows, :] = (silu * u).astype(bf16)
                dup = (da * silu).astype(bf16)
                dgate = (da * u * (sig * (1.0 + g * (1.0 - sig)))).astype(bf16)
                dup_ref[rows, :] = dup
                dgate_ref[rows, :] = dgate
                part = _dot(dgate, wg_ref[...]) + _dot(dup, wu_ref[...])
                if first:
                    dh_ref[rows, :] = part
                else:
                    dh_ref[rows, :] += part
                return carry

            lax.fori_loop(0, ts // FF_ROWS, sub, 0, unroll=True)

        @pl.when(pl.program_id(1) == 0)
        def _():
            rows_pass(True)

        @pl.when(pl.program_id(1) > 0)
        def _():
            rows_pass(False)

    act = pl.BlockSpec((ts, FF_TILE), lambda i, j: (i, j))
    row = pl.BlockSpec((ts, D_MODEL), lambda i, j: (i, 0))
    return pl.pallas_call(
        body, name="ffn_bwd", grid=(S // ts, D_FF // FF_TILE),
        in_specs=[row, act, act,
                  pl.BlockSpec((FF_TILE, D_MODEL), lambda i, j: (j, 0)),
                  pl.BlockSpec((FF_TILE, D_MODEL), lambda i, j: (j, 0)),
                  pl.BlockSpec((FF_TILE, D_MODEL), lambda i, j: (j, 0))],
        out_specs=[act, act, act, row],
        out_shape=[jax.ShapeDtypeStruct((S, D_FF), bf16)] * 3 + [jax.ShapeDtypeStruct((S, D_MODEL), f32)],
        compiler_params=_params("parallel", "arbitrary"),
    )(df, gate, up, w_gate, w_up, w_down)


def _norm_bwd(dh2, dy, x2, mix, g3, g2):
    S = dh2.shape[0]
    ts = 512

    def body(dh_ref, dy_ref, x2_ref, mix_ref, g3_ref, g2_ref, dx2_ref, dmix_ref, dg3_ref, dg2_ref):
        @pl.when(pl.program_id(0) == 0)
        def _():
            dg3_ref[...] = jnp.zeros_like(dg3_ref)
            dg2_ref[...] = jnp.zeros_like(dg2_ref)

        dh = dh_ref[...]
        x2 = x2_ref[...]
        r3 = _rms(x2)
        xhat = x2 * r3
        dg3_ref[...] += jnp.sum(dh * xhat, axis=0, keepdims=True)
        dhg = dh * g3_ref[...]
        dx2 = dy_ref[...] + r3 * (dhg - xhat * jnp.mean(dhg * xhat, axis=-1, keepdims=True))
        dx2_ref[...] = dx2
        mix = mix_ref[...]
        r2 = _rms(mix)
        mhat = mix * r2
        dg2_ref[...] += jnp.sum(dx2 * mhat, axis=0, keepdims=True)
        dmg = dx2 * g2_ref[...]
        dmix_ref[...] = (r2 * (dmg - mhat * jnp.mean(dmg * mhat, axis=-1, keepdims=True))).astype(bf16)

    row = pl.BlockSpec((ts, D_MODEL), lambda i: (i, 0))
    gain = pl.BlockSpec((1, D_MODEL), lambda i: (0, 0))
    return pl.pallas_call(
        body, name="norm_bwd", grid=(S // ts,), in_specs=[row, row, row, row, gain, gain],
        out_specs=[row, row, gain, gain],
        out_shape=[jax.ShapeDtypeStruct((S, D_MODEL), f32), jax.ShapeDtypeStruct((S, D_MODEL), bf16),
                   jax.ShapeDtypeStruct((1, D_MODEL), f32), jax.ShapeDtypeStruct((1, D_MODEL), f32)],
        compiler_params=_params("arbitrary"),
    )(dh2, dy, x2, mix, g3, g2)


def _out_proj_bwd(dmix, w_out, attn_out, head_ones):
    S = dmix.shape[0]
    ts = 512

    def body(dm_ref, w_ref, o_ref, ones_ref, dp_ref, dl_ref, *rest):
        do_refs, stage = rest[:-N_STAGE], rest[-N_STAGE:]
        dcat = _dot_nt(dm_ref[...], w_ref[...])
        dp_ref[...] = dcat[:, :POOL_WIDTH]
        do = dcat[:, POOL_WIDTH:]
        for j in range(ATTN_WIDTH // 128):
            stage[j][...] = do[:, j * 128:(j + 1) * 128]
        _store_streams(stage, do_refs, ts)
        prod = do * o_ref[...].astype(f32)
        hi = prod.astype(bf16)
        lo = (prod - hi.astype(f32)).astype(bf16)
        dl_ref[...] = _dot(hi, ones_ref[...]) + _dot(lo, ones_ref[...])

    row = lambda w: pl.BlockSpec((ts, w), lambda i: (i, 0))
    res = pl.pallas_call(
        body, name="out_proj_bwd", grid=(S // ts,),
        in_specs=[row(D_MODEL), pl.BlockSpec((D_MODEL, D_MODEL), lambda i: (0, 0)), row(ATTN_WIDTH),
                  pl.BlockSpec((ATTN_WIDTH, ATTN_WIDTH), lambda i: (0, 0))],
        out_specs=[row(POOL_WIDTH), row(ATTN_WIDTH)] + [_stream_spec(d, ts) for d in DILATIONS],
        out_shape=[jax.ShapeDtypeStruct((S, POOL_WIDTH), f32), jax.ShapeDtypeStruct((S, ATTN_WIDTH), f32)]
        + [_stream_shape(S, d) for d in DILATIONS],
        scratch_shapes=_stage_scratch(ts),
        compiler_params=_params("parallel"),
    )(dmix, w_out, attn_out, head_ones)
    return res[0], res[1], res[2:]


def _in_proj_bwd(du, dq, dk, dv, cos_t, sin_t, w_in, x, dx2, g1):
    S = x.shape[0]
    ts = 256

    def body(du_ref, dq_ref, dk_ref, dv_ref, cos_ref, sin_ref, w_ref, x_ref, dx2_ref, g_ref, gx_ref, dproj_ref, dg_ref):
        @pl.when(pl.program_id(0) == 0)
        def _():
            dg_ref[...] = jnp.zeros_like(dg_ref)

        dproj_ref[:, :POOL_WIDTH] = du_ref[...]
        cos = cos_ref[...]
        sin = sin_ref[...]
        first = _first_half_mask(ts)
        for j in range(ATTN_WIDTH // 128):
            cols = slice(j * 128, (j + 1) * 128)
            for base, ref in ((POOL_WIDTH, dq_ref), (POOL_WIDTH + ATTN_WIDTH, dk_ref)):
                g = ref[:, cols].astype(f32)
                pre = g * cos + _rope_partner(g * sin, first)
                dproj_ref[:, base + j * 128: base + (j + 1) * 128] = pre.astype(bf16)
        dproj_ref[:, POOL_WIDTH + 2 * ATTN_WIDTH:] = dv_ref[...]

        dh = _dot(dproj_ref[...], w_ref[...])
        xv = x_ref[...]
        r = _rms(xv)
        xhat = xv * r
        dg_ref[...] += jnp.sum(dh * xhat, axis=0, keepdims=True)
        dhg = dh * g_ref[...]
        gx_ref[...] = dx2_ref[...] + r * (dhg - xhat * jnp.mean(dhg * xhat, axis=-1, keepdims=True))

    row = lambda w: pl.BlockSpec((ts, w), lambda i: (i, 0))
    gain = pl.BlockSpec((1, D_MODEL), lambda i: (0, 0))
    return pl.pallas_call(
        body, name="in_proj_bwd", grid=(S // ts,),
        in_specs=[row(POOL_WIDTH)] + [row(ATTN_WIDTH)] * 3 + [row(128), row(128),
                  pl.BlockSpec((IN_WIDTH, D_MODEL), lambda i: (0, 0)), row(D_MODEL), row(D_MODEL), gain],
        out_specs=[row(D_MODEL), row(IN_WIDTH), gain],
        out_shape=[jax.ShapeDtypeStruct((S, D_MODEL), f32), jax.ShapeDtypeStruct((S, IN_WIDTH), bf16),
                   jax.ShapeDtypeStruct((1, D_MODEL), f32)],
        compiler_params=_params("arbitrary"),
    )(du, dq, dk, dv, cos_t, sin_t, w_in, x, dx2, g1)


def _matmul_tn(a, b, tn, name):
    K, M = a.shape
    N = b.shape[1]
    tk = 1024

    def body(a_ref, b_ref, o_ref):
        part = _dot_tn(a_ref[...], b_ref[...])

        @pl.when(pl.program_id(1) == 0)
        def _():
            o_ref[...] = part

        @pl.when(pl.program_id(1) > 0)
        def _():
            o_ref[...] += part

    return pl.pallas_call(
        body, name=name, grid=(N // tn, K // tk),
        in_specs=[pl.BlockSpec((tk, M), lambda n, k: (k, 0)), pl.BlockSpec((tk, tn), lambda n, k: (k, n))],
        out_specs=pl.BlockSpec((M, tn), lambda n, k: (0, n)),
        out_shape=jax.ShapeDtypeStruct((M, N), f32),
        compiler_params=_params("parallel", "arbitrary"),
    )(a, b)


def _rope_tables(S):
    half = HEAD_DIM // 2
    freqs = ROPE_THETA ** (-jnp.arange(half, dtype=f32) * (2.0 / HEAD_DIM))
    ang = jnp.arange(S).astype(f32)[:, None] * freqs[None, :]
    cos = jnp.tile(jnp.cos(ang), (1, 4))
    sin = jnp.sin(ang)
    sin = jnp.tile(jnp.concatenate([-sin, sin], axis=1), (1, 2))
    return cos, sin


def _block_diag(w_pool):
    w = jnp.zeros((POOL_WIDTH, POOL_WIDTH), w_pool.dtype)
    for g in range(POOL_WIDTH // POOL_GROUP):
        w = lax.dynamic_update_slice(w, w_pool[g], (g * POOL_GROUP, g * POOL_GROUP))
    return w


def _head_ones():
    head = np.arange(ATTN_WIDTH) // HEAD_DIM
    return jnp.asarray(head[:, None] == head[None, :], dtype=bf16)


def _place():
    x, y, c = lax.axis_index("x"), lax.axis_index("y"), lax.axis_index("c")
    chips = [(1 - x, y), (x, 1 - y), (1 - x, 1 - y)]
    return x, y, c, chips


ANY = pl.BlockSpec(memory_space=pl.ANY)
N_PEER_CHIPS = N_CHIPS - 1
ICI_PIECES = 4
D2D_PIECES = 8
LOCAL_PIECES = 8


def _row_chunks(rows, n, unit=32):
    units = rows // unit
    out, start = [], 0
    for i in range(n):
        size = (units // n + (1 if i < units % n else 0)) * unit
        out.append((start, size))
        start += size
    return [piece for piece in out if piece[1]]


class _LocalCopy:
    def __init__(self, src_rows, dst_rows, rows, buf, sems_in, sems_out):
        self.loads, self.stores = [], []
        for i, (start, size) in enumerate(_row_chunks(rows, LOCAL_PIECES)):
            r = pl.ds(start, size)
            self.loads.append(pltpu.make_async_copy(src_rows(r), buf.at[r], sems_in.at[i]))
            self.stores.append(pltpu.make_async_copy(buf.at[r], dst_rows(r), sems_out.at[i]))

    def start(self):
        for cp in self.loads:
            cp.start()

    def pass_on(self):
        for load, store in zip(self.loads, self.stores):
            load.wait()
            store.start()

    def finish(self):
        for store in self.stores:
            store.wait()

    @staticmethod
    def scratch(rows, dtype):
        return [pltpu.VMEM((rows, D_MODEL), dtype), pltpu.SemaphoreType.DMA((LOCAL_PIECES,)),
                pltpu.SemaphoreType.DMA((LOCAL_PIECES,))]


class _Gather:
    def __init__(self, w_ref, out_ref, send1, recv1, send2, recv2, buf, sems_in, sems_out):
        x, y, c, chips = _place()
        me = 2 * x + y
        rows = w_ref.shape[0]
        half = rows // 2
        pieces = _row_chunks(half, ICI_PIECES)
        self.own = _LocalCopy(lambda r: w_ref.at[r], lambda r: out_ref.at[me, r], rows, buf, sems_in, sems_out)

        def rows_of(core, piece):
            start, size = piece
            return pl.ds(core * half + start, size)

        self.sends, self.arrivals, self.forwards, self.forward_arrivals = [], [], [], []
        for i, piece in enumerate(pieces):
            for j, (cx, cy) in enumerate(chips):
                k = j * len(pieces) + i
                there = 2 * cx + cy

                def direct(src_chip, cx=cx, cy=cy, k=k, piece=piece):
                    return pltpu.make_async_remote_copy(
                        src_ref=w_ref.at[rows_of(c, piece)], dst_ref=out_ref.at[src_chip, rows_of(c, piece)],
                        send_sem=send1.at[k], recv_sem=recv1.at[k], device_id=(cx, cy, c), device_id_type=MESH)

                def passed(core, there=there, k=k, piece=piece):
                    return pltpu.make_async_remote_copy(
                        src_ref=out_ref.at[there, rows_of(core, piece)], dst_ref=out_ref.at[there, rows_of(core, piece)],
                        send_sem=send2.at[k], recv_sem=recv2.at[k], device_id=(x, y, 1 - c), device_id_type=MESH)

                self.sends.append(direct(me))
                self.arrivals.append(direct(there))
                self.forwards.append(passed(c))
                self.forward_arrivals.append(passed(1 - c))

    def start(self):
        for cp in self.sends:
            cp.start()
        self.own.start()

    def pass_on(self):
        self.own.pass_on()
        for arrival, forward in zip(self.arrivals, self.forwards):
            arrival.wait_recv()
            forward.start()

    def finish(self):
        for arrival in self.forward_arrivals:
            arrival.wait_recv()
        for cp in self.sends + self.forwards:
            cp.wait_send()
        self.own.finish()

    @staticmethod
    def scratch(rows, dtype):
        n = N_PEER_CHIPS * len(_row_chunks(rows // 2, ICI_PIECES))
        return [pltpu.SemaphoreType.DMA((n,))] * 4 + _LocalCopy.scratch(rows, dtype)

    @staticmethod
    def out_shape(rows, dtype):
        return jax.ShapeDtypeStruct((N_CHIPS, rows, D_MODEL), dtype)


def _gather_weights(pack):
    rows = pack.shape[0]

    def body(w_ref, out_ref, *scratch):
        gather = _Gather(w_ref, out_ref, *scratch)
        gather.start()
        gather.pass_on()
        gather.finish()

    return pl.pallas_call(
        body, name="gather_weights", in_specs=[ANY], out_specs=ANY, out_shape=_Gather.out_shape(rows, pack.dtype),
        scratch_shapes=_Gather.scratch(rows, pack.dtype),
        compiler_params=pltpu.CompilerParams(vmem_limit_bytes=VMEM_LIMIT_V7X),
    )(pack)


class _Scatter:
    def __init__(self, h_ref, out_ref, send, recv):
        x, y, c, chips = _place()
        pieces = _row_chunks(h_ref.shape[1], ICI_PIECES)
        self.copies = []
        for i, (start, size) in enumerate(pieces):
            for j, (cx, cy) in enumerate(chips):
                k = j * len(pieces) + i
                self.copies.append(pltpu.make_async_remote_copy(
                    src_ref=h_ref.at[2 * cx + cy, pl.ds(start, size)], dst_ref=out_ref.at[j, pl.ds(start, size)],
                    send_sem=send.at[k], recv_sem=recv.at[k], device_id=(cx, cy, c), device_id_type=MESH))

    def start(self):
        for cp in self.copies:
            cp.start()

    def finish(self):
        for cp in self.copies:
            cp.wait_recv()
        for cp in self.copies:
            cp.wait_send()

    @staticmethod
    def scratch(half):
        n = N_PEER_CHIPS * len(_row_chunks(half, ICI_PIECES))
        return [pltpu.SemaphoreType.DMA((n,))] * 2

    @staticmethod
    def out_shape(half, dtype):
        return jax.ShapeDtypeStruct((N_PEER_CHIPS, half, D_MODEL), dtype)


def _scatter_to_chips(h):
    half = h.shape[1]

    def body(h_ref, out_ref, send, recv):
        scatter = _Scatter(h_ref, out_ref, send, recv)
        scatter.start()
        scatter.finish()

    return pl.pallas_call(
        body, name="scatter_to_chips", in_specs=[ANY], out_specs=ANY, out_shape=_Scatter.out_shape(half, h.dtype),
        scratch_shapes=_Scatter.scratch(half),
    )(h)


def _swap_halves(g, name):
    half = g.shape[1] // 2
    pieces = _row_chunks(half, D2D_PIECES)
    n = len(pieces)

    def body(g_ref, theirs_ref, send, recv):
        x, y, c, _ = _place()
        copies = []
        for s in range(N_CHIPS):
            for i, (start, size) in enumerate(pieces):
                copies.append(pltpu.make_async_remote_copy(
                    src_ref=g_ref.at[s, pl.ds((1 - c) * half + start, size)], dst_ref=theirs_ref.at[s, pl.ds(start, size)],
                    send_sem=send.at[s * n + i], recv_sem=recv.at[s * n + i],
                    device_id=(x, y, 1 - c), device_id_type=MESH))
        for cp in copies:
            cp.start()
        for cp in copies:
            cp.wait()

    return pl.pallas_call(
        body, name=name, in_specs=[ANY], out_specs=ANY,
        out_shape=jax.ShapeDtypeStruct((N_CHIPS, half, D_MODEL), g.dtype),
        scratch_shapes=[pltpu.SemaphoreType.DMA((N_CHIPS * n,))] * 2,
    )(g)


ADD_TILE_MAX_ROWS = 600


def _add_tile(half):
    return max(t for t in range(8, ADD_TILE_MAX_ROWS + 1, 8) if half % t == 0)


def _add_cores(g, theirs, name):
    half = theirs.shape[1]
    tr = _add_tile(half)
    n_t = half // tr

    def body(c_ref, g_ref, t_ref, o_ref):
        o_ref[...] = g_ref[...] + t_ref[...]

    blk = pl.BlockSpec((1, tr, D_MODEL), lambda s, t, c_ref: (s, t, 0))
    return pl.pallas_call(
        body, name=name,
        grid_spec=pltpu.PrefetchScalarGridSpec(
            num_scalar_prefetch=1, grid=(N_CHIPS, n_t),
            in_specs=[pl.BlockSpec((1, tr, D_MODEL), lambda s, t, c_ref: (s, c_ref[0] * n_t + t, 0)), blk],
            out_specs=blk),
        out_shape=jax.ShapeDtypeStruct(theirs.shape, theirs.dtype),
        compiler_params=_params("parallel", "parallel"),
    )(lax.axis_index("c").astype(jnp.int32).reshape(1), g, theirs)


def _add_chips(chip_sum, others, name):
    half = chip_sum.shape[1]
    tr = _add_tile(half)

    def body(me_ref, own_ref, o0, o1, o2, out_ref):
        out_ref[...] = ((own_ref[0] + o0[0]) + o1[0]) + o2[0]

    other = lambda j: pl.BlockSpec((1, tr, D_MODEL), lambda t, me_ref: (j, t, 0))
    return pl.pallas_call(
        body, name=name,
        grid_spec=pltpu.PrefetchScalarGridSpec(
            num_scalar_prefetch=1, grid=(half // tr,),
            in_specs=[pl.BlockSpec((1, tr, D_MODEL), lambda t, me_ref: (me_ref[0], t, 0)), other(0), other(1), other(2)],
            out_specs=pl.BlockSpec((tr, D_MODEL), lambda t, me_ref: (t, 0))),
        out_shape=jax.ShapeDtypeStruct((half, D_MODEL), chip_sum.dtype),
        compiler_params=_params("parallel"),
    )((2 * lax.axis_index("x") + lax.axis_index("y")).astype(jnp.int32).reshape(1), chip_sum, others, others, others)


def _join_halves(r):
    half = r.shape[0]
    pieces = _row_chunks(half, 2 * D2D_PIECES)
    n = len(pieces)

    def body(r_ref, out_ref, send, recv, buf, sems_in, sems_out):
        x, y, c, _ = _place()
        own = _LocalCopy(lambda rr: r_ref.at[rr], lambda rr: out_ref.at[c, rr], half, buf, sems_in, sems_out)
        own.start()

        def piece(i, core):
            start, size = pieces[i]
            return pltpu.make_async_remote_copy(
                src_ref=r_ref.at[pl.ds(start, size)], dst_ref=out_ref.at[core, pl.ds(start, size)],
                send_sem=send.at[i], recv_sem=recv.at[i], device_id=(x, y, 1 - c), device_id_type=MESH)

        copies = [piece(i, c) for i in range(n)]
        for cp in copies:
            cp.start()
        own.pass_on()
        for i in range(n):
            piece(i, 1 - c).wait_recv()
        for cp in copies:
            cp.wait_send()
        own.finish()

    return pl.pallas_call(
        body, name="join_halves", in_specs=[ANY], out_specs=ANY,
        out_shape=jax.ShapeDtypeStruct((2,) + r.shape, r.dtype),
        scratch_shapes=[pltpu.SemaphoreType.DMA((n,))] * 2 + _LocalCopy.scratch(half, r.dtype),
        compiler_params=pltpu.CompilerParams(vmem_limit_bytes=VMEM_LIMIT_V7X),
    )(r)


def _sum_small(block):
    def body(b_ref, out_ref, gathered, send, recv):
        x, y, c, _ = _place()
        me = 4 * x + 2 * y + c
        gathered[me] = b_ref[...]
        sends = []
        for kk in range(1, N_DEV):
            flip = lambda v, bit: 1 - v if bit else v
            peer = (flip(x, kk & 4), flip(y, kk & 2), flip(c, kk & 1))
            cp = pltpu.make_async_remote_copy(
                src_ref=b_ref, dst_ref=gathered.at[me], send_sem=send.at[kk - 1], recv_sem=recv.at[kk - 1],
                device_id=peer, device_id_type=MESH)
            cp.start()
            sends.append(cp)
        for kk in range(1, N_DEV):
            peer_index = jnp.bitwise_xor(me, kk)
            pltpu.make_async_remote_copy(
                src_ref=b_ref, dst_ref=gathered.at[peer_index], send_sem=send.at[kk - 1], recv_sem=recv.at[kk - 1],
                device_id=(x, y, c), device_id_type=MESH).wait_recv()
        for cp in sends:
            cp.wait_send()
        acc = gathered[0]
        for dev in range(1, N_DEV):
            acc = acc + gathered[dev]
        out_ref[...] = acc

    vmem = pl.BlockSpec(memory_space=pltpu.VMEM)
    return pl.pallas_call(
        body, name="sum_small", in_specs=[vmem], out_specs=vmem,
        out_shape=jax.ShapeDtypeStruct(block.shape, block.dtype),
        scratch_shapes=[pltpu.VMEM((N_DEV,) + block.shape, block.dtype),
                        pltpu.SemaphoreType.DMA((N_DEV - 1,)), pltpu.SemaphoreType.DMA((N_DEV - 1,))],
    )(block)


def _adamw(w, g, m, v, name):
    rows, cols = w.shape
    tr = rows
    for cand in (512, 256, 128, 64, 32, 16, 8):
        if rows % cand == 0:
            tr = cand
            break
    c1 = 1.0 - ADAM_B1 ** ADAM_STEP
    c2 = 1.0 - ADAM_B2 ** ADAM_STEP

    def body(w_ref, g_ref, m_ref, v_ref, d_ref, nm_ref, nv_ref):
        gv = g_ref[...]
        nm = ADAM_B1 * m_ref[...] + (1.0 - ADAM_B1) * gv
        nv = ADAM_B2 * v_ref[...] + (1.0 - ADAM_B2) * (gv * gv)
        nm_ref[...] = nm
        nv_ref[...] = nv
        d_ref[...] = -ADAM_LR * ((nm / c1) / (jnp.sqrt(nv / c2) + ADAM_EPS) + ADAM_WD * w_ref[...])

    blk = pl.BlockSpec((tr, cols), lambda i: (i, 0))
    shape = jax.ShapeDtypeStruct((rows, cols), f32)
    return pl.pallas_call(
        body, name=name, grid=(rows // tr,), in_specs=[blk] * 4, out_specs=[blk] * 3, out_shape=[shape] * 3,
        compiler_params=_params("parallel"),
    )(w, g, m, v)


LARGE = ("w_in", "w_out", "w_gate", "w_up", "w_down")
SMALL = ("ln_pre_mix", "ln_post_mix", "ln_pre_ffn", "ln_post_ffn", "pool_scale", "w_pool")
SHARD_ROWS = {"w_in": 640, "w_out": 256, "w_gate": 704, "w_up": 704, "w_down": 704}
COLUMN_SHARDED = ("w_in", "w_gate", "w_up")
NEEDED_FIRST = ("w_in",)
NEEDED_LATER = ("w_out", "w_gate", "w_up", "w_down")
READY_EARLY = ("w_out", "w_gate", "w_up", "w_down")
READY_LATE = ("w_in",)


def _pack_shard(shards, names):
    return jnp.concatenate([shards[n].T if n in COLUMN_SHARDED else shards[n] for n in names], axis=0)


def _unpack_shard(pack, names):
    out, row = {}, 0
    for n in names:
        part = pack[row:row + SHARD_ROWS[n]]
        out[n] = part.T if n in COLUMN_SHARDED else part
        row += SHARD_ROWS[n]
    return out


def _whole_from_shards(packs, names):
    out, row = {}, 0
    for n in names:
        rows = SHARD_ROWS[n]
        out[n] = packs[:, row:row + rows].reshape(N_CHIPS * rows, D_MODEL)
        row += rows
    return out


def _shards_from_whole(grads, names):
    return jnp.concatenate([grads[n].reshape(N_CHIPS, SHARD_ROWS[n], D_MODEL) for n in names], axis=1)


def _pack_small(vals):
    rows = [vals[n].reshape(1, D_MODEL) for n in SMALL[:4]]
    rows.append(jnp.pad(vals["pool_scale"].reshape(1, POOL_WIDTH), ((0, 0), (0, D_MODEL - POOL_WIDTH))))
    rows.append(jnp.pad(vals["loss"].reshape(1, 1), ((0, 0), (0, D_MODEL - 1))))
    rows.append(jnp.zeros((2, D_MODEL), f32))
    rows.append(vals["w_pool"].reshape(16, D_MODEL))
    return jnp.concatenate(rows, axis=0)


def _unpack_small(block):
    out = {n: block[i:i + 1] for i, n in enumerate(SMALL[:4])}
    out["pool_scale"] = block[4:5, :POOL_WIDTH]
    out["loss"] = block[5, 0]
    out["w_pool"] = block[8:24].reshape(1, 4, POOL_GROUP, POOL_GROUP)
    return out


def kernel(x, ln_pre_mix, w_in, w_pool, pool_scale, w_out, ln_post_mix, ln_pre_ffn, w_gate, w_up, w_down, ln_post_ffn, loss_target, m_ln_pre_mix, m_w_in, m_w_pool, m_pool_scale, m_w_out, m_ln_post_mix, m_ln_pre_ffn, m_w_gate, m_w_up, m_w_down, m_ln_post_ffn, v_ln_pre_mix, v_w_in, v_w_pool, v_pool_scale, v_w_out, v_ln_post_mix, v_ln_pre_ffn, v_w_gate, v_w_up, v_w_down, v_ln_post_ffn):
    w = dict(ln_pre_mix=ln_pre_mix, w_in=w_in, w_pool=w_pool, pool_scale=pool_scale, w_out=w_out,
             ln_post_mix=ln_post_mix, ln_pre_ffn=ln_pre_ffn, w_gate=w_gate, w_up=w_up, w_down=w_down,
             ln_post_ffn=ln_post_ffn)
    m = dict(ln_pre_mix=m_ln_pre_mix, w_in=m_w_in, w_pool=m_w_pool, pool_scale=m_pool_scale, w_out=m_w_out,
             ln_post_mix=m_ln_post_mix, ln_pre_ffn=m_ln_pre_ffn, w_gate=m_w_gate, w_up=m_w_up, w_down=m_w_down,
             ln_post_ffn=m_ln_post_ffn)
    v = dict(ln_pre_mix=v_ln_pre_mix, w_in=v_w_in, w_pool=v_w_pool, pool_scale=v_pool_scale, w_out=v_w_out,
             ln_post_mix=v_ln_post_mix, ln_pre_ffn=v_ln_pre_ffn, w_gate=v_w_gate, w_up=v_w_up, w_down=v_w_down,
             ln_post_ffn=v_ln_post_ffn)

    xs, target = x[0], loss_target[0]
    cos_t, sin_t = _rope_tables(xs.shape[0])
    w_bd = _block_diag(w_pool[0]).astype(bf16)
    shard = {n: w[n][0].astype(bf16) for n in LARGE}

    w_in_whole = _whole_from_shards(_gather_weights(_pack_shard(shard, NEEDED_FIRST)), NEEDED_FIRST)["w_in"]
    h1, u, qs, ks, vs = _in_proj(xs, ln_pre_mix, w_in_whole, cos_t, sin_t)
    pool_out = _pool_fwd(u, w_bd, pool_scale)
    attn_out, lse, later = _attn_fwd(qs, ks, vs, _pack_shard(shard, NEEDED_LATER))
    whole = _whole_from_shards(later, NEEDED_LATER)
    mix, x2, h2 = _out_proj(pool_out, attn_out, whole["w_out"], xs, ln_post_mix, ln_pre_ffn)
    gate, up, f = _ffn_fwd(h2, whole["w_gate"], whole["w_up"], whole["w_down"])
    dy, df, dg4, loss = _loss_head(f, x2, target, ln_post_ffn)

    large = {}
    a, dgate, dup, dh2 = _ffn_bwd(df, gate, up, whole["w_gate"], whole["w_up"], whole["w_down"])
    large["w_down"] = _matmul_tn(a, df, D_MODEL, "grad_w_down")
    large["w_gate"] = _matmul_tn(dgate, h2, D_MODEL, "grad_w_gate")
    large["w_up"] = _matmul_tn(dup, h2, D_MODEL, "grad_w_up")
    dx2, dmix, dg3, dg2 = _norm_bwd(dh2, dy, x2, mix, ln_pre_ffn, ln_post_mix)
    large["w_out"] = jnp.concatenate([_matmul_tn(pool_out, dmix, D_MODEL, "grad_w_out_pool"),
                                      _matmul_tn(attn_out, dmix, D_MODEL, "grad_w_out_attn")], axis=0)
    early = _shards_from_whole(large, READY_EARLY)
    early_chip = _add_cores(early, _swap_halves(early, "swap_halves_early"), "add_cores_early")
    dpool, delta, dos = _out_proj_bwd(dmix, whole["w_out"], attn_out, _head_ones())
    du, d_w_bd, d_scale = _pool_bwd(u, dpool, w_bd, pool_scale)
    dq, dk, dv, early_others = _attn_bwd(qs, ks, vs, dos, lse, delta, early_chip)
    grad_x, dproj, dg1 = _in_proj_bwd(du, dq, dk, dv, cos_t, sin_t, w_in_whole, xs, dx2, ln_pre_mix)
    large["w_in"] = _matmul_tn(dproj, h1, D_MODEL, "grad_w_in")
    late = _shards_from_whole(large, READY_LATE)
    late_chip = _add_cores(late, _swap_halves(late, "swap_halves_late"), "add_cores_late")
    late_others = _scatter_to_chips(late_chip)
    early_half = _add_chips(early_chip, early_others, "add_chips_early")
    late_half = _add_chips(late_chip, late_others, "add_chips_late")
    joined = _join_halves(jnp.concatenate([early_half, late_half], axis=0))
    n_early = early_half.shape[0]
    grads = _unpack_shard(joined[:, :n_early].reshape(-1, D_MODEL), READY_EARLY)
    grads.update(_unpack_shard(joined[:, n_early:].reshape(-1, D_MODEL), READY_LATE))

    d_w_pool = jnp.stack([d_w_bd[g * POOL_GROUP:(g + 1) * POOL_GROUP, g * POOL_GROUP:(g + 1) * POOL_GROUP]
                          for g in range(POOL_WIDTH // POOL_GROUP)])
    small = dict(ln_pre_mix=dg1, ln_post_mix=dg2, ln_pre_ffn=dg3, ln_post_ffn=dg4, pool_scale=d_scale, w_pool=d_w_pool)
    total = _unpack_small(_sum_small(_pack_small(dict(small, loss=loss))))
    for n in SMALL:
        grads[n] = total[n]

    delta_w, new_m, new_v = {}, {}, {}
    for n in LARGE:
        delta_w[n], new_m[n], new_v[n] = _adamw(w[n][0], grads[n], m[n][0], v[n][0], "adamw_" + n)
    small_state = [_pack_small(dict({n: s[n] for n in SMALL}, loss=jnp.zeros((), f32))) for s in (w, m, v)]
    small_grad = _pack_small(dict({n: grads[n] for n in SMALL}, loss=jnp.zeros((), f32)))
    sd, sm, sv = _adamw(small_state[0], small_grad, small_state[1], small_state[2], "adamw_small")
    for out, block in ((delta_w, sd), (new_m, sm), (new_v, sv)):
        un = _unpack_small(block)
        for n in SMALL:
            out[n] = un[n]

    names = ("ln_pre_mix", "w_in", "w_pool", "pool_scale", "w_out", "ln_post_mix", "ln_pre_ffn", "w_gate", "w_up",
             "w_down", "ln_post_ffn")
    full = lambda d: [d[n].reshape(w[n].shape) for n in names]
    return (total["loss"], grad_x[None], *full(grads), *full(delta_w), *full(new_m), *full(new_v))
```

```python
import numpy as np
import jax
import jax.numpy as jnp
from jax import lax
from jax.experimental import pallas as pl
from jax.experimental.pallas import tpu as pltpu

D_MODEL = 1024
POOL_WIDTH = 256
POOL_GROUP = 64
ATTN_WIDTH = 768
HEAD_DIM = 64
IN_WIDTH = 2560
D_FF = 2816
BLOCK = 128
DILATIONS = (1, 4, 16)
ROPE_THETA = 10000.0
EPS = 1e-6
ATTN_SCALE = 0.125
NEG = -1e30

ADAM_LR = 0.001
ADAM_B1 = 0.9
ADAM_B2 = 0.999
ADAM_EPS = 1e-08
ADAM_WD = 0.01
ADAM_STEP = 10

N_CHIPS = 4
N_DEV = 8
VMEM_LIMIT_V7X = 56 * 1024 * 1024
MESH = pl.DeviceIdType.MESH

f32 = jnp.float32
bf16 = jnp.bfloat16


def _params(*sem):
    return pltpu.CompilerParams(dimension_semantics=sem, vmem_limit_bytes=VMEM_LIMIT_V7X)


def _dot(a, b):
    return jnp.dot(a, b, preferred_element_type=f32)


def _dot_nt(a, b):
    return lax.dot_general(a, b, (((1,), (1,)), ((), ())), preferred_element_type=f32)


def _dot_tn(a, b):
    return lax.dot_general(a, b, (((0,), (0,)), ((), ())), preferred_element_type=f32)


def _rope_partner(a, first_half):
    return jnp.where(first_half, pltpu.roll(a, 96, 1), pltpu.roll(a, 32, 1))


def _first_half_mask(rows):
    lane = lax.broadcasted_iota(jnp.int32, (rows, 128), 1)
    return (lane % HEAD_DIM) < (HEAD_DIM // 2)


def _stream_spec(d, ts):
    return pl.BlockSpec((d, ts // d, ATTN_WIDTH), lambda i: (0, i, 0))


def _stream_shape(S, d):
    return jax.ShapeDtypeStruct((d, S // d, ATTN_WIDTH), bf16)


N_STAGE = ATTN_WIDTH // 128


def _stage_scratch(ts):
    return [pltpu.VMEM((ts, 128), f32)] * N_STAGE


def _store_streams(stage, out_refs, ts):
    for d, ref in zip(DILATIONS, out_refs):
        for r in range(d):
            rows = pl.ds(0, ts) if d == 1 else pl.ds(r, ts // d, stride=d)
            for j in range(N_STAGE):
                ref[r, :, j * 128:(j + 1) * 128] = stage[j][rows, :].astype(bf16)


def _in_proj(x, g1, w_in, cos_t, sin_t):
    S = x.shape[0]
    ts = 512

    def body(x_ref, g_ref, w_ref, cos_ref, sin_ref, h_ref, u_ref, *rest):
        outs, stage = rest[:-N_STAGE], rest[-N_STAGE:]
        xv = x_ref[...]
        r = lax.rsqrt(jnp.mean(xv * xv, axis=-1, keepdims=True) + EPS)
        h = ((xv * r) * g_ref[...]).astype(bf16)
        h_ref[...] = h
        proj = _dot_nt(h, w_ref[...])
        u_ref[...] = proj[:, :POOL_WIDTH]
        cos = cos_ref[...]
        sin = sin_ref[...]
        first = _first_half_mask(ts)
        n_dil = len(DILATIONS)
        for which, base in enumerate((POOL_WIDTH, POOL_WIDTH + ATTN_WIDTH)):
            for j in range(ATTN_WIDTH // 128):
                a = proj[:, base + j * 128: base + (j + 1) * 128]
                stage[j][...] = a * cos + _rope_partner(a, first) * sin
            _store_streams(stage, outs[which * n_dil:(which + 1) * n_dil], ts)
        for j in range(ATTN_WIDTH // 128):
            base = POOL_WIDTH + 2 * ATTN_WIDTH + j * 128
            stage[j][...] = proj[:, base:base + 128]
        _store_streams(stage, outs[2 * n_dil:], ts)

    row = lambda w: pl.BlockSpec((ts, w), lambda i: (i, 0))
    streams = [_stream_spec(d, ts) for d in DILATIONS] * 3
    res = pl.pallas_call(
        body, name="in_proj", grid=(S // ts,),
        in_specs=[row(D_MODEL), pl.BlockSpec((1, D_MODEL), lambda i: (0, 0)),
                  pl.BlockSpec((IN_WIDTH, D_MODEL), lambda i: (0, 0)), row(128), row(128)],
        out_specs=[row(D_MODEL), row(POOL_WIDTH)] + streams,
        out_shape=[jax.ShapeDtypeStruct((S, D_MODEL), bf16), jax.ShapeDtypeStruct((S, POOL_WIDTH), f32)]
        + [_stream_shape(S, d) for d in DILATIONS] * 3,
        scratch_shapes=_stage_scratch(ts),
        compiler_params=_params("parallel"),
    )(x, g1, w_in, cos_t, sin_t)
    n = len(DILATIONS)
    return res[0], res[1], res[2:2 + n], res[2 + n:2 + 2 * n], res[2 + 2 * n:]


POOL_HALO = 16


def _pool_lane_group(rows):
    return lax.broadcasted_iota(jnp.int32, (rows, POOL_WIDTH), 1) // POOL_GROUP


def _pool_select(group, s2, s4, s8, s16):
    return jnp.where(group == 0, s2, jnp.where(group == 1, s4, jnp.where(group == 2, s8, s16)))


def _pool_count(t0, rows):
    group = _pool_lane_group(rows)
    t = t0 + lax.broadcasted_iota(jnp.int32, (rows, POOL_WIDTH), 0)
    win = _pool_select(group, 2, 4, 8, 16)
    return jnp.minimum(t + 1, win).astype(f32)


def _pool_diff(u_halo, u_tile, t0):
    ts = u_tile.shape[0]
    ext = jnp.concatenate([u_halo, u_tile], axis=0)
    s2 = ext + pltpu.roll(ext, 1, 0)
    s4 = s2 + pltpu.roll(s2, 2, 0)
    s8 = s4 + pltpu.roll(s4, 4, 0)
    s16 = s8 + pltpu.roll(s8, 8, 0)
    group = _pool_lane_group(ts + POOL_HALO)
    wsum = _pool_select(group, s2, s4, s8, s16)[POOL_HALO:]
    return wsum / _pool_count(t0, ts) - u_tile


def _pool_specs(ts, n_tiles):
    tile = pl.BlockSpec((ts, POOL_WIDTH), lambda i: (i, 0))
    per = ts // POOL_HALO
    before = pl.BlockSpec((POOL_HALO, POOL_WIDTH), lambda i: (jnp.maximum(i * per - 1, 0), 0))
    after = pl.BlockSpec((POOL_HALO, POOL_WIDTH), lambda i: (jnp.minimum((i + 1) * per, n_tiles * per - 1), 0))
    return tile, before, after


def _pool_fwd(u, w_bd, scale):
    S = u.shape[0]
    ts = 512
    n_tiles = S // ts

    def body(u_ref, halo_ref, w_ref, sc_ref, y_ref):
        i = pl.program_id(0)
        halo = jnp.where(i > 0, halo_ref[...], 0.0)
        d = _pool_diff(halo, u_ref[...], i * ts)
        y_ref[...] = (_dot(d.astype(bf16), w_ref[...]) * sc_ref[...]).astype(bf16)

    tile, before, _ = _pool_specs(ts, n_tiles)
    return pl.pallas_call(
        body, name="pool_fwd", grid=(n_tiles,),
        in_specs=[tile, before, pl.BlockSpec((POOL_WIDTH, POOL_WIDTH), lambda i: (0, 0)),
                  pl.BlockSpec((1, POOL_WIDTH), lambda i: (0, 0))],
        out_specs=tile, out_shape=jax.ShapeDtypeStruct((S, POOL_WIDTH), bf16),
        compiler_params=_params("parallel"),
    )(u, u, w_bd, scale)


def _pool_bwd(u, dy, w_bd, scale):
    S = u.shape[0]
    ts = 512
    n_tiles = S // ts

    def body(u_ref, halo_ref, dy_ref, dy_next_ref, w_ref, sc_ref, du_ref, dw_ref, dsc_ref):
        i = pl.program_id(0)

        @pl.when(i == 0)
        def _():
            dw_ref[...] = jnp.zeros_like(dw_ref)
            dsc_ref[...] = jnp.zeros_like(dsc_ref)

        halo = jnp.where(i > 0, halo_ref[...], 0.0)
        d = _pool_diff(halo, u_ref[...], i * ts).astype(bf16)
        w = w_ref[...]
        sc = sc_ref[...]
        dy_tile = dy_ref[...]
        z = _dot(d, w)
        dsc_ref[...] += jnp.sum(dy_tile * z, axis=0, keepdims=True)
        dy_next = jnp.where(i < n_tiles - 1, dy_next_ref[...], 0.0)
        dz = (jnp.concatenate([dy_tile, dy_next], axis=0) * sc).astype(bf16)
        dw_ref[...] += _dot_tn(d, dz[:ts])
        dd = _dot_nt(dz, w)
        e = dd / _pool_count(i * ts, ts + POOL_HALO)
        n = ts + POOL_HALO
        f2 = e + pltpu.roll(e, n - 1, 0)
        f4 = f2 + pltpu.roll(f2, n - 2, 0)
        f8 = f4 + pltpu.roll(f4, n - 4, 0)
        f16 = f8 + pltpu.roll(f8, n - 8, 0)
        fsum = _pool_select(_pool_lane_group(n), f2, f4, f8, f16)
        du_ref[...] = (fsum[:ts] - dd[:ts]).astype(bf16)

    tile, before, after = _pool_specs(ts, n_tiles)
    return pl.pallas_call(
        body, name="pool_bwd", grid=(n_tiles,),
        in_specs=[tile, before, tile, after, pl.BlockSpec((POOL_WIDTH, POOL_WIDTH), lambda i: (0, 0)),
                  pl.BlockSpec((1, POOL_WIDTH), lambda i: (0, 0))],
        out_specs=[tile, pl.BlockSpec((POOL_WIDTH, POOL_WIDTH), lambda i: (0, 0)),
                   pl.BlockSpec((1, POOL_WIDTH), lambda i: (0, 0))],
        out_shape=[jax.ShapeDtypeStruct((S, POOL_WIDTH), bf16), jax.ShapeDtypeStruct((POOL_WIDTH, POOL_WIDTH), f32),
                   jax.ShapeDtypeStruct((1, POOL_WIDTH), f32)],
        compiler_params=_params("arbitrary"),
    )(u, u, dy, dy, w_bd, scale)


SUPER = BLOCK * DILATIONS[-1]
UNITS = SUPER // BLOCK
FWD_UNROLL = 16
BWD_UNROLL = 8


def _band_mask(has_prev):
    qi = lax.broadcasted_iota(jnp.int32, (BLOCK, 2 * BLOCK), 0)
    kj = lax.broadcasted_iota(jnp.int32, (BLOCK, 2 * BLOCK), 1)
    return (kj >= qi) & (kj <= qi + BLOCK) & ((kj >= BLOCK) | has_prev)


def _head0_mask(rows=BLOCK):
    return lax.broadcasted_iota(jnp.int32, (rows, 128), 1) < HEAD_DIM


def _band_mask_t(has_prev):
    ki = lax.broadcasted_iota(jnp.int32, (2 * BLOCK, 2 * BLOCK), 0)
    qj = lax.broadcasted_iota(jnp.int32, (2 * BLOCK, 2 * BLOCK), 1) % BLOCK
    return (ki >= qj) & (ki <= qj + BLOCK) & ((ki >= BLOCK) | has_prev)


def _head_pair_rows(a, h0):
    zero = jnp.zeros_like(a)
    return jnp.concatenate([jnp.where(h0, a, zero), jnp.where(h0, zero, a)], axis=0)


def _per_query_row(stat):
    t = stat.T
    return jnp.concatenate([jnp.concatenate([t[:HEAD_DIM]] * 4, axis=0), jnp.concatenate([t[HEAD_DIM:]] * 4, axis=0)],
                           axis=1)


def _natural_rows(d, r, n):
    if d == 1:
        return pl.ds(pl.multiple_of(n * BLOCK, BLOCK), BLOCK)
    return pl.ds(n * (BLOCK * d) + r, BLOCK, stride=d)


def _unit_place(d, u):
    per_stream = UNITS // d
    return u // per_stream, u % per_stream, per_stream


def _block_rows(n):
    return pl.ds(pl.multiple_of(n * BLOCK, BLOCK), BLOCK)


def _band(cur_ref, tail_ref, r, n):
    before = jnp.where(n > 0, cur_ref[r, _block_rows(jnp.maximum(n - 1, 0)), :], tail_ref[r])
    return jnp.concatenate([before, cur_ref[r, _block_rows(n), :]], axis=0)


def _attn_in_specs(S, with_do):
    specs = []
    last = S // SUPER - 1
    for d in DILATIONS:
        per_stream = UNITS // d
        cur = pl.BlockSpec((d, SUPER // d, 128), lambda hp, sb: (0, jnp.minimum(sb, last), hp))
        tail = pl.BlockSpec(
            (d, BLOCK, 128),
            lambda hp, sb, per_stream=per_stream: (0, jnp.maximum(jnp.minimum(sb, last) * per_stream - 1, 0), hp))
        specs += [cur] * (2 if with_do else 1) + [cur, tail, cur, tail]
    return specs


def _attn_fwd(qs, ks, vs, pack):
    S = qs[0].shape[1]
    n_dil = len(DILATIONS)
    n_steps = S // SUPER
    n_total = (ATTN_WIDTH // 128) * n_steps

    def body(*refs):
        ins, pack_ref = refs[:5 * n_dil], refs[5 * n_dil]
        out_ref, lse_ref, gathered_ref = refs[5 * n_dil + 1:5 * n_dil + 4]
        scratch = refs[5 * n_dil + 4:]
        o_sc, l_sc = scratch[:n_dil], scratch[n_dil:2 * n_dil]
        gather = _Gather(pack_ref, gathered_ref, *scratch[2 * n_dil:])
        sb = pl.program_id(1)
        step = pl.program_id(0) * n_steps + sb

        @pl.when(step == 0)
        def _():
            gather.start()

        h0 = _head0_mask()
        for ci, d in enumerate(DILATIONS):
            q_ref, kc_ref, kp_ref, vc_ref, vp_ref = ins[5 * ci:5 * ci + 5]

            def unit(u, carry, d=d, ci=ci, q_ref=q_ref, kc_ref=kc_ref, kp_ref=kp_ref, vc_ref=vc_ref, vp_ref=vp_ref):
                r, n, _ = _unit_place(d, u)
                qv = q_ref[r, _block_rows(n), :]
                kb = _band(kc_ref, kp_ref, r, n)
                vb = _band(vc_ref, vp_ref, r, n)
                valid = _band_mask((sb > 0) | (n > 0))
                outs, lses = [], []
                for h in range(2):
                    keep = h0 if h == 0 else jnp.logical_not(h0)
                    qh = jnp.where(keep, qv, jnp.zeros_like(qv))
                    s = jnp.where(valid, _dot_nt(qh, kb) * ATTN_SCALE, NEG)
                    m = jnp.max(s, axis=1, keepdims=True)
                    e = jnp.exp(s - m)
                    den = jnp.sum(e, axis=1, keepdims=True)
                    outs.append(_dot((e / den).astype(bf16), vb))
                    lses.append(jnp.broadcast_to(m + jnp.log(den), (BLOCK, 128)))
                rows = _natural_rows(d, r, n)
                o_sc[ci][rows, :] = jnp.where(h0, outs[0], outs[1])
                l_sc[ci][rows, :] = jnp.where(h0, lses[0], lses[1])
                return carry

            lax.fori_loop(0, UNITS, unit, 0, unroll=FWD_UNROLL)

        def merge(t, carry):
            rows = pl.ds(pl.multiple_of(t * 256, 256), 256)
            a, b, c = l_sc[0][rows, :], l_sc[1][rows, :], l_sc[2][rows, :]
            m = jnp.maximum(jnp.maximum(a, b), c)
            ea, eb, ec = jnp.exp(a - m), jnp.exp(b - m), jnp.exp(c - m)
            tot = ea + eb + ec
            out_ref[rows, :] = ((ea / tot) * o_sc[0][rows, :] + (eb / tot) * o_sc[1][rows, :]
                                + (ec / tot) * o_sc[2][rows, :]).astype(bf16)
            lse_ref[rows, :] = m + jnp.log(tot)
            return carry

        lax.fori_loop(0, SUPER // 256, merge, 0)

        @pl.when(step == n_total // 2)
        def _():
            gather.pass_on()

        @pl.when(step == n_total - 1)
        def _():
            gather.finish()

    args = []
    for q, k, v in zip(qs, ks, vs):
        args += [q, k, k, v, v]
    nat = pl.BlockSpec((SUPER, 128), lambda hp, sb: (sb, hp))
    rows = pack.shape[0]
    return pl.pallas_call(
        body, name="attn_fwd", grid=(ATTN_WIDTH // 128, n_steps),
        in_specs=_attn_in_specs(S, False) + [ANY], out_specs=[nat, nat, ANY],
        out_shape=[jax.ShapeDtypeStruct((S, ATTN_WIDTH), bf16), jax.ShapeDtypeStruct((S, ATTN_WIDTH), f32),
                   _Gather.out_shape(rows, pack.dtype)],
        scratch_shapes=[pltpu.VMEM((SUPER, 128), f32)] * (2 * n_dil) + _Gather.scratch(rows, pack.dtype),
        compiler_params=_params("arbitrary", "arbitrary"),
    )(*args, pack)


def _attn_bwd(qs, ks, vs, dos, lse, delta, chip_sum):
    S = qs[0].shape[1]
    n_steps = S // SUPER
    last = n_steps - 1
    n_dil = len(DILATIONS)
    n_total = (ATTN_WIDTH // 128) * (n_steps + 1)

    def body(*refs):
        ins, (lse_ref, dl_ref, sum_ref) = refs[:6 * n_dil], refs[6 * n_dil:6 * n_dil + 3]
        dq_ref, dk_ref, dv_ref, others_ref = refs[6 * n_dil + 3:6 * n_dil + 7]
        dq_acc, dk_acc, dv_acc = refs[6 * n_dil + 7:6 * n_dil + 10]
        scatter = _Scatter(sum_ref, others_ref, *refs[6 * n_dil + 10:])
        sb = pl.program_id(1)
        step = pl.program_id(0) * (n_steps + 1) + sb
        cur = sb % 2
        prv = 1 - cur

        @pl.when(step == 0)
        def _():
            scatter.start()

        @pl.when(sb < n_steps)
        def _():
            dq_acc[...] = jnp.zeros_like(dq_acc)
            dk_acc[cur] = jnp.zeros((SUPER, 128), f32)
            dv_acc[cur] = jnp.zeros((SUPER, 128), f32)
            h0 = _head0_mask()
            for ci, d in enumerate(DILATIONS):
                q_ref, do_ref, kc_ref, kp_ref, vc_ref, vp_ref = ins[6 * ci:6 * ci + 6]

                def unit(u, carry, d=d, q_ref=q_ref, do_ref=do_ref, kc_ref=kc_ref, kp_ref=kp_ref, vc_ref=vc_ref,
                         vp_ref=vp_ref):
                    r, n, per_stream = _unit_place(d, u)
                    qv = q_ref[r, _block_rows(n), :]
                    dov = do_ref[r, _block_rows(n), :]
                    kb = _band(kc_ref, kp_ref, r, n)
                    vb = _band(vc_ref, vp_ref, r, n)
                    rows = _natural_rows(d, r, n)
                    has_prev = (sb > 0) | (n > 0)
                    q_pair = _head_pair_rows(qv, h0)
                    do_pair = _head_pair_rows(dov, h0)
                    s_t = jnp.where(_band_mask_t(has_prev), _dot_nt(kb, q_pair) * ATTN_SCALE, NEG)
                    p_t = jnp.exp(s_t - _per_query_row(lse_ref[rows, :]))
                    dp_t = _dot_nt(vb, do_pair)
                    ds_t = (p_t * (dp_t - _per_query_row(dl_ref[rows, :])) * ATTN_SCALE).astype(bf16)
                    dvb = _dot(p_t.astype(bf16), do_pair)
                    dkb = _dot(ds_t, q_pair)
                    dq_pair = _dot_tn(ds_t, kb)
                    dq_acc[rows, :] += jnp.where(h0, dq_pair[:BLOCK], dq_pair[BLOCK:])
                    dk_acc[cur, rows, :] += dkb[BLOCK:]
                    dv_acc[cur, rows, :] += dvb[BLOCK:]

                    slot = jnp.where((n > 0) | (sb == 0), cur, prv)
                    before = _natural_rows(d, r, jnp.where(n > 0, n - 1, per_stream - 1))
                    dk_acc[slot, before, :] += dkb[:BLOCK]
                    dv_acc[slot, before, :] += dvb[:BLOCK]
                    return carry

                lax.fori_loop(0, UNITS, unit, 0, unroll=BWD_UNROLL)
            dq_ref[...] = dq_acc[...].astype(bf16)

        @pl.when(sb > 0)
        def _():
            dk_ref[...] = dk_acc[prv].astype(bf16)
            dv_ref[...] = dv_acc[prv].astype(bf16)

        @pl.when(step == n_total - 1)
        def _():
            scatter.finish()

    args = []
    for q, k, v, do in zip(qs, ks, vs, dos):
        args += [q, do, k, k, v, v]
    nat = pl.BlockSpec((SUPER, 128), lambda hp, sb: (jnp.minimum(sb, last), hp))
    nat_before = pl.BlockSpec((SUPER, 128), lambda hp, sb: (jnp.clip(sb - 1, 0, last), hp))
    out = jax.ShapeDtypeStruct((S, ATTN_WIDTH), bf16)
    half = chip_sum.shape[1]
    return pl.pallas_call(
        body, name="attn_bwd", grid=(ATTN_WIDTH // 128, n_steps + 1),
        in_specs=_attn_in_specs(S, True) + [nat, nat, ANY], out_specs=[nat, nat_before, nat_before, ANY],
        out_shape=[out, out, out, _Scatter.out_shape(half, chip_sum.dtype)],
        scratch_shapes=[pltpu.VMEM((SUPER, 128), f32), pltpu.VMEM((2, SUPER, 128), f32),
                        pltpu.VMEM((2, SUPER, 128), f32)] + _Scatter.scratch(half),
        compiler_params=_params("arbitrary", "arbitrary"),
    )(*args, lse, delta, chip_sum)


def _rms(v):
    return lax.rsqrt(jnp.mean(v * v, axis=-1, keepdims=True) + EPS)


def _out_proj(pool_out, attn_out, w_out, x, g2, g3):
    S = x.shape[0]
    ts = 512

    def body(p_ref, a_ref, w_ref, x_ref, g2_ref, g3_ref, mix_ref, x2_ref, h2_ref):
        mix = _dot(p_ref[...], w_ref[:POOL_WIDTH, :]) + _dot(a_ref[...], w_ref[POOL_WIDTH:, :])
        mix_ref[...] = mix
        x2 = x_ref[...] + (mix * _rms(mix)) * g2_ref[...]
        x2_ref[...] = x2
        h2_ref[...] = ((x2 * _rms(x2)) * g3_ref[...]).astype(bf16)

    row = lambda w: pl.BlockSpec((ts, w), lambda i: (i, 0))
    gain = pl.BlockSpec((1, D_MODEL), lambda i: (0, 0))
    return pl.pallas_call(
        body, name="out_proj", grid=(S // ts,),
        in_specs=[row(POOL_WIDTH), row(ATTN_WIDTH), pl.BlockSpec((D_MODEL, D_MODEL), lambda i: (0, 0)),
                  row(D_MODEL), gain, gain],
        out_specs=[row(D_MODEL)] * 3,
        out_shape=[jax.ShapeDtypeStruct((S, D_MODEL), f32), jax.ShapeDtypeStruct((S, D_MODEL), f32),
                   jax.ShapeDtypeStruct((S, D_MODEL), bf16)],
        compiler_params=_params("parallel"),
    )(pool_out, attn_out, w_out, x, g2, g3)


FF_TILE = 256
FF_ROWS = 256


def _sigmoid(g):
    return 1.0 / (1.0 + jnp.exp(-g))


def _ff_act_shape(S):
    return jax.ShapeDtypeStruct((D_FF // FF_TILE, S, FF_TILE), bf16)


def _ff_act_spec(ts):
    return pl.BlockSpec((1, ts, FF_TILE), lambda i, j: (j, i, 0))


def _ffn_fwd(h2, w_gate, w_up, w_down):
    S = h2.shape[0]
    ts = 1024

    def body(h_ref, wg_ref, wu_ref, wd_ref, gate_ref, up_ref, f_ref):
        def rows_pass(first):
            def sub(i, carry):
                rows = pl.ds(pl.multiple_of(i * FF_ROWS, FF_ROWS), FF_ROWS)
                h = h_ref[rows, :]
                gate = _dot_nt(h, wg_ref[...])
                up = _dot_nt(h, wu_ref[...])
                gate_ref[0, rows, :] = gate.astype(bf16)
                up_ref[0, rows, :] = up.astype(bf16)
                part = _dot((gate * _sigmoid(gate) * up).astype(bf16), wd_ref[...])
                if first:
                    f_ref[rows, :] = part
                else:
                    f_ref[rows, :] += part
                return carry

            lax.fori_loop(0, ts // FF_ROWS, sub, 0, unroll=True)

        @pl.when(pl.program_id(1) == 0)
        def _():
            rows_pass(True)

        @pl.when(pl.program_id(1) > 0)
        def _():
            rows_pass(False)

    act = _ff_act_spec(ts)
    return pl.pallas_call(
        body, name="ffn_fwd", grid=(S // ts, D_FF // FF_TILE),
        in_specs=[pl.BlockSpec((ts, D_MODEL), lambda i, j: (i, 0)),
                  pl.BlockSpec((FF_TILE, D_MODEL), lambda i, j: (j, 0)),
                  pl.BlockSpec((FF_TILE, D_MODEL), lambda i, j: (j, 0)),
                  pl.BlockSpec((FF_TILE, D_MODEL), lambda i, j: (j, 0))],
        out_specs=[act, act, pl.BlockSpec((ts, D_MODEL), lambda i, j: (i, 0))],
        out_shape=[_ff_act_shape(S), _ff_act_shape(S), jax.ShapeDtypeStruct((S, D_MODEL), f32)],
        compiler_params=_params("parallel", "arbitrary"),
    )(h2, w_gate, w_up, w_down)


def _loss_head(f, x2, target, g4):
    S = f.shape[0]
    ts = 512

    def body(f_ref, x2_ref, t_ref, g_ref, dy_ref, df_ref, dg_ref, loss_ref):
        @pl.when(pl.program_id(0) == 0)
        def _():
            dg_ref[...] = jnp.zeros_like(dg_ref)
            loss_ref[...] = jnp.zeros_like(loss_ref)

        fv = f_ref[...]
        g = g_ref[...]
        r = _rms(fv)
        fhat = fv * r
        err = (x2_ref[...] + fhat * g) - t_ref[...]
        loss_ref[...] += 0.5 * jnp.sum(jnp.mean(err * err, axis=-1, keepdims=True), axis=0, keepdims=True)
        dy = err * (1.0 / D_MODEL)
        dy_ref[...] = dy
        dg_ref[...] += jnp.sum(dy * fhat, axis=0, keepdims=True)
        dyg = dy * g
        df_ref[...] = (r * (dyg - fhat * jnp.mean(dyg * fhat, axis=-1, keepdims=True))).astype(bf16)

    row = pl.BlockSpec((ts, D_MODEL), lambda i: (i, 0))
    gain = pl.BlockSpec((1, D_MODEL), lambda i: (0, 0))
    return pl.pallas_call(
        body, name="loss_head", grid=(S // ts,), in_specs=[row, row, row, gain],
        out_specs=[row, row, gain, pl.BlockSpec((1, 1), lambda i: (0, 0))],
        out_shape=[jax.ShapeDtypeStruct((S, D_MODEL), f32), jax.ShapeDtypeStruct((S, D_MODEL), bf16),
                   jax.ShapeDtypeStruct((1, D_MODEL), f32), jax.ShapeDtypeStruct((1, 1), f32)],
        compiler_params=_params("arbitrary"),
    )(f, x2, target, g4)


def _ffn_bwd(df, gate, up, w_gate, w_up, w_down):
    S = df.shape[0]
    ts = 1024

    def body(df_ref, gate_ref, up_ref, wg_ref, wu_ref, wd_ref, a_ref, dgate_ref, dup_ref, dh_ref):
        def rows_pass(first):
            def sub(i, carry):
                rows = pl.ds(pl.multiple_of(i * FF_ROWS, FF_ROWS), FF_ROWS)
                da = _dot_nt(df_ref[rows, :], wd_ref[...])
                g = gate_ref[0, rows, :].astype(f32)
                u = up_ref[0, rows, :].astype(f32)
                sig = _sigmoid(g)
                silu = g * sig
                a_ref[0, rows, :] = (silu * u).astype(bf16)
                dup = (da * silu).astype(bf16)
                dgate = (da * u * (sig * (1.0 + g * (1.0 - sig)))).astype(bf16)
                dup_ref[0, rows, :] = dup
                dgate_ref[0, rows, :] = dgate
                part = _dot(dgate, wg_ref[...]) + _dot(dup, wu_ref[...])
                if first:
                    dh_ref[rows, :] = part
                else:
                    dh_ref[rows, :] += part
                return carry

            lax.fori_loop(0, ts // FF_ROWS, sub, 0, unroll=True)

        @pl.when(pl.program_id(1) == 0)
        def _():
            rows_pass(True)

        @pl.when(pl.program_id(1) > 0)
        def _():
            rows_pass(False)

    act = _ff_act_spec(ts)
    row = pl.BlockSpec((ts, D_MODEL), lambda i, j: (i, 0))
    return pl.pallas_call(
        body, name="ffn_bwd", grid=(S // ts, D_FF // FF_TILE),
        in_specs=[row, act, act,
                  pl.BlockSpec((FF_TILE, D_MODEL), lambda i, j: (j, 0)),
                  pl.BlockSpec((FF_TILE, D_MODEL), lambda i, j: (j, 0)),
                  pl.BlockSpec((FF_TILE, D_MODEL), lambda i, j: (j, 0))],
        out_specs=[act, act, act, row],
        out_shape=[_ff_act_shape(S)] * 3 + [jax.ShapeDtypeStruct((S, D_MODEL), f32)],
        compiler_params=_params("parallel", "arbitrary"),
    )(df, gate, up, w_gate, w_up, w_down)


def _norm_bwd(dh2, dy, x2, mix, g3, g2):
    S = dh2.shape[0]
    ts = 512

    def body(dh_ref, dy_ref, x2_ref, mix_ref, g3_ref, g2_ref, dx2_ref, dmix_ref, dg3_ref, dg2_ref):
        @pl.when(pl.program_id(0) == 0)
        def _():
            dg3_ref[...] = jnp.zeros_like(dg3_ref)
            dg2_ref[...] = jnp.zeros_like(dg2_ref)

        dh = dh_ref[...]
        x2 = x2_ref[...]
        r3 = _rms(x2)
        xhat = x2 * r3
        dg3_ref[...] += jnp.sum(dh * xhat, axis=0, keepdims=True)
        dhg = dh * g3_ref[...]
        dx2 = dy_ref[...] + r3 * (dhg - xhat * jnp.mean(dhg * xhat, axis=-1, keepdims=True))
        dx2_ref[...] = dx2
        mix = mix_ref[...]
        r2 = _rms(mix)
        mhat = mix * r2
        dg2_ref[...] += jnp.sum(dx2 * mhat, axis=0, keepdims=True)
        dmg = dx2 * g2_ref[...]
        dmix_ref[...] = (r2 * (dmg - mhat * jnp.mean(dmg * mhat, axis=-1, keepdims=True))).astype(bf16)

    row = pl.BlockSpec((ts, D_MODEL), lambda i: (i, 0))
    gain = pl.BlockSpec((1, D_MODEL), lambda i: (0, 0))
    return pl.pallas_call(
        body, name="norm_bwd", grid=(S // ts,), in_specs=[row, row, row, row, gain, gain],
        out_specs=[row, row, gain, gain],
        out_shape=[jax.ShapeDtypeStruct((S, D_MODEL), f32), jax.ShapeDtypeStruct((S, D_MODEL), bf16),
                   jax.ShapeDtypeStruct((1, D_MODEL), f32), jax.ShapeDtypeStruct((1, D_MODEL), f32)],
        compiler_params=_params("arbitrary"),
    )(dh2, dy, x2, mix, g3, g2)


def _out_proj_bwd(dmix, w_out, attn_out, head_ones):
    S = dmix.shape[0]
    ts = 512

    def body(dm_ref, w_ref, o_ref, ones_ref, dp_ref, dl_ref, *rest):
        do_refs, stage = rest[:-N_STAGE], rest[-N_STAGE:]
        dcat = _dot_nt(dm_ref[...], w_ref[...])
        dp_ref[...] = dcat[:, :POOL_WIDTH]
        do = dcat[:, POOL_WIDTH:]
        for j in range(ATTN_WIDTH // 128):
            stage[j][...] = do[:, j * 128:(j + 1) * 128]
        _store_streams(stage, do_refs, ts)
        prod = do * o_ref[...].astype(f32)
        hi = prod.astype(bf16)
        lo = (prod - hi.astype(f32)).astype(bf16)
        dl_ref[...] = _dot(hi, ones_ref[...]) + _dot(lo, ones_ref[...])

    row = lambda w: pl.BlockSpec((ts, w), lambda i: (i, 0))
    res = pl.pallas_call(
        body, name="out_proj_bwd", grid=(S // ts,),
        in_specs=[row(D_MODEL), pl.BlockSpec((D_MODEL, D_MODEL), lambda i: (0, 0)), row(ATTN_WIDTH),
                  pl.BlockSpec((ATTN_WIDTH, ATTN_WIDTH), lambda i: (0, 0))],
        out_specs=[row(POOL_WIDTH), row(ATTN_WIDTH)] + [_stream_spec(d, ts) for d in DILATIONS],
        out_shape=[jax.ShapeDtypeStruct((S, POOL_WIDTH), f32), jax.ShapeDtypeStruct((S, ATTN_WIDTH), f32)]
        + [_stream_shape(S, d) for d in DILATIONS],
        scratch_shapes=_stage_scratch(ts),
        compiler_params=_params("parallel"),
    )(dmix, w_out, attn_out, head_ones)
    return res[0], res[1], res[2:]


def _in_proj_bwd(du, dq, dk, dv, cos_t, sin_t, w_in, x, dx2, g1):
    S = x.shape[0]
    ts = 256

    def body(du_ref, dq_ref, dk_ref, dv_ref, cos_ref, sin_ref, w_ref, x_ref, dx2_ref, g_ref, gx_ref, dproj_ref, dg_ref):
        @pl.when(pl.program_id(0) == 0)
        def _():
            dg_ref[...] = jnp.zeros_like(dg_ref)

        dproj_ref[:, :POOL_WIDTH] = du_ref[...]
        cos = cos_ref[...]
        sin = sin_ref[...]
        first = _first_half_mask(ts)
        for j in range(ATTN_WIDTH // 128):
            cols = slice(j * 128, (j + 1) * 128)
            for base, ref in ((POOL_WIDTH, dq_ref), (POOL_WIDTH + ATTN_WIDTH, dk_ref)):
                g = ref[:, cols].astype(f32)
                pre = g * cos + _rope_partner(g * sin, first)
                dproj_ref[:, base + j * 128: base + (j + 1) * 128] = pre.astype(bf16)
        dproj_ref[:, POOL_WIDTH + 2 * ATTN_WIDTH:] = dv_ref[...]

        dh = _dot(dproj_ref[...], w_ref[...])
        xv = x_ref[...]
        r = _rms(xv)
        xhat = xv * r
        dg_ref[...] += jnp.sum(dh * xhat, axis=0, keepdims=True)
        dhg = dh * g_ref[...]
        gx_ref[...] = dx2_ref[...] + r * (dhg - xhat * jnp.mean(dhg * xhat, axis=-1, keepdims=True))

    row = lambda w: pl.BlockSpec((ts, w), lambda i: (i, 0))
    gain = pl.BlockSpec((1, D_MODEL), lambda i: (0, 0))
    return pl.pallas_call(
        body, name="in_proj_bwd", grid=(S // ts,),
        in_specs=[row(POOL_WIDTH)] + [row(ATTN_WIDTH)] * 3 + [row(128), row(128),
                  pl.BlockSpec((IN_WIDTH, D_MODEL), lambda i: (0, 0)), row(D_MODEL), row(D_MODEL), gain],
        out_specs=[row(D_MODEL), row(IN_WIDTH), gain],
        out_shape=[jax.ShapeDtypeStruct((S, D_MODEL), f32), jax.ShapeDtypeStruct((S, IN_WIDTH), bf16),
                   jax.ShapeDtypeStruct((1, D_MODEL), f32)],
        compiler_params=_params("arbitrary"),
    )(du, dq, dk, dv, cos_t, sin_t, w_in, x, dx2, g1)


def _matmul_tiles_tn(a, b, name):
    T, K, w = a.shape
    N = b.shape[1]
    tk = 1024

    def body(a_ref, b_ref, o_ref):
        def tiles_pass(first):
            for t in range(T):
                part = _dot_tn(a_ref[t], b_ref[...])
                if first:
                    o_ref[t * w:(t + 1) * w, :] = part
                else:
                    o_ref[t * w:(t + 1) * w, :] += part

        @pl.when(pl.program_id(0) == 0)
        def _():
            tiles_pass(True)

        @pl.when(pl.program_id(0) > 0)
        def _():
            tiles_pass(False)

    return pl.pallas_call(
        body, name=name, grid=(K // tk,),
        in_specs=[pl.BlockSpec((T, tk, w), lambda k: (0, k, 0)), pl.BlockSpec((tk, N), lambda k: (k, 0))],
        out_specs=pl.BlockSpec((T * w, N), lambda k: (0, 0)),
        out_shape=jax.ShapeDtypeStruct((T * w, N), f32),
        compiler_params=_params("arbitrary"),
    )(a, b)


def _matmul_tn(a, b, tn, name):
    K, M = a.shape
    N = b.shape[1]
    tk = 1024

    def body(a_ref, b_ref, o_ref):
        part = _dot_tn(a_ref[...], b_ref[...])

        @pl.when(pl.program_id(1) == 0)
        def _():
            o_ref[...] = part

        @pl.when(pl.program_id(1) > 0)
        def _():
            o_ref[...] += part

    return pl.pallas_call(
        body, name=name, grid=(N // tn, K // tk),
        in_specs=[pl.BlockSpec((tk, M), lambda n, k: (k, 0)), pl.BlockSpec((tk, tn), lambda n, k: (k, n))],
        out_specs=pl.BlockSpec((M, tn), lambda n, k: (0, n)),
        out_shape=jax.ShapeDtypeStruct((M, N), f32),
        compiler_params=_params("parallel", "arbitrary"),
    )(a, b)


def _rope_tables(S):
    half = HEAD_DIM // 2
    freqs = ROPE_THETA ** (-jnp.arange(half, dtype=f32) * (2.0 / HEAD_DIM))
    ang = jnp.arange(S).astype(f32)[:, None] * freqs[None, :]
    cos = jnp.tile(jnp.cos(ang), (1, 4))
    sin = jnp.sin(ang)
    sin = jnp.tile(jnp.concatenate([-sin, sin], axis=1), (1, 2))
    return cos, sin


def _block_diag(w_pool):
    w = jnp.zeros((POOL_WIDTH, POOL_WIDTH), w_pool.dtype)
    for g in range(POOL_WIDTH // POOL_GROUP):
        w = lax.dynamic_update_slice(w, w_pool[g], (g * POOL_GROUP, g * POOL_GROUP))
    return w


def _head_ones():
    head = np.arange(ATTN_WIDTH) // HEAD_DIM
    return jnp.asarray(head[:, None] == head[None, :], dtype=bf16)


def _place():
    x, y, c = lax.axis_index("x"), lax.axis_index("y"), lax.axis_index("c")
    chips = [(1 - x, y), (x, 1 - y), (1 - x, 1 - y)]
    return x, y, c, chips


ANY = pl.BlockSpec(memory_space=pl.ANY)
N_PEER_CHIPS = N_CHIPS - 1
ICI_PIECES = 4
D2D_PIECES = 8
LOCAL_PIECES = 8


def _row_chunks(rows, n, unit=32):
    units = rows // unit
    out, start = [], 0
    for i in range(n):
        size = (units // n + (1 if i < units % n else 0)) * unit
        out.append((start, size))
        start += size
    return [piece for piece in out if piece[1]]


class _LocalCopy:
    def __init__(self, src_rows, dst_rows, rows, buf, sems_in, sems_out):
        self.loads, self.stores = [], []
        for i, (start, size) in enumerate(_row_chunks(rows, LOCAL_PIECES)):
            r = pl.ds(start, size)
            self.loads.append(pltpu.make_async_copy(src_rows(r), buf.at[r], sems_in.at[i]))
            self.stores.append(pltpu.make_async_copy(buf.at[r], dst_rows(r), sems_out.at[i]))

    def start(self):
        for cp in self.loads:
            cp.start()

    def pass_on(self):
        for load, store in zip(self.loads, self.stores):
            load.wait()
            store.start()

    def finish(self):
        for store in self.stores:
            store.wait()

    @staticmethod
    def scratch(rows, dtype):
        return [pltpu.VMEM((rows, D_MODEL), dtype), pltpu.SemaphoreType.DMA((LOCAL_PIECES,)),
                pltpu.SemaphoreType.DMA((LOCAL_PIECES,))]


class _Gather:
    def __init__(self, w_ref, out_ref, send1, recv1, send2, recv2, buf, sems_in, sems_out):
        x, y, c, chips = _place()
        me = 2 * x + y
        rows = w_ref.shape[0]
        half = rows // 2
        pieces = _row_chunks(half, ICI_PIECES)
        self.own = _LocalCopy(lambda r: w_ref.at[r], lambda r: out_ref.at[me, r], rows, buf, sems_in, sems_out)

        def rows_of(core, piece):
            start, size = piece
            return pl.ds(core * half + start, size)

        self.sends, self.arrivals, self.forwards, self.forward_arrivals = [], [], [], []
        for i, piece in enumerate(pieces):
            for j, (cx, cy) in enumerate(chips):
                k = j * len(pieces) + i
                there = 2 * cx + cy

                def direct(src_chip, cx=cx, cy=cy, k=k, piece=piece):
                    return pltpu.make_async_remote_copy(
                        src_ref=w_ref.at[rows_of(c, piece)], dst_ref=out_ref.at[src_chip, rows_of(c, piece)],
                        send_sem=send1.at[k], recv_sem=recv1.at[k], device_id=(cx, cy, c), device_id_type=MESH)

                def passed(core, there=there, k=k, piece=piece):
                    return pltpu.make_async_remote_copy(
                        src_ref=out_ref.at[there, rows_of(core, piece)], dst_ref=out_ref.at[there, rows_of(core, piece)],
                        send_sem=send2.at[k], recv_sem=recv2.at[k], device_id=(x, y, 1 - c), device_id_type=MESH)

                self.sends.append(direct(me))
                self.arrivals.append(direct(there))
                self.forwards.append(passed(c))
                self.forward_arrivals.append(passed(1 - c))

    def start(self):
        for cp in self.sends:
            cp.start()
        self.own.start()

    def pass_on(self):
        self.own.pass_on()
        for arrival, forward in zip(self.arrivals, self.forwards):
            arrival.wait_recv()
            forward.start()

    def finish(self):
        for arrival in self.forward_arrivals:
            arrival.wait_recv()
        for cp in self.sends + self.forwards:
            cp.wait_send()
        self.own.finish()

    @staticmethod
    def scratch(rows, dtype):
        n = N_PEER_CHIPS * len(_row_chunks(rows // 2, ICI_PIECES))
        return [pltpu.SemaphoreType.DMA((n,))] * 4 + _LocalCopy.scratch(rows, dtype)

    @staticmethod
    def out_shape(rows, dtype):
        return jax.ShapeDtypeStruct((N_CHIPS, rows, D_MODEL), dtype)


def _gather_weights(pack):
    rows = pack.shape[0]

    def body(w_ref, out_ref, *scratch):
        gather = _Gather(w_ref, out_ref, *scratch)
        gather.start()
        gather.pass_on()
        gather.finish()

    return pl.pallas_call(
        body, name="gather_weights", in_specs=[ANY], out_specs=ANY, out_shape=_Gather.out_shape(rows, pack.dtype),
        scratch_shapes=_Gather.scratch(rows, pack.dtype),
        compiler_params=pltpu.CompilerParams(vmem_limit_bytes=VMEM_LIMIT_V7X),
    )(pack)


class _Scatter:
    def __init__(self, h_ref, out_ref, send, recv):
        x, y, c, chips = _place()
        pieces = _row_chunks(h_ref.shape[1], ICI_PIECES)
        self.copies = []
        for i, (start, size) in enumerate(pieces):
            for j, (cx, cy) in enumerate(chips):
                k = j * len(pieces) + i
                self.copies.append(pltpu.make_async_remote_copy(
                    src_ref=h_ref.at[2 * cx + cy, pl.ds(start, size)], dst_ref=out_ref.at[j, pl.ds(start, size)],
                    send_sem=send.at[k], recv_sem=recv.at[k], device_id=(cx, cy, c), device_id_type=MESH))

    def start(self):
        for cp in self.copies:
            cp.start()

    def finish(self):
        for cp in self.copies:
            cp.wait_recv()
        for cp in self.copies:
            cp.wait_send()

    @staticmethod
    def scratch(half):
        n = N_PEER_CHIPS * len(_row_chunks(half, ICI_PIECES))
        return [pltpu.SemaphoreType.DMA((n,))] * 2

    @staticmethod
    def out_shape(half, dtype):
        return jax.ShapeDtypeStruct((N_PEER_CHIPS, half, D_MODEL), dtype)


def _scatter_to_chips(h):
    half = h.shape[1]

    def body(h_ref, out_ref, send, recv):
        scatter = _Scatter(h_ref, out_ref, send, recv)
        scatter.start()
        scatter.finish()

    return pl.pallas_call(
        body, name="scatter_to_chips", in_specs=[ANY], out_specs=ANY, out_shape=_Scatter.out_shape(half, h.dtype),
        scratch_shapes=_Scatter.scratch(half),
    )(h)


def _swap_halves(g, name):
    half = g.shape[1] // 2
    pieces = _row_chunks(half, D2D_PIECES)
    n = len(pieces)

    def body(g_ref, theirs_ref, send, recv):
        x, y, c, _ = _place()
        copies = []
        for s in range(N_CHIPS):
            for i, (start, size) in enumerate(pieces):
                copies.append(pltpu.make_async_remote_copy(
                    src_ref=g_ref.at[s, pl.ds((1 - c) * half + start, size)], dst_ref=theirs_ref.at[s, pl.ds(start, size)],
                    send_sem=send.at[s * n + i], recv_sem=recv.at[s * n + i],
                    device_id=(x, y, 1 - c), device_id_type=MESH))
        for cp in copies:
            cp.start()
        for cp in copies:
            cp.wait()

    return pl.pallas_call(
        body, name=name, in_specs=[ANY], out_specs=ANY,
        out_shape=jax.ShapeDtypeStruct((N_CHIPS, half, D_MODEL), g.dtype),
        scratch_shapes=[pltpu.SemaphoreType.DMA((N_CHIPS * n,))] * 2,
    )(g)


ADD_TILE_MAX_ROWS = 600


def _add_tile(half):
    return max(t for t in range(8, ADD_TILE_MAX_ROWS + 1, 8) if half % t == 0)


def _add_cores(g, theirs, name):
    half = theirs.shape[1]
    tr = _add_tile(half)
    n_t = half // tr

    def body(c_ref, g_ref, t_ref, o_ref):
        o_ref[...] = g_ref[...] + t_ref[...]

    blk = pl.BlockSpec((1, tr, D_MODEL), lambda s, t, c_ref: (s, t, 0))
    return pl.pallas_call(
        body, name=name,
        grid_spec=pltpu.PrefetchScalarGridSpec(
            num_scalar_prefetch=1, grid=(N_CHIPS, n_t),
            in_specs=[pl.BlockSpec((1, tr, D_MODEL), lambda s, t, c_ref: (s, c_ref[0] * n_t + t, 0)), blk],
            out_specs=blk),
        out_shape=jax.ShapeDtypeStruct(theirs.shape, theirs.dtype),
        compiler_params=_params("parallel", "parallel"),
    )(lax.axis_index("c").astype(jnp.int32).reshape(1), g, theirs)


def _add_chips(chip_sum, others, name):
    half = chip_sum.shape[1]
    tr = _add_tile(half)

    def body(me_ref, own_ref, o0, o1, o2, out_ref):
        out_ref[...] = ((own_ref[0] + o0[0]) + o1[0]) + o2[0]

    other = lambda j: pl.BlockSpec((1, tr, D_MODEL), lambda t, me_ref: (j, t, 0))
    return pl.pallas_call(
        body, name=name,
        grid_spec=pltpu.PrefetchScalarGridSpec(
            num_scalar_prefetch=1, grid=(half // tr,),
            in_specs=[pl.BlockSpec((1, tr, D_MODEL), lambda t, me_ref: (me_ref[0], t, 0)), other(0), other(1), other(2)],
            out_specs=pl.BlockSpec((tr, D_MODEL), lambda t, me_ref: (t, 0))),
        out_shape=jax.ShapeDtypeStruct((half, D_MODEL), chip_sum.dtype),
        compiler_params=_params("parallel"),
    )((2 * lax.axis_index("x") + lax.axis_index("y")).astype(jnp.int32).reshape(1), chip_sum, others, others, others)


def _join_halves(r):
    half = r.shape[0]
    pieces = _row_chunks(half, 2 * D2D_PIECES)
    n = len(pieces)

    def body(r_ref, out_ref, send, recv, buf, sems_in, sems_out):
        x, y, c, _ = _place()
        own = _LocalCopy(lambda rr: r_ref.at[rr], lambda rr: out_ref.at[c, rr], half, buf, sems_in, sems_out)
        own.start()

        def piece(i, core):
            start, size = pieces[i]
            return pltpu.make_async_remote_copy(
                src_ref=r_ref.at[pl.ds(start, size)], dst_ref=out_ref.at[core, pl.ds(start, size)],
                send_sem=send.at[i], recv_sem=recv.at[i], device_id=(x, y, 1 - c), device_id_type=MESH)

        copies = [piece(i, c) for i in range(n)]
        for cp in copies:
            cp.start()
        own.pass_on()
        for i in range(n):
            piece(i, 1 - c).wait_recv()
        for cp in copies:
            cp.wait_send()
        own.finish()

    return pl.pallas_call(
        body, name="join_halves", in_specs=[ANY], out_specs=ANY,
        out_shape=jax.ShapeDtypeStruct((2,) + r.shape, r.dtype),
        scratch_shapes=[pltpu.SemaphoreType.DMA((n,))] * 2 + _LocalCopy.scratch(half, r.dtype),
        compiler_params=pltpu.CompilerParams(vmem_limit_bytes=VMEM_LIMIT_V7X),
    )(r)


def _sum_small(block):
    def body(b_ref, out_ref, gathered, send, recv):
        x, y, c, _ = _place()
        me = 4 * x + 2 * y + c
        gathered[me] = b_ref[...]
        sends = []
        for kk in range(1, N_DEV):
            flip = lambda v, bit: 1 - v if bit else v
            peer = (flip(x, kk & 4), flip(y, kk & 2), flip(c, kk & 1))
            cp = pltpu.make_async_remote_copy(
                src_ref=b_ref, dst_ref=gathered.at[me], send_sem=send.at[kk - 1], recv_sem=recv.at[kk - 1],
                device_id=peer, device_id_type=MESH)
            cp.start()
            sends.append(cp)
        for kk in range(1, N_DEV):
            peer_index = jnp.bitwise_xor(me, kk)
            pltpu.make_async_remote_copy(
                src_ref=b_ref, dst_ref=gathered.at[peer_index], send_sem=send.at[kk - 1], recv_sem=recv.at[kk - 1],
                device_id=(x, y, c), device_id_type=MESH).wait_recv()
        for cp in sends:
            cp.wait_send()
        acc = gathered[0]
        for dev in range(1, N_DEV):
            acc = acc + gathered[dev]
        out_ref[...] = acc

    vmem = pl.BlockSpec(memory_space=pltpu.VMEM)
    return pl.pallas_call(
        body, name="sum_small", in_specs=[vmem], out_specs=vmem,
        out_shape=jax.ShapeDtypeStruct(block.shape, block.dtype),
        scratch_shapes=[pltpu.VMEM((N_DEV,) + block.shape, block.dtype),
                        pltpu.SemaphoreType.DMA((N_DEV - 1,)), pltpu.SemaphoreType.DMA((N_DEV - 1,))],
    )(block)


def _adamw(w, g, m, v, name):
    rows, cols = w.shape
    tr = rows
    for cand in (512, 256, 128, 64, 32, 16, 8):
        if rows % cand == 0:
            tr = cand
            break
    c1 = 1.0 - ADAM_B1 ** ADAM_STEP
    c2 = 1.0 - ADAM_B2 ** ADAM_STEP

    def body(w_ref, g_ref, m_ref, v_ref, d_ref, nm_ref, nv_ref):
        gv = g_ref[...]
        nm = ADAM_B1 * m_ref[...] + (1.0 - ADAM_B1) * gv
        nv = ADAM_B2 * v_ref[...] + (1.0 - ADAM_B2) * (gv * gv)
        nm_ref[...] = nm
        nv_ref[...] = nv
        d_ref[...] = -ADAM_LR * ((nm / c1) / (jnp.sqrt(nv / c2) + ADAM_EPS) + ADAM_WD * w_ref[...])

    blk = pl.BlockSpec((tr, cols), lambda i: (i, 0))
    shape = jax.ShapeDtypeStruct((rows, cols), f32)
    return pl.pallas_call(
        body, name=name, grid=(rows // tr,), in_specs=[blk] * 4, out_specs=[blk] * 3, out_shape=[shape] * 3,
        compiler_params=_params("parallel"),
    )(w, g, m, v)


LARGE = ("w_in", "w_out", "w_gate", "w_up", "w_down")
SMALL = ("ln_pre_mix", "ln_post_mix", "ln_pre_ffn", "ln_post_ffn", "pool_scale", "w_pool")
SHARD_ROWS = {"w_in": 640, "w_out": 256, "w_gate": 704, "w_up": 704, "w_down": 704}
COLUMN_SHARDED = ("w_in", "w_gate", "w_up")
NEEDED_FIRST = ("w_in",)
NEEDED_LATER = ("w_out", "w_gate", "w_up", "w_down")
READY_EARLY = ("w_out", "w_gate", "w_up", "w_down")
READY_LATE = ("w_in",)


def _pack_shard(shards, names):
    return jnp.concatenate([shards[n].T if n in COLUMN_SHARDED else shards[n] for n in names], axis=0)


def _unpack_shard(pack, names):
    out, row = {}, 0
    for n in names:
        part = pack[row:row + SHARD_ROWS[n]]
        out[n] = part.T if n in COLUMN_SHARDED else part
        row += SHARD_ROWS[n]
    return out


def _whole_from_shards(packs, names):
    out, row = {}, 0
    for n in names:
        rows = SHARD_ROWS[n]
        out[n] = packs[:, row:row + rows].reshape(N_CHIPS * rows, D_MODEL)
        row += rows
    return out


def _shards_from_whole(grads, names):
    return jnp.concatenate([grads[n].reshape(N_CHIPS, SHARD_ROWS[n], D_MODEL) for n in names], axis=1)


def _pack_small(vals):
    rows = [vals[n].reshape(1, D_MODEL) for n in SMALL[:4]]
    rows.append(jnp.pad(vals["pool_scale"].reshape(1, POOL_WIDTH), ((0, 0), (0, D_MODEL - POOL_WIDTH))))
    rows.append(jnp.pad(vals["loss"].reshape(1, 1), ((0, 0), (0, D_MODEL - 1))))
    rows.append(jnp.zeros((2, D_MODEL), f32))
    rows.append(vals["w_pool"].reshape(16, D_MODEL))
    return jnp.concatenate(rows, axis=0)


def _unpack_small(block):
    out = {n: block[i:i + 1] for i, n in enumerate(SMALL[:4])}
    out["pool_scale"] = block[4:5, :POOL_WIDTH]
    out["loss"] = block[5, 0]
    out["w_pool"] = block[8:24].reshape(1, 4, POOL_GROUP, POOL_GROUP)
    return out


def kernel(x, ln_pre_mix, w_in, w_pool, pool_scale, w_out, ln_post_mix, ln_pre_ffn, w_gate, w_up, w_down, ln_post_ffn, loss_target, m_ln_pre_mix, m_w_in, m_w_pool, m_pool_scale, m_w_out, m_ln_post_mix, m_ln_pre_ffn, m_w_gate, m_w_up, m_w_down, m_ln_post_ffn, v_ln_pre_mix, v_w_in, v_w_pool, v_pool_scale, v_w_out, v_ln_post_mix, v_ln_pre_ffn, v_w_gate, v_w_up, v_w_down, v_ln_post_ffn):
    w = dict(ln_pre_mix=ln_pre_mix, w_in=w_in, w_pool=w_pool, pool_scale=pool_scale, w_out=w_out,
             ln_post_mix=ln_post_mix, ln_pre_ffn=ln_pre_ffn, w_gate=w_gate, w_up=w_up, w_down=w_down,
             ln_post_ffn=ln_post_ffn)
    m = dict(ln_pre_mix=m_ln_pre_mix, w_in=m_w_in, w_pool=m_w_pool, pool_scale=m_pool_scale, w_out=m_w_out,
             ln_post_mix=m_ln_post_mix, ln_pre_ffn=m_ln_pre_ffn, w_gate=m_w_gate, w_up=m_w_up, w_down=m_w_down,
             ln_post_ffn=m_ln_post_ffn)
    v = dict(ln_pre_mix=v_ln_pre_mix, w_in=v_w_in, w_pool=v_w_pool, pool_scale=v_pool_scale, w_out=v_w_out,
             ln_post_mix=v_ln_post_mix, ln_pre_ffn=v_ln_pre_ffn, w_gate=v_w_gate, w_up=v_w_up, w_down=v_w_down,
             ln_post_ffn=v_ln_post_ffn)

    xs, target = x[0], loss_target[0]
    cos_t, sin_t = _rope_tables(xs.shape[0])
    w_bd = _block_diag(w_pool[0]).astype(bf16)
    shard = {n: w[n][0].astype(bf16) for n in LARGE}

    w_in_whole = _whole_from_shards(_gather_weights(_pack_shard(shard, NEEDED_FIRST)), NEEDED_FIRST)["w_in"]
    h1, u, qs, ks, vs = _in_proj(xs, ln_pre_mix, w_in_whole, cos_t, sin_t)
    pool_out = _pool_fwd(u, w_bd, pool_scale)
    attn_out, lse, later = _attn_fwd(qs, ks, vs, _pack_shard(shard, NEEDED_LATER))
    whole = _whole_from_shards(later, NEEDED_LATER)
    mix, x2, h2 = _out_proj(pool_out, attn_out, whole["w_out"], xs, ln_post_mix, ln_pre_ffn)
    gate, up, f = _ffn_fwd(h2, whole["w_gate"], whole["w_up"], whole["w_down"])
    dy, df, dg4, loss = _loss_head(f, x2, target, ln_post_ffn)

    large = {}
    a, dgate, dup, dh2 = _ffn_bwd(df, gate, up, whole["w_gate"], whole["w_up"], whole["w_down"])
    large["w_down"] = _matmul_tiles_tn(a, df, "grad_w_down")
    large["w_gate"] = _matmul_tiles_tn(dgate, h2, "grad_w_gate")
    large["w_up"] = _matmul_tiles_tn(dup, h2, "grad_w_up")
    dx2, dmix, dg3, dg2 = _norm_bwd(dh2, dy, x2, mix, ln_pre_ffn, ln_post_mix)
    large["w_out"] = jnp.concatenate([_matmul_tn(pool_out, dmix, D_MODEL, "grad_w_out_pool"),
                                      _matmul_tn(attn_out, dmix, D_MODEL, "grad_w_out_attn")], axis=0)
    early = _shards_from_whole(large, READY_EARLY)
    early_chip = _add_cores(early, _swap_halves(early, "swap_halves_early"), "add_cores_early")
    dpool, delta, dos = _out_proj_bwd(dmix, whole["w_out"], attn_out, _head_ones())
    du, d_w_bd, d_scale = _pool_bwd(u, dpool, w_bd, pool_scale)
    dq, dk, dv, early_others = _attn_bwd(qs, ks, vs, dos, lse, delta, early_chip)
    grad_x, dproj, dg1 = _in_proj_bwd(du, dq, dk, dv, cos_t, sin_t, w_in_whole, xs, dx2, ln_pre_mix)
    large["w_in"] = _matmul_tn(dproj, h1, D_MODEL, "grad_w_in")
    late = _shards_from_whole(large, READY_LATE)
    late_chip = _add_cores(late, _swap_halves(late, "swap_halves_late"), "add_cores_late")
    late_others = _scatter_to_chips(late_chip)
    early_half = _add_chips(early_chip, early_others, "add_chips_early")
    late_half = _add_chips(late_chip, late_others, "add_chips_late")
    joined = _join_halves(jnp.concatenate([early_half, late_half], axis=0))
    n_early = early_half.shape[0]
    grads = _unpack_shard(joined[:, :n_early].reshape(-1, D_MODEL), READY_EARLY)
    grads.update(_unpack_shard(joined[:, n_early:].reshape(-1, D_MODEL), READY_LATE))

    d_w_pool = jnp.stack([d_w_bd[g * POOL_GROUP:(g + 1) * POOL_GROUP, g * POOL_GROUP:(g + 1) * POOL_GROUP]
                          for g in range(POOL_WIDTH // POOL_GROUP)])
    small = dict(ln_pre_mix=dg1, ln_post_mix=dg2, ln_pre_ffn=dg3, ln_post_ffn=dg4, pool_scale=d_scale, w_pool=d_w_pool)
    total = _unpack_small(_sum_small(_pack_small(dict(small, loss=loss))))
    for n in SMALL:
        grads[n] = total[n]

    delta_w, new_m, new_v = {}, {}, {}
    for n in LARGE:
        delta_w[n], new_m[n], new_v[n] = _adamw(w[n][0], grads[n], m[n][0], v[n][0], "adamw_" + n)
    small_state = [_pack_small(dict({n: s[n] for n in SMALL}, loss=jnp.zeros((), f32))) for s in (w, m, v)]
    small_grad = _pack_small(dict({n: grads[n] for n in SMALL}, loss=jnp.zeros((), f32)))
    sd, sm, sv = _adamw(small_state[0], small_grad, small_state[1], small_state[2], "adamw_small")
    for out, block in ((delta_w, sd), (new_m, sm), (new_v, sv)):
        un = _unpack_small(block)
        for n in SMALL:
            out[n] = un[n]

    names = ("ln_pre_mix", "w_in", "w_pool", "pool_scale", "w_out", "ln_post_mix", "ln_pre_ffn", "w_gate", "w_up",
             "w_down", "ln_post_ffn")
    full = lambda d: [d[n].reshape(w[n].shape) for n in names]
    return (total["loss"], grad_x[None], *full(grads), *full(delta_w), *full(new_m), *full(new_v))
```

```python
import numpy as np
import jax
import jax.numpy as jnp
from jax import lax
from jax.experimental import pallas as pl
from jax.experimental.pallas import tpu as pltpu

D_MODEL = 1024
POOL_WIDTH = 256
POOL_GROUP = 64
ATTN_WIDTH = 768
HEAD_DIM = 64
IN_WIDTH = 2560
D_FF = 2816
BLOCK = 128
DILATIONS = (1, 4, 16)
ROPE_THETA = 10000.0
EPS = 1e-6
ATTN_SCALE = 0.125
NEG = -1e30

ADAM_LR = 0.001
ADAM_B1 = 0.9
ADAM_B2 = 0.999
ADAM_EPS = 1e-08
ADAM_WD = 0.01
ADAM_STEP = 10

N_CHIPS = 4
N_DEV = 8
VMEM_LIMIT_V7X = 56 * 1024 * 1024
MESH = pl.DeviceIdType.MESH

f32 = jnp.float32
bf16 = jnp.bfloat16


def _params(*sem):
    return pltpu.CompilerParams(dimension_semantics=sem, vmem_limit_bytes=VMEM_LIMIT_V7X)


def _dot(a, b):
    return jnp.dot(a, b, preferred_element_type=f32)


def _dot_nt(a, b):
    return lax.dot_general(a, b, (((1,), (1,)), ((), ())), preferred_element_type=f32)


def _dot_tn(a, b):
    return lax.dot_general(a, b, (((0,), (0,)), ((), ())), preferred_element_type=f32)


def _rope_partner(a, first_half):
    return jnp.where(first_half, pltpu.roll(a, 96, 1), pltpu.roll(a, 32, 1))


def _first_half_mask(rows):
    lane = lax.broadcasted_iota(jnp.int32, (rows, 128), 1)
    return (lane % HEAD_DIM) < (HEAD_DIM // 2)


def _stream_spec(d, ts):
    return pl.BlockSpec((d, ts // d, ATTN_WIDTH), lambda i: (0, i, 0))


def _stream_shape(S, d):
    return jax.ShapeDtypeStruct((d, S // d, ATTN_WIDTH), bf16)


N_STAGE = ATTN_WIDTH // 128


def _stage_scratch(ts):
    return [pltpu.VMEM((ts, 128), f32)] * N_STAGE


def _store_streams(stage, out_refs, ts):
    for d, ref in zip(DILATIONS, out_refs):
        for r in range(d):
            rows = pl.ds(0, ts) if d == 1 else pl.ds(r, ts // d, stride=d)
            for j in range(N_STAGE):
                ref[r, :, j * 128:(j + 1) * 128] = stage[j][rows, :].astype(bf16)


def _in_proj(x, g1, w_in, cos_t, sin_t):
    S = x.shape[0]
    ts = 512

    def body(x_ref, g_ref, w_ref, cos_ref, sin_ref, h_ref, u_ref, *rest):
        outs, stage = rest[:-N_STAGE], rest[-N_STAGE:]
        xv = x_ref[...]
        r = lax.rsqrt(jnp.mean(xv * xv, axis=-1, keepdims=True) + EPS)
        h = ((xv * r) * g_ref[...]).astype(bf16)
        h_ref[...] = h
        proj = _dot_nt(h, w_ref[...])
        u_ref[...] = proj[:, :POOL_WIDTH]
        cos = cos_ref[...]
        sin = sin_ref[...]
        first = _first_half_mask(ts)
        n_dil = len(DILATIONS)
        for which, base in enumerate((POOL_WIDTH, POOL_WIDTH + ATTN_WIDTH)):
            for j in range(ATTN_WIDTH // 128):
                a = proj[:, base + j * 128: base + (j + 1) * 128]
                if which == 0:
                    a = a * ATTN_SCALE
                stage[j][...] = a * cos + _rope_partner(a, first) * sin
            _store_streams(stage, outs[which * n_dil:(which + 1) * n_dil], ts)
        for j in range(ATTN_WIDTH // 128):
            base = POOL_WIDTH + 2 * ATTN_WIDTH + j * 128
            stage[j][...] = proj[:, base:base + 128]
        _store_streams(stage, outs[2 * n_dil:], ts)

    row = lambda w: pl.BlockSpec((ts, w), lambda i: (i, 0))
    streams = [_stream_spec(d, ts) for d in DILATIONS] * 3
    res = pl.pallas_call(
        body, name="in_proj", grid=(S // ts,),
        in_specs=[row(D_MODEL), pl.BlockSpec((1, D_MODEL), lambda i: (0, 0)),
                  pl.BlockSpec((IN_WIDTH, D_MODEL), lambda i: (0, 0)), row(128), row(128)],
        out_specs=[row(D_MODEL), row(POOL_WIDTH)] + streams,
        out_shape=[jax.ShapeDtypeStruct((S, D_MODEL), bf16), jax.ShapeDtypeStruct((S, POOL_WIDTH), f32)]
        + [_stream_shape(S, d) for d in DILATIONS] * 3,
        scratch_shapes=_stage_scratch(ts),
        compiler_params=_params("parallel"),
    )(x, g1, w_in, cos_t, sin_t)
    n = len(DILATIONS)
    return res[0], res[1], res[2:2 + n], res[2 + n:2 + 2 * n], res[2 + 2 * n:]


POOL_HALO = 16


def _pool_lane_group(rows):
    return lax.broadcasted_iota(jnp.int32, (rows, POOL_WIDTH), 1) // POOL_GROUP


def _pool_select(group, s2, s4, s8, s16):
    return jnp.where(group == 0, s2, jnp.where(group == 1, s4, jnp.where(group == 2, s8, s16)))


def _pool_count(t0, rows):
    group = _pool_lane_group(rows)
    t = t0 + lax.broadcasted_iota(jnp.int32, (rows, POOL_WIDTH), 0)
    win = _pool_select(group, 2, 4, 8, 16)
    return jnp.minimum(t + 1, win).astype(f32)


def _pool_diff(u_halo, u_tile, t0):
    ts = u_tile.shape[0]
    ext = jnp.concatenate([u_halo, u_tile], axis=0)
    s2 = ext + pltpu.roll(ext, 1, 0)
    s4 = s2 + pltpu.roll(s2, 2, 0)
    s8 = s4 + pltpu.roll(s4, 4, 0)
    s16 = s8 + pltpu.roll(s8, 8, 0)
    group = _pool_lane_group(ts + POOL_HALO)
    wsum = _pool_select(group, s2, s4, s8, s16)[POOL_HALO:]
    return wsum / _pool_count(t0, ts) - u_tile


def _pool_specs(ts, n_tiles):
    tile = pl.BlockSpec((ts, POOL_WIDTH), lambda i: (i, 0))
    per = ts // POOL_HALO
    before = pl.BlockSpec((POOL_HALO, POOL_WIDTH), lambda i: (jnp.maximum(i * per - 1, 0), 0))
    after = pl.BlockSpec((POOL_HALO, POOL_WIDTH), lambda i: (jnp.minimum((i + 1) * per, n_tiles * per - 1), 0))
    return tile, before, after


def _pool_fwd(u, w_bd, scale):
    S = u.shape[0]
    ts = 512
    n_tiles = S // ts

    def body(u_ref, halo_ref, w_ref, sc_ref, y_ref):
        i = pl.program_id(0)
        halo = jnp.where(i > 0, halo_ref[...], 0.0)
        d = _pool_diff(halo, u_ref[...], i * ts)
        y_ref[...] = (_dot(d.astype(bf16), w_ref[...]) * sc_ref[...]).astype(bf16)

    tile, before, _ = _pool_specs(ts, n_tiles)
    return pl.pallas_call(
        body, name="pool_fwd", grid=(n_tiles,),
        in_specs=[tile, before, pl.BlockSpec((POOL_WIDTH, POOL_WIDTH), lambda i: (0, 0)),
                  pl.BlockSpec((1, POOL_WIDTH), lambda i: (0, 0))],
        out_specs=tile, out_shape=jax.ShapeDtypeStruct((S, POOL_WIDTH), bf16),
        compiler_params=_params("parallel"),
    )(u, u, w_bd, scale)


def _pool_bwd(u, dy, w_bd, scale):
    S = u.shape[0]
    ts = 512
    n_tiles = S // ts

    def body(u_ref, halo_ref, dy_ref, dy_next_ref, w_ref, sc_ref, du_ref, dw_ref, dsc_ref):
        i = pl.program_id(0)

        @pl.when(i == 0)
        def _():
            dw_ref[...] = jnp.zeros_like(dw_ref)
            dsc_ref[...] = jnp.zeros_like(dsc_ref)

        halo = jnp.where(i > 0, halo_ref[...], 0.0)
        d = _pool_diff(halo, u_ref[...], i * ts).astype(bf16)
        w = w_ref[...]
        sc = sc_ref[...]
        dy_tile = dy_ref[...]
        z = _dot(d, w)
        dsc_ref[...] += jnp.sum(dy_tile * z, axis=0, keepdims=True)
        dy_next = jnp.where(i < n_tiles - 1, dy_next_ref[...], 0.0)
        dz = (jnp.concatenate([dy_tile, dy_next], axis=0) * sc).astype(bf16)
        dw_ref[...] += _dot_tn(d, dz[:ts])
        dd = _dot_nt(dz, w)
        e = dd / _pool_count(i * ts, ts + POOL_HALO)
        n = ts + POOL_HALO
        f2 = e + pltpu.roll(e, n - 1, 0)
        f4 = f2 + pltpu.roll(f2, n - 2, 0)
        f8 = f4 + pltpu.roll(f4, n - 4, 0)
        f16 = f8 + pltpu.roll(f8, n - 8, 0)
        fsum = _pool_select(_pool_lane_group(n), f2, f4, f8, f16)
        du_ref[...] = (fsum[:ts] - dd[:ts]).astype(bf16)

    tile, before, after = _pool_specs(ts, n_tiles)
    return pl.pallas_call(
        body, name="pool_bwd", grid=(n_tiles,),
        in_specs=[tile, before, tile, after, pl.BlockSpec((POOL_WIDTH, POOL_WIDTH), lambda i: (0, 0)),
                  pl.BlockSpec((1, POOL_WIDTH), lambda i: (0, 0))],
        out_specs=[tile, pl.BlockSpec((POOL_WIDTH, POOL_WIDTH), lambda i: (0, 0)),
                   pl.BlockSpec((1, POOL_WIDTH), lambda i: (0, 0))],
        out_shape=[jax.ShapeDtypeStruct((S, POOL_WIDTH), bf16), jax.ShapeDtypeStruct((POOL_WIDTH, POOL_WIDTH), f32),
                   jax.ShapeDtypeStruct((1, POOL_WIDTH), f32)],
        compiler_params=_params("arbitrary"),
    )(u, u, dy, dy, w_bd, scale)


SUPER = BLOCK * DILATIONS[-1]
UNITS = SUPER // BLOCK
FWD_UNROLL = 16
BWD_UNROLL = 8


def _band_mask(has_prev):
    qi = lax.broadcasted_iota(jnp.int32, (BLOCK, 2 * BLOCK), 0)
    kj = lax.broadcasted_iota(jnp.int32, (BLOCK, 2 * BLOCK), 1)
    return (kj >= qi) & (kj <= qi + BLOCK) & ((kj >= BLOCK) | has_prev)


def _head0_mask(rows=BLOCK):
    return lax.broadcasted_iota(jnp.int32, (rows, 128), 1) < HEAD_DIM


def _band_mask_t(has_prev):
    ki = lax.broadcasted_iota(jnp.int32, (2 * BLOCK, 2 * BLOCK), 0)
    qj = lax.broadcasted_iota(jnp.int32, (2 * BLOCK, 2 * BLOCK), 1) % BLOCK
    return (ki >= qj) & (ki <= qj + BLOCK) & ((ki >= BLOCK) | has_prev)


def _head_pair_rows(a, h0):
    zero = jnp.zeros_like(a)
    return jnp.concatenate([jnp.where(h0, a, zero), jnp.where(h0, zero, a)], axis=0)


def _per_query_row(stat):
    t = stat.T
    return jnp.concatenate([jnp.concatenate([t[:HEAD_DIM]] * 4, axis=0), jnp.concatenate([t[HEAD_DIM:]] * 4, axis=0)],
                           axis=1)


def _natural_rows(d, r, n):
    if d == 1:
        return pl.ds(pl.multiple_of(n * BLOCK, BLOCK), BLOCK)
    return pl.ds(n * (BLOCK * d) + r, BLOCK, stride=d)


def _unit_place(d, u):
    per_stream = UNITS // d
    return u // per_stream, u % per_stream, per_stream


def _block_rows(n):
    return pl.ds(pl.multiple_of(n * BLOCK, BLOCK), BLOCK)


def _band(cur_ref, tail_ref, r, n):
    before = jnp.where(n > 0, cur_ref[r, _block_rows(jnp.maximum(n - 1, 0)), :], tail_ref[r])
    return jnp.concatenate([before, cur_ref[r, _block_rows(n), :]], axis=0)


def _attn_in_specs(S, with_do):
    specs = []
    last = S // SUPER - 1
    for d in DILATIONS:
        per_stream = UNITS // d
        cur = pl.BlockSpec((d, SUPER // d, 128), lambda hp, sb: (0, jnp.minimum(sb, last), hp))
        tail = pl.BlockSpec(
            (d, BLOCK, 128),
            lambda hp, sb, per_stream=per_stream: (0, jnp.maximum(jnp.minimum(sb, last) * per_stream - 1, 0), hp))
        specs += [cur] * (2 if with_do else 1) + [cur, tail, cur, tail]
    return specs


def _attn_fwd(qs, ks, vs, pack):
    S = qs[0].shape[1]
    n_dil = len(DILATIONS)
    n_steps = S // SUPER
    n_total = (ATTN_WIDTH // 128) * n_steps

    def body(*refs):
        ins, pack_ref = refs[:5 * n_dil], refs[5 * n_dil]
        out_ref, lse_ref, gathered_ref = refs[5 * n_dil + 1:5 * n_dil + 4]
        scratch = refs[5 * n_dil + 4:]
        o_sc, l_sc = scratch[:n_dil], scratch[n_dil:2 * n_dil]
        gather = _Gather(pack_ref, gathered_ref, *scratch[2 * n_dil:])
        sb = pl.program_id(1)
        step = pl.program_id(0) * n_steps + sb

        @pl.when(step == 0)
        def _():
            gather.start()

        h0 = _head0_mask()
        for ci, d in enumerate(DILATIONS):
            q_ref, kc_ref, kp_ref, vc_ref, vp_ref = ins[5 * ci:5 * ci + 5]

            def unit(u, carry, d=d, ci=ci, q_ref=q_ref, kc_ref=kc_ref, kp_ref=kp_ref, vc_ref=vc_ref, vp_ref=vp_ref):
                r, n, _ = _unit_place(d, u)
                qv = q_ref[r, _block_rows(n), :]
                kb = _band(kc_ref, kp_ref, r, n)
                vb = _band(vc_ref, vp_ref, r, n)
                valid = _band_mask((sb > 0) | (n > 0))
                outs, lses = [], []
                for h in range(2):
                    keep = h0 if h == 0 else jnp.logical_not(h0)
                    qh = jnp.where(keep, qv, jnp.zeros_like(qv))
                    s = jnp.where(valid, _dot_nt(qh, kb), NEG)
                    m = jnp.max(s, axis=1, keepdims=True)
                    e = jnp.exp(s - m)
                    den = jnp.sum(e, axis=1, keepdims=True)
                    outs.append(_dot(e.astype(bf16), vb) * (1.0 / den))
                    lses.append(jnp.broadcast_to(m + jnp.log(den), (BLOCK, 128)))
                rows = _natural_rows(d, r, n)
                o_sc[ci][rows, :] = jnp.where(h0, outs[0], outs[1])
                l_sc[ci][rows, :] = jnp.where(h0, lses[0], lses[1])
                return carry

            lax.fori_loop(0, UNITS, unit, 0, unroll=FWD_UNROLL)

        def merge(t, carry):
            rows = pl.ds(pl.multiple_of(t * 256, 256), 256)
            a, b, c = l_sc[0][rows, :], l_sc[1][rows, :], l_sc[2][rows, :]
            m = jnp.maximum(jnp.maximum(a, b), c)
            ea, eb, ec = jnp.exp(a - m), jnp.exp(b - m), jnp.exp(c - m)
            tot = ea + eb + ec
            out_ref[rows, :] = ((ea / tot) * o_sc[0][rows, :] + (eb / tot) * o_sc[1][rows, :]
                                + (ec / tot) * o_sc[2][rows, :]).astype(bf16)
            lse_ref[rows, :] = m + jnp.log(tot)
            return carry

        lax.fori_loop(0, SUPER // 256, merge, 0)

        @pl.when(step == n_total // 2)
        def _():
            gather.pass_on()

        @pl.when(step == n_total - 1)
        def _():
            gather.finish()

    args = []
    for q, k, v in zip(qs, ks, vs):
        args += [q, k, k, v, v]
    nat = pl.BlockSpec((SUPER, 128), lambda hp, sb: (sb, hp))
    rows = pack.shape[0]
    return pl.pallas_call(
        body, name="attn_fwd", grid=(ATTN_WIDTH // 128, n_steps),
        in_specs=_attn_in_specs(S, False) + [ANY], out_specs=[nat, nat, ANY],
        out_shape=[jax.ShapeDtypeStruct((S, ATTN_WIDTH), bf16), jax.ShapeDtypeStruct((S, ATTN_WIDTH), f32),
                   _Gather.out_shape(rows, pack.dtype)],
        scratch_shapes=[pltpu.VMEM((SUPER, 128), f32)] * (2 * n_dil) + _Gather.scratch(rows, pack.dtype),
        compiler_params=_params("arbitrary", "arbitrary"),
    )(*args, pack)


def _attn_bwd(qs, ks, vs, dos, lse, delta, chip_sum):
    S = qs[0].shape[1]
    n_steps = S // SUPER
    last = n_steps - 1
    n_dil = len(DILATIONS)
    n_total = (ATTN_WIDTH // 128) * (n_steps + 1)

    def body(*refs):
        ins, (lse_ref, dl_ref, sum_ref) = refs[:6 * n_dil], refs[6 * n_dil:6 * n_dil + 3]
        dq_ref, dk_ref, dv_ref, others_ref = refs[6 * n_dil + 3:6 * n_dil + 7]
        dq_acc, dk_acc, dv_acc = refs[6 * n_dil + 7:6 * n_dil + 10]
        scatter = _Scatter(sum_ref, others_ref, *refs[6 * n_dil + 10:])
        sb = pl.program_id(1)
        step = pl.program_id(0) * (n_steps + 1) + sb
        cur = sb % 2
        prv = 1 - cur

        @pl.when(step == 0)
        def _():
            scatter.start()

        @pl.when(sb < n_steps)
        def _():
            dq_acc[...] = jnp.zeros_like(dq_acc)
            dk_acc[cur] = jnp.zeros((SUPER, 128), f32)
            dv_acc[cur] = jnp.zeros((SUPER, 128), f32)
            h0 = _head0_mask()
            for ci, d in enumerate(DILATIONS):
                q_ref, do_ref, kc_ref, kp_ref, vc_ref, vp_ref = ins[6 * ci:6 * ci + 6]

                def unit(u, carry, d=d, q_ref=q_ref, do_ref=do_ref, kc_ref=kc_ref, kp_ref=kp_ref, vc_ref=vc_ref,
                         vp_ref=vp_ref):
                    r, n, per_stream = _unit_place(d, u)
                    qv = q_ref[r, _block_rows(n), :]
                    dov = do_ref[r, _block_rows(n), :]
                    kb = _band(kc_ref, kp_ref, r, n)
                    vb = _band(vc_ref, vp_ref, r, n)
                    rows = _natural_rows(d, r, n)
                    has_prev = (sb > 0) | (n > 0)
                    q_pair = _head_pair_rows(qv, h0)
                    do_pair = _head_pair_rows(dov, h0)
                    s_t = jnp.where(_band_mask_t(has_prev), _dot_nt(kb, q_pair), NEG)
                    p_t = jnp.exp(s_t - _per_query_row(lse_ref[rows, :]))
                    dp_t = _dot_nt(vb, do_pair)
                    ds_t = (p_t * (dp_t - _per_query_row(dl_ref[rows, :]))).astype(bf16)
                    dvb = _dot(p_t.astype(bf16), do_pair)
                    dkb = _dot(ds_t, q_pair)
                    dq_pair = _dot_tn(ds_t, kb)
                    dq_acc[rows, :] += jnp.where(h0, dq_pair[:BLOCK], dq_pair[BLOCK:])
                    dk_acc[cur, rows, :] += dkb[BLOCK:]
                    dv_acc[cur, rows, :] += dvb[BLOCK:]

                    slot = jnp.where((n > 0) | (sb == 0), cur, prv)
                    before = _natural_rows(d, r, jnp.where(n > 0, n - 1, per_stream - 1))
                    dk_acc[slot, before, :] += dkb[:BLOCK]
                    dv_acc[slot, before, :] += dvb[:BLOCK]
                    return carry

                lax.fori_loop(0, UNITS, unit, 0, unroll=BWD_UNROLL)
            dq_ref[...] = (dq_acc[...] * ATTN_SCALE).astype(bf16)

        @pl.when(sb > 0)
        def _():
            dk_ref[...] = dk_acc[prv].astype(bf16)
            dv_ref[...] = dv_acc[prv].astype(bf16)

        @pl.when(step == n_total - 1)
        def _():
            scatter.finish()

    args = []
    for q, k, v, do in zip(qs, ks, vs, dos):
        args += [q, do, k, k, v, v]
    nat = pl.BlockSpec((SUPER, 128), lambda hp, sb: (jnp.minimum(sb, last), hp))
    nat_before = pl.BlockSpec((SUPER, 128), lambda hp, sb: (jnp.clip(sb - 1, 0, last), hp))
    out = jax.ShapeDtypeStruct((S, ATTN_WIDTH), bf16)
    half = chip_sum.shape[1]
    return pl.pallas_call(
        body, name="attn_bwd", grid=(ATTN_WIDTH // 128, n_steps + 1),
        in_specs=_attn_in_specs(S, True) + [nat, nat, ANY], out_specs=[nat, nat_before, nat_before, ANY],
        out_shape=[out, out, out, _Scatter.out_shape(half, chip_sum.dtype)],
        scratch_shapes=[pltpu.VMEM((SUPER, 128), f32), pltpu.VMEM((2, SUPER, 128), f32),
                        pltpu.VMEM((2, SUPER, 128), f32)] + _Scatter.scratch(half),
        compiler_params=_params("arbitrary", "arbitrary"),
    )(*args, lse, delta, chip_sum)


def _rms(v):
    return lax.rsqrt(jnp.mean(v * v, axis=-1, keepdims=True) + EPS)


def _out_proj(pool_out, attn_out, w_out, x, g2, g3):
    S = x.shape[0]
    ts = 512

    def body(p_ref, a_ref, w_ref, x_ref, g2_ref, g3_ref, mix_ref, x2_ref, h2_ref):
        mix = _dot(p_ref[...], w_ref[:POOL_WIDTH, :]) + _dot(a_ref[...], w_ref[POOL_WIDTH:, :])
        mix_ref[...] = mix
        x2 = x_ref[...] + (mix * _rms(mix)) * g2_ref[...]
        x2_ref[...] = x2
        h2_ref[...] = ((x2 * _rms(x2)) * g3_ref[...]).astype(bf16)

    row = lambda w: pl.BlockSpec((ts, w), lambda i: (i, 0))
    gain = pl.BlockSpec((1, D_MODEL), lambda i: (0, 0))
    return pl.pallas_call(
        body, name="out_proj", grid=(S // ts,),
        in_specs=[row(POOL_WIDTH), row(ATTN_WIDTH), pl.BlockSpec((D_MODEL, D_MODEL), lambda i: (0, 0)),
                  row(D_MODEL), gain, gain],
        out_specs=[row(D_MODEL)] * 3,
        out_shape=[jax.ShapeDtypeStruct((S, D_MODEL), f32), jax.ShapeDtypeStruct((S, D_MODEL), f32),
                   jax.ShapeDtypeStruct((S, D_MODEL), bf16)],
        compiler_params=_params("parallel"),
    )(pool_out, attn_out, w_out, x, g2, g3)


FF_TILE = 256
FF_STEP_ROWS = 2048
FF_ROWS = 256


def _sigmoid(g):
    return 1.0 / (1.0 + jnp.exp(-g))


def _ff_act_shape(S):
    return jax.ShapeDtypeStruct((D_FF // FF_TILE, S, FF_TILE), bf16)


def _ff_act_spec(ts):
    return pl.BlockSpec((1, ts, FF_TILE), lambda i, j: (j, i, 0))


def _ffn_fwd(h2, w_gate, w_up, w_down):
    S = h2.shape[0]
    ts = min(S, FF_STEP_ROWS)

    def body(h_ref, wg_ref, wu_ref, wd_ref, gate_ref, up_ref, f_ref):
        def rows_pass(first):
            def sub(i, carry):
                rows = pl.ds(pl.multiple_of(i * FF_ROWS, FF_ROWS), FF_ROWS)
                h = h_ref[rows, :]
                gate = _dot_nt(h, wg_ref[...])
                up = _dot_nt(h, wu_ref[...])
                gate_ref[0, rows, :] = gate.astype(bf16)
                up_ref[0, rows, :] = up.astype(bf16)
                part = _dot((gate * _sigmoid(gate) * up).astype(bf16), wd_ref[...])
                if first:
                    f_ref[rows, :] = part
                else:
                    f_ref[rows, :] += part
                return carry

            lax.fori_loop(0, ts // FF_ROWS, sub, 0, unroll=True)

        @pl.when(pl.program_id(1) == 0)
        def _():
            rows_pass(True)

        @pl.when(pl.program_id(1) > 0)
        def _():
            rows_pass(False)

    act = _ff_act_spec(ts)
    return pl.pallas_call(
        body, name="ffn_fwd", grid=(S // ts, D_FF // FF_TILE),
        in_specs=[pl.BlockSpec((ts, D_MODEL), lambda i, j: (i, 0)),
                  pl.BlockSpec((FF_TILE, D_MODEL), lambda i, j: (j, 0)),
                  pl.BlockSpec((FF_TILE, D_MODEL), lambda i, j: (j, 0)),
                  pl.BlockSpec((FF_TILE, D_MODEL), lambda i, j: (j, 0))],
        out_specs=[act, act, pl.BlockSpec((ts, D_MODEL), lambda i, j: (i, 0))],
        out_shape=[_ff_act_shape(S), _ff_act_shape(S), jax.ShapeDtypeStruct((S, D_MODEL), f32)],
        compiler_params=_params("parallel", "arbitrary"),
    )(h2, w_gate, w_up, w_down)


def _loss_head(f, x2, target, g4):
    S = f.shape[0]
    ts = 512

    def body(f_ref, x2_ref, t_ref, g_ref, dy_ref, df_ref, dg_ref, loss_ref):
        @pl.when(pl.program_id(0) == 0)
        def _():
            dg_ref[...] = jnp.zeros_like(dg_ref)
            loss_ref[...] = jnp.zeros_like(loss_ref)

        fv = f_ref[...]
        g = g_ref[...]
        r = _rms(fv)
        fhat = fv * r
        err = (x2_ref[...] + fhat * g) - t_ref[...]
        loss_ref[...] += 0.5 * jnp.sum(jnp.mean(err * err, axis=-1, keepdims=True), axis=0, keepdims=True)
        dy = err * (1.0 / D_MODEL)
        dy_ref[...] = dy
        dg_ref[...] += jnp.sum(dy * fhat, axis=0, keepdims=True)
        dyg = dy * g
        df_ref[...] = (r * (dyg - fhat * jnp.mean(dyg * fhat, axis=-1, keepdims=True))).astype(bf16)

    row = pl.BlockSpec((ts, D_MODEL), lambda i: (i, 0))
    gain = pl.BlockSpec((1, D_MODEL), lambda i: (0, 0))
    return pl.pallas_call(
        body, name="loss_head", grid=(S // ts,), in_specs=[row, row, row, gain],
        out_specs=[row, row, gain, pl.BlockSpec((1, 1), lambda i: (0, 0))],
        out_shape=[jax.ShapeDtypeStruct((S, D_MODEL), f32), jax.ShapeDtypeStruct((S, D_MODEL), bf16),
                   jax.ShapeDtypeStruct((1, D_MODEL), f32), jax.ShapeDtypeStruct((1, 1), f32)],
        compiler_params=_params("arbitrary"),
    )(f, x2, target, g4)


def _ffn_bwd(df, gate, up, w_gate, w_up, w_down):
    S = df.shape[0]
    ts = min(S, FF_STEP_ROWS)

    def body(df_ref, gate_ref, up_ref, wg_ref, wu_ref, wd_ref, a_ref, dgate_ref, dup_ref, dh_ref):
        def rows_pass(first):
            def sub(i, carry):
                rows = pl.ds(pl.multiple_of(i * FF_ROWS, FF_ROWS), FF_ROWS)
                da = _dot_nt(df_ref[rows, :], wd_ref[...])
                g = gate_ref[0, rows, :].astype(f32)
                u = up_ref[0, rows, :].astype(f32)
                sig = _sigmoid(g)
                silu = g * sig
                a_ref[0, rows, :] = (silu * u).astype(bf16)
                dup = (da * silu).astype(bf16)
                dgate = (da * u * (sig * (1.0 + g * (1.0 - sig)))).astype(bf16)
                dup_ref[0, rows, :] = dup
                dgate_ref[0, rows, :] = dgate
                part = _dot(dgate, wg_ref[...]) + _dot(dup, wu_ref[...])
                if first:
                    dh_ref[rows, :] = part
                else:
                    dh_ref[rows, :] += part
                return carry

            lax.fori_loop(0, ts // FF_ROWS, sub, 0, unroll=True)

        @pl.when(pl.program_id(1) == 0)
        def _():
            rows_pass(True)

        @pl.when(pl.program_id(1) > 0)
        def _():
            rows_pass(False)

    act = _ff_act_spec(ts)
    row = pl.BlockSpec((ts, D_MODEL), lambda i, j: (i, 0))
    return pl.pallas_call(
        body, name="ffn_bwd", grid=(S // ts, D_FF // FF_TILE),
        in_specs=[row, act, act,
                  pl.BlockSpec((FF_TILE, D_MODEL), lambda i, j: (j, 0)),
                  pl.BlockSpec((FF_TILE, D_MODEL), lambda i, j: (j, 0)),
                  pl.BlockSpec((FF_TILE, D_MODEL), lambda i, j: (j, 0))],
        out_specs=[act, act, act, row],
        out_shape=[_ff_act_shape(S)] * 3 + [jax.ShapeDtypeStruct((S, D_MODEL), f32)],
        compiler_params=_params("parallel", "arbitrary"),
    )(df, gate, up, w_gate, w_up, w_down)


def _norm_bwd(dh2, dy, x2, mix, g3, g2):
    S = dh2.shape[0]
    ts = 512

    def body(dh_ref, dy_ref, x2_ref, mix_ref, g3_ref, g2_ref, dx2_ref, dmix_ref, dg3_ref, dg2_ref):
        @pl.when(pl.program_id(0) == 0)
        def _():
            dg3_ref[...] = jnp.zeros_like(dg3_ref)
            dg2_ref[...] = jnp.zeros_like(dg2_ref)

        dh = dh_ref[...]
        x2 = x2_ref[...]
        r3 = _rms(x2)
        xhat = x2 * r3
        dg3_ref[...] += jnp.sum(dh * xhat, axis=0, keepdims=True)
        dhg = dh * g3_ref[...]
        dx2 = dy_ref[...] + r3 * (dhg - xhat * jnp.mean(dhg * xhat, axis=-1, keepdims=True))
        dx2_ref[...] = dx2
        mix = mix_ref[...]
        r2 = _rms(mix)
        mhat = mix * r2
        dg2_ref[...] += jnp.sum(dx2 * mhat, axis=0, keepdims=True)
        dmg = dx2 * g2_ref[...]
        dmix_ref[...] = (r2 * (dmg - mhat * jnp.mean(dmg * mhat, axis=-1, keepdims=True))).astype(bf16)

    row = pl.BlockSpec((ts, D_MODEL), lambda i: (i, 0))
    gain = pl.BlockSpec((1, D_MODEL), lambda i: (0, 0))
    return pl.pallas_call(
        body, name="norm_bwd", grid=(S // ts,), in_specs=[row, row, row, row, gain, gain],
        out_specs=[row, row, gain, gain],
        out_shape=[jax.ShapeDtypeStruct((S, D_MODEL), f32), jax.ShapeDtypeStruct((S, D_MODEL), bf16),
                   jax.ShapeDtypeStruct((1, D_MODEL), f32), jax.ShapeDtypeStruct((1, D_MODEL), f32)],
        compiler_params=_params("arbitrary"),
    )(dh2, dy, x2, mix, g3, g2)


def _out_proj_bwd(dmix, w_out, attn_out, head_ones):
    S = dmix.shape[0]
    ts = 512

    def body(dm_ref, w_ref, o_ref, ones_ref, dp_ref, dl_ref, *rest):
        do_refs, stage = rest[:-N_STAGE], rest[-N_STAGE:]
        dcat = _dot_nt(dm_ref[...], w_ref[...])
        dp_ref[...] = dcat[:, :POOL_WIDTH]
        do = dcat[:, POOL_WIDTH:]
        for j in range(ATTN_WIDTH // 128):
            stage[j][...] = do[:, j * 128:(j + 1) * 128]
        _store_streams(stage, do_refs, ts)
        prod = do * o_ref[...].astype(f32)
        hi = prod.astype(bf16)
        lo = (prod - hi.astype(f32)).astype(bf16)
        dl_ref[...] = _dot(hi, ones_ref[...]) + _dot(lo, ones_ref[...])

    row = lambda w: pl.BlockSpec((ts, w), lambda i: (i, 0))
    res = pl.pallas_call(
        body, name="out_proj_bwd", grid=(S // ts,),
        in_specs=[row(D_MODEL), pl.BlockSpec((D_MODEL, D_MODEL), lambda i: (0, 0)), row(ATTN_WIDTH),
                  pl.BlockSpec((ATTN_WIDTH, ATTN_WIDTH), lambda i: (0, 0))],
        out_specs=[row(POOL_WIDTH), row(ATTN_WIDTH)] + [_stream_spec(d, ts) for d in DILATIONS],
        out_shape=[jax.ShapeDtypeStruct((S, POOL_WIDTH), f32), jax.ShapeDtypeStruct((S, ATTN_WIDTH), f32)]
        + [_stream_shape(S, d) for d in DILATIONS],
        scratch_shapes=_stage_scratch(ts),
        compiler_params=_params("parallel"),
    )(dmix, w_out, attn_out, head_ones)
    return res[0], res[1], res[2:]


def _in_proj_bwd(du, dq, dk, dv, cos_t, sin_t, w_in, x, dx2, g1):
    S = x.shape[0]
    ts = 256

    def body(du_ref, dq_ref, dk_ref, dv_ref, cos_ref, sin_ref, w_ref, x_ref, dx2_ref, g_ref, gx_ref, dproj_ref, dg_ref):
        @pl.when(pl.program_id(0) == 0)
        def _():
            dg_ref[...] = jnp.zeros_like(dg_ref)

        dproj_ref[:, :POOL_WIDTH] = du_ref[...]
        cos = cos_ref[...]
        sin = sin_ref[...]
        first = _first_half_mask(ts)
        for j in range(ATTN_WIDTH // 128):
            cols = slice(j * 128, (j + 1) * 128)
            for base, ref in ((POOL_WIDTH, dq_ref), (POOL_WIDTH + ATTN_WIDTH, dk_ref)):
                g = ref[:, cols].astype(f32)
                pre = g * cos + _rope_partner(g * sin, first)
                dproj_ref[:, base + j * 128: base + (j + 1) * 128] = pre.astype(bf16)
        dproj_ref[:, POOL_WIDTH + 2 * ATTN_WIDTH:] = dv_ref[...]

        dh = _dot(dproj_ref[...], w_ref[...])
        xv = x_ref[...]
        r = _rms(xv)
        xhat = xv * r
        dg_ref[...] += jnp.sum(dh * xhat, axis=0, keepdims=True)
        dhg = dh * g_ref[...]
        gx_ref[...] = dx2_ref[...] + r * (dhg - xhat * jnp.mean(dhg * xhat, axis=-1, keepdims=True))

    row = lambda w: pl.BlockSpec((ts, w), lambda i: (i, 0))
    gain = pl.BlockSpec((1, D_MODEL), lambda i: (0, 0))
    return pl.pallas_call(
        body, name="in_proj_bwd", grid=(S // ts,),
        in_specs=[row(POOL_WIDTH)] + [row(ATTN_WIDTH)] * 3 + [row(128), row(128),
                  pl.BlockSpec((IN_WIDTH, D_MODEL), lambda i: (0, 0)), row(D_MODEL), row(D_MODEL), gain],
        out_specs=[row(D_MODEL), row(IN_WIDTH), gain],
        out_shape=[jax.ShapeDtypeStruct((S, D_MODEL), f32), jax.ShapeDtypeStruct((S, IN_WIDTH), bf16),
                   jax.ShapeDtypeStruct((1, D_MODEL), f32)],
        compiler_params=_params("arbitrary"),
    )(du, dq, dk, dv, cos_t, sin_t, w_in, x, dx2, g1)


def _matmul_tiles_tn(a, b, name):
    T, K, w = a.shape
    N = b.shape[1]
    tk = 1024

    def body(a_ref, b_ref, o_ref):
        def tiles_pass(first):
            for t in range(T):
                part = _dot_tn(a_ref[t], b_ref[...])
                if first:
                    o_ref[t * w:(t + 1) * w, :] = part
                else:
                    o_ref[t * w:(t + 1) * w, :] += part

        @pl.when(pl.program_id(0) == 0)
        def _():
            tiles_pass(True)

        @pl.when(pl.program_id(0) > 0)
        def _():
            tiles_pass(False)

    return pl.pallas_call(
        body, name=name, grid=(K // tk,),
        in_specs=[pl.BlockSpec((T, tk, w), lambda k: (0, k, 0)), pl.BlockSpec((tk, N), lambda k: (k, 0))],
        out_specs=pl.BlockSpec((T * w, N), lambda k: (0, 0)),
        out_shape=jax.ShapeDtypeStruct((T * w, N), f32),
        compiler_params=_params("arbitrary"),
    )(a, b)


def _matmul_tn(a, b, tn, name):
    K, M = a.shape
    N = b.shape[1]
    tk = 1024

    def body(a_ref, b_ref, o_ref):
        part = _dot_tn(a_ref[...], b_ref[...])

        @pl.when(pl.program_id(1) == 0)
        def _():
            o_ref[...] = part

        @pl.when(pl.program_id(1) > 0)
        def _():
            o_ref[...] += part

    return pl.pallas_call(
        body, name=name, grid=(N // tn, K // tk),
        in_specs=[pl.BlockSpec((tk, M), lambda n, k: (k, 0)), pl.BlockSpec((tk, tn), lambda n, k: (k, n))],
        out_specs=pl.BlockSpec((M, tn), lambda n, k: (0, n)),
        out_shape=jax.ShapeDtypeStruct((M, N), f32),
        compiler_params=_params("parallel", "arbitrary"),
    )(a, b)


def _rope_tables(S):
    half = HEAD_DIM // 2
    freqs = ROPE_THETA ** (-jnp.arange(half, dtype=f32) * (2.0 / HEAD_DIM))
    ang = jnp.arange(S).astype(f32)[:, None] * freqs[None, :]
    cos = jnp.tile(jnp.cos(ang), (1, 4))
    sin = jnp.sin(ang)
    sin = jnp.tile(jnp.concatenate([-sin, sin], axis=1), (1, 2))
    return cos, sin


def _block_diag(w_pool):
    w = jnp.zeros((POOL_WIDTH, POOL_WIDTH), w_pool.dtype)
    for g in range(POOL_WIDTH // POOL_GROUP):
        w = lax.dynamic_update_slice(w, w_pool[g], (g * POOL_GROUP, g * POOL_GROUP))
    return w


def _head_ones():
    head = np.arange(ATTN_WIDTH) // HEAD_DIM
    return jnp.asarray(head[:, None] == head[None, :], dtype=bf16)


def _place():
    x, y, c = lax.axis_index("x"), lax.axis_index("y"), lax.axis_index("c")
    chips = [(1 - x, y), (x, 1 - y), (1 - x, 1 - y)]
    return x, y, c, chips


ANY = pl.BlockSpec(memory_space=pl.ANY)
N_PEER_CHIPS = N_CHIPS - 1
ICI_PIECES = 4
D2D_PIECES = 8
LOCAL_PIECES = 8


def _row_chunks(rows, n, unit=32):
    units = rows // unit
    out, start = [], 0
    for i in range(n):
        size = (units // n + (1 if i < units % n else 0)) * unit
        out.append((start, size))
        start += size
    return [piece for piece in out if piece[1]]


class _LocalCopy:
    def __init__(self, src_rows, dst_rows, rows, buf, sems_in, sems_out):
        self.loads, self.stores = [], []
        for i, (start, size) in enumerate(_row_chunks(rows, LOCAL_PIECES)):
            r = pl.ds(start, size)
            self.loads.append(pltpu.make_async_copy(src_rows(r), buf.at[r], sems_in.at[i]))
            self.stores.append(pltpu.make_async_copy(buf.at[r], dst_rows(r), sems_out.at[i]))

    def start(self):
        for cp in self.loads:
            cp.start()

    def pass_on(self):
        for load, store in zip(self.loads, self.stores):
            load.wait()
            store.start()

    def finish(self):
        for store in self.stores:
            store.wait()

    @staticmethod
    def scratch(rows, dtype):
        return [pltpu.VMEM((rows, D_MODEL), dtype), pltpu.SemaphoreType.DMA((LOCAL_PIECES,)),
                pltpu.SemaphoreType.DMA((LOCAL_PIECES,))]


class _Gather:
    def __init__(self, w_ref, out_ref, send1, recv1, send2, recv2, buf, sems_in, sems_out):
        x, y, c, chips = _place()
        me = 2 * x + y
        rows = w_ref.shape[0]
        half = rows // 2
        pieces = _row_chunks(half, ICI_PIECES)
        self.own = _LocalCopy(lambda r: w_ref.at[r], lambda r: out_ref.at[me, r], rows, buf, sems_in, sems_out)

        def rows_of(core, piece):
            start, size = piece
            return pl.ds(core * half + start, size)

        self.sends, self.arrivals, self.forwards, self.forward_arrivals = [], [], [], []
        for i, piece in enumerate(pieces):
            for j, (cx, cy) in enumerate(chips):
                k = j * len(pieces) + i
                there = 2 * cx + cy

                def direct(src_chip, cx=cx, cy=cy, k=k, piece=piece):
                    return pltpu.make_async_remote_copy(
                        src_ref=w_ref.at[rows_of(c, piece)], dst_ref=out_ref.at[src_chip, rows_of(c, piece)],
                        send_sem=send1.at[k], recv_sem=recv1.at[k], device_id=(cx, cy, c), device_id_type=MESH)

                def passed(core, there=there, k=k, piece=piece):
                    return pltpu.make_async_remote_copy(
                        src_ref=out_ref.at[there, rows_of(core, piece)], dst_ref=out_ref.at[there, rows_of(core, piece)],
                        send_sem=send2.at[k], recv_sem=recv2.at[k], device_id=(x, y, 1 - c), device_id_type=MESH)

                self.sends.append(direct(me))
                self.arrivals.append(direct(there))
                self.forwards.append(passed(c))
                self.forward_arrivals.append(passed(1 - c))

    def start(self):
        for cp in self.sends:
            cp.start()
        self.own.start()

    def pass_on(self):
        self.own.pass_on()
        for arrival, forward in zip(self.arrivals, self.forwards):
            arrival.wait_recv()
            forward.start()

    def finish(self):
        for arrival in self.forward_arrivals:
            arrival.wait_recv()
        for cp in self.sends + self.forwards:
            cp.wait_send()
        self.own.finish()

    @staticmethod
    def scratch(rows, dtype):
        n = N_PEER_CHIPS * len(_row_chunks(rows // 2, ICI_PIECES))
        return [pltpu.SemaphoreType.DMA((n,))] * 4 + _LocalCopy.scratch(rows, dtype)

    @staticmethod
    def out_shape(rows, dtype):
        return jax.ShapeDtypeStruct((N_CHIPS, rows, D_MODEL), dtype)


def _gather_weights(pack):
    rows = pack.shape[0]

    def body(w_ref, out_ref, *scratch):
        gather = _Gather(w_ref, out_ref, *scratch)
        gather.start()
        gather.pass_on()
        gather.finish()

    return pl.pallas_call(
        body, name="gather_weights", in_specs=[ANY], out_specs=ANY, out_shape=_Gather.out_shape(rows, pack.dtype),
        scratch_shapes=_Gather.scratch(rows, pack.dtype),
        compiler_params=pltpu.CompilerParams(vmem_limit_bytes=VMEM_LIMIT_V7X),
    )(pack)


class _Scatter:
    def __init__(self, h_ref, out_ref, send, recv):
        x, y, c, chips = _place()
        pieces = _row_chunks(h_ref.shape[1], ICI_PIECES)
        self.copies = []
        for i, (start, size) in enumerate(pieces):
            for j, (cx, cy) in enumerate(chips):
                k = j * len(pieces) + i
                self.copies.append(pltpu.make_async_remote_copy(
                    src_ref=h_ref.at[2 * cx + cy, pl.ds(start, size)], dst_ref=out_ref.at[j, pl.ds(start, size)],
                    send_sem=send.at[k], recv_sem=recv.at[k], device_id=(cx, cy, c), device_id_type=MESH))

    def start(self):
        for cp in self.copies:
            cp.start()

    def finish(self):
        for cp in self.copies:
            cp.wait_recv()
        for cp in self.copies:
            cp.wait_send()

    @staticmethod
    def scratch(half):
        n = N_PEER_CHIPS * len(_row_chunks(half, ICI_PIECES))
        return [pltpu.SemaphoreType.DMA((n,))] * 2

    @staticmethod
    def out_shape(half, dtype):
        return jax.ShapeDtypeStruct((N_PEER_CHIPS, half, D_MODEL), dtype)


def _scatter_to_chips(h):
    half = h.shape[1]

    def body(h_ref, out_ref, send, recv):
        scatter = _Scatter(h_ref, out_ref, send, recv)
        scatter.start()
        scatter.finish()

    return pl.pallas_call(
        body, name="scatter_to_chips", in_specs=[ANY], out_specs=ANY, out_shape=_Scatter.out_shape(half, h.dtype),
        scratch_shapes=_Scatter.scratch(half),
    )(h)


def _swap_halves(g, name):
    half = g.shape[1] // 2
    pieces = _row_chunks(half, D2D_PIECES)
    n = len(pieces)

    def body(g_ref, theirs_ref, send, recv):
        x, y, c, _ = _place()
        copies = []
        for s in range(N_CHIPS):
            for i, (start, size) in enumerate(pieces):
                copies.append(pltpu.make_async_remote_copy(
                    src_ref=g_ref.at[s, pl.ds((1 - c) * half + start, size)], dst_ref=theirs_ref.at[s, pl.ds(start, size)],
                    send_sem=send.at[s * n + i], recv_sem=recv.at[s * n + i],
                    device_id=(x, y, 1 - c), device_id_type=MESH))
        for cp in copies:
            cp.start()
        for cp in copies:
            cp.wait()

    return pl.pallas_call(
        body, name=name, in_specs=[ANY], out_specs=ANY,
        out_shape=jax.ShapeDtypeStruct((N_CHIPS, half, D_MODEL), g.dtype),
        scratch_shapes=[pltpu.SemaphoreType.DMA((N_CHIPS * n,))] * 2,
    )(g)


ADD_TILE_MAX_ROWS = 600


def _add_tile(half):
    return max(t for t in range(8, ADD_TILE_MAX_ROWS + 1, 8) if half % t == 0)


def _add_cores(g, theirs, name, out_dtype=f32):
    half = theirs.shape[1]
    tr = _add_tile(half)
    n_t = half // tr

    def body(c_ref, g_ref, t_ref, o_ref):
        o_ref[...] = (g_ref[...] + t_ref[...]).astype(out_dtype)

    blk = pl.BlockSpec((1, tr, D_MODEL), lambda s, t, c_ref: (s, t, 0))
    return pl.pallas_call(
        body, name=name,
        grid_spec=pltpu.PrefetchScalarGridSpec(
            num_scalar_prefetch=1, grid=(N_CHIPS, n_t),
            in_specs=[pl.BlockSpec((1, tr, D_MODEL), lambda s, t, c_ref: (s, c_ref[0] * n_t + t, 0)), blk],
            out_specs=blk),
        out_shape=jax.ShapeDtypeStruct(theirs.shape, out_dtype),
        compiler_params=_params("parallel", "parallel"),
    )(lax.axis_index("c").astype(jnp.int32).reshape(1), g, theirs)


def _add_chips(chip_sum, others, name):
    half = chip_sum.shape[1]
    tr = _add_tile(half)

    def body(me_ref, own_ref, o0, o1, o2, out_ref):
        out_ref[...] = ((own_ref[0].astype(f32) + o0[0].astype(f32)) + o1[0].astype(f32)) + o2[0].astype(f32)

    other = lambda j: pl.BlockSpec((1, tr, D_MODEL), lambda t, me_ref: (j, t, 0))
    return pl.pallas_call(
        body, name=name,
        grid_spec=pltpu.PrefetchScalarGridSpec(
            num_scalar_prefetch=1, grid=(half // tr,),
            in_specs=[pl.BlockSpec((1, tr, D_MODEL), lambda t, me_ref: (me_ref[0], t, 0)), other(0), other(1), other(2)],
            out_specs=pl.BlockSpec((tr, D_MODEL), lambda t, me_ref: (t, 0))),
        out_shape=jax.ShapeDtypeStruct((half, D_MODEL), f32),
        compiler_params=_params("parallel"),
    )((2 * lax.axis_index("x") + lax.axis_index("y")).astype(jnp.int32).reshape(1), chip_sum, others, others, others)


def _join_halves(r):
    half = r.shape[0]
    pieces = _row_chunks(half, 2 * D2D_PIECES)
    n = len(pieces)

    def body(r_ref, out_ref, send, recv, buf, sems_in, sems_out):
        x, y, c, _ = _place()
        own = _LocalCopy(lambda rr: r_ref.at[rr], lambda rr: out_ref.at[c, rr], half, buf, sems_in, sems_out)
        own.start()

        def piece(i, core):
            start, size = pieces[i]
            return pltpu.make_async_remote_copy(
                src_ref=r_ref.at[pl.ds(start, size)], dst_ref=out_ref.at[core, pl.ds(start, size)],
                send_sem=send.at[i], recv_sem=recv.at[i], device_id=(x, y, 1 - c), device_id_type=MESH)

        copies = [piece(i, c) for i in range(n)]
        for cp in copies:
            cp.start()
        own.pass_on()
        for i in range(n):
            piece(i, 1 - c).wait_recv()
        for cp in copies:
            cp.wait_send()
        own.finish()

    return pl.pallas_call(
        body, name="join_halves", in_specs=[ANY], out_specs=ANY,
        out_shape=jax.ShapeDtypeStruct((2,) + r.shape, r.dtype),
        scratch_shapes=[pltpu.SemaphoreType.DMA((n,))] * 2 + _LocalCopy.scratch(half, r.dtype),
        compiler_params=pltpu.CompilerParams(vmem_limit_bytes=VMEM_LIMIT_V7X),
    )(r)


def _sum_small(block):
    def body(b_ref, out_ref, gathered, send, recv):
        x, y, c, _ = _place()
        me = 4 * x + 2 * y + c
        gathered[me] = b_ref[...]
        sends = []
        for kk in range(1, N_DEV):
            flip = lambda v, bit: 1 - v if bit else v
            peer = (flip(x, kk & 4), flip(y, kk & 2), flip(c, kk & 1))
            cp = pltpu.make_async_remote_copy(
                src_ref=b_ref, dst_ref=gathered.at[me], send_sem=send.at[kk - 1], recv_sem=recv.at[kk - 1],
                device_id=peer, device_id_type=MESH)
            cp.start()
            sends.append(cp)
        for kk in range(1, N_DEV):
            peer_index = jnp.bitwise_xor(me, kk)
            pltpu.make_async_remote_copy(
                src_ref=b_ref, dst_ref=gathered.at[peer_index], send_sem=send.at[kk - 1], recv_sem=recv.at[kk - 1],
                device_id=(x, y, c), device_id_type=MESH).wait_recv()
        for cp in sends:
            cp.wait_send()
        acc = gathered[0]
        for dev in range(1, N_DEV):
            acc = acc + gathered[dev]
        out_ref[...] = acc

    vmem = pl.BlockSpec(memory_space=pltpu.VMEM)
    return pl.pallas_call(
        body, name="sum_small", in_specs=[vmem], out_specs=vmem,
        out_shape=jax.ShapeDtypeStruct(block.shape, block.dtype),
        scratch_shapes=[pltpu.VMEM((N_DEV,) + block.shape, block.dtype),
                        pltpu.SemaphoreType.DMA((N_DEV - 1,)), pltpu.SemaphoreType.DMA((N_DEV - 1,))],
    )(block)


def _adamw(w, g, m, v, name):
    rows, cols = w.shape
    tr = rows
    for cand in (512, 256, 128, 64, 32, 16, 8):
        if rows % cand == 0:
            tr = cand
            break
    c1 = 1.0 - ADAM_B1 ** ADAM_STEP
    c2 = 1.0 - ADAM_B2 ** ADAM_STEP

    def body(w_ref, g_ref, m_ref, v_ref, d_ref, nm_ref, nv_ref):
        gv = g_ref[...]
        nm = ADAM_B1 * m_ref[...] + (1.0 - ADAM_B1) * gv
        nv = ADAM_B2 * v_ref[...] + (1.0 - ADAM_B2) * (gv * gv)
        nm_ref[...] = nm
        nv_ref[...] = nv
        d_ref[...] = -ADAM_LR * ((nm / c1) / (jnp.sqrt(nv / c2) + ADAM_EPS) + ADAM_WD * w_ref[...])

    blk = pl.BlockSpec((tr, cols), lambda i: (i, 0))
    shape = jax.ShapeDtypeStruct((rows, cols), f32)
    return pl.pallas_call(
        body, name=name, grid=(rows // tr,), in_specs=[blk] * 4, out_specs=[blk] * 3, out_shape=[shape] * 3,
        compiler_params=_params("parallel"),
    )(w, g, m, v)


LARGE = ("w_in", "w_out", "w_gate", "w_up", "w_down")
SMALL = ("ln_pre_mix", "ln_post_mix", "ln_pre_ffn", "ln_post_ffn", "pool_scale", "w_pool")
SHARD_ROWS = {"w_in": 640, "w_out": 256, "w_gate": 704, "w_up": 704, "w_down": 704}
COLUMN_SHARDED = ("w_in", "w_gate", "w_up")
NEEDED_FIRST = ("w_in",)
NEEDED_LATER = ("w_out", "w_gate", "w_up", "w_down")
READY_EARLY = ("w_out", "w_gate", "w_up", "w_down")
READY_LATE = ("w_in",)


def _pack_shard(shards, names):
    return jnp.concatenate([shards[n].T if n in COLUMN_SHARDED else shards[n] for n in names], axis=0)


def _unpack_shard(pack, names):
    out, row = {}, 0
    for n in names:
        part = pack[row:row + SHARD_ROWS[n]]
        out[n] = part.T if n in COLUMN_SHARDED else part
        row += SHARD_ROWS[n]
    return out


def _whole_from_shards(packs, names):
    out, row = {}, 0
    for n in names:
        rows = SHARD_ROWS[n]
        out[n] = packs[:, row:row + rows].reshape(N_CHIPS * rows, D_MODEL)
        row += rows
    return out


def _shards_from_whole(grads, names):
    return jnp.concatenate([grads[n].reshape(N_CHIPS, SHARD_ROWS[n], D_MODEL) for n in names], axis=1)


def _pack_small(vals):
    rows = [vals[n].reshape(1, D_MODEL) for n in SMALL[:4]]
    rows.append(jnp.pad(vals["pool_scale"].reshape(1, POOL_WIDTH), ((0, 0), (0, D_MODEL - POOL_WIDTH))))
    rows.append(jnp.pad(vals["loss"].reshape(1, 1), ((0, 0), (0, D_MODEL - 1))))
    rows.append(jnp.zeros((2, D_MODEL), f32))
    rows.append(vals["w_pool"].reshape(16, D_MODEL))
    return jnp.concatenate(rows, axis=0)


def _unpack_small(block):
    out = {n: block[i:i + 1] for i, n in enumerate(SMALL[:4])}
    out["pool_scale"] = block[4:5, :POOL_WIDTH]
    out["loss"] = block[5, 0]
    out["w_pool"] = block[8:24].reshape(1, 4, POOL_GROUP, POOL_GROUP)
    return out


def kernel(x, ln_pre_mix, w_in, w_pool, pool_scale, w_out, ln_post_mix, ln_pre_ffn, w_gate, w_up, w_down, ln_post_ffn, loss_target, m_ln_pre_mix, m_w_in, m_w_pool, m_pool_scale, m_w_out, m_ln_post_mix, m_ln_pre_ffn, m_w_gate, m_w_up, m_w_down, m_ln_post_ffn, v_ln_pre_mix, v_w_in, v_w_pool, v_pool_scale, v_w_out, v_ln_post_mix, v_ln_pre_ffn, v_w_gate, v_w_up, v_w_down, v_ln_post_ffn):
    w = dict(ln_pre_mix=ln_pre_mix, w_in=w_in, w_pool=w_pool, pool_scale=pool_scale, w_out=w_out,
             ln_post_mix=ln_post_mix, ln_pre_ffn=ln_pre_ffn, w_gate=w_gate, w_up=w_up, w_down=w_down,
             ln_post_ffn=ln_post_ffn)
    m = dict(ln_pre_mix=m_ln_pre_mix, w_in=m_w_in, w_pool=m_w_pool, pool_scale=m_pool_scale, w_out=m_w_out,
             ln_post_mix=m_ln_post_mix, ln_pre_ffn=m_ln_pre_ffn, w_gate=m_w_gate, w_up=m_w_up, w_down=m_w_down,
             ln_post_ffn=m_ln_post_ffn)
    v = dict(ln_pre_mix=v_ln_pre_mix, w_in=v_w_in, w_pool=v_w_pool, pool_scale=v_pool_scale, w_out=v_w_out,
             ln_post_mix=v_ln_post_mix, ln_pre_ffn=v_ln_pre_ffn, w_gate=v_w_gate, w_up=v_w_up, w_down=v_w_down,
             ln_post_ffn=v_ln_post_ffn)

    xs, target = x[0], loss_target[0]
    cos_t, sin_t = _rope_tables(xs.shape[0])
    w_bd = _block_diag(w_pool[0]).astype(bf16)
    shard = {n: w[n][0].astype(bf16) for n in LARGE}

    w_in_whole = _whole_from_shards(_gather_weights(_pack_shard(shard, NEEDED_FIRST)), NEEDED_FIRST)["w_in"]
    h1, u, qs, ks, vs = _in_proj(xs, ln_pre_mix, w_in_whole, cos_t, sin_t)
    pool_out = _pool_fwd(u, w_bd, pool_scale)
    attn_out, lse, later = _attn_fwd(qs, ks, vs, _pack_shard(shard, NEEDED_LATER))
    whole = _whole_from_shards(later, NEEDED_LATER)
    mix, x2, h2 = _out_proj(pool_out, attn_out, whole["w_out"], xs, ln_post_mix, ln_pre_ffn)
    gate, up, f = _ffn_fwd(h2, whole["w_gate"], whole["w_up"], whole["w_down"])
    dy, df, dg4, loss = _loss_head(f, x2, target, ln_post_ffn)

    large = {}
    a, dgate, dup, dh2 = _ffn_bwd(df, gate, up, whole["w_gate"], whole["w_up"], whole["w_down"])
    large["w_down"] = _matmul_tiles_tn(a, df, "grad_w_down")
    large["w_gate"] = _matmul_tiles_tn(dgate, h2, "grad_w_gate")
    large["w_up"] = _matmul_tiles_tn(dup, h2, "grad_w_up")
    dx2, dmix, dg3, dg2 = _norm_bwd(dh2, dy, x2, mix, ln_pre_ffn, ln_post_mix)
    large["w_out"] = jnp.concatenate([_matmul_tn(pool_out, dmix, D_MODEL, "grad_w_out_pool"),
                                      _matmul_tn(attn_out, dmix, D_MODEL, "grad_w_out_attn")], axis=0)
    early = _shards_from_whole(large, READY_EARLY)
    early_chip = _add_cores(early, _swap_halves(early, "swap_halves_early"), "add_cores_early")
    dpool, delta, dos = _out_proj_bwd(dmix, whole["w_out"], attn_out, _head_ones())
    du, d_w_bd, d_scale = _pool_bwd(u, dpool, w_bd, pool_scale)
    dq, dk, dv, early_others = _attn_bwd(qs, ks, vs, dos, lse, delta, early_chip)
    grad_x, dproj, dg1 = _in_proj_bwd(du, dq, dk, dv, cos_t, sin_t, w_in_whole, xs, dx2, ln_pre_mix)
    large["w_in"] = _matmul_tn(dproj, h1, D_MODEL, "grad_w_in")
    late = _shards_from_whole(large, READY_LATE)
    late_chip = _add_cores(late, _swap_halves(late, "swap_halves_late"), "add_cores_late", bf16)
    late_others = _scatter_to_chips(late_chip)
    early_half = _add_chips(early_chip, early_others, "add_chips_early")
    late_half = _add_chips(late_chip, late_others, "add_chips_late")
    joined = _join_halves(jnp.concatenate([early_half, late_half], axis=0))
    n_early = early_half.shape[0]
    grads = _unpack_shard(joined[:, :n_early].reshape(-1, D_MODEL), READY_EARLY)
    grads.update(_unpack_shard(joined[:, n_early:].reshape(-1, D_MODEL), READY_LATE))

    d_w_pool = jnp.stack([d_w_bd[g * POOL_GROUP:(g + 1) * POOL_GROUP, g * POOL_GROUP:(g + 1) * POOL_GROUP]
                          for g in range(POOL_WIDTH // POOL_GROUP)])
    small = dict(ln_pre_mix=dg1, ln_post_mix=dg2, ln_pre_ffn=dg3, ln_post_ffn=dg4, pool_scale=d_scale, w_pool=d_w_pool)
    total = _unpack_small(_sum_small(_pack_small(dict(small, loss=loss))))
    for n in SMALL:
        grads[n] = total[n]

    delta_w, new_m, new_v = {}, {}, {}
    for n in LARGE:
        delta_w[n], new_m[n], new_v[n] = _adamw(w[n][0], grads[n], m[n][0], v[n][0], "adamw_" + n)
    small_state = [_pack_small(dict({n: s[n] for n in SMALL}, loss=jnp.zeros((), f32))) for s in (w, m, v)]
    small_grad = _pack_small(dict({n: grads[n] for n in SMALL}, loss=jnp.zeros((), f32)))
    sd, sm, sv = _adamw(small_state[0], small_grad, small_state[1], small_state[2], "adamw_small")
    for out, block in ((delta_w, sd), (new_m, sm), (new_v, sv)):
        un = _unpack_small(block)
        for n in SMALL:
            out[n] = un[n]

    names = ("ln_pre_mix", "w_in", "w_pool", "pool_scale", "w_out", "ln_post_mix", "ln_pre_ffn", "w_gate", "w_up",
             "w_down", "ln_post_ffn")
    full = lambda d: [d[n].reshape(w[n].shape) for n in names]
    return (total["loss"], grad_x[None], *full(grads), *full(delta_w), *full(new_m), *full(new_v))
```

```python
import numpy as np
import jax
import jax.numpy as jnp
from jax import lax
from jax.experimental import pallas as pl
from jax.experimental.pallas import tpu as pltpu

D_MODEL = 1024
POOL_WIDTH = 256
POOL_GROUP = 64
ATTN_WIDTH = 768
HEAD_DIM = 64
IN_WIDTH = 2560
D_FF = 2816
BLOCK = 128
DILATIONS = (1, 4, 16)
ROPE_THETA = 10000.0
EPS = 1e-6
ATTN_SCALE = 0.125
NEG = -1e30

ADAM_LR = 0.001
ADAM_B1 = 0.9
ADAM_B2 = 0.999
ADAM_EPS = 1e-08
ADAM_WD = 0.01
ADAM_STEP = 10

N_CHIPS = 4
N_DEV = 8
VMEM_LIMIT_V7X = 56 * 1024 * 1024
MESH = pl.DeviceIdType.MESH

f32 = jnp.float32
bf16 = jnp.bfloat16


def _params(*sem):
    return pltpu.CompilerParams(dimension_semantics=sem, vmem_limit_bytes=VMEM_LIMIT_V7X)


def _dot(a, b):
    return jnp.dot(a, b, preferred_element_type=f32)


def _dot_nt(a, b):
    return lax.dot_general(a, b, (((1,), (1,)), ((), ())), preferred_element_type=f32)


def _dot_tn(a, b):
    return lax.dot_general(a, b, (((0,), (0,)), ((), ())), preferred_element_type=f32)


def _rope_partner(a, first_half):
    return jnp.where(first_half, pltpu.roll(a, 96, 1), pltpu.roll(a, 32, 1))


def _first_half_mask(rows):
    lane = lax.broadcasted_iota(jnp.int32, (rows, 128), 1)
    return (lane % HEAD_DIM) < (HEAD_DIM // 2)


def _stream_spec(d, ts):
    return pl.BlockSpec((d, ts // d, ATTN_WIDTH), lambda i: (0, i, 0))


def _stream_shape(S, d):
    return jax.ShapeDtypeStruct((d, S // d, ATTN_WIDTH), bf16)


N_STAGE = ATTN_WIDTH // 128


def _stage_scratch(ts):
    return [pltpu.VMEM((ts, 128), f32)] * N_STAGE


def _store_streams(stage, out_refs, ts):
    for d, ref in zip(DILATIONS, out_refs):
        for r in range(d):
            rows = pl.ds(0, ts) if d == 1 else pl.ds(r, ts // d, stride=d)
            for j in range(N_STAGE):
                ref[r, :, j * 128:(j + 1) * 128] = stage[j][rows, :].astype(bf16)


def _in_proj(x, g1, w_in, cos_t, sin_t):
    S = x.shape[0]
    ts = 512

    def body(x_ref, g_ref, w_ref, cos_ref, sin_ref, h_ref, u_ref, *rest):
        outs, stage = rest[:-N_STAGE], rest[-N_STAGE:]
        xv = x_ref[...]
        r = lax.rsqrt(jnp.mean(xv * xv, axis=-1, keepdims=True) + EPS)
        h = ((xv * r) * g_ref[...]).astype(bf16)
        h_ref[...] = h
        proj = _dot_nt(h, w_ref[...])
        u_ref[...] = proj[:, :POOL_WIDTH]
        cos = cos_ref[...]
        sin = sin_ref[...]
        first = _first_half_mask(ts)
        n_dil = len(DILATIONS)
        for which, base in enumerate((POOL_WIDTH, POOL_WIDTH + ATTN_WIDTH)):
            for j in range(ATTN_WIDTH // 128):
                a = proj[:, base + j * 128: base + (j + 1) * 128]
                if which == 0:
                    a = a * ATTN_SCALE
                stage[j][...] = a * cos + _rope_partner(a, first) * sin
            _store_streams(stage, outs[which * n_dil:(which + 1) * n_dil], ts)
        for j in range(ATTN_WIDTH // 128):
            base = POOL_WIDTH + 2 * ATTN_WIDTH + j * 128
            stage[j][...] = proj[:, base:base + 128]
        _store_streams(stage, outs[2 * n_dil:], ts)

    row = lambda w: pl.BlockSpec((ts, w), lambda i: (i, 0))
    streams = [_stream_spec(d, ts) for d in DILATIONS] * 3
    res = pl.pallas_call(
        body, name="in_proj", grid=(S // ts,),
        in_specs=[row(D_MODEL), pl.BlockSpec((1, D_MODEL), lambda i: (0, 0)),
                  pl.BlockSpec((IN_WIDTH, D_MODEL), lambda i: (0, 0)), row(128), row(128)],
        out_specs=[row(D_MODEL), row(POOL_WIDTH)] + streams,
        out_shape=[jax.ShapeDtypeStruct((S, D_MODEL), bf16), jax.ShapeDtypeStruct((S, POOL_WIDTH), f32)]
        + [_stream_shape(S, d) for d in DILATIONS] * 3,
        scratch_shapes=_stage_scratch(ts),
        compiler_params=_params("parallel"),
    )(x, g1, w_in, cos_t, sin_t)
    n = len(DILATIONS)
    return res[0], res[1], res[2:2 + n], res[2 + n:2 + 2 * n], res[2 + 2 * n:]


POOL_HALO = 16


def _pool_lane_group(rows):
    return lax.broadcasted_iota(jnp.int32, (rows, POOL_WIDTH), 1) // POOL_GROUP


def _pool_select(group, s2, s4, s8, s16):
    return jnp.where(group == 0, s2, jnp.where(group == 1, s4, jnp.where(group == 2, s8, s16)))


def _pool_count(t0, rows):
    group = _pool_lane_group(rows)
    t = t0 + lax.broadcasted_iota(jnp.int32, (rows, POOL_WIDTH), 0)
    win = _pool_select(group, 2, 4, 8, 16)
    return jnp.minimum(t + 1, win).astype(f32)


def _pool_diff(u_halo, u_tile, t0):
    ts = u_tile.shape[0]
    ext = jnp.concatenate([u_halo, u_tile], axis=0)
    s2 = ext + pltpu.roll(ext, 1, 0)
    s4 = s2 + pltpu.roll(s2, 2, 0)
    s8 = s4 + pltpu.roll(s4, 4, 0)
    s16 = s8 + pltpu.roll(s8, 8, 0)
    group = _pool_lane_group(ts + POOL_HALO)
    wsum = _pool_select(group, s2, s4, s8, s16)[POOL_HALO:]
    return wsum / _pool_count(t0, ts) - u_tile


def _pool_specs(ts, n_tiles):
    tile = pl.BlockSpec((ts, POOL_WIDTH), lambda i: (i, 0))
    per = ts // POOL_HALO
    before = pl.BlockSpec((POOL_HALO, POOL_WIDTH), lambda i: (jnp.maximum(i * per - 1, 0), 0))
    after = pl.BlockSpec((POOL_HALO, POOL_WIDTH), lambda i: (jnp.minimum((i + 1) * per, n_tiles * per - 1), 0))
    return tile, before, after


def _pool_fwd(u, w_bd, scale):
    S = u.shape[0]
    ts = 512
    n_tiles = S // ts

    def body(u_ref, halo_ref, w_ref, sc_ref, y_ref):
        i = pl.program_id(0)
        halo = jnp.where(i > 0, halo_ref[...], 0.0)
        d = _pool_diff(halo, u_ref[...], i * ts)
        y_ref[...] = (_dot(d.astype(bf16), w_ref[...]) * sc_ref[...]).astype(bf16)

    tile, before, _ = _pool_specs(ts, n_tiles)
    return pl.pallas_call(
        body, name="pool_fwd", grid=(n_tiles,),
        in_specs=[tile, before, pl.BlockSpec((POOL_WIDTH, POOL_WIDTH), lambda i: (0, 0)),
                  pl.BlockSpec((1, POOL_WIDTH), lambda i: (0, 0))],
        out_specs=tile, out_shape=jax.ShapeDtypeStruct((S, POOL_WIDTH), bf16),
        compiler_params=_params("parallel"),
    )(u, u, w_bd, scale)


def _pool_bwd(u, dy, w_bd, scale):
    S = u.shape[0]
    ts = 512
    n_tiles = S // ts

    def body(u_ref, halo_ref, dy_ref, dy_next_ref, w_ref, sc_ref, du_ref, dw_ref, dsc_ref):
        i = pl.program_id(0)

        @pl.when(i == 0)
        def _():
            dw_ref[...] = jnp.zeros_like(dw_ref)
            dsc_ref[...] = jnp.zeros_like(dsc_ref)

        halo = jnp.where(i > 0, halo_ref[...], 0.0)
        d = _pool_diff(halo, u_ref[...], i * ts).astype(bf16)
        w = w_ref[...]
        sc = sc_ref[...]
        dy_tile = dy_ref[...]
        z = _dot(d, w)
        dsc_ref[...] += jnp.sum(dy_tile * z, axis=0, keepdims=True)
        dy_next = jnp.where(i < n_tiles - 1, dy_next_ref[...], 0.0)
        dz = (jnp.concatenate([dy_tile, dy_next], axis=0) * sc).astype(bf16)
        dw_ref[...] += _dot_tn(d, dz[:ts])
        dd = _dot_nt(dz, w)
        e = dd / _pool_count(i * ts, ts + POOL_HALO)
        n = ts + POOL_HALO
        f2 = e + pltpu.roll(e, n - 1, 0)
        f4 = f2 + pltpu.roll(f2, n - 2, 0)
        f8 = f4 + pltpu.roll(f4, n - 4, 0)
        f16 = f8 + pltpu.roll(f8, n - 8, 0)
        fsum = _pool_select(_pool_lane_group(n), f2, f4, f8, f16)
        du_ref[...] = (fsum[:ts] - dd[:ts]).astype(bf16)

    tile, before, after = _pool_specs(ts, n_tiles)
    return pl.pallas_call(
        body, name="pool_bwd", grid=(n_tiles,),
        in_specs=[tile, before, tile, after, pl.BlockSpec((POOL_WIDTH, POOL_WIDTH), lambda i: (0, 0)),
                  pl.BlockSpec((1, POOL_WIDTH), lambda i: (0, 0))],
        out_specs=[tile, pl.BlockSpec((POOL_WIDTH, POOL_WIDTH), lambda i: (0, 0)),
                   pl.BlockSpec((1, POOL_WIDTH), lambda i: (0, 0))],
        out_shape=[jax.ShapeDtypeStruct((S, POOL_WIDTH), bf16), jax.ShapeDtypeStruct((POOL_WIDTH, POOL_WIDTH), f32),
                   jax.ShapeDtypeStruct((1, POOL_WIDTH), f32)],
        compiler_params=_params("arbitrary"),
    )(u, u, dy, dy, w_bd, scale)


SUPER = BLOCK * DILATIONS[-1]
UNITS = SUPER // BLOCK
FWD_UNROLL = 16
BWD_UNROLL = 8


def _band_mask(has_prev):
    qi = lax.broadcasted_iota(jnp.int32, (BLOCK, 2 * BLOCK), 0)
    kj = lax.broadcasted_iota(jnp.int32, (BLOCK, 2 * BLOCK), 1)
    return (kj >= qi) & (kj <= qi + BLOCK) & ((kj >= BLOCK) | has_prev)


def _head0_mask(rows=BLOCK):
    return lax.broadcasted_iota(jnp.int32, (rows, 128), 1) < HEAD_DIM


def _band_mask_t(has_prev):
    ki = lax.broadcasted_iota(jnp.int32, (2 * BLOCK, 2 * BLOCK), 0)
    qj = lax.broadcasted_iota(jnp.int32, (2 * BLOCK, 2 * BLOCK), 1) % BLOCK
    return (ki >= qj) & (ki <= qj + BLOCK) & ((ki >= BLOCK) | has_prev)


def _head_pair_rows(a, h0):
    zero = jnp.zeros_like(a)
    return jnp.concatenate([jnp.where(h0, a, zero), jnp.where(h0, zero, a)], axis=0)


def _per_query_row(stat):
    t = stat.T
    return jnp.concatenate([jnp.concatenate([t[:HEAD_DIM]] * 4, axis=0), jnp.concatenate([t[HEAD_DIM:]] * 4, axis=0)],
                           axis=1)


def _natural_rows(d, r, n):
    if d == 1:
        return pl.ds(pl.multiple_of(n * BLOCK, BLOCK), BLOCK)
    return pl.ds(n * (BLOCK * d) + r, BLOCK, stride=d)


def _unit_place(d, u):
    per_stream = UNITS // d
    return u // per_stream, u % per_stream, per_stream


def _block_rows(n):
    return pl.ds(pl.multiple_of(n * BLOCK, BLOCK), BLOCK)


def _band(cur_ref, tail_ref, r, n):
    before = jnp.where(n > 0, cur_ref[r, _block_rows(jnp.maximum(n - 1, 0)), :], tail_ref[r])
    return jnp.concatenate([before, cur_ref[r, _block_rows(n), :]], axis=0)


def _attn_in_specs(S, with_do):
    specs = []
    last = S // SUPER - 1
    for d in DILATIONS:
        per_stream = UNITS // d
        cur = pl.BlockSpec((d, SUPER // d, 128), lambda hp, sb: (0, jnp.minimum(sb, last), hp))
        tail = pl.BlockSpec(
            (d, BLOCK, 128),
            lambda hp, sb, per_stream=per_stream: (0, jnp.maximum(jnp.minimum(sb, last) * per_stream - 1, 0), hp))
        specs += [cur] * (2 if with_do else 1) + [cur, tail, cur, tail]
    return specs


def _attn_fwd(qs, ks, vs, pack):
    S = qs[0].shape[1]
    n_dil = len(DILATIONS)
    n_steps = S // SUPER
    n_total = (ATTN_WIDTH // 128) * n_steps

    def body(*refs):
        ins, pack_ref = refs[:5 * n_dil], refs[5 * n_dil]
        out_ref, lse_ref, gathered_ref = refs[5 * n_dil + 1:5 * n_dil + 4]
        scratch = refs[5 * n_dil + 4:]
        o_sc, l_sc = scratch[:n_dil], scratch[n_dil:2 * n_dil]
        gather = _Gather(pack_ref, gathered_ref, *scratch[2 * n_dil:])
        sb = pl.program_id(1)
        step = pl.program_id(0) * n_steps + sb

        @pl.when(step == 0)
        def _():
            gather.start()

        h0 = _head0_mask()
        for ci, d in enumerate(DILATIONS):
            q_ref, kc_ref, kp_ref, vc_ref, vp_ref = ins[5 * ci:5 * ci + 5]

            def unit(u, carry, d=d, ci=ci, q_ref=q_ref, kc_ref=kc_ref, kp_ref=kp_ref, vc_ref=vc_ref, vp_ref=vp_ref):
                r, n, _ = _unit_place(d, u)
                qv = q_ref[r, _block_rows(n), :]
                kb = _band(kc_ref, kp_ref, r, n)
                vb = _band(vc_ref, vp_ref, r, n)
                valid = _band_mask((sb > 0) | (n > 0))
                outs, lses = [], []
                for h in range(2):
                    keep = h0 if h == 0 else jnp.logical_not(h0)
                    qh = jnp.where(keep, qv, jnp.zeros_like(qv))
                    s = jnp.where(valid, _dot_nt(qh, kb), NEG)
                    m = jnp.max(s, axis=1, keepdims=True)
                    e = jnp.exp(s - m)
                    den = jnp.sum(e, axis=1, keepdims=True)
                    outs.append(_dot(e.astype(bf16), vb) * (1.0 / den))
                    lses.append(jnp.broadcast_to(m + jnp.log(den), (BLOCK, 128)))
                rows = _natural_rows(d, r, n)
                o_sc[ci][rows, :] = jnp.where(h0, outs[0], outs[1])
                l_sc[ci][rows, :] = jnp.where(h0, lses[0], lses[1])
                return carry

            lax.fori_loop(0, UNITS, unit, 0, unroll=FWD_UNROLL)

        def merge(t, carry):
            rows = pl.ds(pl.multiple_of(t * 256, 256), 256)
            a, b, c = l_sc[0][rows, :], l_sc[1][rows, :], l_sc[2][rows, :]
            m = jnp.maximum(jnp.maximum(a, b), c)
            ea, eb, ec = jnp.exp(a - m), jnp.exp(b - m), jnp.exp(c - m)
            tot = ea + eb + ec
            out_ref[rows, :] = ((ea / tot) * o_sc[0][rows, :] + (eb / tot) * o_sc[1][rows, :]
                                + (ec / tot) * o_sc[2][rows, :]).astype(bf16)
            lse_ref[rows, :] = m + jnp.log(tot)
            return carry

        lax.fori_loop(0, SUPER // 256, merge, 0)

        @pl.when(step == n_total // 2)
        def _():
            gather.pass_on()

        @pl.when(step == n_total - 1)
        def _():
            gather.finish()

    args = []
    for q, k, v in zip(qs, ks, vs):
        args += [q, k, k, v, v]
    nat = pl.BlockSpec((SUPER, 128), lambda hp, sb: (sb, hp))
    rows = pack.shape[0]
    return pl.pallas_call(
        body, name="attn_fwd", grid=(ATTN_WIDTH // 128, n_steps),
        in_specs=_attn_in_specs(S, False) + [ANY], out_specs=[nat, nat, ANY],
        out_shape=[jax.ShapeDtypeStruct((S, ATTN_WIDTH), bf16), jax.ShapeDtypeStruct((S, ATTN_WIDTH), f32),
                   _Gather.out_shape(rows, pack.dtype)],
        scratch_shapes=[pltpu.VMEM((SUPER, 128), f32)] * (2 * n_dil) + _Gather.scratch(rows, pack.dtype),
        compiler_params=_params("arbitrary", "arbitrary"),
    )(*args, pack)


def _attn_bwd(qs, ks, vs, dos, lse, delta, chip_sum):
    S = qs[0].shape[1]
    n_steps = S // SUPER
    last = n_steps - 1
    n_dil = len(DILATIONS)
    n_total = (ATTN_WIDTH // 128) * (n_steps + 1)

    def body(*refs):
        ins, (lse_ref, dl_ref, sum_ref) = refs[:6 * n_dil], refs[6 * n_dil:6 * n_dil + 3]
        dq_ref, dk_ref, dv_ref, others_ref = refs[6 * n_dil + 3:6 * n_dil + 7]
        dq_acc, dk_acc, dv_acc = refs[6 * n_dil + 7:6 * n_dil + 10]
        scatter = _Scatter(sum_ref, others_ref, *refs[6 * n_dil + 10:])
        sb = pl.program_id(1)
        step = pl.program_id(0) * (n_steps + 1) + sb
        cur = sb % 2
        prv = 1 - cur

        @pl.when(step == 0)
        def _():
            scatter.start()

        @pl.when(sb < n_steps)
        def _():
            dq_acc[...] = jnp.zeros_like(dq_acc)
            dk_acc[cur] = jnp.zeros((SUPER, 128), f32)
            dv_acc[cur] = jnp.zeros((SUPER, 128), f32)
            h0 = _head0_mask()
            for ci, d in enumerate(DILATIONS):
                q_ref, do_ref, kc_ref, kp_ref, vc_ref, vp_ref = ins[6 * ci:6 * ci + 6]

                def unit(u, carry, d=d, q_ref=q_ref, do_ref=do_ref, kc_ref=kc_ref, kp_ref=kp_ref, vc_ref=vc_ref,
                         vp_ref=vp_ref):
                    r, n, per_stream = _unit_place(d, u)
                    qv = q_ref[r, _block_rows(n), :]
                    dov = do_ref[r, _block_rows(n), :]
                    kb = _band(kc_ref, kp_ref, r, n)
                    vb = _band(vc_ref, vp_ref, r, n)
                    rows = _natural_rows(d, r, n)
                    has_prev = (sb > 0) | (n > 0)
                    q_pair = _head_pair_rows(qv, h0)
                    do_pair = _head_pair_rows(dov, h0)
                    s_t = jnp.where(_band_mask_t(has_prev), _dot_nt(kb, q_pair), NEG)
                    p_t = jnp.exp(s_t - _per_query_row(lse_ref[rows, :]))
                    dp_t = _dot_nt(vb, do_pair)
                    ds_t = (p_t * (dp_t - _per_query_row(dl_ref[rows, :]))).astype(bf16)
                    dvb = _dot(p_t.astype(bf16), do_pair)
                    dkb = _dot(ds_t, q_pair)
                    dq_pair = _dot_tn(ds_t, kb)
                    dq_acc[rows, :] += jnp.where(h0, dq_pair[:BLOCK], dq_pair[BLOCK:])
                    dk_acc[cur, rows, :] += dkb[BLOCK:]
                    dv_acc[cur, rows, :] += dvb[BLOCK:]

                    slot = jnp.where((n > 0) | (sb == 0), cur, prv)
                    before = _natural_rows(d, r, jnp.where(n > 0, n - 1, per_stream - 1))
                    dk_acc[slot, before, :] += dkb[:BLOCK]
                    dv_acc[slot, before, :] += dvb[:BLOCK]
                    return carry

                lax.fori_loop(0, UNITS, unit, 0, unroll=BWD_UNROLL)
            dq_ref[...] = (dq_acc[...] * ATTN_SCALE).astype(bf16)

        @pl.when(sb > 0)
        def _():
            dk_ref[...] = dk_acc[prv].astype(bf16)
            dv_ref[...] = dv_acc[prv].astype(bf16)

        @pl.when(step == n_total - 1)
        def _():
            scatter.finish()

    args = []
    for q, k, v, do in zip(qs, ks, vs, dos):
        args += [q, do, k, k, v, v]
    nat = pl.BlockSpec((SUPER, 128), lambda hp, sb: (jnp.minimum(sb, last), hp))
    nat_before = pl.BlockSpec((SUPER, 128), lambda hp, sb: (jnp.clip(sb - 1, 0, last), hp))
    out = jax.ShapeDtypeStruct((S, ATTN_WIDTH), bf16)
    half = chip_sum.shape[1]
    return pl.pallas_call(
        body, name="attn_bwd", grid=(ATTN_WIDTH // 128, n_steps + 1),
        in_specs=_attn_in_specs(S, True) + [nat, nat, ANY], out_specs=[nat, nat_before, nat_before, ANY],
        out_shape=[out, out, out, _Scatter.out_shape(half, chip_sum.dtype)],
        scratch_shapes=[pltpu.VMEM((SUPER, 128), f32), pltpu.VMEM((2, SUPER, 128), f32),
                        pltpu.VMEM((2, SUPER, 128), f32)] + _Scatter.scratch(half),
        compiler_params=_params("arbitrary", "arbitrary"),
    )(*args, lse, delta, chip_sum)


def _rms(v):
    return lax.rsqrt(jnp.mean(v * v, axis=-1, keepdims=True) + EPS)


def _out_proj(pool_out, attn_out, w_out, x, g2, g3):
    S = x.shape[0]
    ts = 512

    def body(p_ref, a_ref, w_ref, x_ref, g2_ref, g3_ref, mix_ref, x2_ref, h2_ref):
        mix = _dot(p_ref[...], w_ref[:POOL_WIDTH, :]) + _dot(a_ref[...], w_ref[POOL_WIDTH:, :])
        mix_ref[...] = mix
        x2 = x_ref[...] + (mix * _rms(mix)) * g2_ref[...]
        x2_ref[...] = x2
        h2_ref[...] = ((x2 * _rms(x2)) * g3_ref[...]).astype(bf16)

    row = lambda w: pl.BlockSpec((ts, w), lambda i: (i, 0))
    gain = pl.BlockSpec((1, D_MODEL), lambda i: (0, 0))
    return pl.pallas_call(
        body, name="out_proj", grid=(S // ts,),
        in_specs=[row(POOL_WIDTH), row(ATTN_WIDTH), pl.BlockSpec((D_MODEL, D_MODEL), lambda i: (0, 0)),
                  row(D_MODEL), gain, gain],
        out_specs=[row(D_MODEL)] * 3,
        out_shape=[jax.ShapeDtypeStruct((S, D_MODEL), f32), jax.ShapeDtypeStruct((S, D_MODEL), f32),
                   jax.ShapeDtypeStruct((S, D_MODEL), bf16)],
        compiler_params=_params("parallel"),
    )(pool_out, attn_out, w_out, x, g2, g3)


FF_TILE = 256
FF_STEP_ROWS = 2048
FF_ROWS = 256


def _sigmoid(g):
    return 1.0 / (1.0 + jnp.exp(-g))


def _ff_act_shape(S):
    return jax.ShapeDtypeStruct((D_FF // FF_TILE, S, FF_TILE), bf16)


def _ff_act_spec(ts):
    return pl.BlockSpec((1, ts, FF_TILE), lambda i, j: (j, i, 0))


def _ffn_fwd(h2, w_gate, w_up, w_down):
    S = h2.shape[0]
    ts = min(S, FF_STEP_ROWS)

    def body(h_ref, wg_ref, wu_ref, wd_ref, gate_ref, up_ref, f_ref):
        def rows_pass(first):
            def sub(i, carry):
                rows = pl.ds(pl.multiple_of(i * FF_ROWS, FF_ROWS), FF_ROWS)
                h = h_ref[rows, :]
                gate = _dot_nt(h, wg_ref[...])
                up = _dot_nt(h, wu_ref[...])
                gate_ref[0, rows, :] = gate.astype(bf16)
                up_ref[0, rows, :] = up.astype(bf16)
                part = _dot((gate * _sigmoid(gate) * up).astype(bf16), wd_ref[...])
                if first:
                    f_ref[rows, :] = part
                else:
                    f_ref[rows, :] += part
                return carry

            lax.fori_loop(0, ts // FF_ROWS, sub, 0, unroll=True)

        @pl.when(pl.program_id(1) == 0)
        def _():
            rows_pass(True)

        @pl.when(pl.program_id(1) > 0)
        def _():
            rows_pass(False)

    act = _ff_act_spec(ts)
    return pl.pallas_call(
        body, name="ffn_fwd", grid=(S // ts, D_FF // FF_TILE),
        in_specs=[pl.BlockSpec((ts, D_MODEL), lambda i, j: (i, 0)),
                  pl.BlockSpec((FF_TILE, D_MODEL), lambda i, j: (j, 0)),
                  pl.BlockSpec((FF_TILE, D_MODEL), lambda i, j: (j, 0)),
                  pl.BlockSpec((FF_TILE, D_MODEL), lambda i, j: (j, 0))],
        out_specs=[act, act, pl.BlockSpec((ts, D_MODEL), lambda i, j: (i, 0))],
        out_shape=[_ff_act_shape(S), _ff_act_shape(S), jax.ShapeDtypeStruct((S, D_MODEL), f32)],
        compiler_params=_params("parallel", "arbitrary"),
    )(h2, w_gate, w_up, w_down)


def _loss_head(f, x2, target, g4):
    S = f.shape[0]
    ts = 512

    def body(f_ref, x2_ref, t_ref, g_ref, dy_ref, df_ref, dg_ref, loss_ref):
        @pl.when(pl.program_id(0) == 0)
        def _():
            dg_ref[...] = jnp.zeros_like(dg_ref)
            loss_ref[...] = jnp.zeros_like(loss_ref)

        fv = f_ref[...]
        g = g_ref[...]
        r = _rms(fv)
        fhat = fv * r
        err = (x2_ref[...] + fhat * g) - t_ref[...]
        loss_ref[...] += 0.5 * jnp.sum(jnp.mean(err * err, axis=-1, keepdims=True), axis=0, keepdims=True)
        dy = err * (1.0 / D_MODEL)
        dy_ref[...] = dy
        dg_ref[...] += jnp.sum(dy * fhat, axis=0, keepdims=True)
        dyg = dy * g
        df_ref[...] = (r * (dyg - fhat * jnp.mean(dyg * fhat, axis=-1, keepdims=True))).astype(bf16)

    row = pl.BlockSpec((ts, D_MODEL), lambda i: (i, 0))
    gain = pl.BlockSpec((1, D_MODEL), lambda i: (0, 0))
    return pl.pallas_call(
        body, name="loss_head", grid=(S // ts,), in_specs=[row, row, row, gain],
        out_specs=[row, row, gain, pl.BlockSpec((1, 1), lambda i: (0, 0))],
        out_shape=[jax.ShapeDtypeStruct((S, D_MODEL), f32), jax.ShapeDtypeStruct((S, D_MODEL), bf16),
                   jax.ShapeDtypeStruct((1, D_MODEL), f32), jax.ShapeDtypeStruct((1, 1), f32)],
        compiler_params=_params("arbitrary"),
    )(f, x2, target, g4)


def _ffn_bwd(df, gate, up, w_gate, w_up, w_down):
    S = df.shape[0]
    ts = min(S, FF_STEP_ROWS)

    def body(df_ref, gate_ref, up_ref, wg_ref, wu_ref, wd_ref, a_ref, dgate_ref, dup_ref, dh_ref):
        def rows_pass(first):
            def sub(i, carry):
                rows = pl.ds(pl.multiple_of(i * FF_ROWS, FF_ROWS), FF_ROWS)
                da = _dot_nt(df_ref[rows, :], wd_ref[...])
                g = gate_ref[0, rows, :].astype(f32)
                u = up_ref[0, rows, :].astype(f32)
                sig = _sigmoid(g)
                silu = g * sig
                a_ref[0, rows, :] = (silu * u).astype(bf16)
                dup = (da * silu).astype(bf16)
                dgate = (da * u * (sig * (1.0 + g * (1.0 - sig)))).astype(bf16)
                dup_ref[0, rows, :] = dup
                dgate_ref[0, rows, :] = dgate
                part = _dot(dgate, wg_ref[...]) + _dot(dup, wu_ref[...])
                if first:
                    dh_ref[rows, :] = part
                else:
                    dh_ref[rows, :] += part
                return carry

            lax.fori_loop(0, ts // FF_ROWS, sub, 0, unroll=True)

        @pl.when(pl.program_id(1) == 0)
        def _():
            rows_pass(True)

        @pl.when(pl.program_id(1) > 0)
        def _():
            rows_pass(False)

    act = _ff_act_spec(ts)
    row = pl.BlockSpec((ts, D_MODEL), lambda i, j: (i, 0))
    return pl.pallas_call(
        body, name="ffn_bwd", grid=(S // ts, D_FF // FF_TILE),
        in_specs=[row, act, act,
                  pl.BlockSpec((FF_TILE, D_MODEL), lambda i, j: (j, 0)),
                  pl.BlockSpec((FF_TILE, D_MODEL), lambda i, j: (j, 0)),
                  pl.BlockSpec((FF_TILE, D_MODEL), lambda i, j: (j, 0))],
        out_specs=[act, act, act, row],
        out_shape=[_ff_act_shape(S)] * 3 + [jax.ShapeDtypeStruct((S, D_MODEL), f32)],
        compiler_params=_params("parallel", "arbitrary"),
    )(df, gate, up, w_gate, w_up, w_down)


def _norm_bwd(dh2, dy, x2, mix, g3, g2):
    S = dh2.shape[0]
    ts = 512

    def body(dh_ref, dy_ref, x2_ref, mix_ref, g3_ref, g2_ref, dx2_ref, dmix_ref, dg3_ref, dg2_ref):
        @pl.when(pl.program_id(0) == 0)
        def _():
            dg3_ref[...] = jnp.zeros_like(dg3_ref)
            dg2_ref[...] = jnp.zeros_like(dg2_ref)

        dh = dh_ref[...]
        x2 = x2_ref[...]
        r3 = _rms(x2)
        xhat = x2 * r3
        dg3_ref[...] += jnp.sum(dh * xhat, axis=0, keepdims=True)
        dhg = dh * g3_ref[...]
        dx2 = dy_ref[...] + r3 * (dhg - xhat * jnp.mean(dhg * xhat, axis=-1, keepdims=True))
        dx2_ref[...] = dx2
        mix = mix_ref[...]
        r2 = _rms(mix)
        mhat = mix * r2
        dg2_ref[...] += jnp.sum(dx2 * mhat, axis=0, keepdims=True)
        dmg = dx2 * g2_ref[...]
        dmix_ref[...] = (r2 * (dmg - mhat * jnp.mean(dmg * mhat, axis=-1, keepdims=True))).astype(bf16)

    row = pl.BlockSpec((ts, D_MODEL), lambda i: (i, 0))
    gain = pl.BlockSpec((1, D_MODEL), lambda i: (0, 0))
    return pl.pallas_call(
        body, name="norm_bwd", grid=(S // ts,), in_specs=[row, row, row, row, gain, gain],
        out_specs=[row, row, gain, gain],
        out_shape=[jax.ShapeDtypeStruct((S, D_MODEL), f32), jax.ShapeDtypeStruct((S, D_MODEL), bf16),
                   jax.ShapeDtypeStruct((1, D_MODEL), f32), jax.ShapeDtypeStruct((1, D_MODEL), f32)],
        compiler_params=_params("arbitrary"),
    )(dh2, dy, x2, mix, g3, g2)


def _out_proj_bwd(dmix, w_out, attn_out, head_ones, grads):
    S = dmix.shape[0]
    ts = 512
    n_dil = len(DILATIONS)

    def body(dm_ref, w_ref, o_ref, ones_ref, g_ref, dp_ref, dl_ref, *rest):
        do_refs, theirs_ref = rest[:n_dil], rest[n_dil]
        stage = rest[n_dil + 1:n_dil + 1 + N_STAGE]
        swap = _Swap(g_ref, theirs_ref, *rest[n_dil + 1 + N_STAGE:])

        @pl.when(pl.program_id(0) == 0)
        def _():
            swap.start()

        @pl.when(pl.program_id(0) == S // ts - 1)
        def _():
            swap.finish()

        dcat = _dot_nt(dm_ref[...], w_ref[...])
        dp_ref[...] = dcat[:, :POOL_WIDTH]
        do = dcat[:, POOL_WIDTH:]
        for j in range(ATTN_WIDTH // 128):
            stage[j][...] = do[:, j * 128:(j + 1) * 128]
        _store_streams(stage, do_refs, ts)
        prod = do * o_ref[...].astype(f32)
        hi = prod.astype(bf16)
        lo = (prod - hi.astype(f32)).astype(bf16)
        dl_ref[...] = _dot(hi, ones_ref[...]) + _dot(lo, ones_ref[...])

    row = lambda w: pl.BlockSpec((ts, w), lambda i: (i, 0))
    res = pl.pallas_call(
        body, name="out_proj_bwd", grid=(S // ts,),
        in_specs=[row(D_MODEL), pl.BlockSpec((D_MODEL, D_MODEL), lambda i: (0, 0)), row(ATTN_WIDTH),
                  pl.BlockSpec((ATTN_WIDTH, ATTN_WIDTH), lambda i: (0, 0)), ANY],
        out_specs=[row(POOL_WIDTH), row(ATTN_WIDTH)] + [_stream_spec(d, ts) for d in DILATIONS] + [ANY],
        out_shape=[jax.ShapeDtypeStruct((S, POOL_WIDTH), f32), jax.ShapeDtypeStruct((S, ATTN_WIDTH), f32)]
        + [_stream_shape(S, d) for d in DILATIONS] + [_Swap.out_shape(grads)],
        scratch_shapes=_stage_scratch(ts) + _Swap.scratch(grads),
        compiler_params=_params("arbitrary"),
    )(dmix, w_out, attn_out, head_ones, grads)
    return res[0], res[1], res[2:2 + n_dil], res[2 + n_dil]


def _in_proj_bwd(du, dq, dk, dv, cos_t, sin_t, w_in, x, dx2, g1):
    S = x.shape[0]
    ts = 256

    def body(du_ref, dq_ref, dk_ref, dv_ref, cos_ref, sin_ref, w_ref, x_ref, dx2_ref, g_ref, gx_ref, dproj_ref, dg_ref):
        @pl.when(pl.program_id(0) == 0)
        def _():
            dg_ref[...] = jnp.zeros_like(dg_ref)

        dproj_ref[:, :POOL_WIDTH] = du_ref[...]
        cos = cos_ref[...]
        sin = sin_ref[...]
        first = _first_half_mask(ts)
        for j in range(ATTN_WIDTH // 128):
            cols = slice(j * 128, (j + 1) * 128)
            for base, ref in ((POOL_WIDTH, dq_ref), (POOL_WIDTH + ATTN_WIDTH, dk_ref)):
                g = ref[:, cols].astype(f32)
                pre = g * cos + _rope_partner(g * sin, first)
                dproj_ref[:, base + j * 128: base + (j + 1) * 128] = pre.astype(bf16)
        dproj_ref[:, POOL_WIDTH + 2 * ATTN_WIDTH:] = dv_ref[...]

        dh = _dot(dproj_ref[...], w_ref[...])
        xv = x_ref[...]
        r = _rms(xv)
        xhat = xv * r
        dg_ref[...] += jnp.sum(dh * xhat, axis=0, keepdims=True)
        dhg = dh * g_ref[...]
        gx_ref[...] = dx2_ref[...] + r * (dhg - xhat * jnp.mean(dhg * xhat, axis=-1, keepdims=True))

    row = lambda w: pl.BlockSpec((ts, w), lambda i: (i, 0))
    gain = pl.BlockSpec((1, D_MODEL), lambda i: (0, 0))
    return pl.pallas_call(
        body, name="in_proj_bwd", grid=(S // ts,),
        in_specs=[row(POOL_WIDTH)] + [row(ATTN_WIDTH)] * 3 + [row(128), row(128),
                  pl.BlockSpec((IN_WIDTH, D_MODEL), lambda i: (0, 0)), row(D_MODEL), row(D_MODEL), gain],
        out_specs=[row(D_MODEL), row(IN_WIDTH), gain],
        out_shape=[jax.ShapeDtypeStruct((S, D_MODEL), f32), jax.ShapeDtypeStruct((S, IN_WIDTH), bf16),
                   jax.ShapeDtypeStruct((1, D_MODEL), f32)],
        compiler_params=_params("arbitrary"),
    )(du, dq, dk, dv, cos_t, sin_t, w_in, x, dx2, g1)


def _matmul_tiles_tn(a, b, name):
    T, K, w = a.shape
    N = b.shape[1]
    tk = 1024

    def body(a_ref, b_ref, o_ref):
        def tiles_pass(first):
            for t in range(T):
                part = _dot_tn(a_ref[t], b_ref[...])
                if first:
                    o_ref[t * w:(t + 1) * w, :] = part
                else:
                    o_ref[t * w:(t + 1) * w, :] += part

        @pl.when(pl.program_id(0) == 0)
        def _():
            tiles_pass(True)

        @pl.when(pl.program_id(0) > 0)
        def _():
            tiles_pass(False)

    return pl.pallas_call(
        body, name=name, grid=(K // tk,),
        in_specs=[pl.BlockSpec((T, tk, w), lambda k: (0, k, 0)), pl.BlockSpec((tk, N), lambda k: (k, 0))],
        out_specs=pl.BlockSpec((T * w, N), lambda k: (0, 0)),
        out_shape=jax.ShapeDtypeStruct((T * w, N), f32),
        compiler_params=_params("arbitrary"),
    )(a, b)


def _matmul_tn(a, b, tn, name):
    K, M = a.shape
    N = b.shape[1]
    tk = 1024

    def body(a_ref, b_ref, o_ref):
        part = _dot_tn(a_ref[...], b_ref[...])

        @pl.when(pl.program_id(1) == 0)
        def _():
            o_ref[...] = part

        @pl.when(pl.program_id(1) > 0)
        def _():
            o_ref[...] += part

    return pl.pallas_call(
        body, name=name, grid=(N // tn, K // tk),
        in_specs=[pl.BlockSpec((tk, M), lambda n, k: (k, 0)), pl.BlockSpec((tk, tn), lambda n, k: (k, n))],
        out_specs=pl.BlockSpec((M, tn), lambda n, k: (0, n)),
        out_shape=jax.ShapeDtypeStruct((M, N), f32),
        compiler_params=_params("parallel", "arbitrary"),
    )(a, b)


def _rope_tables(S):
    half = HEAD_DIM // 2
    freqs = ROPE_THETA ** (-jnp.arange(half, dtype=f32) * (2.0 / HEAD_DIM))
    ang = jnp.arange(S).astype(f32)[:, None] * freqs[None, :]
    cos = jnp.tile(jnp.cos(ang), (1, 4))
    sin = jnp.sin(ang)
    sin = jnp.tile(jnp.concatenate([-sin, sin], axis=1), (1, 2))
    return cos, sin


def _block_diag(w_pool):
    w = jnp.zeros((POOL_WIDTH, POOL_WIDTH), w_pool.dtype)
    for g in range(POOL_WIDTH // POOL_GROUP):
        w = lax.dynamic_update_slice(w, w_pool[g], (g * POOL_GROUP, g * POOL_GROUP))
    return w


def _head_ones():
    head = np.arange(ATTN_WIDTH) // HEAD_DIM
    return jnp.asarray(head[:, None] == head[None, :], dtype=bf16)


def _place():
    x, y, c = lax.axis_index("x"), lax.axis_index("y"), lax.axis_index("c")
    chips = [(1 - x, y), (x, 1 - y), (1 - x, 1 - y)]
    return x, y, c, chips


ANY = pl.BlockSpec(memory_space=pl.ANY)
N_PEER_CHIPS = N_CHIPS - 1
ICI_PIECES = 4
D2D_PIECES = 8
LOCAL_PIECES = 8


def _row_chunks(rows, n, unit=32):
    units = rows // unit
    out, start = [], 0
    for i in range(n):
        size = (units // n + (1 if i < units % n else 0)) * unit
        out.append((start, size))
        start += size
    return [piece for piece in out if piece[1]]


class _LocalCopy:
    def __init__(self, src_rows, dst_rows, rows, buf, sems_in, sems_out):
        self.loads, self.stores = [], []
        for i, (start, size) in enumerate(_row_chunks(rows, LOCAL_PIECES)):
            r = pl.ds(start, size)
            self.loads.append(pltpu.make_async_copy(src_rows(r), buf.at[r], sems_in.at[i]))
            self.stores.append(pltpu.make_async_copy(buf.at[r], dst_rows(r), sems_out.at[i]))

    def start(self):
        for cp in self.loads:
            cp.start()

    def pass_on(self):
        for load, store in zip(self.loads, self.stores):
            load.wait()
            store.start()

    def finish(self):
        for store in self.stores:
            store.wait()

    @staticmethod
    def scratch(rows, dtype):
        return [pltpu.VMEM((rows, D_MODEL), dtype), pltpu.SemaphoreType.DMA((LOCAL_PIECES,)),
                pltpu.SemaphoreType.DMA((LOCAL_PIECES,))]


class _Gather:
    def __init__(self, w_ref, out_ref, send1, recv1, send2, recv2, buf, sems_in, sems_out):
        x, y, c, chips = _place()
        me = 2 * x + y
        rows = w_ref.shape[0]
        half = rows // 2
        pieces = _row_chunks(half, ICI_PIECES)
        self.own = _LocalCopy(lambda r: w_ref.at[r], lambda r: out_ref.at[me, r], rows, buf, sems_in, sems_out)

        def rows_of(core, piece):
            start, size = piece
            return pl.ds(core * half + start, size)

        self.sends, self.arrivals, self.forwards, self.forward_arrivals = [], [], [], []
        for i, piece in enumerate(pieces):
            for j, (cx, cy) in enumerate(chips):
                k = j * len(pieces) + i
                there = 2 * cx + cy

                def direct(src_chip, cx=cx, cy=cy, k=k, piece=piece):
                    return pltpu.make_async_remote_copy(
                        src_ref=w_ref.at[rows_of(c, piece)], dst_ref=out_ref.at[src_chip, rows_of(c, piece)],
                        send_sem=send1.at[k], recv_sem=recv1.at[k], device_id=(cx, cy, c), device_id_type=MESH)

                def passed(core, there=there, k=k, piece=piece):
                    return pltpu.make_async_remote_copy(
                        src_ref=out_ref.at[there, rows_of(core, piece)], dst_ref=out_ref.at[there, rows_of(core, piece)],
                        send_sem=send2.at[k], recv_sem=recv2.at[k], device_id=(x, y, 1 - c), device_id_type=MESH)

                self.sends.append(direct(me))
                self.arrivals.append(direct(there))
                self.forwards.append(passed(c))
                self.forward_arrivals.append(passed(1 - c))

    def start(self):
        for cp in self.sends:
            cp.start()
        self.own.start()

    def pass_on(self):
        self.own.pass_on()
        for arrival, forward in zip(self.arrivals, self.forwards):
            arrival.wait_recv()
            forward.start()

    def finish(self):
        for arrival in self.forward_arrivals:
            arrival.wait_recv()
        for cp in self.sends + self.forwards:
            cp.wait_send()
        self.own.finish()

    @staticmethod
    def scratch(rows, dtype):
        n = N_PEER_CHIPS * len(_row_chunks(rows // 2, ICI_PIECES))
        return [pltpu.SemaphoreType.DMA((n,))] * 4 + _LocalCopy.scratch(rows, dtype)

    @staticmethod
    def out_shape(rows, dtype):
        return jax.ShapeDtypeStruct((N_CHIPS, rows, D_MODEL), dtype)


def _gather_weights(pack):
    rows = pack.shape[0]

    def body(w_ref, out_ref, *scratch):
        gather = _Gather(w_ref, out_ref, *scratch)
        gather.start()
        gather.pass_on()
        gather.finish()

    return pl.pallas_call(
        body, name="gather_weights", in_specs=[ANY], out_specs=ANY, out_shape=_Gather.out_shape(rows, pack.dtype),
        scratch_shapes=_Gather.scratch(rows, pack.dtype),
        compiler_params=pltpu.CompilerParams(vmem_limit_bytes=VMEM_LIMIT_V7X),
    )(pack)


class _Scatter:
    def __init__(self, h_ref, out_ref, send, recv):
        x, y, c, chips = _place()
        pieces = _row_chunks(h_ref.shape[1], ICI_PIECES)
        self.copies = []
        for i, (start, size) in enumerate(pieces):
            for j, (cx, cy) in enumerate(chips):
                k = j * len(pieces) + i
                self.copies.append(pltpu.make_async_remote_copy(
                    src_ref=h_ref.at[2 * cx + cy, pl.ds(start, size)], dst_ref=out_ref.at[j, pl.ds(start, size)],
                    send_sem=send.at[k], recv_sem=recv.at[k], device_id=(cx, cy, c), device_id_type=MESH))

    def start(self):
        for cp in self.copies:
            cp.start()

    def finish(self):
        for cp in self.copies:
            cp.wait_recv()
        for cp in self.copies:
            cp.wait_send()

    @staticmethod
    def scratch(half):
        n = N_PEER_CHIPS * len(_row_chunks(half, ICI_PIECES))
        return [pltpu.SemaphoreType.DMA((n,))] * 2

    @staticmethod
    def out_shape(half, dtype):
        return jax.ShapeDtypeStruct((N_PEER_CHIPS, half, D_MODEL), dtype)


def _scatter_to_chips(h):
    half = h.shape[1]

    def body(h_ref, out_ref, send, recv):
        scatter = _Scatter(h_ref, out_ref, send, recv)
        scatter.start()
        scatter.finish()

    return pl.pallas_call(
        body, name="scatter_to_chips", in_specs=[ANY], out_specs=ANY, out_shape=_Scatter.out_shape(half, h.dtype),
        scratch_shapes=_Scatter.scratch(half),
    )(h)


class _Swap:
    def __init__(self, g_ref, theirs_ref, send, recv):
        x, y, c, _ = _place()
        half = g_ref.shape[1] // 2
        pieces = _row_chunks(half, D2D_PIECES)
        self.copies = []
        for s in range(N_CHIPS):
            for i, (start, size) in enumerate(pieces):
                k = s * len(pieces) + i
                self.copies.append(pltpu.make_async_remote_copy(
                    src_ref=g_ref.at[s, pl.ds((1 - c) * half + start, size)], dst_ref=theirs_ref.at[s, pl.ds(start, size)],
                    send_sem=send.at[k], recv_sem=recv.at[k], device_id=(x, y, 1 - c), device_id_type=MESH))

    def start(self):
        for cp in self.copies:
            cp.start()

    def finish(self):
        for cp in self.copies:
            cp.wait()

    @staticmethod
    def scratch(g):
        n = N_CHIPS * len(_row_chunks(g.shape[1] // 2, D2D_PIECES))
        return [pltpu.SemaphoreType.DMA((n,))] * 2

    @staticmethod
    def out_shape(g):
        return jax.ShapeDtypeStruct((N_CHIPS, g.shape[1] // 2, D_MODEL), g.dtype)


def _swap_halves(g):
    def body(g_ref, theirs_ref, send, recv):
        swap = _Swap(g_ref, theirs_ref, send, recv)
        swap.start()
        swap.finish()

    return pl.pallas_call(
        body, name="swap_halves", in_specs=[ANY], out_specs=ANY, out_shape=_Swap.out_shape(g),
        scratch_shapes=_Swap.scratch(g),
    )(g)


ADD_TILE_MAX_ROWS = 600


def _add_tile(half):
    return max(t for t in range(8, ADD_TILE_MAX_ROWS + 1, 8) if half % t == 0)


def _add_cores(g, theirs, name, out_dtype=f32):
    half = theirs.shape[1]
    tr = _add_tile(half)
    n_t = half // tr

    def body(c_ref, g_ref, t_ref, o_ref):
        o_ref[...] = (g_ref[...] + t_ref[...]).astype(out_dtype)

    blk = pl.BlockSpec((1, tr, D_MODEL), lambda s, t, c_ref: (s, t, 0))
    return pl.pallas_call(
        body, name=name,
        grid_spec=pltpu.PrefetchScalarGridSpec(
            num_scalar_prefetch=1, grid=(N_CHIPS, n_t),
            in_specs=[pl.BlockSpec((1, tr, D_MODEL), lambda s, t, c_ref: (s, c_ref[0] * n_t + t, 0)), blk],
            out_specs=blk),
        out_shape=jax.ShapeDtypeStruct(theirs.shape, out_dtype),
        compiler_params=_params("parallel", "parallel"),
    )(lax.axis_index("c").astype(jnp.int32).reshape(1), g, theirs)


def _add_chips(chip_sum, others, name):
    half = chip_sum.shape[1]
    tr = _add_tile(half)

    def body(me_ref, own_ref, o0, o1, o2, out_ref):
        out_ref[...] = ((own_ref[0].astype(f32) + o0[0].astype(f32)) + o1[0].astype(f32)) + o2[0].astype(f32)

    other = lambda j: pl.BlockSpec((1, tr, D_MODEL), lambda t, me_ref: (j, t, 0))
    return pl.pallas_call(
        body, name=name,
        grid_spec=pltpu.PrefetchScalarGridSpec(
            num_scalar_prefetch=1, grid=(half // tr,),
            in_specs=[pl.BlockSpec((1, tr, D_MODEL), lambda t, me_ref: (me_ref[0], t, 0)), other(0), other(1), other(2)],
            out_specs=pl.BlockSpec((tr, D_MODEL), lambda t, me_ref: (t, 0))),
        out_shape=jax.ShapeDtypeStruct((half, D_MODEL), f32),
        compiler_params=_params("parallel"),
    )((2 * lax.axis_index("x") + lax.axis_index("y")).astype(jnp.int32).reshape(1), chip_sum, others, others, others)


def _join_halves(r):
    half = r.shape[0]
    pieces = _row_chunks(half, 2 * D2D_PIECES)
    n = len(pieces)

    def body(r_ref, out_ref, send, recv, buf, sems_in, sems_out):
        x, y, c, _ = _place()
        own = _LocalCopy(lambda rr: r_ref.at[rr], lambda rr: out_ref.at[c, rr], half, buf, sems_in, sems_out)
        own.start()

        def piece(i, core):
            start, size = pieces[i]
            return pltpu.make_async_remote_copy(
                src_ref=r_ref.at[pl.ds(start, size)], dst_ref=out_ref.at[core, pl.ds(start, size)],
                send_sem=send.at[i], recv_sem=recv.at[i], device_id=(x, y, 1 - c), device_id_type=MESH)

        copies = [piece(i, c) for i in range(n)]
        for cp in copies:
            cp.start()
        own.pass_on()
        for i in range(n):
            piece(i, 1 - c).wait_recv()
        for cp in copies:
            cp.wait_send()
        own.finish()

    return pl.pallas_call(
        body, name="join_halves", in_specs=[ANY], out_specs=ANY,
        out_shape=jax.ShapeDtypeStruct((2,) + r.shape, r.dtype),
        scratch_shapes=[pltpu.SemaphoreType.DMA((n,))] * 2 + _LocalCopy.scratch(half, r.dtype),
        compiler_params=pltpu.CompilerParams(vmem_limit_bytes=VMEM_LIMIT_V7X),
    )(r)


def _sum_small(block):
    def body(b_ref, out_ref, gathered, send, recv):
        x, y, c, _ = _place()
        me = 4 * x + 2 * y + c
        gathered[me] = b_ref[...]
        sends = []
        for kk in range(1, N_DEV):
            flip = lambda v, bit: 1 - v if bit else v
            peer = (flip(x, kk & 4), flip(y, kk & 2), flip(c, kk & 1))
            cp = pltpu.make_async_remote_copy(
                src_ref=b_ref, dst_ref=gathered.at[me], send_sem=send.at[kk - 1], recv_sem=recv.at[kk - 1],
                device_id=peer, device_id_type=MESH)
            cp.start()
            sends.append(cp)
        for kk in range(1, N_DEV):
            peer_index = jnp.bitwise_xor(me, kk)
            pltpu.make_async_remote_copy(
                src_ref=b_ref, dst_ref=gathered.at[peer_index], send_sem=send.at[kk - 1], recv_sem=recv.at[kk - 1],
                device_id=(x, y, c), device_id_type=MESH).wait_recv()
        for cp in sends:
            cp.wait_send()
        acc = gathered[0]
        for dev in range(1, N_DEV):
            acc = acc + gathered[dev]
        out_ref[...] = acc

    vmem = pl.BlockSpec(memory_space=pltpu.VMEM)
    return pl.pallas_call(
        body, name="sum_small", in_specs=[vmem], out_specs=vmem,
        out_shape=jax.ShapeDtypeStruct(block.shape, block.dtype),
        scratch_shapes=[pltpu.VMEM((N_DEV,) + block.shape, block.dtype),
                        pltpu.SemaphoreType.DMA((N_DEV - 1,)), pltpu.SemaphoreType.DMA((N_DEV - 1,))],
    )(block)


def _adamw(w, g, m, v, name):
    rows, cols = w.shape
    tr = rows
    for cand in (512, 256, 128, 64, 32, 16, 8):
        if rows % cand == 0:
            tr = cand
            break
    c1 = 1.0 - ADAM_B1 ** ADAM_STEP
    c2 = 1.0 - ADAM_B2 ** ADAM_STEP

    def body(w_ref, g_ref, m_ref, v_ref, d_ref, nm_ref, nv_ref):
        gv = g_ref[...]
        nm = ADAM_B1 * m_ref[...] + (1.0 - ADAM_B1) * gv
        nv = ADAM_B2 * v_ref[...] + (1.0 - ADAM_B2) * (gv * gv)
        nm_ref[...] = nm
        nv_ref[...] = nv
        d_ref[...] = -ADAM_LR * ((nm / c1) / (jnp.sqrt(nv / c2) + ADAM_EPS) + ADAM_WD * w_ref[...])

    blk = pl.BlockSpec((tr, cols), lambda i: (i, 0))
    shape = jax.ShapeDtypeStruct((rows, cols), f32)
    return pl.pallas_call(
        body, name=name, grid=(rows // tr,), in_specs=[blk] * 4, out_specs=[blk] * 3, out_shape=[shape] * 3,
        compiler_params=_params("parallel"),
    )(w, g, m, v)


LARGE = ("w_in", "w_out", "w_gate", "w_up", "w_down")
SMALL = ("ln_pre_mix", "ln_post_mix", "ln_pre_ffn", "ln_post_ffn", "pool_scale", "w_pool")
SHARD_ROWS = {"w_in": 640, "w_out": 256, "w_gate": 704, "w_up": 704, "w_down": 704}
COLUMN_SHARDED = ("w_in", "w_gate", "w_up")
NEEDED_FIRST = ("w_in",)
NEEDED_LATER = ("w_out", "w_gate", "w_up", "w_down")
READY_EARLY = ("w_out", "w_gate", "w_up", "w_down")
READY_LATE = ("w_in",)


def _pack_shard(shards, names):
    return jnp.concatenate([shards[n].T if n in COLUMN_SHARDED else shards[n] for n in names], axis=0)


def _unpack_shard(pack, names):
    out, row = {}, 0
    for n in names:
        part = pack[row:row + SHARD_ROWS[n]]
        out[n] = part.T if n in COLUMN_SHARDED else part
        row += SHARD_ROWS[n]
    return out


def _whole_from_shards(packs, names):
    out, row = {}, 0
    for n in names:
        rows = SHARD_ROWS[n]
        out[n] = packs[:, row:row + rows].reshape(N_CHIPS * rows, D_MODEL)
        row += rows
    return out


def _shards_from_whole(grads, names):
    return jnp.concatenate([grads[n].reshape(N_CHIPS, SHARD_ROWS[n], D_MODEL) for n in names], axis=1)


def _pack_small(vals):
    rows = [vals[n].reshape(1, D_MODEL) for n in SMALL[:4]]
    rows.append(jnp.pad(vals["pool_scale"].reshape(1, POOL_WIDTH), ((0, 0), (0, D_MODEL - POOL_WIDTH))))
    rows.append(jnp.pad(vals["loss"].reshape(1, 1), ((0, 0), (0, D_MODEL - 1))))
    rows.append(jnp.zeros((2, D_MODEL), f32))
    rows.append(vals["w_pool"].reshape(16, D_MODEL))
    return jnp.concatenate(rows, axis=0)


def _unpack_small(block):
    out = {n: block[i:i + 1] for i, n in enumerate(SMALL[:4])}
    out["pool_scale"] = block[4:5, :POOL_WIDTH]
    out["loss"] = block[5, 0]
    out["w_pool"] = block[8:24].reshape(1, 4, POOL_GROUP, POOL_GROUP)
    return out


def kernel(x, ln_pre_mix, w_in, w_pool, pool_scale, w_out, ln_post_mix, ln_pre_ffn, w_gate, w_up, w_down, ln_post_ffn, loss_target, m_ln_pre_mix, m_w_in, m_w_pool, m_pool_scale, m_w_out, m_ln_post_mix, m_ln_pre_ffn, m_w_gate, m_w_up, m_w_down, m_ln_post_ffn, v_ln_pre_mix, v_w_in, v_w_pool, v_pool_scale, v_w_out, v_ln_post_mix, v_ln_pre_ffn, v_w_gate, v_w_up, v_w_down, v_ln_post_ffn):
    w = dict(ln_pre_mix=ln_pre_mix, w_in=w_in, w_pool=w_pool, pool_scale=pool_scale, w_out=w_out,
             ln_post_mix=ln_post_mix, ln_pre_ffn=ln_pre_ffn, w_gate=w_gate, w_up=w_up, w_down=w_down,
             ln_post_ffn=ln_post_ffn)
    m = dict(ln_pre_mix=m_ln_pre_mix, w_in=m_w_in, w_pool=m_w_pool, pool_scale=m_pool_scale, w_out=m_w_out,
             ln_post_mix=m_ln_post_mix, ln_pre_ffn=m_ln_pre_ffn, w_gate=m_w_gate, w_up=m_w_up, w_down=m_w_down,
             ln_post_ffn=m_ln_post_ffn)
    v = dict(ln_pre_mix=v_ln_pre_mix, w_in=v_w_in, w_pool=v_w_pool, pool_scale=v_pool_scale, w_out=v_w_out,
             ln_post_mix=v_ln_post_mix, ln_pre_ffn=v_ln_pre_ffn, w_gate=v_w_gate, w_up=v_w_up, w_down=v_w_down,
             ln_post_ffn=v_ln_post_ffn)

    xs, target = x[0], loss_target[0]
    cos_t, sin_t = _rope_tables(xs.shape[0])
    w_bd = _block_diag(w_pool[0]).astype(bf16)
    shard = {n: w[n][0].astype(bf16) for n in LARGE}

    w_in_whole = _whole_from_shards(_gather_weights(_pack_shard(shard, NEEDED_FIRST)), NEEDED_FIRST)["w_in"]
    h1, u, qs, ks, vs = _in_proj(xs, ln_pre_mix, w_in_whole, cos_t, sin_t)
    pool_out = _pool_fwd(u, w_bd, pool_scale)
    attn_out, lse, later = _attn_fwd(qs, ks, vs, _pack_shard(shard, NEEDED_LATER))
    whole = _whole_from_shards(later, NEEDED_LATER)
    mix, x2, h2 = _out_proj(pool_out, attn_out, whole["w_out"], xs, ln_post_mix, ln_pre_ffn)
    gate, up, f = _ffn_fwd(h2, whole["w_gate"], whole["w_up"], whole["w_down"])
    dy, df, dg4, loss = _loss_head(f, x2, target, ln_post_ffn)

    large = {}
    a, dgate, dup, dh2 = _ffn_bwd(df, gate, up, whole["w_gate"], whole["w_up"], whole["w_down"])
    large["w_down"] = _matmul_tiles_tn(a, df, "grad_w_down")
    large["w_gate"] = _matmul_tiles_tn(dgate, h2, "grad_w_gate")
    large["w_up"] = _matmul_tiles_tn(dup, h2, "grad_w_up")
    dx2, dmix, dg3, dg2 = _norm_bwd(dh2, dy, x2, mix, ln_pre_ffn, ln_post_mix)
    large["w_out"] = jnp.concatenate([_matmul_tn(pool_out, dmix, D_MODEL, "grad_w_out_pool"),
                                      _matmul_tn(attn_out, dmix, D_MODEL, "grad_w_out_attn")], axis=0)
    early = _shards_from_whole(large, READY_EARLY)
    dpool, delta, dos, early_theirs = _out_proj_bwd(dmix, whole["w_out"], attn_out, _head_ones(), early)
    early_chip = _add_cores(early, early_theirs, "add_cores_early")
    du, d_w_bd, d_scale = _pool_bwd(u, dpool, w_bd, pool_scale)
    dq, dk, dv, early_others = _attn_bwd(qs, ks, vs, dos, lse, delta, early_chip)
    grad_x, dproj, dg1 = _in_proj_bwd(du, dq, dk, dv, cos_t, sin_t, w_in_whole, xs, dx2, ln_pre_mix)
    large["w_in"] = _matmul_tn(dproj, h1, D_MODEL, "grad_w_in")
    late = _shards_from_whole(large, READY_LATE)
    late_chip = _add_cores(late, _swap_halves(late), "add_cores_late", bf16)
    late_others = _scatter_to_chips(late_chip)
    early_half = _add_chips(early_chip, early_others, "add_chips_early")
    late_half = _add_chips(late_chip, late_others, "add_chips_late")
    joined = _join_halves(jnp.concatenate([early_half, late_half], axis=0))
    n_early = early_half.shape[0]
    grads = _unpack_shard(joined[:, :n_early].reshape(-1, D_MODEL), READY_EARLY)
    grads.update(_unpack_shard(joined[:, n_early:].reshape(-1, D_MODEL), READY_LATE))

    d_w_pool = jnp.stack([d_w_bd[g * POOL_GROUP:(g + 1) * POOL_GROUP, g * POOL_GROUP:(g + 1) * POOL_GROUP]
                          for g in range(POOL_WIDTH // POOL_GROUP)])
    small = dict(ln_pre_mix=dg1, ln_post_mix=dg2, ln_pre_ffn=dg3, ln_post_ffn=dg4, pool_scale=d_scale, w_pool=d_w_pool)
    total = _unpack_small(_sum_small(_pack_small(dict(small, loss=loss))))
    for n in SMALL:
        grads[n] = total[n]

    delta_w, new_m, new_v = {}, {}, {}
    for n in LARGE:
        delta_w[n], new_m[n], new_v[n] = _adamw(w[n][0], grads[n], m[n][0], v[n][0], "adamw_" + n)
    small_state = [_pack_small(dict({n: s[n] for n in SMALL}, loss=jnp.zeros((), f32))) for s in (w, m, v)]
    small_grad = _pack_small(dict({n: grads[n] for n in SMALL}, loss=jnp.zeros((), f32)))
    sd, sm, sv = _adamw(small_state[0], small_grad, small_state[1], small_state[2], "adamw_small")
    for out, block in ((delta_w, sd), (new_m, sm), (new_v, sv)):
        un = _unpack_small(block)
        for n in SMALL:
            out[n] = un[n]

    names = ("ln_pre_mix", "w_in", "w_pool", "pool_scale", "w_out", "ln_post_mix", "ln_pre_ffn", "w_gate", "w_up",
             "w_down", "ln_post_ffn")
    full = lambda d: [d[n].reshape(w[n].shape) for n in names]
    return (total["loss"], grad_x[None], *full(grads), *full(delta_w), *full(new_m), *full(new_v))
```

```python
import numpy as np
import jax
import jax.numpy as jnp
from jax import lax
from jax.experimental import pallas as pl
from jax.experimental.pallas import tpu as pltpu

D_MODEL = 1024
POOL_WIDTH = 256
POOL_GROUP = 64
ATTN_WIDTH = 768
HEAD_DIM = 64
IN_WIDTH = 2560
D_FF = 2816
BLOCK = 128
DILATIONS = (1, 4, 16)
ROPE_THETA = 10000.0
EPS = 1e-6
ATTN_SCALE = 0.125
NEG = -1e30

ADAM_LR = 0.001
ADAM_B1 = 0.9
ADAM_B2 = 0.999
ADAM_EPS = 1e-08
ADAM_WD = 0.01
ADAM_STEP = 10

N_CHIPS = 4
N_DEV = 8
VMEM_LIMIT_V7X = 56 * 1024 * 1024
MESH = pl.DeviceIdType.MESH

f32 = jnp.float32
bf16 = jnp.bfloat16


def _params(*sem):
    return pltpu.CompilerParams(dimension_semantics=sem, vmem_limit_bytes=VMEM_LIMIT_V7X)


def _dot(a, b):
    return jnp.dot(a, b, preferred_element_type=f32)


def _dot_nt(a, b):
    return lax.dot_general(a, b, (((1,), (1,)), ((), ())), preferred_element_type=f32)


def _dot_tn(a, b):
    return lax.dot_general(a, b, (((0,), (0,)), ((), ())), preferred_element_type=f32)


def _rope_partner(a, first_half):
    return jnp.where(first_half, pltpu.roll(a, 96, 1), pltpu.roll(a, 32, 1))


def _first_half_mask(rows):
    lane = lax.broadcasted_iota(jnp.int32, (rows, 128), 1)
    return (lane % HEAD_DIM) < (HEAD_DIM // 2)


def _stream_spec(d, ts):
    return pl.BlockSpec((d, ts // d, ATTN_WIDTH), lambda i: (0, i, 0))


def _stream_shape(S, d):
    return jax.ShapeDtypeStruct((d, S // d, ATTN_WIDTH), bf16)


N_STAGE = ATTN_WIDTH // 128


def _stage_scratch(ts):
    return [pltpu.VMEM((ts, 128), f32)] * N_STAGE


def _store_streams(stage, out_refs, ts):
    for d, ref in zip(DILATIONS, out_refs):
        for r in range(d):
            rows = pl.ds(0, ts) if d == 1 else pl.ds(r, ts // d, stride=d)
            for j in range(N_STAGE):
                ref[r, :, j * 128:(j + 1) * 128] = stage[j][rows, :].astype(bf16)


def _in_proj(x, g1, w_in, cos_t, sin_t):
    S = x.shape[0]
    ts = 512

    def body(x_ref, g_ref, w_ref, cos_ref, sin_ref, h_ref, u_ref, *rest):
        outs, stage = rest[:-N_STAGE], rest[-N_STAGE:]
        xv = x_ref[...]
        r = lax.rsqrt(jnp.mean(xv * xv, axis=-1, keepdims=True) + EPS)
        h = ((xv * r) * g_ref[...]).astype(bf16)
        h_ref[...] = h
        proj = _dot_nt(h, w_ref[...])
        u_ref[...] = proj[:, :POOL_WIDTH]
        cos = cos_ref[...]
        sin = sin_ref[...]
        first = _first_half_mask(ts)
        n_dil = len(DILATIONS)
        for which, base in enumerate((POOL_WIDTH, POOL_WIDTH + ATTN_WIDTH)):
            for j in range(ATTN_WIDTH // 128):
                a = proj[:, base + j * 128: base + (j + 1) * 128]
                if which == 0:
                    a = a * ATTN_SCALE
                stage[j][...] = a * cos + _rope_partner(a, first) * sin
            _store_streams(stage, outs[which * n_dil:(which + 1) * n_dil], ts)
        for j in range(ATTN_WIDTH // 128):
            base = POOL_WIDTH + 2 * ATTN_WIDTH + j * 128
            stage[j][...] = proj[:, base:base + 128]
        _store_streams(stage, outs[2 * n_dil:], ts)

    row = lambda w: pl.BlockSpec((ts, w), lambda i: (i, 0))
    streams = [_stream_spec(d, ts) for d in DILATIONS] * 3
    res = pl.pallas_call(
        body, name="in_proj", grid=(S // ts,),
        in_specs=[row(D_MODEL), pl.BlockSpec((1, D_MODEL), lambda i: (0, 0)),
                  pl.BlockSpec((IN_WIDTH, D_MODEL), lambda i: (0, 0)), row(128), row(128)],
        out_specs=[row(D_MODEL), row(POOL_WIDTH)] + streams,
        out_shape=[jax.ShapeDtypeStruct((S, D_MODEL), bf16), jax.ShapeDtypeStruct((S, POOL_WIDTH), f32)]
        + [_stream_shape(S, d) for d in DILATIONS] * 3,
        scratch_shapes=_stage_scratch(ts),
        compiler_params=_params("parallel"),
    )(x, g1, w_in, cos_t, sin_t)
    n = len(DILATIONS)
    return res[0], res[1], res[2:2 + n], res[2 + n:2 + 2 * n], res[2 + 2 * n:]


POOL_HALO = 16


def _pool_lane_group(rows):
    return lax.broadcasted_iota(jnp.int32, (rows, POOL_WIDTH), 1) // POOL_GROUP


def _pool_select(group, s2, s4, s8, s16):
    return jnp.where(group == 0, s2, jnp.where(group == 1, s4, jnp.where(group == 2, s8, s16)))


def _pool_count(t0, rows):
    group = _pool_lane_group(rows)
    t = t0 + lax.broadcasted_iota(jnp.int32, (rows, POOL_WIDTH), 0)
    win = _pool_select(group, 2, 4, 8, 16)
    return jnp.minimum(t + 1, win).astype(f32)


def _pool_diff(u_halo, u_tile, t0):
    ts = u_tile.shape[0]
    ext = jnp.concatenate([u_halo, u_tile], axis=0)
    s2 = ext + pltpu.roll(ext, 1, 0)
    s4 = s2 + pltpu.roll(s2, 2, 0)
    s8 = s4 + pltpu.roll(s4, 4, 0)
    s16 = s8 + pltpu.roll(s8, 8, 0)
    group = _pool_lane_group(ts + POOL_HALO)
    wsum = _pool_select(group, s2, s4, s8, s16)[POOL_HALO:]
    return wsum / _pool_count(t0, ts) - u_tile


def _pool_specs(ts, n_tiles):
    tile = pl.BlockSpec((ts, POOL_WIDTH), lambda i: (i, 0))
    per = ts // POOL_HALO
    before = pl.BlockSpec((POOL_HALO, POOL_WIDTH), lambda i: (jnp.maximum(i * per - 1, 0), 0))
    after = pl.BlockSpec((POOL_HALO, POOL_WIDTH), lambda i: (jnp.minimum((i + 1) * per, n_tiles * per - 1), 0))
    return tile, before, after


def _pool_fwd(u, w_bd, scale):
    S = u.shape[0]
    ts = 512
    n_tiles = S // ts

    def body(u_ref, halo_ref, w_ref, sc_ref, y_ref):
        i = pl.program_id(0)
        halo = jnp.where(i > 0, halo_ref[...], 0.0)
        d = _pool_diff(halo, u_ref[...], i * ts)
        y_ref[...] = (_dot(d.astype(bf16), w_ref[...]) * sc_ref[...]).astype(bf16)

    tile, before, _ = _pool_specs(ts, n_tiles)
    return pl.pallas_call(
        body, name="pool_fwd", grid=(n_tiles,),
        in_specs=[tile, before, pl.BlockSpec((POOL_WIDTH, POOL_WIDTH), lambda i: (0, 0)),
                  pl.BlockSpec((1, POOL_WIDTH), lambda i: (0, 0))],
        out_specs=tile, out_shape=jax.ShapeDtypeStruct((S, POOL_WIDTH), bf16),
        compiler_params=_params("parallel"),
    )(u, u, w_bd, scale)


def _pool_bwd(u, dy, w_bd, scale):
    S = u.shape[0]
    ts = 512
    n_tiles = S // ts

    def body(u_ref, halo_ref, dy_ref, dy_next_ref, w_ref, sc_ref, du_ref, dw_ref, dsc_ref):
        i = pl.program_id(0)

        @pl.when(i == 0)
        def _():
            dw_ref[...] = jnp.zeros_like(dw_ref)
            dsc_ref[...] = jnp.zeros_like(dsc_ref)

        halo = jnp.where(i > 0, halo_ref[...], 0.0)
        d = _pool_diff(halo, u_ref[...], i * ts).astype(bf16)
        w = w_ref[...]
        sc = sc_ref[...]
        dy_tile = dy_ref[...]
        z = _dot(d, w)
        dsc_ref[...] += jnp.sum(dy_tile * z, axis=0, keepdims=True)
        dy_next = jnp.where(i < n_tiles - 1, dy_next_ref[...], 0.0)
        dz = (jnp.concatenate([dy_tile, dy_next], axis=0) * sc).astype(bf16)
        dw_ref[...] += _dot_tn(d, dz[:ts])
        dd = _dot_nt(dz, w)
        e = dd / _pool_count(i * ts, ts + POOL_HALO)
        n = ts + POOL_HALO
        f2 = e + pltpu.roll(e, n - 1, 0)
        f4 = f2 + pltpu.roll(f2, n - 2, 0)
        f8 = f4 + pltpu.roll(f4, n - 4, 0)
        f16 = f8 + pltpu.roll(f8, n - 8, 0)
        fsum = _pool_select(_pool_lane_group(n), f2, f4, f8, f16)
        du_ref[...] = (fsum[:ts] - dd[:ts]).astype(bf16)

    tile, before, after = _pool_specs(ts, n_tiles)
    return pl.pallas_call(
        body, name="pool_bwd", grid=(n_tiles,),
        in_specs=[tile, before, tile, after, pl.BlockSpec((POOL_WIDTH, POOL_WIDTH), lambda i: (0, 0)),
                  pl.BlockSpec((1, POOL_WIDTH), lambda i: (0, 0))],
        out_specs=[tile, pl.BlockSpec((POOL_WIDTH, POOL_WIDTH), lambda i: (0, 0)),
                   pl.BlockSpec((1, POOL_WIDTH), lambda i: (0, 0))],
        out_shape=[jax.ShapeDtypeStruct((S, POOL_WIDTH), bf16), jax.ShapeDtypeStruct((POOL_WIDTH, POOL_WIDTH), f32),
                   jax.ShapeDtypeStruct((1, POOL_WIDTH), f32)],
        compiler_params=_params("arbitrary"),
    )(u, u, dy, dy, w_bd, scale)


SUPER = BLOCK * DILATIONS[-1]
UNITS = SUPER // BLOCK
FWD_UNROLL = 16
BWD_UNROLL = 8


def _band_mask(has_prev):
    qi = lax.broadcasted_iota(jnp.int32, (BLOCK, 2 * BLOCK), 0)
    kj = lax.broadcasted_iota(jnp.int32, (BLOCK, 2 * BLOCK), 1)
    return (kj >= qi) & (kj <= qi + BLOCK) & ((kj >= BLOCK) | has_prev)


def _head0_mask(rows=BLOCK):
    return lax.broadcasted_iota(jnp.int32, (rows, 128), 1) < HEAD_DIM


def _band_mask_t(has_prev):
    ki = lax.broadcasted_iota(jnp.int32, (2 * BLOCK, 2 * BLOCK), 0)
    qj = lax.broadcasted_iota(jnp.int32, (2 * BLOCK, 2 * BLOCK), 1) % BLOCK
    return (ki >= qj) & (ki <= qj + BLOCK) & ((ki >= BLOCK) | has_prev)


def _head_pair_rows(a, h0):
    zero = jnp.zeros_like(a)
    return jnp.concatenate([jnp.where(h0, a, zero), jnp.where(h0, zero, a)], axis=0)


def _per_query_row(stat):
    t = stat.T
    return jnp.concatenate([jnp.concatenate([t[:HEAD_DIM]] * 4, axis=0), jnp.concatenate([t[HEAD_DIM:]] * 4, axis=0)],
                           axis=1)


def _natural_rows(d, r, n):
    if d == 1:
        return pl.ds(pl.multiple_of(n * BLOCK, BLOCK), BLOCK)
    return pl.ds(n * (BLOCK * d) + r, BLOCK, stride=d)


def _unit_place(d, u):
    per_stream = UNITS // d
    return u // per_stream, u % per_stream, per_stream


def _block_rows(n):
    return pl.ds(pl.multiple_of(n * BLOCK, BLOCK), BLOCK)


def _band(cur_ref, tail_ref, r, n):
    before = jnp.where(n > 0, cur_ref[r, _block_rows(jnp.maximum(n - 1, 0)), :], tail_ref[r])
    return jnp.concatenate([before, cur_ref[r, _block_rows(n), :]], axis=0)


def _attn_in_specs(S, with_do):
    specs = []
    last = S // SUPER - 1
    for d in DILATIONS:
        per_stream = UNITS // d
        cur = pl.BlockSpec((d, SUPER // d, 128), lambda hp, sb: (0, jnp.minimum(sb, last), hp))
        tail = pl.BlockSpec(
            (d, BLOCK, 128),
            lambda hp, sb, per_stream=per_stream: (0, jnp.maximum(jnp.minimum(sb, last) * per_stream - 1, 0), hp))
        specs += [cur] * (2 if with_do else 1) + [cur, tail, cur, tail]
    return specs


def _attn_fwd(qs, ks, vs, pack):
    S = qs[0].shape[1]
    n_dil = len(DILATIONS)
    n_steps = S // SUPER
    n_total = (ATTN_WIDTH // 128) * n_steps

    def body(*refs):
        ins, pack_ref = refs[:5 * n_dil], refs[5 * n_dil]
        out_ref, lse_ref, gathered_ref = refs[5 * n_dil + 1:5 * n_dil + 4]
        scratch = refs[5 * n_dil + 4:]
        o_sc, l_sc = scratch[:n_dil], scratch[n_dil:2 * n_dil]
        gather = _Gather(pack_ref, gathered_ref, *scratch[2 * n_dil:])
        sb = pl.program_id(1)
        step = pl.program_id(0) * n_steps + sb

        @pl.when(step == 0)
        def _():
            gather.start()

        h0 = _head0_mask()
        for ci, d in enumerate(DILATIONS):
            q_ref, kc_ref, kp_ref, vc_ref, vp_ref = ins[5 * ci:5 * ci + 5]

            def unit(u, carry, d=d, ci=ci, q_ref=q_ref, kc_ref=kc_ref, kp_ref=kp_ref, vc_ref=vc_ref, vp_ref=vp_ref):
                r, n, _ = _unit_place(d, u)
                qv = q_ref[r, _block_rows(n), :]
                kb = _band(kc_ref, kp_ref, r, n)
                vb = _band(vc_ref, vp_ref, r, n)
                valid = _band_mask((sb > 0) | (n > 0))
                outs, lses = [], []
                for h in range(2):
                    keep = h0 if h == 0 else jnp.logical_not(h0)
                    qh = jnp.where(keep, qv, jnp.zeros_like(qv))
                    s = jnp.where(valid, _dot_nt(qh, kb), NEG)
                    m = jnp.max(s, axis=1, keepdims=True)
                    e = jnp.exp(s - m)
                    den = jnp.sum(e, axis=1, keepdims=True)
                    outs.append(_dot(e.astype(bf16), vb) * (1.0 / den))
                    lses.append(jnp.broadcast_to(m + jnp.log(den), (BLOCK, 128)))
                rows = _natural_rows(d, r, n)
                o_sc[ci][rows, :] = jnp.where(h0, outs[0], outs[1])
                l_sc[ci][rows, :] = jnp.where(h0, lses[0], lses[1])
                return carry

            lax.fori_loop(0, UNITS, unit, 0, unroll=FWD_UNROLL)

        def merge(t, carry):
            rows = pl.ds(pl.multiple_of(t * 256, 256), 256)
            a, b, c = l_sc[0][rows, :], l_sc[1][rows, :], l_sc[2][rows, :]
            m = jnp.maximum(jnp.maximum(a, b), c)
            ea, eb, ec = jnp.exp(a - m), jnp.exp(b - m), jnp.exp(c - m)
            tot = ea + eb + ec
            out_ref[rows, :] = ((ea / tot) * o_sc[0][rows, :] + (eb / tot) * o_sc[1][rows, :]
                                + (ec / tot) * o_sc[2][rows, :]).astype(bf16)
            lse_ref[rows, :] = m + jnp.log(tot)
            return carry

        lax.fori_loop(0, SUPER // 256, merge, 0)

        @pl.when(step == (2 * n_total) // 3)
        def _():
            gather.pass_on()

        @pl.when(step == n_total - 1)
        def _():
            gather.finish()

    args = []
    for q, k, v in zip(qs, ks, vs):
        args += [q, k, k, v, v]
    nat = pl.BlockSpec((SUPER, 128), lambda hp, sb: (sb, hp))
    rows = pack.shape[0]
    return pl.pallas_call(
        body, name="attn_fwd", grid=(ATTN_WIDTH // 128, n_steps),
        in_specs=_attn_in_specs(S, False) + [ANY], out_specs=[nat, nat, ANY],
        out_shape=[jax.ShapeDtypeStruct((S, ATTN_WIDTH), bf16), jax.ShapeDtypeStruct((S, ATTN_WIDTH), f32),
                   _Gather.out_shape(rows, pack.dtype)],
        scratch_shapes=[pltpu.VMEM((SUPER, 128), f32)] * (2 * n_dil) + _Gather.scratch(rows, pack.dtype),
        compiler_params=_params("arbitrary", "arbitrary"),
    )(*args, pack)


def _attn_bwd(qs, ks, vs, dos, lse, delta, chip_sum):
    S = qs[0].shape[1]
    n_steps = S // SUPER
    last = n_steps - 1
    n_dil = len(DILATIONS)
    n_total = (ATTN_WIDTH // 128) * (n_steps + 1)

    def body(*refs):
        ins, (lse_ref, dl_ref, sum_ref) = refs[:6 * n_dil], refs[6 * n_dil:6 * n_dil + 3]
        dq_ref, dk_ref, dv_ref, others_ref = refs[6 * n_dil + 3:6 * n_dil + 7]
        dq_acc, dk_acc, dv_acc = refs[6 * n_dil + 7:6 * n_dil + 10]
        scatter = _Scatter(sum_ref, others_ref, *refs[6 * n_dil + 10:])
        sb = pl.program_id(1)
        step = pl.program_id(0) * (n_steps + 1) + sb
        cur = sb % 2
        prv = 1 - cur

        @pl.when(step == 0)
        def _():
            scatter.start()

        @pl.when(sb < n_steps)
        def _():
            dq_acc[...] = jnp.zeros_like(dq_acc)
            dk_acc[cur] = jnp.zeros((SUPER, 128), f32)
            dv_acc[cur] = jnp.zeros((SUPER, 128), f32)
            h0 = _head0_mask()
            for ci, d in enumerate(DILATIONS):
                q_ref, do_ref, kc_ref, kp_ref, vc_ref, vp_ref = ins[6 * ci:6 * ci + 6]

                def unit(u, carry, d=d, q_ref=q_ref, do_ref=do_ref, kc_ref=kc_ref, kp_ref=kp_ref, vc_ref=vc_ref,
                         vp_ref=vp_ref):
                    r, n, per_stream = _unit_place(d, u)
                    qv = q_ref[r, _block_rows(n), :]
                    dov = do_ref[r, _block_rows(n), :]
                    kb = _band(kc_ref, kp_ref, r, n)
                    vb = _band(vc_ref, vp_ref, r, n)
                    rows = _natural_rows(d, r, n)
                    has_prev = (sb > 0) | (n > 0)
                    q_pair = _head_pair_rows(qv, h0)
                    do_pair = _head_pair_rows(dov, h0)
                    s_t = jnp.where(_band_mask_t(has_prev), _dot_nt(kb, q_pair), NEG)
                    p_t = jnp.exp(s_t - _per_query_row(lse_ref[rows, :]))
                    dp_t = _dot_nt(vb, do_pair)
                    ds_t = (p_t * (dp_t - _per_query_row(dl_ref[rows, :]))).astype(bf16)
                    dvb = _dot(p_t.astype(bf16), do_pair)
                    dkb = _dot(ds_t, q_pair)
                    dq_pair = _dot_tn(ds_t, kb)
                    dq_acc[rows, :] += jnp.where(h0, dq_pair[:BLOCK], dq_pair[BLOCK:])
                    dk_acc[cur, rows, :] += dkb[BLOCK:]
                    dv_acc[cur, rows, :] += dvb[BLOCK:]

                    slot = jnp.where((n > 0) | (sb == 0), cur, prv)
                    before = _natural_rows(d, r, jnp.where(n > 0, n - 1, per_stream - 1))
                    dk_acc[slot, before, :] += dkb[:BLOCK]
                    dv_acc[slot, before, :] += dvb[:BLOCK]
                    return carry

                lax.fori_loop(0, UNITS, unit, 0, unroll=BWD_UNROLL)
            dq_ref[...] = (dq_acc[...] * ATTN_SCALE).astype(bf16)

        @pl.when(sb > 0)
        def _():
            dk_ref[...] = dk_acc[prv].astype(bf16)
            dv_ref[...] = dv_acc[prv].astype(bf16)

        @pl.when(step == n_total - 1)
        def _():
            scatter.finish()

    args = []
    for q, k, v, do in zip(qs, ks, vs, dos):
        args += [q, do, k, k, v, v]
    nat = pl.BlockSpec((SUPER, 128), lambda hp, sb: (jnp.minimum(sb, last), hp))
    nat_before = pl.BlockSpec((SUPER, 128), lambda hp, sb: (jnp.clip(sb - 1, 0, last), hp))
    out = jax.ShapeDtypeStruct((S, ATTN_WIDTH), bf16)
    half = chip_sum.shape[1]
    return pl.pallas_call(
        body, name="attn_bwd", grid=(ATTN_WIDTH // 128, n_steps + 1),
        in_specs=_attn_in_specs(S, True) + [nat, nat, ANY], out_specs=[nat, nat_before, nat_before, ANY],
        out_shape=[out, out, out, _Scatter.out_shape(half, chip_sum.dtype)],
        scratch_shapes=[pltpu.VMEM((SUPER, 128), f32), pltpu.VMEM((2, SUPER, 128), f32),
                        pltpu.VMEM((2, SUPER, 128), f32)] + _Scatter.scratch(half),
        compiler_params=_params("arbitrary", "arbitrary"),
    )(*args, lse, delta, chip_sum)


def _rms(v):
    return lax.rsqrt(jnp.mean(v * v, axis=-1, keepdims=True) + EPS)


def _out_proj(pool_out, attn_out, w_out, x, g2, g3):
    S = x.shape[0]
    ts = 512

    def body(p_ref, a_ref, w_ref, x_ref, g2_ref, g3_ref, mix_ref, x2_ref, h2_ref):
        mix = _dot(p_ref[...], w_ref[:POOL_WIDTH, :]) + _dot(a_ref[...], w_ref[POOL_WIDTH:, :])
        mix_ref[...] = mix
        x2 = x_ref[...] + (mix * _rms(mix)) * g2_ref[...]
        x2_ref[...] = x2
        h2_ref[...] = ((x2 * _rms(x2)) * g3_ref[...]).astype(bf16)

    row = lambda w: pl.BlockSpec((ts, w), lambda i: (i, 0))
    gain = pl.BlockSpec((1, D_MODEL), lambda i: (0, 0))
    return pl.pallas_call(
        body, name="out_proj", grid=(S // ts,),
        in_specs=[row(POOL_WIDTH), row(ATTN_WIDTH), pl.BlockSpec((D_MODEL, D_MODEL), lambda i: (0, 0)),
                  row(D_MODEL), gain, gain],
        out_specs=[row(D_MODEL)] * 3,
        out_shape=[jax.ShapeDtypeStruct((S, D_MODEL), f32), jax.ShapeDtypeStruct((S, D_MODEL), f32),
                   jax.ShapeDtypeStruct((S, D_MODEL), bf16)],
        compiler_params=_params("parallel"),
    )(pool_out, attn_out, w_out, x, g2, g3)


FF_TILE = 256
FF_STEP_ROWS = 2048
FF_FWD_STEP_ROWS = 1024
FF_ROWS = 256


def _sigmoid(g):
    return 1.0 / (1.0 + jnp.exp(-g))


def _ff_act_shape(S):
    return jax.ShapeDtypeStruct((D_FF // FF_TILE, S, FF_TILE), bf16)


def _ff_act_spec(ts):
    return pl.BlockSpec((1, ts, FF_TILE), lambda i, j: (j, i, 0))


def _ffn_fwd(h2, w_gate, w_up, w_down, x2, target, g4):
    S = h2.shape[0]
    ts = min(S, FF_FWD_STEP_ROWS)
    n_tiles = D_FF // FF_TILE

    def body(h_ref, wg_ref, wu_ref, wd_ref, x2_ref, t_ref, g_ref, gate_ref, up_ref, dy_ref, df_ref, dg_ref, loss_ref,
             f_acc):
        def rows_pass(first):
            def sub(i, carry):
                rows = pl.ds(pl.multiple_of(i * FF_ROWS, FF_ROWS), FF_ROWS)
                h = h_ref[rows, :]
                gate = _dot_nt(h, wg_ref[...])
                up = _dot_nt(h, wu_ref[...])
                gate_ref[0, rows, :] = gate.astype(bf16)
                up_ref[0, rows, :] = up.astype(bf16)
                part = _dot((gate * _sigmoid(gate) * up).astype(bf16), wd_ref[...])
                if first:
                    f_acc[rows, :] = part
                else:
                    f_acc[rows, :] += part
                return carry

            lax.fori_loop(0, ts // FF_ROWS, sub, 0, unroll=True)

        @pl.when(pl.program_id(1) == 0)
        def _():
            rows_pass(True)

        @pl.when(pl.program_id(1) > 0)
        def _():
            rows_pass(False)

        @pl.when((pl.program_id(0) == 0) & (pl.program_id(1) == 0))
        def _():
            dg_ref[...] = jnp.zeros_like(dg_ref)
            loss_ref[...] = jnp.zeros_like(loss_ref)

        @pl.when(pl.program_id(1) == n_tiles - 1)
        def _():
            g = g_ref[...]

            def head(i, carry):
                rows = pl.ds(pl.multiple_of(i * FF_ROWS, FF_ROWS), FF_ROWS)
                fv = f_acc[rows, :]
                r = _rms(fv)
                fhat = fv * r
                err = (x2_ref[rows, :] + fhat * g) - t_ref[rows, :]
                loss_ref[...] += 0.5 * jnp.sum(jnp.mean(err * err, axis=-1, keepdims=True), axis=0, keepdims=True)
                dy = err * (1.0 / D_MODEL)
                dy_ref[rows, :] = dy
                dg_ref[...] += jnp.sum(dy * fhat, axis=0, keepdims=True)
                dyg = dy * g
                df_ref[rows, :] = (r * (dyg - fhat * jnp.mean(dyg * fhat, axis=-1, keepdims=True))).astype(bf16)
                return carry

            lax.fori_loop(0, ts // FF_ROWS, head, 0)

    act = _ff_act_spec(ts)
    row = pl.BlockSpec((ts, D_MODEL), lambda i, j: (i, 0))
    weight = pl.BlockSpec((FF_TILE, D_MODEL), lambda i, j: (j, 0))
    gain = pl.BlockSpec((1, D_MODEL), lambda i, j: (0, 0))
    return pl.pallas_call(
        body, name="ffn_fwd", grid=(S // ts, n_tiles),
        in_specs=[row, weight, weight, weight, row, row, gain],
        out_specs=[act, act, row, row, gain, pl.BlockSpec((1, 1), lambda i, j: (0, 0))],
        out_shape=[_ff_act_shape(S), _ff_act_shape(S), jax.ShapeDtypeStruct((S, D_MODEL), f32),
                   jax.ShapeDtypeStruct((S, D_MODEL), bf16), jax.ShapeDtypeStruct((1, D_MODEL), f32),
                   jax.ShapeDtypeStruct((1, 1), f32)],
        scratch_shapes=[pltpu.VMEM((ts, D_MODEL), f32)],
        compiler_params=_params("arbitrary", "arbitrary"),
    )(h2, w_gate, w_up, w_down, x2, target, g4)


def _ffn_bwd(df, gate, up, w_gate, w_up, w_down):
    S = df.shape[0]
    ts = min(S, FF_STEP_ROWS)

    def body(df_ref, gate_ref, up_ref, wg_ref, wu_ref, wd_ref, a_ref, dgate_ref, dup_ref, dh_ref):
        def rows_pass(first):
            def sub(i, carry):
                rows = pl.ds(pl.multiple_of(i * FF_ROWS, FF_ROWS), FF_ROWS)
                da = _dot_nt(df_ref[rows, :], wd_ref[...])
                g = gate_ref[0, rows, :].astype(f32)
                u = up_ref[0, rows, :].astype(f32)
                sig = _sigmoid(g)
                silu = g * sig
                a_ref[0, rows, :] = (silu * u).astype(bf16)
                dup = (da * silu).astype(bf16)
                dgate = (da * u * (sig * (1.0 + g * (1.0 - sig)))).astype(bf16)
                dup_ref[0, rows, :] = dup
                dgate_ref[0, rows, :] = dgate
                part = _dot(dgate, wg_ref[...]) + _dot(dup, wu_ref[...])
                if first:
                    dh_ref[rows, :] = part
                else:
                    dh_ref[rows, :] += part
                return carry

            lax.fori_loop(0, ts // FF_ROWS, sub, 0, unroll=True)

        @pl.when(pl.program_id(1) == 0)
        def _():
            rows_pass(True)

        @pl.when(pl.program_id(1) > 0)
        def _():
            rows_pass(False)

    act = _ff_act_spec(ts)
    row = pl.BlockSpec((ts, D_MODEL), lambda i, j: (i, 0))
    return pl.pallas_call(
        body, name="ffn_bwd", grid=(S // ts, D_FF // FF_TILE),
        in_specs=[row, act, act,
                  pl.BlockSpec((FF_TILE, D_MODEL), lambda i, j: (j, 0)),
                  pl.BlockSpec((FF_TILE, D_MODEL), lambda i, j: (j, 0)),
                  pl.BlockSpec((FF_TILE, D_MODEL), lambda i, j: (j, 0))],
        out_specs=[act, act, act, row],
        out_shape=[_ff_act_shape(S)] * 3 + [jax.ShapeDtypeStruct((S, D_MODEL), f32)],
        compiler_params=_params("parallel", "arbitrary"),
    )(df, gate, up, w_gate, w_up, w_down)


def _norm_bwd(dh2, dy, x2, mix, g3, g2):
    S = dh2.shape[0]
    ts = 512

    def body(dh_ref, dy_ref, x2_ref, mix_ref, g3_ref, g2_ref, dx2_ref, dmix_ref, dg3_ref, dg2_ref):
        @pl.when(pl.program_id(0) == 0)
        def _():
            dg3_ref[...] = jnp.zeros_like(dg3_ref)
            dg2_ref[...] = jnp.zeros_like(dg2_ref)

        dh = dh_ref[...]
        x2 = x2_ref[...]
        r3 = _rms(x2)
        xhat = x2 * r3
        dg3_ref[...] += jnp.sum(dh * xhat, axis=0, keepdims=True)
        dhg = dh * g3_ref[...]
        dx2 = dy_ref[...] + r3 * (dhg - xhat * jnp.mean(dhg * xhat, axis=-1, keepdims=True))
        dx2_ref[...] = dx2
        mix = mix_ref[...]
        r2 = _rms(mix)
        mhat = mix * r2
        dg2_ref[...] += jnp.sum(dx2 * mhat, axis=0, keepdims=True)
        dmg = dx2 * g2_ref[...]
        dmix_ref[...] = (r2 * (dmg - mhat * jnp.mean(dmg * mhat, axis=-1, keepdims=True))).astype(bf16)

    row = pl.BlockSpec((ts, D_MODEL), lambda i: (i, 0))
    gain = pl.BlockSpec((1, D_MODEL), lambda i: (0, 0))
    return pl.pallas_call(
        body, name="norm_bwd", grid=(S // ts,), in_specs=[row, row, row, row, gain, gain],
        out_specs=[row, row, gain, gain],
        out_shape=[jax.ShapeDtypeStruct((S, D_MODEL), f32), jax.ShapeDtypeStruct((S, D_MODEL), bf16),
                   jax.ShapeDtypeStruct((1, D_MODEL), f32), jax.ShapeDtypeStruct((1, D_MODEL), f32)],
        compiler_params=_params("arbitrary"),
    )(dh2, dy, x2, mix, g3, g2)


def _out_proj_bwd(dmix, w_out, attn_out, head_ones, grads):
    S = dmix.shape[0]
    ts = 512
    n_dil = len(DILATIONS)

    def body(dm_ref, w_ref, o_ref, ones_ref, g_ref, dp_ref, dl_ref, *rest):
        do_refs, theirs_ref = rest[:n_dil], rest[n_dil]
        stage = rest[n_dil + 1:n_dil + 1 + N_STAGE]
        swap = _Swap(g_ref, theirs_ref, *rest[n_dil + 1 + N_STAGE:])

        @pl.when(pl.program_id(0) == 0)
        def _():
            swap.start()

        @pl.when(pl.program_id(0) == S // ts - 1)
        def _():
            swap.finish()

        dcat = _dot_nt(dm_ref[...], w_ref[...])
        dp_ref[...] = dcat[:, :POOL_WIDTH]
        do = dcat[:, POOL_WIDTH:]
        for j in range(ATTN_WIDTH // 128):
            stage[j][...] = do[:, j * 128:(j + 1) * 128]
        _store_streams(stage, do_refs, ts)
        prod = do * o_ref[...].astype(f32)
        hi = prod.astype(bf16)
        lo = (prod - hi.astype(f32)).astype(bf16)
        dl_ref[...] = _dot(hi, ones_ref[...]) + _dot(lo, ones_ref[...])

    row = lambda w: pl.BlockSpec((ts, w), lambda i: (i, 0))
    res = pl.pallas_call(
        body, name="out_proj_bwd", grid=(S // ts,),
        in_specs=[row(D_MODEL), pl.BlockSpec((D_MODEL, D_MODEL), lambda i: (0, 0)), row(ATTN_WIDTH),
                  pl.BlockSpec((ATTN_WIDTH, ATTN_WIDTH), lambda i: (0, 0)), ANY],
        out_specs=[row(POOL_WIDTH), row(ATTN_WIDTH)] + [_stream_spec(d, ts) for d in DILATIONS] + [ANY],
        out_shape=[jax.ShapeDtypeStruct((S, POOL_WIDTH), f32), jax.ShapeDtypeStruct((S, ATTN_WIDTH), f32)]
        + [_stream_shape(S, d) for d in DILATIONS] + [_Swap.out_shape(grads)],
        scratch_shapes=_stage_scratch(ts) + _Swap.scratch(grads),
        compiler_params=_params("arbitrary"),
    )(dmix, w_out, attn_out, head_ones, grads)
    return res[0], res[1], res[2:2 + n_dil], res[2 + n_dil]


def _in_proj_bwd(du, dq, dk, dv, cos_t, sin_t, w_in, x, dx2, g1):
    S = x.shape[0]
    ts = 256

    def body(du_ref, dq_ref, dk_ref, dv_ref, cos_ref, sin_ref, w_ref, x_ref, dx2_ref, g_ref, gx_ref, dproj_ref, dg_ref):
        @pl.when(pl.program_id(0) == 0)
        def _():
            dg_ref[...] = jnp.zeros_like(dg_ref)

        dproj_ref[:, :POOL_WIDTH] = du_ref[...]
        cos = cos_ref[...]
        sin = sin_ref[...]
        first = _first_half_mask(ts)
        for j in range(ATTN_WIDTH // 128):
            cols = slice(j * 128, (j + 1) * 128)
            for base, ref in ((POOL_WIDTH, dq_ref), (POOL_WIDTH + ATTN_WIDTH, dk_ref)):
                g = ref[:, cols].astype(f32)
                pre = g * cos + _rope_partner(g * sin, first)
                dproj_ref[:, base + j * 128: base + (j + 1) * 128] = pre.astype(bf16)
        dproj_ref[:, POOL_WIDTH + 2 * ATTN_WIDTH:] = dv_ref[...]

        dh = _dot(dproj_ref[...], w_ref[...])
        xv = x_ref[...]
        r = _rms(xv)
        xhat = xv * r
        dg_ref[...] += jnp.sum(dh * xhat, axis=0, keepdims=True)
        dhg = dh * g_ref[...]
        gx_ref[...] = dx2_ref[...] + r * (dhg - xhat * jnp.mean(dhg * xhat, axis=-1, keepdims=True))

    row = lambda w: pl.BlockSpec((ts, w), lambda i: (i, 0))
    gain = pl.BlockSpec((1, D_MODEL), lambda i: (0, 0))
    return pl.pallas_call(
        body, name="in_proj_bwd", grid=(S // ts,),
        in_specs=[row(POOL_WIDTH)] + [row(ATTN_WIDTH)] * 3 + [row(128), row(128),
                  pl.BlockSpec((IN_WIDTH, D_MODEL), lambda i: (0, 0)), row(D_MODEL), row(D_MODEL), gain],
        out_specs=[row(D_MODEL), row(IN_WIDTH), gain],
        out_shape=[jax.ShapeDtypeStruct((S, D_MODEL), f32), jax.ShapeDtypeStruct((S, IN_WIDTH), bf16),
                   jax.ShapeDtypeStruct((1, D_MODEL), f32)],
        compiler_params=_params("arbitrary"),
    )(du, dq, dk, dv, cos_t, sin_t, w_in, x, dx2, g1)


def _matmul_tiles_tn(a, b, name):
    T, K, w = a.shape
    N = b.shape[1]
    tk = 1024

    def body(a_ref, b_ref, o_ref):
        def tiles_pass(first):
            for t in range(T):
                part = _dot_tn(a_ref[t], b_ref[...])
                if first:
                    o_ref[t * w:(t + 1) * w, :] = part
                else:
                    o_ref[t * w:(t + 1) * w, :] += part

        @pl.when(pl.program_id(0) == 0)
        def _():
            tiles_pass(True)

        @pl.when(pl.program_id(0) > 0)
        def _():
            tiles_pass(False)

    return pl.pallas_call(
        body, name=name, grid=(K // tk,),
        in_specs=[pl.BlockSpec((T, tk, w), lambda k: (0, k, 0)), pl.BlockSpec((tk, N), lambda k: (k, 0))],
        out_specs=pl.BlockSpec((T * w, N), lambda k: (0, 0)),
        out_shape=jax.ShapeDtypeStruct((T * w, N), f32),
        compiler_params=_params("arbitrary"),
    )(a, b)


def _matmul_tn(a, b, tn, name):
    K, M = a.shape
    N = b.shape[1]
    tk = 1024

    def body(a_ref, b_ref, o_ref):
        part = _dot_tn(a_ref[...], b_ref[...])

        @pl.when(pl.program_id(1) == 0)
        def _():
            o_ref[...] = part

        @pl.when(pl.program_id(1) > 0)
        def _():
            o_ref[...] += part

    return pl.pallas_call(
        body, name=name, grid=(N // tn, K // tk),
        in_specs=[pl.BlockSpec((tk, M), lambda n, k: (k, 0)), pl.BlockSpec((tk, tn), lambda n, k: (k, n))],
        out_specs=pl.BlockSpec((M, tn), lambda n, k: (0, n)),
        out_shape=jax.ShapeDtypeStruct((M, N), f32),
        compiler_params=_params("parallel", "arbitrary"),
    )(a, b)


def _rope_tables(S):
    half = HEAD_DIM // 2
    freqs = ROPE_THETA ** (-jnp.arange(half, dtype=f32) * (2.0 / HEAD_DIM))
    ang = jnp.arange(S).astype(f32)[:, None] * freqs[None, :]
    cos = jnp.tile(jnp.cos(ang), (1, 4))
    sin = jnp.sin(ang)
    sin = jnp.tile(jnp.concatenate([-sin, sin], axis=1), (1, 2))
    return cos, sin


def _block_diag(w_pool):
    w = jnp.zeros((POOL_WIDTH, POOL_WIDTH), w_pool.dtype)
    for g in range(POOL_WIDTH // POOL_GROUP):
        w = lax.dynamic_update_slice(w, w_pool[g], (g * POOL_GROUP, g * POOL_GROUP))
    return w


def _head_ones():
    head = np.arange(ATTN_WIDTH) // HEAD_DIM
    return jnp.asarray(head[:, None] == head[None, :], dtype=bf16)


def _place():
    x, y, c = lax.axis_index("x"), lax.axis_index("y"), lax.axis_index("c")
    chips = [(1 - x, y), (x, 1 - y), (1 - x, 1 - y)]
    return x, y, c, chips


ANY = pl.BlockSpec(memory_space=pl.ANY)
N_PEER_CHIPS = N_CHIPS - 1
ICI_PIECES = 4
D2D_PIECES = 8
LOCAL_PIECES = 8


def _row_chunks(rows, n, unit=32):
    units = rows // unit
    out, start = [], 0
    for i in range(n):
        size = (units // n + (1 if i < units % n else 0)) * unit
        out.append((start, size))
        start += size
    return [piece for piece in out if piece[1]]


class _LocalCopy:
    def __init__(self, src_rows, dst_rows, rows, buf, sems_in, sems_out):
        self.loads, self.stores = [], []
        for i, (start, size) in enumerate(_row_chunks(rows, LOCAL_PIECES)):
            r = pl.ds(start, size)
            self.loads.append(pltpu.make_async_copy(src_rows(r), buf.at[r], sems_in.at[i]))
            self.stores.append(pltpu.make_async_copy(buf.at[r], dst_rows(r), sems_out.at[i]))

    def start(self):
        for cp in self.loads:
            cp.start()

    def pass_on(self):
        for load, store in zip(self.loads, self.stores):
            load.wait()
            store.start()

    def finish(self):
        for store in self.stores:
            store.wait()

    @staticmethod
    def scratch(rows, dtype):
        return [pltpu.VMEM((rows, D_MODEL), dtype), pltpu.SemaphoreType.DMA((LOCAL_PIECES,)),
                pltpu.SemaphoreType.DMA((LOCAL_PIECES,))]


class _Gather:
    def __init__(self, w_ref, out_ref, send1, recv1, send2, recv2, buf, sems_in, sems_out):
        x, y, c, chips = _place()
        me = 2 * x + y
        rows = w_ref.shape[0]
        half = rows // 2
        pieces = _row_chunks(half, ICI_PIECES)
        self.own = _LocalCopy(lambda r: w_ref.at[r], lambda r: out_ref.at[me, r], rows, buf, sems_in, sems_out)

        def rows_of(core, piece):
            start, size = piece
            return pl.ds(core * half + start, size)

        self.sends, self.arrivals, self.forwards, self.forward_arrivals = [], [], [], []
        for i, piece in enumerate(pieces):
            for j, (cx, cy) in enumerate(chips):
                k = j * len(pieces) + i
                there = 2 * cx + cy

                def direct(src_chip, cx=cx, cy=cy, k=k, piece=piece):
                    return pltpu.make_async_remote_copy(
                        src_ref=w_ref.at[rows_of(c, piece)], dst_ref=out_ref.at[src_chip, rows_of(c, piece)],
                        send_sem=send1.at[k], recv_sem=recv1.at[k], device_id=(cx, cy, c), device_id_type=MESH)

                def passed(core, there=there, k=k, piece=piece):
                    return pltpu.make_async_remote_copy(
                        src_ref=out_ref.at[there, rows_of(core, piece)], dst_ref=out_ref.at[there, rows_of(core, piece)],
                        send_sem=send2.at[k], recv_sem=recv2.at[k], device_id=(x, y, 1 - c), device_id_type=MESH)

                self.sends.append(direct(me))
                self.arrivals.append(direct(there))
                self.forwards.append(passed(c))
                self.forward_arrivals.append(passed(1 - c))

    def start(self):
        for cp in self.sends:
            cp.start()
        self.own.start()

    def pass_on(self):
        self.own.pass_on()
        for arrival, forward in zip(self.arrivals, self.forwards):
            arrival.wait_recv()
            forward.start()

    def finish(self):
        for arrival in self.forward_arrivals:
            arrival.wait_recv()
        for cp in self.sends + self.forwards:
            cp.wait_send()
        self.own.finish()

    @staticmethod
    def scratch(rows, dtype):
        n = N_PEER_CHIPS * len(_row_chunks(rows // 2, ICI_PIECES))
        return [pltpu.SemaphoreType.DMA((n,))] * 4 + _LocalCopy.scratch(rows, dtype)

    @staticmethod
    def out_shape(rows, dtype):
        return jax.ShapeDtypeStruct((N_CHIPS, rows, D_MODEL), dtype)


def _gather_weights(pack):
    rows = pack.shape[0]

    def body(w_ref, out_ref, *scratch):
        gather = _Gather(w_ref, out_ref, *scratch)
        gather.start()
        gather.pass_on()
        gather.finish()

    return pl.pallas_call(
        body, name="gather_weights", in_specs=[ANY], out_specs=ANY, out_shape=_Gather.out_shape(rows, pack.dtype),
        scratch_shapes=_Gather.scratch(rows, pack.dtype),
        compiler_params=pltpu.CompilerParams(vmem_limit_bytes=VMEM_LIMIT_V7X),
    )(pack)


class _Scatter:
    def __init__(self, h_ref, out_ref, send, recv):
        x, y, c, chips = _place()
        pieces = _row_chunks(h_ref.shape[1], ICI_PIECES)
        self.copies = []
        for i, (start, size) in enumerate(pieces):
            for j, (cx, cy) in enumerate(chips):
                k = j * len(pieces) + i
                self.copies.append(pltpu.make_async_remote_copy(
                    src_ref=h_ref.at[2 * cx + cy, pl.ds(start, size)], dst_ref=out_ref.at[j, pl.ds(start, size)],
                    send_sem=send.at[k], recv_sem=recv.at[k], device_id=(cx, cy, c), device_id_type=MESH))

    def start(self):
        for cp in self.copies:
            cp.start()

    def finish(self):
        for cp in self.copies:
            cp.wait_recv()
        for cp in self.copies:
            cp.wait_send()

    @staticmethod
    def scratch(half):
        n = N_PEER_CHIPS * len(_row_chunks(half, ICI_PIECES))
        return [pltpu.SemaphoreType.DMA((n,))] * 2

    @staticmethod
    def out_shape(half, dtype):
        return jax.ShapeDtypeStruct((N_PEER_CHIPS, half, D_MODEL), dtype)


def _scatter_to_chips(h):
    half = h.shape[1]

    def body(h_ref, out_ref, send, recv):
        scatter = _Scatter(h_ref, out_ref, send, recv)
        scatter.start()
        scatter.finish()

    return pl.pallas_call(
        body, name="scatter_to_chips", in_specs=[ANY], out_specs=ANY, out_shape=_Scatter.out_shape(half, h.dtype),
        scratch_shapes=_Scatter.scratch(half),
    )(h)


class _Swap:
    def __init__(self, g_ref, theirs_ref, send, recv):
        x, y, c, _ = _place()
        half = g_ref.shape[1] // 2
        pieces = _row_chunks(half, D2D_PIECES)
        self.copies = []
        for s in range(N_CHIPS):
            for i, (start, size) in enumerate(pieces):
                k = s * len(pieces) + i
                self.copies.append(pltpu.make_async_remote_copy(
                    src_ref=g_ref.at[s, pl.ds((1 - c) * half + start, size)], dst_ref=theirs_ref.at[s, pl.ds(start, size)],
                    send_sem=send.at[k], recv_sem=recv.at[k], device_id=(x, y, 1 - c), device_id_type=MESH))

    def start(self):
        for cp in self.copies:
            cp.start()

    def finish(self):
        for cp in self.copies:
            cp.wait()

    @staticmethod
    def scratch(g):
        n = N_CHIPS * len(_row_chunks(g.shape[1] // 2, D2D_PIECES))
        return [pltpu.SemaphoreType.DMA((n,))] * 2

    @staticmethod
    def out_shape(g):
        return jax.ShapeDtypeStruct((N_CHIPS, g.shape[1] // 2, D_MODEL), g.dtype)


def _swap_halves(g):
    def body(g_ref, theirs_ref, send, recv):
        swap = _Swap(g_ref, theirs_ref, send, recv)
        swap.start()
        swap.finish()

    return pl.pallas_call(
        body, name="swap_halves", in_specs=[ANY], out_specs=ANY, out_shape=_Swap.out_shape(g),
        scratch_shapes=_Swap.scratch(g),
    )(g)


ADD_TILE_MAX_ROWS = 600


def _add_tile(half):
    return max(t for t in range(8, ADD_TILE_MAX_ROWS + 1, 8) if half % t == 0)


def _add_cores(g, theirs, name, out_dtype=f32):
    half = theirs.shape[1]
    tr = _add_tile(half)
    n_t = half // tr

    def body(c_ref, g_ref, t_ref, o_ref):
        o_ref[...] = (g_ref[...] + t_ref[...]).astype(out_dtype)

    blk = pl.BlockSpec((1, tr, D_MODEL), lambda s, t, c_ref: (s, t, 0))
    return pl.pallas_call(
        body, name=name,
        grid_spec=pltpu.PrefetchScalarGridSpec(
            num_scalar_prefetch=1, grid=(N_CHIPS, n_t),
            in_specs=[pl.BlockSpec((1, tr, D_MODEL), lambda s, t, c_ref: (s, c_ref[0] * n_t + t, 0)), blk],
            out_specs=blk),
        out_shape=jax.ShapeDtypeStruct(theirs.shape, out_dtype),
        compiler_params=_params("parallel", "parallel"),
    )(lax.axis_index("c").astype(jnp.int32).reshape(1), g, theirs)


def _add_chips(chip_sum, others, name):
    half = chip_sum.shape[1]
    tr = _add_tile(half)

    def body(me_ref, own_ref, o0, o1, o2, out_ref):
        out_ref[...] = ((own_ref[0].astype(f32) + o0[0].astype(f32)) + o1[0].astype(f32)) + o2[0].astype(f32)

    other = lambda j: pl.BlockSpec((1, tr, D_MODEL), lambda t, me_ref: (j, t, 0))
    return pl.pallas_call(
        body, name=name,
        grid_spec=pltpu.PrefetchScalarGridSpec(
            num_scalar_prefetch=1, grid=(half // tr,),
            in_specs=[pl.BlockSpec((1, tr, D_MODEL), lambda t, me_ref: (me_ref[0], t, 0)), other(0), other(1), other(2)],
            out_specs=pl.BlockSpec((tr, D_MODEL), lambda t, me_ref: (t, 0))),
        out_shape=jax.ShapeDtypeStruct((half, D_MODEL), f32),
        compiler_params=_params("parallel"),
    )((2 * lax.axis_index("x") + lax.axis_index("y")).astype(jnp.int32).reshape(1), chip_sum, others, others, others)


def _join_halves(r):
    half = r.shape[0]
    pieces = _row_chunks(half, 2 * D2D_PIECES)
    n = len(pieces)

    def body(r_ref, out_ref, send, recv, buf, sems_in, sems_out):
        x, y, c, _ = _place()
        own = _LocalCopy(lambda rr: r_ref.at[rr], lambda rr: out_ref.at[c, rr], half, buf, sems_in, sems_out)
        own.start()

        def piece(i, core):
            start, size = pieces[i]
            return pltpu.make_async_remote_copy(
                src_ref=r_ref.at[pl.ds(start, size)], dst_ref=out_ref.at[core, pl.ds(start, size)],
                send_sem=send.at[i], recv_sem=recv.at[i], device_id=(x, y, 1 - c), device_id_type=MESH)

        copies = [piece(i, c) for i in range(n)]
        for cp in copies:
            cp.start()
        own.pass_on()
        for i in range(n):
            piece(i, 1 - c).wait_recv()
        for cp in copies:
            cp.wait_send()
        own.finish()

    return pl.pallas_call(
        body, name="join_halves", in_specs=[ANY], out_specs=ANY,
        out_shape=jax.ShapeDtypeStruct((2,) + r.shape, r.dtype),
        scratch_shapes=[pltpu.SemaphoreType.DMA((n,))] * 2 + _LocalCopy.scratch(half, r.dtype),
        compiler_params=pltpu.CompilerParams(vmem_limit_bytes=VMEM_LIMIT_V7X),
    )(r)


def _sum_small(block):
    def body(b_ref, out_ref, gathered, send, recv):
        x, y, c, _ = _place()
        me = 4 * x + 2 * y + c
        gathered[me] = b_ref[...]
        sends = []
        for kk in range(1, N_DEV):
            flip = lambda v, bit: 1 - v if bit else v
            peer = (flip(x, kk & 4), flip(y, kk & 2), flip(c, kk & 1))
            cp = pltpu.make_async_remote_copy(
                src_ref=b_ref, dst_ref=gathered.at[me], send_sem=send.at[kk - 1], recv_sem=recv.at[kk - 1],
                device_id=peer, device_id_type=MESH)
            cp.start()
            sends.append(cp)
        for kk in range(1, N_DEV):
            peer_index = jnp.bitwise_xor(me, kk)
            pltpu.make_async_remote_copy(
                src_ref=b_ref, dst_ref=gathered.at[peer_index], send_sem=send.at[kk - 1], recv_sem=recv.at[kk - 1],
                device_id=(x, y, c), device_id_type=MESH).wait_recv()
        for cp in sends:
            cp.wait_send()
        acc = gathered[0]
        for dev in range(1, N_DEV):
            acc = acc + gathered[dev]
        out_ref[...] = acc

    vmem = pl.BlockSpec(memory_space=pltpu.VMEM)
    return pl.pallas_call(
        body, name="sum_small", in_specs=[vmem], out_specs=vmem,
        out_shape=jax.ShapeDtypeStruct(block.shape, block.dtype),
        scratch_shapes=[pltpu.VMEM((N_DEV,) + block.shape, block.dtype),
                        pltpu.SemaphoreType.DMA((N_DEV - 1,)), pltpu.SemaphoreType.DMA((N_DEV - 1,))],
    )(block)


def _adamw(w, g, m, v, name):
    rows, cols = w.shape
    tr = rows
    for cand in (512, 256, 128, 64, 32, 16, 8):
        if rows % cand == 0:
            tr = cand
            break
    c1 = 1.0 - ADAM_B1 ** ADAM_STEP
    c2 = 1.0 - ADAM_B2 ** ADAM_STEP

    def body(w_ref, g_ref, m_ref, v_ref, d_ref, nm_ref, nv_ref):
        gv = g_ref[...]
        nm = ADAM_B1 * m_ref[...] + (1.0 - ADAM_B1) * gv
        nv = ADAM_B2 * v_ref[...] + (1.0 - ADAM_B2) * (gv * gv)
        nm_ref[...] = nm
        nv_ref[...] = nv
        d_ref[...] = -ADAM_LR * ((nm / c1) / (jnp.sqrt(nv / c2) + ADAM_EPS) + ADAM_WD * w_ref[...])

    blk = pl.BlockSpec((tr, cols), lambda i: (i, 0))
    shape = jax.ShapeDtypeStruct((rows, cols), f32)
    return pl.pallas_call(
        body, name=name, grid=(rows // tr,), in_specs=[blk] * 4, out_specs=[blk] * 3, out_shape=[shape] * 3,
        compiler_params=_params("parallel"),
    )(w, g, m, v)


LARGE = ("w_in", "w_out", "w_gate", "w_up", "w_down")
SMALL = ("ln_pre_mix", "ln_post_mix", "ln_pre_ffn", "ln_post_ffn", "pool_scale", "w_pool")
SHARD_ROWS = {"w_in": 640, "w_out": 256, "w_gate": 704, "w_up": 704, "w_down": 704}
COLUMN_SHARDED = ("w_in", "w_gate", "w_up")
NEEDED_FIRST = ("w_in",)
NEEDED_LATER = ("w_out", "w_gate", "w_up", "w_down")
READY_EARLY = ("w_out", "w_gate", "w_up", "w_down")
READY_LATE = ("w_in",)


def _pack_shard(shards, names):
    return jnp.concatenate([shards[n].T if n in COLUMN_SHARDED else shards[n] for n in names], axis=0)


def _unpack_shard(pack, names):
    out, row = {}, 0
    for n in names:
        part = pack[row:row + SHARD_ROWS[n]]
        out[n] = part.T if n in COLUMN_SHARDED else part
        row += SHARD_ROWS[n]
    return out


def _whole_from_shards(packs, names):
    out, row = {}, 0
    for n in names:
        rows = SHARD_ROWS[n]
        out[n] = packs[:, row:row + rows].reshape(N_CHIPS * rows, D_MODEL)
        row += rows
    return out


def _shards_from_whole(grads, names):
    return jnp.concatenate([grads[n].reshape(N_CHIPS, SHARD_ROWS[n], D_MODEL) for n in names], axis=1)


def _pack_small(vals):
    rows = [vals[n].reshape(1, D_MODEL) for n in SMALL[:4]]
    rows.append(jnp.pad(vals["pool_scale"].reshape(1, POOL_WIDTH), ((0, 0), (0, D_MODEL - POOL_WIDTH))))
    rows.append(jnp.pad(vals["loss"].reshape(1, 1), ((0, 0), (0, D_MODEL - 1))))
    rows.append(jnp.zeros((2, D_MODEL), f32))
    rows.append(vals["w_pool"].reshape(16, D_MODEL))
    return jnp.concatenate(rows, axis=0)


def _unpack_small(block):
    out = {n: block[i:i + 1] for i, n in enumerate(SMALL[:4])}
    out["pool_scale"] = block[4:5, :POOL_WIDTH]
    out["loss"] = block[5, 0]
    out["w_pool"] = block[8:24].reshape(1, 4, POOL_GROUP, POOL_GROUP)
    return out


def kernel(x, ln_pre_mix, w_in, w_pool, pool_scale, w_out, ln_post_mix, ln_pre_ffn, w_gate, w_up, w_down, ln_post_ffn, loss_target, m_ln_pre_mix, m_w_in, m_w_pool, m_pool_scale, m_w_out, m_ln_post_mix, m_ln_pre_ffn, m_w_gate, m_w_up, m_w_down, m_ln_post_ffn, v_ln_pre_mix, v_w_in, v_w_pool, v_pool_scale, v_w_out, v_ln_post_mix, v_ln_pre_ffn, v_w_gate, v_w_up, v_w_down, v_ln_post_ffn):
    w = dict(ln_pre_mix=ln_pre_mix, w_in=w_in, w_pool=w_pool, pool_scale=pool_scale, w_out=w_out,
             ln_post_mix=ln_post_mix, ln_pre_ffn=ln_pre_ffn, w_gate=w_gate, w_up=w_up, w_down=w_down,
             ln_post_ffn=ln_post_ffn)
    m = dict(ln_pre_mix=m_ln_pre_mix, w_in=m_w_in, w_pool=m_w_pool, pool_scale=m_pool_scale, w_out=m_w_out,
             ln_post_mix=m_ln_post_mix, ln_pre_ffn=m_ln_pre_ffn, w_gate=m_w_gate, w_up=m_w_up, w_down=m_w_down,
             ln_post_ffn=m_ln_post_ffn)
    v = dict(ln_pre_mix=v_ln_pre_mix, w_in=v_w_in, w_pool=v_w_pool, pool_scale=v_pool_scale, w_out=v_w_out,
             ln_post_mix=v_ln_post_mix, ln_pre_ffn=v_ln_pre_ffn, w_gate=v_w_gate, w_up=v_w_up, w_down=v_w_down,
             ln_post_ffn=v_ln_post_ffn)

    xs, target = x[0], loss_target[0]
    cos_t, sin_t = _rope_tables(xs.shape[0])
    w_bd = _block_diag(w_pool[0]).astype(bf16)
    shard = {n: w[n][0].astype(bf16) for n in LARGE}

    w_in_whole = _whole_from_shards(_gather_weights(_pack_shard(shard, NEEDED_FIRST)), NEEDED_FIRST)["w_in"]
    h1, u, qs, ks, vs = _in_proj(xs, ln_pre_mix, w_in_whole, cos_t, sin_t)
    pool_out = _pool_fwd(u, w_bd, pool_scale)
    attn_out, lse, later = _attn_fwd(qs, ks, vs, _pack_shard(shard, NEEDED_LATER))
    whole = _whole_from_shards(later, NEEDED_LATER)
    mix, x2, h2 = _out_proj(pool_out, attn_out, whole["w_out"], xs, ln_post_mix, ln_pre_ffn)
    gate, up, dy, df, dg4, loss = _ffn_fwd(h2, whole["w_gate"], whole["w_up"], whole["w_down"], x2, target, ln_post_ffn)

    large = {}
    a, dgate, dup, dh2 = _ffn_bwd(df, gate, up, whole["w_gate"], whole["w_up"], whole["w_down"])
    large["w_down"] = _matmul_tiles_tn(a, df, "grad_w_down")
    large["w_gate"] = _matmul_tiles_tn(dgate, h2, "grad_w_gate")
    large["w_up"] = _matmul_tiles_tn(dup, h2, "grad_w_up")
    dx2, dmix, dg3, dg2 = _norm_bwd(dh2, dy, x2, mix, ln_pre_ffn, ln_post_mix)
    large["w_out"] = jnp.concatenate([_matmul_tn(pool_out, dmix, D_MODEL, "grad_w_out_pool"),
                                      _matmul_tn(attn_out, dmix, D_MODEL, "grad_w_out_attn")], axis=0)
    early = _shards_from_whole(large, READY_EARLY)
    dpool, delta, dos, early_theirs = _out_proj_bwd(dmix, whole["w_out"], attn_out, _head_ones(), early)
    early_chip = _add_cores(early, early_theirs, "add_cores_early")
    du, d_w_bd, d_scale = _pool_bwd(u, dpool, w_bd, pool_scale)
    dq, dk, dv, early_others = _attn_bwd(qs, ks, vs, dos, lse, delta, early_chip)
    grad_x, dproj, dg1 = _in_proj_bwd(du, dq, dk, dv, cos_t, sin_t, w_in_whole, xs, dx2, ln_pre_mix)
    large["w_in"] = _matmul_tn(dproj, h1, D_MODEL, "grad_w_in")
    late = _shards_from_whole(large, READY_LATE)
    late_chip = _add_cores(late, _swap_halves(late), "add_cores_late", bf16)
    late_others = _scatter_to_chips(late_chip)
    early_half = _add_chips(early_chip, early_others, "add_chips_early")
    late_half = _add_chips(late_chip, late_others, "add_chips_late")
    joined = _join_halves(jnp.concatenate([early_half, late_half], axis=0))
    n_early = early_half.shape[0]
    grads = _unpack_shard(joined[:, :n_early].reshape(-1, D_MODEL), READY_EARLY)
    grads.update(_unpack_shard(joined[:, n_early:].reshape(-1, D_MODEL), READY_LATE))

    d_w_pool = jnp.stack([d_w_bd[g * POOL_GROUP:(g + 1) * POOL_GROUP, g * POOL_GROUP:(g + 1) * POOL_GROUP]
                          for g in range(POOL_WIDTH // POOL_GROUP)])
    small = dict(ln_pre_mix=dg1, ln_post_mix=dg2, ln_pre_ffn=dg3, ln_post_ffn=dg4, pool_scale=d_scale, w_pool=d_w_pool)
    total = _unpack_small(_sum_small(_pack_small(dict(small, loss=loss))))
    for n in SMALL:
        grads[n] = total[n]

    delta_w, new_m, new_v = {}, {}, {}
    for n in LARGE:
        delta_w[n], new_m[n], new_v[n] = _adamw(w[n][0], grads[n], m[n][0], v[n][0], "adamw_" + n)
    small_state = [_pack_small(dict({n: s[n] for n in SMALL}, loss=jnp.zeros((), f32))) for s in (w, m, v)]
    small_grad = _pack_small(dict({n: grads[n] for n in SMALL}, loss=jnp.zeros((), f32)))
    sd, sm, sv = _adamw(small_state[0], small_grad, small_state[1], small_state[2], "adamw_small")
    for out, block in ((delta_w, sd), (new_m, sm), (new_v, sv)):
        un = _unpack_small(block)
        for n in SMALL:
            out[n] = un[n]

    names = ("ln_pre_mix", "w_in", "w_pool", "pool_scale", "w_out", "ln_post_mix", "ln_pre_ffn", "w_gate", "w_up",
             "w_down", "ln_post_ffn")
    full = lambda d: [d[n].reshape(w[n].shape) for n in names]
    return (total["loss"], grad_x[None], *full(grads), *full(delta_w), *full(new_m), *full(new_v))
```

```python
import numpy as np
import jax
import jax.numpy as jnp
from jax import lax
from jax.experimental import pallas as pl
from jax.experimental.pallas import tpu as pltpu

D_MODEL = 1024
POOL_WIDTH = 256
POOL_GROUP = 64
ATTN_WIDTH = 768
HEAD_DIM = 64
IN_WIDTH = 2560
D_FF = 2816
BLOCK = 128
DILATIONS = (1, 4, 16)
ROPE_THETA = 10000.0
EPS = 1e-6
ATTN_SCALE = 0.125
NEG = -1e30

ADAM_LR = 0.001
ADAM_B1 = 0.9
ADAM_B2 = 0.999
ADAM_EPS = 1e-08
ADAM_WD = 0.01
ADAM_STEP = 10

N_CHIPS = 4
N_DEV = 8
VMEM_LIMIT_V7X = 56 * 1024 * 1024
MESH = pl.DeviceIdType.MESH

f32 = jnp.float32
bf16 = jnp.bfloat16


def _params(*sem):
    return pltpu.CompilerParams(dimension_semantics=sem, vmem_limit_bytes=VMEM_LIMIT_V7X)


def _dot(a, b):
    return jnp.dot(a, b, preferred_element_type=f32)


def _dot_nt(a, b):
    return lax.dot_general(a, b, (((1,), (1,)), ((), ())), preferred_element_type=f32)


def _dot_tn(a, b):
    return lax.dot_general(a, b, (((0,), (0,)), ((), ())), preferred_element_type=f32)


def _rope_partner(a, first_half):
    return jnp.where(first_half, pltpu.roll(a, 96, 1), pltpu.roll(a, 32, 1))


def _first_half_mask(rows):
    lane = lax.broadcasted_iota(jnp.int32, (rows, 128), 1)
    return (lane % HEAD_DIM) < (HEAD_DIM // 2)


def _stream_spec(d, ts):
    return pl.BlockSpec((d, ts // d, ATTN_WIDTH), lambda i: (0, i, 0))


def _stream_shape(S, d):
    return jax.ShapeDtypeStruct((d, S // d, ATTN_WIDTH), bf16)


N_STAGE = ATTN_WIDTH // 128


def _stage_scratch(ts):
    return [pltpu.VMEM((ts, 128), f32)] * N_STAGE


def _store_streams(stage, out_refs, ts):
    for d, ref in zip(DILATIONS, out_refs):
        for r in range(d):
            rows = pl.ds(0, ts) if d == 1 else pl.ds(r, ts // d, stride=d)
            for j in range(N_STAGE):
                ref[r, :, j * 128:(j + 1) * 128] = stage[j][rows, :].astype(bf16)


def _in_proj(x, g1, w_in, cos_t, sin_t):
    S = x.shape[0]
    ts = 512

    def body(x_ref, g_ref, w_ref, cos_ref, sin_ref, h_ref, u_ref, *rest):
        outs, stage = rest[:-N_STAGE], rest[-N_STAGE:]
        xv = x_ref[...]
        r = lax.rsqrt(jnp.mean(xv * xv, axis=-1, keepdims=True) + EPS)
        h = ((xv * r) * g_ref[...]).astype(bf16)
        h_ref[...] = h
        proj = _dot_nt(h, w_ref[...])
        u_ref[...] = proj[:, :POOL_WIDTH]
        cos = cos_ref[...]
        sin = sin_ref[...]
        first = _first_half_mask(ts)
        n_dil = len(DILATIONS)
        for which, base in enumerate((POOL_WIDTH, POOL_WIDTH + ATTN_WIDTH)):
            for j in range(ATTN_WIDTH // 128):
                a = proj[:, base + j * 128: base + (j + 1) * 128]
                if which == 0:
                    a = a * ATTN_SCALE
                stage[j][...] = a * cos + _rope_partner(a, first) * sin
            _store_streams(stage, outs[which * n_dil:(which + 1) * n_dil], ts)
        for j in range(ATTN_WIDTH // 128):
            base = POOL_WIDTH + 2 * ATTN_WIDTH + j * 128
            stage[j][...] = proj[:, base:base + 128]
        _store_streams(stage, outs[2 * n_dil:], ts)

    row = lambda w: pl.BlockSpec((ts, w), lambda i: (i, 0))
    streams = [_stream_spec(d, ts) for d in DILATIONS] * 3
    res = pl.pallas_call(
        body, name="in_proj", grid=(S // ts,),
        in_specs=[row(D_MODEL), pl.BlockSpec((1, D_MODEL), lambda i: (0, 0)),
                  pl.BlockSpec((IN_WIDTH, D_MODEL), lambda i: (0, 0)), row(128), row(128)],
        out_specs=[row(D_MODEL), row(POOL_WIDTH)] + streams,
        out_shape=[jax.ShapeDtypeStruct((S, D_MODEL), bf16), jax.ShapeDtypeStruct((S, POOL_WIDTH), f32)]
        + [_stream_shape(S, d) for d in DILATIONS] * 3,
        scratch_shapes=_stage_scratch(ts),
        compiler_params=_params("parallel"),
    )(x, g1, w_in, cos_t, sin_t)
    n = len(DILATIONS)
    return res[0], res[1], res[2:2 + n], res[2 + n:2 + 2 * n], res[2 + 2 * n:]


POOL_HALO = 16


def _pool_lane_group(rows):
    return lax.broadcasted_iota(jnp.int32, (rows, POOL_WIDTH), 1) // POOL_GROUP


def _pool_select(group, s2, s4, s8, s16):
    return jnp.where(group == 0, s2, jnp.where(group == 1, s4, jnp.where(group == 2, s8, s16)))


def _pool_count(t0, rows):
    group = _pool_lane_group(rows)
    t = t0 + lax.broadcasted_iota(jnp.int32, (rows, POOL_WIDTH), 0)
    win = _pool_select(group, 2, 4, 8, 16)
    return jnp.minimum(t + 1, win).astype(f32)


def _pool_diff(u_halo, u_tile, t0):
    ts = u_tile.shape[0]
    ext = jnp.concatenate([u_halo, u_tile], axis=0)
    s2 = ext + pltpu.roll(ext, 1, 0)
    s4 = s2 + pltpu.roll(s2, 2, 0)
    s8 = s4 + pltpu.roll(s4, 4, 0)
    s16 = s8 + pltpu.roll(s8, 8, 0)
    group = _pool_lane_group(ts + POOL_HALO)
    wsum = _pool_select(group, s2, s4, s8, s16)[POOL_HALO:]
    return wsum / _pool_count(t0, ts) - u_tile


def _pool_specs(ts, n_tiles):
    tile = pl.BlockSpec((ts, POOL_WIDTH), lambda i: (i, 0))
    per = ts // POOL_HALO
    before = pl.BlockSpec((POOL_HALO, POOL_WIDTH), lambda i: (jnp.maximum(i * per - 1, 0), 0))
    after = pl.BlockSpec((POOL_HALO, POOL_WIDTH), lambda i: (jnp.minimum((i + 1) * per, n_tiles * per - 1), 0))
    return tile, before, after


def _pool_fwd(u, w_bd, scale):
    S = u.shape[0]
    ts = 512
    n_tiles = S // ts

    def body(u_ref, halo_ref, w_ref, sc_ref, y_ref):
        i = pl.program_id(0)
        halo = jnp.where(i > 0, halo_ref[...], 0.0)
        d = _pool_diff(halo, u_ref[...], i * ts)
        y_ref[...] = (_dot(d.astype(bf16), w_ref[...]) * sc_ref[...]).astype(bf16)

    tile, before, _ = _pool_specs(ts, n_tiles)
    return pl.pallas_call(
        body, name="pool_fwd", grid=(n_tiles,),
        in_specs=[tile, before, pl.BlockSpec((POOL_WIDTH, POOL_WIDTH), lambda i: (0, 0)),
                  pl.BlockSpec((1, POOL_WIDTH), lambda i: (0, 0))],
        out_specs=tile, out_shape=jax.ShapeDtypeStruct((S, POOL_WIDTH), bf16),
        compiler_params=_params("parallel"),
    )(u, u, w_bd, scale)


def _pool_bwd(u, dy, w_bd, scale):
    S = u.shape[0]
    ts = 512
    n_tiles = S // ts

    def body(u_ref, halo_ref, dy_ref, dy_next_ref, w_ref, sc_ref, du_ref, dw_ref, dsc_ref):
        i = pl.program_id(0)

        @pl.when(i == 0)
        def _():
            dw_ref[...] = jnp.zeros_like(dw_ref)
            dsc_ref[...] = jnp.zeros_like(dsc_ref)

        halo = jnp.where(i > 0, halo_ref[...], 0.0)
        d = _pool_diff(halo, u_ref[...], i * ts).astype(bf16)
        w = w_ref[...]
        sc = sc_ref[...]
        dy_tile = dy_ref[...]
        z = _dot(d, w)
        dsc_ref[...] += jnp.sum(dy_tile * z, axis=0, keepdims=True)
        dy_next = jnp.where(i < n_tiles - 1, dy_next_ref[...], 0.0)
        dz = (jnp.concatenate([dy_tile, dy_next], axis=0) * sc).astype(bf16)
        dw_ref[...] += _dot_tn(d, dz[:ts])
        dd = _dot_nt(dz, w)
        e = dd / _pool_count(i * ts, ts + POOL_HALO)
        n = ts + POOL_HALO
        f2 = e + pltpu.roll(e, n - 1, 0)
        f4 = f2 + pltpu.roll(f2, n - 2, 0)
        f8 = f4 + pltpu.roll(f4, n - 4, 0)
        f16 = f8 + pltpu.roll(f8, n - 8, 0)
        fsum = _pool_select(_pool_lane_group(n), f2, f4, f8, f16)
        du_ref[...] = (fsum[:ts] - dd[:ts]).astype(bf16)

    tile, before, after = _pool_specs(ts, n_tiles)
    return pl.pallas_call(
        body, name="pool_bwd", grid=(n_tiles,),
        in_specs=[tile, before, tile, after, pl.BlockSpec((POOL_WIDTH, POOL_WIDTH), lambda i: (0, 0)),
                  pl.BlockSpec((1, POOL_WIDTH), lambda i: (0, 0))],
        out_specs=[tile, pl.BlockSpec((POOL_WIDTH, POOL_WIDTH), lambda i: (0, 0)),
                   pl.BlockSpec((1, POOL_WIDTH), lambda i: (0, 0))],
        out_shape=[jax.ShapeDtypeStruct((S, POOL_WIDTH), bf16), jax.ShapeDtypeStruct((POOL_WIDTH, POOL_WIDTH), f32),
                   jax.ShapeDtypeStruct((1, POOL_WIDTH), f32)],
        compiler_params=_params("arbitrary"),
    )(u, u, dy, dy, w_bd, scale)


SUPER = BLOCK * DILATIONS[-1]
UNITS = SUPER // BLOCK
FWD_UNROLL = 16
BWD_UNROLL = 8


def _band_mask(has_prev):
    qi = lax.broadcasted_iota(jnp.int32, (BLOCK, 2 * BLOCK), 0)
    kj = lax.broadcasted_iota(jnp.int32, (BLOCK, 2 * BLOCK), 1)
    return (kj >= qi) & (kj <= qi + BLOCK) & ((kj >= BLOCK) | has_prev)


def _head0_mask(rows=BLOCK):
    return lax.broadcasted_iota(jnp.int32, (rows, 128), 1) < HEAD_DIM


def _band_mask_t(has_prev):
    ki = lax.broadcasted_iota(jnp.int32, (2 * BLOCK, 2 * BLOCK), 0)
    qj = lax.broadcasted_iota(jnp.int32, (2 * BLOCK, 2 * BLOCK), 1) % BLOCK
    return (ki >= qj) & (ki <= qj + BLOCK) & ((ki >= BLOCK) | has_prev)


def _head_pair_rows(a, h0):
    zero = jnp.zeros_like(a)
    return jnp.concatenate([jnp.where(h0, a, zero), jnp.where(h0, zero, a)], axis=0)


def _per_query_row(stat):
    t = stat.T
    return jnp.concatenate([jnp.concatenate([t[:HEAD_DIM]] * 4, axis=0), jnp.concatenate([t[HEAD_DIM:]] * 4, axis=0)],
                           axis=1)


def _natural_rows(d, r, n):
    if d == 1:
        return pl.ds(pl.multiple_of(n * BLOCK, BLOCK), BLOCK)
    return pl.ds(n * (BLOCK * d) + r, BLOCK, stride=d)


def _unit_place(d, u):
    per_stream = UNITS // d
    return u // per_stream, u % per_stream, per_stream


def _block_rows(n):
    return pl.ds(pl.multiple_of(n * BLOCK, BLOCK), BLOCK)


def _band(cur_ref, tail_ref, r, n):
    before = jnp.where(n > 0, cur_ref[r, _block_rows(jnp.maximum(n - 1, 0)), :], tail_ref[r])
    return jnp.concatenate([before, cur_ref[r, _block_rows(n), :]], axis=0)


def _attn_in_specs(S, with_do):
    specs = []
    last = S // SUPER - 1
    for d in DILATIONS:
        per_stream = UNITS // d
        cur = pl.BlockSpec((d, SUPER // d, 128), lambda hp, sb: (0, jnp.minimum(sb, last), hp))
        tail = pl.BlockSpec(
            (d, BLOCK, 128),
            lambda hp, sb, per_stream=per_stream: (0, jnp.maximum(jnp.minimum(sb, last) * per_stream - 1, 0), hp))
        specs += [cur] * (2 if with_do else 1) + [cur, tail, cur, tail]
    return specs


def _attn_fwd(qs, ks, vs, pack):
    S = qs[0].shape[1]
    n_dil = len(DILATIONS)
    n_steps = S // SUPER
    n_total = (ATTN_WIDTH // 128) * n_steps

    def body(*refs):
        ins, pack_ref = refs[:5 * n_dil], refs[5 * n_dil]
        out_ref, lse_ref, gathered_ref = refs[5 * n_dil + 1:5 * n_dil + 4]
        scratch = refs[5 * n_dil + 4:]
        o_sc, l_sc = scratch[:n_dil], scratch[n_dil:2 * n_dil]
        gather = _Gather(pack_ref, gathered_ref, *scratch[2 * n_dil:])
        sb = pl.program_id(1)
        step = pl.program_id(0) * n_steps + sb

        @pl.when(step == 0)
        def _():
            gather.start()

        h0 = _head0_mask()
        for ci, d in enumerate(DILATIONS):
            q_ref, kc_ref, kp_ref, vc_ref, vp_ref = ins[5 * ci:5 * ci + 5]

            def unit(u, carry, d=d, ci=ci, q_ref=q_ref, kc_ref=kc_ref, kp_ref=kp_ref, vc_ref=vc_ref, vp_ref=vp_ref):
                r, n, _ = _unit_place(d, u)
                qv = q_ref[r, _block_rows(n), :]
                kb = _band(kc_ref, kp_ref, r, n)
                vb = _band(vc_ref, vp_ref, r, n)
                valid = _band_mask((sb > 0) | (n > 0))
                outs, lses = [], []
                for h in range(2):
                    keep = h0 if h == 0 else jnp.logical_not(h0)
                    qh = jnp.where(keep, qv, jnp.zeros_like(qv))
                    s = jnp.where(valid, _dot_nt(qh, kb), NEG)
                    m = jnp.max(s, axis=1, keepdims=True)
                    e = jnp.exp(s - m)
                    den = jnp.sum(e, axis=1, keepdims=True)
                    outs.append(_dot(e.astype(bf16), vb) * (1.0 / den))
                    lses.append(jnp.broadcast_to(m + jnp.log(den), (BLOCK, 128)))
                rows = _natural_rows(d, r, n)
                o_sc[ci][rows, :] = jnp.where(h0, outs[0], outs[1])
                l_sc[ci][rows, :] = jnp.where(h0, lses[0], lses[1])
                return carry

            lax.fori_loop(0, UNITS, unit, 0, unroll=FWD_UNROLL)

        def merge(t, carry):
            rows = pl.ds(pl.multiple_of(t * 256, 256), 256)
            a, b, c = l_sc[0][rows, :], l_sc[1][rows, :], l_sc[2][rows, :]
            m = jnp.maximum(jnp.maximum(a, b), c)
            ea, eb, ec = jnp.exp(a - m), jnp.exp(b - m), jnp.exp(c - m)
            tot = ea + eb + ec
            out_ref[rows, :] = ((ea / tot) * o_sc[0][rows, :] + (eb / tot) * o_sc[1][rows, :]
                                + (ec / tot) * o_sc[2][rows, :]).astype(bf16)
            lse_ref[rows, :] = m + jnp.log(tot)
            return carry

        lax.fori_loop(0, SUPER // 256, merge, 0)

        @pl.when(step == (2 * n_total) // 3)
        def _():
            gather.pass_on()

        @pl.when(step == n_total - 1)
        def _():
            gather.finish()

    args = []
    for q, k, v in zip(qs, ks, vs):
        args += [q, k, k, v, v]
    nat = pl.BlockSpec((SUPER, 128), lambda hp, sb: (sb, hp))
    rows = pack.shape[0]
    return pl.pallas_call(
        body, name="attn_fwd", grid=(ATTN_WIDTH // 128, n_steps),
        in_specs=_attn_in_specs(S, False) + [ANY], out_specs=[nat, nat, ANY],
        out_shape=[jax.ShapeDtypeStruct((S, ATTN_WIDTH), bf16), jax.ShapeDtypeStruct((S, ATTN_WIDTH), f32),
                   _Gather.out_shape(rows, pack.dtype)],
        scratch_shapes=[pltpu.VMEM((SUPER, 128), f32)] * (2 * n_dil) + _Gather.scratch(rows, pack.dtype),
        compiler_params=_params("arbitrary", "arbitrary"),
    )(*args, pack)


def _attn_bwd(qs, ks, vs, dos, lse, delta, chip_sum):
    S = qs[0].shape[1]
    n_steps = S // SUPER
    last = n_steps - 1
    n_dil = len(DILATIONS)
    n_total = (ATTN_WIDTH // 128) * (n_steps + 1)

    def body(*refs):
        ins, (lse_ref, dl_ref, sum_ref) = refs[:6 * n_dil], refs[6 * n_dil:6 * n_dil + 3]
        dq_ref, dk_ref, dv_ref, others_ref = refs[6 * n_dil + 3:6 * n_dil + 7]
        dq_acc, dk_acc, dv_acc = refs[6 * n_dil + 7:6 * n_dil + 10]
        scatter = _Scatter(sum_ref, others_ref, *refs[6 * n_dil + 10:])
        sb = pl.program_id(1)
        step = pl.program_id(0) * (n_steps + 1) + sb
        cur = sb % 2
        prv = 1 - cur

        @pl.when(step == 0)
        def _():
            scatter.start()

        @pl.when(sb < n_steps)
        def _():
            dq_acc[...] = jnp.zeros_like(dq_acc)
            dk_acc[cur] = jnp.zeros((SUPER, 128), f32)
            dv_acc[cur] = jnp.zeros((SUPER, 128), f32)
            h0 = _head0_mask()
            for ci, d in enumerate(DILATIONS):
                q_ref, do_ref, kc_ref, kp_ref, vc_ref, vp_ref = ins[6 * ci:6 * ci + 6]

                def unit(u, carry, d=d, q_ref=q_ref, do_ref=do_ref, kc_ref=kc_ref, kp_ref=kp_ref, vc_ref=vc_ref,
                         vp_ref=vp_ref):
                    r, n, per_stream = _unit_place(d, u)
                    qv = q_ref[r, _block_rows(n), :]
                    dov = do_ref[r, _block_rows(n), :]
                    kb = _band(kc_ref, kp_ref, r, n)
                    vb = _band(vc_ref, vp_ref, r, n)
                    rows = _natural_rows(d, r, n)
                    has_prev = (sb > 0) | (n > 0)
                    q_pair = _head_pair_rows(qv, h0)
                    do_pair = _head_pair_rows(dov, h0)
                    s_t = jnp.where(_band_mask_t(has_prev), _dot_nt(kb, q_pair), NEG)
                    p_t = jnp.exp(s_t - _per_query_row(lse_ref[rows, :]))
                    dp_t = _dot_nt(vb, do_pair)
                    ds_t = (p_t * (dp_t - _per_query_row(dl_ref[rows, :]))).astype(bf16)
                    dvb = _dot(p_t.astype(bf16), do_pair)
                    dkb = _dot(ds_t, q_pair)
                    dq_pair = _dot_tn(ds_t, kb)
                    dq_acc[rows, :] += jnp.where(h0, dq_pair[:BLOCK], dq_pair[BLOCK:])
                    dk_acc[cur, rows, :] += dkb[BLOCK:]
                    dv_acc[cur, rows, :] += dvb[BLOCK:]

                    slot = jnp.where((n > 0) | (sb == 0), cur, prv)
                    before = _natural_rows(d, r, jnp.where(n > 0, n - 1, per_stream - 1))
                    dk_acc[slot, before, :] += dkb[:BLOCK]
                    dv_acc[slot, before, :] += dvb[:BLOCK]
                    return carry

                lax.fori_loop(0, UNITS, unit, 0, unroll=BWD_UNROLL)
            dq_ref[...] = (dq_acc[...] * ATTN_SCALE).astype(bf16)

        @pl.when(sb > 0)
        def _():
            dk_ref[...] = dk_acc[prv].astype(bf16)
            dv_ref[...] = dv_acc[prv].astype(bf16)

        @pl.when(step == n_total - 1)
        def _():
            scatter.finish()

    args = []
    for q, k, v, do in zip(qs, ks, vs, dos):
        args += [q, do, k, k, v, v]
    nat = pl.BlockSpec((SUPER, 128), lambda hp, sb: (jnp.minimum(sb, last), hp))
    nat_before = pl.BlockSpec((SUPER, 128), lambda hp, sb: (jnp.clip(sb - 1, 0, last), hp))
    out = jax.ShapeDtypeStruct((S, ATTN_WIDTH), bf16)
    half = chip_sum.shape[1]
    return pl.pallas_call(
        body, name="attn_bwd", grid=(ATTN_WIDTH // 128, n_steps + 1),
        in_specs=_attn_in_specs(S, True) + [nat, nat, ANY], out_specs=[nat, nat_before, nat_before, ANY],
        out_shape=[out, out, out, _Scatter.out_shape(half, chip_sum.dtype)],
        scratch_shapes=[pltpu.VMEM((SUPER, 128), f32), pltpu.VMEM((2, SUPER, 128), f32),
                        pltpu.VMEM((2, SUPER, 128), f32)] + _Scatter.scratch(half),
        compiler_params=_params("arbitrary", "arbitrary"),
    )(*args, lse, delta, chip_sum)


def _rms(v):
    return lax.rsqrt(jnp.mean(v * v, axis=-1, keepdims=True) + EPS)


def _out_proj(pool_out, attn_out, w_out, x, g2, g3):
    S = x.shape[0]
    ts = 512

    def body(p_ref, a_ref, w_ref, x_ref, g2_ref, g3_ref, mix_ref, x2_ref, h2_ref):
        mix = _dot(p_ref[...], w_ref[:POOL_WIDTH, :]) + _dot(a_ref[...], w_ref[POOL_WIDTH:, :])
        mix_ref[...] = mix
        x2 = x_ref[...] + (mix * _rms(mix)) * g2_ref[...]
        x2_ref[...] = x2
        h2_ref[...] = ((x2 * _rms(x2)) * g3_ref[...]).astype(bf16)

    row = lambda w: pl.BlockSpec((ts, w), lambda i: (i, 0))
    gain = pl.BlockSpec((1, D_MODEL), lambda i: (0, 0))
    return pl.pallas_call(
        body, name="out_proj", grid=(S // ts,),
        in_specs=[row(POOL_WIDTH), row(ATTN_WIDTH), pl.BlockSpec((D_MODEL, D_MODEL), lambda i: (0, 0)),
                  row(D_MODEL), gain, gain],
        out_specs=[row(D_MODEL)] * 3,
        out_shape=[jax.ShapeDtypeStruct((S, D_MODEL), f32), jax.ShapeDtypeStruct((S, D_MODEL), f32),
                   jax.ShapeDtypeStruct((S, D_MODEL), bf16)],
        compiler_params=_params("parallel"),
    )(pool_out, attn_out, w_out, x, g2, g3)


FF_TILE = 256
FF_STEP_ROWS = 2048
FF_ROWS = 256


def _sigmoid(g):
    return 1.0 / (1.0 + jnp.exp(-g))


def _ff_act_shape(S):
    return jax.ShapeDtypeStruct((D_FF // FF_TILE, S, FF_TILE), bf16)


def _ff_act_spec(ts):
    return pl.BlockSpec((1, ts, FF_TILE), lambda i, j: (j, i, 0))


def _ffn_fwd(h2, w_gate, w_up, w_down):
    S = h2.shape[0]
    ts = min(S, FF_STEP_ROWS)

    def body(h_ref, wg_ref, wu_ref, wd_ref, gate_ref, up_ref, f_ref):
        def rows_pass(first):
            def sub(i, carry):
                rows = pl.ds(pl.multiple_of(i * FF_ROWS, FF_ROWS), FF_ROWS)
                h = h_ref[rows, :]
                gate = _dot_nt(h, wg_ref[...])
                up = _dot_nt(h, wu_ref[...])
                gate_ref[0, rows, :] = gate.astype(bf16)
                up_ref[0, rows, :] = up.astype(bf16)
                part = _dot((gate * _sigmoid(gate) * up).astype(bf16), wd_ref[...])
                if first:
                    f_ref[rows, :] = part
                else:
                    f_ref[rows, :] += part
                return carry

            lax.fori_loop(0, ts // FF_ROWS, sub, 0, unroll=True)

        @pl.when(pl.program_id(1) == 0)
        def _():
            rows_pass(True)

        @pl.when(pl.program_id(1) > 0)
        def _():
            rows_pass(False)

    act = _ff_act_spec(ts)
    weight = pl.BlockSpec((FF_TILE, D_MODEL), lambda i, j: (j, 0))
    return pl.pallas_call(
        body, name="ffn_fwd", grid=(S // ts, D_FF // FF_TILE),
        in_specs=[pl.BlockSpec((ts, D_MODEL), lambda i, j: (i, 0)), weight, weight, weight],
        out_specs=[act, act, pl.BlockSpec((ts, D_MODEL), lambda i, j: (i, 0))],
        out_shape=[_ff_act_shape(S), _ff_act_shape(S), jax.ShapeDtypeStruct((S, D_MODEL), f32)],
        compiler_params=_params("parallel", "arbitrary"),
    )(h2, w_gate, w_up, w_down)


def _loss_head(f, x2, target, g4):
    S = f.shape[0]
    ts = 512

    def body(f_ref, x2_ref, t_ref, g_ref, dy_ref, df_ref, dg_ref, loss_ref):
        @pl.when(pl.program_id(0) == 0)
        def _():
            dg_ref[...] = jnp.zeros_like(dg_ref)
            loss_ref[...] = jnp.zeros_like(loss_ref)

        fv = f_ref[...]
        g = g_ref[...]
        r = _rms(fv)
        fhat = fv * r
        err = (x2_ref[...] + fhat * g) - t_ref[...]
        loss_ref[...] += 0.5 * jnp.sum(jnp.mean(err * err, axis=-1, keepdims=True), axis=0, keepdims=True)
        dy = err * (1.0 / D_MODEL)
        dy_ref[...] = dy
        dg_ref[...] += jnp.sum(dy * fhat, axis=0, keepdims=True)
        dyg = dy * g
        df_ref[...] = (r * (dyg - fhat * jnp.mean(dyg * fhat, axis=-1, keepdims=True))).astype(bf16)

    row = pl.BlockSpec((ts, D_MODEL), lambda i: (i, 0))
    gain = pl.BlockSpec((1, D_MODEL), lambda i: (0, 0))
    return pl.pallas_call(
        body, name="loss_head", grid=(S // ts,), in_specs=[row, row, row, gain],
        out_specs=[row, row, gain, pl.BlockSpec((1, 1), lambda i: (0, 0))],
        out_shape=[jax.ShapeDtypeStruct((S, D_MODEL), f32), jax.ShapeDtypeStruct((S, D_MODEL), bf16),
                   jax.ShapeDtypeStruct((1, D_MODEL), f32), jax.ShapeDtypeStruct((1, 1), f32)],
        compiler_params=_params("arbitrary"),
    )(f, x2, target, g4)


def _ffn_bwd(df, gate, up, w_gate, w_up, w_down):
    S = df.shape[0]
    ts = min(S, FF_STEP_ROWS)

    def body(df_ref, gate_ref, up_ref, wg_ref, wu_ref, wd_ref, a_ref, dgate_ref, dup_ref, dh_ref):
        def rows_pass(first):
            def sub(i, carry):
                rows = pl.ds(pl.multiple_of(i * FF_ROWS, FF_ROWS), FF_ROWS)
                da = _dot_nt(df_ref[rows, :], wd_ref[...])
                g = gate_ref[0, rows, :].astype(f32)
                u = up_ref[0, rows, :].astype(f32)
                sig = _sigmoid(g)
                silu = g * sig
                a_ref[0, rows, :] = (silu * u).astype(bf16)
                dup = (da * silu).astype(bf16)
                dgate = (da * u * (sig * (1.0 + g * (1.0 - sig)))).astype(bf16)
                dup_ref[0, rows, :] = dup
                dgate_ref[0, rows, :] = dgate
                part = _dot(dgate, wg_ref[...]) + _dot(dup, wu_ref[...])
                if first:
                    dh_ref[rows, :] = part
                else:
                    dh_ref[rows, :] += part
                return carry

            lax.fori_loop(0, ts // FF_ROWS, sub, 0, unroll=True)

        @pl.when(pl.program_id(1) == 0)
        def _():
            rows_pass(True)

        @pl.when(pl.program_id(1) > 0)
        def _():
            rows_pass(False)

    act = _ff_act_spec(ts)
    row = pl.BlockSpec((ts, D_MODEL), lambda i, j: (i, 0))
    return pl.pallas_call(
        body, name="ffn_bwd", grid=(S // ts, D_FF // FF_TILE),
        in_specs=[row, act, act,
                  pl.BlockSpec((FF_TILE, D_MODEL), lambda i, j: (j, 0)),
                  pl.BlockSpec((FF_TILE, D_MODEL), lambda i, j: (j, 0)),
                  pl.BlockSpec((FF_TILE, D_MODEL), lambda i, j: (j, 0))],
        out_specs=[act, act, act, row],
        out_shape=[_ff_act_shape(S)] * 3 + [jax.ShapeDtypeStruct((S, D_MODEL), f32)],
        compiler_params=_params("parallel", "arbitrary"),
    )(df, gate, up, w_gate, w_up, w_down)


def _norm_bwd(dh2, dy, x2, mix, g3, g2):
    S = dh2.shape[0]
    ts = 512

    def body(dh_ref, dy_ref, x2_ref, mix_ref, g3_ref, g2_ref, dx2_ref, dmix_ref, dg3_ref, dg2_ref):
        @pl.when(pl.program_id(0) == 0)
        def _():
            dg3_ref[...] = jnp.zeros_like(dg3_ref)
            dg2_ref[...] = jnp.zeros_like(dg2_ref)

        dh = dh_ref[...]
        x2 = x2_ref[...]
        r3 = _rms(x2)
        xhat = x2 * r3
        dg3_ref[...] += jnp.sum(dh * xhat, axis=0, keepdims=True)
        dhg = dh * g3_ref[...]
        dx2 = dy_ref[...] + r3 * (dhg - xhat * jnp.mean(dhg * xhat, axis=-1, keepdims=True))
        dx2_ref[...] = dx2
        mix = mix_ref[...]
        r2 = _rms(mix)
        mhat = mix * r2
        dg2_ref[...] += jnp.sum(dx2 * mhat, axis=0, keepdims=True)
        dmg = dx2 * g2_ref[...]
        dmix_ref[...] = (r2 * (dmg - mhat * jnp.mean(dmg * mhat, axis=-1, keepdims=True))).astype(bf16)

    row = pl.BlockSpec((ts, D_MODEL), lambda i: (i, 0))
    gain = pl.BlockSpec((1, D_MODEL), lambda i: (0, 0))
    return pl.pallas_call(
        body, name="norm_bwd", grid=(S // ts,), in_specs=[row, row, row, row, gain, gain],
        out_specs=[row, row, gain, gain],
        out_shape=[jax.ShapeDtypeStruct((S, D_MODEL), f32), jax.ShapeDtypeStruct((S, D_MODEL), bf16),
                   jax.ShapeDtypeStruct((1, D_MODEL), f32), jax.ShapeDtypeStruct((1, D_MODEL), f32)],
        compiler_params=_params("arbitrary"),
    )(dh2, dy, x2, mix, g3, g2)


def _out_proj_bwd(dmix, w_out, attn_out, head_ones, grads):
    S = dmix.shape[0]
    ts = 512
    n_dil = len(DILATIONS)

    def body(dm_ref, w_ref, o_ref, ones_ref, g_ref, dp_ref, dl_ref, *rest):
        do_refs, theirs_ref = rest[:n_dil], rest[n_dil]
        stage = rest[n_dil + 1:n_dil + 1 + N_STAGE]
        swap = _Swap(g_ref, theirs_ref, *rest[n_dil + 1 + N_STAGE:])

        @pl.when(pl.program_id(0) == 0)
        def _():
            swap.start()

        @pl.when(pl.program_id(0) == S // ts - 1)
        def _():
            swap.finish()

        dcat = _dot_nt(dm_ref[...], w_ref[...])
        dp_ref[...] = dcat[:, :POOL_WIDTH]
        do = dcat[:, POOL_WIDTH:]
        for j in range(ATTN_WIDTH // 128):
            stage[j][...] = do[:, j * 128:(j + 1) * 128]
        _store_streams(stage, do_refs, ts)
        prod = do * o_ref[...].astype(f32)
        hi = prod.astype(bf16)
        lo = (prod - hi.astype(f32)).astype(bf16)
        ones = ones_ref[...]
        for j in range(ATTN_WIDTH // 128):
            cols = slice(j * 128, (j + 1) * 128)
            dl_ref[:, cols] = _dot(hi[:, cols], ones) + _dot(lo[:, cols], ones)

    row = lambda w: pl.BlockSpec((ts, w), lambda i: (i, 0))
    res = pl.pallas_call(
        body, name="out_proj_bwd", grid=(S // ts,),
        in_specs=[row(D_MODEL), pl.BlockSpec((D_MODEL, D_MODEL), lambda i: (0, 0)), row(ATTN_WIDTH),
                  pl.BlockSpec((128, 128), lambda i: (0, 0)), ANY],
        out_specs=[row(POOL_WIDTH), row(ATTN_WIDTH)] + [_stream_spec(d, ts) for d in DILATIONS] + [ANY],
        out_shape=[jax.ShapeDtypeStruct((S, POOL_WIDTH), f32), jax.ShapeDtypeStruct((S, ATTN_WIDTH), f32)]
        + [_stream_shape(S, d) for d in DILATIONS] + [_Swap.out_shape(grads)],
        scratch_shapes=_stage_scratch(ts) + _Swap.scratch(grads),
        compiler_params=_params("arbitrary"),
    )(dmix, w_out, attn_out, head_ones, grads)
    return res[0], res[1], res[2:2 + n_dil], res[2 + n_dil]


def _in_proj_bwd(du, dq, dk, dv, cos_t, sin_t, w_in, x, dx2, g1):
    S = x.shape[0]
    ts = 512

    def body(du_ref, dq_ref, dk_ref, dv_ref, cos_ref, sin_ref, w_ref, x_ref, dx2_ref, g_ref, gx_ref, dproj_ref, dg_ref):
        @pl.when(pl.program_id(0) == 0)
        def _():
            dg_ref[...] = jnp.zeros_like(dg_ref)

        dproj_ref[:, :POOL_WIDTH] = du_ref[...]
        cos = cos_ref[...]
        sin = sin_ref[...]
        first = _first_half_mask(ts)
        for j in range(ATTN_WIDTH // 128):
            cols = slice(j * 128, (j + 1) * 128)
            for base, ref in ((POOL_WIDTH, dq_ref), (POOL_WIDTH + ATTN_WIDTH, dk_ref)):
                g = ref[:, cols].astype(f32)
                pre = g * cos + _rope_partner(g * sin, first)
                dproj_ref[:, base + j * 128: base + (j + 1) * 128] = pre.astype(bf16)
        dproj_ref[:, POOL_WIDTH + 2 * ATTN_WIDTH:] = dv_ref[...]

        dh = _dot(dproj_ref[...], w_ref[...])
        xv = x_ref[...]
        r = _rms(xv)
        xhat = xv * r
        dg_ref[...] += jnp.sum(dh * xhat, axis=0, keepdims=True)
        dhg = dh * g_ref[...]
        gx_ref[...] = dx2_ref[...] + r * (dhg - xhat * jnp.mean(dhg * xhat, axis=-1, keepdims=True))

    row = lambda w: pl.BlockSpec((ts, w), lambda i: (i, 0))
    gain = pl.BlockSpec((1, D_MODEL), lambda i: (0, 0))
    return pl.pallas_call(
        body, name="in_proj_bwd", grid=(S // ts,),
        in_specs=[row(POOL_WIDTH)] + [row(ATTN_WIDTH)] * 3 + [row(128), row(128),
                  pl.BlockSpec((IN_WIDTH, D_MODEL), lambda i: (0, 0)), row(D_MODEL), row(D_MODEL), gain],
        out_specs=[row(D_MODEL), row(IN_WIDTH), gain],
        out_shape=[jax.ShapeDtypeStruct((S, D_MODEL), f32), jax.ShapeDtypeStruct((S, IN_WIDTH), bf16),
                   jax.ShapeDtypeStruct((1, D_MODEL), f32)],
        compiler_params=_params("arbitrary"),
    )(du, dq, dk, dv, cos_t, sin_t, w_in, x, dx2, g1)


def _matmul_tiles_tn(a, b, name):
    T, K, w = a.shape
    N = b.shape[1]
    tk = 1024

    def body(a_ref, b_ref, o_ref):
        def tiles_pass(first):
            for t in range(T):
                part = _dot_tn(a_ref[t], b_ref[...])
                if first:
                    o_ref[t * w:(t + 1) * w, :] = part
                else:
                    o_ref[t * w:(t + 1) * w, :] += part

        @pl.when(pl.program_id(0) == 0)
        def _():
            tiles_pass(True)

        @pl.when(pl.program_id(0) > 0)
        def _():
            tiles_pass(False)

    return pl.pallas_call(
        body, name=name, grid=(K // tk,),
        in_specs=[pl.BlockSpec((T, tk, w), lambda k: (0, k, 0)), pl.BlockSpec((tk, N), lambda k: (k, 0))],
        out_specs=pl.BlockSpec((T * w, N), lambda k: (0, 0)),
        out_shape=jax.ShapeDtypeStruct((T * w, N), f32),
        compiler_params=_params("arbitrary"),
    )(a, b)


def _matmul_tn(a, b, name):
    K, M = a.shape
    N = b.shape[1]
    tk = 1024
    w = 256

    def body(a_ref, b_ref, o_ref):
        def tiles_pass(first):
            for t in range(M // w):
                part = _dot_tn(a_ref[:, t * w:(t + 1) * w], b_ref[...])
                if first:
                    o_ref[t * w:(t + 1) * w, :] = part
                else:
                    o_ref[t * w:(t + 1) * w, :] += part

        @pl.when(pl.program_id(0) == 0)
        def _():
            tiles_pass(True)

        @pl.when(pl.program_id(0) > 0)
        def _():
            tiles_pass(False)

    return pl.pallas_call(
        body, name=name, grid=(K // tk,),
        in_specs=[pl.BlockSpec((tk, M), lambda k: (k, 0)), pl.BlockSpec((tk, N), lambda k: (k, 0))],
        out_specs=pl.BlockSpec((M, N), lambda k: (0, 0)),
        out_shape=jax.ShapeDtypeStruct((M, N), f32),
        compiler_params=_params("arbitrary"),
    )(a, b)


def _rope_tables(S):
    half = HEAD_DIM // 2
    freqs = ROPE_THETA ** (-jnp.arange(half, dtype=f32) * (2.0 / HEAD_DIM))
    ang = jnp.arange(S).astype(f32)[:, None] * freqs[None, :]
    cos = jnp.tile(jnp.cos(ang), (1, 4))
    sin = jnp.sin(ang)
    sin = jnp.tile(jnp.concatenate([-sin, sin], axis=1), (1, 2))
    return cos, sin


def _block_diag(w_pool):
    w = jnp.zeros((POOL_WIDTH, POOL_WIDTH), w_pool.dtype)
    for g in range(POOL_WIDTH // POOL_GROUP):
        w = lax.dynamic_update_slice(w, w_pool[g], (g * POOL_GROUP, g * POOL_GROUP))
    return w


def _head_ones():
    head = np.arange(128) // HEAD_DIM
    return jnp.asarray(head[:, None] == head[None, :], dtype=bf16)


def _place():
    x, y, c = lax.axis_index("x"), lax.axis_index("y"), lax.axis_index("c")
    chips = [(1 - x, y), (x, 1 - y), (1 - x, 1 - y)]
    return x, y, c, chips


ANY = pl.BlockSpec(memory_space=pl.ANY)
N_PEER_CHIPS = N_CHIPS - 1
ICI_PIECES = 4
D2D_PIECES = 8
LOCAL_PIECES = 8


def _row_chunks(rows, n, unit=32):
    units = rows // unit
    out, start = [], 0
    for i in range(n):
        size = (units // n + (1 if i < units % n else 0)) * unit
        out.append((start, size))
        start += size
    return [piece for piece in out if piece[1]]


class _LocalCopy:
    def __init__(self, src_rows, dst_rows, rows, buf, sems_in, sems_out):
        self.loads, self.stores = [], []
        for i, (start, size) in enumerate(_row_chunks(rows, LOCAL_PIECES)):
            r = pl.ds(start, size)
            self.loads.append(pltpu.make_async_copy(src_rows(r), buf.at[r], sems_in.at[i]))
            self.stores.append(pltpu.make_async_copy(buf.at[r], dst_rows(r), sems_out.at[i]))

    def start(self):
        for cp in self.loads:
            cp.start()

    def pass_on(self):
        for load, store in zip(self.loads, self.stores):
            load.wait()
            store.start()

    def finish(self):
        for store in self.stores:
            store.wait()

    @staticmethod
    def scratch(rows, dtype):
        return [pltpu.VMEM((rows, D_MODEL), dtype), pltpu.SemaphoreType.DMA((LOCAL_PIECES,)),
                pltpu.SemaphoreType.DMA((LOCAL_PIECES,))]


class _Gather:
    def __init__(self, w_ref, out_ref, send1, recv1, send2, recv2, buf, sems_in, sems_out):
        x, y, c, chips = _place()
        me = 2 * x + y
        rows = w_ref.shape[0]
        half = rows // 2
        pieces = _row_chunks(half, ICI_PIECES)
        self.own = _LocalCopy(lambda r: w_ref.at[r], lambda r: out_ref.at[me, r], rows, buf, sems_in, sems_out)

        def rows_of(core, piece):
            start, size = piece
            return pl.ds(core * half + start, size)

        self.sends, self.arrivals, self.forwards, self.forward_arrivals = [], [], [], []
        for i, piece in enumerate(pieces):
            for j, (cx, cy) in enumerate(chips):
                k = j * len(pieces) + i
                there = 2 * cx + cy

                def direct(src_chip, cx=cx, cy=cy, k=k, piece=piece):
                    return pltpu.make_async_remote_copy(
                        src_ref=w_ref.at[rows_of(c, piece)], dst_ref=out_ref.at[src_chip, rows_of(c, piece)],
                        send_sem=send1.at[k], recv_sem=recv1.at[k], device_id=(cx, cy, c), device_id_type=MESH)

                def passed(core, there=there, k=k, piece=piece):
                    return pltpu.make_async_remote_copy(
                        src_ref=out_ref.at[there, rows_of(core, piece)], dst_ref=out_ref.at[there, rows_of(core, piece)],
                        send_sem=send2.at[k], recv_sem=recv2.at[k], device_id=(x, y, 1 - c), device_id_type=MESH)

                self.sends.append(direct(me))
                self.arrivals.append(direct(there))
                self.forwards.append(passed(c))
                self.forward_arrivals.append(passed(1 - c))

    def start(self):
        for cp in self.sends:
            cp.start()
        self.own.start()

    def pass_on(self):
        self.own.pass_on()
        for arrival, forward in zip(self.arrivals, self.forwards):
            arrival.wait_recv()
            forward.start()

    def finish(self):
        for arrival in self.forward_arrivals:
            arrival.wait_recv()
        for cp in self.sends + self.forwards:
            cp.wait_send()
        self.own.finish()

    @staticmethod
    def scratch(rows, dtype):
        n = N_PEER_CHIPS * len(_row_chunks(rows // 2, ICI_PIECES))
        return [pltpu.SemaphoreType.DMA((n,))] * 4 + _LocalCopy.scratch(rows, dtype)

    @staticmethod
    def out_shape(rows, dtype):
        return jax.ShapeDtypeStruct((N_CHIPS, rows, D_MODEL), dtype)


def _gather_weights(pack):
    rows = pack.shape[0]

    def body(w_ref, out_ref, *scratch):
        gather = _Gather(w_ref, out_ref, *scratch)
        gather.start()
        gather.pass_on()
        gather.finish()

    return pl.pallas_call(
        body, name="gather_weights", in_specs=[ANY], out_specs=ANY, out_shape=_Gather.out_shape(rows, pack.dtype),
        scratch_shapes=_Gather.scratch(rows, pack.dtype),
        compiler_params=pltpu.CompilerParams(vmem_limit_bytes=VMEM_LIMIT_V7X),
    )(pack)


class _Scatter:
    def __init__(self, h_ref, out_ref, send, recv):
        x, y, c, chips = _place()
        pieces = _row_chunks(h_ref.shape[1], ICI_PIECES)
        self.copies = []
        for i, (start, size) in enumerate(pieces):
            for j, (cx, cy) in enumerate(chips):
                k = j * len(pieces) + i
                self.copies.append(pltpu.make_async_remote_copy(
                    src_ref=h_ref.at[2 * cx + cy, pl.ds(start, size)], dst_ref=out_ref.at[j, pl.ds(start, size)],
                    send_sem=send.at[k], recv_sem=recv.at[k], device_id=(cx, cy, c), device_id_type=MESH))

    def start(self):
        for cp in self.copies:
            cp.start()

    def finish(self):
        for cp in self.copies:
            cp.wait_recv()
        for cp in self.copies:
            cp.wait_send()

    @staticmethod
    def scratch(half):
        n = N_PEER_CHIPS * len(_row_chunks(half, ICI_PIECES))
        return [pltpu.SemaphoreType.DMA((n,))] * 2

    @staticmethod
    def out_shape(half, dtype):
        return jax.ShapeDtypeStruct((N_PEER_CHIPS, half, D_MODEL), dtype)


def _scatter_to_chips(h):
    half = h.shape[1]

    def body(h_ref, out_ref, send, recv):
        scatter = _Scatter(h_ref, out_ref, send, recv)
        scatter.start()
        scatter.finish()

    return pl.pallas_call(
        body, name="scatter_to_chips", in_specs=[ANY], out_specs=ANY, out_shape=_Scatter.out_shape(half, h.dtype),
        scratch_shapes=_Scatter.scratch(half),
    )(h)


class _Swap:
    def __init__(self, g_ref, theirs_ref, send, recv):
        x, y, c, _ = _place()
        half = g_ref.shape[1] // 2
        pieces = _row_chunks(half, D2D_PIECES)
        self.copies = []
        for s in range(N_CHIPS):
            for i, (start, size) in enumerate(pieces):
                k = s * len(pieces) + i
                self.copies.append(pltpu.make_async_remote_copy(
                    src_ref=g_ref.at[s, pl.ds((1 - c) * half + start, size)], dst_ref=theirs_ref.at[s, pl.ds(start, size)],
                    send_sem=send.at[k], recv_sem=recv.at[k], device_id=(x, y, 1 - c), device_id_type=MESH))

    def start(self):
        for cp in self.copies:
            cp.start()

    def finish(self):
        for cp in self.copies:
            cp.wait()

    @staticmethod
    def scratch(g):
        n = N_CHIPS * len(_row_chunks(g.shape[1] // 2, D2D_PIECES))
        return [pltpu.SemaphoreType.DMA((n,))] * 2

    @staticmethod
    def out_shape(g):
        return jax.ShapeDtypeStruct((N_CHIPS, g.shape[1] // 2, D_MODEL), g.dtype)


def _swap_halves(g):
    def body(g_ref, theirs_ref, send, recv):
        swap = _Swap(g_ref, theirs_ref, send, recv)
        swap.start()
        swap.finish()

    return pl.pallas_call(
        body, name="swap_halves", in_specs=[ANY], out_specs=ANY, out_shape=_Swap.out_shape(g),
        scratch_shapes=_Swap.scratch(g),
    )(g)


ADD_TILE_MAX_ROWS = 600


def _add_tile(half):
    return max(t for t in range(8, ADD_TILE_MAX_ROWS + 1, 8) if half % t == 0)


def _add_cores(g, theirs, name, out_dtype=f32):
    half = theirs.shape[1]
    tr = _add_tile(half)
    n_t = half // tr

    def body(c_ref, g_ref, t_ref, o_ref):
        o_ref[...] = (g_ref[...] + t_ref[...]).astype(out_dtype)

    blk = pl.BlockSpec((1, tr, D_MODEL), lambda s, t, c_ref: (s, t, 0))
    return pl.pallas_call(
        body, name=name,
        grid_spec=pltpu.PrefetchScalarGridSpec(
            num_scalar_prefetch=1, grid=(N_CHIPS, n_t),
            in_specs=[pl.BlockSpec((1, tr, D_MODEL), lambda s, t, c_ref: (s, c_ref[0] * n_t + t, 0)), blk],
            out_specs=blk),
        out_shape=jax.ShapeDtypeStruct(theirs.shape, out_dtype),
        compiler_params=_params("parallel", "parallel"),
    )(lax.axis_index("c").astype(jnp.int32).reshape(1), g, theirs)


def _add_chips(chip_sum, others, name):
    half = chip_sum.shape[1]
    tr = _add_tile(half)

    def body(me_ref, own_ref, o0, o1, o2, out_ref):
        out_ref[...] = ((own_ref[0].astype(f32) + o0[0].astype(f32)) + o1[0].astype(f32)) + o2[0].astype(f32)

    other = lambda j: pl.BlockSpec((1, tr, D_MODEL), lambda t, me_ref: (j, t, 0))
    return pl.pallas_call(
        body, name=name,
        grid_spec=pltpu.PrefetchScalarGridSpec(
            num_scalar_prefetch=1, grid=(half // tr,),
            in_specs=[pl.BlockSpec((1, tr, D_MODEL), lambda t, me_ref: (me_ref[0], t, 0)), other(0), other(1), other(2)],
            out_specs=pl.BlockSpec((tr, D_MODEL), lambda t, me_ref: (t, 0))),
        out_shape=jax.ShapeDtypeStruct((half, D_MODEL), f32),
        compiler_params=_params("parallel"),
    )((2 * lax.axis_index("x") + lax.axis_index("y")).astype(jnp.int32).reshape(1), chip_sum, others, others, others)


def _join_halves(r):
    half = r.shape[0]
    pieces = _row_chunks(half, 2 * D2D_PIECES)
    n = len(pieces)

    def body(r_ref, out_ref, send, recv, buf, sems_in, sems_out):
        x, y, c, _ = _place()
        own = _LocalCopy(lambda rr: r_ref.at[rr], lambda rr: out_ref.at[c, rr], half, buf, sems_in, sems_out)
        own.start()

        def piece(i, core):
            start, size = pieces[i]
            return pltpu.make_async_remote_copy(
                src_ref=r_ref.at[pl.ds(start, size)], dst_ref=out_ref.at[core, pl.ds(start, size)],
                send_sem=send.at[i], recv_sem=recv.at[i], device_id=(x, y, 1 - c), device_id_type=MESH)

        copies = [piece(i, c) for i in range(n)]
        for cp in copies:
            cp.start()
        own.pass_on()
        for i in range(n):
            piece(i, 1 - c).wait_recv()
        for cp in copies:
            cp.wait_send()
        own.finish()

    return pl.pallas_call(
        body, name="join_halves", in_specs=[ANY], out_specs=ANY,
        out_shape=jax.ShapeDtypeStruct((2,) + r.shape, r.dtype),
        scratch_shapes=[pltpu.SemaphoreType.DMA((n,))] * 2 + _LocalCopy.scratch(half, r.dtype),
        compiler_params=pltpu.CompilerParams(vmem_limit_bytes=VMEM_LIMIT_V7X),
    )(r)


def _sum_small(block):
    def body(b_ref, out_ref, gathered, send, recv):
        x, y, c, _ = _place()
        me = 4 * x + 2 * y + c
        gathered[me] = b_ref[...]
        sends = []
        for kk in range(1, N_DEV):
            flip = lambda v, bit: 1 - v if bit else v
            peer = (flip(x, kk & 4), flip(y, kk & 2), flip(c, kk & 1))
            cp = pltpu.make_async_remote_copy(
                src_ref=b_ref, dst_ref=gathered.at[me], send_sem=send.at[kk - 1], recv_sem=recv.at[kk - 1],
                device_id=peer, device_id_type=MESH)
            cp.start()
            sends.append(cp)
        for kk in range(1, N_DEV):
            peer_index = jnp.bitwise_xor(me, kk)
            pltpu.make_async_remote_copy(
                src_ref=b_ref, dst_ref=gathered.at[peer_index], send_sem=send.at[kk - 1], recv_sem=recv.at[kk - 1],
                device_id=(x, y, c), device_id_type=MESH).wait_recv()
        for cp in sends:
            cp.wait_send()
        acc = gathered[0]
        for dev in range(1, N_DEV):
            acc = acc + gathered[dev]
        out_ref[...] = acc

    vmem = pl.BlockSpec(memory_space=pltpu.VMEM)
    return pl.pallas_call(
        body, name="sum_small", in_specs=[vmem], out_specs=vmem,
        out_shape=jax.ShapeDtypeStruct(block.shape, block.dtype),
        scratch_shapes=[pltpu.VMEM((N_DEV,) + block.shape, block.dtype),
                        pltpu.SemaphoreType.DMA((N_DEV - 1,)), pltpu.SemaphoreType.DMA((N_DEV - 1,))],
    )(block)


def _adamw(w, g, m, v, name):
    rows, cols = w.shape
    tr = rows
    for cand in (512, 256, 128, 64, 32, 16, 8):
        if rows % cand == 0:
            tr = cand
            break
    c1 = 1.0 - ADAM_B1 ** ADAM_STEP
    c2 = 1.0 - ADAM_B2 ** ADAM_STEP

    def body(w_ref, g_ref, m_ref, v_ref, d_ref, nm_ref, nv_ref):
        gv = g_ref[...]
        nm = ADAM_B1 * m_ref[...] + (1.0 - ADAM_B1) * gv
        nv = ADAM_B2 * v_ref[...] + (1.0 - ADAM_B2) * (gv * gv)
        nm_ref[...] = nm
        nv_ref[...] = nv
        d_ref[...] = -ADAM_LR * ((nm / c1) / (jnp.sqrt(nv / c2) + ADAM_EPS) + ADAM_WD * w_ref[...])

    blk = pl.BlockSpec((tr, cols), lambda i: (i, 0))
    shape = jax.ShapeDtypeStruct((rows, cols), f32)
    return pl.pallas_call(
        body, name=name, grid=(rows // tr,), in_specs=[blk] * 4, out_specs=[blk] * 3, out_shape=[shape] * 3,
        compiler_params=_params("parallel"),
    )(w, g, m, v)


LARGE = ("w_in", "w_out", "w_gate", "w_up", "w_down")
SMALL = ("ln_pre_mix", "ln_post_mix", "ln_pre_ffn", "ln_post_ffn", "pool_scale", "w_pool")
SHARD_ROWS = {"w_in": 640, "w_out": 256, "w_gate": 704, "w_up": 704, "w_down": 704}
COLUMN_SHARDED = ("w_in", "w_gate", "w_up")
NEEDED_FIRST = ("w_in",)
NEEDED_LATER = ("w_out", "w_gate", "w_up", "w_down")
READY_EARLY = ("w_out", "w_gate", "w_up", "w_down")
READY_LATE = ("w_in",)


def _pack_shard(shards, names):
    return jnp.concatenate([shards[n].T if n in COLUMN_SHARDED else shards[n] for n in names], axis=0)


def _unpack_shard(pack, names):
    out, row = {}, 0
    for n in names:
        part = pack[row:row + SHARD_ROWS[n]]
        out[n] = part.T if n in COLUMN_SHARDED else part
        row += SHARD_ROWS[n]
    return out


def _whole_from_shards(packs, names):
    out, row = {}, 0
    for n in names:
        rows = SHARD_ROWS[n]
        out[n] = packs[:, row:row + rows].reshape(N_CHIPS * rows, D_MODEL)
        row += rows
    return out


def _shards_from_whole(grads, names):
    return jnp.concatenate([grads[n].reshape(N_CHIPS, SHARD_ROWS[n], D_MODEL) for n in names], axis=1)


def _pack_small(vals):
    rows = [vals[n].reshape(1, D_MODEL) for n in SMALL[:4]]
    rows.append(jnp.pad(vals["pool_scale"].reshape(1, POOL_WIDTH), ((0, 0), (0, D_MODEL - POOL_WIDTH))))
    rows.append(jnp.pad(vals["loss"].reshape(1, 1), ((0, 0), (0, D_MODEL - 1))))
    rows.append(jnp.zeros((2, D_MODEL), f32))
    rows.append(vals["w_pool"].reshape(16, D_MODEL))
    return jnp.concatenate(rows, axis=0)


def _unpack_small(block):
    out = {n: block[i:i + 1] for i, n in enumerate(SMALL[:4])}
    out["pool_scale"] = block[4:5, :POOL_WIDTH]
    out["loss"] = block[5, 0]
    out["w_pool"] = block[8:24].reshape(1, 4, POOL_GROUP, POOL_GROUP)
    return out


def kernel(x, ln_pre_mix, w_in, w_pool, pool_scale, w_out, ln_post_mix, ln_pre_ffn, w_gate, w_up, w_down, ln_post_ffn, loss_target, m_ln_pre_mix, m_w_in, m_w_pool, m_pool_scale, m_w_out, m_ln_post_mix, m_ln_pre_ffn, m_w_gate, m_w_up, m_w_down, m_ln_post_ffn, v_ln_pre_mix, v_w_in, v_w_pool, v_pool_scale, v_w_out, v_ln_post_mix, v_ln_pre_ffn, v_w_gate, v_w_up, v_w_down, v_ln_post_ffn):
    w = dict(ln_pre_mix=ln_pre_mix, w_in=w_in, w_pool=w_pool, pool_scale=pool_scale, w_out=w_out,
             ln_post_mix=ln_post_mix, ln_pre_ffn=ln_pre_ffn, w_gate=w_gate, w_up=w_up, w_down=w_down,
             ln_post_ffn=ln_post_ffn)
    m = dict(ln_pre_mix=m_ln_pre_mix, w_in=m_w_in, w_pool=m_w_pool, pool_scale=m_pool_scale, w_out=m_w_out,
             ln_post_mix=m_ln_post_mix, ln_pre_ffn=m_ln_pre_ffn, w_gate=m_w_gate, w_up=m_w_up, w_down=m_w_down,
             ln_post_ffn=m_ln_post_ffn)
    v = dict(ln_pre_mix=v_ln_pre_mix, w_in=v_w_in, w_pool=v_w_pool, pool_scale=v_pool_scale, w_out=v_w_out,
             ln_post_mix=v_ln_post_mix, ln_pre_ffn=v_ln_pre_ffn, w_gate=v_w_gate, w_up=v_w_up, w_down=v_w_down,
             ln_post_ffn=v_ln_post_ffn)

    xs, target = x[0], loss_target[0]
    cos_t, sin_t = _rope_tables(xs.shape[0])
    w_bd = _block_diag(w_pool[0]).astype(bf16)
    shard = {n: w[n][0].astype(bf16) for n in LARGE}

    w_in_whole = _whole_from_shards(_gather_weights(_pack_shard(shard, NEEDED_FIRST)), NEEDED_FIRST)["w_in"]
    h1, u, qs, ks, vs = _in_proj(xs, ln_pre_mix, w_in_whole, cos_t, sin_t)
    pool_out = _pool_fwd(u, w_bd, pool_scale)
    attn_out, lse, later = _attn_fwd(qs, ks, vs, _pack_shard(shard, NEEDED_LATER))
    whole = _whole_from_shards(later, NEEDED_LATER)
    mix, x2, h2 = _out_proj(pool_out, attn_out, whole["w_out"], xs, ln_post_mix, ln_pre_ffn)
    gate, up, f = _ffn_fwd(h2, whole["w_gate"], whole["w_up"], whole["w_down"])
    dy, df, dg4, loss = _loss_head(f, x2, target, ln_post_ffn)

    large = {}
    a, dgate, dup, dh2 = _ffn_bwd(df, gate, up, whole["w_gate"], whole["w_up"], whole["w_down"])
    large["w_down"] = _matmul_tiles_tn(a, df, "grad_w_down")
    large["w_gate"] = _matmul_tiles_tn(dgate, h2, "grad_w_gate")
    large["w_up"] = _matmul_tiles_tn(dup, h2, "grad_w_up")
    dx2, dmix, dg3, dg2 = _norm_bwd(dh2, dy, x2, mix, ln_pre_ffn, ln_post_mix)
    large["w_out"] = jnp.concatenate([_matmul_tn(pool_out, dmix, "grad_w_out_pool"),
                                      _matmul_tn(attn_out, dmix, "grad_w_out_attn")], axis=0)
    early = _shards_from_whole(large, READY_EARLY)
    dpool, delta, dos, early_theirs = _out_proj_bwd(dmix, whole["w_out"], attn_out, _head_ones(), early)
    early_chip = _add_cores(early, early_theirs, "add_cores_early")
    du, d_w_bd, d_scale = _pool_bwd(u, dpool, w_bd, pool_scale)
    dq, dk, dv, early_others = _attn_bwd(qs, ks, vs, dos, lse, delta, early_chip)
    grad_x, dproj, dg1 = _in_proj_bwd(du, dq, dk, dv, cos_t, sin_t, w_in_whole, xs, dx2, ln_pre_mix)
    large["w_in"] = _matmul_tn(dproj, h1, "grad_w_in")
    late = _shards_from_whole(large, READY_LATE)
    late_chip = _add_cores(late, _swap_halves(late), "add_cores_late", bf16)
    late_others = _scatter_to_chips(late_chip)
    early_half = _add_chips(early_chip, early_others, "add_chips_early")
    late_half = _add_chips(late_chip, late_others, "add_chips_late")
    joined = _join_halves(jnp.concatenate([early_half, late_half], axis=0))
    n_early = early_half.shape[0]
    grads = _unpack_shard(joined[:, :n_early].reshape(-1, D_MODEL), READY_EARLY)
    grads.update(_unpack_shard(joined[:, n_early:].reshape(-1, D_MODEL), READY_LATE))

    d_w_pool = jnp.stack([d_w_bd[g * POOL_GROUP:(g + 1) * POOL_GROUP, g * POOL_GROUP:(g + 1) * POOL_GROUP]
                          for g in range(POOL_WIDTH // POOL_GROUP)])
    small = dict(ln_pre_mix=dg1, ln_post_mix=dg2, ln_pre_ffn=dg3, ln_post_ffn=dg4, pool_scale=d_scale, w_pool=d_w_pool)
    total = _unpack_small(_sum_small(_pack_small(dict(small, loss=loss))))
    for n in SMALL:
        grads[n] = total[n]

    delta_w, new_m, new_v = {}, {}, {}
    for n in LARGE:
        delta_w[n], new_m[n], new_v[n] = _adamw(w[n][0], grads[n], m[n][0], v[n][0], "adamw_" + n)
    small_state = [_pack_small(dict({n: s[n] for n in SMALL}, loss=jnp.zeros((), f32))) for s in (w, m, v)]
    small_grad = _pack_small(dict({n: grads[n] for n in SMALL}, loss=jnp.zeros((), f32)))
    sd, sm, sv = _adamw(small_state[0], small_grad, small_state[1], small_state[2], "adamw_small")
    for out, block in ((delta_w, sd), (new_m, sm), (new_v, sv)):
        un = _unpack_small(block)
        for n in SMALL:
            out[n] = un[n]

    names = ("ln_pre_mix", "w_in", "w_pool", "pool_scale", "w_out", "ln_post_mix", "ln_pre_ffn", "w_gate", "w_up",
             "w_down", "ln_post_ffn")
    full = lambda d: [d[n].reshape(w[n].shape) for n in names]
    return (total["loss"], grad_x[None], *full(grads), *full(delta_w), *full(new_m), *full(new_v))
```

```python
import numpy as np
import jax
import jax.numpy as jnp
from jax import lax
from jax.experimental import pallas as pl
from jax.experimental.pallas import tpu as pltpu

D_MODEL = 1024
POOL_WIDTH = 256
POOL_GROUP = 64
ATTN_WIDTH = 768
HEAD_DIM = 64
IN_WIDTH = 2560
D_FF = 2816
BLOCK = 128
DILATIONS = (1, 4, 16)
ROPE_THETA = 10000.0
EPS = 1e-6
ATTN_SCALE = 0.125
NEG = -1e30

ADAM_LR = 0.001
ADAM_B1 = 0.9
ADAM_B2 = 0.999
ADAM_EPS = 1e-08
ADAM_WD = 0.01
ADAM_STEP = 10

N_CHIPS = 4
N_DEV = 8
VMEM_LIMIT_V7X = 56 * 1024 * 1024
MESH = pl.DeviceIdType.MESH

f32 = jnp.float32
bf16 = jnp.bfloat16


def _params(*sem):
    return pltpu.CompilerParams(dimension_semantics=sem, vmem_limit_bytes=VMEM_LIMIT_V7X)


def _dot(a, b):
    return jnp.dot(a, b, preferred_element_type=f32)


def _dot_nt(a, b):
    return lax.dot_general(a, b, (((1,), (1,)), ((), ())), preferred_element_type=f32)


def _dot_tn(a, b):
    return lax.dot_general(a, b, (((0,), (0,)), ((), ())), preferred_element_type=f32)


def _rope_partner(a, first_half):
    return jnp.where(first_half, pltpu.roll(a, 96, 1), pltpu.roll(a, 32, 1))


def _first_half_mask(rows):
    lane = lax.broadcasted_iota(jnp.int32, (rows, 128), 1)
    return (lane % HEAD_DIM) < (HEAD_DIM // 2)


def _stream_spec(d, ts):
    return pl.BlockSpec((d, ts // d, ATTN_WIDTH), lambda i: (0, i, 0))


def _stream_shape(S, d):
    return jax.ShapeDtypeStruct((d, S // d, ATTN_WIDTH), bf16)


N_STAGE = ATTN_WIDTH // 128


def _stage_scratch(ts):
    return [pltpu.VMEM((ts, 128), f32)] * N_STAGE


def _store_streams(stage, out_refs, ts):
    for d, ref in zip(DILATIONS, out_refs):
        for r in range(d):
            rows = pl.ds(0, ts) if d == 1 else pl.ds(r, ts // d, stride=d)
            for j in range(N_STAGE):
                ref[r, :, j * 128:(j + 1) * 128] = stage[j][rows, :].astype(bf16)


def _in_proj(x, g1, w_in, cos_t, sin_t):
    S = x.shape[0]
    ts = 512

    def body(x_ref, g_ref, w_ref, cos_ref, sin_ref, h_ref, u_ref, *rest):
        outs, stage = rest[:-N_STAGE], rest[-N_STAGE:]
        xv = x_ref[...]
        r = lax.rsqrt(jnp.mean(xv * xv, axis=-1, keepdims=True) + EPS)
        h = ((xv * r) * g_ref[...]).astype(bf16)
        h_ref[...] = h
        proj = _dot_nt(h, w_ref[...])
        u_ref[...] = proj[:, :POOL_WIDTH]
        cos = cos_ref[...]
        sin = sin_ref[...]
        first = _first_half_mask(ts)
        n_dil = len(DILATIONS)
        for which, base in enumerate((POOL_WIDTH, POOL_WIDTH + ATTN_WIDTH)):
            for j in range(ATTN_WIDTH // 128):
                a = proj[:, base + j * 128: base + (j + 1) * 128]
                if which == 0:
                    a = a * ATTN_SCALE
                stage[j][...] = a * cos + _rope_partner(a, first) * sin
            _store_streams(stage, outs[which * n_dil:(which + 1) * n_dil], ts)
        for j in range(ATTN_WIDTH // 128):
            base = POOL_WIDTH + 2 * ATTN_WIDTH + j * 128
            stage[j][...] = proj[:, base:base + 128]
        _store_streams(stage, outs[2 * n_dil:], ts)

    row = lambda w: pl.BlockSpec((ts, w), lambda i: (i, 0))
    streams = [_stream_spec(d, ts) for d in DILATIONS] * 3
    res = pl.pallas_call(
        body, name="in_proj", grid=(S // ts,),
        in_specs=[row(D_MODEL), pl.BlockSpec((1, D_MODEL), lambda i: (0, 0)),
                  pl.BlockSpec((IN_WIDTH, D_MODEL), lambda i: (0, 0)), row(128), row(128)],
        out_specs=[row(D_MODEL), row(POOL_WIDTH)] + streams,
        out_shape=[jax.ShapeDtypeStruct((S, D_MODEL), bf16), jax.ShapeDtypeStruct((S, POOL_WIDTH), f32)]
        + [_stream_shape(S, d) for d in DILATIONS] * 3,
        scratch_shapes=_stage_scratch(ts),
        compiler_params=_params("parallel"),
    )(x, g1, w_in, cos_t, sin_t)
    n = len(DILATIONS)
    return res[0], res[1], res[2:2 + n], res[2 + n:2 + 2 * n], res[2 + 2 * n:]


POOL_HALO = 16


def _pool_lane_group(rows):
    return lax.broadcasted_iota(jnp.int32, (rows, POOL_WIDTH), 1) // POOL_GROUP


def _pool_select(group, s2, s4, s8, s16):
    return jnp.where(group == 0, s2, jnp.where(group == 1, s4, jnp.where(group == 2, s8, s16)))


def _pool_count(t0, rows):
    group = _pool_lane_group(rows)
    t = t0 + lax.broadcasted_iota(jnp.int32, (rows, POOL_WIDTH), 0)
    win = _pool_select(group, 2, 4, 8, 16)
    return jnp.minimum(t + 1, win).astype(f32)


def _pool_diff(u_halo, u_tile, t0):
    ts = u_tile.shape[0]
    ext = jnp.concatenate([u_halo, u_tile], axis=0)
    s2 = ext + pltpu.roll(ext, 1, 0)
    s4 = s2 + pltpu.roll(s2, 2, 0)
    s8 = s4 + pltpu.roll(s4, 4, 0)
    s16 = s8 + pltpu.roll(s8, 8, 0)
    group = _pool_lane_group(ts + POOL_HALO)
    wsum = _pool_select(group, s2, s4, s8, s16)[POOL_HALO:]
    return wsum / _pool_count(t0, ts) - u_tile


def _pool_specs(ts, n_tiles):
    tile = pl.BlockSpec((ts, POOL_WIDTH), lambda i: (i, 0))
    per = ts // POOL_HALO
    before = pl.BlockSpec((POOL_HALO, POOL_WIDTH), lambda i: (jnp.maximum(i * per - 1, 0), 0))
    after = pl.BlockSpec((POOL_HALO, POOL_WIDTH), lambda i: (jnp.minimum((i + 1) * per, n_tiles * per - 1), 0))
    return tile, before, after


def _pool_fwd(u, w_bd, scale):
    S = u.shape[0]
    ts = 512
    n_tiles = S // ts

    def body(u_ref, halo_ref, w_ref, sc_ref, y_ref):
        i = pl.program_id(0)
        halo = jnp.where(i > 0, halo_ref[...], 0.0)
        d = _pool_diff(halo, u_ref[...], i * ts)
        y_ref[...] = (_dot(d.astype(bf16), w_ref[...]) * sc_ref[...]).astype(bf16)

    tile, before, _ = _pool_specs(ts, n_tiles)
    return pl.pallas_call(
        body, name="pool_fwd", grid=(n_tiles,),
        in_specs=[tile, before, pl.BlockSpec((POOL_WIDTH, POOL_WIDTH), lambda i: (0, 0)),
                  pl.BlockSpec((1, POOL_WIDTH), lambda i: (0, 0))],
        out_specs=tile, out_shape=jax.ShapeDtypeStruct((S, POOL_WIDTH), bf16),
        compiler_params=_params("parallel"),
    )(u, u, w_bd, scale)


def _pool_bwd(u, dy, w_bd, scale):
    S = u.shape[0]
    ts = 512
    n_tiles = S // ts

    def body(u_ref, halo_ref, dy_ref, dy_next_ref, w_ref, sc_ref, du_ref, dw_ref, dsc_ref):
        i = pl.program_id(0)

        @pl.when(i == 0)
        def _():
            dw_ref[...] = jnp.zeros_like(dw_ref)
            dsc_ref[...] = jnp.zeros_like(dsc_ref)

        halo = jnp.where(i > 0, halo_ref[...], 0.0)
        d = _pool_diff(halo, u_ref[...], i * ts).astype(bf16)
        w = w_ref[...]
        sc = sc_ref[...]
        dy_tile = dy_ref[...]
        z = _dot(d, w)
        dsc_ref[...] += jnp.sum(dy_tile * z, axis=0, keepdims=True)
        dy_next = jnp.where(i < n_tiles - 1, dy_next_ref[...], 0.0)
        dz = (jnp.concatenate([dy_tile, dy_next], axis=0) * sc).astype(bf16)
        dw_ref[...] += _dot_tn(d, dz[:ts])
        dd = _dot_nt(dz, w)
        e = dd / _pool_count(i * ts, ts + POOL_HALO)
        n = ts + POOL_HALO
        f2 = e + pltpu.roll(e, n - 1, 0)
        f4 = f2 + pltpu.roll(f2, n - 2, 0)
        f8 = f4 + pltpu.roll(f4, n - 4, 0)
        f16 = f8 + pltpu.roll(f8, n - 8, 0)
        fsum = _pool_select(_pool_lane_group(n), f2, f4, f8, f16)
        du_ref[...] = (fsum[:ts] - dd[:ts]).astype(bf16)

    tile, before, after = _pool_specs(ts, n_tiles)
    return pl.pallas_call(
        body, name="pool_bwd", grid=(n_tiles,),
        in_specs=[tile, before, tile, after, pl.BlockSpec((POOL_WIDTH, POOL_WIDTH), lambda i: (0, 0)),
                  pl.BlockSpec((1, POOL_WIDTH), lambda i: (0, 0))],
        out_specs=[tile, pl.BlockSpec((POOL_WIDTH, POOL_WIDTH), lambda i: (0, 0)),
                   pl.BlockSpec((1, POOL_WIDTH), lambda i: (0, 0))],
        out_shape=[jax.ShapeDtypeStruct((S, POOL_WIDTH), bf16), jax.ShapeDtypeStruct((POOL_WIDTH, POOL_WIDTH), f32),
                   jax.ShapeDtypeStruct((1, POOL_WIDTH), f32)],
        compiler_params=_params("arbitrary"),
    )(u, u, dy, dy, w_bd, scale)


SUPER = BLOCK * DILATIONS[-1]
UNITS = SUPER // BLOCK
FWD_UNROLL = 16
BWD_UNROLL = 8


def _band_mask(has_prev):
    qi = lax.broadcasted_iota(jnp.int32, (BLOCK, 2 * BLOCK), 0)
    kj = lax.broadcasted_iota(jnp.int32, (BLOCK, 2 * BLOCK), 1)
    return (kj >= qi) & (kj <= qi + BLOCK) & ((kj >= BLOCK) | has_prev)


def _head0_mask(rows=BLOCK):
    return lax.broadcasted_iota(jnp.int32, (rows, 128), 1) < HEAD_DIM


def _band_mask_t(has_prev):
    ki = lax.broadcasted_iota(jnp.int32, (2 * BLOCK, 2 * BLOCK), 0)
    qj = lax.broadcasted_iota(jnp.int32, (2 * BLOCK, 2 * BLOCK), 1) % BLOCK
    return (ki >= qj) & (ki <= qj + BLOCK) & ((ki >= BLOCK) | has_prev)


def _head_pair_rows(a, h0):
    zero = jnp.zeros_like(a)
    return jnp.concatenate([jnp.where(h0, a, zero), jnp.where(h0, zero, a)], axis=0)


def _per_query_row(stat):
    t = stat.T
    return jnp.concatenate([jnp.concatenate([t[:HEAD_DIM]] * 4, axis=0), jnp.concatenate([t[HEAD_DIM:]] * 4, axis=0)],
                           axis=1)


def _natural_rows(d, r, n):
    if d == 1:
        return pl.ds(pl.multiple_of(n * BLOCK, BLOCK), BLOCK)
    return pl.ds(n * (BLOCK * d) + r, BLOCK, stride=d)


def _unit_place(d, u):
    per_stream = UNITS // d
    return u // per_stream, u % per_stream, per_stream


def _block_rows(n):
    return pl.ds(pl.multiple_of(n * BLOCK, BLOCK), BLOCK)


def _band(cur_ref, tail_ref, r, n):
    before = jnp.where(n > 0, cur_ref[r, _block_rows(jnp.maximum(n - 1, 0)), :], tail_ref[r])
    return jnp.concatenate([before, cur_ref[r, _block_rows(n), :]], axis=0)


def _attn_in_specs(S, with_do):
    specs = []
    last = S // SUPER - 1
    for d in DILATIONS:
        per_stream = UNITS // d
        cur = pl.BlockSpec((d, SUPER // d, 128), lambda hp, sb: (0, jnp.minimum(sb, last), hp))
        tail = pl.BlockSpec(
            (d, BLOCK, 128),
            lambda hp, sb, per_stream=per_stream: (0, jnp.maximum(jnp.minimum(sb, last) * per_stream - 1, 0), hp))
        specs += [cur] * (2 if with_do else 1) + [cur, tail, cur, tail]
    return specs


def _attn_fwd(qs, ks, vs, pack):
    S = qs[0].shape[1]
    n_dil = len(DILATIONS)
    n_steps = S // SUPER
    n_total = (ATTN_WIDTH // 128) * n_steps

    def body(*refs):
        ins, pack_ref = refs[:5 * n_dil], refs[5 * n_dil]
        out_ref, lse_ref, gathered_ref = refs[5 * n_dil + 1:5 * n_dil + 4]
        scratch = refs[5 * n_dil + 4:]
        o_sc, l_sc = scratch[:n_dil], scratch[n_dil:2 * n_dil]
        gather = _Gather(pack_ref, gathered_ref, *scratch[2 * n_dil:])
        sb = pl.program_id(1)
        step = pl.program_id(0) * n_steps + sb

        @pl.when(step == 0)
        def _():
            gather.start()

        h0 = _head0_mask()
        for ci, d in enumerate(DILATIONS):
            q_ref, kc_ref, kp_ref, vc_ref, vp_ref = ins[5 * ci:5 * ci + 5]

            def unit(u, carry, d=d, ci=ci, q_ref=q_ref, kc_ref=kc_ref, kp_ref=kp_ref, vc_ref=vc_ref, vp_ref=vp_ref):
                r, n, _ = _unit_place(d, u)
                qv = q_ref[r, _block_rows(n), :]
                kb = _band(kc_ref, kp_ref, r, n)
                vb = _band(vc_ref, vp_ref, r, n)
                valid = _band_mask((sb > 0) | (n > 0))
                outs, lses = [], []
                for h in range(2):
                    keep = h0 if h == 0 else jnp.logical_not(h0)
                    qh = jnp.where(keep, qv, jnp.zeros_like(qv))
                    s = jnp.where(valid, _dot_nt(qh, kb), NEG)
                    m = jnp.max(s, axis=1, keepdims=True)
                    e = jnp.exp(s - m)
                    den = jnp.sum(e, axis=1, keepdims=True)
                    outs.append(_dot(e.astype(bf16), vb) * (1.0 / den))
                    lses.append(jnp.broadcast_to(m + jnp.log(den), (BLOCK, 128)))
                rows = _natural_rows(d, r, n)
                o_sc[ci][rows, :] = jnp.where(h0, outs[0], outs[1])
                l_sc[ci][rows, :] = jnp.where(h0, lses[0], lses[1])
                return carry

            lax.fori_loop(0, UNITS, unit, 0, unroll=FWD_UNROLL)

        def merge(t, carry):
            rows = pl.ds(pl.multiple_of(t * 256, 256), 256)
            a, b, c = l_sc[0][rows, :], l_sc[1][rows, :], l_sc[2][rows, :]
            m = jnp.maximum(jnp.maximum(a, b), c)
            ea, eb, ec = jnp.exp(a - m), jnp.exp(b - m), jnp.exp(c - m)
            tot = ea + eb + ec
            out_ref[rows, :] = ((ea / tot) * o_sc[0][rows, :] + (eb / tot) * o_sc[1][rows, :]
                                + (ec / tot) * o_sc[2][rows, :]).astype(bf16)
            lse_ref[rows, :] = m + jnp.log(tot)
            return carry

        lax.fori_loop(0, SUPER // 256, merge, 0)

        @pl.when(step == (2 * n_total) // 3)
        def _():
            gather.pass_on()

        @pl.when(step == n_total - 1)
        def _():
            gather.finish()

    args = []
    for q, k, v in zip(qs, ks, vs):
        args += [q, k, k, v, v]
    nat = pl.BlockSpec((SUPER, 128), lambda hp, sb: (sb, hp))
    rows = pack.shape[0]
    return pl.pallas_call(
        body, name="attn_fwd", grid=(ATTN_WIDTH // 128, n_steps),
        in_specs=_attn_in_specs(S, False) + [ANY], out_specs=[nat, nat, ANY],
        out_shape=[jax.ShapeDtypeStruct((S, ATTN_WIDTH), bf16), jax.ShapeDtypeStruct((S, ATTN_WIDTH), f32),
                   _Gather.out_shape(rows, pack.dtype)],
        scratch_shapes=[pltpu.VMEM((SUPER, 128), f32)] * (2 * n_dil) + _Gather.scratch(rows, pack.dtype),
        compiler_params=_params("arbitrary", "arbitrary"),
    )(*args, pack)


def _attn_bwd(qs, ks, vs, dos, lse, delta, chip_sum):
    S = qs[0].shape[1]
    n_steps = S // SUPER
    last = n_steps - 1
    n_dil = len(DILATIONS)
    n_total = (ATTN_WIDTH // 128) * (n_steps + 1)

    def body(*refs):
        ins, (lse_ref, dl_ref, sum_ref) = refs[:6 * n_dil], refs[6 * n_dil:6 * n_dil + 3]
        dq_ref, dk_ref, dv_ref, others_ref = refs[6 * n_dil + 3:6 * n_dil + 7]
        dq_acc, dk_acc, dv_acc = refs[6 * n_dil + 7:6 * n_dil + 10]
        scatter = _Scatter(sum_ref, others_ref, *refs[6 * n_dil + 10:])
        sb = pl.program_id(1)
        step = pl.program_id(0) * (n_steps + 1) + sb
        cur = sb % 2
        prv = 1 - cur

        @pl.when(step == 0)
        def _():
            scatter.start()

        @pl.when(sb < n_steps)
        def _():
            dq_acc[...] = jnp.zeros_like(dq_acc)
            dk_acc[cur] = jnp.zeros((SUPER, 128), f32)
            dv_acc[cur] = jnp.zeros((SUPER, 128), f32)
            h0 = _head0_mask()
            for ci, d in enumerate(DILATIONS):
                q_ref, do_ref, kc_ref, kp_ref, vc_ref, vp_ref = ins[6 * ci:6 * ci + 6]

                def unit(u, carry, d=d, q_ref=q_ref, do_ref=do_ref, kc_ref=kc_ref, kp_ref=kp_ref, vc_ref=vc_ref,
                         vp_ref=vp_ref):
                    r, n, per_stream = _unit_place(d, u)
                    qv = q_ref[r, _block_rows(n), :]
                    dov = do_ref[r, _block_rows(n), :]
                    kb = _band(kc_ref, kp_ref, r, n)
                    vb = _band(vc_ref, vp_ref, r, n)
                    rows = _natural_rows(d, r, n)
                    has_prev = (sb > 0) | (n > 0)
                    q_pair = _head_pair_rows(qv, h0)
                    do_pair = _head_pair_rows(dov, h0)
                    s_t = jnp.where(_band_mask_t(has_prev), _dot_nt(kb, q_pair), NEG)
                    p_t = jnp.exp(s_t - _per_query_row(lse_ref[rows, :]))
                    dp_t = _dot_nt(vb, do_pair)
                    ds_t = (p_t * (dp_t - _per_query_row(dl_ref[rows, :]))).astype(bf16)
                    dvb = _dot(p_t.astype(bf16), do_pair)
                    dkb = _dot(ds_t, q_pair)
                    dq_pair = _dot_tn(ds_t, kb)
                    dq_acc[rows, :] += jnp.where(h0, dq_pair[:BLOCK], dq_pair[BLOCK:])
                    dk_acc[cur, rows, :] += dkb[BLOCK:]
                    dv_acc[cur, rows, :] += dvb[BLOCK:]

                    slot = jnp.where((n > 0) | (sb == 0), cur, prv)
                    before = _natural_rows(d, r, jnp.where(n > 0, n - 1, per_stream - 1))
                    dk_acc[slot, before, :] += dkb[:BLOCK]
                    dv_acc[slot, before, :] += dvb[:BLOCK]
                    return carry

                lax.fori_loop(0, UNITS, unit, 0, unroll=BWD_UNROLL)
            dq_ref[...] = (dq_acc[...] * ATTN_SCALE).astype(bf16)

        @pl.when(sb > 0)
        def _():
            dk_ref[...] = dk_acc[prv].astype(bf16)
            dv_ref[...] = dv_acc[prv].astype(bf16)

        @pl.when(step == n_total - 1)
        def _():
            scatter.finish()

    args = []
    for q, k, v, do in zip(qs, ks, vs, dos):
        args += [q, do, k, k, v, v]
    nat = pl.BlockSpec((SUPER, 128), lambda hp, sb: (jnp.minimum(sb, last), hp))
    nat_before = pl.BlockSpec((SUPER, 128), lambda hp, sb: (jnp.clip(sb - 1, 0, last), hp))
    out = jax.ShapeDtypeStruct((S, ATTN_WIDTH), bf16)
    half = chip_sum.shape[1]
    return pl.pallas_call(
        body, name="attn_bwd", grid=(ATTN_WIDTH // 128, n_steps + 1),
        in_specs=_attn_in_specs(S, True) + [nat, nat, ANY], out_specs=[nat, nat_before, nat_before, ANY],
        out_shape=[out, out, out, _Scatter.out_shape(half, chip_sum.dtype)],
        scratch_shapes=[pltpu.VMEM((SUPER, 128), f32), pltpu.VMEM((2, SUPER, 128), f32),
                        pltpu.VMEM((2, SUPER, 128), f32)] + _Scatter.scratch(half),
        compiler_params=_params("arbitrary", "arbitrary"),
    )(*args, lse, delta, chip_sum)


def _rms(v):
    return lax.rsqrt(jnp.mean(v * v, axis=-1, keepdims=True) + EPS)


def _out_proj(pool_out, attn_out, w_out, x, g2, g3):
    S = x.shape[0]
    ts = 512

    def body(p_ref, a_ref, w_ref, x_ref, g2_ref, g3_ref, mix_ref, x2_ref, h2_ref):
        mix = _dot(p_ref[...], w_ref[:POOL_WIDTH, :]) + _dot(a_ref[...], w_ref[POOL_WIDTH:, :])
        mix_ref[...] = mix
        x2 = x_ref[...] + (mix * _rms(mix)) * g2_ref[...]
        x2_ref[...] = x2
        h2_ref[...] = ((x2 * _rms(x2)) * g3_ref[...]).astype(bf16)

    row = lambda w: pl.BlockSpec((ts, w), lambda i: (i, 0))
    gain = pl.BlockSpec((1, D_MODEL), lambda i: (0, 0))
    return pl.pallas_call(
        body, name="out_proj", grid=(S // ts,),
        in_specs=[row(POOL_WIDTH), row(ATTN_WIDTH), pl.BlockSpec((D_MODEL, D_MODEL), lambda i: (0, 0)),
                  row(D_MODEL), gain, gain],
        out_specs=[row(D_MODEL)] * 3,
        out_shape=[jax.ShapeDtypeStruct((S, D_MODEL), f32), jax.ShapeDtypeStruct((S, D_MODEL), f32),
                   jax.ShapeDtypeStruct((S, D_MODEL), bf16)],
        compiler_params=_params("parallel"),
    )(pool_out, attn_out, w_out, x, g2, g3)


FF_TILE = 256
FF_STEP_ROWS = 2048
FF_ROWS = 512
FF_BWD_ROWS = 256


def _sigmoid(g):
    return 1.0 / (1.0 + jnp.exp(-g))


def _ff_act_shape(S):
    return jax.ShapeDtypeStruct((D_FF // FF_TILE, S, FF_TILE), bf16)


def _ff_act_spec(ts):
    return pl.BlockSpec((1, ts, FF_TILE), lambda i, j: (j, i, 0))


def _ffn_fwd(h2, w_gate, w_up, w_down):
    S = h2.shape[0]
    ts = min(S, FF_STEP_ROWS)

    def body(h_ref, wg_ref, wu_ref, wd_ref, gate_ref, up_ref, f_ref):
        def rows_pass(first):
            def sub(i, carry):
                rows = pl.ds(pl.multiple_of(i * FF_ROWS, FF_ROWS), FF_ROWS)
                h = h_ref[rows, :]
                gate = _dot_nt(h, wg_ref[...])
                up = _dot_nt(h, wu_ref[...])
                gate_ref[0, rows, :] = gate.astype(bf16)
                up_ref[0, rows, :] = up.astype(bf16)
                part = _dot((gate * _sigmoid(gate) * up).astype(bf16), wd_ref[...])
                if first:
                    f_ref[rows, :] = part
                else:
                    f_ref[rows, :] += part
                return carry

            lax.fori_loop(0, ts // FF_ROWS, sub, 0, unroll=True)

        @pl.when(pl.program_id(1) == 0)
        def _():
            rows_pass(True)

        @pl.when(pl.program_id(1) > 0)
        def _():
            rows_pass(False)

    act = _ff_act_spec(ts)
    weight = pl.BlockSpec((FF_TILE, D_MODEL), lambda i, j: (j, 0))
    return pl.pallas_call(
        body, name="ffn_fwd", grid=(S // ts, D_FF // FF_TILE),
        in_specs=[pl.BlockSpec((ts, D_MODEL), lambda i, j: (i, 0)), weight, weight, weight],
        out_specs=[act, act, pl.BlockSpec((ts, D_MODEL), lambda i, j: (i, 0))],
        out_shape=[_ff_act_shape(S), _ff_act_shape(S), jax.ShapeDtypeStruct((S, D_MODEL), f32)],
        compiler_params=_params("parallel", "arbitrary"),
    )(h2, w_gate, w_up, w_down)


def _loss_head(f, x2, target, g4):
    S = f.shape[0]
    ts = 512

    def body(f_ref, x2_ref, t_ref, g_ref, dy_ref, df_ref, dg_ref, loss_ref):
        @pl.when(pl.program_id(0) == 0)
        def _():
            dg_ref[...] = jnp.zeros_like(dg_ref)
            loss_ref[...] = jnp.zeros_like(loss_ref)

        fv = f_ref[...]
        g = g_ref[...]
        r = _rms(fv)
        fhat = fv * r
        err = (x2_ref[...] + fhat * g) - t_ref[...]
        loss_ref[...] += 0.5 * jnp.sum(jnp.mean(err * err, axis=-1, keepdims=True), axis=0, keepdims=True)
        dy = err * (1.0 / D_MODEL)
        dy_ref[...] = dy
        dg_ref[...] += jnp.sum(dy * fhat, axis=0, keepdims=True)
        dyg = dy * g
        df_ref[...] = (r * (dyg - fhat * jnp.mean(dyg * fhat, axis=-1, keepdims=True))).astype(bf16)

    row = pl.BlockSpec((ts, D_MODEL), lambda i: (i, 0))
    gain = pl.BlockSpec((1, D_MODEL), lambda i: (0, 0))
    return pl.pallas_call(
        body, name="loss_head", grid=(S // ts,), in_specs=[row, row, row, gain],
        out_specs=[row, row, gain, pl.BlockSpec((1, 1), lambda i: (0, 0))],
        out_shape=[jax.ShapeDtypeStruct((S, D_MODEL), f32), jax.ShapeDtypeStruct((S, D_MODEL), bf16),
                   jax.ShapeDtypeStruct((1, D_MODEL), f32), jax.ShapeDtypeStruct((1, 1), f32)],
        compiler_params=_params("arbitrary"),
    )(f, x2, target, g4)


def _ffn_bwd(df, gate, up, w_gate, w_up, w_down):
    S = df.shape[0]
    ts = min(S, FF_STEP_ROWS)

    def body(df_ref, gate_ref, up_ref, wg_ref, wu_ref, wd_ref, a_ref, dgate_ref, dup_ref, dh_ref):
        def rows_pass(first):
            def sub(i, carry):
                rows = pl.ds(pl.multiple_of(i * FF_BWD_ROWS, FF_BWD_ROWS), FF_BWD_ROWS)
                da = _dot_nt(df_ref[rows, :], wd_ref[...])
                g = gate_ref[0, rows, :].astype(f32)
                u = up_ref[0, rows, :].astype(f32)
                sig = _sigmoid(g)
                silu = g * sig
                a_ref[0, rows, :] = (silu * u).astype(bf16)
                dup = (da * silu).astype(bf16)
                dgate = (da * u * (sig * (1.0 + g * (1.0 - sig)))).astype(bf16)
                dup_ref[0, rows, :] = dup
                dgate_ref[0, rows, :] = dgate
                part = _dot(dgate, wg_ref[...]) + _dot(dup, wu_ref[...])
                if first:
                    dh_ref[rows, :] = part
                else:
                    dh_ref[rows, :] += part
                return carry

            lax.fori_loop(0, ts // FF_BWD_ROWS, sub, 0, unroll=True)

        @pl.when(pl.program_id(1) == 0)
        def _():
            rows_pass(True)

        @pl.when(pl.program_id(1) > 0)
        def _():
            rows_pass(False)

    act = _ff_act_spec(ts)
    row = pl.BlockSpec((ts, D_MODEL), lambda i, j: (i, 0))
    return pl.pallas_call(
        body, name="ffn_bwd", grid=(S // ts, D_FF // FF_TILE),
        in_specs=[row, act, act,
                  pl.BlockSpec((FF_TILE, D_MODEL), lambda i, j: (j, 0)),
                  pl.BlockSpec((FF_TILE, D_MODEL), lambda i, j: (j, 0)),
                  pl.BlockSpec((FF_TILE, D_MODEL), lambda i, j: (j, 0))],
        out_specs=[act, act, act, row],
        out_shape=[_ff_act_shape(S)] * 3 + [jax.ShapeDtypeStruct((S, D_MODEL), f32)],
        compiler_params=_params("parallel", "arbitrary"),
    )(df, gate, up, w_gate, w_up, w_down)


def _norm_bwd(dh2, dy, x2, mix, g3, g2):
    S = dh2.shape[0]
    ts = 512

    def body(dh_ref, dy_ref, x2_ref, mix_ref, g3_ref, g2_ref, dx2_ref, dmix_ref, dg3_ref, dg2_ref):
        @pl.when(pl.program_id(0) == 0)
        def _():
            dg3_ref[...] = jnp.zeros_like(dg3_ref)
            dg2_ref[...] = jnp.zeros_like(dg2_ref)

        dh = dh_ref[...]
        x2 = x2_ref[...]
        r3 = _rms(x2)
        xhat = x2 * r3
        dg3_ref[...] += jnp.sum(dh * xhat, axis=0, keepdims=True)
        dhg = dh * g3_ref[...]
        dx2 = dy_ref[...] + r3 * (dhg - xhat * jnp.mean(dhg * xhat, axis=-1, keepdims=True))
        dx2_ref[...] = dx2
        mix = mix_ref[...]
        r2 = _rms(mix)
        mhat = mix * r2
        dg2_ref[...] += jnp.sum(dx2 * mhat, axis=0, keepdims=True)
        dmg = dx2 * g2_ref[...]
        dmix_ref[...] = (r2 * (dmg - mhat * jnp.mean(dmg * mhat, axis=-1, keepdims=True))).astype(bf16)

    row = pl.BlockSpec((ts, D_MODEL), lambda i: (i, 0))
    gain = pl.BlockSpec((1, D_MODEL), lambda i: (0, 0))
    return pl.pallas_call(
        body, name="norm_bwd", grid=(S // ts,), in_specs=[row, row, row, row, gain, gain],
        out_specs=[row, row, gain, gain],
        out_shape=[jax.ShapeDtypeStruct((S, D_MODEL), f32), jax.ShapeDtypeStruct((S, D_MODEL), bf16),
                   jax.ShapeDtypeStruct((1, D_MODEL), f32), jax.ShapeDtypeStruct((1, D_MODEL), f32)],
        compiler_params=_params("arbitrary"),
    )(dh2, dy, x2, mix, g3, g2)


def _out_proj_bwd(dmix, w_out, attn_out, head_ones, grads):
    S = dmix.shape[0]
    ts = 512
    n_dil = len(DILATIONS)

    def body(dm_ref, w_ref, o_ref, ones_ref, g_ref, dp_ref, dl_ref, *rest):
        do_refs, theirs_ref = rest[:n_dil], rest[n_dil]
        stage = rest[n_dil + 1:n_dil + 1 + N_STAGE]
        swap = _Swap(g_ref, theirs_ref, *rest[n_dil + 1 + N_STAGE:])

        @pl.when(pl.program_id(0) == 0)
        def _():
            swap.start()

        @pl.when(pl.program_id(0) == S // ts - 1)
        def _():
            swap.finish()

        dcat = _dot_nt(dm_ref[...], w_ref[...])
        dp_ref[...] = dcat[:, :POOL_WIDTH]
        do = dcat[:, POOL_WIDTH:]
        for j in range(ATTN_WIDTH // 128):
            stage[j][...] = do[:, j * 128:(j + 1) * 128]
        _store_streams(stage, do_refs, ts)
        prod = do * o_ref[...].astype(f32)
        hi = prod.astype(bf16)
        lo = (prod - hi.astype(f32)).astype(bf16)
        ones = ones_ref[...]
        for j in range(ATTN_WIDTH // 128):
            cols = slice(j * 128, (j + 1) * 128)
            dl_ref[:, cols] = _dot(hi[:, cols], ones) + _dot(lo[:, cols], ones)

    row = lambda w: pl.BlockSpec((ts, w), lambda i: (i, 0))
    res = pl.pallas_call(
        body, name="out_proj_bwd", grid=(S // ts,),
        in_specs=[row(D_MODEL), pl.BlockSpec((D_MODEL, D_MODEL), lambda i: (0, 0)), row(ATTN_WIDTH),
                  pl.BlockSpec((128, 128), lambda i: (0, 0)), ANY],
        out_specs=[row(POOL_WIDTH), row(ATTN_WIDTH)] + [_stream_spec(d, ts) for d in DILATIONS] + [ANY],
        out_shape=[jax.ShapeDtypeStruct((S, POOL_WIDTH), f32), jax.ShapeDtypeStruct((S, ATTN_WIDTH), f32)]
        + [_stream_shape(S, d) for d in DILATIONS] + [_Swap.out_shape(grads)],
        scratch_shapes=_stage_scratch(ts) + _Swap.scratch(grads),
        compiler_params=_params("arbitrary"),
    )(dmix, w_out, attn_out, head_ones, grads)
    return res[0], res[1], res[2:2 + n_dil], res[2 + n_dil]


def _in_proj_bwd(du, dq, dk, dv, cos_t, sin_t, w_in, x, dx2, g1):
    S = x.shape[0]
    ts = 512

    def body(du_ref, dq_ref, dk_ref, dv_ref, cos_ref, sin_ref, w_ref, x_ref, dx2_ref, g_ref, gx_ref, dproj_ref, dg_ref):
        @pl.when(pl.program_id(0) == 0)
        def _():
            dg_ref[...] = jnp.zeros_like(dg_ref)

        dproj_ref[:, :POOL_WIDTH] = du_ref[...]
        cos = cos_ref[...]
        sin = sin_ref[...]
        first = _first_half_mask(ts)
        for j in range(ATTN_WIDTH // 128):
            cols = slice(j * 128, (j + 1) * 128)
            for base, ref in ((POOL_WIDTH, dq_ref), (POOL_WIDTH + ATTN_WIDTH, dk_ref)):
                g = ref[:, cols].astype(f32)
                pre = g * cos + _rope_partner(g * sin, first)
                dproj_ref[:, base + j * 128: base + (j + 1) * 128] = pre.astype(bf16)
        dproj_ref[:, POOL_WIDTH + 2 * ATTN_WIDTH:] = dv_ref[...]

        dh = _dot(dproj_ref[...], w_ref[...])
        xv = x_ref[...]
        r = _rms(xv)
        xhat = xv * r
        dg_ref[...] += jnp.sum(dh * xhat, axis=0, keepdims=True)
        dhg = dh * g_ref[...]
        gx_ref[...] = dx2_ref[...] + r * (dhg - xhat * jnp.mean(dhg * xhat, axis=-1, keepdims=True))

    row = lambda w: pl.BlockSpec((ts, w), lambda i: (i, 0))
    gain = pl.BlockSpec((1, D_MODEL), lambda i: (0, 0))
    return pl.pallas_call(
        body, name="in_proj_bwd", grid=(S // ts,),
        in_specs=[row(POOL_WIDTH)] + [row(ATTN_WIDTH)] * 3 + [row(128), row(128),
                  pl.BlockSpec((IN_WIDTH, D_MODEL), lambda i: (0, 0)), row(D_MODEL), row(D_MODEL), gain],
        out_specs=[row(D_MODEL), row(IN_WIDTH), gain],
        out_shape=[jax.ShapeDtypeStruct((S, D_MODEL), f32), jax.ShapeDtypeStruct((S, IN_WIDTH), bf16),
                   jax.ShapeDtypeStruct((1, D_MODEL), f32)],
        compiler_params=_params("arbitrary"),
    )(du, dq, dk, dv, cos_t, sin_t, w_in, x, dx2, g1)


def _matmul_tiles_tn(a, b, name):
    T, K, w = a.shape
    N = b.shape[1]
    tk = 1024

    def body(a_ref, b_ref, o_ref):
        def tiles_pass(first):
            for t in range(T):
                part = _dot_tn(a_ref[t], b_ref[...])
                if first:
                    o_ref[t * w:(t + 1) * w, :] = part
                else:
                    o_ref[t * w:(t + 1) * w, :] += part

        @pl.when(pl.program_id(0) == 0)
        def _():
            tiles_pass(True)

        @pl.when(pl.program_id(0) > 0)
        def _():
            tiles_pass(False)

    return pl.pallas_call(
        body, name=name, grid=(K // tk,),
        in_specs=[pl.BlockSpec((T, tk, w), lambda k: (0, k, 0)), pl.BlockSpec((tk, N), lambda k: (k, 0))],
        out_specs=pl.BlockSpec((T * w, N), lambda k: (0, 0)),
        out_shape=jax.ShapeDtypeStruct((T * w, N), f32),
        compiler_params=_params("arbitrary"),
    )(a, b)


def _matmul_tn(a, b, name):
    K, M = a.shape
    N = b.shape[1]
    tk = 1024
    w = 256

    def body(a_ref, b_ref, o_ref):
        def tiles_pass(first):
            for t in range(M // w):
                part = _dot_tn(a_ref[:, t * w:(t + 1) * w], b_ref[...])
                if first:
                    o_ref[t * w:(t + 1) * w, :] = part
                else:
                    o_ref[t * w:(t + 1) * w, :] += part

        @pl.when(pl.program_id(0) == 0)
        def _():
            tiles_pass(True)

        @pl.when(pl.program_id(0) > 0)
        def _():
            tiles_pass(False)

    return pl.pallas_call(
        body, name=name, grid=(K // tk,),
        in_specs=[pl.BlockSpec((tk, M), lambda k: (k, 0)), pl.BlockSpec((tk, N), lambda k: (k, 0))],
        out_specs=pl.BlockSpec((M, N), lambda k: (0, 0)),
        out_shape=jax.ShapeDtypeStruct((M, N), f32),
        compiler_params=_params("arbitrary"),
    )(a, b)


def _rope_tables(S):
    half = HEAD_DIM // 2
    freqs = ROPE_THETA ** (-jnp.arange(half, dtype=f32) * (2.0 / HEAD_DIM))
    ang = jnp.arange(S).astype(f32)[:, None] * freqs[None, :]
    cos = jnp.tile(jnp.cos(ang), (1, 4))
    sin = jnp.sin(ang)
    sin = jnp.tile(jnp.concatenate([-sin, sin], axis=1), (1, 2))
    return cos, sin


def _block_diag(w_pool):
    w = jnp.zeros((POOL_WIDTH, POOL_WIDTH), w_pool.dtype)
    for g in range(POOL_WIDTH // POOL_GROUP):
        w = lax.dynamic_update_slice(w, w_pool[g], (g * POOL_GROUP, g * POOL_GROUP))
    return w


def _head_ones():
    head = np.arange(128) // HEAD_DIM
    return jnp.asarray(head[:, None] == head[None, :], dtype=bf16)


def _place():
    x, y, c = lax.axis_index("x"), lax.axis_index("y"), lax.axis_index("c")
    chips = [(1 - x, y), (x, 1 - y), (1 - x, 1 - y)]
    return x, y, c, chips


ANY = pl.BlockSpec(memory_space=pl.ANY)
N_PEER_CHIPS = N_CHIPS - 1
ICI_PIECES = 4
D2D_PIECES = 8
LOCAL_PIECES = 8


def _row_chunks(rows, n, unit=32):
    units = rows // unit
    out, start = [], 0
    for i in range(n):
        size = (units // n + (1 if i < units % n else 0)) * unit
        out.append((start, size))
        start += size
    return [piece for piece in out if piece[1]]


class _LocalCopy:
    def __init__(self, src_rows, dst_rows, rows, buf, sems_in, sems_out):
        self.loads, self.stores = [], []
        for i, (start, size) in enumerate(_row_chunks(rows, LOCAL_PIECES)):
            r = pl.ds(start, size)
            self.loads.append(pltpu.make_async_copy(src_rows(r), buf.at[r], sems_in.at[i]))
            self.stores.append(pltpu.make_async_copy(buf.at[r], dst_rows(r), sems_out.at[i]))

    def start(self):
        for cp in self.loads:
            cp.start()

    def pass_on(self):
        for load, store in zip(self.loads, self.stores):
            load.wait()
            store.start()

    def finish(self):
        for store in self.stores:
            store.wait()

    @staticmethod
    def scratch(rows, dtype):
        return [pltpu.VMEM((rows, D_MODEL), dtype), pltpu.SemaphoreType.DMA((LOCAL_PIECES,)),
                pltpu.SemaphoreType.DMA((LOCAL_PIECES,))]


class _Gather:
    def __init__(self, w_ref, out_ref, send1, recv1, send2, recv2, buf, sems_in, sems_out):
        x, y, c, chips = _place()
        me = 2 * x + y
        rows = w_ref.shape[0]
        half = rows // 2
        pieces = _row_chunks(half, ICI_PIECES)
        self.own = _LocalCopy(lambda r: w_ref.at[r], lambda r: out_ref.at[me, r], rows, buf, sems_in, sems_out)

        def rows_of(core, piece):
            start, size = piece
            return pl.ds(core * half + start, size)

        self.sends, self.arrivals, self.forwards, self.forward_arrivals = [], [], [], []
        for i, piece in enumerate(pieces):
            for j, (cx, cy) in enumerate(chips):
                k = j * len(pieces) + i
                there = 2 * cx + cy

                def direct(src_chip, cx=cx, cy=cy, k=k, piece=piece):
                    return pltpu.make_async_remote_copy(
                        src_ref=w_ref.at[rows_of(c, piece)], dst_ref=out_ref.at[src_chip, rows_of(c, piece)],
                        send_sem=send1.at[k], recv_sem=recv1.at[k], device_id=(cx, cy, c), device_id_type=MESH)

                def passed(core, there=there, k=k, piece=piece):
                    return pltpu.make_async_remote_copy(
                        src_ref=out_ref.at[there, rows_of(core, piece)], dst_ref=out_ref.at[there, rows_of(core, piece)],
                        send_sem=send2.at[k], recv_sem=recv2.at[k], device_id=(x, y, 1 - c), device_id_type=MESH)

                self.sends.append(direct(me))
                self.arrivals.append(direct(there))
                self.forwards.append(passed(c))
                self.forward_arrivals.append(passed(1 - c))

    def start(self):
        for cp in self.sends:
            cp.start()
        self.own.start()

    def pass_on(self):
        self.own.pass_on()
        for arrival, forward in zip(self.arrivals, self.forwards):
            arrival.wait_recv()
            forward.start()

    def finish(self):
        for arrival in self.forward_arrivals:
            arrival.wait_recv()
        for cp in self.sends + self.forwards:
            cp.wait_send()
        self.own.finish()

    @staticmethod
    def scratch(rows, dtype):
        n = N_PEER_CHIPS * len(_row_chunks(rows // 2, ICI_PIECES))
        return [pltpu.SemaphoreType.DMA((n,))] * 4 + _LocalCopy.scratch(rows, dtype)

    @staticmethod
    def out_shape(rows, dtype):
        return jax.ShapeDtypeStruct((N_CHIPS, rows, D_MODEL), dtype)


def _gather_weights(pack):
    rows = pack.shape[0]

    def body(w_ref, out_ref, *scratch):
        gather = _Gather(w_ref, out_ref, *scratch)
        gather.start()
        gather.pass_on()
        gather.finish()

    return pl.pallas_call(
        body, name="gather_weights", in_specs=[ANY], out_specs=ANY, out_shape=_Gather.out_shape(rows, pack.dtype),
        scratch_shapes=_Gather.scratch(rows, pack.dtype),
        compiler_params=pltpu.CompilerParams(vmem_limit_bytes=VMEM_LIMIT_V7X),
    )(pack)


class _Scatter:
    def __init__(self, h_ref, out_ref, send, recv):
        x, y, c, chips = _place()
        pieces = _row_chunks(h_ref.shape[1], ICI_PIECES)
        self.copies = []
        for i, (start, size) in enumerate(pieces):
            for j, (cx, cy) in enumerate(chips):
                k = j * len(pieces) + i
                self.copies.append(pltpu.make_async_remote_copy(
                    src_ref=h_ref.at[2 * cx + cy, pl.ds(start, size)], dst_ref=out_ref.at[j, pl.ds(start, size)],
                    send_sem=send.at[k], recv_sem=recv.at[k], device_id=(cx, cy, c), device_id_type=MESH))

    def start(self):
        for cp in self.copies:
            cp.start()

    def finish(self):
        for cp in self.copies:
            cp.wait_recv()
        for cp in self.copies:
            cp.wait_send()

    @staticmethod
    def scratch(half):
        n = N_PEER_CHIPS * len(_row_chunks(half, ICI_PIECES))
        return [pltpu.SemaphoreType.DMA((n,))] * 2

    @staticmethod
    def out_shape(half, dtype):
        return jax.ShapeDtypeStruct((N_PEER_CHIPS, half, D_MODEL), dtype)


def _scatter_to_chips(h):
    half = h.shape[1]

    def body(h_ref, out_ref, send, recv):
        scatter = _Scatter(h_ref, out_ref, send, recv)
        scatter.start()
        scatter.finish()

    return pl.pallas_call(
        body, name="scatter_to_chips", in_specs=[ANY], out_specs=ANY, out_shape=_Scatter.out_shape(half, h.dtype),
        scratch_shapes=_Scatter.scratch(half),
    )(h)


class _Swap:
    def __init__(self, g_ref, theirs_ref, send, recv):
        x, y, c, _ = _place()
        half = g_ref.shape[1] // 2
        pieces = _row_chunks(half, D2D_PIECES)
        self.copies = []
        for s in range(N_CHIPS):
            for i, (start, size) in enumerate(pieces):
                k = s * len(pieces) + i
                self.copies.append(pltpu.make_async_remote_copy(
                    src_ref=g_ref.at[s, pl.ds((1 - c) * half + start, size)], dst_ref=theirs_ref.at[s, pl.ds(start, size)],
                    send_sem=send.at[k], recv_sem=recv.at[k], device_id=(x, y, 1 - c), device_id_type=MESH))

    def start(self):
        for cp in self.copies:
            cp.start()

    def finish(self):
        for cp in self.copies:
            cp.wait()

    @staticmethod
    def scratch(g):
        n = N_CHIPS * len(_row_chunks(g.shape[1] // 2, D2D_PIECES))
        return [pltpu.SemaphoreType.DMA((n,))] * 2

    @staticmethod
    def out_shape(g):
        return jax.ShapeDtypeStruct((N_CHIPS, g.shape[1] // 2, D_MODEL), g.dtype)


def _swap_halves(g):
    def body(g_ref, theirs_ref, send, recv):
        swap = _Swap(g_ref, theirs_ref, send, recv)
        swap.start()
        swap.finish()

    return pl.pallas_call(
        body, name="swap_halves", in_specs=[ANY], out_specs=ANY, out_shape=_Swap.out_shape(g),
        scratch_shapes=_Swap.scratch(g),
    )(g)


ADD_TILE_MAX_ROWS = 600


def _add_tile(half):
    return max(t for t in range(8, ADD_TILE_MAX_ROWS + 1, 8) if half % t == 0)


def _add_cores(g, theirs, name, out_dtype=f32):
    half = theirs.shape[1]
    tr = _add_tile(half)
    n_t = half // tr

    def body(c_ref, g_ref, t_ref, o_ref):
        o_ref[...] = (g_ref[...] + t_ref[...]).astype(out_dtype)

    blk = pl.BlockSpec((1, tr, D_MODEL), lambda s, t, c_ref: (s, t, 0))
    return pl.pallas_call(
        body, name=name,
        grid_spec=pltpu.PrefetchScalarGridSpec(
            num_scalar_prefetch=1, grid=(N_CHIPS, n_t),
            in_specs=[pl.BlockSpec((1, tr, D_MODEL), lambda s, t, c_ref: (s, c_ref[0] * n_t + t, 0)), blk],
            out_specs=blk),
        out_shape=jax.ShapeDtypeStruct(theirs.shape, out_dtype),
        compiler_params=_params("parallel", "parallel"),
    )(lax.axis_index("c").astype(jnp.int32).reshape(1), g, theirs)


def _add_chips(chip_sum, others, name):
    half = chip_sum.shape[1]
    tr = _add_tile(half)

    def body(me_ref, own_ref, o0, o1, o2, out_ref):
        out_ref[...] = ((own_ref[0].astype(f32) + o0[0].astype(f32)) + o1[0].astype(f32)) + o2[0].astype(f32)

    other = lambda j: pl.BlockSpec((1, tr, D_MODEL), lambda t, me_ref: (j, t, 0))
    return pl.pallas_call(
        body, name=name,
        grid_spec=pltpu.PrefetchScalarGridSpec(
            num_scalar_prefetch=1, grid=(half // tr,),
            in_specs=[pl.BlockSpec((1, tr, D_MODEL), lambda t, me_ref: (me_ref[0], t, 0)), other(0), other(1), other(2)],
            out_specs=pl.BlockSpec((tr, D_MODEL), lambda t, me_ref: (t, 0))),
        out_shape=jax.ShapeDtypeStruct((half, D_MODEL), f32),
        compiler_params=_params("parallel"),
    )((2 * lax.axis_index("x") + lax.axis_index("y")).astype(jnp.int32).reshape(1), chip_sum, others, others, others)


def _join_halves(r):
    half = r.shape[0]
    pieces = _row_chunks(half, 2 * D2D_PIECES)
    n = len(pieces)

    def body(r_ref, out_ref, send, recv, buf, sems_in, sems_out):
        x, y, c, _ = _place()
        own = _LocalCopy(lambda rr: r_ref.at[rr], lambda rr: out_ref.at[c, rr], half, buf, sems_in, sems_out)
        own.start()

        def piece(i, core):
            start, size = pieces[i]
            return pltpu.make_async_remote_copy(
                src_ref=r_ref.at[pl.ds(start, size)], dst_ref=out_ref.at[core, pl.ds(start, size)],
                send_sem=send.at[i], recv_sem=recv.at[i], device_id=(x, y, 1 - c), device_id_type=MESH)

        copies = [piece(i, c) for i in range(n)]
        for cp in copies:
            cp.start()
        own.pass_on()
        for i in range(n):
            piece(i, 1 - c).wait_recv()
        for cp in copies:
            cp.wait_send()
        own.finish()

    return pl.pallas_call(
        body, name="join_halves", in_specs=[ANY], out_specs=ANY,
        out_shape=jax.ShapeDtypeStruct((2,) + r.shape, r.dtype),
        scratch_shapes=[pltpu.SemaphoreType.DMA((n,))] * 2 + _LocalCopy.scratch(half, r.dtype),
        compiler_params=pltpu.CompilerParams(vmem_limit_bytes=VMEM_LIMIT_V7X),
    )(r)


def _sum_small(block):
    def body(b_ref, out_ref, gathered, send, recv):
        x, y, c, _ = _place()
        me = 4 * x + 2 * y + c
        gathered[me] = b_ref[...]
        sends = []
        for kk in range(1, N_DEV):
            flip = lambda v, bit: 1 - v if bit else v
            peer = (flip(x, kk & 4), flip(y, kk & 2), flip(c, kk & 1))
            cp = pltpu.make_async_remote_copy(
                src_ref=b_ref, dst_ref=gathered.at[me], send_sem=send.at[kk - 1], recv_sem=recv.at[kk - 1],
                device_id=peer, device_id_type=MESH)
            cp.start()
            sends.append(cp)
        for kk in range(1, N_DEV):
            peer_index = jnp.bitwise_xor(me, kk)
            pltpu.make_async_remote_copy(
                src_ref=b_ref, dst_ref=gathered.at[peer_index], send_sem=send.at[kk - 1], recv_sem=recv.at[kk - 1],
                device_id=(x, y, c), device_id_type=MESH).wait_recv()
        for cp in sends:
            cp.wait_send()
        acc = gathered[0]
        for dev in range(1, N_DEV):
            acc = acc + gathered[dev]
        out_ref[...] = acc

    vmem = pl.BlockSpec(memory_space=pltpu.VMEM)
    return pl.pallas_call(
        body, name="sum_small", in_specs=[vmem], out_specs=vmem,
        out_shape=jax.ShapeDtypeStruct(block.shape, block.dtype),
        scratch_shapes=[pltpu.VMEM((N_DEV,) + block.shape, block.dtype),
                        pltpu.SemaphoreType.DMA((N_DEV - 1,)), pltpu.SemaphoreType.DMA((N_DEV - 1,))],
    )(block)


def _adamw(w, g, m, v, name):
    rows, cols = w.shape
    tr = rows
    for cand in (512, 256, 128, 64, 32, 16, 8):
        if rows % cand == 0:
            tr = cand
            break
    c1 = 1.0 - ADAM_B1 ** ADAM_STEP
    c2 = 1.0 - ADAM_B2 ** ADAM_STEP

    def body(w_ref, g_ref, m_ref, v_ref, d_ref, nm_ref, nv_ref):
        gv = g_ref[...]
        nm = ADAM_B1 * m_ref[...] + (1.0 - ADAM_B1) * gv
        nv = ADAM_B2 * v_ref[...] + (1.0 - ADAM_B2) * (gv * gv)
        nm_ref[...] = nm
        nv_ref[...] = nv
        d_ref[...] = -ADAM_LR * ((nm / c1) / (jnp.sqrt(nv / c2) + ADAM_EPS) + ADAM_WD * w_ref[...])

    blk = pl.BlockSpec((tr, cols), lambda i: (i, 0))
    shape = jax.ShapeDtypeStruct((rows, cols), f32)
    return pl.pallas_call(
        body, name=name, grid=(rows // tr,), in_specs=[blk] * 4, out_specs=[blk] * 3, out_shape=[shape] * 3,
        compiler_params=_params("parallel"),
    )(w, g, m, v)


LARGE = ("w_in", "w_out", "w_gate", "w_up", "w_down")
SMALL = ("ln_pre_mix", "ln_post_mix", "ln_pre_ffn", "ln_post_ffn", "pool_scale", "w_pool")
SHARD_ROWS = {"w_in": 640, "w_out": 256, "w_gate": 704, "w_up": 704, "w_down": 704}
COLUMN_SHARDED = ("w_in", "w_gate", "w_up")
NEEDED_FIRST = ("w_in",)
NEEDED_LATER = ("w_out", "w_gate", "w_up", "w_down")
READY_EARLY = ("w_out", "w_gate", "w_up", "w_down")
READY_LATE = ("w_in",)


def _pack_shard(shards, names):
    return jnp.concatenate([shards[n].T if n in COLUMN_SHARDED else shards[n] for n in names], axis=0)


def _unpack_shard(pack, names):
    out, row = {}, 0
    for n in names:
        part = pack[row:row + SHARD_ROWS[n]]
        out[n] = part.T if n in COLUMN_SHARDED else part
        row += SHARD_ROWS[n]
    return out


def _whole_from_shards(packs, names):
    out, row = {}, 0
    for n in names:
        rows = SHARD_ROWS[n]
        out[n] = packs[:, row:row + rows].reshape(N_CHIPS * rows, D_MODEL)
        row += rows
    return out


def _shards_from_whole(grads, names):
    return jnp.concatenate([grads[n].reshape(N_CHIPS, SHARD_ROWS[n], D_MODEL) for n in names], axis=1)


def _pack_small(vals):
    rows = [vals[n].reshape(1, D_MODEL) for n in SMALL[:4]]
    rows.append(jnp.pad(vals["pool_scale"].reshape(1, POOL_WIDTH), ((0, 0), (0, D_MODEL - POOL_WIDTH))))
    rows.append(jnp.pad(vals["loss"].reshape(1, 1), ((0, 0), (0, D_MODEL - 1))))
    rows.append(jnp.zeros((2, D_MODEL), f32))
    rows.append(vals["w_pool"].reshape(16, D_MODEL))
    return jnp.concatenate(rows, axis=0)


def _unpack_small(block):
    out = {n: block[i:i + 1] for i, n in enumerate(SMALL[:4])}
    out["pool_scale"] = block[4:5, :POOL_WIDTH]
    out["loss"] = block[5, 0]
    out["w_pool"] = block[8:24].reshape(1, 4, POOL_GROUP, POOL_GROUP)
    return out


def kernel(x, ln_pre_mix, w_in, w_pool, pool_scale, w_out, ln_post_mix, ln_pre_ffn, w_gate, w_up, w_down, ln_post_ffn, loss_target, m_ln_pre_mix, m_w_in, m_w_pool, m_pool_scale, m_w_out, m_ln_post_mix, m_ln_pre_ffn, m_w_gate, m_w_up, m_w_down, m_ln_post_ffn, v_ln_pre_mix, v_w_in, v_w_pool, v_pool_scale, v_w_out, v_ln_post_mix, v_ln_pre_ffn, v_w_gate, v_w_up, v_w_down, v_ln_post_ffn):
    w = dict(ln_pre_mix=ln_pre_mix, w_in=w_in, w_pool=w_pool, pool_scale=pool_scale, w_out=w_out,
             ln_post_mix=ln_post_mix, ln_pre_ffn=ln_pre_ffn, w_gate=w_gate, w_up=w_up, w_down=w_down,
             ln_post_ffn=ln_post_ffn)
    m = dict(ln_pre_mix=m_ln_pre_mix, w_in=m_w_in, w_pool=m_w_pool, pool_scale=m_pool_scale, w_out=m_w_out,
             ln_post_mix=m_ln_post_mix, ln_pre_ffn=m_ln_pre_ffn, w_gate=m_w_gate, w_up=m_w_up, w_down=m_w_down,
             ln_post_ffn=m_ln_post_ffn)
    v = dict(ln_pre_mix=v_ln_pre_mix, w_in=v_w_in, w_pool=v_w_pool, pool_scale=v_pool_scale, w_out=v_w_out,
             ln_post_mix=v_ln_post_mix, ln_pre_ffn=v_ln_pre_ffn, w_gate=v_w_gate, w_up=v_w_up, w_down=v_w_down,
             ln_post_ffn=v_ln_post_ffn)

    xs, target = x[0], loss_target[0]
    cos_t, sin_t = _rope_tables(xs.shape[0])
    w_bd = _block_diag(w_pool[0]).astype(bf16)
    shard = {n: w[n][0].astype(bf16) for n in LARGE}

    w_in_whole = _whole_from_shards(_gather_weights(_pack_shard(shard, NEEDED_FIRST)), NEEDED_FIRST)["w_in"]
    h1, u, qs, ks, vs = _in_proj(xs, ln_pre_mix, w_in_whole, cos_t, sin_t)
    pool_out = _pool_fwd(u, w_bd, pool_scale)
    attn_out, lse, later = _attn_fwd(qs, ks, vs, _pack_shard(shard, NEEDED_LATER))
    whole = _whole_from_shards(later, NEEDED_LATER)
    mix, x2, h2 = _out_proj(pool_out, attn_out, whole["w_out"], xs, ln_post_mix, ln_pre_ffn)
    gate, up, f = _ffn_fwd(h2, whole["w_gate"], whole["w_up"], whole["w_down"])
    dy, df, dg4, loss = _loss_head(f, x2, target, ln_post_ffn)

    large = {}
    a, dgate, dup, dh2 = _ffn_bwd(df, gate, up, whole["w_gate"], whole["w_up"], whole["w_down"])
    large["w_down"] = _matmul_tiles_tn(a, df, "grad_w_down")
    large["w_gate"] = _matmul_tiles_tn(dgate, h2, "grad_w_gate")
    large["w_up"] = _matmul_tiles_tn(dup, h2, "grad_w_up")
    dx2, dmix, dg3, dg2 = _norm_bwd(dh2, dy, x2, mix, ln_pre_ffn, ln_post_mix)
    large["w_out"] = jnp.concatenate([_matmul_tn(pool_out, dmix, "grad_w_out_pool"),
                                      _matmul_tn(attn_out, dmix, "grad_w_out_attn")], axis=0)
    early = _shards_from_whole(large, READY_EARLY)
    dpool, delta, dos, early_theirs = _out_proj_bwd(dmix, whole["w_out"], attn_out, _head_ones(), early)
    early_chip = _add_cores(early, early_theirs, "add_cores_early")
    du, d_w_bd, d_scale = _pool_bwd(u, dpool, w_bd, pool_scale)
    dq, dk, dv, early_others = _attn_bwd(qs, ks, vs, dos, lse, delta, early_chip)
    grad_x, dproj, dg1 = _in_proj_bwd(du, dq, dk, dv, cos_t, sin_t, w_in_whole, xs, dx2, ln_pre_mix)
    large["w_in"] = _matmul_tn(dproj, h1, "grad_w_in")
    late = _shards_from_whole(large, READY_LATE)
    late_chip = _add_cores(late, _swap_halves(late), "add_cores_late", bf16)
    late_others = _scatter_to_chips(late_chip)
    early_half = _add_chips(early_chip, early_others, "add_chips_early")
    late_half = _add_chips(late_chip, late_others, "add_chips_late")
    joined = _join_halves(jnp.concatenate([early_half, late_half], axis=0))
    n_early = early_half.shape[0]
    grads = _unpack_shard(joined[:, :n_early].reshape(-1, D_MODEL), READY_EARLY)
    grads.update(_unpack_shard(joined[:, n_early:].reshape(-1, D_MODEL), READY_LATE))

    d_w_pool = jnp.stack([d_w_bd[g * POOL_GROUP:(g + 1) * POOL_GROUP, g * POOL_GROUP:(g + 1) * POOL_GROUP]
                          for g in range(POOL_WIDTH // POOL_GROUP)])
    small = dict(ln_pre_mix=dg1, ln_post_mix=dg2, ln_pre_ffn=dg3, ln_post_ffn=dg4, pool_scale=d_scale, w_pool=d_w_pool)
    total = _unpack_small(_sum_small(_pack_small(dict(small, loss=loss))))
    for n in SMALL:
        grads[n] = total[n]

    delta_w, new_m, new_v = {}, {}, {}
    for n in LARGE:
        delta_w[n], new_m[n], new_v[n] = _adamw(w[n][0], grads[n], m[n][0], v[n][0], "adamw_" + n)
    small_state = [_pack_small(dict({n: s[n] for n in SMALL}, loss=jnp.zeros((), f32))) for s in (w, m, v)]
    small_grad = _pack_small(dict({n: grads[n] for n in SMALL}, loss=jnp.zeros((), f32)))
    sd, sm, sv = _adamw(small_state[0], small_grad, small_state[1], small_state[2], "adamw_small")
    for out, block in ((delta_w, sd), (new_m, sm), (new_v, sv)):
        un = _unpack_small(block)
        for n in SMALL:
            out[n] = un[n]

    names = ("ln_pre_mix", "w_in", "w_pool", "pool_scale", "w_out", "ln_post_mix", "ln_pre_ffn", "w_gate", "w_up",
             "w_down", "ln_post_ffn")
    full = lambda d: [d[n].reshape(w[n].shape) for n in names]
    return (total["loss"], grad_x[None], *full(grads), *full(delta_w), *full(new_m), *full(new_v))
```

```python
import numpy as np
import jax
import jax.numpy as jnp
from jax import lax
from jax.experimental import pallas as pl
from jax.experimental.pallas import tpu as pltpu

D_MODEL = 1024
POOL_WIDTH = 256
POOL_GROUP = 64
ATTN_WIDTH = 768
HEAD_DIM = 64
IN_WIDTH = 2560
D_FF = 2816
BLOCK = 128
DILATIONS = (1, 4, 16)
ROPE_THETA = 10000.0
EPS = 1e-6
ATTN_SCALE = 0.125
NEG = -1e30

ADAM_LR = 0.001
ADAM_B1 = 0.9
ADAM_B2 = 0.999
ADAM_EPS = 1e-08
ADAM_WD = 0.01
ADAM_STEP = 10

N_CHIPS = 4
N_DEV = 8
VMEM_LIMIT_V7X = 56 * 1024 * 1024
MESH = pl.DeviceIdType.MESH

f32 = jnp.float32
bf16 = jnp.bfloat16


def _params(*sem):
    return pltpu.CompilerParams(dimension_semantics=sem, vmem_limit_bytes=VMEM_LIMIT_V7X)


def _dot(a, b):
    return jnp.dot(a, b, preferred_element_type=f32)


def _dot_nt(a, b):
    return lax.dot_general(a, b, (((1,), (1,)), ((), ())), preferred_element_type=f32)


def _dot_tn(a, b):
    return lax.dot_general(a, b, (((0,), (0,)), ((), ())), preferred_element_type=f32)


def _rope_partner(a, first_half):
    return jnp.where(first_half, pltpu.roll(a, 96, 1), pltpu.roll(a, 32, 1))


def _first_half_mask(rows):
    lane = lax.broadcasted_iota(jnp.int32, (rows, 128), 1)
    return (lane % HEAD_DIM) < (HEAD_DIM // 2)


def _stream_spec(d, ts):
    return pl.BlockSpec((d, ts // d, ATTN_WIDTH), lambda i: (0, i, 0))


def _stream_shape(S, d):
    return jax.ShapeDtypeStruct((d, S // d, ATTN_WIDTH), bf16)


N_STAGE = ATTN_WIDTH // 128


def _stage_scratch(ts):
    return [pltpu.VMEM((ts, 128), f32)] * N_STAGE


def _store_streams(stage, out_refs, ts):
    for d, ref in zip(DILATIONS, out_refs):
        for r in range(d):
            rows = pl.ds(0, ts) if d == 1 else pl.ds(r, ts // d, stride=d)
            for j in range(N_STAGE):
                ref[r, :, j * 128:(j + 1) * 128] = stage[j][rows, :].astype(bf16)


def _in_proj(x, g1, w_in, cos_t, sin_t):
    S = x.shape[0]
    ts = 512

    def body(x_ref, g_ref, w_ref, cos_ref, sin_ref, h_ref, u_ref, *rest):
        outs, stage = rest[:-N_STAGE], rest[-N_STAGE:]
        xv = x_ref[...]
        r = lax.rsqrt(jnp.mean(xv * xv, axis=-1, keepdims=True) + EPS)
        h = ((xv * r) * g_ref[...]).astype(bf16)
        h_ref[...] = h
        proj = _dot_nt(h, w_ref[...])
        u_ref[...] = proj[:, :POOL_WIDTH]
        cos = cos_ref[...]
        sin = sin_ref[...]
        first = _first_half_mask(ts)
        n_dil = len(DILATIONS)
        for which, base in enumerate((POOL_WIDTH, POOL_WIDTH + ATTN_WIDTH)):
            for j in range(ATTN_WIDTH // 128):
                a = proj[:, base + j * 128: base + (j + 1) * 128]
                if which == 0:
                    a = a * ATTN_SCALE
                stage[j][...] = a * cos + _rope_partner(a, first) * sin
            _store_streams(stage, outs[which * n_dil:(which + 1) * n_dil], ts)
        for j in range(ATTN_WIDTH // 128):
            base = POOL_WIDTH + 2 * ATTN_WIDTH + j * 128
            stage[j][...] = proj[:, base:base + 128]
        _store_streams(stage, outs[2 * n_dil:], ts)

    row = lambda w: pl.BlockSpec((ts, w), lambda i: (i, 0))
    streams = [_stream_spec(d, ts) for d in DILATIONS] * 3
    res = pl.pallas_call(
        body, name="in_proj", grid=(S // ts,),
        in_specs=[row(D_MODEL), pl.BlockSpec((1, D_MODEL), lambda i: (0, 0)),
                  pl.BlockSpec((IN_WIDTH, D_MODEL), lambda i: (0, 0)), row(128), row(128)],
        out_specs=[row(D_MODEL), row(POOL_WIDTH)] + streams,
        out_shape=[jax.ShapeDtypeStruct((S, D_MODEL), bf16), jax.ShapeDtypeStruct((S, POOL_WIDTH), f32)]
        + [_stream_shape(S, d) for d in DILATIONS] * 3,
        scratch_shapes=_stage_scratch(ts),
        compiler_params=_params("parallel"),
    )(x, g1, w_in, cos_t, sin_t)
    n = len(DILATIONS)
    return res[0], res[1], res[2:2 + n], res[2 + n:2 + 2 * n], res[2 + 2 * n:]


POOL_HALO = 16


def _pool_lane_group(rows):
    return lax.broadcasted_iota(jnp.int32, (rows, POOL_WIDTH), 1) // POOL_GROUP


def _pool_select(group, s2, s4, s8, s16):
    return jnp.where(group == 0, s2, jnp.where(group == 1, s4, jnp.where(group == 2, s8, s16)))


def _pool_count(t0, rows):
    group = _pool_lane_group(rows)
    t = t0 + lax.broadcasted_iota(jnp.int32, (rows, POOL_WIDTH), 0)
    win = _pool_select(group, 2, 4, 8, 16)
    return jnp.minimum(t + 1, win).astype(f32)


def _pool_diff(u_halo, u_tile, t0):
    ts = u_tile.shape[0]
    ext = jnp.concatenate([u_halo, u_tile], axis=0)
    s2 = ext + pltpu.roll(ext, 1, 0)
    s4 = s2 + pltpu.roll(s2, 2, 0)
    s8 = s4 + pltpu.roll(s4, 4, 0)
    s16 = s8 + pltpu.roll(s8, 8, 0)
    group = _pool_lane_group(ts + POOL_HALO)
    wsum = _pool_select(group, s2, s4, s8, s16)[POOL_HALO:]
    return wsum / _pool_count(t0, ts) - u_tile


def _pool_specs(ts, n_tiles):
    tile = pl.BlockSpec((ts, POOL_WIDTH), lambda i: (i, 0))
    per = ts // POOL_HALO
    before = pl.BlockSpec((POOL_HALO, POOL_WIDTH), lambda i: (jnp.maximum(i * per - 1, 0), 0))
    after = pl.BlockSpec((POOL_HALO, POOL_WIDTH), lambda i: (jnp.minimum((i + 1) * per, n_tiles * per - 1), 0))
    return tile, before, after


def _pool_fwd(u, w_bd, scale):
    S = u.shape[0]
    ts = 512
    n_tiles = S // ts

    def body(u_ref, halo_ref, w_ref, sc_ref, y_ref):
        i = pl.program_id(0)
        halo = jnp.where(i > 0, halo_ref[...], 0.0)
        d = _pool_diff(halo, u_ref[...], i * ts)
        y_ref[...] = (_dot(d.astype(bf16), w_ref[...]) * sc_ref[...]).astype(bf16)

    tile, before, _ = _pool_specs(ts, n_tiles)
    return pl.pallas_call(
        body, name="pool_fwd", grid=(n_tiles,),
        in_specs=[tile, before, pl.BlockSpec((POOL_WIDTH, POOL_WIDTH), lambda i: (0, 0)),
                  pl.BlockSpec((1, POOL_WIDTH), lambda i: (0, 0))],
        out_specs=tile, out_shape=jax.ShapeDtypeStruct((S, POOL_WIDTH), bf16),
        compiler_params=_params("parallel"),
    )(u, u, w_bd, scale)


def _pool_bwd(u, dy, w_bd, scale):
    S = u.shape[0]
    ts = 512
    n_tiles = S // ts

    def body(u_ref, halo_ref, dy_ref, dy_next_ref, w_ref, sc_ref, du_ref, dw_ref, dsc_ref):
        i = pl.program_id(0)

        @pl.when(i == 0)
        def _():
            dw_ref[...] = jnp.zeros_like(dw_ref)
            dsc_ref[...] = jnp.zeros_like(dsc_ref)

        halo = jnp.where(i > 0, halo_ref[...], 0.0)
        d = _pool_diff(halo, u_ref[...], i * ts).astype(bf16)
        w = w_ref[...]
        sc = sc_ref[...]
        dy_tile = dy_ref[...]
        z = _dot(d, w)
        dsc_ref[...] += jnp.sum(dy_tile * z, axis=0, keepdims=True)
        dy_next = jnp.where(i < n_tiles - 1, dy_next_ref[...], 0.0)
        dz = (jnp.concatenate([dy_tile, dy_next], axis=0) * sc).astype(bf16)
        dw_ref[...] += _dot_tn(d, dz[:ts])
        dd = _dot_nt(dz, w)
        e = dd / _pool_count(i * ts, ts + POOL_HALO)
        n = ts + POOL_HALO
        f2 = e + pltpu.roll(e, n - 1, 0)
        f4 = f2 + pltpu.roll(f2, n - 2, 0)
        f8 = f4 + pltpu.roll(f4, n - 4, 0)
        f16 = f8 + pltpu.roll(f8, n - 8, 0)
        fsum = _pool_select(_pool_lane_group(n), f2, f4, f8, f16)
        du_ref[...] = (fsum[:ts] - dd[:ts]).astype(bf16)

    tile, before, after = _pool_specs(ts, n_tiles)
    return pl.pallas_call(
        body, name="pool_bwd", grid=(n_tiles,),
        in_specs=[tile, before, tile, after, pl.BlockSpec((POOL_WIDTH, POOL_WIDTH), lambda i: (0, 0)),
                  pl.BlockSpec((1, POOL_WIDTH), lambda i: (0, 0))],
        out_specs=[tile, pl.BlockSpec((POOL_WIDTH, POOL_WIDTH), lambda i: (0, 0)),
                   pl.BlockSpec((1, POOL_WIDTH), lambda i: (0, 0))],
        out_shape=[jax.ShapeDtypeStruct((S, POOL_WIDTH), bf16), jax.ShapeDtypeStruct((POOL_WIDTH, POOL_WIDTH), f32),
                   jax.ShapeDtypeStruct((1, POOL_WIDTH), f32)],
        compiler_params=_params("arbitrary"),
    )(u, u, dy, dy, w_bd, scale)


SUPER = BLOCK * DILATIONS[-1]
UNITS = SUPER // BLOCK
FWD_UNROLL = 16
BWD_UNROLL = 8


def _band_mask(has_prev):
    qi = lax.broadcasted_iota(jnp.int32, (BLOCK, 2 * BLOCK), 0)
    kj = lax.broadcasted_iota(jnp.int32, (BLOCK, 2 * BLOCK), 1)
    return (kj >= qi) & (kj <= qi + BLOCK) & ((kj >= BLOCK) | has_prev)


def _head0_mask(rows=BLOCK):
    return lax.broadcasted_iota(jnp.int32, (rows, 128), 1) < HEAD_DIM


def _band_mask_t(has_prev):
    ki = lax.broadcasted_iota(jnp.int32, (2 * BLOCK, 2 * BLOCK), 0)
    qj = lax.broadcasted_iota(jnp.int32, (2 * BLOCK, 2 * BLOCK), 1) % BLOCK
    return (ki >= qj) & (ki <= qj + BLOCK) & ((ki >= BLOCK) | has_prev)


def _head_pair_rows(a, h0):
    zero = jnp.zeros_like(a)
    return jnp.concatenate([jnp.where(h0, a, zero), jnp.where(h0, zero, a)], axis=0)


def _per_query_row(stat):
    t = stat.T
    return jnp.concatenate([jnp.concatenate([t[:HEAD_DIM]] * 4, axis=0), jnp.concatenate([t[HEAD_DIM:]] * 4, axis=0)],
                           axis=1)


def _natural_rows(d, r, n):
    if d == 1:
        return pl.ds(pl.multiple_of(n * BLOCK, BLOCK), BLOCK)
    return pl.ds(n * (BLOCK * d) + r, BLOCK, stride=d)


def _unit_place(d, u):
    per_stream = UNITS // d
    return u // per_stream, u % per_stream, per_stream


def _block_rows(n):
    return pl.ds(pl.multiple_of(n * BLOCK, BLOCK), BLOCK)


def _band(cur_ref, tail_ref, r, n):
    before = jnp.where(n > 0, cur_ref[r, _block_rows(jnp.maximum(n - 1, 0)), :], tail_ref[r])
    return jnp.concatenate([before, cur_ref[r, _block_rows(n), :]], axis=0)


def _attn_in_specs(S, with_do):
    specs = []
    last = S // SUPER - 1
    for d in DILATIONS:
        per_stream = UNITS // d
        cur = pl.BlockSpec((d, SUPER // d, 128), lambda hp, sb: (0, jnp.minimum(sb, last), hp))
        tail = pl.BlockSpec(
            (d, BLOCK, 128),
            lambda hp, sb, per_stream=per_stream: (0, jnp.maximum(jnp.minimum(sb, last) * per_stream - 1, 0), hp))
        specs += [cur] * (2 if with_do else 1) + [cur, tail, cur, tail]
    return specs


def _attn_fwd(qs, ks, vs, pack):
    S = qs[0].shape[1]
    n_dil = len(DILATIONS)
    n_steps = S // SUPER
    n_total = (ATTN_WIDTH // 128) * n_steps

    def body(*refs):
        ins, pack_ref = refs[:5 * n_dil], refs[5 * n_dil]
        out_ref, lse_ref, gathered_ref = refs[5 * n_dil + 1:5 * n_dil + 4]
        scratch = refs[5 * n_dil + 4:]
        o_sc, l_sc = scratch[:n_dil], scratch[n_dil:2 * n_dil]
        gather = _Gather(pack_ref, gathered_ref, *scratch[2 * n_dil:])
        sb = pl.program_id(1)
        step = pl.program_id(0) * n_steps + sb

        @pl.when(step == 0)
        def _():
            gather.start()

        h0 = _head0_mask()
        for ci, d in enumerate(DILATIONS):
            q_ref, kc_ref, kp_ref, vc_ref, vp_ref = ins[5 * ci:5 * ci + 5]

            def unit(u, carry, d=d, ci=ci, q_ref=q_ref, kc_ref=kc_ref, kp_ref=kp_ref, vc_ref=vc_ref, vp_ref=vp_ref):
                r, n, _ = _unit_place(d, u)
                qv = q_ref[r, _block_rows(n), :]
                kb = _band(kc_ref, kp_ref, r, n)
                vb = _band(vc_ref, vp_ref, r, n)
                valid = _band_mask((sb > 0) | (n > 0))
                outs, lses = [], []
                for h in range(2):
                    keep = h0 if h == 0 else jnp.logical_not(h0)
                    qh = jnp.where(keep, qv, jnp.zeros_like(qv))
                    s = jnp.where(valid, _dot_nt(qh, kb), NEG)
                    m = jnp.max(s, axis=1, keepdims=True)
                    e = jnp.exp(s - m)
                    den = jnp.sum(e, axis=1, keepdims=True)
                    outs.append(_dot(e.astype(bf16), vb) * (1.0 / den))
                    lses.append(jnp.broadcast_to(m + jnp.log(den), (BLOCK, 128)))
                rows = _natural_rows(d, r, n)
                o_sc[ci][rows, :] = jnp.where(h0, outs[0], outs[1])
                l_sc[ci][rows, :] = jnp.where(h0, lses[0], lses[1])
                return carry

            lax.fori_loop(0, UNITS, unit, 0, unroll=FWD_UNROLL)

        def merge(t, carry):
            rows = pl.ds(pl.multiple_of(t * 256, 256), 256)
            a, b, c = l_sc[0][rows, :], l_sc[1][rows, :], l_sc[2][rows, :]
            m = jnp.maximum(jnp.maximum(a, b), c)
            ea, eb, ec = jnp.exp(a - m), jnp.exp(b - m), jnp.exp(c - m)
            tot = ea + eb + ec
            out_ref[rows, :] = ((ea / tot) * o_sc[0][rows, :] + (eb / tot) * o_sc[1][rows, :]
                                + (ec / tot) * o_sc[2][rows, :]).astype(bf16)
            lse_ref[rows, :] = m + jnp.log(tot)
            return carry

        lax.fori_loop(0, SUPER // 256, merge, 0)

        @pl.when(step == (2 * n_total) // 3)
        def _():
            gather.pass_on()

        @pl.when(step == n_total - 1)
        def _():
            gather.finish()

    args = []
    for q, k, v in zip(qs, ks, vs):
        args += [q, k, k, v, v]
    nat = pl.BlockSpec((SUPER, 128), lambda hp, sb: (sb, hp))
    rows = pack.shape[0]
    return pl.pallas_call(
        body, name="attn_fwd", grid=(ATTN_WIDTH // 128, n_steps),
        in_specs=_attn_in_specs(S, False) + [ANY], out_specs=[nat, nat, ANY],
        out_shape=[jax.ShapeDtypeStruct((S, ATTN_WIDTH), bf16), jax.ShapeDtypeStruct((S, ATTN_WIDTH), f32),
                   _Gather.out_shape(rows, pack.dtype)],
        scratch_shapes=[pltpu.VMEM((SUPER, 128), f32)] * (2 * n_dil) + _Gather.scratch(rows, pack.dtype),
        compiler_params=_params("arbitrary", "arbitrary"),
    )(*args, pack)


def _attn_bwd(qs, ks, vs, dos, lse, delta, chip_sum):
    S = qs[0].shape[1]
    n_steps = S // SUPER
    last = n_steps - 1
    n_dil = len(DILATIONS)
    n_total = (ATTN_WIDTH // 128) * (n_steps + 1)

    def body(*refs):
        ins, (lse_ref, dl_ref, sum_ref) = refs[:6 * n_dil], refs[6 * n_dil:6 * n_dil + 3]
        dq_ref, dk_ref, dv_ref, others_ref = refs[6 * n_dil + 3:6 * n_dil + 7]
        dq_acc, dk_acc, dv_acc = refs[6 * n_dil + 7:6 * n_dil + 10]
        scatter = _Scatter(sum_ref, others_ref, *refs[6 * n_dil + 10:])
        sb = pl.program_id(1)
        step = pl.program_id(0) * (n_steps + 1) + sb
        cur = sb % 2
        prv = 1 - cur

        @pl.when(step == 0)
        def _():
            scatter.start()

        @pl.when(sb < n_steps)
        def _():
            dq_acc[...] = jnp.zeros_like(dq_acc)
            dk_acc[cur] = jnp.zeros((SUPER, 128), f32)
            dv_acc[cur] = jnp.zeros((SUPER, 128), f32)
            h0 = _head0_mask()
            for ci, d in enumerate(DILATIONS):
                q_ref, do_ref, kc_ref, kp_ref, vc_ref, vp_ref = ins[6 * ci:6 * ci + 6]

                def unit(u, carry, d=d, q_ref=q_ref, do_ref=do_ref, kc_ref=kc_ref, kp_ref=kp_ref, vc_ref=vc_ref,
                         vp_ref=vp_ref):
                    r, n, per_stream = _unit_place(d, u)
                    qv = q_ref[r, _block_rows(n), :]
                    dov = do_ref[r, _block_rows(n), :]
                    kb = _band(kc_ref, kp_ref, r, n)
                    vb = _band(vc_ref, vp_ref, r, n)
                    rows = _natural_rows(d, r, n)
                    has_prev = (sb > 0) | (n > 0)
                    q_pair = _head_pair_rows(qv, h0)
                    do_pair = _head_pair_rows(dov, h0)
                    s_t = jnp.where(_band_mask_t(has_prev), _dot_nt(kb, q_pair), NEG)
                    p_t = jnp.exp(s_t - _per_query_row(lse_ref[rows, :]))
                    dp_t = _dot_nt(vb, do_pair)
                    ds_t = (p_t * (dp_t - _per_query_row(dl_ref[rows, :]))).astype(bf16)
                    dvb = _dot(p_t.astype(bf16), do_pair)
                    dkb = _dot(ds_t, q_pair)
                    dq_pair = _dot_tn(ds_t, kb)
                    dq_acc[rows, :] += jnp.where(h0, dq_pair[:BLOCK], dq_pair[BLOCK:])
                    dk_acc[cur, rows, :] += dkb[BLOCK:]
                    dv_acc[cur, rows, :] += dvb[BLOCK:]

                    slot = jnp.where((n > 0) | (sb == 0), cur, prv)
                    before = _natural_rows(d, r, jnp.where(n > 0, n - 1, per_stream - 1))
                    dk_acc[slot, before, :] += dkb[:BLOCK]
                    dv_acc[slot, before, :] += dvb[:BLOCK]
                    return carry

                lax.fori_loop(0, UNITS, unit, 0, unroll=BWD_UNROLL)
            dq_ref[...] = (dq_acc[...] * ATTN_SCALE).astype(bf16)

        @pl.when(sb > 0)
        def _():
            dk_ref[...] = dk_acc[prv].astype(bf16)
            dv_ref[...] = dv_acc[prv].astype(bf16)

        @pl.when(step == n_total - 1)
        def _():
            scatter.finish()

    args = []
    for q, k, v, do in zip(qs, ks, vs, dos):
        args += [q, do, k, k, v, v]
    nat = pl.BlockSpec((SUPER, 128), lambda hp, sb: (jnp.minimum(sb, last), hp))
    nat_before = pl.BlockSpec((SUPER, 128), lambda hp, sb: (jnp.clip(sb - 1, 0, last), hp))
    out = jax.ShapeDtypeStruct((S, ATTN_WIDTH), bf16)
    half = chip_sum.shape[1]
    return pl.pallas_call(
        body, name="attn_bwd", grid=(ATTN_WIDTH // 128, n_steps + 1),
        in_specs=_attn_in_specs(S, True) + [nat, nat, ANY], out_specs=[nat, nat_before, nat_before, ANY],
        out_shape=[out, out, out, _Scatter.out_shape(half, chip_sum.dtype)],
        scratch_shapes=[pltpu.VMEM((SUPER, 128), f32), pltpu.VMEM((2, SUPER, 128), f32),
                        pltpu.VMEM((2, SUPER, 128), f32)] + _Scatter.scratch(half),
        compiler_params=_params("arbitrary", "arbitrary"),
    )(*args, lse, delta, chip_sum)


def _rms(v):
    return lax.rsqrt(jnp.mean(v * v, axis=-1, keepdims=True) + EPS)


def _out_proj(pool_out, attn_out, w_out, x, g2, g3):
    S = x.shape[0]
    ts = 512

    def body(p_ref, a_ref, w_ref, x_ref, g2_ref, g3_ref, mix_ref, x2_ref, h2_ref):
        mix = _dot(p_ref[...], w_ref[:POOL_WIDTH, :]) + _dot(a_ref[...], w_ref[POOL_WIDTH:, :])
        mix_ref[...] = mix
        x2 = x_ref[...] + (mix * _rms(mix)) * g2_ref[...]
        x2_ref[...] = x2
        h2_ref[...] = ((x2 * _rms(x2)) * g3_ref[...]).astype(bf16)

    row = lambda w: pl.BlockSpec((ts, w), lambda i: (i, 0))
    gain = pl.BlockSpec((1, D_MODEL), lambda i: (0, 0))
    return pl.pallas_call(
        body, name="out_proj", grid=(S // ts,),
        in_specs=[row(POOL_WIDTH), row(ATTN_WIDTH), pl.BlockSpec((D_MODEL, D_MODEL), lambda i: (0, 0)),
                  row(D_MODEL), gain, gain],
        out_specs=[row(D_MODEL)] * 3,
        out_shape=[jax.ShapeDtypeStruct((S, D_MODEL), f32), jax.ShapeDtypeStruct((S, D_MODEL), f32),
                   jax.ShapeDtypeStruct((S, D_MODEL), bf16)],
        compiler_params=_params("parallel"),
    )(pool_out, attn_out, w_out, x, g2, g3)


FF_TILE = 256
FF_STEP_ROWS = 2048
FF_ROWS = 512
FF_BWD_ROWS = 256


def _sigmoid(g):
    return 1.0 / (1.0 + jnp.exp(-g))


def _ff_act_shape(S):
    return jax.ShapeDtypeStruct((D_FF // FF_TILE, S, FF_TILE), bf16)


def _ff_act_spec(ts):
    return pl.BlockSpec((1, ts, FF_TILE), lambda i, j: (j, i, 0))


def _ffn_fwd(h2, w_gate, w_up, w_down):
    S = h2.shape[0]
    ts = min(S, FF_STEP_ROWS)

    def body(h_ref, wg_ref, wu_ref, wd_ref, gate_ref, up_ref, f_ref):
        def rows_pass(first):
            def sub(i, carry):
                rows = pl.ds(pl.multiple_of(i * FF_ROWS, FF_ROWS), FF_ROWS)
                h = h_ref[rows, :]
                gate = _dot_nt(h, wg_ref[...])
                up = _dot_nt(h, wu_ref[...])
                gate_ref[0, rows, :] = gate.astype(bf16)
                up_ref[0, rows, :] = up.astype(bf16)
                part = _dot((gate * _sigmoid(gate) * up).astype(bf16), wd_ref[...])
                if first:
                    f_ref[rows, :] = part
                else:
                    f_ref[rows, :] += part
                return carry

            lax.fori_loop(0, ts // FF_ROWS, sub, 0, unroll=True)

        @pl.when(pl.program_id(1) == 0)
        def _():
            rows_pass(True)

        @pl.when(pl.program_id(1) > 0)
        def _():
            rows_pass(False)

    act = _ff_act_spec(ts)
    weight = pl.BlockSpec((FF_TILE, D_MODEL), lambda i, j: (j, 0))
    return pl.pallas_call(
        body, name="ffn_fwd", grid=(S // ts, D_FF // FF_TILE),
        in_specs=[pl.BlockSpec((ts, D_MODEL), lambda i, j: (i, 0)), weight, weight, weight],
        out_specs=[act, act, pl.BlockSpec((ts, D_MODEL), lambda i, j: (i, 0))],
        out_shape=[_ff_act_shape(S), _ff_act_shape(S), jax.ShapeDtypeStruct((S, D_MODEL), f32)],
        compiler_params=_params("parallel", "arbitrary"),
    )(h2, w_gate, w_up, w_down)


def _loss_head(f, x2, target, g4):
    S = f.shape[0]
    ts = 512

    def body(f_ref, x2_ref, t_ref, g_ref, dy_ref, df_ref, dg_ref, loss_ref):
        @pl.when(pl.program_id(0) == 0)
        def _():
            dg_ref[...] = jnp.zeros_like(dg_ref)
            loss_ref[...] = jnp.zeros_like(loss_ref)

        fv = f_ref[...]
        g = g_ref[...]
        r = _rms(fv)
        fhat = fv * r
        err = (x2_ref[...] + fhat * g) - t_ref[...]
        loss_ref[...] += 0.5 * jnp.sum(jnp.mean(err * err, axis=-1, keepdims=True), axis=0, keepdims=True)
        dy = err * (1.0 / D_MODEL)
        dy_ref[...] = dy
        dg_ref[...] += jnp.sum(dy * fhat, axis=0, keepdims=True)
        dyg = dy * g
        df_ref[...] = (r * (dyg - fhat * jnp.mean(dyg * fhat, axis=-1, keepdims=True))).astype(bf16)

    row = pl.BlockSpec((ts, D_MODEL), lambda i: (i, 0))
    gain = pl.BlockSpec((1, D_MODEL), lambda i: (0, 0))
    return pl.pallas_call(
        body, name="loss_head", grid=(S // ts,), in_specs=[row, row, row, gain],
        out_specs=[row, row, gain, pl.BlockSpec((1, 1), lambda i: (0, 0))],
        out_shape=[jax.ShapeDtypeStruct((S, D_MODEL), f32), jax.ShapeDtypeStruct((S, D_MODEL), bf16),
                   jax.ShapeDtypeStruct((1, D_MODEL), f32), jax.ShapeDtypeStruct((1, 1), f32)],
        compiler_params=_params("arbitrary"),
    )(f, x2, target, g4)


def _ffn_bwd(df, gate, up, w_gate, w_up, w_down):
    S = df.shape[0]
    ts = min(S, FF_STEP_ROWS)

    def body(df_ref, gate_ref, up_ref, wg_ref, wu_ref, wd_ref, a_ref, dgate_ref, dup_ref, dh_ref):
        def rows_pass(first):
            def sub(i, carry):
                rows = pl.ds(pl.multiple_of(i * FF_BWD_ROWS, FF_BWD_ROWS), FF_BWD_ROWS)
                da = _dot_nt(df_ref[rows, :], wd_ref[...])
                g = gate_ref[0, rows, :].astype(f32)
                u = up_ref[0, rows, :].astype(f32)
                sig = _sigmoid(g)
                silu = g * sig
                a_ref[0, rows, :] = (silu * u).astype(bf16)
                dup = (da * silu).astype(bf16)
                dgate = (da * u * (sig * (1.0 + g * (1.0 - sig)))).astype(bf16)
                dup_ref[0, rows, :] = dup
                dgate_ref[0, rows, :] = dgate
                part = _dot(dgate, wg_ref[...]) + _dot(dup, wu_ref[...])
                if first:
                    dh_ref[rows, :] = part
                else:
                    dh_ref[rows, :] += part
                return carry

            lax.fori_loop(0, ts // FF_BWD_ROWS, sub, 0, unroll=True)

        @pl.when(pl.program_id(1) == 0)
        def _():
            rows_pass(True)

        @pl.when(pl.program_id(1) > 0)
        def _():
            rows_pass(False)

    act = _ff_act_spec(ts)
    row = pl.BlockSpec((ts, D_MODEL), lambda i, j: (i, 0))
    return pl.pallas_call(
        body, name="ffn_bwd", grid=(S // ts, D_FF // FF_TILE),
        in_specs=[row, act, act,
                  pl.BlockSpec((FF_TILE, D_MODEL), lambda i, j: (j, 0)),
                  pl.BlockSpec((FF_TILE, D_MODEL), lambda i, j: (j, 0)),
                  pl.BlockSpec((FF_TILE, D_MODEL), lambda i, j: (j, 0))],
        out_specs=[act, act, act, row],
        out_shape=[_ff_act_shape(S)] * 3 + [jax.ShapeDtypeStruct((S, D_MODEL), f32)],
        compiler_params=_params("parallel", "arbitrary"),
    )(df, gate, up, w_gate, w_up, w_down)


def _norm_bwd(dh2, dy, x2, mix, g3, g2):
    S = dh2.shape[0]
    ts = 512

    def body(dh_ref, dy_ref, x2_ref, mix_ref, g3_ref, g2_ref, dx2_ref, dmix_ref, dg3_ref, dg2_ref):
        @pl.when(pl.program_id(0) == 0)
        def _():
            dg3_ref[...] = jnp.zeros_like(dg3_ref)
            dg2_ref[...] = jnp.zeros_like(dg2_ref)

        dh = dh_ref[...]
        x2 = x2_ref[...]
        r3 = _rms(x2)
        xhat = x2 * r3
        dg3_ref[...] += jnp.sum(dh * xhat, axis=0, keepdims=True)
        dhg = dh * g3_ref[...]
        dx2 = dy_ref[...] + r3 * (dhg - xhat * jnp.mean(dhg * xhat, axis=-1, keepdims=True))
        dx2_ref[...] = dx2
        mix = mix_ref[...]
        r2 = _rms(mix)
        mhat = mix * r2
        dg2_ref[...] += jnp.sum(dx2 * mhat, axis=0, keepdims=True)
        dmg = dx2 * g2_ref[...]
        dmix_ref[...] = (r2 * (dmg - mhat * jnp.mean(dmg * mhat, axis=-1, keepdims=True))).astype(bf16)

    row = pl.BlockSpec((ts, D_MODEL), lambda i: (i, 0))
    gain = pl.BlockSpec((1, D_MODEL), lambda i: (0, 0))
    return pl.pallas_call(
        body, name="norm_bwd", grid=(S // ts,), in_specs=[row, row, row, row, gain, gain],
        out_specs=[row, row, gain, gain],
        out_shape=[jax.ShapeDtypeStruct((S, D_MODEL), f32), jax.ShapeDtypeStruct((S, D_MODEL), bf16),
                   jax.ShapeDtypeStruct((1, D_MODEL), f32), jax.ShapeDtypeStruct((1, D_MODEL), f32)],
        compiler_params=_params("arbitrary"),
    )(dh2, dy, x2, mix, g3, g2)


def _out_proj_bwd(dmix, w_out, attn_out, head_ones, grads):
    S = dmix.shape[0]
    ts = 512
    n_dil = len(DILATIONS)

    def body(dm_ref, w_ref, o_ref, ones_ref, g_ref, dp_ref, dl_ref, *rest):
        do_refs, theirs_ref = rest[:n_dil], rest[n_dil]
        stage = rest[n_dil + 1:n_dil + 1 + N_STAGE]
        swap = _Swap(g_ref, theirs_ref, *rest[n_dil + 1 + N_STAGE:])

        @pl.when(pl.program_id(0) == 0)
        def _():
            swap.start()

        @pl.when(pl.program_id(0) == S // ts - 1)
        def _():
            swap.finish()

        dcat = _dot_nt(dm_ref[...], w_ref[...])
        dp_ref[...] = dcat[:, :POOL_WIDTH]
        do = dcat[:, POOL_WIDTH:]
        for j in range(ATTN_WIDTH // 128):
            stage[j][...] = do[:, j * 128:(j + 1) * 128]
        _store_streams(stage, do_refs, ts)
        prod = do * o_ref[...].astype(f32)
        hi = prod.astype(bf16)
        lo = (prod - hi.astype(f32)).astype(bf16)
        ones = ones_ref[...]
        for j in range(ATTN_WIDTH // 128):
            cols = slice(j * 128, (j + 1) * 128)
            dl_ref[:, cols] = _dot(hi[:, cols], ones) + _dot(lo[:, cols], ones)

    row = lambda w: pl.BlockSpec((ts, w), lambda i: (i, 0))
    res = pl.pallas_call(
        body, name="out_proj_bwd", grid=(S // ts,),
        in_specs=[row(D_MODEL), pl.BlockSpec((D_MODEL, D_MODEL), lambda i: (0, 0)), row(ATTN_WIDTH),
                  pl.BlockSpec((128, 128), lambda i: (0, 0)), ANY],
        out_specs=[row(POOL_WIDTH), row(ATTN_WIDTH)] + [_stream_spec(d, ts) for d in DILATIONS] + [ANY],
        out_shape=[jax.ShapeDtypeStruct((S, POOL_WIDTH), f32), jax.ShapeDtypeStruct((S, ATTN_WIDTH), f32)]
        + [_stream_shape(S, d) for d in DILATIONS] + [_Swap.out_shape(grads)],
        scratch_shapes=_stage_scratch(ts) + _Swap.scratch(grads),
        compiler_params=_params("arbitrary"),
    )(dmix, w_out, attn_out, head_ones, grads)
    return res[0], res[1], res[2:2 + n_dil], res[2 + n_dil]


def _in_proj_bwd(du, dq, dk, dv, cos_t, sin_t, w_in, x, dx2, g1):
    S = x.shape[0]
    ts = 512

    def body(du_ref, dq_ref, dk_ref, dv_ref, cos_ref, sin_ref, w_ref, x_ref, dx2_ref, g_ref, gx_ref, dproj_ref, dg_ref):
        @pl.when(pl.program_id(0) == 0)
        def _():
            dg_ref[...] = jnp.zeros_like(dg_ref)

        dproj_ref[:, :POOL_WIDTH] = du_ref[...]
        cos = cos_ref[...]
        sin = sin_ref[...]
        first = _first_half_mask(ts)
        for j in range(ATTN_WIDTH // 128):
            cols = slice(j * 128, (j + 1) * 128)
            for base, ref in ((POOL_WIDTH, dq_ref), (POOL_WIDTH + ATTN_WIDTH, dk_ref)):
                g = ref[:, cols].astype(f32)
                pre = g * cos + _rope_partner(g * sin, first)
                dproj_ref[:, base + j * 128: base + (j + 1) * 128] = pre.astype(bf16)
        dproj_ref[:, POOL_WIDTH + 2 * ATTN_WIDTH:] = dv_ref[...]

        dh = _dot(dproj_ref[...], w_ref[...])
        xv = x_ref[...]
        r = _rms(xv)
        xhat = xv * r
        dg_ref[...] += jnp.sum(dh * xhat, axis=0, keepdims=True)
        dhg = dh * g_ref[...]
        gx_ref[...] = dx2_ref[...] + r * (dhg - xhat * jnp.mean(dhg * xhat, axis=-1, keepdims=True))

    row = lambda w: pl.BlockSpec((ts, w), lambda i: (i, 0))
    gain = pl.BlockSpec((1, D_MODEL), lambda i: (0, 0))
    return pl.pallas_call(
        body, name="in_proj_bwd", grid=(S // ts,),
        in_specs=[row(POOL_WIDTH)] + [row(ATTN_WIDTH)] * 3 + [row(128), row(128),
                  pl.BlockSpec((IN_WIDTH, D_MODEL), lambda i: (0, 0)), row(D_MODEL), row(D_MODEL), gain],
        out_specs=[row(D_MODEL), row(IN_WIDTH), gain],
        out_shape=[jax.ShapeDtypeStruct((S, D_MODEL), f32), jax.ShapeDtypeStruct((S, IN_WIDTH), bf16),
                   jax.ShapeDtypeStruct((1, D_MODEL), f32)],
        compiler_params=_params("arbitrary"),
    )(du, dq, dk, dv, cos_t, sin_t, w_in, x, dx2, g1)


def _matmul_tiles_tn(a, b, name):
    T, K, w = a.shape
    N = b.shape[1]
    tk = 1024

    def body(a_ref, b_ref, o_ref):
        def tiles_pass(first):
            for t in range(T):
                part = _dot_tn(a_ref[t], b_ref[...])
                if first:
                    o_ref[t * w:(t + 1) * w, :] = part
                else:
                    o_ref[t * w:(t + 1) * w, :] += part

        @pl.when(pl.program_id(0) == 0)
        def _():
            tiles_pass(True)

        @pl.when(pl.program_id(0) > 0)
        def _():
            tiles_pass(False)

    return pl.pallas_call(
        body, name=name, grid=(K // tk,),
        in_specs=[pl.BlockSpec((T, tk, w), lambda k: (0, k, 0)), pl.BlockSpec((tk, N), lambda k: (k, 0))],
        out_specs=pl.BlockSpec((T * w, N), lambda k: (0, 0)),
        out_shape=jax.ShapeDtypeStruct((T * w, N), f32),
        compiler_params=_params("arbitrary"),
    )(a, b)


def _matmul_tn(a, b, name):
    K, M = a.shape
    N = b.shape[1]
    tk = 1024

    def body(a_ref, b_ref, o_ref):
        _tn_step(a_ref, b_ref, o_ref, M)

    return pl.pallas_call(
        body, name=name, grid=(K // tk,),
        in_specs=[pl.BlockSpec((tk, M), lambda k: (k, 0)), pl.BlockSpec((tk, N), lambda k: (k, 0))],
        out_specs=pl.BlockSpec((M, N), lambda k: (0, 0)),
        out_shape=jax.ShapeDtypeStruct((M, N), f32),
        compiler_params=_params("arbitrary"),
    )(a, b)


def _tn_step(a_ref, b_ref, o_ref, M):
    w = 256

    def tiles_pass(first):
        for t in range(M // w):
            part = _dot_tn(a_ref[:, t * w:(t + 1) * w], b_ref[...])
            if first:
                o_ref[t * w:(t + 1) * w, :] = part
            else:
                o_ref[t * w:(t + 1) * w, :] += part

    @pl.when(pl.program_id(0) == 0)
    def _():
        tiles_pass(True)

    @pl.when(pl.program_id(0) > 0)
    def _():
        tiles_pass(False)


def _matmul_tn_and_small_sum(a, b, block, name):
    K, M = a.shape
    N = b.shape[1]
    tk = 1024
    n_steps = K // tk

    def body(a_ref, b_ref, block_ref, o_ref, total_ref, *scratch):
        small = _SmallSum(block_ref, *scratch)

        @pl.when(pl.program_id(0) == 0)
        def _():
            small.start()

        _tn_step(a_ref, b_ref, o_ref, M)

        @pl.when(pl.program_id(0) == n_steps - 1)
        def _():
            small.finish(total_ref)

    return pl.pallas_call(
        body, name=name, grid=(n_steps,),
        in_specs=[pl.BlockSpec((tk, M), lambda k: (k, 0)), pl.BlockSpec((tk, N), lambda k: (k, 0)), ANY],
        out_specs=[pl.BlockSpec((M, N), lambda k: (0, 0)), pl.BlockSpec(block.shape, lambda k: (0, 0))],
        out_shape=[jax.ShapeDtypeStruct((M, N), f32), jax.ShapeDtypeStruct(block.shape, block.dtype)],
        scratch_shapes=_SmallSum.scratch(block),
        compiler_params=_params("arbitrary"),
    )(a, b, block)


def _rope_tables(S):
    half = HEAD_DIM // 2
    freqs = ROPE_THETA ** (-jnp.arange(half, dtype=f32) * (2.0 / HEAD_DIM))
    ang = jnp.arange(S).astype(f32)[:, None] * freqs[None, :]
    cos = jnp.tile(jnp.cos(ang), (1, 4))
    sin = jnp.sin(ang)
    sin = jnp.tile(jnp.concatenate([-sin, sin], axis=1), (1, 2))
    return cos, sin


def _block_diag(w_pool):
    w = jnp.zeros((POOL_WIDTH, POOL_WIDTH), w_pool.dtype)
    for g in range(POOL_WIDTH // POOL_GROUP):
        w = lax.dynamic_update_slice(w, w_pool[g], (g * POOL_GROUP, g * POOL_GROUP))
    return w


def _head_ones():
    head = np.arange(128) // HEAD_DIM
    return jnp.asarray(head[:, None] == head[None, :], dtype=bf16)


def _place():
    x, y, c = lax.axis_index("x"), lax.axis_index("y"), lax.axis_index("c")
    chips = [(1 - x, y), (x, 1 - y), (1 - x, 1 - y)]
    return x, y, c, chips


ANY = pl.BlockSpec(memory_space=pl.ANY)
N_PEER_CHIPS = N_CHIPS - 1
ICI_PIECES = 4
D2D_PIECES = 8
LOCAL_PIECES = 8


def _row_chunks(rows, n, unit=32):
    units = rows // unit
    out, start = [], 0
    for i in range(n):
        size = (units // n + (1 if i < units % n else 0)) * unit
        out.append((start, size))
        start += size
    return [piece for piece in out if piece[1]]


class _LocalCopy:
    def __init__(self, src_rows, dst_rows, rows, buf, sems_in, sems_out):
        self.loads, self.stores = [], []
        for i, (start, size) in enumerate(_row_chunks(rows, LOCAL_PIECES)):
            r = pl.ds(start, size)
            self.loads.append(pltpu.make_async_copy(src_rows(r), buf.at[r], sems_in.at[i]))
            self.stores.append(pltpu.make_async_copy(buf.at[r], dst_rows(r), sems_out.at[i]))

    def start(self):
        for cp in self.loads:
            cp.start()

    def pass_on(self):
        for load, store in zip(self.loads, self.stores):
            load.wait()
            store.start()

    def finish(self):
        for store in self.stores:
            store.wait()

    @staticmethod
    def scratch(rows, dtype):
        return [pltpu.VMEM((rows, D_MODEL), dtype), pltpu.SemaphoreType.DMA((LOCAL_PIECES,)),
                pltpu.SemaphoreType.DMA((LOCAL_PIECES,))]


class _Gather:
    def __init__(self, w_ref, out_ref, send1, recv1, send2, recv2, buf, sems_in, sems_out):
        x, y, c, chips = _place()
        me = 2 * x + y
        rows = w_ref.shape[0]
        half = rows // 2
        pieces = _row_chunks(half, ICI_PIECES)
        self.own = _LocalCopy(lambda r: w_ref.at[r], lambda r: out_ref.at[me, r], rows, buf, sems_in, sems_out)

        def rows_of(core, piece):
            start, size = piece
            return pl.ds(core * half + start, size)

        self.sends, self.arrivals, self.forwards, self.forward_arrivals = [], [], [], []
        for i, piece in enumerate(pieces):
            for j, (cx, cy) in enumerate(chips):
                k = j * len(pieces) + i
                there = 2 * cx + cy

                def direct(src_chip, cx=cx, cy=cy, k=k, piece=piece):
                    return pltpu.make_async_remote_copy(
                        src_ref=w_ref.at[rows_of(c, piece)], dst_ref=out_ref.at[src_chip, rows_of(c, piece)],
                        send_sem=send1.at[k], recv_sem=recv1.at[k], device_id=(cx, cy, c), device_id_type=MESH)

                def passed(core, there=there, k=k, piece=piece):
                    return pltpu.make_async_remote_copy(
                        src_ref=out_ref.at[there, rows_of(core, piece)], dst_ref=out_ref.at[there, rows_of(core, piece)],
                        send_sem=send2.at[k], recv_sem=recv2.at[k], device_id=(x, y, 1 - c), device_id_type=MESH)

                self.sends.append(direct(me))
                self.arrivals.append(direct(there))
                self.forwards.append(passed(c))
                self.forward_arrivals.append(passed(1 - c))

    def start(self):
        for cp in self.sends:
            cp.start()
        self.own.start()

    def pass_on(self):
        self.own.pass_on()
        for arrival, forward in zip(self.arrivals, self.forwards):
            arrival.wait_recv()
            forward.start()

    def finish(self):
        for arrival in self.forward_arrivals:
            arrival.wait_recv()
        for cp in self.sends + self.forwards:
            cp.wait_send()
        self.own.finish()

    @staticmethod
    def scratch(rows, dtype):
        n = N_PEER_CHIPS * len(_row_chunks(rows // 2, ICI_PIECES))
        return [pltpu.SemaphoreType.DMA((n,))] * 4 + _LocalCopy.scratch(rows, dtype)

    @staticmethod
    def out_shape(rows, dtype):
        return jax.ShapeDtypeStruct((N_CHIPS, rows, D_MODEL), dtype)


def _gather_weights(pack):
    rows = pack.shape[0]

    def body(w_ref, out_ref, *scratch):
        gather = _Gather(w_ref, out_ref, *scratch)
        gather.start()
        gather.pass_on()
        gather.finish()

    return pl.pallas_call(
        body, name="gather_weights", in_specs=[ANY], out_specs=ANY, out_shape=_Gather.out_shape(rows, pack.dtype),
        scratch_shapes=_Gather.scratch(rows, pack.dtype),
        compiler_params=pltpu.CompilerParams(vmem_limit_bytes=VMEM_LIMIT_V7X),
    )(pack)


class _Scatter:
    def __init__(self, h_ref, out_ref, send, recv):
        x, y, c, chips = _place()
        pieces = _row_chunks(h_ref.shape[1], ICI_PIECES)
        self.copies = []
        for i, (start, size) in enumerate(pieces):
            for j, (cx, cy) in enumerate(chips):
                k = j * len(pieces) + i
                self.copies.append(pltpu.make_async_remote_copy(
                    src_ref=h_ref.at[2 * cx + cy, pl.ds(start, size)], dst_ref=out_ref.at[j, pl.ds(start, size)],
                    send_sem=send.at[k], recv_sem=recv.at[k], device_id=(cx, cy, c), device_id_type=MESH))

    def start(self):
        for cp in self.copies:
            cp.start()

    def finish(self):
        for cp in self.copies:
            cp.wait_recv()
        for cp in self.copies:
            cp.wait_send()

    @staticmethod
    def scratch(half):
        n = N_PEER_CHIPS * len(_row_chunks(half, ICI_PIECES))
        return [pltpu.SemaphoreType.DMA((n,))] * 2

    @staticmethod
    def out_shape(half, dtype):
        return jax.ShapeDtypeStruct((N_PEER_CHIPS, half, D_MODEL), dtype)


def _scatter_to_chips(h):
    half = h.shape[1]

    def body(h_ref, out_ref, send, recv):
        scatter = _Scatter(h_ref, out_ref, send, recv)
        scatter.start()
        scatter.finish()

    return pl.pallas_call(
        body, name="scatter_to_chips", in_specs=[ANY], out_specs=ANY, out_shape=_Scatter.out_shape(half, h.dtype),
        scratch_shapes=_Scatter.scratch(half),
    )(h)


class _Swap:
    def __init__(self, g_ref, theirs_ref, send, recv):
        x, y, c, _ = _place()
        half = g_ref.shape[1] // 2
        pieces = _row_chunks(half, D2D_PIECES)
        self.copies = []
        for s in range(N_CHIPS):
            for i, (start, size) in enumerate(pieces):
                k = s * len(pieces) + i
                self.copies.append(pltpu.make_async_remote_copy(
                    src_ref=g_ref.at[s, pl.ds((1 - c) * half + start, size)], dst_ref=theirs_ref.at[s, pl.ds(start, size)],
                    send_sem=send.at[k], recv_sem=recv.at[k], device_id=(x, y, 1 - c), device_id_type=MESH))

    def start(self):
        for cp in self.copies:
            cp.start()

    def finish(self):
        for cp in self.copies:
            cp.wait()

    @staticmethod
    def scratch(g):
        n = N_CHIPS * len(_row_chunks(g.shape[1] // 2, D2D_PIECES))
        return [pltpu.SemaphoreType.DMA((n,))] * 2

    @staticmethod
    def out_shape(g):
        return jax.ShapeDtypeStruct((N_CHIPS, g.shape[1] // 2, D_MODEL), g.dtype)


def _swap_halves(g):
    def body(g_ref, theirs_ref, send, recv):
        swap = _Swap(g_ref, theirs_ref, send, recv)
        swap.start()
        swap.finish()

    return pl.pallas_call(
        body, name="swap_halves", in_specs=[ANY], out_specs=ANY, out_shape=_Swap.out_shape(g),
        scratch_shapes=_Swap.scratch(g),
    )(g)


ADD_TILE_MAX_ROWS = 600


def _add_tile(half):
    return max(t for t in range(8, ADD_TILE_MAX_ROWS + 1, 8) if half % t == 0)


def _add_cores(g, theirs, name, out_dtype=f32):
    half = theirs.shape[1]
    tr = _add_tile(half)
    n_t = half // tr

    def body(c_ref, g_ref, t_ref, o_ref):
        o_ref[...] = (g_ref[...] + t_ref[...]).astype(out_dtype)

    blk = pl.BlockSpec((1, tr, D_MODEL), lambda s, t, c_ref: (s, t, 0))
    return pl.pallas_call(
        body, name=name,
        grid_spec=pltpu.PrefetchScalarGridSpec(
            num_scalar_prefetch=1, grid=(N_CHIPS, n_t),
            in_specs=[pl.BlockSpec((1, tr, D_MODEL), lambda s, t, c_ref: (s, c_ref[0] * n_t + t, 0)), blk],
            out_specs=blk),
        out_shape=jax.ShapeDtypeStruct(theirs.shape, out_dtype),
        compiler_params=_params("parallel", "parallel"),
    )(lax.axis_index("c").astype(jnp.int32).reshape(1), g, theirs)


def _add_chips(chip_sum, others, name):
    half = chip_sum.shape[1]
    tr = _add_tile(half)

    def body(me_ref, own_ref, o0, o1, o2, out_ref):
        out_ref[...] = ((own_ref[0].astype(f32) + o0[0].astype(f32)) + o1[0].astype(f32)) + o2[0].astype(f32)

    other = lambda j: pl.BlockSpec((1, tr, D_MODEL), lambda t, me_ref: (j, t, 0))
    return pl.pallas_call(
        body, name=name,
        grid_spec=pltpu.PrefetchScalarGridSpec(
            num_scalar_prefetch=1, grid=(half // tr,),
            in_specs=[pl.BlockSpec((1, tr, D_MODEL), lambda t, me_ref: (me_ref[0], t, 0)), other(0), other(1), other(2)],
            out_specs=pl.BlockSpec((tr, D_MODEL), lambda t, me_ref: (t, 0))),
        out_shape=jax.ShapeDtypeStruct((half, D_MODEL), f32),
        compiler_params=_params("parallel"),
    )((2 * lax.axis_index("x") + lax.axis_index("y")).astype(jnp.int32).reshape(1), chip_sum, others, others, others)


def _join_halves(r):
    half = r.shape[0]
    pieces = _row_chunks(half, 2 * D2D_PIECES)
    n = len(pieces)

    def body(r_ref, out_ref, send, recv, buf, sems_in, sems_out):
        x, y, c, _ = _place()
        own = _LocalCopy(lambda rr: r_ref.at[rr], lambda rr: out_ref.at[c, rr], half, buf, sems_in, sems_out)
        own.start()

        def piece(i, core):
            start, size = pieces[i]
            return pltpu.make_async_remote_copy(
                src_ref=r_ref.at[pl.ds(start, size)], dst_ref=out_ref.at[core, pl.ds(start, size)],
                send_sem=send.at[i], recv_sem=recv.at[i], device_id=(x, y, 1 - c), device_id_type=MESH)

        copies = [piece(i, c) for i in range(n)]
        for cp in copies:
            cp.start()
        own.pass_on()
        for i in range(n):
            piece(i, 1 - c).wait_recv()
        for cp in copies:
            cp.wait_send()
        own.finish()

    return pl.pallas_call(
        body, name="join_halves", in_specs=[ANY], out_specs=ANY,
        out_shape=jax.ShapeDtypeStruct((2,) + r.shape, r.dtype),
        scratch_shapes=[pltpu.SemaphoreType.DMA((n,))] * 2 + _LocalCopy.scratch(half, r.dtype),
        compiler_params=pltpu.CompilerParams(vmem_limit_bytes=VMEM_LIMIT_V7X),
    )(r)


class _SmallSum:
    def __init__(self, b_ref, gathered, send, recv, local_sem):
        x, y, c, _ = _place()
        me = 4 * x + 2 * y + c
        self.gathered = gathered
        self.own = pltpu.make_async_copy(b_ref, gathered.at[me], local_sem)
        self.sends, self.arrivals = [], []
        for kk in range(1, N_DEV):
            flip = lambda v, bit: 1 - v if bit else v
            peer = (flip(x, kk & 4), flip(y, kk & 2), flip(c, kk & 1))
            self.sends.append(pltpu.make_async_remote_copy(
                src_ref=b_ref, dst_ref=gathered.at[me], send_sem=send.at[kk - 1], recv_sem=recv.at[kk - 1],
                device_id=peer, device_id_type=MESH))
            self.arrivals.append(pltpu.make_async_remote_copy(
                src_ref=b_ref, dst_ref=gathered.at[jnp.bitwise_xor(me, kk)], send_sem=send.at[kk - 1],
                recv_sem=recv.at[kk - 1], device_id=peer, device_id_type=MESH))

    def start(self):
        self.own.start()
        for cp in self.sends:
            cp.start()

    def finish(self, out_ref):
        self.own.wait()
        for cp in self.arrivals:
            cp.wait_recv()
        for cp in self.sends:
            cp.wait_send()
        acc = self.gathered[0]
        for dev in range(1, N_DEV):
            acc = acc + self.gathered[dev]
        out_ref[...] = acc

    @staticmethod
    def scratch(block):
        return [pltpu.VMEM((N_DEV,) + block.shape, block.dtype), pltpu.SemaphoreType.DMA((N_DEV - 1,)),
                pltpu.SemaphoreType.DMA((N_DEV - 1,)), pltpu.SemaphoreType.DMA]


def _adamw(w, g, m, v, name):
    rows, cols = w.shape
    tr = rows
    for cand in (512, 256, 128, 64, 32, 16, 8):
        if rows % cand == 0:
            tr = cand
            break
    c1 = 1.0 - ADAM_B1 ** ADAM_STEP
    c2 = 1.0 - ADAM_B2 ** ADAM_STEP

    def body(w_ref, g_ref, m_ref, v_ref, d_ref, nm_ref, nv_ref):
        gv = g_ref[...]
        nm = ADAM_B1 * m_ref[...] + (1.0 - ADAM_B1) * gv
        nv = ADAM_B2 * v_ref[...] + (1.0 - ADAM_B2) * (gv * gv)
        nm_ref[...] = nm
        nv_ref[...] = nv
        d_ref[...] = -ADAM_LR * ((nm / c1) / (jnp.sqrt(nv / c2) + ADAM_EPS) + ADAM_WD * w_ref[...])

    blk = pl.BlockSpec((tr, cols), lambda i: (i, 0))
    shape = jax.ShapeDtypeStruct((rows, cols), f32)
    return pl.pallas_call(
        body, name=name, grid=(rows // tr,), in_specs=[blk] * 4, out_specs=[blk] * 3, out_shape=[shape] * 3,
        compiler_params=_params("parallel"),
    )(w, g, m, v)


LARGE = ("w_in", "w_out", "w_gate", "w_up", "w_down")
SMALL = ("ln_pre_mix", "ln_post_mix", "ln_pre_ffn", "ln_post_ffn", "pool_scale", "w_pool")
SHARD_ROWS = {"w_in": 640, "w_out": 256, "w_gate": 704, "w_up": 704, "w_down": 704}
COLUMN_SHARDED = ("w_in", "w_gate", "w_up")
NEEDED_FIRST = ("w_in",)
NEEDED_LATER = ("w_out", "w_gate", "w_up", "w_down")
READY_EARLY = ("w_out", "w_gate", "w_up", "w_down")
READY_LATE = ("w_in",)


def _pack_shard(shards, names):
    return jnp.concatenate([shards[n].T if n in COLUMN_SHARDED else shards[n] for n in names], axis=0)


def _unpack_shard(pack, names):
    out, row = {}, 0
    for n in names:
        part = pack[row:row + SHARD_ROWS[n]]
        out[n] = part.T if n in COLUMN_SHARDED else part
        row += SHARD_ROWS[n]
    return out


def _whole_from_shards(packs, names):
    out, row = {}, 0
    for n in names:
        rows = SHARD_ROWS[n]
        out[n] = packs[:, row:row + rows].reshape(N_CHIPS * rows, D_MODEL)
        row += rows
    return out


def _shards_from_whole(grads, names):
    return jnp.concatenate([grads[n].reshape(N_CHIPS, SHARD_ROWS[n], D_MODEL) for n in names], axis=1)


def _pack_small(vals):
    rows = [vals[n].reshape(1, D_MODEL) for n in SMALL[:4]]
    rows.append(jnp.pad(vals["pool_scale"].reshape(1, POOL_WIDTH), ((0, 0), (0, D_MODEL - POOL_WIDTH))))
    rows.append(jnp.pad(vals["loss"].reshape(1, 1), ((0, 0), (0, D_MODEL - 1))))
    rows.append(jnp.zeros((2, D_MODEL), f32))
    rows.append(vals["w_pool"].reshape(16, D_MODEL))
    return jnp.concatenate(rows, axis=0)


def _unpack_small(block):
    out = {n: block[i:i + 1] for i, n in enumerate(SMALL[:4])}
    out["pool_scale"] = block[4:5, :POOL_WIDTH]
    out["loss"] = block[5, 0]
    out["w_pool"] = block[8:24].reshape(1, 4, POOL_GROUP, POOL_GROUP)
    return out


def kernel(x, ln_pre_mix, w_in, w_pool, pool_scale, w_out, ln_post_mix, ln_pre_ffn, w_gate, w_up, w_down, ln_post_ffn, loss_target, m_ln_pre_mix, m_w_in, m_w_pool, m_pool_scale, m_w_out, m_ln_post_mix, m_ln_pre_ffn, m_w_gate, m_w_up, m_w_down, m_ln_post_ffn, v_ln_pre_mix, v_w_in, v_w_pool, v_pool_scale, v_w_out, v_ln_post_mix, v_ln_pre_ffn, v_w_gate, v_w_up, v_w_down, v_ln_post_ffn):
    w = dict(ln_pre_mix=ln_pre_mix, w_in=w_in, w_pool=w_pool, pool_scale=pool_scale, w_out=w_out,
             ln_post_mix=ln_post_mix, ln_pre_ffn=ln_pre_ffn, w_gate=w_gate, w_up=w_up, w_down=w_down,
             ln_post_ffn=ln_post_ffn)
    m = dict(ln_pre_mix=m_ln_pre_mix, w_in=m_w_in, w_pool=m_w_pool, pool_scale=m_pool_scale, w_out=m_w_out,
             ln_post_mix=m_ln_post_mix, ln_pre_ffn=m_ln_pre_ffn, w_gate=m_w_gate, w_up=m_w_up, w_down=m_w_down,
             ln_post_ffn=m_ln_post_ffn)
    v = dict(ln_pre_mix=v_ln_pre_mix, w_in=v_w_in, w_pool=v_w_pool, pool_scale=v_pool_scale, w_out=v_w_out,
             ln_post_mix=v_ln_post_mix, ln_pre_ffn=v_ln_pre_ffn, w_gate=v_w_gate, w_up=v_w_up, w_down=v_w_down,
             ln_post_ffn=v_ln_post_ffn)

    xs, target = x[0], loss_target[0]
    cos_t, sin_t = _rope_tables(xs.shape[0])
    w_bd = _block_diag(w_pool[0]).astype(bf16)
    shard = {n: w[n][0].astype(bf16) for n in LARGE}

    w_in_whole = _whole_from_shards(_gather_weights(_pack_shard(shard, NEEDED_FIRST)), NEEDED_FIRST)["w_in"]
    h1, u, qs, ks, vs = _in_proj(xs, ln_pre_mix, w_in_whole, cos_t, sin_t)
    pool_out = _pool_fwd(u, w_bd, pool_scale)
    attn_out, lse, later = _attn_fwd(qs, ks, vs, _pack_shard(shard, NEEDED_LATER))
    whole = _whole_from_shards(later, NEEDED_LATER)
    mix, x2, h2 = _out_proj(pool_out, attn_out, whole["w_out"], xs, ln_post_mix, ln_pre_ffn)
    gate, up, f = _ffn_fwd(h2, whole["w_gate"], whole["w_up"], whole["w_down"])
    dy, df, dg4, loss = _loss_head(f, x2, target, ln_post_ffn)

    large = {}
    a, dgate, dup, dh2 = _ffn_bwd(df, gate, up, whole["w_gate"], whole["w_up"], whole["w_down"])
    large["w_down"] = _matmul_tiles_tn(a, df, "grad_w_down")
    large["w_gate"] = _matmul_tiles_tn(dgate, h2, "grad_w_gate")
    large["w_up"] = _matmul_tiles_tn(dup, h2, "grad_w_up")
    dx2, dmix, dg3, dg2 = _norm_bwd(dh2, dy, x2, mix, ln_pre_ffn, ln_post_mix)
    large["w_out"] = jnp.concatenate([_matmul_tn(pool_out, dmix, "grad_w_out_pool"),
                                      _matmul_tn(attn_out, dmix, "grad_w_out_attn")], axis=0)
    early = _shards_from_whole(large, READY_EARLY)
    dpool, delta, dos, early_theirs = _out_proj_bwd(dmix, whole["w_out"], attn_out, _head_ones(), early)
    early_chip = _add_cores(early, early_theirs, "add_cores_early")
    du, d_w_bd, d_scale = _pool_bwd(u, dpool, w_bd, pool_scale)
    dq, dk, dv, early_others = _attn_bwd(qs, ks, vs, dos, lse, delta, early_chip)
    grad_x, dproj, dg1 = _in_proj_bwd(du, dq, dk, dv, cos_t, sin_t, w_in_whole, xs, dx2, ln_pre_mix)
    d_w_pool = jnp.stack([d_w_bd[g * POOL_GROUP:(g + 1) * POOL_GROUP, g * POOL_GROUP:(g + 1) * POOL_GROUP]
                          for g in range(POOL_WIDTH // POOL_GROUP)])
    small = dict(ln_pre_mix=dg1, ln_post_mix=dg2, ln_pre_ffn=dg3, ln_post_ffn=dg4, pool_scale=d_scale, w_pool=d_w_pool)
    large["w_in"], small_total = _matmul_tn_and_small_sum(dproj, h1, _pack_small(dict(small, loss=loss)), "grad_w_in")
    late = _shards_from_whole(large, READY_LATE)
    late_chip = _add_cores(late, _swap_halves(late), "add_cores_late", bf16)
    late_others = _scatter_to_chips(late_chip)
    early_half = _add_chips(early_chip, early_others, "add_chips_early")
    late_half = _add_chips(late_chip, late_others, "add_chips_late")
    joined = _join_halves(jnp.concatenate([early_half, late_half], axis=0))
    n_early = early_half.shape[0]
    grads = _unpack_shard(joined[:, :n_early].reshape(-1, D_MODEL), READY_EARLY)
    grads.update(_unpack_shard(joined[:, n_early:].reshape(-1, D_MODEL), READY_LATE))

    total = _unpack_small(small_total)
    for n in SMALL:
        grads[n] = total[n]

    delta_w, new_m, new_v = {}, {}, {}
    for n in LARGE:
        delta_w[n], new_m[n], new_v[n] = _adamw(w[n][0], grads[n], m[n][0], v[n][0], "adamw_" + n)
    small_state = [_pack_small(dict({n: s[n] for n in SMALL}, loss=jnp.zeros((), f32))) for s in (w, m, v)]
    small_grad = _pack_small(dict({n: grads[n] for n in SMALL}, loss=jnp.zeros((), f32)))
    sd, sm, sv = _adamw(small_state[0], small_grad, small_state[1], small_state[2], "adamw_small")
    for out, block in ((delta_w, sd), (new_m, sm), (new_v, sv)):
        un = _unpack_small(block)
        for n in SMALL:
            out[n] = un[n]

    names = ("ln_pre_mix", "w_in", "w_pool", "pool_scale", "w_out", "ln_post_mix", "ln_pre_ffn", "w_gate", "w_up",
             "w_down", "ln_post_ffn")
    full = lambda d: [d[n].reshape(w[n].shape) for n in names]
    return (total["loss"], grad_x[None], *full(grads), *full(delta_w), *full(new_m), *full(new_v))
```

```python
import numpy as np
import jax
import jax.numpy as jnp
from jax import lax
from jax.experimental import pallas as pl
from jax.experimental.pallas import tpu as pltpu

D_MODEL = 1024
POOL_WIDTH = 256
POOL_GROUP = 64
ATTN_WIDTH = 768
HEAD_DIM = 64
IN_WIDTH = 2560
D_FF = 2816
BLOCK = 128
DILATIONS = (1, 4, 16)
ROPE_THETA = 10000.0
EPS = 1e-6
ATTN_SCALE = 0.125
NEG = -1e30

ADAM_LR = 0.001
ADAM_B1 = 0.9
ADAM_B2 = 0.999
ADAM_EPS = 1e-08
ADAM_WD = 0.01
ADAM_STEP = 10

N_CHIPS = 4
N_DEV = 8
VMEM_LIMIT_V7X = 56 * 1024 * 1024
MESH = pl.DeviceIdType.MESH

f32 = jnp.float32
bf16 = jnp.bfloat16


def _params(*sem):
    return pltpu.CompilerParams(dimension_semantics=sem, vmem_limit_bytes=VMEM_LIMIT_V7X)


def _dot(a, b):
    return jnp.dot(a, b, preferred_element_type=f32)


def _dot_nt(a, b):
    return lax.dot_general(a, b, (((1,), (1,)), ((), ())), preferred_element_type=f32)


def _dot_tn(a, b):
    return lax.dot_general(a, b, (((0,), (0,)), ((), ())), preferred_element_type=f32)


def _rope_partner(a, first_half):
    return jnp.where(first_half, pltpu.roll(a, 96, 1), pltpu.roll(a, 32, 1))


def _first_half_mask(rows):
    lane = lax.broadcasted_iota(jnp.int32, (rows, 128), 1)
    return (lane % HEAD_DIM) < (HEAD_DIM // 2)


def _stream_spec(d, ts):
    return pl.BlockSpec((d, ts // d, ATTN_WIDTH), lambda i: (0, i, 0))


def _stream_shape(S, d):
    return jax.ShapeDtypeStruct((d, S // d, ATTN_WIDTH), bf16)


N_STAGE = ATTN_WIDTH // 128


def _stage_scratch(ts):
    return [pltpu.VMEM((ts, 128), f32)] * N_STAGE


def _store_streams(stage, out_refs, ts):
    for d, ref in zip(DILATIONS, out_refs):
        for r in range(d):
            rows = pl.ds(0, ts) if d == 1 else pl.ds(r, ts // d, stride=d)
            for j in range(N_STAGE):
                ref[r, :, j * 128:(j + 1) * 128] = stage[j][rows, :].astype(bf16)


def _in_proj(x, g1, w_in, cos_t, sin_t):
    S = x.shape[0]
    ts = 512

    def body(x_ref, g_ref, w_ref, cos_ref, sin_ref, h_ref, u_ref, *rest):
        outs, stage = rest[:-N_STAGE], rest[-N_STAGE:]
        xv = x_ref[...]
        r = lax.rsqrt(jnp.mean(xv * xv, axis=-1, keepdims=True) + EPS)
        h = ((xv * r) * g_ref[...]).astype(bf16)
        h_ref[...] = h
        proj = _dot_nt(h, w_ref[...])
        u_ref[...] = proj[:, :POOL_WIDTH]
        cos = cos_ref[...]
        sin = sin_ref[...]
        first = _first_half_mask(ts)
        n_dil = len(DILATIONS)
        for which, base in enumerate((POOL_WIDTH, POOL_WIDTH + ATTN_WIDTH)):
            for j in range(ATTN_WIDTH // 128):
                a = proj[:, base + j * 128: base + (j + 1) * 128]
                if which == 0:
                    a = a * ATTN_SCALE
                stage[j][...] = a * cos + _rope_partner(a, first) * sin
            _store_streams(stage, outs[which * n_dil:(which + 1) * n_dil], ts)
        for j in range(ATTN_WIDTH // 128):
            base = POOL_WIDTH + 2 * ATTN_WIDTH + j * 128
            stage[j][...] = proj[:, base:base + 128]
        _store_streams(stage, outs[2 * n_dil:], ts)

    row = lambda w: pl.BlockSpec((ts, w), lambda i: (i, 0))
    streams = [_stream_spec(d, ts) for d in DILATIONS] * 3
    res = pl.pallas_call(
        body, name="in_proj", grid=(S // ts,),
        in_specs=[row(D_MODEL), pl.BlockSpec((1, D_MODEL), lambda i: (0, 0)),
                  pl.BlockSpec((IN_WIDTH, D_MODEL), lambda i: (0, 0)), row(128), row(128)],
        out_specs=[row(D_MODEL), row(POOL_WIDTH)] + streams,
        out_shape=[jax.ShapeDtypeStruct((S, D_MODEL), bf16), jax.ShapeDtypeStruct((S, POOL_WIDTH), f32)]
        + [_stream_shape(S, d) for d in DILATIONS] * 3,
        scratch_shapes=_stage_scratch(ts),
        compiler_params=_params("parallel"),
    )(x, g1, w_in, cos_t, sin_t)
    n = len(DILATIONS)
    return res[0], res[1], res[2:2 + n], res[2 + n:2 + 2 * n], res[2 + 2 * n:]


POOL_HALO = 16


def _pool_lane_group(rows):
    return lax.broadcasted_iota(jnp.int32, (rows, POOL_WIDTH), 1) // POOL_GROUP


def _pool_select(group, s2, s4, s8, s16):
    return jnp.where(group == 0, s2, jnp.where(group == 1, s4, jnp.where(group == 2, s8, s16)))


def _pool_count(t0, rows):
    group = _pool_lane_group(rows)
    t = t0 + lax.broadcasted_iota(jnp.int32, (rows, POOL_WIDTH), 0)
    win = _pool_select(group, 2, 4, 8, 16)
    return jnp.minimum(t + 1, win).astype(f32)


def _pool_diff(u_halo, u_tile, t0):
    ts = u_tile.shape[0]
    ext = jnp.concatenate([u_halo, u_tile], axis=0)
    s2 = ext + pltpu.roll(ext, 1, 0)
    s4 = s2 + pltpu.roll(s2, 2, 0)
    s8 = s4 + pltpu.roll(s4, 4, 0)
    s16 = s8 + pltpu.roll(s8, 8, 0)
    group = _pool_lane_group(ts + POOL_HALO)
    wsum = _pool_select(group, s2, s4, s8, s16)[POOL_HALO:]
    return wsum / _pool_count(t0, ts) - u_tile


def _pool_specs(ts, n_tiles):
    tile = pl.BlockSpec((ts, POOL_WIDTH), lambda i: (i, 0))
    per = ts // POOL_HALO
    before = pl.BlockSpec((POOL_HALO, POOL_WIDTH), lambda i: (jnp.maximum(i * per - 1, 0), 0))
    after = pl.BlockSpec((POOL_HALO, POOL_WIDTH), lambda i: (jnp.minimum((i + 1) * per, n_tiles * per - 1), 0))
    return tile, before, after


def _pool_fwd(u, w_bd, scale):
    S = u.shape[0]
    ts = 512
    n_tiles = S // ts

    def body(u_ref, halo_ref, w_ref, sc_ref, y_ref):
        i = pl.program_id(0)
        halo = jnp.where(i > 0, halo_ref[...], 0.0)
        d = _pool_diff(halo, u_ref[...], i * ts)
        y_ref[...] = (_dot(d.astype(bf16), w_ref[...]) * sc_ref[...]).astype(bf16)

    tile, before, _ = _pool_specs(ts, n_tiles)
    return pl.pallas_call(
        body, name="pool_fwd", grid=(n_tiles,),
        in_specs=[tile, before, pl.BlockSpec((POOL_WIDTH, POOL_WIDTH), lambda i: (0, 0)),
                  pl.BlockSpec((1, POOL_WIDTH), lambda i: (0, 0))],
        out_specs=tile, out_shape=jax.ShapeDtypeStruct((S, POOL_WIDTH), bf16),
        compiler_params=_params("parallel"),
    )(u, u, w_bd, scale)


def _pool_bwd(u, dy, w_bd, scale):
    S = u.shape[0]
    ts = 512
    n_tiles = S // ts

    def body(u_ref, halo_ref, dy_ref, dy_next_ref, w_ref, sc_ref, du_ref, dw_ref, dsc_ref):
        i = pl.program_id(0)

        @pl.when(i == 0)
        def _():
            dw_ref[...] = jnp.zeros_like(dw_ref)
            dsc_ref[...] = jnp.zeros_like(dsc_ref)

        halo = jnp.where(i > 0, halo_ref[...], 0.0)
        d = _pool_diff(halo, u_ref[...], i * ts).astype(bf16)
        w = w_ref[...]
        sc = sc_ref[...]
        dy_tile = dy_ref[...]
        z = _dot(d, w)
        dsc_ref[...] += jnp.sum(dy_tile * z, axis=0, keepdims=True)
        dy_next = jnp.where(i < n_tiles - 1, dy_next_ref[...], 0.0)
        dz = (jnp.concatenate([dy_tile, dy_next], axis=0) * sc).astype(bf16)
        dw_ref[...] += _dot_tn(d, dz[:ts])
        dd = _dot_nt(dz, w)
        e = dd / _pool_count(i * ts, ts + POOL_HALO)
        n = ts + POOL_HALO
        f2 = e + pltpu.roll(e, n - 1, 0)
        f4 = f2 + pltpu.roll(f2, n - 2, 0)
        f8 = f4 + pltpu.roll(f4, n - 4, 0)
        f16 = f8 + pltpu.roll(f8, n - 8, 0)
        fsum = _pool_select(_pool_lane_group(n), f2, f4, f8, f16)
        du_ref[...] = (fsum[:ts] - dd[:ts]).astype(bf16)

    tile, before, after = _pool_specs(ts, n_tiles)
    return pl.pallas_call(
        body, name="pool_bwd", grid=(n_tiles,),
        in_specs=[tile, before, tile, after, pl.BlockSpec((POOL_WIDTH, POOL_WIDTH), lambda i: (0, 0)),
                  pl.BlockSpec((1, POOL_WIDTH), lambda i: (0, 0))],
        out_specs=[tile, pl.BlockSpec((POOL_WIDTH, POOL_WIDTH), lambda i: (0, 0)),
                   pl.BlockSpec((1, POOL_WIDTH), lambda i: (0, 0))],
        out_shape=[jax.ShapeDtypeStruct((S, POOL_WIDTH), bf16), jax.ShapeDtypeStruct((POOL_WIDTH, POOL_WIDTH), f32),
                   jax.ShapeDtypeStruct((1, POOL_WIDTH), f32)],
        compiler_params=_params("arbitrary"),
    )(u, u, dy, dy, w_bd, scale)


SUPER = BLOCK * DILATIONS[-1]
UNITS = SUPER // BLOCK
FWD_UNROLL = 16
BWD_UNROLL = 8


def _band_mask(has_prev):
    qi = lax.broadcasted_iota(jnp.int32, (BLOCK, 2 * BLOCK), 0)
    kj = lax.broadcasted_iota(jnp.int32, (BLOCK, 2 * BLOCK), 1)
    return (kj >= qi) & (kj <= qi + BLOCK) & ((kj >= BLOCK) | has_prev)


def _head0_mask(rows=BLOCK):
    return lax.broadcasted_iota(jnp.int32, (rows, 128), 1) < HEAD_DIM


def _band_mask_t(has_prev):
    ki = lax.broadcasted_iota(jnp.int32, (2 * BLOCK, 2 * BLOCK), 0)
    qj = lax.broadcasted_iota(jnp.int32, (2 * BLOCK, 2 * BLOCK), 1) % BLOCK
    return (ki >= qj) & (ki <= qj + BLOCK) & ((ki >= BLOCK) | has_prev)


def _head_pair_rows(a, h0):
    zero = jnp.zeros_like(a)
    return jnp.concatenate([jnp.where(h0, a, zero), jnp.where(h0, zero, a)], axis=0)


def _per_query_row(stat):
    t = stat.T
    return jnp.concatenate([t[0:1], t[HEAD_DIM:HEAD_DIM + 1]], axis=1)


def _natural_rows(d, r, n):
    if d == 1:
        return pl.ds(pl.multiple_of(n * BLOCK, BLOCK), BLOCK)
    return pl.ds(n * (BLOCK * d) + r, BLOCK, stride=d)


def _unit_place(d, u):
    per_stream = UNITS // d
    return u // per_stream, u % per_stream, per_stream


def _block_rows(n):
    return pl.ds(pl.multiple_of(n * BLOCK, BLOCK), BLOCK)


def _band(cur_ref, tail_ref, r, n):
    before = jnp.where(n > 0, cur_ref[r, _block_rows(jnp.maximum(n - 1, 0)), :], tail_ref[r])
    return jnp.concatenate([before, cur_ref[r, _block_rows(n), :]], axis=0)


def _attn_in_specs(S, with_do):
    specs = []
    last = S // SUPER - 1
    for d in DILATIONS:
        per_stream = UNITS // d
        cur = pl.BlockSpec((d, SUPER // d, 128), lambda hp, sb: (0, jnp.minimum(sb, last), hp))
        tail = pl.BlockSpec(
            (d, BLOCK, 128),
            lambda hp, sb, per_stream=per_stream: (0, jnp.maximum(jnp.minimum(sb, last) * per_stream - 1, 0), hp))
        specs += [cur] * (2 if with_do else 1) + [cur, tail, cur, tail]
    return specs


def _attn_fwd(qs, ks, vs, pack):
    S = qs[0].shape[1]
    n_dil = len(DILATIONS)
    n_steps = S // SUPER
    n_total = (ATTN_WIDTH // 128) * n_steps

    def body(*refs):
        ins, pack_ref = refs[:5 * n_dil], refs[5 * n_dil]
        out_ref, lse_ref, gathered_ref = refs[5 * n_dil + 1:5 * n_dil + 4]
        scratch = refs[5 * n_dil + 4:]
        o_sc, l_sc = scratch[:n_dil], scratch[n_dil:2 * n_dil]
        gather = _Gather(pack_ref, gathered_ref, *scratch[2 * n_dil:])
        sb = pl.program_id(1)
        step = pl.program_id(0) * n_steps + sb

        @pl.when(step == 0)
        def _():
            gather.start()

        h0 = _head0_mask()
        for ci, d in enumerate(DILATIONS):
            q_ref, kc_ref, kp_ref, vc_ref, vp_ref = ins[5 * ci:5 * ci + 5]

            def unit(u, carry, d=d, ci=ci, q_ref=q_ref, kc_ref=kc_ref, kp_ref=kp_ref, vc_ref=vc_ref, vp_ref=vp_ref):
                r, n, _ = _unit_place(d, u)
                qv = q_ref[r, _block_rows(n), :]
                kb = _band(kc_ref, kp_ref, r, n)
                vb = _band(vc_ref, vp_ref, r, n)
                valid = _band_mask((sb > 0) | (n > 0))
                outs, lses = [], []
                for h in range(2):
                    keep = h0 if h == 0 else jnp.logical_not(h0)
                    qh = jnp.where(keep, qv, jnp.zeros_like(qv))
                    s = jnp.where(valid, _dot_nt(qh, kb), NEG)
                    m = jnp.max(s, axis=1, keepdims=True)
                    e = jnp.exp(s - m)
                    den = jnp.sum(e, axis=1, keepdims=True)
                    outs.append(_dot(e.astype(bf16), vb) * (1.0 / den))
                    lses.append(jnp.broadcast_to(m + jnp.log(den), (BLOCK, 128)))
                rows = _natural_rows(d, r, n)
                o_sc[ci][rows, :] = jnp.where(h0, outs[0], outs[1])
                l_sc[ci][rows, :] = jnp.where(h0, lses[0], lses[1])
                return carry

            lax.fori_loop(0, UNITS, unit, 0, unroll=FWD_UNROLL)

        def merge(t, carry):
            rows = pl.ds(pl.multiple_of(t * 256, 256), 256)
            a, b, c = l_sc[0][rows, :], l_sc[1][rows, :], l_sc[2][rows, :]
            m = jnp.maximum(jnp.maximum(a, b), c)
            ea, eb, ec = jnp.exp(a - m), jnp.exp(b - m), jnp.exp(c - m)
            tot = ea + eb + ec
            out_ref[rows, :] = ((ea / tot) * o_sc[0][rows, :] + (eb / tot) * o_sc[1][rows, :]
                                + (ec / tot) * o_sc[2][rows, :]).astype(bf16)
            lse_ref[rows, :] = m + jnp.log(tot)
            return carry

        lax.fori_loop(0, SUPER // 256, merge, 0)

        @pl.when(step == (2 * n_total) // 3)
        def _():
            gather.pass_on()

        @pl.when(step == n_total - 1)
        def _():
            gather.finish()

    args = []
    for q, k, v in zip(qs, ks, vs):
        args += [q, k, k, v, v]
    nat = pl.BlockSpec((SUPER, 128), lambda hp, sb: (sb, hp))
    rows = pack.shape[0]
    return pl.pallas_call(
        body, name="attn_fwd", grid=(ATTN_WIDTH // 128, n_steps),
        in_specs=_attn_in_specs(S, False) + [ANY], out_specs=[nat, nat, ANY],
        out_shape=[jax.ShapeDtypeStruct((S, ATTN_WIDTH), bf16), jax.ShapeDtypeStruct((S, ATTN_WIDTH), f32),
                   _Gather.out_shape(rows, pack.dtype)],
        scratch_shapes=[pltpu.VMEM((SUPER, 128), f32)] * (2 * n_dil) + _Gather.scratch(rows, pack.dtype),
        compiler_params=_params("arbitrary", "arbitrary"),
    )(*args, pack)


def _attn_bwd(qs, ks, vs, dos, lse, delta, chip_sum):
    S = qs[0].shape[1]
    n_steps = S // SUPER
    last = n_steps - 1
    n_dil = len(DILATIONS)
    n_total = (ATTN_WIDTH // 128) * (n_steps + 1)

    def body(*refs):
        ins, (lse_ref, dl_ref, sum_ref) = refs[:6 * n_dil], refs[6 * n_dil:6 * n_dil + 3]
        dq_ref, dk_ref, dv_ref, others_ref = refs[6 * n_dil + 3:6 * n_dil + 7]
        dq_acc, dk_acc, dv_acc = refs[6 * n_dil + 7:6 * n_dil + 10]
        scatter = _Scatter(sum_ref, others_ref, *refs[6 * n_dil + 10:])
        sb = pl.program_id(1)
        step = pl.program_id(0) * (n_steps + 1) + sb
        cur = sb % 2
        prv = 1 - cur

        @pl.when(step == 0)
        def _():
            scatter.start()

        @pl.when(sb < n_steps)
        def _():
            dq_acc[...] = jnp.zeros_like(dq_acc)
            dk_acc[cur] = jnp.zeros((SUPER, 128), f32)
            dv_acc[cur] = jnp.zeros((SUPER, 128), f32)
            h0 = _head0_mask()
            for ci, d in enumerate(DILATIONS):
                q_ref, do_ref, kc_ref, kp_ref, vc_ref, vp_ref = ins[6 * ci:6 * ci + 6]

                def unit(u, carry, d=d, q_ref=q_ref, do_ref=do_ref, kc_ref=kc_ref, kp_ref=kp_ref, vc_ref=vc_ref,
                         vp_ref=vp_ref):
                    r, n, per_stream = _unit_place(d, u)
                    qv = q_ref[r, _block_rows(n), :]
                    dov = do_ref[r, _block_rows(n), :]
                    kb = _band(kc_ref, kp_ref, r, n)
                    vb = _band(vc_ref, vp_ref, r, n)
                    rows = _natural_rows(d, r, n)
                    has_prev = (sb > 0) | (n > 0)
                    q_pair = _head_pair_rows(qv, h0)
                    do_pair = _head_pair_rows(dov, h0)
                    s_t = jnp.where(_band_mask_t(has_prev), _dot_nt(kb, q_pair), NEG)
                    p_t = jnp.exp(s_t - _per_query_row(lse_ref[rows, :]))
                    dp_t = _dot_nt(vb, do_pair)
                    ds_t = (p_t * (dp_t - _per_query_row(dl_ref[rows, :]))).astype(bf16)
                    dvb = _dot(p_t.astype(bf16), do_pair)
                    dkb = _dot(ds_t, q_pair)
                    dq_pair = _dot_tn(ds_t, kb)
                    dq_acc[rows, :] += jnp.where(h0, dq_pair[:BLOCK], dq_pair[BLOCK:])
                    dk_acc[cur, rows, :] += dkb[BLOCK:]
                    dv_acc[cur, rows, :] += dvb[BLOCK:]

                    slot = jnp.where((n > 0) | (sb == 0), cur, prv)
                    before = _natural_rows(d, r, jnp.where(n > 0, n - 1, per_stream - 1))
                    dk_acc[slot, before, :] += dkb[:BLOCK]
                    dv_acc[slot, before, :] += dvb[:BLOCK]
                    return carry

                lax.fori_loop(0, UNITS, unit, 0, unroll=BWD_UNROLL)
            dq_ref[...] = (dq_acc[...] * ATTN_SCALE).astype(bf16)

        @pl.when(sb > 0)
        def _():
            dk_ref[...] = dk_acc[prv].astype(bf16)
            dv_ref[...] = dv_acc[prv].astype(bf16)

        @pl.when(step == n_total - 1)
        def _():
            scatter.finish()

    args = []
    for q, k, v, do in zip(qs, ks, vs, dos):
        args += [q, do, k, k, v, v]
    nat = pl.BlockSpec((SUPER, 128), lambda hp, sb: (jnp.minimum(sb, last), hp))
    nat_before = pl.BlockSpec((SUPER, 128), lambda hp, sb: (jnp.clip(sb - 1, 0, last), hp))
    out = jax.ShapeDtypeStruct((S, ATTN_WIDTH), bf16)
    half = chip_sum.shape[1]
    return pl.pallas_call(
        body, name="attn_bwd", grid=(ATTN_WIDTH // 128, n_steps + 1),
        in_specs=_attn_in_specs(S, True) + [nat, nat, ANY], out_specs=[nat, nat_before, nat_before, ANY],
        out_shape=[out, out, out, _Scatter.out_shape(half, chip_sum.dtype)],
        scratch_shapes=[pltpu.VMEM((SUPER, 128), f32), pltpu.VMEM((2, SUPER, 128), f32),
                        pltpu.VMEM((2, SUPER, 128), f32)] + _Scatter.scratch(half),
        compiler_params=_params("arbitrary", "arbitrary"),
    )(*args, lse, delta, chip_sum)


def _rms(v):
    return lax.rsqrt(jnp.mean(v * v, axis=-1, keepdims=True) + EPS)


def _out_proj(pool_out, attn_out, w_out, x, g2, g3):
    S = x.shape[0]
    ts = 512

    def body(p_ref, a_ref, w_ref, x_ref, g2_ref, g3_ref, mix_ref, x2_ref, h2_ref):
        mix = _dot(p_ref[...], w_ref[:POOL_WIDTH, :]) + _dot(a_ref[...], w_ref[POOL_WIDTH:, :])
        mix_ref[...] = mix
        x2 = x_ref[...] + (mix * _rms(mix)) * g2_ref[...]
        x2_ref[...] = x2
        h2_ref[...] = ((x2 * _rms(x2)) * g3_ref[...]).astype(bf16)

    row = lambda w: pl.BlockSpec((ts, w), lambda i: (i, 0))
    gain = pl.BlockSpec((1, D_MODEL), lambda i: (0, 0))
    return pl.pallas_call(
        body, name="out_proj", grid=(S // ts,),
        in_specs=[row(POOL_WIDTH), row(ATTN_WIDTH), pl.BlockSpec((D_MODEL, D_MODEL), lambda i: (0, 0)),
                  row(D_MODEL), gain, gain],
        out_specs=[row(D_MODEL)] * 3,
        out_shape=[jax.ShapeDtypeStruct((S, D_MODEL), f32), jax.ShapeDtypeStruct((S, D_MODEL), f32),
                   jax.ShapeDtypeStruct((S, D_MODEL), bf16)],
        compiler_params=_params("parallel"),
    )(pool_out, attn_out, w_out, x, g2, g3)


FF_TILE = 256
FF_STEP_ROWS = 2048
FF_ROWS = 512
FF_BWD_ROWS = 256


def _sigmoid(g):
    return 1.0 / (1.0 + jnp.exp(-g))


def _ff_act_shape(S):
    return jax.ShapeDtypeStruct((D_FF // FF_TILE, S, FF_TILE), bf16)


def _ff_act_spec(ts):
    return pl.BlockSpec((1, ts, FF_TILE), lambda i, j: (j, i, 0))


def _ffn_fwd(h2, w_gate, w_up, w_down):
    S = h2.shape[0]
    ts = min(S, FF_STEP_ROWS)

    def body(h_ref, wg_ref, wu_ref, wd_ref, gate_ref, up_ref, f_ref):
        def rows_pass(first):
            def sub(i, carry):
                rows = pl.ds(pl.multiple_of(i * FF_ROWS, FF_ROWS), FF_ROWS)
                h = h_ref[rows, :]
                gate = _dot_nt(h, wg_ref[...])
                up = _dot_nt(h, wu_ref[...])
                gate_ref[0, rows, :] = gate.astype(bf16)
                up_ref[0, rows, :] = up.astype(bf16)
                part = _dot((gate * _sigmoid(gate) * up).astype(bf16), wd_ref[...])
                if first:
                    f_ref[rows, :] = part
                else:
                    f_ref[rows, :] += part
                return carry

            lax.fori_loop(0, ts // FF_ROWS, sub, 0, unroll=True)

        @pl.when(pl.program_id(1) == 0)
        def _():
            rows_pass(True)

        @pl.when(pl.program_id(1) > 0)
        def _():
            rows_pass(False)

    act = _ff_act_spec(ts)
    weight = pl.BlockSpec((FF_TILE, D_MODEL), lambda i, j: (j, 0))
    return pl.pallas_call(
        body, name="ffn_fwd", grid=(S // ts, D_FF // FF_TILE),
        in_specs=[pl.BlockSpec((ts, D_MODEL), lambda i, j: (i, 0)), weight, weight, weight],
        out_specs=[act, act, pl.BlockSpec((ts, D_MODEL), lambda i, j: (i, 0))],
        out_shape=[_ff_act_shape(S), _ff_act_shape(S), jax.ShapeDtypeStruct((S, D_MODEL), f32)],
        compiler_params=_params("parallel", "arbitrary"),
    )(h2, w_gate, w_up, w_down)


def _loss_head(f, x2, target, g4):
    S = f.shape[0]
    ts = 512

    def body(f_ref, x2_ref, t_ref, g_ref, dy_ref, df_ref, dg_ref, loss_ref):
        @pl.when(pl.program_id(0) == 0)
        def _():
            dg_ref[...] = jnp.zeros_like(dg_ref)
            loss_ref[...] = jnp.zeros_like(loss_ref)

        fv = f_ref[...]
        g = g_ref[...]
        r = _rms(fv)
        fhat = fv * r
        err = (x2_ref[...] + fhat * g) - t_ref[...]
        loss_ref[...] += 0.5 * jnp.sum(jnp.mean(err * err, axis=-1, keepdims=True), axis=0, keepdims=True)
        dy = err * (1.0 / D_MODEL)
        dy_ref[...] = dy
        dg_ref[...] += jnp.sum(dy * fhat, axis=0, keepdims=True)
        dyg = dy * g
        df_ref[...] = (r * (dyg - fhat * jnp.mean(dyg * fhat, axis=-1, keepdims=True))).astype(bf16)

    row = pl.BlockSpec((ts, D_MODEL), lambda i: (i, 0))
    gain = pl.BlockSpec((1, D_MODEL), lambda i: (0, 0))
    return pl.pallas_call(
        body, name="loss_head", grid=(S // ts,), in_specs=[row, row, row, gain],
        out_specs=[row, row, gain, pl.BlockSpec((1, 1), lambda i: (0, 0))],
        out_shape=[jax.ShapeDtypeStruct((S, D_MODEL), f32), jax.ShapeDtypeStruct((S, D_MODEL), bf16),
                   jax.ShapeDtypeStruct((1, D_MODEL), f32), jax.ShapeDtypeStruct((1, 1), f32)],
        compiler_params=_params("arbitrary"),
    )(f, x2, target, g4)


def _ffn_bwd(df, gate, up, w_gate, w_up, w_down):
    S = df.shape[0]
    ts = min(S, FF_STEP_ROWS)

    def body(df_ref, gate_ref, up_ref, wg_ref, wu_ref, wd_ref, a_ref, dgate_ref, dup_ref, dh_ref):
        def rows_pass(first):
            def sub(i, carry):
                rows = pl.ds(pl.multiple_of(i * FF_BWD_ROWS, FF_BWD_ROWS), FF_BWD_ROWS)
                da = _dot_nt(df_ref[rows, :], wd_ref[...])
                g = gate_ref[0, rows, :].astype(f32)
                u = up_ref[0, rows, :].astype(f32)
                sig = _sigmoid(g)
                silu = g * sig
                a_ref[0, rows, :] = (silu * u).astype(bf16)
                dup = (da * silu).astype(bf16)
                dgate = (da * u * (sig * (1.0 + g * (1.0 - sig)))).astype(bf16)
                dup_ref[0, rows, :] = dup
                dgate_ref[0, rows, :] = dgate
                part = _dot(dgate, wg_ref[...]) + _dot(dup, wu_ref[...])
                if first:
                    dh_ref[rows, :] = part
                else:
                    dh_ref[rows, :] += part
                return carry

            lax.fori_loop(0, ts // FF_BWD_ROWS, sub, 0, unroll=True)

        @pl.when(pl.program_id(1) == 0)
        def _():
            rows_pass(True)

        @pl.when(pl.program_id(1) > 0)
        def _():
            rows_pass(False)

    act = _ff_act_spec(ts)
    row = pl.BlockSpec((ts, D_MODEL), lambda i, j: (i, 0))
    return pl.pallas_call(
        body, name="ffn_bwd", grid=(S // ts, D_FF // FF_TILE),
        in_specs=[row, act, act,
                  pl.BlockSpec((FF_TILE, D_MODEL), lambda i, j: (j, 0)),
                  pl.BlockSpec((FF_TILE, D_MODEL), lambda i, j: (j, 0)),
                  pl.BlockSpec((FF_TILE, D_MODEL), lambda i, j: (j, 0))],
        out_specs=[act, act, act, row],
        out_shape=[_ff_act_shape(S)] * 3 + [jax.ShapeDtypeStruct((S, D_MODEL), f32)],
        compiler_params=_params("parallel", "arbitrary"),
    )(df, gate, up, w_gate, w_up, w_down)


def _norm_bwd(dh2, dy, x2, mix, g3, g2):
    S = dh2.shape[0]
    ts = 512

    def body(dh_ref, dy_ref, x2_ref, mix_ref, g3_ref, g2_ref, dx2_ref, dmix_ref, dg3_ref, dg2_ref):
        @pl.when(pl.program_id(0) == 0)
        def _():
            dg3_ref[...] = jnp.zeros_like(dg3_ref)
            dg2_ref[...] = jnp.zeros_like(dg2_ref)

        dh = dh_ref[...]
        x2 = x2_ref[...]
        r3 = _rms(x2)
        xhat = x2 * r3
        dg3_ref[...] += jnp.sum(dh * xhat, axis=0, keepdims=True)
        dhg = dh * g3_ref[...]
        dx2 = dy_ref[...] + r3 * (dhg - xhat * jnp.mean(dhg * xhat, axis=-1, keepdims=True))
        dx2_ref[...] = dx2
        mix = mix_ref[...]
        r2 = _rms(mix)
        mhat = mix * r2
        dg2_ref[...] += jnp.sum(dx2 * mhat, axis=0, keepdims=True)
        dmg = dx2 * g2_ref[...]
        dmix_ref[...] = (r2 * (dmg - mhat * jnp.mean(dmg * mhat, axis=-1, keepdims=True))).astype(bf16)

    row = pl.BlockSpec((ts, D_MODEL), lambda i: (i, 0))
    gain = pl.BlockSpec((1, D_MODEL), lambda i: (0, 0))
    return pl.pallas_call(
        body, name="norm_bwd", grid=(S // ts,), in_specs=[row, row, row, row, gain, gain],
        out_specs=[row, row, gain, gain],
        out_shape=[jax.ShapeDtypeStruct((S, D_MODEL), f32), jax.ShapeDtypeStruct((S, D_MODEL), bf16),
                   jax.ShapeDtypeStruct((1, D_MODEL), f32), jax.ShapeDtypeStruct((1, D_MODEL), f32)],
        compiler_params=_params("arbitrary"),
    )(dh2, dy, x2, mix, g3, g2)


def _out_proj_bwd(dmix, w_out, attn_out, head_ones, grads):
    S = dmix.shape[0]
    ts = 512
    n_dil = len(DILATIONS)

    def body(dm_ref, w_ref, o_ref, ones_ref, g_ref, dp_ref, dl_ref, *rest):
        do_refs, theirs_ref = rest[:n_dil], rest[n_dil]
        stage = rest[n_dil + 1:n_dil + 1 + N_STAGE]
        swap = _Swap(g_ref, theirs_ref, *rest[n_dil + 1 + N_STAGE:])

        @pl.when(pl.program_id(0) == 0)
        def _():
            swap.start()

        @pl.when(pl.program_id(0) == S // ts - 1)
        def _():
            swap.finish()

        dcat = _dot_nt(dm_ref[...], w_ref[...])
        dp_ref[...] = dcat[:, :POOL_WIDTH]
        do = dcat[:, POOL_WIDTH:]
        for j in range(ATTN_WIDTH // 128):
            stage[j][...] = do[:, j * 128:(j + 1) * 128]
        _store_streams(stage, do_refs, ts)
        prod = do * o_ref[...].astype(f32)
        hi = prod.astype(bf16)
        lo = (prod - hi.astype(f32)).astype(bf16)
        ones = ones_ref[...]
        for j in range(ATTN_WIDTH // 128):
            cols = slice(j * 128, (j + 1) * 128)
            dl_ref[:, cols] = _dot(hi[:, cols], ones) + _dot(lo[:, cols], ones)

    row = lambda w: pl.BlockSpec((ts, w), lambda i: (i, 0))
    res = pl.pallas_call(
        body, name="out_proj_bwd", grid=(S // ts,),
        in_specs=[row(D_MODEL), pl.BlockSpec((D_MODEL, D_MODEL), lambda i: (0, 0)), row(ATTN_WIDTH),
                  pl.BlockSpec((128, 128), lambda i: (0, 0)), ANY],
        out_specs=[row(POOL_WIDTH), row(ATTN_WIDTH)] + [_stream_spec(d, ts) for d in DILATIONS] + [ANY],
        out_shape=[jax.ShapeDtypeStruct((S, POOL_WIDTH), f32), jax.ShapeDtypeStruct((S, ATTN_WIDTH), f32)]
        + [_stream_shape(S, d) for d in DILATIONS] + [_Swap.out_shape(grads)],
        scratch_shapes=_stage_scratch(ts) + _Swap.scratch(grads),
        compiler_params=_params("arbitrary"),
    )(dmix, w_out, attn_out, head_ones, grads)
    return res[0], res[1], res[2:2 + n_dil], res[2 + n_dil]


def _in_proj_bwd(du, dq, dk, dv, cos_t, sin_t, w_in, x, dx2, g1):
    S = x.shape[0]
    ts = 512

    def body(du_ref, dq_ref, dk_ref, dv_ref, cos_ref, sin_ref, w_ref, x_ref, dx2_ref, g_ref, gx_ref, dproj_ref, dg_ref):
        @pl.when(pl.program_id(0) == 0)
        def _():
            dg_ref[...] = jnp.zeros_like(dg_ref)

        dproj_ref[:, :POOL_WIDTH] = du_ref[...]
        cos = cos_ref[...]
        sin = sin_ref[...]
        first = _first_half_mask(ts)
        for j in range(ATTN_WIDTH // 128):
            cols = slice(j * 128, (j + 1) * 128)
            for base, ref in ((POOL_WIDTH, dq_ref), (POOL_WIDTH + ATTN_WIDTH, dk_ref)):
                g = ref[:, cols].astype(f32)
                pre = g * cos + _rope_partner(g * sin, first)
                dproj_ref[:, base + j * 128: base + (j + 1) * 128] = pre.astype(bf16)
        dproj_ref[:, POOL_WIDTH + 2 * ATTN_WIDTH:] = dv_ref[...]

        dh = _dot(dproj_ref[...], w_ref[...])
        xv = x_ref[...]
        r = _rms(xv)
        xhat = xv * r
        dg_ref[...] += jnp.sum(dh * xhat, axis=0, keepdims=True)
        dhg = dh * g_ref[...]
        gx_ref[...] = dx2_ref[...] + r * (dhg - xhat * jnp.mean(dhg * xhat, axis=-1, keepdims=True))

    row = lambda w: pl.BlockSpec((ts, w), lambda i: (i, 0))
    gain = pl.BlockSpec((1, D_MODEL), lambda i: (0, 0))
    return pl.pallas_call(
        body, name="in_proj_bwd", grid=(S // ts,),
        in_specs=[row(POOL_WIDTH)] + [row(ATTN_WIDTH)] * 3 + [row(128), row(128),
                  pl.BlockSpec((IN_WIDTH, D_MODEL), lambda i: (0, 0)), row(D_MODEL), row(D_MODEL), gain],
        out_specs=[row(D_MODEL), row(IN_WIDTH), gain],
        out_shape=[jax.ShapeDtypeStruct((S, D_MODEL), f32), jax.ShapeDtypeStruct((S, IN_WIDTH), bf16),
                   jax.ShapeDtypeStruct((1, D_MODEL), f32)],
        compiler_params=_params("arbitrary"),
    )(du, dq, dk, dv, cos_t, sin_t, w_in, x, dx2, g1)


def _matmul_tiles_tn(a, b, name):
    T, K, w = a.shape
    N = b.shape[1]
    tk = 1024

    def body(a_ref, b_ref, o_ref):
        def tiles_pass(first):
            for t in range(T):
                part = _dot_tn(a_ref[t], b_ref[...])
                if first:
                    o_ref[t * w:(t + 1) * w, :] = part
                else:
                    o_ref[t * w:(t + 1) * w, :] += part

        @pl.when(pl.program_id(0) == 0)
        def _():
            tiles_pass(True)

        @pl.when(pl.program_id(0) > 0)
        def _():
            tiles_pass(False)

    return pl.pallas_call(
        body, name=name, grid=(K // tk,),
        in_specs=[pl.BlockSpec((T, tk, w), lambda k: (0, k, 0)), pl.BlockSpec((tk, N), lambda k: (k, 0))],
        out_specs=pl.BlockSpec((T * w, N), lambda k: (0, 0)),
        out_shape=jax.ShapeDtypeStruct((T * w, N), f32),
        compiler_params=_params("arbitrary"),
    )(a, b)


def _matmul_tn(a, b, name):
    K, M = a.shape
    N = b.shape[1]
    tk = 1024

    def body(a_ref, b_ref, o_ref):
        _tn_step(a_ref, b_ref, o_ref, M)

    return pl.pallas_call(
        body, name=name, grid=(K // tk,),
        in_specs=[pl.BlockSpec((tk, M), lambda k: (k, 0)), pl.BlockSpec((tk, N), lambda k: (k, 0))],
        out_specs=pl.BlockSpec((M, N), lambda k: (0, 0)),
        out_shape=jax.ShapeDtypeStruct((M, N), f32),
        compiler_params=_params("arbitrary"),
    )(a, b)


def _tn_step(a_ref, b_ref, o_ref, M):
    w = 256

    def tiles_pass(first):
        for t in range(M // w):
            part = _dot_tn(a_ref[:, t * w:(t + 1) * w], b_ref[...])
            if first:
                o_ref[t * w:(t + 1) * w, :] = part
            else:
                o_ref[t * w:(t + 1) * w, :] += part

    @pl.when(pl.program_id(0) == 0)
    def _():
        tiles_pass(True)

    @pl.when(pl.program_id(0) > 0)
    def _():
        tiles_pass(False)


def _matmul_tn_and_small_sum(a, b, block, name):
    K, M = a.shape
    N = b.shape[1]
    tk = 1024
    n_steps = K // tk

    def body(a_ref, b_ref, block_ref, o_ref, total_ref, *scratch):
        small = _SmallSum(block_ref, *scratch)

        @pl.when(pl.program_id(0) == 0)
        def _():
            small.start()

        _tn_step(a_ref, b_ref, o_ref, M)

        @pl.when(pl.program_id(0) == n_steps - 1)
        def _():
            small.finish(total_ref)

    return pl.pallas_call(
        body, name=name, grid=(n_steps,),
        in_specs=[pl.BlockSpec((tk, M), lambda k: (k, 0)), pl.BlockSpec((tk, N), lambda k: (k, 0)), ANY],
        out_specs=[pl.BlockSpec((M, N), lambda k: (0, 0)), pl.BlockSpec(block.shape, lambda k: (0, 0))],
        out_shape=[jax.ShapeDtypeStruct((M, N), f32), jax.ShapeDtypeStruct(block.shape, block.dtype)],
        scratch_shapes=_SmallSum.scratch(block),
        compiler_params=_params("arbitrary"),
    )(a, b, block)


def _rope_tables(S):
    half = HEAD_DIM // 2
    freqs = ROPE_THETA ** (-jnp.arange(half, dtype=f32) * (2.0 / HEAD_DIM))
    ang = jnp.arange(S).astype(f32)[:, None] * freqs[None, :]
    cos = jnp.tile(jnp.cos(ang), (1, 4))
    sin = jnp.sin(ang)
    sin = jnp.tile(jnp.concatenate([-sin, sin], axis=1), (1, 2))
    return cos, sin


def _block_diag(w_pool):
    w = jnp.zeros((POOL_WIDTH, POOL_WIDTH), w_pool.dtype)
    for g in range(POOL_WIDTH // POOL_GROUP):
        w = lax.dynamic_update_slice(w, w_pool[g], (g * POOL_GROUP, g * POOL_GROUP))
    return w


def _head_ones():
    head = np.arange(128) // HEAD_DIM
    return jnp.asarray(head[:, None] == head[None, :], dtype=bf16)


def _place():
    x, y, c = lax.axis_index("x"), lax.axis_index("y"), lax.axis_index("c")
    chips = [(1 - x, y), (x, 1 - y), (1 - x, 1 - y)]
    return x, y, c, chips


ANY = pl.BlockSpec(memory_space=pl.ANY)
N_PEER_CHIPS = N_CHIPS - 1
ICI_PIECES = 4
D2D_PIECES = 8
LOCAL_PIECES = 8


def _row_chunks(rows, n, unit=32):
    units = rows // unit
    out, start = [], 0
    for i in range(n):
        size = (units // n + (1 if i < units % n else 0)) * unit
        out.append((start, size))
        start += size
    return [piece for piece in out if piece[1]]


class _LocalCopy:
    def __init__(self, src_rows, dst_rows, rows, buf, sems_in, sems_out):
        self.loads, self.stores = [], []
        for i, (start, size) in enumerate(_row_chunks(rows, LOCAL_PIECES)):
            r = pl.ds(start, size)
            self.loads.append(pltpu.make_async_copy(src_rows(r), buf.at[r], sems_in.at[i]))
            self.stores.append(pltpu.make_async_copy(buf.at[r], dst_rows(r), sems_out.at[i]))

    def start(self):
        for cp in self.loads:
            cp.start()

    def pass_on(self):
        for load, store in zip(self.loads, self.stores):
            load.wait()
            store.start()

    def finish(self):
        for store in self.stores:
            store.wait()

    @staticmethod
    def scratch(rows, dtype):
        return [pltpu.VMEM((rows, D_MODEL), dtype), pltpu.SemaphoreType.DMA((LOCAL_PIECES,)),
                pltpu.SemaphoreType.DMA((LOCAL_PIECES,))]


class _Gather:
    def __init__(self, w_ref, out_ref, send1, recv1, send2, recv2, buf, sems_in, sems_out):
        x, y, c, chips = _place()
        me = 2 * x + y
        rows = w_ref.shape[0]
        half = rows // 2
        pieces = _row_chunks(half, ICI_PIECES)
        self.own = _LocalCopy(lambda r: w_ref.at[r], lambda r: out_ref.at[me, r], rows, buf, sems_in, sems_out)

        def rows_of(core, piece):
            start, size = piece
            return pl.ds(core * half + start, size)

        self.sends, self.arrivals, self.forwards, self.forward_arrivals = [], [], [], []
        for i, piece in enumerate(pieces):
            for j, (cx, cy) in enumerate(chips):
                k = j * len(pieces) + i
                there = 2 * cx + cy

                def direct(src_chip, cx=cx, cy=cy, k=k, piece=piece):
                    return pltpu.make_async_remote_copy(
                        src_ref=w_ref.at[rows_of(c, piece)], dst_ref=out_ref.at[src_chip, rows_of(c, piece)],
                        send_sem=send1.at[k], recv_sem=recv1.at[k], device_id=(cx, cy, c), device_id_type=MESH)

                def passed(core, there=there, k=k, piece=piece):
                    return pltpu.make_async_remote_copy(
                        src_ref=out_ref.at[there, rows_of(core, piece)], dst_ref=out_ref.at[there, rows_of(core, piece)],
                        send_sem=send2.at[k], recv_sem=recv2.at[k], device_id=(x, y, 1 - c), device_id_type=MESH)

                self.sends.append(direct(me))
                self.arrivals.append(direct(there))
                self.forwards.append(passed(c))
                self.forward_arrivals.append(passed(1 - c))

    def start(self):
        for cp in self.sends:
            cp.start()
        self.own.start()

    def pass_on(self):
        self.own.pass_on()
        for arrival, forward in zip(self.arrivals, self.forwards):
            arrival.wait_recv()
            forward.start()

    def finish(self):
        for arrival in self.forward_arrivals:
            arrival.wait_recv()
        for cp in self.sends + self.forwards:
            cp.wait_send()
        self.own.finish()

    @staticmethod
    def scratch(rows, dtype):
        n = N_PEER_CHIPS * len(_row_chunks(rows // 2, ICI_PIECES))
        return [pltpu.SemaphoreType.DMA((n,))] * 4 + _LocalCopy.scratch(rows, dtype)

    @staticmethod
    def out_shape(rows, dtype):
        return jax.ShapeDtypeStruct((N_CHIPS, rows, D_MODEL), dtype)


def _gather_weights(pack):
    rows = pack.shape[0]

    def body(w_ref, out_ref, *scratch):
        gather = _Gather(w_ref, out_ref, *scratch)
        gather.start()
        gather.pass_on()
        gather.finish()

    return pl.pallas_call(
        body, name="gather_weights", in_specs=[ANY], out_specs=ANY, out_shape=_Gather.out_shape(rows, pack.dtype),
        scratch_shapes=_Gather.scratch(rows, pack.dtype),
        compiler_params=pltpu.CompilerParams(vmem_limit_bytes=VMEM_LIMIT_V7X),
    )(pack)


class _Scatter:
    def __init__(self, h_ref, out_ref, send, recv):
        x, y, c, chips = _place()
        pieces = _row_chunks(h_ref.shape[1], ICI_PIECES)
        self.copies = []
        for i, (start, size) in enumerate(pieces):
            for j, (cx, cy) in enumerate(chips):
                k = j * len(pieces) + i
                self.copies.append(pltpu.make_async_remote_copy(
                    src_ref=h_ref.at[2 * cx + cy, pl.ds(start, size)], dst_ref=out_ref.at[j, pl.ds(start, size)],
                    send_sem=send.at[k], recv_sem=recv.at[k], device_id=(cx, cy, c), device_id_type=MESH))

    def start(self):
        for cp in self.copies:
            cp.start()

    def finish(self):
        for cp in self.copies:
            cp.wait_recv()
        for cp in self.copies:
            cp.wait_send()

    @staticmethod
    def scratch(half):
        n = N_PEER_CHIPS * len(_row_chunks(half, ICI_PIECES))
        return [pltpu.SemaphoreType.DMA((n,))] * 2

    @staticmethod
    def out_shape(half, dtype):
        return jax.ShapeDtypeStruct((N_PEER_CHIPS, half, D_MODEL), dtype)


def _scatter_to_chips(h):
    half = h.shape[1]

    def body(h_ref, out_ref, send, recv):
        scatter = _Scatter(h_ref, out_ref, send, recv)
        scatter.start()
        scatter.finish()

    return pl.pallas_call(
        body, name="scatter_to_chips", in_specs=[ANY], out_specs=ANY, out_shape=_Scatter.out_shape(half, h.dtype),
        scratch_shapes=_Scatter.scratch(half),
    )(h)


class _Swap:
    def __init__(self, g_ref, theirs_ref, send, recv):
        x, y, c, _ = _place()
        half = g_ref.shape[1] // 2
        pieces = _row_chunks(half, D2D_PIECES)
        self.copies = []
        for s in range(N_CHIPS):
            for i, (start, size) in enumerate(pieces):
                k = s * len(pieces) + i
                self.copies.append(pltpu.make_async_remote_copy(
                    src_ref=g_ref.at[s, pl.ds((1 - c) * half + start, size)], dst_ref=theirs_ref.at[s, pl.ds(start, size)],
                    send_sem=send.at[k], recv_sem=recv.at[k], device_id=(x, y, 1 - c), device_id_type=MESH))

    def start(self):
        for cp in self.copies:
            cp.start()

    def finish(self):
        for cp in self.copies:
            cp.wait()

    @staticmethod
    def scratch(g):
        n = N_CHIPS * len(_row_chunks(g.shape[1] // 2, D2D_PIECES))
        return [pltpu.SemaphoreType.DMA((n,))] * 2

    @staticmethod
    def out_shape(g):
        return jax.ShapeDtypeStruct((N_CHIPS, g.shape[1] // 2, D_MODEL), g.dtype)


def _swap_halves(g):
    def body(g_ref, theirs_ref, send, recv):
        swap = _Swap(g_ref, theirs_ref, send, recv)
        swap.start()
        swap.finish()

    return pl.pallas_call(
        body, name="swap_halves", in_specs=[ANY], out_specs=ANY, out_shape=_Swap.out_shape(g),
        scratch_shapes=_Swap.scratch(g),
    )(g)


ADD_TILE_MAX_ROWS = 600


def _add_tile(half):
    return max(t for t in range(8, ADD_TILE_MAX_ROWS + 1, 8) if half % t == 0)


def _add_cores(g, theirs, name, out_dtype=f32):
    half = theirs.shape[1]
    tr = _add_tile(half)
    n_t = half // tr

    def body(c_ref, g_ref, t_ref, o_ref):
        o_ref[...] = (g_ref[...] + t_ref[...]).astype(out_dtype)

    blk = pl.BlockSpec((1, tr, D_MODEL), lambda s, t, c_ref: (s, t, 0))
    return pl.pallas_call(
        body, name=name,
        grid_spec=pltpu.PrefetchScalarGridSpec(
            num_scalar_prefetch=1, grid=(N_CHIPS, n_t),
            in_specs=[pl.BlockSpec((1, tr, D_MODEL), lambda s, t, c_ref: (s, c_ref[0] * n_t + t, 0)), blk],
            out_specs=blk),
        out_shape=jax.ShapeDtypeStruct(theirs.shape, out_dtype),
        compiler_params=_params("parallel", "parallel"),
    )(lax.axis_index("c").astype(jnp.int32).reshape(1), g, theirs)


def _add_chips(chip_sum, others, name):
    half = chip_sum.shape[1]
    tr = _add_tile(half)

    def body(me_ref, own_ref, o0, o1, o2, out_ref):
        out_ref[...] = ((own_ref[0].astype(f32) + o0[0].astype(f32)) + o1[0].astype(f32)) + o2[0].astype(f32)

    other = lambda j: pl.BlockSpec((1, tr, D_MODEL), lambda t, me_ref: (j, t, 0))
    return pl.pallas_call(
        body, name=name,
        grid_spec=pltpu.PrefetchScalarGridSpec(
            num_scalar_prefetch=1, grid=(half // tr,),
            in_specs=[pl.BlockSpec((1, tr, D_MODEL), lambda t, me_ref: (me_ref[0], t, 0)), other(0), other(1), other(2)],
            out_specs=pl.BlockSpec((tr, D_MODEL), lambda t, me_ref: (t, 0))),
        out_shape=jax.ShapeDtypeStruct((half, D_MODEL), f32),
        compiler_params=_params("parallel"),
    )((2 * lax.axis_index("x") + lax.axis_index("y")).astype(jnp.int32).reshape(1), chip_sum, others, others, others)


def _join_halves(r):
    half = r.shape[0]
    pieces = _row_chunks(half, 2 * D2D_PIECES)
    n = len(pieces)

    def body(r_ref, out_ref, send, recv, buf, sems_in, sems_out):
        x, y, c, _ = _place()
        own = _LocalCopy(lambda rr: r_ref.at[rr], lambda rr: out_ref.at[c, rr], half, buf, sems_in, sems_out)
        own.start()

        def piece(i, core):
            start, size = pieces[i]
            return pltpu.make_async_remote_copy(
                src_ref=r_ref.at[pl.ds(start, size)], dst_ref=out_ref.at[core, pl.ds(start, size)],
                send_sem=send.at[i], recv_sem=recv.at[i], device_id=(x, y, 1 - c), device_id_type=MESH)

        copies = [piece(i, c) for i in range(n)]
        for cp in copies:
            cp.start()
        own.pass_on()
        for i in range(n):
            piece(i, 1 - c).wait_recv()
        for cp in copies:
            cp.wait_send()
        own.finish()

    return pl.pallas_call(
        body, name="join_halves", in_specs=[ANY], out_specs=ANY,
        out_shape=jax.ShapeDtypeStruct((2,) + r.shape, r.dtype),
        scratch_shapes=[pltpu.SemaphoreType.DMA((n,))] * 2 + _LocalCopy.scratch(half, r.dtype),
        compiler_params=pltpu.CompilerParams(vmem_limit_bytes=VMEM_LIMIT_V7X),
    )(r)


class _SmallSum:
    def __init__(self, b_ref, gathered, send, recv, local_sem):
        x, y, c, _ = _place()
        me = 4 * x + 2 * y + c
        self.gathered = gathered
        self.own = pltpu.make_async_copy(b_ref, gathered.at[me], local_sem)
        self.sends, self.arrivals = [], []
        for kk in range(1, N_DEV):
            flip = lambda v, bit: 1 - v if bit else v
            peer = (flip(x, kk & 4), flip(y, kk & 2), flip(c, kk & 1))
            self.sends.append(pltpu.make_async_remote_copy(
                src_ref=b_ref, dst_ref=gathered.at[me], send_sem=send.at[kk - 1], recv_sem=recv.at[kk - 1],
                device_id=peer, device_id_type=MESH))
            self.arrivals.append(pltpu.make_async_remote_copy(
                src_ref=b_ref, dst_ref=gathered.at[jnp.bitwise_xor(me, kk)], send_sem=send.at[kk - 1],
                recv_sem=recv.at[kk - 1], device_id=peer, device_id_type=MESH))

    def start(self):
        self.own.start()
        for cp in self.sends:
            cp.start()

    def finish(self, out_ref):
        self.own.wait()
        for cp in self.arrivals:
            cp.wait_recv()
        for cp in self.sends:
            cp.wait_send()
        acc = self.gathered[0]
        for dev in range(1, N_DEV):
            acc = acc + self.gathered[dev]
        out_ref[...] = acc

    @staticmethod
    def scratch(block):
        return [pltpu.VMEM((N_DEV,) + block.shape, block.dtype), pltpu.SemaphoreType.DMA((N_DEV - 1,)),
                pltpu.SemaphoreType.DMA((N_DEV - 1,)), pltpu.SemaphoreType.DMA]


def _adamw(w, g, m, v, name):
    rows, cols = w.shape
    tr = rows
    for cand in (512, 256, 128, 64, 32, 16, 8):
        if rows % cand == 0:
            tr = cand
            break
    c1 = 1.0 - ADAM_B1 ** ADAM_STEP
    c2 = 1.0 - ADAM_B2 ** ADAM_STEP

    def body(w_ref, g_ref, m_ref, v_ref, d_ref, nm_ref, nv_ref):
        gv = g_ref[...]
        nm = ADAM_B1 * m_ref[...] + (1.0 - ADAM_B1) * gv
        nv = ADAM_B2 * v_ref[...] + (1.0 - ADAM_B2) * (gv * gv)
        nm_ref[...] = nm
        nv_ref[...] = nv
        d_ref[...] = -ADAM_LR * ((nm / c1) / (jnp.sqrt(nv / c2) + ADAM_EPS) + ADAM_WD * w_ref[...])

    blk = pl.BlockSpec((tr, cols), lambda i: (i, 0))
    shape = jax.ShapeDtypeStruct((rows, cols), f32)
    return pl.pallas_call(
        body, name=name, grid=(rows // tr,), in_specs=[blk] * 4, out_specs=[blk] * 3, out_shape=[shape] * 3,
        compiler_params=_params("parallel"),
    )(w, g, m, v)


LARGE = ("w_in", "w_out", "w_gate", "w_up", "w_down")
SMALL = ("ln_pre_mix", "ln_post_mix", "ln_pre_ffn", "ln_post_ffn", "pool_scale", "w_pool")
SHARD_ROWS = {"w_in": 640, "w_out": 256, "w_gate": 704, "w_up": 704, "w_down": 704}
COLUMN_SHARDED = ("w_in", "w_gate", "w_up")
NEEDED_FIRST = ("w_in",)
NEEDED_LATER = ("w_out", "w_gate", "w_up", "w_down")
READY_EARLY = ("w_out", "w_gate", "w_up", "w_down")
READY_LATE = ("w_in",)


def _pack_shard(shards, names):
    return jnp.concatenate([shards[n].T if n in COLUMN_SHARDED else shards[n] for n in names], axis=0)


def _unpack_shard(pack, names):
    out, row = {}, 0
    for n in names:
        part = pack[row:row + SHARD_ROWS[n]]
        out[n] = part.T if n in COLUMN_SHARDED else part
        row += SHARD_ROWS[n]
    return out


def _whole_from_shards(packs, names):
    out, row = {}, 0
    for n in names:
        rows = SHARD_ROWS[n]
        out[n] = packs[:, row:row + rows].reshape(N_CHIPS * rows, D_MODEL)
        row += rows
    return out


def _shards_from_whole(grads, names):
    return jnp.concatenate([grads[n].reshape(N_CHIPS, SHARD_ROWS[n], D_MODEL) for n in names], axis=1)


def _pack_small(vals):
    rows = [vals[n].reshape(1, D_MODEL) for n in SMALL[:4]]
    rows.append(jnp.pad(vals["pool_scale"].reshape(1, POOL_WIDTH), ((0, 0), (0, D_MODEL - POOL_WIDTH))))
    rows.append(jnp.pad(vals["loss"].reshape(1, 1), ((0, 0), (0, D_MODEL - 1))))
    rows.append(jnp.zeros((2, D_MODEL), f32))
    rows.append(vals["w_pool"].reshape(16, D_MODEL))
    return jnp.concatenate(rows, axis=0)


def _unpack_small(block):
    out = {n: block[i:i + 1] for i, n in enumerate(SMALL[:4])}
    out["pool_scale"] = block[4:5, :POOL_WIDTH]
    out["loss"] = block[5, 0]
    out["w_pool"] = block[8:24].reshape(1, 4, POOL_GROUP, POOL_GROUP)
    return out


def kernel(x, ln_pre_mix, w_in, w_pool, pool_scale, w_out, ln_post_mix, ln_pre_ffn, w_gate, w_up, w_down, ln_post_ffn, loss_target, m_ln_pre_mix, m_w_in, m_w_pool, m_pool_scale, m_w_out, m_ln_post_mix, m_ln_pre_ffn, m_w_gate, m_w_up, m_w_down, m_ln_post_ffn, v_ln_pre_mix, v_w_in, v_w_pool, v_pool_scale, v_w_out, v_ln_post_mix, v_ln_pre_ffn, v_w_gate, v_w_up, v_w_down, v_ln_post_ffn):
    w = dict(ln_pre_mix=ln_pre_mix, w_in=w_in, w_pool=w_pool, pool_scale=pool_scale, w_out=w_out,
             ln_post_mix=ln_post_mix, ln_pre_ffn=ln_pre_ffn, w_gate=w_gate, w_up=w_up, w_down=w_down,
             ln_post_ffn=ln_post_ffn)
    m = dict(ln_pre_mix=m_ln_pre_mix, w_in=m_w_in, w_pool=m_w_pool, pool_scale=m_pool_scale, w_out=m_w_out,
             ln_post_mix=m_ln_post_mix, ln_pre_ffn=m_ln_pre_ffn, w_gate=m_w_gate, w_up=m_w_up, w_down=m_w_down,
             ln_post_ffn=m_ln_post_ffn)
    v = dict(ln_pre_mix=v_ln_pre_mix, w_in=v_w_in, w_pool=v_w_pool, pool_scale=v_pool_scale, w_out=v_w_out,
             ln_post_mix=v_ln_post_mix, ln_pre_ffn=v_ln_pre_ffn, w_gate=v_w_gate, w_up=v_w_up, w_down=v_w_down,
             ln_post_ffn=v_ln_post_ffn)

    xs, target = x[0], loss_target[0]
    cos_t, sin_t = _rope_tables(xs.shape[0])
    w_bd = _block_diag(w_pool[0]).astype(bf16)
    shard = {n: w[n][0].astype(bf16) for n in LARGE}

    w_in_whole = _whole_from_shards(_gather_weights(_pack_shard(shard, NEEDED_FIRST)), NEEDED_FIRST)["w_in"]
    h1, u, qs, ks, vs = _in_proj(xs, ln_pre_mix, w_in_whole, cos_t, sin_t)
    pool_out = _pool_fwd(u, w_bd, pool_scale)
    attn_out, lse, later = _attn_fwd(qs, ks, vs, _pack_shard(shard, NEEDED_LATER))
    whole = _whole_from_shards(later, NEEDED_LATER)
    mix, x2, h2 = _out_proj(pool_out, attn_out, whole["w_out"], xs, ln_post_mix, ln_pre_ffn)
    gate, up, f = _ffn_fwd(h2, whole["w_gate"], whole["w_up"], whole["w_down"])
    dy, df, dg4, loss = _loss_head(f, x2, target, ln_post_ffn)

    large = {}
    a, dgate, dup, dh2 = _ffn_bwd(df, gate, up, whole["w_gate"], whole["w_up"], whole["w_down"])
    large["w_down"] = _matmul_tiles_tn(a, df, "grad_w_down")
    large["w_gate"] = _matmul_tiles_tn(dgate, h2, "grad_w_gate")
    large["w_up"] = _matmul_tiles_tn(dup, h2, "grad_w_up")
    dx2, dmix, dg3, dg2 = _norm_bwd(dh2, dy, x2, mix, ln_pre_ffn, ln_post_mix)
    large["w_out"] = jnp.concatenate([_matmul_tn(pool_out, dmix, "grad_w_out_pool"),
                                      _matmul_tn(attn_out, dmix, "grad_w_out_attn")], axis=0)
    early = _shards_from_whole(large, READY_EARLY)
    dpool, delta, dos, early_theirs = _out_proj_bwd(dmix, whole["w_out"], attn_out, _head_ones(), early)
    early_chip = _add_cores(early, early_theirs, "add_cores_early")
    du, d_w_bd, d_scale = _pool_bwd(u, dpool, w_bd, pool_scale)
    dq, dk, dv, early_others = _attn_bwd(qs, ks, vs, dos, lse, delta, early_chip)
    grad_x, dproj, dg1 = _in_proj_bwd(du, dq, dk, dv, cos_t, sin_t, w_in_whole, xs, dx2, ln_pre_mix)
    d_w_pool = jnp.stack([d_w_bd[g * POOL_GROUP:(g + 1) * POOL_GROUP, g * POOL_GROUP:(g + 1) * POOL_GROUP]
                          for g in range(POOL_WIDTH // POOL_GROUP)])
    small = dict(ln_pre_mix=dg1, ln_post_mix=dg2, ln_pre_ffn=dg3, ln_post_ffn=dg4, pool_scale=d_scale, w_pool=d_w_pool)
    large["w_in"], small_total = _matmul_tn_and_small_sum(dproj, h1, _pack_small(dict(small, loss=loss)), "grad_w_in")
    late = _shards_from_whole(large, READY_LATE)
    late_chip = _add_cores(late, _swap_halves(late), "add_cores_late", bf16)
    late_others = _scatter_to_chips(late_chip)
    early_half = _add_chips(early_chip, early_others, "add_chips_early")
    late_half = _add_chips(late_chip, late_others, "add_chips_late")
    joined = _join_halves(jnp.concatenate([early_half, late_half], axis=0))
    n_early = early_half.shape[0]
    grads = _unpack_shard(joined[:, :n_early].reshape(-1, D_MODEL), READY_EARLY)
    grads.update(_unpack_shard(joined[:, n_early:].reshape(-1, D_MODEL), READY_LATE))

    total = _unpack_small(small_total)
    for n in SMALL:
        grads[n] = total[n]

    delta_w, new_m, new_v = {}, {}, {}
    for n in LARGE:
        delta_w[n], new_m[n], new_v[n] = _adamw(w[n][0], grads[n], m[n][0], v[n][0], "adamw_" + n)
    small_state = [_pack_small(dict({n: s[n] for n in SMALL}, loss=jnp.zeros((), f32))) for s in (w, m, v)]
    small_grad = _pack_small(dict({n: grads[n] for n in SMALL}, loss=jnp.zeros((), f32)))
    sd, sm, sv = _adamw(small_state[0], small_grad, small_state[1], small_state[2], "adamw_small")
    for out, block in ((delta_w, sd), (new_m, sm), (new_v, sv)):
        un = _unpack_small(block)
        for n in SMALL:
            out[n] = un[n]

    names = ("ln_pre_mix", "w_in", "w_pool", "pool_scale", "w_out", "ln_post_mix", "ln_pre_ffn", "w_gate", "w_up",
             "w_down", "ln_post_ffn")
    full = lambda d: [d[n].reshape(w[n].shape) for n in names]
    return (total["loss"], grad_x[None], *full(grads), *full(delta_w), *full(new_m), *full(new_v))
```

```python
import numpy as np
import jax
import jax.numpy as jnp
from jax import lax
from jax.experimental import pallas as pl
from jax.experimental.pallas import tpu as pltpu

D_MODEL = 1024
POOL_WIDTH = 256
POOL_GROUP = 64
ATTN_WIDTH = 768
HEAD_DIM = 64
IN_WIDTH = 2560
D_FF = 2816
BLOCK = 128
DILATIONS = (1, 4, 16)
ROPE_THETA = 10000.0
EPS = 1e-6
ATTN_SCALE = 0.125
NEG = -1e30

ADAM_LR = 0.001
ADAM_B1 = 0.9
ADAM_B2 = 0.999
ADAM_EPS = 1e-08
ADAM_WD = 0.01
ADAM_STEP = 10

N_CHIPS = 4
N_DEV = 8
VMEM_LIMIT_V7X = 56 * 1024 * 1024
MESH = pl.DeviceIdType.MESH

f32 = jnp.float32
bf16 = jnp.bfloat16


def _params(*sem):
    return pltpu.CompilerParams(dimension_semantics=sem, vmem_limit_bytes=VMEM_LIMIT_V7X)


def _dot(a, b):
    return jnp.dot(a, b, preferred_element_type=f32)


def _dot_nt(a, b):
    return lax.dot_general(a, b, (((1,), (1,)), ((), ())), preferred_element_type=f32)


def _dot_tn(a, b):
    return lax.dot_general(a, b, (((0,), (0,)), ((), ())), preferred_element_type=f32)


def _rope_partner(a, first_half):
    return jnp.where(first_half, pltpu.roll(a, 96, 1), pltpu.roll(a, 32, 1))


def _first_half_mask(rows):
    lane = lax.broadcasted_iota(jnp.int32, (rows, 128), 1)
    return (lane % HEAD_DIM) < (HEAD_DIM // 2)


def _stream_spec(d, ts):
    return pl.BlockSpec((d, ts // d, ATTN_WIDTH), lambda i: (0, i, 0))


def _stream_shape(S, d):
    return jax.ShapeDtypeStruct((d, S // d, ATTN_WIDTH), bf16)


N_STAGE = ATTN_WIDTH // 128


def _stage_scratch(ts):
    return [pltpu.VMEM((ts, 128), f32)] * N_STAGE


def _store_streams(stage, out_refs, ts):
    for d, ref in zip(DILATIONS, out_refs):
        for r in range(d):
            rows = pl.ds(0, ts) if d == 1 else pl.ds(r, ts // d, stride=d)
            for j in range(N_STAGE):
                ref[r, :, j * 128:(j + 1) * 128] = stage[j][rows, :].astype(bf16)


def _in_proj(x, g1, w_in, cos_t, sin_t):
    S = x.shape[0]
    ts = 512

    def body(x_ref, g_ref, w_ref, cos_ref, sin_ref, h_ref, u_ref, *rest):
        outs, stage = rest[:-N_STAGE], rest[-N_STAGE:]
        xv = x_ref[...]
        r = lax.rsqrt(jnp.mean(xv * xv, axis=-1, keepdims=True) + EPS)
        h = ((xv * r) * g_ref[...]).astype(bf16)
        h_ref[...] = h
        proj = _dot_nt(h, w_ref[...])
        u_ref[...] = proj[:, :POOL_WIDTH]
        cos = cos_ref[...]
        sin = sin_ref[...]
        first = _first_half_mask(ts)
        n_dil = len(DILATIONS)
        for which, base in enumerate((POOL_WIDTH, POOL_WIDTH + ATTN_WIDTH)):
            for j in range(ATTN_WIDTH // 128):
                a = proj[:, base + j * 128: base + (j + 1) * 128]
                if which == 0:
                    a = a * ATTN_SCALE
                stage[j][...] = a * cos + _rope_partner(a, first) * sin
            _store_streams(stage, outs[which * n_dil:(which + 1) * n_dil], ts)
        for j in range(ATTN_WIDTH // 128):
            base = POOL_WIDTH + 2 * ATTN_WIDTH + j * 128
            stage[j][...] = proj[:, base:base + 128]
        _store_streams(stage, outs[2 * n_dil:], ts)

    row = lambda w: pl.BlockSpec((ts, w), lambda i: (i, 0))
    streams = [_stream_spec(d, ts) for d in DILATIONS] * 3
    res = pl.pallas_call(
        body, name="in_proj", grid=(S // ts,),
        in_specs=[row(D_MODEL), pl.BlockSpec((1, D_MODEL), lambda i: (0, 0)),
                  pl.BlockSpec((IN_WIDTH, D_MODEL), lambda i: (0, 0)), row(128), row(128)],
        out_specs=[row(D_MODEL), row(POOL_WIDTH)] + streams,
        out_shape=[jax.ShapeDtypeStruct((S, D_MODEL), bf16), jax.ShapeDtypeStruct((S, POOL_WIDTH), f32)]
        + [_stream_shape(S, d) for d in DILATIONS] * 3,
        scratch_shapes=_stage_scratch(ts),
        compiler_params=_params("parallel"),
    )(x, g1, w_in, cos_t, sin_t)
    n = len(DILATIONS)
    return res[0], res[1], res[2:2 + n], res[2 + n:2 + 2 * n], res[2 + 2 * n:]


POOL_HALO = 16


def _pool_lane_group(rows):
    return lax.broadcasted_iota(jnp.int32, (rows, POOL_WIDTH), 1) // POOL_GROUP


def _pool_select(group, s2, s4, s8, s16):
    return jnp.where(group == 0, s2, jnp.where(group == 1, s4, jnp.where(group == 2, s8, s16)))


def _pool_count(t0, rows):
    group = _pool_lane_group(rows)
    t = t0 + lax.broadcasted_iota(jnp.int32, (rows, POOL_WIDTH), 0)
    win = _pool_select(group, 2, 4, 8, 16)
    return jnp.minimum(t + 1, win).astype(f32)


def _pool_diff(u_halo, u_tile, t0):
    ts = u_tile.shape[0]
    ext = jnp.concatenate([u_halo, u_tile], axis=0)
    s2 = ext + pltpu.roll(ext, 1, 0)
    s4 = s2 + pltpu.roll(s2, 2, 0)
    s8 = s4 + pltpu.roll(s4, 4, 0)
    s16 = s8 + pltpu.roll(s8, 8, 0)
    group = _pool_lane_group(ts + POOL_HALO)
    wsum = _pool_select(group, s2, s4, s8, s16)[POOL_HALO:]
    return wsum / _pool_count(t0, ts) - u_tile


def _pool_specs(ts, n_tiles):
    tile = pl.BlockSpec((ts, POOL_WIDTH), lambda i: (i, 0))
    per = ts // POOL_HALO
    before = pl.BlockSpec((POOL_HALO, POOL_WIDTH), lambda i: (jnp.maximum(i * per - 1, 0), 0))
    after = pl.BlockSpec((POOL_HALO, POOL_WIDTH), lambda i: (jnp.minimum((i + 1) * per, n_tiles * per - 1), 0))
    return tile, before, after


def _pool_fwd(u, w_bd, scale):
    S = u.shape[0]
    ts = 512
    n_tiles = S // ts

    def body(u_ref, halo_ref, w_ref, sc_ref, y_ref):
        i = pl.program_id(0)
        halo = jnp.where(i > 0, halo_ref[...], 0.0)
        d = _pool_diff(halo, u_ref[...], i * ts)
        y_ref[...] = (_dot(d.astype(bf16), w_ref[...]) * sc_ref[...]).astype(bf16)

    tile, before, _ = _pool_specs(ts, n_tiles)
    return pl.pallas_call(
        body, name="pool_fwd", grid=(n_tiles,),
        in_specs=[tile, before, pl.BlockSpec((POOL_WIDTH, POOL_WIDTH), lambda i: (0, 0)),
                  pl.BlockSpec((1, POOL_WIDTH), lambda i: (0, 0))],
        out_specs=tile, out_shape=jax.ShapeDtypeStruct((S, POOL_WIDTH), bf16),
        compiler_params=_params("parallel"),
    )(u, u, w_bd, scale)


def _pool_bwd(u, dy, w_bd, scale):
    S = u.shape[0]
    ts = 512
    n_tiles = S // ts

    def body(u_ref, halo_ref, dy_ref, dy_next_ref, w_ref, sc_ref, du_ref, dw_ref, dsc_ref):
        i = pl.program_id(0)

        @pl.when(i == 0)
        def _():
            dw_ref[...] = jnp.zeros_like(dw_ref)
            dsc_ref[...] = jnp.zeros_like(dsc_ref)

        halo = jnp.where(i > 0, halo_ref[...], 0.0)
        d = _pool_diff(halo, u_ref[...], i * ts).astype(bf16)
        w = w_ref[...]
        sc = sc_ref[...]
        dy_tile = dy_ref[...]
        z = _dot(d, w)
        dsc_ref[...] += jnp.sum(dy_tile * z, axis=0, keepdims=True)
        dy_next = jnp.where(i < n_tiles - 1, dy_next_ref[...], 0.0)
        dz = (jnp.concatenate([dy_tile, dy_next], axis=0) * sc).astype(bf16)
        dw_ref[...] += _dot_tn(d, dz[:ts])
        dd = _dot_nt(dz, w)
        e = dd / _pool_count(i * ts, ts + POOL_HALO)
        n = ts + POOL_HALO
        f2 = e + pltpu.roll(e, n - 1, 0)
        f4 = f2 + pltpu.roll(f2, n - 2, 0)
        f8 = f4 + pltpu.roll(f4, n - 4, 0)
        f16 = f8 + pltpu.roll(f8, n - 8, 0)
        fsum = _pool_select(_pool_lane_group(n), f2, f4, f8, f16)
        du_ref[...] = (fsum[:ts] - dd[:ts]).astype(bf16)

    tile, before, after = _pool_specs(ts, n_tiles)
    return pl.pallas_call(
        body, name="pool_bwd", grid=(n_tiles,),
        in_specs=[tile, before, tile, after, pl.BlockSpec((POOL_WIDTH, POOL_WIDTH), lambda i: (0, 0)),
                  pl.BlockSpec((1, POOL_WIDTH), lambda i: (0, 0))],
        out_specs=[tile, pl.BlockSpec((POOL_WIDTH, POOL_WIDTH), lambda i: (0, 0)),
                   pl.BlockSpec((1, POOL_WIDTH), lambda i: (0, 0))],
        out_shape=[jax.ShapeDtypeStruct((S, POOL_WIDTH), bf16), jax.ShapeDtypeStruct((POOL_WIDTH, POOL_WIDTH), f32),
                   jax.ShapeDtypeStruct((1, POOL_WIDTH), f32)],
        compiler_params=_params("arbitrary"),
    )(u, u, dy, dy, w_bd, scale)


SUPER = BLOCK * DILATIONS[-1]
UNITS = SUPER // BLOCK
FWD_UNROLL = 16
BWD_UNROLL = 8


def _band_mask(has_prev):
    qi = lax.broadcasted_iota(jnp.int32, (BLOCK, 2 * BLOCK), 0)
    kj = lax.broadcasted_iota(jnp.int32, (BLOCK, 2 * BLOCK), 1)
    return (kj >= qi) & (kj <= qi + BLOCK) & ((kj >= BLOCK) | has_prev)


def _head0_mask(rows=BLOCK):
    return lax.broadcasted_iota(jnp.int32, (rows, 128), 1) < HEAD_DIM


def _band_mask_t(has_prev):
    ki = lax.broadcasted_iota(jnp.int32, (2 * BLOCK, 2 * BLOCK), 0)
    qj = lax.broadcasted_iota(jnp.int32, (2 * BLOCK, 2 * BLOCK), 1) % BLOCK
    return (ki >= qj) & (ki <= qj + BLOCK) & ((ki >= BLOCK) | has_prev)


def _head_pair_rows(a, h0):
    zero = jnp.zeros_like(a)
    return jnp.concatenate([jnp.where(h0, a, zero), jnp.where(h0, zero, a)], axis=0)


def _per_query_row(stat):
    t = stat.T
    return jnp.concatenate([t[0:1], t[HEAD_DIM:HEAD_DIM + 1]], axis=1)


def _natural_rows(d, r, n):
    if d == 1:
        return pl.ds(pl.multiple_of(n * BLOCK, BLOCK), BLOCK)
    return pl.ds(n * (BLOCK * d) + r, BLOCK, stride=d)


def _unit_place(d, u):
    per_stream = UNITS // d
    return u // per_stream, u % per_stream, per_stream


def _block_rows(n):
    return pl.ds(pl.multiple_of(n * BLOCK, BLOCK), BLOCK)


def _band(cur_ref, tail_ref, r, n):
    before = jnp.where(n > 0, cur_ref[r, _block_rows(jnp.maximum(n - 1, 0)), :], tail_ref[r])
    return jnp.concatenate([before, cur_ref[r, _block_rows(n), :]], axis=0)


def _attn_in_specs(S, with_do):
    specs = []
    last = S // SUPER - 1
    for d in DILATIONS:
        per_stream = UNITS // d
        cur = pl.BlockSpec((d, SUPER // d, 128), lambda hp, sb: (0, jnp.minimum(sb, last), hp))
        tail = pl.BlockSpec(
            (d, BLOCK, 128),
            lambda hp, sb, per_stream=per_stream: (0, jnp.maximum(jnp.minimum(sb, last) * per_stream - 1, 0), hp))
        specs += [cur] * (2 if with_do else 1) + [cur, tail, cur, tail]
    return specs


def _attn_fwd(qs, ks, vs, pack):
    S = qs[0].shape[1]
    n_dil = len(DILATIONS)
    n_steps = S // SUPER
    n_total = (ATTN_WIDTH // 128) * n_steps

    def body(*refs):
        ins, pack_ref = refs[:5 * n_dil], refs[5 * n_dil]
        out_ref, lse_ref, gathered_ref = refs[5 * n_dil + 1:5 * n_dil + 4]
        scratch = refs[5 * n_dil + 4:]
        o_sc, l_sc = scratch[:n_dil], scratch[n_dil:2 * n_dil]
        gather = _Gather(pack_ref, gathered_ref, *scratch[2 * n_dil:])
        sb = pl.program_id(1)
        step = pl.program_id(0) * n_steps + sb

        @pl.when(step == 0)
        def _():
            gather.start()

        h0 = _head0_mask()
        for ci, d in enumerate(DILATIONS):
            q_ref, kc_ref, kp_ref, vc_ref, vp_ref = ins[5 * ci:5 * ci + 5]

            def unit(u, carry, d=d, ci=ci, q_ref=q_ref, kc_ref=kc_ref, kp_ref=kp_ref, vc_ref=vc_ref, vp_ref=vp_ref):
                r, n, _ = _unit_place(d, u)
                qv = q_ref[r, _block_rows(n), :]
                kb = _band(kc_ref, kp_ref, r, n)
                vb = _band(vc_ref, vp_ref, r, n)
                valid = _band_mask((sb > 0) | (n > 0))
                outs, lses = [], []
                for h in range(2):
                    keep = h0 if h == 0 else jnp.logical_not(h0)
                    qh = jnp.where(keep, qv, jnp.zeros_like(qv))
                    s = jnp.where(valid, _dot_nt(qh, kb), NEG)
                    m = jnp.max(s, axis=1, keepdims=True)
                    e = jnp.exp(s - m)
                    den = jnp.sum(e, axis=1, keepdims=True)
                    outs.append(_dot(e.astype(bf16), vb) * (1.0 / den))
                    lses.append(jnp.broadcast_to(m + jnp.log(den), (BLOCK, 128)))
                rows = _natural_rows(d, r, n)
                o_sc[ci][rows, :] = jnp.where(h0, outs[0], outs[1])
                l_sc[ci][rows, :] = jnp.where(h0, lses[0], lses[1])
                return carry

            lax.fori_loop(0, UNITS, unit, 0, unroll=FWD_UNROLL)

        def merge(t, carry):
            rows = pl.ds(pl.multiple_of(t * 256, 256), 256)
            a, b, c = l_sc[0][rows, :], l_sc[1][rows, :], l_sc[2][rows, :]
            m = jnp.maximum(jnp.maximum(a, b), c)
            ea, eb, ec = jnp.exp(a - m), jnp.exp(b - m), jnp.exp(c - m)
            tot = ea + eb + ec
            out_ref[rows, :] = ((ea / tot) * o_sc[0][rows, :] + (eb / tot) * o_sc[1][rows, :]
                                + (ec / tot) * o_sc[2][rows, :]).astype(bf16)
            lse_ref[rows, :] = m + jnp.log(tot)
            return carry

        lax.fori_loop(0, SUPER // 256, merge, 0)

        @pl.when(step == (2 * n_total) // 3)
        def _():
            gather.pass_on()

        @pl.when(step == n_total - 1)
        def _():
            gather.finish()

    args = []
    for q, k, v in zip(qs, ks, vs):
        args += [q, k, k, v, v]
    nat = pl.BlockSpec((SUPER, 128), lambda hp, sb: (sb, hp))
    rows = pack.shape[0]
    return pl.pallas_call(
        body, name="attn_fwd", grid=(ATTN_WIDTH // 128, n_steps),
        in_specs=_attn_in_specs(S, False) + [ANY], out_specs=[nat, nat, ANY],
        out_shape=[jax.ShapeDtypeStruct((S, ATTN_WIDTH), bf16), jax.ShapeDtypeStruct((S, ATTN_WIDTH), f32),
                   _Gather.out_shape(rows, pack.dtype)],
        scratch_shapes=[pltpu.VMEM((SUPER, 128), f32)] * (2 * n_dil) + _Gather.scratch(rows, pack.dtype),
        compiler_params=_params("arbitrary", "arbitrary"),
    )(*args, pack)


def _attn_bwd(qs, ks, vs, dos, lse, delta, chip_sum):
    S = qs[0].shape[1]
    n_steps = S // SUPER
    last = n_steps - 1
    n_dil = len(DILATIONS)
    n_total = (ATTN_WIDTH // 128) * (n_steps + 1)

    def body(*refs):
        ins, (lse_ref, dl_ref, sum_ref) = refs[:6 * n_dil], refs[6 * n_dil:6 * n_dil + 3]
        dq_ref, dk_ref, dv_ref, others_ref = refs[6 * n_dil + 3:6 * n_dil + 7]
        dq_acc, dk_acc, dv_acc = refs[6 * n_dil + 7:6 * n_dil + 10]
        scatter = _Scatter(sum_ref, others_ref, *refs[6 * n_dil + 10:])
        sb = pl.program_id(1)
        step = pl.program_id(0) * (n_steps + 1) + sb
        cur = sb % 2
        prv = 1 - cur

        @pl.when(step == 0)
        def _():
            scatter.start()

        @pl.when(sb < n_steps)
        def _():
            dq_acc[...] = jnp.zeros_like(dq_acc)
            dk_acc[cur] = jnp.zeros((SUPER, 128), f32)
            dv_acc[cur] = jnp.zeros((SUPER, 128), f32)
            h0 = _head0_mask()
            for ci, d in enumerate(DILATIONS):
                q_ref, do_ref, kc_ref, kp_ref, vc_ref, vp_ref = ins[6 * ci:6 * ci + 6]

                def unit(u, carry, d=d, q_ref=q_ref, do_ref=do_ref, kc_ref=kc_ref, kp_ref=kp_ref, vc_ref=vc_ref,
                         vp_ref=vp_ref):
                    r, n, per_stream = _unit_place(d, u)
                    qv = q_ref[r, _block_rows(n), :]
                    dov = do_ref[r, _block_rows(n), :]
                    kb = _band(kc_ref, kp_ref, r, n)
                    vb = _band(vc_ref, vp_ref, r, n)
                    rows = _natural_rows(d, r, n)
                    has_prev = (sb > 0) | (n > 0)
                    q_pair = _head_pair_rows(qv, h0)
                    do_pair = _head_pair_rows(dov, h0)
                    s_t = jnp.where(_band_mask_t(has_prev), _dot_nt(kb, q_pair), NEG)
                    p_t = jnp.exp(s_t - _per_query_row(lse_ref[rows, :]))
                    dp_t = _dot_nt(vb, do_pair)
                    ds_t = (p_t * (dp_t - _per_query_row(dl_ref[rows, :]))).astype(bf16)
                    dvb = _dot(p_t.astype(bf16), do_pair)
                    dkb = _dot(ds_t, q_pair)
                    dq_pair = _dot_tn(ds_t, kb)
                    dq_acc[rows, :] += jnp.where(h0, dq_pair[:BLOCK], dq_pair[BLOCK:])
                    dk_acc[cur, rows, :] += dkb[BLOCK:]
                    dv_acc[cur, rows, :] += dvb[BLOCK:]

                    slot = jnp.where((n > 0) | (sb == 0), cur, prv)
                    before = _natural_rows(d, r, jnp.where(n > 0, n - 1, per_stream - 1))
                    dk_acc[slot, before, :] += dkb[:BLOCK]
                    dv_acc[slot, before, :] += dvb[:BLOCK]
                    return carry

                lax.fori_loop(0, UNITS, unit, 0, unroll=BWD_UNROLL)
            dq_ref[...] = (dq_acc[...] * ATTN_SCALE).astype(bf16)

        @pl.when(sb > 0)
        def _():
            dk_ref[...] = dk_acc[prv].astype(bf16)
            dv_ref[...] = dv_acc[prv].astype(bf16)

        @pl.when(step == n_total - 1)
        def _():
            scatter.finish()

    args = []
    for q, k, v, do in zip(qs, ks, vs, dos):
        args += [q, do, k, k, v, v]
    nat = pl.BlockSpec((SUPER, 128), lambda hp, sb: (jnp.minimum(sb, last), hp))
    nat_before = pl.BlockSpec((SUPER, 128), lambda hp, sb: (jnp.clip(sb - 1, 0, last), hp))
    out = jax.ShapeDtypeStruct((S, ATTN_WIDTH), bf16)
    half = chip_sum.shape[1]
    return pl.pallas_call(
        body, name="attn_bwd", grid=(ATTN_WIDTH // 128, n_steps + 1),
        in_specs=_attn_in_specs(S, True) + [nat, nat, ANY], out_specs=[nat, nat_before, nat_before, ANY],
        out_shape=[out, out, out, _Scatter.out_shape(half, chip_sum.dtype)],
        scratch_shapes=[pltpu.VMEM((SUPER, 128), f32), pltpu.VMEM((2, SUPER, 128), f32),
                        pltpu.VMEM((2, SUPER, 128), f32)] + _Scatter.scratch(half),
        compiler_params=_params("arbitrary", "arbitrary"),
    )(*args, lse, delta, chip_sum)


def _rms(v):
    return lax.rsqrt(jnp.mean(v * v, axis=-1, keepdims=True) + EPS)


def _out_proj(pool_out, attn_out, w_out, x, g2, g3):
    S = x.shape[0]
    ts = 512

    def body(p_ref, a_ref, w_ref, x_ref, g2_ref, g3_ref, mix_ref, x2_ref, h2_ref):
        mix = _dot(p_ref[...], w_ref[:POOL_WIDTH, :]) + _dot(a_ref[...], w_ref[POOL_WIDTH:, :])
        mix_ref[...] = mix
        x2 = x_ref[...] + (mix * _rms(mix)) * g2_ref[...]
        x2_ref[...] = x2
        h2_ref[...] = ((x2 * _rms(x2)) * g3_ref[...]).astype(bf16)

    row = lambda w: pl.BlockSpec((ts, w), lambda i: (i, 0))
    gain = pl.BlockSpec((1, D_MODEL), lambda i: (0, 0))
    return pl.pallas_call(
        body, name="out_proj", grid=(S // ts,),
        in_specs=[row(POOL_WIDTH), row(ATTN_WIDTH), pl.BlockSpec((D_MODEL, D_MODEL), lambda i: (0, 0)),
                  row(D_MODEL), gain, gain],
        out_specs=[row(D_MODEL)] * 3,
        out_shape=[jax.ShapeDtypeStruct((S, D_MODEL), f32), jax.ShapeDtypeStruct((S, D_MODEL), f32),
                   jax.ShapeDtypeStruct((S, D_MODEL), bf16)],
        compiler_params=_params("parallel"),
    )(pool_out, attn_out, w_out, x, g2, g3)


FF_TILE = 256
FF_STEP_ROWS = 2048
FF_ROWS = 512
FF_BWD_ROWS = 256


def _sigmoid(g):
    return 1.0 / (1.0 + jnp.exp(-g))


def _ff_act_shape(S):
    return jax.ShapeDtypeStruct((D_FF // FF_TILE, S, FF_TILE), bf16)


def _ff_act_spec(ts):
    return pl.BlockSpec((1, ts, FF_TILE), lambda i, j: (j, i, 0))


def _ffn_fwd(h2, w_gate, w_up, w_down):
    S = h2.shape[0]
    ts = min(S, FF_STEP_ROWS)

    def body(h_ref, wg_ref, wu_ref, wd_ref, gate_ref, up_ref, f_ref):
        def rows_pass(first):
            def sub(i, carry):
                rows = pl.ds(pl.multiple_of(i * FF_ROWS, FF_ROWS), FF_ROWS)
                h = h_ref[rows, :]
                gate = _dot_nt(h, wg_ref[...])
                up = _dot_nt(h, wu_ref[...])
                gate_ref[0, rows, :] = gate.astype(bf16)
                up_ref[0, rows, :] = up.astype(bf16)
                part = _dot((gate * _sigmoid(gate) * up).astype(bf16), wd_ref[...])
                if first:
                    f_ref[rows, :] = part
                else:
                    f_ref[rows, :] += part
                return carry

            lax.fori_loop(0, ts // FF_ROWS, sub, 0, unroll=True)

        @pl.when(pl.program_id(1) == 0)
        def _():
            rows_pass(True)

        @pl.when(pl.program_id(1) > 0)
        def _():
            rows_pass(False)

    act = _ff_act_spec(ts)
    weight = pl.BlockSpec((FF_TILE, D_MODEL), lambda i, j: (j, 0))
    return pl.pallas_call(
        body, name="ffn_fwd", grid=(S // ts, D_FF // FF_TILE),
        in_specs=[pl.BlockSpec((ts, D_MODEL), lambda i, j: (i, 0)), weight, weight, weight],
        out_specs=[act, act, pl.BlockSpec((ts, D_MODEL), lambda i, j: (i, 0))],
        out_shape=[_ff_act_shape(S), _ff_act_shape(S), jax.ShapeDtypeStruct((S, D_MODEL), f32)],
        compiler_params=_params("parallel", "arbitrary"),
    )(h2, w_gate, w_up, w_down)


def _loss_head(f, x2, target, g4):
    S = f.shape[0]
    ts = 512

    def body(f_ref, x2_ref, t_ref, g_ref, dy_ref, df_ref, dg_ref, loss_ref):
        @pl.when(pl.program_id(0) == 0)
        def _():
            dg_ref[...] = jnp.zeros_like(dg_ref)
            loss_ref[...] = jnp.zeros_like(loss_ref)

        fv = f_ref[...]
        g = g_ref[...]
        r = _rms(fv)
        fhat = fv * r
        err = (x2_ref[...] + fhat * g) - t_ref[...]
        loss_ref[...] += 0.5 * jnp.sum(jnp.mean(err * err, axis=-1, keepdims=True), axis=0, keepdims=True)
        dy = err * (1.0 / D_MODEL)
        dy_ref[...] = dy
        dg_ref[...] += jnp.sum(dy * fhat, axis=0, keepdims=True)
        dyg = dy * g
        df_ref[...] = (r * (dyg - fhat * jnp.mean(dyg * fhat, axis=-1, keepdims=True))).astype(bf16)

    row = pl.BlockSpec((ts, D_MODEL), lambda i: (i, 0))
    gain = pl.BlockSpec((1, D_MODEL), lambda i: (0, 0))
    return pl.pallas_call(
        body, name="loss_head", grid=(S // ts,), in_specs=[row, row, row, gain],
        out_specs=[row, row, gain, pl.BlockSpec((1, 1), lambda i: (0, 0))],
        out_shape=[jax.ShapeDtypeStruct((S, D_MODEL), f32), jax.ShapeDtypeStruct((S, D_MODEL), bf16),
                   jax.ShapeDtypeStruct((1, D_MODEL), f32), jax.ShapeDtypeStruct((1, 1), f32)],
        compiler_params=_params("arbitrary"),
    )(f, x2, target, g4)


def _ffn_bwd(df, gate, up, w_gate, w_up, w_down):
    S = df.shape[0]
    ts = min(S, FF_STEP_ROWS)

    def body(df_ref, gate_ref, up_ref, wg_ref, wu_ref, wd_ref, a_ref, dgate_ref, dup_ref, dh_ref):
        def rows_pass(first):
            def sub(i, carry):
                rows = pl.ds(pl.multiple_of(i * FF_BWD_ROWS, FF_BWD_ROWS), FF_BWD_ROWS)
                da = _dot_nt(df_ref[rows, :], wd_ref[...])
                g = gate_ref[0, rows, :].astype(f32)
                u = up_ref[0, rows, :].astype(f32)
                sig = _sigmoid(g)
                silu = g * sig
                a_ref[0, rows, :] = (silu * u).astype(bf16)
                dup = (da * silu).astype(bf16)
                dgate = (da * u * (sig * (1.0 + g * (1.0 - sig)))).astype(bf16)
                dup_ref[0, rows, :] = dup
                dgate_ref[0, rows, :] = dgate
                part = _dot(dgate, wg_ref[...]) + _dot(dup, wu_ref[...])
                if first:
                    dh_ref[rows, :] = part
                else:
                    dh_ref[rows, :] += part
                return carry

            lax.fori_loop(0, ts // FF_BWD_ROWS, sub, 0, unroll=True)

        @pl.when(pl.program_id(1) == 0)
        def _():
            rows_pass(True)

        @pl.when(pl.program_id(1) > 0)
        def _():
            rows_pass(False)

    act = _ff_act_spec(ts)
    row = pl.BlockSpec((ts, D_MODEL), lambda i, j: (i, 0))
    return pl.pallas_call(
        body, name="ffn_bwd", grid=(S // ts, D_FF // FF_TILE),
        in_specs=[row, act, act,
                  pl.BlockSpec((FF_TILE, D_MODEL), lambda i, j: (j, 0)),
                  pl.BlockSpec((FF_TILE, D_MODEL), lambda i, j: (j, 0)),
                  pl.BlockSpec((FF_TILE, D_MODEL), lambda i, j: (j, 0))],
        out_specs=[act, act, act, row],
        out_shape=[_ff_act_shape(S)] * 3 + [jax.ShapeDtypeStruct((S, D_MODEL), f32)],
        compiler_params=_params("parallel", "arbitrary"),
    )(df, gate, up, w_gate, w_up, w_down)


def _norm_bwd(dh2, dy, x2, mix, g3, g2):
    S = dh2.shape[0]
    ts = 512

    def body(dh_ref, dy_ref, x2_ref, mix_ref, g3_ref, g2_ref, dx2_ref, dmix_ref, dg3_ref, dg2_ref):
        @pl.when(pl.program_id(0) == 0)
        def _():
            dg3_ref[...] = jnp.zeros_like(dg3_ref)
            dg2_ref[...] = jnp.zeros_like(dg2_ref)

        dh = dh_ref[...]
        x2 = x2_ref[...]
        r3 = _rms(x2)
        xhat = x2 * r3
        dg3_ref[...] += jnp.sum(dh * xhat, axis=0, keepdims=True)
        dhg = dh * g3_ref[...]
        dx2 = dy_ref[...] + r3 * (dhg - xhat * jnp.mean(dhg * xhat, axis=-1, keepdims=True))
        dx2_ref[...] = dx2
        mix = mix_ref[...]
        r2 = _rms(mix)
        mhat = mix * r2
        dg2_ref[...] += jnp.sum(dx2 * mhat, axis=0, keepdims=True)
        dmg = dx2 * g2_ref[...]
        dmix_ref[...] = (r2 * (dmg - mhat * jnp.mean(dmg * mhat, axis=-1, keepdims=True))).astype(bf16)

    row = pl.BlockSpec((ts, D_MODEL), lambda i: (i, 0))
    gain = pl.BlockSpec((1, D_MODEL), lambda i: (0, 0))
    return pl.pallas_call(
        body, name="norm_bwd", grid=(S // ts,), in_specs=[row, row, row, row, gain, gain],
        out_specs=[row, row, gain, gain],
        out_shape=[jax.ShapeDtypeStruct((S, D_MODEL), f32), jax.ShapeDtypeStruct((S, D_MODEL), bf16),
                   jax.ShapeDtypeStruct((1, D_MODEL), f32), jax.ShapeDtypeStruct((1, D_MODEL), f32)],
        compiler_params=_params("arbitrary"),
    )(dh2, dy, x2, mix, g3, g2)


def _out_proj_bwd(dmix, w_out, attn_out, head_ones, grads):
    S = dmix.shape[0]
    ts = 512
    n_dil = len(DILATIONS)

    def body(dm_ref, w_ref, o_ref, ones_ref, g_ref, dp_ref, dl_ref, *rest):
        do_refs, theirs_ref = rest[:n_dil], rest[n_dil]
        stage = rest[n_dil + 1:n_dil + 1 + N_STAGE]
        swap = _Swap(g_ref, theirs_ref, *rest[n_dil + 1 + N_STAGE:])

        @pl.when(pl.program_id(0) == 0)
        def _():
            swap.start()

        @pl.when(pl.program_id(0) == S // ts - 1)
        def _():
            swap.finish()

        dcat = _dot_nt(dm_ref[...], w_ref[...])
        dp_ref[...] = dcat[:, :POOL_WIDTH]
        do = dcat[:, POOL_WIDTH:]
        for j in range(ATTN_WIDTH // 128):
            stage[j][...] = do[:, j * 128:(j + 1) * 128]
        _store_streams(stage, do_refs, ts)
        prod = do * o_ref[...].astype(f32)
        hi = prod.astype(bf16)
        lo = (prod - hi.astype(f32)).astype(bf16)
        ones = ones_ref[...]
        for j in range(ATTN_WIDTH // 128):
            cols = slice(j * 128, (j + 1) * 128)
            dl_ref[:, cols] = _dot(hi[:, cols], ones) + _dot(lo[:, cols], ones)

    row = lambda w: pl.BlockSpec((ts, w), lambda i: (i, 0))
    res = pl.pallas_call(
        body, name="out_proj_bwd", grid=(S // ts,),
        in_specs=[row(D_MODEL), pl.BlockSpec((D_MODEL, D_MODEL), lambda i: (0, 0)), row(ATTN_WIDTH),
                  pl.BlockSpec((128, 128), lambda i: (0, 0)), ANY],
        out_specs=[row(POOL_WIDTH), row(ATTN_WIDTH)] + [_stream_spec(d, ts) for d in DILATIONS] + [ANY],
        out_shape=[jax.ShapeDtypeStruct((S, POOL_WIDTH), f32), jax.ShapeDtypeStruct((S, ATTN_WIDTH), f32)]
        + [_stream_shape(S, d) for d in DILATIONS] + [_Swap.out_shape(grads)],
        scratch_shapes=_stage_scratch(ts) + _Swap.scratch(grads),
        compiler_params=_params("arbitrary"),
    )(dmix, w_out, attn_out, head_ones, grads)
    return res[0], res[1], res[2:2 + n_dil], res[2 + n_dil]


def _in_proj_bwd(du, dq, dk, dv, cos_t, sin_t, w_in, x, dx2, g1):
    S = x.shape[0]
    ts = 512

    def body(du_ref, dq_ref, dk_ref, dv_ref, cos_ref, sin_ref, w_ref, x_ref, dx2_ref, g_ref, gx_ref, dproj_ref, dg_ref):
        @pl.when(pl.program_id(0) == 0)
        def _():
            dg_ref[...] = jnp.zeros_like(dg_ref)

        dproj_ref[:, :POOL_WIDTH] = du_ref[...]
        cos = cos_ref[...]
        sin = sin_ref[...]
        first = _first_half_mask(ts)
        for j in range(ATTN_WIDTH // 128):
            cols = slice(j * 128, (j + 1) * 128)
            for base, ref in ((POOL_WIDTH, dq_ref), (POOL_WIDTH + ATTN_WIDTH, dk_ref)):
                g = ref[:, cols].astype(f32)
                pre = g * cos + _rope_partner(g * sin, first)
                dproj_ref[:, base + j * 128: base + (j + 1) * 128] = pre.astype(bf16)
        dproj_ref[:, POOL_WIDTH + 2 * ATTN_WIDTH:] = dv_ref[...]

        dh = _dot(dproj_ref[...], w_ref[...])
        xv = x_ref[...]
        r = _rms(xv)
        xhat = xv * r
        dg_ref[...] += jnp.sum(dh * xhat, axis=0, keepdims=True)
        dhg = dh * g_ref[...]
        gx_ref[...] = dx2_ref[...] + r * (dhg - xhat * jnp.mean(dhg * xhat, axis=-1, keepdims=True))

    row = lambda w: pl.BlockSpec((ts, w), lambda i: (i, 0))
    gain = pl.BlockSpec((1, D_MODEL), lambda i: (0, 0))
    return pl.pallas_call(
        body, name="in_proj_bwd", grid=(S // ts,),
        in_specs=[row(POOL_WIDTH)] + [row(ATTN_WIDTH)] * 3 + [row(128), row(128),
                  pl.BlockSpec((IN_WIDTH, D_MODEL), lambda i: (0, 0)), row(D_MODEL), row(D_MODEL), gain],
        out_specs=[row(D_MODEL), row(IN_WIDTH), gain],
        out_shape=[jax.ShapeDtypeStruct((S, D_MODEL), f32), jax.ShapeDtypeStruct((S, IN_WIDTH), bf16),
                   jax.ShapeDtypeStruct((1, D_MODEL), f32)],
        compiler_params=_params("arbitrary"),
    )(du, dq, dk, dv, cos_t, sin_t, w_in, x, dx2, g1)


def _matmul_tiles_tn(a, b, name):
    T, K, w = a.shape
    N = b.shape[1]
    tk = 1024

    def body(a_ref, b_ref, o_ref):
        def tiles_pass(first):
            for t in range(T):
                part = _dot_tn(a_ref[t], b_ref[...])
                if first:
                    o_ref[t * w:(t + 1) * w, :] = part
                else:
                    o_ref[t * w:(t + 1) * w, :] += part

        @pl.when(pl.program_id(0) == 0)
        def _():
            tiles_pass(True)

        @pl.when(pl.program_id(0) > 0)
        def _():
            tiles_pass(False)

    return pl.pallas_call(
        body, name=name, grid=(K // tk,),
        in_specs=[pl.BlockSpec((T, tk, w), lambda k: (0, k, 0)), pl.BlockSpec((tk, N), lambda k: (k, 0))],
        out_specs=pl.BlockSpec((T * w, N), lambda k: (0, 0)),
        out_shape=jax.ShapeDtypeStruct((T * w, N), f32),
        compiler_params=_params("arbitrary"),
    )(a, b)


def _matmul_tn(a, b, name):
    K, M = a.shape
    N = b.shape[1]
    tk = 1024

    def body(a_ref, b_ref, o_ref):
        _tn_step(a_ref, b_ref, o_ref, M)

    return pl.pallas_call(
        body, name=name, grid=(K // tk,),
        in_specs=[pl.BlockSpec((tk, M), lambda k: (k, 0)), pl.BlockSpec((tk, N), lambda k: (k, 0))],
        out_specs=pl.BlockSpec((M, N), lambda k: (0, 0)),
        out_shape=jax.ShapeDtypeStruct((M, N), f32),
        compiler_params=_params("arbitrary"),
    )(a, b)


def _tn_step(a_ref, b_ref, o_ref, M):
    w = 256

    def tiles_pass(first):
        for t in range(M // w):
            part = _dot_tn(a_ref[:, t * w:(t + 1) * w], b_ref[...])
            if first:
                o_ref[t * w:(t + 1) * w, :] = part
            else:
                o_ref[t * w:(t + 1) * w, :] += part

    @pl.when(pl.program_id(0) == 0)
    def _():
        tiles_pass(True)

    @pl.when(pl.program_id(0) > 0)
    def _():
        tiles_pass(False)


def _matmul_tn_and_small_sum(a, b, block, name):
    K, M = a.shape
    N = b.shape[1]
    tk = 1024
    n_steps = K // tk

    def body(a_ref, b_ref, block_ref, o_ref, total_ref, *scratch):
        small = _SmallSum(block_ref, *scratch)

        @pl.when(pl.program_id(0) == 0)
        def _():
            small.start()

        _tn_step(a_ref, b_ref, o_ref, M)

        @pl.when(pl.program_id(0) == n_steps - 1)
        def _():
            small.finish(total_ref)

    return pl.pallas_call(
        body, name=name, grid=(n_steps,),
        in_specs=[pl.BlockSpec((tk, M), lambda k: (k, 0)), pl.BlockSpec((tk, N), lambda k: (k, 0)), ANY],
        out_specs=[pl.BlockSpec((M, N), lambda k: (0, 0)), pl.BlockSpec(block.shape, lambda k: (0, 0))],
        out_shape=[jax.ShapeDtypeStruct((M, N), f32), jax.ShapeDtypeStruct(block.shape, block.dtype)],
        scratch_shapes=_SmallSum.scratch(block),
        compiler_params=_params("arbitrary"),
    )(a, b, block)


def _rope_tables(S):
    half = HEAD_DIM // 2
    freqs = ROPE_THETA ** (-jnp.arange(half, dtype=f32) * (2.0 / HEAD_DIM))
    ang = jnp.arange(S).astype(f32)[:, None] * freqs[None, :]
    cos = jnp.tile(jnp.cos(ang), (1, 4))
    sin = jnp.sin(ang)
    sin = jnp.tile(jnp.concatenate([-sin, sin], axis=1), (1, 2))
    return cos, sin


def _block_diag(w_pool):
    w = jnp.zeros((POOL_WIDTH, POOL_WIDTH), w_pool.dtype)
    for g in range(POOL_WIDTH // POOL_GROUP):
        w = lax.dynamic_update_slice(w, w_pool[g], (g * POOL_GROUP, g * POOL_GROUP))
    return w


def _head_ones():
    head = np.arange(128) // HEAD_DIM
    return jnp.asarray(head[:, None] == head[None, :], dtype=bf16)


def _place():
    x, y, c = lax.axis_index("x"), lax.axis_index("y"), lax.axis_index("c")
    chips = [(1 - x, y), (x, 1 - y), (1 - x, 1 - y)]
    return x, y, c, chips


ANY = pl.BlockSpec(memory_space=pl.ANY)
N_PEER_CHIPS = N_CHIPS - 1
ICI_PIECES = 4
D2D_PIECES = 8
LOCAL_PIECES = 8


def _row_chunks(rows, n, unit=32):
    units = rows // unit
    out, start = [], 0
    for i in range(n):
        size = (units // n + (1 if i < units % n else 0)) * unit
        out.append((start, size))
        start += size
    return [piece for piece in out if piece[1]]


class _LocalCopy:
    def __init__(self, src_rows, dst_rows, rows, buf, sems_in, sems_out):
        self.loads, self.stores = [], []
        for i, (start, size) in enumerate(_row_chunks(rows, LOCAL_PIECES)):
            r = pl.ds(start, size)
            self.loads.append(pltpu.make_async_copy(src_rows(r), buf.at[r], sems_in.at[i]))
            self.stores.append(pltpu.make_async_copy(buf.at[r], dst_rows(r), sems_out.at[i]))

    def start(self):
        for cp in self.loads:
            cp.start()

    def pass_on(self):
        for load, store in zip(self.loads, self.stores):
            load.wait()
            store.start()

    def finish(self):
        for store in self.stores:
            store.wait()

    @staticmethod
    def scratch(rows, dtype):
        return [pltpu.VMEM((rows, D_MODEL), dtype), pltpu.SemaphoreType.DMA((LOCAL_PIECES,)),
                pltpu.SemaphoreType.DMA((LOCAL_PIECES,))]


class _Gather:
    def __init__(self, w_ref, out_ref, send1, recv1, send2, recv2, buf, sems_in, sems_out):
        x, y, c, chips = _place()
        me = 2 * x + y
        rows = w_ref.shape[0]
        half = rows // 2
        pieces = _row_chunks(half, ICI_PIECES)
        self.own = _LocalCopy(lambda r: w_ref.at[r], lambda r: out_ref.at[me, r], rows, buf, sems_in, sems_out)

        def rows_of(core, piece):
            start, size = piece
            return pl.ds(core * half + start, size)

        self.sends, self.arrivals, self.forwards, self.forward_arrivals = [], [], [], []
        for i, piece in enumerate(pieces):
            for j, (cx, cy) in enumerate(chips):
                k = j * len(pieces) + i
                there = 2 * cx + cy

                def direct(src_chip, cx=cx, cy=cy, k=k, piece=piece):
                    return pltpu.make_async_remote_copy(
                        src_ref=w_ref.at[rows_of(c, piece)], dst_ref=out_ref.at[src_chip, rows_of(c, piece)],
                        send_sem=send1.at[k], recv_sem=recv1.at[k], device_id=(cx, cy, c), device_id_type=MESH)

                def passed(core, there=there, k=k, piece=piece):
                    return pltpu.make_async_remote_copy(
                        src_ref=out_ref.at[there, rows_of(core, piece)], dst_ref=out_ref.at[there, rows_of(core, piece)],
                        send_sem=send2.at[k], recv_sem=recv2.at[k], device_id=(x, y, 1 - c), device_id_type=MESH)

                self.sends.append(direct(me))
                self.arrivals.append(direct(there))
                self.forwards.append(passed(c))
                self.forward_arrivals.append(passed(1 - c))

    def start(self):
        for cp in self.sends:
            cp.start()
        self.own.start()

    def pass_on(self):
        self.own.pass_on()
        for arrival, forward in zip(self.arrivals, self.forwards):
            arrival.wait_recv()
            forward.start()

    def finish(self):
        for arrival in self.forward_arrivals:
            arrival.wait_recv()
        for cp in self.sends + self.forwards:
            cp.wait_send()
        self.own.finish()

    @staticmethod
    def scratch(rows, dtype):
        n = N_PEER_CHIPS * len(_row_chunks(rows // 2, ICI_PIECES))
        return [pltpu.SemaphoreType.DMA((n,))] * 4 + _LocalCopy.scratch(rows, dtype)

    @staticmethod
    def out_shape(rows, dtype):
        return jax.ShapeDtypeStruct((N_CHIPS, rows, D_MODEL), dtype)


def _gather_weights(pack):
    rows = pack.shape[0]

    def body(w_ref, out_ref, *scratch):
        gather = _Gather(w_ref, out_ref, *scratch)
        gather.start()
        gather.pass_on()
        gather.finish()

    return pl.pallas_call(
        body, name="gather_weights", in_specs=[ANY], out_specs=ANY, out_shape=_Gather.out_shape(rows, pack.dtype),
        scratch_shapes=_Gather.scratch(rows, pack.dtype),
        compiler_params=pltpu.CompilerParams(vmem_limit_bytes=VMEM_LIMIT_V7X),
    )(pack)


class _Scatter:
    def __init__(self, h_ref, out_ref, send, recv):
        x, y, c, chips = _place()
        pieces = _row_chunks(h_ref.shape[1], ICI_PIECES)
        self.copies = []
        for i, (start, size) in enumerate(pieces):
            for j, (cx, cy) in enumerate(chips):
                k = j * len(pieces) + i
                self.copies.append(pltpu.make_async_remote_copy(
                    src_ref=h_ref.at[2 * cx + cy, pl.ds(start, size)], dst_ref=out_ref.at[j, pl.ds(start, size)],
                    send_sem=send.at[k], recv_sem=recv.at[k], device_id=(cx, cy, c), device_id_type=MESH))

    def start(self):
        for cp in self.copies:
            cp.start()

    def finish(self):
        for cp in self.copies:
            cp.wait_recv()
        for cp in self.copies:
            cp.wait_send()

    @staticmethod
    def scratch(half):
        n = N_PEER_CHIPS * len(_row_chunks(half, ICI_PIECES))
        return [pltpu.SemaphoreType.DMA((n,))] * 2

    @staticmethod
    def out_shape(half, dtype):
        return jax.ShapeDtypeStruct((N_PEER_CHIPS, half, D_MODEL), dtype)


def _scatter_to_chips(h):
    half = h.shape[1]

    def body(h_ref, out_ref, send, recv):
        scatter = _Scatter(h_ref, out_ref, send, recv)
        scatter.start()
        scatter.finish()

    return pl.pallas_call(
        body, name="scatter_to_chips", in_specs=[ANY], out_specs=ANY, out_shape=_Scatter.out_shape(half, h.dtype),
        scratch_shapes=_Scatter.scratch(half),
    )(h)


class _Swap:
    def __init__(self, g_ref, theirs_ref, send, recv):
        x, y, c, _ = _place()
        half = g_ref.shape[1] // 2
        pieces = _row_chunks(half, D2D_PIECES)
        self.copies = []
        for s in range(N_CHIPS):
            for i, (start, size) in enumerate(pieces):
                k = s * len(pieces) + i
                self.copies.append(pltpu.make_async_remote_copy(
                    src_ref=g_ref.at[s, pl.ds((1 - c) * half + start, size)], dst_ref=theirs_ref.at[s, pl.ds(start, size)],
                    send_sem=send.at[k], recv_sem=recv.at[k], device_id=(x, y, 1 - c), device_id_type=MESH))

    def start(self):
        for cp in self.copies:
            cp.start()

    def finish(self):
        for cp in self.copies:
            cp.wait()

    @staticmethod
    def scratch(g):
        n = N_CHIPS * len(_row_chunks(g.shape[1] // 2, D2D_PIECES))
        return [pltpu.SemaphoreType.DMA((n,))] * 2

    @staticmethod
    def out_shape(g):
        return jax.ShapeDtypeStruct((N_CHIPS, g.shape[1] // 2, D_MODEL), g.dtype)


def _swap_halves(g):
    def body(g_ref, theirs_ref, send, recv):
        swap = _Swap(g_ref, theirs_ref, send, recv)
        swap.start()
        swap.finish()

    return pl.pallas_call(
        body, name="swap_halves", in_specs=[ANY], out_specs=ANY, out_shape=_Swap.out_shape(g),
        scratch_shapes=_Swap.scratch(g),
    )(g)


ADD_TILE_MAX_ROWS = 600


def _add_tile(half):
    return max(t for t in range(8, ADD_TILE_MAX_ROWS + 1, 8) if half % t == 0)


def _add_cores(g, theirs, name, out_dtype=f32):
    half = theirs.shape[1]
    tr = _add_tile(half)
    n_t = half // tr

    def body(c_ref, g_ref, t_ref, o_ref):
        o_ref[...] = (g_ref[...] + t_ref[...]).astype(out_dtype)

    blk = pl.BlockSpec((1, tr, D_MODEL), lambda s, t, c_ref: (s, t, 0))
    return pl.pallas_call(
        body, name=name,
        grid_spec=pltpu.PrefetchScalarGridSpec(
            num_scalar_prefetch=1, grid=(N_CHIPS, n_t),
            in_specs=[pl.BlockSpec((1, tr, D_MODEL), lambda s, t, c_ref: (s, c_ref[0] * n_t + t, 0)), blk],
            out_specs=blk),
        out_shape=jax.ShapeDtypeStruct(theirs.shape, out_dtype),
        compiler_params=_params("parallel", "parallel"),
    )(lax.axis_index("c").astype(jnp.int32).reshape(1), g, theirs)


def _add_chips(chip_sum, others, name):
    half = chip_sum.shape[1]
    tr = _add_tile(half)

    def body(me_ref, own_ref, o0, o1, o2, out_ref):
        out_ref[...] = ((own_ref[0].astype(f32) + o0[0].astype(f32)) + o1[0].astype(f32)) + o2[0].astype(f32)

    other = lambda j: pl.BlockSpec((1, tr, D_MODEL), lambda t, me_ref: (j, t, 0))
    return pl.pallas_call(
        body, name=name,
        grid_spec=pltpu.PrefetchScalarGridSpec(
            num_scalar_prefetch=1, grid=(half // tr,),
            in_specs=[pl.BlockSpec((1, tr, D_MODEL), lambda t, me_ref: (me_ref[0], t, 0)), other(0), other(1), other(2)],
            out_specs=pl.BlockSpec((tr, D_MODEL), lambda t, me_ref: (t, 0))),
        out_shape=jax.ShapeDtypeStruct((half, D_MODEL), f32),
        compiler_params=_params("parallel"),
    )((2 * lax.axis_index("x") + lax.axis_index("y")).astype(jnp.int32).reshape(1), chip_sum, others, others, others)


def _join_halves(r):
    half = r.shape[0]
    pieces = _row_chunks(half, 2 * D2D_PIECES)
    n = len(pieces)

    def body(r_ref, out_ref, send, recv, buf, sems_in, sems_out):
        x, y, c, _ = _place()
        own = _LocalCopy(lambda rr: r_ref.at[rr], lambda rr: out_ref.at[c, rr], half, buf, sems_in, sems_out)
        own.start()

        def piece(i, core):
            start, size = pieces[i]
            return pltpu.make_async_remote_copy(
                src_ref=r_ref.at[pl.ds(start, size)], dst_ref=out_ref.at[core, pl.ds(start, size)],
                send_sem=send.at[i], recv_sem=recv.at[i], device_id=(x, y, 1 - c), device_id_type=MESH)

        copies = [piece(i, c) for i in range(n)]
        for cp in copies:
            cp.start()
        own.pass_on()
        for i in range(n):
            piece(i, 1 - c).wait_recv()
        for cp in copies:
            cp.wait_send()
        own.finish()

    return pl.pallas_call(
        body, name="join_halves", in_specs=[ANY], out_specs=ANY,
        out_shape=jax.ShapeDtypeStruct((2,) + r.shape, r.dtype),
        scratch_shapes=[pltpu.SemaphoreType.DMA((n,))] * 2 + _LocalCopy.scratch(half, r.dtype),
        compiler_params=pltpu.CompilerParams(vmem_limit_bytes=VMEM_LIMIT_V7X),
    )(r)


class _SmallSum:
    def __init__(self, b_ref, gathered, send, recv, local_sem):
        x, y, c, _ = _place()
        me = 4 * x + 2 * y + c
        self.gathered = gathered
        self.own = pltpu.make_async_copy(b_ref, gathered.at[me], local_sem)
        self.sends, self.arrivals = [], []
        for kk in range(1, N_DEV):
            flip = lambda v, bit: 1 - v if bit else v
            peer = (flip(x, kk & 4), flip(y, kk & 2), flip(c, kk & 1))
            self.sends.append(pltpu.make_async_remote_copy(
                src_ref=b_ref, dst_ref=gathered.at[me], send_sem=send.at[kk - 1], recv_sem=recv.at[kk - 1],
                device_id=peer, device_id_type=MESH))
            self.arrivals.append(pltpu.make_async_remote_copy(
                src_ref=b_ref, dst_ref=gathered.at[jnp.bitwise_xor(me, kk)], send_sem=send.at[kk - 1],
                recv_sem=recv.at[kk - 1], device_id=peer, device_id_type=MESH))

    def start(self):
        self.own.start()
        for cp in self.sends:
            cp.start()

    def finish(self, out_ref):
        self.own.wait()
        for cp in self.arrivals:
            cp.wait_recv()
        for cp in self.sends:
            cp.wait_send()
        acc = self.gathered[0]
        for dev in range(1, N_DEV):
            acc = acc + self.gathered[dev]
        out_ref[...] = acc

    @staticmethod
    def scratch(block):
        return [pltpu.VMEM((N_DEV,) + block.shape, block.dtype), pltpu.SemaphoreType.DMA((N_DEV - 1,)),
                pltpu.SemaphoreType.DMA((N_DEV - 1,)), pltpu.SemaphoreType.DMA]


def _adamw(w, g, m, v, name):
    rows, cols = w.shape
    tr = max(t for t in range(8, 513, 8) if rows % t == 0)
    c1 = 1.0 - ADAM_B1 ** ADAM_STEP
    c2 = 1.0 - ADAM_B2 ** ADAM_STEP

    def body(w_ref, g_ref, m_ref, v_ref, d_ref, nm_ref, nv_ref):
        gv = g_ref[...]
        nm = ADAM_B1 * m_ref[...] + (1.0 - ADAM_B1) * gv
        nv = ADAM_B2 * v_ref[...] + (1.0 - ADAM_B2) * (gv * gv)
        nm_ref[...] = nm
        nv_ref[...] = nv
        d_ref[...] = -ADAM_LR * ((nm / c1) / (jnp.sqrt(nv / c2) + ADAM_EPS) + ADAM_WD * w_ref[...])

    blk = pl.BlockSpec((tr, cols), lambda i: (i, 0))
    shape = jax.ShapeDtypeStruct((rows, cols), f32)
    return pl.pallas_call(
        body, name=name, grid=(rows // tr,), in_specs=[blk] * 4, out_specs=[blk] * 3, out_shape=[shape] * 3,
        compiler_params=_params("parallel"),
    )(w, g, m, v)


LARGE = ("w_in", "w_out", "w_gate", "w_up", "w_down")
SMALL = ("ln_pre_mix", "ln_post_mix", "ln_pre_ffn", "ln_post_ffn", "pool_scale", "w_pool")
SHARD_ROWS = {"w_in": 640, "w_out": 256, "w_gate": 704, "w_up": 704, "w_down": 704}
COLUMN_SHARDED = ("w_in", "w_gate", "w_up")
UPDATED_TRANSPOSED = ("w_gate", "w_up")
NEEDED_FIRST = ("w_in",)
NEEDED_LATER = ("w_out", "w_gate", "w_up", "w_down")
READY_EARLY = ("w_out", "w_gate", "w_up", "w_down")
READY_LATE = ("w_in",)


def _pack_shard(shards, names):
    return jnp.concatenate([shards[n].T if n in COLUMN_SHARDED else shards[n] for n in names], axis=0)


def _unpack_shard(pack, names):
    out, row = {}, 0
    for n in names:
        out[n] = pack[row:row + SHARD_ROWS[n]]
        row += SHARD_ROWS[n]
    return out


def _whole_from_shards(packs, names):
    out, row = {}, 0
    for n in names:
        rows = SHARD_ROWS[n]
        out[n] = packs[:, row:row + rows].reshape(N_CHIPS * rows, D_MODEL)
        row += rows
    return out


def _shards_from_whole(grads, names):
    return jnp.concatenate([grads[n].reshape(N_CHIPS, SHARD_ROWS[n], D_MODEL) for n in names], axis=1)


def _pack_small(vals):
    rows = [vals[n].reshape(1, D_MODEL) for n in SMALL[:4]]
    rows.append(jnp.pad(vals["pool_scale"].reshape(1, POOL_WIDTH), ((0, 0), (0, D_MODEL - POOL_WIDTH))))
    rows.append(jnp.pad(vals["loss"].reshape(1, 1), ((0, 0), (0, D_MODEL - 1))))
    rows.append(jnp.zeros((2, D_MODEL), f32))
    rows.append(vals["w_pool"].reshape(16, D_MODEL))
    return jnp.concatenate(rows, axis=0)


def _unpack_small(block):
    out = {n: block[i:i + 1] for i, n in enumerate(SMALL[:4])}
    out["pool_scale"] = block[4:5, :POOL_WIDTH]
    out["loss"] = block[5, 0]
    out["w_pool"] = block[8:24].reshape(1, 4, POOL_GROUP, POOL_GROUP)
    return out


def kernel(x, ln_pre_mix, w_in, w_pool, pool_scale, w_out, ln_post_mix, ln_pre_ffn, w_gate, w_up, w_down, ln_post_ffn, loss_target, m_ln_pre_mix, m_w_in, m_w_pool, m_pool_scale, m_w_out, m_ln_post_mix, m_ln_pre_ffn, m_w_gate, m_w_up, m_w_down, m_ln_post_ffn, v_ln_pre_mix, v_w_in, v_w_pool, v_pool_scale, v_w_out, v_ln_post_mix, v_ln_pre_ffn, v_w_gate, v_w_up, v_w_down, v_ln_post_ffn):
    w = dict(ln_pre_mix=ln_pre_mix, w_in=w_in, w_pool=w_pool, pool_scale=pool_scale, w_out=w_out,
             ln_post_mix=ln_post_mix, ln_pre_ffn=ln_pre_ffn, w_gate=w_gate, w_up=w_up, w_down=w_down,
             ln_post_ffn=ln_post_ffn)
    m = dict(ln_pre_mix=m_ln_pre_mix, w_in=m_w_in, w_pool=m_w_pool, pool_scale=m_pool_scale, w_out=m_w_out,
             ln_post_mix=m_ln_post_mix, ln_pre_ffn=m_ln_pre_ffn, w_gate=m_w_gate, w_up=m_w_up, w_down=m_w_down,
             ln_post_ffn=m_ln_post_ffn)
    v = dict(ln_pre_mix=v_ln_pre_mix, w_in=v_w_in, w_pool=v_w_pool, pool_scale=v_pool_scale, w_out=v_w_out,
             ln_post_mix=v_ln_post_mix, ln_pre_ffn=v_ln_pre_ffn, w_gate=v_w_gate, w_up=v_w_up, w_down=v_w_down,
             ln_post_ffn=v_ln_post_ffn)

    xs, target = x[0], loss_target[0]
    cos_t, sin_t = _rope_tables(xs.shape[0])
    w_bd = _block_diag(w_pool[0]).astype(bf16)
    shard = {n: w[n][0].astype(bf16) for n in LARGE}

    w_in_whole = _whole_from_shards(_gather_weights(_pack_shard(shard, NEEDED_FIRST)), NEEDED_FIRST)["w_in"]
    h1, u, qs, ks, vs = _in_proj(xs, ln_pre_mix, w_in_whole, cos_t, sin_t)
    pool_out = _pool_fwd(u, w_bd, pool_scale)
    attn_out, lse, later = _attn_fwd(qs, ks, vs, _pack_shard(shard, NEEDED_LATER))
    whole = _whole_from_shards(later, NEEDED_LATER)
    mix, x2, h2 = _out_proj(pool_out, attn_out, whole["w_out"], xs, ln_post_mix, ln_pre_ffn)
    gate, up, f = _ffn_fwd(h2, whole["w_gate"], whole["w_up"], whole["w_down"])
    dy, df, dg4, loss = _loss_head(f, x2, target, ln_post_ffn)

    large = {}
    a, dgate, dup, dh2 = _ffn_bwd(df, gate, up, whole["w_gate"], whole["w_up"], whole["w_down"])
    large["w_down"] = _matmul_tiles_tn(a, df, "grad_w_down")
    large["w_gate"] = _matmul_tiles_tn(dgate, h2, "grad_w_gate")
    large["w_up"] = _matmul_tiles_tn(dup, h2, "grad_w_up")
    dx2, dmix, dg3, dg2 = _norm_bwd(dh2, dy, x2, mix, ln_pre_ffn, ln_post_mix)
    large["w_out"] = jnp.concatenate([_matmul_tn(pool_out, dmix, "grad_w_out_pool"),
                                      _matmul_tn(attn_out, dmix, "grad_w_out_attn")], axis=0)
    early = _shards_from_whole(large, READY_EARLY)
    dpool, delta, dos, early_theirs = _out_proj_bwd(dmix, whole["w_out"], attn_out, _head_ones(), early)
    early_chip = _add_cores(early, early_theirs, "add_cores_early")
    du, d_w_bd, d_scale = _pool_bwd(u, dpool, w_bd, pool_scale)
    dq, dk, dv, early_others = _attn_bwd(qs, ks, vs, dos, lse, delta, early_chip)
    grad_x, dproj, dg1 = _in_proj_bwd(du, dq, dk, dv, cos_t, sin_t, w_in_whole, xs, dx2, ln_pre_mix)
    d_w_pool = jnp.stack([d_w_bd[g * POOL_GROUP:(g + 1) * POOL_GROUP, g * POOL_GROUP:(g + 1) * POOL_GROUP]
                          for g in range(POOL_WIDTH // POOL_GROUP)])
    small = dict(ln_pre_mix=dg1, ln_post_mix=dg2, ln_pre_ffn=dg3, ln_post_ffn=dg4, pool_scale=d_scale, w_pool=d_w_pool)
    large["w_in"], small_total = _matmul_tn_and_small_sum(dproj, h1, _pack_small(dict(small, loss=loss)), "grad_w_in")
    late = _shards_from_whole(large, READY_LATE)
    late_chip = _add_cores(late, _swap_halves(late), "add_cores_late", bf16)
    late_others = _scatter_to_chips(late_chip)
    early_half = _add_chips(early_chip, early_others, "add_chips_early")
    late_half = _add_chips(late_chip, late_others, "add_chips_late")
    joined = _join_halves(jnp.concatenate([early_half, late_half], axis=0))
    n_early = early_half.shape[0]
    grads = _unpack_shard(joined[:, :n_early].reshape(-1, D_MODEL), READY_EARLY)
    grads.update(_unpack_shard(joined[:, n_early:].reshape(-1, D_MODEL), READY_LATE))

    total = _unpack_small(small_total)
    for n in SMALL:
        grads[n] = total[n]

    delta_w, new_m, new_v = {}, {}, {}
    for n in LARGE:
        if n in UPDATED_TRANSPOSED:
            update = _adamw(w[n][0].T, grads[n], m[n][0].T, v[n][0].T, "adamw_" + n)
            delta_w[n], new_m[n], new_v[n], grads[n] = [a.T for a in (*update, grads[n])]
        else:
            if n in COLUMN_SHARDED:
                grads[n] = grads[n].T
            delta_w[n], new_m[n], new_v[n] = _adamw(w[n][0], grads[n], m[n][0], v[n][0], "adamw_" + n)
    small_state = [_pack_small(dict({n: s[n] for n in SMALL}, loss=jnp.zeros((), f32))) for s in (w, m, v)]
    small_grad = _pack_small(dict({n: grads[n] for n in SMALL}, loss=jnp.zeros((), f32)))
    sd, sm, sv = _adamw(small_state[0], small_grad, small_state[1], small_state[2], "adamw_small")
    for out, block in ((delta_w, sd), (new_m, sm), (new_v, sv)):
        un = _unpack_small(block)
        for n in SMALL:
            out[n] = un[n]

    names = ("ln_pre_mix", "w_in", "w_pool", "pool_scale", "w_out", "ln_post_mix", "ln_pre_ffn", "w_gate", "w_up",
             "w_down", "ln_post_ffn")
    full = lambda d: [d[n].reshape(w[n].shape) for n in names]
    return (total["loss"], grad_x[None], *full(grads), *full(delta_w), *full(new_m), *full(new_v))
```

```python
import numpy as np
import jax
import jax.numpy as jnp
from jax import lax
from jax.experimental import pallas as pl
from jax.experimental.pallas import tpu as pltpu

D_MODEL = 1024
POOL_WIDTH = 256
POOL_GROUP = 64
ATTN_WIDTH = 768
HEAD_DIM = 64
IN_WIDTH = 2560
D_FF = 2816
BLOCK = 128
DILATIONS = (1, 4, 16)
ROPE_THETA = 10000.0
EPS = 1e-6
ATTN_SCALE = 0.125
NEG = -1e30

ADAM_LR = 0.001
ADAM_B1 = 0.9
ADAM_B2 = 0.999
ADAM_EPS = 1e-08
ADAM_WD = 0.01
ADAM_STEP = 10

N_CHIPS = 4
N_DEV = 8
VMEM_LIMIT_V7X = 56 * 1024 * 1024
MESH = pl.DeviceIdType.MESH

f32 = jnp.float32
bf16 = jnp.bfloat16


def _params(*sem):
    return pltpu.CompilerParams(dimension_semantics=sem, vmem_limit_bytes=VMEM_LIMIT_V7X)


def _dot(a, b):
    return jnp.dot(a, b, preferred_element_type=f32)


def _dot_nt(a, b):
    return lax.dot_general(a, b, (((1,), (1,)), ((), ())), preferred_element_type=f32)


def _dot_tn(a, b):
    return lax.dot_general(a, b, (((0,), (0,)), ((), ())), preferred_element_type=f32)


def _rope_partner(a, first_half):
    return jnp.where(first_half, pltpu.roll(a, 96, 1), pltpu.roll(a, 32, 1))


def _first_half_mask(rows):
    lane = lax.broadcasted_iota(jnp.int32, (rows, 128), 1)
    return (lane % HEAD_DIM) < (HEAD_DIM // 2)


def _stream_spec(d, ts):
    return pl.BlockSpec((d, ts // d, ATTN_WIDTH), lambda i: (0, i, 0))


def _stream_shape(S, d):
    return jax.ShapeDtypeStruct((d, S // d, ATTN_WIDTH), bf16)


N_STAGE = ATTN_WIDTH // 128


def _stage_scratch(ts):
    return [pltpu.VMEM((ts, 128), f32)] * N_STAGE


def _store_streams(stage, out_refs, ts):
    for d, ref in zip(DILATIONS, out_refs):
        for r in range(d):
            rows = pl.ds(0, ts) if d == 1 else pl.ds(r, ts // d, stride=d)
            for j in range(N_STAGE):
                ref[r, :, j * 128:(j + 1) * 128] = stage[j][rows, :].astype(bf16)


def _in_proj(x, g1, w_in, cos_t, sin_t):
    S = x.shape[0]
    ts = 512

    def body(x_ref, g_ref, w_ref, cos_ref, sin_ref, h_ref, u_ref, *rest):
        outs, stage = rest[:-N_STAGE], rest[-N_STAGE:]
        xv = x_ref[...]
        r = lax.rsqrt(jnp.mean(xv * xv, axis=-1, keepdims=True) + EPS)
        h = ((xv * r) * g_ref[...]).astype(bf16)
        h_ref[...] = h
        proj = _dot_nt(h, w_ref[...])
        u_ref[...] = proj[:, :POOL_WIDTH]
        cos = cos_ref[...]
        sin = sin_ref[...]
        first = _first_half_mask(ts)
        n_dil = len(DILATIONS)
        for which, base in enumerate((POOL_WIDTH, POOL_WIDTH + ATTN_WIDTH)):
            for j in range(ATTN_WIDTH // 128):
                a = proj[:, base + j * 128: base + (j + 1) * 128]
                if which == 0:
                    a = a * ATTN_SCALE
                stage[j][...] = a * cos + _rope_partner(a, first) * sin
            _store_streams(stage, outs[which * n_dil:(which + 1) * n_dil], ts)
        for j in range(ATTN_WIDTH // 128):
            base = POOL_WIDTH + 2 * ATTN_WIDTH + j * 128
            stage[j][...] = proj[:, base:base + 128]
        _store_streams(stage, outs[2 * n_dil:], ts)

    row = lambda w: pl.BlockSpec((ts, w), lambda i: (i, 0))
    streams = [_stream_spec(d, ts) for d in DILATIONS] * 3
    res = pl.pallas_call(
        body, name="in_proj", grid=(S // ts,),
        in_specs=[row(D_MODEL), pl.BlockSpec((1, D_MODEL), lambda i: (0, 0)),
                  pl.BlockSpec((IN_WIDTH, D_MODEL), lambda i: (0, 0)), row(128), row(128)],
        out_specs=[row(D_MODEL), row(POOL_WIDTH)] + streams,
        out_shape=[jax.ShapeDtypeStruct((S, D_MODEL), bf16), jax.ShapeDtypeStruct((S, POOL_WIDTH), f32)]
        + [_stream_shape(S, d) for d in DILATIONS] * 3,
        scratch_shapes=_stage_scratch(ts),
        compiler_params=_params("parallel"),
    )(x, g1, w_in, cos_t, sin_t)
    n = len(DILATIONS)
    return res[0], res[1], res[2:2 + n], res[2 + n:2 + 2 * n], res[2 + 2 * n:]


POOL_HALO = 16


def _pool_lane_group(rows):
    return lax.broadcasted_iota(jnp.int32, (rows, POOL_WIDTH), 1) // POOL_GROUP


def _pool_select(group, s2, s4, s8, s16):
    return jnp.where(group == 0, s2, jnp.where(group == 1, s4, jnp.where(group == 2, s8, s16)))


def _pool_count(t0, rows):
    group = _pool_lane_group(rows)
    t = t0 + lax.broadcasted_iota(jnp.int32, (rows, POOL_WIDTH), 0)
    win = _pool_select(group, 2, 4, 8, 16)
    return jnp.minimum(t + 1, win).astype(f32)


def _pool_diff(u_halo, u_tile, t0):
    ts = u_tile.shape[0]
    ext = jnp.concatenate([u_halo, u_tile], axis=0)
    s2 = ext + pltpu.roll(ext, 1, 0)
    s4 = s2 + pltpu.roll(s2, 2, 0)
    s8 = s4 + pltpu.roll(s4, 4, 0)
    s16 = s8 + pltpu.roll(s8, 8, 0)
    group = _pool_lane_group(ts + POOL_HALO)
    wsum = _pool_select(group, s2, s4, s8, s16)[POOL_HALO:]
    return wsum / _pool_count(t0, ts) - u_tile


def _pool_specs(ts, n_tiles):
    tile = pl.BlockSpec((ts, POOL_WIDTH), lambda i: (i, 0))
    per = ts // POOL_HALO
    before = pl.BlockSpec((POOL_HALO, POOL_WIDTH), lambda i: (jnp.maximum(i * per - 1, 0), 0))
    after = pl.BlockSpec((POOL_HALO, POOL_WIDTH), lambda i: (jnp.minimum((i + 1) * per, n_tiles * per - 1), 0))
    return tile, before, after


def _pool_fwd(u, w_bd, scale):
    S = u.shape[0]
    ts = 512
    n_tiles = S // ts

    def body(u_ref, halo_ref, w_ref, sc_ref, y_ref):
        i = pl.program_id(0)
        halo = jnp.where(i > 0, halo_ref[...], 0.0)
        d = _pool_diff(halo, u_ref[...], i * ts)
        y_ref[...] = (_dot(d.astype(bf16), w_ref[...]) * sc_ref[...]).astype(bf16)

    tile, before, _ = _pool_specs(ts, n_tiles)
    return pl.pallas_call(
        body, name="pool_fwd", grid=(n_tiles,),
        in_specs=[tile, before, pl.BlockSpec((POOL_WIDTH, POOL_WIDTH), lambda i: (0, 0)),
                  pl.BlockSpec((1, POOL_WIDTH), lambda i: (0, 0))],
        out_specs=tile, out_shape=jax.ShapeDtypeStruct((S, POOL_WIDTH), bf16),
        compiler_params=_params("parallel"),
    )(u, u, w_bd, scale)


def _pool_bwd(u, dy, w_bd, scale):
    S = u.shape[0]
    ts = 512
    n_tiles = S // ts

    def body(u_ref, halo_ref, dy_ref, dy_next_ref, w_ref, sc_ref, du_ref, dw_ref, dsc_ref):
        i = pl.program_id(0)

        @pl.when(i == 0)
        def _():
            dw_ref[...] = jnp.zeros_like(dw_ref)
            dsc_ref[...] = jnp.zeros_like(dsc_ref)

        halo = jnp.where(i > 0, halo_ref[...], 0.0)
        d = _pool_diff(halo, u_ref[...], i * ts).astype(bf16)
        w = w_ref[...]
        sc = sc_ref[...]
        dy_tile = dy_ref[...]
        z = _dot(d, w)
        dsc_ref[...] += jnp.sum(dy_tile * z, axis=0, keepdims=True)
        dy_next = jnp.where(i < n_tiles - 1, dy_next_ref[...], 0.0)
        dz = (jnp.concatenate([dy_tile, dy_next], axis=0) * sc).astype(bf16)
        dw_ref[...] += _dot_tn(d, dz[:ts])
        dd = _dot_nt(dz, w)
        e = dd / _pool_count(i * ts, ts + POOL_HALO)
        n = ts + POOL_HALO
        f2 = e + pltpu.roll(e, n - 1, 0)
        f4 = f2 + pltpu.roll(f2, n - 2, 0)
        f8 = f4 + pltpu.roll(f4, n - 4, 0)
        f16 = f8 + pltpu.roll(f8, n - 8, 0)
        fsum = _pool_select(_pool_lane_group(n), f2, f4, f8, f16)
        du_ref[...] = (fsum[:ts] - dd[:ts]).astype(bf16)

    tile, before, after = _pool_specs(ts, n_tiles)
    return pl.pallas_call(
        body, name="pool_bwd", grid=(n_tiles,),
        in_specs=[tile, before, tile, after, pl.BlockSpec((POOL_WIDTH, POOL_WIDTH), lambda i: (0, 0)),
                  pl.BlockSpec((1, POOL_WIDTH), lambda i: (0, 0))],
        out_specs=[tile, pl.BlockSpec((POOL_WIDTH, POOL_WIDTH), lambda i: (0, 0)),
                   pl.BlockSpec((1, POOL_WIDTH), lambda i: (0, 0))],
        out_shape=[jax.ShapeDtypeStruct((S, POOL_WIDTH), bf16), jax.ShapeDtypeStruct((POOL_WIDTH, POOL_WIDTH), f32),
                   jax.ShapeDtypeStruct((1, POOL_WIDTH), f32)],
        compiler_params=_params("arbitrary"),
    )(u, u, dy, dy, w_bd, scale)


SUPER = BLOCK * DILATIONS[-1]
UNITS = SUPER // BLOCK
FWD_UNROLL = 16
BWD_UNROLL = 8


def _band_mask(has_prev):
    qi = lax.broadcasted_iota(jnp.int32, (BLOCK, 2 * BLOCK), 0)
    kj = lax.broadcasted_iota(jnp.int32, (BLOCK, 2 * BLOCK), 1)
    return (kj >= qi) & (kj <= qi + BLOCK) & ((kj >= BLOCK) | has_prev)


def _head0_mask(rows=BLOCK):
    return lax.broadcasted_iota(jnp.int32, (rows, 128), 1) < HEAD_DIM


def _band_mask_t(has_prev):
    ki = lax.broadcasted_iota(jnp.int32, (2 * BLOCK, 2 * BLOCK), 0)
    qj = lax.broadcasted_iota(jnp.int32, (2 * BLOCK, 2 * BLOCK), 1) % BLOCK
    return (ki >= qj) & (ki <= qj + BLOCK) & ((ki >= BLOCK) | has_prev)


def _head_pair_rows(a, h0):
    zero = jnp.zeros_like(a)
    return jnp.concatenate([jnp.where(h0, a, zero), jnp.where(h0, zero, a)], axis=0)


def _per_query_row(stat):
    t = stat.T
    return jnp.concatenate([t[0:1], t[HEAD_DIM:HEAD_DIM + 1]], axis=1)


def _natural_rows(d, r, n):
    if d == 1:
        return pl.ds(pl.multiple_of(n * BLOCK, BLOCK), BLOCK)
    return pl.ds(n * (BLOCK * d) + r, BLOCK, stride=d)


def _unit_place(d, u):
    per_stream = UNITS // d
    return u // per_stream, u % per_stream, per_stream


def _block_rows(n):
    return pl.ds(pl.multiple_of(n * BLOCK, BLOCK), BLOCK)


def _band(cur_ref, tail_ref, r, n):
    before = jnp.where(n > 0, cur_ref[r, _block_rows(jnp.maximum(n - 1, 0)), :], tail_ref[r])
    return jnp.concatenate([before, cur_ref[r, _block_rows(n), :]], axis=0)


def _attn_in_specs(S, with_do):
    specs = []
    last = S // SUPER - 1
    for d in DILATIONS:
        per_stream = UNITS // d
        cur = pl.BlockSpec((d, SUPER // d, 128), lambda hp, sb: (0, jnp.minimum(sb, last), hp))
        tail = pl.BlockSpec(
            (d, BLOCK, 128),
            lambda hp, sb, per_stream=per_stream: (0, jnp.maximum(jnp.minimum(sb, last) * per_stream - 1, 0), hp))
        specs += [cur] * (2 if with_do else 1) + [cur, tail, cur, tail]
    return specs


def _attn_fwd(qs, ks, vs, pack):
    S = qs[0].shape[1]
    n_dil = len(DILATIONS)
    n_steps = S // SUPER
    n_total = (ATTN_WIDTH // 128) * n_steps

    def body(*refs):
        ins, pack_ref = refs[:5 * n_dil], refs[5 * n_dil]
        out_ref, lse_ref, gathered_ref = refs[5 * n_dil + 1:5 * n_dil + 4]
        scratch = refs[5 * n_dil + 4:]
        o_sc, l_sc = scratch[:n_dil], scratch[n_dil:2 * n_dil]
        gather = _Gather(pack_ref, gathered_ref, *scratch[2 * n_dil:])
        sb = pl.program_id(1)
        step = pl.program_id(0) * n_steps + sb

        @pl.when(step == 0)
        def _():
            gather.start()

        h0 = _head0_mask()
        for ci, d in enumerate(DILATIONS):
            q_ref, kc_ref, kp_ref, vc_ref, vp_ref = ins[5 * ci:5 * ci + 5]

            def unit(u, carry, d=d, ci=ci, q_ref=q_ref, kc_ref=kc_ref, kp_ref=kp_ref, vc_ref=vc_ref, vp_ref=vp_ref):
                r, n, _ = _unit_place(d, u)
                qv = q_ref[r, _block_rows(n), :]
                kb = _band(kc_ref, kp_ref, r, n)
                vb = _band(vc_ref, vp_ref, r, n)
                valid = _band_mask((sb > 0) | (n > 0))
                outs, lses = [], []
                for h in range(2):
                    keep = h0 if h == 0 else jnp.logical_not(h0)
                    qh = jnp.where(keep, qv, jnp.zeros_like(qv))
                    s = jnp.where(valid, _dot_nt(qh, kb), NEG)
                    m = jnp.max(s, axis=1, keepdims=True)
                    e = jnp.exp(s - m)
                    den = jnp.sum(e, axis=1, keepdims=True)
                    outs.append(_dot(e.astype(bf16), vb) * (1.0 / den))
                    lses.append(jnp.broadcast_to(m + jnp.log(den), (BLOCK, 128)))
                rows = _natural_rows(d, r, n)
                o_sc[ci][rows, :] = jnp.where(h0, outs[0], outs[1])
                l_sc[ci][rows, :] = jnp.where(h0, lses[0], lses[1])
                return carry

            lax.fori_loop(0, UNITS, unit, 0, unroll=FWD_UNROLL)

        def merge(t, carry):
            rows = pl.ds(pl.multiple_of(t * 256, 256), 256)
            a, b, c = l_sc[0][rows, :], l_sc[1][rows, :], l_sc[2][rows, :]
            m = jnp.maximum(jnp.maximum(a, b), c)
            ea, eb, ec = jnp.exp(a - m), jnp.exp(b - m), jnp.exp(c - m)
            tot = ea + eb + ec
            out_ref[rows, :] = ((ea / tot) * o_sc[0][rows, :] + (eb / tot) * o_sc[1][rows, :]
                                + (ec / tot) * o_sc[2][rows, :]).astype(bf16)
            lse_ref[rows, :] = m + jnp.log(tot)
            return carry

        lax.fori_loop(0, SUPER // 256, merge, 0)

        @pl.when(step == (2 * n_total) // 3)
        def _():
            gather.pass_on()

        @pl.when(step == n_total - 1)
        def _():
            gather.finish()

    args = []
    for q, k, v in zip(qs, ks, vs):
        args += [q, k, k, v, v]
    nat = pl.BlockSpec((SUPER, 128), lambda hp, sb: (sb, hp))
    rows = pack.shape[0]
    return pl.pallas_call(
        body, name="attn_fwd", grid=(ATTN_WIDTH // 128, n_steps),
        in_specs=_attn_in_specs(S, False) + [ANY], out_specs=[nat, nat, ANY],
        out_shape=[jax.ShapeDtypeStruct((S, ATTN_WIDTH), bf16), jax.ShapeDtypeStruct((S, ATTN_WIDTH), f32),
                   _Gather.out_shape(rows, pack.dtype)],
        scratch_shapes=[pltpu.VMEM((SUPER, 128), f32)] * (2 * n_dil) + _Gather.scratch(rows, pack.dtype),
        compiler_params=_params("arbitrary", "arbitrary"),
    )(*args, pack)


def _attn_bwd(qs, ks, vs, dos, lse, delta, chip_sum):
    S = qs[0].shape[1]
    n_steps = S // SUPER
    last = n_steps - 1
    n_dil = len(DILATIONS)
    n_total = (ATTN_WIDTH // 128) * (n_steps + 1)

    def body(*refs):
        ins, (lse_ref, dl_ref, sum_ref) = refs[:6 * n_dil], refs[6 * n_dil:6 * n_dil + 3]
        dq_ref, dk_ref, dv_ref, others_ref = refs[6 * n_dil + 3:6 * n_dil + 7]
        dq_acc, dk_acc, dv_acc = refs[6 * n_dil + 7:6 * n_dil + 10]
        scatter = _Scatter(sum_ref, others_ref, *refs[6 * n_dil + 10:])
        sb = pl.program_id(1)
        step = pl.program_id(0) * (n_steps + 1) + sb
        cur = sb % 2
        prv = 1 - cur

        @pl.when(step == 0)
        def _():
            scatter.start()

        @pl.when(sb < n_steps)
        def _():
            dq_acc[...] = jnp.zeros_like(dq_acc)
            dk_acc[cur] = jnp.zeros((SUPER, 128), f32)
            dv_acc[cur] = jnp.zeros((SUPER, 128), f32)
            h0 = _head0_mask()
            for ci, d in enumerate(DILATIONS):
                q_ref, do_ref, kc_ref, kp_ref, vc_ref, vp_ref = ins[6 * ci:6 * ci + 6]

                def unit(u, carry, d=d, q_ref=q_ref, do_ref=do_ref, kc_ref=kc_ref, kp_ref=kp_ref, vc_ref=vc_ref,
                         vp_ref=vp_ref):
                    r, n, per_stream = _unit_place(d, u)
                    qv = q_ref[r, _block_rows(n), :]
                    dov = do_ref[r, _block_rows(n), :]
                    kb = _band(kc_ref, kp_ref, r, n)
                    vb = _band(vc_ref, vp_ref, r, n)
                    rows = _natural_rows(d, r, n)
                    has_prev = (sb > 0) | (n > 0)
                    q_pair = _head_pair_rows(qv, h0)
                    do_pair = _head_pair_rows(dov, h0)
                    s_t = jnp.where(_band_mask_t(has_prev), _dot_nt(kb, q_pair), NEG)
                    p_t = jnp.exp(s_t - _per_query_row(lse_ref[rows, :]))
                    dp_t = _dot_nt(vb, do_pair)
                    ds_t = (p_t * (dp_t - _per_query_row(dl_ref[rows, :]))).astype(bf16)
                    dvb = _dot(p_t.astype(bf16), do_pair)
                    dkb = _dot(ds_t, q_pair)
                    dq_pair = _dot_tn(ds_t, kb)
                    dq_acc[rows, :] += jnp.where(h0, dq_pair[:BLOCK], dq_pair[BLOCK:])
                    dk_acc[cur, rows, :] += dkb[BLOCK:]
                    dv_acc[cur, rows, :] += dvb[BLOCK:]

                    slot = jnp.where((n > 0) | (sb == 0), cur, prv)
                    before = _natural_rows(d, r, jnp.where(n > 0, n - 1, per_stream - 1))
                    dk_acc[slot, before, :] += dkb[:BLOCK]
                    dv_acc[slot, before, :] += dvb[:BLOCK]
                    return carry

                lax.fori_loop(0, UNITS, unit, 0, unroll=BWD_UNROLL)
            dq_ref[...] = (dq_acc[...] * ATTN_SCALE).astype(bf16)

        @pl.when(sb > 0)
        def _():
            dk_ref[...] = dk_acc[prv].astype(bf16)
            dv_ref[...] = dv_acc[prv].astype(bf16)

        @pl.when(step == n_total - 1)
        def _():
            scatter.finish()

    args = []
    for q, k, v, do in zip(qs, ks, vs, dos):
        args += [q, do, k, k, v, v]
    nat = pl.BlockSpec((SUPER, 128), lambda hp, sb: (jnp.minimum(sb, last), hp))
    nat_before = pl.BlockSpec((SUPER, 128), lambda hp, sb: (jnp.clip(sb - 1, 0, last), hp))
    out = jax.ShapeDtypeStruct((S, ATTN_WIDTH), bf16)
    half = chip_sum.shape[1]
    return pl.pallas_call(
        body, name="attn_bwd", grid=(ATTN_WIDTH // 128, n_steps + 1),
        in_specs=_attn_in_specs(S, True) + [nat, nat, ANY], out_specs=[nat, nat_before, nat_before, ANY],
        out_shape=[out, out, out, _Scatter.out_shape(half, chip_sum.dtype)],
        scratch_shapes=[pltpu.VMEM((SUPER, 128), f32), pltpu.VMEM((2, SUPER, 128), f32),
                        pltpu.VMEM((2, SUPER, 128), f32)] + _Scatter.scratch(half),
        compiler_params=_params("arbitrary", "arbitrary"),
    )(*args, lse, delta, chip_sum)


def _rms(v):
    return lax.rsqrt(jnp.mean(v * v, axis=-1, keepdims=True) + EPS)


def _out_proj(pool_out, attn_out, w_out, x, g2, g3):
    S = x.shape[0]
    ts = 512

    def body(p_ref, a_ref, w_ref, x_ref, g2_ref, g3_ref, mix_ref, x2_ref, h2_ref):
        mix = _dot(p_ref[...], w_ref[:POOL_WIDTH, :]) + _dot(a_ref[...], w_ref[POOL_WIDTH:, :])
        mix_ref[...] = mix
        x2 = x_ref[...] + (mix * _rms(mix)) * g2_ref[...]
        x2_ref[...] = x2
        h2_ref[...] = ((x2 * _rms(x2)) * g3_ref[...]).astype(bf16)

    row = lambda w: pl.BlockSpec((ts, w), lambda i: (i, 0))
    gain = pl.BlockSpec((1, D_MODEL), lambda i: (0, 0))
    return pl.pallas_call(
        body, name="out_proj", grid=(S // ts,),
        in_specs=[row(POOL_WIDTH), row(ATTN_WIDTH), pl.BlockSpec((D_MODEL, D_MODEL), lambda i: (0, 0)),
                  row(D_MODEL), gain, gain],
        out_specs=[row(D_MODEL)] * 3,
        out_shape=[jax.ShapeDtypeStruct((S, D_MODEL), f32), jax.ShapeDtypeStruct((S, D_MODEL), f32),
                   jax.ShapeDtypeStruct((S, D_MODEL), bf16)],
        compiler_params=_params("parallel"),
    )(pool_out, attn_out, w_out, x, g2, g3)


FF_TILE = 256
FF_STEP_ROWS = 2048
FF_ROWS = 512
FF_BWD_ROWS = 256


def _sigmoid(g):
    return 1.0 / (1.0 + jnp.exp(-g))


def _ff_act_shape(S):
    return jax.ShapeDtypeStruct((D_FF // FF_TILE, S, FF_TILE), bf16)


def _ff_act_spec(ts):
    return pl.BlockSpec((1, ts, FF_TILE), lambda i, j: (j, i, 0))


def _ffn_fwd(h2, w_gate, w_up, w_down):
    S = h2.shape[0]
    ts = min(S, FF_STEP_ROWS)

    def body(h_ref, wg_ref, wu_ref, wd_ref, gate_ref, up_ref, f_ref):
        def rows_pass(first):
            def sub(i, carry):
                rows = pl.ds(pl.multiple_of(i * FF_ROWS, FF_ROWS), FF_ROWS)
                h = h_ref[rows, :]
                gate = _dot_nt(h, wg_ref[...])
                up = _dot_nt(h, wu_ref[...])
                gate_ref[0, rows, :] = gate.astype(bf16)
                up_ref[0, rows, :] = up.astype(bf16)
                part = _dot((gate * _sigmoid(gate) * up).astype(bf16), wd_ref[...])
                if first:
                    f_ref[rows, :] = part
                else:
                    f_ref[rows, :] += part
                return carry

            lax.fori_loop(0, ts // FF_ROWS, sub, 0, unroll=True)

        @pl.when(pl.program_id(1) == 0)
        def _():
            rows_pass(True)

        @pl.when(pl.program_id(1) > 0)
        def _():
            rows_pass(False)

    act = _ff_act_spec(ts)
    weight = pl.BlockSpec((FF_TILE, D_MODEL), lambda i, j: (j, 0))
    return pl.pallas_call(
        body, name="ffn_fwd", grid=(S // ts, D_FF // FF_TILE),
        in_specs=[pl.BlockSpec((ts, D_MODEL), lambda i, j: (i, 0)), weight, weight, weight],
        out_specs=[act, act, pl.BlockSpec((ts, D_MODEL), lambda i, j: (i, 0))],
        out_shape=[_ff_act_shape(S), _ff_act_shape(S), jax.ShapeDtypeStruct((S, D_MODEL), f32)],
        compiler_params=_params("parallel", "arbitrary"),
    )(h2, w_gate, w_up, w_down)


def _loss_head(f, x2, target, g4):
    S = f.shape[0]
    ts = 512

    def body(f_ref, x2_ref, t_ref, g_ref, dy_ref, df_ref, dg_ref, loss_ref):
        @pl.when(pl.program_id(0) == 0)
        def _():
            dg_ref[...] = jnp.zeros_like(dg_ref)
            loss_ref[...] = jnp.zeros_like(loss_ref)

        fv = f_ref[...]
        g = g_ref[...]
        r = _rms(fv)
        fhat = fv * r
        err = (x2_ref[...] + fhat * g) - t_ref[...]
        loss_ref[...] += 0.5 * jnp.sum(jnp.mean(err * err, axis=-1, keepdims=True), axis=0, keepdims=True)
        dy = err * (1.0 / D_MODEL)
        dy_ref[...] = dy
        dg_ref[...] += jnp.sum(dy * fhat, axis=0, keepdims=True)
        dyg = dy * g
        df_ref[...] = (r * (dyg - fhat * jnp.mean(dyg * fhat, axis=-1, keepdims=True))).astype(bf16)

    row = pl.BlockSpec((ts, D_MODEL), lambda i: (i, 0))
    gain = pl.BlockSpec((1, D_MODEL), lambda i: (0, 0))
    return pl.pallas_call(
        body, name="loss_head", grid=(S // ts,), in_specs=[row, row, row, gain],
        out_specs=[row, row, gain, pl.BlockSpec((1, 1), lambda i: (0, 0))],
        out_shape=[jax.ShapeDtypeStruct((S, D_MODEL), f32), jax.ShapeDtypeStruct((S, D_MODEL), bf16),
                   jax.ShapeDtypeStruct((1, D_MODEL), f32), jax.ShapeDtypeStruct((1, 1), f32)],
        compiler_params=_params("arbitrary"),
    )(f, x2, target, g4)


def _ffn_bwd(df, gate, up, w_gate, w_up, w_down):
    S = df.shape[0]
    ts = min(S, FF_STEP_ROWS)

    def body(df_ref, gate_ref, up_ref, wg_ref, wu_ref, wd_ref, a_ref, dgate_ref, dup_ref, dh_ref):
        def rows_pass(first):
            def sub(i, carry):
                rows = pl.ds(pl.multiple_of(i * FF_BWD_ROWS, FF_BWD_ROWS), FF_BWD_ROWS)
                da = _dot_nt(df_ref[rows, :], wd_ref[...])
                g = gate_ref[0, rows, :].astype(f32)
                u = up_ref[0, rows, :].astype(f32)
                sig = _sigmoid(g)
                silu = g * sig
                a_ref[0, rows, :] = (silu * u).astype(bf16)
                dup = (da * silu).astype(bf16)
                dgate = (da * u * (sig * (1.0 + g * (1.0 - sig)))).astype(bf16)
                dup_ref[0, rows, :] = dup
                dgate_ref[0, rows, :] = dgate
                part = _dot(dgate, wg_ref[...]) + _dot(dup, wu_ref[...])
                if first:
                    dh_ref[rows, :] = part
                else:
                    dh_ref[rows, :] += part
                return carry

            lax.fori_loop(0, ts // FF_BWD_ROWS, sub, 0, unroll=True)

        @pl.when(pl.program_id(1) == 0)
        def _():
            rows_pass(True)

        @pl.when(pl.program_id(1) > 0)
        def _():
            rows_pass(False)

    act = _ff_act_spec(ts)
    row = pl.BlockSpec((ts, D_MODEL), lambda i, j: (i, 0))
    return pl.pallas_call(
        body, name="ffn_bwd", grid=(S // ts, D_FF // FF_TILE),
        in_specs=[row, act, act,
                  pl.BlockSpec((FF_TILE, D_MODEL), lambda i, j: (j, 0)),
                  pl.BlockSpec((FF_TILE, D_MODEL), lambda i, j: (j, 0)),
                  pl.BlockSpec((FF_TILE, D_MODEL), lambda i, j: (j, 0))],
        out_specs=[act, act, act, row],
        out_shape=[_ff_act_shape(S)] * 3 + [jax.ShapeDtypeStruct((S, D_MODEL), f32)],
        compiler_params=_params("parallel", "arbitrary"),
    )(df, gate, up, w_gate, w_up, w_down)


def _norm_bwd(dh2, dy, x2, mix, g3, g2):
    S = dh2.shape[0]
    ts = 512

    def body(dh_ref, dy_ref, x2_ref, mix_ref, g3_ref, g2_ref, dx2_ref, dmix_ref, dg3_ref, dg2_ref):
        @pl.when(pl.program_id(0) == 0)
        def _():
            dg3_ref[...] = jnp.zeros_like(dg3_ref)
            dg2_ref[...] = jnp.zeros_like(dg2_ref)

        dh = dh_ref[...]
        x2 = x2_ref[...]
        r3 = _rms(x2)
        xhat = x2 * r3
        dg3_ref[...] += jnp.sum(dh * xhat, axis=0, keepdims=True)
        dhg = dh * g3_ref[...]
        dx2 = dy_ref[...] + r3 * (dhg - xhat * jnp.mean(dhg * xhat, axis=-1, keepdims=True))
        dx2_ref[...] = dx2
        mix = mix_ref[...]
        r2 = _rms(mix)
        mhat = mix * r2
        dg2_ref[...] += jnp.sum(dx2 * mhat, axis=0, keepdims=True)
        dmg = dx2 * g2_ref[...]
        dmix_ref[...] = (r2 * (dmg - mhat * jnp.mean(dmg * mhat, axis=-1, keepdims=True))).astype(bf16)

    row = pl.BlockSpec((ts, D_MODEL), lambda i: (i, 0))
    gain = pl.BlockSpec((1, D_MODEL), lambda i: (0, 0))
    return pl.pallas_call(
        body, name="norm_bwd", grid=(S // ts,), in_specs=[row, row, row, row, gain, gain],
        out_specs=[row, row, gain, gain],
        out_shape=[jax.ShapeDtypeStruct((S, D_MODEL), f32), jax.ShapeDtypeStruct((S, D_MODEL), bf16),
                   jax.ShapeDtypeStruct((1, D_MODEL), f32), jax.ShapeDtypeStruct((1, D_MODEL), f32)],
        compiler_params=_params("arbitrary"),
    )(dh2, dy, x2, mix, g3, g2)


def _out_proj_bwd(dmix, w_out, attn_out, head_ones, grads):
    S = dmix.shape[0]
    ts = 512
    n_dil = len(DILATIONS)

    def body(dm_ref, w_ref, o_ref, ones_ref, g_ref, dp_ref, dl_ref, *rest):
        do_refs, theirs_ref = rest[:n_dil], rest[n_dil]
        stage = rest[n_dil + 1:n_dil + 1 + N_STAGE]
        swap = _Swap(g_ref, theirs_ref, *rest[n_dil + 1 + N_STAGE:])

        @pl.when(pl.program_id(0) == 0)
        def _():
            swap.start()

        @pl.when(pl.program_id(0) == S // ts - 1)
        def _():
            swap.finish()

        dcat = _dot_nt(dm_ref[...], w_ref[...])
        dp_ref[...] = dcat[:, :POOL_WIDTH]
        do = dcat[:, POOL_WIDTH:]
        for j in range(ATTN_WIDTH // 128):
            stage[j][...] = do[:, j * 128:(j + 1) * 128]
        _store_streams(stage, do_refs, ts)
        prod = do * o_ref[...].astype(f32)
        hi = prod.astype(bf16)
        lo = (prod - hi.astype(f32)).astype(bf16)
        ones = ones_ref[...]
        for j in range(ATTN_WIDTH // 128):
            cols = slice(j * 128, (j + 1) * 128)
            dl_ref[:, cols] = _dot(hi[:, cols], ones) + _dot(lo[:, cols], ones)

    row = lambda w: pl.BlockSpec((ts, w), lambda i: (i, 0))
    res = pl.pallas_call(
        body, name="out_proj_bwd", grid=(S // ts,),
        in_specs=[row(D_MODEL), pl.BlockSpec((D_MODEL, D_MODEL), lambda i: (0, 0)), row(ATTN_WIDTH),
                  pl.BlockSpec((128, 128), lambda i: (0, 0)), ANY],
        out_specs=[row(POOL_WIDTH), row(ATTN_WIDTH)] + [_stream_spec(d, ts) for d in DILATIONS] + [ANY],
        out_shape=[jax.ShapeDtypeStruct((S, POOL_WIDTH), f32), jax.ShapeDtypeStruct((S, ATTN_WIDTH), f32)]
        + [_stream_shape(S, d) for d in DILATIONS] + [_Swap.out_shape(grads)],
        scratch_shapes=_stage_scratch(ts) + _Swap.scratch(grads),
        compiler_params=_params("arbitrary"),
    )(dmix, w_out, attn_out, head_ones, grads)
    return res[0], res[1], res[2:2 + n_dil], res[2 + n_dil]


def _in_proj_bwd(du, dq, dk, dv, cos_t, sin_t, w_in, x, dx2, g1):
    S = x.shape[0]
    ts = 512

    def body(du_ref, dq_ref, dk_ref, dv_ref, cos_ref, sin_ref, w_ref, x_ref, dx2_ref, g_ref, gx_ref, dproj_ref, dg_ref):
        @pl.when(pl.program_id(0) == 0)
        def _():
            dg_ref[...] = jnp.zeros_like(dg_ref)

        dproj_ref[:, :POOL_WIDTH] = du_ref[...]
        cos = cos_ref[...]
        sin = sin_ref[...]
        first = _first_half_mask(ts)
        for j in range(ATTN_WIDTH // 128):
            cols = slice(j * 128, (j + 1) * 128)
            for base, ref in ((POOL_WIDTH, dq_ref), (POOL_WIDTH + ATTN_WIDTH, dk_ref)):
                g = ref[:, cols].astype(f32)
                pre = g * cos + _rope_partner(g * sin, first)
                dproj_ref[:, base + j * 128: base + (j + 1) * 128] = pre.astype(bf16)
        dproj_ref[:, POOL_WIDTH + 2 * ATTN_WIDTH:] = dv_ref[...]

        dh = _dot(dproj_ref[...], w_ref[...])
        xv = x_ref[...]
        r = _rms(xv)
        xhat = xv * r
        dg_ref[...] += jnp.sum(dh * xhat, axis=0, keepdims=True)
        dhg = dh * g_ref[...]
        gx_ref[...] = dx2_ref[...] + r * (dhg - xhat * jnp.mean(dhg * xhat, axis=-1, keepdims=True))

    row = lambda w: pl.BlockSpec((ts, w), lambda i: (i, 0))
    gain = pl.BlockSpec((1, D_MODEL), lambda i: (0, 0))
    return pl.pallas_call(
        body, name="in_proj_bwd", grid=(S // ts,),
        in_specs=[row(POOL_WIDTH)] + [row(ATTN_WIDTH)] * 3 + [row(128), row(128),
                  pl.BlockSpec((IN_WIDTH, D_MODEL), lambda i: (0, 0)), row(D_MODEL), row(D_MODEL), gain],
        out_specs=[row(D_MODEL), row(IN_WIDTH), gain],
        out_shape=[jax.ShapeDtypeStruct((S, D_MODEL), f32), jax.ShapeDtypeStruct((S, IN_WIDTH), bf16),
                   jax.ShapeDtypeStruct((1, D_MODEL), f32)],
        compiler_params=_params("arbitrary"),
    )(du, dq, dk, dv, cos_t, sin_t, w_in, x, dx2, g1)


def _matmul_tiles_tn(a, b, name):
    T, K, w = a.shape
    N = b.shape[1]
    tk = 1024

    def body(a_ref, b_ref, o_ref):
        def tiles_pass(first):
            for t in range(T):
                part = _dot_tn(a_ref[t], b_ref[...])
                if first:
                    o_ref[t * w:(t + 1) * w, :] = part
                else:
                    o_ref[t * w:(t + 1) * w, :] += part

        @pl.when(pl.program_id(0) == 0)
        def _():
            tiles_pass(True)

        @pl.when(pl.program_id(0) > 0)
        def _():
            tiles_pass(False)

    return pl.pallas_call(
        body, name=name, grid=(K // tk,),
        in_specs=[pl.BlockSpec((T, tk, w), lambda k: (0, k, 0)), pl.BlockSpec((tk, N), lambda k: (k, 0))],
        out_specs=pl.BlockSpec((T * w, N), lambda k: (0, 0)),
        out_shape=jax.ShapeDtypeStruct((T * w, N), f32),
        compiler_params=_params("arbitrary"),
    )(a, b)


def _matmul_tn(a, b, name):
    K, M = a.shape
    N = b.shape[1]
    tk = 1024

    def body(a_ref, b_ref, o_ref):
        _tn_step(a_ref, b_ref, o_ref, M)

    return pl.pallas_call(
        body, name=name, grid=(K // tk,),
        in_specs=[pl.BlockSpec((tk, M), lambda k: (k, 0)), pl.BlockSpec((tk, N), lambda k: (k, 0))],
        out_specs=pl.BlockSpec((M, N), lambda k: (0, 0)),
        out_shape=jax.ShapeDtypeStruct((M, N), f32),
        compiler_params=_params("arbitrary"),
    )(a, b)


def _tn_step(a_ref, b_ref, o_ref, M):
    w = 256

    def tiles_pass(first):
        for t in range(M // w):
            part = _dot_tn(a_ref[:, t * w:(t + 1) * w], b_ref[...])
            if first:
                o_ref[t * w:(t + 1) * w, :] = part
            else:
                o_ref[t * w:(t + 1) * w, :] += part

    @pl.when(pl.program_id(0) == 0)
    def _():
        tiles_pass(True)

    @pl.when(pl.program_id(0) > 0)
    def _():
        tiles_pass(False)


def _matmul_tn_and_small_sum(a, b, block, name):
    K, M = a.shape
    N = b.shape[1]
    tk = 1024
    n_steps = K // tk

    def body(a_ref, b_ref, block_ref, o_ref, total_ref, *scratch):
        small = _SmallSum(block_ref, *scratch)

        @pl.when(pl.program_id(0) == 0)
        def _():
            small.start()

        _tn_step(a_ref, b_ref, o_ref, M)

        @pl.when(pl.program_id(0) == n_steps - 1)
        def _():
            small.finish(total_ref)

    return pl.pallas_call(
        body, name=name, grid=(n_steps,),
        in_specs=[pl.BlockSpec((tk, M), lambda k: (k, 0)), pl.BlockSpec((tk, N), lambda k: (k, 0)), ANY],
        out_specs=[pl.BlockSpec((M, N), lambda k: (0, 0)), pl.BlockSpec(block.shape, lambda k: (0, 0))],
        out_shape=[jax.ShapeDtypeStruct((M, N), f32), jax.ShapeDtypeStruct(block.shape, block.dtype)],
        scratch_shapes=_SmallSum.scratch(block),
        compiler_params=_params("arbitrary"),
    )(a, b, block)


def _rope_tables(S):
    half = HEAD_DIM // 2
    freqs = ROPE_THETA ** (-jnp.arange(half, dtype=f32) * (2.0 / HEAD_DIM))
    ang = jnp.arange(S).astype(f32)[:, None] * freqs[None, :]
    cos = jnp.tile(jnp.cos(ang), (1, 4))
    sin = jnp.sin(ang)
    sin = jnp.tile(jnp.concatenate([-sin, sin], axis=1), (1, 2))
    return cos, sin


def _block_diag(w_pool):
    w = jnp.zeros((POOL_WIDTH, POOL_WIDTH), w_pool.dtype)
    for g in range(POOL_WIDTH // POOL_GROUP):
        w = lax.dynamic_update_slice(w, w_pool[g], (g * POOL_GROUP, g * POOL_GROUP))
    return w


def _head_ones():
    head = np.arange(128) // HEAD_DIM
    return jnp.asarray(head[:, None] == head[None, :], dtype=bf16)


def _place():
    x, y, c = lax.axis_index("x"), lax.axis_index("y"), lax.axis_index("c")
    chips = [(1 - x, y), (x, 1 - y), (1 - x, 1 - y)]
    return x, y, c, chips


ANY = pl.BlockSpec(memory_space=pl.ANY)
N_PEER_CHIPS = N_CHIPS - 1
ICI_PIECES = 4
D2D_PIECES = 8
LOCAL_PIECES = 8


def _row_chunks(rows, n, unit=32):
    units = rows // unit
    out, start = [], 0
    for i in range(n):
        size = (units // n + (1 if i < units % n else 0)) * unit
        out.append((start, size))
        start += size
    return [piece for piece in out if piece[1]]


class _LocalCopy:
    def __init__(self, src_rows, dst_rows, rows, buf, sems_in, sems_out):
        self.loads, self.stores = [], []
        for i, (start, size) in enumerate(_row_chunks(rows, LOCAL_PIECES)):
            r = pl.ds(start, size)
            self.loads.append(pltpu.make_async_copy(src_rows(r), buf.at[r], sems_in.at[i]))
            self.stores.append(pltpu.make_async_copy(buf.at[r], dst_rows(r), sems_out.at[i]))

    def start(self):
        for cp in self.loads:
            cp.start()

    def pass_on(self):
        for load, store in zip(self.loads, self.stores):
            load.wait()
            store.start()

    def finish(self):
        for store in self.stores:
            store.wait()

    @staticmethod
    def scratch(rows, dtype):
        return [pltpu.VMEM((rows, D_MODEL), dtype), pltpu.SemaphoreType.DMA((LOCAL_PIECES,)),
                pltpu.SemaphoreType.DMA((LOCAL_PIECES,))]


class _Gather:
    def __init__(self, w_ref, out_ref, send1, recv1, send2, recv2, buf, sems_in, sems_out):
        x, y, c, chips = _place()
        me = 2 * x + y
        rows = w_ref.shape[0]
        half = rows // 2
        pieces = _row_chunks(half, ICI_PIECES)
        self.own = _LocalCopy(lambda r: w_ref.at[r], lambda r: out_ref.at[me, r], rows, buf, sems_in, sems_out)

        def rows_of(core, piece):
            start, size = piece
            return pl.ds(core * half + start, size)

        self.sends, self.arrivals, self.forwards, self.forward_arrivals = [], [], [], []
        for i, piece in enumerate(pieces):
            for j, (cx, cy) in enumerate(chips):
                k = j * len(pieces) + i
                there = 2 * cx + cy

                def direct(src_chip, cx=cx, cy=cy, k=k, piece=piece):
                    return pltpu.make_async_remote_copy(
                        src_ref=w_ref.at[rows_of(c, piece)], dst_ref=out_ref.at[src_chip, rows_of(c, piece)],
                        send_sem=send1.at[k], recv_sem=recv1.at[k], device_id=(cx, cy, c), device_id_type=MESH)

                def passed(core, there=there, k=k, piece=piece):
                    return pltpu.make_async_remote_copy(
                        src_ref=out_ref.at[there, rows_of(core, piece)], dst_ref=out_ref.at[there, rows_of(core, piece)],
                        send_sem=send2.at[k], recv_sem=recv2.at[k], device_id=(x, y, 1 - c), device_id_type=MESH)

                self.sends.append(direct(me))
                self.arrivals.append(direct(there))
                self.forwards.append(passed(c))
                self.forward_arrivals.append(passed(1 - c))

    def start(self):
        for cp in self.sends:
            cp.start()
        self.own.start()

    def pass_on(self):
        self.own.pass_on()
        for arrival, forward in zip(self.arrivals, self.forwards):
            arrival.wait_recv()
            forward.start()

    def finish(self):
        for arrival in self.forward_arrivals:
            arrival.wait_recv()
        for cp in self.sends + self.forwards:
            cp.wait_send()
        self.own.finish()

    @staticmethod
    def scratch(rows, dtype):
        n = N_PEER_CHIPS * len(_row_chunks(rows // 2, ICI_PIECES))
        return [pltpu.SemaphoreType.DMA((n,))] * 4 + _LocalCopy.scratch(rows, dtype)

    @staticmethod
    def out_shape(rows, dtype):
        return jax.ShapeDtypeStruct((N_CHIPS, rows, D_MODEL), dtype)


def _gather_weights(pack):
    rows = pack.shape[0]

    def body(w_ref, out_ref, *scratch):
        gather = _Gather(w_ref, out_ref, *scratch)
        gather.start()
        gather.pass_on()
        gather.finish()

    return pl.pallas_call(
        body, name="gather_weights", in_specs=[ANY], out_specs=ANY, out_shape=_Gather.out_shape(rows, pack.dtype),
        scratch_shapes=_Gather.scratch(rows, pack.dtype),
        compiler_params=pltpu.CompilerParams(vmem_limit_bytes=VMEM_LIMIT_V7X),
    )(pack)


class _Scatter:
    def __init__(self, h_ref, out_ref, send, recv):
        x, y, c, chips = _place()
        pieces = _row_chunks(h_ref.shape[1], ICI_PIECES)
        self.copies = []
        for i, (start, size) in enumerate(pieces):
            for j, (cx, cy) in enumerate(chips):
                k = j * len(pieces) + i
                self.copies.append(pltpu.make_async_remote_copy(
                    src_ref=h_ref.at[2 * cx + cy, pl.ds(start, size)], dst_ref=out_ref.at[j, pl.ds(start, size)],
                    send_sem=send.at[k], recv_sem=recv.at[k], device_id=(cx, cy, c), device_id_type=MESH))

    def start(self):
        for cp in self.copies:
            cp.start()

    def finish(self):
        for cp in self.copies:
            cp.wait_recv()
        for cp in self.copies:
            cp.wait_send()

    @staticmethod
    def scratch(half):
        n = N_PEER_CHIPS * len(_row_chunks(half, ICI_PIECES))
        return [pltpu.SemaphoreType.DMA((n,))] * 2

    @staticmethod
    def out_shape(half, dtype):
        return jax.ShapeDtypeStruct((N_PEER_CHIPS, half, D_MODEL), dtype)


def _scatter_to_chips(h):
    half = h.shape[1]

    def body(h_ref, out_ref, send, recv):
        scatter = _Scatter(h_ref, out_ref, send, recv)
        scatter.start()
        scatter.finish()

    return pl.pallas_call(
        body, name="scatter_to_chips", in_specs=[ANY], out_specs=ANY, out_shape=_Scatter.out_shape(half, h.dtype),
        scratch_shapes=_Scatter.scratch(half),
    )(h)


class _Swap:
    def __init__(self, g_ref, theirs_ref, send, recv):
        x, y, c, _ = _place()
        half = g_ref.shape[1] // 2
        pieces = _row_chunks(half, D2D_PIECES)
        self.copies = []
        for s in range(N_CHIPS):
            for i, (start, size) in enumerate(pieces):
                k = s * len(pieces) + i
                self.copies.append(pltpu.make_async_remote_copy(
                    src_ref=g_ref.at[s, pl.ds((1 - c) * half + start, size)], dst_ref=theirs_ref.at[s, pl.ds(start, size)],
                    send_sem=send.at[k], recv_sem=recv.at[k], device_id=(x, y, 1 - c), device_id_type=MESH))

    def start(self):
        for cp in self.copies:
            cp.start()

    def finish(self):
        for cp in self.copies:
            cp.wait()

    @staticmethod
    def scratch(g):
        n = N_CHIPS * len(_row_chunks(g.shape[1] // 2, D2D_PIECES))
        return [pltpu.SemaphoreType.DMA((n,))] * 2

    @staticmethod
    def out_shape(g):
        return jax.ShapeDtypeStruct((N_CHIPS, g.shape[1] // 2, D_MODEL), g.dtype)


def _swap_halves(g):
    def body(g_ref, theirs_ref, send, recv):
        swap = _Swap(g_ref, theirs_ref, send, recv)
        swap.start()
        swap.finish()

    return pl.pallas_call(
        body, name="swap_halves", in_specs=[ANY], out_specs=ANY, out_shape=_Swap.out_shape(g),
        scratch_shapes=_Swap.scratch(g),
    )(g)


ADD_TILE_MAX_ROWS = 600


def _add_tile(half):
    return max(t for t in range(8, ADD_TILE_MAX_ROWS + 1, 8) if half % t == 0)


def _add_cores(g, theirs, name, out_dtype=f32):
    half = theirs.shape[1]
    tr = _add_tile(half)
    n_t = half // tr

    def body(c_ref, g_ref, t_ref, o_ref):
        o_ref[...] = (g_ref[...] + t_ref[...]).astype(out_dtype)

    blk = pl.BlockSpec((1, tr, D_MODEL), lambda s, t, c_ref: (s, t, 0))
    return pl.pallas_call(
        body, name=name,
        grid_spec=pltpu.PrefetchScalarGridSpec(
            num_scalar_prefetch=1, grid=(N_CHIPS, n_t),
            in_specs=[pl.BlockSpec((1, tr, D_MODEL), lambda s, t, c_ref: (s, c_ref[0] * n_t + t, 0)), blk],
            out_specs=blk),
        out_shape=jax.ShapeDtypeStruct(theirs.shape, out_dtype),
        compiler_params=_params("parallel", "parallel"),
    )(lax.axis_index("c").astype(jnp.int32).reshape(1), g, theirs)


def _add_chips(chip_sum, others, name):
    half = chip_sum.shape[1]
    tr = _add_tile(half)

    def body(me_ref, own_ref, o0, o1, o2, out_ref):
        out_ref[...] = ((own_ref[0].astype(f32) + o0[0].astype(f32)) + o1[0].astype(f32)) + o2[0].astype(f32)

    other = lambda j: pl.BlockSpec((1, tr, D_MODEL), lambda t, me_ref: (j, t, 0))
    return pl.pallas_call(
        body, name=name,
        grid_spec=pltpu.PrefetchScalarGridSpec(
            num_scalar_prefetch=1, grid=(half // tr,),
            in_specs=[pl.BlockSpec((1, tr, D_MODEL), lambda t, me_ref: (me_ref[0], t, 0)), other(0), other(1), other(2)],
            out_specs=pl.BlockSpec((tr, D_MODEL), lambda t, me_ref: (t, 0))),
        out_shape=jax.ShapeDtypeStruct((half, D_MODEL), f32),
        compiler_params=_params("parallel"),
    )((2 * lax.axis_index("x") + lax.axis_index("y")).astype(jnp.int32).reshape(1), chip_sum, others, others, others)


def _join_halves(parts):
    n_parts = len(parts)
    pieces = [_row_chunks(r.shape[0], D2D_PIECES) for r in parts]
    first = [sum(len(p) for p in pieces[:i]) for i in range(n_parts)]
    n = sum(len(p) for p in pieces)

    def body(*refs):
        r_refs, out_refs = refs[:n_parts], refs[n_parts:2 * n_parts]
        send, recv = refs[2 * n_parts:2 * n_parts + 2]
        local = refs[2 * n_parts + 2:]
        x, y, c, _ = _place()
        owns = [_LocalCopy(lambda rr, r_ref=r_ref: r_ref.at[rr], lambda rr, out_ref=out_ref: out_ref.at[c, rr],
                           r_ref.shape[0], *local[3 * i:3 * i + 3])
                for i, (r_ref, out_ref) in enumerate(zip(r_refs, out_refs))]
        for own in owns:
            own.start()

        def piece(i, j, core):
            start, size = pieces[i][j]
            return pltpu.make_async_remote_copy(
                src_ref=r_refs[i].at[pl.ds(start, size)], dst_ref=out_refs[i].at[core, pl.ds(start, size)],
                send_sem=send.at[first[i] + j], recv_sem=recv.at[first[i] + j],
                device_id=(x, y, 1 - c), device_id_type=MESH)

        every = [(i, j) for i in range(n_parts) for j in range(len(pieces[i]))]
        copies = [piece(i, j, c) for i, j in every]
        for cp in copies:
            cp.start()
        for own in owns:
            own.pass_on()
        for i, j in every:
            piece(i, j, 1 - c).wait_recv()
        for cp in copies:
            cp.wait_send()
        for own in owns:
            own.finish()

    local_scratch = []
    for r in parts:
        local_scratch += _LocalCopy.scratch(r.shape[0], r.dtype)
    return pl.pallas_call(
        body, name="join_halves", in_specs=[ANY] * n_parts, out_specs=[ANY] * n_parts,
        out_shape=[jax.ShapeDtypeStruct((2,) + r.shape, r.dtype) for r in parts],
        scratch_shapes=[pltpu.SemaphoreType.DMA((n,))] * 2 + local_scratch,
        compiler_params=pltpu.CompilerParams(vmem_limit_bytes=VMEM_LIMIT_V7X),
    )(*parts)


class _SmallSum:
    def __init__(self, b_ref, gathered, send, recv, local_sem):
        x, y, c, _ = _place()
        me = 4 * x + 2 * y + c
        self.gathered = gathered
        self.own = pltpu.make_async_copy(b_ref, gathered.at[me], local_sem)
        self.sends, self.arrivals = [], []
        for kk in range(1, N_DEV):
            flip = lambda v, bit: 1 - v if bit else v
            peer = (flip(x, kk & 4), flip(y, kk & 2), flip(c, kk & 1))
            self.sends.append(pltpu.make_async_remote_copy(
                src_ref=b_ref, dst_ref=gathered.at[me], send_sem=send.at[kk - 1], recv_sem=recv.at[kk - 1],
                device_id=peer, device_id_type=MESH))
            self.arrivals.append(pltpu.make_async_remote_copy(
                src_ref=b_ref, dst_ref=gathered.at[jnp.bitwise_xor(me, kk)], send_sem=send.at[kk - 1],
                recv_sem=recv.at[kk - 1], device_id=peer, device_id_type=MESH))

    def start(self):
        self.own.start()
        for cp in self.sends:
            cp.start()

    def finish(self, out_ref):
        self.own.wait()
        for cp in self.arrivals:
            cp.wait_recv()
        for cp in self.sends:
            cp.wait_send()
        acc = self.gathered[0]
        for dev in range(1, N_DEV):
            acc = acc + self.gathered[dev]
        out_ref[...] = acc

    @staticmethod
    def scratch(block):
        return [pltpu.VMEM((N_DEV,) + block.shape, block.dtype), pltpu.SemaphoreType.DMA((N_DEV - 1,)),
                pltpu.SemaphoreType.DMA((N_DEV - 1,)), pltpu.SemaphoreType.DMA]


def _adamw(w, g, m, v, name):
    rows, cols = w.shape
    tr = max(t for t in range(8, 513, 8) if rows % t == 0)
    c1 = 1.0 - ADAM_B1 ** ADAM_STEP
    c2 = 1.0 - ADAM_B2 ** ADAM_STEP

    def body(w_ref, g_ref, m_ref, v_ref, d_ref, nm_ref, nv_ref):
        gv = g_ref[...]
        nm = ADAM_B1 * m_ref[...] + (1.0 - ADAM_B1) * gv
        nv = ADAM_B2 * v_ref[...] + (1.0 - ADAM_B2) * (gv * gv)
        nm_ref[...] = nm
        nv_ref[...] = nv
        d_ref[...] = -ADAM_LR * ((nm / c1) / (jnp.sqrt(nv / c2) + ADAM_EPS) + ADAM_WD * w_ref[...])

    blk = pl.BlockSpec((tr, cols), lambda i: (i, 0))
    shape = jax.ShapeDtypeStruct((rows, cols), f32)
    return pl.pallas_call(
        body, name=name, grid=(rows // tr,), in_specs=[blk] * 4, out_specs=[blk] * 3, out_shape=[shape] * 3,
        compiler_params=_params("parallel"),
    )(w, g, m, v)


LARGE = ("w_in", "w_out", "w_gate", "w_up", "w_down")
SMALL = ("ln_pre_mix", "ln_post_mix", "ln_pre_ffn", "ln_post_ffn", "pool_scale", "w_pool")
SHARD_ROWS = {"w_in": 640, "w_out": 256, "w_gate": 704, "w_up": 704, "w_down": 704}
COLUMN_SHARDED = ("w_in", "w_gate", "w_up")
UPDATED_TRANSPOSED = ("w_gate", "w_up")
NEEDED_FIRST = ("w_in",)
NEEDED_LATER = ("w_out", "w_gate", "w_up", "w_down")
READY_EARLY = ("w_out", "w_gate", "w_up", "w_down")
READY_LATE = ("w_in",)


def _pack_shard(shards, names):
    return jnp.concatenate([shards[n].T if n in COLUMN_SHARDED else shards[n] for n in names], axis=0)


def _unpack_shard(pack, names):
    out, row = {}, 0
    for n in names:
        out[n] = pack[row:row + SHARD_ROWS[n]]
        row += SHARD_ROWS[n]
    return out


def _whole_from_shards(packs, names):
    out, row = {}, 0
    for n in names:
        rows = SHARD_ROWS[n]
        out[n] = packs[:, row:row + rows].reshape(N_CHIPS * rows, D_MODEL)
        row += rows
    return out


def _shards_from_whole(grads, names):
    return jnp.concatenate([grads[n].reshape(N_CHIPS, SHARD_ROWS[n], D_MODEL) for n in names], axis=1)


def _pack_small(vals):
    rows = [vals[n].reshape(1, D_MODEL) for n in SMALL[:4]]
    rows.append(jnp.pad(vals["pool_scale"].reshape(1, POOL_WIDTH), ((0, 0), (0, D_MODEL - POOL_WIDTH))))
    rows.append(jnp.pad(vals["loss"].reshape(1, 1), ((0, 0), (0, D_MODEL - 1))))
    rows.append(jnp.zeros((2, D_MODEL), f32))
    rows.append(vals["w_pool"].reshape(16, D_MODEL))
    return jnp.concatenate(rows, axis=0)


def _unpack_small(block):
    out = {n: block[i:i + 1] for i, n in enumerate(SMALL[:4])}
    out["pool_scale"] = block[4:5, :POOL_WIDTH]
    out["loss"] = block[5, 0]
    out["w_pool"] = block[8:24].reshape(1, 4, POOL_GROUP, POOL_GROUP)
    return out


def kernel(x, ln_pre_mix, w_in, w_pool, pool_scale, w_out, ln_post_mix, ln_pre_ffn, w_gate, w_up, w_down, ln_post_ffn, loss_target, m_ln_pre_mix, m_w_in, m_w_pool, m_pool_scale, m_w_out, m_ln_post_mix, m_ln_pre_ffn, m_w_gate, m_w_up, m_w_down, m_ln_post_ffn, v_ln_pre_mix, v_w_in, v_w_pool, v_pool_scale, v_w_out, v_ln_post_mix, v_ln_pre_ffn, v_w_gate, v_w_up, v_w_down, v_ln_post_ffn):
    w = dict(ln_pre_mix=ln_pre_mix, w_in=w_in, w_pool=w_pool, pool_scale=pool_scale, w_out=w_out,
             ln_post_mix=ln_post_mix, ln_pre_ffn=ln_pre_ffn, w_gate=w_gate, w_up=w_up, w_down=w_down,
             ln_post_ffn=ln_post_ffn)
    m = dict(ln_pre_mix=m_ln_pre_mix, w_in=m_w_in, w_pool=m_w_pool, pool_scale=m_pool_scale, w_out=m_w_out,
             ln_post_mix=m_ln_post_mix, ln_pre_ffn=m_ln_pre_ffn, w_gate=m_w_gate, w_up=m_w_up, w_down=m_w_down,
             ln_post_ffn=m_ln_post_ffn)
    v = dict(ln_pre_mix=v_ln_pre_mix, w_in=v_w_in, w_pool=v_w_pool, pool_scale=v_pool_scale, w_out=v_w_out,
             ln_post_mix=v_ln_post_mix, ln_pre_ffn=v_ln_pre_ffn, w_gate=v_w_gate, w_up=v_w_up, w_down=v_w_down,
             ln_post_ffn=v_ln_post_ffn)

    xs, target = x[0], loss_target[0]
    cos_t, sin_t = _rope_tables(xs.shape[0])
    w_bd = _block_diag(w_pool[0]).astype(bf16)
    shard = {n: w[n][0].astype(bf16) for n in LARGE}

    w_in_whole = _whole_from_shards(_gather_weights(_pack_shard(shard, NEEDED_FIRST)), NEEDED_FIRST)["w_in"]
    h1, u, qs, ks, vs = _in_proj(xs, ln_pre_mix, w_in_whole, cos_t, sin_t)
    pool_out = _pool_fwd(u, w_bd, pool_scale)
    attn_out, lse, later = _attn_fwd(qs, ks, vs, _pack_shard(shard, NEEDED_LATER))
    whole = _whole_from_shards(later, NEEDED_LATER)
    mix, x2, h2 = _out_proj(pool_out, attn_out, whole["w_out"], xs, ln_post_mix, ln_pre_ffn)
    gate, up, f = _ffn_fwd(h2, whole["w_gate"], whole["w_up"], whole["w_down"])
    dy, df, dg4, loss = _loss_head(f, x2, target, ln_post_ffn)

    large = {}
    a, dgate, dup, dh2 = _ffn_bwd(df, gate, up, whole["w_gate"], whole["w_up"], whole["w_down"])
    large["w_down"] = _matmul_tiles_tn(a, df, "grad_w_down")
    large["w_gate"] = _matmul_tiles_tn(dgate, h2, "grad_w_gate")
    large["w_up"] = _matmul_tiles_tn(dup, h2, "grad_w_up")
    dx2, dmix, dg3, dg2 = _norm_bwd(dh2, dy, x2, mix, ln_pre_ffn, ln_post_mix)
    large["w_out"] = jnp.concatenate([_matmul_tn(pool_out, dmix, "grad_w_out_pool"),
                                      _matmul_tn(attn_out, dmix, "grad_w_out_attn")], axis=0)
    early = _shards_from_whole(large, READY_EARLY)
    dpool, delta, dos, early_theirs = _out_proj_bwd(dmix, whole["w_out"], attn_out, _head_ones(), early)
    early_chip = _add_cores(early, early_theirs, "add_cores_early")
    du, d_w_bd, d_scale = _pool_bwd(u, dpool, w_bd, pool_scale)
    dq, dk, dv, early_others = _attn_bwd(qs, ks, vs, dos, lse, delta, early_chip)
    grad_x, dproj, dg1 = _in_proj_bwd(du, dq, dk, dv, cos_t, sin_t, w_in_whole, xs, dx2, ln_pre_mix)
    d_w_pool = jnp.stack([d_w_bd[g * POOL_GROUP:(g + 1) * POOL_GROUP, g * POOL_GROUP:(g + 1) * POOL_GROUP]
                          for g in range(POOL_WIDTH // POOL_GROUP)])
    small = dict(ln_pre_mix=dg1, ln_post_mix=dg2, ln_pre_ffn=dg3, ln_post_ffn=dg4, pool_scale=d_scale, w_pool=d_w_pool)
    large["w_in"], small_total = _matmul_tn_and_small_sum(dproj, h1, _pack_small(dict(small, loss=loss)), "grad_w_in")
    late = _shards_from_whole(large, READY_LATE)
    late_chip = _add_cores(late, _swap_halves(late), "add_cores_late", bf16)
    late_others = _scatter_to_chips(late_chip)
    early_half = _add_chips(early_chip, early_others, "add_chips_early")
    late_half = _add_chips(late_chip, late_others, "add_chips_late")
    early_whole, late_whole = _join_halves([early_half, late_half])
    grads = _unpack_shard(early_whole.reshape(-1, D_MODEL), READY_EARLY)
    grads.update(_unpack_shard(late_whole.reshape(-1, D_MODEL), READY_LATE))

    total = _unpack_small(small_total)
    for n in SMALL:
        grads[n] = total[n]

    delta_w, new_m, new_v = {}, {}, {}
    for n in LARGE:
        if n in UPDATED_TRANSPOSED:
            update = _adamw(w[n][0].T, grads[n], m[n][0].T, v[n][0].T, "adamw_" + n)
            delta_w[n], new_m[n], new_v[n], grads[n] = [a.T for a in (*update, grads[n])]
        else:
            if n in COLUMN_SHARDED:
                grads[n] = grads[n].T
            delta_w[n], new_m[n], new_v[n] = _adamw(w[n][0], grads[n], m[n][0], v[n][0], "adamw_" + n)
    small_state = [_pack_small(dict({n: s[n] for n in SMALL}, loss=jnp.zeros((), f32))) for s in (w, m, v)]
    small_grad = _pack_small(dict({n: grads[n] for n in SMALL}, loss=jnp.zeros((), f32)))
    sd, sm, sv = _adamw(small_state[0], small_grad, small_state[1], small_state[2], "adamw_small")
    for out, block in ((delta_w, sd), (new_m, sm), (new_v, sv)):
        un = _unpack_small(block)
        for n in SMALL:
            out[n] = un[n]

    names = ("ln_pre_mix", "w_in", "w_pool", "pool_scale", "w_out", "ln_post_mix", "ln_pre_ffn", "w_gate", "w_up",
             "w_down", "ln_post_ffn")
    full = lambda d: [d[n].reshape(w[n].shape) for n in names]
    return (total["loss"], grad_x[None], *full(grads), *full(delta_w), *full(new_m), *full(new_v))
```

```python
import numpy as np
import jax
import jax.numpy as jnp
from jax import lax
from jax.experimental import pallas as pl
from jax.experimental.pallas import tpu as pltpu

D_MODEL = 1024
POOL_WIDTH = 256
POOL_GROUP = 64
ATTN_WIDTH = 768
HEAD_DIM = 64
IN_WIDTH = 2560
D_FF = 2816
BLOCK = 128
DILATIONS = (1, 4, 16)
ROPE_THETA = 10000.0
EPS = 1e-6
ATTN_SCALE = 0.125
NEG = -1e30

ADAM_LR = 0.001
ADAM_B1 = 0.9
ADAM_B2 = 0.999
ADAM_EPS = 1e-08
ADAM_WD = 0.01
ADAM_STEP = 10

N_CHIPS = 4
N_DEV = 8
VMEM_LIMIT_V7X = 56 * 1024 * 1024
MESH = pl.DeviceIdType.MESH

f32 = jnp.float32
bf16 = jnp.bfloat16


def _params(*sem):
    return pltpu.CompilerParams(dimension_semantics=sem, vmem_limit_bytes=VMEM_LIMIT_V7X)


def _dot(a, b):
    return jnp.dot(a, b, preferred_element_type=f32)


def _dot_nt(a, b):
    return lax.dot_general(a, b, (((1,), (1,)), ((), ())), preferred_element_type=f32)


def _dot_tn(a, b):
    return lax.dot_general(a, b, (((0,), (0,)), ((), ())), preferred_element_type=f32)


def _rope_partner(a, first_half):
    return jnp.where(first_half, pltpu.roll(a, 96, 1), pltpu.roll(a, 32, 1))


def _first_half_mask(rows):
    lane = lax.broadcasted_iota(jnp.int32, (rows, 128), 1)
    return (lane % HEAD_DIM) < (HEAD_DIM // 2)


def _stream_spec(d, ts):
    return pl.BlockSpec((d, ts // d, ATTN_WIDTH), lambda i: (0, i, 0))


def _stream_shape(S, d):
    return jax.ShapeDtypeStruct((d, S // d, ATTN_WIDTH), bf16)


N_STAGE = ATTN_WIDTH // 128


def _stage_scratch(ts):
    return [pltpu.VMEM((ts, 128), f32)] * N_STAGE


def _store_streams(stage, out_refs, ts):
    for d, ref in zip(DILATIONS, out_refs):
        for r in range(d):
            rows = pl.ds(0, ts) if d == 1 else pl.ds(r, ts // d, stride=d)
            for j in range(N_STAGE):
                ref[r, :, j * 128:(j + 1) * 128] = stage[j][rows, :].astype(bf16)


def _in_proj(x, g1, w_in, cos_t, sin_t):
    S = x.shape[0]
    ts = 512

    def body(x_ref, g_ref, w_ref, cos_ref, sin_ref, h_ref, u_ref, *rest):
        outs, stage = rest[:-N_STAGE], rest[-N_STAGE:]
        xv = x_ref[...]
        r = lax.rsqrt(jnp.mean(xv * xv, axis=-1, keepdims=True) + EPS)
        h = ((xv * r) * g_ref[...]).astype(bf16)
        h_ref[...] = h
        proj = _dot_nt(h, w_ref[...])
        u_ref[...] = proj[:, :POOL_WIDTH]
        cos = cos_ref[...]
        sin = sin_ref[...]
        first = _first_half_mask(ts)
        n_dil = len(DILATIONS)
        for which, base in enumerate((POOL_WIDTH, POOL_WIDTH + ATTN_WIDTH)):
            for j in range(ATTN_WIDTH // 128):
                a = proj[:, base + j * 128: base + (j + 1) * 128]
                if which == 0:
                    a = a * ATTN_SCALE
                stage[j][...] = a * cos + _rope_partner(a, first) * sin
            _store_streams(stage, outs[which * n_dil:(which + 1) * n_dil], ts)
        for j in range(ATTN_WIDTH // 128):
            base = POOL_WIDTH + 2 * ATTN_WIDTH + j * 128
            stage[j][...] = proj[:, base:base + 128]
        _store_streams(stage, outs[2 * n_dil:], ts)

    row = lambda w: pl.BlockSpec((ts, w), lambda i: (i, 0))
    streams = [_stream_spec(d, ts) for d in DILATIONS] * 3
    res = pl.pallas_call(
        body, name="in_proj", grid=(S // ts,),
        in_specs=[row(D_MODEL), pl.BlockSpec((1, D_MODEL), lambda i: (0, 0)),
                  pl.BlockSpec((IN_WIDTH, D_MODEL), lambda i: (0, 0)), row(128), row(128)],
        out_specs=[row(D_MODEL), row(POOL_WIDTH)] + streams,
        out_shape=[jax.ShapeDtypeStruct((S, D_MODEL), bf16), jax.ShapeDtypeStruct((S, POOL_WIDTH), f32)]
        + [_stream_shape(S, d) for d in DILATIONS] * 3,
        scratch_shapes=_stage_scratch(ts),
        compiler_params=_params("parallel"),
    )(x, g1, w_in, cos_t, sin_t)
    n = len(DILATIONS)
    return res[0], res[1], res[2:2 + n], res[2 + n:2 + 2 * n], res[2 + 2 * n:]


POOL_HALO = 16


def _pool_lane_group(rows):
    return lax.broadcasted_iota(jnp.int32, (rows, POOL_WIDTH), 1) // POOL_GROUP


def _pool_select(group, s2, s4, s8, s16):
    return jnp.where(group == 0, s2, jnp.where(group == 1, s4, jnp.where(group == 2, s8, s16)))


def _pool_count(t0, rows):
    group = _pool_lane_group(rows)
    t = t0 + lax.broadcasted_iota(jnp.int32, (rows, POOL_WIDTH), 0)
    win = _pool_select(group, 2, 4, 8, 16)
    return jnp.minimum(t + 1, win).astype(f32)


def _pool_diff(u_halo, u_tile, t0):
    ts = u_tile.shape[0]
    ext = jnp.concatenate([u_halo, u_tile], axis=0)
    s2 = ext + pltpu.roll(ext, 1, 0)
    s4 = s2 + pltpu.roll(s2, 2, 0)
    s8 = s4 + pltpu.roll(s4, 4, 0)
    s16 = s8 + pltpu.roll(s8, 8, 0)
    group = _pool_lane_group(ts + POOL_HALO)
    wsum = _pool_select(group, s2, s4, s8, s16)[POOL_HALO:]
    return wsum / _pool_count(t0, ts) - u_tile


def _pool_specs(ts, n_tiles):
    tile = pl.BlockSpec((ts, POOL_WIDTH), lambda i: (i, 0))
    per = ts // POOL_HALO
    before = pl.BlockSpec((POOL_HALO, POOL_WIDTH), lambda i: (jnp.maximum(i * per - 1, 0), 0))
    after = pl.BlockSpec((POOL_HALO, POOL_WIDTH), lambda i: (jnp.minimum((i + 1) * per, n_tiles * per - 1), 0))
    return tile, before, after


def _pool_fwd(u, w_bd, scale):
    S = u.shape[0]
    ts = 512
    n_tiles = S // ts

    def body(u_ref, halo_ref, w_ref, sc_ref, y_ref):
        i = pl.program_id(0)
        halo = jnp.where(i > 0, halo_ref[...], 0.0)
        d = _pool_diff(halo, u_ref[...], i * ts)
        y_ref[...] = (_dot(d.astype(bf16), w_ref[...]) * sc_ref[...]).astype(bf16)

    tile, before, _ = _pool_specs(ts, n_tiles)
    return pl.pallas_call(
        body, name="pool_fwd", grid=(n_tiles,),
        in_specs=[tile, before, pl.BlockSpec((POOL_WIDTH, POOL_WIDTH), lambda i: (0, 0)),
                  pl.BlockSpec((1, POOL_WIDTH), lambda i: (0, 0))],
        out_specs=tile, out_shape=jax.ShapeDtypeStruct((S, POOL_WIDTH), bf16),
        compiler_params=_params("parallel"),
    )(u, u, w_bd, scale)


def _pool_bwd(u, dy, w_bd, scale):
    S = u.shape[0]
    ts = 512
    n_tiles = S // ts

    def body(u_ref, halo_ref, dy_ref, dy_next_ref, w_ref, sc_ref, du_ref, dw_ref, dsc_ref):
        i = pl.program_id(0)

        @pl.when(i == 0)
        def _():
            dw_ref[...] = jnp.zeros_like(dw_ref)
            dsc_ref[...] = jnp.zeros_like(dsc_ref)

        halo = jnp.where(i > 0, halo_ref[...], 0.0)
        d = _pool_diff(halo, u_ref[...], i * ts).astype(bf16)
        w = w_ref[...]
        sc = sc_ref[...]
        dy_tile = dy_ref[...]
        z = _dot(d, w)
        dsc_ref[...] += jnp.sum(dy_tile * z, axis=0, keepdims=True)
        dy_next = jnp.where(i < n_tiles - 1, dy_next_ref[...], 0.0)
        dz = (jnp.concatenate([dy_tile, dy_next], axis=0) * sc).astype(bf16)
        dw_ref[...] += _dot_tn(d, dz[:ts])
        dd = _dot_nt(dz, w)
        e = dd / _pool_count(i * ts, ts + POOL_HALO)
        n = ts + POOL_HALO
        f2 = e + pltpu.roll(e, n - 1, 0)
        f4 = f2 + pltpu.roll(f2, n - 2, 0)
        f8 = f4 + pltpu.roll(f4, n - 4, 0)
        f16 = f8 + pltpu.roll(f8, n - 8, 0)
        fsum = _pool_select(_pool_lane_group(n), f2, f4, f8, f16)
        du_ref[...] = (fsum[:ts] - dd[:ts]).astype(bf16)

    tile, before, after = _pool_specs(ts, n_tiles)
    return pl.pallas_call(
        body, name="pool_bwd", grid=(n_tiles,),
        in_specs=[tile, before, tile, after, pl.BlockSpec((POOL_WIDTH, POOL_WIDTH), lambda i: (0, 0)),
                  pl.BlockSpec((1, POOL_WIDTH), lambda i: (0, 0))],
        out_specs=[tile, pl.BlockSpec((POOL_WIDTH, POOL_WIDTH), lambda i: (0, 0)),
                   pl.BlockSpec((1, POOL_WIDTH), lambda i: (0, 0))],
        out_shape=[jax.ShapeDtypeStruct((S, POOL_WIDTH), bf16), jax.ShapeDtypeStruct((POOL_WIDTH, POOL_WIDTH), f32),
                   jax.ShapeDtypeStruct((1, POOL_WIDTH), f32)],
        compiler_params=_params("arbitrary"),
    )(u, u, dy, dy, w_bd, scale)


SUPER = BLOCK * DILATIONS[-1]
UNITS = SUPER // BLOCK
FWD_UNROLL = 16
BWD_UNROLL = 8


def _band_mask(has_prev):
    qi = lax.broadcasted_iota(jnp.int32, (BLOCK, 2 * BLOCK), 0)
    kj = lax.broadcasted_iota(jnp.int32, (BLOCK, 2 * BLOCK), 1)
    return (kj >= qi) & (kj <= qi + BLOCK) & ((kj >= BLOCK) | has_prev)


def _head0_mask(rows=BLOCK):
    return lax.broadcasted_iota(jnp.int32, (rows, 128), 1) < HEAD_DIM


def _band_mask_t(has_prev):
    ki = lax.broadcasted_iota(jnp.int32, (2 * BLOCK, 2 * BLOCK), 0)
    qj = lax.broadcasted_iota(jnp.int32, (2 * BLOCK, 2 * BLOCK), 1) % BLOCK
    return (ki >= qj) & (ki <= qj + BLOCK) & ((ki >= BLOCK) | has_prev)


def _head_pair_rows(a, h0):
    zero = jnp.zeros_like(a)
    return jnp.concatenate([jnp.where(h0, a, zero), jnp.where(h0, zero, a)], axis=0)


def _per_query_row(stat):
    t = stat.T
    return jnp.concatenate([t[0:1], t[HEAD_DIM:HEAD_DIM + 1]], axis=1)


def _natural_rows(d, r, n):
    if d == 1:
        return pl.ds(pl.multiple_of(n * BLOCK, BLOCK), BLOCK)
    return pl.ds(n * (BLOCK * d) + r, BLOCK, stride=d)


def _unit_place(d, u):
    per_stream = UNITS // d
    return u // per_stream, u % per_stream, per_stream


def _block_rows(n):
    return pl.ds(pl.multiple_of(n * BLOCK, BLOCK), BLOCK)


def _band(cur_ref, tail_ref, r, n):
    before = jnp.where(n > 0, cur_ref[r, _block_rows(jnp.maximum(n - 1, 0)), :], tail_ref[r])
    return jnp.concatenate([before, cur_ref[r, _block_rows(n), :]], axis=0)


def _attn_in_specs(S, with_do):
    specs = []
    last = S // SUPER - 1
    for d in DILATIONS:
        per_stream = UNITS // d
        cur = pl.BlockSpec((d, SUPER // d, 128), lambda hp, sb: (0, jnp.minimum(sb, last), hp))
        tail = pl.BlockSpec(
            (d, BLOCK, 128),
            lambda hp, sb, per_stream=per_stream: (0, jnp.maximum(jnp.minimum(sb, last) * per_stream - 1, 0), hp))
        specs += [cur] * (2 if with_do else 1) + [cur, tail, cur, tail]
    return specs


def _attn_fwd(qs, ks, vs, pack):
    S = qs[0].shape[1]
    n_dil = len(DILATIONS)
    n_steps = S // SUPER
    n_total = (ATTN_WIDTH // 128) * n_steps

    def body(*refs):
        ins, pack_ref = refs[:5 * n_dil], refs[5 * n_dil]
        out_ref, lse_ref, gathered_ref = refs[5 * n_dil + 1:5 * n_dil + 4]
        scratch = refs[5 * n_dil + 4:]
        o_sc, l_sc = scratch[:n_dil], scratch[n_dil:2 * n_dil]
        gather = _Gather(pack_ref, gathered_ref, *scratch[2 * n_dil:])
        sb = pl.program_id(1)
        step = pl.program_id(0) * n_steps + sb

        @pl.when(step == 0)
        def _():
            gather.start()

        h0 = _head0_mask()
        for ci, d in enumerate(DILATIONS):
            q_ref, kc_ref, kp_ref, vc_ref, vp_ref = ins[5 * ci:5 * ci + 5]

            def unit(u, carry, d=d, ci=ci, q_ref=q_ref, kc_ref=kc_ref, kp_ref=kp_ref, vc_ref=vc_ref, vp_ref=vp_ref):
                r, n, _ = _unit_place(d, u)
                qv = q_ref[r, _block_rows(n), :]
                kb = _band(kc_ref, kp_ref, r, n)
                vb = _band(vc_ref, vp_ref, r, n)
                valid = _band_mask((sb > 0) | (n > 0))
                s = jnp.where(jnp.concatenate([valid, valid], axis=0), _dot_nt(_head_pair_rows(qv, h0), kb), NEG)
                m = jnp.max(s, axis=1, keepdims=True)
                e = jnp.exp(s - m)
                den = jnp.sum(e, axis=1, keepdims=True)
                o_pair = _dot(e.astype(bf16), vb) * (1.0 / den)
                lse_pair = jnp.broadcast_to(m + jnp.log(den), (2 * BLOCK, 128))
                rows = _natural_rows(d, r, n)
                o_sc[ci][rows, :] = jnp.where(h0, o_pair[:BLOCK], o_pair[BLOCK:])
                l_sc[ci][rows, :] = jnp.where(h0, lse_pair[:BLOCK], lse_pair[BLOCK:])
                return carry

            lax.fori_loop(0, UNITS, unit, 0, unroll=FWD_UNROLL)

        def merge(t, carry):
            rows = pl.ds(pl.multiple_of(t * 256, 256), 256)
            a, b, c = l_sc[0][rows, :], l_sc[1][rows, :], l_sc[2][rows, :]
            m = jnp.maximum(jnp.maximum(a, b), c)
            ea, eb, ec = jnp.exp(a - m), jnp.exp(b - m), jnp.exp(c - m)
            tot = ea + eb + ec
            out_ref[rows, :] = ((ea / tot) * o_sc[0][rows, :] + (eb / tot) * o_sc[1][rows, :]
                                + (ec / tot) * o_sc[2][rows, :]).astype(bf16)
            lse_ref[rows, :] = m + jnp.log(tot)
            return carry

        lax.fori_loop(0, SUPER // 256, merge, 0)

        @pl.when(step == (2 * n_total) // 3)
        def _():
            gather.pass_on()

        @pl.when(step == n_total - 1)
        def _():
            gather.finish()

    args = []
    for q, k, v in zip(qs, ks, vs):
        args += [q, k, k, v, v]
    nat = pl.BlockSpec((SUPER, 128), lambda hp, sb: (sb, hp))
    rows = pack.shape[0]
    return pl.pallas_call(
        body, name="attn_fwd", grid=(ATTN_WIDTH // 128, n_steps),
        in_specs=_attn_in_specs(S, False) + [ANY], out_specs=[nat, nat, ANY],
        out_shape=[jax.ShapeDtypeStruct((S, ATTN_WIDTH), bf16), jax.ShapeDtypeStruct((S, ATTN_WIDTH), f32),
                   _Gather.out_shape(rows, pack.dtype)],
        scratch_shapes=[pltpu.VMEM((SUPER, 128), f32)] * (2 * n_dil) + _Gather.scratch(rows, pack.dtype),
        compiler_params=_params("arbitrary", "arbitrary"),
    )(*args, pack)


def _attn_bwd(qs, ks, vs, dos, lse, delta, chip_sum):
    S = qs[0].shape[1]
    n_steps = S // SUPER
    last = n_steps - 1
    n_dil = len(DILATIONS)
    n_total = (ATTN_WIDTH // 128) * (n_steps + 1)

    def body(*refs):
        ins, (lse_ref, dl_ref, sum_ref) = refs[:6 * n_dil], refs[6 * n_dil:6 * n_dil + 3]
        dq_ref, dk_ref, dv_ref, others_ref = refs[6 * n_dil + 3:6 * n_dil + 7]
        dq_acc, dk_acc, dv_acc = refs[6 * n_dil + 7:6 * n_dil + 10]
        scatter = _Scatter(sum_ref, others_ref, *refs[6 * n_dil + 10:])
        sb = pl.program_id(1)
        step = pl.program_id(0) * (n_steps + 1) + sb
        cur = sb % 2
        prv = 1 - cur

        @pl.when(step == 0)
        def _():
            scatter.start()

        @pl.when(sb < n_steps)
        def _():
            dq_acc[...] = jnp.zeros_like(dq_acc)
            dk_acc[cur] = jnp.zeros((SUPER, 128), f32)
            dv_acc[cur] = jnp.zeros((SUPER, 128), f32)
            h0 = _head0_mask()
            for ci, d in enumerate(DILATIONS):
                q_ref, do_ref, kc_ref, kp_ref, vc_ref, vp_ref = ins[6 * ci:6 * ci + 6]

                def unit(u, carry, d=d, q_ref=q_ref, do_ref=do_ref, kc_ref=kc_ref, kp_ref=kp_ref, vc_ref=vc_ref,
                         vp_ref=vp_ref):
                    r, n, per_stream = _unit_place(d, u)
                    qv = q_ref[r, _block_rows(n), :]
                    dov = do_ref[r, _block_rows(n), :]
                    kb = _band(kc_ref, kp_ref, r, n)
                    vb = _band(vc_ref, vp_ref, r, n)
                    rows = _natural_rows(d, r, n)
                    has_prev = (sb > 0) | (n > 0)
                    q_pair = _head_pair_rows(qv, h0)
                    do_pair = _head_pair_rows(dov, h0)
                    s_t = jnp.where(_band_mask_t(has_prev), _dot_nt(kb, q_pair), NEG)
                    p_t = jnp.exp(s_t - _per_query_row(lse_ref[rows, :]))
                    dp_t = _dot_nt(vb, do_pair)
                    ds_t = (p_t * (dp_t - _per_query_row(dl_ref[rows, :]))).astype(bf16)
                    dvb = _dot(p_t.astype(bf16), do_pair)
                    dkb = _dot(ds_t, q_pair)
                    dq_pair = _dot_tn(ds_t, kb)
                    dq_acc[rows, :] += jnp.where(h0, dq_pair[:BLOCK], dq_pair[BLOCK:])
                    dk_acc[cur, rows, :] += dkb[BLOCK:]
                    dv_acc[cur, rows, :] += dvb[BLOCK:]

                    slot = jnp.where((n > 0) | (sb == 0), cur, prv)
                    before = _natural_rows(d, r, jnp.where(n > 0, n - 1, per_stream - 1))
                    dk_acc[slot, before, :] += dkb[:BLOCK]
                    dv_acc[slot, before, :] += dvb[:BLOCK]
                    return carry

                lax.fori_loop(0, UNITS, unit, 0, unroll=BWD_UNROLL)
            dq_ref[...] = (dq_acc[...] * ATTN_SCALE).astype(bf16)

        @pl.when(sb > 0)
        def _():
            dk_ref[...] = dk_acc[prv].astype(bf16)
            dv_ref[...] = dv_acc[prv].astype(bf16)

        @pl.when(step == n_total - 1)
        def _():
            scatter.finish()

    args = []
    for q, k, v, do in zip(qs, ks, vs, dos):
        args += [q, do, k, k, v, v]
    nat = pl.BlockSpec((SUPER, 128), lambda hp, sb: (jnp.minimum(sb, last), hp))
    nat_before = pl.BlockSpec((SUPER, 128), lambda hp, sb: (jnp.clip(sb - 1, 0, last), hp))
    out = jax.ShapeDtypeStruct((S, ATTN_WIDTH), bf16)
    half = chip_sum.shape[1]
    return pl.pallas_call(
        body, name="attn_bwd", grid=(ATTN_WIDTH // 128, n_steps + 1),
        in_specs=_attn_in_specs(S, True) + [nat, nat, ANY], out_specs=[nat, nat_before, nat_before, ANY],
        out_shape=[out, out, out, _Scatter.out_shape(half, chip_sum.dtype)],
        scratch_shapes=[pltpu.VMEM((SUPER, 128), f32), pltpu.VMEM((2, SUPER, 128), f32),
                        pltpu.VMEM((2, SUPER, 128), f32)] + _Scatter.scratch(half),
        compiler_params=_params("arbitrary", "arbitrary"),
    )(*args, lse, delta, chip_sum)


def _rms(v):
    return lax.rsqrt(jnp.mean(v * v, axis=-1, keepdims=True) + EPS)


def _out_proj(pool_out, attn_out, w_out, x, g2, g3):
    S = x.shape[0]
    ts = 512

    def body(p_ref, a_ref, w_ref, x_ref, g2_ref, g3_ref, mix_ref, x2_ref, h2_ref):
        mix = _dot(p_ref[...], w_ref[:POOL_WIDTH, :]) + _dot(a_ref[...], w_ref[POOL_WIDTH:, :])
        mix_ref[...] = mix
        x2 = x_ref[...] + (mix * _rms(mix)) * g2_ref[...]
        x2_ref[...] = x2
        h2_ref[...] = ((x2 * _rms(x2)) * g3_ref[...]).astype(bf16)

    row = lambda w: pl.BlockSpec((ts, w), lambda i: (i, 0))
    gain = pl.BlockSpec((1, D_MODEL), lambda i: (0, 0))
    return pl.pallas_call(
        body, name="out_proj", grid=(S // ts,),
        in_specs=[row(POOL_WIDTH), row(ATTN_WIDTH), pl.BlockSpec((D_MODEL, D_MODEL), lambda i: (0, 0)),
                  row(D_MODEL), gain, gain],
        out_specs=[row(D_MODEL)] * 3,
        out_shape=[jax.ShapeDtypeStruct((S, D_MODEL), f32), jax.ShapeDtypeStruct((S, D_MODEL), f32),
                   jax.ShapeDtypeStruct((S, D_MODEL), bf16)],
        compiler_params=_params("parallel"),
    )(pool_out, attn_out, w_out, x, g2, g3)


FF_TILE = 256
FF_STEP_ROWS = 2048
FF_ROWS = 512
FF_BWD_ROWS = 256


def _sigmoid(g):
    return 1.0 / (1.0 + jnp.exp(-g))


def _ff_act_shape(S):
    return jax.ShapeDtypeStruct((D_FF // FF_TILE, S, FF_TILE), bf16)


def _ff_act_spec(ts):
    return pl.BlockSpec((1, ts, FF_TILE), lambda i, j: (j, i, 0))


def _ffn_fwd(h2, w_gate, w_up, w_down):
    S = h2.shape[0]
    ts = min(S, FF_STEP_ROWS)

    def body(h_ref, wg_ref, wu_ref, wd_ref, gate_ref, up_ref, f_ref):
        def rows_pass(first):
            def sub(i, carry):
                rows = pl.ds(pl.multiple_of(i * FF_ROWS, FF_ROWS), FF_ROWS)
                h = h_ref[rows, :]
                gate = _dot_nt(h, wg_ref[...])
                up = _dot_nt(h, wu_ref[...])
                gate_ref[0, rows, :] = gate.astype(bf16)
                up_ref[0, rows, :] = up.astype(bf16)
                part = _dot((gate * _sigmoid(gate) * up).astype(bf16), wd_ref[...])
                if first:
                    f_ref[rows, :] = part
                else:
                    f_ref[rows, :] += part
                return carry

            lax.fori_loop(0, ts // FF_ROWS, sub, 0, unroll=True)

        @pl.when(pl.program_id(1) == 0)
        def _():
            rows_pass(True)

        @pl.when(pl.program_id(1) > 0)
        def _():
            rows_pass(False)

    act = _ff_act_spec(ts)
    weight = pl.BlockSpec((FF_TILE, D_MODEL), lambda i, j: (j, 0))
    return pl.pallas_call(
        body, name="ffn_fwd", grid=(S // ts, D_FF // FF_TILE),
        in_specs=[pl.BlockSpec((ts, D_MODEL), lambda i, j: (i, 0)), weight, weight, weight],
        out_specs=[act, act, pl.BlockSpec((ts, D_MODEL), lambda i, j: (i, 0))],
        out_shape=[_ff_act_shape(S), _ff_act_shape(S), jax.ShapeDtypeStruct((S, D_MODEL), f32)],
        compiler_params=_params("parallel", "arbitrary"),
    )(h2, w_gate, w_up, w_down)


def _loss_head(f, x2, target, g4):
    S = f.shape[0]
    ts = 512

    def body(f_ref, x2_ref, t_ref, g_ref, dy_ref, df_ref, dg_ref, loss_ref):
        @pl.when(pl.program_id(0) == 0)
        def _():
            dg_ref[...] = jnp.zeros_like(dg_ref)
            loss_ref[...] = jnp.zeros_like(loss_ref)

        fv = f_ref[...]
        g = g_ref[...]
        r = _rms(fv)
        fhat = fv * r
        err = (x2_ref[...] + fhat * g) - t_ref[...]
        loss_ref[...] += 0.5 * jnp.sum(jnp.mean(err * err, axis=-1, keepdims=True), axis=0, keepdims=True)
        dy = err * (1.0 / D_MODEL)
        dy_ref[...] = dy
        dg_ref[...] += jnp.sum(dy * fhat, axis=0, keepdims=True)
        dyg = dy * g
        df_ref[...] = (r * (dyg - fhat * jnp.mean(dyg * fhat, axis=-1, keepdims=True))).astype(bf16)

    row = pl.BlockSpec((ts, D_MODEL), lambda i: (i, 0))
    gain = pl.BlockSpec((1, D_MODEL), lambda i: (0, 0))
    return pl.pallas_call(
        body, name="loss_head", grid=(S // ts,), in_specs=[row, row, row, gain],
        out_specs=[row, row, gain, pl.BlockSpec((1, 1), lambda i: (0, 0))],
        out_shape=[jax.ShapeDtypeStruct((S, D_MODEL), f32), jax.ShapeDtypeStruct((S, D_MODEL), bf16),
                   jax.ShapeDtypeStruct((1, D_MODEL), f32), jax.ShapeDtypeStruct((1, 1), f32)],
        compiler_params=_params("arbitrary"),
    )(f, x2, target, g4)


def _ffn_bwd(df, gate, up, w_gate, w_up, w_down):
    S = df.shape[0]
    ts = min(S, FF_STEP_ROWS)

    def body(df_ref, gate_ref, up_ref, wg_ref, wu_ref, wd_ref, a_ref, dgate_ref, dup_ref, dh_ref):
        def rows_pass(first):
            def sub(i, carry):
                rows = pl.ds(pl.multiple_of(i * FF_BWD_ROWS, FF_BWD_ROWS), FF_BWD_ROWS)
                da = _dot_nt(df_ref[rows, :], wd_ref[...])
                g = gate_ref[0, rows, :].astype(f32)
                u = up_ref[0, rows, :].astype(f32)
                sig = _sigmoid(g)
                silu = g * sig
                a_ref[0, rows, :] = (silu * u).astype(bf16)
                dup = (da * silu).astype(bf16)
                dgate = (da * u * (sig * (1.0 + g * (1.0 - sig)))).astype(bf16)
                dup_ref[0, rows, :] = dup
                dgate_ref[0, rows, :] = dgate
                part = _dot(dgate, wg_ref[...]) + _dot(dup, wu_ref[...])
                if first:
                    dh_ref[rows, :] = part
                else:
                    dh_ref[rows, :] += part
                return carry

            lax.fori_loop(0, ts // FF_BWD_ROWS, sub, 0, unroll=True)

        @pl.when(pl.program_id(1) == 0)
        def _():
            rows_pass(True)

        @pl.when(pl.program_id(1) > 0)
        def _():
            rows_pass(False)

    act = _ff_act_spec(ts)
    row = pl.BlockSpec((ts, D_MODEL), lambda i, j: (i, 0))
    return pl.pallas_call(
        body, name="ffn_bwd", grid=(S // ts, D_FF // FF_TILE),
        in_specs=[row, act, act,
                  pl.BlockSpec((FF_TILE, D_MODEL), lambda i, j: (j, 0)),
                  pl.BlockSpec((FF_TILE, D_MODEL), lambda i, j: (j, 0)),
                  pl.BlockSpec((FF_TILE, D_MODEL), lambda i, j: (j, 0))],
        out_specs=[act, act, act, row],
        out_shape=[_ff_act_shape(S)] * 3 + [jax.ShapeDtypeStruct((S, D_MODEL), f32)],
        compiler_params=_params("parallel", "arbitrary"),
    )(df, gate, up, w_gate, w_up, w_down)


def _norm_bwd(dh2, dy, x2, mix, g3, g2):
    S = dh2.shape[0]
    ts = 512

    def body(dh_ref, dy_ref, x2_ref, mix_ref, g3_ref, g2_ref, dx2_ref, dmix_ref, dg3_ref, dg2_ref):
        @pl.when(pl.program_id(0) == 0)
        def _():
            dg3_ref[...] = jnp.zeros_like(dg3_ref)
            dg2_ref[...] = jnp.zeros_like(dg2_ref)

        dh = dh_ref[...]
        x2 = x2_ref[...]
        r3 = _rms(x2)
        xhat = x2 * r3
        dg3_ref[...] += jnp.sum(dh * xhat, axis=0, keepdims=True)
        dhg = dh * g3_ref[...]
        dx2 = dy_ref[...] + r3 * (dhg - xhat * jnp.mean(dhg * xhat, axis=-1, keepdims=True))
        dx2_ref[...] = dx2
        mix = mix_ref[...]
        r2 = _rms(mix)
        mhat = mix * r2
        dg2_ref[...] += jnp.sum(dx2 * mhat, axis=0, keepdims=True)
        dmg = dx2 * g2_ref[...]
        dmix_ref[...] = (r2 * (dmg - mhat * jnp.mean(dmg * mhat, axis=-1, keepdims=True))).astype(bf16)

    row = pl.BlockSpec((ts, D_MODEL), lambda i: (i, 0))
    gain = pl.BlockSpec((1, D_MODEL), lambda i: (0, 0))
    return pl.pallas_call(
        body, name="norm_bwd", grid=(S // ts,), in_specs=[row, row, row, row, gain, gain],
        out_specs=[row, row, gain, gain],
        out_shape=[jax.ShapeDtypeStruct((S, D_MODEL), f32), jax.ShapeDtypeStruct((S, D_MODEL), bf16),
                   jax.ShapeDtypeStruct((1, D_MODEL), f32), jax.ShapeDtypeStruct((1, D_MODEL), f32)],
        compiler_params=_params("arbitrary"),
    )(dh2, dy, x2, mix, g3, g2)


def _out_proj_bwd(dmix, w_out, attn_out, head_ones, grads):
    S = dmix.shape[0]
    ts = 512
    n_dil = len(DILATIONS)

    def body(dm_ref, w_ref, o_ref, ones_ref, g_ref, dp_ref, dl_ref, *rest):
        do_refs, theirs_ref = rest[:n_dil], rest[n_dil]
        stage = rest[n_dil + 1:n_dil + 1 + N_STAGE]
        swap = _Swap(g_ref, theirs_ref, *rest[n_dil + 1 + N_STAGE:])

        @pl.when(pl.program_id(0) == 0)
        def _():
            swap.start()

        @pl.when(pl.program_id(0) == S // ts - 1)
        def _():
            swap.finish()

        dcat = _dot_nt(dm_ref[...], w_ref[...])
        dp_ref[...] = dcat[:, :POOL_WIDTH]
        do = dcat[:, POOL_WIDTH:]
        for j in range(ATTN_WIDTH // 128):
            stage[j][...] = do[:, j * 128:(j + 1) * 128]
        _store_streams(stage, do_refs, ts)
        prod = do * o_ref[...].astype(f32)
        hi = prod.astype(bf16)
        lo = (prod - hi.astype(f32)).astype(bf16)
        ones = ones_ref[...]
        for j in range(ATTN_WIDTH // 128):
            cols = slice(j * 128, (j + 1) * 128)
            dl_ref[:, cols] = _dot(hi[:, cols], ones) + _dot(lo[:, cols], ones)

    row = lambda w: pl.BlockSpec((ts, w), lambda i: (i, 0))
    res = pl.pallas_call(
        body, name="out_proj_bwd", grid=(S // ts,),
        in_specs=[row(D_MODEL), pl.BlockSpec((D_MODEL, D_MODEL), lambda i: (0, 0)), row(ATTN_WIDTH),
                  pl.BlockSpec((128, 128), lambda i: (0, 0)), ANY],
        out_specs=[row(POOL_WIDTH), row(ATTN_WIDTH)] + [_stream_spec(d, ts) for d in DILATIONS] + [ANY],
        out_shape=[jax.ShapeDtypeStruct((S, POOL_WIDTH), f32), jax.ShapeDtypeStruct((S, ATTN_WIDTH), f32)]
        + [_stream_shape(S, d) for d in DILATIONS] + [_Swap.out_shape(grads)],
        scratch_shapes=_stage_scratch(ts) + _Swap.scratch(grads),
        compiler_params=_params("arbitrary"),
    )(dmix, w_out, attn_out, head_ones, grads)
    return res[0], res[1], res[2:2 + n_dil], res[2 + n_dil]


def _in_proj_bwd(du, dq, dk, dv, cos_t, sin_t, w_in, x, dx2, g1):
    S = x.shape[0]
    ts = 512

    def body(du_ref, dq_ref, dk_ref, dv_ref, cos_ref, sin_ref, w_ref, x_ref, dx2_ref, g_ref, gx_ref, dproj_ref, dg_ref):
        @pl.when(pl.program_id(0) == 0)
        def _():
            dg_ref[...] = jnp.zeros_like(dg_ref)

        dproj_ref[:, :POOL_WIDTH] = du_ref[...]
        cos = cos_ref[...]
        sin = sin_ref[...]
        first = _first_half_mask(ts)
        for j in range(ATTN_WIDTH // 128):
            cols = slice(j * 128, (j + 1) * 128)
            for base, ref in ((POOL_WIDTH, dq_ref), (POOL_WIDTH + ATTN_WIDTH, dk_ref)):
                g = ref[:, cols].astype(f32)
                pre = g * cos + _rope_partner(g * sin, first)
                dproj_ref[:, base + j * 128: base + (j + 1) * 128] = pre.astype(bf16)
        dproj_ref[:, POOL_WIDTH + 2 * ATTN_WIDTH:] = dv_ref[...]

        dh = _dot(dproj_ref[...], w_ref[...])
        xv = x_ref[...]
        r = _rms(xv)
        xhat = xv * r
        dg_ref[...] += jnp.sum(dh * xhat, axis=0, keepdims=True)
        dhg = dh * g_ref[...]
        gx_ref[...] = dx2_ref[...] + r * (dhg - xhat * jnp.mean(dhg * xhat, axis=-1, keepdims=True))

    row = lambda w: pl.BlockSpec((ts, w), lambda i: (i, 0))
    gain = pl.BlockSpec((1, D_MODEL), lambda i: (0, 0))
    return pl.pallas_call(
        body, name="in_proj_bwd", grid=(S // ts,),
        in_specs=[row(POOL_WIDTH)] + [row(ATTN_WIDTH)] * 3 + [row(128), row(128),
                  pl.BlockSpec((IN_WIDTH, D_MODEL), lambda i: (0, 0)), row(D_MODEL), row(D_MODEL), gain],
        out_specs=[row(D_MODEL), row(IN_WIDTH), gain],
        out_shape=[jax.ShapeDtypeStruct((S, D_MODEL), f32), jax.ShapeDtypeStruct((S, IN_WIDTH), bf16),
                   jax.ShapeDtypeStruct((1, D_MODEL), f32)],
        compiler_params=_params("arbitrary"),
    )(du, dq, dk, dv, cos_t, sin_t, w_in, x, dx2, g1)


def _matmul_tiles_tn(a, b, name):
    T, K, w = a.shape
    N = b.shape[1]
    tk = 1024

    def body(a_ref, b_ref, o_ref):
        def tiles_pass(first):
            for t in range(T):
                part = _dot_tn(a_ref[t], b_ref[...])
                if first:
                    o_ref[t * w:(t + 1) * w, :] = part
                else:
                    o_ref[t * w:(t + 1) * w, :] += part

        @pl.when(pl.program_id(0) == 0)
        def _():
            tiles_pass(True)

        @pl.when(pl.program_id(0) > 0)
        def _():
            tiles_pass(False)

    return pl.pallas_call(
        body, name=name, grid=(K // tk,),
        in_specs=[pl.BlockSpec((T, tk, w), lambda k: (0, k, 0)), pl.BlockSpec((tk, N), lambda k: (k, 0))],
        out_specs=pl.BlockSpec((T * w, N), lambda k: (0, 0)),
        out_shape=jax.ShapeDtypeStruct((T * w, N), f32),
        compiler_params=_params("arbitrary"),
    )(a, b)


def _matmul_tn(a, b, name):
    K, M = a.shape
    N = b.shape[1]
    tk = 1024

    def body(a_ref, b_ref, o_ref):
        _tn_step(a_ref, b_ref, o_ref, M)

    return pl.pallas_call(
        body, name=name, grid=(K // tk,),
        in_specs=[pl.BlockSpec((tk, M), lambda k: (k, 0)), pl.BlockSpec((tk, N), lambda k: (k, 0))],
        out_specs=pl.BlockSpec((M, N), lambda k: (0, 0)),
        out_shape=jax.ShapeDtypeStruct((M, N), f32),
        compiler_params=_params("arbitrary"),
    )(a, b)


def _tn_step(a_ref, b_ref, o_ref, M):
    w = 256

    def tiles_pass(first):
        for t in range(M // w):
            part = _dot_tn(a_ref[:, t * w:(t + 1) * w], b_ref[...])
            if first:
                o_ref[t * w:(t + 1) * w, :] = part
            else:
                o_ref[t * w:(t + 1) * w, :] += part

    @pl.when(pl.program_id(0) == 0)
    def _():
        tiles_pass(True)

    @pl.when(pl.program_id(0) > 0)
    def _():
        tiles_pass(False)


def _matmul_tn_and_small_sum(a, b, block, name):
    K, M = a.shape
    N = b.shape[1]
    tk = 1024
    n_steps = K // tk

    def body(a_ref, b_ref, block_ref, o_ref, total_ref, *scratch):
        small = _SmallSum(block_ref, *scratch)

        @pl.when(pl.program_id(0) == 0)
        def _():
            small.start()

        _tn_step(a_ref, b_ref, o_ref, M)

        @pl.when(pl.program_id(0) == n_steps - 1)
        def _():
            small.finish(total_ref)

    return pl.pallas_call(
        body, name=name, grid=(n_steps,),
        in_specs=[pl.BlockSpec((tk, M), lambda k: (k, 0)), pl.BlockSpec((tk, N), lambda k: (k, 0)), ANY],
        out_specs=[pl.BlockSpec((M, N), lambda k: (0, 0)), pl.BlockSpec(block.shape, lambda k: (0, 0))],
        out_shape=[jax.ShapeDtypeStruct((M, N), f32), jax.ShapeDtypeStruct(block.shape, block.dtype)],
        scratch_shapes=_SmallSum.scratch(block),
        compiler_params=_params("arbitrary"),
    )(a, b, block)


def _rope_tables(S):
    half = HEAD_DIM // 2
    freqs = ROPE_THETA ** (-jnp.arange(half, dtype=f32) * (2.0 / HEAD_DIM))
    ang = jnp.arange(S).astype(f32)[:, None] * freqs[None, :]
    cos = jnp.tile(jnp.cos(ang), (1, 4))
    sin = jnp.sin(ang)
    sin = jnp.tile(jnp.concatenate([-sin, sin], axis=1), (1, 2))
    return cos, sin


def _block_diag(w_pool):
    w = jnp.zeros((POOL_WIDTH, POOL_WIDTH), w_pool.dtype)
    for g in range(POOL_WIDTH // POOL_GROUP):
        w = lax.dynamic_update_slice(w, w_pool[g], (g * POOL_GROUP, g * POOL_GROUP))
    return w


def _head_ones():
    head = np.arange(128) // HEAD_DIM
    return jnp.asarray(head[:, None] == head[None, :], dtype=bf16)


def _place():
    x, y, c = lax.axis_index("x"), lax.axis_index("y"), lax.axis_index("c")
    chips = [(1 - x, y), (x, 1 - y), (1 - x, 1 - y)]
    return x, y, c, chips


ANY = pl.BlockSpec(memory_space=pl.ANY)
N_PEER_CHIPS = N_CHIPS - 1
ICI_PIECES = 4
D2D_PIECES = 8
LOCAL_PIECES = 8


def _row_chunks(rows, n, unit=32):
    units = rows // unit
    out, start = [], 0
    for i in range(n):
        size = (units // n + (1 if i < units % n else 0)) * unit
        out.append((start, size))
        start += size
    return [piece for piece in out if piece[1]]


class _LocalCopy:
    def __init__(self, src_rows, dst_rows, rows, buf, sems_in, sems_out):
        self.loads, self.stores = [], []
        for i, (start, size) in enumerate(_row_chunks(rows, LOCAL_PIECES)):
            r = pl.ds(start, size)
            self.loads.append(pltpu.make_async_copy(src_rows(r), buf.at[r], sems_in.at[i]))
            self.stores.append(pltpu.make_async_copy(buf.at[r], dst_rows(r), sems_out.at[i]))

    def start(self):
        for cp in self.loads:
            cp.start()

    def pass_on(self):
        for load, store in zip(self.loads, self.stores):
            load.wait()
            store.start()

    def finish(self):
        for store in self.stores:
            store.wait()

    @staticmethod
    def scratch(rows, dtype):
        return [pltpu.VMEM((rows, D_MODEL), dtype), pltpu.SemaphoreType.DMA((LOCAL_PIECES,)),
                pltpu.SemaphoreType.DMA((LOCAL_PIECES,))]


class _Gather:
    def __init__(self, w_ref, out_ref, send1, recv1, send2, recv2, buf, sems_in, sems_out):
        x, y, c, chips = _place()
        me = 2 * x + y
        rows = w_ref.shape[0]
        half = rows // 2
        pieces = _row_chunks(half, ICI_PIECES)
        self.own = _LocalCopy(lambda r: w_ref.at[r], lambda r: out_ref.at[me, r], rows, buf, sems_in, sems_out)

        def rows_of(core, piece):
            start, size = piece
            return pl.ds(core * half + start, size)

        self.sends, self.arrivals, self.forwards, self.forward_arrivals = [], [], [], []
        for i, piece in enumerate(pieces):
            for j, (cx, cy) in enumerate(chips):
                k = j * len(pieces) + i
                there = 2 * cx + cy

                def direct(src_chip, cx=cx, cy=cy, k=k, piece=piece):
                    return pltpu.make_async_remote_copy(
                        src_ref=w_ref.at[rows_of(c, piece)], dst_ref=out_ref.at[src_chip, rows_of(c, piece)],
                        send_sem=send1.at[k], recv_sem=recv1.at[k], device_id=(cx, cy, c), device_id_type=MESH)

                def passed(core, there=there, k=k, piece=piece):
                    return pltpu.make_async_remote_copy(
                        src_ref=out_ref.at[there, rows_of(core, piece)], dst_ref=out_ref.at[there, rows_of(core, piece)],
                        send_sem=send2.at[k], recv_sem=recv2.at[k], device_id=(x, y, 1 - c), device_id_type=MESH)

                self.sends.append(direct(me))
                self.arrivals.append(direct(there))
                self.forwards.append(passed(c))
                self.forward_arrivals.append(passed(1 - c))

    def start(self):
        for cp in self.sends:
            cp.start()
        self.own.start()

    def pass_on(self):
        self.own.pass_on()
        for arrival, forward in zip(self.arrivals, self.forwards):
            arrival.wait_recv()
            forward.start()

    def finish(self):
        for arrival in self.forward_arrivals:
            arrival.wait_recv()
        for cp in self.sends + self.forwards:
            cp.wait_send()
        self.own.finish()

    @staticmethod
    def scratch(rows, dtype):
        n = N_PEER_CHIPS * len(_row_chunks(rows // 2, ICI_PIECES))
        return [pltpu.SemaphoreType.DMA((n,))] * 4 + _LocalCopy.scratch(rows, dtype)

    @staticmethod
    def out_shape(rows, dtype):
        return jax.ShapeDtypeStruct((N_CHIPS, rows, D_MODEL), dtype)


def _gather_weights(pack):
    rows = pack.shape[0]

    def body(w_ref, out_ref, *scratch):
        gather = _Gather(w_ref, out_ref, *scratch)
        gather.start()
        gather.pass_on()
        gather.finish()

    return pl.pallas_call(
        body, name="gather_weights", in_specs=[ANY], out_specs=ANY, out_shape=_Gather.out_shape(rows, pack.dtype),
        scratch_shapes=_Gather.scratch(rows, pack.dtype),
        compiler_params=pltpu.CompilerParams(vmem_limit_bytes=VMEM_LIMIT_V7X),
    )(pack)


class _Scatter:
    def __init__(self, h_ref, out_ref, send, recv):
        x, y, c, chips = _place()
        pieces = _row_chunks(h_ref.shape[1], ICI_PIECES)
        self.copies = []
        for i, (start, size) in enumerate(pieces):
            for j, (cx, cy) in enumerate(chips):
                k = j * len(pieces) + i
                self.copies.append(pltpu.make_async_remote_copy(
                    src_ref=h_ref.at[2 * cx + cy, pl.ds(start, size)], dst_ref=out_ref.at[j, pl.ds(start, size)],
                    send_sem=send.at[k], recv_sem=recv.at[k], device_id=(cx, cy, c), device_id_type=MESH))

    def start(self):
        for cp in self.copies:
            cp.start()

    def finish(self):
        for cp in self.copies:
            cp.wait_recv()
        for cp in self.copies:
            cp.wait_send()

    @staticmethod
    def scratch(half):
        n = N_PEER_CHIPS * len(_row_chunks(half, ICI_PIECES))
        return [pltpu.SemaphoreType.DMA((n,))] * 2

    @staticmethod
    def out_shape(half, dtype):
        return jax.ShapeDtypeStruct((N_PEER_CHIPS, half, D_MODEL), dtype)


def _scatter_to_chips(h):
    half = h.shape[1]

    def body(h_ref, out_ref, send, recv):
        scatter = _Scatter(h_ref, out_ref, send, recv)
        scatter.start()
        scatter.finish()

    return pl.pallas_call(
        body, name="scatter_to_chips", in_specs=[ANY], out_specs=ANY, out_shape=_Scatter.out_shape(half, h.dtype),
        scratch_shapes=_Scatter.scratch(half),
    )(h)


class _Swap:
    def __init__(self, g_ref, theirs_ref, send, recv):
        x, y, c, _ = _place()
        half = g_ref.shape[1] // 2
        pieces = _row_chunks(half, D2D_PIECES)
        self.copies = []
        for s in range(N_CHIPS):
            for i, (start, size) in enumerate(pieces):
                k = s * len(pieces) + i
                self.copies.append(pltpu.make_async_remote_copy(
                    src_ref=g_ref.at[s, pl.ds((1 - c) * half + start, size)], dst_ref=theirs_ref.at[s, pl.ds(start, size)],
                    send_sem=send.at[k], recv_sem=recv.at[k], device_id=(x, y, 1 - c), device_id_type=MESH))

    def start(self):
        for cp in self.copies:
            cp.start()

    def finish(self):
        for cp in self.copies:
            cp.wait()

    @staticmethod
    def scratch(g):
        n = N_CHIPS * len(_row_chunks(g.shape[1] // 2, D2D_PIECES))
        return [pltpu.SemaphoreType.DMA((n,))] * 2

    @staticmethod
    def out_shape(g):
        return jax.ShapeDtypeStruct((N_CHIPS, g.shape[1] // 2, D_MODEL), g.dtype)


def _swap_halves(g):
    def body(g_ref, theirs_ref, send, recv):
        swap = _Swap(g_ref, theirs_ref, send, recv)
        swap.start()
        swap.finish()

    return pl.pallas_call(
        body, name="swap_halves", in_specs=[ANY], out_specs=ANY, out_shape=_Swap.out_shape(g),
        scratch_shapes=_Swap.scratch(g),
    )(g)


ADD_TILE_MAX_ROWS = 600


def _add_tile(half):
    return max(t for t in range(8, ADD_TILE_MAX_ROWS + 1, 8) if half % t == 0)


def _add_cores(g, theirs, name, out_dtype=f32):
    half = theirs.shape[1]
    tr = _add_tile(half)
    n_t = half // tr

    def body(c_ref, g_ref, t_ref, o_ref):
        o_ref[...] = (g_ref[...] + t_ref[...]).astype(out_dtype)

    blk = pl.BlockSpec((1, tr, D_MODEL), lambda s, t, c_ref: (s, t, 0))
    return pl.pallas_call(
        body, name=name,
        grid_spec=pltpu.PrefetchScalarGridSpec(
            num_scalar_prefetch=1, grid=(N_CHIPS, n_t),
            in_specs=[pl.BlockSpec((1, tr, D_MODEL), lambda s, t, c_ref: (s, c_ref[0] * n_t + t, 0)), blk],
            out_specs=blk),
        out_shape=jax.ShapeDtypeStruct(theirs.shape, out_dtype),
        compiler_params=_params("parallel", "parallel"),
    )(lax.axis_index("c").astype(jnp.int32).reshape(1), g, theirs)


def _add_chips(chip_sum, others, name):
    half = chip_sum.shape[1]
    tr = _add_tile(half)

    def body(me_ref, own_ref, o0, o1, o2, out_ref):
        out_ref[...] = ((own_ref[0].astype(f32) + o0[0].astype(f32)) + o1[0].astype(f32)) + o2[0].astype(f32)

    other = lambda j: pl.BlockSpec((1, tr, D_MODEL), lambda t, me_ref: (j, t, 0))
    return pl.pallas_call(
        body, name=name,
        grid_spec=pltpu.PrefetchScalarGridSpec(
            num_scalar_prefetch=1, grid=(half // tr,),
            in_specs=[pl.BlockSpec((1, tr, D_MODEL), lambda t, me_ref: (me_ref[0], t, 0)), other(0), other(1), other(2)],
            out_specs=pl.BlockSpec((tr, D_MODEL), lambda t, me_ref: (t, 0))),
        out_shape=jax.ShapeDtypeStruct((half, D_MODEL), f32),
        compiler_params=_params("parallel"),
    )((2 * lax.axis_index("x") + lax.axis_index("y")).astype(jnp.int32).reshape(1), chip_sum, others, others, others)


def _join_halves(parts):
    n_parts = len(parts)
    pieces = [_row_chunks(r.shape[0], D2D_PIECES) for r in parts]
    first = [sum(len(p) for p in pieces[:i]) for i in range(n_parts)]
    n = sum(len(p) for p in pieces)

    def body(*refs):
        r_refs, out_refs = refs[:n_parts], refs[n_parts:2 * n_parts]
        send, recv = refs[2 * n_parts:2 * n_parts + 2]
        local = refs[2 * n_parts + 2:]
        x, y, c, _ = _place()
        owns = [_LocalCopy(lambda rr, r_ref=r_ref: r_ref.at[rr], lambda rr, out_ref=out_ref: out_ref.at[c, rr],
                           r_ref.shape[0], *local[3 * i:3 * i + 3])
                for i, (r_ref, out_ref) in enumerate(zip(r_refs, out_refs))]
        for own in owns:
            own.start()

        def piece(i, j, core):
            start, size = pieces[i][j]
            return pltpu.make_async_remote_copy(
                src_ref=r_refs[i].at[pl.ds(start, size)], dst_ref=out_refs[i].at[core, pl.ds(start, size)],
                send_sem=send.at[first[i] + j], recv_sem=recv.at[first[i] + j],
                device_id=(x, y, 1 - c), device_id_type=MESH)

        every = [(i, j) for i in range(n_parts) for j in range(len(pieces[i]))]
        copies = [piece(i, j, c) for i, j in every]
        for cp in copies:
            cp.start()
        for own in owns:
            own.pass_on()
        for i, j in every:
            piece(i, j, 1 - c).wait_recv()
        for cp in copies:
            cp.wait_send()
        for own in owns:
            own.finish()

    local_scratch = []
    for r in parts:
        local_scratch += _LocalCopy.scratch(r.shape[0], r.dtype)
    return pl.pallas_call(
        body, name="join_halves", in_specs=[ANY] * n_parts, out_specs=[ANY] * n_parts,
        out_shape=[jax.ShapeDtypeStruct((2,) + r.shape, r.dtype) for r in parts],
        scratch_shapes=[pltpu.SemaphoreType.DMA((n,))] * 2 + local_scratch,
        compiler_params=pltpu.CompilerParams(vmem_limit_bytes=VMEM_LIMIT_V7X),
    )(*parts)


class _SmallSum:
    def __init__(self, b_ref, gathered, send, recv, local_sem):
        x, y, c, _ = _place()
        me = 4 * x + 2 * y + c
        self.gathered = gathered
        self.own = pltpu.make_async_copy(b_ref, gathered.at[me], local_sem)
        self.sends, self.arrivals = [], []
        for kk in range(1, N_DEV):
            flip = lambda v, bit: 1 - v if bit else v
            peer = (flip(x, kk & 4), flip(y, kk & 2), flip(c, kk & 1))
            self.sends.append(pltpu.make_async_remote_copy(
                src_ref=b_ref, dst_ref=gathered.at[me], send_sem=send.at[kk - 1], recv_sem=recv.at[kk - 1],
                device_id=peer, device_id_type=MESH))
            self.arrivals.append(pltpu.make_async_remote_copy(
                src_ref=b_ref, dst_ref=gathered.at[jnp.bitwise_xor(me, kk)], send_sem=send.at[kk - 1],
                recv_sem=recv.at[kk - 1], device_id=peer, device_id_type=MESH))

    def start(self):
        self.own.start()
        for cp in self.sends:
            cp.start()

    def finish(self, out_ref):
        self.own.wait()
        for cp in self.arrivals:
            cp.wait_recv()
        for cp in self.sends:
            cp.wait_send()
        acc = self.gathered[0]
        for dev in range(1, N_DEV):
            acc = acc + self.gathered[dev]
        out_ref[...] = acc

    @staticmethod
    def scratch(block):
        return [pltpu.VMEM((N_DEV,) + block.shape, block.dtype), pltpu.SemaphoreType.DMA((N_DEV - 1,)),
                pltpu.SemaphoreType.DMA((N_DEV - 1,)), pltpu.SemaphoreType.DMA]


def _adamw(w, g, m, v, name):
    rows, cols = w.shape
    tr = max(t for t in range(8, 513, 8) if rows % t == 0)
    c1 = 1.0 - ADAM_B1 ** ADAM_STEP
    c2 = 1.0 - ADAM_B2 ** ADAM_STEP

    def body(w_ref, g_ref, m_ref, v_ref, d_ref, nm_ref, nv_ref):
        gv = g_ref[...]
        nm = ADAM_B1 * m_ref[...] + (1.0 - ADAM_B1) * gv
        nv = ADAM_B2 * v_ref[...] + (1.0 - ADAM_B2) * (gv * gv)
        nm_ref[...] = nm
        nv_ref[...] = nv
        d_ref[...] = -ADAM_LR * ((nm / c1) / (jnp.sqrt(nv / c2) + ADAM_EPS) + ADAM_WD * w_ref[...])

    blk = pl.BlockSpec((tr, cols), lambda i: (i, 0))
    shape = jax.ShapeDtypeStruct((rows, cols), f32)
    return pl.pallas_call(
        body, name=name, grid=(rows // tr,), in_specs=[blk] * 4, out_specs=[blk] * 3, out_shape=[shape] * 3,
        compiler_params=_params("parallel"),
    )(w, g, m, v)


LARGE = ("w_in", "w_out", "w_gate", "w_up", "w_down")
SMALL = ("ln_pre_mix", "ln_post_mix", "ln_pre_ffn", "ln_post_ffn", "pool_scale", "w_pool")
SHARD_ROWS = {"w_in": 640, "w_out": 256, "w_gate": 704, "w_up": 704, "w_down": 704}
COLUMN_SHARDED = ("w_in", "w_gate", "w_up")
UPDATED_TRANSPOSED = ("w_gate", "w_up")
NEEDED_FIRST = ("w_in",)
NEEDED_LATER = ("w_out", "w_gate", "w_up", "w_down")
READY_EARLY = ("w_out", "w_gate", "w_up", "w_down")
READY_LATE = ("w_in",)


def _pack_shard(shards, names):
    return jnp.concatenate([shards[n].T if n in COLUMN_SHARDED else shards[n] for n in names], axis=0)


def _unpack_shard(pack, names):
    out, row = {}, 0
    for n in names:
        out[n] = pack[row:row + SHARD_ROWS[n]]
        row += SHARD_ROWS[n]
    return out


def _whole_from_shards(packs, names):
    out, row = {}, 0
    for n in names:
        rows = SHARD_ROWS[n]
        out[n] = packs[:, row:row + rows].reshape(N_CHIPS * rows, D_MODEL)
        row += rows
    return out


def _shards_from_whole(grads, names):
    return jnp.concatenate([grads[n].reshape(N_CHIPS, SHARD_ROWS[n], D_MODEL) for n in names], axis=1)


def _pack_small(vals):
    rows = [vals[n].reshape(1, D_MODEL) for n in SMALL[:4]]
    rows.append(jnp.pad(vals["pool_scale"].reshape(1, POOL_WIDTH), ((0, 0), (0, D_MODEL - POOL_WIDTH))))
    rows.append(jnp.pad(vals["loss"].reshape(1, 1), ((0, 0), (0, D_MODEL - 1))))
    rows.append(jnp.zeros((2, D_MODEL), f32))
    rows.append(vals["w_pool"].reshape(16, D_MODEL))
    return jnp.concatenate(rows, axis=0)


def _unpack_small(block):
    out = {n: block[i:i + 1] for i, n in enumerate(SMALL[:4])}
    out["pool_scale"] = block[4:5, :POOL_WIDTH]
    out["loss"] = block[5, 0]
    out["w_pool"] = block[8:24].reshape(1, 4, POOL_GROUP, POOL_GROUP)
    return out


def kernel(x, ln_pre_mix, w_in, w_pool, pool_scale, w_out, ln_post_mix, ln_pre_ffn, w_gate, w_up, w_down, ln_post_ffn, loss_target, m_ln_pre_mix, m_w_in, m_w_pool, m_pool_scale, m_w_out, m_ln_post_mix, m_ln_pre_ffn, m_w_gate, m_w_up, m_w_down, m_ln_post_ffn, v_ln_pre_mix, v_w_in, v_w_pool, v_pool_scale, v_w_out, v_ln_post_mix, v_ln_pre_ffn, v_w_gate, v_w_up, v_w_down, v_ln_post_ffn):
    w = dict(ln_pre_mix=ln_pre_mix, w_in=w_in, w_pool=w_pool, pool_scale=pool_scale, w_out=w_out,
             ln_post_mix=ln_post_mix, ln_pre_ffn=ln_pre_ffn, w_gate=w_gate, w_up=w_up, w_down=w_down,
             ln_post_ffn=ln_post_ffn)
    m = dict(ln_pre_mix=m_ln_pre_mix, w_in=m_w_in, w_pool=m_w_pool, pool_scale=m_pool_scale, w_out=m_w_out,
             ln_post_mix=m_ln_post_mix, ln_pre_ffn=m_ln_pre_ffn, w_gate=m_w_gate, w_up=m_w_up, w_down=m_w_down,
             ln_post_ffn=m_ln_post_ffn)
    v = dict(ln_pre_mix=v_ln_pre_mix, w_in=v_w_in, w_pool=v_w_pool, pool_scale=v_pool_scale, w_out=v_w_out,
             ln_post_mix=v_ln_post_mix, ln_pre_ffn=v_ln_pre_ffn, w_gate=v_w_gate, w_up=v_w_up, w_down=v_w_down,
             ln_post_ffn=v_ln_post_ffn)

    xs, target = x[0], loss_target[0]
    cos_t, sin_t = _rope_tables(xs.shape[0])
    w_bd = _block_diag(w_pool[0]).astype(bf16)
    shard = {n: w[n][0].astype(bf16) for n in LARGE}

    w_in_whole = _whole_from_shards(_gather_weights(_pack_shard(shard, NEEDED_FIRST)), NEEDED_FIRST)["w_in"]
    h1, u, qs, ks, vs = _in_proj(xs, ln_pre_mix, w_in_whole, cos_t, sin_t)
    pool_out = _pool_fwd(u, w_bd, pool_scale)
    attn_out, lse, later = _attn_fwd(qs, ks, vs, _pack_shard(shard, NEEDED_LATER))
    whole = _whole_from_shards(later, NEEDED_LATER)
    mix, x2, h2 = _out_proj(pool_out, attn_out, whole["w_out"], xs, ln_post_mix, ln_pre_ffn)
    gate, up, f = _ffn_fwd(h2, whole["w_gate"], whole["w_up"], whole["w_down"])
    dy, df, dg4, loss = _loss_head(f, x2, target, ln_post_ffn)

    large = {}
    a, dgate, dup, dh2 = _ffn_bwd(df, gate, up, whole["w_gate"], whole["w_up"], whole["w_down"])
    large["w_down"] = _matmul_tiles_tn(a, df, "grad_w_down")
    large["w_gate"] = _matmul_tiles_tn(dgate, h2, "grad_w_gate")
    large["w_up"] = _matmul_tiles_tn(dup, h2, "grad_w_up")
    dx2, dmix, dg3, dg2 = _norm_bwd(dh2, dy, x2, mix, ln_pre_ffn, ln_post_mix)
    large["w_out"] = jnp.concatenate([_matmul_tn(pool_out, dmix, "grad_w_out_pool"),
                                      _matmul_tn(attn_out, dmix, "grad_w_out_attn")], axis=0)
    early = _shards_from_whole(large, READY_EARLY)
    dpool, delta, dos, early_theirs = _out_proj_bwd(dmix, whole["w_out"], attn_out, _head_ones(), early)
    early_chip = _add_cores(early, early_theirs, "add_cores_early")
    du, d_w_bd, d_scale = _pool_bwd(u, dpool, w_bd, pool_scale)
    dq, dk, dv, early_others = _attn_bwd(qs, ks, vs, dos, lse, delta, early_chip)
    grad_x, dproj, dg1 = _in_proj_bwd(du, dq, dk, dv, cos_t, sin_t, w_in_whole, xs, dx2, ln_pre_mix)
    d_w_pool = jnp.stack([d_w_bd[g * POOL_GROUP:(g + 1) * POOL_GROUP, g * POOL_GROUP:(g + 1) * POOL_GROUP]
                          for g in range(POOL_WIDTH // POOL_GROUP)])
    small = dict(ln_pre_mix=dg1, ln_post_mix=dg2, ln_pre_ffn=dg3, ln_post_ffn=dg4, pool_scale=d_scale, w_pool=d_w_pool)
    large["w_in"], small_total = _matmul_tn_and_small_sum(dproj, h1, _pack_small(dict(small, loss=loss)), "grad_w_in")
    late = _shards_from_whole(large, READY_LATE)
    late_chip = _add_cores(late, _swap_halves(late), "add_cores_late", bf16)
    late_others = _scatter_to_chips(late_chip)
    early_half = _add_chips(early_chip, early_others, "add_chips_early")
    late_half = _add_chips(late_chip, late_others, "add_chips_late")
    early_whole, late_whole = _join_halves([early_half, late_half])
    grads = _unpack_shard(early_whole.reshape(-1, D_MODEL), READY_EARLY)
    grads.update(_unpack_shard(late_whole.reshape(-1, D_MODEL), READY_LATE))

    total = _unpack_small(small_total)
    for n in SMALL:
        grads[n] = total[n]

    delta_w, new_m, new_v = {}, {}, {}
    for n in LARGE:
        if n in UPDATED_TRANSPOSED:
            update = _adamw(w[n][0].T, grads[n], m[n][0].T, v[n][0].T, "adamw_" + n)
            delta_w[n], new_m[n], new_v[n], grads[n] = [a.T for a in (*update, grads[n])]
        else:
            if n in COLUMN_SHARDED:
                grads[n] = grads[n].T
            delta_w[n], new_m[n], new_v[n] = _adamw(w[n][0], grads[n], m[n][0], v[n][0], "adamw_" + n)
    small_state = [_pack_small(dict({n: s[n] for n in SMALL}, loss=jnp.zeros((), f32))) for s in (w, m, v)]
    small_grad = _pack_small(dict({n: grads[n] for n in SMALL}, loss=jnp.zeros((), f32)))
    sd, sm, sv = _adamw(small_state[0], small_grad, small_state[1], small_state[2], "adamw_small")
    for out, block in ((delta_w, sd), (new_m, sm), (new_v, sv)):
        un = _unpack_small(block)
        for n in SMALL:
            out[n] = un[n]

    names = ("ln_pre_mix", "w_in", "w_pool", "pool_scale", "w_out", "ln_post_mix", "ln_pre_ffn", "w_gate", "w_up",
             "w_down", "ln_post_ffn")
    full = lambda d: [d[n].reshape(w[n].shape) for n in names]
    return (total["loss"], grad_x[None], *full(grads), *full(delta_w), *full(new_m), *full(new_v))
```

```python
import numpy as np
import jax
import jax.numpy as jnp
from jax import lax
from jax.experimental import pallas as pl
from jax.experimental.pallas import tpu as pltpu

D_MODEL = 1024
POOL_WIDTH = 256
POOL_GROUP = 64
ATTN_WIDTH = 768
HEAD_DIM = 64
IN_WIDTH = 2560
D_FF = 2816
BLOCK = 128
DILATIONS = (1, 4, 16)
ROPE_THETA = 10000.0
EPS = 1e-6
ATTN_SCALE = 0.125
NEG = -1e30

ADAM_LR = 0.001
ADAM_B1 = 0.9
ADAM_B2 = 0.999
ADAM_EPS = 1e-08
ADAM_WD = 0.01
ADAM_STEP = 10

N_CHIPS = 4
N_DEV = 8
VMEM_LIMIT_V7X = 56 * 1024 * 1024
MESH = pl.DeviceIdType.MESH

f32 = jnp.float32
bf16 = jnp.bfloat16


def _params(*sem):
    return pltpu.CompilerParams(dimension_semantics=sem, vmem_limit_bytes=VMEM_LIMIT_V7X)


def _dot(a, b):
    return jnp.dot(a, b, preferred_element_type=f32)


def _dot_nt(a, b):
    return lax.dot_general(a, b, (((1,), (1,)), ((), ())), preferred_element_type=f32)


def _dot_tn(a, b):
    return lax.dot_general(a, b, (((0,), (0,)), ((), ())), preferred_element_type=f32)


def _rope_partner(a, first_half):
    return jnp.where(first_half, pltpu.roll(a, 96, 1), pltpu.roll(a, 32, 1))


def _first_half_mask(rows):
    lane = lax.broadcasted_iota(jnp.int32, (rows, 128), 1)
    return (lane % HEAD_DIM) < (HEAD_DIM // 2)


def _stream_spec(d, ts):
    return pl.BlockSpec((d, ts // d, ATTN_WIDTH), lambda i: (0, i, 0))


def _stream_shape(S, d):
    return jax.ShapeDtypeStruct((d, S // d, ATTN_WIDTH), bf16)


N_STAGE = ATTN_WIDTH // 128


def _stage_scratch(ts):
    return [pltpu.VMEM((ts, 128), f32)] * N_STAGE


def _store_streams(stage, out_refs, ts):
    for d, ref in zip(DILATIONS, out_refs):
        for r in range(d):
            rows = pl.ds(0, ts) if d == 1 else pl.ds(r, ts // d, stride=d)
            for j in range(N_STAGE):
                ref[r, :, j * 128:(j + 1) * 128] = stage[j][rows, :].astype(bf16)


def _in_proj(x, g1, w_in, cos_t, sin_t):
    S = x.shape[0]
    ts = 512

    def body(x_ref, g_ref, w_ref, cos_ref, sin_ref, h_ref, u_ref, *rest):
        outs, stage = rest[:-N_STAGE], rest[-N_STAGE:]
        xv = x_ref[...]
        r = lax.rsqrt(jnp.mean(xv * xv, axis=-1, keepdims=True) + EPS)
        h = ((xv * r) * g_ref[...]).astype(bf16)
        h_ref[...] = h
        proj = _dot_nt(h, w_ref[...])
        u_ref[...] = proj[:, :POOL_WIDTH]
        cos = cos_ref[...]
        sin = sin_ref[...]
        first = _first_half_mask(ts)
        n_dil = len(DILATIONS)
        for which, base in enumerate((POOL_WIDTH, POOL_WIDTH + ATTN_WIDTH)):
            for j in range(ATTN_WIDTH // 128):
                a = proj[:, base + j * 128: base + (j + 1) * 128]
                if which == 0:
                    a = a * ATTN_SCALE
                stage[j][...] = a * cos + _rope_partner(a, first) * sin
            _store_streams(stage, outs[which * n_dil:(which + 1) * n_dil], ts)
        for j in range(ATTN_WIDTH // 128):
            base = POOL_WIDTH + 2 * ATTN_WIDTH + j * 128
            stage[j][...] = proj[:, base:base + 128]
        _store_streams(stage, outs[2 * n_dil:], ts)

    row = lambda w: pl.BlockSpec((ts, w), lambda i: (i, 0))
    streams = [_stream_spec(d, ts) for d in DILATIONS] * 3
    res = pl.pallas_call(
        body, name="in_proj", grid=(S // ts,),
        in_specs=[row(D_MODEL), pl.BlockSpec((1, D_MODEL), lambda i: (0, 0)),
                  pl.BlockSpec((IN_WIDTH, D_MODEL), lambda i: (0, 0)), row(128), row(128)],
        out_specs=[row(D_MODEL), row(POOL_WIDTH)] + streams,
        out_shape=[jax.ShapeDtypeStruct((S, D_MODEL), bf16), jax.ShapeDtypeStruct((S, POOL_WIDTH), f32)]
        + [_stream_shape(S, d) for d in DILATIONS] * 3,
        scratch_shapes=_stage_scratch(ts),
        compiler_params=_params("parallel"),
    )(x, g1, w_in, cos_t, sin_t)
    n = len(DILATIONS)
    return res[0], res[1], res[2:2 + n], res[2 + n:2 + 2 * n], res[2 + 2 * n:]


POOL_HALO = 16


def _pool_lane_group(rows):
    return lax.broadcasted_iota(jnp.int32, (rows, POOL_WIDTH), 1) // POOL_GROUP


def _pool_select(group, s2, s4, s8, s16):
    return jnp.where(group == 0, s2, jnp.where(group == 1, s4, jnp.where(group == 2, s8, s16)))


def _pool_count(t0, rows):
    group = _pool_lane_group(rows)
    t = t0 + lax.broadcasted_iota(jnp.int32, (rows, POOL_WIDTH), 0)
    win = _pool_select(group, 2, 4, 8, 16)
    return jnp.minimum(t + 1, win).astype(f32)


def _pool_diff(u_halo, u_tile, t0):
    ts = u_tile.shape[0]
    ext = jnp.concatenate([u_halo, u_tile], axis=0)
    s2 = ext + pltpu.roll(ext, 1, 0)
    s4 = s2 + pltpu.roll(s2, 2, 0)
    s8 = s4 + pltpu.roll(s4, 4, 0)
    s16 = s8 + pltpu.roll(s8, 8, 0)
    group = _pool_lane_group(ts + POOL_HALO)
    wsum = _pool_select(group, s2, s4, s8, s16)[POOL_HALO:]
    return wsum / _pool_count(t0, ts) - u_tile


def _pool_specs(ts, n_tiles):
    tile = pl.BlockSpec((ts, POOL_WIDTH), lambda i: (i, 0))
    per = ts // POOL_HALO
    before = pl.BlockSpec((POOL_HALO, POOL_WIDTH), lambda i: (jnp.maximum(i * per - 1, 0), 0))
    after = pl.BlockSpec((POOL_HALO, POOL_WIDTH), lambda i: (jnp.minimum((i + 1) * per, n_tiles * per - 1), 0))
    return tile, before, after


def _pool_fwd(u, w_bd, scale):
    S = u.shape[0]
    ts = 2048
    n_tiles = S // ts

    def body(u_ref, halo_ref, w_ref, sc_ref, y_ref):
        i = pl.program_id(0)
        halo = jnp.where(i > 0, halo_ref[...], 0.0)
        d = _pool_diff(halo, u_ref[...], i * ts)
        y_ref[...] = (_dot(d.astype(bf16), w_ref[...]) * sc_ref[...]).astype(bf16)

    tile, before, _ = _pool_specs(ts, n_tiles)
    return pl.pallas_call(
        body, name="pool_fwd", grid=(n_tiles,),
        in_specs=[tile, before, pl.BlockSpec((POOL_WIDTH, POOL_WIDTH), lambda i: (0, 0)),
                  pl.BlockSpec((1, POOL_WIDTH), lambda i: (0, 0))],
        out_specs=tile, out_shape=jax.ShapeDtypeStruct((S, POOL_WIDTH), bf16),
        compiler_params=_params("parallel"),
    )(u, u, w_bd, scale)


def _pool_bwd(u, dy, w_bd, scale):
    S = u.shape[0]
    ts = 2048
    n_tiles = S // ts

    def body(u_ref, halo_ref, dy_ref, dy_next_ref, w_ref, sc_ref, du_ref, dw_ref, dsc_ref):
        i = pl.program_id(0)

        @pl.when(i == 0)
        def _():
            dw_ref[...] = jnp.zeros_like(dw_ref)
            dsc_ref[...] = jnp.zeros_like(dsc_ref)

        halo = jnp.where(i > 0, halo_ref[...], 0.0)
        d = _pool_diff(halo, u_ref[...], i * ts).astype(bf16)
        w = w_ref[...]
        sc = sc_ref[...]
        dy_tile = dy_ref[...]
        z = _dot(d, w)
        dsc_ref[...] += jnp.sum(dy_tile * z, axis=0, keepdims=True)
        dy_next = jnp.where(i < n_tiles - 1, dy_next_ref[...], 0.0)
        dz = (jnp.concatenate([dy_tile, dy_next], axis=0) * sc).astype(bf16)
        dw_ref[...] += _dot_tn(d, dz[:ts])
        dd = _dot_nt(dz, w)
        e = dd / _pool_count(i * ts, ts + POOL_HALO)
        n = ts + POOL_HALO
        f2 = e + pltpu.roll(e, n - 1, 0)
        f4 = f2 + pltpu.roll(f2, n - 2, 0)
        f8 = f4 + pltpu.roll(f4, n - 4, 0)
        f16 = f8 + pltpu.roll(f8, n - 8, 0)
        fsum = _pool_select(_pool_lane_group(n), f2, f4, f8, f16)
        du_ref[...] = (fsum[:ts] - dd[:ts]).astype(bf16)

    tile, before, after = _pool_specs(ts, n_tiles)
    return pl.pallas_call(
        body, name="pool_bwd", grid=(n_tiles,),
        in_specs=[tile, before, tile, after, pl.BlockSpec((POOL_WIDTH, POOL_WIDTH), lambda i: (0, 0)),
                  pl.BlockSpec((1, POOL_WIDTH), lambda i: (0, 0))],
        out_specs=[tile, pl.BlockSpec((POOL_WIDTH, POOL_WIDTH), lambda i: (0, 0)),
                   pl.BlockSpec((1, POOL_WIDTH), lambda i: (0, 0))],
        out_shape=[jax.ShapeDtypeStruct((S, POOL_WIDTH), bf16), jax.ShapeDtypeStruct((POOL_WIDTH, POOL_WIDTH), f32),
                   jax.ShapeDtypeStruct((1, POOL_WIDTH), f32)],
        compiler_params=_params("arbitrary"),
    )(u, u, dy, dy, w_bd, scale)


SUPER = BLOCK * DILATIONS[-1]
UNITS = SUPER // BLOCK
FWD_UNROLL = 16
BWD_UNROLL = 8


def _band_mask(has_prev):
    qi = lax.broadcasted_iota(jnp.int32, (BLOCK, 2 * BLOCK), 0)
    kj = lax.broadcasted_iota(jnp.int32, (BLOCK, 2 * BLOCK), 1)
    return (kj >= qi) & (kj <= qi + BLOCK) & ((kj >= BLOCK) | has_prev)


def _head0_mask(rows=BLOCK):
    return lax.broadcasted_iota(jnp.int32, (rows, 128), 1) < HEAD_DIM


def _band_mask_t(has_prev):
    ki = lax.broadcasted_iota(jnp.int32, (2 * BLOCK, 2 * BLOCK), 0)
    qj = lax.broadcasted_iota(jnp.int32, (2 * BLOCK, 2 * BLOCK), 1) % BLOCK
    return (ki >= qj) & (ki <= qj + BLOCK) & ((ki >= BLOCK) | has_prev)


def _head_pair_rows(a, h0):
    zero = jnp.zeros_like(a)
    return jnp.concatenate([jnp.where(h0, a, zero), jnp.where(h0, zero, a)], axis=0)


def _per_query_row(stat):
    t = stat.T
    return jnp.concatenate([t[0:1], t[HEAD_DIM:HEAD_DIM + 1]], axis=1)


def _natural_rows(d, r, n):
    if d == 1:
        return pl.ds(pl.multiple_of(n * BLOCK, BLOCK), BLOCK)
    return pl.ds(n * (BLOCK * d) + r, BLOCK, stride=d)


def _unit_place(d, u):
    per_stream = UNITS // d
    return u // per_stream, u % per_stream, per_stream


def _block_rows(n):
    return pl.ds(pl.multiple_of(n * BLOCK, BLOCK), BLOCK)


def _band(cur_ref, tail_ref, r, n):
    before = jnp.where(n > 0, cur_ref[r, _block_rows(jnp.maximum(n - 1, 0)), :], tail_ref[r])
    return jnp.concatenate([before, cur_ref[r, _block_rows(n), :]], axis=0)


def _attn_in_specs(S, with_do):
    specs = []
    last = S // SUPER - 1
    for d in DILATIONS:
        per_stream = UNITS // d
        cur = pl.BlockSpec((d, SUPER // d, 128), lambda hp, sb: (0, jnp.minimum(sb, last), hp))
        tail = pl.BlockSpec(
            (d, BLOCK, 128),
            lambda hp, sb, per_stream=per_stream: (0, jnp.maximum(jnp.minimum(sb, last) * per_stream - 1, 0), hp))
        specs += [cur] * (2 if with_do else 1) + [cur, tail, cur, tail]
    return specs


def _attn_fwd(qs, ks, vs, pack):
    S = qs[0].shape[1]
    n_dil = len(DILATIONS)
    n_steps = S // SUPER
    n_total = (ATTN_WIDTH // 128) * n_steps

    def body(*refs):
        ins, pack_ref = refs[:5 * n_dil], refs[5 * n_dil]
        out_ref, lse_ref, gathered_ref = refs[5 * n_dil + 1:5 * n_dil + 4]
        scratch = refs[5 * n_dil + 4:]
        o_sc, l_sc = scratch[:n_dil], scratch[n_dil:2 * n_dil]
        gather = _Gather(pack_ref, gathered_ref, *scratch[2 * n_dil:])
        sb = pl.program_id(1)
        step = pl.program_id(0) * n_steps + sb

        @pl.when(step == 0)
        def _():
            gather.start()

        h0 = _head0_mask()
        for ci, d in enumerate(DILATIONS):
            q_ref, kc_ref, kp_ref, vc_ref, vp_ref = ins[5 * ci:5 * ci + 5]

            def unit(u, carry, d=d, ci=ci, q_ref=q_ref, kc_ref=kc_ref, kp_ref=kp_ref, vc_ref=vc_ref, vp_ref=vp_ref):
                r, n, _ = _unit_place(d, u)
                qv = q_ref[r, _block_rows(n), :]
                kb = _band(kc_ref, kp_ref, r, n)
                vb = _band(vc_ref, vp_ref, r, n)
                valid = _band_mask((sb > 0) | (n > 0))
                s = jnp.where(jnp.concatenate([valid, valid], axis=0), _dot_nt(_head_pair_rows(qv, h0), kb), NEG)
                m = jnp.max(s, axis=1, keepdims=True)
                e = jnp.exp(s - m)
                den = jnp.sum(e, axis=1, keepdims=True)
                o_pair = _dot(e.astype(bf16), vb) * (1.0 / den)
                lse_pair = jnp.broadcast_to(m + jnp.log(den), (2 * BLOCK, 128))
                rows = _natural_rows(d, r, n)
                o_sc[ci][rows, :] = jnp.where(h0, o_pair[:BLOCK], o_pair[BLOCK:])
                l_sc[ci][rows, :] = jnp.where(h0, lse_pair[:BLOCK], lse_pair[BLOCK:])
                return carry

            lax.fori_loop(0, UNITS, unit, 0, unroll=FWD_UNROLL)

        def merge(t, carry):
            rows = pl.ds(pl.multiple_of(t * 256, 256), 256)
            a, b, c = l_sc[0][rows, :], l_sc[1][rows, :], l_sc[2][rows, :]
            m = jnp.maximum(jnp.maximum(a, b), c)
            ea, eb, ec = jnp.exp(a - m), jnp.exp(b - m), jnp.exp(c - m)
            tot = ea + eb + ec
            out_ref[rows, :] = ((ea / tot) * o_sc[0][rows, :] + (eb / tot) * o_sc[1][rows, :]
                                + (ec / tot) * o_sc[2][rows, :]).astype(bf16)
            lse_ref[rows, :] = m + jnp.log(tot)
            return carry

        lax.fori_loop(0, SUPER // 256, merge, 0)

        @pl.when(step == (2 * n_total) // 3)
        def _():
            gather.pass_on()

        @pl.when(step == n_total - 1)
        def _():
            gather.finish()

    args = []
    for q, k, v in zip(qs, ks, vs):
        args += [q, k, k, v, v]
    nat = pl.BlockSpec((SUPER, 128), lambda hp, sb: (sb, hp))
    rows = pack.shape[0]
    return pl.pallas_call(
        body, name="attn_fwd", grid=(ATTN_WIDTH // 128, n_steps),
        in_specs=_attn_in_specs(S, False) + [ANY], out_specs=[nat, nat, ANY],
        out_shape=[jax.ShapeDtypeStruct((S, ATTN_WIDTH), bf16), jax.ShapeDtypeStruct((S, ATTN_WIDTH), f32),
                   _Gather.out_shape(rows, pack.dtype)],
        scratch_shapes=[pltpu.VMEM((SUPER, 128), f32)] * (2 * n_dil) + _Gather.scratch(rows, pack.dtype),
        compiler_params=_params("arbitrary", "arbitrary"),
    )(*args, pack)


def _attn_bwd(qs, ks, vs, dos, lse, delta, chip_sum):
    S = qs[0].shape[1]
    n_steps = S // SUPER
    last = n_steps - 1
    n_dil = len(DILATIONS)
    n_total = (ATTN_WIDTH // 128) * (n_steps + 1)

    def body(*refs):
        ins, (lse_ref, dl_ref, sum_ref) = refs[:6 * n_dil], refs[6 * n_dil:6 * n_dil + 3]
        dq_ref, dk_ref, dv_ref, others_ref = refs[6 * n_dil + 3:6 * n_dil + 7]
        dq_acc, dk_acc, dv_acc = refs[6 * n_dil + 7:6 * n_dil + 10]
        scatter = _Scatter(sum_ref, others_ref, *refs[6 * n_dil + 10:])
        sb = pl.program_id(1)
        step = pl.program_id(0) * (n_steps + 1) + sb
        cur = sb % 2
        prv = 1 - cur

        @pl.when(step == 0)
        def _():
            scatter.start()

        @pl.when(sb < n_steps)
        def _():
            dq_acc[...] = jnp.zeros_like(dq_acc)
            dk_acc[cur] = jnp.zeros((SUPER, 128), f32)
            dv_acc[cur] = jnp.zeros((SUPER, 128), f32)
            h0 = _head0_mask()
            for ci, d in enumerate(DILATIONS):
                q_ref, do_ref, kc_ref, kp_ref, vc_ref, vp_ref = ins[6 * ci:6 * ci + 6]

                def unit(u, carry, d=d, q_ref=q_ref, do_ref=do_ref, kc_ref=kc_ref, kp_ref=kp_ref, vc_ref=vc_ref,
                         vp_ref=vp_ref):
                    r, n, per_stream = _unit_place(d, u)
                    qv = q_ref[r, _block_rows(n), :]
                    dov = do_ref[r, _block_rows(n), :]
                    kb = _band(kc_ref, kp_ref, r, n)
                    vb = _band(vc_ref, vp_ref, r, n)
                    rows = _natural_rows(d, r, n)
                    has_prev = (sb > 0) | (n > 0)
                    q_pair = _head_pair_rows(qv, h0)
                    do_pair = _head_pair_rows(dov, h0)
                    s_t = jnp.where(_band_mask_t(has_prev), _dot_nt(kb, q_pair), NEG)
                    p_t = jnp.exp(s_t - _per_query_row(lse_ref[rows, :]))
                    dp_t = _dot_nt(vb, do_pair)
                    ds_t = (p_t * (dp_t - _per_query_row(dl_ref[rows, :]))).astype(bf16)
                    dvb = _dot(p_t.astype(bf16), do_pair)
                    dkb = _dot(ds_t, q_pair)
                    dq_pair = _dot_tn(ds_t, kb)
                    dq_acc[rows, :] += jnp.where(h0, dq_pair[:BLOCK], dq_pair[BLOCK:])
                    dk_acc[cur, rows, :] += dkb[BLOCK:]
                    dv_acc[cur, rows, :] += dvb[BLOCK:]

                    slot = jnp.where((n > 0) | (sb == 0), cur, prv)
                    before = _natural_rows(d, r, jnp.where(n > 0, n - 1, per_stream - 1))
                    dk_acc[slot, before, :] += dkb[:BLOCK]
                    dv_acc[slot, before, :] += dvb[:BLOCK]
                    return carry

                lax.fori_loop(0, UNITS, unit, 0, unroll=BWD_UNROLL)
            dq_ref[...] = (dq_acc[...] * ATTN_SCALE).astype(bf16)

        @pl.when(sb > 0)
        def _():
            dk_ref[...] = dk_acc[prv].astype(bf16)
            dv_ref[...] = dv_acc[prv].astype(bf16)

        @pl.when(step == n_total - 1)
        def _():
            scatter.finish()

    args = []
    for q, k, v, do in zip(qs, ks, vs, dos):
        args += [q, do, k, k, v, v]
    nat = pl.BlockSpec((SUPER, 128), lambda hp, sb: (jnp.minimum(sb, last), hp))
    nat_before = pl.BlockSpec((SUPER, 128), lambda hp, sb: (jnp.clip(sb - 1, 0, last), hp))
    out = jax.ShapeDtypeStruct((S, ATTN_WIDTH), bf16)
    half = chip_sum.shape[1]
    return pl.pallas_call(
        body, name="attn_bwd", grid=(ATTN_WIDTH // 128, n_steps + 1),
        in_specs=_attn_in_specs(S, True) + [nat, nat, ANY], out_specs=[nat, nat_before, nat_before, ANY],
        out_shape=[out, out, out, _Scatter.out_shape(half, chip_sum.dtype)],
        scratch_shapes=[pltpu.VMEM((SUPER, 128), f32), pltpu.VMEM((2, SUPER, 128), f32),
                        pltpu.VMEM((2, SUPER, 128), f32)] + _Scatter.scratch(half),
        compiler_params=_params("arbitrary", "arbitrary"),
    )(*args, lse, delta, chip_sum)


def _rms(v):
    return lax.rsqrt(jnp.mean(v * v, axis=-1, keepdims=True) + EPS)


def _out_proj(pool_out, attn_out, w_out, x, g2, g3):
    S = x.shape[0]
    ts = 512

    def body(p_ref, a_ref, w_ref, x_ref, g2_ref, g3_ref, mix_ref, x2_ref, h2_ref):
        mix = _dot(p_ref[...], w_ref[:POOL_WIDTH, :]) + _dot(a_ref[...], w_ref[POOL_WIDTH:, :])
        mix_ref[...] = mix
        x2 = x_ref[...] + (mix * _rms(mix)) * g2_ref[...]
        x2_ref[...] = x2
        h2_ref[...] = ((x2 * _rms(x2)) * g3_ref[...]).astype(bf16)

    row = lambda w: pl.BlockSpec((ts, w), lambda i: (i, 0))
    gain = pl.BlockSpec((1, D_MODEL), lambda i: (0, 0))
    return pl.pallas_call(
        body, name="out_proj", grid=(S // ts,),
        in_specs=[row(POOL_WIDTH), row(ATTN_WIDTH), pl.BlockSpec((D_MODEL, D_MODEL), lambda i: (0, 0)),
                  row(D_MODEL), gain, gain],
        out_specs=[row(D_MODEL)] * 3,
        out_shape=[jax.ShapeDtypeStruct((S, D_MODEL), f32), jax.ShapeDtypeStruct((S, D_MODEL), f32),
                   jax.ShapeDtypeStruct((S, D_MODEL), bf16)],
        compiler_params=_params("parallel"),
    )(pool_out, attn_out, w_out, x, g2, g3)


FF_TILE = 256
FF_STEP_ROWS = 2048
FF_ROWS = 512
FF_BWD_ROWS = 256


def _sigmoid(g):
    return 1.0 / (1.0 + jnp.exp(-g))


def _ff_act_shape(S):
    return jax.ShapeDtypeStruct((D_FF // FF_TILE, S, FF_TILE), bf16)


def _ff_act_spec(ts):
    return pl.BlockSpec((1, ts, FF_TILE), lambda i, j: (j, i, 0))


def _ffn_fwd(h2, w_gate, w_up, w_down):
    S = h2.shape[0]
    ts = min(S, FF_STEP_ROWS)

    def body(h_ref, wg_ref, wu_ref, wd_ref, gate_ref, up_ref, f_ref):
        def rows_pass(first):
            def sub(i, carry):
                rows = pl.ds(pl.multiple_of(i * FF_ROWS, FF_ROWS), FF_ROWS)
                h = h_ref[rows, :]
                gate = _dot_nt(h, wg_ref[...])
                up = _dot_nt(h, wu_ref[...])
                gate_ref[0, rows, :] = gate.astype(bf16)
                up_ref[0, rows, :] = up.astype(bf16)
                part = _dot((gate * _sigmoid(gate) * up).astype(bf16), wd_ref[...])
                if first:
                    f_ref[rows, :] = part
                else:
                    f_ref[rows, :] += part
                return carry

            lax.fori_loop(0, ts // FF_ROWS, sub, 0, unroll=True)

        @pl.when(pl.program_id(1) == 0)
        def _():
            rows_pass(True)

        @pl.when(pl.program_id(1) > 0)
        def _():
            rows_pass(False)

    act = _ff_act_spec(ts)
    weight = pl.BlockSpec((FF_TILE, D_MODEL), lambda i, j: (j, 0))
    return pl.pallas_call(
        body, name="ffn_fwd", grid=(S // ts, D_FF // FF_TILE),
        in_specs=[pl.BlockSpec((ts, D_MODEL), lambda i, j: (i, 0)), weight, weight, weight],
        out_specs=[act, act, pl.BlockSpec((ts, D_MODEL), lambda i, j: (i, 0))],
        out_shape=[_ff_act_shape(S), _ff_act_shape(S), jax.ShapeDtypeStruct((S, D_MODEL), f32)],
        compiler_params=_params("parallel", "arbitrary"),
    )(h2, w_gate, w_up, w_down)


def _loss_head(f, x2, target, g4):
    S = f.shape[0]
    ts = 512

    def body(f_ref, x2_ref, t_ref, g_ref, dy_ref, df_ref, dg_ref, loss_ref):
        @pl.when(pl.program_id(0) == 0)
        def _():
            dg_ref[...] = jnp.zeros_like(dg_ref)
            loss_ref[...] = jnp.zeros_like(loss_ref)

        fv = f_ref[...]
        g = g_ref[...]
        r = _rms(fv)
        fhat = fv * r
        err = (x2_ref[...] + fhat * g) - t_ref[...]
        loss_ref[...] += 0.5 * jnp.sum(jnp.mean(err * err, axis=-1, keepdims=True), axis=0, keepdims=True)
        dy = err * (1.0 / D_MODEL)
        dy_ref[...] = dy
        dg_ref[...] += jnp.sum(dy * fhat, axis=0, keepdims=True)
        dyg = dy * g
        df_ref[...] = (r * (dyg - fhat * jnp.mean(dyg * fhat, axis=-1, keepdims=True))).astype(bf16)

    row = pl.BlockSpec((ts, D_MODEL), lambda i: (i, 0))
    gain = pl.BlockSpec((1, D_MODEL), lambda i: (0, 0))
    return pl.pallas_call(
        body, name="loss_head", grid=(S // ts,), in_specs=[row, row, row, gain],
        out_specs=[row, row, gain, pl.BlockSpec((1, 1), lambda i: (0, 0))],
        out_shape=[jax.ShapeDtypeStruct((S, D_MODEL), f32), jax.ShapeDtypeStruct((S, D_MODEL), bf16),
                   jax.ShapeDtypeStruct((1, D_MODEL), f32), jax.ShapeDtypeStruct((1, 1), f32)],
        compiler_params=_params("arbitrary"),
    )(f, x2, target, g4)


def _ffn_bwd(df, gate, up, w_gate, w_up, w_down):
    S = df.shape[0]
    ts = min(S, FF_STEP_ROWS)

    def body(df_ref, gate_ref, up_ref, wg_ref, wu_ref, wd_ref, a_ref, dgate_ref, dup_ref, dh_ref):
        def rows_pass(first):
            def sub(i, carry):
                rows = pl.ds(pl.multiple_of(i * FF_BWD_ROWS, FF_BWD_ROWS), FF_BWD_ROWS)
                da = _dot_nt(df_ref[rows, :], wd_ref[...])
                g = gate_ref[0, rows, :].astype(f32)
                u = up_ref[0, rows, :].astype(f32)
                sig = _sigmoid(g)
                silu = g * sig
                a_ref[0, rows, :] = (silu * u).astype(bf16)
                dup = (da * silu).astype(bf16)
                dgate = (da * u * (sig * (1.0 + g * (1.0 - sig)))).astype(bf16)
                dup_ref[0, rows, :] = dup
                dgate_ref[0, rows, :] = dgate
                part = _dot(dgate, wg_ref[...]) + _dot(dup, wu_ref[...])
                if first:
                    dh_ref[rows, :] = part
                else:
                    dh_ref[rows, :] += part
                return carry

            lax.fori_loop(0, ts // FF_BWD_ROWS, sub, 0, unroll=True)

        @pl.when(pl.program_id(1) == 0)
        def _():
            rows_pass(True)

        @pl.when(pl.program_id(1) > 0)
        def _():
            rows_pass(False)

    act = _ff_act_spec(ts)
    row = pl.BlockSpec((ts, D_MODEL), lambda i, j: (i, 0))
    return pl.pallas_call(
        body, name="ffn_bwd", grid=(S // ts, D_FF // FF_TILE),
        in_specs=[row, act, act,
                  pl.BlockSpec((FF_TILE, D_MODEL), lambda i, j: (j, 0)),
                  pl.BlockSpec((FF_TILE, D_MODEL), lambda i, j: (j, 0)),
                  pl.BlockSpec((FF_TILE, D_MODEL), lambda i, j: (j, 0))],
        out_specs=[act, act, act, row],
        out_shape=[_ff_act_shape(S)] * 3 + [jax.ShapeDtypeStruct((S, D_MODEL), f32)],
        compiler_params=_params("parallel", "arbitrary"),
    )(df, gate, up, w_gate, w_up, w_down)


def _norm_bwd(dh2, dy, x2, mix, g3, g2):
    S = dh2.shape[0]
    ts = 512

    def body(dh_ref, dy_ref, x2_ref, mix_ref, g3_ref, g2_ref, dx2_ref, dmix_ref, dg3_ref, dg2_ref):
        @pl.when(pl.program_id(0) == 0)
        def _():
            dg3_ref[...] = jnp.zeros_like(dg3_ref)
            dg2_ref[...] = jnp.zeros_like(dg2_ref)

        dh = dh_ref[...]
        x2 = x2_ref[...]
        r3 = _rms(x2)
        xhat = x2 * r3
        dg3_ref[...] += jnp.sum(dh * xhat, axis=0, keepdims=True)
        dhg = dh * g3_ref[...]
        dx2 = dy_ref[...] + r3 * (dhg - xhat * jnp.mean(dhg * xhat, axis=-1, keepdims=True))
        dx2_ref[...] = dx2
        mix = mix_ref[...]
        r2 = _rms(mix)
        mhat = mix * r2
        dg2_ref[...] += jnp.sum(dx2 * mhat, axis=0, keepdims=True)
        dmg = dx2 * g2_ref[...]
        dmix_ref[...] = (r2 * (dmg - mhat * jnp.mean(dmg * mhat, axis=-1, keepdims=True))).astype(bf16)

    row = pl.BlockSpec((ts, D_MODEL), lambda i: (i, 0))
    gain = pl.BlockSpec((1, D_MODEL), lambda i: (0, 0))
    return pl.pallas_call(
        body, name="norm_bwd", grid=(S // ts,), in_specs=[row, row, row, row, gain, gain],
        out_specs=[row, row, gain, gain],
        out_shape=[jax.ShapeDtypeStruct((S, D_MODEL), f32), jax.ShapeDtypeStruct((S, D_MODEL), bf16),
                   jax.ShapeDtypeStruct((1, D_MODEL), f32), jax.ShapeDtypeStruct((1, D_MODEL), f32)],
        compiler_params=_params("arbitrary"),
    )(dh2, dy, x2, mix, g3, g2)


def _out_proj_bwd(dmix, w_out, attn_out, head_ones, grads):
    S = dmix.shape[0]
    ts = 512
    n_dil = len(DILATIONS)

    def body(dm_ref, w_ref, o_ref, ones_ref, g_ref, dp_ref, dl_ref, *rest):
        do_refs, theirs_ref = rest[:n_dil], rest[n_dil]
        stage = rest[n_dil + 1:n_dil + 1 + N_STAGE]
        swap = _Swap(g_ref, theirs_ref, *rest[n_dil + 1 + N_STAGE:])

        @pl.when(pl.program_id(0) == 0)
        def _():
            swap.start()

        @pl.when(pl.program_id(0) == S // ts - 1)
        def _():
            swap.finish()

        dcat = _dot_nt(dm_ref[...], w_ref[...])
        dp_ref[...] = dcat[:, :POOL_WIDTH]
        do = dcat[:, POOL_WIDTH:]
        for j in range(ATTN_WIDTH // 128):
            stage[j][...] = do[:, j * 128:(j + 1) * 128]
        _store_streams(stage, do_refs, ts)
        prod = do * o_ref[...].astype(f32)
        hi = prod.astype(bf16)
        lo = (prod - hi.astype(f32)).astype(bf16)
        ones = ones_ref[...]
        for j in range(ATTN_WIDTH // 128):
            cols = slice(j * 128, (j + 1) * 128)
            dl_ref[:, cols] = _dot(hi[:, cols], ones) + _dot(lo[:, cols], ones)

    row = lambda w: pl.BlockSpec((ts, w), lambda i: (i, 0))
    res = pl.pallas_call(
        body, name="out_proj_bwd", grid=(S // ts,),
        in_specs=[row(D_MODEL), pl.BlockSpec((D_MODEL, D_MODEL), lambda i: (0, 0)), row(ATTN_WIDTH),
                  pl.BlockSpec((128, 128), lambda i: (0, 0)), ANY],
        out_specs=[row(POOL_WIDTH), row(ATTN_WIDTH)] + [_stream_spec(d, ts) for d in DILATIONS] + [ANY],
        out_shape=[jax.ShapeDtypeStruct((S, POOL_WIDTH), f32), jax.ShapeDtypeStruct((S, ATTN_WIDTH), f32)]
        + [_stream_shape(S, d) for d in DILATIONS] + [_Swap.out_shape(grads)],
        scratch_shapes=_stage_scratch(ts) + _Swap.scratch(grads),
        compiler_params=_params("arbitrary"),
    )(dmix, w_out, attn_out, head_ones, grads)
    return res[0], res[1], res[2:2 + n_dil], res[2 + n_dil]


IN_BWD_ROWS = 256


def _in_proj_bwd(du, dq, dk, dv, cos_t, sin_t, w_in, x, dx2, g1):
    S = x.shape[0]
    ts = 512

    def body(du_ref, dq_ref, dk_ref, dv_ref, cos_ref, sin_ref, w_ref, x_ref, dx2_ref, g_ref, gx_ref, dproj_ref, dg_ref):
        @pl.when(pl.program_id(0) == 0)
        def _():
            dg_ref[...] = jnp.zeros_like(dg_ref)

        first = _first_half_mask(IN_BWD_ROWS)

        def sub(i, carry):
            rows = pl.ds(pl.multiple_of(i * IN_BWD_ROWS, IN_BWD_ROWS), IN_BWD_ROWS)
            dproj_ref[rows, :POOL_WIDTH] = du_ref[rows, :]
            cos = cos_ref[rows, :]
            sin = sin_ref[rows, :]
            for j in range(ATTN_WIDTH // 128):
                cols = slice(j * 128, (j + 1) * 128)
                for base, ref in ((POOL_WIDTH, dq_ref), (POOL_WIDTH + ATTN_WIDTH, dk_ref)):
                    g = ref[rows, cols].astype(f32)
                    pre = g * cos + _rope_partner(g * sin, first)
                    dproj_ref[rows, base + j * 128: base + (j + 1) * 128] = pre.astype(bf16)
            dproj_ref[rows, POOL_WIDTH + 2 * ATTN_WIDTH:] = dv_ref[rows, :]

            dh = _dot(dproj_ref[rows, :], w_ref[...])
            xv = x_ref[rows, :]
            r = _rms(xv)
            xhat = xv * r
            dg_ref[...] += jnp.sum(dh * xhat, axis=0, keepdims=True)
            dhg = dh * g_ref[...]
            gx_ref[rows, :] = dx2_ref[rows, :] + r * (dhg - xhat * jnp.mean(dhg * xhat, axis=-1, keepdims=True))
            return carry

        lax.fori_loop(0, ts // IN_BWD_ROWS, sub, 0, unroll=True)

    row = lambda w: pl.BlockSpec((ts, w), lambda i: (i, 0))
    gain = pl.BlockSpec((1, D_MODEL), lambda i: (0, 0))
    return pl.pallas_call(
        body, name="in_proj_bwd", grid=(S // ts,),
        in_specs=[row(POOL_WIDTH)] + [row(ATTN_WIDTH)] * 3 + [row(128), row(128),
                  pl.BlockSpec((IN_WIDTH, D_MODEL), lambda i: (0, 0)), row(D_MODEL), row(D_MODEL), gain],
        out_specs=[row(D_MODEL), row(IN_WIDTH), gain],
        out_shape=[jax.ShapeDtypeStruct((S, D_MODEL), f32), jax.ShapeDtypeStruct((S, IN_WIDTH), bf16),
                   jax.ShapeDtypeStruct((1, D_MODEL), f32)],
        compiler_params=_params("arbitrary"),
    )(du, dq, dk, dv, cos_t, sin_t, w_in, x, dx2, g1)


def _matmul_tiles_tn(a, b, name):
    T, K, w = a.shape
    N = b.shape[1]
    tk = 1024

    def body(a_ref, b_ref, o_ref):
        def tiles_pass(first):
            for t in range(T):
                part = _dot_tn(a_ref[t], b_ref[...])
                if first:
                    o_ref[t * w:(t + 1) * w, :] = part
                else:
                    o_ref[t * w:(t + 1) * w, :] += part

        @pl.when(pl.program_id(0) == 0)
        def _():
            tiles_pass(True)

        @pl.when(pl.program_id(0) > 0)
        def _():
            tiles_pass(False)

    return pl.pallas_call(
        body, name=name, grid=(K // tk,),
        in_specs=[pl.BlockSpec((T, tk, w), lambda k: (0, k, 0)), pl.BlockSpec((tk, N), lambda k: (k, 0))],
        out_specs=pl.BlockSpec((T * w, N), lambda k: (0, 0)),
        out_shape=jax.ShapeDtypeStruct((T * w, N), f32),
        compiler_params=_params("arbitrary"),
    )(a, b)


def _matmul_tn(a, b, name):
    K, M = a.shape
    N = b.shape[1]
    tk = 1024

    def body(a_ref, b_ref, o_ref):
        _tn_step(a_ref, b_ref, o_ref, M)

    return pl.pallas_call(
        body, name=name, grid=(K // tk,),
        in_specs=[pl.BlockSpec((tk, M), lambda k: (k, 0)), pl.BlockSpec((tk, N), lambda k: (k, 0))],
        out_specs=pl.BlockSpec((M, N), lambda k: (0, 0)),
        out_shape=jax.ShapeDtypeStruct((M, N), f32),
        compiler_params=_params("arbitrary"),
    )(a, b)


def _tn_step(a_ref, b_ref, o_ref, M):
    w = 256

    def tiles_pass(first):
        for t in range(M // w):
            part = _dot_tn(a_ref[:, t * w:(t + 1) * w], b_ref[...])
            if first:
                o_ref[t * w:(t + 1) * w, :] = part
            else:
                o_ref[t * w:(t + 1) * w, :] += part

    @pl.when(pl.program_id(0) == 0)
    def _():
        tiles_pass(True)

    @pl.when(pl.program_id(0) > 0)
    def _():
        tiles_pass(False)


def _matmul_tn_and_small_sum(a, b, block, name):
    K, M = a.shape
    N = b.shape[1]
    tk = 1024
    n_steps = K // tk

    def body(a_ref, b_ref, block_ref, o_ref, total_ref, *scratch):
        small = _SmallSum(block_ref, *scratch)

        @pl.when(pl.program_id(0) == 0)
        def _():
            small.start()

        _tn_step(a_ref, b_ref, o_ref, M)

        @pl.when(pl.program_id(0) == n_steps - 1)
        def _():
            small.finish(total_ref)

    return pl.pallas_call(
        body, name=name, grid=(n_steps,),
        in_specs=[pl.BlockSpec((tk, M), lambda k: (k, 0)), pl.BlockSpec((tk, N), lambda k: (k, 0)), ANY],
        out_specs=[pl.BlockSpec((M, N), lambda k: (0, 0)), pl.BlockSpec(block.shape, lambda k: (0, 0))],
        out_shape=[jax.ShapeDtypeStruct((M, N), f32), jax.ShapeDtypeStruct(block.shape, block.dtype)],
        scratch_shapes=_SmallSum.scratch(block),
        compiler_params=_params("arbitrary"),
    )(a, b, block)


def _rope_tables(S):
    half = HEAD_DIM // 2
    freqs = ROPE_THETA ** (-jnp.arange(half, dtype=f32) * (2.0 / HEAD_DIM))
    ang = jnp.arange(S).astype(f32)[:, None] * freqs[None, :]
    cos = jnp.tile(jnp.cos(ang), (1, 4))
    sin = jnp.sin(ang)
    sin = jnp.tile(jnp.concatenate([-sin, sin], axis=1), (1, 2))
    return cos, sin


def _block_diag(w_pool):
    w = jnp.zeros((POOL_WIDTH, POOL_WIDTH), w_pool.dtype)
    for g in range(POOL_WIDTH // POOL_GROUP):
        w = lax.dynamic_update_slice(w, w_pool[g], (g * POOL_GROUP, g * POOL_GROUP))
    return w


def _head_ones():
    head = np.arange(128) // HEAD_DIM
    return jnp.asarray(head[:, None] == head[None, :], dtype=bf16)


def _place():
    x, y, c = lax.axis_index("x"), lax.axis_index("y"), lax.axis_index("c")
    chips = [(1 - x, y), (x, 1 - y), (1 - x, 1 - y)]
    return x, y, c, chips


ANY = pl.BlockSpec(memory_space=pl.ANY)
N_PEER_CHIPS = N_CHIPS - 1
ICI_PIECES = 4
D2D_PIECES = 8
LOCAL_PIECES = 8


def _row_chunks(rows, n, unit=32):
    units = rows // unit
    out, start = [], 0
    for i in range(n):
        size = (units // n + (1 if i < units % n else 0)) * unit
        out.append((start, size))
        start += size
    return [piece for piece in out if piece[1]]


class _LocalCopy:
    def __init__(self, src_rows, dst_rows, rows, buf, sems_in, sems_out):
        self.loads, self.stores = [], []
        for i, (start, size) in enumerate(_row_chunks(rows, LOCAL_PIECES)):
            r = pl.ds(start, size)
            self.loads.append(pltpu.make_async_copy(src_rows(r), buf.at[r], sems_in.at[i]))
            self.stores.append(pltpu.make_async_copy(buf.at[r], dst_rows(r), sems_out.at[i]))

    def start(self):
        for cp in self.loads:
            cp.start()

    def pass_on(self):
        for load, store in zip(self.loads, self.stores):
            load.wait()
            store.start()

    def finish(self):
        for store in self.stores:
            store.wait()

    @staticmethod
    def scratch(rows, dtype):
        return [pltpu.VMEM((rows, D_MODEL), dtype), pltpu.SemaphoreType.DMA((LOCAL_PIECES,)),
                pltpu.SemaphoreType.DMA((LOCAL_PIECES,))]


class _Gather:
    def __init__(self, w_ref, out_ref, send1, recv1, send2, recv2, buf, sems_in, sems_out):
        x, y, c, chips = _place()
        me = 2 * x + y
        rows = w_ref.shape[0]
        half = rows // 2
        pieces = _row_chunks(half, ICI_PIECES)
        self.own = _LocalCopy(lambda r: w_ref.at[r], lambda r: out_ref.at[me, r], rows, buf, sems_in, sems_out)

        def rows_of(core, piece):
            start, size = piece
            return pl.ds(core * half + start, size)

        self.sends, self.arrivals, self.forwards, self.forward_arrivals = [], [], [], []
        for i, piece in enumerate(pieces):
            for j, (cx, cy) in enumerate(chips):
                k = j * len(pieces) + i
                there = 2 * cx + cy

                def direct(src_chip, cx=cx, cy=cy, k=k, piece=piece):
                    return pltpu.make_async_remote_copy(
                        src_ref=w_ref.at[rows_of(c, piece)], dst_ref=out_ref.at[src_chip, rows_of(c, piece)],
                        send_sem=send1.at[k], recv_sem=recv1.at[k], device_id=(cx, cy, c), device_id_type=MESH)

                def passed(core, there=there, k=k, piece=piece):
                    return pltpu.make_async_remote_copy(
                        src_ref=out_ref.at[there, rows_of(core, piece)], dst_ref=out_ref.at[there, rows_of(core, piece)],
                        send_sem=send2.at[k], recv_sem=recv2.at[k], device_id=(x, y, 1 - c), device_id_type=MESH)

                self.sends.append(direct(me))
                self.arrivals.append(direct(there))
                self.forwards.append(passed(c))
                self.forward_arrivals.append(passed(1 - c))

    def start(self):
        for cp in self.sends:
            cp.start()
        self.own.start()

    def pass_on(self):
        self.own.pass_on()
        for arrival, forward in zip(self.arrivals, self.forwards):
            arrival.wait_recv()
            forward.start()

    def finish(self):
        for arrival in self.forward_arrivals:
            arrival.wait_recv()
        for cp in self.sends + self.forwards:
            cp.wait_send()
        self.own.finish()

    @staticmethod
    def scratch(rows, dtype):
        n = N_PEER_CHIPS * len(_row_chunks(rows // 2, ICI_PIECES))
        return [pltpu.SemaphoreType.DMA((n,))] * 4 + _LocalCopy.scratch(rows, dtype)

    @staticmethod
    def out_shape(rows, dtype):
        return jax.ShapeDtypeStruct((N_CHIPS, rows, D_MODEL), dtype)


def _gather_weights(pack):
    rows = pack.shape[0]

    def body(w_ref, out_ref, *scratch):
        gather = _Gather(w_ref, out_ref, *scratch)
        gather.start()
        gather.pass_on()
        gather.finish()

    return pl.pallas_call(
        body, name="gather_weights", in_specs=[ANY], out_specs=ANY, out_shape=_Gather.out_shape(rows, pack.dtype),
        scratch_shapes=_Gather.scratch(rows, pack.dtype),
        compiler_params=pltpu.CompilerParams(vmem_limit_bytes=VMEM_LIMIT_V7X),
    )(pack)


class _Scatter:
    def __init__(self, h_ref, out_ref, send, recv):
        x, y, c, chips = _place()
        pieces = _row_chunks(h_ref.shape[1], ICI_PIECES)
        self.copies = []
        for i, (start, size) in enumerate(pieces):
            for j, (cx, cy) in enumerate(chips):
                k = j * len(pieces) + i
                self.copies.append(pltpu.make_async_remote_copy(
                    src_ref=h_ref.at[2 * cx + cy, pl.ds(start, size)], dst_ref=out_ref.at[j, pl.ds(start, size)],
                    send_sem=send.at[k], recv_sem=recv.at[k], device_id=(cx, cy, c), device_id_type=MESH))

    def start(self):
        for cp in self.copies:
            cp.start()

    def finish(self):
        for cp in self.copies:
            cp.wait_recv()
        for cp in self.copies:
            cp.wait_send()

    @staticmethod
    def scratch(half):
        n = N_PEER_CHIPS * len(_row_chunks(half, ICI_PIECES))
        return [pltpu.SemaphoreType.DMA((n,))] * 2

    @staticmethod
    def out_shape(half, dtype):
        return jax.ShapeDtypeStruct((N_PEER_CHIPS, half, D_MODEL), dtype)


def _scatter_to_chips(h):
    half = h.shape[1]

    def body(h_ref, out_ref, send, recv):
        scatter = _Scatter(h_ref, out_ref, send, recv)
        scatter.start()
        scatter.finish()

    return pl.pallas_call(
        body, name="scatter_to_chips", in_specs=[ANY], out_specs=ANY, out_shape=_Scatter.out_shape(half, h.dtype),
        scratch_shapes=_Scatter.scratch(half),
    )(h)


class _Swap:
    def __init__(self, g_ref, theirs_ref, send, recv):
        x, y, c, _ = _place()
        half = g_ref.shape[1] // 2
        pieces = _row_chunks(half, D2D_PIECES)
        self.copies = []
        for s in range(N_CHIPS):
            for i, (start, size) in enumerate(pieces):
                k = s * len(pieces) + i
                self.copies.append(pltpu.make_async_remote_copy(
                    src_ref=g_ref.at[s, pl.ds((1 - c) * half + start, size)], dst_ref=theirs_ref.at[s, pl.ds(start, size)],
                    send_sem=send.at[k], recv_sem=recv.at[k], device_id=(x, y, 1 - c), device_id_type=MESH))

    def start(self):
        for cp in self.copies:
            cp.start()

    def finish(self):
        for cp in self.copies:
            cp.wait()

    @staticmethod
    def scratch(g):
        n = N_CHIPS * len(_row_chunks(g.shape[1] // 2, D2D_PIECES))
        return [pltpu.SemaphoreType.DMA((n,))] * 2

    @staticmethod
    def out_shape(g):
        return jax.ShapeDtypeStruct((N_CHIPS, g.shape[1] // 2, D_MODEL), g.dtype)


def _swap_halves(g):
    def body(g_ref, theirs_ref, send, recv):
        swap = _Swap(g_ref, theirs_ref, send, recv)
        swap.start()
        swap.finish()

    return pl.pallas_call(
        body, name="swap_halves", in_specs=[ANY], out_specs=ANY, out_shape=_Swap.out_shape(g),
        scratch_shapes=_Swap.scratch(g),
    )(g)


ADD_TILE_MAX_ROWS = 600


def _add_tile(half):
    return max(t for t in range(8, ADD_TILE_MAX_ROWS + 1, 8) if half % t == 0)


def _add_cores(g, theirs, name, out_dtype=f32):
    half = theirs.shape[1]
    tr = _add_tile(half)
    n_t = half // tr

    def body(c_ref, g_ref, t_ref, o_ref):
        o_ref[...] = (g_ref[...] + t_ref[...]).astype(out_dtype)

    blk = pl.BlockSpec((1, tr, D_MODEL), lambda s, t, c_ref: (s, t, 0))
    return pl.pallas_call(
        body, name=name,
        grid_spec=pltpu.PrefetchScalarGridSpec(
            num_scalar_prefetch=1, grid=(N_CHIPS, n_t),
            in_specs=[pl.BlockSpec((1, tr, D_MODEL), lambda s, t, c_ref: (s, c_ref[0] * n_t + t, 0)), blk],
            out_specs=blk),
        out_shape=jax.ShapeDtypeStruct(theirs.shape, out_dtype),
        compiler_params=_params("parallel", "parallel"),
    )(lax.axis_index("c").astype(jnp.int32).reshape(1), g, theirs)


def _add_chips(chip_sum, others, name):
    half = chip_sum.shape[1]
    tr = _add_tile(half)

    def body(me_ref, own_ref, o0, o1, o2, out_ref):
        out_ref[...] = ((own_ref[0].astype(f32) + o0[0].astype(f32)) + o1[0].astype(f32)) + o2[0].astype(f32)

    other = lambda j: pl.BlockSpec((1, tr, D_MODEL), lambda t, me_ref: (j, t, 0))
    return pl.pallas_call(
        body, name=name,
        grid_spec=pltpu.PrefetchScalarGridSpec(
            num_scalar_prefetch=1, grid=(half // tr,),
            in_specs=[pl.BlockSpec((1, tr, D_MODEL), lambda t, me_ref: (me_ref[0], t, 0)), other(0), other(1), other(2)],
            out_specs=pl.BlockSpec((tr, D_MODEL), lambda t, me_ref: (t, 0))),
        out_shape=jax.ShapeDtypeStruct((half, D_MODEL), f32),
        compiler_params=_params("parallel"),
    )((2 * lax.axis_index("x") + lax.axis_index("y")).astype(jnp.int32).reshape(1), chip_sum, others, others, others)


def _join_halves(parts):
    n_parts = len(parts)
    pieces = [_row_chunks(r.shape[0], D2D_PIECES) for r in parts]
    first = [sum(len(p) for p in pieces[:i]) for i in range(n_parts)]
    n = sum(len(p) for p in pieces)

    def body(*refs):
        r_refs, out_refs = refs[:n_parts], refs[n_parts:2 * n_parts]
        send, recv = refs[2 * n_parts:2 * n_parts + 2]
        local = refs[2 * n_parts + 2:]
        x, y, c, _ = _place()
        owns = [_LocalCopy(lambda rr, r_ref=r_ref: r_ref.at[rr], lambda rr, out_ref=out_ref: out_ref.at[c, rr],
                           r_ref.shape[0], *local[3 * i:3 * i + 3])
                for i, (r_ref, out_ref) in enumerate(zip(r_refs, out_refs))]
        for own in owns:
            own.start()

        def piece(i, j, core):
            start, size = pieces[i][j]
            return pltpu.make_async_remote_copy(
                src_ref=r_refs[i].at[pl.ds(start, size)], dst_ref=out_refs[i].at[core, pl.ds(start, size)],
                send_sem=send.at[first[i] + j], recv_sem=recv.at[first[i] + j],
                device_id=(x, y, 1 - c), device_id_type=MESH)

        every = [(i, j) for i in range(n_parts) for j in range(len(pieces[i]))]
        copies = [piece(i, j, c) for i, j in every]
        for cp in copies:
            cp.start()
        for own in owns:
            own.pass_on()
        for i, j in every:
            piece(i, j, 1 - c).wait_recv()
        for cp in copies:
            cp.wait_send()
        for own in owns:
            own.finish()

    local_scratch = []
    for r in parts:
        local_scratch += _LocalCopy.scratch(r.shape[0], r.dtype)
    return pl.pallas_call(
        body, name="join_halves", in_specs=[ANY] * n_parts, out_specs=[ANY] * n_parts,
        out_shape=[jax.ShapeDtypeStruct((2,) + r.shape, r.dtype) for r in parts],
        scratch_shapes=[pltpu.SemaphoreType.DMA((n,))] * 2 + local_scratch,
        compiler_params=pltpu.CompilerParams(vmem_limit_bytes=VMEM_LIMIT_V7X),
    )(*parts)


class _SmallSum:
    def __init__(self, b_ref, gathered, send, recv, local_sem):
        x, y, c, _ = _place()
        me = 4 * x + 2 * y + c
        self.gathered = gathered
        self.own = pltpu.make_async_copy(b_ref, gathered.at[me], local_sem)
        self.sends, self.arrivals = [], []
        for kk in range(1, N_DEV):
            flip = lambda v, bit: 1 - v if bit else v
            peer = (flip(x, kk & 4), flip(y, kk & 2), flip(c, kk & 1))
            self.sends.append(pltpu.make_async_remote_copy(
                src_ref=b_ref, dst_ref=gathered.at[me], send_sem=send.at[kk - 1], recv_sem=recv.at[kk - 1],
                device_id=peer, device_id_type=MESH))
            self.arrivals.append(pltpu.make_async_remote_copy(
                src_ref=b_ref, dst_ref=gathered.at[jnp.bitwise_xor(me, kk)], send_sem=send.at[kk - 1],
                recv_sem=recv.at[kk - 1], device_id=peer, device_id_type=MESH))

    def start(self):
        self.own.start()
        for cp in self.sends:
            cp.start()

    def finish(self, out_ref):
        self.own.wait()
        for cp in self.arrivals:
            cp.wait_recv()
        for cp in self.sends:
            cp.wait_send()
        acc = self.gathered[0]
        for dev in range(1, N_DEV):
            acc = acc + self.gathered[dev]
        out_ref[...] = acc

    @staticmethod
    def scratch(block):
        return [pltpu.VMEM((N_DEV,) + block.shape, block.dtype), pltpu.SemaphoreType.DMA((N_DEV - 1,)),
                pltpu.SemaphoreType.DMA((N_DEV - 1,)), pltpu.SemaphoreType.DMA]


def _adamw(w, g, m, v, name):
    rows, cols = w.shape
    tr = max(t for t in range(8, 513, 8) if rows % t == 0)
    c1 = 1.0 - ADAM_B1 ** ADAM_STEP
    c2 = 1.0 - ADAM_B2 ** ADAM_STEP

    def body(w_ref, g_ref, m_ref, v_ref, d_ref, nm_ref, nv_ref):
        gv = g_ref[...]
        nm = ADAM_B1 * m_ref[...] + (1.0 - ADAM_B1) * gv
        nv = ADAM_B2 * v_ref[...] + (1.0 - ADAM_B2) * (gv * gv)
        nm_ref[...] = nm
        nv_ref[...] = nv
        d_ref[...] = -ADAM_LR * ((nm / c1) / (jnp.sqrt(nv / c2) + ADAM_EPS) + ADAM_WD * w_ref[...])

    blk = pl.BlockSpec((tr, cols), lambda i: (i, 0))
    shape = jax.ShapeDtypeStruct((rows, cols), f32)
    return pl.pallas_call(
        body, name=name, grid=(rows // tr,), in_specs=[blk] * 4, out_specs=[blk] * 3, out_shape=[shape] * 3,
        compiler_params=_params("parallel"),
    )(w, g, m, v)


LARGE = ("w_in", "w_out", "w_gate", "w_up", "w_down")
SMALL = ("ln_pre_mix", "ln_post_mix", "ln_pre_ffn", "ln_post_ffn", "pool_scale", "w_pool")
SHARD_ROWS = {"w_in": 640, "w_out": 256, "w_gate": 704, "w_up": 704, "w_down": 704}
COLUMN_SHARDED = ("w_in", "w_gate", "w_up")
UPDATED_TRANSPOSED = ("w_gate", "w_up")
NEEDED_FIRST = ("w_in",)
NEEDED_LATER = ("w_out", "w_gate", "w_up", "w_down")
READY_EARLY = ("w_out", "w_gate", "w_up", "w_down")
READY_LATE = ("w_in",)


def _pack_shard(shards, names):
    return jnp.concatenate([shards[n].T if n in COLUMN_SHARDED else shards[n] for n in names], axis=0)


def _unpack_shard(pack, names):
    out, row = {}, 0
    for n in names:
        out[n] = pack[row:row + SHARD_ROWS[n]]
        row += SHARD_ROWS[n]
    return out


def _whole_from_shards(packs, names):
    out, row = {}, 0
    for n in names:
        rows = SHARD_ROWS[n]
        out[n] = packs[:, row:row + rows].reshape(N_CHIPS * rows, D_MODEL)
        row += rows
    return out


def _shards_from_whole(grads, names):
    return jnp.concatenate([grads[n].reshape(N_CHIPS, SHARD_ROWS[n], D_MODEL) for n in names], axis=1)


def _pack_small(vals):
    rows = [vals[n].reshape(1, D_MODEL) for n in SMALL[:4]]
    rows.append(jnp.pad(vals["pool_scale"].reshape(1, POOL_WIDTH), ((0, 0), (0, D_MODEL - POOL_WIDTH))))
    rows.append(jnp.pad(vals["loss"].reshape(1, 1), ((0, 0), (0, D_MODEL - 1))))
    rows.append(jnp.zeros((2, D_MODEL), f32))
    rows.append(vals["w_pool"].reshape(16, D_MODEL))
    return jnp.concatenate(rows, axis=0)


def _unpack_small(block):
    out = {n: block[i:i + 1] for i, n in enumerate(SMALL[:4])}
    out["pool_scale"] = block[4:5, :POOL_WIDTH]
    out["loss"] = block[5, 0]
    out["w_pool"] = block[8:24].reshape(1, 4, POOL_GROUP, POOL_GROUP)
    return out


def kernel(x, ln_pre_mix, w_in, w_pool, pool_scale, w_out, ln_post_mix, ln_pre_ffn, w_gate, w_up, w_down, ln_post_ffn, loss_target, m_ln_pre_mix, m_w_in, m_w_pool, m_pool_scale, m_w_out, m_ln_post_mix, m_ln_pre_ffn, m_w_gate, m_w_up, m_w_down, m_ln_post_ffn, v_ln_pre_mix, v_w_in, v_w_pool, v_pool_scale, v_w_out, v_ln_post_mix, v_ln_pre_ffn, v_w_gate, v_w_up, v_w_down, v_ln_post_ffn):
    w = dict(ln_pre_mix=ln_pre_mix, w_in=w_in, w_pool=w_pool, pool_scale=pool_scale, w_out=w_out,
             ln_post_mix=ln_post_mix, ln_pre_ffn=ln_pre_ffn, w_gate=w_gate, w_up=w_up, w_down=w_down,
             ln_post_ffn=ln_post_ffn)
    m = dict(ln_pre_mix=m_ln_pre_mix, w_in=m_w_in, w_pool=m_w_pool, pool_scale=m_pool_scale, w_out=m_w_out,
             ln_post_mix=m_ln_post_mix, ln_pre_ffn=m_ln_pre_ffn, w_gate=m_w_gate, w_up=m_w_up, w_down=m_w_down,
             ln_post_ffn=m_ln_post_ffn)
    v = dict(ln_pre_mix=v_ln_pre_mix, w_in=v_w_in, w_pool=v_w_pool, pool_scale=v_pool_scale, w_out=v_w_out,
             ln_post_mix=v_ln_post_mix, ln_pre_ffn=v_ln_pre_ffn, w_gate=v_w_gate, w_up=v_w_up, w_down=v_w_down,
             ln_post_ffn=v_ln_post_ffn)

    xs, target = x[0], loss_target[0]
    cos_t, sin_t = _rope_tables(xs.shape[0])
    w_bd = _block_diag(w_pool[0]).astype(bf16)
    shard = {n: w[n][0].astype(bf16) for n in LARGE}

    w_in_whole = _whole_from_shards(_gather_weights(_pack_shard(shard, NEEDED_FIRST)), NEEDED_FIRST)["w_in"]
    h1, u, qs, ks, vs = _in_proj(xs, ln_pre_mix, w_in_whole, cos_t, sin_t)
    pool_out = _pool_fwd(u, w_bd, pool_scale)
    attn_out, lse, later = _attn_fwd(qs, ks, vs, _pack_shard(shard, NEEDED_LATER))
    whole = _whole_from_shards(later, NEEDED_LATER)
    mix, x2, h2 = _out_proj(pool_out, attn_out, whole["w_out"], xs, ln_post_mix, ln_pre_ffn)
    gate, up, f = _ffn_fwd(h2, whole["w_gate"], whole["w_up"], whole["w_down"])
    dy, df, dg4, loss = _loss_head(f, x2, target, ln_post_ffn)

    large = {}
    a, dgate, dup, dh2 = _ffn_bwd(df, gate, up, whole["w_gate"], whole["w_up"], whole["w_down"])
    large["w_down"] = _matmul_tiles_tn(a, df, "grad_w_down")
    large["w_gate"] = _matmul_tiles_tn(dgate, h2, "grad_w_gate")
    large["w_up"] = _matmul_tiles_tn(dup, h2, "grad_w_up")
    dx2, dmix, dg3, dg2 = _norm_bwd(dh2, dy, x2, mix, ln_pre_ffn, ln_post_mix)
    large["w_out"] = jnp.concatenate([_matmul_tn(pool_out, dmix, "grad_w_out_pool"),
                                      _matmul_tn(attn_out, dmix, "grad_w_out_attn")], axis=0)
    early = _shards_from_whole(large, READY_EARLY)
    dpool, delta, dos, early_theirs = _out_proj_bwd(dmix, whole["w_out"], attn_out, _head_ones(), early)
    early_chip = _add_cores(early, early_theirs, "add_cores_early")
    du, d_w_bd, d_scale = _pool_bwd(u, dpool, w_bd, pool_scale)
    dq, dk, dv, early_others = _attn_bwd(qs, ks, vs, dos, lse, delta, early_chip)
    grad_x, dproj, dg1 = _in_proj_bwd(du, dq, dk, dv, cos_t, sin_t, w_in_whole, xs, dx2, ln_pre_mix)
    d_w_pool = jnp.stack([d_w_bd[g * POOL_GROUP:(g + 1) * POOL_GROUP, g * POOL_GROUP:(g + 1) * POOL_GROUP]
                          for g in range(POOL_WIDTH // POOL_GROUP)])
    small = dict(ln_pre_mix=dg1, ln_post_mix=dg2, ln_pre_ffn=dg3, ln_post_ffn=dg4, pool_scale=d_scale, w_pool=d_w_pool)
    large["w_in"], small_total = _matmul_tn_and_small_sum(dproj, h1, _pack_small(dict(small, loss=loss)), "grad_w_in")
    late = _shards_from_whole(large, READY_LATE)
    late_chip = _add_cores(late, _swap_halves(late), "add_cores_late", bf16)
    late_others = _scatter_to_chips(late_chip)
    early_half = _add_chips(early_chip, early_others, "add_chips_early")
    late_half = _add_chips(late_chip, late_others, "add_chips_late")
    early_whole, late_whole = _join_halves([early_half, late_half])
    grads = _unpack_shard(early_whole.reshape(-1, D_MODEL), READY_EARLY)
    grads.update(_unpack_shard(late_whole.reshape(-1, D_MODEL), READY_LATE))

    total = _unpack_small(small_total)
    for n in SMALL:
        grads[n] = total[n]

    delta_w, new_m, new_v = {}, {}, {}
    for n in LARGE:
        if n in UPDATED_TRANSPOSED:
            update = _adamw(w[n][0].T, grads[n], m[n][0].T, v[n][0].T, "adamw_" + n)
            delta_w[n], new_m[n], new_v[n], grads[n] = [a.T for a in (*update, grads[n])]
        else:
            if n in COLUMN_SHARDED:
                grads[n] = grads[n].T
            delta_w[n], new_m[n], new_v[n] = _adamw(w[n][0], grads[n], m[n][0], v[n][0], "adamw_" + n)
    small_state = [_pack_small(dict({n: s[n] for n in SMALL}, loss=jnp.zeros((), f32))) for s in (w, m, v)]
    small_grad = _pack_small(dict({n: grads[n] for n in SMALL}, loss=jnp.zeros((), f32)))
    sd, sm, sv = _adamw(small_state[0], small_grad, small_state[1], small_state[2], "adamw_small")
    for out, block in ((delta_w, sd), (new_m, sm), (new_v, sv)):
        un = _unpack_small(block)
        for n in SMALL:
            out[n] = un[n]

    names = ("ln_pre_mix", "w_in", "w_pool", "pool_scale", "w_out", "ln_post_mix", "ln_pre_ffn", "w_gate", "w_up",
             "w_down", "ln_post_ffn")
    full = lambda d: [d[n].reshape(w[n].shape) for n in names]
    return (total["loss"], grad_x[None], *full(grads), *full(delta_w), *full(new_m), *full(new_v))
```

```python
import numpy as np
import jax
import jax.numpy as jnp
from jax import lax
from jax.experimental import pallas as pl
from jax.experimental.pallas import tpu as pltpu

D_MODEL = 1024
POOL_WIDTH = 256
POOL_GROUP = 64
ATTN_WIDTH = 768
HEAD_DIM = 64
IN_WIDTH = 2560
D_FF = 2816
BLOCK = 128
DILATIONS = (1, 4, 16)
ROPE_THETA = 10000.0
EPS = 1e-6
ATTN_SCALE = 0.125
NEG = -1e30

ADAM_LR = 0.001
ADAM_B1 = 0.9
ADAM_B2 = 0.999
ADAM_EPS = 1e-08
ADAM_WD = 0.01
ADAM_STEP = 10

N_CHIPS = 4
N_DEV = 8
VMEM_LIMIT_V7X = 56 * 1024 * 1024
MESH = pl.DeviceIdType.MESH

f32 = jnp.float32
bf16 = jnp.bfloat16


def _params(*sem):
    return pltpu.CompilerParams(dimension_semantics=sem, vmem_limit_bytes=VMEM_LIMIT_V7X)


def _dot(a, b):
    return jnp.dot(a, b, preferred_element_type=f32)


def _dot_nt(a, b):
    return lax.dot_general(a, b, (((1,), (1,)), ((), ())), preferred_element_type=f32)


def _dot_tn(a, b):
    return lax.dot_general(a, b, (((0,), (0,)), ((), ())), preferred_element_type=f32)


def _rope_partner(a, first_half):
    return jnp.where(first_half, pltpu.roll(a, 96, 1), pltpu.roll(a, 32, 1))


def _first_half_mask(rows):
    lane = lax.broadcasted_iota(jnp.int32, (rows, 128), 1)
    return (lane % HEAD_DIM) < (HEAD_DIM // 2)


def _stream_spec(d, ts):
    return pl.BlockSpec((d, ts // d, ATTN_WIDTH), lambda i: (0, i, 0))


def _stream_shape(S, d):
    return jax.ShapeDtypeStruct((d, S // d, ATTN_WIDTH), bf16)


N_STAGE = ATTN_WIDTH // 128


def _stage_scratch(ts):
    return [pltpu.VMEM((ts, 128), f32)] * N_STAGE


def _store_streams(stage, out_refs, ts):
    for d, ref in zip(DILATIONS, out_refs):
        for r in range(d):
            rows = pl.ds(0, ts) if d == 1 else pl.ds(r, ts // d, stride=d)
            for j in range(N_STAGE):
                ref[r, :, j * 128:(j + 1) * 128] = stage[j][rows, :].astype(bf16)


def _in_proj(x, g1, w_in, cos_t, sin_t):
    S = x.shape[0]
    ts = 512

    def body(x_ref, g_ref, w_ref, cos_ref, sin_ref, h_ref, u_ref, *rest):
        outs, stage = rest[:-N_STAGE], rest[-N_STAGE:]
        xv = x_ref[...]
        r = lax.rsqrt(jnp.mean(xv * xv, axis=-1, keepdims=True) + EPS)
        h = ((xv * r) * g_ref[...]).astype(bf16)
        h_ref[...] = h
        proj = _dot_nt(h, w_ref[...])
        u_ref[...] = proj[:, :POOL_WIDTH]
        cos = cos_ref[...]
        sin = sin_ref[...]
        first = _first_half_mask(ts)
        n_dil = len(DILATIONS)
        for which, base in enumerate((POOL_WIDTH, POOL_WIDTH + ATTN_WIDTH)):
            for j in range(ATTN_WIDTH // 128):
                a = proj[:, base + j * 128: base + (j + 1) * 128]
                if which == 0:
                    a = a * ATTN_SCALE
                stage[j][...] = a * cos + _rope_partner(a, first) * sin
            _store_streams(stage, outs[which * n_dil:(which + 1) * n_dil], ts)
        for j in range(ATTN_WIDTH // 128):
            base = POOL_WIDTH + 2 * ATTN_WIDTH + j * 128
            stage[j][...] = proj[:, base:base + 128]
        _store_streams(stage, outs[2 * n_dil:], ts)

    row = lambda w: pl.BlockSpec((ts, w), lambda i: (i, 0))
    streams = [_stream_spec(d, ts) for d in DILATIONS] * 3
    res = pl.pallas_call(
        body, name="in_proj", grid=(S // ts,),
        in_specs=[row(D_MODEL), pl.BlockSpec((1, D_MODEL), lambda i: (0, 0)),
                  pl.BlockSpec((IN_WIDTH, D_MODEL), lambda i: (0, 0)), row(128), row(128)],
        out_specs=[row(D_MODEL), row(POOL_WIDTH)] + streams,
        out_shape=[jax.ShapeDtypeStruct((S, D_MODEL), bf16), jax.ShapeDtypeStruct((S, POOL_WIDTH), f32)]
        + [_stream_shape(S, d) for d in DILATIONS] * 3,
        scratch_shapes=_stage_scratch(ts),
        compiler_params=_params("parallel"),
    )(x, g1, w_in, cos_t, sin_t)
    n = len(DILATIONS)
    return res[0], res[1], res[2:2 + n], res[2 + n:2 + 2 * n], res[2 + 2 * n:]


POOL_HALO = 16


def _pool_lane_group(rows):
    return lax.broadcasted_iota(jnp.int32, (rows, POOL_WIDTH), 1) // POOL_GROUP


def _pool_select(group, s2, s4, s8, s16):
    return jnp.where(group == 0, s2, jnp.where(group == 1, s4, jnp.where(group == 2, s8, s16)))


def _pool_count(t0, rows):
    group = _pool_lane_group(rows)
    t = t0 + lax.broadcasted_iota(jnp.int32, (rows, POOL_WIDTH), 0)
    win = _pool_select(group, 2, 4, 8, 16)
    return jnp.minimum(t + 1, win).astype(f32)


def _pool_diff(u_halo, u_tile, t0):
    ts = u_tile.shape[0]
    ext = jnp.concatenate([u_halo, u_tile], axis=0)
    s2 = ext + pltpu.roll(ext, 1, 0)
    s4 = s2 + pltpu.roll(s2, 2, 0)
    s8 = s4 + pltpu.roll(s4, 4, 0)
    s16 = s8 + pltpu.roll(s8, 8, 0)
    group = _pool_lane_group(ts + POOL_HALO)
    wsum = _pool_select(group, s2, s4, s8, s16)[POOL_HALO:]
    return wsum / _pool_count(t0, ts) - u_tile


def _pool_specs(ts, n_tiles):
    tile = pl.BlockSpec((ts, POOL_WIDTH), lambda i: (i, 0))
    per = ts // POOL_HALO
    before = pl.BlockSpec((POOL_HALO, POOL_WIDTH), lambda i: (jnp.maximum(i * per - 1, 0), 0))
    after = pl.BlockSpec((POOL_HALO, POOL_WIDTH), lambda i: (jnp.minimum((i + 1) * per, n_tiles * per - 1), 0))
    return tile, before, after


def _pool_fwd(u, w_bd, scale):
    S = u.shape[0]
    ts = 2048
    n_tiles = S // ts

    def body(u_ref, halo_ref, w_ref, sc_ref, y_ref):
        i = pl.program_id(0)
        halo = jnp.where(i > 0, halo_ref[...], 0.0)
        d = _pool_diff(halo, u_ref[...], i * ts)
        y_ref[...] = (_dot(d.astype(bf16), w_ref[...]) * sc_ref[...]).astype(bf16)

    tile, before, _ = _pool_specs(ts, n_tiles)
    return pl.pallas_call(
        body, name="pool_fwd", grid=(n_tiles,),
        in_specs=[tile, before, pl.BlockSpec((POOL_WIDTH, POOL_WIDTH), lambda i: (0, 0)),
                  pl.BlockSpec((1, POOL_WIDTH), lambda i: (0, 0))],
        out_specs=tile, out_shape=jax.ShapeDtypeStruct((S, POOL_WIDTH), bf16),
        compiler_params=_params("parallel"),
    )(u, u, w_bd, scale)


def _pool_bwd(u, dy, w_bd, scale):
    S = u.shape[0]
    ts = 2048
    n_tiles = S // ts

    def body(u_ref, halo_ref, dy_ref, dy_next_ref, w_ref, sc_ref, du_ref, dw_ref, dsc_ref):
        i = pl.program_id(0)

        @pl.when(i == 0)
        def _():
            dw_ref[...] = jnp.zeros_like(dw_ref)
            dsc_ref[...] = jnp.zeros_like(dsc_ref)

        halo = jnp.where(i > 0, halo_ref[...], 0.0)
        d = _pool_diff(halo, u_ref[...], i * ts).astype(bf16)
        w = w_ref[...]
        sc = sc_ref[...]
        dy_tile = dy_ref[...]
        z = _dot(d, w)
        dsc_ref[...] += jnp.sum(dy_tile * z, axis=0, keepdims=True)
        dy_next = jnp.where(i < n_tiles - 1, dy_next_ref[...], 0.0)
        dz = (jnp.concatenate([dy_tile, dy_next], axis=0) * sc).astype(bf16)
        dw_ref[...] += _dot_tn(d, dz[:ts])
        dd = _dot_nt(dz, w)
        e = dd / _pool_count(i * ts, ts + POOL_HALO)
        n = ts + POOL_HALO
        f2 = e + pltpu.roll(e, n - 1, 0)
        f4 = f2 + pltpu.roll(f2, n - 2, 0)
        f8 = f4 + pltpu.roll(f4, n - 4, 0)
        f16 = f8 + pltpu.roll(f8, n - 8, 0)
        fsum = _pool_select(_pool_lane_group(n), f2, f4, f8, f16)
        du_ref[...] = (fsum[:ts] - dd[:ts]).astype(bf16)

    tile, before, after = _pool_specs(ts, n_tiles)
    return pl.pallas_call(
        body, name="pool_bwd", grid=(n_tiles,),
        in_specs=[tile, before, tile, after, pl.BlockSpec((POOL_WIDTH, POOL_WIDTH), lambda i: (0, 0)),
                  pl.BlockSpec((1, POOL_WIDTH), lambda i: (0, 0))],
        out_specs=[tile, pl.BlockSpec((POOL_WIDTH, POOL_WIDTH), lambda i: (0, 0)),
                   pl.BlockSpec((1, POOL_WIDTH), lambda i: (0, 0))],
        out_shape=[jax.ShapeDtypeStruct((S, POOL_WIDTH), bf16), jax.ShapeDtypeStruct((POOL_WIDTH, POOL_WIDTH), f32),
                   jax.ShapeDtypeStruct((1, POOL_WIDTH), f32)],
        compiler_params=_params("arbitrary"),
    )(u, u, dy, dy, w_bd, scale)


SUPER = BLOCK * DILATIONS[-1]
UNITS = SUPER // BLOCK
FWD_UNROLL = 16
BWD_UNROLL = 16


def _band_mask(has_prev):
    qi = lax.broadcasted_iota(jnp.int32, (BLOCK, 2 * BLOCK), 0)
    kj = lax.broadcasted_iota(jnp.int32, (BLOCK, 2 * BLOCK), 1)
    return (kj >= qi) & (kj <= qi + BLOCK) & ((kj >= BLOCK) | has_prev)


def _head0_mask(rows=BLOCK):
    return lax.broadcasted_iota(jnp.int32, (rows, 128), 1) < HEAD_DIM


def _band_mask_t(has_prev):
    ki = lax.broadcasted_iota(jnp.int32, (2 * BLOCK, 2 * BLOCK), 0)
    qj = lax.broadcasted_iota(jnp.int32, (2 * BLOCK, 2 * BLOCK), 1) % BLOCK
    return (ki >= qj) & (ki <= qj + BLOCK) & ((ki >= BLOCK) | has_prev)


def _head_pair_rows(a, h0):
    zero = jnp.zeros_like(a)
    return jnp.concatenate([jnp.where(h0, a, zero), jnp.where(h0, zero, a)], axis=0)


def _per_query_row(stat):
    t = stat.T
    return jnp.concatenate([t[0:1], t[HEAD_DIM:HEAD_DIM + 1]], axis=1)


def _natural_rows(d, r, n):
    if d == 1:
        return pl.ds(pl.multiple_of(n * BLOCK, BLOCK), BLOCK)
    return pl.ds(n * (BLOCK * d) + r, BLOCK, stride=d)


def _unit_place(d, u):
    per_stream = UNITS // d
    return u // per_stream, u % per_stream, per_stream


def _block_rows(n):
    return pl.ds(pl.multiple_of(n * BLOCK, BLOCK), BLOCK)


def _band(cur_ref, tail_ref, r, n):
    before = jnp.where(n > 0, cur_ref[r, _block_rows(jnp.maximum(n - 1, 0)), :], tail_ref[r])
    return jnp.concatenate([before, cur_ref[r, _block_rows(n), :]], axis=0)


def _attn_in_specs(S, with_do):
    specs = []
    last = S // SUPER - 1
    for d in DILATIONS:
        per_stream = UNITS // d
        cur = pl.BlockSpec((d, SUPER // d, 128), lambda hp, sb: (0, jnp.minimum(sb, last), hp))
        tail = pl.BlockSpec(
            (d, BLOCK, 128),
            lambda hp, sb, per_stream=per_stream: (0, jnp.maximum(jnp.minimum(sb, last) * per_stream - 1, 0), hp))
        specs += [cur] * (2 if with_do else 1) + [cur, tail, cur, tail]
    return specs


def _attn_fwd(qs, ks, vs, pack):
    S = qs[0].shape[1]
    n_dil = len(DILATIONS)
    n_steps = S // SUPER
    n_total = (ATTN_WIDTH // 128) * n_steps

    def body(*refs):
        ins, pack_ref = refs[:5 * n_dil], refs[5 * n_dil]
        out_ref, lse_ref, gathered_ref = refs[5 * n_dil + 1:5 * n_dil + 4]
        scratch = refs[5 * n_dil + 4:]
        o_sc, l_sc = scratch[:n_dil], scratch[n_dil:2 * n_dil]
        gather = _Gather(pack_ref, gathered_ref, *scratch[2 * n_dil:])
        sb = pl.program_id(1)
        step = pl.program_id(0) * n_steps + sb

        @pl.when(step == 0)
        def _():
            gather.start()

        h0 = _head0_mask()
        for ci, d in enumerate(DILATIONS):
            q_ref, kc_ref, kp_ref, vc_ref, vp_ref = ins[5 * ci:5 * ci + 5]

            def unit(u, carry, d=d, ci=ci, q_ref=q_ref, kc_ref=kc_ref, kp_ref=kp_ref, vc_ref=vc_ref, vp_ref=vp_ref):
                r, n, _ = _unit_place(d, u)
                qv = q_ref[r, _block_rows(n), :]
                kb = _band(kc_ref, kp_ref, r, n)
                vb = _band(vc_ref, vp_ref, r, n)
                valid = _band_mask((sb > 0) | (n > 0))
                s = jnp.where(jnp.concatenate([valid, valid], axis=0), _dot_nt(_head_pair_rows(qv, h0), kb), NEG)
                m = jnp.max(s, axis=1, keepdims=True)
                e = jnp.exp(s - m)
                den = jnp.sum(e, axis=1, keepdims=True)
                o_pair = _dot(e.astype(bf16), vb) * (1.0 / den)
                lse_pair = jnp.broadcast_to(m + jnp.log(den), (2 * BLOCK, 128))
                rows = _natural_rows(d, r, n)
                o_sc[ci][rows, :] = jnp.where(h0, o_pair[:BLOCK], o_pair[BLOCK:])
                l_sc[ci][rows, :] = jnp.where(h0, lse_pair[:BLOCK], lse_pair[BLOCK:])
                return carry

            lax.fori_loop(0, UNITS, unit, 0, unroll=FWD_UNROLL)

        def merge(t, carry):
            rows = pl.ds(pl.multiple_of(t * 256, 256), 256)
            a, b, c = l_sc[0][rows, :], l_sc[1][rows, :], l_sc[2][rows, :]
            m = jnp.maximum(jnp.maximum(a, b), c)
            ea, eb, ec = jnp.exp(a - m), jnp.exp(b - m), jnp.exp(c - m)
            tot = ea + eb + ec
            out_ref[rows, :] = ((ea / tot) * o_sc[0][rows, :] + (eb / tot) * o_sc[1][rows, :]
                                + (ec / tot) * o_sc[2][rows, :]).astype(bf16)
            lse_ref[rows, :] = m + jnp.log(tot)
            return carry

        lax.fori_loop(0, SUPER // 256, merge, 0)

        @pl.when(step == (2 * n_total) // 3)
        def _():
            gather.pass_on()

        @pl.when(step == n_total - 1)
        def _():
            gather.finish()

    args = []
    for q, k, v in zip(qs, ks, vs):
        args += [q, k, k, v, v]
    nat = pl.BlockSpec((SUPER, 128), lambda hp, sb: (sb, hp))
    rows = pack.shape[0]
    return pl.pallas_call(
        body, name="attn_fwd", grid=(ATTN_WIDTH // 128, n_steps),
        in_specs=_attn_in_specs(S, False) + [ANY], out_specs=[nat, nat, ANY],
        out_shape=[jax.ShapeDtypeStruct((S, ATTN_WIDTH), bf16), jax.ShapeDtypeStruct((S, ATTN_WIDTH), f32),
                   _Gather.out_shape(rows, pack.dtype)],
        scratch_shapes=[pltpu.VMEM((SUPER, 128), f32)] * (2 * n_dil) + _Gather.scratch(rows, pack.dtype),
        compiler_params=_params("arbitrary", "arbitrary"),
    )(*args, pack)


def _attn_bwd(qs, ks, vs, dos, lse, delta, chip_sum):
    S = qs[0].shape[1]
    n_steps = S // SUPER
    last = n_steps - 1
    n_dil = len(DILATIONS)
    n_total = (ATTN_WIDTH // 128) * (n_steps + 1)

    def body(*refs):
        ins, (lse_ref, dl_ref, sum_ref) = refs[:6 * n_dil], refs[6 * n_dil:6 * n_dil + 3]
        dq_ref, dk_ref, dv_ref, others_ref = refs[6 * n_dil + 3:6 * n_dil + 7]
        dq_acc, dk_acc, dv_acc = refs[6 * n_dil + 7:6 * n_dil + 10]
        scatter = _Scatter(sum_ref, others_ref, *refs[6 * n_dil + 10:])
        sb = pl.program_id(1)
        step = pl.program_id(0) * (n_steps + 1) + sb
        cur = sb % 2
        prv = 1 - cur

        @pl.when(step == 0)
        def _():
            scatter.start()

        @pl.when(sb < n_steps)
        def _():
            dq_acc[...] = jnp.zeros_like(dq_acc)
            dk_acc[cur] = jnp.zeros((SUPER, 128), f32)
            dv_acc[cur] = jnp.zeros((SUPER, 128), f32)
            h0 = _head0_mask()
            for ci, d in enumerate(DILATIONS):
                q_ref, do_ref, kc_ref, kp_ref, vc_ref, vp_ref = ins[6 * ci:6 * ci + 6]

                def unit(u, carry, d=d, q_ref=q_ref, do_ref=do_ref, kc_ref=kc_ref, kp_ref=kp_ref, vc_ref=vc_ref,
                         vp_ref=vp_ref):
                    r, n, per_stream = _unit_place(d, u)
                    qv = q_ref[r, _block_rows(n), :]
                    dov = do_ref[r, _block_rows(n), :]
                    kb = _band(kc_ref, kp_ref, r, n)
                    vb = _band(vc_ref, vp_ref, r, n)
                    rows = _natural_rows(d, r, n)
                    has_prev = (sb > 0) | (n > 0)
                    q_pair = _head_pair_rows(qv, h0)
                    do_pair = _head_pair_rows(dov, h0)
                    s_t = jnp.where(_band_mask_t(has_prev), _dot_nt(kb, q_pair), NEG)
                    p_t = jnp.exp(s_t - _per_query_row(lse_ref[rows, :]))
                    dp_t = _dot_nt(vb, do_pair)
                    ds_t = (p_t * (dp_t - _per_query_row(dl_ref[rows, :]))).astype(bf16)
                    dvb = _dot(p_t.astype(bf16), do_pair)
                    dkb = _dot(ds_t, q_pair)
                    dq_pair = _dot_tn(ds_t, kb)
                    dq_acc[rows, :] += jnp.where(h0, dq_pair[:BLOCK], dq_pair[BLOCK:])
                    dk_acc[cur, rows, :] += dkb[BLOCK:]
                    dv_acc[cur, rows, :] += dvb[BLOCK:]

                    slot = jnp.where((n > 0) | (sb == 0), cur, prv)
                    before = _natural_rows(d, r, jnp.where(n > 0, n - 1, per_stream - 1))
                    dk_acc[slot, before, :] += dkb[:BLOCK]
                    dv_acc[slot, before, :] += dvb[:BLOCK]
                    return carry

                lax.fori_loop(0, UNITS, unit, 0, unroll=BWD_UNROLL)
            dq_ref[...] = (dq_acc[...] * ATTN_SCALE).astype(bf16)

        @pl.when(sb > 0)
        def _():
            dk_ref[...] = dk_acc[prv].astype(bf16)
            dv_ref[...] = dv_acc[prv].astype(bf16)

        @pl.when(step == n_total - 1)
        def _():
            scatter.finish()

    args = []
    for q, k, v, do in zip(qs, ks, vs, dos):
        args += [q, do, k, k, v, v]
    nat = pl.BlockSpec((SUPER, 128), lambda hp, sb: (jnp.minimum(sb, last), hp))
    nat_before = pl.BlockSpec((SUPER, 128), lambda hp, sb: (jnp.clip(sb - 1, 0, last), hp))
    out = jax.ShapeDtypeStruct((S, ATTN_WIDTH), bf16)
    half = chip_sum.shape[1]
    return pl.pallas_call(
        body, name="attn_bwd", grid=(ATTN_WIDTH // 128, n_steps + 1),
        in_specs=_attn_in_specs(S, True) + [nat, nat, ANY], out_specs=[nat, nat_before, nat_before, ANY],
        out_shape=[out, out, out, _Scatter.out_shape(half, chip_sum.dtype)],
        scratch_shapes=[pltpu.VMEM((SUPER, 128), f32), pltpu.VMEM((2, SUPER, 128), f32),
                        pltpu.VMEM((2, SUPER, 128), f32)] + _Scatter.scratch(half),
        compiler_params=_params("arbitrary", "arbitrary"),
    )(*args, lse, delta, chip_sum)


def _rms(v):
    return lax.rsqrt(jnp.mean(v * v, axis=-1, keepdims=True) + EPS)


def _out_proj(pool_out, attn_out, w_out, x, g2, g3):
    S = x.shape[0]
    ts = 512

    def body(p_ref, a_ref, w_ref, x_ref, g2_ref, g3_ref, mix_ref, x2_ref, h2_ref):
        mix = _dot(p_ref[...], w_ref[:POOL_WIDTH, :]) + _dot(a_ref[...], w_ref[POOL_WIDTH:, :])
        mix_ref[...] = mix
        x2 = x_ref[...] + (mix * _rms(mix)) * g2_ref[...]
        x2_ref[...] = x2
        h2_ref[...] = ((x2 * _rms(x2)) * g3_ref[...]).astype(bf16)

    row = lambda w: pl.BlockSpec((ts, w), lambda i: (i, 0))
    gain = pl.BlockSpec((1, D_MODEL), lambda i: (0, 0))
    return pl.pallas_call(
        body, name="out_proj", grid=(S // ts,),
        in_specs=[row(POOL_WIDTH), row(ATTN_WIDTH), pl.BlockSpec((D_MODEL, D_MODEL), lambda i: (0, 0)),
                  row(D_MODEL), gain, gain],
        out_specs=[row(D_MODEL)] * 3,
        out_shape=[jax.ShapeDtypeStruct((S, D_MODEL), f32), jax.ShapeDtypeStruct((S, D_MODEL), f32),
                   jax.ShapeDtypeStruct((S, D_MODEL), bf16)],
        compiler_params=_params("parallel"),
    )(pool_out, attn_out, w_out, x, g2, g3)


FF_TILE = 256
FF_STEP_ROWS = 2048
FF_ROWS = 512
FF_BWD_ROWS = 256


def _sigmoid(g):
    return 1.0 / (1.0 + jnp.exp(-g))


def _ff_act_shape(S):
    return jax.ShapeDtypeStruct((D_FF // FF_TILE, S, FF_TILE), bf16)


def _ff_act_spec(ts):
    return pl.BlockSpec((1, ts, FF_TILE), lambda i, j: (j, i, 0))


def _ffn_fwd(h2, w_gate, w_up, w_down):
    S = h2.shape[0]
    ts = min(S, FF_STEP_ROWS)

    def body(h_ref, wg_ref, wu_ref, wd_ref, gate_ref, up_ref, f_ref):
        def rows_pass(first):
            def sub(i, carry):
                rows = pl.ds(pl.multiple_of(i * FF_ROWS, FF_ROWS), FF_ROWS)
                h = h_ref[rows, :]
                gate = _dot_nt(h, wg_ref[...])
                up = _dot_nt(h, wu_ref[...])
                gate_ref[0, rows, :] = gate.astype(bf16)
                up_ref[0, rows, :] = up.astype(bf16)
                part = _dot((gate * _sigmoid(gate) * up).astype(bf16), wd_ref[...])
                if first:
                    f_ref[rows, :] = part
                else:
                    f_ref[rows, :] += part
                return carry

            lax.fori_loop(0, ts // FF_ROWS, sub, 0, unroll=True)

        @pl.when(pl.program_id(1) == 0)
        def _():
            rows_pass(True)

        @pl.when(pl.program_id(1) > 0)
        def _():
            rows_pass(False)

    act = _ff_act_spec(ts)
    weight = pl.BlockSpec((FF_TILE, D_MODEL), lambda i, j: (j, 0))
    return pl.pallas_call(
        body, name="ffn_fwd", grid=(S // ts, D_FF // FF_TILE),
        in_specs=[pl.BlockSpec((ts, D_MODEL), lambda i, j: (i, 0)), weight, weight, weight],
        out_specs=[act, act, pl.BlockSpec((ts, D_MODEL), lambda i, j: (i, 0))],
        out_shape=[_ff_act_shape(S), _ff_act_shape(S), jax.ShapeDtypeStruct((S, D_MODEL), f32)],
        compiler_params=_params("parallel", "arbitrary"),
    )(h2, w_gate, w_up, w_down)


def _loss_head(f, x2, target, g4):
    S = f.shape[0]
    ts = 512

    def body(f_ref, x2_ref, t_ref, g_ref, dy_ref, df_ref, dg_ref, loss_ref):
        @pl.when(pl.program_id(0) == 0)
        def _():
            dg_ref[...] = jnp.zeros_like(dg_ref)
            loss_ref[...] = jnp.zeros_like(loss_ref)

        fv = f_ref[...]
        g = g_ref[...]
        r = _rms(fv)
        fhat = fv * r
        err = (x2_ref[...] + fhat * g) - t_ref[...]
        loss_ref[...] += 0.5 * jnp.sum(jnp.mean(err * err, axis=-1, keepdims=True), axis=0, keepdims=True)
        dy = err * (1.0 / D_MODEL)
        dy_ref[...] = dy
        dg_ref[...] += jnp.sum(dy * fhat, axis=0, keepdims=True)
        dyg = dy * g
        df_ref[...] = (r * (dyg - fhat * jnp.mean(dyg * fhat, axis=-1, keepdims=True))).astype(bf16)

    row = pl.BlockSpec((ts, D_MODEL), lambda i: (i, 0))
    gain = pl.BlockSpec((1, D_MODEL), lambda i: (0, 0))
    return pl.pallas_call(
        body, name="loss_head", grid=(S // ts,), in_specs=[row, row, row, gain],
        out_specs=[row, row, gain, pl.BlockSpec((1, 1), lambda i: (0, 0))],
        out_shape=[jax.ShapeDtypeStruct((S, D_MODEL), f32), jax.ShapeDtypeStruct((S, D_MODEL), bf16),
                   jax.ShapeDtypeStruct((1, D_MODEL), f32), jax.ShapeDtypeStruct((1, 1), f32)],
        compiler_params=_params("arbitrary"),
    )(f, x2, target, g4)


def _ffn_bwd(df, gate, up, w_gate, w_up, w_down):
    S = df.shape[0]
    ts = min(S, FF_STEP_ROWS)

    def body(df_ref, gate_ref, up_ref, wg_ref, wu_ref, wd_ref, a_ref, dgate_ref, dup_ref, dh_ref):
        def rows_pass(first):
            def sub(i, carry):
                rows = pl.ds(pl.multiple_of(i * FF_BWD_ROWS, FF_BWD_ROWS), FF_BWD_ROWS)
                da = _dot_nt(df_ref[rows, :], wd_ref[...])
                g = gate_ref[0, rows, :].astype(f32)
                u = up_ref[0, rows, :].astype(f32)
                sig = _sigmoid(g)
                silu = g * sig
                a_ref[0, rows, :] = (silu * u).astype(bf16)
                dup = (da * silu).astype(bf16)
                dgate = (da * u * (sig * (1.0 + g * (1.0 - sig)))).astype(bf16)
                dup_ref[0, rows, :] = dup
                dgate_ref[0, rows, :] = dgate
                part = _dot(dgate, wg_ref[...]) + _dot(dup, wu_ref[...])
                if first:
                    dh_ref[rows, :] = part
                else:
                    dh_ref[rows, :] += part
                return carry

            lax.fori_loop(0, ts // FF_BWD_ROWS, sub, 0, unroll=True)

        @pl.when(pl.program_id(1) == 0)
        def _():
            rows_pass(True)

        @pl.when(pl.program_id(1) > 0)
        def _():
            rows_pass(False)

    act = _ff_act_spec(ts)
    row = pl.BlockSpec((ts, D_MODEL), lambda i, j: (i, 0))
    return pl.pallas_call(
        body, name="ffn_bwd", grid=(S // ts, D_FF // FF_TILE),
        in_specs=[row, act, act,
                  pl.BlockSpec((FF_TILE, D_MODEL), lambda i, j: (j, 0)),
                  pl.BlockSpec((FF_TILE, D_MODEL), lambda i, j: (j, 0)),
                  pl.BlockSpec((FF_TILE, D_MODEL), lambda i, j: (j, 0))],
        out_specs=[act, act, act, row],
        out_shape=[_ff_act_shape(S)] * 3 + [jax.ShapeDtypeStruct((S, D_MODEL), f32)],
        compiler_params=_params("parallel", "arbitrary"),
    )(df, gate, up, w_gate, w_up, w_down)


def _norm_bwd(dh2, dy, x2, mix, g3, g2):
    S = dh2.shape[0]
    ts = 512

    def body(dh_ref, dy_ref, x2_ref, mix_ref, g3_ref, g2_ref, dx2_ref, dmix_ref, dg3_ref, dg2_ref):
        @pl.when(pl.program_id(0) == 0)
        def _():
            dg3_ref[...] = jnp.zeros_like(dg3_ref)
            dg2_ref[...] = jnp.zeros_like(dg2_ref)

        dh = dh_ref[...]
        x2 = x2_ref[...]
        r3 = _rms(x2)
        xhat = x2 * r3
        dg3_ref[...] += jnp.sum(dh * xhat, axis=0, keepdims=True)
        dhg = dh * g3_ref[...]
        dx2 = dy_ref[...] + r3 * (dhg - xhat * jnp.mean(dhg * xhat, axis=-1, keepdims=True))
        dx2_ref[...] = dx2
        mix = mix_ref[...]
        r2 = _rms(mix)
        mhat = mix * r2
        dg2_ref[...] += jnp.sum(dx2 * mhat, axis=0, keepdims=True)
        dmg = dx2 * g2_ref[...]
        dmix_ref[...] = (r2 * (dmg - mhat * jnp.mean(dmg * mhat, axis=-1, keepdims=True))).astype(bf16)

    row = pl.BlockSpec((ts, D_MODEL), lambda i: (i, 0))
    gain = pl.BlockSpec((1, D_MODEL), lambda i: (0, 0))
    return pl.pallas_call(
        body, name="norm_bwd", grid=(S // ts,), in_specs=[row, row, row, row, gain, gain],
        out_specs=[row, row, gain, gain],
        out_shape=[jax.ShapeDtypeStruct((S, D_MODEL), f32), jax.ShapeDtypeStruct((S, D_MODEL), bf16),
                   jax.ShapeDtypeStruct((1, D_MODEL), f32), jax.ShapeDtypeStruct((1, D_MODEL), f32)],
        compiler_params=_params("arbitrary"),
    )(dh2, dy, x2, mix, g3, g2)


def _out_proj_bwd(dmix, w_out, attn_out, head_ones, grads):
    S = dmix.shape[0]
    ts = 512
    n_dil = len(DILATIONS)

    def body(dm_ref, w_ref, o_ref, ones_ref, g_ref, dp_ref, dl_ref, *rest):
        do_refs, theirs_ref = rest[:n_dil], rest[n_dil]
        stage = rest[n_dil + 1:n_dil + 1 + N_STAGE]
        swap = _Swap(g_ref, theirs_ref, *rest[n_dil + 1 + N_STAGE:])

        @pl.when(pl.program_id(0) == 0)
        def _():
            swap.start()

        @pl.when(pl.program_id(0) == S // ts - 1)
        def _():
            swap.finish()

        dcat = _dot_nt(dm_ref[...], w_ref[...])
        dp_ref[...] = dcat[:, :POOL_WIDTH]
        do = dcat[:, POOL_WIDTH:]
        for j in range(ATTN_WIDTH // 128):
            stage[j][...] = do[:, j * 128:(j + 1) * 128]
        _store_streams(stage, do_refs, ts)
        prod = do * o_ref[...].astype(f32)
        hi = prod.astype(bf16)
        lo = (prod - hi.astype(f32)).astype(bf16)
        ones = ones_ref[...]
        for j in range(ATTN_WIDTH // 128):
            cols = slice(j * 128, (j + 1) * 128)
            dl_ref[:, cols] = _dot(hi[:, cols], ones) + _dot(lo[:, cols], ones)

    row = lambda w: pl.BlockSpec((ts, w), lambda i: (i, 0))
    res = pl.pallas_call(
        body, name="out_proj_bwd", grid=(S // ts,),
        in_specs=[row(D_MODEL), pl.BlockSpec((D_MODEL, D_MODEL), lambda i: (0, 0)), row(ATTN_WIDTH),
                  pl.BlockSpec((128, 128), lambda i: (0, 0)), ANY],
        out_specs=[row(POOL_WIDTH), row(ATTN_WIDTH)] + [_stream_spec(d, ts) for d in DILATIONS] + [ANY],
        out_shape=[jax.ShapeDtypeStruct((S, POOL_WIDTH), f32), jax.ShapeDtypeStruct((S, ATTN_WIDTH), f32)]
        + [_stream_shape(S, d) for d in DILATIONS] + [_Swap.out_shape(grads)],
        scratch_shapes=_stage_scratch(ts) + _Swap.scratch(grads),
        compiler_params=_params("arbitrary"),
    )(dmix, w_out, attn_out, head_ones, grads)
    return res[0], res[1], res[2:2 + n_dil], res[2 + n_dil]


IN_BWD_ROWS = 256


def _in_proj_bwd(du, dq, dk, dv, cos_t, sin_t, w_in, x, dx2, g1):
    S = x.shape[0]
    ts = 512

    def body(du_ref, dq_ref, dk_ref, dv_ref, cos_ref, sin_ref, w_ref, x_ref, dx2_ref, g_ref, gx_ref, dproj_ref, dg_ref):
        @pl.when(pl.program_id(0) == 0)
        def _():
            dg_ref[...] = jnp.zeros_like(dg_ref)

        first = _first_half_mask(IN_BWD_ROWS)

        def sub(i, carry):
            rows = pl.ds(pl.multiple_of(i * IN_BWD_ROWS, IN_BWD_ROWS), IN_BWD_ROWS)
            dproj_ref[rows, :POOL_WIDTH] = du_ref[rows, :]
            cos = cos_ref[rows, :]
            sin = sin_ref[rows, :]
            for j in range(ATTN_WIDTH // 128):
                cols = slice(j * 128, (j + 1) * 128)
                for base, ref in ((POOL_WIDTH, dq_ref), (POOL_WIDTH + ATTN_WIDTH, dk_ref)):
                    g = ref[rows, cols].astype(f32)
                    pre = g * cos + _rope_partner(g * sin, first)
                    dproj_ref[rows, base + j * 128: base + (j + 1) * 128] = pre.astype(bf16)
            dproj_ref[rows, POOL_WIDTH + 2 * ATTN_WIDTH:] = dv_ref[rows, :]

            dh = _dot(dproj_ref[rows, :], w_ref[...])
            xv = x_ref[rows, :]
            r = _rms(xv)
            xhat = xv * r
            dg_ref[...] += jnp.sum(dh * xhat, axis=0, keepdims=True)
            dhg = dh * g_ref[...]
            gx_ref[rows, :] = dx2_ref[rows, :] + r * (dhg - xhat * jnp.mean(dhg * xhat, axis=-1, keepdims=True))
            return carry

        lax.fori_loop(0, ts // IN_BWD_ROWS, sub, 0, unroll=True)

    row = lambda w: pl.BlockSpec((ts, w), lambda i: (i, 0))
    gain = pl.BlockSpec((1, D_MODEL), lambda i: (0, 0))
    return pl.pallas_call(
        body, name="in_proj_bwd", grid=(S // ts,),
        in_specs=[row(POOL_WIDTH)] + [row(ATTN_WIDTH)] * 3 + [row(128), row(128),
                  pl.BlockSpec((IN_WIDTH, D_MODEL), lambda i: (0, 0)), row(D_MODEL), row(D_MODEL), gain],
        out_specs=[row(D_MODEL), row(IN_WIDTH), gain],
        out_shape=[jax.ShapeDtypeStruct((S, D_MODEL), f32), jax.ShapeDtypeStruct((S, IN_WIDTH), bf16),
                   jax.ShapeDtypeStruct((1, D_MODEL), f32)],
        compiler_params=_params("arbitrary"),
    )(du, dq, dk, dv, cos_t, sin_t, w_in, x, dx2, g1)


def _matmul_tiles_tn(a, b, name):
    T, K, w = a.shape
    N = b.shape[1]
    tk = 1024

    def body(a_ref, b_ref, o_ref):
        def tiles_pass(first):
            for t in range(T):
                part = _dot_tn(a_ref[t], b_ref[...])
                if first:
                    o_ref[t * w:(t + 1) * w, :] = part
                else:
                    o_ref[t * w:(t + 1) * w, :] += part

        @pl.when(pl.program_id(0) == 0)
        def _():
            tiles_pass(True)

        @pl.when(pl.program_id(0) > 0)
        def _():
            tiles_pass(False)

    return pl.pallas_call(
        body, name=name, grid=(K // tk,),
        in_specs=[pl.BlockSpec((T, tk, w), lambda k: (0, k, 0)), pl.BlockSpec((tk, N), lambda k: (k, 0))],
        out_specs=pl.BlockSpec((T * w, N), lambda k: (0, 0)),
        out_shape=jax.ShapeDtypeStruct((T * w, N), f32),
        compiler_params=_params("arbitrary"),
    )(a, b)


def _matmul_tn(a, b, name):
    K, M = a.shape
    N = b.shape[1]
    tk = 1024

    def body(a_ref, b_ref, o_ref):
        _tn_step(a_ref, b_ref, o_ref, M)

    return pl.pallas_call(
        body, name=name, grid=(K // tk,),
        in_specs=[pl.BlockSpec((tk, M), lambda k: (k, 0)), pl.BlockSpec((tk, N), lambda k: (k, 0))],
        out_specs=pl.BlockSpec((M, N), lambda k: (0, 0)),
        out_shape=jax.ShapeDtypeStruct((M, N), f32),
        compiler_params=_params("arbitrary"),
    )(a, b)


def _tn_step(a_ref, b_ref, o_ref, M):
    w = 256

    def tiles_pass(first):
        for t in range(M // w):
            part = _dot_tn(a_ref[:, t * w:(t + 1) * w], b_ref[...])
            if first:
                o_ref[t * w:(t + 1) * w, :] = part
            else:
                o_ref[t * w:(t + 1) * w, :] += part

    @pl.when(pl.program_id(0) == 0)
    def _():
        tiles_pass(True)

    @pl.when(pl.program_id(0) > 0)
    def _():
        tiles_pass(False)


def _matmul_tn_and_small_sum(a, b, block, name):
    K, M = a.shape
    N = b.shape[1]
    tk = 1024
    n_steps = K // tk

    def body(a_ref, b_ref, block_ref, o_ref, total_ref, *scratch):
        small = _SmallSum(block_ref, *scratch)

        @pl.when(pl.program_id(0) == 0)
        def _():
            small.start()

        _tn_step(a_ref, b_ref, o_ref, M)

        @pl.when(pl.program_id(0) == n_steps - 1)
        def _():
            small.finish(total_ref)

    return pl.pallas_call(
        body, name=name, grid=(n_steps,),
        in_specs=[pl.BlockSpec((tk, M), lambda k: (k, 0)), pl.BlockSpec((tk, N), lambda k: (k, 0)), ANY],
        out_specs=[pl.BlockSpec((M, N), lambda k: (0, 0)), pl.BlockSpec(block.shape, lambda k: (0, 0))],
        out_shape=[jax.ShapeDtypeStruct((M, N), f32), jax.ShapeDtypeStruct(block.shape, block.dtype)],
        scratch_shapes=_SmallSum.scratch(block),
        compiler_params=_params("arbitrary"),
    )(a, b, block)


def _rope_tables(S):
    half = HEAD_DIM // 2
    freqs = ROPE_THETA ** (-jnp.arange(half, dtype=f32) * (2.0 / HEAD_DIM))
    ang = jnp.arange(S).astype(f32)[:, None] * freqs[None, :]
    cos = jnp.tile(jnp.cos(ang), (1, 4))
    sin = jnp.sin(ang)
    sin = jnp.tile(jnp.concatenate([-sin, sin], axis=1), (1, 2))
    return cos, sin


def _block_diag(w_pool):
    w = jnp.zeros((POOL_WIDTH, POOL_WIDTH), w_pool.dtype)
    for g in range(POOL_WIDTH // POOL_GROUP):
        w = lax.dynamic_update_slice(w, w_pool[g], (g * POOL_GROUP, g * POOL_GROUP))
    return w


def _head_ones():
    head = np.arange(128) // HEAD_DIM
    return jnp.asarray(head[:, None] == head[None, :], dtype=bf16)


def _place():
    x, y, c = lax.axis_index("x"), lax.axis_index("y"), lax.axis_index("c")
    chips = [(1 - x, y), (x, 1 - y), (1 - x, 1 - y)]
    return x, y, c, chips


ANY = pl.BlockSpec(memory_space=pl.ANY)
N_PEER_CHIPS = N_CHIPS - 1
ICI_PIECES = 4
D2D_PIECES = 8
LOCAL_PIECES = 8


def _row_chunks(rows, n, unit=32):
    units = rows // unit
    out, start = [], 0
    for i in range(n):
        size = (units // n + (1 if i < units % n else 0)) * unit
        out.append((start, size))
        start += size
    return [piece for piece in out if piece[1]]


class _LocalCopy:
    def __init__(self, src_rows, dst_rows, rows, buf, sems_in, sems_out):
        self.loads, self.stores = [], []
        for i, (start, size) in enumerate(_row_chunks(rows, LOCAL_PIECES)):
            r = pl.ds(start, size)
            self.loads.append(pltpu.make_async_copy(src_rows(r), buf.at[r], sems_in.at[i]))
            self.stores.append(pltpu.make_async_copy(buf.at[r], dst_rows(r), sems_out.at[i]))

    def start(self):
        for cp in self.loads:
            cp.start()

    def pass_on(self):
        for load, store in zip(self.loads, self.stores):
            load.wait()
            store.start()

    def finish(self):
        for store in self.stores:
            store.wait()

    @staticmethod
    def scratch(rows, dtype):
        return [pltpu.VMEM((rows, D_MODEL), dtype), pltpu.SemaphoreType.DMA((LOCAL_PIECES,)),
                pltpu.SemaphoreType.DMA((LOCAL_PIECES,))]


class _Gather:
    def __init__(self, w_ref, out_ref, send1, recv1, send2, recv2, buf, sems_in, sems_out):
        x, y, c, chips = _place()
        me = 2 * x + y
        rows = w_ref.shape[0]
        half = rows // 2
        pieces = _row_chunks(half, ICI_PIECES)
        self.own = _LocalCopy(lambda r: w_ref.at[r], lambda r: out_ref.at[me, r], rows, buf, sems_in, sems_out)

        def rows_of(core, piece):
            start, size = piece
            return pl.ds(core * half + start, size)

        self.sends, self.arrivals, self.forwards, self.forward_arrivals = [], [], [], []
        for i, piece in enumerate(pieces):
            for j, (cx, cy) in enumerate(chips):
                k = j * len(pieces) + i
                there = 2 * cx + cy

                def direct(src_chip, cx=cx, cy=cy, k=k, piece=piece):
                    return pltpu.make_async_remote_copy(
                        src_ref=w_ref.at[rows_of(c, piece)], dst_ref=out_ref.at[src_chip, rows_of(c, piece)],
                        send_sem=send1.at[k], recv_sem=recv1.at[k], device_id=(cx, cy, c), device_id_type=MESH)

                def passed(core, there=there, k=k, piece=piece):
                    return pltpu.make_async_remote_copy(
                        src_ref=out_ref.at[there, rows_of(core, piece)], dst_ref=out_ref.at[there, rows_of(core, piece)],
                        send_sem=send2.at[k], recv_sem=recv2.at[k], device_id=(x, y, 1 - c), device_id_type=MESH)

                self.sends.append(direct(me))
                self.arrivals.append(direct(there))
                self.forwards.append(passed(c))
                self.forward_arrivals.append(passed(1 - c))

    def start(self):
        for cp in self.sends:
            cp.start()
        self.own.start()

    def pass_on(self):
        self.own.pass_on()
        for arrival, forward in zip(self.arrivals, self.forwards):
            arrival.wait_recv()
            forward.start()

    def finish(self):
        for arrival in self.forward_arrivals:
            arrival.wait_recv()
        for cp in self.sends + self.forwards:
            cp.wait_send()
        self.own.finish()

    @staticmethod
    def scratch(rows, dtype):
        n = N_PEER_CHIPS * len(_row_chunks(rows // 2, ICI_PIECES))
        return [pltpu.SemaphoreType.DMA((n,))] * 4 + _LocalCopy.scratch(rows, dtype)

    @staticmethod
    def out_shape(rows, dtype):
        return jax.ShapeDtypeStruct((N_CHIPS, rows, D_MODEL), dtype)


def _gather_weights(pack):
    rows = pack.shape[0]

    def body(w_ref, out_ref, *scratch):
        gather = _Gather(w_ref, out_ref, *scratch)
        gather.start()
        gather.pass_on()
        gather.finish()

    return pl.pallas_call(
        body, name="gather_weights", in_specs=[ANY], out_specs=ANY, out_shape=_Gather.out_shape(rows, pack.dtype),
        scratch_shapes=_Gather.scratch(rows, pack.dtype),
        compiler_params=pltpu.CompilerParams(vmem_limit_bytes=VMEM_LIMIT_V7X),
    )(pack)


class _Scatter:
    def __init__(self, h_ref, out_ref, send, recv):
        x, y, c, chips = _place()
        pieces = _row_chunks(h_ref.shape[1], ICI_PIECES)
        self.copies = []
        for i, (start, size) in enumerate(pieces):
            for j, (cx, cy) in enumerate(chips):
                k = j * len(pieces) + i
                self.copies.append(pltpu.make_async_remote_copy(
                    src_ref=h_ref.at[2 * cx + cy, pl.ds(start, size)], dst_ref=out_ref.at[j, pl.ds(start, size)],
                    send_sem=send.at[k], recv_sem=recv.at[k], device_id=(cx, cy, c), device_id_type=MESH))

    def start(self):
        for cp in self.copies:
            cp.start()

    def finish(self):
        for cp in self.copies:
            cp.wait_recv()
        for cp in self.copies:
            cp.wait_send()

    @staticmethod
    def scratch(half):
        n = N_PEER_CHIPS * len(_row_chunks(half, ICI_PIECES))
        return [pltpu.SemaphoreType.DMA((n,))] * 2

    @staticmethod
    def out_shape(half, dtype):
        return jax.ShapeDtypeStruct((N_PEER_CHIPS, half, D_MODEL), dtype)


def _scatter_to_chips(h):
    half = h.shape[1]

    def body(h_ref, out_ref, send, recv):
        scatter = _Scatter(h_ref, out_ref, send, recv)
        scatter.start()
        scatter.finish()

    return pl.pallas_call(
        body, name="scatter_to_chips", in_specs=[ANY], out_specs=ANY, out_shape=_Scatter.out_shape(half, h.dtype),
        scratch_shapes=_Scatter.scratch(half),
    )(h)


class _Swap:
    def __init__(self, g_ref, theirs_ref, send, recv):
        x, y, c, _ = _place()
        half = g_ref.shape[1] // 2
        pieces = _row_chunks(half, D2D_PIECES)
        self.copies = []
        for s in range(N_CHIPS):
            for i, (start, size) in enumerate(pieces):
                k = s * len(pieces) + i
                self.copies.append(pltpu.make_async_remote_copy(
                    src_ref=g_ref.at[s, pl.ds((1 - c) * half + start, size)], dst_ref=theirs_ref.at[s, pl.ds(start, size)],
                    send_sem=send.at[k], recv_sem=recv.at[k], device_id=(x, y, 1 - c), device_id_type=MESH))

    def start(self):
        for cp in self.copies:
            cp.start()

    def finish(self):
        for cp in self.copies:
            cp.wait()

    @staticmethod
    def scratch(g):
        n = N_CHIPS * len(_row_chunks(g.shape[1] // 2, D2D_PIECES))
        return [pltpu.SemaphoreType.DMA((n,))] * 2

    @staticmethod
    def out_shape(g):
        return jax.ShapeDtypeStruct((N_CHIPS, g.shape[1] // 2, D_MODEL), g.dtype)


def _swap_halves(g):
    def body(g_ref, theirs_ref, send, recv):
        swap = _Swap(g_ref, theirs_ref, send, recv)
        swap.start()
        swap.finish()

    return pl.pallas_call(
        body, name="swap_halves", in_specs=[ANY], out_specs=ANY, out_shape=_Swap.out_shape(g),
        scratch_shapes=_Swap.scratch(g),
    )(g)


ADD_TILE_MAX_ROWS = 600


def _add_tile(half):
    return max(t for t in range(8, ADD_TILE_MAX_ROWS + 1, 8) if half % t == 0)


def _add_cores(g, theirs, name, out_dtype=f32):
    half = theirs.shape[1]
    tr = _add_tile(half)
    n_t = half // tr

    def body(c_ref, g_ref, t_ref, o_ref):
        o_ref[...] = (g_ref[...] + t_ref[...]).astype(out_dtype)

    blk = pl.BlockSpec((1, tr, D_MODEL), lambda s, t, c_ref: (s, t, 0))
    return pl.pallas_call(
        body, name=name,
        grid_spec=pltpu.PrefetchScalarGridSpec(
            num_scalar_prefetch=1, grid=(N_CHIPS, n_t),
            in_specs=[pl.BlockSpec((1, tr, D_MODEL), lambda s, t, c_ref: (s, c_ref[0] * n_t + t, 0)), blk],
            out_specs=blk),
        out_shape=jax.ShapeDtypeStruct(theirs.shape, out_dtype),
        compiler_params=_params("parallel", "parallel"),
    )(lax.axis_index("c").astype(jnp.int32).reshape(1), g, theirs)


def _add_chips(chip_sum, others, name):
    half = chip_sum.shape[1]
    tr = _add_tile(half)

    def body(me_ref, own_ref, o0, o1, o2, out_ref):
        out_ref[...] = ((own_ref[0].astype(f32) + o0[0].astype(f32)) + o1[0].astype(f32)) + o2[0].astype(f32)

    other = lambda j: pl.BlockSpec((1, tr, D_MODEL), lambda t, me_ref: (j, t, 0))
    return pl.pallas_call(
        body, name=name,
        grid_spec=pltpu.PrefetchScalarGridSpec(
            num_scalar_prefetch=1, grid=(half // tr,),
            in_specs=[pl.BlockSpec((1, tr, D_MODEL), lambda t, me_ref: (me_ref[0], t, 0)), other(0), other(1), other(2)],
            out_specs=pl.BlockSpec((tr, D_MODEL), lambda t, me_ref: (t, 0))),
        out_shape=jax.ShapeDtypeStruct((half, D_MODEL), f32),
        compiler_params=_params("parallel"),
    )((2 * lax.axis_index("x") + lax.axis_index("y")).astype(jnp.int32).reshape(1), chip_sum, others, others, others)


def _join_halves(parts):
    n_parts = len(parts)
    pieces = [_row_chunks(r.shape[0], D2D_PIECES) for r in parts]
    first = [sum(len(p) for p in pieces[:i]) for i in range(n_parts)]
    n = sum(len(p) for p in pieces)

    def body(*refs):
        r_refs, out_refs = refs[:n_parts], refs[n_parts:2 * n_parts]
        send, recv = refs[2 * n_parts:2 * n_parts + 2]
        local = refs[2 * n_parts + 2:]
        x, y, c, _ = _place()
        owns = [_LocalCopy(lambda rr, r_ref=r_ref: r_ref.at[rr], lambda rr, out_ref=out_ref: out_ref.at[c, rr],
                           r_ref.shape[0], *local[3 * i:3 * i + 3])
                for i, (r_ref, out_ref) in enumerate(zip(r_refs, out_refs))]
        for own in owns:
            own.start()

        def piece(i, j, core):
            start, size = pieces[i][j]
            return pltpu.make_async_remote_copy(
                src_ref=r_refs[i].at[pl.ds(start, size)], dst_ref=out_refs[i].at[core, pl.ds(start, size)],
                send_sem=send.at[first[i] + j], recv_sem=recv.at[first[i] + j],
                device_id=(x, y, 1 - c), device_id_type=MESH)

        every = [(i, j) for i in range(n_parts) for j in range(len(pieces[i]))]
        copies = [piece(i, j, c) for i, j in every]
        for cp in copies:
            cp.start()
        for own in owns:
            own.pass_on()
        for i, j in every:
            piece(i, j, 1 - c).wait_recv()
        for cp in copies:
            cp.wait_send()
        for own in owns:
            own.finish()

    local_scratch = []
    for r in parts:
        local_scratch += _LocalCopy.scratch(r.shape[0], r.dtype)
    return pl.pallas_call(
        body, name="join_halves", in_specs=[ANY] * n_parts, out_specs=[ANY] * n_parts,
        out_shape=[jax.ShapeDtypeStruct((2,) + r.shape, r.dtype) for r in parts],
        scratch_shapes=[pltpu.SemaphoreType.DMA((n,))] * 2 + local_scratch,
        compiler_params=pltpu.CompilerParams(vmem_limit_bytes=VMEM_LIMIT_V7X),
    )(*parts)


class _SmallSum:
    def __init__(self, b_ref, gathered, send, recv, local_sem):
        x, y, c, _ = _place()
        me = 4 * x + 2 * y + c
        self.gathered = gathered
        self.own = pltpu.make_async_copy(b_ref, gathered.at[me], local_sem)
        self.sends, self.arrivals = [], []
        for kk in range(1, N_DEV):
            flip = lambda v, bit: 1 - v if bit else v
            peer = (flip(x, kk & 4), flip(y, kk & 2), flip(c, kk & 1))
            self.sends.append(pltpu.make_async_remote_copy(
                src_ref=b_ref, dst_ref=gathered.at[me], send_sem=send.at[kk - 1], recv_sem=recv.at[kk - 1],
                device_id=peer, device_id_type=MESH))
            self.arrivals.append(pltpu.make_async_remote_copy(
                src_ref=b_ref, dst_ref=gathered.at[jnp.bitwise_xor(me, kk)], send_sem=send.at[kk - 1],
                recv_sem=recv.at[kk - 1], device_id=peer, device_id_type=MESH))

    def start(self):
        self.own.start()
        for cp in self.sends:
            cp.start()

    def finish(self, out_ref):
        self.own.wait()
        for cp in self.arrivals:
            cp.wait_recv()
        for cp in self.sends:
            cp.wait_send()
        acc = self.gathered[0]
        for dev in range(1, N_DEV):
            acc = acc + self.gathered[dev]
        out_ref[...] = acc

    @staticmethod
    def scratch(block):
        return [pltpu.VMEM((N_DEV,) + block.shape, block.dtype), pltpu.SemaphoreType.DMA((N_DEV - 1,)),
                pltpu.SemaphoreType.DMA((N_DEV - 1,)), pltpu.SemaphoreType.DMA]


def _adamw(w, g, m, v, name):
    rows, cols = w.shape
    tr = max(t for t in range(8, 513, 8) if rows % t == 0)
    c1 = 1.0 - ADAM_B1 ** ADAM_STEP
    c2 = 1.0 - ADAM_B2 ** ADAM_STEP

    def body(w_ref, g_ref, m_ref, v_ref, d_ref, nm_ref, nv_ref):
        gv = g_ref[...]
        nm = ADAM_B1 * m_ref[...] + (1.0 - ADAM_B1) * gv
        nv = ADAM_B2 * v_ref[...] + (1.0 - ADAM_B2) * (gv * gv)
        nm_ref[...] = nm
        nv_ref[...] = nv
        d_ref[...] = -ADAM_LR * ((nm / c1) / (jnp.sqrt(nv / c2) + ADAM_EPS) + ADAM_WD * w_ref[...])

    blk = pl.BlockSpec((tr, cols), lambda i: (i, 0))
    shape = jax.ShapeDtypeStruct((rows, cols), f32)
    return pl.pallas_call(
        body, name=name, grid=(rows // tr,), in_specs=[blk] * 4, out_specs=[blk] * 3, out_shape=[shape] * 3,
        compiler_params=_params("parallel"),
    )(w, g, m, v)


LARGE = ("w_in", "w_out", "w_gate", "w_up", "w_down")
SMALL = ("ln_pre_mix", "ln_post_mix", "ln_pre_ffn", "ln_post_ffn", "pool_scale", "w_pool")
SHARD_ROWS = {"w_in": 640, "w_out": 256, "w_gate": 704, "w_up": 704, "w_down": 704}
COLUMN_SHARDED = ("w_in", "w_gate", "w_up")
UPDATED_TRANSPOSED = ("w_gate", "w_up")
NEEDED_FIRST = ("w_in",)
NEEDED_LATER = ("w_out", "w_gate", "w_up", "w_down")
READY_EARLY = ("w_out", "w_gate", "w_up", "w_down")
READY_LATE = ("w_in",)


def _pack_shard(shards, names):
    return jnp.concatenate([shards[n].T if n in COLUMN_SHARDED else shards[n] for n in names], axis=0)


def _unpack_shard(pack, names):
    out, row = {}, 0
    for n in names:
        out[n] = pack[row:row + SHARD_ROWS[n]]
        row += SHARD_ROWS[n]
    return out


def _whole_from_shards(packs, names):
    out, row = {}, 0
    for n in names:
        rows = SHARD_ROWS[n]
        out[n] = packs[:, row:row + rows].reshape(N_CHIPS * rows, D_MODEL)
        row += rows
    return out


def _shards_from_whole(grads, names):
    return jnp.concatenate([grads[n].reshape(N_CHIPS, SHARD_ROWS[n], D_MODEL) for n in names], axis=1)


def _pack_small(vals):
    rows = [vals[n].reshape(1, D_MODEL) for n in SMALL[:4]]
    rows.append(jnp.pad(vals["pool_scale"].reshape(1, POOL_WIDTH), ((0, 0), (0, D_MODEL - POOL_WIDTH))))
    rows.append(jnp.pad(vals["loss"].reshape(1, 1), ((0, 0), (0, D_MODEL - 1))))
    rows.append(jnp.zeros((2, D_MODEL), f32))
    rows.append(vals["w_pool"].reshape(16, D_MODEL))
    return jnp.concatenate(rows, axis=0)


def _unpack_small(block):
    out = {n: block[i:i + 1] for i, n in enumerate(SMALL[:4])}
    out["pool_scale"] = block[4:5, :POOL_WIDTH]
    out["loss"] = block[5, 0]
    out["w_pool"] = block[8:24].reshape(1, 4, POOL_GROUP, POOL_GROUP)
    return out


def kernel(x, ln_pre_mix, w_in, w_pool, pool_scale, w_out, ln_post_mix, ln_pre_ffn, w_gate, w_up, w_down, ln_post_ffn, loss_target, m_ln_pre_mix, m_w_in, m_w_pool, m_pool_scale, m_w_out, m_ln_post_mix, m_ln_pre_ffn, m_w_gate, m_w_up, m_w_down, m_ln_post_ffn, v_ln_pre_mix, v_w_in, v_w_pool, v_pool_scale, v_w_out, v_ln_post_mix, v_ln_pre_ffn, v_w_gate, v_w_up, v_w_down, v_ln_post_ffn):
    w = dict(ln_pre_mix=ln_pre_mix, w_in=w_in, w_pool=w_pool, pool_scale=pool_scale, w_out=w_out,
             ln_post_mix=ln_post_mix, ln_pre_ffn=ln_pre_ffn, w_gate=w_gate, w_up=w_up, w_down=w_down,
             ln_post_ffn=ln_post_ffn)
    m = dict(ln_pre_mix=m_ln_pre_mix, w_in=m_w_in, w_pool=m_w_pool, pool_scale=m_pool_scale, w_out=m_w_out,
             ln_post_mix=m_ln_post_mix, ln_pre_ffn=m_ln_pre_ffn, w_gate=m_w_gate, w_up=m_w_up, w_down=m_w_down,
             ln_post_ffn=m_ln_post_ffn)
    v = dict(ln_pre_mix=v_ln_pre_mix, w_in=v_w_in, w_pool=v_w_pool, pool_scale=v_pool_scale, w_out=v_w_out,
             ln_post_mix=v_ln_post_mix, ln_pre_ffn=v_ln_pre_ffn, w_gate=v_w_gate, w_up=v_w_up, w_down=v_w_down,
             ln_post_ffn=v_ln_post_ffn)

    xs, target = x[0], loss_target[0]
    cos_t, sin_t = _rope_tables(xs.shape[0])
    w_bd = _block_diag(w_pool[0]).astype(bf16)
    shard = {n: w[n][0].astype(bf16) for n in LARGE}

    w_in_whole = _whole_from_shards(_gather_weights(_pack_shard(shard, NEEDED_FIRST)), NEEDED_FIRST)["w_in"]
    h1, u, qs, ks, vs = _in_proj(xs, ln_pre_mix, w_in_whole, cos_t, sin_t)
    pool_out = _pool_fwd(u, w_bd, pool_scale)
    attn_out, lse, later = _attn_fwd(qs, ks, vs, _pack_shard(shard, NEEDED_LATER))
    whole = _whole_from_shards(later, NEEDED_LATER)
    mix, x2, h2 = _out_proj(pool_out, attn_out, whole["w_out"], xs, ln_post_mix, ln_pre_ffn)
    gate, up, f = _ffn_fwd(h2, whole["w_gate"], whole["w_up"], whole["w_down"])
    dy, df, dg4, loss = _loss_head(f, x2, target, ln_post_ffn)

    large = {}
    a, dgate, dup, dh2 = _ffn_bwd(df, gate, up, whole["w_gate"], whole["w_up"], whole["w_down"])
    large["w_down"] = _matmul_tiles_tn(a, df, "grad_w_down")
    large["w_gate"] = _matmul_tiles_tn(dgate, h2, "grad_w_gate")
    large["w_up"] = _matmul_tiles_tn(dup, h2, "grad_w_up")
    dx2, dmix, dg3, dg2 = _norm_bwd(dh2, dy, x2, mix, ln_pre_ffn, ln_post_mix)
    large["w_out"] = jnp.concatenate([_matmul_tn(pool_out, dmix, "grad_w_out_pool"),
                                      _matmul_tn(attn_out, dmix, "grad_w_out_attn")], axis=0)
    early = _shards_from_whole(large, READY_EARLY)
    dpool, delta, dos, early_theirs = _out_proj_bwd(dmix, whole["w_out"], attn_out, _head_ones(), early)
    early_chip = _add_cores(early, early_theirs, "add_cores_early")
    du, d_w_bd, d_scale = _pool_bwd(u, dpool, w_bd, pool_scale)
    dq, dk, dv, early_others = _attn_bwd(qs, ks, vs, dos, lse, delta, early_chip)
    grad_x, dproj, dg1 = _in_proj_bwd(du, dq, dk, dv, cos_t, sin_t, w_in_whole, xs, dx2, ln_pre_mix)
    d_w_pool = jnp.stack([d_w_bd[g * POOL_GROUP:(g + 1) * POOL_GROUP, g * POOL_GROUP:(g + 1) * POOL_GROUP]
                          for g in range(POOL_WIDTH // POOL_GROUP)])
    small = dict(ln_pre_mix=dg1, ln_post_mix=dg2, ln_pre_ffn=dg3, ln_post_ffn=dg4, pool_scale=d_scale, w_pool=d_w_pool)
    large["w_in"], small_total = _matmul_tn_and_small_sum(dproj, h1, _pack_small(dict(small, loss=loss)), "grad_w_in")
    late = _shards_from_whole(large, READY_LATE)
    late_chip = _add_cores(late, _swap_halves(late), "add_cores_late", bf16)
    late_others = _scatter_to_chips(late_chip)
    early_half = _add_chips(early_chip, early_others, "add_chips_early")
    late_half = _add_chips(late_chip, late_others, "add_chips_late")
    early_whole, late_whole = _join_halves([early_half, late_half])
    grads = _unpack_shard(early_whole.reshape(-1, D_MODEL), READY_EARLY)
    grads.update(_unpack_shard(late_whole.reshape(-1, D_MODEL), READY_LATE))

    total = _unpack_small(small_total)
    for n in SMALL:
        grads[n] = total[n]

    delta_w, new_m, new_v = {}, {}, {}
    for n in LARGE:
        if n in UPDATED_TRANSPOSED:
            update = _adamw(w[n][0].T, grads[n], m[n][0].T, v[n][0].T, "adamw_" + n)
            delta_w[n], new_m[n], new_v[n], grads[n] = [a.T for a in (*update, grads[n])]
        else:
            if n in COLUMN_SHARDED:
                grads[n] = grads[n].T
            delta_w[n], new_m[n], new_v[n] = _adamw(w[n][0], grads[n], m[n][0], v[n][0], "adamw_" + n)
    small_state = [_pack_small(dict({n: s[n] for n in SMALL}, loss=jnp.zeros((), f32))) for s in (w, m, v)]
    small_grad = _pack_small(dict({n: grads[n] for n in SMALL}, loss=jnp.zeros((), f32)))
    sd, sm, sv = _adamw(small_state[0], small_grad, small_state[1], small_state[2], "adamw_small")
    for out, block in ((delta_w, sd), (new_m, sm), (new_v, sv)):
        un = _unpack_small(block)
        for n in SMALL:
            out[n] = un[n]

    names = ("ln_pre_mix", "w_in", "w_pool", "pool_scale", "w_out", "ln_post_mix", "ln_pre_ffn", "w_gate", "w_up",
             "w_down", "ln_post_ffn")
    full = lambda d: [d[n].reshape(w[n].shape) for n in names]
    return (total["loss"], grad_x[None], *full(grads), *full(delta_w), *full(new_m), *full(new_v))
```

```python
import numpy as np
import jax
import jax.numpy as jnp
from jax import lax
from jax.experimental import pallas as pl
from jax.experimental.pallas import tpu as pltpu

D_MODEL = 1024
POOL_WIDTH = 256
POOL_GROUP = 64
ATTN_WIDTH = 768
HEAD_DIM = 64
IN_WIDTH = 2560
D_FF = 2816
BLOCK = 128
DILATIONS = (1, 4, 16)
ROPE_THETA = 10000.0
EPS = 1e-6
ATTN_SCALE = 0.125
NEG = -1e30

ADAM_LR = 0.001
ADAM_B1 = 0.9
ADAM_B2 = 0.999
ADAM_EPS = 1e-08
ADAM_WD = 0.01
ADAM_STEP = 10

N_CHIPS = 4
N_DEV = 8
VMEM_LIMIT_V7X = 56 * 1024 * 1024
MESH = pl.DeviceIdType.MESH

f32 = jnp.float32
bf16 = jnp.bfloat16


def _params(*sem):
    return pltpu.CompilerParams(dimension_semantics=sem, vmem_limit_bytes=VMEM_LIMIT_V7X)


def _dot(a, b):
    return jnp.dot(a, b, preferred_element_type=f32)


def _dot_nt(a, b):
    return lax.dot_general(a, b, (((1,), (1,)), ((), ())), preferred_element_type=f32)


def _dot_tn(a, b):
    return lax.dot_general(a, b, (((0,), (0,)), ((), ())), preferred_element_type=f32)


def _rope_partner(a, first_half):
    return jnp.where(first_half, pltpu.roll(a, 96, 1), pltpu.roll(a, 32, 1))


def _first_half_mask(rows):
    lane = lax.broadcasted_iota(jnp.int32, (rows, 128), 1)
    return (lane % HEAD_DIM) < (HEAD_DIM // 2)


def _stream_spec(d, ts):
    return pl.BlockSpec((d, ts // d, ATTN_WIDTH), lambda i: (0, i, 0))


def _stream_shape(S, d):
    return jax.ShapeDtypeStruct((d, S // d, ATTN_WIDTH), bf16)


N_STAGE = ATTN_WIDTH // 128


def _stage_scratch(ts):
    return [pltpu.VMEM((ts, 128), f32)] * N_STAGE


def _store_streams(stage, out_refs, ts):
    for d, ref in zip(DILATIONS, out_refs):
        for r in range(d):
            rows = pl.ds(0, ts) if d == 1 else pl.ds(r, ts // d, stride=d)
            for j in range(N_STAGE):
                ref[r, :, j * 128:(j + 1) * 128] = stage[j][rows, :].astype(bf16)


def _in_proj(x, g1, w_in, cos_t, sin_t):
    S = x.shape[0]
    ts = 512

    def body(x_ref, g_ref, w_ref, cos_ref, sin_ref, h_ref, u_ref, *rest):
        outs, stage = rest[:-N_STAGE], rest[-N_STAGE:]
        xv = x_ref[...]
        r = lax.rsqrt(jnp.mean(xv * xv, axis=-1, keepdims=True) + EPS)
        h = ((xv * r) * g_ref[...]).astype(bf16)
        h_ref[...] = h
        proj = _dot_nt(h, w_ref[...])
        u_ref[...] = proj[:, :POOL_WIDTH]
        cos = cos_ref[...]
        sin = sin_ref[...]
        first = _first_half_mask(ts)
        n_dil = len(DILATIONS)
        for which, base in enumerate((POOL_WIDTH, POOL_WIDTH + ATTN_WIDTH)):
            for j in range(ATTN_WIDTH // 128):
                a = proj[:, base + j * 128: base + (j + 1) * 128]
                if which == 0:
                    a = a * ATTN_SCALE
                stage[j][...] = a * cos + _rope_partner(a, first) * sin
            _store_streams(stage, outs[which * n_dil:(which + 1) * n_dil], ts)
        for j in range(ATTN_WIDTH // 128):
            base = POOL_WIDTH + 2 * ATTN_WIDTH + j * 128
            stage[j][...] = proj[:, base:base + 128]
        _store_streams(stage, outs[2 * n_dil:], ts)

    row = lambda w: pl.BlockSpec((ts, w), lambda i: (i, 0))
    streams = [_stream_spec(d, ts) for d in DILATIONS] * 3
    res = pl.pallas_call(
        body, name="in_proj", grid=(S // ts,),
        in_specs=[row(D_MODEL), pl.BlockSpec((1, D_MODEL), lambda i: (0, 0)),
                  pl.BlockSpec((IN_WIDTH, D_MODEL), lambda i: (0, 0)), row(128), row(128)],
        out_specs=[row(D_MODEL), row(POOL_WIDTH)] + streams,
        out_shape=[jax.ShapeDtypeStruct((S, D_MODEL), bf16), jax.ShapeDtypeStruct((S, POOL_WIDTH), f32)]
        + [_stream_shape(S, d) for d in DILATIONS] * 3,
        scratch_shapes=_stage_scratch(ts),
        compiler_params=_params("parallel"),
    )(x, g1, w_in, cos_t, sin_t)
    n = len(DILATIONS)
    return res[0], res[1], res[2:2 + n], res[2 + n:2 + 2 * n], res[2 + 2 * n:]


POOL_HALO = 16


def _pool_lane_group(rows):
    return lax.broadcasted_iota(jnp.int32, (rows, POOL_WIDTH), 1) // POOL_GROUP


def _pool_select(group, s2, s4, s8, s16):
    return jnp.where(group == 0, s2, jnp.where(group == 1, s4, jnp.where(group == 2, s8, s16)))


def _pool_count(t0, rows):
    group = _pool_lane_group(rows)
    t = t0 + lax.broadcasted_iota(jnp.int32, (rows, POOL_WIDTH), 0)
    win = _pool_select(group, 2, 4, 8, 16)
    return jnp.minimum(t + 1, win).astype(f32)


def _pool_diff(u_halo, u_tile, t0):
    ts = u_tile.shape[0]
    ext = jnp.concatenate([u_halo, u_tile], axis=0)
    s2 = ext + pltpu.roll(ext, 1, 0)
    s4 = s2 + pltpu.roll(s2, 2, 0)
    s8 = s4 + pltpu.roll(s4, 4, 0)
    s16 = s8 + pltpu.roll(s8, 8, 0)
    group = _pool_lane_group(ts + POOL_HALO)
    wsum = _pool_select(group, s2, s4, s8, s16)[POOL_HALO:]
    return wsum / _pool_count(t0, ts) - u_tile


def _pool_specs(ts, n_tiles):
    tile = pl.BlockSpec((ts, POOL_WIDTH), lambda i: (i, 0))
    per = ts // POOL_HALO
    before = pl.BlockSpec((POOL_HALO, POOL_WIDTH), lambda i: (jnp.maximum(i * per - 1, 0), 0))
    after = pl.BlockSpec((POOL_HALO, POOL_WIDTH), lambda i: (jnp.minimum((i + 1) * per, n_tiles * per - 1), 0))
    return tile, before, after


def _pool_fwd(u, w_bd, scale):
    S = u.shape[0]
    ts = 2048
    n_tiles = S // ts

    def body(u_ref, halo_ref, w_ref, sc_ref, y_ref):
        i = pl.program_id(0)
        halo = jnp.where(i > 0, halo_ref[...], 0.0)
        d = _pool_diff(halo, u_ref[...], i * ts)
        y_ref[...] = (_dot(d.astype(bf16), w_ref[...]) * sc_ref[...]).astype(bf16)

    tile, before, _ = _pool_specs(ts, n_tiles)
    return pl.pallas_call(
        body, name="pool_fwd", grid=(n_tiles,),
        in_specs=[tile, before, pl.BlockSpec((POOL_WIDTH, POOL_WIDTH), lambda i: (0, 0)),
                  pl.BlockSpec((1, POOL_WIDTH), lambda i: (0, 0))],
        out_specs=tile, out_shape=jax.ShapeDtypeStruct((S, POOL_WIDTH), bf16),
        compiler_params=_params("parallel"),
    )(u, u, w_bd, scale)


def _pool_bwd(u, dy, w_bd, scale):
    S = u.shape[0]
    ts = 2048
    n_tiles = S // ts

    def body(u_ref, halo_ref, dy_ref, dy_next_ref, w_ref, sc_ref, du_ref, dw_ref, dsc_ref):
        i = pl.program_id(0)

        @pl.when(i == 0)
        def _():
            dw_ref[...] = jnp.zeros_like(dw_ref)
            dsc_ref[...] = jnp.zeros_like(dsc_ref)

        halo = jnp.where(i > 0, halo_ref[...], 0.0)
        d = _pool_diff(halo, u_ref[...], i * ts).astype(bf16)
        w = w_ref[...]
        sc = sc_ref[...]
        dy_tile = dy_ref[...]
        z = _dot(d, w)
        dsc_ref[...] += jnp.sum(dy_tile * z, axis=0, keepdims=True)
        dy_next = jnp.where(i < n_tiles - 1, dy_next_ref[...], 0.0)
        dz = (jnp.concatenate([dy_tile, dy_next], axis=0) * sc).astype(bf16)
        dw_ref[...] += _dot_tn(d, dz[:ts])
        dd = _dot_nt(dz, w)
        e = dd / _pool_count(i * ts, ts + POOL_HALO)
        n = ts + POOL_HALO
        f2 = e + pltpu.roll(e, n - 1, 0)
        f4 = f2 + pltpu.roll(f2, n - 2, 0)
        f8 = f4 + pltpu.roll(f4, n - 4, 0)
        f16 = f8 + pltpu.roll(f8, n - 8, 0)
        fsum = _pool_select(_pool_lane_group(n), f2, f4, f8, f16)
        du_ref[...] = (fsum[:ts] - dd[:ts]).astype(bf16)

    tile, before, after = _pool_specs(ts, n_tiles)
    return pl.pallas_call(
        body, name="pool_bwd", grid=(n_tiles,),
        in_specs=[tile, before, tile, after, pl.BlockSpec((POOL_WIDTH, POOL_WIDTH), lambda i: (0, 0)),
                  pl.BlockSpec((1, POOL_WIDTH), lambda i: (0, 0))],
        out_specs=[tile, pl.BlockSpec((POOL_WIDTH, POOL_WIDTH), lambda i: (0, 0)),
                   pl.BlockSpec((1, POOL_WIDTH), lambda i: (0, 0))],
        out_shape=[jax.ShapeDtypeStruct((S, POOL_WIDTH), bf16), jax.ShapeDtypeStruct((POOL_WIDTH, POOL_WIDTH), f32),
                   jax.ShapeDtypeStruct((1, POOL_WIDTH), f32)],
        compiler_params=_params("arbitrary"),
    )(u, u, dy, dy, w_bd, scale)


SUPER = BLOCK * DILATIONS[-1]
UNITS = SUPER // BLOCK
FWD_UNROLL = 16
BWD_UNROLL = 16


def _band_mask(has_prev):
    qi = lax.broadcasted_iota(jnp.int32, (BLOCK, 2 * BLOCK), 0)
    kj = lax.broadcasted_iota(jnp.int32, (BLOCK, 2 * BLOCK), 1)
    return (kj >= qi) & (kj <= qi + BLOCK) & ((kj >= BLOCK) | has_prev)


def _head0_mask(rows=BLOCK):
    return lax.broadcasted_iota(jnp.int32, (rows, 128), 1) < HEAD_DIM


def _band_mask_t(has_prev):
    ki = lax.broadcasted_iota(jnp.int32, (2 * BLOCK, 2 * BLOCK), 0)
    qj = lax.broadcasted_iota(jnp.int32, (2 * BLOCK, 2 * BLOCK), 1) % BLOCK
    return (ki >= qj) & (ki <= qj + BLOCK) & ((ki >= BLOCK) | has_prev)


def _head_pair_rows(a, h0):
    zero = jnp.zeros_like(a)
    return jnp.concatenate([jnp.where(h0, a, zero), jnp.where(h0, zero, a)], axis=0)


def _per_query_row(stat):
    t = stat.T
    return jnp.concatenate([t[0:1], t[HEAD_DIM:HEAD_DIM + 1]], axis=1)


def _natural_rows(d, r, n):
    if d == 1:
        return pl.ds(pl.multiple_of(n * BLOCK, BLOCK), BLOCK)
    return pl.ds(n * (BLOCK * d) + r, BLOCK, stride=d)


def _unit_place(d, u):
    per_stream = UNITS // d
    return u // per_stream, u % per_stream, per_stream


def _block_rows(n):
    return pl.ds(pl.multiple_of(n * BLOCK, BLOCK), BLOCK)


def _band(cur_ref, tail_ref, r, n):
    before = jnp.where(n > 0, cur_ref[r, _block_rows(jnp.maximum(n - 1, 0)), :], tail_ref[r])
    return jnp.concatenate([before, cur_ref[r, _block_rows(n), :]], axis=0)


def _attn_in_specs(S, with_do):
    specs = []
    last = S // SUPER - 1
    for d in DILATIONS:
        per_stream = UNITS // d
        cur = pl.BlockSpec((d, SUPER // d, 128), lambda hp, sb: (0, jnp.minimum(sb, last), hp))
        tail = pl.BlockSpec(
            (d, BLOCK, 128),
            lambda hp, sb, per_stream=per_stream: (0, jnp.maximum(jnp.minimum(sb, last) * per_stream - 1, 0), hp))
        specs += [cur] * (2 if with_do else 1) + [cur, tail, cur, tail]
    return specs


def _attn_fwd(qs, ks, vs, pack):
    S = qs[0].shape[1]
    n_dil = len(DILATIONS)
    n_steps = S // SUPER
    n_total = (ATTN_WIDTH // 128) * n_steps

    def body(*refs):
        ins, pack_ref = refs[:5 * n_dil], refs[5 * n_dil]
        out_ref, lse_ref, gathered_ref = refs[5 * n_dil + 1:5 * n_dil + 4]
        scratch = refs[5 * n_dil + 4:]
        o_sc, l_sc = scratch[:n_dil], scratch[n_dil:2 * n_dil]
        gather = _Gather(pack_ref, gathered_ref, *scratch[2 * n_dil:])
        sb = pl.program_id(1)
        step = pl.program_id(0) * n_steps + sb

        @pl.when(step == 0)
        def _():
            gather.start()

        h0 = _head0_mask()
        for ci, d in enumerate(DILATIONS):
            q_ref, kc_ref, kp_ref, vc_ref, vp_ref = ins[5 * ci:5 * ci + 5]

            def unit(u, carry, d=d, ci=ci, q_ref=q_ref, kc_ref=kc_ref, kp_ref=kp_ref, vc_ref=vc_ref, vp_ref=vp_ref):
                r, n, _ = _unit_place(d, u)
                qv = q_ref[r, _block_rows(n), :]
                kb = _band(kc_ref, kp_ref, r, n)
                vb = _band(vc_ref, vp_ref, r, n)
                valid = _band_mask((sb > 0) | (n > 0))
                s = jnp.where(jnp.concatenate([valid, valid], axis=0), _dot_nt(_head_pair_rows(qv, h0), kb), NEG)
                m = jnp.max(s, axis=1, keepdims=True)
                e = jnp.exp(s - m)
                den = jnp.sum(e, axis=1, keepdims=True)
                o_pair = _dot(e.astype(bf16), vb) * (1.0 / den)
                lse_pair = jnp.broadcast_to(m + jnp.log(den), (2 * BLOCK, 128))
                rows = _natural_rows(d, r, n)
                o_sc[ci][rows, :] = jnp.where(h0, o_pair[:BLOCK], o_pair[BLOCK:])
                l_sc[ci][rows, :] = jnp.where(h0, lse_pair[:BLOCK], lse_pair[BLOCK:])
                return carry

            lax.fori_loop(0, UNITS, unit, 0, unroll=FWD_UNROLL)

        def merge(t, carry):
            rows = pl.ds(pl.multiple_of(t * 256, 256), 256)
            a, b, c = l_sc[0][rows, :], l_sc[1][rows, :], l_sc[2][rows, :]
            m = jnp.maximum(jnp.maximum(a, b), c)
            ea, eb, ec = jnp.exp(a - m), jnp.exp(b - m), jnp.exp(c - m)
            tot = ea + eb + ec
            out_ref[rows, :] = ((ea / tot) * o_sc[0][rows, :] + (eb / tot) * o_sc[1][rows, :]
                                + (ec / tot) * o_sc[2][rows, :]).astype(bf16)
            lse_ref[rows, :] = m + jnp.log(tot)
            return carry

        lax.fori_loop(0, SUPER // 256, merge, 0)

        @pl.when(step == (2 * n_total) // 3)
        def _():
            gather.pass_on()

        @pl.when(step == n_total - 1)
        def _():
            gather.finish()

    args = []
    for q, k, v in zip(qs, ks, vs):
        args += [q, k, k, v, v]
    nat = pl.BlockSpec((SUPER, 128), lambda hp, sb: (sb, hp))
    rows = pack.shape[0]
    return pl.pallas_call(
        body, name="attn_fwd", grid=(ATTN_WIDTH // 128, n_steps),
        in_specs=_attn_in_specs(S, False) + [ANY], out_specs=[nat, nat, ANY],
        out_shape=[jax.ShapeDtypeStruct((S, ATTN_WIDTH), bf16), jax.ShapeDtypeStruct((S, ATTN_WIDTH), f32),
                   _Gather.out_shape(rows, pack.dtype)],
        scratch_shapes=[pltpu.VMEM((SUPER, 128), f32)] * (2 * n_dil) + _Gather.scratch(rows, pack.dtype),
        compiler_params=_params("arbitrary", "arbitrary"),
    )(*args, pack)


def _attn_bwd(qs, ks, vs, dos, lse, delta, chip_sum):
    S = qs[0].shape[1]
    n_steps = S // SUPER
    last = n_steps - 1
    n_dil = len(DILATIONS)
    n_total = (ATTN_WIDTH // 128) * (n_steps + 1)

    def body(*refs):
        ins, (lse_ref, dl_ref, sum_ref) = refs[:6 * n_dil], refs[6 * n_dil:6 * n_dil + 3]
        dq_ref, dk_ref, dv_ref, others_ref = refs[6 * n_dil + 3:6 * n_dil + 7]
        dq_acc, dk_acc, dv_acc = refs[6 * n_dil + 7:6 * n_dil + 10]
        scatter = _Scatter(sum_ref, others_ref, *refs[6 * n_dil + 10:])
        sb = pl.program_id(1)
        step = pl.program_id(0) * (n_steps + 1) + sb
        cur = sb % 2
        prv = 1 - cur

        @pl.when(step == 0)
        def _():
            scatter.start()

        @pl.when(sb < n_steps)
        def _():
            dq_acc[...] = jnp.zeros_like(dq_acc)
            dk_acc[cur] = jnp.zeros((SUPER, 128), f32)
            dv_acc[cur] = jnp.zeros((SUPER, 128), f32)
            h0 = _head0_mask()
            for ci, d in enumerate(DILATIONS):
                q_ref, do_ref, kc_ref, kp_ref, vc_ref, vp_ref = ins[6 * ci:6 * ci + 6]

                def unit(u, carry, d=d, q_ref=q_ref, do_ref=do_ref, kc_ref=kc_ref, kp_ref=kp_ref, vc_ref=vc_ref,
                         vp_ref=vp_ref):
                    r, n, per_stream = _unit_place(d, u)
                    qv = q_ref[r, _block_rows(n), :]
                    dov = do_ref[r, _block_rows(n), :]
                    kb = _band(kc_ref, kp_ref, r, n)
                    vb = _band(vc_ref, vp_ref, r, n)
                    rows = _natural_rows(d, r, n)
                    has_prev = (sb > 0) | (n > 0)
                    q_pair = _head_pair_rows(qv, h0)
                    do_pair = _head_pair_rows(dov, h0)
                    s_t = jnp.where(_band_mask_t(has_prev), _dot_nt(kb, q_pair), NEG)
                    p_t = jnp.exp(s_t - _per_query_row(lse_ref[rows, :]))
                    dp_t = _dot_nt(vb, do_pair)
                    ds_t = (p_t * (dp_t - _per_query_row(dl_ref[rows, :]))).astype(bf16)
                    dvb = _dot(p_t.astype(bf16), do_pair)
                    dkb = _dot(ds_t, q_pair)
                    dq_pair = _dot_tn(ds_t, kb)
                    dq_acc[rows, :] += jnp.where(h0, dq_pair[:BLOCK], dq_pair[BLOCK:])
                    dk_acc[cur, rows, :] += dkb[BLOCK:]
                    dv_acc[cur, rows, :] += dvb[BLOCK:]

                    slot = jnp.where((n > 0) | (sb == 0), cur, prv)
                    before = _natural_rows(d, r, jnp.where(n > 0, n - 1, per_stream - 1))
                    dk_acc[slot, before, :] += dkb[:BLOCK]
                    dv_acc[slot, before, :] += dvb[:BLOCK]
                    return carry

                lax.fori_loop(0, UNITS, unit, 0, unroll=BWD_UNROLL)
            dq_ref[...] = (dq_acc[...] * ATTN_SCALE).astype(bf16)

        @pl.when(sb > 0)
        def _():
            dk_ref[...] = dk_acc[prv].astype(bf16)
            dv_ref[...] = dv_acc[prv].astype(bf16)

        @pl.when(step == n_total - 1)
        def _():
            scatter.finish()

    args = []
    for q, k, v, do in zip(qs, ks, vs, dos):
        args += [q, do, k, k, v, v]
    nat = pl.BlockSpec((SUPER, 128), lambda hp, sb: (jnp.minimum(sb, last), hp))
    nat_before = pl.BlockSpec((SUPER, 128), lambda hp, sb: (jnp.clip(sb - 1, 0, last), hp))
    out = jax.ShapeDtypeStruct((S, ATTN_WIDTH), bf16)
    half = chip_sum.shape[1]
    return pl.pallas_call(
        body, name="attn_bwd", grid=(ATTN_WIDTH // 128, n_steps + 1),
        in_specs=_attn_in_specs(S, True) + [nat, nat, ANY], out_specs=[nat, nat_before, nat_before, ANY],
        out_shape=[out, out, out, _Scatter.out_shape(half, chip_sum.dtype)],
        scratch_shapes=[pltpu.VMEM((SUPER, 128), f32), pltpu.VMEM((2, SUPER, 128), f32),
                        pltpu.VMEM((2, SUPER, 128), f32)] + _Scatter.scratch(half),
        compiler_params=_params("arbitrary", "arbitrary"),
    )(*args, lse, delta, chip_sum)


def _rms(v):
    return lax.rsqrt(jnp.mean(v * v, axis=-1, keepdims=True) + EPS)


def _out_proj(pool_out, attn_out, w_out, x, g2, g3):
    S = x.shape[0]
    ts = 1024

    def body(p_ref, a_ref, w_ref, x_ref, g2_ref, g3_ref, mix_ref, x2_ref, h2_ref):
        mix = _dot(p_ref[...], w_ref[:POOL_WIDTH, :]) + _dot(a_ref[...], w_ref[POOL_WIDTH:, :])
        mix_ref[...] = mix
        x2 = x_ref[...] + (mix * _rms(mix)) * g2_ref[...]
        x2_ref[...] = x2
        h2_ref[...] = ((x2 * _rms(x2)) * g3_ref[...]).astype(bf16)

    row = lambda w: pl.BlockSpec((ts, w), lambda i: (i, 0))
    gain = pl.BlockSpec((1, D_MODEL), lambda i: (0, 0))
    return pl.pallas_call(
        body, name="out_proj", grid=(S // ts,),
        in_specs=[row(POOL_WIDTH), row(ATTN_WIDTH), pl.BlockSpec((D_MODEL, D_MODEL), lambda i: (0, 0)),
                  row(D_MODEL), gain, gain],
        out_specs=[row(D_MODEL)] * 3,
        out_shape=[jax.ShapeDtypeStruct((S, D_MODEL), f32), jax.ShapeDtypeStruct((S, D_MODEL), f32),
                   jax.ShapeDtypeStruct((S, D_MODEL), bf16)],
        compiler_params=_params("parallel"),
    )(pool_out, attn_out, w_out, x, g2, g3)


FF_TILE = 256
FF_STEP_ROWS = 2048
FF_ROWS = 512
FF_BWD_ROWS = 256


def _sigmoid(g):
    return 1.0 / (1.0 + jnp.exp(-g))


def _ff_act_shape(S):
    return jax.ShapeDtypeStruct((D_FF // FF_TILE, S, FF_TILE), bf16)


def _ff_act_spec(ts):
    return pl.BlockSpec((1, ts, FF_TILE), lambda i, j: (j, i, 0))


def _ffn_fwd(h2, w_gate, w_up, w_down):
    S = h2.shape[0]
    ts = min(S, FF_STEP_ROWS)

    def body(h_ref, wg_ref, wu_ref, wd_ref, gate_ref, up_ref, f_ref):
        def rows_pass(first):
            def sub(i, carry):
                rows = pl.ds(pl.multiple_of(i * FF_ROWS, FF_ROWS), FF_ROWS)
                h = h_ref[rows, :]
                gate = _dot_nt(h, wg_ref[...])
                up = _dot_nt(h, wu_ref[...])
                gate_ref[0, rows, :] = gate.astype(bf16)
                up_ref[0, rows, :] = up.astype(bf16)
                part = _dot((gate * _sigmoid(gate) * up).astype(bf16), wd_ref[...])
                if first:
                    f_ref[rows, :] = part
                else:
                    f_ref[rows, :] += part
                return carry

            lax.fori_loop(0, ts // FF_ROWS, sub, 0, unroll=True)

        @pl.when(pl.program_id(1) == 0)
        def _():
            rows_pass(True)

        @pl.when(pl.program_id(1) > 0)
        def _():
            rows_pass(False)

    act = _ff_act_spec(ts)
    weight = pl.BlockSpec((FF_TILE, D_MODEL), lambda i, j: (j, 0))
    return pl.pallas_call(
        body, name="ffn_fwd", grid=(S // ts, D_FF // FF_TILE),
        in_specs=[pl.BlockSpec((ts, D_MODEL), lambda i, j: (i, 0)), weight, weight, weight],
        out_specs=[act, act, pl.BlockSpec((ts, D_MODEL), lambda i, j: (i, 0))],
        out_shape=[_ff_act_shape(S), _ff_act_shape(S), jax.ShapeDtypeStruct((S, D_MODEL), f32)],
        compiler_params=_params("parallel", "arbitrary"),
    )(h2, w_gate, w_up, w_down)


def _loss_head(f, x2, target, g4):
    S = f.shape[0]
    ts = 1024

    def body(f_ref, x2_ref, t_ref, g_ref, dy_ref, df_ref, dg_ref, loss_ref):
        @pl.when(pl.program_id(0) == 0)
        def _():
            dg_ref[...] = jnp.zeros_like(dg_ref)
            loss_ref[...] = jnp.zeros_like(loss_ref)

        fv = f_ref[...]
        g = g_ref[...]
        r = _rms(fv)
        fhat = fv * r
        err = (x2_ref[...] + fhat * g) - t_ref[...]
        loss_ref[...] += 0.5 * jnp.sum(jnp.mean(err * err, axis=-1, keepdims=True), axis=0, keepdims=True)
        dy = err * (1.0 / D_MODEL)
        dy_ref[...] = dy
        dg_ref[...] += jnp.sum(dy * fhat, axis=0, keepdims=True)
        dyg = dy * g
        df_ref[...] = (r * (dyg - fhat * jnp.mean(dyg * fhat, axis=-1, keepdims=True))).astype(bf16)

    row = pl.BlockSpec((ts, D_MODEL), lambda i: (i, 0))
    gain = pl.BlockSpec((1, D_MODEL), lambda i: (0, 0))
    return pl.pallas_call(
        body, name="loss_head", grid=(S // ts,), in_specs=[row, row, row, gain],
        out_specs=[row, row, gain, pl.BlockSpec((1, 1), lambda i: (0, 0))],
        out_shape=[jax.ShapeDtypeStruct((S, D_MODEL), f32), jax.ShapeDtypeStruct((S, D_MODEL), bf16),
                   jax.ShapeDtypeStruct((1, D_MODEL), f32), jax.ShapeDtypeStruct((1, 1), f32)],
        compiler_params=_params("arbitrary"),
    )(f, x2, target, g4)


def _ffn_bwd(df, gate, up, w_gate, w_up, w_down):
    S = df.shape[0]
    ts = min(S, FF_STEP_ROWS)

    def body(df_ref, gate_ref, up_ref, wg_ref, wu_ref, wd_ref, a_ref, dgate_ref, dup_ref, dh_ref):
        def rows_pass(first):
            def sub(i, carry):
                rows = pl.ds(pl.multiple_of(i * FF_BWD_ROWS, FF_BWD_ROWS), FF_BWD_ROWS)
                da = _dot_nt(df_ref[rows, :], wd_ref[...])
                g = gate_ref[0, rows, :].astype(f32)
                u = up_ref[0, rows, :].astype(f32)
                sig = _sigmoid(g)
                silu = g * sig
                a_ref[0, rows, :] = (silu * u).astype(bf16)
                dup = (da * silu).astype(bf16)
                dgate = (da * u * (sig * (1.0 + g * (1.0 - sig)))).astype(bf16)
                dup_ref[0, rows, :] = dup
                dgate_ref[0, rows, :] = dgate
                part = _dot(dgate, wg_ref[...]) + _dot(dup, wu_ref[...])
                if first:
                    dh_ref[rows, :] = part
                else:
                    dh_ref[rows, :] += part
                return carry

            lax.fori_loop(0, ts // FF_BWD_ROWS, sub, 0, unroll=True)

        @pl.when(pl.program_id(1) == 0)
        def _():
            rows_pass(True)

        @pl.when(pl.program_id(1) > 0)
        def _():
            rows_pass(False)

    act = _ff_act_spec(ts)
    row = pl.BlockSpec((ts, D_MODEL), lambda i, j: (i, 0))
    return pl.pallas_call(
        body, name="ffn_bwd", grid=(S // ts, D_FF // FF_TILE),
        in_specs=[row, act, act,
                  pl.BlockSpec((FF_TILE, D_MODEL), lambda i, j: (j, 0)),
                  pl.BlockSpec((FF_TILE, D_MODEL), lambda i, j: (j, 0)),
                  pl.BlockSpec((FF_TILE, D_MODEL), lambda i, j: (j, 0))],
        out_specs=[act, act, act, row],
        out_shape=[_ff_act_shape(S)] * 3 + [jax.ShapeDtypeStruct((S, D_MODEL), f32)],
        compiler_params=_params("parallel", "arbitrary"),
    )(df, gate, up, w_gate, w_up, w_down)


def _norm_bwd(dh2, dy, x2, mix, g3, g2):
    S = dh2.shape[0]
    ts = 512

    def body(dh_ref, dy_ref, x2_ref, mix_ref, g3_ref, g2_ref, dx2_ref, dmix_ref, dg3_ref, dg2_ref):
        @pl.when(pl.program_id(0) == 0)
        def _():
            dg3_ref[...] = jnp.zeros_like(dg3_ref)
            dg2_ref[...] = jnp.zeros_like(dg2_ref)

        dh = dh_ref[...]
        x2 = x2_ref[...]
        r3 = _rms(x2)
        xhat = x2 * r3
        dg3_ref[...] += jnp.sum(dh * xhat, axis=0, keepdims=True)
        dhg = dh * g3_ref[...]
        dx2 = dy_ref[...] + r3 * (dhg - xhat * jnp.mean(dhg * xhat, axis=-1, keepdims=True))
        dx2_ref[...] = dx2
        mix = mix_ref[...]
        r2 = _rms(mix)
        mhat = mix * r2
        dg2_ref[...] += jnp.sum(dx2 * mhat, axis=0, keepdims=True)
        dmg = dx2 * g2_ref[...]
        dmix_ref[...] = (r2 * (dmg - mhat * jnp.mean(dmg * mhat, axis=-1, keepdims=True))).astype(bf16)

    row = pl.BlockSpec((ts, D_MODEL), lambda i: (i, 0))
    gain = pl.BlockSpec((1, D_MODEL), lambda i: (0, 0))
    return pl.pallas_call(
        body, name="norm_bwd", grid=(S // ts,), in_specs=[row, row, row, row, gain, gain],
        out_specs=[row, row, gain, gain],
        out_shape=[jax.ShapeDtypeStruct((S, D_MODEL), f32), jax.ShapeDtypeStruct((S, D_MODEL), bf16),
                   jax.ShapeDtypeStruct((1, D_MODEL), f32), jax.ShapeDtypeStruct((1, D_MODEL), f32)],
        compiler_params=_params("arbitrary"),
    )(dh2, dy, x2, mix, g3, g2)


def _out_proj_bwd(dmix, w_out, attn_out, head_ones, grads):
    S = dmix.shape[0]
    ts = 512
    n_dil = len(DILATIONS)

    def body(dm_ref, w_ref, o_ref, ones_ref, g_ref, dp_ref, dl_ref, *rest):
        do_refs, theirs_ref = rest[:n_dil], rest[n_dil]
        stage = rest[n_dil + 1:n_dil + 1 + N_STAGE]
        swap = _Swap(g_ref, theirs_ref, *rest[n_dil + 1 + N_STAGE:])

        @pl.when(pl.program_id(0) == 0)
        def _():
            swap.start()

        @pl.when(pl.program_id(0) == S // ts - 1)
        def _():
            swap.finish()

        dcat = _dot_nt(dm_ref[...], w_ref[...])
        dp_ref[...] = dcat[:, :POOL_WIDTH]
        do = dcat[:, POOL_WIDTH:]
        for j in range(ATTN_WIDTH // 128):
            stage[j][...] = do[:, j * 128:(j + 1) * 128]
        _store_streams(stage, do_refs, ts)
        prod = do * o_ref[...].astype(f32)
        hi = prod.astype(bf16)
        lo = (prod - hi.astype(f32)).astype(bf16)
        ones = ones_ref[...]
        for j in range(ATTN_WIDTH // 128):
            cols = slice(j * 128, (j + 1) * 128)
            dl_ref[:, cols] = _dot(hi[:, cols], ones) + _dot(lo[:, cols], ones)

    row = lambda w: pl.BlockSpec((ts, w), lambda i: (i, 0))
    res = pl.pallas_call(
        body, name="out_proj_bwd", grid=(S // ts,),
        in_specs=[row(D_MODEL), pl.BlockSpec((D_MODEL, D_MODEL), lambda i: (0, 0)), row(ATTN_WIDTH),
                  pl.BlockSpec((128, 128), lambda i: (0, 0)), ANY],
        out_specs=[row(POOL_WIDTH), row(ATTN_WIDTH)] + [_stream_spec(d, ts) for d in DILATIONS] + [ANY],
        out_shape=[jax.ShapeDtypeStruct((S, POOL_WIDTH), f32), jax.ShapeDtypeStruct((S, ATTN_WIDTH), f32)]
        + [_stream_shape(S, d) for d in DILATIONS] + [_Swap.out_shape(grads)],
        scratch_shapes=_stage_scratch(ts) + _Swap.scratch(grads),
        compiler_params=_params("arbitrary"),
    )(dmix, w_out, attn_out, head_ones, grads)
    return res[0], res[1], res[2:2 + n_dil], res[2 + n_dil]


IN_BWD_ROWS = 256


def _in_proj_bwd(du, dq, dk, dv, cos_t, sin_t, w_in, x, dx2, g1):
    S = x.shape[0]
    ts = 512

    def body(du_ref, dq_ref, dk_ref, dv_ref, cos_ref, sin_ref, w_ref, x_ref, dx2_ref, g_ref, gx_ref, dproj_ref, dg_ref):
        @pl.when(pl.program_id(0) == 0)
        def _():
            dg_ref[...] = jnp.zeros_like(dg_ref)

        first = _first_half_mask(IN_BWD_ROWS)

        def sub(i, carry):
            rows = pl.ds(pl.multiple_of(i * IN_BWD_ROWS, IN_BWD_ROWS), IN_BWD_ROWS)
            dproj_ref[rows, :POOL_WIDTH] = du_ref[rows, :]
            cos = cos_ref[rows, :]
            sin = sin_ref[rows, :]
            for j in range(ATTN_WIDTH // 128):
                cols = slice(j * 128, (j + 1) * 128)
                for base, ref in ((POOL_WIDTH, dq_ref), (POOL_WIDTH + ATTN_WIDTH, dk_ref)):
                    g = ref[rows, cols].astype(f32)
                    pre = g * cos + _rope_partner(g * sin, first)
                    dproj_ref[rows, base + j * 128: base + (j + 1) * 128] = pre.astype(bf16)
            dproj_ref[rows, POOL_WIDTH + 2 * ATTN_WIDTH:] = dv_ref[rows, :]

            dh = _dot(dproj_ref[rows, :], w_ref[...])
            xv = x_ref[rows, :]
            r = _rms(xv)
            xhat = xv * r
            dg_ref[...] += jnp.sum(dh * xhat, axis=0, keepdims=True)
            dhg = dh * g_ref[...]
            gx_ref[rows, :] = dx2_ref[rows, :] + r * (dhg - xhat * jnp.mean(dhg * xhat, axis=-1, keepdims=True))
            return carry

        lax.fori_loop(0, ts // IN_BWD_ROWS, sub, 0, unroll=True)

    row = lambda w: pl.BlockSpec((ts, w), lambda i: (i, 0))
    gain = pl.BlockSpec((1, D_MODEL), lambda i: (0, 0))
    return pl.pallas_call(
        body, name="in_proj_bwd", grid=(S // ts,),
        in_specs=[row(POOL_WIDTH)] + [row(ATTN_WIDTH)] * 3 + [row(128), row(128),
                  pl.BlockSpec((IN_WIDTH, D_MODEL), lambda i: (0, 0)), row(D_MODEL), row(D_MODEL), gain],
        out_specs=[row(D_MODEL), row(IN_WIDTH), gain],
        out_shape=[jax.ShapeDtypeStruct((S, D_MODEL), f32), jax.ShapeDtypeStruct((S, IN_WIDTH), bf16),
                   jax.ShapeDtypeStruct((1, D_MODEL), f32)],
        compiler_params=_params("arbitrary"),
    )(du, dq, dk, dv, cos_t, sin_t, w_in, x, dx2, g1)


def _matmul_tiles_tn(a, b, name):
    T, K, w = a.shape
    N = b.shape[1]
    tk = 1024

    def body(a_ref, b_ref, o_ref):
        def tiles_pass(first):
            for t in range(T):
                part = _dot_tn(a_ref[t], b_ref[...])
                if first:
                    o_ref[t * w:(t + 1) * w, :] = part
                else:
                    o_ref[t * w:(t + 1) * w, :] += part

        @pl.when(pl.program_id(0) == 0)
        def _():
            tiles_pass(True)

        @pl.when(pl.program_id(0) > 0)
        def _():
            tiles_pass(False)

    return pl.pallas_call(
        body, name=name, grid=(K // tk,),
        in_specs=[pl.BlockSpec((T, tk, w), lambda k: (0, k, 0)), pl.BlockSpec((tk, N), lambda k: (k, 0))],
        out_specs=pl.BlockSpec((T * w, N), lambda k: (0, 0)),
        out_shape=jax.ShapeDtypeStruct((T * w, N), f32),
        compiler_params=_params("arbitrary"),
    )(a, b)


def _matmul_tn(a, b, name):
    K, M = a.shape
    N = b.shape[1]
    tk = 1024

    def body(a_ref, b_ref, o_ref):
        _tn_step(a_ref, b_ref, o_ref, M)

    return pl.pallas_call(
        body, name=name, grid=(K // tk,),
        in_specs=[pl.BlockSpec((tk, M), lambda k: (k, 0)), pl.BlockSpec((tk, N), lambda k: (k, 0))],
        out_specs=pl.BlockSpec((M, N), lambda k: (0, 0)),
        out_shape=jax.ShapeDtypeStruct((M, N), f32),
        compiler_params=_params("arbitrary"),
    )(a, b)


def _tn_step(a_ref, b_ref, o_ref, M):
    w = 256

    def tiles_pass(first):
        for t in range(M // w):
            part = _dot_tn(a_ref[:, t * w:(t + 1) * w], b_ref[...])
            if first:
                o_ref[t * w:(t + 1) * w, :] = part
            else:
                o_ref[t * w:(t + 1) * w, :] += part

    @pl.when(pl.program_id(0) == 0)
    def _():
        tiles_pass(True)

    @pl.when(pl.program_id(0) > 0)
    def _():
        tiles_pass(False)


def _matmul_tn_and_small_sum(a, b, block, name):
    K, M = a.shape
    N = b.shape[1]
    tk = 1024
    n_steps = K // tk

    def body(a_ref, b_ref, block_ref, o_ref, total_ref, *scratch):
        small = _SmallSum(block_ref, *scratch)

        @pl.when(pl.program_id(0) == 0)
        def _():
            small.start()

        _tn_step(a_ref, b_ref, o_ref, M)

        @pl.when(pl.program_id(0) == n_steps - 1)
        def _():
            small.finish(total_ref)

    return pl.pallas_call(
        body, name=name, grid=(n_steps,),
        in_specs=[pl.BlockSpec((tk, M), lambda k: (k, 0)), pl.BlockSpec((tk, N), lambda k: (k, 0)), ANY],
        out_specs=[pl.BlockSpec((M, N), lambda k: (0, 0)), pl.BlockSpec(block.shape, lambda k: (0, 0))],
        out_shape=[jax.ShapeDtypeStruct((M, N), f32), jax.ShapeDtypeStruct(block.shape, block.dtype)],
        scratch_shapes=_SmallSum.scratch(block),
        compiler_params=_params("arbitrary"),
    )(a, b, block)


def _rope_tables(S):
    half = HEAD_DIM // 2
    freqs = ROPE_THETA ** (-jnp.arange(half, dtype=f32) * (2.0 / HEAD_DIM))
    ang = jnp.arange(S).astype(f32)[:, None] * freqs[None, :]
    cos = jnp.tile(jnp.cos(ang), (1, 4))
    sin = jnp.sin(ang)
    sin = jnp.tile(jnp.concatenate([-sin, sin], axis=1), (1, 2))
    return cos, sin


def _block_diag(w_pool):
    w = jnp.zeros((POOL_WIDTH, POOL_WIDTH), w_pool.dtype)
    for g in range(POOL_WIDTH // POOL_GROUP):
        w = lax.dynamic_update_slice(w, w_pool[g], (g * POOL_GROUP, g * POOL_GROUP))
    return w


def _head_ones():
    head = np.arange(128) // HEAD_DIM
    return jnp.asarray(head[:, None] == head[None, :], dtype=bf16)


def _place():
    x, y, c = lax.axis_index("x"), lax.axis_index("y"), lax.axis_index("c")
    chips = [(1 - x, y), (x, 1 - y), (1 - x, 1 - y)]
    return x, y, c, chips


ANY = pl.BlockSpec(memory_space=pl.ANY)
N_PEER_CHIPS = N_CHIPS - 1
ICI_PIECES = 4
D2D_PIECES = 8
LOCAL_PIECES = 8


def _row_chunks(rows, n, unit=32):
    units = rows // unit
    out, start = [], 0
    for i in range(n):
        size = (units // n + (1 if i < units % n else 0)) * unit
        out.append((start, size))
        start += size
    return [piece for piece in out if piece[1]]


class _LocalCopy:
    def __init__(self, src_rows, dst_rows, rows, buf, sems_in, sems_out):
        self.loads, self.stores = [], []
        for i, (start, size) in enumerate(_row_chunks(rows, LOCAL_PIECES)):
            r = pl.ds(start, size)
            self.loads.append(pltpu.make_async_copy(src_rows(r), buf.at[r], sems_in.at[i]))
            self.stores.append(pltpu.make_async_copy(buf.at[r], dst_rows(r), sems_out.at[i]))

    def start(self):
        for cp in self.loads:
            cp.start()

    def pass_on(self):
        for load, store in zip(self.loads, self.stores):
            load.wait()
            store.start()

    def finish(self):
        for store in self.stores:
            store.wait()

    @staticmethod
    def scratch(rows, dtype):
        return [pltpu.VMEM((rows, D_MODEL), dtype), pltpu.SemaphoreType.DMA((LOCAL_PIECES,)),
                pltpu.SemaphoreType.DMA((LOCAL_PIECES,))]


class _Gather:
    def __init__(self, w_ref, out_ref, send1, recv1, send2, recv2, buf, sems_in, sems_out):
        x, y, c, chips = _place()
        me = 2 * x + y
        rows = w_ref.shape[0]
        half = rows // 2
        pieces = _row_chunks(half, ICI_PIECES)
        self.own = _LocalCopy(lambda r: w_ref.at[r], lambda r: out_ref.at[me, r], rows, buf, sems_in, sems_out)

        def rows_of(core, piece):
            start, size = piece
            return pl.ds(core * half + start, size)

        self.sends, self.arrivals, self.forwards, self.forward_arrivals = [], [], [], []
        for i, piece in enumerate(pieces):
            for j, (cx, cy) in enumerate(chips):
                k = j * len(pieces) + i
                there = 2 * cx + cy

                def direct(src_chip, cx=cx, cy=cy, k=k, piece=piece):
                    return pltpu.make_async_remote_copy(
                        src_ref=w_ref.at[rows_of(c, piece)], dst_ref=out_ref.at[src_chip, rows_of(c, piece)],
                        send_sem=send1.at[k], recv_sem=recv1.at[k], device_id=(cx, cy, c), device_id_type=MESH)

                def passed(core, there=there, k=k, piece=piece):
                    return pltpu.make_async_remote_copy(
                        src_ref=out_ref.at[there, rows_of(core, piece)], dst_ref=out_ref.at[there, rows_of(core, piece)],
                        send_sem=send2.at[k], recv_sem=recv2.at[k], device_id=(x, y, 1 - c), device_id_type=MESH)

                self.sends.append(direct(me))
                self.arrivals.append(direct(there))
                self.forwards.append(passed(c))
                self.forward_arrivals.append(passed(1 - c))

    def start(self):
        for cp in self.sends:
            cp.start()
        self.own.start()

    def pass_on(self):
        self.own.pass_on()
        for arrival, forward in zip(self.arrivals, self.forwards):
            arrival.wait_recv()
            forward.start()

    def finish(self):
        for arrival in self.forward_arrivals:
            arrival.wait_recv()
        for cp in self.sends + self.forwards:
            cp.wait_send()
        self.own.finish()

    @staticmethod
    def scratch(rows, dtype):
        n = N_PEER_CHIPS * len(_row_chunks(rows // 2, ICI_PIECES))
        return [pltpu.SemaphoreType.DMA((n,))] * 4 + _LocalCopy.scratch(rows, dtype)

    @staticmethod
    def out_shape(rows, dtype):
        return jax.ShapeDtypeStruct((N_CHIPS, rows, D_MODEL), dtype)


def _gather_weights(pack):
    rows = pack.shape[0]

    def body(w_ref, out_ref, *scratch):
        gather = _Gather(w_ref, out_ref, *scratch)
        gather.start()
        gather.pass_on()
        gather.finish()

    return pl.pallas_call(
        body, name="gather_weights", in_specs=[ANY], out_specs=ANY, out_shape=_Gather.out_shape(rows, pack.dtype),
        scratch_shapes=_Gather.scratch(rows, pack.dtype),
        compiler_params=pltpu.CompilerParams(vmem_limit_bytes=VMEM_LIMIT_V7X),
    )(pack)


class _Scatter:
    def __init__(self, h_ref, out_ref, send, recv):
        x, y, c, chips = _place()
        pieces = _row_chunks(h_ref.shape[1], ICI_PIECES)
        self.copies = []
        for i, (start, size) in enumerate(pieces):
            for j, (cx, cy) in enumerate(chips):
                k = j * len(pieces) + i
                self.copies.append(pltpu.make_async_remote_copy(
                    src_ref=h_ref.at[2 * cx + cy, pl.ds(start, size)], dst_ref=out_ref.at[j, pl.ds(start, size)],
                    send_sem=send.at[k], recv_sem=recv.at[k], device_id=(cx, cy, c), device_id_type=MESH))

    def start(self):
        for cp in self.copies:
            cp.start()

    def finish(self):
        for cp in self.copies:
            cp.wait_recv()
        for cp in self.copies:
            cp.wait_send()

    @staticmethod
    def scratch(half):
        n = N_PEER_CHIPS * len(_row_chunks(half, ICI_PIECES))
        return [pltpu.SemaphoreType.DMA((n,))] * 2

    @staticmethod
    def out_shape(half, dtype):
        return jax.ShapeDtypeStruct((N_PEER_CHIPS, half, D_MODEL), dtype)


def _scatter_to_chips(h):
    half = h.shape[1]

    def body(h_ref, out_ref, send, recv):
        scatter = _Scatter(h_ref, out_ref, send, recv)
        scatter.start()
        scatter.finish()

    return pl.pallas_call(
        body, name="scatter_to_chips", in_specs=[ANY], out_specs=ANY, out_shape=_Scatter.out_shape(half, h.dtype),
        scratch_shapes=_Scatter.scratch(half),
    )(h)


class _Swap:
    def __init__(self, g_ref, theirs_ref, send, recv):
        x, y, c, _ = _place()
        half = g_ref.shape[1] // 2
        pieces = _row_chunks(half, D2D_PIECES)
        self.copies = []
        for s in range(N_CHIPS):
            for i, (start, size) in enumerate(pieces):
                k = s * len(pieces) + i
                self.copies.append(pltpu.make_async_remote_copy(
                    src_ref=g_ref.at[s, pl.ds((1 - c) * half + start, size)], dst_ref=theirs_ref.at[s, pl.ds(start, size)],
                    send_sem=send.at[k], recv_sem=recv.at[k], device_id=(x, y, 1 - c), device_id_type=MESH))

    def start(self):
        for cp in self.copies:
            cp.start()

    def finish(self):
        for cp in self.copies:
            cp.wait()

    @staticmethod
    def scratch(g):
        n = N_CHIPS * len(_row_chunks(g.shape[1] // 2, D2D_PIECES))
        return [pltpu.SemaphoreType.DMA((n,))] * 2

    @staticmethod
    def out_shape(g):
        return jax.ShapeDtypeStruct((N_CHIPS, g.shape[1] // 2, D_MODEL), g.dtype)


def _swap_halves(g):
    def body(g_ref, theirs_ref, send, recv):
        swap = _Swap(g_ref, theirs_ref, send, recv)
        swap.start()
        swap.finish()

    return pl.pallas_call(
        body, name="swap_halves", in_specs=[ANY], out_specs=ANY, out_shape=_Swap.out_shape(g),
        scratch_shapes=_Swap.scratch(g),
    )(g)


ADD_TILE_MAX_ROWS = 600


def _add_tile(half):
    return max(t for t in range(8, ADD_TILE_MAX_ROWS + 1, 8) if half % t == 0)


def _add_cores(g, theirs, name, out_dtype=f32):
    half = theirs.shape[1]
    tr = _add_tile(half)
    n_t = half // tr

    def body(c_ref, g_ref, t_ref, o_ref):
        o_ref[...] = (g_ref[...] + t_ref[...]).astype(out_dtype)

    blk = pl.BlockSpec((1, tr, D_MODEL), lambda s, t, c_ref: (s, t, 0))
    return pl.pallas_call(
        body, name=name,
        grid_spec=pltpu.PrefetchScalarGridSpec(
            num_scalar_prefetch=1, grid=(N_CHIPS, n_t),
            in_specs=[pl.BlockSpec((1, tr, D_MODEL), lambda s, t, c_ref: (s, c_ref[0] * n_t + t, 0)), blk],
            out_specs=blk),
        out_shape=jax.ShapeDtypeStruct(theirs.shape, out_dtype),
        compiler_params=_params("parallel", "parallel"),
    )(lax.axis_index("c").astype(jnp.int32).reshape(1), g, theirs)


def _add_chips(chip_sum, others, name):
    half = chip_sum.shape[1]
    tr = _add_tile(half)

    def body(me_ref, own_ref, o0, o1, o2, out_ref):
        out_ref[...] = ((own_ref[0].astype(f32) + o0[0].astype(f32)) + o1[0].astype(f32)) + o2[0].astype(f32)

    other = lambda j: pl.BlockSpec((1, tr, D_MODEL), lambda t, me_ref: (j, t, 0))
    return pl.pallas_call(
        body, name=name,
        grid_spec=pltpu.PrefetchScalarGridSpec(
            num_scalar_prefetch=1, grid=(half // tr,),
            in_specs=[pl.BlockSpec((1, tr, D_MODEL), lambda t, me_ref: (me_ref[0], t, 0)), other(0), other(1), other(2)],
            out_specs=pl.BlockSpec((tr, D_MODEL), lambda t, me_ref: (t, 0))),
        out_shape=jax.ShapeDtypeStruct((half, D_MODEL), f32),
        compiler_params=_params("parallel"),
    )((2 * lax.axis_index("x") + lax.axis_index("y")).astype(jnp.int32).reshape(1), chip_sum, others, others, others)


def _join_halves(parts):
    n_parts = len(parts)
    pieces = [_row_chunks(r.shape[0], D2D_PIECES) for r in parts]
    first = [sum(len(p) for p in pieces[:i]) for i in range(n_parts)]
    n = sum(len(p) for p in pieces)

    def body(*refs):
        r_refs, out_refs = refs[:n_parts], refs[n_parts:2 * n_parts]
        send, recv = refs[2 * n_parts:2 * n_parts + 2]
        local = refs[2 * n_parts + 2:]
        x, y, c, _ = _place()
        owns = [_LocalCopy(lambda rr, r_ref=r_ref: r_ref.at[rr], lambda rr, out_ref=out_ref: out_ref.at[c, rr],
                           r_ref.shape[0], *local[3 * i:3 * i + 3])
                for i, (r_ref, out_ref) in enumerate(zip(r_refs, out_refs))]
        for own in owns:
            own.start()

        def piece(i, j, core):
            start, size = pieces[i][j]
            return pltpu.make_async_remote_copy(
                src_ref=r_refs[i].at[pl.ds(start, size)], dst_ref=out_refs[i].at[core, pl.ds(start, size)],
                send_sem=send.at[first[i] + j], recv_sem=recv.at[first[i] + j],
                device_id=(x, y, 1 - c), device_id_type=MESH)

        every = [(i, j) for i in range(n_parts) for j in range(len(pieces[i]))]
        copies = [piece(i, j, c) for i, j in every]
        for cp in copies:
            cp.start()
        for own in owns:
            own.pass_on()
        for i, j in every:
            piece(i, j, 1 - c).wait_recv()
        for cp in copies:
            cp.wait_send()
        for own in owns:
            own.finish()

    local_scratch = []
    for r in parts:
        local_scratch += _LocalCopy.scratch(r.shape[0], r.dtype)
    return pl.pallas_call(
        body, name="join_halves", in_specs=[ANY] * n_parts, out_specs=[ANY] * n_parts,
        out_shape=[jax.ShapeDtypeStruct((2,) + r.shape, r.dtype) for r in parts],
        scratch_shapes=[pltpu.SemaphoreType.DMA((n,))] * 2 + local_scratch,
        compiler_params=pltpu.CompilerParams(vmem_limit_bytes=VMEM_LIMIT_V7X),
    )(*parts)


class _SmallSum:
    def __init__(self, b_ref, gathered, send, recv, local_sem):
        x, y, c, _ = _place()
        me = 4 * x + 2 * y + c
        self.gathered = gathered
        self.own = pltpu.make_async_copy(b_ref, gathered.at[me], local_sem)
        self.sends, self.arrivals = [], []
        for kk in range(1, N_DEV):
            flip = lambda v, bit: 1 - v if bit else v
            peer = (flip(x, kk & 4), flip(y, kk & 2), flip(c, kk & 1))
            self.sends.append(pltpu.make_async_remote_copy(
                src_ref=b_ref, dst_ref=gathered.at[me], send_sem=send.at[kk - 1], recv_sem=recv.at[kk - 1],
                device_id=peer, device_id_type=MESH))
            self.arrivals.append(pltpu.make_async_remote_copy(
                src_ref=b_ref, dst_ref=gathered.at[jnp.bitwise_xor(me, kk)], send_sem=send.at[kk - 1],
                recv_sem=recv.at[kk - 1], device_id=peer, device_id_type=MESH))

    def start(self):
        self.own.start()
        for cp in self.sends:
            cp.start()

    def finish(self, out_ref):
        self.own.wait()
        for cp in self.arrivals:
            cp.wait_recv()
        for cp in self.sends:
            cp.wait_send()
        acc = self.gathered[0]
        for dev in range(1, N_DEV):
            acc = acc + self.gathered[dev]
        out_ref[...] = acc

    @staticmethod
    def scratch(block):
        return [pltpu.VMEM((N_DEV,) + block.shape, block.dtype), pltpu.SemaphoreType.DMA((N_DEV - 1,)),
                pltpu.SemaphoreType.DMA((N_DEV - 1,)), pltpu.SemaphoreType.DMA]


def _adamw(w, g, m, v, name):
    rows, cols = w.shape
    tr = max(t for t in range(8, 513, 8) if rows % t == 0)
    c1 = 1.0 - ADAM_B1 ** ADAM_STEP
    c2 = 1.0 - ADAM_B2 ** ADAM_STEP

    def body(w_ref, g_ref, m_ref, v_ref, d_ref, nm_ref, nv_ref):
        gv = g_ref[...]
        nm = ADAM_B1 * m_ref[...] + (1.0 - ADAM_B1) * gv
        nv = ADAM_B2 * v_ref[...] + (1.0 - ADAM_B2) * (gv * gv)
        nm_ref[...] = nm
        nv_ref[...] = nv
        d_ref[...] = -ADAM_LR * ((nm / c1) / (jnp.sqrt(nv / c2) + ADAM_EPS) + ADAM_WD * w_ref[...])

    blk = pl.BlockSpec((tr, cols), lambda i: (i, 0))
    shape = jax.ShapeDtypeStruct((rows, cols), f32)
    return pl.pallas_call(
        body, name=name, grid=(rows // tr,), in_specs=[blk] * 4, out_specs=[blk] * 3, out_shape=[shape] * 3,
        compiler_params=_params("parallel"),
    )(w, g, m, v)


LARGE = ("w_in", "w_out", "w_gate", "w_up", "w_down")
SMALL = ("ln_pre_mix", "ln_post_mix", "ln_pre_ffn", "ln_post_ffn", "pool_scale", "w_pool")
SHARD_ROWS = {"w_in": 640, "w_out": 256, "w_gate": 704, "w_up": 704, "w_down": 704}
COLUMN_SHARDED = ("w_in", "w_gate", "w_up")
UPDATED_TRANSPOSED = ("w_gate", "w_up")
NEEDED_FIRST = ("w_in",)
NEEDED_LATER = ("w_out", "w_gate", "w_up", "w_down")
READY_EARLY = ("w_out", "w_gate", "w_up", "w_down")
READY_LATE = ("w_in",)


def _pack_shard(shards, names):
    return jnp.concatenate([shards[n].T if n in COLUMN_SHARDED else shards[n] for n in names], axis=0)


def _unpack_shard(pack, names):
    out, row = {}, 0
    for n in names:
        out[n] = pack[row:row + SHARD_ROWS[n]]
        row += SHARD_ROWS[n]
    return out


def _whole_from_shards(packs, names):
    out, row = {}, 0
    for n in names:
        rows = SHARD_ROWS[n]
        out[n] = packs[:, row:row + rows].reshape(N_CHIPS * rows, D_MODEL)
        row += rows
    return out


def _shards_from_whole(grads, names):
    return jnp.concatenate([grads[n].reshape(N_CHIPS, SHARD_ROWS[n], D_MODEL) for n in names], axis=1)


def _pack_small(vals):
    rows = [vals[n].reshape(1, D_MODEL) for n in SMALL[:4]]
    rows.append(jnp.pad(vals["pool_scale"].reshape(1, POOL_WIDTH), ((0, 0), (0, D_MODEL - POOL_WIDTH))))
    rows.append(jnp.pad(vals["loss"].reshape(1, 1), ((0, 0), (0, D_MODEL - 1))))
    rows.append(jnp.zeros((2, D_MODEL), f32))
    rows.append(vals["w_pool"].reshape(16, D_MODEL))
    return jnp.concatenate(rows, axis=0)


def _unpack_small(block):
    out = {n: block[i:i + 1] for i, n in enumerate(SMALL[:4])}
    out["pool_scale"] = block[4:5, :POOL_WIDTH]
    out["loss"] = block[5, 0]
    out["w_pool"] = block[8:24].reshape(1, 4, POOL_GROUP, POOL_GROUP)
    return out


def kernel(x, ln_pre_mix, w_in, w_pool, pool_scale, w_out, ln_post_mix, ln_pre_ffn, w_gate, w_up, w_down, ln_post_ffn, loss_target, m_ln_pre_mix, m_w_in, m_w_pool, m_pool_scale, m_w_out, m_ln_post_mix, m_ln_pre_ffn, m_w_gate, m_w_up, m_w_down, m_ln_post_ffn, v_ln_pre_mix, v_w_in, v_w_pool, v_pool_scale, v_w_out, v_ln_post_mix, v_ln_pre_ffn, v_w_gate, v_w_up, v_w_down, v_ln_post_ffn):
    w = dict(ln_pre_mix=ln_pre_mix, w_in=w_in, w_pool=w_pool, pool_scale=pool_scale, w_out=w_out,
             ln_post_mix=ln_post_mix, ln_pre_ffn=ln_pre_ffn, w_gate=w_gate, w_up=w_up, w_down=w_down,
             ln_post_ffn=ln_post_ffn)
    m = dict(ln_pre_mix=m_ln_pre_mix, w_in=m_w_in, w_pool=m_w_pool, pool_scale=m_pool_scale, w_out=m_w_out,
             ln_post_mix=m_ln_post_mix, ln_pre_ffn=m_ln_pre_ffn, w_gate=m_w_gate, w_up=m_w_up, w_down=m_w_down,
             ln_post_ffn=m_ln_post_ffn)
    v = dict(ln_pre_mix=v_ln_pre_mix, w_in=v_w_in, w_pool=v_w_pool, pool_scale=v_pool_scale, w_out=v_w_out,
             ln_post_mix=v_ln_post_mix, ln_pre_ffn=v_ln_pre_ffn, w_gate=v_w_gate, w_up=v_w_up, w_down=v_w_down,
             ln_post_ffn=v_ln_post_ffn)

    xs, target = x[0], loss_target[0]
    cos_t, sin_t = _rope_tables(xs.shape[0])
    w_bd = _block_diag(w_pool[0]).astype(bf16)
    shard = {n: w[n][0].astype(bf16) for n in LARGE}

    w_in_whole = _whole_from_shards(_gather_weights(_pack_shard(shard, NEEDED_FIRST)), NEEDED_FIRST)["w_in"]
    h1, u, qs, ks, vs = _in_proj(xs, ln_pre_mix, w_in_whole, cos_t, sin_t)
    pool_out = _pool_fwd(u, w_bd, pool_scale)
    attn_out, lse, later = _attn_fwd(qs, ks, vs, _pack_shard(shard, NEEDED_LATER))
    whole = _whole_from_shards(later, NEEDED_LATER)
    mix, x2, h2 = _out_proj(pool_out, attn_out, whole["w_out"], xs, ln_post_mix, ln_pre_ffn)
    gate, up, f = _ffn_fwd(h2, whole["w_gate"], whole["w_up"], whole["w_down"])
    dy, df, dg4, loss = _loss_head(f, x2, target, ln_post_ffn)

    large = {}
    a, dgate, dup, dh2 = _ffn_bwd(df, gate, up, whole["w_gate"], whole["w_up"], whole["w_down"])
    large["w_down"] = _matmul_tiles_tn(a, df, "grad_w_down")
    large["w_gate"] = _matmul_tiles_tn(dgate, h2, "grad_w_gate")
    large["w_up"] = _matmul_tiles_tn(dup, h2, "grad_w_up")
    dx2, dmix, dg3, dg2 = _norm_bwd(dh2, dy, x2, mix, ln_pre_ffn, ln_post_mix)
    large["w_out"] = jnp.concatenate([_matmul_tn(pool_out, dmix, "grad_w_out_pool"),
                                      _matmul_tn(attn_out, dmix, "grad_w_out_attn")], axis=0)
    early = _shards_from_whole(large, READY_EARLY)
    dpool, delta, dos, early_theirs = _out_proj_bwd(dmix, whole["w_out"], attn_out, _head_ones(), early)
    early_chip = _add_cores(early, early_theirs, "add_cores_early")
    du, d_w_bd, d_scale = _pool_bwd(u, dpool, w_bd, pool_scale)
    dq, dk, dv, early_others = _attn_bwd(qs, ks, vs, dos, lse, delta, early_chip)
    grad_x, dproj, dg1 = _in_proj_bwd(du, dq, dk, dv, cos_t, sin_t, w_in_whole, xs, dx2, ln_pre_mix)
    d_w_pool = jnp.stack([d_w_bd[g * POOL_GROUP:(g + 1) * POOL_GROUP, g * POOL_GROUP:(g + 1) * POOL_GROUP]
                          for g in range(POOL_WIDTH // POOL_GROUP)])
    small = dict(ln_pre_mix=dg1, ln_post_mix=dg2, ln_pre_ffn=dg3, ln_post_ffn=dg4, pool_scale=d_scale, w_pool=d_w_pool)
    large["w_in"], small_total = _matmul_tn_and_small_sum(dproj, h1, _pack_small(dict(small, loss=loss)), "grad_w_in")
    late = _shards_from_whole(large, READY_LATE)
    late_chip = _add_cores(late, _swap_halves(late), "add_cores_late", bf16)
    late_others = _scatter_to_chips(late_chip)
    early_half = _add_chips(early_chip, early_others, "add_chips_early")
    late_half = _add_chips(late_chip, late_others, "add_chips_late")
    early_whole, late_whole = _join_halves([early_half, late_half])
    grads = _unpack_shard(early_whole.reshape(-1, D_MODEL), READY_EARLY)
    grads.update(_unpack_shard(late_whole.reshape(-1, D_MODEL), READY_LATE))

    total = _unpack_small(small_total)
    for n in SMALL:
        grads[n] = total[n]

    delta_w, new_m, new_v = {}, {}, {}
    for n in LARGE:
        if n in UPDATED_TRANSPOSED:
            update = _adamw(w[n][0].T, grads[n], m[n][0].T, v[n][0].T, "adamw_" + n)
            delta_w[n], new_m[n], new_v[n], grads[n] = [a.T for a in (*update, grads[n])]
        else:
            if n in COLUMN_SHARDED:
                grads[n] = grads[n].T
            delta_w[n], new_m[n], new_v[n] = _adamw(w[n][0], grads[n], m[n][0], v[n][0], "adamw_" + n)
    small_state = [_pack_small(dict({n: s[n] for n in SMALL}, loss=jnp.zeros((), f32))) for s in (w, m, v)]
    small_grad = _pack_small(dict({n: grads[n] for n in SMALL}, loss=jnp.zeros((), f32)))
    sd, sm, sv = _adamw(small_state[0], small_grad, small_state[1], small_state[2], "adamw_small")
    for out, block in ((delta_w, sd), (new_m, sm), (new_v, sv)):
        un = _unpack_small(block)
        for n in SMALL:
            out[n] = un[n]

    names = ("ln_pre_mix", "w_in", "w_pool", "pool_scale", "w_out", "ln_post_mix", "ln_pre_ffn", "w_gate", "w_up",
             "w_down", "ln_post_ffn")
    full = lambda d: [d[n].reshape(w[n].shape) for n in names]
    return (total["loss"], grad_x[None], *full(grads), *full(delta_w), *full(new_m), *full(new_v))
```

```python
import numpy as np
import jax
import jax.numpy as jnp
from jax import lax
from jax.experimental import pallas as pl
from jax.experimental.pallas import tpu as pltpu

D_MODEL = 1024
POOL_WIDTH = 256
POOL_GROUP = 64
ATTN_WIDTH = 768
HEAD_DIM = 64
IN_WIDTH = 2560
D_FF = 2816
BLOCK = 128
DILATIONS = (1, 4, 16)
ROPE_THETA = 10000.0
EPS = 1e-6
ATTN_SCALE = 0.125
NEG = -1e30

ADAM_LR = 0.001
ADAM_B1 = 0.9
ADAM_B2 = 0.999
ADAM_EPS = 1e-08
ADAM_WD = 0.01
ADAM_STEP = 10

N_CHIPS = 4
N_DEV = 8
VMEM_LIMIT_V7X = 56 * 1024 * 1024
MESH = pl.DeviceIdType.MESH

f32 = jnp.float32
bf16 = jnp.bfloat16


def _params(*sem):
    return pltpu.CompilerParams(dimension_semantics=sem, vmem_limit_bytes=VMEM_LIMIT_V7X)


def _dot(a, b):
    return jnp.dot(a, b, preferred_element_type=f32)


def _dot_nt(a, b):
    return lax.dot_general(a, b, (((1,), (1,)), ((), ())), preferred_element_type=f32)


def _dot_tn(a, b):
    return lax.dot_general(a, b, (((0,), (0,)), ((), ())), preferred_element_type=f32)


def _rope_partner(a, first_half):
    return jnp.where(first_half, pltpu.roll(a, 96, 1), pltpu.roll(a, 32, 1))


def _first_half_mask(rows):
    lane = lax.broadcasted_iota(jnp.int32, (rows, 128), 1)
    return (lane % HEAD_DIM) < (HEAD_DIM // 2)


def _stream_spec(d, ts):
    return pl.BlockSpec((d, ts // d, ATTN_WIDTH), lambda i: (0, i, 0))


def _stream_shape(S, d):
    return jax.ShapeDtypeStruct((d, S // d, ATTN_WIDTH), bf16)


N_STAGE = ATTN_WIDTH // 128


def _stage_scratch(ts):
    return [pltpu.VMEM((ts, 128), f32)] * N_STAGE


def _store_streams(stage, out_refs, ts):
    for d, ref in zip(DILATIONS, out_refs):
        for r in range(d):
            rows = pl.ds(0, ts) if d == 1 else pl.ds(r, ts // d, stride=d)
            for j in range(N_STAGE):
                ref[r, :, j * 128:(j + 1) * 128] = stage[j][rows, :].astype(bf16)


def _in_proj(x, g1, w_in, cos_t, sin_t):
    S = x.shape[0]
    ts = 512

    def body(x_ref, g_ref, w_ref, cos_ref, sin_ref, h_ref, u_ref, *rest):
        outs, stage = rest[:-N_STAGE], rest[-N_STAGE:]
        xv = x_ref[...]
        r = lax.rsqrt(jnp.mean(xv * xv, axis=-1, keepdims=True) + EPS)
        h = ((xv * r) * g_ref[...]).astype(bf16)
        h_ref[...] = h
        proj = _dot_nt(h, w_ref[...])
        u_ref[...] = proj[:, :POOL_WIDTH]
        cos = cos_ref[...]
        sin = sin_ref[...]
        first = _first_half_mask(ts)
        n_dil = len(DILATIONS)
        for which, base in enumerate((POOL_WIDTH, POOL_WIDTH + ATTN_WIDTH)):
            for j in range(ATTN_WIDTH // 128):
                a = proj[:, base + j * 128: base + (j + 1) * 128]
                if which == 0:
                    a = a * ATTN_SCALE
                stage[j][...] = a * cos + _rope_partner(a, first) * sin
            _store_streams(stage, outs[which * n_dil:(which + 1) * n_dil], ts)
        for j in range(ATTN_WIDTH // 128):
            base = POOL_WIDTH + 2 * ATTN_WIDTH + j * 128
            stage[j][...] = proj[:, base:base + 128]
        _store_streams(stage, outs[2 * n_dil:], ts)

    row = lambda w: pl.BlockSpec((ts, w), lambda i: (i, 0))
    streams = [_stream_spec(d, ts) for d in DILATIONS] * 3
    res = pl.pallas_call(
        body, name="in_proj", grid=(S // ts,),
        in_specs=[row(D_MODEL), pl.BlockSpec((1, D_MODEL), lambda i: (0, 0)),
                  pl.BlockSpec((IN_WIDTH, D_MODEL), lambda i: (0, 0)), row(128), row(128)],
        out_specs=[row(D_MODEL), row(POOL_WIDTH)] + streams,
        out_shape=[jax.ShapeDtypeStruct((S, D_MODEL), bf16), jax.ShapeDtypeStruct((S, POOL_WIDTH), f32)]
        + [_stream_shape(S, d) for d in DILATIONS] * 3,
        scratch_shapes=_stage_scratch(ts),
        compiler_params=_params("parallel"),
    )(x, g1, w_in, cos_t, sin_t)
    n = len(DILATIONS)
    return res[0], res[1], res[2:2 + n], res[2 + n:2 + 2 * n], res[2 + 2 * n:]


POOL_HALO = 16


def _pool_lane_group(rows):
    return lax.broadcasted_iota(jnp.int32, (rows, POOL_WIDTH), 1) // POOL_GROUP


def _pool_select(group, s2, s4, s8, s16):
    return jnp.where(group == 0, s2, jnp.where(group == 1, s4, jnp.where(group == 2, s8, s16)))


def _pool_count(t0, rows):
    group = _pool_lane_group(rows)
    t = t0 + lax.broadcasted_iota(jnp.int32, (rows, POOL_WIDTH), 0)
    win = _pool_select(group, 2, 4, 8, 16)
    return jnp.minimum(t + 1, win).astype(f32)


def _pool_diff(u_halo, u_tile, t0):
    ts = u_tile.shape[0]
    ext = jnp.concatenate([u_halo, u_tile], axis=0)
    s2 = ext + pltpu.roll(ext, 1, 0)
    s4 = s2 + pltpu.roll(s2, 2, 0)
    s8 = s4 + pltpu.roll(s4, 4, 0)
    s16 = s8 + pltpu.roll(s8, 8, 0)
    group = _pool_lane_group(ts + POOL_HALO)
    wsum = _pool_select(group, s2, s4, s8, s16)[POOL_HALO:]
    return wsum / _pool_count(t0, ts) - u_tile


def _pool_specs(ts, n_tiles):
    tile = pl.BlockSpec((ts, POOL_WIDTH), lambda i: (i, 0))
    per = ts // POOL_HALO
    before = pl.BlockSpec((POOL_HALO, POOL_WIDTH), lambda i: (jnp.maximum(i * per - 1, 0), 0))
    after = pl.BlockSpec((POOL_HALO, POOL_WIDTH), lambda i: (jnp.minimum((i + 1) * per, n_tiles * per - 1), 0))
    return tile, before, after


def _pool_fwd(u, w_bd, scale):
    S = u.shape[0]
    ts = 2048
    n_tiles = S // ts

    def body(u_ref, halo_ref, w_ref, sc_ref, y_ref):
        i = pl.program_id(0)
        halo = jnp.where(i > 0, halo_ref[...], 0.0)
        d = _pool_diff(halo, u_ref[...], i * ts)
        y_ref[...] = (_dot(d.astype(bf16), w_ref[...]) * sc_ref[...]).astype(bf16)

    tile, before, _ = _pool_specs(ts, n_tiles)
    return pl.pallas_call(
        body, name="pool_fwd", grid=(n_tiles,),
        in_specs=[tile, before, pl.BlockSpec((POOL_WIDTH, POOL_WIDTH), lambda i: (0, 0)),
                  pl.BlockSpec((1, POOL_WIDTH), lambda i: (0, 0))],
        out_specs=tile, out_shape=jax.ShapeDtypeStruct((S, POOL_WIDTH), bf16),
        compiler_params=_params("parallel"),
    )(u, u, w_bd, scale)


def _pool_bwd(u, dy, w_bd, scale):
    S = u.shape[0]
    ts = 2048
    n_tiles = S // ts

    def body(u_ref, halo_ref, dy_ref, dy_next_ref, w_ref, sc_ref, du_ref, dw_ref, dsc_ref):
        i = pl.program_id(0)

        @pl.when(i == 0)
        def _():
            dw_ref[...] = jnp.zeros_like(dw_ref)
            dsc_ref[...] = jnp.zeros_like(dsc_ref)

        halo = jnp.where(i > 0, halo_ref[...], 0.0)
        d = _pool_diff(halo, u_ref[...], i * ts).astype(bf16)
        w = w_ref[...]
        sc = sc_ref[...]
        dy_tile = dy_ref[...]
        z = _dot(d, w)
        dsc_ref[...] += jnp.sum(dy_tile * z, axis=0, keepdims=True)
        dy_next = jnp.where(i < n_tiles - 1, dy_next_ref[...], 0.0)
        dz = (jnp.concatenate([dy_tile, dy_next], axis=0) * sc).astype(bf16)
        dw_ref[...] += _dot_tn(d, dz[:ts])
        dd = _dot_nt(dz, w)
        e = dd / _pool_count(i * ts, ts + POOL_HALO)
        n = ts + POOL_HALO
        f2 = e + pltpu.roll(e, n - 1, 0)
        f4 = f2 + pltpu.roll(f2, n - 2, 0)
        f8 = f4 + pltpu.roll(f4, n - 4, 0)
        f16 = f8 + pltpu.roll(f8, n - 8, 0)
        fsum = _pool_select(_pool_lane_group(n), f2, f4, f8, f16)
        du_ref[...] = (fsum[:ts] - dd[:ts]).astype(bf16)

    tile, before, after = _pool_specs(ts, n_tiles)
    return pl.pallas_call(
        body, name="pool_bwd", grid=(n_tiles,),
        in_specs=[tile, before, tile, after, pl.BlockSpec((POOL_WIDTH, POOL_WIDTH), lambda i: (0, 0)),
                  pl.BlockSpec((1, POOL_WIDTH), lambda i: (0, 0))],
        out_specs=[tile, pl.BlockSpec((POOL_WIDTH, POOL_WIDTH), lambda i: (0, 0)),
                   pl.BlockSpec((1, POOL_WIDTH), lambda i: (0, 0))],
        out_shape=[jax.ShapeDtypeStruct((S, POOL_WIDTH), bf16), jax.ShapeDtypeStruct((POOL_WIDTH, POOL_WIDTH), f32),
                   jax.ShapeDtypeStruct((1, POOL_WIDTH), f32)],
        compiler_params=_params("arbitrary"),
    )(u, u, dy, dy, w_bd, scale)


SUPER = BLOCK * DILATIONS[-1]
UNITS = SUPER // BLOCK
FWD_UNROLL = 16
BWD_UNROLL = 16


def _band_mask(has_prev):
    qi = lax.broadcasted_iota(jnp.int32, (BLOCK, 2 * BLOCK), 0)
    kj = lax.broadcasted_iota(jnp.int32, (BLOCK, 2 * BLOCK), 1)
    return (kj >= qi) & (kj <= qi + BLOCK) & ((kj >= BLOCK) | has_prev)


def _head0_mask(rows=BLOCK):
    return lax.broadcasted_iota(jnp.int32, (rows, 128), 1) < HEAD_DIM


def _band_mask_t(has_prev):
    ki = lax.broadcasted_iota(jnp.int32, (2 * BLOCK, 2 * BLOCK), 0)
    qj = lax.broadcasted_iota(jnp.int32, (2 * BLOCK, 2 * BLOCK), 1) % BLOCK
    return (ki >= qj) & (ki <= qj + BLOCK) & ((ki >= BLOCK) | has_prev)


def _head_pair_rows(a, h0):
    zero = jnp.zeros_like(a)
    return jnp.concatenate([jnp.where(h0, a, zero), jnp.where(h0, zero, a)], axis=0)


def _per_query_row(stat):
    t = stat.T
    return jnp.concatenate([t[0:1], t[HEAD_DIM:HEAD_DIM + 1]], axis=1)


def _natural_rows(d, r, n):
    if d == 1:
        return pl.ds(pl.multiple_of(n * BLOCK, BLOCK), BLOCK)
    return pl.ds(n * (BLOCK * d) + r, BLOCK, stride=d)


def _unit_place(d, u):
    per_stream = UNITS // d
    return u // per_stream, u % per_stream, per_stream


def _block_rows(n):
    return pl.ds(pl.multiple_of(n * BLOCK, BLOCK), BLOCK)


def _band(cur_ref, tail_ref, r, n):
    before = jnp.where(n > 0, cur_ref[r, _block_rows(jnp.maximum(n - 1, 0)), :], tail_ref[r])
    return jnp.concatenate([before, cur_ref[r, _block_rows(n), :]], axis=0)


def _attn_in_specs(S, with_do):
    specs = []
    last = S // SUPER - 1
    for d in DILATIONS:
        per_stream = UNITS // d
        cur = pl.BlockSpec((d, SUPER // d, 128), lambda hp, sb: (0, jnp.minimum(sb, last), hp))
        tail = pl.BlockSpec(
            (d, BLOCK, 128),
            lambda hp, sb, per_stream=per_stream: (0, jnp.maximum(jnp.minimum(sb, last) * per_stream - 1, 0), hp))
        specs += [cur] * (2 if with_do else 1) + [cur, tail, cur, tail]
    return specs


def _attn_fwd(qs, ks, vs, pack):
    S = qs[0].shape[1]
    n_dil = len(DILATIONS)
    n_steps = S // SUPER
    n_total = (ATTN_WIDTH // 128) * n_steps

    def body(*refs):
        ins, pack_ref = refs[:5 * n_dil], refs[5 * n_dil]
        out_ref, lse_ref, gathered_ref = refs[5 * n_dil + 1:5 * n_dil + 4]
        scratch = refs[5 * n_dil + 4:]
        o_sc, l_sc = scratch[:n_dil], scratch[n_dil:2 * n_dil]
        gather = _Gather(pack_ref, gathered_ref, *scratch[2 * n_dil:])
        sb = pl.program_id(1)
        step = pl.program_id(0) * n_steps + sb

        @pl.when(step == 0)
        def _():
            gather.start()

        h0 = _head0_mask()
        for ci, d in enumerate(DILATIONS):
            q_ref, kc_ref, kp_ref, vc_ref, vp_ref = ins[5 * ci:5 * ci + 5]

            def unit(u, carry, d=d, ci=ci, q_ref=q_ref, kc_ref=kc_ref, kp_ref=kp_ref, vc_ref=vc_ref, vp_ref=vp_ref):
                r, n, _ = _unit_place(d, u)
                qv = q_ref[r, _block_rows(n), :]
                kb = _band(kc_ref, kp_ref, r, n)
                vb = _band(vc_ref, vp_ref, r, n)
                valid = _band_mask((sb > 0) | (n > 0))
                s = jnp.where(jnp.concatenate([valid, valid], axis=0), _dot_nt(_head_pair_rows(qv, h0), kb), NEG)
                m = jnp.max(s, axis=1, keepdims=True)
                e = jnp.exp(s - m)
                den = jnp.sum(e, axis=1, keepdims=True)
                o_pair = _dot(e.astype(bf16), vb) * (1.0 / den)
                lse_pair = jnp.broadcast_to(m + jnp.log(den), (2 * BLOCK, 128))
                rows = _natural_rows(d, r, n)
                o_sc[ci][rows, :] = jnp.where(h0, o_pair[:BLOCK], o_pair[BLOCK:])
                l_sc[ci][rows, :] = jnp.where(h0, lse_pair[:BLOCK], lse_pair[BLOCK:])
                return carry

            lax.fori_loop(0, UNITS, unit, 0, unroll=FWD_UNROLL)

        def merge(t, carry):
            rows = pl.ds(pl.multiple_of(t * 256, 256), 256)
            a, b, c = l_sc[0][rows, :], l_sc[1][rows, :], l_sc[2][rows, :]
            m = jnp.maximum(jnp.maximum(a, b), c)
            ea, eb, ec = jnp.exp(a - m), jnp.exp(b - m), jnp.exp(c - m)
            tot = ea + eb + ec
            out_ref[rows, :] = ((ea / tot) * o_sc[0][rows, :] + (eb / tot) * o_sc[1][rows, :]
                                + (ec / tot) * o_sc[2][rows, :]).astype(bf16)
            lse_ref[rows, :] = m + jnp.log(tot)
            return carry

        lax.fori_loop(0, SUPER // 256, merge, 0)

        @pl.when(step == (2 * n_total) // 3)
        def _():
            gather.pass_on()

        @pl.when(step == n_total - 1)
        def _():
            gather.finish()

    args = []
    for q, k, v in zip(qs, ks, vs):
        args += [q, k, k, v, v]
    nat = pl.BlockSpec((SUPER, 128), lambda hp, sb: (sb, hp))
    rows = pack.shape[0]
    return pl.pallas_call(
        body, name="attn_fwd", grid=(ATTN_WIDTH // 128, n_steps),
        in_specs=_attn_in_specs(S, False) + [ANY], out_specs=[nat, nat, ANY],
        out_shape=[jax.ShapeDtypeStruct((S, ATTN_WIDTH), bf16), jax.ShapeDtypeStruct((S, ATTN_WIDTH), f32),
                   _Gather.out_shape(rows, pack.dtype)],
        scratch_shapes=[pltpu.VMEM((SUPER, 128), f32)] * (2 * n_dil) + _Gather.scratch(rows, pack.dtype),
        compiler_params=_params("arbitrary", "arbitrary"),
    )(*args, pack)


def _attn_bwd(qs, ks, vs, dos, lse, delta, chip_sum):
    S = qs[0].shape[1]
    n_steps = S // SUPER
    last = n_steps - 1
    n_dil = len(DILATIONS)
    n_total = (ATTN_WIDTH // 128) * (n_steps + 1)

    def body(*refs):
        ins, (lse_ref, dl_ref, sum_ref) = refs[:6 * n_dil], refs[6 * n_dil:6 * n_dil + 3]
        dq_ref, dk_ref, dv_ref, others_ref = refs[6 * n_dil + 3:6 * n_dil + 7]
        dq_acc, dk_acc, dv_acc = refs[6 * n_dil + 7:6 * n_dil + 10]
        scatter = _Scatter(sum_ref, others_ref, *refs[6 * n_dil + 10:])
        sb = pl.program_id(1)
        step = pl.program_id(0) * (n_steps + 1) + sb
        cur = sb % 2
        prv = 1 - cur

        @pl.when(step == 0)
        def _():
            scatter.start()

        @pl.when(sb < n_steps)
        def _():
            dq_acc[...] = jnp.zeros_like(dq_acc)
            dk_acc[cur] = jnp.zeros((SUPER, 128), f32)
            dv_acc[cur] = jnp.zeros((SUPER, 128), f32)
            h0 = _head0_mask()
            for ci, d in enumerate(DILATIONS):
                q_ref, do_ref, kc_ref, kp_ref, vc_ref, vp_ref = ins[6 * ci:6 * ci + 6]

                def unit(u, carry, d=d, q_ref=q_ref, do_ref=do_ref, kc_ref=kc_ref, kp_ref=kp_ref, vc_ref=vc_ref,
                         vp_ref=vp_ref):
                    r, n, per_stream = _unit_place(d, u)
                    qv = q_ref[r, _block_rows(n), :]
                    dov = do_ref[r, _block_rows(n), :]
                    kb = _band(kc_ref, kp_ref, r, n)
                    vb = _band(vc_ref, vp_ref, r, n)
                    rows = _natural_rows(d, r, n)
                    has_prev = (sb > 0) | (n > 0)
                    q_pair = _head_pair_rows(qv, h0)
                    do_pair = _head_pair_rows(dov, h0)
                    s_t = jnp.where(_band_mask_t(has_prev), _dot_nt(kb, q_pair), NEG)
                    p_t = jnp.exp(s_t - _per_query_row(lse_ref[rows, :]))
                    dp_t = _dot_nt(vb, do_pair)
                    ds_t = (p_t * (dp_t - _per_query_row(dl_ref[rows, :]))).astype(bf16)
                    dvb = _dot(p_t.astype(bf16), do_pair)
                    dkb = _dot(ds_t, q_pair)
                    dq_pair = _dot_tn(ds_t, kb)
                    dq_acc[rows, :] += jnp.where(h0, dq_pair[:BLOCK], dq_pair[BLOCK:])
                    dk_acc[cur, rows, :] += dkb[BLOCK:]
                    dv_acc[cur, rows, :] += dvb[BLOCK:]

                    slot = jnp.where((n > 0) | (sb == 0), cur, prv)
                    before = _natural_rows(d, r, jnp.where(n > 0, n - 1, per_stream - 1))
                    dk_acc[slot, before, :] += dkb[:BLOCK]
                    dv_acc[slot, before, :] += dvb[:BLOCK]
                    return carry

                lax.fori_loop(0, UNITS, unit, 0, unroll=BWD_UNROLL)
            dq_ref[...] = (dq_acc[...] * ATTN_SCALE).astype(bf16)

        @pl.when(sb > 0)
        def _():
            dk_ref[...] = dk_acc[prv].astype(bf16)
            dv_ref[...] = dv_acc[prv].astype(bf16)

        @pl.when(step == n_total - 1)
        def _():
            scatter.finish()

    args = []
    for q, k, v, do in zip(qs, ks, vs, dos):
        args += [q, do, k, k, v, v]
    nat = pl.BlockSpec((SUPER, 128), lambda hp, sb: (jnp.minimum(sb, last), hp))
    nat_before = pl.BlockSpec((SUPER, 128), lambda hp, sb: (jnp.clip(sb - 1, 0, last), hp))
    out = jax.ShapeDtypeStruct((S, ATTN_WIDTH), bf16)
    half = chip_sum.shape[1]
    return pl.pallas_call(
        body, name="attn_bwd", grid=(ATTN_WIDTH // 128, n_steps + 1),
        in_specs=_attn_in_specs(S, True) + [nat, nat, ANY], out_specs=[nat, nat_before, nat_before, ANY],
        out_shape=[out, out, out, _Scatter.out_shape(half, chip_sum.dtype)],
        scratch_shapes=[pltpu.VMEM((SUPER, 128), f32), pltpu.VMEM((2, SUPER, 128), f32),
                        pltpu.VMEM((2, SUPER, 128), f32)] + _Scatter.scratch(half),
        compiler_params=_params("arbitrary", "arbitrary"),
    )(*args, lse, delta, chip_sum)


def _rms(v):
    return lax.rsqrt(jnp.mean(v * v, axis=-1, keepdims=True) + EPS)


def _out_proj(pool_out, attn_out, w_out, x, g2, g3):
    S = x.shape[0]
    ts = 1024

    def body(p_ref, a_ref, w_ref, x_ref, g2_ref, g3_ref, mix_ref, x2_ref, h2_ref):
        mix = _dot(p_ref[...], w_ref[:POOL_WIDTH, :]) + _dot(a_ref[...], w_ref[POOL_WIDTH:, :])
        mix_ref[...] = mix
        x2 = x_ref[...] + (mix * _rms(mix)) * g2_ref[...]
        x2_ref[...] = x2
        h2_ref[...] = ((x2 * _rms(x2)) * g3_ref[...]).astype(bf16)

    row = lambda w: pl.BlockSpec((ts, w), lambda i: (i, 0))
    gain = pl.BlockSpec((1, D_MODEL), lambda i: (0, 0))
    return pl.pallas_call(
        body, name="out_proj", grid=(S // ts,),
        in_specs=[row(POOL_WIDTH), row(ATTN_WIDTH), pl.BlockSpec((D_MODEL, D_MODEL), lambda i: (0, 0)),
                  row(D_MODEL), gain, gain],
        out_specs=[row(D_MODEL)] * 3,
        out_shape=[jax.ShapeDtypeStruct((S, D_MODEL), f32), jax.ShapeDtypeStruct((S, D_MODEL), f32),
                   jax.ShapeDtypeStruct((S, D_MODEL), bf16)],
        compiler_params=_params("parallel"),
    )(pool_out, attn_out, w_out, x, g2, g3)


FF_TILE = 256
FF_STEP_ROWS = 2048
FF_ROWS = 512
FF_BWD_ROWS = 256


def _sigmoid(g):
    return 1.0 / (1.0 + jnp.exp(-g))


def _ff_act_shape(S):
    return jax.ShapeDtypeStruct((D_FF // FF_TILE, S, FF_TILE), bf16)


def _ff_act_spec(ts):
    return pl.BlockSpec((1, ts, FF_TILE), lambda i, j: (j, i, 0))


def _ffn_fwd(h2, w_gate, w_up, w_down):
    S = h2.shape[0]
    ts = min(S, FF_STEP_ROWS)

    n_row_tiles = S // ts

    def body(h_hbm, wg_ref, wu_ref, wd_ref, gate_ref, up_ref, f_ref, h_buf, h_sem):
        tile, col = pl.program_id(0), pl.program_id(1)
        slot = tile % 2

        def fetch(t, s):
            return pltpu.make_async_copy(h_hbm.at[pl.ds(pl.multiple_of(t * ts, ts), ts)], h_buf.at[s], h_sem.at[s])

        @pl.when((tile == 0) & (col == 0))
        def _():
            fetch(0, 0).start()

        @pl.when(col == 0)
        def _():
            fetch(tile, slot).wait()

        @pl.when((col == 0) & (tile + 1 < n_row_tiles))
        def _():
            fetch(tile + 1, 1 - slot).start()

        def rows_pass(first):
            def sub(i, carry):
                rows = pl.ds(pl.multiple_of(i * FF_ROWS, FF_ROWS), FF_ROWS)
                h = h_buf[slot, rows, :]
                gate = _dot_nt(h, wg_ref[...])
                up = _dot_nt(h, wu_ref[...])
                gate_ref[0, rows, :] = gate.astype(bf16)
                up_ref[0, rows, :] = up.astype(bf16)
                part = _dot((gate * _sigmoid(gate) * up).astype(bf16), wd_ref[...])
                if first:
                    f_ref[rows, :] = part
                else:
                    f_ref[rows, :] += part
                return carry

            lax.fori_loop(0, ts // FF_ROWS, sub, 0, unroll=True)

        @pl.when(pl.program_id(1) == 0)
        def _():
            rows_pass(True)

        @pl.when(pl.program_id(1) > 0)
        def _():
            rows_pass(False)

    act = _ff_act_spec(ts)
    weight = pl.BlockSpec((FF_TILE, D_MODEL), lambda i, j: (j, 0))
    return pl.pallas_call(
        body, name="ffn_fwd", grid=(S // ts, D_FF // FF_TILE),
        in_specs=[pl.BlockSpec(memory_space=pl.ANY), weight, weight, weight],
        out_specs=[act, act, pl.BlockSpec((ts, D_MODEL), lambda i, j: (i, 0))],
        out_shape=[_ff_act_shape(S), _ff_act_shape(S), jax.ShapeDtypeStruct((S, D_MODEL), f32)],
        scratch_shapes=[pltpu.VMEM((2, ts, D_MODEL), bf16), pltpu.SemaphoreType.DMA((2,))],
        compiler_params=_params("arbitrary", "arbitrary"),
    )(h2, w_gate, w_up, w_down)


def _loss_head(f, x2, target, g4):
    S = f.shape[0]
    ts = 1024

    def body(f_ref, x2_ref, t_ref, g_ref, dy_ref, df_ref, dg_ref, loss_ref):
        @pl.when(pl.program_id(0) == 0)
        def _():
            dg_ref[...] = jnp.zeros_like(dg_ref)
            loss_ref[...] = jnp.zeros_like(loss_ref)

        fv = f_ref[...]
        g = g_ref[...]
        r = _rms(fv)
        fhat = fv * r
        err = (x2_ref[...] + fhat * g) - t_ref[...]
        loss_ref[...] += 0.5 * jnp.sum(jnp.mean(err * err, axis=-1, keepdims=True), axis=0, keepdims=True)
        dy = err * (1.0 / D_MODEL)
        dy_ref[...] = dy
        dg_ref[...] += jnp.sum(dy * fhat, axis=0, keepdims=True)
        dyg = dy * g
        df_ref[...] = (r * (dyg - fhat * jnp.mean(dyg * fhat, axis=-1, keepdims=True))).astype(bf16)

    row = pl.BlockSpec((ts, D_MODEL), lambda i: (i, 0))
    gain = pl.BlockSpec((1, D_MODEL), lambda i: (0, 0))
    return pl.pallas_call(
        body, name="loss_head", grid=(S // ts,), in_specs=[row, row, row, gain],
        out_specs=[row, row, gain, pl.BlockSpec((1, 1), lambda i: (0, 0))],
        out_shape=[jax.ShapeDtypeStruct((S, D_MODEL), f32), jax.ShapeDtypeStruct((S, D_MODEL), bf16),
                   jax.ShapeDtypeStruct((1, D_MODEL), f32), jax.ShapeDtypeStruct((1, 1), f32)],
        compiler_params=_params("arbitrary"),
    )(f, x2, target, g4)


def _ffn_bwd(df, gate, up, w_gate, w_up, w_down):
    S = df.shape[0]
    ts = min(S, FF_STEP_ROWS)

    def body(df_ref, gate_ref, up_ref, wg_ref, wu_ref, wd_ref, a_ref, dgate_ref, dup_ref, dh_ref):
        def rows_pass(first):
            def sub(i, carry):
                rows = pl.ds(pl.multiple_of(i * FF_BWD_ROWS, FF_BWD_ROWS), FF_BWD_ROWS)
                da = _dot_nt(df_ref[rows, :], wd_ref[...])
                g = gate_ref[0, rows, :].astype(f32)
                u = up_ref[0, rows, :].astype(f32)
                sig = _sigmoid(g)
                silu = g * sig
                a_ref[0, rows, :] = (silu * u).astype(bf16)
                dup = (da * silu).astype(bf16)
                dgate = (da * u * (sig * (1.0 + g * (1.0 - sig)))).astype(bf16)
                dup_ref[0, rows, :] = dup
                dgate_ref[0, rows, :] = dgate
                part = _dot(dgate, wg_ref[...]) + _dot(dup, wu_ref[...])
                if first:
                    dh_ref[rows, :] = part
                else:
                    dh_ref[rows, :] += part
                return carry

            lax.fori_loop(0, ts // FF_BWD_ROWS, sub, 0, unroll=True)

        @pl.when(pl.program_id(1) == 0)
        def _():
            rows_pass(True)

        @pl.when(pl.program_id(1) > 0)
        def _():
            rows_pass(False)

    act = _ff_act_spec(ts)
    row = pl.BlockSpec((ts, D_MODEL), lambda i, j: (i, 0))
    return pl.pallas_call(
        body, name="ffn_bwd", grid=(S // ts, D_FF // FF_TILE),
        in_specs=[row, act, act,
                  pl.BlockSpec((FF_TILE, D_MODEL), lambda i, j: (j, 0)),
                  pl.BlockSpec((FF_TILE, D_MODEL), lambda i, j: (j, 0)),
                  pl.BlockSpec((FF_TILE, D_MODEL), lambda i, j: (j, 0))],
        out_specs=[act, act, act, row],
        out_shape=[_ff_act_shape(S)] * 3 + [jax.ShapeDtypeStruct((S, D_MODEL), f32)],
        compiler_params=_params("parallel", "arbitrary"),
    )(df, gate, up, w_gate, w_up, w_down)


def _norm_bwd(dh2, dy, x2, mix, g3, g2):
    S = dh2.shape[0]
    ts = 512

    def body(dh_ref, dy_ref, x2_ref, mix_ref, g3_ref, g2_ref, dx2_ref, dmix_ref, dg3_ref, dg2_ref):
        @pl.when(pl.program_id(0) == 0)
        def _():
            dg3_ref[...] = jnp.zeros_like(dg3_ref)
            dg2_ref[...] = jnp.zeros_like(dg2_ref)

        dh = dh_ref[...]
        x2 = x2_ref[...]
        r3 = _rms(x2)
        xhat = x2 * r3
        dg3_ref[...] += jnp.sum(dh * xhat, axis=0, keepdims=True)
        dhg = dh * g3_ref[...]
        dx2 = dy_ref[...] + r3 * (dhg - xhat * jnp.mean(dhg * xhat, axis=-1, keepdims=True))
        dx2_ref[...] = dx2
        mix = mix_ref[...]
        r2 = _rms(mix)
        mhat = mix * r2
        dg2_ref[...] += jnp.sum(dx2 * mhat, axis=0, keepdims=True)
        dmg = dx2 * g2_ref[...]
        dmix_ref[...] = (r2 * (dmg - mhat * jnp.mean(dmg * mhat, axis=-1, keepdims=True))).astype(bf16)

    row = pl.BlockSpec((ts, D_MODEL), lambda i: (i, 0))
    gain = pl.BlockSpec((1, D_MODEL), lambda i: (0, 0))
    return pl.pallas_call(
        body, name="norm_bwd", grid=(S // ts,), in_specs=[row, row, row, row, gain, gain],
        out_specs=[row, row, gain, gain],
        out_shape=[jax.ShapeDtypeStruct((S, D_MODEL), f32), jax.ShapeDtypeStruct((S, D_MODEL), bf16),
                   jax.ShapeDtypeStruct((1, D_MODEL), f32), jax.ShapeDtypeStruct((1, D_MODEL), f32)],
        compiler_params=_params("arbitrary"),
    )(dh2, dy, x2, mix, g3, g2)


def _out_proj_bwd(dmix, w_out, attn_out, head_ones, grads):
    S = dmix.shape[0]
    ts = 512
    n_dil = len(DILATIONS)

    def body(dm_ref, w_ref, o_ref, ones_ref, g_ref, dp_ref, dl_ref, *rest):
        do_refs, theirs_ref = rest[:n_dil], rest[n_dil]
        stage = rest[n_dil + 1:n_dil + 1 + N_STAGE]
        swap = _Swap(g_ref, theirs_ref, *rest[n_dil + 1 + N_STAGE:])

        @pl.when(pl.program_id(0) == 0)
        def _():
            swap.start()

        @pl.when(pl.program_id(0) == S // ts - 1)
        def _():
            swap.finish()

        dcat = _dot_nt(dm_ref[...], w_ref[...])
        dp_ref[...] = dcat[:, :POOL_WIDTH]
        do = dcat[:, POOL_WIDTH:]
        for j in range(ATTN_WIDTH // 128):
            stage[j][...] = do[:, j * 128:(j + 1) * 128]
        _store_streams(stage, do_refs, ts)
        prod = do * o_ref[...].astype(f32)
        hi = prod.astype(bf16)
        lo = (prod - hi.astype(f32)).astype(bf16)
        ones = ones_ref[...]
        for j in range(ATTN_WIDTH // 128):
            cols = slice(j * 128, (j + 1) * 128)
            dl_ref[:, cols] = _dot(hi[:, cols], ones) + _dot(lo[:, cols], ones)

    row = lambda w: pl.BlockSpec((ts, w), lambda i: (i, 0))
    res = pl.pallas_call(
        body, name="out_proj_bwd", grid=(S // ts,),
        in_specs=[row(D_MODEL), pl.BlockSpec((D_MODEL, D_MODEL), lambda i: (0, 0)), row(ATTN_WIDTH),
                  pl.BlockSpec((128, 128), lambda i: (0, 0)), ANY],
        out_specs=[row(POOL_WIDTH), row(ATTN_WIDTH)] + [_stream_spec(d, ts) for d in DILATIONS] + [ANY],
        out_shape=[jax.ShapeDtypeStruct((S, POOL_WIDTH), f32), jax.ShapeDtypeStruct((S, ATTN_WIDTH), f32)]
        + [_stream_shape(S, d) for d in DILATIONS] + [_Swap.out_shape(grads)],
        scratch_shapes=_stage_scratch(ts) + _Swap.scratch(grads),
        compiler_params=_params("arbitrary"),
    )(dmix, w_out, attn_out, head_ones, grads)
    return res[0], res[1], res[2:2 + n_dil], res[2 + n_dil]


IN_BWD_ROWS = 256


def _in_proj_bwd(du, dq, dk, dv, cos_t, sin_t, w_in, x, dx2, g1):
    S = x.shape[0]
    ts = 512

    def body(du_ref, dq_ref, dk_ref, dv_ref, cos_ref, sin_ref, w_ref, x_ref, dx2_ref, g_ref, gx_ref, dproj_ref, dg_ref):
        @pl.when(pl.program_id(0) == 0)
        def _():
            dg_ref[...] = jnp.zeros_like(dg_ref)

        first = _first_half_mask(IN_BWD_ROWS)

        def sub(i, carry):
            rows = pl.ds(pl.multiple_of(i * IN_BWD_ROWS, IN_BWD_ROWS), IN_BWD_ROWS)
            dproj_ref[rows, :POOL_WIDTH] = du_ref[rows, :]
            cos = cos_ref[rows, :]
            sin = sin_ref[rows, :]
            for j in range(ATTN_WIDTH // 128):
                cols = slice(j * 128, (j + 1) * 128)
                for base, ref in ((POOL_WIDTH, dq_ref), (POOL_WIDTH + ATTN_WIDTH, dk_ref)):
                    g = ref[rows, cols].astype(f32)
                    pre = g * cos + _rope_partner(g * sin, first)
                    dproj_ref[rows, base + j * 128: base + (j + 1) * 128] = pre.astype(bf16)
            dproj_ref[rows, POOL_WIDTH + 2 * ATTN_WIDTH:] = dv_ref[rows, :]

            dh = _dot(dproj_ref[rows, :], w_ref[...])
            xv = x_ref[rows, :]
            r = _rms(xv)
            xhat = xv * r
            dg_ref[...] += jnp.sum(dh * xhat, axis=0, keepdims=True)
            dhg = dh * g_ref[...]
            gx_ref[rows, :] = dx2_ref[rows, :] + r * (dhg - xhat * jnp.mean(dhg * xhat, axis=-1, keepdims=True))
            return carry

        lax.fori_loop(0, ts // IN_BWD_ROWS, sub, 0, unroll=True)

    row = lambda w: pl.BlockSpec((ts, w), lambda i: (i, 0))
    gain = pl.BlockSpec((1, D_MODEL), lambda i: (0, 0))
    return pl.pallas_call(
        body, name="in_proj_bwd", grid=(S // ts,),
        in_specs=[row(POOL_WIDTH)] + [row(ATTN_WIDTH)] * 3 + [row(128), row(128),
                  pl.BlockSpec((IN_WIDTH, D_MODEL), lambda i: (0, 0)), row(D_MODEL), row(D_MODEL), gain],
        out_specs=[row(D_MODEL), row(IN_WIDTH), gain],
        out_shape=[jax.ShapeDtypeStruct((S, D_MODEL), f32), jax.ShapeDtypeStruct((S, IN_WIDTH), bf16),
                   jax.ShapeDtypeStruct((1, D_MODEL), f32)],
        compiler_params=_params("arbitrary"),
    )(du, dq, dk, dv, cos_t, sin_t, w_in, x, dx2, g1)


def _matmul_tiles_tn(a, b, name):
    T, K, w = a.shape
    N = b.shape[1]
    tk = 1024

    def body(a_ref, b_ref, o_ref):
        def tiles_pass(first):
            for t in range(T):
                part = _dot_tn(a_ref[t], b_ref[...])
                if first:
                    o_ref[t * w:(t + 1) * w, :] = part
                else:
                    o_ref[t * w:(t + 1) * w, :] += part

        @pl.when(pl.program_id(0) == 0)
        def _():
            tiles_pass(True)

        @pl.when(pl.program_id(0) > 0)
        def _():
            tiles_pass(False)

    return pl.pallas_call(
        body, name=name, grid=(K // tk,),
        in_specs=[pl.BlockSpec((T, tk, w), lambda k: (0, k, 0)), pl.BlockSpec((tk, N), lambda k: (k, 0))],
        out_specs=pl.BlockSpec((T * w, N), lambda k: (0, 0)),
        out_shape=jax.ShapeDtypeStruct((T * w, N), f32),
        compiler_params=_params("arbitrary"),
    )(a, b)


def _matmul_tn(a, b, name):
    K, M = a.shape
    N = b.shape[1]
    tk = 1024

    def body(a_ref, b_ref, o_ref):
        _tn_step(a_ref, b_ref, o_ref, M)

    return pl.pallas_call(
        body, name=name, grid=(K // tk,),
        in_specs=[pl.BlockSpec((tk, M), lambda k: (k, 0)), pl.BlockSpec((tk, N), lambda k: (k, 0))],
        out_specs=pl.BlockSpec((M, N), lambda k: (0, 0)),
        out_shape=jax.ShapeDtypeStruct((M, N), f32),
        compiler_params=_params("arbitrary"),
    )(a, b)


def _tn_step(a_ref, b_ref, o_ref, M):
    w = 256

    def tiles_pass(first):
        for t in range(M // w):
            part = _dot_tn(a_ref[:, t * w:(t + 1) * w], b_ref[...])
            if first:
                o_ref[t * w:(t + 1) * w, :] = part
            else:
                o_ref[t * w:(t + 1) * w, :] += part

    @pl.when(pl.program_id(0) == 0)
    def _():
        tiles_pass(True)

    @pl.when(pl.program_id(0) > 0)
    def _():
        tiles_pass(False)


def _matmul_tn_and_small_sum(a, b, block, name):
    K, M = a.shape
    N = b.shape[1]
    tk = 1024
    n_steps = K // tk

    def body(a_ref, b_ref, block_ref, o_ref, total_ref, *scratch):
        small = _SmallSum(block_ref, *scratch)

        @pl.when(pl.program_id(0) == 0)
        def _():
            small.start()

        _tn_step(a_ref, b_ref, o_ref, M)

        @pl.when(pl.program_id(0) == n_steps - 1)
        def _():
            small.finish(total_ref)

    return pl.pallas_call(
        body, name=name, grid=(n_steps,),
        in_specs=[pl.BlockSpec((tk, M), lambda k: (k, 0)), pl.BlockSpec((tk, N), lambda k: (k, 0)), ANY],
        out_specs=[pl.BlockSpec((M, N), lambda k: (0, 0)), pl.BlockSpec(block.shape, lambda k: (0, 0))],
        out_shape=[jax.ShapeDtypeStruct((M, N), f32), jax.ShapeDtypeStruct(block.shape, block.dtype)],
        scratch_shapes=_SmallSum.scratch(block),
        compiler_params=_params("arbitrary"),
    )(a, b, block)


def _rope_tables(S):
    half = HEAD_DIM // 2
    freqs = ROPE_THETA ** (-jnp.arange(half, dtype=f32) * (2.0 / HEAD_DIM))
    ang = jnp.arange(S).astype(f32)[:, None] * freqs[None, :]
    cos = jnp.tile(jnp.cos(ang), (1, 4))
    sin = jnp.sin(ang)
    sin = jnp.tile(jnp.concatenate([-sin, sin], axis=1), (1, 2))
    return cos, sin


def _block_diag(w_pool):
    w = jnp.zeros((POOL_WIDTH, POOL_WIDTH), w_pool.dtype)
    for g in range(POOL_WIDTH // POOL_GROUP):
        w = lax.dynamic_update_slice(w, w_pool[g], (g * POOL_GROUP, g * POOL_GROUP))
    return w


def _head_ones():
    head = np.arange(128) // HEAD_DIM
    return jnp.asarray(head[:, None] == head[None, :], dtype=bf16)


def _place():
    x, y, c = lax.axis_index("x"), lax.axis_index("y"), lax.axis_index("c")
    chips = [(1 - x, y), (x, 1 - y), (1 - x, 1 - y)]
    return x, y, c, chips


ANY = pl.BlockSpec(memory_space=pl.ANY)
N_PEER_CHIPS = N_CHIPS - 1
ICI_PIECES = 4
D2D_PIECES = 8
LOCAL_PIECES = 8


def _row_chunks(rows, n, unit=32):
    units = rows // unit
    out, start = [], 0
    for i in range(n):
        size = (units // n + (1 if i < units % n else 0)) * unit
        out.append((start, size))
        start += size
    return [piece for piece in out if piece[1]]


class _LocalCopy:
    def __init__(self, src_rows, dst_rows, rows, buf, sems_in, sems_out):
        self.loads, self.stores = [], []
        for i, (start, size) in enumerate(_row_chunks(rows, LOCAL_PIECES)):
            r = pl.ds(start, size)
            self.loads.append(pltpu.make_async_copy(src_rows(r), buf.at[r], sems_in.at[i]))
            self.stores.append(pltpu.make_async_copy(buf.at[r], dst_rows(r), sems_out.at[i]))

    def start(self):
        for cp in self.loads:
            cp.start()

    def pass_on(self):
        for load, store in zip(self.loads, self.stores):
            load.wait()
            store.start()

    def finish(self):
        for store in self.stores:
            store.wait()

    @staticmethod
    def scratch(rows, dtype):
        return [pltpu.VMEM((rows, D_MODEL), dtype), pltpu.SemaphoreType.DMA((LOCAL_PIECES,)),
                pltpu.SemaphoreType.DMA((LOCAL_PIECES,))]


class _Gather:
    def __init__(self, w_ref, out_ref, send1, recv1, send2, recv2, buf, sems_in, sems_out):
        x, y, c, chips = _place()
        me = 2 * x + y
        rows = w_ref.shape[0]
        half = rows // 2
        pieces = _row_chunks(half, ICI_PIECES)
        self.own = _LocalCopy(lambda r: w_ref.at[r], lambda r: out_ref.at[me, r], rows, buf, sems_in, sems_out)

        def rows_of(core, piece):
            start, size = piece
            return pl.ds(core * half + start, size)

        self.sends, self.arrivals, self.forwards, self.forward_arrivals = [], [], [], []
        for i, piece in enumerate(pieces):
            for j, (cx, cy) in enumerate(chips):
                k = j * len(pieces) + i
                there = 2 * cx + cy

                def direct(src_chip, cx=cx, cy=cy, k=k, piece=piece):
                    return pltpu.make_async_remote_copy(
                        src_ref=w_ref.at[rows_of(c, piece)], dst_ref=out_ref.at[src_chip, rows_of(c, piece)],
                        send_sem=send1.at[k], recv_sem=recv1.at[k], device_id=(cx, cy, c), device_id_type=MESH)

                def passed(core, there=there, k=k, piece=piece):
                    return pltpu.make_async_remote_copy(
                        src_ref=out_ref.at[there, rows_of(core, piece)], dst_ref=out_ref.at[there, rows_of(core, piece)],
                        send_sem=send2.at[k], recv_sem=recv2.at[k], device_id=(x, y, 1 - c), device_id_type=MESH)

                self.sends.append(direct(me))
                self.arrivals.append(direct(there))
                self.forwards.append(passed(c))
                self.forward_arrivals.append(passed(1 - c))

    def start(self):
        for cp in self.sends:
            cp.start()
        self.own.start()

    def pass_on(self):
        self.own.pass_on()
        for arrival, forward in zip(self.arrivals, self.forwards):
            arrival.wait_recv()
            forward.start()

    def finish(self):
        for arrival in self.forward_arrivals:
            arrival.wait_recv()
        for cp in self.sends + self.forwards:
            cp.wait_send()
        self.own.finish()

    @staticmethod
    def scratch(rows, dtype):
        n = N_PEER_CHIPS * len(_row_chunks(rows // 2, ICI_PIECES))
        return [pltpu.SemaphoreType.DMA((n,))] * 4 + _LocalCopy.scratch(rows, dtype)

    @staticmethod
    def out_shape(rows, dtype):
        return jax.ShapeDtypeStruct((N_CHIPS, rows, D_MODEL), dtype)


def _gather_weights(pack):
    rows = pack.shape[0]

    def body(w_ref, out_ref, *scratch):
        gather = _Gather(w_ref, out_ref, *scratch)
        gather.start()
        gather.pass_on()
        gather.finish()

    return pl.pallas_call(
        body, name="gather_weights", in_specs=[ANY], out_specs=ANY, out_shape=_Gather.out_shape(rows, pack.dtype),
        scratch_shapes=_Gather.scratch(rows, pack.dtype),
        compiler_params=pltpu.CompilerParams(vmem_limit_bytes=VMEM_LIMIT_V7X),
    )(pack)


class _Scatter:
    def __init__(self, h_ref, out_ref, send, recv):
        x, y, c, chips = _place()
        pieces = _row_chunks(h_ref.shape[1], ICI_PIECES)
        self.copies = []
        for i, (start, size) in enumerate(pieces):
            for j, (cx, cy) in enumerate(chips):
                k = j * len(pieces) + i
                self.copies.append(pltpu.make_async_remote_copy(
                    src_ref=h_ref.at[2 * cx + cy, pl.ds(start, size)], dst_ref=out_ref.at[j, pl.ds(start, size)],
                    send_sem=send.at[k], recv_sem=recv.at[k], device_id=(cx, cy, c), device_id_type=MESH))

    def start(self):
        for cp in self.copies:
            cp.start()

    def finish(self):
        for cp in self.copies:
            cp.wait_recv()
        for cp in self.copies:
            cp.wait_send()

    @staticmethod
    def scratch(half):
        n = N_PEER_CHIPS * len(_row_chunks(half, ICI_PIECES))
        return [pltpu.SemaphoreType.DMA((n,))] * 2

    @staticmethod
    def out_shape(half, dtype):
        return jax.ShapeDtypeStruct((N_PEER_CHIPS, half, D_MODEL), dtype)


def _scatter_to_chips(h):
    half = h.shape[1]

    def body(h_ref, out_ref, send, recv):
        scatter = _Scatter(h_ref, out_ref, send, recv)
        scatter.start()
        scatter.finish()

    return pl.pallas_call(
        body, name="scatter_to_chips", in_specs=[ANY], out_specs=ANY, out_shape=_Scatter.out_shape(half, h.dtype),
        scratch_shapes=_Scatter.scratch(half),
    )(h)


class _Swap:
    def __init__(self, g_ref, theirs_ref, send, recv):
        x, y, c, _ = _place()
        half = g_ref.shape[1] // 2
        pieces = _row_chunks(half, D2D_PIECES)
        self.copies = []
        for s in range(N_CHIPS):
            for i, (start, size) in enumerate(pieces):
                k = s * len(pieces) + i
                self.copies.append(pltpu.make_async_remote_copy(
                    src_ref=g_ref.at[s, pl.ds((1 - c) * half + start, size)], dst_ref=theirs_ref.at[s, pl.ds(start, size)],
                    send_sem=send.at[k], recv_sem=recv.at[k], device_id=(x, y, 1 - c), device_id_type=MESH))

    def start(self):
        for cp in self.copies:
            cp.start()

    def finish(self):
        for cp in self.copies:
            cp.wait()

    @staticmethod
    def scratch(g):
        n = N_CHIPS * len(_row_chunks(g.shape[1] // 2, D2D_PIECES))
        return [pltpu.SemaphoreType.DMA((n,))] * 2

    @staticmethod
    def out_shape(g):
        return jax.ShapeDtypeStruct((N_CHIPS, g.shape[1] // 2, D_MODEL), g.dtype)


def _swap_halves(g):
    def body(g_ref, theirs_ref, send, recv):
        swap = _Swap(g_ref, theirs_ref, send, recv)
        swap.start()
        swap.finish()

    return pl.pallas_call(
        body, name="swap_halves", in_specs=[ANY], out_specs=ANY, out_shape=_Swap.out_shape(g),
        scratch_shapes=_Swap.scratch(g),
    )(g)


ADD_TILE_MAX_ROWS = 600


def _add_tile(half):
    return max(t for t in range(8, ADD_TILE_MAX_ROWS + 1, 8) if half % t == 0)


def _add_cores(g, theirs, name, out_dtype=f32):
    half = theirs.shape[1]
    tr = _add_tile(half)
    n_t = half // tr

    def body(c_ref, g_ref, t_ref, o_ref):
        o_ref[...] = (g_ref[...] + t_ref[...]).astype(out_dtype)

    blk = pl.BlockSpec((1, tr, D_MODEL), lambda s, t, c_ref: (s, t, 0))
    return pl.pallas_call(
        body, name=name,
        grid_spec=pltpu.PrefetchScalarGridSpec(
            num_scalar_prefetch=1, grid=(N_CHIPS, n_t),
            in_specs=[pl.BlockSpec((1, tr, D_MODEL), lambda s, t, c_ref: (s, c_ref[0] * n_t + t, 0)), blk],
            out_specs=blk),
        out_shape=jax.ShapeDtypeStruct(theirs.shape, out_dtype),
        compiler_params=_params("parallel", "parallel"),
    )(lax.axis_index("c").astype(jnp.int32).reshape(1), g, theirs)


def _add_chips(chip_sum, others, name):
    half = chip_sum.shape[1]
    tr = _add_tile(half)

    def body(me_ref, own_ref, o0, o1, o2, out_ref):
        out_ref[...] = ((own_ref[0].astype(f32) + o0[0].astype(f32)) + o1[0].astype(f32)) + o2[0].astype(f32)

    other = lambda j: pl.BlockSpec((1, tr, D_MODEL), lambda t, me_ref: (j, t, 0))
    return pl.pallas_call(
        body, name=name,
        grid_spec=pltpu.PrefetchScalarGridSpec(
            num_scalar_prefetch=1, grid=(half // tr,),
            in_specs=[pl.BlockSpec((1, tr, D_MODEL), lambda t, me_ref: (me_ref[0], t, 0)), other(0), other(1), other(2)],
            out_specs=pl.BlockSpec((tr, D_MODEL), lambda t, me_ref: (t, 0))),
        out_shape=jax.ShapeDtypeStruct((half, D_MODEL), f32),
        compiler_params=_params("parallel"),
    )((2 * lax.axis_index("x") + lax.axis_index("y")).astype(jnp.int32).reshape(1), chip_sum, others, others, others)


def _join_halves(parts):
    n_parts = len(parts)
    pieces = [_row_chunks(r.shape[0], D2D_PIECES) for r in parts]
    first = [sum(len(p) for p in pieces[:i]) for i in range(n_parts)]
    n = sum(len(p) for p in pieces)

    def body(*refs):
        r_refs, out_refs = refs[:n_parts], refs[n_parts:2 * n_parts]
        send, recv = refs[2 * n_parts:2 * n_parts + 2]
        local = refs[2 * n_parts + 2:]
        x, y, c, _ = _place()
        owns = [_LocalCopy(lambda rr, r_ref=r_ref: r_ref.at[rr], lambda rr, out_ref=out_ref: out_ref.at[c, rr],
                           r_ref.shape[0], *local[3 * i:3 * i + 3])
                for i, (r_ref, out_ref) in enumerate(zip(r_refs, out_refs))]
        for own in owns:
            own.start()

        def piece(i, j, core):
            start, size = pieces[i][j]
            return pltpu.make_async_remote_copy(
                src_ref=r_refs[i].at[pl.ds(start, size)], dst_ref=out_refs[i].at[core, pl.ds(start, size)],
                send_sem=send.at[first[i] + j], recv_sem=recv.at[first[i] + j],
                device_id=(x, y, 1 - c), device_id_type=MESH)

        every = [(i, j) for i in range(n_parts) for j in range(len(pieces[i]))]
        copies = [piece(i, j, c) for i, j in every]
        for cp in copies:
            cp.start()
        for own in owns:
            own.pass_on()
        for i, j in every:
            piece(i, j, 1 - c).wait_recv()
        for cp in copies:
            cp.wait_send()
        for own in owns:
            own.finish()

    local_scratch = []
    for r in parts:
        local_scratch += _LocalCopy.scratch(r.shape[0], r.dtype)
    return pl.pallas_call(
        body, name="join_halves", in_specs=[ANY] * n_parts, out_specs=[ANY] * n_parts,
        out_shape=[jax.ShapeDtypeStruct((2,) + r.shape, r.dtype) for r in parts],
        scratch_shapes=[pltpu.SemaphoreType.DMA((n,))] * 2 + local_scratch,
        compiler_params=pltpu.CompilerParams(vmem_limit_bytes=VMEM_LIMIT_V7X),
    )(*parts)


class _SmallSum:
    def __init__(self, b_ref, gathered, send, recv, local_sem):
        x, y, c, _ = _place()
        me = 4 * x + 2 * y + c
        self.gathered = gathered
        self.own = pltpu.make_async_copy(b_ref, gathered.at[me], local_sem)
        self.sends, self.arrivals = [], []
        for kk in range(1, N_DEV):
            flip = lambda v, bit: 1 - v if bit else v
            peer = (flip(x, kk & 4), flip(y, kk & 2), flip(c, kk & 1))
            self.sends.append(pltpu.make_async_remote_copy(
                src_ref=b_ref, dst_ref=gathered.at[me], send_sem=send.at[kk - 1], recv_sem=recv.at[kk - 1],
                device_id=peer, device_id_type=MESH))
            self.arrivals.append(pltpu.make_async_remote_copy(
                src_ref=b_ref, dst_ref=gathered.at[jnp.bitwise_xor(me, kk)], send_sem=send.at[kk - 1],
                recv_sem=recv.at[kk - 1], device_id=peer, device_id_type=MESH))

    def start(self):
        self.own.start()
        for cp in self.sends:
            cp.start()

    def finish(self, out_ref):
        self.own.wait()
        for cp in self.arrivals:
            cp.wait_recv()
        for cp in self.sends:
            cp.wait_send()
        acc = self.gathered[0]
        for dev in range(1, N_DEV):
            acc = acc + self.gathered[dev]
        out_ref[...] = acc

    @staticmethod
    def scratch(block):
        return [pltpu.VMEM((N_DEV,) + block.shape, block.dtype), pltpu.SemaphoreType.DMA((N_DEV - 1,)),
                pltpu.SemaphoreType.DMA((N_DEV - 1,)), pltpu.SemaphoreType.DMA]


def _adamw(w, g, m, v, name):
    rows, cols = w.shape
    tr = max(t for t in range(8, 513, 8) if rows % t == 0)
    c1 = 1.0 - ADAM_B1 ** ADAM_STEP
    c2 = 1.0 - ADAM_B2 ** ADAM_STEP

    def body(w_ref, g_ref, m_ref, v_ref, d_ref, nm_ref, nv_ref):
        gv = g_ref[...]
        nm = ADAM_B1 * m_ref[...] + (1.0 - ADAM_B1) * gv
        nv = ADAM_B2 * v_ref[...] + (1.0 - ADAM_B2) * (gv * gv)
        nm_ref[...] = nm
        nv_ref[...] = nv
        d_ref[...] = -ADAM_LR * ((nm / c1) / (jnp.sqrt(nv / c2) + ADAM_EPS) + ADAM_WD * w_ref[...])

    blk = pl.BlockSpec((tr, cols), lambda i: (i, 0))
    shape = jax.ShapeDtypeStruct((rows, cols), f32)
    return pl.pallas_call(
        body, name=name, grid=(rows // tr,), in_specs=[blk] * 4, out_specs=[blk] * 3, out_shape=[shape] * 3,
        compiler_params=_params("parallel"),
    )(w, g, m, v)


LARGE = ("w_in", "w_out", "w_gate", "w_up", "w_down")
SMALL = ("ln_pre_mix", "ln_post_mix", "ln_pre_ffn", "ln_post_ffn", "pool_scale", "w_pool")
SHARD_ROWS = {"w_in": 640, "w_out": 256, "w_gate": 704, "w_up": 704, "w_down": 704}
COLUMN_SHARDED = ("w_in", "w_gate", "w_up")
UPDATED_TRANSPOSED = ("w_gate", "w_up")
NEEDED_FIRST = ("w_in",)
NEEDED_LATER = ("w_out", "w_gate", "w_up", "w_down")
READY_EARLY = ("w_out", "w_gate", "w_up", "w_down")
READY_LATE = ("w_in",)


def _pack_shard(shards, names):
    return jnp.concatenate([shards[n].T if n in COLUMN_SHARDED else shards[n] for n in names], axis=0)


def _unpack_shard(pack, names):
    out, row = {}, 0
    for n in names:
        out[n] = pack[row:row + SHARD_ROWS[n]]
        row += SHARD_ROWS[n]
    return out


def _whole_from_shards(packs, names):
    out, row = {}, 0
    for n in names:
        rows = SHARD_ROWS[n]
        out[n] = packs[:, row:row + rows].reshape(N_CHIPS * rows, D_MODEL)
        row += rows
    return out


def _shards_from_whole(grads, names):
    return jnp.concatenate([grads[n].reshape(N_CHIPS, SHARD_ROWS[n], D_MODEL) for n in names], axis=1)


def _pack_small(vals):
    rows = [vals[n].reshape(1, D_MODEL) for n in SMALL[:4]]
    rows.append(jnp.pad(vals["pool_scale"].reshape(1, POOL_WIDTH), ((0, 0), (0, D_MODEL - POOL_WIDTH))))
    rows.append(jnp.pad(vals["loss"].reshape(1, 1), ((0, 0), (0, D_MODEL - 1))))
    rows.append(jnp.zeros((2, D_MODEL), f32))
    rows.append(vals["w_pool"].reshape(16, D_MODEL))
    return jnp.concatenate(rows, axis=0)


def _unpack_small(block):
    out = {n: block[i:i + 1] for i, n in enumerate(SMALL[:4])}
    out["pool_scale"] = block[4:5, :POOL_WIDTH]
    out["loss"] = block[5, 0]
    out["w_pool"] = block[8:24].reshape(1, 4, POOL_GROUP, POOL_GROUP)
    return out


def kernel(x, ln_pre_mix, w_in, w_pool, pool_scale, w_out, ln_post_mix, ln_pre_ffn, w_gate, w_up, w_down, ln_post_ffn, loss_target, m_ln_pre_mix, m_w_in, m_w_pool, m_pool_scale, m_w_out, m_ln_post_mix, m_ln_pre_ffn, m_w_gate, m_w_up, m_w_down, m_ln_post_ffn, v_ln_pre_mix, v_w_in, v_w_pool, v_pool_scale, v_w_out, v_ln_post_mix, v_ln_pre_ffn, v_w_gate, v_w_up, v_w_down, v_ln_post_ffn):
    w = dict(ln_pre_mix=ln_pre_mix, w_in=w_in, w_pool=w_pool, pool_scale=pool_scale, w_out=w_out,
             ln_post_mix=ln_post_mix, ln_pre_ffn=ln_pre_ffn, w_gate=w_gate, w_up=w_up, w_down=w_down,
             ln_post_ffn=ln_post_ffn)
    m = dict(ln_pre_mix=m_ln_pre_mix, w_in=m_w_in, w_pool=m_w_pool, pool_scale=m_pool_scale, w_out=m_w_out,
             ln_post_mix=m_ln_post_mix, ln_pre_ffn=m_ln_pre_ffn, w_gate=m_w_gate, w_up=m_w_up, w_down=m_w_down,
             ln_post_ffn=m_ln_post_ffn)
    v = dict(ln_pre_mix=v_ln_pre_mix, w_in=v_w_in, w_pool=v_w_pool, pool_scale=v_pool_scale, w_out=v_w_out,
             ln_post_mix=v_ln_post_mix, ln_pre_ffn=v_ln_pre_ffn, w_gate=v_w_gate, w_up=v_w_up, w_down=v_w_down,
             ln_post_ffn=v_ln_post_ffn)

    xs, target = x[0], loss_target[0]
    cos_t, sin_t = _rope_tables(xs.shape[0])
    w_bd = _block_diag(w_pool[0]).astype(bf16)
    shard = {n: w[n][0].astype(bf16) for n in LARGE}

    w_in_whole = _whole_from_shards(_gather_weights(_pack_shard(shard, NEEDED_FIRST)), NEEDED_FIRST)["w_in"]
    h1, u, qs, ks, vs = _in_proj(xs, ln_pre_mix, w_in_whole, cos_t, sin_t)
    pool_out = _pool_fwd(u, w_bd, pool_scale)
    attn_out, lse, later = _attn_fwd(qs, ks, vs, _pack_shard(shard, NEEDED_LATER))
    whole = _whole_from_shards(later, NEEDED_LATER)
    mix, x2, h2 = _out_proj(pool_out, attn_out, whole["w_out"], xs, ln_post_mix, ln_pre_ffn)
    gate, up, f = _ffn_fwd(h2, whole["w_gate"], whole["w_up"], whole["w_down"])
    dy, df, dg4, loss = _loss_head(f, x2, target, ln_post_ffn)

    large = {}
    a, dgate, dup, dh2 = _ffn_bwd(df, gate, up, whole["w_gate"], whole["w_up"], whole["w_down"])
    large["w_down"] = _matmul_tiles_tn(a, df, "grad_w_down")
    large["w_gate"] = _matmul_tiles_tn(dgate, h2, "grad_w_gate")
    large["w_up"] = _matmul_tiles_tn(dup, h2, "grad_w_up")
    dx2, dmix, dg3, dg2 = _norm_bwd(dh2, dy, x2, mix, ln_pre_ffn, ln_post_mix)
    large["w_out"] = jnp.concatenate([_matmul_tn(pool_out, dmix, "grad_w_out_pool"),
                                      _matmul_tn(attn_out, dmix, "grad_w_out_attn")], axis=0)
    early = _shards_from_whole(large, READY_EARLY)
    dpool, delta, dos, early_theirs = _out_proj_bwd(dmix, whole["w_out"], attn_out, _head_ones(), early)
    early_chip = _add_cores(early, early_theirs, "add_cores_early")
    du, d_w_bd, d_scale = _pool_bwd(u, dpool, w_bd, pool_scale)
    dq, dk, dv, early_others = _attn_bwd(qs, ks, vs, dos, lse, delta, early_chip)
    grad_x, dproj, dg1 = _in_proj_bwd(du, dq, dk, dv, cos_t, sin_t, w_in_whole, xs, dx2, ln_pre_mix)
    d_w_pool = jnp.stack([d_w_bd[g * POOL_GROUP:(g + 1) * POOL_GROUP, g * POOL_GROUP:(g + 1) * POOL_GROUP]
                          for g in range(POOL_WIDTH // POOL_GROUP)])
    small = dict(ln_pre_mix=dg1, ln_post_mix=dg2, ln_pre_ffn=dg3, ln_post_ffn=dg4, pool_scale=d_scale, w_pool=d_w_pool)
    large["w_in"], small_total = _matmul_tn_and_small_sum(dproj, h1, _pack_small(dict(small, loss=loss)), "grad_w_in")
    late = _shards_from_whole(large, READY_LATE)
    late_chip = _add_cores(late, _swap_halves(late), "add_cores_late", bf16)
    late_others = _scatter_to_chips(late_chip)
    early_half = _add_chips(early_chip, early_others, "add_chips_early")
    late_half = _add_chips(late_chip, late_others, "add_chips_late")
    early_whole, late_whole = _join_halves([early_half, late_half])
    grads = _unpack_shard(early_whole.reshape(-1, D_MODEL), READY_EARLY)
    grads.update(_unpack_shard(late_whole.reshape(-1, D_MODEL), READY_LATE))

    total = _unpack_small(small_total)
    for n in SMALL:
        grads[n] = total[n]

    delta_w, new_m, new_v = {}, {}, {}
    for n in LARGE:
        if n in UPDATED_TRANSPOSED:
            update = _adamw(w[n][0].T, grads[n], m[n][0].T, v[n][0].T, "adamw_" + n)
            delta_w[n], new_m[n], new_v[n], grads[n] = [a.T for a in (*update, grads[n])]
        else:
            if n in COLUMN_SHARDED:
                grads[n] = grads[n].T
            delta_w[n], new_m[n], new_v[n] = _adamw(w[n][0], grads[n], m[n][0], v[n][0], "adamw_" + n)
    small_state = [_pack_small(dict({n: s[n] for n in SMALL}, loss=jnp.zeros((), f32))) for s in (w, m, v)]
    small_grad = _pack_small(dict({n: grads[n] for n in SMALL}, loss=jnp.zeros((), f32)))
    sd, sm, sv = _adamw(small_state[0], small_grad, small_state[1], small_state[2], "adamw_small")
    for out, block in ((delta_w, sd), (new_m, sm), (new_v, sv)):
        un = _unpack_small(block)
        for n in SMALL:
            out[n] = un[n]

    names = ("ln_pre_mix", "w_in", "w_pool", "pool_scale", "w_out", "ln_post_mix", "ln_pre_ffn", "w_gate", "w_up",
             "w_down", "ln_post_ffn")
    full = lambda d: [d[n].reshape(w[n].shape) for n in names]
    return (total["loss"], grad_x[None], *full(grads), *full(delta_w), *full(new_m), *full(new_v))
```

```python
import numpy as np
import jax
import jax.numpy as jnp
from jax import lax
from jax.experimental import pallas as pl
from jax.experimental.pallas import tpu as pltpu

D_MODEL = 1024
POOL_WIDTH = 256
POOL_GROUP = 64
ATTN_WIDTH = 768
HEAD_DIM = 64
IN_WIDTH = 2560
D_FF = 2816
BLOCK = 128
DILATIONS = (1, 4, 16)
ROPE_THETA = 10000.0
EPS = 1e-6
ATTN_SCALE = 0.125
NEG = -1e30

ADAM_LR = 0.001
ADAM_B1 = 0.9
ADAM_B2 = 0.999
ADAM_EPS = 1e-08
ADAM_WD = 0.01
ADAM_STEP = 10

N_CHIPS = 4
N_DEV = 8
VMEM_LIMIT_V7X = 56 * 1024 * 1024
MESH = pl.DeviceIdType.MESH

f32 = jnp.float32
bf16 = jnp.bfloat16


def _params(*sem):
    return pltpu.CompilerParams(dimension_semantics=sem, vmem_limit_bytes=VMEM_LIMIT_V7X)


def _dot(a, b):
    return jnp.dot(a, b, preferred_element_type=f32)


def _dot_nt(a, b):
    return lax.dot_general(a, b, (((1,), (1,)), ((), ())), preferred_element_type=f32)


def _dot_tn(a, b):
    return lax.dot_general(a, b, (((0,), (0,)), ((), ())), preferred_element_type=f32)


def _rope_partner(a, first_half):
    return jnp.where(first_half, pltpu.roll(a, 96, 1), pltpu.roll(a, 32, 1))


def _first_half_mask(rows):
    lane = lax.broadcasted_iota(jnp.int32, (rows, 128), 1)
    return (lane % HEAD_DIM) < (HEAD_DIM // 2)


def _stream_spec(d, ts):
    return pl.BlockSpec((d, ts // d, ATTN_WIDTH), lambda i: (0, i, 0))


def _stream_shape(S, d):
    return jax.ShapeDtypeStruct((d, S // d, ATTN_WIDTH), bf16)


N_STAGE = ATTN_WIDTH // 128


def _stage_scratch(ts):
    return [pltpu.VMEM((ts, 128), f32)] * N_STAGE


def _store_streams(stage, out_refs, ts):
    for d, ref in zip(DILATIONS, out_refs):
        for r in range(d):
            rows = pl.ds(0, ts) if d == 1 else pl.ds(r, ts // d, stride=d)
            for j in range(N_STAGE):
                ref[r, :, j * 128:(j + 1) * 128] = stage[j][rows, :].astype(bf16)


def _in_proj(x, g1, w_in, cos_t, sin_t):
    S = x.shape[0]
    ts = 512

    def body(x_ref, g_ref, w_ref, cos_ref, sin_ref, h_ref, u_ref, *rest):
        outs, stage = rest[:-N_STAGE], rest[-N_STAGE:]
        xv = x_ref[...]
        r = lax.rsqrt(jnp.mean(xv * xv, axis=-1, keepdims=True) + EPS)
        h = ((xv * r) * g_ref[...]).astype(bf16)
        h_ref[...] = h
        proj = _dot_nt(h, w_ref[...])
        u_ref[...] = proj[:, :POOL_WIDTH]
        cos = cos_ref[...]
        sin = sin_ref[...]
        first = _first_half_mask(ts)
        n_dil = len(DILATIONS)
        for which, base in enumerate((POOL_WIDTH, POOL_WIDTH + ATTN_WIDTH)):
            for j in range(ATTN_WIDTH // 128):
                a = proj[:, base + j * 128: base + (j + 1) * 128]
                if which == 0:
                    a = a * ATTN_SCALE
                stage[j][...] = a * cos + _rope_partner(a, first) * sin
            _store_streams(stage, outs[which * n_dil:(which + 1) * n_dil], ts)
        for j in range(ATTN_WIDTH // 128):
            base = POOL_WIDTH + 2 * ATTN_WIDTH + j * 128
            stage[j][...] = proj[:, base:base + 128]
        _store_streams(stage, outs[2 * n_dil:], ts)

    row = lambda w: pl.BlockSpec((ts, w), lambda i: (i, 0))
    streams = [_stream_spec(d, ts) for d in DILATIONS] * 3
    res = pl.pallas_call(
        body, name="in_proj", grid=(S // ts,),
        in_specs=[row(D_MODEL), pl.BlockSpec((1, D_MODEL), lambda i: (0, 0)),
                  pl.BlockSpec((IN_WIDTH, D_MODEL), lambda i: (0, 0)), row(128), row(128)],
        out_specs=[row(D_MODEL), row(POOL_WIDTH)] + streams,
        out_shape=[jax.ShapeDtypeStruct((S, D_MODEL), bf16), jax.ShapeDtypeStruct((S, POOL_WIDTH), f32)]
        + [_stream_shape(S, d) for d in DILATIONS] * 3,
        scratch_shapes=_stage_scratch(ts),
        compiler_params=_params("parallel"),
    )(x, g1, w_in, cos_t, sin_t)
    n = len(DILATIONS)
    return res[0], res[1], res[2:2 + n], res[2 + n:2 + 2 * n], res[2 + 2 * n:]


POOL_HALO = 16


def _pool_lane_group(rows):
    return lax.broadcasted_iota(jnp.int32, (rows, POOL_WIDTH), 1) // POOL_GROUP


def _pool_select(group, s2, s4, s8, s16):
    return jnp.where(group == 0, s2, jnp.where(group == 1, s4, jnp.where(group == 2, s8, s16)))


def _pool_count(t0, rows):
    group = _pool_lane_group(rows)
    t = t0 + lax.broadcasted_iota(jnp.int32, (rows, POOL_WIDTH), 0)
    win = _pool_select(group, 2, 4, 8, 16)
    return jnp.minimum(t + 1, win).astype(f32)


def _pool_diff(u_halo, u_tile, t0):
    ts = u_tile.shape[0]
    ext = jnp.concatenate([u_halo, u_tile], axis=0)
    s2 = ext + pltpu.roll(ext, 1, 0)
    s4 = s2 + pltpu.roll(s2, 2, 0)
    s8 = s4 + pltpu.roll(s4, 4, 0)
    s16 = s8 + pltpu.roll(s8, 8, 0)
    group = _pool_lane_group(ts + POOL_HALO)
    wsum = _pool_select(group, s2, s4, s8, s16)[POOL_HALO:]
    return wsum / _pool_count(t0, ts) - u_tile


def _pool_specs(ts, n_tiles):
    tile = pl.BlockSpec((ts, POOL_WIDTH), lambda i: (i, 0))
    per = ts // POOL_HALO
    before = pl.BlockSpec((POOL_HALO, POOL_WIDTH), lambda i: (jnp.maximum(i * per - 1, 0), 0))
    after = pl.BlockSpec((POOL_HALO, POOL_WIDTH), lambda i: (jnp.minimum((i + 1) * per, n_tiles * per - 1), 0))
    return tile, before, after


def _pool_fwd(u, w_bd, scale):
    S = u.shape[0]
    ts = 2048
    n_tiles = S // ts

    def body(u_ref, halo_ref, w_ref, sc_ref, y_ref):
        i = pl.program_id(0)
        halo = jnp.where(i > 0, halo_ref[...], 0.0)
        d = _pool_diff(halo, u_ref[...], i * ts)
        y_ref[...] = (_dot(d.astype(bf16), w_ref[...]) * sc_ref[...]).astype(bf16)

    tile, before, _ = _pool_specs(ts, n_tiles)
    return pl.pallas_call(
        body, name="pool_fwd", grid=(n_tiles,),
        in_specs=[tile, before, pl.BlockSpec((POOL_WIDTH, POOL_WIDTH), lambda i: (0, 0)),
                  pl.BlockSpec((1, POOL_WIDTH), lambda i: (0, 0))],
        out_specs=tile, out_shape=jax.ShapeDtypeStruct((S, POOL_WIDTH), bf16),
        compiler_params=_params("parallel"),
    )(u, u, w_bd, scale)


def _pool_bwd(u, dy, w_bd, scale):
    S = u.shape[0]
    ts = 2048
    n_tiles = S // ts

    def body(u_ref, halo_ref, dy_ref, dy_next_ref, w_ref, sc_ref, du_ref, dw_ref, dsc_ref):
        i = pl.program_id(0)

        @pl.when(i == 0)
        def _():
            dw_ref[...] = jnp.zeros_like(dw_ref)
            dsc_ref[...] = jnp.zeros_like(dsc_ref)

        halo = jnp.where(i > 0, halo_ref[...], 0.0)
        d = _pool_diff(halo, u_ref[...], i * ts).astype(bf16)
        w = w_ref[...]
        sc = sc_ref[...]
        dy_tile = dy_ref[...]
        z = _dot(d, w)
        dsc_ref[...] += jnp.sum(dy_tile * z, axis=0, keepdims=True)
        dy_next = jnp.where(i < n_tiles - 1, dy_next_ref[...], 0.0)
        dz = (jnp.concatenate([dy_tile, dy_next], axis=0) * sc).astype(bf16)
        dw_ref[...] += _dot_tn(d, dz[:ts])
        dd = _dot_nt(dz, w)
        e = dd / _pool_count(i * ts, ts + POOL_HALO)
        n = ts + POOL_HALO
        f2 = e + pltpu.roll(e, n - 1, 0)
        f4 = f2 + pltpu.roll(f2, n - 2, 0)
        f8 = f4 + pltpu.roll(f4, n - 4, 0)
        f16 = f8 + pltpu.roll(f8, n - 8, 0)
        fsum = _pool_select(_pool_lane_group(n), f2, f4, f8, f16)
        du_ref[...] = (fsum[:ts] - dd[:ts]).astype(bf16)

    tile, before, after = _pool_specs(ts, n_tiles)
    return pl.pallas_call(
        body, name="pool_bwd", grid=(n_tiles,),
        in_specs=[tile, before, tile, after, pl.BlockSpec((POOL_WIDTH, POOL_WIDTH), lambda i: (0, 0)),
                  pl.BlockSpec((1, POOL_WIDTH), lambda i: (0, 0))],
        out_specs=[tile, pl.BlockSpec((POOL_WIDTH, POOL_WIDTH), lambda i: (0, 0)),
                   pl.BlockSpec((1, POOL_WIDTH), lambda i: (0, 0))],
        out_shape=[jax.ShapeDtypeStruct((S, POOL_WIDTH), bf16), jax.ShapeDtypeStruct((POOL_WIDTH, POOL_WIDTH), f32),
                   jax.ShapeDtypeStruct((1, POOL_WIDTH), f32)],
        compiler_params=_params("arbitrary"),
    )(u, u, dy, dy, w_bd, scale)


SUPER = BLOCK * DILATIONS[-1]
UNITS = SUPER // BLOCK
FWD_UNROLL = 16
BWD_UNROLL = 16


def _band_mask(has_prev):
    qi = lax.broadcasted_iota(jnp.int32, (BLOCK, 2 * BLOCK), 0)
    kj = lax.broadcasted_iota(jnp.int32, (BLOCK, 2 * BLOCK), 1)
    return (kj >= qi) & (kj <= qi + BLOCK) & ((kj >= BLOCK) | has_prev)


def _head0_mask(rows=BLOCK):
    return lax.broadcasted_iota(jnp.int32, (rows, 128), 1) < HEAD_DIM


def _band_mask_t(has_prev):
    ki = lax.broadcasted_iota(jnp.int32, (2 * BLOCK, 2 * BLOCK), 0)
    qj = lax.broadcasted_iota(jnp.int32, (2 * BLOCK, 2 * BLOCK), 1) % BLOCK
    return (ki >= qj) & (ki <= qj + BLOCK) & ((ki >= BLOCK) | has_prev)


def _head_pair_rows(a, h0):
    zero = jnp.zeros_like(a)
    return jnp.concatenate([jnp.where(h0, a, zero), jnp.where(h0, zero, a)], axis=0)


def _per_query_row(stat):
    t = stat.T
    return jnp.concatenate([t[0:1], t[HEAD_DIM:HEAD_DIM + 1]], axis=1)


def _natural_rows(d, r, n):
    if d == 1:
        return pl.ds(pl.multiple_of(n * BLOCK, BLOCK), BLOCK)
    return pl.ds(n * (BLOCK * d) + r, BLOCK, stride=d)


def _unit_place(d, u):
    per_stream = UNITS // d
    return u // per_stream, u % per_stream, per_stream


def _block_rows(n):
    return pl.ds(pl.multiple_of(n * BLOCK, BLOCK), BLOCK)


def _band(cur_ref, tail_ref, r, n):
    before = jnp.where(n > 0, cur_ref[r, _block_rows(jnp.maximum(n - 1, 0)), :], tail_ref[r])
    return jnp.concatenate([before, cur_ref[r, _block_rows(n), :]], axis=0)


def _attn_in_specs(S, with_do):
    specs = []
    last = S // SUPER - 1
    for d in DILATIONS:
        per_stream = UNITS // d
        cur = pl.BlockSpec((d, SUPER // d, 128), lambda hp, sb: (0, jnp.minimum(sb, last), hp))
        tail = pl.BlockSpec(
            (d, BLOCK, 128),
            lambda hp, sb, per_stream=per_stream: (0, jnp.maximum(jnp.minimum(sb, last) * per_stream - 1, 0), hp))
        specs += [cur] * (2 if with_do else 1) + [cur, tail, cur, tail]
    return specs


def _attn_fwd(qs, ks, vs, pack):
    S = qs[0].shape[1]
    n_dil = len(DILATIONS)
    n_steps = S // SUPER
    n_total = (ATTN_WIDTH // 128) * n_steps

    def body(*refs):
        ins, pack_ref = refs[:5 * n_dil], refs[5 * n_dil]
        out_ref, lse_ref, gathered_ref = refs[5 * n_dil + 1:5 * n_dil + 4]
        scratch = refs[5 * n_dil + 4:]
        o_sc, l_sc = scratch[:n_dil], scratch[n_dil:2 * n_dil]
        gather = _Gather(pack_ref, gathered_ref, *scratch[2 * n_dil:])
        sb = pl.program_id(1)
        step = pl.program_id(0) * n_steps + sb

        @pl.when(step == 0)
        def _():
            gather.start()

        h0 = _head0_mask()
        for ci, d in enumerate(DILATIONS):
            q_ref, kc_ref, kp_ref, vc_ref, vp_ref = ins[5 * ci:5 * ci + 5]

            def unit(u, carry, d=d, ci=ci, q_ref=q_ref, kc_ref=kc_ref, kp_ref=kp_ref, vc_ref=vc_ref, vp_ref=vp_ref):
                r, n, _ = _unit_place(d, u)
                qv = q_ref[r, _block_rows(n), :]
                kb = _band(kc_ref, kp_ref, r, n)
                vb = _band(vc_ref, vp_ref, r, n)
                valid = _band_mask((sb > 0) | (n > 0))
                s = jnp.where(jnp.concatenate([valid, valid], axis=0), _dot_nt(_head_pair_rows(qv, h0), kb), NEG)
                m = jnp.max(s, axis=1, keepdims=True)
                e = jnp.exp(s - m)
                den = jnp.sum(e, axis=1, keepdims=True)
                o_pair = _dot(e.astype(bf16), vb) * (1.0 / den)
                lse_pair = jnp.broadcast_to(m + jnp.log(den), (2 * BLOCK, 128))
                rows = _natural_rows(d, r, n)
                o_sc[ci][rows, :] = jnp.where(h0, o_pair[:BLOCK], o_pair[BLOCK:])
                l_sc[ci][rows, :] = jnp.where(h0, lse_pair[:BLOCK], lse_pair[BLOCK:])
                return carry

            lax.fori_loop(0, UNITS, unit, 0, unroll=FWD_UNROLL)

        def merge(t, carry):
            rows = pl.ds(pl.multiple_of(t * 256, 256), 256)
            a, b, c = l_sc[0][rows, :], l_sc[1][rows, :], l_sc[2][rows, :]
            m = jnp.maximum(jnp.maximum(a, b), c)
            ea, eb, ec = jnp.exp(a - m), jnp.exp(b - m), jnp.exp(c - m)
            tot = ea + eb + ec
            out_ref[rows, :] = ((ea / tot) * o_sc[0][rows, :] + (eb / tot) * o_sc[1][rows, :]
                                + (ec / tot) * o_sc[2][rows, :]).astype(bf16)
            lse_ref[rows, :] = m + jnp.log(tot)
            return carry

        lax.fori_loop(0, SUPER // 256, merge, 0)

        @pl.when(step == (2 * n_total) // 3)
        def _():
            gather.pass_on()

        @pl.when(step == n_total - 1)
        def _():
            gather.finish()

    args = []
    for q, k, v in zip(qs, ks, vs):
        args += [q, k, k, v, v]
    nat = pl.BlockSpec((SUPER, 128), lambda hp, sb: (sb, hp))
    rows = pack.shape[0]
    return pl.pallas_call(
        body, name="attn_fwd", grid=(ATTN_WIDTH // 128, n_steps),
        in_specs=_attn_in_specs(S, False) + [ANY], out_specs=[nat, nat, ANY],
        out_shape=[jax.ShapeDtypeStruct((S, ATTN_WIDTH), bf16), jax.ShapeDtypeStruct((S, ATTN_WIDTH), f32),
                   _Gather.out_shape(rows, pack.dtype)],
        scratch_shapes=[pltpu.VMEM((SUPER, 128), f32)] * (2 * n_dil) + _Gather.scratch(rows, pack.dtype),
        compiler_params=_params("arbitrary", "arbitrary"),
    )(*args, pack)


def _attn_bwd(qs, ks, vs, dos, lse, delta, chip_sum):
    S = qs[0].shape[1]
    n_steps = S // SUPER
    last = n_steps - 1
    n_dil = len(DILATIONS)
    n_total = (ATTN_WIDTH // 128) * (n_steps + 1)

    def body(*refs):
        ins, (lse_ref, dl_ref, sum_ref) = refs[:6 * n_dil], refs[6 * n_dil:6 * n_dil + 3]
        dq_ref, dk_ref, dv_ref, others_ref = refs[6 * n_dil + 3:6 * n_dil + 7]
        dq_acc, dk_acc, dv_acc = refs[6 * n_dil + 7:6 * n_dil + 10]
        scatter = _Scatter(sum_ref, others_ref, *refs[6 * n_dil + 10:])
        sb = pl.program_id(1)
        step = pl.program_id(0) * (n_steps + 1) + sb
        cur = sb % 2
        prv = 1 - cur

        @pl.when(step == 0)
        def _():
            scatter.start()

        @pl.when(sb < n_steps)
        def _():
            dq_acc[...] = jnp.zeros_like(dq_acc)
            dk_acc[cur] = jnp.zeros((SUPER, 128), f32)
            dv_acc[cur] = jnp.zeros((SUPER, 128), f32)
            h0 = _head0_mask()
            for ci, d in enumerate(DILATIONS):
                q_ref, do_ref, kc_ref, kp_ref, vc_ref, vp_ref = ins[6 * ci:6 * ci + 6]

                def unit(u, carry, d=d, q_ref=q_ref, do_ref=do_ref, kc_ref=kc_ref, kp_ref=kp_ref, vc_ref=vc_ref,
                         vp_ref=vp_ref):
                    r, n, per_stream = _unit_place(d, u)
                    qv = q_ref[r, _block_rows(n), :]
                    dov = do_ref[r, _block_rows(n), :]
                    kb = _band(kc_ref, kp_ref, r, n)
                    vb = _band(vc_ref, vp_ref, r, n)
                    rows = _natural_rows(d, r, n)
                    has_prev = (sb > 0) | (n > 0)
                    q_pair = _head_pair_rows(qv, h0)
                    do_pair = _head_pair_rows(dov, h0)
                    s_t = jnp.where(_band_mask_t(has_prev), _dot_nt(kb, q_pair), NEG)
                    p_t = jnp.exp(s_t - _per_query_row(lse_ref[rows, :]))
                    dp_t = _dot_nt(vb, do_pair)
                    ds_t = (p_t * (dp_t - _per_query_row(dl_ref[rows, :]))).astype(bf16)
                    dvb = _dot(p_t.astype(bf16), do_pair)
                    dkb = _dot(ds_t, q_pair)
                    dq_pair = _dot_tn(ds_t, kb)
                    dq_acc[rows, :] += jnp.where(h0, dq_pair[:BLOCK], dq_pair[BLOCK:])
                    dk_acc[cur, rows, :] += dkb[BLOCK:]
                    dv_acc[cur, rows, :] += dvb[BLOCK:]

                    slot = jnp.where((n > 0) | (sb == 0), cur, prv)
                    before = _natural_rows(d, r, jnp.where(n > 0, n - 1, per_stream - 1))
                    dk_acc[slot, before, :] += dkb[:BLOCK]
                    dv_acc[slot, before, :] += dvb[:BLOCK]
                    return carry

                lax.fori_loop(0, UNITS, unit, 0, unroll=BWD_UNROLL)
            dq_ref[...] = (dq_acc[...] * ATTN_SCALE).astype(bf16)

        @pl.when(sb > 0)
        def _():
            dk_ref[...] = dk_acc[prv].astype(bf16)
            dv_ref[...] = dv_acc[prv].astype(bf16)

        @pl.when(step == n_total - 1)
        def _():
            scatter.finish()

    args = []
    for q, k, v, do in zip(qs, ks, vs, dos):
        args += [q, do, k, k, v, v]
    nat = pl.BlockSpec((SUPER, 128), lambda hp, sb: (jnp.minimum(sb, last), hp))
    nat_before = pl.BlockSpec((SUPER, 128), lambda hp, sb: (jnp.clip(sb - 1, 0, last), hp))
    out = jax.ShapeDtypeStruct((S, ATTN_WIDTH), bf16)
    half = chip_sum.shape[1]
    return pl.pallas_call(
        body, name="attn_bwd", grid=(ATTN_WIDTH // 128, n_steps + 1),
        in_specs=_attn_in_specs(S, True) + [nat, nat, ANY], out_specs=[nat, nat_before, nat_before, ANY],
        out_shape=[out, out, out, _Scatter.out_shape(half, chip_sum.dtype)],
        scratch_shapes=[pltpu.VMEM((SUPER, 128), f32), pltpu.VMEM((2, SUPER, 128), f32),
                        pltpu.VMEM((2, SUPER, 128), f32)] + _Scatter.scratch(half),
        compiler_params=_params("arbitrary", "arbitrary"),
    )(*args, lse, delta, chip_sum)


def _rms(v):
    return lax.rsqrt(jnp.mean(v * v, axis=-1, keepdims=True) + EPS)


def _out_proj(pool_out, attn_out, w_out, x, g2, g3):
    S = x.shape[0]
    ts = 1024

    def body(p_ref, a_ref, w_ref, x_ref, g2_ref, g3_ref, mix_ref, x2_ref, h2_ref):
        mix = _dot(p_ref[...], w_ref[:POOL_WIDTH, :]) + _dot(a_ref[...], w_ref[POOL_WIDTH:, :])
        mix_ref[...] = mix
        x2 = x_ref[...] + (mix * _rms(mix)) * g2_ref[...]
        x2_ref[...] = x2
        h2_ref[...] = ((x2 * _rms(x2)) * g3_ref[...]).astype(bf16)

    row = lambda w: pl.BlockSpec((ts, w), lambda i: (i, 0))
    gain = pl.BlockSpec((1, D_MODEL), lambda i: (0, 0))
    return pl.pallas_call(
        body, name="out_proj", grid=(S // ts,),
        in_specs=[row(POOL_WIDTH), row(ATTN_WIDTH), pl.BlockSpec((D_MODEL, D_MODEL), lambda i: (0, 0)),
                  row(D_MODEL), gain, gain],
        out_specs=[row(D_MODEL)] * 3,
        out_shape=[jax.ShapeDtypeStruct((S, D_MODEL), f32), jax.ShapeDtypeStruct((S, D_MODEL), f32),
                   jax.ShapeDtypeStruct((S, D_MODEL), bf16)],
        compiler_params=_params("parallel"),
    )(pool_out, attn_out, w_out, x, g2, g3)


FF_TILE = 256
FF_STEP_ROWS = 2048
FF_ROWS = 512
FF_BWD_ROWS = 256


def _sigmoid(g):
    return 1.0 / (1.0 + jnp.exp(-g))


def _ff_act_shape(S):
    return jax.ShapeDtypeStruct((D_FF // FF_TILE, S, FF_TILE), bf16)


def _ff_act_spec(ts):
    return pl.BlockSpec((1, ts, FF_TILE), lambda i, j: (j, i, 0))


def _ffn_fwd(h2, w_gate, w_up, w_down):
    S = h2.shape[0]
    ts = min(S, FF_STEP_ROWS)

    def body(h_ref, wg_ref, wu_ref, wd_ref, gate_ref, up_ref, f_ref):
        def rows_pass(first):
            def sub(i, carry):
                rows = pl.ds(pl.multiple_of(i * FF_ROWS, FF_ROWS), FF_ROWS)
                h = h_ref[rows, :]
                gate = _dot_nt(h, wg_ref[...])
                up = _dot_nt(h, wu_ref[...])
                gate_ref[0, rows, :] = gate.astype(bf16)
                up_ref[0, rows, :] = up.astype(bf16)
                part = _dot((gate * _sigmoid(gate) * up).astype(bf16), wd_ref[...])
                if first:
                    f_ref[rows, :] = part
                else:
                    f_ref[rows, :] += part
                return carry

            lax.fori_loop(0, ts // FF_ROWS, sub, 0, unroll=True)

        @pl.when(pl.program_id(1) == 0)
        def _():
            rows_pass(True)

        @pl.when(pl.program_id(1) > 0)
        def _():
            rows_pass(False)

    act = _ff_act_spec(ts)
    weight = pl.BlockSpec((FF_TILE, D_MODEL), lambda i, j: (j, 0))
    return pl.pallas_call(
        body, name="ffn_fwd", grid=(S // ts, D_FF // FF_TILE),
        in_specs=[pl.BlockSpec((ts, D_MODEL), lambda i, j: (i, 0)), weight, weight, weight],
        out_specs=[act, act, pl.BlockSpec((ts, D_MODEL), lambda i, j: (i, 0))],
        out_shape=[_ff_act_shape(S), _ff_act_shape(S), jax.ShapeDtypeStruct((S, D_MODEL), f32)],
        compiler_params=_params("parallel", "arbitrary"),
    )(h2, w_gate, w_up, w_down)


def _loss_head(f, x2, target, g4):
    S = f.shape[0]
    ts = 1024

    def body(f_ref, x2_ref, t_ref, g_ref, dy_ref, df_ref, dg_ref, loss_ref):
        @pl.when(pl.program_id(0) == 0)
        def _():
            dg_ref[...] = jnp.zeros_like(dg_ref)
            loss_ref[...] = jnp.zeros_like(loss_ref)

        fv = f_ref[...]
        g = g_ref[...]
        r = _rms(fv)
        fhat = fv * r
        err = (x2_ref[...] + fhat * g) - t_ref[...]
        loss_ref[...] += 0.5 * jnp.sum(jnp.mean(err * err, axis=-1, keepdims=True), axis=0, keepdims=True)
        dy = err * (1.0 / D_MODEL)
        dy_ref[...] = dy
        dg_ref[...] += jnp.sum(dy * fhat, axis=0, keepdims=True)
        dyg = dy * g
        df_ref[...] = (r * (dyg - fhat * jnp.mean(dyg * fhat, axis=-1, keepdims=True))).astype(bf16)

    row = pl.BlockSpec((ts, D_MODEL), lambda i: (i, 0))
    gain = pl.BlockSpec((1, D_MODEL), lambda i: (0, 0))
    return pl.pallas_call(
        body, name="loss_head", grid=(S // ts,), in_specs=[row, row, row, gain],
        out_specs=[row, row, gain, pl.BlockSpec((1, 1), lambda i: (0, 0))],
        out_shape=[jax.ShapeDtypeStruct((S, D_MODEL), f32), jax.ShapeDtypeStruct((S, D_MODEL), bf16),
                   jax.ShapeDtypeStruct((1, D_MODEL), f32), jax.ShapeDtypeStruct((1, 1), f32)],
        compiler_params=_params("arbitrary"),
    )(f, x2, target, g4)


def _ffn_bwd(df, gate, up, w_gate, w_up, w_down):
    S = df.shape[0]
    ts = min(S, FF_STEP_ROWS)

    def body(df_ref, gate_ref, up_ref, wg_ref, wu_ref, wd_ref, a_ref, dgate_ref, dup_ref, dh_ref):
        def rows_pass(first):
            def sub(i, carry):
                rows = pl.ds(pl.multiple_of(i * FF_BWD_ROWS, FF_BWD_ROWS), FF_BWD_ROWS)
                da = _dot_nt(df_ref[rows, :], wd_ref[...])
                g = gate_ref[0, rows, :].astype(f32)
                u = up_ref[0, rows, :].astype(f32)
                sig = _sigmoid(g)
                silu = g * sig
                a_ref[0, rows, :] = (silu * u).astype(bf16)
                dup = (da * silu).astype(bf16)
                dgate = (da * u * (sig * (1.0 + g * (1.0 - sig)))).astype(bf16)
                dup_ref[0, rows, :] = dup
                dgate_ref[0, rows, :] = dgate
                part = _dot(dgate, wg_ref[...]) + _dot(dup, wu_ref[...])
                if first:
                    dh_ref[rows, :] = part
                else:
                    dh_ref[rows, :] += part
                return carry

            lax.fori_loop(0, ts // FF_BWD_ROWS, sub, 0, unroll=True)

        @pl.when(pl.program_id(1) == 0)
        def _():
            rows_pass(True)

        @pl.when(pl.program_id(1) > 0)
        def _():
            rows_pass(False)

    act = _ff_act_spec(ts)
    row = pl.BlockSpec((ts, D_MODEL), lambda i, j: (i, 0))
    return pl.pallas_call(
        body, name="ffn_bwd", grid=(S // ts, D_FF // FF_TILE),
        in_specs=[row, act, act,
                  pl.BlockSpec((FF_TILE, D_MODEL), lambda i, j: (j, 0)),
                  pl.BlockSpec((FF_TILE, D_MODEL), lambda i, j: (j, 0)),
                  pl.BlockSpec((FF_TILE, D_MODEL), lambda i, j: (j, 0))],
        out_specs=[act, act, act, row],
        out_shape=[_ff_act_shape(S)] * 3 + [jax.ShapeDtypeStruct((S, D_MODEL), f32)],
        compiler_params=_params("parallel", "arbitrary"),
    )(df, gate, up, w_gate, w_up, w_down)


def _norm_bwd(dh2, dy, x2, mix, g3, g2):
    S = dh2.shape[0]
    ts = 512

    def body(dh_ref, dy_ref, x2_ref, mix_ref, g3_ref, g2_ref, dx2_ref, dmix_ref, dg3_ref, dg2_ref):
        @pl.when(pl.program_id(0) == 0)
        def _():
            dg3_ref[...] = jnp.zeros_like(dg3_ref)
            dg2_ref[...] = jnp.zeros_like(dg2_ref)

        dh = dh_ref[...]
        x2 = x2_ref[...]
        r3 = _rms(x2)
        xhat = x2 * r3
        dg3_ref[...] += jnp.sum(dh * xhat, axis=0, keepdims=True)
        dhg = dh * g3_ref[...]
        dx2 = dy_ref[...] + r3 * (dhg - xhat * jnp.mean(dhg * xhat, axis=-1, keepdims=True))
        dx2_ref[...] = dx2
        mix = mix_ref[...]
        r2 = _rms(mix)
        mhat = mix * r2
        dg2_ref[...] += jnp.sum(dx2 * mhat, axis=0, keepdims=True)
        dmg = dx2 * g2_ref[...]
        dmix_ref[...] = (r2 * (dmg - mhat * jnp.mean(dmg * mhat, axis=-1, keepdims=True))).astype(bf16)

    row = pl.BlockSpec((ts, D_MODEL), lambda i: (i, 0))
    gain = pl.BlockSpec((1, D_MODEL), lambda i: (0, 0))
    return pl.pallas_call(
        body, name="norm_bwd", grid=(S // ts,), in_specs=[row, row, row, row, gain, gain],
        out_specs=[row, row, gain, gain],
        out_shape=[jax.ShapeDtypeStruct((S, D_MODEL), f32), jax.ShapeDtypeStruct((S, D_MODEL), bf16),
                   jax.ShapeDtypeStruct((1, D_MODEL), f32), jax.ShapeDtypeStruct((1, D_MODEL), f32)],
        compiler_params=_params("arbitrary"),
    )(dh2, dy, x2, mix, g3, g2)


def _out_proj_bwd(dmix, w_out, attn_out, head_ones, grads):
    S = dmix.shape[0]
    ts = 512
    n_dil = len(DILATIONS)

    def body(dm_ref, w_ref, o_ref, ones_ref, g_ref, dp_ref, dl_ref, *rest):
        do_refs, theirs_ref = rest[:n_dil], rest[n_dil]
        stage = rest[n_dil + 1:n_dil + 1 + N_STAGE]
        swap = _Swap(g_ref, theirs_ref, *rest[n_dil + 1 + N_STAGE:])

        @pl.when(pl.program_id(0) == 0)
        def _():
            swap.start()

        @pl.when(pl.program_id(0) == S // ts - 1)
        def _():
            swap.finish()

        dcat = _dot_nt(dm_ref[...], w_ref[...])
        dp_ref[...] = dcat[:, :POOL_WIDTH]
        do = dcat[:, POOL_WIDTH:]
        for j in range(ATTN_WIDTH // 128):
            stage[j][...] = do[:, j * 128:(j + 1) * 128]
        _store_streams(stage, do_refs, ts)
        prod = do * o_ref[...].astype(f32)
        hi = prod.astype(bf16)
        lo = (prod - hi.astype(f32)).astype(bf16)
        ones = ones_ref[...]
        for j in range(ATTN_WIDTH // 128):
            cols = slice(j * 128, (j + 1) * 128)
            dl_ref[:, cols] = _dot(hi[:, cols], ones) + _dot(lo[:, cols], ones)

    row = lambda w: pl.BlockSpec((ts, w), lambda i: (i, 0))
    res = pl.pallas_call(
        body, name="out_proj_bwd", grid=(S // ts,),
        in_specs=[row(D_MODEL), pl.BlockSpec((D_MODEL, D_MODEL), lambda i: (0, 0)), row(ATTN_WIDTH),
                  pl.BlockSpec((128, 128), lambda i: (0, 0)), ANY],
        out_specs=[row(POOL_WIDTH), row(ATTN_WIDTH)] + [_stream_spec(d, ts) for d in DILATIONS] + [ANY],
        out_shape=[jax.ShapeDtypeStruct((S, POOL_WIDTH), f32), jax.ShapeDtypeStruct((S, ATTN_WIDTH), f32)]
        + [_stream_shape(S, d) for d in DILATIONS] + [_Swap.out_shape(grads)],
        scratch_shapes=_stage_scratch(ts) + _Swap.scratch(grads),
        compiler_params=_params("arbitrary"),
    )(dmix, w_out, attn_out, head_ones, grads)
    return res[0], res[1], res[2:2 + n_dil], res[2 + n_dil]


IN_BWD_ROWS = 256


def _in_proj_bwd(du, dq, dk, dv, cos_t, sin_t, w_in, x, dx2, g1):
    S = x.shape[0]
    ts = 512

    def body(du_ref, dq_ref, dk_ref, dv_ref, cos_ref, sin_ref, w_ref, x_ref, dx2_ref, g_ref, gx_ref, dproj_ref, dg_ref):
        @pl.when(pl.program_id(0) == 0)
        def _():
            dg_ref[...] = jnp.zeros_like(dg_ref)

        first = _first_half_mask(IN_BWD_ROWS)

        def sub(i, carry):
            rows = pl.ds(pl.multiple_of(i * IN_BWD_ROWS, IN_BWD_ROWS), IN_BWD_ROWS)
            dproj_ref[rows, :POOL_WIDTH] = du_ref[rows, :]
            cos = cos_ref[rows, :]
            sin = sin_ref[rows, :]
            for j in range(ATTN_WIDTH // 128):
                cols = slice(j * 128, (j + 1) * 128)
                for base, ref in ((POOL_WIDTH, dq_ref), (POOL_WIDTH + ATTN_WIDTH, dk_ref)):
                    g = ref[rows, cols].astype(f32)
                    pre = g * cos + _rope_partner(g * sin, first)
                    dproj_ref[rows, base + j * 128: base + (j + 1) * 128] = pre.astype(bf16)
            dproj_ref[rows, POOL_WIDTH + 2 * ATTN_WIDTH:] = dv_ref[rows, :]

            dh = _dot(dproj_ref[rows, :], w_ref[...])
            xv = x_ref[rows, :]
            r = _rms(xv)
            xhat = xv * r
            dg_ref[...] += jnp.sum(dh * xhat, axis=0, keepdims=True)
            dhg = dh * g_ref[...]
            gx_ref[rows, :] = dx2_ref[rows, :] + r * (dhg - xhat * jnp.mean(dhg * xhat, axis=-1, keepdims=True))
            return carry

        lax.fori_loop(0, ts // IN_BWD_ROWS, sub, 0, unroll=True)

    row = lambda w: pl.BlockSpec((ts, w), lambda i: (i, 0))
    gain = pl.BlockSpec((1, D_MODEL), lambda i: (0, 0))
    return pl.pallas_call(
        body, name="in_proj_bwd", grid=(S // ts,),
        in_specs=[row(POOL_WIDTH)] + [row(ATTN_WIDTH)] * 3 + [row(128), row(128),
                  pl.BlockSpec((IN_WIDTH, D_MODEL), lambda i: (0, 0)), row(D_MODEL), row(D_MODEL), gain],
        out_specs=[row(D_MODEL), row(IN_WIDTH), gain],
        out_shape=[jax.ShapeDtypeStruct((S, D_MODEL), f32), jax.ShapeDtypeStruct((S, IN_WIDTH), bf16),
                   jax.ShapeDtypeStruct((1, D_MODEL), f32)],
        compiler_params=_params("arbitrary"),
    )(du, dq, dk, dv, cos_t, sin_t, w_in, x, dx2, g1)


def _matmul_tiles_tn(a, b, name):
    T, K, w = a.shape
    N = b.shape[1]
    tk = 1024

    def body(a_ref, b_ref, o_ref):
        def tiles_pass(first):
            for t in range(T):
                part = _dot_tn(a_ref[t], b_ref[...])
                if first:
                    o_ref[t * w:(t + 1) * w, :] = part
                else:
                    o_ref[t * w:(t + 1) * w, :] += part

        @pl.when(pl.program_id(0) == 0)
        def _():
            tiles_pass(True)

        @pl.when(pl.program_id(0) > 0)
        def _():
            tiles_pass(False)

    return pl.pallas_call(
        body, name=name, grid=(K // tk,),
        in_specs=[pl.BlockSpec((T, tk, w), lambda k: (0, k, 0)), pl.BlockSpec((tk, N), lambda k: (k, 0))],
        out_specs=pl.BlockSpec((T * w, N), lambda k: (0, 0)),
        out_shape=jax.ShapeDtypeStruct((T * w, N), f32),
        compiler_params=_params("arbitrary"),
    )(a, b)


def _matmul_tn(a, b, name):
    K, M = a.shape
    N = b.shape[1]
    tk = 1024

    def body(a_ref, b_ref, o_ref):
        _tn_step(a_ref, b_ref, o_ref, M)

    return pl.pallas_call(
        body, name=name, grid=(K // tk,),
        in_specs=[pl.BlockSpec((tk, M), lambda k: (k, 0)), pl.BlockSpec((tk, N), lambda k: (k, 0))],
        out_specs=pl.BlockSpec((M, N), lambda k: (0, 0)),
        out_shape=jax.ShapeDtypeStruct((M, N), f32),
        compiler_params=_params("arbitrary"),
    )(a, b)


def _tn_step(a_ref, b_ref, o_ref, M):
    w = 256

    def tiles_pass(first):
        for t in range(M // w):
            part = _dot_tn(a_ref[:, t * w:(t + 1) * w], b_ref[...])
            if first:
                o_ref[t * w:(t + 1) * w, :] = part
            else:
                o_ref[t * w:(t + 1) * w, :] += part

    @pl.when(pl.program_id(0) == 0)
    def _():
        tiles_pass(True)

    @pl.when(pl.program_id(0) > 0)
    def _():
        tiles_pass(False)


def _matmul_tn_and_small_sum(a, b, block, name):
    K, M = a.shape
    N = b.shape[1]
    tk = 1024
    n_steps = K // tk

    def body(a_ref, b_ref, block_ref, o_ref, total_ref, *scratch):
        small = _SmallSum(block_ref, *scratch)

        @pl.when(pl.program_id(0) == 0)
        def _():
            small.start()

        _tn_step(a_ref, b_ref, o_ref, M)

        @pl.when(pl.program_id(0) == n_steps - 1)
        def _():
            small.finish(total_ref)

    return pl.pallas_call(
        body, name=name, grid=(n_steps,),
        in_specs=[pl.BlockSpec((tk, M), lambda k: (k, 0)), pl.BlockSpec((tk, N), lambda k: (k, 0)), ANY],
        out_specs=[pl.BlockSpec((M, N), lambda k: (0, 0)), pl.BlockSpec(block.shape, lambda k: (0, 0))],
        out_shape=[jax.ShapeDtypeStruct((M, N), f32), jax.ShapeDtypeStruct(block.shape, block.dtype)],
        scratch_shapes=_SmallSum.scratch(block),
        compiler_params=_params("arbitrary"),
    )(a, b, block)


def _rope_tables(S):
    half = HEAD_DIM // 2
    freqs = ROPE_THETA ** (-jnp.arange(half, dtype=f32) * (2.0 / HEAD_DIM))
    ang = jnp.arange(S).astype(f32)[:, None] * freqs[None, :]
    cos = jnp.tile(jnp.cos(ang), (1, 4))
    sin = jnp.sin(ang)
    sin = jnp.tile(jnp.concatenate([-sin, sin], axis=1), (1, 2))
    return cos, sin


def _block_diag(w_pool):
    w = jnp.zeros((POOL_WIDTH, POOL_WIDTH), w_pool.dtype)
    for g in range(POOL_WIDTH // POOL_GROUP):
        w = lax.dynamic_update_slice(w, w_pool[g], (g * POOL_GROUP, g * POOL_GROUP))
    return w


def _head_ones():
    head = np.arange(128) // HEAD_DIM
    return jnp.asarray(head[:, None] == head[None, :], dtype=bf16)


def _place():
    x, y, c = lax.axis_index("x"), lax.axis_index("y"), lax.axis_index("c")
    chips = [(1 - x, y), (x, 1 - y), (1 - x, 1 - y)]
    return x, y, c, chips


ANY = pl.BlockSpec(memory_space=pl.ANY)
N_PEER_CHIPS = N_CHIPS - 1
ICI_PIECES = 4
D2D_PIECES = 8
LOCAL_PIECES = 8


def _row_chunks(rows, n, unit=32):
    units = rows // unit
    out, start = [], 0
    for i in range(n):
        size = (units // n + (1 if i < units % n else 0)) * unit
        out.append((start, size))
        start += size
    return [piece for piece in out if piece[1]]


class _LocalCopy:
    def __init__(self, src_rows, dst_rows, rows, buf, sems_in, sems_out):
        self.loads, self.stores = [], []
        for i, (start, size) in enumerate(_row_chunks(rows, LOCAL_PIECES)):
            r = pl.ds(start, size)
            self.loads.append(pltpu.make_async_copy(src_rows(r), buf.at[r], sems_in.at[i]))
            self.stores.append(pltpu.make_async_copy(buf.at[r], dst_rows(r), sems_out.at[i]))

    def start(self):
        for i, cp in enumerate(self.loads):
            cp.start(priority=i % 2)

    def pass_on(self):
        for i, (load, store) in enumerate(zip(self.loads, self.stores)):
            load.wait()
            store.start(priority=i % 2)

    def finish(self):
        for store in self.stores:
            store.wait()

    @staticmethod
    def scratch(rows, dtype):
        return [pltpu.VMEM((rows, D_MODEL), dtype), pltpu.SemaphoreType.DMA((LOCAL_PIECES,)),
                pltpu.SemaphoreType.DMA((LOCAL_PIECES,))]


class _Gather:
    def __init__(self, w_ref, out_ref, send1, recv1, send2, recv2, buf, sems_in, sems_out):
        x, y, c, chips = _place()
        me = 2 * x + y
        rows = w_ref.shape[0]
        half = rows // 2
        pieces = _row_chunks(half, ICI_PIECES)
        self.own = _LocalCopy(lambda r: w_ref.at[r], lambda r: out_ref.at[me, r], rows, buf, sems_in, sems_out)

        def rows_of(core, piece):
            start, size = piece
            return pl.ds(core * half + start, size)

        self.sends, self.arrivals, self.forwards, self.forward_arrivals = [], [], [], []
        for i, piece in enumerate(pieces):
            for j, (cx, cy) in enumerate(chips):
                k = j * len(pieces) + i
                there = 2 * cx + cy

                def direct(src_chip, cx=cx, cy=cy, k=k, piece=piece):
                    return pltpu.make_async_remote_copy(
                        src_ref=w_ref.at[rows_of(c, piece)], dst_ref=out_ref.at[src_chip, rows_of(c, piece)],
                        send_sem=send1.at[k], recv_sem=recv1.at[k], device_id=(cx, cy, c), device_id_type=MESH)

                def passed(core, there=there, k=k, piece=piece):
                    return pltpu.make_async_remote_copy(
                        src_ref=out_ref.at[there, rows_of(core, piece)], dst_ref=out_ref.at[there, rows_of(core, piece)],
                        send_sem=send2.at[k], recv_sem=recv2.at[k], device_id=(x, y, 1 - c), device_id_type=MESH)

                self.sends.append(direct(me))
                self.arrivals.append(direct(there))
                self.forwards.append(passed(c))
                self.forward_arrivals.append(passed(1 - c))

    def start(self):
        for cp in self.sends:
            cp.start()
        self.own.start()

    def pass_on(self):
        self.own.pass_on()
        for arrival, forward in zip(self.arrivals, self.forwards):
            arrival.wait_recv()
            forward.start()

    def finish(self):
        for arrival in self.forward_arrivals:
            arrival.wait_recv()
        for cp in self.sends + self.forwards:
            cp.wait_send()
        self.own.finish()

    @staticmethod
    def scratch(rows, dtype):
        n = N_PEER_CHIPS * len(_row_chunks(rows // 2, ICI_PIECES))
        return [pltpu.SemaphoreType.DMA((n,))] * 4 + _LocalCopy.scratch(rows, dtype)

    @staticmethod
    def out_shape(rows, dtype):
        return jax.ShapeDtypeStruct((N_CHIPS, rows, D_MODEL), dtype)


def _gather_weights(pack):
    rows = pack.shape[0]

    def body(w_ref, out_ref, *scratch):
        gather = _Gather(w_ref, out_ref, *scratch)
        gather.start()
        gather.pass_on()
        gather.finish()

    return pl.pallas_call(
        body, name="gather_weights", in_specs=[ANY], out_specs=ANY, out_shape=_Gather.out_shape(rows, pack.dtype),
        scratch_shapes=_Gather.scratch(rows, pack.dtype),
        compiler_params=pltpu.CompilerParams(vmem_limit_bytes=VMEM_LIMIT_V7X),
    )(pack)


class _Scatter:
    def __init__(self, h_ref, out_ref, send, recv):
        x, y, c, chips = _place()
        pieces = _row_chunks(h_ref.shape[1], ICI_PIECES)
        self.copies = []
        for i, (start, size) in enumerate(pieces):
            for j, (cx, cy) in enumerate(chips):
                k = j * len(pieces) + i
                self.copies.append(pltpu.make_async_remote_copy(
                    src_ref=h_ref.at[2 * cx + cy, pl.ds(start, size)], dst_ref=out_ref.at[j, pl.ds(start, size)],
                    send_sem=send.at[k], recv_sem=recv.at[k], device_id=(cx, cy, c), device_id_type=MESH))

    def start(self):
        for cp in self.copies:
            cp.start()

    def finish(self):
        for cp in self.copies:
            cp.wait_recv()
        for cp in self.copies:
            cp.wait_send()

    @staticmethod
    def scratch(half):
        n = N_PEER_CHIPS * len(_row_chunks(half, ICI_PIECES))
        return [pltpu.SemaphoreType.DMA((n,))] * 2

    @staticmethod
    def out_shape(half, dtype):
        return jax.ShapeDtypeStruct((N_PEER_CHIPS, half, D_MODEL), dtype)


def _scatter_to_chips(h):
    half = h.shape[1]

    def body(h_ref, out_ref, send, recv):
        scatter = _Scatter(h_ref, out_ref, send, recv)
        scatter.start()
        scatter.finish()

    return pl.pallas_call(
        body, name="scatter_to_chips", in_specs=[ANY], out_specs=ANY, out_shape=_Scatter.out_shape(half, h.dtype),
        scratch_shapes=_Scatter.scratch(half),
    )(h)


class _Swap:
    def __init__(self, g_ref, theirs_ref, send, recv):
        x, y, c, _ = _place()
        half = g_ref.shape[1] // 2
        pieces = _row_chunks(half, D2D_PIECES)
        self.copies = []
        for s in range(N_CHIPS):
            for i, (start, size) in enumerate(pieces):
                k = s * len(pieces) + i
                self.copies.append(pltpu.make_async_remote_copy(
                    src_ref=g_ref.at[s, pl.ds((1 - c) * half + start, size)], dst_ref=theirs_ref.at[s, pl.ds(start, size)],
                    send_sem=send.at[k], recv_sem=recv.at[k], device_id=(x, y, 1 - c), device_id_type=MESH))

    def start(self):
        for cp in self.copies:
            cp.start()

    def finish(self):
        for cp in self.copies:
            cp.wait()

    @staticmethod
    def scratch(g):
        n = N_CHIPS * len(_row_chunks(g.shape[1] // 2, D2D_PIECES))
        return [pltpu.SemaphoreType.DMA((n,))] * 2

    @staticmethod
    def out_shape(g):
        return jax.ShapeDtypeStruct((N_CHIPS, g.shape[1] // 2, D_MODEL), g.dtype)


def _swap_halves(g):
    def body(g_ref, theirs_ref, send, recv):
        swap = _Swap(g_ref, theirs_ref, send, recv)
        swap.start()
        swap.finish()

    return pl.pallas_call(
        body, name="swap_halves", in_specs=[ANY], out_specs=ANY, out_shape=_Swap.out_shape(g),
        scratch_shapes=_Swap.scratch(g),
    )(g)


ADD_TILE_MAX_ROWS = 600


def _add_tile(half):
    return max(t for t in range(8, ADD_TILE_MAX_ROWS + 1, 8) if half % t == 0)


def _add_cores(g, theirs, name, out_dtype=f32):
    half = theirs.shape[1]
    tr = _add_tile(half)
    n_t = half // tr

    def body(c_ref, g_ref, t_ref, o_ref):
        o_ref[...] = (g_ref[...] + t_ref[...]).astype(out_dtype)

    blk = pl.BlockSpec((1, tr, D_MODEL), lambda s, t, c_ref: (s, t, 0))
    return pl.pallas_call(
        body, name=name,
        grid_spec=pltpu.PrefetchScalarGridSpec(
            num_scalar_prefetch=1, grid=(N_CHIPS, n_t),
            in_specs=[pl.BlockSpec((1, tr, D_MODEL), lambda s, t, c_ref: (s, c_ref[0] * n_t + t, 0)), blk],
            out_specs=blk),
        out_shape=jax.ShapeDtypeStruct(theirs.shape, out_dtype),
        compiler_params=_params("parallel", "parallel"),
    )(lax.axis_index("c").astype(jnp.int32).reshape(1), g, theirs)


def _add_chips(chip_sum, others, name):
    half = chip_sum.shape[1]
    tr = _add_tile(half)

    def body(me_ref, own_ref, o0, o1, o2, out_ref):
        out_ref[...] = ((own_ref[0].astype(f32) + o0[0].astype(f32)) + o1[0].astype(f32)) + o2[0].astype(f32)

    other = lambda j: pl.BlockSpec((1, tr, D_MODEL), lambda t, me_ref: (j, t, 0))
    return pl.pallas_call(
        body, name=name,
        grid_spec=pltpu.PrefetchScalarGridSpec(
            num_scalar_prefetch=1, grid=(half // tr,),
            in_specs=[pl.BlockSpec((1, tr, D_MODEL), lambda t, me_ref: (me_ref[0], t, 0)), other(0), other(1), other(2)],
            out_specs=pl.BlockSpec((tr, D_MODEL), lambda t, me_ref: (t, 0))),
        out_shape=jax.ShapeDtypeStruct((half, D_MODEL), f32),
        compiler_params=_params("parallel"),
    )((2 * lax.axis_index("x") + lax.axis_index("y")).astype(jnp.int32).reshape(1), chip_sum, others, others, others)


def _join_halves(parts):
    n_parts = len(parts)
    pieces = [_row_chunks(r.shape[0], D2D_PIECES) for r in parts]
    first = [sum(len(p) for p in pieces[:i]) for i in range(n_parts)]
    n = sum(len(p) for p in pieces)

    def body(*refs):
        r_refs, out_refs = refs[:n_parts], refs[n_parts:2 * n_parts]
        send, recv = refs[2 * n_parts:2 * n_parts + 2]
        local = refs[2 * n_parts + 2:]
        x, y, c, _ = _place()
        owns = [_LocalCopy(lambda rr, r_ref=r_ref: r_ref.at[rr], lambda rr, out_ref=out_ref: out_ref.at[c, rr],
                           r_ref.shape[0], *local[3 * i:3 * i + 3])
                for i, (r_ref, out_ref) in enumerate(zip(r_refs, out_refs))]
        for own in owns:
            own.start()

        def piece(i, j, core):
            start, size = pieces[i][j]
            return pltpu.make_async_remote_copy(
                src_ref=r_refs[i].at[pl.ds(start, size)], dst_ref=out_refs[i].at[core, pl.ds(start, size)],
                send_sem=send.at[first[i] + j], recv_sem=recv.at[first[i] + j],
                device_id=(x, y, 1 - c), device_id_type=MESH)

        every = [(i, j) for i in range(n_parts) for j in range(len(pieces[i]))]
        copies = [piece(i, j, c) for i, j in every]
        for cp in copies:
            cp.start()
        for own in owns:
            own.pass_on()
        for i, j in every:
            piece(i, j, 1 - c).wait_recv()
        for cp in copies:
            cp.wait_send()
        for own in owns:
            own.finish()

    local_scratch = []
    for r in parts:
        local_scratch += _LocalCopy.scratch(r.shape[0], r.dtype)
    return pl.pallas_call(
        body, name="join_halves", in_specs=[ANY] * n_parts, out_specs=[ANY] * n_parts,
        out_shape=[jax.ShapeDtypeStruct((2,) + r.shape, r.dtype) for r in parts],
        scratch_shapes=[pltpu.SemaphoreType.DMA((n,))] * 2 + local_scratch,
        compiler_params=pltpu.CompilerParams(vmem_limit_bytes=VMEM_LIMIT_V7X),
    )(*parts)


class _SmallSum:
    def __init__(self, b_ref, gathered, send, recv, local_sem):
        x, y, c, _ = _place()
        me = 4 * x + 2 * y + c
        self.gathered = gathered
        self.own = pltpu.make_async_copy(b_ref, gathered.at[me], local_sem)
        self.sends, self.arrivals = [], []
        for kk in range(1, N_DEV):
            flip = lambda v, bit: 1 - v if bit else v
            peer = (flip(x, kk & 4), flip(y, kk & 2), flip(c, kk & 1))
            self.sends.append(pltpu.make_async_remote_copy(
                src_ref=b_ref, dst_ref=gathered.at[me], send_sem=send.at[kk - 1], recv_sem=recv.at[kk - 1],
                device_id=peer, device_id_type=MESH))
            self.arrivals.append(pltpu.make_async_remote_copy(
                src_ref=b_ref, dst_ref=gathered.at[jnp.bitwise_xor(me, kk)], send_sem=send.at[kk - 1],
                recv_sem=recv.at[kk - 1], device_id=peer, device_id_type=MESH))

    def start(self):
        self.own.start()
        for cp in self.sends:
            cp.start()

    def finish(self, out_ref):
        self.own.wait()
        for cp in self.arrivals:
            cp.wait_recv()
        for cp in self.sends:
            cp.wait_send()
        acc = self.gathered[0]
        for dev in range(1, N_DEV):
            acc = acc + self.gathered[dev]
        out_ref[...] = acc

    @staticmethod
    def scratch(block):
        return [pltpu.VMEM((N_DEV,) + block.shape, block.dtype), pltpu.SemaphoreType.DMA((N_DEV - 1,)),
                pltpu.SemaphoreType.DMA((N_DEV - 1,)), pltpu.SemaphoreType.DMA]


def _adamw(w, g, m, v, name):
    rows, cols = w.shape
    tr = max(t for t in range(8, 513, 8) if rows % t == 0)
    c1 = 1.0 - ADAM_B1 ** ADAM_STEP
    c2 = 1.0 - ADAM_B2 ** ADAM_STEP

    def body(w_ref, g_ref, m_ref, v_ref, d_ref, nm_ref, nv_ref):
        gv = g_ref[...]
        nm = ADAM_B1 * m_ref[...] + (1.0 - ADAM_B1) * gv
        nv = ADAM_B2 * v_ref[...] + (1.0 - ADAM_B2) * (gv * gv)
        nm_ref[...] = nm
        nv_ref[...] = nv
        d_ref[...] = -ADAM_LR * ((nm / c1) / (jnp.sqrt(nv / c2) + ADAM_EPS) + ADAM_WD * w_ref[...])

    blk = pl.BlockSpec((tr, cols), lambda i: (i, 0))
    shape = jax.ShapeDtypeStruct((rows, cols), f32)
    return pl.pallas_call(
        body, name=name, grid=(rows // tr,), in_specs=[blk] * 4, out_specs=[blk] * 3, out_shape=[shape] * 3,
        compiler_params=_params("parallel"),
    )(w, g, m, v)


LARGE = ("w_in", "w_out", "w_gate", "w_up", "w_down")
SMALL = ("ln_pre_mix", "ln_post_mix", "ln_pre_ffn", "ln_post_ffn", "pool_scale", "w_pool")
SHARD_ROWS = {"w_in": 640, "w_out": 256, "w_gate": 704, "w_up": 704, "w_down": 704}
COLUMN_SHARDED = ("w_in", "w_gate", "w_up")
UPDATED_TRANSPOSED = ("w_gate", "w_up")
NEEDED_FIRST = ("w_in",)
NEEDED_LATER = ("w_out", "w_gate", "w_up", "w_down")
READY_EARLY = ("w_out", "w_gate", "w_up", "w_down")
READY_LATE = ("w_in",)


def _pack_shard(shards, names):
    return jnp.concatenate([shards[n].T if n in COLUMN_SHARDED else shards[n] for n in names], axis=0)


def _unpack_shard(pack, names):
    out, row = {}, 0
    for n in names:
        out[n] = pack[row:row + SHARD_ROWS[n]]
        row += SHARD_ROWS[n]
    return out


def _whole_from_shards(packs, names):
    out, row = {}, 0
    for n in names:
        rows = SHARD_ROWS[n]
        out[n] = packs[:, row:row + rows].reshape(N_CHIPS * rows, D_MODEL)
        row += rows
    return out


def _shards_from_whole(grads, names):
    return jnp.concatenate([grads[n].reshape(N_CHIPS, SHARD_ROWS[n], D_MODEL) for n in names], axis=1)


def _pack_small(vals):
    rows = [vals[n].reshape(1, D_MODEL) for n in SMALL[:4]]
    rows.append(jnp.pad(vals["pool_scale"].reshape(1, POOL_WIDTH), ((0, 0), (0, D_MODEL - POOL_WIDTH))))
    rows.append(jnp.pad(vals["loss"].reshape(1, 1), ((0, 0), (0, D_MODEL - 1))))
    rows.append(jnp.zeros((2, D_MODEL), f32))
    rows.append(vals["w_pool"].reshape(16, D_MODEL))
    return jnp.concatenate(rows, axis=0)


def _unpack_small(block):
    out = {n: block[i:i + 1] for i, n in enumerate(SMALL[:4])}
    out["pool_scale"] = block[4:5, :POOL_WIDTH]
    out["loss"] = block[5, 0]
    out["w_pool"] = block[8:24].reshape(1, 4, POOL_GROUP, POOL_GROUP)
    return out


def kernel(x, ln_pre_mix, w_in, w_pool, pool_scale, w_out, ln_post_mix, ln_pre_ffn, w_gate, w_up, w_down, ln_post_ffn, loss_target, m_ln_pre_mix, m_w_in, m_w_pool, m_pool_scale, m_w_out, m_ln_post_mix, m_ln_pre_ffn, m_w_gate, m_w_up, m_w_down, m_ln_post_ffn, v_ln_pre_mix, v_w_in, v_w_pool, v_pool_scale, v_w_out, v_ln_post_mix, v_ln_pre_ffn, v_w_gate, v_w_up, v_w_down, v_ln_post_ffn):
    w = dict(ln_pre_mix=ln_pre_mix, w_in=w_in, w_pool=w_pool, pool_scale=pool_scale, w_out=w_out,
             ln_post_mix=ln_post_mix, ln_pre_ffn=ln_pre_ffn, w_gate=w_gate, w_up=w_up, w_down=w_down,
             ln_post_ffn=ln_post_ffn)
    m = dict(ln_pre_mix=m_ln_pre_mix, w_in=m_w_in, w_pool=m_w_pool, pool_scale=m_pool_scale, w_out=m_w_out,
             ln_post_mix=m_ln_post_mix, ln_pre_ffn=m_ln_pre_ffn, w_gate=m_w_gate, w_up=m_w_up, w_down=m_w_down,
             ln_post_ffn=m_ln_post_ffn)
    v = dict(ln_pre_mix=v_ln_pre_mix, w_in=v_w_in, w_pool=v_w_pool, pool_scale=v_pool_scale, w_out=v_w_out,
             ln_post_mix=v_ln_post_mix, ln_pre_ffn=v_ln_pre_ffn, w_gate=v_w_gate, w_up=v_w_up, w_down=v_w_down,
             ln_post_ffn=v_ln_post_ffn)

    xs, target = x[0], loss_target[0]
    cos_t, sin_t = _rope_tables(xs.shape[0])
    w_bd = _block_diag(w_pool[0]).astype(bf16)
    shard = {n: w[n][0].astype(bf16) for n in LARGE}

    w_in_whole = _whole_from_shards(_gather_weights(_pack_shard(shard, NEEDED_FIRST)), NEEDED_FIRST)["w_in"]
    h1, u, qs, ks, vs = _in_proj(xs, ln_pre_mix, w_in_whole, cos_t, sin_t)
    pool_out = _pool_fwd(u, w_bd, pool_scale)
    attn_out, lse, later = _attn_fwd(qs, ks, vs, _pack_shard(shard, NEEDED_LATER))
    whole = _whole_from_shards(later, NEEDED_LATER)
    mix, x2, h2 = _out_proj(pool_out, attn_out, whole["w_out"], xs, ln_post_mix, ln_pre_ffn)
    gate, up, f = _ffn_fwd(h2, whole["w_gate"], whole["w_up"], whole["w_down"])
    dy, df, dg4, loss = _loss_head(f, x2, target, ln_post_ffn)

    large = {}
    a, dgate, dup, dh2 = _ffn_bwd(df, gate, up, whole["w_gate"], whole["w_up"], whole["w_down"])
    large["w_down"] = _matmul_tiles_tn(a, df, "grad_w_down")
    large["w_gate"] = _matmul_tiles_tn(dgate, h2, "grad_w_gate")
    large["w_up"] = _matmul_tiles_tn(dup, h2, "grad_w_up")
    dx2, dmix, dg3, dg2 = _norm_bwd(dh2, dy, x2, mix, ln_pre_ffn, ln_post_mix)
    large["w_out"] = jnp.concatenate([_matmul_tn(pool_out, dmix, "grad_w_out_pool"),
                                      _matmul_tn(attn_out, dmix, "grad_w_out_attn")], axis=0)
    early = _shards_from_whole(large, READY_EARLY)
    dpool, delta, dos, early_theirs = _out_proj_bwd(dmix, whole["w_out"], attn_out, _head_ones(), early)
    early_chip = _add_cores(early, early_theirs, "add_cores_early")
    du, d_w_bd, d_scale = _pool_bwd(u, dpool, w_bd, pool_scale)
    dq, dk, dv, early_others = _attn_bwd(qs, ks, vs, dos, lse, delta, early_chip)
    grad_x, dproj, dg1 = _in_proj_bwd(du, dq, dk, dv, cos_t, sin_t, w_in_whole, xs, dx2, ln_pre_mix)
    d_w_pool = jnp.stack([d_w_bd[g * POOL_GROUP:(g + 1) * POOL_GROUP, g * POOL_GROUP:(g + 1) * POOL_GROUP]
                          for g in range(POOL_WIDTH // POOL_GROUP)])
    small = dict(ln_pre_mix=dg1, ln_post_mix=dg2, ln_pre_ffn=dg3, ln_post_ffn=dg4, pool_scale=d_scale, w_pool=d_w_pool)
    large["w_in"], small_total = _matmul_tn_and_small_sum(dproj, h1, _pack_small(dict(small, loss=loss)), "grad_w_in")
    late = _shards_from_whole(large, READY_LATE)
    late_chip = _add_cores(late, _swap_halves(late), "add_cores_late", bf16)
    late_others = _scatter_to_chips(late_chip)
    early_half = _add_chips(early_chip, early_others, "add_chips_early")
    late_half = _add_chips(late_chip, late_others, "add_chips_late")
    early_whole, late_whole = _join_halves([early_half, late_half])
    grads = _unpack_shard(early_whole.reshape(-1, D_MODEL), READY_EARLY)
    grads.update(_unpack_shard(late_whole.reshape(-1, D_MODEL), READY_LATE))

    total = _unpack_small(small_total)
    for n in SMALL:
        grads[n] = total[n]

    delta_w, new_m, new_v = {}, {}, {}
    for n in LARGE:
        if n in UPDATED_TRANSPOSED:
            update = _adamw(w[n][0].T, grads[n], m[n][0].T, v[n][0].T, "adamw_" + n)
            delta_w[n], new_m[n], new_v[n], grads[n] = [a.T for a in (*update, grads[n])]
        else:
            if n in COLUMN_SHARDED:
                grads[n] = grads[n].T
            delta_w[n], new_m[n], new_v[n] = _adamw(w[n][0], grads[n], m[n][0], v[n][0], "adamw_" + n)
    small_state = [_pack_small(dict({n: s[n] for n in SMALL}, loss=jnp.zeros((), f32))) for s in (w, m, v)]
    small_grad = _pack_small(dict({n: grads[n] for n in SMALL}, loss=jnp.zeros((), f32)))
    sd, sm, sv = _adamw(small_state[0], small_grad, small_state[1], small_state[2], "adamw_small")
    for out, block in ((delta_w, sd), (new_m, sm), (new_v, sv)):
        un = _unpack_small(block)
        for n in SMALL:
            out[n] = un[n]

    names = ("ln_pre_mix", "w_in", "w_pool", "pool_scale", "w_out", "ln_post_mix", "ln_pre_ffn", "w_gate", "w_up",
             "w_down", "ln_post_ffn")
    full = lambda d: [d[n].reshape(w[n].shape) for n in names]
    return (total["loss"], grad_x[None], *full(grads), *full(delta_w), *full(new_m), *full(new_v))
```
